```python
import math
import jax, jax.numpy as jnp
from jax import lax
import numpy as np

D_MODEL = 1024
BATCH = 8
SEQ = 2048
DEPTH = 1

PLE_DIM = 256
D_MIX = D_MODEL
D_GMLP = D_MIX // 2
GMLP_GROUPS = 4
GMLP_GROUP_DIM = D_GMLP // GMLP_GROUPS
CHUNK = 128
D_ATTN = D_MIX - D_GMLP
HEAD_DIM = 64
N_Q_HEADS = D_ATTN // HEAD_DIM
N_KV_HEADS = 2
Q_PER_KV = N_Q_HEADS // N_KV_HEADS
WINDOW = 128
BLOCK = WINDOW
REL_BUCKETS = 32
REL_MAX_DIST = 128
D_FF = 4 * D_MODEL
D_IN = 2 * D_GMLP + D_ATTN + 2 * N_KV_HEADS * HEAD_DIM
EPS = 1e-6
NEG_INF = -1e30

kernel_name = "hybrid_gmlp_swa_sink_block"


def rmsnorm(x, g):
    xf = x.astype(jnp.float32)
    y = xf * lax.rsqrt(jnp.mean(xf * xf, axis=-1, keepdims=True) + EPS)
    return (y * g.astype(jnp.float32)).astype(x.dtype)


def t5_causal_bucket(n):
    max_exact = REL_BUCKETS // 2
    nf = jnp.maximum(n, 1).astype(jnp.float32)
    large = max_exact + (jnp.log(nf / max_exact) / math.log(REL_MAX_DIST / max_exact)
                         * (REL_BUCKETS - max_exact)).astype(jnp.int32)
    large = jnp.minimum(large, REL_BUCKETS - 1)
    return jnp.where(n < max_exact, n, large)


def band_bias_and_mask(rel_table):
    a = jnp.arange(BLOCK)[:, None]
    j = jnp.arange(2 * BLOCK)[None, :]
    n = BLOCK + a - j
    valid = (n >= 0) & (n < WINDOW)
    bucket = t5_causal_bucket(jnp.maximum(n, 0))
    bias = jnp.transpose(rel_table[bucket], (2, 0, 1))
    return bias, valid


def gmlp_chunk_mixer(u, v, v_gain, w_spatial, b_spatial):
    B, S, _ = u.shape
    nc = S // CHUNK
    vg = v.reshape(B, S, GMLP_GROUPS, GMLP_GROUP_DIM)
    vg = rmsnorm(vg, v_gain.reshape(GMLP_GROUPS, GMLP_GROUP_DIM))
    vc = vg.reshape(B, nc, CHUNK, GMLP_GROUPS, GMLP_GROUP_DIM)
    causal = jnp.tril(jnp.ones((CHUNK, CHUNK), dtype=bool))
    w = jnp.where(causal[None], w_spatial, 0).astype(vc.dtype)
    sv = jnp.einsum('gts,bnsgd->bntgd', w, vc) + jnp.transpose(b_spatial)[:, :, None].astype(vc.dtype)
    return u * sv.reshape(B, S, D_GMLP)


def swa_sink_attention(q, k, v, sinks, rel_table):
    B, S, _ = q.shape
    nb = S // BLOCK
    qb = q.reshape(B, nb, BLOCK, N_KV_HEADS, Q_PER_KV, HEAD_DIM)

    def band(t):
        t = t.reshape(B, S, N_KV_HEADS, HEAD_DIM)
        t = jnp.pad(t, ((0, 0), (BLOCK, 0), (0, 0), (0, 0)))
        t = t.reshape(B, nb + 1, BLOCK, N_KV_HEADS, HEAD_DIM)
        return jnp.concatenate([t[:, :-1], t[:, 1:]], axis=2)

    kb, vb = band(k), band(v)
    logits = jnp.einsum('bnqkgd,bnskd->bkgnqs', qb, kb).astype(jnp.float32) * (HEAD_DIM ** -0.5)
    bias, valid = band_bias_and_mask(rel_table)
    bias = bias.astype(jnp.float32).reshape(N_KV_HEADS, Q_PER_KV, 1, BLOCK, 2 * BLOCK)
    first = (jnp.arange(nb)[:, None, None] == 0) & (jnp.arange(2 * BLOCK)[None, None, :] < BLOCK)
    mask = valid[None] & ~first
    logits = jnp.where(mask, logits + bias, NEG_INF)
    sink = sinks.astype(jnp.float32).reshape(N_KV_HEADS, Q_PER_KV, 1, 1, 1)
    m = jnp.maximum(jnp.max(logits, axis=-1, keepdims=True), sink)
    e = jnp.exp(logits - m)
    denom = jnp.sum(e, axis=-1, keepdims=True) + jnp.exp(sink - m)
    probs = (e / denom).astype(v.dtype)
    out = jnp.einsum('bkgnqs,bnskd->bnqkgd', probs, vb)
    return out.reshape(B, S, D_ATTN)


def _fwd_setup_inputs(seed: int = 0) -> dict:
    key = jax.random.key(seed)
    ks = jax.random.split(key, 16)
    f32 = jnp.float32
    nrm = lambda k, shape, s: jax.random.normal(k, shape, f32) * s
    return {
        "x": nrm(ks[0], (BATCH, SEQ, D_MODEL), 1.0),
        "p": nrm(ks[1], (DEPTH, BATCH, SEQ, PLE_DIM), 1.0),
        "norm1_gain": 1.0 + nrm(ks[2], (DEPTH, D_MODEL), 0.02),
        "w_in": nrm(ks[3], (DEPTH, D_MODEL, D_IN), D_MODEL ** -0.5),
        "gmlp_v_gain": 1.0 + nrm(ks[4], (DEPTH, D_GMLP), 0.02),
        "w_spatial": nrm(ks[5], (DEPTH, GMLP_GROUPS, CHUNK, CHUNK), CHUNK ** -0.5),
        "b_spatial": 1.0 + nrm(ks[6], (DEPTH, GMLP_GROUPS, CHUNK), 0.01),
        "attn_sinks": nrm(ks[7], (DEPTH, N_Q_HEADS), 0.5),
        "rel_bias_table": nrm(ks[8], (REL_BUCKETS, N_Q_HEADS), 0.5),
        "w_out": nrm(ks[9], (DEPTH, D_MIX, D_MODEL), D_MIX ** -0.5),
        "norm2_gain": 1.0 + nrm(ks[10], (DEPTH, D_MODEL), 0.02),
        "w_ff1": nrm(ks[11], (DEPTH, D_MODEL, D_FF), D_MODEL ** -0.5),
        "w_ff2": nrm(ks[12], (DEPTH, D_FF, D_MODEL), D_FF ** -0.5),
        "w_ple_proj": nrm(ks[13], (DEPTH, PLE_DIM, D_MODEL), PLE_DIM ** -0.5),
        "w_ple_gate": nrm(ks[14], (DEPTH, D_MODEL, D_MODEL), D_MODEL ** -0.5),
        "final_gain": 1.0 + nrm(ks[15], (D_MODEL,), 0.02),
    }


def _fwd_reference(x, p, norm1_gain, w_in, gmlp_v_gain, w_spatial, b_spatial, attn_sinks,
              rel_bias_table, w_out, norm2_gain, w_ff1, w_ff2, w_ple_proj, w_ple_gate,
              final_gain):
    h = x
    for i in range(DEPTH):
        hn = rmsnorm(h, norm1_gain[i])
        z = hn @ w_in[i]
        zg = jax.nn.gelu(z[..., :2 * D_GMLP])
        u, vg = zg[..., :D_GMLP], zg[..., D_GMLP:]
        o = 2 * D_GMLP
        q = z[..., o:o + D_ATTN]
        kv_w = N_KV_HEADS * HEAD_DIM
        k = z[..., o + D_ATTN:o + D_ATTN + kv_w]
        v = z[..., o + D_ATTN + kv_w:]
        a_out = gmlp_chunk_mixer(u, vg, gmlp_v_gain[i], w_spatial[i], b_spatial[i])
        b_out = swa_sink_attention(q, k, v, attn_sinks[i], rel_bias_table)
        h = h + jnp.concatenate([a_out, b_out], axis=-1) @ w_out[i]
        hn = rmsnorm(h, norm2_gain[i])
        h = h + jnp.square(jax.nn.relu(hn @ w_ff1[i])) @ w_ff2[i]
        gate = jax.nn.sigmoid(h @ w_ple_gate[i])
        h = h + gate * (p[i] @ w_ple_proj[i])
    return rmsnorm(h, final_gain)


import jax as _jax
import jax.numpy as _jnp

TWIN_FORMAT = 'train_step'
FWD_PARAMS = ['x', 'p', 'norm1_gain', 'w_in', 'gmlp_v_gain', 'w_spatial', 'b_spatial', 'attn_sinks', 'rel_bias_table', 'w_out', 'norm2_gain', 'w_ff1', 'w_ff2', 'w_ple_proj', 'w_ple_gate', 'final_gain']
TWIN_WEIGHTS = ['norm1_gain', 'w_in', 'gmlp_v_gain', 'w_spatial', 'b_spatial', 'attn_sinks', 'rel_bias_table', 'w_out', 'norm2_gain', 'w_ff1', 'w_ff2', 'w_ple_proj', 'w_ple_gate', 'final_gain']
TWIN_DIFF_INPUT = 'x'
TWIN_INPUTS = ['x', 'p', 'norm1_gain', 'w_in', 'gmlp_v_gain', 'w_spatial', 'b_spatial', 'attn_sinks', 'rel_bias_table', 'w_out', 'norm2_gain', 'w_ff1', 'w_ff2', 'w_ple_proj', 'w_ple_gate', 'final_gain', 'loss_target', 'm_norm1_gain', 'm_w_in', 'm_gmlp_v_gain', 'm_w_spatial', 'm_b_spatial', 'm_attn_sinks', 'm_rel_bias_table', 'm_w_out', 'm_norm2_gain', 'm_w_ff1', 'm_w_ff2', 'm_w_ple_proj', 'm_w_ple_gate', 'm_final_gain', 'v_norm1_gain', 'v_w_in', 'v_gmlp_v_gain', 'v_w_spatial', 'v_b_spatial', 'v_attn_sinks', 'v_rel_bias_table', 'v_w_out', 'v_norm2_gain', 'v_w_ff1', 'v_w_ff2', 'v_w_ple_proj', 'v_w_ple_gate', 'v_final_gain']
TWIN_OUTPUTS = ['loss', 'grad_x', 'grad_norm1_gain', 'grad_w_in', 'grad_gmlp_v_gain', 'grad_w_spatial', 'grad_b_spatial', 'grad_attn_sinks', 'grad_rel_bias_table', 'grad_w_out', 'grad_norm2_gain', 'grad_w_ff1', 'grad_w_ff2', 'grad_w_ple_proj', 'grad_w_ple_gate', 'grad_final_gain', 'delta_norm1_gain', 'delta_w_in', 'delta_gmlp_v_gain', 'delta_w_spatial', 'delta_b_spatial', 'delta_attn_sinks', 'delta_rel_bias_table', 'delta_w_out', 'delta_norm2_gain', 'delta_w_ff1', 'delta_w_ff2', 'delta_w_ple_proj', 'delta_w_ple_gate', 'delta_final_gain', 'new_m_norm1_gain', 'new_m_w_in', 'new_m_gmlp_v_gain', 'new_m_w_spatial', 'new_m_b_spatial', 'new_m_attn_sinks', 'new_m_rel_bias_table', 'new_m_w_out', 'new_m_norm2_gain', 'new_m_w_ff1', 'new_m_w_ff2', 'new_m_w_ple_proj', 'new_m_w_ple_gate', 'new_m_final_gain', 'new_v_norm1_gain', 'new_v_w_in', 'new_v_gmlp_v_gain', 'new_v_w_spatial', 'new_v_b_spatial', 'new_v_attn_sinks', 'new_v_rel_bias_table', 'new_v_w_out', 'new_v_norm2_gain', 'new_v_w_ff1', 'new_v_w_ff2', 'new_v_w_ple_proj', 'new_v_w_ple_gate', 'new_v_final_gain']
TWIN_LEAF_KINDS = {'loss': 'loss', 'grad_x': 'grad_x', 'grad_norm1_gain': 'grad_w', 'grad_w_in': 'grad_w', 'grad_gmlp_v_gain': 'grad_w', 'grad_w_spatial': 'grad_w', 'grad_b_spatial': 'grad_w', 'grad_attn_sinks': 'grad_w', 'grad_rel_bias_table': 'grad_w', 'grad_w_out': 'grad_w', 'grad_norm2_gain': 'grad_w', 'grad_w_ff1': 'grad_w', 'grad_w_ff2': 'grad_w', 'grad_w_ple_proj': 'grad_w', 'grad_w_ple_gate': 'grad_w', 'grad_final_gain': 'grad_w', 'delta_norm1_gain': 'delta_w', 'delta_w_in': 'delta_w', 'delta_gmlp_v_gain': 'delta_w', 'delta_w_spatial': 'delta_w', 'delta_b_spatial': 'delta_w', 'delta_attn_sinks': 'delta_w', 'delta_rel_bias_table': 'delta_w', 'delta_w_out': 'delta_w', 'delta_norm2_gain': 'delta_w', 'delta_w_ff1': 'delta_w', 'delta_w_ff2': 'delta_w', 'delta_w_ple_proj': 'delta_w', 'delta_w_ple_gate': 'delta_w', 'delta_final_gain': 'delta_w', 'new_m_norm1_gain': 'new_m', 'new_m_w_in': 'new_m', 'new_m_gmlp_v_gain': 'new_m', 'new_m_w_spatial': 'new_m', 'new_m_b_spatial': 'new_m', 'new_m_attn_sinks': 'new_m', 'new_m_rel_bias_table': 'new_m', 'new_m_w_out': 'new_m', 'new_m_norm2_gain': 'new_m', 'new_m_w_ff1': 'new_m', 'new_m_w_ff2': 'new_m', 'new_m_w_ple_proj': 'new_m', 'new_m_w_ple_gate': 'new_m', 'new_m_final_gain': 'new_m', 'new_v_norm1_gain': 'new_v', 'new_v_w_in': 'new_v', 'new_v_gmlp_v_gain': 'new_v', 'new_v_w_spatial': 'new_v', 'new_v_b_spatial': 'new_v', 'new_v_attn_sinks': 'new_v', 'new_v_rel_bias_table': 'new_v', 'new_v_w_out': 'new_v', 'new_v_norm2_gain': 'new_v', 'new_v_w_ff1': 'new_v', 'new_v_w_ff2': 'new_v', 'new_v_w_ple_proj': 'new_v', 'new_v_w_ple_gate': 'new_v', 'new_v_final_gain': 'new_v'}


def _forward(args):
    return _fwd_reference(*[args[k] for k in FWD_PARAMS])


def _output_shape():
    out = _jax.eval_shape(lambda: _forward(_fwd_setup_inputs(0)))
    return out.shape, out.dtype

N_MICROBATCH = 1
ADAM_LR = 0.001
ADAM_B1 = 0.9
ADAM_B2 = 0.999
ADAM_EPS = 1e-08
ADAM_WD = 0.01
ADAM_STEP = 10
PER_EXAMPLE_BATCH_AXIS = {'x': 0, 'p': 1, 'loss_target': 0}
SHARED_INPUTS = []
_WEIGHT_DTYPES = {'norm1_gain': _jnp.float32, 'w_in': _jnp.float32, 'gmlp_v_gain': _jnp.float32, 'w_spatial': _jnp.float32, 'b_spatial': _jnp.float32, 'attn_sinks': _jnp.float32, 'rel_bias_table': _jnp.float32, 'w_out': _jnp.float32, 'norm2_gain': _jnp.float32, 'w_ff1': _jnp.float32, 'w_ff2': _jnp.float32, 'w_ple_proj': _jnp.float32, 'w_ple_gate': _jnp.float32, 'final_gain': _jnp.float32}
MOMENT_SCALE = {'norm1_gain': 8.075326e-02, 'w_in': 6.042168e-02, 'gmlp_v_gain': 5.239618e-02, 'w_spatial': 5.158209e-02, 'b_spatial': 7.348417e-02, 'attn_sinks': 2.155401e-02, 'rel_bias_table': 3.054722e-02, 'w_out': 6.671294e-02, 'norm2_gain': 1.045905e-01, 'w_ff1': 5.103385e-02, 'w_ff2': 9.663589e-02, 'w_ple_proj': 4.056737e-02, 'w_ple_gate': 2.259457e-02, 'final_gain': 1.612585e+01}


def _to_microbatches(a, axis):
    t = _jnp.moveaxis(a, axis, 0)
    t = t.reshape((N_MICROBATCH, t.shape[0] // N_MICROBATCH) + t.shape[1:])
    return _jnp.moveaxis(t, 1, axis + 1)


def setup_inputs(seed: int = 0) -> dict:
    inp = _fwd_setup_inputs(seed)
    key = _jax.random.fold_in(_jax.random.key(seed), 7919)
    shape, _ = _output_shape()
    out = dict(inp)
    out["loss_target"] = _jax.random.normal(_jax.random.fold_in(key, 0), shape, _jnp.float32)
    for i, name in enumerate(TWIN_WEIGHTS):
        w = inp[name].astype(_jnp.float32)
        if MOMENT_SCALE is None:
            s = _jnp.sqrt(_jnp.mean(_jnp.square(w)) + 1e-30)
        else:
            s = MOMENT_SCALE[name]
        km, kv = _jax.random.split(_jax.random.fold_in(key, i + 1))
        out[name] = w
        out["m_" + name] = s * _jax.random.normal(km, w.shape, _jnp.float32)
        out["v_" + name] = (s * s) * _jax.random.uniform(kv, w.shape, _jnp.float32, 0.5, 1.5)
    if N_MICROBATCH > 1:
        for name, axis in PER_EXAMPLE_BATCH_AXIS.items():
            out[name] = _to_microbatches(out[name], axis)
    return {'x': out['x'], 'p': out['p'], 'norm1_gain': out['norm1_gain'], 'w_in': out['w_in'], 'gmlp_v_gain': out['gmlp_v_gain'], 'w_spatial': out['w_spatial'], 'b_spatial': out['b_spatial'], 'attn_sinks': out['attn_sinks'], 'rel_bias_table': out['rel_bias_table'], 'w_out': out['w_out'], 'norm2_gain': out['norm2_gain'], 'w_ff1': out['w_ff1'], 'w_ff2': out['w_ff2'], 'w_ple_proj': out['w_ple_proj'], 'w_ple_gate': out['w_ple_gate'], 'final_gain': out['final_gain'], 'loss_target': out['loss_target'], 'm_norm1_gain': out['m_norm1_gain'], 'm_w_in': out['m_w_in'], 'm_gmlp_v_gain': out['m_gmlp_v_gain'], 'm_w_spatial': out['m_w_spatial'], 'm_b_spatial': out['m_b_spatial'], 'm_attn_sinks': out['m_attn_sinks'], 'm_rel_bias_table': out['m_rel_bias_table'], 'm_w_out': out['m_w_out'], 'm_norm2_gain': out['m_norm2_gain'], 'm_w_ff1': out['m_w_ff1'], 'm_w_ff2': out['m_w_ff2'], 'm_w_ple_proj': out['m_w_ple_proj'], 'm_w_ple_gate': out['m_w_ple_gate'], 'm_final_gain': out['m_final_gain'], 'v_norm1_gain': out['v_norm1_gain'], 'v_w_in': out['v_w_in'], 'v_gmlp_v_gain': out['v_gmlp_v_gain'], 'v_w_spatial': out['v_w_spatial'], 'v_b_spatial': out['v_b_spatial'], 'v_attn_sinks': out['v_attn_sinks'], 'v_rel_bias_table': out['v_rel_bias_table'], 'v_w_out': out['v_w_out'], 'v_norm2_gain': out['v_norm2_gain'], 'v_w_ff1': out['v_w_ff1'], 'v_w_ff2': out['v_w_ff2'], 'v_w_ple_proj': out['v_w_ple_proj'], 'v_w_ple_gate': out['v_w_ple_gate'], 'v_final_gain': out['v_final_gain']}


def _loss(weights, diff, rest, loss_target):
    with _jax.named_scope("forward"):
        args = {**rest, TWIN_DIFF_INPUT: diff, **{k: w.astype(_WEIGHT_DTYPES[k]) for k, w in weights.items()}}
        y = _forward(args)
    with _jax.named_scope("loss_head"):
        err = _jnp.square(y.astype(_jnp.float32) - loss_target)
        return 0.5 * _jnp.sum(_jnp.mean(err, axis=-1)) if err.ndim else 0.5 * err


def _adamw(w, g, m, v):
    m = ADAM_B1 * m + (1.0 - ADAM_B1) * g
    v = ADAM_B2 * v + (1.0 - ADAM_B2) * _jnp.square(g)
    m_hat = m / (1.0 - ADAM_B1 ** ADAM_STEP)
    v_hat = v / (1.0 - ADAM_B2 ** ADAM_STEP)
    delta = -ADAM_LR * (m_hat / (_jnp.sqrt(v_hat) + ADAM_EPS) + ADAM_WD * w)
    return delta, m, v


def reference(x, p, norm1_gain, w_in, gmlp_v_gain, w_spatial, b_spatial, attn_sinks, rel_bias_table, w_out, norm2_gain, w_ff1, w_ff2, w_ple_proj, w_ple_gate, final_gain, loss_target, m_norm1_gain, m_w_in, m_gmlp_v_gain, m_w_spatial, m_b_spatial, m_attn_sinks, m_rel_bias_table, m_w_out, m_norm2_gain, m_w_ff1, m_w_ff2, m_w_ple_proj, m_w_ple_gate, m_final_gain, v_norm1_gain, v_w_in, v_gmlp_v_gain, v_w_spatial, v_b_spatial, v_attn_sinks, v_rel_bias_table, v_w_out, v_norm2_gain, v_w_ff1, v_w_ff2, v_w_ple_proj, v_w_ple_gate, v_final_gain):
    given = dict(x=x, p=p, norm1_gain=norm1_gain, w_in=w_in, gmlp_v_gain=gmlp_v_gain, w_spatial=w_spatial, b_spatial=b_spatial, attn_sinks=attn_sinks, rel_bias_table=rel_bias_table, w_out=w_out, norm2_gain=norm2_gain, w_ff1=w_ff1, w_ff2=w_ff2, w_ple_proj=w_ple_proj, w_ple_gate=w_ple_gate, final_gain=final_gain, loss_target=loss_target, m_norm1_gain=m_norm1_gain, m_w_in=m_w_in, m_gmlp_v_gain=m_gmlp_v_gain, m_w_spatial=m_w_spatial, m_b_spatial=m_b_spatial, m_attn_sinks=m_attn_sinks, m_rel_bias_table=m_rel_bias_table, m_w_out=m_w_out, m_norm2_gain=m_norm2_gain, m_w_ff1=m_w_ff1, m_w_ff2=m_w_ff2, m_w_ple_proj=m_w_ple_proj, m_w_ple_gate=m_w_ple_gate, m_final_gain=m_final_gain, v_norm1_gain=v_norm1_gain, v_w_in=v_w_in, v_gmlp_v_gain=v_gmlp_v_gain, v_w_spatial=v_w_spatial, v_b_spatial=v_b_spatial, v_attn_sinks=v_attn_sinks, v_rel_bias_table=v_rel_bias_table, v_w_out=v_w_out, v_norm2_gain=v_norm2_gain, v_w_ff1=v_w_ff1, v_w_ff2=v_w_ff2, v_w_ple_proj=v_w_ple_proj, v_w_ple_gate=v_w_ple_gate, v_final_gain=v_final_gain)
    weights = {n: given[n] for n in TWIN_WEIGHTS}
    shared = {n: given[n] for n in SHARED_INPUTS}
    per_example = {n: given[n] for n in ['x', 'p']}
    grad_fn = _jax.value_and_grad(_loss, argnums=(0, 1))

    def one_microbatch(ex, loss_target):
        ex = dict(ex)
        diff = ex.pop(TWIN_DIFF_INPUT)
        return grad_fn(weights, diff, {**shared, **ex}, loss_target)

    if N_MICROBATCH == 1:
        loss, (grad_w, grad_x) = one_microbatch(per_example, given["loss_target"])
    else:
        def body(carry, xs):
            loss_sum, grad_sum = carry
            l_k, (gw_k, gx_k) = one_microbatch(xs[0], xs[1])
            with _jax.named_scope("update"):
                return (loss_sum + l_k, _jax.tree.map(_jnp.add, grad_sum, gw_k)), gx_k

        init = (_jnp.zeros((), _jnp.float32), _jax.tree.map(_jnp.zeros_like, weights))
        (loss, grad_w), grad_x = _jax.lax.scan(body, init, (per_example, given["loss_target"]))
    with _jax.named_scope("update"):
        delta_w, new_m, new_v = {}, {}, {}
        for n in TWIN_WEIGHTS:
            delta_w[n], new_m[n], new_v[n] = _adamw(weights[n], grad_w[n], given["m_" + n], given["v_" + n])
    return (loss, grad_x, *[grad_w[n] for n in TWIN_WEIGHTS], *[delta_w[n] for n in TWIN_WEIGHTS],
            *[new_m[n] for n in TWIN_WEIGHTS], *[new_v[n] for n in TWIN_WEIGHTS])
```

```python
import functools
import math

import jax
import jax.numpy as jnp
from jax import lax
from jax.experimental import pallas as pl
from jax.experimental.pallas import tpu as pltpu

S = 2048
D = 1024
D_IN = 1792
D_FF = 4096
PLE = 256
N_CHIP = 4
N_GROUP = 4
CHUNK = 128
N_HEAD = 8
N_BLOCK = S // CHUNK
N_BUCKET = 32
EPS = 1e-6
NEG_INF = -1e30
QK_SCALE = 0.125
GELU_C = math.sqrt(2.0 / math.pi)

ADAM_LR = 0.001
ADAM_B1 = 0.9
ADAM_B2 = 0.999
ADAM_EPS = 1e-08
ADAM_WD = 0.01
ADAM_STEP = 10

F32 = jnp.float32
BF16 = jnp.bfloat16
MIB = 1024 * 1024
MESH = pl.DeviceIdType.MESH

NT = (((1,), (1,)), ((), ()))
TN = (((0,), (0,)), ((), ()))


def _dot(a, b):
    return jnp.dot(a, b, preferred_element_type=F32)


def _dot_nt(a, b):
    return lax.dot_general(a, b, NT, preferred_element_type=F32)


def _dot_tn(a, b):
    return lax.dot_general(a, b, TN, preferred_element_type=F32)


def _params(vmem_mib, n_axes=1):
    return pltpu.CompilerParams(dimension_semantics=("arbitrary",) * n_axes, vmem_limit_bytes=vmem_mib * MIB)


def _rms_scale(v):
    return lax.rsqrt(jnp.mean(v * v, axis=-1, keepdims=True) + EPS)


def _rms_bwd(dy_gain, xhat, r):
    return r * (dy_gain - xhat * jnp.mean(dy_gain * xhat, axis=-1, keepdims=True))


def _cast_bf16(a):
    rows, cols = a.shape
    tr = _row_tile(rows)

    def body(a_ref, o_ref):
        o_ref[...] = a_ref[...].astype(BF16)

    return pl.pallas_call(
        body, grid=(rows // tr,), name="cast_bf16",
        in_specs=[pl.BlockSpec((tr, cols), lambda i: (i, 0))],
        out_specs=pl.BlockSpec((tr, cols), lambda i: (i, 0)),
        out_shape=jax.ShapeDtypeStruct((rows, cols), BF16),
        compiler_params=_params(16),
    )(a)


def _in_proj(x, gain1, w_in_t):
    tm = 256

    def body(x_ref, g_ref, w_ref, z_ref, hn_ref):
        xv = x_ref[...]
        hn = (xv * _rms_scale(xv) * g_ref[...]).astype(BF16)
        hn_ref[...] = hn
        z_ref[...] = _dot_nt(hn, w_ref[...])

    return pl.pallas_call(
        body, grid=(S // tm,), name="in_proj",
        in_specs=[pl.BlockSpec((tm, D), lambda i: (i, 0)), pl.BlockSpec((1, D), lambda i: (0, 0)),
                  pl.BlockSpec((D_IN, D), lambda i: (0, 0))],
        out_specs=[pl.BlockSpec((tm, D_IN), lambda i: (i, 0)), pl.BlockSpec((tm, D), lambda i: (i, 0))],
        out_shape=[jax.ShapeDtypeStruct((S, D_IN), F32), jax.ShapeDtypeStruct((S, D), BF16)],
        compiler_params=_params(40),
    )(x, gain1, w_in_t)


def _gelu_parts(v):
    t = jnp.tanh(GELU_C * (v + 0.044715 * (v * v * v)))
    cdf = 0.5 * (1.0 + t)
    return cdf, t


def _band_mask(n):
    a = lax.broadcasted_iota(jnp.int32, (CHUNK, 2 * CHUNK), 0)
    j = lax.broadcasted_iota(jnp.int32, (CHUNK, 2 * CHUNK), 1)
    dist = CHUNK + a - j
    valid = (dist >= 0) & (dist < CHUNK)
    return valid & ((n > 0) | (j >= CHUNK))


def _fill_bias(bucket_ref, table_ref, bias_ref):
    bucket = bucket_ref[...]
    for h in range(N_HEAD):
        acc = jnp.zeros((CHUNK, 2 * CHUNK), F32)
        for b in range(N_BUCKET):
            acc = jnp.where(bucket == b, table_ref[b, h], acc)
        bias_ref[h] = acc


def _fill_tril(ws_ref, wt_ref, wtt_ref=None):
    r = lax.broadcasted_iota(jnp.int32, (CHUNK, CHUNK), 0)
    c = lax.broadcasted_iota(jnp.int32, (CHUNK, CHUNK), 1)
    for g in range(N_GROUP):
        w = jnp.where(c <= r, ws_ref[g], 0.0)
        wt_ref[g] = w.astype(BF16)
        if wtt_ref is not None:
            wtt_ref[g] = w.T.astype(BF16)


def _kv_layouts(kv_prev, kv_cur):
    both = jnp.concatenate([kv_prev, kv_cur], axis=0)
    k = both[:, :128]
    v = both[:, 128:]
    return (k.astype(BF16), pltpu.roll(k, 64, axis=1).astype(BF16),
            v.astype(BF16), pltpu.roll(v, 64, axis=1).astype(BF16))


def _head_place(h):
    pair, pos, kvh = h // 2, h % 2, h // 4
    return pair, pos, kvh == pos


def _softmax_sink(qm, k_use, bias_h, sink, valid):
    s = _dot_nt(qm, k_use) * QK_SCALE + bias_h
    s = jnp.where(valid, s, NEG_INF)
    m = jnp.maximum(jnp.max(s, axis=-1, keepdims=True), sink)
    e = jnp.exp(s - m)
    es = jnp.exp(sink - m)
    denom = jnp.sum(e, axis=-1, keepdims=True) + es
    return e / denom, es / denom


def _mixer_fwd(z, v_gain, w_spatial, b_spatial_t, sinks, rel_table, bucket):
    def body(z_ref, kvp_ref, gain_ref, ws_ref, bt_ref, sink_ref, table_ref, bucket_ref, out_ref, bias_ref, wt_ref):
        n = pl.program_id(0)

        @pl.when(n == 0)
        def _():
            _fill_bias(bucket_ref, table_ref, bias_ref)
            _fill_tril(ws_ref, wt_ref)

        zuv = z_ref[:, :1024]
        cdf, _ = _gelu_parts(zuv)
        guv = zuv * cdf
        for g in range(N_GROUP):
            vg = guv[:, 512 + 128 * g:512 + 128 * (g + 1)]
            vn = vg * _rms_scale(vg) * gain_ref[:, 128 * g:128 * (g + 1)]
            sv = _dot(wt_ref[g], vn.astype(BF16)) + bt_ref[:, g:g + 1]
            out_ref[:, 128 * g:128 * (g + 1)] = (guv[:, 128 * g:128 * (g + 1)] * sv).astype(BF16)

        k_same, k_swap, v_same, v_swap = _kv_layouts(kvp_ref[...], z_ref[:, 1536:1792])
        valid = _band_mask(n)
        lane_half = lax.broadcasted_iota(jnp.int32, (1, 128), 1) // 64
        for pair in range(N_HEAD // 2):
            qq = z_ref[:, 1024 + 128 * pair:1024 + 128 * (pair + 1)]
            acc = jnp.zeros((CHUNK, 128), F32)
            for pos in range(2):
                h = 2 * pair + pos
                _, _, same = _head_place(h)
                qm = jnp.where(lane_half == pos, qq, 0.0).astype(BF16)
                p, _ = _softmax_sink(qm, k_same if same else k_swap, bias_ref[h], sink_ref[h], valid)
                vm = jnp.where(lane_half == pos, v_same if same else v_swap, jnp.zeros((), BF16))
                acc = acc + _dot(p.astype(BF16), vm)
            out_ref[:, 512 + 128 * pair:512 + 128 * (pair + 1)] = acc.astype(BF16)

    return pl.pallas_call(
        body, grid=(N_BLOCK,), name="mixer_fwd",
        in_specs=[pl.BlockSpec((CHUNK, D_IN), lambda n: (n, 0)),
                  pl.BlockSpec((CHUNK, 256), lambda n: (jnp.maximum(n - 1, 0), 6)),
                  pl.BlockSpec((1, 512), lambda n: (0, 0)),
                  pl.BlockSpec((N_GROUP, CHUNK, CHUNK), lambda n: (0, 0, 0)),
                  pl.BlockSpec((CHUNK, N_GROUP), lambda n: (0, 0)),
                  pl.BlockSpec(memory_space=pltpu.SMEM),
                  pl.BlockSpec(memory_space=pltpu.SMEM),
                  pl.BlockSpec((CHUNK, 2 * CHUNK), lambda n: (0, 0))],
        out_specs=pl.BlockSpec((CHUNK, D), lambda n: (n, 0)),
        out_shape=jax.ShapeDtypeStruct((S, D), BF16),
        scratch_shapes=[pltpu.VMEM((N_HEAD, CHUNK, 2 * CHUNK), F32), pltpu.VMEM((N_GROUP, CHUNK, CHUNK), BF16)],
        compiler_params=_params(32),
    )(z, z, v_gain, w_spatial, b_spatial_t, sinks, rel_table, bucket)


def _out_proj(x, mix, w_out, gain2):
    tm = 256

    def body(x_ref, mix_ref, w_ref, g_ref, h1_ref, hn_ref):
        h1 = x_ref[...] + _dot(mix_ref[...], w_ref[...])
        h1_ref[...] = h1
        hn_ref[...] = (h1 * _rms_scale(h1) * g_ref[...]).astype(BF16)

    return pl.pallas_call(
        body, grid=(S // tm,), name="out_proj",
        in_specs=[pl.BlockSpec((tm, D), lambda i: (i, 0)), pl.BlockSpec((tm, D), lambda i: (i, 0)),
                  pl.BlockSpec((D, D), lambda i: (0, 0)), pl.BlockSpec((1, D), lambda i: (0, 0))],
        out_specs=[pl.BlockSpec((tm, D), lambda i: (i, 0)), pl.BlockSpec((tm, D), lambda i: (i, 0))],
        out_shape=[jax.ShapeDtypeStruct((S, D), F32), jax.ShapeDtypeStruct((S, D), BF16)],
        compiler_params=_params(32),
    )(x, mix, w_out, gain2)


def _ffn_fwd(h1, hn2, w_ff1, w_ff2):
    tm = 512
    nj = D_FF // 1024

    def body(h1_ref, hn_ref, w1_ref, w2_ref, h2_ref, r_ref, a_ref, acc_ref):
        j = pl.program_id(1)
        f = _dot(hn_ref[...], w1_ref[0])
        r = jnp.maximum(f, 0.0)
        a = (r * r).astype(BF16)
        r_ref[...] = r.astype(BF16)
        a_ref[...] = a
        part = _dot(a, w2_ref[0])

        @pl.when(j == 0)
        def _():
            acc_ref[...] = part

        @pl.when(j > 0)
        def _():
            acc_ref[...] += part

        @pl.when(j == nj - 1)
        def _():
            h2_ref[...] = h1_ref[...] + acc_ref[...]

    return pl.pallas_call(
        body, grid=(S // tm, nj), name="ffn_fwd",
        in_specs=[pl.BlockSpec((tm, D), lambda i, j: (i, 0)), pl.BlockSpec((tm, D), lambda i, j: (i, 0)),
                  pl.BlockSpec((1, D, 1024), lambda i, j: (j, 0, 0)), pl.BlockSpec((1, 1024, D), lambda i, j: (j, 0, 0))],
        out_specs=[pl.BlockSpec((tm, D), lambda i, j: (i, 0)), pl.BlockSpec((tm, 1024), lambda i, j: (i, j)),
                   pl.BlockSpec((tm, 1024), lambda i, j: (i, j))],
        out_shape=[jax.ShapeDtypeStruct((S, D), F32), jax.ShapeDtypeStruct((S, D_FF), BF16),
                   jax.ShapeDtypeStruct((S, D_FF), BF16)],
        scratch_shapes=[pltpu.VMEM((tm, D), F32)],
        compiler_params=_params(40, 2),
    )(h1, hn2, w_ff1, w_ff2)


def _tail(h2, p, target, w_gate, w_proj, final_gain):
    tm = 256
    steps = S // tm

    def body(h2_ref, p_ref, t_ref, wg_ref, wp_ref, gf_ref, dh2_ref, dwg_ref, dwp_ref, dgf_ref, loss_ref, dwp_acc):
        i = pl.program_id(0)
        h2 = h2_ref[...]
        h2b = h2.astype(BF16)
        pb = p_ref[...].astype(BF16)
        gate = jax.nn.sigmoid(_dot(h2b, wg_ref[...]))
        pp = jnp.concatenate([_dot(pb, wp_ref[j]) for j in range(N_CHIP)], axis=1)
        h3 = h2 + gate * pp
        r3 = _rms_scale(h3)
        xhat = h3 * r3
        gf = gf_ref[...]
        err = xhat * gf - t_ref[...]
        dy = err * (1.0 / D)
        dh3 = _rms_bwd(dy * gf, xhat, r3)
        dgp = (dh3 * pp * gate * (1.0 - gate)).astype(BF16)
        dpp = (dh3 * gate).astype(BF16)
        dh2_ref[...] = dh3 + _dot_nt(dgp, wg_ref[...])
        dwg = _dot_tn(h2b, dgp)
        dwp = _dot_tn(pb, dpp)
        dgf = jnp.sum(dy * xhat, axis=0, keepdims=True)
        sq = jnp.sum(jnp.sum(err * err, axis=1, keepdims=True), axis=0, keepdims=True)

        @pl.when(i == 0)
        def _():
            dwg_ref[...] = dwg
            dwp_acc[...] = dwp
            dgf_ref[...] = dgf
            loss_ref[...] = jnp.broadcast_to(sq, (8, 128))

        @pl.when(i > 0)
        def _():
            dwg_ref[...] += dwg
            dwp_acc[...] += dwp
            dgf_ref[...] += dgf
            loss_ref[...] += jnp.broadcast_to(sq, (8, 128))

        @pl.when(i == steps - 1)
        def _():
            for j in range(N_CHIP):
                dwp_ref[j] = dwp_acc[:, 256 * j:256 * (j + 1)]

    return pl.pallas_call(
        body, grid=(steps,), name="tail",
        in_specs=[pl.BlockSpec((tm, D), lambda i: (i, 0)), pl.BlockSpec((tm, PLE), lambda i: (i, 0)),
                  pl.BlockSpec((tm, D), lambda i: (i, 0)), pl.BlockSpec((D, D), lambda i: (0, 0)),
                  pl.BlockSpec((N_CHIP, PLE, 256), lambda i: (0, 0, 0)), pl.BlockSpec((1, D), lambda i: (0, 0))],
        out_specs=[pl.BlockSpec((tm, D), lambda i: (i, 0)), pl.BlockSpec((D, D), lambda i: (0, 0)),
                   pl.BlockSpec((N_CHIP, PLE, 256), lambda i: (0, 0, 0)), pl.BlockSpec((1, D), lambda i: (0, 0)),
                   pl.BlockSpec((8, 128), lambda i: (0, 0))],
        out_shape=[jax.ShapeDtypeStruct((S, D), F32), jax.ShapeDtypeStruct((D, D), F32),
                   jax.ShapeDtypeStruct((N_CHIP, PLE, 256), F32), jax.ShapeDtypeStruct((1, D), F32),
                   jax.ShapeDtypeStruct((8, 128), F32)],
        scratch_shapes=[pltpu.VMEM((PLE, D), F32)],
        compiler_params=_params(48),
    )(h2, p, target, w_gate, w_proj, final_gain)


def _ffn_bwd_weights(dh2, hn2, r, a, w_ff2):
    tm = 512
    nj = D_FF // 1024

    def body(dh2_ref, hn_ref, r_ref, a_ref, w2_ref, df_ref, dw1_ref, dw2_ref):
        i = pl.program_id(1)
        dh2b = dh2_ref[...].astype(BF16)
        da = _dot_nt(dh2b, w2_ref[0])
        df = (da * (2.0 * r_ref[...].astype(F32))).astype(BF16)
        df_ref[...] = df
        dw1 = _dot_tn(hn_ref[...], df)
        dw2 = _dot_tn(a_ref[...], dh2b)

        @pl.when(i == 0)
        def _():
            dw1_ref[0] = dw1
            dw2_ref[0] = dw2

        @pl.when(i > 0)
        def _():
            dw1_ref[0] += dw1
            dw2_ref[0] += dw2

    return pl.pallas_call(
        body, grid=(nj, S // tm), name="ffn_bwd_weights",
        in_specs=[pl.BlockSpec((tm, D), lambda j, i: (i, 0)), pl.BlockSpec((tm, D), lambda j, i: (i, 0)),
                  pl.BlockSpec((tm, 1024), lambda j, i: (i, j)), pl.BlockSpec((tm, 1024), lambda j, i: (i, j)),
                  pl.BlockSpec((1, 1024, D), lambda j, i: (j, 0, 0))],
        out_specs=[pl.BlockSpec((tm, 1024), lambda j, i: (i, j)), pl.BlockSpec((1, D, 1024), lambda j, i: (j, 0, 0)),
                   pl.BlockSpec((1, 1024, D), lambda j, i: (j, 0, 0))],
        out_shape=[jax.ShapeDtypeStruct((S, D_FF), BF16), jax.ShapeDtypeStruct((nj, D, 1024), F32),
                   jax.ShapeDtypeStruct((nj, 1024, D), F32)],
        compiler_params=_params(48, 2),
    )(dh2, hn2, r, a, w_ff2)


def _ffn_bwd_input(df, w_ff1, dh2, h1, gain2, mix, w_out):
    tm = 256
    nj = D_FF // 1024
    steps = S // tm

    def body(df_ref, w1_ref, dh2_ref, h1_ref, g_ref, mix_ref, wo_ref, dh1_ref, dmix_ref, dwo_ref, dg_ref, acc_ref):
        i = pl.program_id(0)
        j = pl.program_id(1)
        part = _dot_nt(df_ref[...], w1_ref[0])

        @pl.when(j == 0)
        def _():
            acc_ref[...] = part

        @pl.when(j > 0)
        def _():
            acc_ref[...] += part

        @pl.when(j == nj - 1)
        def _():
            dhn = acc_ref[...]
            h1 = h1_ref[...]
            r2 = _rms_scale(h1)
            xhat = h1 * r2
            dh1 = dh2_ref[...] + _rms_bwd(dhn * g_ref[...], xhat, r2)
            dh1_ref[...] = dh1
            dh1b = dh1.astype(BF16)
            dmix_ref[...] = _dot_nt(dh1b, wo_ref[...])
            dwo = _dot_tn(mix_ref[...], dh1b)
            dg = jnp.sum(dhn * xhat, axis=0, keepdims=True)

            @pl.when(i == 0)
            def _():
                dwo_ref[...] = dwo
                dg_ref[...] = dg

            @pl.when(i > 0)
            def _():
                dwo_ref[...] += dwo
                dg_ref[...] += dg

    return pl.pallas_call(
        body, grid=(steps, nj), name="ffn_bwd_input",
        in_specs=[pl.BlockSpec((tm, 1024), lambda i, j: (i, j)), pl.BlockSpec((1, D, 1024), lambda i, j: (j, 0, 0)),
                  pl.BlockSpec((tm, D), lambda i, j: (i, 0)), pl.BlockSpec((tm, D), lambda i, j: (i, 0)),
                  pl.BlockSpec((1, D), lambda i, j: (0, 0)), pl.BlockSpec((tm, D), lambda i, j: (i, 0)),
                  pl.BlockSpec((D, D), lambda i, j: (0, 0))],
        out_specs=[pl.BlockSpec((tm, D), lambda i, j: (i, 0)), pl.BlockSpec((tm, D), lambda i, j: (i, 0)),
                   pl.BlockSpec((D, D), lambda i, j: (0, 0)), pl.BlockSpec((1, D), lambda i, j: (0, 0))],
        out_shape=[jax.ShapeDtypeStruct((S, D), F32), jax.ShapeDtypeStruct((S, D), F32),
                   jax.ShapeDtypeStruct((D, D), F32), jax.ShapeDtypeStruct((1, D), F32)],
        scratch_shapes=[pltpu.VMEM((tm, D), F32)],
        compiler_params=_params(48, 2),
    )(df, w_ff1, dh2, h1, gain2, mix, w_out)


def _mixer_bwd(z, dmix, v_gain, w_spatial, b_spatial_t, sinks, rel_table, bucket):
    def body(z_ref, kvp_ref, dm_ref, gain_ref, ws_ref, bt_ref, sink_ref, table_ref, bucket_ref,
             dz_ref, dws_ref, db_ref, dgain_ref, dsink_ref, drel_ref,
             bias_ref, wt_ref, wtt_ref, dbias_ref, dsv_ref, carry_ref):
        n = pl.program_id(0)

        @pl.when(n == 0)
        def _():
            _fill_bias(bucket_ref, table_ref, bias_ref)
            _fill_tril(ws_ref, wt_ref, wtt_ref)
            dbias_ref[...] = jnp.zeros_like(dbias_ref)
            dsv_ref[...] = jnp.zeros_like(dsv_ref)
            dws_ref[...] = jnp.zeros_like(dws_ref)
            dgain_ref[...] = jnp.zeros_like(dgain_ref)
            dsink_ref[...] = jnp.zeros_like(dsink_ref)

        rows = pl.ds(pl.multiple_of(n * CHUNK, CHUNK), CHUNK)

        zuv = z_ref[:, :1024]
        cdf, t = _gelu_parts(zuv)
        guv = zuv * cdf
        dgelu = cdf + zuv * (0.5 * (1.0 - t * t)) * (GELU_C * (1.0 + 3.0 * 0.044715 * (zuv * zuv)))
        for g in range(N_GROUP):
            lo, hi = 128 * g, 128 * (g + 1)
            u = guv[:, lo:hi]
            vg = guv[:, 512 + lo:512 + hi]
            rr = _rms_scale(vg)
            vhat = vg * rr
            gain = gain_ref[:, lo:hi]
            vnb = (vhat * gain).astype(BF16)
            sv = _dot(wt_ref[g], vnb) + bt_ref[:, g:g + 1]
            da = dm_ref[:, lo:hi]
            dsv = da * u
            dsvb = dsv.astype(BF16)
            dsv_ref[g] += dsv
            dws_ref[g] += _dot_nt(dsvb, vnb)
            dvn = _dot(wtt_ref[g], dsvb)
            dgain_ref[:, lo:hi] += jnp.sum(dvn * vhat, axis=0, keepdims=True)
            dvg = _rms_bwd(dvn * gain, vhat, rr)
            dz_ref[rows, lo:hi] = (da * sv * dgelu[:, lo:hi]).astype(BF16)
            dz_ref[rows, 512 + lo:512 + hi] = (dvg * dgelu[:, 512 + lo:512 + hi]).astype(BF16)

        k_same, k_swap, v_same, v_swap = _kv_layouts(kvp_ref[...], z_ref[:, 1536:1792])
        valid = _band_mask(n)
        lane_half = lax.broadcasted_iota(jnp.int32, (1, 128), 1) // 64
        zero = jnp.zeros((2 * CHUNK, 128), F32)
        dk_same, dk_swap, dv_same, dv_swap = zero, zero, zero, zero
        for pair in range(N_HEAD // 2):
            cols = slice(1024 + 128 * pair, 1024 + 128 * (pair + 1))
            qq = z_ref[:, cols]
            do_pair = dm_ref[:, 512 + 128 * pair:512 + 128 * (pair + 1)]
            dq = jnp.zeros((CHUNK, 128), F32)
            for pos in range(2):
                h = 2 * pair + pos
                _, _, same = _head_place(h)
                on_half = lane_half == pos
                qm = jnp.where(on_half, qq, 0.0).astype(BF16)
                k_use = k_same if same else k_swap
                v_use = v_same if same else v_swap
                p, p_sink = _softmax_sink(qm, k_use, bias_ref[h], sink_ref[h], valid)
                dom = jnp.where(on_half, do_pair, 0.0).astype(BF16)
                dp = _dot_nt(dom, v_use)
                dsum = jnp.sum(p * dp, axis=-1, keepdims=True)
                ds = p * (dp - dsum)
                dbias_ref[h] += ds
                dsink_ref[h:h + 1, :] += jnp.broadcast_to(jnp.sum(-p_sink * dsum, axis=0, keepdims=True), (1, 128))
                dsb = ds.astype(BF16)
                dq = dq + jnp.where(on_half, _dot(dsb, k_use), 0.0)
                dk_h = _dot_tn(dsb, qm)
                dv_h = _dot_tn(p.astype(BF16), dom)
                if same:
                    dk_same, dv_same = dk_same + dk_h, dv_same + dv_h
                else:
                    dk_swap, dv_swap = dk_swap + dk_h, dv_swap + dv_h
            dz_ref[rows, cols] = (dq * QK_SCALE).astype(BF16)
        dk = (dk_same + pltpu.roll(dk_swap, 64, axis=1)) * QK_SCALE
        dv = dv_same + pltpu.roll(dv_swap, 64, axis=1)
        dkv = jnp.concatenate([dk, dv], axis=1)

        @pl.when(n > 0)
        def _():
            prev_rows = pl.ds(pl.multiple_of((n - 1) * CHUNK, CHUNK), CHUNK)
            dz_ref[prev_rows, 1536:1792] = (carry_ref[...] + dkv[:CHUNK]).astype(BF16)

        carry_ref[...] = dkv[CHUNK:]

        @pl.when(n == N_BLOCK - 1)
        def _():
            dz_ref[rows, 1536:1792] = dkv[CHUNK:].astype(BF16)
            r = lax.broadcasted_iota(jnp.int32, (CHUNK, CHUNK), 0)
            c = lax.broadcasted_iota(jnp.int32, (CHUNK, CHUNK), 1)
            for g in range(N_GROUP):
                dws_ref[g] = jnp.where(c <= r, dws_ref[g], 0.0)
                db_ref[g] = jnp.sum(dsv_ref[g], axis=1, keepdims=True)
            bucket = bucket_ref[...]
            for h in range(N_HEAD):
                dbh = dbias_ref[h]
                per_bucket = [jnp.sum(jnp.where(bucket == b, dbh, 0.0), axis=0, keepdims=True) for b in range(N_BUCKET)]
                drel_ref[h] = jnp.sum(jnp.concatenate(per_bucket, axis=0), axis=1, keepdims=True)

    return pl.pallas_call(
        body, grid=(N_BLOCK,), name="mixer_bwd",
        in_specs=[pl.BlockSpec((CHUNK, D_IN), lambda n: (n, 0)),
                  pl.BlockSpec((CHUNK, 256), lambda n: (jnp.maximum(n - 1, 0), 6)),
                  pl.BlockSpec((CHUNK, D), lambda n: (n, 0)),
                  pl.BlockSpec((1, 512), lambda n: (0, 0)),
                  pl.BlockSpec((N_GROUP, CHUNK, CHUNK), lambda n: (0, 0, 0)),
                  pl.BlockSpec((CHUNK, N_GROUP), lambda n: (0, 0)),
                  pl.BlockSpec(memory_space=pltpu.SMEM),
                  pl.BlockSpec(memory_space=pltpu.SMEM),
                  pl.BlockSpec((CHUNK, 2 * CHUNK), lambda n: (0, 0))],
        out_specs=[pl.BlockSpec((S, D_IN), lambda n: (0, 0)),
                   pl.BlockSpec((N_GROUP, CHUNK, CHUNK), lambda n: (0, 0, 0)),
                   pl.BlockSpec((N_GROUP, CHUNK, 1), lambda n: (0, 0, 0)),
                   pl.BlockSpec((1, 512), lambda n: (0, 0)),
                   pl.BlockSpec((N_HEAD, 128), lambda n: (0, 0)),
                   pl.BlockSpec((N_HEAD, N_BUCKET, 1), lambda n: (0, 0, 0))],
        out_shape=[jax.ShapeDtypeStruct((S, D_IN), BF16), jax.ShapeDtypeStruct((N_GROUP, CHUNK, CHUNK), F32),
                   jax.ShapeDtypeStruct((N_GROUP, CHUNK, 1), F32), jax.ShapeDtypeStruct((1, 512), F32),
                   jax.ShapeDtypeStruct((N_HEAD, 128), F32), jax.ShapeDtypeStruct((N_HEAD, N_BUCKET, 1), F32)],
        scratch_shapes=[pltpu.VMEM((N_HEAD, CHUNK, 2 * CHUNK), F32), pltpu.VMEM((N_GROUP, CHUNK, CHUNK), BF16),
                        pltpu.VMEM((N_GROUP, CHUNK, CHUNK), BF16), pltpu.VMEM((N_HEAD, CHUNK, 2 * CHUNK), F32),
                        pltpu.VMEM((N_GROUP, CHUNK, CHUNK), F32), pltpu.VMEM((CHUNK, 256), F32)],
        compiler_params=_params(48),
    )(z, z, dmix, v_gain, w_spatial, b_spatial_t, sinks, rel_table, bucket)


def _in_bwd(dz, hn1, w_in_t, x, dh1, gain1):
    tm = 256

    def body(dz_ref, hn_ref, w_ref, x_ref, dh1_ref, g_ref, dx_ref, dw_ref, dg_ref):
        i = pl.program_id(0)
        dzb = dz_ref[...]
        dhn = _dot(dzb, w_ref[...])
        xv = x_ref[...]
        r1 = _rms_scale(xv)
        xhat = xv * r1
        dx_ref[...] = dh1_ref[...] + _rms_bwd(dhn * g_ref[...], xhat, r1)
        dw = _dot_tn(dzb, hn_ref[...])
        dg = jnp.sum(dhn * xhat, axis=0, keepdims=True)

        @pl.when(i == 0)
        def _():
            dw_ref[...] = dw
            dg_ref[...] = dg

        @pl.when(i > 0)
        def _():
            dw_ref[...] += dw
            dg_ref[...] += dg

    return pl.pallas_call(
        body, grid=(S // tm,), name="in_bwd",
        in_specs=[pl.BlockSpec((tm, D_IN), lambda i: (i, 0)), pl.BlockSpec((tm, D), lambda i: (i, 0)),
                  pl.BlockSpec((D_IN, D), lambda i: (0, 0)), pl.BlockSpec((tm, D), lambda i: (i, 0)),
                  pl.BlockSpec((tm, D), lambda i: (i, 0)), pl.BlockSpec((1, D), lambda i: (0, 0))],
        out_specs=[pl.BlockSpec((tm, D), lambda i: (i, 0)), pl.BlockSpec((D_IN, D), lambda i: (0, 0)),
                   pl.BlockSpec((1, D), lambda i: (0, 0))],
        out_shape=[jax.ShapeDtypeStruct((S, D), F32), jax.ShapeDtypeStruct((D_IN, D), F32),
                   jax.ShapeDtypeStruct((1, D), F32)],
        compiler_params=_params(48),
    )(dz, hn1, w_in_t, x, dh1, gain1)


def _rel_bucket():
    a = jnp.arange(CHUNK)[:, None]
    j = jnp.arange(2 * CHUNK)[None, :]
    n = jnp.maximum(CHUNK + a - j, 0)
    max_exact = N_BUCKET // 2
    nf = jnp.maximum(n, 1).astype(jnp.float32)
    large = max_exact + (jnp.log(nf / max_exact) / math.log(CHUNK / max_exact) * (N_BUCKET - max_exact)).astype(jnp.int32)
    large = jnp.minimum(large, N_BUCKET - 1)
    return jnp.where(n < max_exact, n, large).astype(jnp.int32)


def _local_step(x, p, target, small, w_in_t, w_out, w_ff1, w_ff2, w_proj, w_gate):
    bucket = _rel_bucket()
    sinks = small["attn_sinks"].reshape(N_HEAD)
    b_t = jnp.transpose(small["b_spatial"].reshape(N_GROUP, CHUNK))
    ws = small["w_spatial"].reshape(N_GROUP, CHUNK, CHUNK)
    gain1, gain2 = small["norm1_gain"], small["norm2_gain"]
    v_gain = small["gmlp_v_gain"]
    final_gain = small["final_gain"].reshape(1, D)
    table = small["rel_bias_table"]

    z, hn1 = _in_proj(x, gain1, w_in_t)
    mix = _mixer_fwd(z, v_gain, ws, b_t, sinks, table, bucket)
    h1, hn2 = _out_proj(x, mix, w_out, gain2)
    h2, r, a = _ffn_fwd(h1, hn2, w_ff1, w_ff2)
    dh2, d_gate, d_proj, d_final, sq = _tail(h2, p, target, w_gate, w_proj, final_gain)
    df, d_ff1, d_ff2 = _ffn_bwd_weights(dh2, hn2, r, a, w_ff2)
    dh1, dmix, d_out, d_gain2 = _ffn_bwd_input(df, w_ff1, dh2, h1, gain2, mix, w_out)
    dz, d_ws, d_b, d_vgain, d_sink, d_rel = _mixer_bwd(z, dmix, v_gain, ws, b_t, sinks, table, bucket)
    dx, d_in_t, d_gain1 = _in_bwd(dz, hn1, w_in_t, x, dh1, gain1)

    small_grads = {
        "norm1_gain": d_gain1, "gmlp_v_gain": d_vgain, "w_spatial": d_ws.reshape(1, N_GROUP, CHUNK, CHUNK),
        "b_spatial": d_b.reshape(1, N_GROUP, CHUNK), "attn_sinks": d_sink[:, 0].reshape(1, N_HEAD),
        "rel_bias_table": jnp.transpose(d_rel.reshape(N_HEAD, N_BUCKET)), "norm2_gain": d_gain2,
        "final_gain": d_final.reshape(D),
    }
    return sq, dx, (d_in_t, d_out, d_ff1, d_ff2, d_proj, d_gate), small_grads


HBM_SPEC = pl.BlockSpec(memory_space=pltpu.HBM)
VMEM_SPEC = pl.BlockSpec(memory_space=pltpu.VMEM)


def _mesh_place():
    x, y, c = lax.axis_index("x"), lax.axis_index("y"), lax.axis_index("c")
    others = [(1 - x, y), (x, 1 - y), (1 - x, 1 - y)]
    return x, y, c, others


def _remote(src, dst, send_sem, recv_sem, device):
    return pltpu.make_async_remote_copy(src_ref=src, dst_ref=dst, send_sem=send_sem, recv_sem=recv_sem,
                                        device_id=device, device_id_type=MESH)


def _hbm_like(a, shape=None, dtype=None):
    return jax.ShapeDtypeStruct(a.shape if shape is None else shape, a.dtype if dtype is None else dtype)


def _gather_weights(shards):
    n = len(shards)

    def body(*refs):
        ins, outs = refs[:n], refs[n:2 * n]
        send_sems, recv_sems, local_sems = refs[2 * n:]
        x, y, c, others = _mesh_place()
        me = 2 * x + y
        sibling = (x, y, 1 - c)
        idx = [2 * ox + oy for ox, oy in others]

        local = [pltpu.make_async_copy(ins[w], outs[w].at[me], local_sems.at[w]) for w in range(n)]
        for cp in local:
            cp.start()
        sends = []
        for w in range(n):
            for k, (ox, oy) in enumerate(others):
                cp = _remote(ins[w].at[c], outs[w].at[me, c], send_sems.at[w, k], recv_sems.at[w, k], (ox, oy, c))
                cp.start()
                sends.append(cp)
        for w in range(n):
            for k in range(3):
                landed = outs[w].at[idx[k], c]
                _remote(landed, landed, send_sems.at[w, k], recv_sems.at[w, k], sibling).wait_recv()
                cp = _remote(landed, landed, send_sems.at[w, 3 + k], recv_sems.at[w, 3 + k], sibling)
                cp.start()
                sends.append(cp)
        for w in range(n):
            for k in range(3):
                landed = outs[w].at[idx[k], 1 - c]
                _remote(landed, landed, send_sems.at[w, 3 + k], recv_sems.at[w, 3 + k], sibling).wait_recv()
        for cp in sends:
            cp.wait_send()
        for cp in local:
            cp.wait()

    return pl.pallas_call(
        body, name="gather_weights",
        in_specs=[HBM_SPEC] * n, out_specs=[HBM_SPEC] * n,
        out_shape=[_hbm_like(s, (N_CHIP,) + s.shape) for s in shards],
        scratch_shapes=[pltpu.SemaphoreType.DMA((n, 6)), pltpu.SemaphoreType.DMA((n, 6)), pltpu.SemaphoreType.DMA((n,))],
    )(*shards)


def _sibling_exchange(grads):
    n = len(grads)

    def body(*refs):
        ins, outs = refs[:n], refs[n:2 * n]
        send_sems, recv_sems = refs[2 * n:]
        x, y, c, _ = _mesh_place()
        sibling = (x, y, 1 - c)
        sends = []
        for w in range(n):
            for j in range(N_CHIP):
                cp = _remote(ins[w].at[j, 1 - c], outs[w].at[j], send_sems.at[w, j], recv_sems.at[w, j], sibling)
                cp.start()
                sends.append(cp)
        for cp in sends:
            cp.wait_recv()
        for cp in sends:
            cp.wait_send()

    return pl.pallas_call(
        body, name="sibling_exchange",
        in_specs=[HBM_SPEC] * n, out_specs=[HBM_SPEC] * n,
        out_shape=[_hbm_like(g, (N_CHIP,) + g.shape[2:]) for g in grads],
        scratch_shapes=[pltpu.SemaphoreType.DMA((n, N_CHIP)), pltpu.SemaphoreType.DMA((n, N_CHIP))],
    )(*grads)


def _chip_exchange(sums):
    n = len(sums)

    def body(*refs):
        ins, outs = refs[:n], refs[n:2 * n]
        send_sems, recv_sems, local_sems = refs[2 * n:]
        x, y, c, others = _mesh_place()
        me = 2 * x + y
        idx = [2 * ox + oy for ox, oy in others]
        local = [pltpu.make_async_copy(ins[w].at[me], outs[w].at[me], local_sems.at[w]) for w in range(n)]
        for cp in local:
            cp.start()
        sends = []
        for w in range(n):
            for k, (ox, oy) in enumerate(others):
                cp = _remote(ins[w].at[idx[k]], outs[w].at[me], send_sems.at[w, k], recv_sems.at[w, k], (ox, oy, c))
                cp.start()
                sends.append(cp)
        for w in range(n):
            for k in range(3):
                landed = outs[w].at[idx[k]]
                _remote(landed, landed, send_sems.at[w, k], recv_sems.at[w, k], (x, y, c)).wait_recv()
        for cp in sends:
            cp.wait_send()
        for cp in local:
            cp.wait()

    return pl.pallas_call(
        body, name="chip_exchange",
        in_specs=[HBM_SPEC] * n, out_specs=[HBM_SPEC] * n,
        out_shape=[_hbm_like(s) for s in sums],
        scratch_shapes=[pltpu.SemaphoreType.DMA((n, 3)), pltpu.SemaphoreType.DMA((n, 3)), pltpu.SemaphoreType.DMA((n,))],
    )(*sums)


def _sibling_allgather(halves):
    n = len(halves)

    def body(*refs):
        ins, outs = refs[:n], refs[n:2 * n]
        send_sems, recv_sems, local_sems = refs[2 * n:]
        x, y, c, _ = _mesh_place()
        sibling = (x, y, 1 - c)
        local = [pltpu.make_async_copy(ins[w], outs[w].at[c], local_sems.at[w]) for w in range(n)]
        sends = [_remote(ins[w], outs[w].at[c], send_sems.at[w], recv_sems.at[w], sibling) for w in range(n)]
        for cp in local + sends:
            cp.start()
        for w in range(n):
            landed = outs[w].at[1 - c]
            _remote(landed, landed, send_sems.at[w], recv_sems.at[w], sibling).wait_recv()
        for cp in sends:
            cp.wait_send()
        for cp in local:
            cp.wait()

    return pl.pallas_call(
        body, name="sibling_allgather",
        in_specs=[HBM_SPEC] * n, out_specs=[HBM_SPEC] * n,
        out_shape=[_hbm_like(h, (2,) + h.shape) for h in halves],
        scratch_shapes=[pltpu.SemaphoreType.DMA((n,)), pltpu.SemaphoreType.DMA((n,)), pltpu.SemaphoreType.DMA((n,))],
    )(*halves)


def _row_tile(h):
    return max(t for t in range(16, 257, 16) if h % t == 0)


def _pair_sum(grad, other, core):
    _, _, h, cols = grad.shape
    tr = _row_tile(h)

    def body(core_ref, g_ref, o_ref, out_ref):
        out_ref[0] = (g_ref[0, 0] + o_ref[0]).astype(BF16)

    return pl.pallas_call(
        body, name="pair_sum",
        grid_spec=pltpu.PrefetchScalarGridSpec(
            num_scalar_prefetch=1, grid=(N_CHIP, h // tr),
            in_specs=[pl.BlockSpec((1, 1, tr, cols), lambda j, r, core_ref: (j, core_ref[0], r, 0)),
                      pl.BlockSpec((1, tr, cols), lambda j, r, core_ref: (j, r, 0))],
            out_specs=pl.BlockSpec((1, tr, cols), lambda j, r, core_ref: (j, r, 0))),
        out_shape=jax.ShapeDtypeStruct((N_CHIP, h, cols), BF16),
        compiler_params=_params(16, 2),
    )(core, grad, other)


def _chip_sum(parts):
    _, h, cols = parts.shape
    tr = _row_tile(h)

    def body(p_ref, out_ref):
        out_ref[...] = ((p_ref[0].astype(F32) + p_ref[1].astype(F32)) + p_ref[2].astype(F32)) + p_ref[3].astype(F32)

    return pl.pallas_call(
        body, grid=(h // tr,), name="chip_sum",
        in_specs=[pl.BlockSpec((N_CHIP, tr, cols), lambda r: (0, r, 0))],
        out_specs=pl.BlockSpec((tr, cols), lambda r: (r, 0)),
        out_shape=jax.ShapeDtypeStruct((h, cols), F32),
        compiler_params=_params(16),
    )(parts)


def _adamw_math(w, g, m, v):
    m = ADAM_B1 * m + (1.0 - ADAM_B1) * g
    v = ADAM_B2 * v + (1.0 - ADAM_B2) * (g * g)
    m_hat = m / (1.0 - ADAM_B1 ** ADAM_STEP)
    v_hat = v / (1.0 - ADAM_B2 ** ADAM_STEP)
    delta = -ADAM_LR * (m_hat / (jnp.sqrt(v_hat) + ADAM_EPS) + ADAM_WD * w)
    return delta, m, v


def _adamw(w, g, m, v):
    rows, cols = w.shape
    tr = _row_tile(rows)

    def body(w_ref, g_ref, m_ref, v_ref, d_ref, nm_ref, nv_ref):
        d_ref[...], nm_ref[...], nv_ref[...] = _adamw_math(w_ref[...], g_ref[...], m_ref[...], v_ref[...])

    spec = pl.BlockSpec((tr, cols), lambda r: (r, 0))
    return pl.pallas_call(
        body, grid=(rows // tr,), name="adamw",
        in_specs=[spec] * 4, out_specs=[spec] * 3,
        out_shape=[jax.ShapeDtypeStruct((rows, cols), F32)] * 3,
        compiler_params=_params(24),
    )(w, g, m, v)


SMALL_NAMES = ("norm1_gain", "gmlp_v_gain", "w_spatial", "b_spatial", "attn_sinks", "rel_bias_table", "norm2_gain",
               "final_gain")
PACK_TILE = 8 * 128


def _pack_small(arrays):
    parts = []
    for a in arrays:
        flat = a.reshape(-1)
        rows = -(-flat.shape[0] // PACK_TILE) * 8
        parts.append(jnp.pad(flat, (0, rows * 128 - flat.shape[0])).reshape(rows, 128))
    return jnp.concatenate(parts, axis=0)


def _unpack_small(packed, like):
    out, row = [], 0
    for a in like:
        size = math.prod(a.shape)
        rows = -(-size // PACK_TILE) * 8
        out.append(packed[row:row + rows].reshape(-1)[:size].reshape(a.shape))
        row += rows
    return out


def _small_update(grad, w, m, v):
    rows = grad.shape[0]

    def body(g_ref, w_ref, m_ref, v_ref, tot_ref, d_ref, nm_ref, nv_ref, buf, send_sems, recv_sems):
        x, y, c, _ = _mesh_place()
        me = 4 * x + 2 * y + c
        buf[me] = g_ref[...]
        sends = []
        for k in range(1, 8):
            kx, ky, kc = k // 4, (k // 2) % 2, k % 2
            peer = (1 - x if kx else x, 1 - y if ky else y, 1 - c if kc else c)
            cp = _remote(g_ref, buf.at[me], send_sems.at[k - 1], recv_sems.at[k - 1], peer)
            cp.start()
            sends.append((cp, 4 * peer[0] + 2 * peer[1] + peer[2]))
        for k, (cp, src) in enumerate(sends):
            _remote(g_ref, buf.at[src], send_sems.at[k], recv_sems.at[k], (x, y, c)).wait_recv()
        for cp, _ in sends:
            cp.wait_send()
        total = buf[0]
        for dev in range(1, 8):
            total = total + buf[dev]
        tot_ref[...] = total
        d_ref[...], nm_ref[...], nv_ref[...] = _adamw_math(w_ref[...], total, m_ref[...], v_ref[...])

    return pl.pallas_call(
        body, name="small_update",
        in_specs=[VMEM_SPEC] * 4, out_specs=[VMEM_SPEC] * 4,
        out_shape=[jax.ShapeDtypeStruct((rows, 128), F32)] * 4,
        scratch_shapes=[pltpu.VMEM((8, rows, 128), F32), pltpu.SemaphoreType.DMA((7,)), pltpu.SemaphoreType.DMA((7,))],
        compiler_params=pltpu.CompilerParams(vmem_limit_bytes=24 * MIB),
    )(grad, w, m, v)


def _halves(a):
    return a.reshape(a.shape[:-2] + (2, a.shape[-2] // 2, a.shape[-1]))


def _whole(a):
    return a.reshape(a.shape[:-3] + (2 * a.shape[-2], a.shape[-1]))


def kernel(x, p, norm1_gain, w_in, gmlp_v_gain, w_spatial, b_spatial, attn_sinks, rel_bias_table, w_out, norm2_gain, w_ff1, w_ff2, w_ple_proj, w_ple_gate, final_gain, loss_target, m_norm1_gain, m_w_in, m_gmlp_v_gain, m_w_spatial, m_b_spatial, m_attn_sinks, m_rel_bias_table, m_w_out, m_norm2_gain, m_w_ff1, m_w_ff2, m_w_ple_proj, m_w_ple_gate, m_final_gain, v_norm1_gain, v_w_in, v_gmlp_v_gain, v_w_spatial, v_b_spatial, v_attn_sinks, v_rel_bias_table, v_w_out, v_norm2_gain, v_w_ff1, v_w_ff2, v_w_ple_proj, v_w_ple_gate, v_final_gain):
    given = dict(locals())
    small = {n: given[n] for n in SMALL_NAMES}
    core = lax.axis_index("c").astype(jnp.int32).reshape(1)

    big_names = ("w_in", "w_out", "w_ff1", "w_ff2", "w_ple_proj", "w_ple_gate")
    shards = {n: given[n][0] for n in big_names}
    travel = dict(shards, w_in=jnp.transpose(shards["w_in"]))
    gathered = _gather_weights([_halves(_cast_bf16(travel[n])) for n in big_names])
    w_in_t, w_out_g, w_ff1_g, w_ff2_g, w_proj_g, w_gate_g = [_whole(g) for g in gathered]

    sq, dx, big_grads, small_grads = _local_step(
        x[0], p[0, 0], loss_target[0], small,
        w_in_t.reshape(D_IN, D), w_out_g.reshape(D, D), w_ff1_g, w_ff2_g, w_proj_g, w_gate_g.reshape(D, D))
    loss = lax.psum(0.5 * sq[0, 0] / D, ("x", "y", "c"))

    rows = [travel[n].shape[0] for n in big_names]
    grads = [_halves(g.reshape(N_CHIP, r, g.shape[-1])) for g, r in zip(big_grads, rows)]
    from_sibling = _sibling_exchange(grads)
    chip_sums = [_pair_sum(g, o, core) for g, o in zip(grads, from_sibling)]
    all_chips = _chip_exchange(chip_sums)
    reduced = _sibling_allgather([_chip_sum(a) for a in all_chips])
    big_grad = {n: _whole(r) for n, r in zip(big_names, reduced)}
    big_grad["w_in"] = jnp.transpose(big_grad["w_in"])

    out_grad, out_delta, out_m, out_v = {}, {}, {}, {}
    for n in big_names:
        g = big_grad[n]
        delta, new_m, new_v = _adamw(shards[n], g, given["m_" + n][0], given["v_" + n][0])
        out_grad[n], out_delta[n], out_m[n], out_v[n] = g[None], delta[None], new_m[None], new_v[None]

    packed = _small_update(_pack_small([small_grads[n] for n in SMALL_NAMES]),
                           _pack_small([given[n] for n in SMALL_NAMES]),
                           _pack_small([given["m_" + n] for n in SMALL_NAMES]),
                           _pack_small([given["v_" + n] for n in SMALL_NAMES]))
    like = [given[n] for n in SMALL_NAMES]
    for res, out in zip(packed, (out_grad, out_delta, out_m, out_v)):
        out.update(zip(SMALL_NAMES, _unpack_small(res, like)))

    order = ("norm1_gain", "w_in", "gmlp_v_gain", "w_spatial", "b_spatial", "attn_sinks", "rel_bias_table", "w_out",
             "norm2_gain", "w_ff1", "w_ff2", "w_ple_proj", "w_ple_gate", "final_gain")
    return (loss, dx[None], *[out_grad[n] for n in order], *[out_delta[n] for n in order],
            *[out_m[n] for n in order], *[out_v[n] for n in order])
```

```python
import functools
import math

import jax
import jax.numpy as jnp
from jax import lax
from jax.experimental import pallas as pl
from jax.experimental.pallas import tpu as pltpu

S = 2048
D = 1024
D_IN = 1792
D_FF = 4096
PLE = 256
N_CHIP = 4
N_GROUP = 4
CHUNK = 128
N_HEAD = 8
N_BLOCK = S // CHUNK
N_BUCKET = 32
EPS = 1e-6
NEG_INF = -1e30
QK_SCALE = 0.125
GELU_C = math.sqrt(2.0 / math.pi)

ADAM_LR = 0.001
ADAM_B1 = 0.9
ADAM_B2 = 0.999
ADAM_EPS = 1e-08
ADAM_WD = 0.01
ADAM_STEP = 10

F32 = jnp.float32
BF16 = jnp.bfloat16
MIB = 1024 * 1024
MESH = pl.DeviceIdType.MESH

NT = (((1,), (1,)), ((), ()))
TN = (((0,), (0,)), ((), ()))


def _dot(a, b):
    return jnp.dot(a, b, preferred_element_type=F32)


def _dot_nt(a, b):
    return lax.dot_general(a, b, NT, preferred_element_type=F32)


def _dot_tn(a, b):
    return lax.dot_general(a, b, TN, preferred_element_type=F32)


def _params(vmem_mib, n_axes=1):
    return pltpu.CompilerParams(dimension_semantics=("arbitrary",) * n_axes, vmem_limit_bytes=vmem_mib * MIB)


def _rms_scale(v):
    return lax.rsqrt(jnp.mean(v * v, axis=-1, keepdims=True) + EPS)


def _rms_bwd(dy_gain, xhat, r):
    return r * (dy_gain - xhat * jnp.mean(dy_gain * xhat, axis=-1, keepdims=True))


def _row_tile(h):
    return max(t for t in range(16, 257, 16) if h % t == 0)


def _cast_shard(a, chip):
    rows, cols = a.shape
    h = rows // 2
    tr = _row_tile(h)

    def body(chip_ref, a_ref, o_ref):
        o_ref[0, 0] = a_ref[0].astype(BF16)

    return pl.pallas_call(
        body, name="cast_shard",
        grid_spec=pltpu.PrefetchScalarGridSpec(
            num_scalar_prefetch=1, grid=(2, h // tr),
            in_specs=[pl.BlockSpec((1, tr, cols), lambda s, r, chip_ref: (s, r, 0))],
            out_specs=pl.BlockSpec((1, 1, tr, cols), lambda s, r, chip_ref: (chip_ref[0], s, r, 0))),
        out_shape=jax.ShapeDtypeStruct((N_CHIP, 2, h, cols), BF16),
        compiler_params=_params(16, 2),
    )(chip, a.reshape(2, h, cols))


def _in_proj(x, gain1, w_in_t):
    tm = 256

    def body(x_ref, g_ref, w_ref, z_ref, hn_ref):
        xv = x_ref[...]
        hn = (xv * _rms_scale(xv) * g_ref[...]).astype(BF16)
        hn_ref[...] = hn
        z_ref[...] = _dot_nt(hn, w_ref[...])

    return pl.pallas_call(
        body, grid=(S // tm,), name="in_proj",
        in_specs=[pl.BlockSpec((tm, D), lambda i: (i, 0)), pl.BlockSpec((1, D), lambda i: (0, 0)),
                  pl.BlockSpec((D_IN, D), lambda i: (0, 0))],
        out_specs=[pl.BlockSpec((tm, D_IN), lambda i: (i, 0)), pl.BlockSpec((tm, D), lambda i: (i, 0))],
        out_shape=[jax.ShapeDtypeStruct((S, D_IN), F32), jax.ShapeDtypeStruct((S, D), BF16)],
        compiler_params=_params(40),
    )(x, gain1, w_in_t)


def _gelu_parts(v):
    t = jnp.tanh(GELU_C * (v + 0.044715 * (v * v * v)))
    cdf = 0.5 * (1.0 + t)
    return cdf, t


def _band_mask(n):
    a = lax.broadcasted_iota(jnp.int32, (CHUNK, 2 * CHUNK), 0)
    j = lax.broadcasted_iota(jnp.int32, (CHUNK, 2 * CHUNK), 1)
    dist = CHUNK + a - j
    valid = (dist >= 0) & (dist < CHUNK)
    return valid & ((n > 0) | (j >= CHUNK))


def _fill_bias(bucket_ref, table_ref, bias_ref):
    bucket = bucket_ref[...]
    for h in range(N_HEAD):
        acc = jnp.zeros((CHUNK, 2 * CHUNK), F32)
        for b in range(N_BUCKET):
            acc = jnp.where(bucket == b, table_ref[b, h], acc)
        bias_ref[h] = acc


def _fill_tril(ws_ref, wt_ref, wtt_ref=None):
    r = lax.broadcasted_iota(jnp.int32, (CHUNK, CHUNK), 0)
    c = lax.broadcasted_iota(jnp.int32, (CHUNK, CHUNK), 1)
    for g in range(N_GROUP):
        w = jnp.where(c <= r, ws_ref[g], 0.0)
        wt_ref[g] = w.astype(BF16)
        if wtt_ref is not None:
            wtt_ref[g] = w.T.astype(BF16)


def _kv_layouts(kv_prev, kv_cur):
    both = jnp.concatenate([kv_prev, kv_cur], axis=0)
    k = both[:, :128]
    v = both[:, 128:]
    return (k.astype(BF16), pltpu.roll(k, 64, axis=1).astype(BF16),
            v.astype(BF16), pltpu.roll(v, 64, axis=1).astype(BF16))


def _head_place(h):
    pair, pos, kvh = h // 2, h % 2, h // 4
    return pair, pos, kvh == pos


def _softmax_sink(qm, k_use, bias_h, sink, valid):
    s = _dot_nt(qm, k_use) * QK_SCALE + bias_h
    s = jnp.where(valid, s, NEG_INF)
    m = jnp.maximum(jnp.max(s, axis=-1, keepdims=True), sink)
    e = jnp.exp(s - m)
    es = jnp.exp(sink - m)
    denom = jnp.sum(e, axis=-1, keepdims=True) + es
    return e / denom, es / denom


def _mixer_fwd(z, v_gain, w_spatial, b_spatial_t, sinks, rel_table, bucket):
    def body(z_ref, kvp_ref, gain_ref, ws_ref, bt_ref, sink_ref, table_ref, bucket_ref, out_ref, bias_ref, wt_ref):
        n = pl.program_id(0)

        @pl.when(n == 0)
        def _():
            _fill_bias(bucket_ref, table_ref, bias_ref)
            _fill_tril(ws_ref, wt_ref)

        zuv = z_ref[:, :1024]
        cdf, _ = _gelu_parts(zuv)
        guv = zuv * cdf
        for g in range(N_GROUP):
            vg = guv[:, 512 + 128 * g:512 + 128 * (g + 1)]
            vn = vg * _rms_scale(vg) * gain_ref[:, 128 * g:128 * (g + 1)]
            sv = _dot(wt_ref[g], vn.astype(BF16)) + bt_ref[:, g:g + 1]
            out_ref[:, 128 * g:128 * (g + 1)] = (guv[:, 128 * g:128 * (g + 1)] * sv).astype(BF16)

        k_same, k_swap, v_same, v_swap = _kv_layouts(kvp_ref[...], z_ref[:, 1536:1792])
        valid = _band_mask(n)
        lane_half = lax.broadcasted_iota(jnp.int32, (1, 128), 1) // 64
        for pair in range(N_HEAD // 2):
            qq = z_ref[:, 1024 + 128 * pair:1024 + 128 * (pair + 1)]
            acc = jnp.zeros((CHUNK, 128), F32)
            for pos in range(2):
                h = 2 * pair + pos
                _, _, same = _head_place(h)
                qm = jnp.where(lane_half == pos, qq, 0.0).astype(BF16)
                p, _ = _softmax_sink(qm, k_same if same else k_swap, bias_ref[h], sink_ref[h], valid)
                vm = jnp.where(lane_half == pos, v_same if same else v_swap, jnp.zeros((), BF16))
                acc = acc + _dot(p.astype(BF16), vm)
            out_ref[:, 512 + 128 * pair:512 + 128 * (pair + 1)] = acc.astype(BF16)

    return pl.pallas_call(
        body, grid=(N_BLOCK,), name="mixer_fwd",
        in_specs=[pl.BlockSpec((CHUNK, D_IN), lambda n: (n, 0)),
                  pl.BlockSpec((CHUNK, 256), lambda n: (jnp.maximum(n - 1, 0), 6)),
                  pl.BlockSpec((1, 512), lambda n: (0, 0)),
                  pl.BlockSpec((N_GROUP, CHUNK, CHUNK), lambda n: (0, 0, 0)),
                  pl.BlockSpec((CHUNK, N_GROUP), lambda n: (0, 0)),
                  pl.BlockSpec(memory_space=pltpu.SMEM),
                  pl.BlockSpec(memory_space=pltpu.SMEM),
                  pl.BlockSpec((CHUNK, 2 * CHUNK), lambda n: (0, 0))],
        out_specs=pl.BlockSpec((CHUNK, D), lambda n: (n, 0)),
        out_shape=jax.ShapeDtypeStruct((S, D), BF16),
        scratch_shapes=[pltpu.VMEM((N_HEAD, CHUNK, 2 * CHUNK), F32), pltpu.VMEM((N_GROUP, CHUNK, CHUNK), BF16)],
        compiler_params=_params(32),
    )(z, z, v_gain, w_spatial, b_spatial_t, sinks, rel_table, bucket)


def _out_proj(x, mix, w_out, gain2):
    tm = 256

    def body(x_ref, mix_ref, w_ref, g_ref, h1_ref, hn_ref):
        h1 = x_ref[...] + _dot(mix_ref[...], w_ref[...])
        h1_ref[...] = h1
        hn_ref[...] = (h1 * _rms_scale(h1) * g_ref[...]).astype(BF16)

    return pl.pallas_call(
        body, grid=(S // tm,), name="out_proj",
        in_specs=[pl.BlockSpec((tm, D), lambda i: (i, 0)), pl.BlockSpec((tm, D), lambda i: (i, 0)),
                  pl.BlockSpec((D, D), lambda i: (0, 0)), pl.BlockSpec((1, D), lambda i: (0, 0))],
        out_specs=[pl.BlockSpec((tm, D), lambda i: (i, 0)), pl.BlockSpec((tm, D), lambda i: (i, 0))],
        out_shape=[jax.ShapeDtypeStruct((S, D), F32), jax.ShapeDtypeStruct((S, D), BF16)],
        compiler_params=_params(32),
    )(x, mix, w_out, gain2)


def _ffn_fwd(h1, hn2, w_ff1, w_ff2):
    tm = 512
    nj = D_FF // 1024

    def body(h1_ref, hn_ref, w1_ref, w2_ref, h2_ref, r_ref, a_ref, acc_ref):
        j = pl.program_id(1)
        f = _dot(hn_ref[...], w1_ref[0])
        r = jnp.maximum(f, 0.0)
        a = (r * r).astype(BF16)
        r_ref[...] = r.astype(BF16)
        a_ref[...] = a
        part = _dot(a, w2_ref[0])

        @pl.when(j == 0)
        def _():
            acc_ref[...] = part

        @pl.when(j > 0)
        def _():
            acc_ref[...] += part

        @pl.when(j == nj - 1)
        def _():
            h2_ref[...] = h1_ref[...] + acc_ref[...]

    return pl.pallas_call(
        body, grid=(S // tm, nj), name="ffn_fwd",
        in_specs=[pl.BlockSpec((tm, D), lambda i, j: (i, 0)), pl.BlockSpec((tm, D), lambda i, j: (i, 0)),
                  pl.BlockSpec((1, D, 1024), lambda i, j: (j, 0, 0)), pl.BlockSpec((1, 1024, D), lambda i, j: (j, 0, 0))],
        out_specs=[pl.BlockSpec((tm, D), lambda i, j: (i, 0)), pl.BlockSpec((tm, 1024), lambda i, j: (i, j)),
                   pl.BlockSpec((tm, 1024), lambda i, j: (i, j))],
        out_shape=[jax.ShapeDtypeStruct((S, D), F32), jax.ShapeDtypeStruct((S, D_FF), BF16),
                   jax.ShapeDtypeStruct((S, D_FF), BF16)],
        scratch_shapes=[pltpu.VMEM((tm, D), F32)],
        compiler_params=_params(40, 2),
    )(h1, hn2, w_ff1, w_ff2)


def _tail(h2, p, target, w_gate, w_proj, final_gain):
    tm = 256
    steps = S // tm

    def body(h2_ref, p_ref, t_ref, wg_ref, wp_ref, gf_ref, dh2_ref, dwg_ref, dwp_ref, dgf_ref, loss_ref, dwp_acc):
        i = pl.program_id(0)
        h2 = h2_ref[...]
        h2b = h2.astype(BF16)
        pb = p_ref[...].astype(BF16)
        gate = jax.nn.sigmoid(_dot(h2b, wg_ref[...]))
        pp = jnp.concatenate([_dot(pb, wp_ref[j]) for j in range(N_CHIP)], axis=1)
        h3 = h2 + gate * pp
        r3 = _rms_scale(h3)
        xhat = h3 * r3
        gf = gf_ref[...]
        err = xhat * gf - t_ref[...]
        dy = err * (1.0 / D)
        dh3 = _rms_bwd(dy * gf, xhat, r3)
        dgp = (dh3 * pp * gate * (1.0 - gate)).astype(BF16)
        dpp = (dh3 * gate).astype(BF16)
        dh2_ref[...] = dh3 + _dot_nt(dgp, wg_ref[...])
        dwg = _dot_tn(h2b, dgp)
        dwp = _dot_tn(pb, dpp)
        dgf = jnp.sum(dy * xhat, axis=0, keepdims=True)
        sq = jnp.sum(jnp.sum(err * err, axis=1, keepdims=True), axis=0, keepdims=True)

        @pl.when(i == 0)
        def _():
            dwg_ref[...] = dwg
            dwp_acc[...] = dwp
            dgf_ref[...] = dgf
            loss_ref[...] = jnp.broadcast_to(sq, (8, 128))

        @pl.when(i > 0)
        def _():
            dwg_ref[...] += dwg
            dwp_acc[...] += dwp
            dgf_ref[...] += dgf
            loss_ref[...] += jnp.broadcast_to(sq, (8, 128))

        @pl.when(i == steps - 1)
        def _():
            for j in range(N_CHIP):
                dwp_ref[j] = dwp_acc[:, 256 * j:256 * (j + 1)]

    return pl.pallas_call(
        body, grid=(steps,), name="tail",
        in_specs=[pl.BlockSpec((tm, D), lambda i: (i, 0)), pl.BlockSpec((tm, PLE), lambda i: (i, 0)),
                  pl.BlockSpec((tm, D), lambda i: (i, 0)), pl.BlockSpec((D, D), lambda i: (0, 0)),
                  pl.BlockSpec((N_CHIP, PLE, 256), lambda i: (0, 0, 0)), pl.BlockSpec((1, D), lambda i: (0, 0))],
        out_specs=[pl.BlockSpec((tm, D), lambda i: (i, 0)), pl.BlockSpec((D, D), lambda i: (0, 0)),
                   pl.BlockSpec((N_CHIP, PLE, 256), lambda i: (0, 0, 0)), pl.BlockSpec((1, D), lambda i: (0, 0)),
                   pl.BlockSpec((8, 128), lambda i: (0, 0))],
        out_shape=[jax.ShapeDtypeStruct((S, D), F32), jax.ShapeDtypeStruct((D, D), F32),
                   jax.ShapeDtypeStruct((N_CHIP, PLE, 256), F32), jax.ShapeDtypeStruct((1, D), F32),
                   jax.ShapeDtypeStruct((8, 128), F32)],
        scratch_shapes=[pltpu.VMEM((PLE, D), F32)],
        compiler_params=_params(48),
    )(h2, p, target, w_gate, w_proj, final_gain)


def _ffn_bwd_weights(dh2, hn2, r, a, w_ff2):
    tm = 512
    nj = D_FF // 1024

    def body(dh2_ref, hn_ref, r_ref, a_ref, w2_ref, df_ref, dw1_ref, dw2_ref):
        i = pl.program_id(1)
        dh2b = dh2_ref[...].astype(BF16)
        da = _dot_nt(dh2b, w2_ref[0])
        df = (da * (2.0 * r_ref[...].astype(F32))).astype(BF16)
        df_ref[...] = df
        dw1 = _dot_tn(hn_ref[...], df)
        dw2 = _dot_tn(a_ref[...], dh2b)

        @pl.when(i == 0)
        def _():
            dw1_ref[0] = dw1
            dw2_ref[0] = dw2

        @pl.when(i > 0)
        def _():
            dw1_ref[0] += dw1
            dw2_ref[0] += dw2

    return pl.pallas_call(
        body, grid=(nj, S // tm), name="ffn_bwd_weights",
        in_specs=[pl.BlockSpec((tm, D), lambda j, i: (i, 0)), pl.BlockSpec((tm, D), lambda j, i: (i, 0)),
                  pl.BlockSpec((tm, 1024), lambda j, i: (i, j)), pl.BlockSpec((tm, 1024), lambda j, i: (i, j)),
                  pl.BlockSpec((1, 1024, D), lambda j, i: (j, 0, 0))],
        out_specs=[pl.BlockSpec((tm, 1024), lambda j, i: (i, j)), pl.BlockSpec((1, D, 1024), lambda j, i: (j, 0, 0)),
                   pl.BlockSpec((1, 1024, D), lambda j, i: (j, 0, 0))],
        out_shape=[jax.ShapeDtypeStruct((S, D_FF), BF16), jax.ShapeDtypeStruct((nj, D, 1024), F32),
                   jax.ShapeDtypeStruct((nj, 1024, D), F32)],
        compiler_params=_params(48, 2),
    )(dh2, hn2, r, a, w_ff2)


def _ffn_bwd_input(df, w_ff1, dh2, h1, gain2, mix, w_out):
    tm = 256
    nj = D_FF // 1024
    steps = S // tm

    def body(df_ref, w1_ref, dh2_ref, h1_ref, g_ref, mix_ref, wo_ref, dh1_ref, dmix_ref, dwo_ref, dg_ref, acc_ref):
        i = pl.program_id(0)
        j = pl.program_id(1)
        part = _dot_nt(df_ref[...], w1_ref[0])

        @pl.when(j == 0)
        def _():
            acc_ref[...] = part

        @pl.when(j > 0)
        def _():
            acc_ref[...] += part

        @pl.when(j == nj - 1)
        def _():
            dhn = acc_ref[...]
            h1 = h1_ref[...]
            r2 = _rms_scale(h1)
            xhat = h1 * r2
            dh1 = dh2_ref[...] + _rms_bwd(dhn * g_ref[...], xhat, r2)
            dh1_ref[...] = dh1
            dh1b = dh1.astype(BF16)
            dmix_ref[...] = _dot_nt(dh1b, wo_ref[...])
            dwo = _dot_tn(mix_ref[...], dh1b)
            dg = jnp.sum(dhn * xhat, axis=0, keepdims=True)

            @pl.when(i == 0)
            def _():
                dwo_ref[...] = dwo
                dg_ref[...] = dg

            @pl.when(i > 0)
            def _():
                dwo_ref[...] += dwo
                dg_ref[...] += dg

    return pl.pallas_call(
        body, grid=(steps, nj), name="ffn_bwd_input",
        in_specs=[pl.BlockSpec((tm, 1024), lambda i, j: (i, j)), pl.BlockSpec((1, D, 1024), lambda i, j: (j, 0, 0)),
                  pl.BlockSpec((tm, D), lambda i, j: (i, 0)), pl.BlockSpec((tm, D), lambda i, j: (i, 0)),
                  pl.BlockSpec((1, D), lambda i, j: (0, 0)), pl.BlockSpec((tm, D), lambda i, j: (i, 0)),
                  pl.BlockSpec((D, D), lambda i, j: (0, 0))],
        out_specs=[pl.BlockSpec((tm, D), lambda i, j: (i, 0)), pl.BlockSpec((tm, D), lambda i, j: (i, 0)),
                   pl.BlockSpec((D, D), lambda i, j: (0, 0)), pl.BlockSpec((1, D), lambda i, j: (0, 0))],
        out_shape=[jax.ShapeDtypeStruct((S, D), F32), jax.ShapeDtypeStruct((S, D), F32),
                   jax.ShapeDtypeStruct((D, D), F32), jax.ShapeDtypeStruct((1, D), F32)],
        scratch_shapes=[pltpu.VMEM((tm, D), F32)],
        compiler_params=_params(48, 2),
    )(df, w_ff1, dh2, h1, gain2, mix, w_out)


def _mixer_bwd(z, dmix, v_gain, w_spatial, b_spatial_t, sinks, rel_table, bucket):
    def body(z_ref, kvp_ref, dm_ref, gain_ref, ws_ref, bt_ref, sink_ref, table_ref, bucket_ref,
             dz_ref, dws_ref, db_ref, dgain_ref, dsink_ref, drel_ref,
             bias_ref, wt_ref, wtt_ref, dbias_ref, dsv_ref, carry_ref):
        n = pl.program_id(0)

        @pl.when(n == 0)
        def _():
            _fill_bias(bucket_ref, table_ref, bias_ref)
            _fill_tril(ws_ref, wt_ref, wtt_ref)
            dbias_ref[...] = jnp.zeros_like(dbias_ref)
            dsv_ref[...] = jnp.zeros_like(dsv_ref)
            dws_ref[...] = jnp.zeros_like(dws_ref)
            dgain_ref[...] = jnp.zeros_like(dgain_ref)
            dsink_ref[...] = jnp.zeros_like(dsink_ref)

        rows = pl.ds(pl.multiple_of(n * CHUNK, CHUNK), CHUNK)

        zuv = z_ref[:, :1024]
        cdf, t = _gelu_parts(zuv)
        guv = zuv * cdf
        dgelu = cdf + zuv * (0.5 * (1.0 - t * t)) * (GELU_C * (1.0 + 3.0 * 0.044715 * (zuv * zuv)))
        for g in range(N_GROUP):
            lo, hi = 128 * g, 128 * (g + 1)
            u = guv[:, lo:hi]
            vg = guv[:, 512 + lo:512 + hi]
            rr = _rms_scale(vg)
            vhat = vg * rr
            gain = gain_ref[:, lo:hi]
            vnb = (vhat * gain).astype(BF16)
            sv = _dot(wt_ref[g], vnb) + bt_ref[:, g:g + 1]
            da = dm_ref[:, lo:hi]
            dsv = da * u
            dsvb = dsv.astype(BF16)
            dsv_ref[g] += dsv
            dws_ref[g] += _dot_nt(dsvb, vnb)
            dvn = _dot(wtt_ref[g], dsvb)
            dgain_ref[:, lo:hi] += jnp.sum(dvn * vhat, axis=0, keepdims=True)
            dvg = _rms_bwd(dvn * gain, vhat, rr)
            dz_ref[rows, lo:hi] = (da * sv * dgelu[:, lo:hi]).astype(BF16)
            dz_ref[rows, 512 + lo:512 + hi] = (dvg * dgelu[:, 512 + lo:512 + hi]).astype(BF16)

        k_same, k_swap, v_same, v_swap = _kv_layouts(kvp_ref[...], z_ref[:, 1536:1792])
        valid = _band_mask(n)
        lane_half = lax.broadcasted_iota(jnp.int32, (1, 128), 1) // 64
        zero = jnp.zeros((2 * CHUNK, 128), F32)
        dk_same, dk_swap, dv_same, dv_swap = zero, zero, zero, zero
        for pair in range(N_HEAD // 2):
            cols = slice(1024 + 128 * pair, 1024 + 128 * (pair + 1))
            qq = z_ref[:, cols]
            do_pair = dm_ref[:, 512 + 128 * pair:512 + 128 * (pair + 1)]
            dq = jnp.zeros((CHUNK, 128), F32)
            for pos in range(2):
                h = 2 * pair + pos
                _, _, same = _head_place(h)
                on_half = lane_half == pos
                qm = jnp.where(on_half, qq, 0.0).astype(BF16)
                k_use = k_same if same else k_swap
                v_use = v_same if same else v_swap
                p, p_sink = _softmax_sink(qm, k_use, bias_ref[h], sink_ref[h], valid)
                dom = jnp.where(on_half, do_pair, 0.0).astype(BF16)
                dp = _dot_nt(dom, v_use)
                dsum = jnp.sum(p * dp, axis=-1, keepdims=True)
                ds = p * (dp - dsum)
                dbias_ref[h] += ds
                dsink_ref[h:h + 1, :] += jnp.broadcast_to(jnp.sum(-p_sink * dsum, axis=0, keepdims=True), (1, 128))
                dsb = ds.astype(BF16)
                dq = dq + jnp.where(on_half, _dot(dsb, k_use), 0.0)
                dk_h = _dot_tn(dsb, qm)
                dv_h = _dot_tn(p.astype(BF16), dom)
                if same:
                    dk_same, dv_same = dk_same + dk_h, dv_same + dv_h
                else:
                    dk_swap, dv_swap = dk_swap + dk_h, dv_swap + dv_h
            dz_ref[rows, cols] = (dq * QK_SCALE).astype(BF16)
        dk = (dk_same + pltpu.roll(dk_swap, 64, axis=1)) * QK_SCALE
        dv = dv_same + pltpu.roll(dv_swap, 64, axis=1)
        dkv = jnp.concatenate([dk, dv], axis=1)

        @pl.when(n > 0)
        def _():
            prev_rows = pl.ds(pl.multiple_of((n - 1) * CHUNK, CHUNK), CHUNK)
            dz_ref[prev_rows, 1536:1792] = (carry_ref[...] + dkv[:CHUNK]).astype(BF16)

        carry_ref[...] = dkv[CHUNK:]

        @pl.when(n == N_BLOCK - 1)
        def _():
            dz_ref[rows, 1536:1792] = dkv[CHUNK:].astype(BF16)
            r = lax.broadcasted_iota(jnp.int32, (CHUNK, CHUNK), 0)
            c = lax.broadcasted_iota(jnp.int32, (CHUNK, CHUNK), 1)
            for g in range(N_GROUP):
                dws_ref[g] = jnp.where(c <= r, dws_ref[g], 0.0)
                db_ref[g] = jnp.sum(dsv_ref[g], axis=1, keepdims=True)
            bucket = bucket_ref[...]
            for h in range(N_HEAD):
                dbh = dbias_ref[h]
                per_bucket = [jnp.sum(jnp.where(bucket == b, dbh, 0.0), axis=0, keepdims=True) for b in range(N_BUCKET)]
                drel_ref[h] = jnp.sum(jnp.concatenate(per_bucket, axis=0), axis=1, keepdims=True)

    return pl.pallas_call(
        body, grid=(N_BLOCK,), name="mixer_bwd",
        in_specs=[pl.BlockSpec((CHUNK, D_IN), lambda n: (n, 0)),
                  pl.BlockSpec((CHUNK, 256), lambda n: (jnp.maximum(n - 1, 0), 6)),
                  pl.BlockSpec((CHUNK, D), lambda n: (n, 0)),
                  pl.BlockSpec((1, 512), lambda n: (0, 0)),
                  pl.BlockSpec((N_GROUP, CHUNK, CHUNK), lambda n: (0, 0, 0)),
                  pl.BlockSpec((CHUNK, N_GROUP), lambda n: (0, 0)),
                  pl.BlockSpec(memory_space=pltpu.SMEM),
                  pl.BlockSpec(memory_space=pltpu.SMEM),
                  pl.BlockSpec((CHUNK, 2 * CHUNK), lambda n: (0, 0))],
        out_specs=[pl.BlockSpec((S, D_IN), lambda n: (0, 0)),
                   pl.BlockSpec((N_GROUP, CHUNK, CHUNK), lambda n: (0, 0, 0)),
                   pl.BlockSpec((N_GROUP, CHUNK, 1), lambda n: (0, 0, 0)),
                   pl.BlockSpec((1, 512), lambda n: (0, 0)),
                   pl.BlockSpec((N_HEAD, 128), lambda n: (0, 0)),
                   pl.BlockSpec((N_HEAD, N_BUCKET, 1), lambda n: (0, 0, 0))],
        out_shape=[jax.ShapeDtypeStruct((S, D_IN), BF16), jax.ShapeDtypeStruct((N_GROUP, CHUNK, CHUNK), F32),
                   jax.ShapeDtypeStruct((N_GROUP, CHUNK, 1), F32), jax.ShapeDtypeStruct((1, 512), F32),
                   jax.ShapeDtypeStruct((N_HEAD, 128), F32), jax.ShapeDtypeStruct((N_HEAD, N_BUCKET, 1), F32)],
        scratch_shapes=[pltpu.VMEM((N_HEAD, CHUNK, 2 * CHUNK), F32), pltpu.VMEM((N_GROUP, CHUNK, CHUNK), BF16),
                        pltpu.VMEM((N_GROUP, CHUNK, CHUNK), BF16), pltpu.VMEM((N_HEAD, CHUNK, 2 * CHUNK), F32),
                        pltpu.VMEM((N_GROUP, CHUNK, CHUNK), F32), pltpu.VMEM((CHUNK, 256), F32)],
        compiler_params=_params(48),
    )(z, z, dmix, v_gain, w_spatial, b_spatial_t, sinks, rel_table, bucket)


def _in_bwd(dz, hn1, w_in_t, x, dh1, gain1):
    tm = 256

    def body(dz_ref, hn_ref, w_ref, x_ref, dh1_ref, g_ref, dx_ref, dw_ref, dg_ref):
        i = pl.program_id(0)
        dzb = dz_ref[...]
        dhn = _dot(dzb, w_ref[...])
        xv = x_ref[...]
        r1 = _rms_scale(xv)
        xhat = xv * r1
        dx_ref[...] = dh1_ref[...] + _rms_bwd(dhn * g_ref[...], xhat, r1)
        dw = _dot_tn(dzb, hn_ref[...])
        dg = jnp.sum(dhn * xhat, axis=0, keepdims=True)

        @pl.when(i == 0)
        def _():
            dw_ref[...] = dw
            dg_ref[...] = dg

        @pl.when(i > 0)
        def _():
            dw_ref[...] += dw
            dg_ref[...] += dg

    return pl.pallas_call(
        body, grid=(S // tm,), name="in_bwd",
        in_specs=[pl.BlockSpec((tm, D_IN), lambda i: (i, 0)), pl.BlockSpec((tm, D), lambda i: (i, 0)),
                  pl.BlockSpec((D_IN, D), lambda i: (0, 0)), pl.BlockSpec((tm, D), lambda i: (i, 0)),
                  pl.BlockSpec((tm, D), lambda i: (i, 0)), pl.BlockSpec((1, D), lambda i: (0, 0))],
        out_specs=[pl.BlockSpec((tm, D), lambda i: (i, 0)), pl.BlockSpec((D_IN, D), lambda i: (0, 0)),
                   pl.BlockSpec((1, D), lambda i: (0, 0))],
        out_shape=[jax.ShapeDtypeStruct((S, D), F32), jax.ShapeDtypeStruct((D_IN, D), F32),
                   jax.ShapeDtypeStruct((1, D), F32)],
        compiler_params=_params(48),
    )(dz, hn1, w_in_t, x, dh1, gain1)


def _rel_bucket():
    a = jnp.arange(CHUNK)[:, None]
    j = jnp.arange(2 * CHUNK)[None, :]
    n = jnp.maximum(CHUNK + a - j, 0)
    max_exact = N_BUCKET // 2
    nf = jnp.maximum(n, 1).astype(jnp.float32)
    large = max_exact + (jnp.log(nf / max_exact) / math.log(CHUNK / max_exact) * (N_BUCKET - max_exact)).astype(jnp.int32)
    large = jnp.minimum(large, N_BUCKET - 1)
    return jnp.where(n < max_exact, n, large).astype(jnp.int32)


def _local_step(x, p, target, small, w_in_t, w_out, w_ff1, w_ff2, w_proj, w_gate):
    bucket = _rel_bucket()
    sinks = small["attn_sinks"].reshape(N_HEAD)
    b_t = jnp.transpose(small["b_spatial"].reshape(N_GROUP, CHUNK))
    ws = small["w_spatial"].reshape(N_GROUP, CHUNK, CHUNK)
    gain1, gain2 = small["norm1_gain"], small["norm2_gain"]
    v_gain = small["gmlp_v_gain"]
    final_gain = small["final_gain"].reshape(1, D)
    table = small["rel_bias_table"]

    z, hn1 = _in_proj(x, gain1, w_in_t)
    mix = _mixer_fwd(z, v_gain, ws, b_t, sinks, table, bucket)
    h1, hn2 = _out_proj(x, mix, w_out, gain2)
    h2, r, a = _ffn_fwd(h1, hn2, w_ff1, w_ff2)
    dh2, d_gate, d_proj, d_final, sq = _tail(h2, p, target, w_gate, w_proj, final_gain)
    df, d_ff1, d_ff2 = _ffn_bwd_weights(dh2, hn2, r, a, w_ff2)
    dh1, dmix, d_out, d_gain2 = _ffn_bwd_input(df, w_ff1, dh2, h1, gain2, mix, w_out)
    dz, d_ws, d_b, d_vgain, d_sink, d_rel = _mixer_bwd(z, dmix, v_gain, ws, b_t, sinks, table, bucket)
    dx, d_in_t, d_gain1 = _in_bwd(dz, hn1, w_in_t, x, dh1, gain1)

    small_grads = {
        "norm1_gain": d_gain1, "gmlp_v_gain": d_vgain, "w_spatial": d_ws.reshape(1, N_GROUP, CHUNK, CHUNK),
        "b_spatial": d_b.reshape(1, N_GROUP, CHUNK), "attn_sinks": d_sink[:, 0].reshape(1, N_HEAD),
        "rel_bias_table": jnp.transpose(d_rel.reshape(N_HEAD, N_BUCKET)), "norm2_gain": d_gain2,
        "final_gain": d_final.reshape(D),
    }
    return sq, dx, (d_in_t, d_out, d_ff1, d_ff2, d_proj, d_gate), small_grads


HBM_SPEC = pl.BlockSpec(memory_space=pltpu.HBM)
VMEM_SPEC = pl.BlockSpec(memory_space=pltpu.VMEM)


def _mesh_place():
    x, y, c = lax.axis_index("x"), lax.axis_index("y"), lax.axis_index("c")
    others = [(1 - x, y), (x, 1 - y), (1 - x, 1 - y)]
    return x, y, c, others


def _remote(src, dst, send_sem, recv_sem, device):
    return pltpu.make_async_remote_copy(src_ref=src, dst_ref=dst, send_sem=send_sem, recv_sem=recv_sem,
                                        device_id=device, device_id_type=MESH)


def _hbm_like(a, shape=None, dtype=None):
    return jax.ShapeDtypeStruct(a.shape if shape is None else shape, a.dtype if dtype is None else dtype)


def _gather_start(bufs, send_sems, recv_sems):
    x, y, c, others = _mesh_place()
    me = 2 * x + y
    for w, buf in enumerate(bufs):
        for k, (ox, oy) in enumerate(others):
            mine = buf.at[me, c]
            _remote(mine, mine, send_sems.at[w, k], recv_sems.at[w, k], (ox, oy, c)).start()


def _gather_finish(bufs, send_sems, recv_sems):
    x, y, c, others = _mesh_place()
    me = 2 * x + y
    sibling = (x, y, 1 - c)
    idx = [2 * ox + oy for ox, oy in others]
    for w, buf in enumerate(bufs):
        for k in range(3):
            landed = buf.at[idx[k], c]
            _remote(landed, landed, send_sems.at[w, k], recv_sems.at[w, k], sibling).wait_recv()
            _remote(landed, landed, send_sems.at[w, 3 + k], recv_sems.at[w, 3 + k], sibling).start()
    for w, buf in enumerate(bufs):
        for k in range(3):
            landed = buf.at[idx[k], 1 - c]
            _remote(landed, landed, send_sems.at[w, 3 + k], recv_sems.at[w, 3 + k], sibling).wait_recv()
    for w, buf in enumerate(bufs):
        for k in range(3):
            mine, passed = buf.at[me, c], buf.at[idx[k], c]
            _remote(mine, mine, send_sems.at[w, k], recv_sems.at[w, k], sibling).wait_send()
            _remote(passed, passed, send_sems.at[w, 3 + k], recv_sems.at[w, 3 + k], sibling).wait_send()


def _gather_sems(n):
    return [pltpu.SemaphoreType.DMA((n, 6)), pltpu.SemaphoreType.DMA((n, 6))]


def _gather_weights(bufs):
    n = len(bufs)

    def body(*refs):
        outs = refs[n:2 * n]
        send_sems, recv_sems = refs[2 * n:]
        _gather_start(outs, send_sems, recv_sems)
        _gather_finish(outs, send_sems, recv_sems)

    return pl.pallas_call(
        body, name="gather_weights",
        in_specs=[HBM_SPEC] * n, out_specs=[HBM_SPEC] * n,
        out_shape=[_hbm_like(b) for b in bufs],
        input_output_aliases={w: w for w in range(n)},
        scratch_shapes=_gather_sems(n),
    )(*bufs)


def _sibling_exchange(grads):
    n = len(grads)

    def body(*refs):
        ins, outs = refs[:n], refs[n:2 * n]
        send_sems, recv_sems = refs[2 * n:]
        x, y, c, _ = _mesh_place()
        sibling = (x, y, 1 - c)
        sends = []
        for w in range(n):
            for j in range(N_CHIP):
                cp = _remote(ins[w].at[j, 1 - c], outs[w].at[j], send_sems.at[w, j], recv_sems.at[w, j], sibling)
                cp.start()
                sends.append(cp)
        for cp in sends:
            cp.wait_recv()
        for cp in sends:
            cp.wait_send()

    return pl.pallas_call(
        body, name="sibling_exchange",
        in_specs=[HBM_SPEC] * n, out_specs=[HBM_SPEC] * n,
        out_shape=[_hbm_like(g, (N_CHIP,) + g.shape[2:]) for g in grads],
        scratch_shapes=[pltpu.SemaphoreType.DMA((n, N_CHIP)), pltpu.SemaphoreType.DMA((n, N_CHIP))],
    )(*grads)


def _chip_exchange_start(sums, landing, send_sems, recv_sems):
    x, y, c, others = _mesh_place()
    me = 2 * x + y
    for w in range(len(sums)):
        for k, (ox, oy) in enumerate(others):
            _remote(sums[w].at[2 * ox + oy], landing[w].at[me], send_sems.at[w, k], recv_sems.at[w, k],
                    (ox, oy, c)).start()


def _chip_exchange_finish(sums, landing, send_sems, recv_sems):
    x, y, c, others = _mesh_place()
    for w in range(len(sums)):
        for k, (ox, oy) in enumerate(others):
            piece = landing[w].at[2 * ox + oy]
            _remote(piece, piece, send_sems.at[w, k], recv_sems.at[w, k], (x, y, c)).wait_recv()
    for w in range(len(sums)):
        for k, (ox, oy) in enumerate(others):
            piece = sums[w].at[2 * ox + oy]
            _remote(piece, piece, send_sems.at[w, k], recv_sems.at[w, k], (x, y, c)).wait_send()


def _chip_exchange_sems(n):
    return [pltpu.SemaphoreType.DMA((n, 3)), pltpu.SemaphoreType.DMA((n, 3))]


def _chip_exchange(sums, landing):
    n = len(sums)

    def body(*refs):
        ins, outs = refs[:n], refs[2 * n:3 * n]
        send_sems, recv_sems = refs[3 * n:]
        _chip_exchange_start(ins, outs, send_sems, recv_sems)
        _chip_exchange_finish(ins, outs, send_sems, recv_sems)

    return pl.pallas_call(
        body, name="chip_exchange",
        in_specs=[HBM_SPEC] * (2 * n), out_specs=[HBM_SPEC] * n,
        out_shape=[_hbm_like(b) for b in landing],
        input_output_aliases={n + w: w for w in range(n)},
        scratch_shapes=_chip_exchange_sems(n),
    )(*sums, *landing)


def _sibling_allgather(bufs):
    n = len(bufs)

    def body(*refs):
        outs = refs[n:2 * n]
        send_sems, recv_sems = refs[2 * n:]
        x, y, c, _ = _mesh_place()
        sibling = (x, y, 1 - c)
        sends = [_remote(outs[w].at[c], outs[w].at[c], send_sems.at[w], recv_sems.at[w], sibling) for w in range(n)]
        for cp in sends:
            cp.start()
        for w in range(n):
            landed = outs[w].at[1 - c]
            _remote(landed, landed, send_sems.at[w], recv_sems.at[w], sibling).wait_recv()
        for cp in sends:
            cp.wait_send()

    return pl.pallas_call(
        body, name="sibling_allgather",
        in_specs=[HBM_SPEC] * n, out_specs=[HBM_SPEC] * n,
        out_shape=[_hbm_like(b) for b in bufs],
        input_output_aliases={w: w for w in range(n)},
        scratch_shapes=[pltpu.SemaphoreType.DMA((n,)), pltpu.SemaphoreType.DMA((n,))],
    )(*bufs)


def _pair_sum(grad, other, place):
    _, _, h, cols = grad.shape
    tr = _row_tile(h)

    def body(place_ref, g_ref, o_ref, sums_ref, own_ref):
        s = (g_ref[0, 0] + o_ref[0]).astype(BF16)
        sums_ref[0] = s

        @pl.when(pl.program_id(1) == place_ref[0])
        def _():
            own_ref[0] = s

    return pl.pallas_call(
        body, name="pair_sum",
        grid_spec=pltpu.PrefetchScalarGridSpec(
            num_scalar_prefetch=1, grid=(h // tr, N_CHIP),
            in_specs=[pl.BlockSpec((1, 1, tr, cols), lambda r, j, place_ref: (j, place_ref[1], r, 0)),
                      pl.BlockSpec((1, tr, cols), lambda r, j, place_ref: (j, r, 0))],
            out_specs=[pl.BlockSpec((1, tr, cols), lambda r, j, place_ref: (j, r, 0)),
                       pl.BlockSpec((1, tr, cols), lambda r, j, place_ref: (place_ref[0], r, 0))]),
        out_shape=[jax.ShapeDtypeStruct((N_CHIP, h, cols), BF16)] * 2,
        compiler_params=_params(16, 2),
    )(place, grad, other)


def _chip_sum(parts, place):
    _, h, cols = parts.shape
    tr = _row_tile(h)

    def body(place_ref, p_ref, out_ref):
        out_ref[0] = ((p_ref[0].astype(F32) + p_ref[1].astype(F32)) + p_ref[2].astype(F32)) + p_ref[3].astype(F32)

    return pl.pallas_call(
        body, name="chip_sum",
        grid_spec=pltpu.PrefetchScalarGridSpec(
            num_scalar_prefetch=1, grid=(h // tr,),
            in_specs=[pl.BlockSpec((N_CHIP, tr, cols), lambda r, place_ref: (0, r, 0))],
            out_specs=pl.BlockSpec((1, tr, cols), lambda r, place_ref: (place_ref[1], r, 0))),
        out_shape=jax.ShapeDtypeStruct((2, h, cols), F32),
        compiler_params=_params(16),
    )(place, parts)


def _adamw_math(w, g, m, v):
    m = ADAM_B1 * m + (1.0 - ADAM_B1) * g
    v = ADAM_B2 * v + (1.0 - ADAM_B2) * (g * g)
    m_hat = m / (1.0 - ADAM_B1 ** ADAM_STEP)
    v_hat = v / (1.0 - ADAM_B2 ** ADAM_STEP)
    delta = -ADAM_LR * (m_hat / (jnp.sqrt(v_hat) + ADAM_EPS) + ADAM_WD * w)
    return delta, m, v


def _adamw(w, g, m, v):
    rows, cols = w.shape
    tr = _row_tile(rows)

    def body(w_ref, g_ref, m_ref, v_ref, d_ref, nm_ref, nv_ref):
        d_ref[...], nm_ref[...], nv_ref[...] = _adamw_math(w_ref[...], g_ref[...], m_ref[...], v_ref[...])

    spec = pl.BlockSpec((tr, cols), lambda r: (r, 0))
    return pl.pallas_call(
        body, grid=(rows // tr,), name="adamw",
        in_specs=[spec] * 4, out_specs=[spec] * 3,
        out_shape=[jax.ShapeDtypeStruct((rows, cols), F32)] * 3,
        compiler_params=_params(24),
    )(w, g, m, v)


SMALL_NAMES = ("norm1_gain", "gmlp_v_gain", "w_spatial", "b_spatial", "attn_sinks", "rel_bias_table", "norm2_gain",
               "final_gain")
PACK_TILE = 8 * 128


def _pack_small(arrays):
    parts = []
    for a in arrays:
        flat = a.reshape(-1)
        rows = -(-flat.shape[0] // PACK_TILE) * 8
        parts.append(jnp.pad(flat, (0, rows * 128 - flat.shape[0])).reshape(rows, 128))
    return jnp.concatenate(parts, axis=0)


def _unpack_small(packed, like):
    out, row = [], 0
    for a in like:
        size = math.prod(a.shape)
        rows = -(-size // PACK_TILE) * 8
        out.append(packed[row:row + rows].reshape(-1)[:size].reshape(a.shape))
        row += rows
    return out


def _small_update(grad, w, m, v):
    rows = grad.shape[0]

    def body(g_ref, w_ref, m_ref, v_ref, tot_ref, d_ref, nm_ref, nv_ref, buf, send_sems, recv_sems):
        x, y, c, _ = _mesh_place()
        me = 4 * x + 2 * y + c
        buf[me] = g_ref[...]
        sends = []
        for k in range(1, 8):
            kx, ky, kc = k // 4, (k // 2) % 2, k % 2
            peer = (1 - x if kx else x, 1 - y if ky else y, 1 - c if kc else c)
            cp = _remote(g_ref, buf.at[me], send_sems.at[k - 1], recv_sems.at[k - 1], peer)
            cp.start()
            sends.append((cp, 4 * peer[0] + 2 * peer[1] + peer[2]))
        for k, (cp, src) in enumerate(sends):
            _remote(g_ref, buf.at[src], send_sems.at[k], recv_sems.at[k], (x, y, c)).wait_recv()
        for cp, _ in sends:
            cp.wait_send()
        total = buf[0]
        for dev in range(1, 8):
            total = total + buf[dev]
        tot_ref[...] = total
        d_ref[...], nm_ref[...], nv_ref[...] = _adamw_math(w_ref[...], total, m_ref[...], v_ref[...])

    return pl.pallas_call(
        body, name="small_update",
        in_specs=[VMEM_SPEC] * 4, out_specs=[VMEM_SPEC] * 4,
        out_shape=[jax.ShapeDtypeStruct((rows, 128), F32)] * 4,
        scratch_shapes=[pltpu.VMEM((8, rows, 128), F32), pltpu.SemaphoreType.DMA((7,)), pltpu.SemaphoreType.DMA((7,))],
        compiler_params=pltpu.CompilerParams(vmem_limit_bytes=24 * MIB),
    )(grad, w, m, v)


def _halves(a):
    return a.reshape(a.shape[:-2] + (2, a.shape[-2] // 2, a.shape[-1]))


def _whole(a):
    return a.reshape(a.shape[:-3] + (2 * a.shape[-2], a.shape[-1]))


def kernel(x, p, norm1_gain, w_in, gmlp_v_gain, w_spatial, b_spatial, attn_sinks, rel_bias_table, w_out, norm2_gain, w_ff1, w_ff2, w_ple_proj, w_ple_gate, final_gain, loss_target, m_norm1_gain, m_w_in, m_gmlp_v_gain, m_w_spatial, m_b_spatial, m_attn_sinks, m_rel_bias_table, m_w_out, m_norm2_gain, m_w_ff1, m_w_ff2, m_w_ple_proj, m_w_ple_gate, m_final_gain, v_norm1_gain, v_w_in, v_gmlp_v_gain, v_w_spatial, v_b_spatial, v_attn_sinks, v_rel_bias_table, v_w_out, v_norm2_gain, v_w_ff1, v_w_ff2, v_w_ple_proj, v_w_ple_gate, v_final_gain):
    given = dict(locals())
    small = {n: given[n] for n in SMALL_NAMES}
    chip = 2 * lax.axis_index("x") + lax.axis_index("y")
    place = jnp.stack([chip, lax.axis_index("c")]).astype(jnp.int32)

    big_names = ("w_in", "w_out", "w_ff1", "w_ff2", "w_ple_proj", "w_ple_gate")
    shards = {n: given[n][0] for n in big_names}
    travel = dict(shards, w_in=jnp.transpose(shards["w_in"]))
    gathered = _gather_weights([_cast_shard(travel[n], place[:1]) for n in big_names])
    w_in_t, w_out_g, w_ff1_g, w_ff2_g, w_proj_g, w_gate_g = [_whole(g) for g in gathered]

    sq, dx, big_grads, small_grads = _local_step(
        x[0], p[0, 0], loss_target[0], small,
        w_in_t.reshape(D_IN, D), w_out_g.reshape(D, D), w_ff1_g, w_ff2_g, w_proj_g, w_gate_g.reshape(D, D))

    rows = [travel[n].shape[0] for n in big_names]
    grads = [_halves(g.reshape(N_CHIP, r, g.shape[-1])) for g, r in zip(big_grads, rows)]
    from_sibling = _sibling_exchange(grads)
    chip_sums, landing = zip(*[_pair_sum(g, o, place) for g, o in zip(grads, from_sibling)])
    all_chips = _chip_exchange(chip_sums, landing)
    reduced = _sibling_allgather([_chip_sum(a, place) for a in all_chips])
    big_grad = {n: _whole(r) for n, r in zip(big_names, reduced)}
    big_grad["w_in"] = jnp.transpose(big_grad["w_in"])

    out_grad, out_delta, out_m, out_v = {}, {}, {}, {}
    for n in big_names:
        g = big_grad[n]
        delta, new_m, new_v = _adamw(shards[n], g, given["m_" + n][0], given["v_" + n][0])
        out_grad[n], out_delta[n], out_m[n], out_v[n] = g[None], delta[None], new_m[None], new_v[None]

    no_state = jnp.zeros((8, 128), F32)
    packed = _small_update(_pack_small([small_grads[n] for n in SMALL_NAMES] + [sq]),
                           _pack_small([given[n] for n in SMALL_NAMES] + [no_state]),
                           _pack_small([given["m_" + n] for n in SMALL_NAMES] + [no_state]),
                           _pack_small([given["v_" + n] for n in SMALL_NAMES] + [no_state]))
    like = [given[n] for n in SMALL_NAMES] + [sq]
    for res, out in zip(packed, (out_grad, out_delta, out_m, out_v)):
        out.update(zip(SMALL_NAMES + ("squared_error",), _unpack_small(res, like)))
    loss = 0.5 * out_grad["squared_error"][0, 0] / D

    order = ("norm1_gain", "w_in", "gmlp_v_gain", "w_spatial", "b_spatial", "attn_sinks", "rel_bias_table", "w_out",
             "norm2_gain", "w_ff1", "w_ff2", "w_ple_proj", "w_ple_gate", "final_gain")
    return (loss, dx[None], *[out_grad[n] for n in order], *[out_delta[n] for n in order],
            *[out_m[n] for n in order], *[out_v[n] for n in order])
```

```python
import functools
import math

import jax
import jax.numpy as jnp
from jax import lax
from jax.experimental import pallas as pl
from jax.experimental.pallas import tpu as pltpu

S = 2048
D = 1024
D_IN = 1792
D_FF = 4096
PLE = 256
N_CHIP = 4
N_GROUP = 4
CHUNK = 128
N_HEAD = 8
N_BLOCK = S // CHUNK
N_BUCKET = 32
EPS = 1e-6
NEG_INF = -1e30
QK_SCALE = 0.125
GELU_C = math.sqrt(2.0 / math.pi)

ADAM_LR = 0.001
ADAM_B1 = 0.9
ADAM_B2 = 0.999
ADAM_EPS = 1e-08
ADAM_WD = 0.01
ADAM_STEP = 10

F32 = jnp.float32
BF16 = jnp.bfloat16
MIB = 1024 * 1024
MESH = pl.DeviceIdType.MESH

NT = (((1,), (1,)), ((), ()))
TN = (((0,), (0,)), ((), ()))


def _dot(a, b):
    return jnp.dot(a, b, preferred_element_type=F32)


def _dot_nt(a, b):
    return lax.dot_general(a, b, NT, preferred_element_type=F32)


def _dot_tn(a, b):
    return lax.dot_general(a, b, TN, preferred_element_type=F32)


def _params(vmem_mib, n_axes=1):
    return pltpu.CompilerParams(dimension_semantics=("arbitrary",) * n_axes, vmem_limit_bytes=vmem_mib * MIB)


def _rms_scale(v):
    return lax.rsqrt(jnp.mean(v * v, axis=-1, keepdims=True) + EPS)


def _rms_bwd(dy_gain, xhat, r):
    return r * (dy_gain - xhat * jnp.mean(dy_gain * xhat, axis=-1, keepdims=True))


class _Gather:
    def __init__(self, bufs):
        self.operands = list(bufs)
        self.n_out = len(self.operands)
        self.aliases = {w: w for w in range(self.n_out)}
        self.sems = _gather_sems(self.n_out)

    def start(self, ins, outs, sems):
        _gather_start(outs, *sems)

    def finish(self, ins, outs, sems):
        _gather_finish(outs, *sems)


class _ChipExchange:
    def __init__(self, sums, landing):
        self.n_out = len(landing)
        self.operands = list(sums) + list(landing)
        self.aliases = {self.n_out + w: w for w in range(self.n_out)}
        self.sems = _chip_exchange_sems(self.n_out)

    def start(self, ins, outs, sems):
        _chip_exchange_start(ins[:self.n_out], outs, *sems)

    def finish(self, ins, outs, sems):
        _chip_exchange_finish(ins[:self.n_out], outs, *sems)


def _call(body, operands, *, grid, in_specs, out_specs, out_shape, name, compiler_params, scratch_shapes=(),
          exchange=None):
    if exchange is None:
        res = pl.pallas_call(body, grid=grid, in_specs=in_specs, out_specs=out_specs, out_shape=out_shape, name=name,
                             scratch_shapes=list(scratch_shapes), compiler_params=compiler_params)(*operands)
        return list(res), []
    n_in, n_out, n_scr = len(in_specs), len(out_specs), len(scratch_shapes)
    k_in, k_out = len(exchange.operands), exchange.n_out

    def fused(*refs):
        ins, refs = refs[:n_in], refs[n_in:]
        ex_ins, refs = refs[:k_in], refs[k_in:]
        outs, refs = refs[:n_out], refs[n_out:]
        ex_outs, refs = refs[:k_out], refs[k_out:]
        scratch, sems = refs[:n_scr], refs[n_scr:]
        ids = [pl.program_id(a) for a in range(len(grid))]
        first = functools.reduce(jnp.logical_and, [i == 0 for i in ids])
        last = functools.reduce(jnp.logical_and, [i == g - 1 for i, g in zip(ids, grid)])

        @pl.when(first)
        def _():
            exchange.start(ex_ins, ex_outs, sems)

        body(*ins, *outs, *scratch)

        @pl.when(last)
        def _():
            exchange.finish(ex_ins, ex_outs, sems)

    res = pl.pallas_call(
        fused, grid=grid, name=name,
        in_specs=list(in_specs) + [HBM_SPEC] * k_in, out_specs=list(out_specs) + [HBM_SPEC] * k_out,
        out_shape=list(out_shape) + [_hbm_like(exchange.operands[i]) for i in sorted(exchange.aliases)],
        input_output_aliases={n_in + i: n_out + o for i, o in exchange.aliases.items()},
        scratch_shapes=list(scratch_shapes) + exchange.sems, compiler_params=compiler_params,
    )(*operands, *exchange.operands)
    return list(res[:n_out]), list(res[n_out:])


def _row_tile(h):
    return max(t for t in range(16, 257, 16) if h % t == 0)


def _cast_shard(a, chip):
    rows, cols = a.shape
    h = rows // 2
    tr = _row_tile(h)

    def body(chip_ref, a_ref, o_ref):
        o_ref[0, 0] = a_ref[0].astype(BF16)

    return pl.pallas_call(
        body, name="cast_shard",
        grid_spec=pltpu.PrefetchScalarGridSpec(
            num_scalar_prefetch=1, grid=(2, h // tr),
            in_specs=[pl.BlockSpec((1, tr, cols), lambda s, r, chip_ref: (s, r, 0))],
            out_specs=pl.BlockSpec((1, 1, tr, cols), lambda s, r, chip_ref: (chip_ref[0], s, r, 0))),
        out_shape=jax.ShapeDtypeStruct((N_CHIP, 2, h, cols), BF16),
        compiler_params=_params(16, 2),
    )(chip, a.reshape(2, h, cols))


def _in_proj(x, gain1, w_in_t, exchange=None):
    tm = 256

    def body(x_ref, g_ref, w_ref, z_ref, hn_ref):
        xv = x_ref[...]
        hn = (xv * _rms_scale(xv) * g_ref[...]).astype(BF16)
        hn_ref[...] = hn
        z_ref[...] = _dot_nt(hn, w_ref[...])

    return _call(
        body, (x, gain1, w_in_t), grid=(S // tm,), name="in_proj",
        in_specs=[pl.BlockSpec((tm, D), lambda i: (i, 0)), pl.BlockSpec((1, D), lambda i: (0, 0)),
                  pl.BlockSpec((D_IN, D), lambda i: (0, 0))],
        out_specs=[pl.BlockSpec((tm, D_IN), lambda i: (i, 0)), pl.BlockSpec((tm, D), lambda i: (i, 0))],
        out_shape=[jax.ShapeDtypeStruct((S, D_IN), F32), jax.ShapeDtypeStruct((S, D), BF16)],
        compiler_params=_params(40), exchange=exchange)


def _gelu_parts(v):
    t = jnp.tanh(GELU_C * (v + 0.044715 * (v * v * v)))
    cdf = 0.5 * (1.0 + t)
    return cdf, t


def _band_mask(n):
    a = lax.broadcasted_iota(jnp.int32, (CHUNK, 2 * CHUNK), 0)
    j = lax.broadcasted_iota(jnp.int32, (CHUNK, 2 * CHUNK), 1)
    dist = CHUNK + a - j
    valid = (dist >= 0) & (dist < CHUNK)
    return valid & ((n > 0) | (j >= CHUNK))


def _fill_bias(bucket_ref, table_ref, bias_ref):
    bucket = bucket_ref[...]
    for h in range(N_HEAD):
        acc = jnp.zeros((CHUNK, 2 * CHUNK), F32)
        for b in range(N_BUCKET):
            acc = jnp.where(bucket == b, table_ref[b, h], acc)
        bias_ref[h] = acc


def _fill_tril(ws_ref, wt_ref, wtt_ref=None):
    r = lax.broadcasted_iota(jnp.int32, (CHUNK, CHUNK), 0)
    c = lax.broadcasted_iota(jnp.int32, (CHUNK, CHUNK), 1)
    for g in range(N_GROUP):
        w = jnp.where(c <= r, ws_ref[g], 0.0)
        wt_ref[g] = w.astype(BF16)
        if wtt_ref is not None:
            wtt_ref[g] = w.T.astype(BF16)


def _kv_layouts(kv_prev, kv_cur):
    both = jnp.concatenate([kv_prev, kv_cur], axis=0)
    k = both[:, :128]
    v = both[:, 128:]
    return (k.astype(BF16), pltpu.roll(k, 64, axis=1).astype(BF16),
            v.astype(BF16), pltpu.roll(v, 64, axis=1).astype(BF16))


def _head_place(h):
    pair, pos, kvh = h // 2, h % 2, h // 4
    return pair, pos, kvh == pos


def _softmax_sink(qm, k_use, bias_h, sink, valid):
    s = _dot_nt(qm, k_use) * QK_SCALE + bias_h
    s = jnp.where(valid, s, NEG_INF)
    m = jnp.maximum(jnp.max(s, axis=-1, keepdims=True), sink)
    e = jnp.exp(s - m)
    es = jnp.exp(sink - m)
    denom = jnp.sum(e, axis=-1, keepdims=True) + es
    return e / denom, es / denom


def _mixer_fwd(z, v_gain, w_spatial, b_spatial_t, sinks, rel_table, bucket, exchange=None):
    def body(z_ref, kvp_ref, gain_ref, ws_ref, bt_ref, sink_ref, table_ref, bucket_ref, out_ref, bias_ref, wt_ref):
        n = pl.program_id(0)

        @pl.when(n == 0)
        def _():
            _fill_bias(bucket_ref, table_ref, bias_ref)
            _fill_tril(ws_ref, wt_ref)

        zuv = z_ref[:, :1024]
        cdf, _ = _gelu_parts(zuv)
        guv = zuv * cdf
        for g in range(N_GROUP):
            vg = guv[:, 512 + 128 * g:512 + 128 * (g + 1)]
            vn = vg * _rms_scale(vg) * gain_ref[:, 128 * g:128 * (g + 1)]
            sv = _dot(wt_ref[g], vn.astype(BF16)) + bt_ref[:, g:g + 1]
            out_ref[:, 128 * g:128 * (g + 1)] = (guv[:, 128 * g:128 * (g + 1)] * sv).astype(BF16)

        k_same, k_swap, v_same, v_swap = _kv_layouts(kvp_ref[...], z_ref[:, 1536:1792])
        valid = _band_mask(n)
        lane_half = lax.broadcasted_iota(jnp.int32, (1, 128), 1) // 64
        for pair in range(N_HEAD // 2):
            qq = z_ref[:, 1024 + 128 * pair:1024 + 128 * (pair + 1)]
            acc = jnp.zeros((CHUNK, 128), F32)
            for pos in range(2):
                h = 2 * pair + pos
                _, _, same = _head_place(h)
                qm = jnp.where(lane_half == pos, qq, 0.0).astype(BF16)
                p, _ = _softmax_sink(qm, k_same if same else k_swap, bias_ref[h], sink_ref[h], valid)
                vm = jnp.where(lane_half == pos, v_same if same else v_swap, jnp.zeros((), BF16))
                acc = acc + _dot(p.astype(BF16), vm)
            out_ref[:, 512 + 128 * pair:512 + 128 * (pair + 1)] = acc.astype(BF16)

    return _call(
        body, (z, z, v_gain, w_spatial, b_spatial_t, sinks, rel_table, bucket), grid=(N_BLOCK,), name="mixer_fwd",
        in_specs=[pl.BlockSpec((CHUNK, D_IN), lambda n: (n, 0)),
                  pl.BlockSpec((CHUNK, 256), lambda n: (jnp.maximum(n - 1, 0), 6)),
                  pl.BlockSpec((1, 512), lambda n: (0, 0)),
                  pl.BlockSpec((N_GROUP, CHUNK, CHUNK), lambda n: (0, 0, 0)),
                  pl.BlockSpec((CHUNK, N_GROUP), lambda n: (0, 0)),
                  pl.BlockSpec(memory_space=pltpu.SMEM),
                  pl.BlockSpec(memory_space=pltpu.SMEM),
                  pl.BlockSpec((CHUNK, 2 * CHUNK), lambda n: (0, 0))],
        out_specs=[pl.BlockSpec((CHUNK, D), lambda n: (n, 0))],
        out_shape=[jax.ShapeDtypeStruct((S, D), BF16)],
        scratch_shapes=[pltpu.VMEM((N_HEAD, CHUNK, 2 * CHUNK), F32), pltpu.VMEM((N_GROUP, CHUNK, CHUNK), BF16)],
        compiler_params=_params(32), exchange=exchange)


def _out_proj(x, mix, w_out, gain2, exchange=None):
    tm = 256

    def body(x_ref, mix_ref, w_ref, g_ref, h1_ref, hn_ref):
        h1 = x_ref[...] + _dot(mix_ref[...], w_ref[...])
        h1_ref[...] = h1
        hn_ref[...] = (h1 * _rms_scale(h1) * g_ref[...]).astype(BF16)

    return _call(
        body, (x, mix, w_out, gain2), grid=(S // tm,), name="out_proj",
        in_specs=[pl.BlockSpec((tm, D), lambda i: (i, 0)), pl.BlockSpec((tm, D), lambda i: (i, 0)),
                  pl.BlockSpec((D, D), lambda i: (0, 0)), pl.BlockSpec((1, D), lambda i: (0, 0))],
        out_specs=[pl.BlockSpec((tm, D), lambda i: (i, 0)), pl.BlockSpec((tm, D), lambda i: (i, 0))],
        out_shape=[jax.ShapeDtypeStruct((S, D), F32), jax.ShapeDtypeStruct((S, D), BF16)],
        compiler_params=_params(32), exchange=exchange)


def _ffn_fwd(h1, hn2, w_ff1, w_ff2):
    tm = 512
    nj = D_FF // 1024

    def body(h1_ref, hn_ref, w1_ref, w2_ref, h2_ref, r_ref, a_ref, acc_ref):
        j = pl.program_id(1)
        f = _dot(hn_ref[...], w1_ref[0])
        r = jnp.maximum(f, 0.0)
        a = (r * r).astype(BF16)
        r_ref[...] = r.astype(BF16)
        a_ref[...] = a
        part = _dot(a, w2_ref[0])

        @pl.when(j == 0)
        def _():
            acc_ref[...] = part

        @pl.when(j > 0)
        def _():
            acc_ref[...] += part

        @pl.when(j == nj - 1)
        def _():
            h2_ref[...] = h1_ref[...] + acc_ref[...]

    return pl.pallas_call(
        body, grid=(S // tm, nj), name="ffn_fwd",
        in_specs=[pl.BlockSpec((tm, D), lambda i, j: (i, 0)), pl.BlockSpec((tm, D), lambda i, j: (i, 0)),
                  pl.BlockSpec((1, D, 1024), lambda i, j: (j, 0, 0)), pl.BlockSpec((1, 1024, D), lambda i, j: (j, 0, 0))],
        out_specs=[pl.BlockSpec((tm, D), lambda i, j: (i, 0)), pl.BlockSpec((tm, 1024), lambda i, j: (i, j)),
                   pl.BlockSpec((tm, 1024), lambda i, j: (i, j))],
        out_shape=[jax.ShapeDtypeStruct((S, D), F32), jax.ShapeDtypeStruct((S, D_FF), BF16),
                   jax.ShapeDtypeStruct((S, D_FF), BF16)],
        scratch_shapes=[pltpu.VMEM((tm, D), F32)],
        compiler_params=_params(40, 2),
    )(h1, hn2, w_ff1, w_ff2)


def _tail(h2, p, target, w_gate, w_proj, final_gain):
    tm = 256
    steps = S // tm

    def body(h2_ref, p_ref, t_ref, wg_ref, wp_ref, gf_ref, dh2_ref, dwg_ref, dwp_ref, dgf_ref, loss_ref, dwp_acc):
        i = pl.program_id(0)
        h2 = h2_ref[...]
        h2b = h2.astype(BF16)
        pb = p_ref[...].astype(BF16)
        gate = jax.nn.sigmoid(_dot(h2b, wg_ref[...]))
        pp = jnp.concatenate([_dot(pb, wp_ref[j]) for j in range(N_CHIP)], axis=1)
        h3 = h2 + gate * pp
        r3 = _rms_scale(h3)
        xhat = h3 * r3
        gf = gf_ref[...]
        err = xhat * gf - t_ref[...]
        dy = err * (1.0 / D)
        dh3 = _rms_bwd(dy * gf, xhat, r3)
        dgp = (dh3 * pp * gate * (1.0 - gate)).astype(BF16)
        dpp = (dh3 * gate).astype(BF16)
        dh2_ref[...] = dh3 + _dot_nt(dgp, wg_ref[...])
        dwg = _dot_tn(h2b, dgp)
        dwp = _dot_tn(pb, dpp)
        dgf = jnp.sum(dy * xhat, axis=0, keepdims=True)
        sq = jnp.sum(jnp.sum(err * err, axis=1, keepdims=True), axis=0, keepdims=True)

        @pl.when(i == 0)
        def _():
            dwg_ref[...] = dwg
            dwp_acc[...] = dwp
            dgf_ref[...] = dgf
            loss_ref[...] = jnp.broadcast_to(sq, (8, 128))

        @pl.when(i > 0)
        def _():
            dwg_ref[...] += dwg
            dwp_acc[...] += dwp
            dgf_ref[...] += dgf
            loss_ref[...] += jnp.broadcast_to(sq, (8, 128))

        @pl.when(i == steps - 1)
        def _():
            for j in range(N_CHIP):
                dwp_ref[j] = dwp_acc[:, 256 * j:256 * (j + 1)]

    return pl.pallas_call(
        body, grid=(steps,), name="tail",
        in_specs=[pl.BlockSpec((tm, D), lambda i: (i, 0)), pl.BlockSpec((tm, PLE), lambda i: (i, 0)),
                  pl.BlockSpec((tm, D), lambda i: (i, 0)), pl.BlockSpec((D, D), lambda i: (0, 0)),
                  pl.BlockSpec((N_CHIP, PLE, 256), lambda i: (0, 0, 0)), pl.BlockSpec((1, D), lambda i: (0, 0))],
        out_specs=[pl.BlockSpec((tm, D), lambda i: (i, 0)), pl.BlockSpec((D, D), lambda i: (0, 0)),
                   pl.BlockSpec((N_CHIP, PLE, 256), lambda i: (0, 0, 0)), pl.BlockSpec((1, D), lambda i: (0, 0)),
                   pl.BlockSpec((8, 128), lambda i: (0, 0))],
        out_shape=[jax.ShapeDtypeStruct((S, D), F32), jax.ShapeDtypeStruct((D, D), F32),
                   jax.ShapeDtypeStruct((N_CHIP, PLE, 256), F32), jax.ShapeDtypeStruct((1, D), F32),
                   jax.ShapeDtypeStruct((8, 128), F32)],
        scratch_shapes=[pltpu.VMEM((PLE, D), F32)],
        compiler_params=_params(48),
    )(h2, p, target, w_gate, w_proj, final_gain)


def _ffn_bwd_weights(dh2, hn2, r, a, w_ff2, exchange=None):
    tm = 512
    nj = D_FF // 1024

    def body(dh2_ref, hn_ref, r_ref, a_ref, w2_ref, df_ref, dw1_ref, dw2_ref):
        i = pl.program_id(1)
        dh2b = dh2_ref[...].astype(BF16)
        da = _dot_nt(dh2b, w2_ref[0])
        df = (da * (2.0 * r_ref[...].astype(F32))).astype(BF16)
        df_ref[...] = df
        dw1 = _dot_tn(hn_ref[...], df)
        dw2 = _dot_tn(a_ref[...], dh2b)

        @pl.when(i == 0)
        def _():
            dw1_ref[0] = dw1
            dw2_ref[0] = dw2

        @pl.when(i > 0)
        def _():
            dw1_ref[0] += dw1
            dw2_ref[0] += dw2

    return _call(
        body, (dh2, hn2, r, a, w_ff2), grid=(nj, S // tm), name="ffn_bwd_weights",
        in_specs=[pl.BlockSpec((tm, D), lambda j, i: (i, 0)), pl.BlockSpec((tm, D), lambda j, i: (i, 0)),
                  pl.BlockSpec((tm, 1024), lambda j, i: (i, j)), pl.BlockSpec((tm, 1024), lambda j, i: (i, j)),
                  pl.BlockSpec((1, 1024, D), lambda j, i: (j, 0, 0))],
        out_specs=[pl.BlockSpec((tm, 1024), lambda j, i: (i, j)), pl.BlockSpec((1, D, 1024), lambda j, i: (j, 0, 0)),
                   pl.BlockSpec((1, 1024, D), lambda j, i: (j, 0, 0))],
        out_shape=[jax.ShapeDtypeStruct((S, D_FF), BF16), jax.ShapeDtypeStruct((nj, D, 1024), F32),
                   jax.ShapeDtypeStruct((nj, 1024, D), F32)],
        compiler_params=_params(48, 2), exchange=exchange)


def _ffn_bwd_input(df, w_ff1, dh2, h1, gain2, mix, w_out, exchange=None):
    tm = 256
    nj = D_FF // 1024
    steps = S // tm

    def body(df_ref, w1_ref, dh2_ref, h1_ref, g_ref, mix_ref, wo_ref, dh1_ref, dmix_ref, dwo_ref, dg_ref, acc_ref):
        i = pl.program_id(0)
        j = pl.program_id(1)
        part = _dot_nt(df_ref[...], w1_ref[0])

        @pl.when(j == 0)
        def _():
            acc_ref[...] = part

        @pl.when(j > 0)
        def _():
            acc_ref[...] += part

        @pl.when(j == nj - 1)
        def _():
            dhn = acc_ref[...]
            h1 = h1_ref[...]
            r2 = _rms_scale(h1)
            xhat = h1 * r2
            dh1 = dh2_ref[...] + _rms_bwd(dhn * g_ref[...], xhat, r2)
            dh1_ref[...] = dh1
            dh1b = dh1.astype(BF16)
            dmix_ref[...] = _dot_nt(dh1b, wo_ref[...])
            dwo = _dot_tn(mix_ref[...], dh1b)
            dg = jnp.sum(dhn * xhat, axis=0, keepdims=True)

            @pl.when(i == 0)
            def _():
                dwo_ref[...] = dwo
                dg_ref[...] = dg

            @pl.when(i > 0)
            def _():
                dwo_ref[...] += dwo
                dg_ref[...] += dg

    return _call(
        body, (df, w_ff1, dh2, h1, gain2, mix, w_out), grid=(steps, nj), name="ffn_bwd_input",
        in_specs=[pl.BlockSpec((tm, 1024), lambda i, j: (i, j)), pl.BlockSpec((1, D, 1024), lambda i, j: (j, 0, 0)),
                  pl.BlockSpec((tm, D), lambda i, j: (i, 0)), pl.BlockSpec((tm, D), lambda i, j: (i, 0)),
                  pl.BlockSpec((1, D), lambda i, j: (0, 0)), pl.BlockSpec((tm, D), lambda i, j: (i, 0)),
                  pl.BlockSpec((D, D), lambda i, j: (0, 0))],
        out_specs=[pl.BlockSpec((tm, D), lambda i, j: (i, 0)), pl.BlockSpec((tm, D), lambda i, j: (i, 0)),
                   pl.BlockSpec((D, D), lambda i, j: (0, 0)), pl.BlockSpec((1, D), lambda i, j: (0, 0))],
        out_shape=[jax.ShapeDtypeStruct((S, D), F32), jax.ShapeDtypeStruct((S, D), F32),
                   jax.ShapeDtypeStruct((D, D), F32), jax.ShapeDtypeStruct((1, D), F32)],
        scratch_shapes=[pltpu.VMEM((tm, D), F32)],
        compiler_params=_params(48, 2), exchange=exchange)


def _mixer_bwd(z, dmix, v_gain, w_spatial, b_spatial_t, sinks, rel_table, bucket, exchange=None):
    def body(z_ref, kvp_ref, dm_ref, gain_ref, ws_ref, bt_ref, sink_ref, table_ref, bucket_ref,
             dz_ref, dws_ref, db_ref, dgain_ref, dsink_ref, drel_ref,
             bias_ref, wt_ref, wtt_ref, dbias_ref, dsv_ref, carry_ref):
        n = pl.program_id(0)

        @pl.when(n == 0)
        def _():
            _fill_bias(bucket_ref, table_ref, bias_ref)
            _fill_tril(ws_ref, wt_ref, wtt_ref)
            dbias_ref[...] = jnp.zeros_like(dbias_ref)
            dsv_ref[...] = jnp.zeros_like(dsv_ref)
            dws_ref[...] = jnp.zeros_like(dws_ref)
            dgain_ref[...] = jnp.zeros_like(dgain_ref)
            dsink_ref[...] = jnp.zeros_like(dsink_ref)

        rows = pl.ds(pl.multiple_of(n * CHUNK, CHUNK), CHUNK)

        zuv = z_ref[:, :1024]
        cdf, t = _gelu_parts(zuv)
        guv = zuv * cdf
        dgelu = cdf + zuv * (0.5 * (1.0 - t * t)) * (GELU_C * (1.0 + 3.0 * 0.044715 * (zuv * zuv)))
        for g in range(N_GROUP):
            lo, hi = 128 * g, 128 * (g + 1)
            u = guv[:, lo:hi]
            vg = guv[:, 512 + lo:512 + hi]
            rr = _rms_scale(vg)
            vhat = vg * rr
            gain = gain_ref[:, lo:hi]
            vnb = (vhat * gain).astype(BF16)
            sv = _dot(wt_ref[g], vnb) + bt_ref[:, g:g + 1]
            da = dm_ref[:, lo:hi]
            dsv = da * u
            dsvb = dsv.astype(BF16)
            dsv_ref[g] += dsv
            dws_ref[g] += _dot_nt(dsvb, vnb)
            dvn = _dot(wtt_ref[g], dsvb)
            dgain_ref[:, lo:hi] += jnp.sum(dvn * vhat, axis=0, keepdims=True)
            dvg = _rms_bwd(dvn * gain, vhat, rr)
            dz_ref[rows, lo:hi] = (da * sv * dgelu[:, lo:hi]).astype(BF16)
            dz_ref[rows, 512 + lo:512 + hi] = (dvg * dgelu[:, 512 + lo:512 + hi]).astype(BF16)

        k_same, k_swap, v_same, v_swap = _kv_layouts(kvp_ref[...], z_ref[:, 1536:1792])
        valid = _band_mask(n)
        lane_half = lax.broadcasted_iota(jnp.int32, (1, 128), 1) // 64
        zero = jnp.zeros((2 * CHUNK, 128), F32)
        dk_same, dk_swap, dv_same, dv_swap = zero, zero, zero, zero
        for pair in range(N_HEAD // 2):
            cols = slice(1024 + 128 * pair, 1024 + 128 * (pair + 1))
            qq = z_ref[:, cols]
            do_pair = dm_ref[:, 512 + 128 * pair:512 + 128 * (pair + 1)]
            dq = jnp.zeros((CHUNK, 128), F32)
            for pos in range(2):
                h = 2 * pair + pos
                _, _, same = _head_place(h)
                on_half = lane_half == pos
                qm = jnp.where(on_half, qq, 0.0).astype(BF16)
                k_use = k_same if same else k_swap
                v_use = v_same if same else v_swap
                p, p_sink = _softmax_sink(qm, k_use, bias_ref[h], sink_ref[h], valid)
                dom = jnp.where(on_half, do_pair, 0.0).astype(BF16)
                dp = _dot_nt(dom, v_use)
                dsum = jnp.sum(p * dp, axis=-1, keepdims=True)
                ds = p * (dp - dsum)
                dbias_ref[h] += ds
                dsink_ref[h:h + 1, :] += jnp.broadcast_to(jnp.sum(-p_sink * dsum, axis=0, keepdims=True), (1, 128))
                dsb = ds.astype(BF16)
                dq = dq + jnp.where(on_half, _dot(dsb, k_use), 0.0)
                dk_h = _dot_tn(dsb, qm)
                dv_h = _dot_tn(p.astype(BF16), dom)
                if same:
                    dk_same, dv_same = dk_same + dk_h, dv_same + dv_h
                else:
                    dk_swap, dv_swap = dk_swap + dk_h, dv_swap + dv_h
            dz_ref[rows, cols] = (dq * QK_SCALE).astype(BF16)
        dk = (dk_same + pltpu.roll(dk_swap, 64, axis=1)) * QK_SCALE
        dv = dv_same + pltpu.roll(dv_swap, 64, axis=1)
        dkv = jnp.concatenate([dk, dv], axis=1)

        @pl.when(n > 0)
        def _():
            prev_rows = pl.ds(pl.multiple_of((n - 1) * CHUNK, CHUNK), CHUNK)
            dz_ref[prev_rows, 1536:1792] = (carry_ref[...] + dkv[:CHUNK]).astype(BF16)

        carry_ref[...] = dkv[CHUNK:]

        @pl.when(n == N_BLOCK - 1)
        def _():
            dz_ref[rows, 1536:1792] = dkv[CHUNK:].astype(BF16)
            r = lax.broadcasted_iota(jnp.int32, (CHUNK, CHUNK), 0)
            c = lax.broadcasted_iota(jnp.int32, (CHUNK, CHUNK), 1)
            for g in range(N_GROUP):
                dws_ref[g] = jnp.where(c <= r, dws_ref[g], 0.0)
                db_ref[g] = jnp.sum(dsv_ref[g], axis=1, keepdims=True)
            bucket = bucket_ref[...]
            for h in range(N_HEAD):
                dbh = dbias_ref[h]
                per_bucket = [jnp.sum(jnp.where(bucket == b, dbh, 0.0), axis=0, keepdims=True) for b in range(N_BUCKET)]
                drel_ref[h] = jnp.sum(jnp.concatenate(per_bucket, axis=0), axis=1, keepdims=True)

    return _call(
        body, (z, z, dmix, v_gain, w_spatial, b_spatial_t, sinks, rel_table, bucket), grid=(N_BLOCK,), name="mixer_bwd",
        in_specs=[pl.BlockSpec((CHUNK, D_IN), lambda n: (n, 0)),
                  pl.BlockSpec((CHUNK, 256), lambda n: (jnp.maximum(n - 1, 0), 6)),
                  pl.BlockSpec((CHUNK, D), lambda n: (n, 0)),
                  pl.BlockSpec((1, 512), lambda n: (0, 0)),
                  pl.BlockSpec((N_GROUP, CHUNK, CHUNK), lambda n: (0, 0, 0)),
                  pl.BlockSpec((CHUNK, N_GROUP), lambda n: (0, 0)),
                  pl.BlockSpec(memory_space=pltpu.SMEM),
                  pl.BlockSpec(memory_space=pltpu.SMEM),
                  pl.BlockSpec((CHUNK, 2 * CHUNK), lambda n: (0, 0))],
        out_specs=[pl.BlockSpec((S, D_IN), lambda n: (0, 0)),
                   pl.BlockSpec((N_GROUP, CHUNK, CHUNK), lambda n: (0, 0, 0)),
                   pl.BlockSpec((N_GROUP, CHUNK, 1), lambda n: (0, 0, 0)),
                   pl.BlockSpec((1, 512), lambda n: (0, 0)),
                   pl.BlockSpec((N_HEAD, 128), lambda n: (0, 0)),
                   pl.BlockSpec((N_HEAD, N_BUCKET, 1), lambda n: (0, 0, 0))],
        out_shape=[jax.ShapeDtypeStruct((S, D_IN), BF16), jax.ShapeDtypeStruct((N_GROUP, CHUNK, CHUNK), F32),
                   jax.ShapeDtypeStruct((N_GROUP, CHUNK, 1), F32), jax.ShapeDtypeStruct((1, 512), F32),
                   jax.ShapeDtypeStruct((N_HEAD, 128), F32), jax.ShapeDtypeStruct((N_HEAD, N_BUCKET, 1), F32)],
        scratch_shapes=[pltpu.VMEM((N_HEAD, CHUNK, 2 * CHUNK), F32), pltpu.VMEM((N_GROUP, CHUNK, CHUNK), BF16),
                        pltpu.VMEM((N_GROUP, CHUNK, CHUNK), BF16), pltpu.VMEM((N_HEAD, CHUNK, 2 * CHUNK), F32),
                        pltpu.VMEM((N_GROUP, CHUNK, CHUNK), F32), pltpu.VMEM((CHUNK, 256), F32)],
        compiler_params=_params(48), exchange=exchange)


def _in_bwd(dz, hn1, w_in_t, x, dh1, gain1):
    tm = 256

    def body(dz_ref, hn_ref, w_ref, x_ref, dh1_ref, g_ref, dx_ref, dw_ref, dg_ref):
        i = pl.program_id(0)
        dzb = dz_ref[...]
        dhn = _dot(dzb, w_ref[...])
        xv = x_ref[...]
        r1 = _rms_scale(xv)
        xhat = xv * r1
        dx_ref[...] = dh1_ref[...] + _rms_bwd(dhn * g_ref[...], xhat, r1)
        dw = _dot_tn(dzb, hn_ref[...])
        dg = jnp.sum(dhn * xhat, axis=0, keepdims=True)

        @pl.when(i == 0)
        def _():
            dw_ref[...] = dw
            dg_ref[...] = dg

        @pl.when(i > 0)
        def _():
            dw_ref[...] += dw
            dg_ref[...] += dg

    return pl.pallas_call(
        body, grid=(S // tm,), name="in_bwd",
        in_specs=[pl.BlockSpec((tm, D_IN), lambda i: (i, 0)), pl.BlockSpec((tm, D), lambda i: (i, 0)),
                  pl.BlockSpec((D_IN, D), lambda i: (0, 0)), pl.BlockSpec((tm, D), lambda i: (i, 0)),
                  pl.BlockSpec((tm, D), lambda i: (i, 0)), pl.BlockSpec((1, D), lambda i: (0, 0))],
        out_specs=[pl.BlockSpec((tm, D), lambda i: (i, 0)), pl.BlockSpec((D_IN, D), lambda i: (0, 0)),
                   pl.BlockSpec((1, D), lambda i: (0, 0))],
        out_shape=[jax.ShapeDtypeStruct((S, D), F32), jax.ShapeDtypeStruct((D_IN, D), F32),
                   jax.ShapeDtypeStruct((1, D), F32)],
        compiler_params=_params(48),
    )(dz, hn1, w_in_t, x, dh1, gain1)


def _rel_bucket():
    a = jnp.arange(CHUNK)[:, None]
    j = jnp.arange(2 * CHUNK)[None, :]
    n = jnp.maximum(CHUNK + a - j, 0)
    max_exact = N_BUCKET // 2
    nf = jnp.maximum(n, 1).astype(jnp.float32)
    large = max_exact + (jnp.log(nf / max_exact) / math.log(CHUNK / max_exact) * (N_BUCKET - max_exact)).astype(jnp.int32)
    large = jnp.minimum(large, N_BUCKET - 1)
    return jnp.where(n < max_exact, n, large).astype(jnp.int32)


def _reduce_begin(grads, place):
    halves = [_halves(g) for g in grads]
    from_sibling = _sibling_exchange(halves)
    sums, landing = zip(*[_pair_sum(g, o, place) for g, o in zip(halves, from_sibling)])
    return list(sums), list(landing)


def _step(x, p, target, small, bufs, place):
    bucket = _rel_bucket()
    sinks = small["attn_sinks"].reshape(N_HEAD)
    b_t = jnp.transpose(small["b_spatial"].reshape(N_GROUP, CHUNK))
    ws = small["w_spatial"].reshape(N_GROUP, CHUNK, CHUNK)
    gain1, gain2 = small["norm1_gain"], small["norm2_gain"]
    v_gain = small["gmlp_v_gain"]
    final_gain = small["final_gain"].reshape(1, D)
    table = small["rel_bias_table"]
    bufs = dict(bufs)

    def gather(*names):
        return _Gather([bufs[n] for n in names])

    def took(names, got):
        bufs.update(zip(names, got))

    took(["w_in"], _gather_weights([bufs["w_in"]]))
    w_in_t = _whole(bufs["w_in"]).reshape(D_IN, D)
    (z, hn1), got = _in_proj(x, gain1, w_in_t, gather("w_out", "w_ple_gate", "w_ple_proj"))
    took(["w_out", "w_ple_gate", "w_ple_proj"], got)
    (mix,), got = _mixer_fwd(z, v_gain, ws, b_t, sinks, table, bucket, gather("w_ff1"))
    took(["w_ff1"], got)
    w_out = _whole(bufs["w_out"]).reshape(D, D)
    (h1, hn2), got = _out_proj(x, mix, w_out, gain2, gather("w_ff2"))
    took(["w_ff2"], got)
    w_ff1, w_ff2 = _whole(bufs["w_ff1"]), _whole(bufs["w_ff2"])
    h2, r, a = _ffn_fwd(h1, hn2, w_ff1, w_ff2)
    dh2, d_gate, d_proj, d_final, sq = _tail(h2, p, target, _whole(bufs["w_ple_gate"]).reshape(D, D),
                                             _whole(bufs["w_ple_proj"]), final_gain)

    def exchange(grads):
        return _ChipExchange(*_reduce_begin(grads, place))

    landed = {}
    (df, d_ff1, d_ff2), got = _ffn_bwd_weights(dh2, hn2, r, a, w_ff2, exchange([d_gate.reshape(N_CHIP, 256, D), d_proj]))
    landed.update(zip(["w_ple_gate", "w_ple_proj"], got))
    (dh1, dmix, d_out, d_gain2), got = _ffn_bwd_input(df, w_ff1, dh2, h1, gain2, mix, w_out, exchange([d_ff1, d_ff2]))
    landed.update(zip(["w_ff1", "w_ff2"], got))
    (dz, d_ws, d_b, d_vgain, d_sink, d_rel), got = _mixer_bwd(z, dmix, v_gain, ws, b_t, sinks, table, bucket,
                                                             exchange([d_out.reshape(N_CHIP, 256, D)]))
    landed.update(zip(["w_out"], got))
    dx, d_in_t, d_gain1 = _in_bwd(dz, hn1, w_in_t, x, dh1, gain1)
    landed.update(zip(["w_in"], _chip_exchange(*_reduce_begin([d_in_t.reshape(N_CHIP, 448, D)], place))))

    small_grads = {
        "norm1_gain": d_gain1, "gmlp_v_gain": d_vgain, "w_spatial": d_ws.reshape(1, N_GROUP, CHUNK, CHUNK),
        "b_spatial": d_b.reshape(1, N_GROUP, CHUNK), "attn_sinks": d_sink[:, 0].reshape(1, N_HEAD),
        "rel_bias_table": jnp.transpose(d_rel.reshape(N_HEAD, N_BUCKET)), "norm2_gain": d_gain2,
        "final_gain": d_final.reshape(D),
    }
    return sq, dx, landed, small_grads


HBM_SPEC = pl.BlockSpec(memory_space=pltpu.HBM)
VMEM_SPEC = pl.BlockSpec(memory_space=pltpu.VMEM)


def _mesh_place():
    x, y, c = lax.axis_index("x"), lax.axis_index("y"), lax.axis_index("c")
    others = [(1 - x, y), (x, 1 - y), (1 - x, 1 - y)]
    return x, y, c, others


def _remote(src, dst, send_sem, recv_sem, device):
    return pltpu.make_async_remote_copy(src_ref=src, dst_ref=dst, send_sem=send_sem, recv_sem=recv_sem,
                                        device_id=device, device_id_type=MESH)


def _hbm_like(a, shape=None, dtype=None):
    return jax.ShapeDtypeStruct(a.shape if shape is None else shape, a.dtype if dtype is None else dtype)


def _gather_start(bufs, send_sems, recv_sems):
    x, y, c, others = _mesh_place()
    me = 2 * x + y
    for w, buf in enumerate(bufs):
        for k, (ox, oy) in enumerate(others):
            mine = buf.at[me, c]
            _remote(mine, mine, send_sems.at[w, k], recv_sems.at[w, k], (ox, oy, c)).start()


def _gather_finish(bufs, send_sems, recv_sems):
    x, y, c, others = _mesh_place()
    me = 2 * x + y
    sibling = (x, y, 1 - c)
    idx = [2 * ox + oy for ox, oy in others]
    for w, buf in enumerate(bufs):
        for k in range(3):
            landed = buf.at[idx[k], c]
            _remote(landed, landed, send_sems.at[w, k], recv_sems.at[w, k], sibling).wait_recv()
            _remote(landed, landed, send_sems.at[w, 3 + k], recv_sems.at[w, 3 + k], sibling).start()
    for w, buf in enumerate(bufs):
        for k in range(3):
            landed = buf.at[idx[k], 1 - c]
            _remote(landed, landed, send_sems.at[w, 3 + k], recv_sems.at[w, 3 + k], sibling).wait_recv()
    for w, buf in enumerate(bufs):
        for k in range(3):
            mine, passed = buf.at[me, c], buf.at[idx[k], c]
            _remote(mine, mine, send_sems.at[w, k], recv_sems.at[w, k], sibling).wait_send()
            _remote(passed, passed, send_sems.at[w, 3 + k], recv_sems.at[w, 3 + k], sibling).wait_send()


def _gather_sems(n):
    return [pltpu.SemaphoreType.DMA((n, 6)), pltpu.SemaphoreType.DMA((n, 6))]


def _gather_weights(bufs):
    n = len(bufs)

    def body(*refs):
        outs = refs[n:2 * n]
        send_sems, recv_sems = refs[2 * n:]
        _gather_start(outs, send_sems, recv_sems)
        _gather_finish(outs, send_sems, recv_sems)

    return pl.pallas_call(
        body, name="gather_weights",
        in_specs=[HBM_SPEC] * n, out_specs=[HBM_SPEC] * n,
        out_shape=[_hbm_like(b) for b in bufs],
        input_output_aliases={w: w for w in range(n)},
        scratch_shapes=_gather_sems(n),
    )(*bufs)


def _sibling_exchange(grads):
    n = len(grads)

    def body(*refs):
        ins, outs = refs[:n], refs[n:2 * n]
        send_sems, recv_sems = refs[2 * n:]
        x, y, c, _ = _mesh_place()
        sibling = (x, y, 1 - c)
        sends = []
        for w in range(n):
            for j in range(N_CHIP):
                cp = _remote(ins[w].at[j, 1 - c], outs[w].at[j], send_sems.at[w, j], recv_sems.at[w, j], sibling)
                cp.start()
                sends.append(cp)
        for cp in sends:
            cp.wait_recv()
        for cp in sends:
            cp.wait_send()

    return pl.pallas_call(
        body, name="sibling_exchange",
        in_specs=[HBM_SPEC] * n, out_specs=[HBM_SPEC] * n,
        out_shape=[_hbm_like(g, (N_CHIP,) + g.shape[2:]) for g in grads],
        scratch_shapes=[pltpu.SemaphoreType.DMA((n, N_CHIP)), pltpu.SemaphoreType.DMA((n, N_CHIP))],
    )(*grads)


def _chip_exchange_start(sums, landing, send_sems, recv_sems):
    x, y, c, others = _mesh_place()
    me = 2 * x + y
    for w in range(len(sums)):
        for k, (ox, oy) in enumerate(others):
            _remote(sums[w].at[2 * ox + oy], landing[w].at[me], send_sems.at[w, k], recv_sems.at[w, k],
                    (ox, oy, c)).start()


def _chip_exchange_finish(sums, landing, send_sems, recv_sems):
    x, y, c, others = _mesh_place()
    for w in range(len(sums)):
        for k, (ox, oy) in enumerate(others):
            piece = landing[w].at[2 * ox + oy]
            _remote(piece, piece, send_sems.at[w, k], recv_sems.at[w, k], (x, y, c)).wait_recv()
    for w in range(len(sums)):
        for k, (ox, oy) in enumerate(others):
            piece = sums[w].at[2 * ox + oy]
            _remote(piece, piece, send_sems.at[w, k], recv_sems.at[w, k], (x, y, c)).wait_send()


def _chip_exchange_sems(n):
    return [pltpu.SemaphoreType.DMA((n, 3)), pltpu.SemaphoreType.DMA((n, 3))]


def _chip_exchange(sums, landing):
    n = len(sums)

    def body(*refs):
        ins, outs = refs[:n], refs[2 * n:3 * n]
        send_sems, recv_sems = refs[3 * n:]
        _chip_exchange_start(ins, outs, send_sems, recv_sems)
        _chip_exchange_finish(ins, outs, send_sems, recv_sems)

    return pl.pallas_call(
        body, name="chip_exchange",
        in_specs=[HBM_SPEC] * (2 * n), out_specs=[HBM_SPEC] * n,
        out_shape=[_hbm_like(b) for b in landing],
        input_output_aliases={n + w: w for w in range(n)},
        scratch_shapes=_chip_exchange_sems(n),
    )(*sums, *landing)


def _sibling_allgather(bufs):
    n = len(bufs)

    def body(*refs):
        outs = refs[n:2 * n]
        send_sems, recv_sems = refs[2 * n:]
        x, y, c, _ = _mesh_place()
        sibling = (x, y, 1 - c)
        sends = [_remote(outs[w].at[c], outs[w].at[c], send_sems.at[w], recv_sems.at[w], sibling) for w in range(n)]
        for cp in sends:
            cp.start()
        for w in range(n):
            landed = outs[w].at[1 - c]
            _remote(landed, landed, send_sems.at[w], recv_sems.at[w], sibling).wait_recv()
        for cp in sends:
            cp.wait_send()

    return pl.pallas_call(
        body, name="sibling_allgather",
        in_specs=[HBM_SPEC] * n, out_specs=[HBM_SPEC] * n,
        out_shape=[_hbm_like(b) for b in bufs],
        input_output_aliases={w: w for w in range(n)},
        scratch_shapes=[pltpu.SemaphoreType.DMA((n,)), pltpu.SemaphoreType.DMA((n,))],
    )(*bufs)


def _pair_sum(grad, other, place):
    _, _, h, cols = grad.shape
    tr = _row_tile(h)

    def body(place_ref, g_ref, o_ref, sums_ref, own_ref):
        s = (g_ref[0, 0] + o_ref[0]).astype(BF16)
        sums_ref[0] = s

        @pl.when(pl.program_id(1) == place_ref[0])
        def _():
            own_ref[0] = s

    return pl.pallas_call(
        body, name="pair_sum",
        grid_spec=pltpu.PrefetchScalarGridSpec(
            num_scalar_prefetch=1, grid=(h // tr, N_CHIP),
            in_specs=[pl.BlockSpec((1, 1, tr, cols), lambda r, j, place_ref: (j, place_ref[1], r, 0)),
                      pl.BlockSpec((1, tr, cols), lambda r, j, place_ref: (j, r, 0))],
            out_specs=[pl.BlockSpec((1, tr, cols), lambda r, j, place_ref: (j, r, 0)),
                       pl.BlockSpec((1, tr, cols), lambda r, j, place_ref: (place_ref[0], r, 0))]),
        out_shape=[jax.ShapeDtypeStruct((N_CHIP, h, cols), BF16)] * 2,
        compiler_params=_params(16, 2),
    )(place, grad, other)


def _chip_sum(parts, place):
    _, h, cols = parts.shape
    tr = _row_tile(h)

    def body(place_ref, p_ref, out_ref):
        out_ref[0] = ((p_ref[0].astype(F32) + p_ref[1].astype(F32)) + p_ref[2].astype(F32)) + p_ref[3].astype(F32)

    return pl.pallas_call(
        body, name="chip_sum",
        grid_spec=pltpu.PrefetchScalarGridSpec(
            num_scalar_prefetch=1, grid=(h // tr,),
            in_specs=[pl.BlockSpec((N_CHIP, tr, cols), lambda r, place_ref: (0, r, 0))],
            out_specs=pl.BlockSpec((1, tr, cols), lambda r, place_ref: (place_ref[1], r, 0))),
        out_shape=jax.ShapeDtypeStruct((2, h, cols), F32),
        compiler_params=_params(16),
    )(place, parts)


def _adamw_math(w, g, m, v):
    m = ADAM_B1 * m + (1.0 - ADAM_B1) * g
    v = ADAM_B2 * v + (1.0 - ADAM_B2) * (g * g)
    m_hat = m / (1.0 - ADAM_B1 ** ADAM_STEP)
    v_hat = v / (1.0 - ADAM_B2 ** ADAM_STEP)
    delta = -ADAM_LR * (m_hat / (jnp.sqrt(v_hat) + ADAM_EPS) + ADAM_WD * w)
    return delta, m, v


def _adamw(w, g, m, v):
    rows, cols = w.shape
    tr = _row_tile(rows)

    def body(w_ref, g_ref, m_ref, v_ref, d_ref, nm_ref, nv_ref):
        d_ref[...], nm_ref[...], nv_ref[...] = _adamw_math(w_ref[...], g_ref[...], m_ref[...], v_ref[...])

    spec = pl.BlockSpec((tr, cols), lambda r: (r, 0))
    return pl.pallas_call(
        body, grid=(rows // tr,), name="adamw",
        in_specs=[spec] * 4, out_specs=[spec] * 3,
        out_shape=[jax.ShapeDtypeStruct((rows, cols), F32)] * 3,
        compiler_params=_params(24),
    )(w, g, m, v)


SMALL_NAMES = ("norm1_gain", "gmlp_v_gain", "w_spatial", "b_spatial", "attn_sinks", "rel_bias_table", "norm2_gain",
               "final_gain")
PACK_TILE = 8 * 128


def _pack_small(arrays):
    parts = []
    for a in arrays:
        flat = a.reshape(-1)
        rows = -(-flat.shape[0] // PACK_TILE) * 8
        parts.append(jnp.pad(flat, (0, rows * 128 - flat.shape[0])).reshape(rows, 128))
    return jnp.concatenate(parts, axis=0)


def _unpack_small(packed, like):
    out, row = [], 0
    for a in like:
        size = math.prod(a.shape)
        rows = -(-size // PACK_TILE) * 8
        out.append(packed[row:row + rows].reshape(-1)[:size].reshape(a.shape))
        row += rows
    return out


def _small_update(grad, w, m, v):
    rows = grad.shape[0]

    def body(g_ref, w_ref, m_ref, v_ref, tot_ref, d_ref, nm_ref, nv_ref, buf, send_sems, recv_sems):
        x, y, c, _ = _mesh_place()
        me = 4 * x + 2 * y + c
        buf[me] = g_ref[...]
        sends = []
        for k in range(1, 8):
            kx, ky, kc = k // 4, (k // 2) % 2, k % 2
            peer = (1 - x if kx else x, 1 - y if ky else y, 1 - c if kc else c)
            cp = _remote(g_ref, buf.at[me], send_sems.at[k - 1], recv_sems.at[k - 1], peer)
            cp.start()
            sends.append((cp, 4 * peer[0] + 2 * peer[1] + peer[2]))
        for k, (cp, src) in enumerate(sends):
            _remote(g_ref, buf.at[src], send_sems.at[k], recv_sems.at[k], (x, y, c)).wait_recv()
        for cp, _ in sends:
            cp.wait_send()
        total = buf[0]
        for dev in range(1, 8):
            total = total + buf[dev]
        tot_ref[...] = total
        d_ref[...], nm_ref[...], nv_ref[...] = _adamw_math(w_ref[...], total, m_ref[...], v_ref[...])

    return pl.pallas_call(
        body, name="small_update",
        in_specs=[VMEM_SPEC] * 4, out_specs=[VMEM_SPEC] * 4,
        out_shape=[jax.ShapeDtypeStruct((rows, 128), F32)] * 4,
        scratch_shapes=[pltpu.VMEM((8, rows, 128), F32), pltpu.SemaphoreType.DMA((7,)), pltpu.SemaphoreType.DMA((7,))],
        compiler_params=pltpu.CompilerParams(vmem_limit_bytes=24 * MIB),
    )(grad, w, m, v)


def _halves(a):
    return a.reshape(a.shape[:-2] + (2, a.shape[-2] // 2, a.shape[-1]))


def _whole(a):
    return a.reshape(a.shape[:-3] + (2 * a.shape[-2], a.shape[-1]))


def kernel(x, p, norm1_gain, w_in, gmlp_v_gain, w_spatial, b_spatial, attn_sinks, rel_bias_table, w_out, norm2_gain, w_ff1, w_ff2, w_ple_proj, w_ple_gate, final_gain, loss_target, m_norm1_gain, m_w_in, m_gmlp_v_gain, m_w_spatial, m_b_spatial, m_attn_sinks, m_rel_bias_table, m_w_out, m_norm2_gain, m_w_ff1, m_w_ff2, m_w_ple_proj, m_w_ple_gate, m_final_gain, v_norm1_gain, v_w_in, v_gmlp_v_gain, v_w_spatial, v_b_spatial, v_attn_sinks, v_rel_bias_table, v_w_out, v_norm2_gain, v_w_ff1, v_w_ff2, v_w_ple_proj, v_w_ple_gate, v_final_gain):
    given = dict(locals())
    small = {n: given[n] for n in SMALL_NAMES}
    chip = 2 * lax.axis_index("x") + lax.axis_index("y")
    place = jnp.stack([chip, lax.axis_index("c")]).astype(jnp.int32)

    big_names = ("w_in", "w_out", "w_ff1", "w_ff2", "w_ple_proj", "w_ple_gate")
    shards = {n: given[n][0] for n in big_names}
    travel = dict(shards, w_in=jnp.transpose(shards["w_in"]))
    bufs = {n: _cast_shard(travel[n], place[:1]) for n in big_names}
    sq, dx, landed, small_grads = _step(x[0], p[0, 0], loss_target[0], small, bufs, place)

    reduced = _sibling_allgather([_chip_sum(landed[n], place) for n in big_names])
    big_grad = {n: _whole(r) for n, r in zip(big_names, reduced)}
    big_grad["w_in"] = jnp.transpose(big_grad["w_in"])

    out_grad, out_delta, out_m, out_v = {}, {}, {}, {}
    for n in big_names:
        g = big_grad[n]
        delta, new_m, new_v = _adamw(shards[n], g, given["m_" + n][0], given["v_" + n][0])
        out_grad[n], out_delta[n], out_m[n], out_v[n] = g[None], delta[None], new_m[None], new_v[None]

    no_state = jnp.zeros((8, 128), F32)
    packed = _small_update(_pack_small([small_grads[n] for n in SMALL_NAMES] + [sq]),
                           _pack_small([given[n] for n in SMALL_NAMES] + [no_state]),
                           _pack_small([given["m_" + n] for n in SMALL_NAMES] + [no_state]),
                           _pack_small([given["v_" + n] for n in SMALL_NAMES] + [no_state]))
    like = [given[n] for n in SMALL_NAMES] + [sq]
    for res, out in zip(packed, (out_grad, out_delta, out_m, out_v)):
        out.update(zip(SMALL_NAMES + ("squared_error",), _unpack_small(res, like)))
    loss = 0.5 * out_grad["squared_error"][0, 0] / D

    order = ("norm1_gain", "w_in", "gmlp_v_gain", "w_spatial", "b_spatial", "attn_sinks", "rel_bias_table", "w_out",
             "norm2_gain", "w_ff1", "w_ff2", "w_ple_proj", "w_ple_gate", "final_gain")
    return (loss, dx[None], *[out_grad[n] for n in order], *[out_delta[n] for n in order],
            *[out_m[n] for n in order], *[out_v[n] for n in order])
```

```python
import functools
import math

import jax
import jax.numpy as jnp
from jax import lax
from jax.experimental import pallas as pl
from jax.experimental.pallas import tpu as pltpu

S = 2048
D = 1024
D_IN = 1792
D_FF = 4096
PLE = 256
N_CHIP = 4
N_GROUP = 4
CHUNK = 128
N_HEAD = 8
N_BLOCK = S // CHUNK
N_BUCKET = 32
EPS = 1e-6
NEG_INF = -1e30
QK_SCALE = 0.125
GELU_C = math.sqrt(2.0 / math.pi)

ADAM_LR = 0.001
ADAM_B1 = 0.9
ADAM_B2 = 0.999
ADAM_EPS = 1e-08
ADAM_WD = 0.01
ADAM_STEP = 10

F32 = jnp.float32
BF16 = jnp.bfloat16
MIB = 1024 * 1024
MESH = pl.DeviceIdType.MESH

NT = (((1,), (1,)), ((), ()))
TN = (((0,), (0,)), ((), ()))


def _dot(a, b):
    return jnp.dot(a, b, preferred_element_type=F32)


def _dot_nt(a, b):
    return lax.dot_general(a, b, NT, preferred_element_type=F32)


def _dot_tn(a, b):
    return lax.dot_general(a, b, TN, preferred_element_type=F32)


def _params(vmem_mib, n_axes=1):
    return pltpu.CompilerParams(dimension_semantics=("arbitrary",) * n_axes, vmem_limit_bytes=vmem_mib * MIB)


def _rms_scale(v):
    return lax.rsqrt(jnp.mean(v * v, axis=-1, keepdims=True) + EPS)


def _rms_bwd(dy_gain, xhat, r):
    return r * (dy_gain - xhat * jnp.mean(dy_gain * xhat, axis=-1, keepdims=True))


class _Gather:
    def __init__(self, bufs):
        self.operands = list(bufs)
        self.n_out = len(self.operands)
        self.out_shape = [_hbm_like(b) for b in bufs]
        self.aliases = {w: w for w in range(self.n_out)}
        self.sems = _gather_sems(self.n_out)

    def start(self, ins, outs, sems):
        _gather_start(outs, *sems)

    def finish(self, ins, outs, sems):
        _gather_finish(outs, *sems)


class _ChipExchange:
    def __init__(self, sums, landing):
        self.n_out = len(landing)
        self.operands = list(sums) + list(landing)
        self.out_shape = [_hbm_like(b) for b in landing]
        self.aliases = {self.n_out + w: w for w in range(self.n_out)}
        self.sems = _chip_exchange_sems(self.n_out)

    def start(self, ins, outs, sems):
        _chip_exchange_start(ins[:self.n_out], outs, *sems)

    def finish(self, ins, outs, sems):
        _chip_exchange_finish(ins[:self.n_out], outs, *sems)


class _SiblingExchange:
    def __init__(self, grads):
        self.operands = list(grads)
        self.n_out = len(self.operands)
        self.out_shape = [_hbm_like(g, (N_CHIP,) + g.shape[2:]) for g in grads]
        self.aliases = {}
        self.sems = _sibling_exchange_sems(self.n_out)

    def start(self, ins, outs, sems):
        _sibling_exchange_start(ins, outs, *sems)

    def finish(self, ins, outs, sems):
        _sibling_exchange_finish(ins, outs, *sems)


def _call(body, operands, *, grid, in_specs, out_specs, out_shape, name, compiler_params, scratch_shapes=(),
          exchange=None):
    operands = [o if getattr(spec, "memory_space", None) == pltpu.SMEM else _in_hbm(o)
                for o, spec in zip(operands, in_specs)]
    out_shape = [pltpu.HBM(s.shape, s.dtype) for s in out_shape]
    if exchange is None:
        res = pl.pallas_call(body, grid=grid, in_specs=in_specs, out_specs=out_specs, out_shape=out_shape, name=name,
                             scratch_shapes=list(scratch_shapes), compiler_params=compiler_params)(*operands)
        return list(res), []
    n_in, n_out, n_scr = len(in_specs), len(out_specs), len(scratch_shapes)
    k_in, k_out = len(exchange.operands), exchange.n_out

    def fused(*refs):
        ins, refs = refs[:n_in], refs[n_in:]
        ex_ins, refs = refs[:k_in], refs[k_in:]
        outs, refs = refs[:n_out], refs[n_out:]
        ex_outs, refs = refs[:k_out], refs[k_out:]
        scratch, sems = refs[:n_scr], refs[n_scr:]
        ids = [pl.program_id(a) for a in range(len(grid))]
        first = functools.reduce(jnp.logical_and, [i == 0 for i in ids])
        last = functools.reduce(jnp.logical_and, [i == g - 1 for i, g in zip(ids, grid)])

        @pl.when(first)
        def _():
            exchange.start(ex_ins, ex_outs, sems)

        body(*ins, *outs, *scratch)

        @pl.when(last)
        def _():
            exchange.finish(ex_ins, ex_outs, sems)

    res = pl.pallas_call(
        fused, grid=grid, name=name,
        in_specs=list(in_specs) + [HBM_SPEC] * k_in, out_specs=list(out_specs) + [HBM_SPEC] * k_out,
        out_shape=list(out_shape) + exchange.out_shape,
        input_output_aliases={n_in + i: n_out + o for i, o in exchange.aliases.items()},
        scratch_shapes=list(scratch_shapes) + exchange.sems, compiler_params=compiler_params,
    )(*operands, *[_in_hbm(o) for o in exchange.operands])
    return list(res[:n_out]), list(res[n_out:])


def _in_hbm(a):
    return pltpu.with_memory_space_constraint(a, pltpu.HBM)


def _row_tile(h):
    return max(t for t in range(16, 257, 16) if h % t == 0)


def _cast_shard(a, chip):
    rows, cols = a.shape
    h = rows // 2
    tr = _row_tile(h)

    def body(chip_ref, a_ref, o_ref):
        o_ref[0, 0] = a_ref[0].astype(BF16)

    return pl.pallas_call(
        body, name="cast_shard",
        grid_spec=pltpu.PrefetchScalarGridSpec(
            num_scalar_prefetch=1, grid=(2, h // tr),
            in_specs=[pl.BlockSpec((1, tr, cols), lambda s, r, chip_ref: (s, r, 0))],
            out_specs=pl.BlockSpec((1, 1, tr, cols), lambda s, r, chip_ref: (chip_ref[0], s, r, 0))),
        out_shape=pltpu.HBM((N_CHIP, 2, h, cols), BF16),
        compiler_params=_params(16, 2),
    )(chip, _in_hbm(a.reshape(2, h, cols)))


def _in_proj(x, gain1, w_in_t, exchange=None):
    tm = 256

    def body(x_ref, g_ref, w_ref, z_ref, hn_ref):
        xv = x_ref[...]
        hn = (xv * _rms_scale(xv) * g_ref[...]).astype(BF16)
        hn_ref[...] = hn
        z_ref[...] = _dot_nt(hn, w_ref[...])

    return _call(
        body, (x, gain1, w_in_t), grid=(S // tm,), name="in_proj",
        in_specs=[pl.BlockSpec((tm, D), lambda i: (i, 0)), pl.BlockSpec((1, D), lambda i: (0, 0)),
                  pl.BlockSpec((D_IN, D), lambda i: (0, 0))],
        out_specs=[pl.BlockSpec((tm, D_IN), lambda i: (i, 0)), pl.BlockSpec((tm, D), lambda i: (i, 0))],
        out_shape=[jax.ShapeDtypeStruct((S, D_IN), F32), jax.ShapeDtypeStruct((S, D), BF16)],
        compiler_params=_params(40), exchange=exchange)


def _gelu_parts(v):
    t = jnp.tanh(GELU_C * (v + 0.044715 * (v * v * v)))
    cdf = 0.5 * (1.0 + t)
    return cdf, t


def _band_mask(n):
    a = lax.broadcasted_iota(jnp.int32, (CHUNK, 2 * CHUNK), 0)
    j = lax.broadcasted_iota(jnp.int32, (CHUNK, 2 * CHUNK), 1)
    dist = CHUNK + a - j
    valid = (dist >= 0) & (dist < CHUNK)
    return valid & ((n > 0) | (j >= CHUNK))


def _fill_bias(bucket_ref, table_ref, bias_ref):
    bucket = bucket_ref[...]
    for h in range(N_HEAD):
        acc = jnp.zeros((CHUNK, 2 * CHUNK), F32)
        for b in range(N_BUCKET):
            acc = jnp.where(bucket == b, table_ref[b, h], acc)
        bias_ref[h] = acc


def _fill_tril(ws_ref, wt_ref, wtt_ref=None):
    r = lax.broadcasted_iota(jnp.int32, (CHUNK, CHUNK), 0)
    c = lax.broadcasted_iota(jnp.int32, (CHUNK, CHUNK), 1)
    for g in range(N_GROUP):
        w = jnp.where(c <= r, ws_ref[g], 0.0)
        wt_ref[g] = w.astype(BF16)
        if wtt_ref is not None:
            wtt_ref[g] = w.T.astype(BF16)


def _kv_layouts(kv_prev, kv_cur):
    both = jnp.concatenate([kv_prev, kv_cur], axis=0)
    k = both[:, :128]
    v = both[:, 128:]
    return (k.astype(BF16), pltpu.roll(k, 64, axis=1).astype(BF16),
            v.astype(BF16), pltpu.roll(v, 64, axis=1).astype(BF16))


def _head_place(h):
    pair, pos, kvh = h // 2, h % 2, h // 4
    return pair, pos, kvh == pos


def _softmax_sink(qm, k_use, bias_h, sink, valid):
    s = _dot_nt(qm, k_use) * QK_SCALE + bias_h
    s = jnp.where(valid, s, NEG_INF)
    m = jnp.maximum(jnp.max(s, axis=-1, keepdims=True), sink)
    e = jnp.exp(s - m)
    es = jnp.exp(sink - m)
    denom = jnp.sum(e, axis=-1, keepdims=True) + es
    return e / denom, es / denom


def _mixer_fwd(z, v_gain, w_spatial, b_spatial_t, sinks, rel_table, bucket, exchange=None):
    def body(z_ref, kvp_ref, gain_ref, ws_ref, bt_ref, sink_ref, table_ref, bucket_ref, out_ref, bias_ref, wt_ref):
        n = pl.program_id(0)

        @pl.when(n == 0)
        def _():
            _fill_bias(bucket_ref, table_ref, bias_ref)
            _fill_tril(ws_ref, wt_ref)

        zuv = z_ref[:, :1024]
        cdf, _ = _gelu_parts(zuv)
        guv = zuv * cdf
        for g in range(N_GROUP):
            vg = guv[:, 512 + 128 * g:512 + 128 * (g + 1)]
            vn = vg * _rms_scale(vg) * gain_ref[:, 128 * g:128 * (g + 1)]
            sv = _dot(wt_ref[g], vn.astype(BF16)) + bt_ref[:, g:g + 1]
            out_ref[:, 128 * g:128 * (g + 1)] = (guv[:, 128 * g:128 * (g + 1)] * sv).astype(BF16)

        k_same, k_swap, v_same, v_swap = _kv_layouts(kvp_ref[...], z_ref[:, 1536:1792])
        valid = _band_mask(n)
        lane_half = lax.broadcasted_iota(jnp.int32, (1, 128), 1) // 64
        for pair in range(N_HEAD // 2):
            qq = z_ref[:, 1024 + 128 * pair:1024 + 128 * (pair + 1)]
            acc = jnp.zeros((CHUNK, 128), F32)
            for pos in range(2):
                h = 2 * pair + pos
                _, _, same = _head_place(h)
                qm = jnp.where(lane_half == pos, qq, 0.0).astype(BF16)
                p, _ = _softmax_sink(qm, k_same if same else k_swap, bias_ref[h], sink_ref[h], valid)
                vm = jnp.where(lane_half == pos, v_same if same else v_swap, jnp.zeros((), BF16))
                acc = acc + _dot(p.astype(BF16), vm)
            out_ref[:, 512 + 128 * pair:512 + 128 * (pair + 1)] = acc.astype(BF16)

    return _call(
        body, (z, z, v_gain, w_spatial, b_spatial_t, sinks, rel_table, bucket), grid=(N_BLOCK,), name="mixer_fwd",
        in_specs=[pl.BlockSpec((CHUNK, D_IN), lambda n: (n, 0)),
                  pl.BlockSpec((CHUNK, 256), lambda n: (jnp.maximum(n - 1, 0), 6)),
                  pl.BlockSpec((1, 512), lambda n: (0, 0)),
                  pl.BlockSpec((N_GROUP, CHUNK, CHUNK), lambda n: (0, 0, 0)),
                  pl.BlockSpec((CHUNK, N_GROUP), lambda n: (0, 0)),
                  pl.BlockSpec(memory_space=pltpu.SMEM),
                  pl.BlockSpec(memory_space=pltpu.SMEM),
                  pl.BlockSpec((CHUNK, 2 * CHUNK), lambda n: (0, 0))],
        out_specs=[pl.BlockSpec((CHUNK, D), lambda n: (n, 0))],
        out_shape=[jax.ShapeDtypeStruct((S, D), BF16)],
        scratch_shapes=[pltpu.VMEM((N_HEAD, CHUNK, 2 * CHUNK), F32), pltpu.VMEM((N_GROUP, CHUNK, CHUNK), BF16)],
        compiler_params=_params(32), exchange=exchange)


def _out_proj(x, mix, w_out, gain2, exchange=None):
    tm = 256

    def body(x_ref, mix_ref, w_ref, g_ref, h1_ref, hn_ref):
        h1 = x_ref[...] + _dot(mix_ref[...], w_ref[...])
        h1_ref[...] = h1
        hn_ref[...] = (h1 * _rms_scale(h1) * g_ref[...]).astype(BF16)

    return _call(
        body, (x, mix, w_out, gain2), grid=(S // tm,), name="out_proj",
        in_specs=[pl.BlockSpec((tm, D), lambda i: (i, 0)), pl.BlockSpec((tm, D), lambda i: (i, 0)),
                  pl.BlockSpec((D, D), lambda i: (0, 0)), pl.BlockSpec((1, D), lambda i: (0, 0))],
        out_specs=[pl.BlockSpec((tm, D), lambda i: (i, 0)), pl.BlockSpec((tm, D), lambda i: (i, 0))],
        out_shape=[jax.ShapeDtypeStruct((S, D), F32), jax.ShapeDtypeStruct((S, D), BF16)],
        compiler_params=_params(32), exchange=exchange)


def _ffn_up(hn2, w_ff1, exchange=None):
    tm = 512
    nj = D_FF // 1024

    def body(hn_ref, w1_ref, r_ref, a_ref):
        r = jnp.maximum(_dot(hn_ref[...], w1_ref[0]), 0.0)
        r_ref[...] = r.astype(BF16)
        a_ref[...] = (r * r).astype(BF16)

    return _call(
        body, (hn2, w_ff1), grid=(nj, S // tm), name="ffn_up",
        in_specs=[pl.BlockSpec((tm, D), lambda j, i: (i, 0)), pl.BlockSpec((1, D, 1024), lambda j, i: (j, 0, 0))],
        out_specs=[pl.BlockSpec((tm, 1024), lambda j, i: (i, j)), pl.BlockSpec((tm, 1024), lambda j, i: (i, j))],
        out_shape=[jax.ShapeDtypeStruct((S, D_FF), BF16), jax.ShapeDtypeStruct((S, D_FF), BF16)],
        compiler_params=_params(32, 2), exchange=exchange)


def _ffn_down(h1, a, w_ff2, exchange=None):
    tm = 512
    nj = D_FF // 1024

    def body(h1_ref, a_ref, w2_ref, h2_ref, acc_ref):
        j = pl.program_id(1)
        part = _dot(a_ref[...], w2_ref[0])

        @pl.when(j == 0)
        def _():
            acc_ref[...] = part

        @pl.when(j > 0)
        def _():
            acc_ref[...] += part

        @pl.when(j == nj - 1)
        def _():
            h2_ref[...] = h1_ref[...] + acc_ref[...]

    return _call(
        body, (h1, a, w_ff2), grid=(S // tm, nj), name="ffn_down",
        in_specs=[pl.BlockSpec((tm, D), lambda i, j: (i, 0)), pl.BlockSpec((tm, 1024), lambda i, j: (i, j)),
                  pl.BlockSpec((1, 1024, D), lambda i, j: (j, 0, 0))],
        out_specs=[pl.BlockSpec((tm, D), lambda i, j: (i, 0))],
        out_shape=[jax.ShapeDtypeStruct((S, D), F32)],
        scratch_shapes=[pltpu.VMEM((tm, D), F32)],
        compiler_params=_params(32, 2), exchange=exchange)


def _tail(h2, p, target, w_gate, w_proj, final_gain):
    tm = 256
    steps = S // tm

    def body(h2_ref, p_ref, t_ref, wg_ref, wp_ref, gf_ref, dh2_ref, dwg_ref, dwp_ref, dgf_ref, loss_ref, dwp_acc):
        i = pl.program_id(0)
        h2 = h2_ref[...]
        h2b = h2.astype(BF16)
        pb = p_ref[...].astype(BF16)
        gate = jax.nn.sigmoid(_dot(h2b, wg_ref[...]))
        pp = jnp.concatenate([_dot(pb, wp_ref[j]) for j in range(N_CHIP)], axis=1)
        h3 = h2 + gate * pp
        r3 = _rms_scale(h3)
        xhat = h3 * r3
        gf = gf_ref[...]
        err = xhat * gf - t_ref[...]
        dy = err * (1.0 / D)
        dh3 = _rms_bwd(dy * gf, xhat, r3)
        dgp = (dh3 * pp * gate * (1.0 - gate)).astype(BF16)
        dpp = (dh3 * gate).astype(BF16)
        dh2_ref[...] = dh3 + _dot_nt(dgp, wg_ref[...])
        dwg = _dot_tn(h2b, dgp)
        dwp = _dot_tn(pb, dpp)
        dgf = jnp.sum(dy * xhat, axis=0, keepdims=True)
        sq = jnp.sum(jnp.sum(err * err, axis=1, keepdims=True), axis=0, keepdims=True)

        @pl.when(i == 0)
        def _():
            dwg_ref[...] = dwg
            dwp_acc[...] = dwp
            dgf_ref[...] = dgf
            loss_ref[...] = jnp.broadcast_to(sq, (8, 128))

        @pl.when(i > 0)
        def _():
            dwg_ref[...] += dwg
            dwp_acc[...] += dwp
            dgf_ref[...] += dgf
            loss_ref[...] += jnp.broadcast_to(sq, (8, 128))

        @pl.when(i == steps - 1)
        def _():
            for j in range(N_CHIP):
                dwp_ref[j] = dwp_acc[:, 256 * j:256 * (j + 1)]

    return _call(
        body, (h2, p, target, w_gate, w_proj, final_gain), grid=(steps,), name="tail",
        in_specs=[pl.BlockSpec((tm, D), lambda i: (i, 0)), pl.BlockSpec((tm, PLE), lambda i: (i, 0)),
                  pl.BlockSpec((tm, D), lambda i: (i, 0)), pl.BlockSpec((D, D), lambda i: (0, 0)),
                  pl.BlockSpec((N_CHIP, PLE, 256), lambda i: (0, 0, 0)), pl.BlockSpec((1, D), lambda i: (0, 0))],
        out_specs=[pl.BlockSpec((tm, D), lambda i: (i, 0)), pl.BlockSpec((D, D), lambda i: (0, 0)),
                   pl.BlockSpec((N_CHIP, PLE, 256), lambda i: (0, 0, 0)), pl.BlockSpec((1, D), lambda i: (0, 0)),
                   pl.BlockSpec((8, 128), lambda i: (0, 0))],
        out_shape=[jax.ShapeDtypeStruct((S, D), F32), jax.ShapeDtypeStruct((D, D), F32),
                   jax.ShapeDtypeStruct((N_CHIP, PLE, 256), F32), jax.ShapeDtypeStruct((1, D), F32),
                   jax.ShapeDtypeStruct((8, 128), F32)],
        scratch_shapes=[pltpu.VMEM((PLE, D), F32)],
        compiler_params=_params(48))[0]


def _ffn_bwd_weights(dh2, hn2, r, a, w_ff2, exchange=None):
    tm = 512
    nj = D_FF // 1024

    def body(dh2_ref, hn_ref, r_ref, a_ref, w2_ref, df_ref, dw1_ref, dw2_ref):
        i = pl.program_id(1)
        dh2b = dh2_ref[...].astype(BF16)
        da = _dot_nt(dh2b, w2_ref[0])
        df = (da * (2.0 * r_ref[...].astype(F32))).astype(BF16)
        df_ref[...] = df
        dw1 = _dot_tn(hn_ref[...], df)
        dw2 = _dot_tn(a_ref[...], dh2b)

        @pl.when(i == 0)
        def _():
            dw1_ref[0] = dw1
            dw2_ref[0] = dw2

        @pl.when(i > 0)
        def _():
            dw1_ref[0] += dw1
            dw2_ref[0] += dw2

    return _call(
        body, (dh2, hn2, r, a, w_ff2), grid=(nj, S // tm), name="ffn_bwd_weights",
        in_specs=[pl.BlockSpec((tm, D), lambda j, i: (i, 0)), pl.BlockSpec((tm, D), lambda j, i: (i, 0)),
                  pl.BlockSpec((tm, 1024), lambda j, i: (i, j)), pl.BlockSpec((tm, 1024), lambda j, i: (i, j)),
                  pl.BlockSpec((1, 1024, D), lambda j, i: (j, 0, 0))],
        out_specs=[pl.BlockSpec((tm, 1024), lambda j, i: (i, j)), pl.BlockSpec((1, D, 1024), lambda j, i: (j, 0, 0)),
                   pl.BlockSpec((1, 1024, D), lambda j, i: (j, 0, 0))],
        out_shape=[jax.ShapeDtypeStruct((S, D_FF), BF16), jax.ShapeDtypeStruct((nj, D, 1024), F32),
                   jax.ShapeDtypeStruct((nj, 1024, D), F32)],
        compiler_params=_params(48, 2), exchange=exchange)


def _ffn_bwd_input(df, w_ff1, dh2, h1, gain2, mix, w_out, exchange=None):
    tm = 256
    nj = D_FF // 1024
    steps = S // tm

    def body(df_ref, w1_ref, dh2_ref, h1_ref, g_ref, mix_ref, wo_ref, dh1_ref, dmix_ref, dwo_ref, dg_ref, acc_ref):
        i = pl.program_id(0)
        j = pl.program_id(1)
        part = _dot_nt(df_ref[...], w1_ref[0])

        @pl.when(j == 0)
        def _():
            acc_ref[...] = part

        @pl.when(j > 0)
        def _():
            acc_ref[...] += part

        @pl.when(j == nj - 1)
        def _():
            dhn = acc_ref[...]
            h1 = h1_ref[...]
            r2 = _rms_scale(h1)
            xhat = h1 * r2
            dh1 = dh2_ref[...] + _rms_bwd(dhn * g_ref[...], xhat, r2)
            dh1_ref[...] = dh1
            dh1b = dh1.astype(BF16)
            dmix_ref[...] = _dot_nt(dh1b, wo_ref[...])
            dwo = _dot_tn(mix_ref[...], dh1b)
            dg = jnp.sum(dhn * xhat, axis=0, keepdims=True)

            @pl.when(i == 0)
            def _():
                dwo_ref[...] = dwo
                dg_ref[...] = dg

            @pl.when(i > 0)
            def _():
                dwo_ref[...] += dwo
                dg_ref[...] += dg

    return _call(
        body, (df, w_ff1, dh2, h1, gain2, mix, w_out), grid=(steps, nj), name="ffn_bwd_input",
        in_specs=[pl.BlockSpec((tm, 1024), lambda i, j: (i, j)), pl.BlockSpec((1, D, 1024), lambda i, j: (j, 0, 0)),
                  pl.BlockSpec((tm, D), lambda i, j: (i, 0)), pl.BlockSpec((tm, D), lambda i, j: (i, 0)),
                  pl.BlockSpec((1, D), lambda i, j: (0, 0)), pl.BlockSpec((tm, D), lambda i, j: (i, 0)),
                  pl.BlockSpec((D, D), lambda i, j: (0, 0))],
        out_specs=[pl.BlockSpec((tm, D), lambda i, j: (i, 0)), pl.BlockSpec((tm, D), lambda i, j: (i, 0)),
                   pl.BlockSpec((D, D), lambda i, j: (0, 0)), pl.BlockSpec((1, D), lambda i, j: (0, 0))],
        out_shape=[jax.ShapeDtypeStruct((S, D), F32), jax.ShapeDtypeStruct((S, D), F32),
                   jax.ShapeDtypeStruct((D, D), F32), jax.ShapeDtypeStruct((1, D), F32)],
        scratch_shapes=[pltpu.VMEM((tm, D), F32)],
        compiler_params=_params(48, 2), exchange=exchange)


def _mixer_bwd(z, dmix, v_gain, w_spatial, b_spatial_t, sinks, rel_table, bucket, exchange=None):
    def body(z_ref, kvp_ref, dm_ref, gain_ref, ws_ref, bt_ref, sink_ref, table_ref, bucket_ref,
             dz_ref, dws_ref, db_ref, dgain_ref, dsink_ref, drel_ref,
             bias_ref, wt_ref, wtt_ref, dbias_ref, dsv_ref, carry_ref):
        n = pl.program_id(0)

        @pl.when(n == 0)
        def _():
            _fill_bias(bucket_ref, table_ref, bias_ref)
            _fill_tril(ws_ref, wt_ref, wtt_ref)
            dbias_ref[...] = jnp.zeros_like(dbias_ref)
            dsv_ref[...] = jnp.zeros_like(dsv_ref)
            dws_ref[...] = jnp.zeros_like(dws_ref)
            dgain_ref[...] = jnp.zeros_like(dgain_ref)
            dsink_ref[...] = jnp.zeros_like(dsink_ref)

        rows = pl.ds(pl.multiple_of(n * CHUNK, CHUNK), CHUNK)

        zuv = z_ref[:, :1024]
        cdf, t = _gelu_parts(zuv)
        guv = zuv * cdf
        dgelu = cdf + zuv * (0.5 * (1.0 - t * t)) * (GELU_C * (1.0 + 3.0 * 0.044715 * (zuv * zuv)))
        for g in range(N_GROUP):
            lo, hi = 128 * g, 128 * (g + 1)
            u = guv[:, lo:hi]
            vg = guv[:, 512 + lo:512 + hi]
            rr = _rms_scale(vg)
            vhat = vg * rr
            gain = gain_ref[:, lo:hi]
            vnb = (vhat * gain).astype(BF16)
            sv = _dot(wt_ref[g], vnb) + bt_ref[:, g:g + 1]
            da = dm_ref[:, lo:hi]
            dsv = da * u
            dsvb = dsv.astype(BF16)
            dsv_ref[g] += dsv
            dws_ref[g] += _dot_nt(dsvb, vnb)
            dvn = _dot(wtt_ref[g], dsvb)
            dgain_ref[:, lo:hi] += jnp.sum(dvn * vhat, axis=0, keepdims=True)
            dvg = _rms_bwd(dvn * gain, vhat, rr)
            dz_ref[rows, lo:hi] = (da * sv * dgelu[:, lo:hi]).astype(BF16)
            dz_ref[rows, 512 + lo:512 + hi] = (dvg * dgelu[:, 512 + lo:512 + hi]).astype(BF16)

        k_same, k_swap, v_same, v_swap = _kv_layouts(kvp_ref[...], z_ref[:, 1536:1792])
        valid = _band_mask(n)
        lane_half = lax.broadcasted_iota(jnp.int32, (1, 128), 1) // 64
        zero = jnp.zeros((2 * CHUNK, 128), F32)
        dk_same, dk_swap, dv_same, dv_swap = zero, zero, zero, zero
        for pair in range(N_HEAD // 2):
            cols = slice(1024 + 128 * pair, 1024 + 128 * (pair + 1))
            qq = z_ref[:, cols]
            do_pair = dm_ref[:, 512 + 128 * pair:512 + 128 * (pair + 1)]
            dq = jnp.zeros((CHUNK, 128), F32)
            for pos in range(2):
                h = 2 * pair + pos
                _, _, same = _head_place(h)
                on_half = lane_half == pos
                qm = jnp.where(on_half, qq, 0.0).astype(BF16)
                k_use = k_same if same else k_swap
                v_use = v_same if same else v_swap
                p, p_sink = _softmax_sink(qm, k_use, bias_ref[h], sink_ref[h], valid)
                dom = jnp.where(on_half, do_pair, 0.0).astype(BF16)
                dp = _dot_nt(dom, v_use)
                dsum = jnp.sum(p * dp, axis=-1, keepdims=True)
                ds = p * (dp - dsum)
                dbias_ref[h] += ds
                dsink_ref[h:h + 1, :] += jnp.broadcast_to(jnp.sum(-p_sink * dsum, axis=0, keepdims=True), (1, 128))
                dsb = ds.astype(BF16)
                dq = dq + jnp.where(on_half, _dot(dsb, k_use), 0.0)
                dk_h = _dot_tn(dsb, qm)
                dv_h = _dot_tn(p.astype(BF16), dom)
                if same:
                    dk_same, dv_same = dk_same + dk_h, dv_same + dv_h
                else:
                    dk_swap, dv_swap = dk_swap + dk_h, dv_swap + dv_h
            dz_ref[rows, cols] = (dq * QK_SCALE).astype(BF16)
        dk = (dk_same + pltpu.roll(dk_swap, 64, axis=1)) * QK_SCALE
        dv = dv_same + pltpu.roll(dv_swap, 64, axis=1)
        dkv = jnp.concatenate([dk, dv], axis=1)

        @pl.when(n > 0)
        def _():
            prev_rows = pl.ds(pl.multiple_of((n - 1) * CHUNK, CHUNK), CHUNK)
            dz_ref[prev_rows, 1536:1792] = (carry_ref[...] + dkv[:CHUNK]).astype(BF16)

        carry_ref[...] = dkv[CHUNK:]

        @pl.when(n == N_BLOCK - 1)
        def _():
            dz_ref[rows, 1536:1792] = dkv[CHUNK:].astype(BF16)
            r = lax.broadcasted_iota(jnp.int32, (CHUNK, CHUNK), 0)
            c = lax.broadcasted_iota(jnp.int32, (CHUNK, CHUNK), 1)
            for g in range(N_GROUP):
                dws_ref[g] = jnp.where(c <= r, dws_ref[g], 0.0)
                db_ref[g] = jnp.sum(dsv_ref[g], axis=1, keepdims=True)
            bucket = bucket_ref[...]
            for h in range(N_HEAD):
                dbh = dbias_ref[h]
                per_bucket = [jnp.sum(jnp.where(bucket == b, dbh, 0.0), axis=0, keepdims=True) for b in range(N_BUCKET)]
                drel_ref[h] = jnp.sum(jnp.concatenate(per_bucket, axis=0), axis=1, keepdims=True)

    return _call(
        body, (z, z, dmix, v_gain, w_spatial, b_spatial_t, sinks, rel_table, bucket), grid=(N_BLOCK,), name="mixer_bwd",
        in_specs=[pl.BlockSpec((CHUNK, D_IN), lambda n: (n, 0)),
                  pl.BlockSpec((CHUNK, 256), lambda n: (jnp.maximum(n - 1, 0), 6)),
                  pl.BlockSpec((CHUNK, D), lambda n: (n, 0)),
                  pl.BlockSpec((1, 512), lambda n: (0, 0)),
                  pl.BlockSpec((N_GROUP, CHUNK, CHUNK), lambda n: (0, 0, 0)),
                  pl.BlockSpec((CHUNK, N_GROUP), lambda n: (0, 0)),
                  pl.BlockSpec(memory_space=pltpu.SMEM),
                  pl.BlockSpec(memory_space=pltpu.SMEM),
                  pl.BlockSpec((CHUNK, 2 * CHUNK), lambda n: (0, 0))],
        out_specs=[pl.BlockSpec((S, D_IN), lambda n: (0, 0)),
                   pl.BlockSpec((N_GROUP, CHUNK, CHUNK), lambda n: (0, 0, 0)),
                   pl.BlockSpec((N_GROUP, CHUNK, 1), lambda n: (0, 0, 0)),
                   pl.BlockSpec((1, 512), lambda n: (0, 0)),
                   pl.BlockSpec((N_HEAD, 128), lambda n: (0, 0)),
                   pl.BlockSpec((N_HEAD, N_BUCKET, 1), lambda n: (0, 0, 0))],
        out_shape=[jax.ShapeDtypeStruct((S, D_IN), BF16), jax.ShapeDtypeStruct((N_GROUP, CHUNK, CHUNK), F32),
                   jax.ShapeDtypeStruct((N_GROUP, CHUNK, 1), F32), jax.ShapeDtypeStruct((1, 512), F32),
                   jax.ShapeDtypeStruct((N_HEAD, 128), F32), jax.ShapeDtypeStruct((N_HEAD, N_BUCKET, 1), F32)],
        scratch_shapes=[pltpu.VMEM((N_HEAD, CHUNK, 2 * CHUNK), F32), pltpu.VMEM((N_GROUP, CHUNK, CHUNK), BF16),
                        pltpu.VMEM((N_GROUP, CHUNK, CHUNK), BF16), pltpu.VMEM((N_HEAD, CHUNK, 2 * CHUNK), F32),
                        pltpu.VMEM((N_GROUP, CHUNK, CHUNK), F32), pltpu.VMEM((CHUNK, 256), F32)],
        compiler_params=_params(48), exchange=exchange)


def _in_bwd(dz, hn1, w_in_t, x, dh1, gain1, exchange=None):
    tm = 256

    def body(dz_ref, hn_ref, w_ref, x_ref, dh1_ref, g_ref, dx_ref, dw_ref, dg_ref):
        i = pl.program_id(0)
        dzb = dz_ref[...]
        dhn = _dot(dzb, w_ref[...])
        xv = x_ref[...]
        r1 = _rms_scale(xv)
        xhat = xv * r1
        dx_ref[...] = dh1_ref[...] + _rms_bwd(dhn * g_ref[...], xhat, r1)
        dw = _dot_tn(dzb, hn_ref[...])
        dg = jnp.sum(dhn * xhat, axis=0, keepdims=True)

        @pl.when(i == 0)
        def _():
            dw_ref[...] = dw
            dg_ref[...] = dg

        @pl.when(i > 0)
        def _():
            dw_ref[...] += dw
            dg_ref[...] += dg

    return _call(
        body, (dz, hn1, w_in_t, x, dh1, gain1), grid=(S // tm,), name="in_bwd",
        in_specs=[pl.BlockSpec((tm, D_IN), lambda i: (i, 0)), pl.BlockSpec((tm, D), lambda i: (i, 0)),
                  pl.BlockSpec((D_IN, D), lambda i: (0, 0)), pl.BlockSpec((tm, D), lambda i: (i, 0)),
                  pl.BlockSpec((tm, D), lambda i: (i, 0)), pl.BlockSpec((1, D), lambda i: (0, 0))],
        out_specs=[pl.BlockSpec((tm, D), lambda i: (i, 0)), pl.BlockSpec((D_IN, D), lambda i: (0, 0)),
                   pl.BlockSpec((1, D), lambda i: (0, 0))],
        out_shape=[jax.ShapeDtypeStruct((S, D), F32), jax.ShapeDtypeStruct((D_IN, D), F32),
                   jax.ShapeDtypeStruct((1, D), F32)],
        compiler_params=_params(48), exchange=exchange)


def _rel_bucket():
    a = jnp.arange(CHUNK)[:, None]
    j = jnp.arange(2 * CHUNK)[None, :]
    n = jnp.maximum(CHUNK + a - j, 0)
    max_exact = N_BUCKET // 2
    nf = jnp.maximum(n, 1).astype(jnp.float32)
    large = max_exact + (jnp.log(nf / max_exact) / math.log(CHUNK / max_exact) * (N_BUCKET - max_exact)).astype(jnp.int32)
    large = jnp.minimum(large, N_BUCKET - 1)
    return jnp.where(n < max_exact, n, large).astype(jnp.int32)


def _step(x, p, target, small, bufs, place):
    bucket = _rel_bucket()
    sinks = small["attn_sinks"].reshape(N_HEAD)
    b_t = jnp.transpose(small["b_spatial"].reshape(N_GROUP, CHUNK))
    ws = small["w_spatial"].reshape(N_GROUP, CHUNK, CHUNK)
    gain1, gain2 = small["norm1_gain"], small["norm2_gain"]
    v_gain = small["gmlp_v_gain"]
    final_gain = small["final_gain"].reshape(1, D)
    table = small["rel_bias_table"]
    bufs = dict(bufs)

    def gather(*names):
        return _Gather([bufs[n] for n in names])

    def took(names, got):
        bufs.update(zip(names, got))

    took(["w_in"], _gather_weights([bufs["w_in"]]))
    w_in_t = _whole(bufs["w_in"]).reshape(D_IN, D)
    (z, hn1), got = _in_proj(x, gain1, w_in_t, gather("w_out"))
    took(["w_out"], got)
    (mix,), got = _mixer_fwd(z, v_gain, ws, b_t, sinks, table, bucket, gather("w_ff1"))
    took(["w_ff1"], got)
    w_out = _whole(bufs["w_out"]).reshape(D, D)
    (h1, hn2), got = _out_proj(x, mix, w_out, gain2, gather("w_ple_gate", "w_ple_proj"))
    took(["w_ple_gate", "w_ple_proj"], got)
    w_ff1 = _whole(bufs["w_ff1"])
    (r, a), got = _ffn_up(hn2, w_ff1, gather("w_ff2"))
    took(["w_ff2"], got)
    w_ff2 = _whole(bufs["w_ff2"])
    (h2,), _ = _ffn_down(h1, a, w_ff2)
    dh2, d_gate, d_proj, d_final, sq = _tail(h2, p, target, _whole(bufs["w_ple_gate"]).reshape(D, D),
                                             _whole(bufs["w_ple_proj"]), final_gain)

    def pair_sums(halves, from_sibling):
        sums, landing = zip(*[_pair_sum(g, o, place) for g, o in zip(halves, from_sibling)])
        return list(sums), list(landing)

    landed = {}
    halves = [_halves(d_gate.reshape(N_CHIP, 256, D)), _halves(d_proj)]
    ex = _ChipExchange(*pair_sums(halves, _sibling_exchange(halves)))
    (df, d_ff1, d_ff2), got = _ffn_bwd_weights(dh2, hn2, r, a, w_ff2, ex)
    landed.update(zip(["w_ple_gate", "w_ple_proj"], got))
    halves = [_halves(d_ff1), _halves(d_ff2)]
    (dh1, dmix, d_out, d_gain2), got = _ffn_bwd_input(df, w_ff1, dh2, h1, gain2, mix, w_out, _SiblingExchange(halves))
    ex = _ChipExchange(*pair_sums(halves, got))
    (dz, d_ws, d_b, d_vgain, d_sink, d_rel), got = _mixer_bwd(z, dmix, v_gain, ws, b_t, sinks, table, bucket, ex)
    landed.update(zip(["w_ff1", "w_ff2"], got))
    halves = [_halves(d_out.reshape(N_CHIP, 256, D))]
    ex = _ChipExchange(*pair_sums(halves, _sibling_exchange(halves)))
    (dx, d_in_t, d_gain1), got = _in_bwd(dz, hn1, w_in_t, x, dh1, gain1, ex)
    landed.update(zip(["w_out"], got))
    halves = [_halves(d_in_t.reshape(N_CHIP, 448, D))]
    last = _ChipExchange(*pair_sums(halves, _sibling_exchange(halves)))

    small_grads = {
        "norm1_gain": d_gain1, "gmlp_v_gain": d_vgain, "w_spatial": d_ws.reshape(1, N_GROUP, CHUNK, CHUNK),
        "b_spatial": d_b.reshape(1, N_GROUP, CHUNK), "attn_sinks": d_sink[:, 0].reshape(1, N_HEAD),
        "rel_bias_table": jnp.transpose(d_rel.reshape(N_HEAD, N_BUCKET)), "norm2_gain": d_gain2,
        "final_gain": d_final.reshape(D),
    }
    return sq, dx, landed, last, small_grads


HBM_SPEC = pl.BlockSpec(memory_space=pltpu.HBM)
VMEM_SPEC = pl.BlockSpec(memory_space=pltpu.VMEM)


def _mesh_place():
    x, y, c = lax.axis_index("x"), lax.axis_index("y"), lax.axis_index("c")
    others = [(1 - x, y), (x, 1 - y), (1 - x, 1 - y)]
    return x, y, c, others


def _remote(src, dst, send_sem, recv_sem, device):
    return pltpu.make_async_remote_copy(src_ref=src, dst_ref=dst, send_sem=send_sem, recv_sem=recv_sem,
                                        device_id=device, device_id_type=MESH)


def _hbm_like(a, shape=None, dtype=None):
    return pltpu.HBM(a.shape if shape is None else shape, a.dtype if dtype is None else dtype)


def _gather_start(bufs, send_sems, recv_sems):
    x, y, c, others = _mesh_place()
    me = 2 * x + y
    for w, buf in enumerate(bufs):
        for k, (ox, oy) in enumerate(others):
            mine = buf.at[me, c]
            _remote(mine, mine, send_sems.at[w, k], recv_sems.at[w, k], (ox, oy, c)).start()


def _gather_finish(bufs, send_sems, recv_sems):
    x, y, c, others = _mesh_place()
    me = 2 * x + y
    sibling = (x, y, 1 - c)
    idx = [2 * ox + oy for ox, oy in others]
    for w, buf in enumerate(bufs):
        for k in range(3):
            landed = buf.at[idx[k], c]
            _remote(landed, landed, send_sems.at[w, k], recv_sems.at[w, k], sibling).wait_recv()
            _remote(landed, landed, send_sems.at[w, 3 + k], recv_sems.at[w, 3 + k], sibling).start()
    for w, buf in enumerate(bufs):
        for k in range(3):
            landed = buf.at[idx[k], 1 - c]
            _remote(landed, landed, send_sems.at[w, 3 + k], recv_sems.at[w, 3 + k], sibling).wait_recv()
    for w, buf in enumerate(bufs):
        for k in range(3):
            mine, passed = buf.at[me, c], buf.at[idx[k], c]
            _remote(mine, mine, send_sems.at[w, k], recv_sems.at[w, k], sibling).wait_send()
            _remote(passed, passed, send_sems.at[w, 3 + k], recv_sems.at[w, 3 + k], sibling).wait_send()


def _gather_sems(n):
    return [pltpu.SemaphoreType.DMA((n, 6)), pltpu.SemaphoreType.DMA((n, 6))]


def _gather_weights(bufs):
    n = len(bufs)

    def body(*refs):
        outs = refs[n:2 * n]
        send_sems, recv_sems = refs[2 * n:]
        _gather_start(outs, send_sems, recv_sems)
        _gather_finish(outs, send_sems, recv_sems)

    return pl.pallas_call(
        body, name="gather_weights",
        in_specs=[HBM_SPEC] * n, out_specs=[HBM_SPEC] * n,
        out_shape=[_hbm_like(b) for b in bufs],
        input_output_aliases={w: w for w in range(n)},
        scratch_shapes=_gather_sems(n),
    )(*bufs)


def _sibling_copies(grads, landing, send_sems, recv_sems):
    x, y, c, _ = _mesh_place()
    return [_remote(grads[w].at[j, 1 - c], landing[w].at[j], send_sems.at[w, j], recv_sems.at[w, j], (x, y, 1 - c))
            for w in range(len(grads)) for j in range(N_CHIP)]


def _sibling_exchange_start(grads, landing, send_sems, recv_sems):
    for cp in _sibling_copies(grads, landing, send_sems, recv_sems):
        cp.start()


def _sibling_exchange_finish(grads, landing, send_sems, recv_sems):
    copies = _sibling_copies(grads, landing, send_sems, recv_sems)
    for cp in copies:
        cp.wait_recv()
    for cp in copies:
        cp.wait_send()


def _sibling_exchange_sems(n):
    return [pltpu.SemaphoreType.DMA((n, N_CHIP)), pltpu.SemaphoreType.DMA((n, N_CHIP))]


def _sibling_exchange(grads):
    n = len(grads)

    def body(*refs):
        ins, outs = refs[:n], refs[n:2 * n]
        _sibling_exchange_start(ins, outs, *refs[2 * n:])
        _sibling_exchange_finish(ins, outs, *refs[2 * n:])

    return pl.pallas_call(
        body, name="sibling_exchange",
        in_specs=[HBM_SPEC] * n, out_specs=[HBM_SPEC] * n,
        out_shape=[_hbm_like(g, (N_CHIP,) + g.shape[2:]) for g in grads],
        scratch_shapes=_sibling_exchange_sems(n),
    )(*[_in_hbm(g) for g in grads])


def _chip_exchange_start(sums, landing, send_sems, recv_sems):
    x, y, c, others = _mesh_place()
    me = 2 * x + y
    for w in range(len(sums)):
        for k, (ox, oy) in enumerate(others):
            _remote(sums[w].at[2 * ox + oy], landing[w].at[me], send_sems.at[w, k], recv_sems.at[w, k],
                    (ox, oy, c)).start()


def _chip_exchange_finish(sums, landing, send_sems, recv_sems):
    x, y, c, others = _mesh_place()
    for w in range(len(sums)):
        for k, (ox, oy) in enumerate(others):
            piece = landing[w].at[2 * ox + oy]
            _remote(piece, piece, send_sems.at[w, k], recv_sems.at[w, k], (x, y, c)).wait_recv()
    for w in range(len(sums)):
        for k, (ox, oy) in enumerate(others):
            piece = sums[w].at[2 * ox + oy]
            _remote(piece, piece, send_sems.at[w, k], recv_sems.at[w, k], (x, y, c)).wait_send()


def _chip_exchange_sems(n):
    return [pltpu.SemaphoreType.DMA((n, 3)), pltpu.SemaphoreType.DMA((n, 3))]


def _chip_exchange(sums, landing):
    n = len(sums)

    def body(*refs):
        ins, outs = refs[:n], refs[2 * n:3 * n]
        send_sems, recv_sems = refs[3 * n:]
        _chip_exchange_start(ins, outs, send_sems, recv_sems)
        _chip_exchange_finish(ins, outs, send_sems, recv_sems)

    return pl.pallas_call(
        body, name="chip_exchange",
        in_specs=[HBM_SPEC] * (2 * n), out_specs=[HBM_SPEC] * n,
        out_shape=[_hbm_like(b) for b in landing],
        input_output_aliases={n + w: w for w in range(n)},
        scratch_shapes=_chip_exchange_sems(n),
    )(*sums, *landing)


def _sibling_allgather(bufs):
    n = len(bufs)

    def body(*refs):
        outs = refs[n:2 * n]
        send_sems, recv_sems = refs[2 * n:]
        x, y, c, _ = _mesh_place()
        sibling = (x, y, 1 - c)
        sends = [_remote(outs[w].at[c], outs[w].at[c], send_sems.at[w], recv_sems.at[w], sibling) for w in range(n)]
        for cp in sends:
            cp.start()
        for w in range(n):
            landed = outs[w].at[1 - c]
            _remote(landed, landed, send_sems.at[w], recv_sems.at[w], sibling).wait_recv()
        for cp in sends:
            cp.wait_send()

    return pl.pallas_call(
        body, name="sibling_allgather",
        in_specs=[HBM_SPEC] * n, out_specs=[HBM_SPEC] * n,
        out_shape=[_hbm_like(b) for b in bufs],
        input_output_aliases={w: w for w in range(n)},
        scratch_shapes=[pltpu.SemaphoreType.DMA((n,)), pltpu.SemaphoreType.DMA((n,))],
    )(*bufs)


def _pair_sum(grad, other, place):
    _, _, h, cols = grad.shape
    tr = _row_tile(h)

    def body(place_ref, g_ref, o_ref, sums_ref, own_ref):
        s = (g_ref[0, 0] + o_ref[0]).astype(BF16)
        sums_ref[0] = s

        @pl.when(pl.program_id(1) == place_ref[0])
        def _():
            own_ref[0] = s

    return pl.pallas_call(
        body, name="pair_sum",
        grid_spec=pltpu.PrefetchScalarGridSpec(
            num_scalar_prefetch=1, grid=(h // tr, N_CHIP),
            in_specs=[pl.BlockSpec((1, 1, tr, cols), lambda r, j, place_ref: (j, place_ref[1], r, 0)),
                      pl.BlockSpec((1, tr, cols), lambda r, j, place_ref: (j, r, 0))],
            out_specs=[pl.BlockSpec((1, tr, cols), lambda r, j, place_ref: (j, r, 0)),
                       pl.BlockSpec((1, tr, cols), lambda r, j, place_ref: (place_ref[0], r, 0))]),
        out_shape=[pltpu.HBM((N_CHIP, h, cols), BF16)] * 2,
        compiler_params=_params(16, 2),
    )(place, _in_hbm(grad), _in_hbm(other))


def _chip_sum(parts, place):
    _, h, cols = parts.shape
    tr = _row_tile(h)

    def body(place_ref, p_ref, out_ref):
        out_ref[0] = ((p_ref[0].astype(F32) + p_ref[1].astype(F32)) + p_ref[2].astype(F32)) + p_ref[3].astype(F32)

    return pl.pallas_call(
        body, name="chip_sum",
        grid_spec=pltpu.PrefetchScalarGridSpec(
            num_scalar_prefetch=1, grid=(h // tr,),
            in_specs=[pl.BlockSpec((N_CHIP, tr, cols), lambda r, place_ref: (0, r, 0))],
            out_specs=pl.BlockSpec((1, tr, cols), lambda r, place_ref: (place_ref[1], r, 0))),
        out_shape=pltpu.HBM((2, h, cols), F32),
        compiler_params=_params(16),
    )(place, _in_hbm(parts))


def _adamw_math(w, g, m, v):
    m = ADAM_B1 * m + (1.0 - ADAM_B1) * g
    v = ADAM_B2 * v + (1.0 - ADAM_B2) * (g * g)
    m_hat = m / (1.0 - ADAM_B1 ** ADAM_STEP)
    v_hat = v / (1.0 - ADAM_B2 ** ADAM_STEP)
    delta = -ADAM_LR * (m_hat / (jnp.sqrt(v_hat) + ADAM_EPS) + ADAM_WD * w)
    return delta, m, v


def _adamw(w, g, m, v, exchange=None):
    rows, cols = w.shape
    tr = _row_tile(rows)

    def body(w_ref, g_ref, m_ref, v_ref, d_ref, nm_ref, nv_ref, g_out_ref):
        g = g_ref[...]
        d_ref[...], nm_ref[...], nv_ref[...] = _adamw_math(w_ref[...], g, m_ref[...], v_ref[...])
        g_out_ref[...] = g

    spec = pl.BlockSpec((tr, cols), lambda r: (r, 0))
    return _call(
        body, (w, g, m, v), grid=(rows // tr,), name="adamw",
        in_specs=[spec] * 4, out_specs=[spec] * 4,
        out_shape=[jax.ShapeDtypeStruct((rows, cols), F32)] * 4,
        compiler_params=_params(32), exchange=exchange)


SMALL_NAMES = ("norm1_gain", "gmlp_v_gain", "w_spatial", "b_spatial", "attn_sinks", "rel_bias_table", "norm2_gain",
               "final_gain")
PACK_TILE = 8 * 128


def _pack_small(arrays):
    parts = []
    for a in arrays:
        flat = a.reshape(-1)
        rows = -(-flat.shape[0] // PACK_TILE) * 8
        parts.append(jnp.pad(flat, (0, rows * 128 - flat.shape[0])).reshape(rows, 128))
    return jnp.concatenate(parts, axis=0)


def _unpack_small(packed, like):
    out, row = [], 0
    for a in like:
        size = math.prod(a.shape)
        rows = -(-size // PACK_TILE) * 8
        out.append(packed[row:row + rows].reshape(-1)[:size].reshape(a.shape))
        row += rows
    return out


def _small_update(grad, w, m, v):
    rows = grad.shape[0]

    def body(g_ref, w_ref, m_ref, v_ref, tot_ref, d_ref, nm_ref, nv_ref, buf, send_sems, recv_sems):
        x, y, c, _ = _mesh_place()
        me = 4 * x + 2 * y + c
        buf[me] = g_ref[...]
        sends = []
        for k in range(1, 8):
            kx, ky, kc = k // 4, (k // 2) % 2, k % 2
            peer = (1 - x if kx else x, 1 - y if ky else y, 1 - c if kc else c)
            cp = _remote(g_ref, buf.at[me], send_sems.at[k - 1], recv_sems.at[k - 1], peer)
            cp.start()
            sends.append((cp, 4 * peer[0] + 2 * peer[1] + peer[2]))
        for k, (cp, src) in enumerate(sends):
            _remote(g_ref, buf.at[src], send_sems.at[k], recv_sems.at[k], (x, y, c)).wait_recv()
        for cp, _ in sends:
            cp.wait_send()
        total = buf[0]
        for dev in range(1, 8):
            total = total + buf[dev]
        tot_ref[...] = total
        d_ref[...], nm_ref[...], nv_ref[...] = _adamw_math(w_ref[...], total, m_ref[...], v_ref[...])

    return pl.pallas_call(
        body, name="small_update",
        in_specs=[VMEM_SPEC] * 4, out_specs=[VMEM_SPEC] * 4,
        out_shape=[jax.ShapeDtypeStruct((rows, 128), F32)] * 4,
        scratch_shapes=[pltpu.VMEM((8, rows, 128), F32), pltpu.SemaphoreType.DMA((7,)), pltpu.SemaphoreType.DMA((7,))],
        compiler_params=pltpu.CompilerParams(vmem_limit_bytes=24 * MIB),
    )(grad, w, m, v)


def _halves(a):
    return a.reshape(a.shape[:-2] + (2, a.shape[-2] // 2, a.shape[-1]))


def _whole(a):
    return a.reshape(a.shape[:-3] + (2 * a.shape[-2], a.shape[-1]))


def kernel(x, p, norm1_gain, w_in, gmlp_v_gain, w_spatial, b_spatial, attn_sinks, rel_bias_table, w_out, norm2_gain, w_ff1, w_ff2, w_ple_proj, w_ple_gate, final_gain, loss_target, m_norm1_gain, m_w_in, m_gmlp_v_gain, m_w_spatial, m_b_spatial, m_attn_sinks, m_rel_bias_table, m_w_out, m_norm2_gain, m_w_ff1, m_w_ff2, m_w_ple_proj, m_w_ple_gate, m_final_gain, v_norm1_gain, v_w_in, v_gmlp_v_gain, v_w_spatial, v_b_spatial, v_attn_sinks, v_rel_bias_table, v_w_out, v_norm2_gain, v_w_ff1, v_w_ff2, v_w_ple_proj, v_w_ple_gate, v_final_gain):
    given = dict(locals())
    small = {n: given[n] for n in SMALL_NAMES}
    chip = 2 * lax.axis_index("x") + lax.axis_index("y")
    place = jnp.stack([chip, lax.axis_index("c")]).astype(jnp.int32)

    big_names = ("w_in", "w_out", "w_ff1", "w_ff2", "w_ple_proj", "w_ple_gate")
    shards = {n: given[n][0] for n in big_names}
    travel = dict(shards, w_in=jnp.transpose(shards["w_in"]))
    bufs = {n: _cast_shard(travel[n], place[:1]) for n in big_names}
    sq, dx, landed, last, small_grads = _step(x[0], p[0, 0], loss_target[0], small, bufs, place)

    out_grad, out_delta, out_m, out_v = {}, {}, {}, {}

    def update(n, g, exchange=None):
        (delta, new_m, new_v, g_out), got = _adamw(shards[n], g, given["m_" + n][0], given["v_" + n][0], exchange)
        out_grad[n], out_delta[n], out_m[n], out_v[n] = g_out[None], delta[None], new_m[None], new_v[None]
        return got

    early = [n for n in big_names if n != "w_in"]
    reduced = dict(zip(early, _sibling_allgather([_chip_sum(landed[n], place) for n in early])))
    landed_in, = update("w_ff1", _whole(reduced["w_ff1"]), last)
    for n in early:
        if n != "w_ff1":
            update(n, _whole(reduced[n]))
    reduced_in, = _sibling_allgather([_chip_sum(landed_in, place)])
    update("w_in", jnp.transpose(_whole(reduced_in)))

    no_state = jnp.zeros((8, 128), F32)
    packed = _small_update(_pack_small([small_grads[n] for n in SMALL_NAMES] + [sq]),
                           _pack_small([given[n] for n in SMALL_NAMES] + [no_state]),
                           _pack_small([given["m_" + n] for n in SMALL_NAMES] + [no_state]),
                           _pack_small([given["v_" + n] for n in SMALL_NAMES] + [no_state]))
    like = [given[n] for n in SMALL_NAMES] + [sq]
    for res, out in zip(packed, (out_grad, out_delta, out_m, out_v)):
        out.update(zip(SMALL_NAMES + ("squared_error",), _unpack_small(res, like)))
    loss = 0.5 * out_grad["squared_error"][0, 0] / D

    order = ("norm1_gain", "w_in", "gmlp_v_gain", "w_spatial", "b_spatial", "attn_sinks", "rel_bias_table", "w_out",
             "norm2_gain", "w_ff1", "w_ff2", "w_ple_proj", "w_ple_gate", "final_gain")
    return (loss, dx[None], *[out_grad[n] for n in order], *[out_delta[n] for n in order],
            *[out_m[n] for n in order], *[out_v[n] for n in order])
```

```python
import functools
import math

import jax
import jax.numpy as jnp
from jax import lax
from jax.experimental import pallas as pl
from jax.experimental.pallas import tpu as pltpu

S = 2048
D = 1024
D_IN = 1792
D_FF = 4096
PLE = 256
N_CHIP = 4
N_GROUP = 4
CHUNK = 128
N_HEAD = 8
N_BLOCK = S // CHUNK
N_BUCKET = 32
EPS = 1e-6
NEG_INF = -1e30
QK_SCALE = 0.125
GELU_C = math.sqrt(2.0 / math.pi)

ADAM_LR = 0.001
ADAM_B1 = 0.9
ADAM_B2 = 0.999
ADAM_EPS = 1e-08
ADAM_WD = 0.01
ADAM_STEP = 10

F32 = jnp.float32
BF16 = jnp.bfloat16
MIB = 1024 * 1024
MESH = pl.DeviceIdType.MESH

NT = (((1,), (1,)), ((), ()))
TN = (((0,), (0,)), ((), ()))


def _dot(a, b):
    return jnp.dot(a, b, preferred_element_type=F32)


def _dot_nt(a, b):
    return lax.dot_general(a, b, NT, preferred_element_type=F32)


def _dot_tn(a, b):
    return lax.dot_general(a, b, TN, preferred_element_type=F32)


def _params(vmem_mib, n_axes=1):
    return pltpu.CompilerParams(dimension_semantics=("arbitrary",) * n_axes, vmem_limit_bytes=vmem_mib * MIB)


def _rms_scale(v):
    return lax.rsqrt(jnp.mean(v * v, axis=-1, keepdims=True) + EPS)


def _rms_bwd(dy_gain, xhat, r):
    return r * (dy_gain - xhat * jnp.mean(dy_gain * xhat, axis=-1, keepdims=True))


class _Gather:
    def __init__(self, bufs):
        self.operands = list(bufs)
        self.n_out = len(self.operands)
        self.out_shape = [_hbm_like(b) for b in bufs]
        self.aliases = {w: w for w in range(self.n_out)}
        self.sems = _gather_sems(self.n_out)

    def start(self, ins, outs, sems):
        _gather_start(outs, *sems)

    def finish(self, ins, outs, sems):
        _gather_finish(outs, *sems)


class _ChipExchange:
    def __init__(self, sums, landing):
        self.n_out = len(landing)
        self.operands = list(sums) + list(landing)
        self.out_shape = [_hbm_like(b) for b in landing]
        self.aliases = {self.n_out + w: w for w in range(self.n_out)}
        self.sems = _chip_exchange_sems(self.n_out)

    def start(self, ins, outs, sems):
        _chip_exchange_start(ins[:self.n_out], outs, *sems)

    def finish(self, ins, outs, sems):
        _chip_exchange_finish(ins[:self.n_out], outs, *sems)


class _GatherAll:
    def __init__(self, packed):
        self.operands = [packed]
        self.n_out = 1
        self.out_shape = [_hbm_like(packed, (8,) + packed.shape)]
        self.aliases = {}
        self.sems = [pltpu.SemaphoreType.DMA((8,)), pltpu.SemaphoreType.DMA((8,))]

    def _copies(self, ins, outs, sems):
        x, y, c, _ = _mesh_place()
        me = 4 * x + 2 * y + c
        send_sems, recv_sems = sems
        copies = []
        for k in range(1, 8):
            peer = (1 - x if k // 4 else x, 1 - y if (k // 2) % 2 else y, 1 - c if k % 2 else c)
            src = 4 * peer[0] + 2 * peer[1] + peer[2]
            copies.append((_remote(ins[0], outs[0].at[me], send_sems.at[k], recv_sems.at[k], peer), outs[0].at[src]))
        own = pltpu.make_async_copy(ins[0], outs[0].at[me], send_sems.at[0])
        return own, copies

    def start(self, ins, outs, sems):
        own, copies = self._copies(ins, outs, sems)
        own.start()
        for cp, _ in copies:
            cp.start()

    def finish(self, ins, outs, sems):
        own, copies = self._copies(ins, outs, sems)
        x, y, c, _ = _mesh_place()
        for k, (cp, landed) in enumerate(copies):
            _remote(landed, landed, sems[0].at[k + 1], sems[1].at[k + 1], (x, y, c)).wait_recv()
        for cp, _ in copies:
            cp.wait_send()
        own.wait()


class _Both:
    def __init__(self, a, b):
        self.a, self.b = a, b
        self.operands = a.operands + b.operands
        self.n_out = a.n_out + b.n_out
        self.out_shape = a.out_shape + b.out_shape
        self.aliases = dict(a.aliases)
        self.aliases.update({len(a.operands) + i: a.n_out + o for i, o in b.aliases.items()})
        self.sems = a.sems + b.sems

    def _split(self, ins, outs, sems):
        ka, na, sa = len(self.a.operands), self.a.n_out, len(self.a.sems)
        return (ins[:ka], outs[:na], sems[:sa]), (ins[ka:], outs[na:], sems[sa:])

    def start(self, ins, outs, sems):
        for ex, args in zip((self.a, self.b), self._split(ins, outs, sems)):
            ex.start(*args)

    def finish(self, ins, outs, sems):
        for ex, args in zip((self.a, self.b), self._split(ins, outs, sems)):
            ex.finish(*args)


class _SiblingExchange:
    def __init__(self, grads):
        self.operands = list(grads)
        self.n_out = len(self.operands)
        self.out_shape = [_hbm_like(g, (N_CHIP,) + g.shape[2:]) for g in grads]
        self.aliases = {}
        self.sems = _sibling_exchange_sems(self.n_out)

    def start(self, ins, outs, sems):
        _sibling_exchange_start(ins, outs, *sems)

    def finish(self, ins, outs, sems):
        _sibling_exchange_finish(ins, outs, *sems)


def _call(body, operands, *, grid, in_specs, out_specs, out_shape, name, compiler_params, scratch_shapes=(),
          exchange=None):
    operands = [o if getattr(spec, "memory_space", None) == pltpu.SMEM else _in_hbm(o)
                for o, spec in zip(operands, in_specs)]
    out_shape = [pltpu.HBM(s.shape, s.dtype) for s in out_shape]
    if exchange is None:
        res = pl.pallas_call(body, grid=grid, in_specs=in_specs, out_specs=out_specs, out_shape=out_shape, name=name,
                             scratch_shapes=list(scratch_shapes), compiler_params=compiler_params)(*operands)
        return list(res), []
    n_in, n_out, n_scr = len(in_specs), len(out_specs), len(scratch_shapes)
    k_in, k_out = len(exchange.operands), exchange.n_out

    def fused(*refs):
        ins, refs = refs[:n_in], refs[n_in:]
        ex_ins, refs = refs[:k_in], refs[k_in:]
        outs, refs = refs[:n_out], refs[n_out:]
        ex_outs, refs = refs[:k_out], refs[k_out:]
        scratch, sems = refs[:n_scr], refs[n_scr:]
        ids = [pl.program_id(a) for a in range(len(grid))]
        first = functools.reduce(jnp.logical_and, [i == 0 for i in ids])
        last = functools.reduce(jnp.logical_and, [i == g - 1 for i, g in zip(ids, grid)])

        @pl.when(first)
        def _():
            exchange.start(ex_ins, ex_outs, sems)

        body(*ins, *outs, *scratch)

        @pl.when(last)
        def _():
            exchange.finish(ex_ins, ex_outs, sems)

    res = pl.pallas_call(
        fused, grid=grid, name=name,
        in_specs=list(in_specs) + [HBM_SPEC] * k_in, out_specs=list(out_specs) + [HBM_SPEC] * k_out,
        out_shape=list(out_shape) + exchange.out_shape,
        input_output_aliases={n_in + i: n_out + o for i, o in exchange.aliases.items()},
        scratch_shapes=list(scratch_shapes) + exchange.sems, compiler_params=compiler_params,
    )(*operands, *[_in_hbm(o) for o in exchange.operands])
    return list(res[:n_out]), list(res[n_out:])


def _in_hbm(a):
    return pltpu.with_memory_space_constraint(a, pltpu.HBM)


def _row_tile(h):
    return max(t for t in range(16, 257, 16) if h % t == 0)


def _cast_shard(a, chip):
    rows, cols = a.shape
    h = rows // 2
    tr = _row_tile(h)

    def body(chip_ref, a_ref, o_ref):
        o_ref[0, 0] = a_ref[0].astype(BF16)

    return pl.pallas_call(
        body, name="cast_shard",
        grid_spec=pltpu.PrefetchScalarGridSpec(
            num_scalar_prefetch=1, grid=(2, h // tr),
            in_specs=[pl.BlockSpec((1, tr, cols), lambda s, r, chip_ref: (s, r, 0))],
            out_specs=pl.BlockSpec((1, 1, tr, cols), lambda s, r, chip_ref: (chip_ref[0], s, r, 0))),
        out_shape=pltpu.HBM((N_CHIP, 2, h, cols), BF16),
        compiler_params=_params(16, 2),
    )(chip, _in_hbm(a.reshape(2, h, cols)))


def _in_proj(x, gain1, w_in_t, exchange=None):
    tm = 256

    def body(x_ref, g_ref, w_ref, z_ref, hn_ref):
        xv = x_ref[...]
        hn = (xv * _rms_scale(xv) * g_ref[...]).astype(BF16)
        hn_ref[...] = hn
        z_ref[...] = _dot_nt(hn, w_ref[...])

    return _call(
        body, (x, gain1, w_in_t), grid=(S // tm,), name="in_proj",
        in_specs=[pl.BlockSpec((tm, D), lambda i: (i, 0)), pl.BlockSpec((1, D), lambda i: (0, 0)),
                  pl.BlockSpec((D_IN, D), lambda i: (0, 0))],
        out_specs=[pl.BlockSpec((tm, D_IN), lambda i: (i, 0)), pl.BlockSpec((tm, D), lambda i: (i, 0))],
        out_shape=[jax.ShapeDtypeStruct((S, D_IN), F32), jax.ShapeDtypeStruct((S, D), BF16)],
        compiler_params=_params(40), exchange=exchange)


def _gelu_parts(v):
    t = jnp.tanh(GELU_C * (v + 0.044715 * (v * v * v)))
    cdf = 0.5 * (1.0 + t)
    return cdf, t


def _band_mask(n):
    a = lax.broadcasted_iota(jnp.int32, (CHUNK, 2 * CHUNK), 0)
    j = lax.broadcasted_iota(jnp.int32, (CHUNK, 2 * CHUNK), 1)
    dist = CHUNK + a - j
    valid = (dist >= 0) & (dist < CHUNK)
    return valid & ((n > 0) | (j >= CHUNK))


def _fill_bias(bucket_ref, table_ref, bias_ref):
    bucket = bucket_ref[...]
    for h in range(N_HEAD):
        acc = jnp.zeros((CHUNK, 2 * CHUNK), F32)
        for b in range(N_BUCKET):
            acc = jnp.where(bucket == b, table_ref[b, h], acc)
        bias_ref[h] = acc


def _fill_tril(ws_ref, wt_ref, wtt_ref=None):
    r = lax.broadcasted_iota(jnp.int32, (CHUNK, CHUNK), 0)
    c = lax.broadcasted_iota(jnp.int32, (CHUNK, CHUNK), 1)
    for g in range(N_GROUP):
        w = jnp.where(c <= r, ws_ref[g], 0.0)
        wt_ref[g] = w.astype(BF16)
        if wtt_ref is not None:
            wtt_ref[g] = w.T.astype(BF16)


def _kv_layouts(kv_prev, kv_cur):
    both = jnp.concatenate([kv_prev, kv_cur], axis=0)
    k = both[:, :128]
    v = both[:, 128:]
    return (k.astype(BF16), pltpu.roll(k, 64, axis=1).astype(BF16),
            v.astype(BF16), pltpu.roll(v, 64, axis=1).astype(BF16))


def _head_place(h):
    pair, pos, kvh = h // 2, h % 2, h // 4
    return pair, pos, kvh == pos


def _softmax_sink(qm, k_use, bias_h, sink, valid):
    s = _dot_nt(qm, k_use) * QK_SCALE + bias_h
    s = jnp.where(valid, s, NEG_INF)
    m = jnp.maximum(jnp.max(s, axis=-1, keepdims=True), sink)
    e = jnp.exp(s - m)
    es = jnp.exp(sink - m)
    denom = jnp.sum(e, axis=-1, keepdims=True) + es
    return e / denom, es / denom


def _mixer_fwd(z, v_gain, w_spatial, b_spatial_t, sinks, rel_table, bucket, exchange=None):
    def body(z_ref, kvp_ref, gain_ref, ws_ref, bt_ref, sink_ref, table_ref, bucket_ref, out_ref, bias_ref, wt_ref):
        n = pl.program_id(0)

        @pl.when(n == 0)
        def _():
            _fill_bias(bucket_ref, table_ref, bias_ref)
            _fill_tril(ws_ref, wt_ref)

        zuv = z_ref[:, :1024]
        cdf, _ = _gelu_parts(zuv)
        guv = zuv * cdf
        for g in range(N_GROUP):
            vg = guv[:, 512 + 128 * g:512 + 128 * (g + 1)]
            vn = vg * _rms_scale(vg) * gain_ref[:, 128 * g:128 * (g + 1)]
            sv = _dot(wt_ref[g], vn.astype(BF16)) + bt_ref[:, g:g + 1]
            out_ref[:, 128 * g:128 * (g + 1)] = (guv[:, 128 * g:128 * (g + 1)] * sv).astype(BF16)

        k_same, k_swap, v_same, v_swap = _kv_layouts(kvp_ref[...], z_ref[:, 1536:1792])
        valid = _band_mask(n)
        lane_half = lax.broadcasted_iota(jnp.int32, (1, 128), 1) // 64
        for pair in range(N_HEAD // 2):
            qq = z_ref[:, 1024 + 128 * pair:1024 + 128 * (pair + 1)]
            acc = jnp.zeros((CHUNK, 128), F32)
            for pos in range(2):
                h = 2 * pair + pos
                _, _, same = _head_place(h)
                qm = jnp.where(lane_half == pos, qq, 0.0).astype(BF16)
                p, _ = _softmax_sink(qm, k_same if same else k_swap, bias_ref[h], sink_ref[h], valid)
                vm = jnp.where(lane_half == pos, v_same if same else v_swap, jnp.zeros((), BF16))
                acc = acc + _dot(p.astype(BF16), vm)
            out_ref[:, 512 + 128 * pair:512 + 128 * (pair + 1)] = acc.astype(BF16)

    return _call(
        body, (z, z, v_gain, w_spatial, b_spatial_t, sinks, rel_table, bucket), grid=(N_BLOCK,), name="mixer_fwd",
        in_specs=[pl.BlockSpec((CHUNK, D_IN), lambda n: (n, 0)),
                  pl.BlockSpec((CHUNK, 256), lambda n: (jnp.maximum(n - 1, 0), 6)),
                  pl.BlockSpec((1, 512), lambda n: (0, 0)),
                  pl.BlockSpec((N_GROUP, CHUNK, CHUNK), lambda n: (0, 0, 0)),
                  pl.BlockSpec((CHUNK, N_GROUP), lambda n: (0, 0)),
                  pl.BlockSpec(memory_space=pltpu.SMEM),
                  pl.BlockSpec(memory_space=pltpu.SMEM),
                  pl.BlockSpec((CHUNK, 2 * CHUNK), lambda n: (0, 0))],
        out_specs=[pl.BlockSpec((CHUNK, D), lambda n: (n, 0))],
        out_shape=[jax.ShapeDtypeStruct((S, D), BF16)],
        scratch_shapes=[pltpu.VMEM((N_HEAD, CHUNK, 2 * CHUNK), F32), pltpu.VMEM((N_GROUP, CHUNK, CHUNK), BF16)],
        compiler_params=_params(32), exchange=exchange)


def _out_proj(x, mix, w_out, gain2, exchange=None):
    tm = 256

    def body(x_ref, mix_ref, w_ref, g_ref, h1_ref, hn_ref):
        h1 = x_ref[...] + _dot(mix_ref[...], w_ref[...])
        h1_ref[...] = h1
        hn_ref[...] = (h1 * _rms_scale(h1) * g_ref[...]).astype(BF16)

    return _call(
        body, (x, mix, w_out, gain2), grid=(S // tm,), name="out_proj",
        in_specs=[pl.BlockSpec((tm, D), lambda i: (i, 0)), pl.BlockSpec((tm, D), lambda i: (i, 0)),
                  pl.BlockSpec((D, D), lambda i: (0, 0)), pl.BlockSpec((1, D), lambda i: (0, 0))],
        out_specs=[pl.BlockSpec((tm, D), lambda i: (i, 0)), pl.BlockSpec((tm, D), lambda i: (i, 0))],
        out_shape=[jax.ShapeDtypeStruct((S, D), F32), jax.ShapeDtypeStruct((S, D), BF16)],
        compiler_params=_params(32), exchange=exchange)


def _ffn_up(hn2, w_ff1, exchange=None):
    tm = 512
    nj = D_FF // 1024

    def body(hn_ref, w1_ref, r_ref, a_ref):
        r = jnp.maximum(_dot(hn_ref[...], w1_ref[0]), 0.0)
        r_ref[...] = r.astype(BF16)
        a_ref[...] = (r * r).astype(BF16)

    return _call(
        body, (hn2, w_ff1), grid=(nj, S // tm), name="ffn_up",
        in_specs=[pl.BlockSpec((tm, D), lambda j, i: (i, 0)), pl.BlockSpec((1, D, 1024), lambda j, i: (j, 0, 0))],
        out_specs=[pl.BlockSpec((tm, 1024), lambda j, i: (i, j)), pl.BlockSpec((tm, 1024), lambda j, i: (i, j))],
        out_shape=[jax.ShapeDtypeStruct((S, D_FF), BF16), jax.ShapeDtypeStruct((S, D_FF), BF16)],
        compiler_params=_params(32, 2), exchange=exchange)


def _ffn_down(h1, a, w_ff2, exchange=None):
    tm = 1024
    nj = D_FF // 1024

    def body(h1_ref, a_ref, w2_ref, h2_ref, acc_ref):
        j = pl.program_id(1)
        part = _dot(a_ref[...], w2_ref[0])

        @pl.when(j == 0)
        def _():
            acc_ref[...] = part

        @pl.when(j > 0)
        def _():
            acc_ref[...] += part

        @pl.when(j == nj - 1)
        def _():
            h2_ref[...] = h1_ref[...] + acc_ref[...]

    return _call(
        body, (h1, a, w_ff2), grid=(S // tm, nj), name="ffn_down",
        in_specs=[pl.BlockSpec((tm, D), lambda i, j: (i, 0)), pl.BlockSpec((tm, 1024), lambda i, j: (i, j)),
                  pl.BlockSpec((1, 1024, D), lambda i, j: (j, 0, 0))],
        out_specs=[pl.BlockSpec((tm, D), lambda i, j: (i, 0))],
        out_shape=[jax.ShapeDtypeStruct((S, D), F32)],
        scratch_shapes=[pltpu.VMEM((tm, D), F32)],
        compiler_params=_params(48, 2), exchange=exchange)


def _tail(h2, p, target, w_gate, w_proj, final_gain):
    tm = 256
    steps = S // tm

    def body(h2_ref, p_ref, t_ref, wg_ref, wp_ref, gf_ref, dh2_ref, dwg_ref, dwp_ref, dgf_ref, loss_ref, dwp_acc):
        i = pl.program_id(0)
        h2 = h2_ref[...]
        h2b = h2.astype(BF16)
        pb = p_ref[...].astype(BF16)
        gate = jax.nn.sigmoid(_dot(h2b, wg_ref[...]))
        pp = jnp.concatenate([_dot(pb, wp_ref[j]) for j in range(N_CHIP)], axis=1)
        h3 = h2 + gate * pp
        r3 = _rms_scale(h3)
        xhat = h3 * r3
        gf = gf_ref[...]
        err = xhat * gf - t_ref[...]
        dy = err * (1.0 / D)
        dh3 = _rms_bwd(dy * gf, xhat, r3)
        dgp = (dh3 * pp * gate * (1.0 - gate)).astype(BF16)
        dpp = (dh3 * gate).astype(BF16)
        dh2_ref[...] = dh3 + _dot_nt(dgp, wg_ref[...])
        dwg = _dot_tn(h2b, dgp)
        dwp = _dot_tn(pb, dpp)
        dgf = jnp.sum(dy * xhat, axis=0, keepdims=True)
        sq = jnp.sum(jnp.sum(err * err, axis=1, keepdims=True), axis=0, keepdims=True)

        @pl.when(i == 0)
        def _():
            dwg_ref[...] = dwg
            dwp_acc[...] = dwp
            dgf_ref[...] = dgf
            loss_ref[...] = jnp.broadcast_to(sq, (8, 128))

        @pl.when(i > 0)
        def _():
            dwg_ref[...] += dwg
            dwp_acc[...] += dwp
            dgf_ref[...] += dgf
            loss_ref[...] += jnp.broadcast_to(sq, (8, 128))

        @pl.when(i == steps - 1)
        def _():
            for j in range(N_CHIP):
                dwp_ref[j] = dwp_acc[:, 256 * j:256 * (j + 1)]

    return _call(
        body, (h2, p, target, w_gate, w_proj, final_gain), grid=(steps,), name="tail",
        in_specs=[pl.BlockSpec((tm, D), lambda i: (i, 0)), pl.BlockSpec((tm, PLE), lambda i: (i, 0)),
                  pl.BlockSpec((tm, D), lambda i: (i, 0)), pl.BlockSpec((D, D), lambda i: (0, 0)),
                  pl.BlockSpec((N_CHIP, PLE, 256), lambda i: (0, 0, 0)), pl.BlockSpec((1, D), lambda i: (0, 0))],
        out_specs=[pl.BlockSpec((tm, D), lambda i: (i, 0)), pl.BlockSpec((D, D), lambda i: (0, 0)),
                   pl.BlockSpec((N_CHIP, PLE, 256), lambda i: (0, 0, 0)), pl.BlockSpec((1, D), lambda i: (0, 0)),
                   pl.BlockSpec((8, 128), lambda i: (0, 0))],
        out_shape=[jax.ShapeDtypeStruct((S, D), F32), jax.ShapeDtypeStruct((D, D), F32),
                   jax.ShapeDtypeStruct((N_CHIP, PLE, 256), F32), jax.ShapeDtypeStruct((1, D), F32),
                   jax.ShapeDtypeStruct((8, 128), F32)],
        scratch_shapes=[pltpu.VMEM((PLE, D), F32)],
        compiler_params=_params(48))[0]


def _ffn_bwd_weights(dh2, hn2, r, a, w_ff2, exchange=None):
    tm = 512
    nj = D_FF // 1024

    def body(dh2_ref, hn_ref, r_ref, a_ref, w2_ref, df_ref, dw1_ref, dw2_ref):
        i = pl.program_id(1)
        dh2b = dh2_ref[...].astype(BF16)
        da = _dot_nt(dh2b, w2_ref[0])
        df = (da * (2.0 * r_ref[...].astype(F32))).astype(BF16)
        df_ref[...] = df
        dw1 = _dot_tn(hn_ref[...], df)
        dw2 = _dot_tn(a_ref[...], dh2b)

        @pl.when(i == 0)
        def _():
            dw1_ref[0] = dw1
            dw2_ref[0] = dw2

        @pl.when(i > 0)
        def _():
            dw1_ref[0] += dw1
            dw2_ref[0] += dw2

    return _call(
        body, (dh2, hn2, r, a, w_ff2), grid=(nj, S // tm), name="ffn_bwd_weights",
        in_specs=[pl.BlockSpec((tm, D), lambda j, i: (i, 0)), pl.BlockSpec((tm, D), lambda j, i: (i, 0)),
                  pl.BlockSpec((tm, 1024), lambda j, i: (i, j)), pl.BlockSpec((tm, 1024), lambda j, i: (i, j)),
                  pl.BlockSpec((1, 1024, D), lambda j, i: (j, 0, 0))],
        out_specs=[pl.BlockSpec((tm, 1024), lambda j, i: (i, j)), pl.BlockSpec((1, D, 1024), lambda j, i: (j, 0, 0)),
                   pl.BlockSpec((1, 1024, D), lambda j, i: (j, 0, 0))],
        out_shape=[jax.ShapeDtypeStruct((S, D_FF), BF16), jax.ShapeDtypeStruct((nj, D, 1024), F32),
                   jax.ShapeDtypeStruct((nj, 1024, D), F32)],
        compiler_params=_params(48, 2), exchange=exchange)


def _ffn_bwd_input(df, w_ff1, dh2, h1, gain2, mix, w_out, exchange=None):
    tm = 512
    nj = D_FF // 1024
    steps = S // tm

    def body(df_ref, w1_ref, dh2_ref, h1_ref, g_ref, mix_ref, wo_ref, dh1_ref, dmix_ref, dwo_ref, dg_ref, acc_ref):
        i = pl.program_id(0)
        j = pl.program_id(1)
        part = _dot_nt(df_ref[...], w1_ref[0])

        @pl.when(j == 0)
        def _():
            acc_ref[...] = part

        @pl.when(j > 0)
        def _():
            acc_ref[...] += part

        @pl.when(j == nj - 1)
        def _():
            dhn = acc_ref[...]
            h1 = h1_ref[...]
            r2 = _rms_scale(h1)
            xhat = h1 * r2
            dh1 = dh2_ref[...] + _rms_bwd(dhn * g_ref[...], xhat, r2)
            dh1_ref[...] = dh1
            dh1b = dh1.astype(BF16)
            dmix_ref[...] = _dot_nt(dh1b, wo_ref[...])
            dwo = _dot_tn(mix_ref[...], dh1b)
            dg = jnp.sum(dhn * xhat, axis=0, keepdims=True)

            @pl.when(i == 0)
            def _():
                dwo_ref[...] = dwo
                dg_ref[...] = dg

            @pl.when(i > 0)
            def _():
                dwo_ref[...] += dwo
                dg_ref[...] += dg

    return _call(
        body, (df, w_ff1, dh2, h1, gain2, mix, w_out), grid=(steps, nj), name="ffn_bwd_input",
        in_specs=[pl.BlockSpec((tm, 1024), lambda i, j: (i, j)), pl.BlockSpec((1, D, 1024), lambda i, j: (j, 0, 0)),
                  pl.BlockSpec((tm, D), lambda i, j: (i, 0)), pl.BlockSpec((tm, D), lambda i, j: (i, 0)),
                  pl.BlockSpec((1, D), lambda i, j: (0, 0)), pl.BlockSpec((tm, D), lambda i, j: (i, 0)),
                  pl.BlockSpec((D, D), lambda i, j: (0, 0))],
        out_specs=[pl.BlockSpec((tm, D), lambda i, j: (i, 0)), pl.BlockSpec((tm, D), lambda i, j: (i, 0)),
                   pl.BlockSpec((D, D), lambda i, j: (0, 0)), pl.BlockSpec((1, D), lambda i, j: (0, 0))],
        out_shape=[jax.ShapeDtypeStruct((S, D), F32), jax.ShapeDtypeStruct((S, D), F32),
                   jax.ShapeDtypeStruct((D, D), F32), jax.ShapeDtypeStruct((1, D), F32)],
        scratch_shapes=[pltpu.VMEM((tm, D), F32)],
        compiler_params=_params(56, 2), exchange=exchange)


def _mixer_bwd(z, dmix, v_gain, w_spatial, b_spatial_t, sinks, rel_table, bucket, exchange=None):
    def body(z_ref, kvp_ref, dm_ref, gain_ref, ws_ref, bt_ref, sink_ref, table_ref, bucket_ref,
             dz_ref, dws_ref, db_ref, dgain_ref, dsink_ref, drel_ref,
             bias_ref, wt_ref, wtt_ref, dbias_ref, dsv_ref, carry_ref):
        n = pl.program_id(0)

        @pl.when(n == 0)
        def _():
            _fill_bias(bucket_ref, table_ref, bias_ref)
            _fill_tril(ws_ref, wt_ref, wtt_ref)
            dbias_ref[...] = jnp.zeros_like(dbias_ref)
            dsv_ref[...] = jnp.zeros_like(dsv_ref)
            dws_ref[...] = jnp.zeros_like(dws_ref)
            dgain_ref[...] = jnp.zeros_like(dgain_ref)
            dsink_ref[...] = jnp.zeros_like(dsink_ref)

        rows = pl.ds(pl.multiple_of(n * CHUNK, CHUNK), CHUNK)

        zuv = z_ref[:, :1024]
        cdf, t = _gelu_parts(zuv)
        guv = zuv * cdf
        dgelu = cdf + zuv * (0.5 * (1.0 - t * t)) * (GELU_C * (1.0 + 3.0 * 0.044715 * (zuv * zuv)))
        for g in range(N_GROUP):
            lo, hi = 128 * g, 128 * (g + 1)
            u = guv[:, lo:hi]
            vg = guv[:, 512 + lo:512 + hi]
            rr = _rms_scale(vg)
            vhat = vg * rr
            gain = gain_ref[:, lo:hi]
            vnb = (vhat * gain).astype(BF16)
            sv = _dot(wt_ref[g], vnb) + bt_ref[:, g:g + 1]
            da = dm_ref[:, lo:hi]
            dsv = da * u
            dsvb = dsv.astype(BF16)
            dsv_ref[g] += dsv
            dws_ref[g] += _dot_nt(dsvb, vnb)
            dvn = _dot(wtt_ref[g], dsvb)
            dgain_ref[:, lo:hi] += jnp.sum(dvn * vhat, axis=0, keepdims=True)
            dvg = _rms_bwd(dvn * gain, vhat, rr)
            dz_ref[rows, lo:hi] = (da * sv * dgelu[:, lo:hi]).astype(BF16)
            dz_ref[rows, 512 + lo:512 + hi] = (dvg * dgelu[:, 512 + lo:512 + hi]).astype(BF16)

        k_same, k_swap, v_same, v_swap = _kv_layouts(kvp_ref[...], z_ref[:, 1536:1792])
        valid = _band_mask(n)
        lane_half = lax.broadcasted_iota(jnp.int32, (1, 128), 1) // 64
        zero = jnp.zeros((2 * CHUNK, 128), F32)
        dk_same, dk_swap, dv_same, dv_swap = zero, zero, zero, zero
        for pair in range(N_HEAD // 2):
            cols = slice(1024 + 128 * pair, 1024 + 128 * (pair + 1))
            qq = z_ref[:, cols]
            do_pair = dm_ref[:, 512 + 128 * pair:512 + 128 * (pair + 1)]
            dq = jnp.zeros((CHUNK, 128), F32)
            for pos in range(2):
                h = 2 * pair + pos
                _, _, same = _head_place(h)
                on_half = lane_half == pos
                qm = jnp.where(on_half, qq, 0.0).astype(BF16)
                k_use = k_same if same else k_swap
                v_use = v_same if same else v_swap
                p, p_sink = _softmax_sink(qm, k_use, bias_ref[h], sink_ref[h], valid)
                dom = jnp.where(on_half, do_pair, 0.0).astype(BF16)
                dp = _dot_nt(dom, v_use)
                dsum = jnp.sum(p * dp, axis=-1, keepdims=True)
                ds = p * (dp - dsum)
                dbias_ref[h] += ds
                dsink_ref[h:h + 1, :] += jnp.broadcast_to(jnp.sum(-p_sink * dsum, axis=0, keepdims=True), (1, 128))
                dsb = ds.astype(BF16)
                dq = dq + jnp.where(on_half, _dot(dsb, k_use), 0.0)
                dk_h = _dot_tn(dsb, qm)
                dv_h = _dot_tn(p.astype(BF16), dom)
                if same:
                    dk_same, dv_same = dk_same + dk_h, dv_same + dv_h
                else:
                    dk_swap, dv_swap = dk_swap + dk_h, dv_swap + dv_h
            dz_ref[rows, cols] = (dq * QK_SCALE).astype(BF16)
        dk = (dk_same + pltpu.roll(dk_swap, 64, axis=1)) * QK_SCALE
        dv = dv_same + pltpu.roll(dv_swap, 64, axis=1)
        dkv = jnp.concatenate([dk, dv], axis=1)

        @pl.when(n > 0)
        def _():
            prev_rows = pl.ds(pl.multiple_of((n - 1) * CHUNK, CHUNK), CHUNK)
            dz_ref[prev_rows, 1536:1792] = (carry_ref[...] + dkv[:CHUNK]).astype(BF16)

        carry_ref[...] = dkv[CHUNK:]

        @pl.when(n == N_BLOCK - 1)
        def _():
            dz_ref[rows, 1536:1792] = dkv[CHUNK:].astype(BF16)
            r = lax.broadcasted_iota(jnp.int32, (CHUNK, CHUNK), 0)
            c = lax.broadcasted_iota(jnp.int32, (CHUNK, CHUNK), 1)
            for g in range(N_GROUP):
                dws_ref[g] = jnp.where(c <= r, dws_ref[g], 0.0)
                db_ref[g] = jnp.sum(dsv_ref[g], axis=1, keepdims=True)
            bucket = bucket_ref[...]
            for h in range(N_HEAD):
                dbh = dbias_ref[h]
                per_bucket = [jnp.sum(jnp.where(bucket == b, dbh, 0.0), axis=0, keepdims=True) for b in range(N_BUCKET)]
                drel_ref[h] = jnp.sum(jnp.concatenate(per_bucket, axis=0), axis=1, keepdims=True)

    return _call(
        body, (z, z, dmix, v_gain, w_spatial, b_spatial_t, sinks, rel_table, bucket), grid=(N_BLOCK,), name="mixer_bwd",
        in_specs=[pl.BlockSpec((CHUNK, D_IN), lambda n: (n, 0)),
                  pl.BlockSpec((CHUNK, 256), lambda n: (jnp.maximum(n - 1, 0), 6)),
                  pl.BlockSpec((CHUNK, D), lambda n: (n, 0)),
                  pl.BlockSpec((1, 512), lambda n: (0, 0)),
                  pl.BlockSpec((N_GROUP, CHUNK, CHUNK), lambda n: (0, 0, 0)),
                  pl.BlockSpec((CHUNK, N_GROUP), lambda n: (0, 0)),
                  pl.BlockSpec(memory_space=pltpu.SMEM),
                  pl.BlockSpec(memory_space=pltpu.SMEM),
                  pl.BlockSpec((CHUNK, 2 * CHUNK), lambda n: (0, 0))],
        out_specs=[pl.BlockSpec((S, D_IN), lambda n: (0, 0)),
                   pl.BlockSpec((N_GROUP, CHUNK, CHUNK), lambda n: (0, 0, 0)),
                   pl.BlockSpec((N_GROUP, CHUNK, 1), lambda n: (0, 0, 0)),
                   pl.BlockSpec((1, 512), lambda n: (0, 0)),
                   pl.BlockSpec((N_HEAD, 128), lambda n: (0, 0)),
                   pl.BlockSpec((N_HEAD, N_BUCKET, 1), lambda n: (0, 0, 0))],
        out_shape=[jax.ShapeDtypeStruct((S, D_IN), BF16), jax.ShapeDtypeStruct((N_GROUP, CHUNK, CHUNK), F32),
                   jax.ShapeDtypeStruct((N_GROUP, CHUNK, 1), F32), jax.ShapeDtypeStruct((1, 512), F32),
                   jax.ShapeDtypeStruct((N_HEAD, 128), F32), jax.ShapeDtypeStruct((N_HEAD, N_BUCKET, 1), F32)],
        scratch_shapes=[pltpu.VMEM((N_HEAD, CHUNK, 2 * CHUNK), F32), pltpu.VMEM((N_GROUP, CHUNK, CHUNK), BF16),
                        pltpu.VMEM((N_GROUP, CHUNK, CHUNK), BF16), pltpu.VMEM((N_HEAD, CHUNK, 2 * CHUNK), F32),
                        pltpu.VMEM((N_GROUP, CHUNK, CHUNK), F32), pltpu.VMEM((CHUNK, 256), F32)],
        compiler_params=_params(48), exchange=exchange)


def _in_bwd(dz, hn1, w_in_t, x, dh1, gain1, exchange=None):
    tm = 512

    def body(dz_ref, hn_ref, w_ref, x_ref, dh1_ref, g_ref, dx_ref, dw_ref, dg_ref):
        i = pl.program_id(0)
        dzb = dz_ref[...]
        dhn = _dot(dzb, w_ref[...])
        xv = x_ref[...]
        r1 = _rms_scale(xv)
        xhat = xv * r1
        dx_ref[...] = dh1_ref[...] + _rms_bwd(dhn * g_ref[...], xhat, r1)
        dw = _dot_tn(dzb, hn_ref[...])
        dg = jnp.sum(dhn * xhat, axis=0, keepdims=True)

        @pl.when(i == 0)
        def _():
            dw_ref[...] = dw
            dg_ref[...] = dg

        @pl.when(i > 0)
        def _():
            dw_ref[...] += dw
            dg_ref[...] += dg

    return _call(
        body, (dz, hn1, w_in_t, x, dh1, gain1), grid=(S // tm,), name="in_bwd",
        in_specs=[pl.BlockSpec((tm, D_IN), lambda i: (i, 0)), pl.BlockSpec((tm, D), lambda i: (i, 0)),
                  pl.BlockSpec((D_IN, D), lambda i: (0, 0)), pl.BlockSpec((tm, D), lambda i: (i, 0)),
                  pl.BlockSpec((tm, D), lambda i: (i, 0)), pl.BlockSpec((1, D), lambda i: (0, 0))],
        out_specs=[pl.BlockSpec((tm, D), lambda i: (i, 0)), pl.BlockSpec((D_IN, D), lambda i: (0, 0)),
                   pl.BlockSpec((1, D), lambda i: (0, 0))],
        out_shape=[jax.ShapeDtypeStruct((S, D), F32), jax.ShapeDtypeStruct((D_IN, D), F32),
                   jax.ShapeDtypeStruct((1, D), F32)],
        compiler_params=_params(56), exchange=exchange)


def _rel_bucket():
    a = jnp.arange(CHUNK)[:, None]
    j = jnp.arange(2 * CHUNK)[None, :]
    n = jnp.maximum(CHUNK + a - j, 0)
    max_exact = N_BUCKET // 2
    nf = jnp.maximum(n, 1).astype(jnp.float32)
    large = max_exact + (jnp.log(nf / max_exact) / math.log(CHUNK / max_exact) * (N_BUCKET - max_exact)).astype(jnp.int32)
    large = jnp.minimum(large, N_BUCKET - 1)
    return jnp.where(n < max_exact, n, large).astype(jnp.int32)


def _step(x, p, target, small, bufs, place):
    bucket = _rel_bucket()
    sinks = small["attn_sinks"].reshape(N_HEAD)
    b_t = jnp.transpose(small["b_spatial"].reshape(N_GROUP, CHUNK))
    ws = small["w_spatial"].reshape(N_GROUP, CHUNK, CHUNK)
    gain1, gain2 = small["norm1_gain"], small["norm2_gain"]
    v_gain = small["gmlp_v_gain"]
    final_gain = small["final_gain"].reshape(1, D)
    table = small["rel_bias_table"]
    bufs = dict(bufs)

    def gather(*names):
        return _Gather([bufs[n] for n in names])

    def took(names, got):
        bufs.update(zip(names, got))

    took(["w_in"], _gather_weights([bufs["w_in"]]))
    w_in_t = _whole(bufs["w_in"]).reshape(D_IN, D)
    (z, hn1), got = _in_proj(x, gain1, w_in_t, gather("w_out"))
    took(["w_out"], got)
    (mix,), got = _mixer_fwd(z, v_gain, ws, b_t, sinks, table, bucket, gather("w_ff1"))
    took(["w_ff1"], got)
    w_out = _whole(bufs["w_out"]).reshape(D, D)
    (h1, hn2), got = _out_proj(x, mix, w_out, gain2, gather("w_ple_gate", "w_ple_proj"))
    took(["w_ple_gate", "w_ple_proj"], got)
    w_ff1 = _whole(bufs["w_ff1"])
    (r, a), got = _ffn_up(hn2, w_ff1, gather("w_ff2"))
    took(["w_ff2"], got)
    w_ff2 = _whole(bufs["w_ff2"])
    (h2,), _ = _ffn_down(h1, a, w_ff2)
    dh2, d_gate, d_proj, d_final, sq = _tail(h2, p, target, _whole(bufs["w_ple_gate"]).reshape(D, D),
                                             _whole(bufs["w_ple_proj"]), final_gain)

    def pair_sums(halves, from_sibling):
        sums, landing = zip(*[_pair_sum(g, o, place) for g, o in zip(halves, from_sibling)])
        return list(sums), list(landing)

    landed = {}
    halves = [_halves(d_gate.reshape(N_CHIP, 256, D)), _halves(d_proj)]
    ex = _ChipExchange(*pair_sums(halves, _sibling_exchange(halves)))
    (df, d_ff1, d_ff2), got = _ffn_bwd_weights(dh2, hn2, r, a, w_ff2, ex)
    landed.update(zip(["w_ple_gate", "w_ple_proj"], got))
    halves = [_halves(d_ff1), _halves(d_ff2)]
    (dh1, dmix, d_out, d_gain2), got = _ffn_bwd_input(df, w_ff1, dh2, h1, gain2, mix, w_out, _SiblingExchange(halves))
    ex = _ChipExchange(*pair_sums(halves, got))
    (dz, d_ws, d_b, d_vgain, d_sink, d_rel), got = _mixer_bwd(z, dmix, v_gain, ws, b_t, sinks, table, bucket, ex)
    landed.update(zip(["w_ff1", "w_ff2"], got))
    small_grads = {
        "gmlp_v_gain": d_vgain, "w_spatial": d_ws.reshape(1, N_GROUP, CHUNK, CHUNK),
        "b_spatial": d_b.reshape(1, N_GROUP, CHUNK), "attn_sinks": d_sink[:, 0].reshape(1, N_HEAD),
        "rel_bias_table": jnp.transpose(d_rel.reshape(N_HEAD, N_BUCKET)), "norm2_gain": d_gain2,
        "final_gain": d_final.reshape(D), "squared_error": sq,
    }
    halves = [_halves(d_out.reshape(N_CHIP, 256, D))]
    ex = _Both(_ChipExchange(*pair_sums(halves, _sibling_exchange(halves))),
               _GatherAll(_pack_small([small_grads[n] for n in SMALL_EARLY])))
    (dx, d_in_t, d_gain1), got = _in_bwd(dz, hn1, w_in_t, x, dh1, gain1, ex)
    landed["w_out"], small_gathered = got
    halves = [_halves(d_in_t.reshape(N_CHIP, 448, D))]
    last = _Both(_ChipExchange(*pair_sums(halves, _sibling_exchange(halves))), _GatherAll(_pack_small([d_gain1])))
    return dx, landed, last, small_gathered


HBM_SPEC = pl.BlockSpec(memory_space=pltpu.HBM)
VMEM_SPEC = pl.BlockSpec(memory_space=pltpu.VMEM)


def _mesh_place():
    x, y, c = lax.axis_index("x"), lax.axis_index("y"), lax.axis_index("c")
    others = [(1 - x, y), (x, 1 - y), (1 - x, 1 - y)]
    return x, y, c, others


def _remote(src, dst, send_sem, recv_sem, device):
    return pltpu.make_async_remote_copy(src_ref=src, dst_ref=dst, send_sem=send_sem, recv_sem=recv_sem,
                                        device_id=device, device_id_type=MESH)


def _hbm_like(a, shape=None, dtype=None):
    return pltpu.HBM(a.shape if shape is None else shape, a.dtype if dtype is None else dtype)


def _gather_start(bufs, send_sems, recv_sems):
    x, y, c, others = _mesh_place()
    me = 2 * x + y
    for w, buf in enumerate(bufs):
        for k, (ox, oy) in enumerate(others):
            mine = buf.at[me, c]
            _remote(mine, mine, send_sems.at[w, k], recv_sems.at[w, k], (ox, oy, c)).start()


def _gather_finish(bufs, send_sems, recv_sems):
    x, y, c, others = _mesh_place()
    me = 2 * x + y
    sibling = (x, y, 1 - c)
    idx = [2 * ox + oy for ox, oy in others]
    for w, buf in enumerate(bufs):
        for k in range(3):
            landed = buf.at[idx[k], c]
            _remote(landed, landed, send_sems.at[w, k], recv_sems.at[w, k], sibling).wait_recv()
            _remote(landed, landed, send_sems.at[w, 3 + k], recv_sems.at[w, 3 + k], sibling).start()
    for w, buf in enumerate(bufs):
        for k in range(3):
            landed = buf.at[idx[k], 1 - c]
            _remote(landed, landed, send_sems.at[w, 3 + k], recv_sems.at[w, 3 + k], sibling).wait_recv()
    for w, buf in enumerate(bufs):
        for k in range(3):
            mine, passed = buf.at[me, c], buf.at[idx[k], c]
            _remote(mine, mine, send_sems.at[w, k], recv_sems.at[w, k], sibling).wait_send()
            _remote(passed, passed, send_sems.at[w, 3 + k], recv_sems.at[w, 3 + k], sibling).wait_send()


def _gather_sems(n):
    return [pltpu.SemaphoreType.DMA((n, 6)), pltpu.SemaphoreType.DMA((n, 6))]


def _gather_weights(bufs):
    n = len(bufs)

    def body(*refs):
        outs = refs[n:2 * n]
        send_sems, recv_sems = refs[2 * n:]
        _gather_start(outs, send_sems, recv_sems)
        _gather_finish(outs, send_sems, recv_sems)

    return pl.pallas_call(
        body, name="gather_weights",
        in_specs=[HBM_SPEC] * n, out_specs=[HBM_SPEC] * n,
        out_shape=[_hbm_like(b) for b in bufs],
        input_output_aliases={w: w for w in range(n)},
        scratch_shapes=_gather_sems(n),
    )(*bufs)


def _sibling_copies(grads, landing, send_sems, recv_sems):
    x, y, c, _ = _mesh_place()
    return [_remote(grads[w].at[j, 1 - c], landing[w].at[j], send_sems.at[w, j], recv_sems.at[w, j], (x, y, 1 - c))
            for w in range(len(grads)) for j in range(N_CHIP)]


def _sibling_exchange_start(grads, landing, send_sems, recv_sems):
    for cp in _sibling_copies(grads, landing, send_sems, recv_sems):
        cp.start()


def _sibling_exchange_finish(grads, landing, send_sems, recv_sems):
    copies = _sibling_copies(grads, landing, send_sems, recv_sems)
    for cp in copies:
        cp.wait_recv()
    for cp in copies:
        cp.wait_send()


def _sibling_exchange_sems(n):
    return [pltpu.SemaphoreType.DMA((n, N_CHIP)), pltpu.SemaphoreType.DMA((n, N_CHIP))]


def _sibling_exchange(grads):
    n = len(grads)

    def body(*refs):
        ins, outs = refs[:n], refs[n:2 * n]
        _sibling_exchange_start(ins, outs, *refs[2 * n:])
        _sibling_exchange_finish(ins, outs, *refs[2 * n:])

    return pl.pallas_call(
        body, name="sibling_exchange",
        in_specs=[HBM_SPEC] * n, out_specs=[HBM_SPEC] * n,
        out_shape=[_hbm_like(g, (N_CHIP,) + g.shape[2:]) for g in grads],
        scratch_shapes=_sibling_exchange_sems(n),
    )(*[_in_hbm(g) for g in grads])


def _chip_exchange_start(sums, landing, send_sems, recv_sems):
    x, y, c, others = _mesh_place()
    me = 2 * x + y
    for w in range(len(sums)):
        for k, (ox, oy) in enumerate(others):
            _remote(sums[w].at[2 * ox + oy], landing[w].at[me], send_sems.at[w, k], recv_sems.at[w, k],
                    (ox, oy, c)).start()


def _chip_exchange_finish(sums, landing, send_sems, recv_sems):
    x, y, c, others = _mesh_place()
    for w in range(len(sums)):
        for k, (ox, oy) in enumerate(others):
            piece = landing[w].at[2 * ox + oy]
            _remote(piece, piece, send_sems.at[w, k], recv_sems.at[w, k], (x, y, c)).wait_recv()
    for w in range(len(sums)):
        for k, (ox, oy) in enumerate(others):
            piece = sums[w].at[2 * ox + oy]
            _remote(piece, piece, send_sems.at[w, k], recv_sems.at[w, k], (x, y, c)).wait_send()


def _chip_exchange_sems(n):
    return [pltpu.SemaphoreType.DMA((n, 3)), pltpu.SemaphoreType.DMA((n, 3))]


def _chip_exchange(sums, landing):
    n = len(sums)

    def body(*refs):
        ins, outs = refs[:n], refs[2 * n:3 * n]
        send_sems, recv_sems = refs[3 * n:]
        _chip_exchange_start(ins, outs, send_sems, recv_sems)
        _chip_exchange_finish(ins, outs, send_sems, recv_sems)

    return pl.pallas_call(
        body, name="chip_exchange",
        in_specs=[HBM_SPEC] * (2 * n), out_specs=[HBM_SPEC] * n,
        out_shape=[_hbm_like(b) for b in landing],
        input_output_aliases={n + w: w for w in range(n)},
        scratch_shapes=_chip_exchange_sems(n),
    )(*sums, *landing)


def _sibling_allgather(bufs):
    n = len(bufs)

    def body(*refs):
        outs = refs[n:2 * n]
        send_sems, recv_sems = refs[2 * n:]
        x, y, c, _ = _mesh_place()
        sibling = (x, y, 1 - c)
        sends = [_remote(outs[w].at[c], outs[w].at[c], send_sems.at[w], recv_sems.at[w], sibling) for w in range(n)]
        for cp in sends:
            cp.start()
        for w in range(n):
            landed = outs[w].at[1 - c]
            _remote(landed, landed, send_sems.at[w], recv_sems.at[w], sibling).wait_recv()
        for cp in sends:
            cp.wait_send()

    return pl.pallas_call(
        body, name="sibling_allgather",
        in_specs=[HBM_SPEC] * n, out_specs=[HBM_SPEC] * n,
        out_shape=[_hbm_like(b) for b in bufs],
        input_output_aliases={w: w for w in range(n)},
        scratch_shapes=[pltpu.SemaphoreType.DMA((n,)), pltpu.SemaphoreType.DMA((n,))],
    )(*bufs)


def _pair_sum(grad, other, place):
    _, _, h, cols = grad.shape
    tr = _row_tile(h)

    def body(place_ref, g_ref, o_ref, sums_ref, own_ref):
        s = (g_ref[0, 0] + o_ref[0]).astype(BF16)
        sums_ref[0] = s

        @pl.when(pl.program_id(1) == place_ref[0])
        def _():
            own_ref[0] = s

    return pl.pallas_call(
        body, name="pair_sum",
        grid_spec=pltpu.PrefetchScalarGridSpec(
            num_scalar_prefetch=1, grid=(h // tr, N_CHIP),
            in_specs=[pl.BlockSpec((1, 1, tr, cols), lambda r, j, place_ref: (j, place_ref[1], r, 0)),
                      pl.BlockSpec((1, tr, cols), lambda r, j, place_ref: (j, r, 0))],
            out_specs=[pl.BlockSpec((1, tr, cols), lambda r, j, place_ref: (j, r, 0)),
                       pl.BlockSpec((1, tr, cols), lambda r, j, place_ref: (place_ref[0], r, 0))]),
        out_shape=[pltpu.HBM((N_CHIP, h, cols), BF16)] * 2,
        compiler_params=_params(16, 2),
    )(place, _in_hbm(grad), _in_hbm(other))


def _chip_sum(parts, place):
    _, h, cols = parts.shape
    tr = _row_tile(h)

    def body(place_ref, p_ref, out_ref):
        out_ref[0] = ((p_ref[0].astype(F32) + p_ref[1].astype(F32)) + p_ref[2].astype(F32)) + p_ref[3].astype(F32)

    return pl.pallas_call(
        body, name="chip_sum",
        grid_spec=pltpu.PrefetchScalarGridSpec(
            num_scalar_prefetch=1, grid=(h // tr,),
            in_specs=[pl.BlockSpec((N_CHIP, tr, cols), lambda r, place_ref: (0, r, 0))],
            out_specs=pl.BlockSpec((1, tr, cols), lambda r, place_ref: (place_ref[1], r, 0))),
        out_shape=pltpu.HBM((2, h, cols), F32),
        compiler_params=_params(16),
    )(place, _in_hbm(parts))


def _adamw_math(w, g, m, v):
    m = ADAM_B1 * m + (1.0 - ADAM_B1) * g
    v = ADAM_B2 * v + (1.0 - ADAM_B2) * (g * g)
    m_hat = m / (1.0 - ADAM_B1 ** ADAM_STEP)
    v_hat = v / (1.0 - ADAM_B2 ** ADAM_STEP)
    delta = -ADAM_LR * (m_hat / (jnp.sqrt(v_hat) + ADAM_EPS) + ADAM_WD * w)
    return delta, m, v


def _adamw(w, g, m, v, exchange=None):
    rows, cols = w.shape
    tr = _row_tile(rows)

    def body(w_ref, g_ref, m_ref, v_ref, d_ref, nm_ref, nv_ref, g_out_ref):
        g = g_ref[...]
        d_ref[...], nm_ref[...], nv_ref[...] = _adamw_math(w_ref[...], g, m_ref[...], v_ref[...])
        g_out_ref[...] = g

    spec = pl.BlockSpec((tr, cols), lambda r: (r, 0))
    return _call(
        body, (w, g, m, v), grid=(rows // tr,), name="adamw",
        in_specs=[spec] * 4, out_specs=[spec] * 4,
        out_shape=[jax.ShapeDtypeStruct((rows, cols), F32)] * 4,
        compiler_params=_params(32), exchange=exchange)


SMALL_NAMES = ("norm1_gain", "gmlp_v_gain", "w_spatial", "b_spatial", "attn_sinks", "rel_bias_table", "norm2_gain",
               "final_gain")
SMALL_EARLY = SMALL_NAMES[1:] + ("squared_error",)
PACK_TILE = 8 * 128


def _pack_small(arrays):
    parts = []
    for a in arrays:
        flat = a.reshape(-1)
        rows = -(-flat.shape[0] // PACK_TILE) * 8
        parts.append(jnp.pad(flat, (0, rows * 128 - flat.shape[0])).reshape(rows, 128))
    return jnp.concatenate(parts, axis=0)


def _unpack_small(packed, like):
    out, row = [], 0
    for a in like:
        size = math.prod(a.shape)
        rows = -(-size // PACK_TILE) * 8
        out.append(packed[row:row + rows].reshape(-1)[:size].reshape(a.shape))
        row += rows
    return out


def _small_update(gathered, w, m, v):
    rows = gathered.shape[1]

    def body(g_ref, w_ref, m_ref, v_ref, tot_ref, d_ref, nm_ref, nv_ref):
        total = g_ref[0]
        for dev in range(1, 8):
            total = total + g_ref[dev]
        tot_ref[...] = total
        d_ref[...], nm_ref[...], nv_ref[...] = _adamw_math(w_ref[...], total, m_ref[...], v_ref[...])

    return pl.pallas_call(
        body, name="small_update",
        in_specs=[VMEM_SPEC] * 4, out_specs=[VMEM_SPEC] * 4,
        out_shape=[jax.ShapeDtypeStruct((rows, 128), F32)] * 4,
        compiler_params=pltpu.CompilerParams(vmem_limit_bytes=24 * MIB),
    )(gathered, w, m, v)


def _halves(a):
    return a.reshape(a.shape[:-2] + (2, a.shape[-2] // 2, a.shape[-1]))


def _whole(a):
    return a.reshape(a.shape[:-3] + (2 * a.shape[-2], a.shape[-1]))


def kernel(x, p, norm1_gain, w_in, gmlp_v_gain, w_spatial, b_spatial, attn_sinks, rel_bias_table, w_out, norm2_gain, w_ff1, w_ff2, w_ple_proj, w_ple_gate, final_gain, loss_target, m_norm1_gain, m_w_in, m_gmlp_v_gain, m_w_spatial, m_b_spatial, m_attn_sinks, m_rel_bias_table, m_w_out, m_norm2_gain, m_w_ff1, m_w_ff2, m_w_ple_proj, m_w_ple_gate, m_final_gain, v_norm1_gain, v_w_in, v_gmlp_v_gain, v_w_spatial, v_b_spatial, v_attn_sinks, v_rel_bias_table, v_w_out, v_norm2_gain, v_w_ff1, v_w_ff2, v_w_ple_proj, v_w_ple_gate, v_final_gain):
    given = dict(locals())
    small = {n: given[n] for n in SMALL_NAMES}
    chip = 2 * lax.axis_index("x") + lax.axis_index("y")
    place = jnp.stack([chip, lax.axis_index("c")]).astype(jnp.int32)

    big_names = ("w_in", "w_out", "w_ff1", "w_ff2", "w_ple_proj", "w_ple_gate")
    shards = {n: given[n][0] for n in big_names}
    travel = dict(shards, w_in=jnp.transpose(shards["w_in"]))
    bufs = {n: _cast_shard(travel[n], place[:1]) for n in big_names}
    dx, landed, last, small_gathered = _step(x[0], p[0, 0], loss_target[0], small, bufs, place)

    out_grad, out_delta, out_m, out_v = {}, {}, {}, {}

    def update(n, g, exchange=None):
        to = jnp.transpose if n == "w_in" else (lambda a: a)
        (delta, new_m, new_v, g_out), got = _adamw(to(shards[n]), g, to(given["m_" + n][0]), to(given["v_" + n][0]),
                                                   exchange)
        out_grad[n], out_delta[n], out_m[n], out_v[n] = [to(a)[None] for a in (g_out, delta, new_m, new_v)]
        return got

    early = [n for n in big_names if n != "w_in"]
    reduced = dict(zip(early, _sibling_allgather([_chip_sum(landed[n], place) for n in early])))
    landed_in, gain1_gathered = update("w_ff1", _whole(reduced["w_ff1"]), last)
    for n in early:
        if n != "w_ff1":
            update(n, _whole(reduced[n]))
    reduced_in, = _sibling_allgather([_chip_sum(landed_in, place)])
    update("w_in", _whole(reduced_in))

    given["squared_error"] = given["m_squared_error"] = given["v_squared_error"] = jnp.zeros((8, 128), F32)
    for names, gathered in ((SMALL_EARLY, small_gathered), (SMALL_NAMES[:1], gain1_gathered)):
        like = [given[n] for n in names]
        packed = _small_update(gathered, *[_pack_small([given[pre + n] for n in names]) for pre in ("", "m_", "v_")])
        for res, out in zip(packed, (out_grad, out_delta, out_m, out_v)):
            out.update(zip(names, _unpack_small(res, like)))
    loss = 0.5 * out_grad["squared_error"][0, 0] / D

    order = ("norm1_gain", "w_in", "gmlp_v_gain", "w_spatial", "b_spatial", "attn_sinks", "rel_bias_table", "w_out",
             "norm2_gain", "w_ff1", "w_ff2", "w_ple_proj", "w_ple_gate", "final_gain")
    return (loss, dx[None], *[out_grad[n] for n in order], *[out_delta[n] for n in order],
            *[out_m[n] for n in order], *[out_v[n] for n in order])
```

```python
import functools
import math

import jax
import jax.numpy as jnp
from jax import lax
from jax.experimental import pallas as pl
from jax.experimental.pallas import tpu as pltpu

S = 2048
D = 1024
D_IN = 1792
D_FF = 4096
PLE = 256
N_CHIP = 4
N_GROUP = 4
CHUNK = 128
N_HEAD = 8
N_BLOCK = S // CHUNK
N_BUCKET = 32
EPS = 1e-6
NEG_INF = -1e30
QK_SCALE = 0.125
GELU_C = math.sqrt(2.0 / math.pi)

ADAM_LR = 0.001
ADAM_B1 = 0.9
ADAM_B2 = 0.999
ADAM_EPS = 1e-08
ADAM_WD = 0.01
ADAM_STEP = 10

F32 = jnp.float32
BF16 = jnp.bfloat16
MIB = 1024 * 1024
MESH = pl.DeviceIdType.MESH

NT = (((1,), (1,)), ((), ()))
TN = (((0,), (0,)), ((), ()))


def _dot(a, b):
    return jnp.dot(a, b, preferred_element_type=F32)


def _dot_nt(a, b):
    return lax.dot_general(a, b, NT, preferred_element_type=F32)


def _dot_tn(a, b):
    return lax.dot_general(a, b, TN, preferred_element_type=F32)


def _params(vmem_mib, n_axes=1):
    return pltpu.CompilerParams(dimension_semantics=("arbitrary",) * n_axes, vmem_limit_bytes=vmem_mib * MIB)


def _rms_scale(v):
    return lax.rsqrt(jnp.mean(v * v, axis=-1, keepdims=True) + EPS)


def _rms_bwd(dy_gain, xhat, r):
    return r * (dy_gain - xhat * jnp.mean(dy_gain * xhat, axis=-1, keepdims=True))


class _Gather:
    def __init__(self, bufs):
        self.operands = list(bufs)
        self.n_out = len(self.operands)
        self.out_shape = [_hbm_like(b) for b in bufs]
        self.aliases = {w: w for w in range(self.n_out)}
        self.sems = _gather_sems(self.n_out)

    def start(self, ins, outs, sems):
        _gather_start(outs, *sems)

    def finish(self, ins, outs, sems):
        _gather_finish(outs, *sems)


class _ChipExchange:
    def __init__(self, sums, landing):
        self.n_out = len(landing)
        self.operands = list(sums) + list(landing)
        self.out_shape = [_hbm_like(b) for b in landing]
        self.aliases = {self.n_out + w: w for w in range(self.n_out)}
        self.sems = _chip_exchange_sems(self.n_out)

    def start(self, ins, outs, sems):
        _chip_exchange_start(ins[:self.n_out], outs, *sems)

    def finish(self, ins, outs, sems):
        _chip_exchange_finish(ins[:self.n_out], outs, *sems)


class _GatherAll:
    def __init__(self, packed):
        self.operands = [packed]
        self.n_out = 1
        self.out_shape = [_hbm_like(packed, (8,) + packed.shape)]
        self.aliases = {}
        self.sems = [pltpu.SemaphoreType.DMA((8,)), pltpu.SemaphoreType.DMA((8,))]

    def _copies(self, ins, outs, sems):
        x, y, c, _ = _mesh_place()
        me = 4 * x + 2 * y + c
        send_sems, recv_sems = sems
        copies = []
        for k in range(1, 8):
            peer = (1 - x if k // 4 else x, 1 - y if (k // 2) % 2 else y, 1 - c if k % 2 else c)
            src = 4 * peer[0] + 2 * peer[1] + peer[2]
            copies.append((_remote(ins[0], outs[0].at[me], send_sems.at[k], recv_sems.at[k], peer), outs[0].at[src]))
        own = pltpu.make_async_copy(ins[0], outs[0].at[me], send_sems.at[0])
        return own, copies

    def start(self, ins, outs, sems):
        own, copies = self._copies(ins, outs, sems)
        own.start()
        for cp, _ in copies:
            cp.start()

    def finish(self, ins, outs, sems):
        own, copies = self._copies(ins, outs, sems)
        x, y, c, _ = _mesh_place()
        for k, (cp, landed) in enumerate(copies):
            _remote(landed, landed, sems[0].at[k + 1], sems[1].at[k + 1], (x, y, c)).wait_recv()
        for cp, _ in copies:
            cp.wait_send()
        own.wait()


class _Both:
    def __init__(self, a, b):
        self.a, self.b = a, b
        self.operands = a.operands + b.operands
        self.n_out = a.n_out + b.n_out
        self.out_shape = a.out_shape + b.out_shape
        self.aliases = dict(a.aliases)
        self.aliases.update({len(a.operands) + i: a.n_out + o for i, o in b.aliases.items()})
        self.sems = a.sems + b.sems

    def _split(self, ins, outs, sems):
        ka, na, sa = len(self.a.operands), self.a.n_out, len(self.a.sems)
        return (ins[:ka], outs[:na], sems[:sa]), (ins[ka:], outs[na:], sems[sa:])

    def start(self, ins, outs, sems):
        for ex, args in zip((self.a, self.b), self._split(ins, outs, sems)):
            ex.start(*args)

    def finish(self, ins, outs, sems):
        for ex, args in zip((self.a, self.b), self._split(ins, outs, sems)):
            ex.finish(*args)


class _SiblingExchange:
    def __init__(self, grads):
        self.operands = list(grads)
        self.n_out = len(self.operands)
        self.out_shape = [_hbm_like(g, (N_CHIP,) + g.shape[2:]) for g in grads]
        self.aliases = {}
        self.sems = _sibling_exchange_sems(self.n_out)

    def start(self, ins, outs, sems):
        _sibling_exchange_start(ins, outs, *sems)

    def finish(self, ins, outs, sems):
        _sibling_exchange_finish(ins, outs, *sems)


def _call(body, operands, *, grid, in_specs, out_specs, out_shape, name, compiler_params, scratch_shapes=(),
          exchange=None):
    operands = [o if getattr(spec, "memory_space", None) == pltpu.SMEM else _in_hbm(o)
                for o, spec in zip(operands, in_specs)]
    out_shape = [pltpu.HBM(s.shape, s.dtype) for s in out_shape]
    if exchange is None:
        res = pl.pallas_call(body, grid=grid, in_specs=in_specs, out_specs=out_specs, out_shape=out_shape, name=name,
                             scratch_shapes=list(scratch_shapes), compiler_params=compiler_params)(*operands)
        return list(res), []
    n_in, n_out, n_scr = len(in_specs), len(out_specs), len(scratch_shapes)
    k_in, k_out = len(exchange.operands), exchange.n_out

    def fused(*refs):
        ins, refs = refs[:n_in], refs[n_in:]
        ex_ins, refs = refs[:k_in], refs[k_in:]
        outs, refs = refs[:n_out], refs[n_out:]
        ex_outs, refs = refs[:k_out], refs[k_out:]
        scratch, sems = refs[:n_scr], refs[n_scr:]
        ids = [pl.program_id(a) for a in range(len(grid))]
        first = functools.reduce(jnp.logical_and, [i == 0 for i in ids])
        last = functools.reduce(jnp.logical_and, [i == g - 1 for i, g in zip(ids, grid)])

        @pl.when(first)
        def _():
            exchange.start(ex_ins, ex_outs, sems)

        body(*ins, *outs, *scratch)

        @pl.when(last)
        def _():
            exchange.finish(ex_ins, ex_outs, sems)

    res = pl.pallas_call(
        fused, grid=grid, name=name,
        in_specs=list(in_specs) + [HBM_SPEC] * k_in, out_specs=list(out_specs) + [HBM_SPEC] * k_out,
        out_shape=list(out_shape) + exchange.out_shape,
        input_output_aliases={n_in + i: n_out + o for i, o in exchange.aliases.items()},
        scratch_shapes=list(scratch_shapes) + exchange.sems, compiler_params=compiler_params,
    )(*operands, *[_in_hbm(o) for o in exchange.operands])
    return list(res[:n_out]), list(res[n_out:])


def _in_hbm(a):
    return pltpu.with_memory_space_constraint(a, pltpu.HBM)


def _row_tile(h):
    return max(t for t in range(16, 257, 16) if h % t == 0)


def _cast_shard(a, chip):
    rows, cols = a.shape
    h = rows // 2
    tr = _row_tile(h)

    def body(chip_ref, a_ref, o_ref):
        o_ref[0, 0] = a_ref[0].astype(BF16)

    return pl.pallas_call(
        body, name="cast_shard",
        grid_spec=pltpu.PrefetchScalarGridSpec(
            num_scalar_prefetch=1, grid=(2, h // tr),
            in_specs=[pl.BlockSpec((1, tr, cols), lambda s, r, chip_ref: (s, r, 0))],
            out_specs=pl.BlockSpec((1, 1, tr, cols), lambda s, r, chip_ref: (chip_ref[0], s, r, 0))),
        out_shape=pltpu.HBM((N_CHIP, 2, h, cols), BF16),
        compiler_params=_params(16, 2),
    )(chip, _in_hbm(a.reshape(2, h, cols)))


def _in_proj(x, gain1, w_in_t, exchange=None):
    tm = 256

    def body(x_ref, g_ref, w_ref, z_ref, hn_ref):
        xv = x_ref[...]
        hn = (xv * _rms_scale(xv) * g_ref[...]).astype(BF16)
        hn_ref[...] = hn
        z_ref[...] = _dot_nt(hn, w_ref[...])

    return _call(
        body, (x, gain1, w_in_t), grid=(S // tm,), name="in_proj",
        in_specs=[pl.BlockSpec((tm, D), lambda i: (i, 0)), pl.BlockSpec((1, D), lambda i: (0, 0)),
                  pl.BlockSpec((D_IN, D), lambda i: (0, 0))],
        out_specs=[pl.BlockSpec((tm, D_IN), lambda i: (i, 0)), pl.BlockSpec((tm, D), lambda i: (i, 0))],
        out_shape=[jax.ShapeDtypeStruct((S, D_IN), F32), jax.ShapeDtypeStruct((S, D), BF16)],
        compiler_params=_params(40), exchange=exchange)


def _gelu_parts(v):
    t = jnp.tanh(GELU_C * (v + 0.044715 * (v * v * v)))
    cdf = 0.5 * (1.0 + t)
    return cdf, t


def _band_mask(n):
    a = lax.broadcasted_iota(jnp.int32, (CHUNK, 2 * CHUNK), 0)
    j = lax.broadcasted_iota(jnp.int32, (CHUNK, 2 * CHUNK), 1)
    dist = CHUNK + a - j
    valid = (dist >= 0) & (dist < CHUNK)
    return valid & ((n > 0) | (j >= CHUNK))


def _fill_bias(bucket_ref, table_ref, bias_ref):
    bucket = bucket_ref[...]
    for h in range(N_HEAD):
        acc = jnp.zeros((CHUNK, 2 * CHUNK), F32)
        for b in range(N_BUCKET):
            acc = jnp.where(bucket == b, table_ref[b, h], acc)
        bias_ref[h] = acc


def _fill_tril(ws_ref, wt_ref, wtt_ref=None):
    r = lax.broadcasted_iota(jnp.int32, (CHUNK, CHUNK), 0)
    c = lax.broadcasted_iota(jnp.int32, (CHUNK, CHUNK), 1)
    for g in range(N_GROUP):
        w = jnp.where(c <= r, ws_ref[g], 0.0)
        wt_ref[g] = w.astype(BF16)
        if wtt_ref is not None:
            wtt_ref[g] = w.T.astype(BF16)


def _kv_layouts(kv_prev, kv_cur):
    both = jnp.concatenate([kv_prev, kv_cur], axis=0)
    k = both[:, :128]
    v = both[:, 128:]
    return (k.astype(BF16), pltpu.roll(k, 64, axis=1).astype(BF16),
            v.astype(BF16), pltpu.roll(v, 64, axis=1).astype(BF16))


def _head_place(h):
    pair, pos, kvh = h // 2, h % 2, h // 4
    return pair, pos, kvh == pos


def _softmax_sink(qm, k_use, bias_h, sink, valid):
    s = _dot_nt(qm, k_use) * QK_SCALE + bias_h
    s = jnp.where(valid, s, NEG_INF)
    m = jnp.maximum(jnp.max(s, axis=-1, keepdims=True), sink)
    e = jnp.exp(s - m)
    es = jnp.exp(sink - m)
    denom = jnp.sum(e, axis=-1, keepdims=True) + es
    return e / denom, es / denom


def _mixer_fwd(z, v_gain, w_spatial, b_spatial_t, sinks, rel_table, bucket, exchange=None):
    def body(z_ref, kvp_ref, gain_ref, ws_ref, bt_ref, sink_ref, table_ref, bucket_ref, out_ref, bias_ref, wt_ref):
        n = pl.program_id(0)

        @pl.when(n == 0)
        def _():
            _fill_bias(bucket_ref, table_ref, bias_ref)
            _fill_tril(ws_ref, wt_ref)

        zuv = z_ref[:, :1024]
        cdf, _ = _gelu_parts(zuv)
        guv = zuv * cdf
        for g in range(N_GROUP):
            vg = guv[:, 512 + 128 * g:512 + 128 * (g + 1)]
            vn = vg * _rms_scale(vg) * gain_ref[:, 128 * g:128 * (g + 1)]
            sv = _dot(wt_ref[g], vn.astype(BF16)) + bt_ref[:, g:g + 1]
            out_ref[:, 128 * g:128 * (g + 1)] = (guv[:, 128 * g:128 * (g + 1)] * sv).astype(BF16)

        k_same, k_swap, v_same, v_swap = _kv_layouts(kvp_ref[...], z_ref[:, 1536:1792])
        valid = _band_mask(n)
        lane_half = lax.broadcasted_iota(jnp.int32, (1, 128), 1) // 64
        for pair in range(N_HEAD // 2):
            qq = z_ref[:, 1024 + 128 * pair:1024 + 128 * (pair + 1)]
            acc = jnp.zeros((CHUNK, 128), F32)
            for pos in range(2):
                h = 2 * pair + pos
                _, _, same = _head_place(h)
                qm = jnp.where(lane_half == pos, qq, 0.0).astype(BF16)
                p, _ = _softmax_sink(qm, k_same if same else k_swap, bias_ref[h], sink_ref[h], valid)
                vm = jnp.where(lane_half == pos, v_same if same else v_swap, jnp.zeros((), BF16))
                acc = acc + _dot(p.astype(BF16), vm)
            out_ref[:, 512 + 128 * pair:512 + 128 * (pair + 1)] = acc.astype(BF16)

    return _call(
        body, (z, z, v_gain, w_spatial, b_spatial_t, sinks, rel_table, bucket), grid=(N_BLOCK,), name="mixer_fwd",
        in_specs=[pl.BlockSpec((CHUNK, D_IN), lambda n: (n, 0)),
                  pl.BlockSpec((CHUNK, 256), lambda n: (jnp.maximum(n - 1, 0), 6)),
                  pl.BlockSpec((1, 512), lambda n: (0, 0)),
                  pl.BlockSpec((N_GROUP, CHUNK, CHUNK), lambda n: (0, 0, 0)),
                  pl.BlockSpec((CHUNK, N_GROUP), lambda n: (0, 0)),
                  pl.BlockSpec(memory_space=pltpu.SMEM),
                  pl.BlockSpec(memory_space=pltpu.SMEM),
                  pl.BlockSpec((CHUNK, 2 * CHUNK), lambda n: (0, 0))],
        out_specs=[pl.BlockSpec((CHUNK, D), lambda n: (n, 0))],
        out_shape=[jax.ShapeDtypeStruct((S, D), BF16)],
        scratch_shapes=[pltpu.VMEM((N_HEAD, CHUNK, 2 * CHUNK), F32), pltpu.VMEM((N_GROUP, CHUNK, CHUNK), BF16)],
        compiler_params=_params(32), exchange=exchange)


def _out_proj(x, mix, w_out, gain2, exchange=None):
    tm = 256

    def body(x_ref, mix_ref, w_ref, g_ref, h1_ref, hn_ref, hnt_ref):
        h1 = x_ref[...] + _dot(mix_ref[...], w_ref[...])
        h1_ref[...] = h1
        hn = h1 * _rms_scale(h1) * g_ref[...]
        hn_ref[...] = hn.astype(BF16)
        hnt_ref[...] = hn.T.astype(BF16)

    return _call(
        body, (x, mix, w_out, gain2), grid=(S // tm,), name="out_proj",
        in_specs=[pl.BlockSpec((tm, D), lambda i: (i, 0)), pl.BlockSpec((tm, D), lambda i: (i, 0)),
                  pl.BlockSpec((D, D), lambda i: (0, 0)), pl.BlockSpec((1, D), lambda i: (0, 0))],
        out_specs=[pl.BlockSpec((tm, D), lambda i: (i, 0)), pl.BlockSpec((tm, D), lambda i: (i, 0)),
                   pl.BlockSpec((D, tm), lambda i: (0, i))],
        out_shape=[jax.ShapeDtypeStruct((S, D), F32), jax.ShapeDtypeStruct((S, D), BF16),
                   jax.ShapeDtypeStruct((D, S), BF16)],
        compiler_params=_params(32), exchange=exchange)


def _ffn_up(hn2, w_ff1, exchange=None):
    tm = 512
    nj = D_FF // 1024

    def body(hn_ref, w1_ref, r_ref, a_ref, at_ref):
        r = jnp.maximum(_dot(hn_ref[...], w1_ref[0]), 0.0)
        r_ref[...] = r.astype(BF16)
        a = r * r
        a_ref[...] = a.astype(BF16)
        at_ref[...] = a.T.astype(BF16)

    return _call(
        body, (hn2, w_ff1), grid=(nj, S // tm), name="ffn_up",
        in_specs=[pl.BlockSpec((tm, D), lambda j, i: (i, 0)), pl.BlockSpec((1, D, 1024), lambda j, i: (j, 0, 0))],
        out_specs=[pl.BlockSpec((tm, 1024), lambda j, i: (i, j)), pl.BlockSpec((tm, 1024), lambda j, i: (i, j)),
                   pl.BlockSpec((1024, tm), lambda j, i: (j, i))],
        out_shape=[jax.ShapeDtypeStruct((S, D_FF), BF16), jax.ShapeDtypeStruct((S, D_FF), BF16),
                   jax.ShapeDtypeStruct((D_FF, S), BF16)],
        compiler_params=_params(40, 2), exchange=exchange)


def _ffn_down(h1, a, w_ff2, exchange=None):
    tm = 1024
    nj = D_FF // 1024

    def body(h1_ref, a_ref, w2_ref, h2_ref, acc_ref):
        j = pl.program_id(1)
        part = _dot(a_ref[...], w2_ref[0])

        @pl.when(j == 0)
        def _():
            acc_ref[...] = part

        @pl.when(j > 0)
        def _():
            acc_ref[...] += part

        @pl.when(j == nj - 1)
        def _():
            h2_ref[...] = h1_ref[...] + acc_ref[...]

    return _call(
        body, (h1, a, w_ff2), grid=(S // tm, nj), name="ffn_down",
        in_specs=[pl.BlockSpec((tm, D), lambda i, j: (i, 0)), pl.BlockSpec((tm, 1024), lambda i, j: (i, j)),
                  pl.BlockSpec((1, 1024, D), lambda i, j: (j, 0, 0))],
        out_specs=[pl.BlockSpec((tm, D), lambda i, j: (i, 0))],
        out_shape=[jax.ShapeDtypeStruct((S, D), F32)],
        scratch_shapes=[pltpu.VMEM((tm, D), F32)],
        compiler_params=_params(48, 2), exchange=exchange)


def _tail(h2, p, target, w_gate, w_proj, final_gain):
    tm = 256
    steps = S // tm

    def body(h2_ref, p_ref, t_ref, wg_ref, wp_ref, gf_ref, dh2_ref, dwg_ref, dwp_ref, dgf_ref, loss_ref, dh2b_ref,
             dwp_acc):
        i = pl.program_id(0)
        h2 = h2_ref[...]
        h2b = h2.astype(BF16)
        pb = p_ref[...].astype(BF16)
        gate = jax.nn.sigmoid(_dot(h2b, wg_ref[...]))
        pp = jnp.concatenate([_dot(pb, wp_ref[j]) for j in range(N_CHIP)], axis=1)
        h3 = h2 + gate * pp
        r3 = _rms_scale(h3)
        xhat = h3 * r3
        gf = gf_ref[...]
        err = xhat * gf - t_ref[...]
        dy = err * (1.0 / D)
        dh3 = _rms_bwd(dy * gf, xhat, r3)
        dgp = (dh3 * pp * gate * (1.0 - gate)).astype(BF16)
        dpp = (dh3 * gate).astype(BF16)
        dh2 = dh3 + _dot_nt(dgp, wg_ref[...])
        dh2_ref[...] = dh2
        dh2b_ref[...] = dh2.astype(BF16)
        dwg = _dot_tn(h2b, dgp)
        dwp = _dot_tn(pb, dpp)
        dgf = jnp.sum(dy * xhat, axis=0, keepdims=True)
        sq = jnp.sum(jnp.sum(err * err, axis=1, keepdims=True), axis=0, keepdims=True)

        @pl.when(i == 0)
        def _():
            dwg_ref[...] = dwg
            dwp_acc[...] = dwp
            dgf_ref[...] = dgf
            loss_ref[...] = jnp.broadcast_to(sq, (8, 128))

        @pl.when(i > 0)
        def _():
            dwg_ref[...] += dwg
            dwp_acc[...] += dwp
            dgf_ref[...] += dgf
            loss_ref[...] += jnp.broadcast_to(sq, (8, 128))

        @pl.when(i == steps - 1)
        def _():
            for j in range(N_CHIP):
                dwp_ref[j] = dwp_acc[:, 256 * j:256 * (j + 1)]

    return _call(
        body, (h2, p, target, w_gate, w_proj, final_gain), grid=(steps,), name="tail",
        in_specs=[pl.BlockSpec((tm, D), lambda i: (i, 0)), pl.BlockSpec((tm, PLE), lambda i: (i, 0)),
                  pl.BlockSpec((tm, D), lambda i: (i, 0)), pl.BlockSpec((D, D), lambda i: (0, 0)),
                  pl.BlockSpec((N_CHIP, PLE, 256), lambda i: (0, 0, 0)), pl.BlockSpec((1, D), lambda i: (0, 0))],
        out_specs=[pl.BlockSpec((tm, D), lambda i: (i, 0)), pl.BlockSpec((D, D), lambda i: (0, 0)),
                   pl.BlockSpec((N_CHIP, PLE, 256), lambda i: (0, 0, 0)), pl.BlockSpec((1, D), lambda i: (0, 0)),
                   pl.BlockSpec((8, 128), lambda i: (0, 0)), pl.BlockSpec((tm, D), lambda i: (i, 0))],
        out_shape=[jax.ShapeDtypeStruct((S, D), F32), jax.ShapeDtypeStruct((D, D), F32),
                   jax.ShapeDtypeStruct((N_CHIP, PLE, 256), F32), jax.ShapeDtypeStruct((1, D), F32),
                   jax.ShapeDtypeStruct((8, 128), F32), jax.ShapeDtypeStruct((S, D), BF16)],
        scratch_shapes=[pltpu.VMEM((PLE, D), F32)],
        compiler_params=_params(48))[0]


def _ffn_bwd_weights(dh2b, hn2_t, r, a_t, w_ff2, exchange=None):
    slab = 256
    per_block = 1024 // slab

    def body(dh2_ref, hnt_ref, r_ref, at_ref, w2_ref, df_ref, dw1_ref, dw2_ref):
        dh2b = dh2_ref[...]
        da = _dot_nt(dh2b, w2_ref[0])
        df = (da * (2.0 * r_ref[...].astype(F32))).astype(BF16)
        df_ref[...] = df
        dw1_ref[0] = _dot(hnt_ref[...], df)
        dw2_ref[0] = _dot(at_ref[...], dh2b)

    return _call(
        body, (dh2b, hn2_t, r, a_t, w_ff2), grid=(D_FF // 1024, per_block), name="ffn_bwd_weights",
        in_specs=[pl.BlockSpec((S, D), lambda j, c: (0, 0)), pl.BlockSpec((D, S), lambda j, c: (0, 0)),
                  pl.BlockSpec((S, slab), lambda j, c: (0, per_block * j + c)),
                  pl.BlockSpec((slab, S), lambda j, c: (per_block * j + c, 0)),
                  pl.BlockSpec((1, slab, D), lambda j, c: (j, c, 0))],
        out_specs=[pl.BlockSpec((S, slab), lambda j, c: (0, per_block * j + c)),
                   pl.BlockSpec((1, D, slab), lambda j, c: (j, 0, c)),
                   pl.BlockSpec((1, slab, D), lambda j, c: (j, c, 0))],
        out_shape=[jax.ShapeDtypeStruct((S, D_FF), BF16), jax.ShapeDtypeStruct((D_FF // 1024, D, 1024), F32),
                   jax.ShapeDtypeStruct((D_FF // 1024, 1024, D), F32)],
        compiler_params=_params(48, 2), exchange=exchange)


def _ffn_bwd_input(df, w_ff1, dh2, h1, gain2, mix, w_out, exchange=None):
    tm = 512
    nj = D_FF // 1024
    steps = S // tm

    def body(df_ref, w1_ref, dh2_ref, h1_ref, g_ref, mix_ref, wo_ref, dh1_ref, dmix_ref, dwo_ref, dg_ref, acc_ref):
        i = pl.program_id(0)
        j = pl.program_id(1)
        part = _dot_nt(df_ref[...], w1_ref[0])

        @pl.when(j == 0)
        def _():
            acc_ref[...] = part

        @pl.when(j > 0)
        def _():
            acc_ref[...] += part

        @pl.when(j == nj - 1)
        def _():
            dhn = acc_ref[...]
            h1 = h1_ref[...]
            r2 = _rms_scale(h1)
            xhat = h1 * r2
            dh1 = dh2_ref[...] + _rms_bwd(dhn * g_ref[...], xhat, r2)
            dh1_ref[...] = dh1
            dh1b = dh1.astype(BF16)
            dmix_ref[...] = _dot_nt(dh1b, wo_ref[...])
            dwo = _dot_tn(mix_ref[...], dh1b)
            dg = jnp.sum(dhn * xhat, axis=0, keepdims=True)

            @pl.when(i == 0)
            def _():
                dwo_ref[...] = dwo
                dg_ref[...] = dg

            @pl.when(i > 0)
            def _():
                dwo_ref[...] += dwo
                dg_ref[...] += dg

    return _call(
        body, (df, w_ff1, dh2, h1, gain2, mix, w_out), grid=(steps, nj), name="ffn_bwd_input",
        in_specs=[pl.BlockSpec((tm, 1024), lambda i, j: (i, j)), pl.BlockSpec((1, D, 1024), lambda i, j: (j, 0, 0)),
                  pl.BlockSpec((tm, D), lambda i, j: (i, 0)), pl.BlockSpec((tm, D), lambda i, j: (i, 0)),
                  pl.BlockSpec((1, D), lambda i, j: (0, 0)), pl.BlockSpec((tm, D), lambda i, j: (i, 0)),
                  pl.BlockSpec((D, D), lambda i, j: (0, 0))],
        out_specs=[pl.BlockSpec((tm, D), lambda i, j: (i, 0)), pl.BlockSpec((tm, D), lambda i, j: (i, 0)),
                   pl.BlockSpec((D, D), lambda i, j: (0, 0)), pl.BlockSpec((1, D), lambda i, j: (0, 0))],
        out_shape=[jax.ShapeDtypeStruct((S, D), F32), jax.ShapeDtypeStruct((S, D), F32),
                   jax.ShapeDtypeStruct((D, D), F32), jax.ShapeDtypeStruct((1, D), F32)],
        scratch_shapes=[pltpu.VMEM((tm, D), F32)],
        compiler_params=_params(56, 2), exchange=exchange)


def _mixer_bwd(z, dmix, v_gain, w_spatial, b_spatial_t, sinks, rel_table, bucket, exchange=None):
    def body(z_ref, kvp_ref, dm_ref, gain_ref, ws_ref, bt_ref, sink_ref, table_ref, bucket_ref,
             dz_ref, dws_ref, db_ref, dgain_ref, dsink_ref, drel_ref,
             bias_ref, wt_ref, wtt_ref, dbias_ref, dsv_ref, carry_ref):
        n = pl.program_id(0)

        @pl.when(n == 0)
        def _():
            _fill_bias(bucket_ref, table_ref, bias_ref)
            _fill_tril(ws_ref, wt_ref, wtt_ref)
            dbias_ref[...] = jnp.zeros_like(dbias_ref)
            dsv_ref[...] = jnp.zeros_like(dsv_ref)
            dws_ref[...] = jnp.zeros_like(dws_ref)
            dgain_ref[...] = jnp.zeros_like(dgain_ref)
            dsink_ref[...] = jnp.zeros_like(dsink_ref)

        rows = pl.ds(pl.multiple_of(n * CHUNK, CHUNK), CHUNK)

        zuv = z_ref[:, :1024]
        cdf, t = _gelu_parts(zuv)
        guv = zuv * cdf
        dgelu = cdf + zuv * (0.5 * (1.0 - t * t)) * (GELU_C * (1.0 + 3.0 * 0.044715 * (zuv * zuv)))
        for g in range(N_GROUP):
            lo, hi = 128 * g, 128 * (g + 1)
            u = guv[:, lo:hi]
            vg = guv[:, 512 + lo:512 + hi]
            rr = _rms_scale(vg)
            vhat = vg * rr
            gain = gain_ref[:, lo:hi]
            vnb = (vhat * gain).astype(BF16)
            sv = _dot(wt_ref[g], vnb) + bt_ref[:, g:g + 1]
            da = dm_ref[:, lo:hi]
            dsv = da * u
            dsvb = dsv.astype(BF16)
            dsv_ref[g] += dsv
            dws_ref[g] += _dot_nt(dsvb, vnb)
            dvn = _dot(wtt_ref[g], dsvb)
            dgain_ref[:, lo:hi] += jnp.sum(dvn * vhat, axis=0, keepdims=True)
            dvg = _rms_bwd(dvn * gain, vhat, rr)
            dz_ref[rows, lo:hi] = (da * sv * dgelu[:, lo:hi]).astype(BF16)
            dz_ref[rows, 512 + lo:512 + hi] = (dvg * dgelu[:, 512 + lo:512 + hi]).astype(BF16)

        k_same, k_swap, v_same, v_swap = _kv_layouts(kvp_ref[...], z_ref[:, 1536:1792])
        valid = _band_mask(n)
        lane_half = lax.broadcasted_iota(jnp.int32, (1, 128), 1) // 64
        zero = jnp.zeros((2 * CHUNK, 128), F32)
        dk_same, dk_swap, dv_same, dv_swap = zero, zero, zero, zero
        for pair in range(N_HEAD // 2):
            cols = slice(1024 + 128 * pair, 1024 + 128 * (pair + 1))
            qq = z_ref[:, cols]
            do_pair = dm_ref[:, 512 + 128 * pair:512 + 128 * (pair + 1)]
            dq = jnp.zeros((CHUNK, 128), F32)
            for pos in range(2):
                h = 2 * pair + pos
                _, _, same = _head_place(h)
                on_half = lane_half == pos
                qm = jnp.where(on_half, qq, 0.0).astype(BF16)
                k_use = k_same if same else k_swap
                v_use = v_same if same else v_swap
                p, p_sink = _softmax_sink(qm, k_use, bias_ref[h], sink_ref[h], valid)
                dom = jnp.where(on_half, do_pair, 0.0).astype(BF16)
                dp = _dot_nt(dom, v_use)
                dsum = jnp.sum(p * dp, axis=-1, keepdims=True)
                ds = p * (dp - dsum)
                dbias_ref[h] += ds
                dsink_ref[h:h + 1, :] += jnp.broadcast_to(jnp.sum(-p_sink * dsum, axis=0, keepdims=True), (1, 128))
                dsb = ds.astype(BF16)
                dq = dq + jnp.where(on_half, _dot(dsb, k_use), 0.0)
                dk_h = _dot_tn(dsb, qm)
                dv_h = _dot_tn(p.astype(BF16), dom)
                if same:
                    dk_same, dv_same = dk_same + dk_h, dv_same + dv_h
                else:
                    dk_swap, dv_swap = dk_swap + dk_h, dv_swap + dv_h
            dz_ref[rows, cols] = (dq * QK_SCALE).astype(BF16)
        dk = (dk_same + pltpu.roll(dk_swap, 64, axis=1)) * QK_SCALE
        dv = dv_same + pltpu.roll(dv_swap, 64, axis=1)
        dkv = jnp.concatenate([dk, dv], axis=1)

        @pl.when(n > 0)
        def _():
            prev_rows = pl.ds(pl.multiple_of((n - 1) * CHUNK, CHUNK), CHUNK)
            dz_ref[prev_rows, 1536:1792] = (carry_ref[...] + dkv[:CHUNK]).astype(BF16)

        carry_ref[...] = dkv[CHUNK:]

        @pl.when(n == N_BLOCK - 1)
        def _():
            dz_ref[rows, 1536:1792] = dkv[CHUNK:].astype(BF16)
            r = lax.broadcasted_iota(jnp.int32, (CHUNK, CHUNK), 0)
            c = lax.broadcasted_iota(jnp.int32, (CHUNK, CHUNK), 1)
            for g in range(N_GROUP):
                dws_ref[g] = jnp.where(c <= r, dws_ref[g], 0.0)
                db_ref[g] = jnp.sum(dsv_ref[g], axis=1, keepdims=True)
            bucket = bucket_ref[...]
            for h in range(N_HEAD):
                dbh = dbias_ref[h]
                per_bucket = [jnp.sum(jnp.where(bucket == b, dbh, 0.0), axis=0, keepdims=True) for b in range(N_BUCKET)]
                drel_ref[h] = jnp.sum(jnp.concatenate(per_bucket, axis=0), axis=1, keepdims=True)

    return _call(
        body, (z, z, dmix, v_gain, w_spatial, b_spatial_t, sinks, rel_table, bucket), grid=(N_BLOCK,), name="mixer_bwd",
        in_specs=[pl.BlockSpec((CHUNK, D_IN), lambda n: (n, 0)),
                  pl.BlockSpec((CHUNK, 256), lambda n: (jnp.maximum(n - 1, 0), 6)),
                  pl.BlockSpec((CHUNK, D), lambda n: (n, 0)),
                  pl.BlockSpec((1, 512), lambda n: (0, 0)),
                  pl.BlockSpec((N_GROUP, CHUNK, CHUNK), lambda n: (0, 0, 0)),
                  pl.BlockSpec((CHUNK, N_GROUP), lambda n: (0, 0)),
                  pl.BlockSpec(memory_space=pltpu.SMEM),
                  pl.BlockSpec(memory_space=pltpu.SMEM),
                  pl.BlockSpec((CHUNK, 2 * CHUNK), lambda n: (0, 0))],
        out_specs=[pl.BlockSpec((S, D_IN), lambda n: (0, 0)),
                   pl.BlockSpec((N_GROUP, CHUNK, CHUNK), lambda n: (0, 0, 0)),
                   pl.BlockSpec((N_GROUP, CHUNK, 1), lambda n: (0, 0, 0)),
                   pl.BlockSpec((1, 512), lambda n: (0, 0)),
                   pl.BlockSpec((N_HEAD, 128), lambda n: (0, 0)),
                   pl.BlockSpec((N_HEAD, N_BUCKET, 1), lambda n: (0, 0, 0))],
        out_shape=[jax.ShapeDtypeStruct((S, D_IN), BF16), jax.ShapeDtypeStruct((N_GROUP, CHUNK, CHUNK), F32),
                   jax.ShapeDtypeStruct((N_GROUP, CHUNK, 1), F32), jax.ShapeDtypeStruct((1, 512), F32),
                   jax.ShapeDtypeStruct((N_HEAD, 128), F32), jax.ShapeDtypeStruct((N_HEAD, N_BUCKET, 1), F32)],
        scratch_shapes=[pltpu.VMEM((N_HEAD, CHUNK, 2 * CHUNK), F32), pltpu.VMEM((N_GROUP, CHUNK, CHUNK), BF16),
                        pltpu.VMEM((N_GROUP, CHUNK, CHUNK), BF16), pltpu.VMEM((N_HEAD, CHUNK, 2 * CHUNK), F32),
                        pltpu.VMEM((N_GROUP, CHUNK, CHUNK), F32), pltpu.VMEM((CHUNK, 256), F32)],
        compiler_params=_params(48), exchange=exchange)


def _in_bwd(dz, hn1, w_in_t, x, dh1, gain1, exchange=None):
    tm = 512

    def body(dz_ref, hn_ref, w_ref, x_ref, dh1_ref, g_ref, dx_ref, dw_ref, dg_ref):
        i = pl.program_id(0)
        dzb = dz_ref[...]
        dhn = _dot(dzb, w_ref[...])
        xv = x_ref[...]
        r1 = _rms_scale(xv)
        xhat = xv * r1
        dx_ref[...] = dh1_ref[...] + _rms_bwd(dhn * g_ref[...], xhat, r1)
        dw = _dot_tn(dzb, hn_ref[...])
        dg = jnp.sum(dhn * xhat, axis=0, keepdims=True)

        @pl.when(i == 0)
        def _():
            dw_ref[...] = dw
            dg_ref[...] = dg

        @pl.when(i > 0)
        def _():
            dw_ref[...] += dw
            dg_ref[...] += dg

    return _call(
        body, (dz, hn1, w_in_t, x, dh1, gain1), grid=(S // tm,), name="in_bwd",
        in_specs=[pl.BlockSpec((tm, D_IN), lambda i: (i, 0)), pl.BlockSpec((tm, D), lambda i: (i, 0)),
                  pl.BlockSpec((D_IN, D), lambda i: (0, 0)), pl.BlockSpec((tm, D), lambda i: (i, 0)),
                  pl.BlockSpec((tm, D), lambda i: (i, 0)), pl.BlockSpec((1, D), lambda i: (0, 0))],
        out_specs=[pl.BlockSpec((tm, D), lambda i: (i, 0)), pl.BlockSpec((D_IN, D), lambda i: (0, 0)),
                   pl.BlockSpec((1, D), lambda i: (0, 0))],
        out_shape=[jax.ShapeDtypeStruct((S, D), F32), jax.ShapeDtypeStruct((D_IN, D), F32),
                   jax.ShapeDtypeStruct((1, D), F32)],
        compiler_params=_params(56), exchange=exchange)


def _rel_bucket():
    a = jnp.arange(CHUNK)[:, None]
    j = jnp.arange(2 * CHUNK)[None, :]
    n = jnp.maximum(CHUNK + a - j, 0)
    max_exact = N_BUCKET // 2
    nf = jnp.maximum(n, 1).astype(jnp.float32)
    large = max_exact + (jnp.log(nf / max_exact) / math.log(CHUNK / max_exact) * (N_BUCKET - max_exact)).astype(jnp.int32)
    large = jnp.minimum(large, N_BUCKET - 1)
    return jnp.where(n < max_exact, n, large).astype(jnp.int32)


def _step(x, p, target, small, bufs, place):
    bucket = _rel_bucket()
    sinks = small["attn_sinks"].reshape(N_HEAD)
    b_t = jnp.transpose(small["b_spatial"].reshape(N_GROUP, CHUNK))
    ws = small["w_spatial"].reshape(N_GROUP, CHUNK, CHUNK)
    gain1, gain2 = small["norm1_gain"], small["norm2_gain"]
    v_gain = small["gmlp_v_gain"]
    final_gain = small["final_gain"].reshape(1, D)
    table = small["rel_bias_table"]
    bufs = dict(bufs)

    def gather(*names):
        return _Gather([bufs[n] for n in names])

    def took(names, got):
        bufs.update(zip(names, got))

    took(["w_in"], _gather_weights([bufs["w_in"]]))
    w_in_t = _whole(bufs["w_in"]).reshape(D_IN, D)
    (z, hn1), got = _in_proj(x, gain1, w_in_t, gather("w_out"))
    took(["w_out"], got)
    (mix,), got = _mixer_fwd(z, v_gain, ws, b_t, sinks, table, bucket, gather("w_ff1"))
    took(["w_ff1"], got)
    w_out = _whole(bufs["w_out"]).reshape(D, D)
    (h1, hn2, hn2_t), _ = _out_proj(x, mix, w_out, gain2)
    w_ff1 = _whole(bufs["w_ff1"])
    (r, a, a_t), got = _ffn_up(hn2, w_ff1, gather("w_ff2"))
    took(["w_ff2"], got)
    w_ff2 = _whole(bufs["w_ff2"])
    (h2,), got = _ffn_down(h1, a, w_ff2, gather("w_ple_gate", "w_ple_proj"))
    took(["w_ple_gate", "w_ple_proj"], got)
    dh2, d_gate, d_proj, d_final, sq, dh2b = _tail(h2, p, target, _whole(bufs["w_ple_gate"]).reshape(D, D),
                                                   _whole(bufs["w_ple_proj"]), final_gain)

    def pair_sums(halves, from_sibling):
        sums, landing = zip(*[_pair_sum(g, o, place) for g, o in zip(halves, from_sibling)])
        return list(sums), list(landing)

    landed = {}
    halves = [_halves(d_gate.reshape(N_CHIP, 256, D)), _halves(d_proj)]
    ex = _ChipExchange(*pair_sums(halves, _sibling_exchange(halves)))
    (df, d_ff1, d_ff2), got = _ffn_bwd_weights(dh2b, hn2_t, r, a_t, w_ff2, ex)
    landed.update(zip(["w_ple_gate", "w_ple_proj"], got))
    halves = [_halves(d_ff1), _halves(d_ff2)]
    (dh1, dmix, d_out, d_gain2), got = _ffn_bwd_input(df, w_ff1, dh2, h1, gain2, mix, w_out, _SiblingExchange(halves))
    ex = _ChipExchange(*pair_sums(halves, got))
    (dz, d_ws, d_b, d_vgain, d_sink, d_rel), got = _mixer_bwd(z, dmix, v_gain, ws, b_t, sinks, table, bucket, ex)
    landed.update(zip(["w_ff1", "w_ff2"], got))
    small_grads = {
        "gmlp_v_gain": d_vgain, "w_spatial": d_ws.reshape(1, N_GROUP, CHUNK, CHUNK),
        "b_spatial": d_b.reshape(1, N_GROUP, CHUNK), "attn_sinks": d_sink[:, 0].reshape(1, N_HEAD),
        "rel_bias_table": jnp.transpose(d_rel.reshape(N_HEAD, N_BUCKET)), "norm2_gain": d_gain2,
        "final_gain": d_final.reshape(D), "squared_error": sq,
    }
    halves = [_halves(d_out.reshape(N_CHIP, 256, D))]
    ex = _Both(_ChipExchange(*pair_sums(halves, _sibling_exchange(halves))),
               _GatherAll(_pack_small([small_grads[n] for n in SMALL_EARLY])))
    (dx, d_in_t, d_gain1), got = _in_bwd(dz, hn1, w_in_t, x, dh1, gain1, ex)
    landed["w_out"], small_gathered = got
    halves = [_halves(d_in_t.reshape(N_CHIP, 448, D))]
    last = _Both(_ChipExchange(*pair_sums(halves, _sibling_exchange(halves))), _GatherAll(_pack_small([d_gain1])))
    return dx, landed, last, small_gathered


HBM_SPEC = pl.BlockSpec(memory_space=pltpu.HBM)
VMEM_SPEC = pl.BlockSpec(memory_space=pltpu.VMEM)


def _mesh_place():
    x, y, c = lax.axis_index("x"), lax.axis_index("y"), lax.axis_index("c")
    others = [(1 - x, y), (x, 1 - y), (1 - x, 1 - y)]
    return x, y, c, others


def _remote(src, dst, send_sem, recv_sem, device):
    return pltpu.make_async_remote_copy(src_ref=src, dst_ref=dst, send_sem=send_sem, recv_sem=recv_sem,
                                        device_id=device, device_id_type=MESH)


def _hbm_like(a, shape=None, dtype=None):
    return pltpu.HBM(a.shape if shape is None else shape, a.dtype if dtype is None else dtype)


def _gather_start(bufs, send_sems, recv_sems):
    x, y, c, others = _mesh_place()
    me = 2 * x + y
    for w, buf in enumerate(bufs):
        for k, (ox, oy) in enumerate(others):
            mine = buf.at[me, c]
            _remote(mine, mine, send_sems.at[w, k], recv_sems.at[w, k], (ox, oy, c)).start()


def _gather_finish(bufs, send_sems, recv_sems):
    x, y, c, others = _mesh_place()
    me = 2 * x + y
    sibling = (x, y, 1 - c)
    idx = [2 * ox + oy for ox, oy in others]
    for w, buf in enumerate(bufs):
        for k in range(3):
            landed = buf.at[idx[k], c]
            _remote(landed, landed, send_sems.at[w, k], recv_sems.at[w, k], sibling).wait_recv()
            _remote(landed, landed, send_sems.at[w, 3 + k], recv_sems.at[w, 3 + k], sibling).start()
    for w, buf in enumerate(bufs):
        for k in range(3):
            landed = buf.at[idx[k], 1 - c]
            _remote(landed, landed, send_sems.at[w, 3 + k], recv_sems.at[w, 3 + k], sibling).wait_recv()
    for w, buf in enumerate(bufs):
        for k in range(3):
            mine, passed = buf.at[me, c], buf.at[idx[k], c]
            _remote(mine, mine, send_sems.at[w, k], recv_sems.at[w, k], sibling).wait_send()
            _remote(passed, passed, send_sems.at[w, 3 + k], recv_sems.at[w, 3 + k], sibling).wait_send()


def _gather_sems(n):
    return [pltpu.SemaphoreType.DMA((n, 6)), pltpu.SemaphoreType.DMA((n, 6))]


def _gather_weights(bufs):
    n = len(bufs)

    def body(*refs):
        outs = refs[n:2 * n]
        send_sems, recv_sems = refs[2 * n:]
        _gather_start(outs, send_sems, recv_sems)
        _gather_finish(outs, send_sems, recv_sems)

    return pl.pallas_call(
        body, name="gather_weights",
        in_specs=[HBM_SPEC] * n, out_specs=[HBM_SPEC] * n,
        out_shape=[_hbm_like(b) for b in bufs],
        input_output_aliases={w: w for w in range(n)},
        scratch_shapes=_gather_sems(n),
    )(*bufs)


def _sibling_copies(grads, landing, send_sems, recv_sems):
    x, y, c, _ = _mesh_place()
    return [_remote(grads[w].at[j, 1 - c], landing[w].at[j], send_sems.at[w, j], recv_sems.at[w, j], (x, y, 1 - c))
            for w in range(len(grads)) for j in range(N_CHIP)]


def _sibling_exchange_start(grads, landing, send_sems, recv_sems):
    for cp in _sibling_copies(grads, landing, send_sems, recv_sems):
        cp.start()


def _sibling_exchange_finish(grads, landing, send_sems, recv_sems):
    copies = _sibling_copies(grads, landing, send_sems, recv_sems)
    for cp in copies:
        cp.wait_recv()
    for cp in copies:
        cp.wait_send()


def _sibling_exchange_sems(n):
    return [pltpu.SemaphoreType.DMA((n, N_CHIP)), pltpu.SemaphoreType.DMA((n, N_CHIP))]


def _sibling_exchange(grads):
    n = len(grads)

    def body(*refs):
        ins, outs = refs[:n], refs[n:2 * n]
        _sibling_exchange_start(ins, outs, *refs[2 * n:])
        _sibling_exchange_finish(ins, outs, *refs[2 * n:])

    return pl.pallas_call(
        body, name="sibling_exchange",
        in_specs=[HBM_SPEC] * n, out_specs=[HBM_SPEC] * n,
        out_shape=[_hbm_like(g, (N_CHIP,) + g.shape[2:]) for g in grads],
        scratch_shapes=_sibling_exchange_sems(n),
    )(*[_in_hbm(g) for g in grads])


def _chip_exchange_start(sums, landing, send_sems, recv_sems):
    x, y, c, others = _mesh_place()
    me = 2 * x + y
    for w in range(len(sums)):
        for k, (ox, oy) in enumerate(others):
            _remote(sums[w].at[2 * ox + oy], landing[w].at[me], send_sems.at[w, k], recv_sems.at[w, k],
                    (ox, oy, c)).start()


def _chip_exchange_finish(sums, landing, send_sems, recv_sems):
    x, y, c, others = _mesh_place()
    for w in range(len(sums)):
        for k, (ox, oy) in enumerate(others):
            piece = landing[w].at[2 * ox + oy]
            _remote(piece, piece, send_sems.at[w, k], recv_sems.at[w, k], (x, y, c)).wait_recv()
    for w in range(len(sums)):
        for k, (ox, oy) in enumerate(others):
            piece = sums[w].at[2 * ox + oy]
            _remote(piece, piece, send_sems.at[w, k], recv_sems.at[w, k], (x, y, c)).wait_send()


def _chip_exchange_sems(n):
    return [pltpu.SemaphoreType.DMA((n, 3)), pltpu.SemaphoreType.DMA((n, 3))]


def _chip_exchange(sums, landing):
    n = len(sums)

    def body(*refs):
        ins, outs = refs[:n], refs[2 * n:3 * n]
        send_sems, recv_sems = refs[3 * n:]
        _chip_exchange_start(ins, outs, send_sems, recv_sems)
        _chip_exchange_finish(ins, outs, send_sems, recv_sems)

    return pl.pallas_call(
        body, name="chip_exchange",
        in_specs=[HBM_SPEC] * (2 * n), out_specs=[HBM_SPEC] * n,
        out_shape=[_hbm_like(b) for b in landing],
        input_output_aliases={n + w: w for w in range(n)},
        scratch_shapes=_chip_exchange_sems(n),
    )(*sums, *landing)


def _sibling_allgather(bufs):
    n = len(bufs)

    def body(*refs):
        outs = refs[n:2 * n]
        send_sems, recv_sems = refs[2 * n:]
        x, y, c, _ = _mesh_place()
        sibling = (x, y, 1 - c)
        sends = [_remote(outs[w].at[c], outs[w].at[c], send_sems.at[w], recv_sems.at[w], sibling) for w in range(n)]
        for cp in sends:
            cp.start()
        for w in range(n):
            landed = outs[w].at[1 - c]
            _remote(landed, landed, send_sems.at[w], recv_sems.at[w], sibling).wait_recv()
        for cp in sends:
            cp.wait_send()

    return pl.pallas_call(
        body, name="sibling_allgather",
        in_specs=[HBM_SPEC] * n, out_specs=[HBM_SPEC] * n,
        out_shape=[_hbm_like(b) for b in bufs],
        input_output_aliases={w: w for w in range(n)},
        scratch_shapes=[pltpu.SemaphoreType.DMA((n,)), pltpu.SemaphoreType.DMA((n,))],
    )(*bufs)


def _pair_sum(grad, other, place):
    _, _, h, cols = grad.shape
    tr = _row_tile(h)

    def body(place_ref, g_ref, o_ref, sums_ref, own_ref):
        s = (g_ref[0, 0] + o_ref[0]).astype(BF16)
        sums_ref[0] = s

        @pl.when(pl.program_id(1) == place_ref[0])
        def _():
            own_ref[0] = s

    return pl.pallas_call(
        body, name="pair_sum",
        grid_spec=pltpu.PrefetchScalarGridSpec(
            num_scalar_prefetch=1, grid=(h // tr, N_CHIP),
            in_specs=[pl.BlockSpec((1, 1, tr, cols), lambda r, j, place_ref: (j, place_ref[1], r, 0)),
                      pl.BlockSpec((1, tr, cols), lambda r, j, place_ref: (j, r, 0))],
            out_specs=[pl.BlockSpec((1, tr, cols), lambda r, j, place_ref: (j, r, 0)),
                       pl.BlockSpec((1, tr, cols), lambda r, j, place_ref: (place_ref[0], r, 0))]),
        out_shape=[pltpu.HBM((N_CHIP, h, cols), BF16)] * 2,
        compiler_params=_params(16, 2),
    )(place, _in_hbm(grad), _in_hbm(other))


def _chip_sum(parts, place):
    _, h, cols = parts.shape
    tr = _row_tile(h)

    def body(place_ref, p_ref, out_ref):
        out_ref[0] = ((p_ref[0].astype(F32) + p_ref[1].astype(F32)) + p_ref[2].astype(F32)) + p_ref[3].astype(F32)

    return pl.pallas_call(
        body, name="chip_sum",
        grid_spec=pltpu.PrefetchScalarGridSpec(
            num_scalar_prefetch=1, grid=(h // tr,),
            in_specs=[pl.BlockSpec((N_CHIP, tr, cols), lambda r, place_ref: (0, r, 0))],
            out_specs=pl.BlockSpec((1, tr, cols), lambda r, place_ref: (place_ref[1], r, 0))),
        out_shape=pltpu.HBM((2, h, cols), F32),
        compiler_params=_params(16),
    )(place, _in_hbm(parts))


def _adamw_math(w, g, m, v):
    m = ADAM_B1 * m + (1.0 - ADAM_B1) * g
    v = ADAM_B2 * v + (1.0 - ADAM_B2) * (g * g)
    m_hat = m / (1.0 - ADAM_B1 ** ADAM_STEP)
    v_hat = v / (1.0 - ADAM_B2 ** ADAM_STEP)
    delta = -ADAM_LR * (m_hat / (jnp.sqrt(v_hat) + ADAM_EPS) + ADAM_WD * w)
    return delta, m, v


def _adamw(w, g, m, v, exchange=None):
    rows, cols = w.shape
    tr = _row_tile(rows)

    def body(w_ref, g_ref, m_ref, v_ref, d_ref, nm_ref, nv_ref, g_out_ref):
        g = g_ref[...]
        d_ref[...], nm_ref[...], nv_ref[...] = _adamw_math(w_ref[...], g, m_ref[...], v_ref[...])
        g_out_ref[...] = g

    spec = pl.BlockSpec((tr, cols), lambda r: (r, 0))
    return _call(
        body, (w, g, m, v), grid=(rows // tr,), name="adamw",
        in_specs=[spec] * 4, out_specs=[spec] * 4,
        out_shape=[jax.ShapeDtypeStruct((rows, cols), F32)] * 4,
        compiler_params=_params(32), exchange=exchange)


SMALL_NAMES = ("norm1_gain", "gmlp_v_gain", "w_spatial", "b_spatial", "attn_sinks", "rel_bias_table", "norm2_gain",
               "final_gain")
SMALL_EARLY = SMALL_NAMES[1:] + ("squared_error",)
PACK_TILE = 8 * 128


def _pack_small(arrays):
    parts = []
    for a in arrays:
        flat = a.reshape(-1)
        rows = -(-flat.shape[0] // PACK_TILE) * 8
        parts.append(jnp.pad(flat, (0, rows * 128 - flat.shape[0])).reshape(rows, 128))
    return jnp.concatenate(parts, axis=0)


def _unpack_small(packed, like):
    out, row = [], 0
    for a in like:
        size = math.prod(a.shape)
        rows = -(-size // PACK_TILE) * 8
        out.append(packed[row:row + rows].reshape(-1)[:size].reshape(a.shape))
        row += rows
    return out


def _small_update(gathered, w, m, v):
    rows = gathered.shape[1]

    def body(g_ref, w_ref, m_ref, v_ref, tot_ref, d_ref, nm_ref, nv_ref):
        total = g_ref[0]
        for dev in range(1, 8):
            total = total + g_ref[dev]
        tot_ref[...] = total
        d_ref[...], nm_ref[...], nv_ref[...] = _adamw_math(w_ref[...], total, m_ref[...], v_ref[...])

    return pl.pallas_call(
        body, name="small_update",
        in_specs=[VMEM_SPEC] * 4, out_specs=[VMEM_SPEC] * 4,
        out_shape=[jax.ShapeDtypeStruct((rows, 128), F32)] * 4,
        compiler_params=pltpu.CompilerParams(vmem_limit_bytes=24 * MIB),
    )(gathered, w, m, v)


def _halves(a):
    return a.reshape(a.shape[:-2] + (2, a.shape[-2] // 2, a.shape[-1]))


def _whole(a):
    return a.reshape(a.shape[:-3] + (2 * a.shape[-2], a.shape[-1]))


def kernel(x, p, norm1_gain, w_in, gmlp_v_gain, w_spatial, b_spatial, attn_sinks, rel_bias_table, w_out, norm2_gain, w_ff1, w_ff2, w_ple_proj, w_ple_gate, final_gain, loss_target, m_norm1_gain, m_w_in, m_gmlp_v_gain, m_w_spatial, m_b_spatial, m_attn_sinks, m_rel_bias_table, m_w_out, m_norm2_gain, m_w_ff1, m_w_ff2, m_w_ple_proj, m_w_ple_gate, m_final_gain, v_norm1_gain, v_w_in, v_gmlp_v_gain, v_w_spatial, v_b_spatial, v_attn_sinks, v_rel_bias_table, v_w_out, v_norm2_gain, v_w_ff1, v_w_ff2, v_w_ple_proj, v_w_ple_gate, v_final_gain):
    given = dict(locals())
    small = {n: given[n] for n in SMALL_NAMES}
    chip = 2 * lax.axis_index("x") + lax.axis_index("y")
    place = jnp.stack([chip, lax.axis_index("c")]).astype(jnp.int32)

    big_names = ("w_in", "w_out", "w_ff1", "w_ff2", "w_ple_proj", "w_ple_gate")
    shards = {n: given[n][0] for n in big_names}
    travel = dict(shards, w_in=jnp.transpose(shards["w_in"]))
    bufs = {n: _cast_shard(travel[n], place[:1]) for n in big_names}
    dx, landed, last, small_gathered = _step(x[0], p[0, 0], loss_target[0], small, bufs, place)

    out_grad, out_delta, out_m, out_v = {}, {}, {}, {}

    def update(n, g, exchange=None):
        to = jnp.transpose if n == "w_in" else (lambda a: a)
        (delta, new_m, new_v, g_out), got = _adamw(to(shards[n]), g, to(given["m_" + n][0]), to(given["v_" + n][0]),
                                                   exchange)
        out_grad[n], out_delta[n], out_m[n], out_v[n] = [to(a)[None] for a in (g_out, delta, new_m, new_v)]
        return got

    early = [n for n in big_names if n != "w_in"]
    reduced = dict(zip(early, _sibling_allgather([_chip_sum(landed[n], place) for n in early])))
    landed_in, gain1_gathered = update("w_ff1", _whole(reduced["w_ff1"]), last)
    for n in early:
        if n != "w_ff1":
            update(n, _whole(reduced[n]))
    reduced_in, = _sibling_allgather([_chip_sum(landed_in, place)])
    update("w_in", _whole(reduced_in))

    given["squared_error"] = given["m_squared_error"] = given["v_squared_error"] = jnp.zeros((8, 128), F32)
    for names, gathered in ((SMALL_EARLY, small_gathered), (SMALL_NAMES[:1], gain1_gathered)):
        like = [given[n] for n in names]
        packed = _small_update(gathered, *[_pack_small([given[pre + n] for n in names]) for pre in ("", "m_", "v_")])
        for res, out in zip(packed, (out_grad, out_delta, out_m, out_v)):
            out.update(zip(names, _unpack_small(res, like)))
    loss = 0.5 * out_grad["squared_error"][0, 0] / D

    order = ("norm1_gain", "w_in", "gmlp_v_gain", "w_spatial", "b_spatial", "attn_sinks", "rel_bias_table", "w_out",
             "norm2_gain", "w_ff1", "w_ff2", "w_ple_proj", "w_ple_gate", "final_gain")
    return (loss, dx[None], *[out_grad[n] for n in order], *[out_delta[n] for n in order],
            *[out_m[n] for n in order], *[out_v[n] for n in order])
```

```python
import functools
import math

import jax
import jax.numpy as jnp
from jax import lax
from jax.experimental import pallas as pl
from jax.experimental.pallas import tpu as pltpu

S = 2048
D = 1024
D_IN = 1792
D_FF = 4096
PLE = 256
N_CHIP = 4
N_GROUP = 4
CHUNK = 128
N_HEAD = 8
N_BLOCK = S // CHUNK
N_BUCKET = 32
EPS = 1e-6
NEG_INF = -1e30
QK_SCALE = 0.125
GELU_C = math.sqrt(2.0 / math.pi)

ADAM_LR = 0.001
ADAM_B1 = 0.9
ADAM_B2 = 0.999
ADAM_EPS = 1e-08
ADAM_WD = 0.01
ADAM_STEP = 10

F32 = jnp.float32
BF16 = jnp.bfloat16
MIB = 1024 * 1024
MESH = pl.DeviceIdType.MESH

NT = (((1,), (1,)), ((), ()))
TN = (((0,), (0,)), ((), ()))


def _dot(a, b):
    return jnp.dot(a, b, preferred_element_type=F32)


def _dot_nt(a, b):
    return lax.dot_general(a, b, NT, preferred_element_type=F32)


def _dot_tn(a, b):
    return lax.dot_general(a, b, TN, preferred_element_type=F32)


def _params(vmem_mib, n_axes=1):
    return pltpu.CompilerParams(dimension_semantics=("arbitrary",) * n_axes, vmem_limit_bytes=vmem_mib * MIB)


def _rms_scale(v):
    return lax.rsqrt(jnp.mean(v * v, axis=-1, keepdims=True) + EPS)


def _rms_bwd(dy_gain, xhat, r):
    return r * (dy_gain - xhat * jnp.mean(dy_gain * xhat, axis=-1, keepdims=True))


class _Gather:
    def __init__(self, bufs):
        self.operands = list(bufs)
        self.n_out = len(self.operands)
        self.out_shape = [_hbm_like(b) for b in bufs]
        self.aliases = {w: w for w in range(self.n_out)}
        self.sems = _gather_sems(self.n_out)

    def start(self, ins, outs, sems):
        _gather_start(outs, *sems)

    def finish(self, ins, outs, sems):
        _gather_finish(outs, *sems)


class _ChipExchange:
    def __init__(self, sums, landing):
        self.n_out = len(landing)
        self.operands = list(sums) + list(landing)
        self.out_shape = [_hbm_like(b) for b in landing]
        self.aliases = {self.n_out + w: w for w in range(self.n_out)}
        self.sems = _chip_exchange_sems(self.n_out)

    def start(self, ins, outs, sems):
        _chip_exchange_start(ins[:self.n_out], outs, *sems)

    def finish(self, ins, outs, sems):
        _chip_exchange_finish(ins[:self.n_out], outs, *sems)


class _GatherAll:
    def __init__(self, packed):
        self.operands = [packed]
        self.n_out = 1
        self.out_shape = [_hbm_like(packed, (8,) + packed.shape)]
        self.aliases = {}
        self.sems = [pltpu.SemaphoreType.DMA((8,)), pltpu.SemaphoreType.DMA((8,))]

    def _copies(self, ins, outs, sems):
        x, y, c, _ = _mesh_place()
        me = 4 * x + 2 * y + c
        send_sems, recv_sems = sems
        copies = []
        for k in range(1, 8):
            peer = (1 - x if k // 4 else x, 1 - y if (k // 2) % 2 else y, 1 - c if k % 2 else c)
            src = 4 * peer[0] + 2 * peer[1] + peer[2]
            copies.append((_remote(ins[0], outs[0].at[me], send_sems.at[k], recv_sems.at[k], peer), outs[0].at[src]))
        own = pltpu.make_async_copy(ins[0], outs[0].at[me], send_sems.at[0])
        return own, copies

    def start(self, ins, outs, sems):
        own, copies = self._copies(ins, outs, sems)
        own.start()
        for cp, _ in copies:
            cp.start()

    def finish(self, ins, outs, sems):
        own, copies = self._copies(ins, outs, sems)
        x, y, c, _ = _mesh_place()
        for k, (cp, landed) in enumerate(copies):
            _remote(landed, landed, sems[0].at[k + 1], sems[1].at[k + 1], (x, y, c)).wait_recv()
        for cp, _ in copies:
            cp.wait_send()
        own.wait()


class _Both:
    def __init__(self, a, b):
        self.a, self.b = a, b
        self.operands = a.operands + b.operands
        self.n_out = a.n_out + b.n_out
        self.out_shape = a.out_shape + b.out_shape
        self.aliases = dict(a.aliases)
        self.aliases.update({len(a.operands) + i: a.n_out + o for i, o in b.aliases.items()})
        self.sems = a.sems + b.sems

    def _split(self, ins, outs, sems):
        ka, na, sa = len(self.a.operands), self.a.n_out, len(self.a.sems)
        return (ins[:ka], outs[:na], sems[:sa]), (ins[ka:], outs[na:], sems[sa:])

    def start(self, ins, outs, sems):
        for ex, args in zip((self.a, self.b), self._split(ins, outs, sems)):
            ex.start(*args)

    def finish(self, ins, outs, sems):
        for ex, args in zip((self.a, self.b), self._split(ins, outs, sems)):
            ex.finish(*args)


class _SiblingExchange:
    def __init__(self, grads):
        self.operands = list(grads)
        self.n_out = len(self.operands)
        self.out_shape = [_hbm_like(g, (N_CHIP,) + g.shape[2:]) for g in grads]
        self.aliases = {}
        self.sems = _sibling_exchange_sems(self.n_out)

    def start(self, ins, outs, sems):
        _sibling_exchange_start(ins, outs, *sems)

    def finish(self, ins, outs, sems):
        _sibling_exchange_finish(ins, outs, *sems)


def _call(body, operands, *, grid, in_specs, out_specs, out_shape, name, compiler_params, scratch_shapes=(),
          exchange=None):
    operands = [o if getattr(spec, "memory_space", None) == pltpu.SMEM else _in_hbm(o)
                for o, spec in zip(operands, in_specs)]
    out_shape = [pltpu.HBM(s.shape, s.dtype) for s in out_shape]
    if exchange is None:
        res = pl.pallas_call(body, grid=grid, in_specs=in_specs, out_specs=out_specs, out_shape=out_shape, name=name,
                             scratch_shapes=list(scratch_shapes), compiler_params=compiler_params)(*operands)
        return list(res), []
    n_in, n_out, n_scr = len(in_specs), len(out_specs), len(scratch_shapes)
    k_in, k_out = len(exchange.operands), exchange.n_out

    def fused(*refs):
        ins, refs = refs[:n_in], refs[n_in:]
        ex_ins, refs = refs[:k_in], refs[k_in:]
        outs, refs = refs[:n_out], refs[n_out:]
        ex_outs, refs = refs[:k_out], refs[k_out:]
        scratch, sems = refs[:n_scr], refs[n_scr:]
        ids = [pl.program_id(a) for a in range(len(grid))]
        first = functools.reduce(jnp.logical_and, [i == 0 for i in ids])
        last = functools.reduce(jnp.logical_and, [i == g - 1 for i, g in zip(ids, grid)])

        @pl.when(first)
        def _():
            exchange.start(ex_ins, ex_outs, sems)

        body(*ins, *outs, *scratch)

        @pl.when(last)
        def _():
            exchange.finish(ex_ins, ex_outs, sems)

    res = pl.pallas_call(
        fused, grid=grid, name=name,
        in_specs=list(in_specs) + [HBM_SPEC] * k_in, out_specs=list(out_specs) + [HBM_SPEC] * k_out,
        out_shape=list(out_shape) + exchange.out_shape,
        input_output_aliases={n_in + i: n_out + o for i, o in exchange.aliases.items()},
        scratch_shapes=list(scratch_shapes) + exchange.sems, compiler_params=compiler_params,
    )(*operands, *[_in_hbm(o) for o in exchange.operands])
    return list(res[:n_out]), list(res[n_out:])


def _in_hbm(a):
    return pltpu.with_memory_space_constraint(a, pltpu.HBM)


def _row_tile(h):
    return max(t for t in range(16, 257, 16) if h % t == 0)


def _cast_shard(a, chip):
    rows, cols = a.shape
    h = rows // 2
    tr = _row_tile(h)

    def body(chip_ref, a_ref, o_ref):
        o_ref[0, 0] = a_ref[0].astype(BF16)

    return pl.pallas_call(
        body, name="cast_shard",
        grid_spec=pltpu.PrefetchScalarGridSpec(
            num_scalar_prefetch=1, grid=(2, h // tr),
            in_specs=[pl.BlockSpec((1, tr, cols), lambda s, r, chip_ref: (s, r, 0))],
            out_specs=pl.BlockSpec((1, 1, tr, cols), lambda s, r, chip_ref: (chip_ref[0], s, r, 0))),
        out_shape=pltpu.HBM((N_CHIP, 2, h, cols), BF16),
        compiler_params=_params(16, 2),
    )(chip, _in_hbm(a.reshape(2, h, cols)))


def _in_proj(x, gain1, w_in_t, exchange=None):
    tm = 256

    def body(x_ref, g_ref, w_ref, z_ref, hn_ref):
        xv = x_ref[...]
        hn = (xv * _rms_scale(xv) * g_ref[...]).astype(BF16)
        hn_ref[...] = hn
        z_ref[...] = _dot_nt(hn, w_ref[...])

    return _call(
        body, (x, gain1, w_in_t), grid=(S // tm,), name="in_proj",
        in_specs=[pl.BlockSpec((tm, D), lambda i: (i, 0)), pl.BlockSpec((1, D), lambda i: (0, 0)),
                  pl.BlockSpec((D_IN, D), lambda i: (0, 0))],
        out_specs=[pl.BlockSpec((tm, D_IN), lambda i: (i, 0)), pl.BlockSpec((tm, D), lambda i: (i, 0))],
        out_shape=[jax.ShapeDtypeStruct((S, D_IN), F32), jax.ShapeDtypeStruct((S, D), BF16)],
        compiler_params=_params(40), exchange=exchange)


def _gelu_parts(v):
    t = jnp.tanh(GELU_C * (v + 0.044715 * (v * v * v)))
    cdf = 0.5 * (1.0 + t)
    return cdf, t


def _band_mask(n):
    a = lax.broadcasted_iota(jnp.int32, (CHUNK, 2 * CHUNK), 0)
    j = lax.broadcasted_iota(jnp.int32, (CHUNK, 2 * CHUNK), 1)
    dist = CHUNK + a - j
    valid = (dist >= 0) & (dist < CHUNK)
    return valid & ((n > 0) | (j >= CHUNK))


def _fill_bias(bucket_ref, table_ref, bias_ref):
    bucket = bucket_ref[...]
    for h in range(N_HEAD):
        acc = jnp.zeros((CHUNK, 2 * CHUNK), F32)
        for b in range(N_BUCKET):
            acc = jnp.where(bucket == b, table_ref[b, h], acc)
        bias_ref[h] = acc


def _fill_tril(ws_ref, wt_ref, wtt_ref=None):
    r = lax.broadcasted_iota(jnp.int32, (CHUNK, CHUNK), 0)
    c = lax.broadcasted_iota(jnp.int32, (CHUNK, CHUNK), 1)
    for g in range(N_GROUP):
        w = jnp.where(c <= r, ws_ref[g], 0.0)
        wt_ref[g] = w.astype(BF16)
        if wtt_ref is not None:
            wtt_ref[g] = w.T.astype(BF16)


def _kv_layouts(kv_prev, kv_cur):
    both = jnp.concatenate([kv_prev, kv_cur], axis=0)
    k = both[:, :128]
    v = both[:, 128:]
    return (k.astype(BF16), pltpu.roll(k, 64, axis=1).astype(BF16),
            v.astype(BF16), pltpu.roll(v, 64, axis=1).astype(BF16))


def _head_place(h):
    pair, pos, kvh = h // 2, h % 2, h // 4
    return pair, pos, kvh == pos


def _softmax_sink(qm, k_use, bias_h, sink, valid):
    s = _dot_nt(qm, k_use) * QK_SCALE + bias_h
    s = jnp.where(valid, s, NEG_INF)
    m = jnp.maximum(jnp.max(s, axis=-1, keepdims=True), sink)
    e = jnp.exp(s - m)
    es = jnp.exp(sink - m)
    denom = jnp.sum(e, axis=-1, keepdims=True) + es
    return e / denom, es / denom


def _mixer_fwd(z, v_gain, w_spatial, b_spatial_t, sinks, rel_table, bucket, exchange=None):
    def body(z_ref, kvp_ref, gain_ref, ws_ref, bt_ref, sink_ref, table_ref, bucket_ref, out_ref, bias_ref, wt_ref):
        n = pl.program_id(0)

        @pl.when(n == 0)
        def _():
            _fill_bias(bucket_ref, table_ref, bias_ref)
            _fill_tril(ws_ref, wt_ref)

        zuv = z_ref[:, :1024]
        cdf, _ = _gelu_parts(zuv)
        guv = zuv * cdf
        for g in range(N_GROUP):
            vg = guv[:, 512 + 128 * g:512 + 128 * (g + 1)]
            vn = vg * _rms_scale(vg) * gain_ref[:, 128 * g:128 * (g + 1)]
            sv = _dot(wt_ref[g], vn.astype(BF16)) + bt_ref[:, g:g + 1]
            out_ref[:, 128 * g:128 * (g + 1)] = (guv[:, 128 * g:128 * (g + 1)] * sv).astype(BF16)

        k_same, k_swap, v_same, v_swap = _kv_layouts(kvp_ref[...], z_ref[:, 1536:1792])
        valid = _band_mask(n)
        lane_half = lax.broadcasted_iota(jnp.int32, (1, 128), 1) // 64
        for pair in range(N_HEAD // 2):
            qq = z_ref[:, 1024 + 128 * pair:1024 + 128 * (pair + 1)]
            acc = jnp.zeros((CHUNK, 128), F32)
            for pos in range(2):
                h = 2 * pair + pos
                _, _, same = _head_place(h)
                qm = jnp.where(lane_half == pos, qq, 0.0).astype(BF16)
                p, _ = _softmax_sink(qm, k_same if same else k_swap, bias_ref[h], sink_ref[h], valid)
                vm = jnp.where(lane_half == pos, v_same if same else v_swap, jnp.zeros((), BF16))
                acc = acc + _dot(p.astype(BF16), vm)
            out_ref[:, 512 + 128 * pair:512 + 128 * (pair + 1)] = acc.astype(BF16)

    return _call(
        body, (z, z, v_gain, w_spatial, b_spatial_t, sinks, rel_table, bucket), grid=(N_BLOCK,), name="mixer_fwd",
        in_specs=[pl.BlockSpec((CHUNK, D_IN), lambda n: (n, 0)),
                  pl.BlockSpec((CHUNK, 256), lambda n: (jnp.maximum(n - 1, 0), 6)),
                  pl.BlockSpec((1, 512), lambda n: (0, 0)),
                  pl.BlockSpec((N_GROUP, CHUNK, CHUNK), lambda n: (0, 0, 0)),
                  pl.BlockSpec((CHUNK, N_GROUP), lambda n: (0, 0)),
                  pl.BlockSpec(memory_space=pltpu.SMEM),
                  pl.BlockSpec(memory_space=pltpu.SMEM),
                  pl.BlockSpec((CHUNK, 2 * CHUNK), lambda n: (0, 0))],
        out_specs=[pl.BlockSpec((CHUNK, D), lambda n: (n, 0))],
        out_shape=[jax.ShapeDtypeStruct((S, D), BF16)],
        scratch_shapes=[pltpu.VMEM((N_HEAD, CHUNK, 2 * CHUNK), F32), pltpu.VMEM((N_GROUP, CHUNK, CHUNK), BF16)],
        compiler_params=_params(32), exchange=exchange)


def _out_proj(x, mix, w_out, gain2, exchange=None):
    tm = 256

    def body(x_ref, mix_ref, w_ref, g_ref, h1_ref, hn_ref, hnt_ref):
        h1 = x_ref[...] + _dot(mix_ref[...], w_ref[...])
        h1_ref[...] = h1
        hn = h1 * _rms_scale(h1) * g_ref[...]
        hn_ref[...] = hn.astype(BF16)
        hnt_ref[...] = hn.T.astype(BF16)

    return _call(
        body, (x, mix, w_out, gain2), grid=(S // tm,), name="out_proj",
        in_specs=[pl.BlockSpec((tm, D), lambda i: (i, 0)), pl.BlockSpec((tm, D), lambda i: (i, 0)),
                  pl.BlockSpec((D, D), lambda i: (0, 0)), pl.BlockSpec((1, D), lambda i: (0, 0))],
        out_specs=[pl.BlockSpec((tm, D), lambda i: (i, 0)), pl.BlockSpec((tm, D), lambda i: (i, 0)),
                   pl.BlockSpec((D, tm), lambda i: (0, i))],
        out_shape=[jax.ShapeDtypeStruct((S, D), F32), jax.ShapeDtypeStruct((S, D), BF16),
                   jax.ShapeDtypeStruct((D, S), BF16)],
        compiler_params=_params(32), exchange=exchange)


def _ffn_up(hn2, w_ff1, exchange=None):
    tm = 512
    nj = D_FF // 1024

    def body(hn_ref, w1_ref, r_ref, a_ref, at_ref):
        r = jnp.maximum(_dot(hn_ref[...], w1_ref[0]), 0.0)
        r_ref[...] = r.astype(BF16)
        a = r * r
        a_ref[...] = a.astype(BF16)
        at_ref[...] = a.T.astype(BF16)

    return _call(
        body, (hn2, w_ff1), grid=(nj, S // tm), name="ffn_up",
        in_specs=[pl.BlockSpec((tm, D), lambda j, i: (i, 0)), pl.BlockSpec((1, D, 1024), lambda j, i: (j, 0, 0))],
        out_specs=[pl.BlockSpec((tm, 1024), lambda j, i: (i, j)), pl.BlockSpec((tm, 1024), lambda j, i: (i, j)),
                   pl.BlockSpec((1024, tm), lambda j, i: (j, i))],
        out_shape=[jax.ShapeDtypeStruct((S, D_FF), BF16), jax.ShapeDtypeStruct((S, D_FF), BF16),
                   jax.ShapeDtypeStruct((D_FF, S), BF16)],
        compiler_params=_params(40, 2), exchange=exchange)


def _ffn_down(h1, a, w_ff2, exchange=None):
    tm = 1024
    nj = D_FF // 1024

    def body(h1_ref, a_ref, w2_ref, h2_ref, acc_ref):
        j = pl.program_id(1)
        part = _dot(a_ref[...], w2_ref[0])

        @pl.when(j == 0)
        def _():
            acc_ref[...] = part

        @pl.when(j > 0)
        def _():
            acc_ref[...] += part

        @pl.when(j == nj - 1)
        def _():
            h2_ref[...] = h1_ref[...] + acc_ref[...]

    return _call(
        body, (h1, a, w_ff2), grid=(S // tm, nj), name="ffn_down",
        in_specs=[pl.BlockSpec((tm, D), lambda i, j: (i, 0)), pl.BlockSpec((tm, 1024), lambda i, j: (i, j)),
                  pl.BlockSpec((1, 1024, D), lambda i, j: (j, 0, 0))],
        out_specs=[pl.BlockSpec((tm, D), lambda i, j: (i, 0))],
        out_shape=[jax.ShapeDtypeStruct((S, D), F32)],
        scratch_shapes=[pltpu.VMEM((tm, D), F32)],
        compiler_params=_params(48, 2), exchange=exchange)


def _tail(h2, p, target, w_gate, w_proj, final_gain):
    tm = 256
    steps = S // tm

    def body(h2_ref, p_ref, t_ref, wg_ref, wp_ref, gf_ref, dh2_ref, dwg_ref, dwp_ref, dgf_ref, loss_ref, dh2b_ref,
             dwp_acc):
        i = pl.program_id(0)
        h2 = h2_ref[...]
        h2b = h2.astype(BF16)
        pb = p_ref[...].astype(BF16)
        gate = jax.nn.sigmoid(_dot(h2b, wg_ref[...]))
        pp = jnp.concatenate([_dot(pb, wp_ref[j]) for j in range(N_CHIP)], axis=1)
        h3 = h2 + gate * pp
        r3 = _rms_scale(h3)
        xhat = h3 * r3
        gf = gf_ref[...]
        err = xhat * gf - t_ref[...]
        dy = err * (1.0 / D)
        dh3 = _rms_bwd(dy * gf, xhat, r3)
        dgp = (dh3 * pp * gate * (1.0 - gate)).astype(BF16)
        dpp = (dh3 * gate).astype(BF16)
        dh2 = dh3 + _dot_nt(dgp, wg_ref[...])
        dh2_ref[...] = dh2
        dh2b_ref[...] = dh2.astype(BF16)
        dwg = _dot_tn(h2b, dgp)
        dwp = _dot_tn(pb, dpp)
        dgf = jnp.sum(dy * xhat, axis=0, keepdims=True)
        sq = jnp.sum(jnp.sum(err * err, axis=1, keepdims=True), axis=0, keepdims=True)

        @pl.when(i == 0)
        def _():
            dwg_ref[...] = dwg
            dwp_acc[...] = dwp
            dgf_ref[...] = dgf
            loss_ref[...] = jnp.broadcast_to(sq, (8, 128))

        @pl.when(i > 0)
        def _():
            dwg_ref[...] += dwg
            dwp_acc[...] += dwp
            dgf_ref[...] += dgf
            loss_ref[...] += jnp.broadcast_to(sq, (8, 128))

        @pl.when(i == steps - 1)
        def _():
            for j in range(N_CHIP):
                dwp_ref[j] = dwp_acc[:, 256 * j:256 * (j + 1)]

    return _call(
        body, (h2, p, target, w_gate, w_proj, final_gain), grid=(steps,), name="tail",
        in_specs=[pl.BlockSpec((tm, D), lambda i: (i, 0)), pl.BlockSpec((tm, PLE), lambda i: (i, 0)),
                  pl.BlockSpec((tm, D), lambda i: (i, 0)), pl.BlockSpec((D, D), lambda i: (0, 0)),
                  pl.BlockSpec((N_CHIP, PLE, 256), lambda i: (0, 0, 0)), pl.BlockSpec((1, D), lambda i: (0, 0))],
        out_specs=[pl.BlockSpec((tm, D), lambda i: (i, 0)), pl.BlockSpec((D, D), lambda i: (0, 0)),
                   pl.BlockSpec((N_CHIP, PLE, 256), lambda i: (0, 0, 0)), pl.BlockSpec((1, D), lambda i: (0, 0)),
                   pl.BlockSpec((8, 128), lambda i: (0, 0)), pl.BlockSpec((tm, D), lambda i: (i, 0))],
        out_shape=[jax.ShapeDtypeStruct((S, D), F32), jax.ShapeDtypeStruct((D, D), F32),
                   jax.ShapeDtypeStruct((N_CHIP, PLE, 256), F32), jax.ShapeDtypeStruct((1, D), F32),
                   jax.ShapeDtypeStruct((8, 128), F32), jax.ShapeDtypeStruct((S, D), BF16)],
        scratch_shapes=[pltpu.VMEM((PLE, D), F32)],
        compiler_params=_params(48))[0]


def _ffn_bwd_weights(dh2b, hn2_t, r, a_t, w_ff2, exchange=None):
    tm = 1024
    nj = D_FF // 1024

    def body(dh2_ref, hnt_ref, r_ref, at_ref, w2_ref, df_ref, dw1_ref, dw2_ref):
        i = pl.program_id(1)
        dh2b = dh2_ref[...]
        da = _dot_nt(dh2b, w2_ref[0])
        df = (da * (2.0 * r_ref[...].astype(F32))).astype(BF16)
        df_ref[...] = df
        dw1 = _dot(hnt_ref[...], df)
        dw2 = _dot(at_ref[...], dh2b)

        @pl.when(i == 0)
        def _():
            dw1_ref[0] = dw1
            dw2_ref[0] = dw2

        @pl.when(i > 0)
        def _():
            dw1_ref[0] += dw1
            dw2_ref[0] += dw2

    return _call(
        body, (dh2b, hn2_t, r, a_t, w_ff2), grid=(nj, S // tm), name="ffn_bwd_weights",
        in_specs=[pl.BlockSpec((tm, D), lambda j, i: (i, 0)), pl.BlockSpec((D, tm), lambda j, i: (0, i)),
                  pl.BlockSpec((tm, 1024), lambda j, i: (i, j)), pl.BlockSpec((1024, tm), lambda j, i: (j, i)),
                  pl.BlockSpec((1, 1024, D), lambda j, i: (j, 0, 0))],
        out_specs=[pl.BlockSpec((tm, 1024), lambda j, i: (i, j)), pl.BlockSpec((1, D, 1024), lambda j, i: (j, 0, 0)),
                   pl.BlockSpec((1, 1024, D), lambda j, i: (j, 0, 0))],
        out_shape=[jax.ShapeDtypeStruct((S, D_FF), BF16), jax.ShapeDtypeStruct((nj, D, 1024), F32),
                   jax.ShapeDtypeStruct((nj, 1024, D), F32)],
        compiler_params=_params(60, 2), exchange=exchange)


def _ffn_bwd_input(df, w_ff1, dh2, h1, gain2, mix, w_out, exchange=None):
    tm = 512
    nj = D_FF // 1024
    steps = S // tm

    def body(df_ref, w1_ref, dh2_ref, h1_ref, g_ref, mix_ref, wo_ref, dh1_ref, dmix_ref, dwo_ref, dg_ref, acc_ref):
        i = pl.program_id(0)
        j = pl.program_id(1)
        part = _dot_nt(df_ref[...], w1_ref[0])

        @pl.when(j == 0)
        def _():
            acc_ref[...] = part

        @pl.when(j > 0)
        def _():
            acc_ref[...] += part

        @pl.when(j == nj - 1)
        def _():
            dhn = acc_ref[...]
            h1 = h1_ref[...]
            r2 = _rms_scale(h1)
            xhat = h1 * r2
            dh1 = dh2_ref[...] + _rms_bwd(dhn * g_ref[...], xhat, r2)
            dh1_ref[...] = dh1
            dh1b = dh1.astype(BF16)
            dmix_ref[...] = _dot_nt(dh1b, wo_ref[...])
            dwo = _dot_tn(mix_ref[...], dh1b)
            dg = jnp.sum(dhn * xhat, axis=0, keepdims=True)

            @pl.when(i == 0)
            def _():
                dwo_ref[...] = dwo
                dg_ref[...] = dg

            @pl.when(i > 0)
            def _():
                dwo_ref[...] += dwo
                dg_ref[...] += dg

    return _call(
        body, (df, w_ff1, dh2, h1, gain2, mix, w_out), grid=(steps, nj), name="ffn_bwd_input",
        in_specs=[pl.BlockSpec((tm, 1024), lambda i, j: (i, j)), pl.BlockSpec((1, D, 1024), lambda i, j: (j, 0, 0)),
                  pl.BlockSpec((tm, D), lambda i, j: (i, 0)), pl.BlockSpec((tm, D), lambda i, j: (i, 0)),
                  pl.BlockSpec((1, D), lambda i, j: (0, 0)), pl.BlockSpec((tm, D), lambda i, j: (i, 0)),
                  pl.BlockSpec((D, D), lambda i, j: (0, 0))],
        out_specs=[pl.BlockSpec((tm, D), lambda i, j: (i, 0)), pl.BlockSpec((tm, D), lambda i, j: (i, 0)),
                   pl.BlockSpec((D, D), lambda i, j: (0, 0)), pl.BlockSpec((1, D), lambda i, j: (0, 0))],
        out_shape=[jax.ShapeDtypeStruct((S, D), F32), jax.ShapeDtypeStruct((S, D), F32),
                   jax.ShapeDtypeStruct((D, D), F32), jax.ShapeDtypeStruct((1, D), F32)],
        scratch_shapes=[pltpu.VMEM((tm, D), F32)],
        compiler_params=_params(56, 2), exchange=exchange)


def _mixer_bwd(z, dmix, v_gain, w_spatial, b_spatial_t, sinks, rel_table, bucket, exchange=None):
    def body(z_ref, kvp_ref, dm_ref, gain_ref, ws_ref, bt_ref, sink_ref, table_ref, bucket_ref,
             dz_ref, dws_ref, db_ref, dgain_ref, dsink_ref, drel_ref,
             bias_ref, wt_ref, wtt_ref, dbias_ref, dsv_ref, carry_ref):
        n = pl.program_id(0)

        @pl.when(n == 0)
        def _():
            _fill_bias(bucket_ref, table_ref, bias_ref)
            _fill_tril(ws_ref, wt_ref, wtt_ref)
            dbias_ref[...] = jnp.zeros_like(dbias_ref)
            dsv_ref[...] = jnp.zeros_like(dsv_ref)
            dws_ref[...] = jnp.zeros_like(dws_ref)
            dgain_ref[...] = jnp.zeros_like(dgain_ref)
            dsink_ref[...] = jnp.zeros_like(dsink_ref)

        rows = pl.ds(pl.multiple_of(n * CHUNK, CHUNK), CHUNK)

        zuv = z_ref[:, :1024]
        cdf, t = _gelu_parts(zuv)
        guv = zuv * cdf
        dgelu = cdf + zuv * (0.5 * (1.0 - t * t)) * (GELU_C * (1.0 + 3.0 * 0.044715 * (zuv * zuv)))
        for g in range(N_GROUP):
            lo, hi = 128 * g, 128 * (g + 1)
            u = guv[:, lo:hi]
            vg = guv[:, 512 + lo:512 + hi]
            rr = _rms_scale(vg)
            vhat = vg * rr
            gain = gain_ref[:, lo:hi]
            vnb = (vhat * gain).astype(BF16)
            sv = _dot(wt_ref[g], vnb) + bt_ref[:, g:g + 1]
            da = dm_ref[:, lo:hi]
            dsv = da * u
            dsvb = dsv.astype(BF16)
            dsv_ref[g] += dsv
            dws_ref[g] += _dot_nt(dsvb, vnb)
            dvn = _dot(wtt_ref[g], dsvb)
            dgain_ref[:, lo:hi] += jnp.sum(dvn * vhat, axis=0, keepdims=True)
            dvg = _rms_bwd(dvn * gain, vhat, rr)
            dz_ref[rows, lo:hi] = (da * sv * dgelu[:, lo:hi]).astype(BF16)
            dz_ref[rows, 512 + lo:512 + hi] = (dvg * dgelu[:, 512 + lo:512 + hi]).astype(BF16)

        k_same, k_swap, v_same, v_swap = _kv_layouts(kvp_ref[...], z_ref[:, 1536:1792])
        valid = _band_mask(n)
        lane_half = lax.broadcasted_iota(jnp.int32, (1, 128), 1) // 64
        zero = jnp.zeros((2 * CHUNK, 128), F32)
        dk_same, dk_swap, dv_same, dv_swap = zero, zero, zero, zero
        for pair in range(N_HEAD // 2):
            cols = slice(1024 + 128 * pair, 1024 + 128 * (pair + 1))
            qq = z_ref[:, cols]
            do_pair = dm_ref[:, 512 + 128 * pair:512 + 128 * (pair + 1)]
            dq = jnp.zeros((CHUNK, 128), F32)
            for pos in range(2):
                h = 2 * pair + pos
                _, _, same = _head_place(h)
                on_half = lane_half == pos
                qm = jnp.where(on_half, qq, 0.0).astype(BF16)
                k_use = k_same if same else k_swap
                v_use = v_same if same else v_swap
                p, p_sink = _softmax_sink(qm, k_use, bias_ref[h], sink_ref[h], valid)
                dom = jnp.where(on_half, do_pair, 0.0).astype(BF16)
                dp = _dot_nt(dom, v_use)
                dsum = jnp.sum(p * dp, axis=-1, keepdims=True)
                ds = p * (dp - dsum)
                dbias_ref[h] += ds
                dsink_ref[h:h + 1, :] += jnp.broadcast_to(jnp.sum(-p_sink * dsum, axis=0, keepdims=True), (1, 128))
                dsb = ds.astype(BF16)
                dq = dq + jnp.where(on_half, _dot(dsb, k_use), 0.0)
                dk_h = _dot_tn(dsb, qm)
                dv_h = _dot_tn(p.astype(BF16), dom)
                if same:
                    dk_same, dv_same = dk_same + dk_h, dv_same + dv_h
                else:
                    dk_swap, dv_swap = dk_swap + dk_h, dv_swap + dv_h
            dz_ref[rows, cols] = (dq * QK_SCALE).astype(BF16)
        dk = (dk_same + pltpu.roll(dk_swap, 64, axis=1)) * QK_SCALE
        dv = dv_same + pltpu.roll(dv_swap, 64, axis=1)
        dkv = jnp.concatenate([dk, dv], axis=1)

        @pl.when(n > 0)
        def _():
            prev_rows = pl.ds(pl.multiple_of((n - 1) * CHUNK, CHUNK), CHUNK)
            dz_ref[prev_rows, 1536:1792] = (carry_ref[...] + dkv[:CHUNK]).astype(BF16)

        carry_ref[...] = dkv[CHUNK:]

        @pl.when(n == N_BLOCK - 1)
        def _():
            dz_ref[rows, 1536:1792] = dkv[CHUNK:].astype(BF16)
            r = lax.broadcasted_iota(jnp.int32, (CHUNK, CHUNK), 0)
            c = lax.broadcasted_iota(jnp.int32, (CHUNK, CHUNK), 1)
            for g in range(N_GROUP):
                dws_ref[g] = jnp.where(c <= r, dws_ref[g], 0.0)
                db_ref[g] = jnp.sum(dsv_ref[g], axis=1, keepdims=True)
            bucket = bucket_ref[...]
            for h in range(N_HEAD):
                dbh = dbias_ref[h]
                per_bucket = [jnp.sum(jnp.where(bucket == b, dbh, 0.0), axis=0, keepdims=True) for b in range(N_BUCKET)]
                drel_ref[h] = jnp.sum(jnp.concatenate(per_bucket, axis=0), axis=1, keepdims=True)

    return _call(
        body, (z, z, dmix, v_gain, w_spatial, b_spatial_t, sinks, rel_table, bucket), grid=(N_BLOCK,), name="mixer_bwd",
        in_specs=[pl.BlockSpec((CHUNK, D_IN), lambda n: (n, 0)),
                  pl.BlockSpec((CHUNK, 256), lambda n: (jnp.maximum(n - 1, 0), 6)),
                  pl.BlockSpec((CHUNK, D), lambda n: (n, 0)),
                  pl.BlockSpec((1, 512), lambda n: (0, 0)),
                  pl.BlockSpec((N_GROUP, CHUNK, CHUNK), lambda n: (0, 0, 0)),
                  pl.BlockSpec((CHUNK, N_GROUP), lambda n: (0, 0)),
                  pl.BlockSpec(memory_space=pltpu.SMEM),
                  pl.BlockSpec(memory_space=pltpu.SMEM),
                  pl.BlockSpec((CHUNK, 2 * CHUNK), lambda n: (0, 0))],
        out_specs=[pl.BlockSpec((S, D_IN), lambda n: (0, 0)),
                   pl.BlockSpec((N_GROUP, CHUNK, CHUNK), lambda n: (0, 0, 0)),
                   pl.BlockSpec((N_GROUP, CHUNK, 1), lambda n: (0, 0, 0)),
                   pl.BlockSpec((1, 512), lambda n: (0, 0)),
                   pl.BlockSpec((N_HEAD, 128), lambda n: (0, 0)),
                   pl.BlockSpec((N_HEAD, N_BUCKET, 1), lambda n: (0, 0, 0))],
        out_shape=[jax.ShapeDtypeStruct((S, D_IN), BF16), jax.ShapeDtypeStruct((N_GROUP, CHUNK, CHUNK), F32),
                   jax.ShapeDtypeStruct((N_GROUP, CHUNK, 1), F32), jax.ShapeDtypeStruct((1, 512), F32),
                   jax.ShapeDtypeStruct((N_HEAD, 128), F32), jax.ShapeDtypeStruct((N_HEAD, N_BUCKET, 1), F32)],
        scratch_shapes=[pltpu.VMEM((N_HEAD, CHUNK, 2 * CHUNK), F32), pltpu.VMEM((N_GROUP, CHUNK, CHUNK), BF16),
                        pltpu.VMEM((N_GROUP, CHUNK, CHUNK), BF16), pltpu.VMEM((N_HEAD, CHUNK, 2 * CHUNK), F32),
                        pltpu.VMEM((N_GROUP, CHUNK, CHUNK), F32), pltpu.VMEM((CHUNK, 256), F32)],
        compiler_params=_params(48), exchange=exchange)


def _in_bwd_weight(dz, hn1, exchange=None):
    tm = 512

    def body(dz_ref, hn_ref, dw_ref):
        i = pl.program_id(0)
        dw = _dot_tn(dz_ref[...], hn_ref[...])

        @pl.when(i == 0)
        def _():
            dw_ref[...] = dw

        @pl.when(i > 0)
        def _():
            dw_ref[...] += dw

    return _call(
        body, (dz, hn1), grid=(S // tm,), name="in_bwd_weight",
        in_specs=[pl.BlockSpec((tm, D_IN), lambda i: (i, 0)), pl.BlockSpec((tm, D), lambda i: (i, 0))],
        out_specs=[pl.BlockSpec((D_IN, D), lambda i: (0, 0))],
        out_shape=[jax.ShapeDtypeStruct((D_IN, D), F32)],
        compiler_params=_params(40), exchange=exchange)


def _in_bwd_input(dz, w_in_t, x, dh1, gain1, exchange=None):
    tm = 512

    def body(dz_ref, w_ref, x_ref, dh1_ref, g_ref, dx_ref, dg_ref):
        i = pl.program_id(0)
        dhn = _dot(dz_ref[...], w_ref[...])
        xv = x_ref[...]
        r1 = _rms_scale(xv)
        xhat = xv * r1
        dx_ref[...] = dh1_ref[...] + _rms_bwd(dhn * g_ref[...], xhat, r1)
        dg = jnp.sum(dhn * xhat, axis=0, keepdims=True)

        @pl.when(i == 0)
        def _():
            dg_ref[...] = dg

        @pl.when(i > 0)
        def _():
            dg_ref[...] += dg

    return _call(
        body, (dz, w_in_t, x, dh1, gain1), grid=(S // tm,), name="in_bwd_input",
        in_specs=[pl.BlockSpec((tm, D_IN), lambda i: (i, 0)), pl.BlockSpec((D_IN, D), lambda i: (0, 0)),
                  pl.BlockSpec((tm, D), lambda i: (i, 0)), pl.BlockSpec((tm, D), lambda i: (i, 0)),
                  pl.BlockSpec((1, D), lambda i: (0, 0))],
        out_specs=[pl.BlockSpec((tm, D), lambda i: (i, 0)), pl.BlockSpec((1, D), lambda i: (0, 0))],
        out_shape=[jax.ShapeDtypeStruct((S, D), F32), jax.ShapeDtypeStruct((1, D), F32)],
        compiler_params=_params(48), exchange=exchange)


def _rel_bucket():
    a = jnp.arange(CHUNK)[:, None]
    j = jnp.arange(2 * CHUNK)[None, :]
    n = jnp.maximum(CHUNK + a - j, 0)
    max_exact = N_BUCKET // 2
    nf = jnp.maximum(n, 1).astype(jnp.float32)
    large = max_exact + (jnp.log(nf / max_exact) / math.log(CHUNK / max_exact) * (N_BUCKET - max_exact)).astype(jnp.int32)
    large = jnp.minimum(large, N_BUCKET - 1)
    return jnp.where(n < max_exact, n, large).astype(jnp.int32)


def _step(x, p, target, small, bufs, place):
    bucket = _rel_bucket()
    sinks = small["attn_sinks"].reshape(N_HEAD)
    b_t = jnp.transpose(small["b_spatial"].reshape(N_GROUP, CHUNK))
    ws = small["w_spatial"].reshape(N_GROUP, CHUNK, CHUNK)
    gain1, gain2 = small["norm1_gain"], small["norm2_gain"]
    v_gain = small["gmlp_v_gain"]
    final_gain = small["final_gain"].reshape(1, D)
    table = small["rel_bias_table"]
    bufs = dict(bufs)

    def gather(*names):
        return _Gather([bufs[n] for n in names])

    def took(names, got):
        bufs.update(zip(names, got))

    took(["w_in"], _gather_weights([bufs["w_in"]]))
    w_in_t = _whole(bufs["w_in"]).reshape(D_IN, D)
    (z, hn1), got = _in_proj(x, gain1, w_in_t, gather("w_out"))
    took(["w_out"], got)
    (mix,), got = _mixer_fwd(z, v_gain, ws, b_t, sinks, table, bucket, gather("w_ff1"))
    took(["w_ff1"], got)
    w_out = _whole(bufs["w_out"]).reshape(D, D)
    (h1, hn2, hn2_t), _ = _out_proj(x, mix, w_out, gain2)
    w_ff1 = _whole(bufs["w_ff1"])
    (r, a, a_t), got = _ffn_up(hn2, w_ff1, gather("w_ff2"))
    took(["w_ff2"], got)
    w_ff2 = _whole(bufs["w_ff2"])
    (h2,), got = _ffn_down(h1, a, w_ff2, gather("w_ple_gate", "w_ple_proj"))
    took(["w_ple_gate", "w_ple_proj"], got)
    dh2, d_gate, d_proj, d_final, sq, dh2b = _tail(h2, p, target, _whole(bufs["w_ple_gate"]).reshape(D, D),
                                                   _whole(bufs["w_ple_proj"]), final_gain)

    def pair_sums(halves, from_sibling):
        sums, landing = zip(*[_pair_sum(g, o, place) for g, o in zip(halves, from_sibling)])
        return list(sums), list(landing)

    landed = {}
    halves = [_halves(d_gate.reshape(N_CHIP, 256, D)), _halves(d_proj)]
    ex = _ChipExchange(*pair_sums(halves, _sibling_exchange(halves)))
    (df, d_ff1, d_ff2), got = _ffn_bwd_weights(dh2b, hn2_t, r, a_t, w_ff2, ex)
    landed.update(zip(["w_ple_gate", "w_ple_proj"], got))
    halves = [_halves(d_ff1), _halves(d_ff2)]
    (dh1, dmix, d_out, d_gain2), got = _ffn_bwd_input(df, w_ff1, dh2, h1, gain2, mix, w_out, _SiblingExchange(halves))
    ex = _ChipExchange(*pair_sums(halves, got))
    (dz, d_ws, d_b, d_vgain, d_sink, d_rel), got = _mixer_bwd(z, dmix, v_gain, ws, b_t, sinks, table, bucket, ex)
    landed.update(zip(["w_ff1", "w_ff2"], got))
    small_grads = {
        "gmlp_v_gain": d_vgain, "w_spatial": d_ws.reshape(1, N_GROUP, CHUNK, CHUNK),
        "b_spatial": d_b.reshape(1, N_GROUP, CHUNK), "attn_sinks": d_sink[:, 0].reshape(1, N_HEAD),
        "rel_bias_table": jnp.transpose(d_rel.reshape(N_HEAD, N_BUCKET)), "norm2_gain": d_gain2,
        "final_gain": d_final.reshape(D),
    }
    halves = [_halves(d_out.reshape(N_CHIP, 256, D))]
    (d_in_t,), got = _in_bwd_weight(dz, hn1, _ChipExchange(*pair_sums(halves, _sibling_exchange(halves))))
    landed.update(zip(["w_out"], got))
    halves = [_halves(d_in_t.reshape(N_CHIP, 448, D))]
    ex = _Both(_Both(_ChipExchange(*pair_sums(halves, _sibling_exchange(halves))),
                     _GatherAll(_pack_small([small_grads[n] for n in SMALL_EARLY]).astype(BF16))), _GatherAll(sq))
    (dx, d_gain1), got = _in_bwd_input(dz, w_in_t, x, dh1, gain1, ex)
    landed["w_in"], small_gathered, sq_gathered = got
    return dx, landed, d_gain1, small_gathered, sq_gathered


HBM_SPEC = pl.BlockSpec(memory_space=pltpu.HBM)
VMEM_SPEC = pl.BlockSpec(memory_space=pltpu.VMEM)


def _mesh_place():
    x, y, c = lax.axis_index("x"), lax.axis_index("y"), lax.axis_index("c")
    others = [(1 - x, y), (x, 1 - y), (1 - x, 1 - y)]
    return x, y, c, others


def _remote(src, dst, send_sem, recv_sem, device):
    return pltpu.make_async_remote_copy(src_ref=src, dst_ref=dst, send_sem=send_sem, recv_sem=recv_sem,
                                        device_id=device, device_id_type=MESH)


def _hbm_like(a, shape=None, dtype=None):
    return pltpu.HBM(a.shape if shape is None else shape, a.dtype if dtype is None else dtype)


def _gather_start(bufs, send_sems, recv_sems):
    x, y, c, others = _mesh_place()
    me = 2 * x + y
    for w, buf in enumerate(bufs):
        for k, (ox, oy) in enumerate(others):
            mine = buf.at[me, c]
            _remote(mine, mine, send_sems.at[w, k], recv_sems.at[w, k], (ox, oy, c)).start()


def _gather_finish(bufs, send_sems, recv_sems):
    x, y, c, others = _mesh_place()
    me = 2 * x + y
    sibling = (x, y, 1 - c)
    idx = [2 * ox + oy for ox, oy in others]
    for w, buf in enumerate(bufs):
        for k in range(3):
            landed = buf.at[idx[k], c]
            _remote(landed, landed, send_sems.at[w, k], recv_sems.at[w, k], sibling).wait_recv()
            _remote(landed, landed, send_sems.at[w, 3 + k], recv_sems.at[w, 3 + k], sibling).start()
    for w, buf in enumerate(bufs):
        for k in range(3):
            landed = buf.at[idx[k], 1 - c]
            _remote(landed, landed, send_sems.at[w, 3 + k], recv_sems.at[w, 3 + k], sibling).wait_recv()
    for w, buf in enumerate(bufs):
        for k in range(3):
            mine, passed = buf.at[me, c], buf.at[idx[k], c]
            _remote(mine, mine, send_sems.at[w, k], recv_sems.at[w, k], sibling).wait_send()
            _remote(passed, passed, send_sems.at[w, 3 + k], recv_sems.at[w, 3 + k], sibling).wait_send()


def _gather_sems(n):
    return [pltpu.SemaphoreType.DMA((n, 6)), pltpu.SemaphoreType.DMA((n, 6))]


def _gather_weights(bufs):
    n = len(bufs)

    def body(*refs):
        outs = refs[n:2 * n]
        send_sems, recv_sems = refs[2 * n:]
        _gather_start(outs, send_sems, recv_sems)
        _gather_finish(outs, send_sems, recv_sems)

    return pl.pallas_call(
        body, name="gather_weights",
        in_specs=[HBM_SPEC] * n, out_specs=[HBM_SPEC] * n,
        out_shape=[_hbm_like(b) for b in bufs],
        input_output_aliases={w: w for w in range(n)},
        scratch_shapes=_gather_sems(n),
    )(*bufs)


def _sibling_copies(grads, landing, send_sems, recv_sems):
    x, y, c, _ = _mesh_place()
    return [_remote(grads[w].at[j, 1 - c], landing[w].at[j], send_sems.at[w, j], recv_sems.at[w, j], (x, y, 1 - c))
            for w in range(len(grads)) for j in range(N_CHIP)]


def _sibling_exchange_start(grads, landing, send_sems, recv_sems):
    for cp in _sibling_copies(grads, landing, send_sems, recv_sems):
        cp.start()


def _sibling_exchange_finish(grads, landing, send_sems, recv_sems):
    copies = _sibling_copies(grads, landing, send_sems, recv_sems)
    for cp in copies:
        cp.wait_recv()
    for cp in copies:
        cp.wait_send()


def _sibling_exchange_sems(n):
    return [pltpu.SemaphoreType.DMA((n, N_CHIP)), pltpu.SemaphoreType.DMA((n, N_CHIP))]


def _sibling_exchange(grads):
    n = len(grads)

    def body(*refs):
        ins, outs = refs[:n], refs[n:2 * n]
        _sibling_exchange_start(ins, outs, *refs[2 * n:])
        _sibling_exchange_finish(ins, outs, *refs[2 * n:])

    return pl.pallas_call(
        body, name="sibling_exchange",
        in_specs=[HBM_SPEC] * n, out_specs=[HBM_SPEC] * n,
        out_shape=[_hbm_like(g, (N_CHIP,) + g.shape[2:]) for g in grads],
        scratch_shapes=_sibling_exchange_sems(n),
    )(*[_in_hbm(g) for g in grads])


def _chip_exchange_start(sums, landing, send_sems, recv_sems):
    x, y, c, others = _mesh_place()
    me = 2 * x + y
    for w in range(len(sums)):
        for k, (ox, oy) in enumerate(others):
            _remote(sums[w].at[2 * ox + oy], landing[w].at[me], send_sems.at[w, k], recv_sems.at[w, k],
                    (ox, oy, c)).start()


def _chip_exchange_finish(sums, landing, send_sems, recv_sems):
    x, y, c, others = _mesh_place()
    for w in range(len(sums)):
        for k, (ox, oy) in enumerate(others):
            piece = landing[w].at[2 * ox + oy]
            _remote(piece, piece, send_sems.at[w, k], recv_sems.at[w, k], (x, y, c)).wait_recv()
    for w in range(len(sums)):
        for k, (ox, oy) in enumerate(others):
            piece = sums[w].at[2 * ox + oy]
            _remote(piece, piece, send_sems.at[w, k], recv_sems.at[w, k], (x, y, c)).wait_send()


def _chip_exchange_sems(n):
    return [pltpu.SemaphoreType.DMA((n, 3)), pltpu.SemaphoreType.DMA((n, 3))]


def _chip_exchange(sums, landing):
    n = len(sums)

    def body(*refs):
        ins, outs = refs[:n], refs[2 * n:3 * n]
        send_sems, recv_sems = refs[3 * n:]
        _chip_exchange_start(ins, outs, send_sems, recv_sems)
        _chip_exchange_finish(ins, outs, send_sems, recv_sems)

    return pl.pallas_call(
        body, name="chip_exchange",
        in_specs=[HBM_SPEC] * (2 * n), out_specs=[HBM_SPEC] * n,
        out_shape=[_hbm_like(b) for b in landing],
        input_output_aliases={n + w: w for w in range(n)},
        scratch_shapes=_chip_exchange_sems(n),
    )(*sums, *landing)


def _sibling_allgather(bufs):
    n = len(bufs)

    def body(*refs):
        outs = refs[n:2 * n]
        send_sems, recv_sems = refs[2 * n:]
        x, y, c, _ = _mesh_place()
        sibling = (x, y, 1 - c)
        sends = [_remote(outs[w].at[c], outs[w].at[c], send_sems.at[w], recv_sems.at[w], sibling) for w in range(n)]
        for cp in sends:
            cp.start()
        for w in range(n):
            landed = outs[w].at[1 - c]
            _remote(landed, landed, send_sems.at[w], recv_sems.at[w], sibling).wait_recv()
        for cp in sends:
            cp.wait_send()

    return pl.pallas_call(
        body, name="sibling_allgather",
        in_specs=[HBM_SPEC] * n, out_specs=[HBM_SPEC] * n,
        out_shape=[_hbm_like(b) for b in bufs],
        input_output_aliases={w: w for w in range(n)},
        scratch_shapes=[pltpu.SemaphoreType.DMA((n,)), pltpu.SemaphoreType.DMA((n,))],
    )(*bufs)


def _pair_sum(grad, other, place):
    _, _, h, cols = grad.shape
    tr = _row_tile(h)

    def body(place_ref, g_ref, o_ref, sums_ref, own_ref):
        s = (g_ref[0, 0] + o_ref[0]).astype(BF16)
        sums_ref[0] = s

        @pl.when(pl.program_id(1) == place_ref[0])
        def _():
            own_ref[0] = s

    return pl.pallas_call(
        body, name="pair_sum",
        grid_spec=pltpu.PrefetchScalarGridSpec(
            num_scalar_prefetch=1, grid=(h // tr, N_CHIP),
            in_specs=[pl.BlockSpec((1, 1, tr, cols), lambda r, j, place_ref: (j, place_ref[1], r, 0)),
                      pl.BlockSpec((1, tr, cols), lambda r, j, place_ref: (j, r, 0))],
            out_specs=[pl.BlockSpec((1, tr, cols), lambda r, j, place_ref: (j, r, 0)),
                       pl.BlockSpec((1, tr, cols), lambda r, j, place_ref: (place_ref[0], r, 0))]),
        out_shape=[pltpu.HBM((N_CHIP, h, cols), BF16)] * 2,
        compiler_params=_params(16, 2),
    )(place, _in_hbm(grad), _in_hbm(other))


def _chip_sum(parts, place):
    _, h, cols = parts.shape
    tr = _row_tile(h)

    def body(place_ref, p_ref, out_ref):
        out_ref[0] = ((p_ref[0].astype(F32) + p_ref[1].astype(F32)) + p_ref[2].astype(F32)) + p_ref[3].astype(F32)

    return pl.pallas_call(
        body, name="chip_sum",
        grid_spec=pltpu.PrefetchScalarGridSpec(
            num_scalar_prefetch=1, grid=(h // tr,),
            in_specs=[pl.BlockSpec((N_CHIP, tr, cols), lambda r, place_ref: (0, r, 0))],
            out_specs=pl.BlockSpec((1, tr, cols), lambda r, place_ref: (place_ref[1], r, 0))),
        out_shape=pltpu.HBM((2, h, cols), F32),
        compiler_params=_params(16),
    )(place, _in_hbm(parts))


def _adamw_math(w, g, m, v):
    m = ADAM_B1 * m + (1.0 - ADAM_B1) * g
    v = ADAM_B2 * v + (1.0 - ADAM_B2) * (g * g)
    m_hat = m / (1.0 - ADAM_B1 ** ADAM_STEP)
    v_hat = v / (1.0 - ADAM_B2 ** ADAM_STEP)
    delta = -ADAM_LR * (m_hat / (jnp.sqrt(v_hat) + ADAM_EPS) + ADAM_WD * w)
    return delta, m, v


def _adamw(w, g, m, v, exchange=None):
    rows, cols = w.shape
    tr = _row_tile(rows)

    def body(w_ref, g_ref, m_ref, v_ref, d_ref, nm_ref, nv_ref, g_out_ref):
        g = g_ref[...]
        d_ref[...], nm_ref[...], nv_ref[...] = _adamw_math(w_ref[...], g, m_ref[...], v_ref[...])
        g_out_ref[...] = g

    spec = pl.BlockSpec((tr, cols), lambda r: (r, 0))
    return _call(
        body, (w, g, m, v), grid=(rows // tr,), name="adamw",
        in_specs=[spec] * 4, out_specs=[spec] * 4,
        out_shape=[jax.ShapeDtypeStruct((rows, cols), F32)] * 4,
        compiler_params=_params(32), exchange=exchange)


SMALL_NAMES = ("norm1_gain", "gmlp_v_gain", "w_spatial", "b_spatial", "attn_sinks", "rel_bias_table", "norm2_gain",
               "final_gain")
SMALL_EARLY = SMALL_NAMES[1:]
PACK_TILE = 8 * 128


def _pack_small(arrays):
    parts = []
    for a in arrays:
        flat = a.reshape(-1)
        rows = -(-flat.shape[0] // PACK_TILE) * 8
        parts.append(jnp.pad(flat, (0, rows * 128 - flat.shape[0])).reshape(rows, 128))
    return jnp.concatenate(parts, axis=0)


def _unpack_small(packed, like):
    out, row = [], 0
    for a in like:
        size = math.prod(a.shape)
        rows = -(-size // PACK_TILE) * 8
        out.append(packed[row:row + rows].reshape(-1)[:size].reshape(a.shape))
        row += rows
    return out


def _small_update(gathered, w, m, v):
    rows = gathered.shape[1]

    def body(g_ref, w_ref, m_ref, v_ref, tot_ref, d_ref, nm_ref, nv_ref):
        total = g_ref[0].astype(F32)
        for dev in range(1, 8):
            total = total + g_ref[dev].astype(F32)
        tot_ref[...] = total
        d_ref[...], nm_ref[...], nv_ref[...] = _adamw_math(w_ref[...], total, m_ref[...], v_ref[...])

    return pl.pallas_call(
        body, name="small_update",
        in_specs=[VMEM_SPEC] * 4, out_specs=[VMEM_SPEC] * 4,
        out_shape=[jax.ShapeDtypeStruct((rows, 128), F32)] * 4,
        compiler_params=pltpu.CompilerParams(vmem_limit_bytes=24 * MIB),
    )(gathered, w, m, v)


def _halves(a):
    return a.reshape(a.shape[:-2] + (2, a.shape[-2] // 2, a.shape[-1]))


def _whole(a):
    return a.reshape(a.shape[:-3] + (2 * a.shape[-2], a.shape[-1]))


def kernel(x, p, norm1_gain, w_in, gmlp_v_gain, w_spatial, b_spatial, attn_sinks, rel_bias_table, w_out, norm2_gain, w_ff1, w_ff2, w_ple_proj, w_ple_gate, final_gain, loss_target, m_norm1_gain, m_w_in, m_gmlp_v_gain, m_w_spatial, m_b_spatial, m_attn_sinks, m_rel_bias_table, m_w_out, m_norm2_gain, m_w_ff1, m_w_ff2, m_w_ple_proj, m_w_ple_gate, m_final_gain, v_norm1_gain, v_w_in, v_gmlp_v_gain, v_w_spatial, v_b_spatial, v_attn_sinks, v_rel_bias_table, v_w_out, v_norm2_gain, v_w_ff1, v_w_ff2, v_w_ple_proj, v_w_ple_gate, v_final_gain):
    given = dict(locals())
    small = {n: given[n] for n in SMALL_NAMES}
    chip = 2 * lax.axis_index("x") + lax.axis_index("y")
    place = jnp.stack([chip, lax.axis_index("c")]).astype(jnp.int32)

    big_names = ("w_in", "w_out", "w_ff1", "w_ff2", "w_ple_proj", "w_ple_gate")
    shards = {n: given[n][0] for n in big_names}
    travel = dict(shards, w_in=jnp.transpose(shards["w_in"]))
    bufs = {n: _cast_shard(travel[n], place[:1]) for n in big_names}
    dx, landed, d_gain1, small_gathered, sq_gathered = _step(x[0], p[0, 0], loss_target[0], small, bufs, place)

    out_grad, out_delta, out_m, out_v = {}, {}, {}, {}

    def update(n, g, exchange=None):
        to = jnp.transpose if n == "w_in" else (lambda a: a)
        (delta, new_m, new_v, g_out), got = _adamw(to(shards[n]), g, to(given["m_" + n][0]), to(given["v_" + n][0]),
                                                   exchange)
        out_grad[n], out_delta[n], out_m[n], out_v[n] = [to(a)[None] for a in (g_out, delta, new_m, new_v)]
        return got

    reduced = dict(zip(big_names, _sibling_allgather([_chip_sum(landed[n], place) for n in big_names])))
    gain1_gathered, = update("w_ple_proj", _whole(reduced["w_ple_proj"]), _GatherAll(_pack_small([d_gain1])))
    for n in big_names:
        if n != "w_ple_proj":
            update(n, _whole(reduced[n]))

    for names, gathered in ((SMALL_EARLY, small_gathered), (SMALL_NAMES[:1], gain1_gathered)):
        like = [given[n] for n in names]
        packed = _small_update(gathered, *[_pack_small([given[pre + n] for n in names]) for pre in ("", "m_", "v_")])
        for res, out in zip(packed, (out_grad, out_delta, out_m, out_v)):
            out.update(zip(names, _unpack_small(res, like)))
    loss = 0.5 * jnp.sum(sq_gathered[:, 0, 0]) / D

    order = ("norm1_gain", "w_in", "gmlp_v_gain", "w_spatial", "b_spatial", "attn_sinks", "rel_bias_table", "w_out",
             "norm2_gain", "w_ff1", "w_ff2", "w_ple_proj", "w_ple_gate", "final_gain")
    return (loss, dx[None], *[out_grad[n] for n in order], *[out_delta[n] for n in order],
            *[out_m[n] for n in order], *[out_v[n] for n in order])
```

```python
import functools
import math

import jax
import jax.numpy as jnp
from jax import lax
from jax.experimental import pallas as pl
from jax.experimental.pallas import tpu as pltpu

S = 2048
D = 1024
D_IN = 1792
D_FF = 4096
PLE = 256
N_CHIP = 4
N_GROUP = 4
CHUNK = 128
N_HEAD = 8
N_BLOCK = S // CHUNK
N_BUCKET = 32
EPS = 1e-6
NEG_INF = -1e30
QK_SCALE = 0.125
GELU_C = math.sqrt(2.0 / math.pi)

ADAM_LR = 0.001
ADAM_B1 = 0.9
ADAM_B2 = 0.999
ADAM_EPS = 1e-08
ADAM_WD = 0.01
ADAM_STEP = 10

F32 = jnp.float32
BF16 = jnp.bfloat16
MIB = 1024 * 1024
MESH = pl.DeviceIdType.MESH

NT = (((1,), (1,)), ((), ()))
TN = (((0,), (0,)), ((), ()))


def _dot(a, b):
    return jnp.dot(a, b, preferred_element_type=F32)


def _dot_nt(a, b):
    return lax.dot_general(a, b, NT, preferred_element_type=F32)


def _dot_tn(a, b):
    return lax.dot_general(a, b, TN, preferred_element_type=F32)


def _params(vmem_mib, n_axes=1):
    return pltpu.CompilerParams(dimension_semantics=("arbitrary",) * n_axes, vmem_limit_bytes=vmem_mib * MIB)


def _rms_scale(v):
    return lax.rsqrt(jnp.mean(v * v, axis=-1, keepdims=True) + EPS)


def _rms_bwd(dy_gain, xhat, r):
    return r * (dy_gain - xhat * jnp.mean(dy_gain * xhat, axis=-1, keepdims=True))


class _Gather:
    def __init__(self, bufs):
        self.operands = list(bufs)
        self.n_out = len(self.operands)
        self.out_shape = [_hbm_like(b) for b in bufs]
        self.aliases = {w: w for w in range(self.n_out)}
        self.sems = _gather_sems(self.n_out)

    def start(self, ins, outs, sems):
        _gather_start(outs, *sems)

    def finish(self, ins, outs, sems):
        _gather_finish(outs, *sems)


class _ChipExchange:
    def __init__(self, sums, landing):
        self.n_out = len(landing)
        self.operands = list(sums) + list(landing)
        self.out_shape = [_hbm_like(b) for b in landing]
        self.aliases = {self.n_out + w: w for w in range(self.n_out)}
        self.sems = _chip_exchange_sems(self.n_out)

    def start(self, ins, outs, sems):
        _chip_exchange_start(ins[:self.n_out], outs, *sems)

    def finish(self, ins, outs, sems):
        _chip_exchange_finish(ins[:self.n_out], outs, *sems)


class _GatherAll:
    def __init__(self, packed):
        self.operands = [packed]
        self.n_out = 1
        self.out_shape = [_hbm_like(packed, (8,) + packed.shape)]
        self.aliases = {}
        self.sems = [pltpu.SemaphoreType.DMA((8,)), pltpu.SemaphoreType.DMA((8,))]

    def _copies(self, ins, outs, sems):
        x, y, c, _ = _mesh_place()
        me = 4 * x + 2 * y + c
        send_sems, recv_sems = sems
        copies = []
        for k in range(1, 8):
            peer = (1 - x if k // 4 else x, 1 - y if (k // 2) % 2 else y, 1 - c if k % 2 else c)
            src = 4 * peer[0] + 2 * peer[1] + peer[2]
            copies.append((_remote(ins[0], outs[0].at[me], send_sems.at[k], recv_sems.at[k], peer), outs[0].at[src]))
        own = pltpu.make_async_copy(ins[0], outs[0].at[me], send_sems.at[0])
        return own, copies

    def start(self, ins, outs, sems):
        own, copies = self._copies(ins, outs, sems)
        own.start()
        for cp, _ in copies:
            cp.start()

    def finish(self, ins, outs, sems):
        own, copies = self._copies(ins, outs, sems)
        x, y, c, _ = _mesh_place()
        for k, (cp, landed) in enumerate(copies):
            _remote(landed, landed, sems[0].at[k + 1], sems[1].at[k + 1], (x, y, c)).wait_recv()
        for cp, _ in copies:
            cp.wait_send()
        own.wait()


class _Both:
    def __init__(self, a, b):
        self.a, self.b = a, b
        self.operands = a.operands + b.operands
        self.n_out = a.n_out + b.n_out
        self.out_shape = a.out_shape + b.out_shape
        self.aliases = dict(a.aliases)
        self.aliases.update({len(a.operands) + i: a.n_out + o for i, o in b.aliases.items()})
        self.sems = a.sems + b.sems

    def _split(self, ins, outs, sems):
        ka, na, sa = len(self.a.operands), self.a.n_out, len(self.a.sems)
        return (ins[:ka], outs[:na], sems[:sa]), (ins[ka:], outs[na:], sems[sa:])

    def start(self, ins, outs, sems):
        for ex, args in zip((self.a, self.b), self._split(ins, outs, sems)):
            ex.start(*args)

    def finish(self, ins, outs, sems):
        for ex, args in zip((self.a, self.b), self._split(ins, outs, sems)):
            ex.finish(*args)


class _SiblingExchange:
    def __init__(self, grads):
        self.operands = list(grads)
        self.n_out = len(self.operands)
        self.out_shape = [_hbm_like(g, (N_CHIP,) + g.shape[2:]) for g in grads]
        self.aliases = {}
        self.sems = _sibling_exchange_sems(self.n_out)

    def start(self, ins, outs, sems):
        _sibling_exchange_start(ins, outs, *sems)

    def finish(self, ins, outs, sems):
        _sibling_exchange_finish(ins, outs, *sems)


def _call(body, operands, *, grid, in_specs, out_specs, out_shape, name, compiler_params, scratch_shapes=(),
          exchange=None):
    operands = [o if getattr(spec, "memory_space", None) == pltpu.SMEM else _in_hbm(o)
                for o, spec in zip(operands, in_specs)]
    out_shape = [pltpu.HBM(s.shape, s.dtype) for s in out_shape]
    if exchange is None:
        res = pl.pallas_call(body, grid=grid, in_specs=in_specs, out_specs=out_specs, out_shape=out_shape, name=name,
                             scratch_shapes=list(scratch_shapes), compiler_params=compiler_params)(*operands)
        return list(res), []
    n_in, n_out, n_scr = len(in_specs), len(out_specs), len(scratch_shapes)
    k_in, k_out = len(exchange.operands), exchange.n_out

    def fused(*refs):
        ins, refs = refs[:n_in], refs[n_in:]
        ex_ins, refs = refs[:k_in], refs[k_in:]
        outs, refs = refs[:n_out], refs[n_out:]
        ex_outs, refs = refs[:k_out], refs[k_out:]
        scratch, sems = refs[:n_scr], refs[n_scr:]
        ids = [pl.program_id(a) for a in range(len(grid))]
        first = functools.reduce(jnp.logical_and, [i == 0 for i in ids])
        last = functools.reduce(jnp.logical_and, [i == g - 1 for i, g in zip(ids, grid)])

        @pl.when(first)
        def _():
            exchange.start(ex_ins, ex_outs, sems)

        body(*ins, *outs, *scratch)

        @pl.when(last)
        def _():
            exchange.finish(ex_ins, ex_outs, sems)

    res = pl.pallas_call(
        fused, grid=grid, name=name,
        in_specs=list(in_specs) + [HBM_SPEC] * k_in, out_specs=list(out_specs) + [HBM_SPEC] * k_out,
        out_shape=list(out_shape) + exchange.out_shape,
        input_output_aliases={n_in + i: n_out + o for i, o in exchange.aliases.items()},
        scratch_shapes=list(scratch_shapes) + exchange.sems, compiler_params=compiler_params,
    )(*operands, *[_in_hbm(o) for o in exchange.operands])
    return list(res[:n_out]), list(res[n_out:])


def _in_hbm(a):
    return pltpu.with_memory_space_constraint(a, pltpu.HBM)


def _row_tile(h):
    return max(t for t in range(16, 257, 16) if h % t == 0)


def _cast_shard(a, chip):
    rows, cols = a.shape
    h = rows // 2
    tr = _row_tile(h)

    def body(chip_ref, a_ref, o_ref):
        o_ref[0, 0] = a_ref[0].astype(BF16)

    return pl.pallas_call(
        body, name="cast_shard",
        grid_spec=pltpu.PrefetchScalarGridSpec(
            num_scalar_prefetch=1, grid=(2, h // tr),
            in_specs=[pl.BlockSpec((1, tr, cols), lambda s, r, chip_ref: (s, r, 0))],
            out_specs=pl.BlockSpec((1, 1, tr, cols), lambda s, r, chip_ref: (chip_ref[0], s, r, 0))),
        out_shape=pltpu.HBM((N_CHIP, 2, h, cols), BF16),
        compiler_params=_params(16, 2),
    )(chip, _in_hbm(a.reshape(2, h, cols)))


def _in_proj(x, gain1, w_in_t, exchange=None):
    tm = 256

    def body(x_ref, g_ref, w_ref, z_ref, hn_ref):
        xv = x_ref[...]
        hn = (xv * _rms_scale(xv) * g_ref[...]).astype(BF16)
        hn_ref[...] = hn
        z_ref[...] = _dot_nt(hn, w_ref[...])

    return _call(
        body, (x, gain1, w_in_t), grid=(S // tm,), name="in_proj",
        in_specs=[pl.BlockSpec((tm, D), lambda i: (i, 0)), pl.BlockSpec((1, D), lambda i: (0, 0)),
                  pl.BlockSpec((D_IN, D), lambda i: (0, 0))],
        out_specs=[pl.BlockSpec((tm, D_IN), lambda i: (i, 0)), pl.BlockSpec((tm, D), lambda i: (i, 0))],
        out_shape=[jax.ShapeDtypeStruct((S, D_IN), F32), jax.ShapeDtypeStruct((S, D), BF16)],
        compiler_params=_params(40), exchange=exchange)


def _gelu_parts(v):
    t = jnp.tanh(GELU_C * (v + 0.044715 * (v * v * v)))
    cdf = 0.5 * (1.0 + t)
    return cdf, t


def _band_mask(n):
    a = lax.broadcasted_iota(jnp.int32, (CHUNK, 2 * CHUNK), 0)
    j = lax.broadcasted_iota(jnp.int32, (CHUNK, 2 * CHUNK), 1)
    dist = CHUNK + a - j
    valid = (dist >= 0) & (dist < CHUNK)
    return valid & ((n > 0) | (j >= CHUNK))


def _fill_bias(bucket_ref, table_ref, bias_ref):
    bucket = bucket_ref[...]
    for h in range(N_HEAD):
        acc = jnp.zeros((CHUNK, 2 * CHUNK), F32)
        for b in range(N_BUCKET):
            acc = jnp.where(bucket == b, table_ref[b, h], acc)
        bias_ref[h] = acc


def _fill_tril(ws_ref, wt_ref, wtt_ref=None):
    r = lax.broadcasted_iota(jnp.int32, (CHUNK, CHUNK), 0)
    c = lax.broadcasted_iota(jnp.int32, (CHUNK, CHUNK), 1)
    for g in range(N_GROUP):
        w = jnp.where(c <= r, ws_ref[g], 0.0)
        wt_ref[g] = w.astype(BF16)
        if wtt_ref is not None:
            wtt_ref[g] = w.T.astype(BF16)


def _kv_layouts(kv_prev, kv_cur):
    both = jnp.concatenate([kv_prev, kv_cur], axis=0)
    k = both[:, :128]
    v = both[:, 128:]
    return (k.astype(BF16), pltpu.roll(k, 64, axis=1).astype(BF16),
            v.astype(BF16), pltpu.roll(v, 64, axis=1).astype(BF16))


def _head_place(h):
    pair, pos, kvh = h // 2, h % 2, h // 4
    return pair, pos, kvh == pos


def _softmax_sink(qm, k_use, bias_h, sink, valid):
    s = _dot_nt(qm, k_use) * QK_SCALE + bias_h
    s = jnp.where(valid, s, NEG_INF)
    m = jnp.maximum(jnp.max(s, axis=-1, keepdims=True), sink)
    e = jnp.exp(s - m)
    es = jnp.exp(sink - m)
    denom = jnp.sum(e, axis=-1, keepdims=True) + es
    return e / denom, es / denom


def _mixer_fwd(z, v_gain, w_spatial, b_spatial_t, sinks, rel_table, bucket, exchange=None):
    def body(z_ref, kvp_ref, gain_ref, ws_ref, bt_ref, sink_ref, table_ref, bucket_ref, out_ref, bias_ref, wt_ref):
        n = pl.program_id(0)

        @pl.when(n == 0)
        def _():
            _fill_bias(bucket_ref, table_ref, bias_ref)
            _fill_tril(ws_ref, wt_ref)

        zuv = z_ref[:, :1024]
        cdf, _ = _gelu_parts(zuv)
        guv = zuv * cdf
        for g in range(N_GROUP):
            vg = guv[:, 512 + 128 * g:512 + 128 * (g + 1)]
            vn = vg * _rms_scale(vg) * gain_ref[:, 128 * g:128 * (g + 1)]
            sv = _dot(wt_ref[g], vn.astype(BF16)) + bt_ref[:, g:g + 1]
            out_ref[:, 128 * g:128 * (g + 1)] = (guv[:, 128 * g:128 * (g + 1)] * sv).astype(BF16)

        k_same, k_swap, v_same, v_swap = _kv_layouts(kvp_ref[...], z_ref[:, 1536:1792])
        valid = _band_mask(n)
        lane_half = lax.broadcasted_iota(jnp.int32, (1, 128), 1) // 64
        for pair in range(N_HEAD // 2):
            qq = z_ref[:, 1024 + 128 * pair:1024 + 128 * (pair + 1)]
            acc = jnp.zeros((CHUNK, 128), F32)
            for pos in range(2):
                h = 2 * pair + pos
                _, _, same = _head_place(h)
                qm = jnp.where(lane_half == pos, qq, 0.0).astype(BF16)
                p, _ = _softmax_sink(qm, k_same if same else k_swap, bias_ref[h], sink_ref[h], valid)
                vm = jnp.where(lane_half == pos, v_same if same else v_swap, jnp.zeros((), BF16))
                acc = acc + _dot(p.astype(BF16), vm)
            out_ref[:, 512 + 128 * pair:512 + 128 * (pair + 1)] = acc.astype(BF16)

    return _call(
        body, (z, z, v_gain, w_spatial, b_spatial_t, sinks, rel_table, bucket), grid=(N_BLOCK,), name="mixer_fwd",
        in_specs=[pl.BlockSpec((CHUNK, D_IN), lambda n: (n, 0)),
                  pl.BlockSpec((CHUNK, 256), lambda n: (jnp.maximum(n - 1, 0), 6)),
                  pl.BlockSpec((1, 512), lambda n: (0, 0)),
                  pl.BlockSpec((N_GROUP, CHUNK, CHUNK), lambda n: (0, 0, 0)),
                  pl.BlockSpec((CHUNK, N_GROUP), lambda n: (0, 0)),
                  pl.BlockSpec(memory_space=pltpu.SMEM),
                  pl.BlockSpec(memory_space=pltpu.SMEM),
                  pl.BlockSpec((CHUNK, 2 * CHUNK), lambda n: (0, 0))],
        out_specs=[pl.BlockSpec((CHUNK, D), lambda n: (n, 0))],
        out_shape=[jax.ShapeDtypeStruct((S, D), BF16)],
        scratch_shapes=[pltpu.VMEM((N_HEAD, CHUNK, 2 * CHUNK), F32), pltpu.VMEM((N_GROUP, CHUNK, CHUNK), BF16)],
        compiler_params=_params(32), exchange=exchange)


def _out_proj(x, mix, w_out, gain2, exchange=None):
    tm = 256

    def body(x_ref, mix_ref, w_ref, g_ref, h1_ref, hn_ref, hnt_ref):
        h1 = x_ref[...] + _dot(mix_ref[...], w_ref[...])
        h1_ref[...] = h1
        hn = h1 * _rms_scale(h1) * g_ref[...]
        hn_ref[...] = hn.astype(BF16)
        hnt_ref[...] = hn.T.astype(BF16)

    return _call(
        body, (x, mix, w_out, gain2), grid=(S // tm,), name="out_proj",
        in_specs=[pl.BlockSpec((tm, D), lambda i: (i, 0)), pl.BlockSpec((tm, D), lambda i: (i, 0)),
                  pl.BlockSpec((D, D), lambda i: (0, 0)), pl.BlockSpec((1, D), lambda i: (0, 0))],
        out_specs=[pl.BlockSpec((tm, D), lambda i: (i, 0)), pl.BlockSpec((tm, D), lambda i: (i, 0)),
                   pl.BlockSpec((D, tm), lambda i: (0, i))],
        out_shape=[jax.ShapeDtypeStruct((S, D), F32), jax.ShapeDtypeStruct((S, D), BF16),
                   jax.ShapeDtypeStruct((D, S), BF16)],
        compiler_params=_params(32), exchange=exchange)


def _ffn_up(hn2, w_ff1, exchange=None):
    tm = 512
    nj = D_FF // 1024

    def body(hn_ref, w1_ref, r_ref, a_ref, at_ref):
        r = jnp.maximum(_dot(hn_ref[...], w1_ref[0]), 0.0)
        r_ref[...] = r.astype(BF16)
        a = r * r
        a_ref[...] = a.astype(BF16)
        at_ref[...] = a.T.astype(BF16)

    return _call(
        body, (hn2, w_ff1), grid=(nj, S // tm), name="ffn_up",
        in_specs=[pl.BlockSpec((tm, D), lambda j, i: (i, 0)), pl.BlockSpec((1, D, 1024), lambda j, i: (j, 0, 0))],
        out_specs=[pl.BlockSpec((tm, 1024), lambda j, i: (i, j)), pl.BlockSpec((tm, 1024), lambda j, i: (i, j)),
                   pl.BlockSpec((1024, tm), lambda j, i: (j, i))],
        out_shape=[jax.ShapeDtypeStruct((S, D_FF), BF16), jax.ShapeDtypeStruct((S, D_FF), BF16),
                   jax.ShapeDtypeStruct((D_FF, S), BF16)],
        compiler_params=_params(40, 2), exchange=exchange)


def _ffn_down(h1, a, w_ff2, exchange=None):
    tm = 1024
    nj = D_FF // 1024

    def body(h1_ref, a_ref, w2_ref, h2_ref, acc_ref):
        j = pl.program_id(1)
        part = _dot(a_ref[...], w2_ref[0])

        @pl.when(j == 0)
        def _():
            acc_ref[...] = part

        @pl.when(j > 0)
        def _():
            acc_ref[...] += part

        @pl.when(j == nj - 1)
        def _():
            h2_ref[...] = h1_ref[...] + acc_ref[...]

    return _call(
        body, (h1, a, w_ff2), grid=(S // tm, nj), name="ffn_down",
        in_specs=[pl.BlockSpec((tm, D), lambda i, j: (i, 0)), pl.BlockSpec((tm, 1024), lambda i, j: (i, j)),
                  pl.BlockSpec((1, 1024, D), lambda i, j: (j, 0, 0))],
        out_specs=[pl.BlockSpec((tm, D), lambda i, j: (i, 0))],
        out_shape=[jax.ShapeDtypeStruct((S, D), F32)],
        scratch_shapes=[pltpu.VMEM((tm, D), F32)],
        compiler_params=_params(48, 2), exchange=exchange)


def _tail(h2, p, target, w_gate, w_proj, final_gain):
    tm = 256
    steps = S // tm

    def body(h2_ref, p_ref, t_ref, wg_ref, wp_ref, gf_ref, dh2_ref, dwg_ref, dwp_ref, dgf_ref, loss_ref, dh2b_ref,
             dwp_acc):
        i = pl.program_id(0)
        h2 = h2_ref[...]
        h2b = h2.astype(BF16)
        pb = p_ref[...].astype(BF16)
        gate = jax.nn.sigmoid(_dot(h2b, wg_ref[...]))
        pp = jnp.concatenate([_dot(pb, wp_ref[j]) for j in range(N_CHIP)], axis=1)
        h3 = h2 + gate * pp
        r3 = _rms_scale(h3)
        xhat = h3 * r3
        gf = gf_ref[...]
        err = xhat * gf - t_ref[...]
        dy = err * (1.0 / D)
        dh3 = _rms_bwd(dy * gf, xhat, r3)
        dgp = (dh3 * pp * gate * (1.0 - gate)).astype(BF16)
        dpp = (dh3 * gate).astype(BF16)
        dh2 = dh3 + _dot_nt(dgp, wg_ref[...])
        dh2_ref[...] = dh2
        dh2b_ref[...] = dh2.astype(BF16)
        dwg = _dot_tn(h2b, dgp)
        dwp = _dot_tn(pb, dpp)
        dgf = jnp.sum(dy * xhat, axis=0, keepdims=True)
        sq = jnp.sum(jnp.sum(err * err, axis=1, keepdims=True), axis=0, keepdims=True)

        @pl.when(i == 0)
        def _():
            dwg_ref[...] = dwg
            dwp_acc[...] = dwp
            dgf_ref[...] = dgf
            loss_ref[...] = jnp.broadcast_to(sq, (8, 128))

        @pl.when(i > 0)
        def _():
            dwg_ref[...] += dwg
            dwp_acc[...] += dwp
            dgf_ref[...] += dgf
            loss_ref[...] += jnp.broadcast_to(sq, (8, 128))

        @pl.when(i == steps - 1)
        def _():
            for j in range(N_CHIP):
                dwp_ref[j] = dwp_acc[:, 256 * j:256 * (j + 1)]

    return _call(
        body, (h2, p, target, w_gate, w_proj, final_gain), grid=(steps,), name="tail",
        in_specs=[pl.BlockSpec((tm, D), lambda i: (i, 0)), pl.BlockSpec((tm, PLE), lambda i: (i, 0)),
                  pl.BlockSpec((tm, D), lambda i: (i, 0)), pl.BlockSpec((D, D), lambda i: (0, 0)),
                  pl.BlockSpec((N_CHIP, PLE, 256), lambda i: (0, 0, 0)), pl.BlockSpec((1, D), lambda i: (0, 0))],
        out_specs=[pl.BlockSpec((tm, D), lambda i: (i, 0)), pl.BlockSpec((D, D), lambda i: (0, 0)),
                   pl.BlockSpec((N_CHIP, PLE, 256), lambda i: (0, 0, 0)), pl.BlockSpec((1, D), lambda i: (0, 0)),
                   pl.BlockSpec((8, 128), lambda i: (0, 0)), pl.BlockSpec((tm, D), lambda i: (i, 0))],
        out_shape=[jax.ShapeDtypeStruct((S, D), F32), jax.ShapeDtypeStruct((D, D), F32),
                   jax.ShapeDtypeStruct((N_CHIP, PLE, 256), F32), jax.ShapeDtypeStruct((1, D), F32),
                   jax.ShapeDtypeStruct((8, 128), F32), jax.ShapeDtypeStruct((S, D), BF16)],
        scratch_shapes=[pltpu.VMEM((PLE, D), F32)],
        compiler_params=_params(48))[0]


def _ffn_bwd_weights(dh2b, hn2_t, r, a_t, w_ff2, exchange=None):
    tm = 1024
    nj = D_FF // 1024

    def body(dh2_ref, hnt_ref, r_ref, at_ref, w2_ref, df_ref, dw1_ref, dw2_ref):
        i = pl.program_id(1)
        dh2b = dh2_ref[...]
        da = _dot_nt(dh2b, w2_ref[0])
        df = (da * (2.0 * r_ref[...].astype(F32))).astype(BF16)
        df_ref[...] = df
        dw1 = _dot(hnt_ref[...], df)
        dw2 = _dot(at_ref[...], dh2b)

        @pl.when(i == 0)
        def _():
            dw1_ref[0] = dw1
            dw2_ref[0] = dw2

        @pl.when(i > 0)
        def _():
            dw1_ref[0] += dw1
            dw2_ref[0] += dw2

    return _call(
        body, (dh2b, hn2_t, r, a_t, w_ff2), grid=(nj, S // tm), name="ffn_bwd_weights",
        in_specs=[pl.BlockSpec((tm, D), lambda j, i: (i, 0)), pl.BlockSpec((D, tm), lambda j, i: (0, i)),
                  pl.BlockSpec((tm, 1024), lambda j, i: (i, j)), pl.BlockSpec((1024, tm), lambda j, i: (j, i)),
                  pl.BlockSpec((1, 1024, D), lambda j, i: (j, 0, 0))],
        out_specs=[pl.BlockSpec((tm, 1024), lambda j, i: (i, j)), pl.BlockSpec((1, D, 1024), lambda j, i: (j, 0, 0)),
                   pl.BlockSpec((1, 1024, D), lambda j, i: (j, 0, 0))],
        out_shape=[jax.ShapeDtypeStruct((S, D_FF), BF16), jax.ShapeDtypeStruct((nj, D, 1024), F32),
                   jax.ShapeDtypeStruct((nj, 1024, D), F32)],
        compiler_params=_params(60, 2), exchange=exchange)


def _ffn_bwd_input(df, w_ff1, dh2, h1, gain2, mix, w_out, exchange=None):
    tm = 512
    nj = D_FF // 1024
    steps = S // tm

    def body(df_ref, w1_ref, dh2_ref, h1_ref, g_ref, mix_ref, wo_ref, dh1_ref, dmix_ref, dwo_ref, dg_ref, acc_ref):
        i = pl.program_id(0)
        j = pl.program_id(1)
        part = _dot_nt(df_ref[...], w1_ref[0])

        @pl.when(j == 0)
        def _():
            acc_ref[...] = part

        @pl.when(j > 0)
        def _():
            acc_ref[...] += part

        @pl.when(j == nj - 1)
        def _():
            dhn = acc_ref[...]
            h1 = h1_ref[...]
            r2 = _rms_scale(h1)
            xhat = h1 * r2
            dh1 = dh2_ref[...] + _rms_bwd(dhn * g_ref[...], xhat, r2)
            dh1_ref[...] = dh1
            dh1b = dh1.astype(BF16)
            dmix_ref[...] = _dot_nt(dh1b, wo_ref[...])
            dwo = _dot_tn(mix_ref[...], dh1b)
            dg = jnp.sum(dhn * xhat, axis=0, keepdims=True)

            @pl.when(i == 0)
            def _():
                dwo_ref[...] = dwo
                dg_ref[...] = dg

            @pl.when(i > 0)
            def _():
                dwo_ref[...] += dwo
                dg_ref[...] += dg

    return _call(
        body, (df, w_ff1, dh2, h1, gain2, mix, w_out), grid=(steps, nj), name="ffn_bwd_input",
        in_specs=[pl.BlockSpec((tm, 1024), lambda i, j: (i, j)), pl.BlockSpec((1, D, 1024), lambda i, j: (j, 0, 0)),
                  pl.BlockSpec((tm, D), lambda i, j: (i, 0)), pl.BlockSpec((tm, D), lambda i, j: (i, 0)),
                  pl.BlockSpec((1, D), lambda i, j: (0, 0)), pl.BlockSpec((tm, D), lambda i, j: (i, 0)),
                  pl.BlockSpec((D, D), lambda i, j: (0, 0))],
        out_specs=[pl.BlockSpec((tm, D), lambda i, j: (i, 0)), pl.BlockSpec((tm, D), lambda i, j: (i, 0)),
                   pl.BlockSpec((D, D), lambda i, j: (0, 0)), pl.BlockSpec((1, D), lambda i, j: (0, 0))],
        out_shape=[jax.ShapeDtypeStruct((S, D), F32), jax.ShapeDtypeStruct((S, D), F32),
                   jax.ShapeDtypeStruct((D, D), F32), jax.ShapeDtypeStruct((1, D), F32)],
        scratch_shapes=[pltpu.VMEM((tm, D), F32)],
        compiler_params=_params(56, 2), exchange=exchange)


def _mixer_bwd(z, dmix, v_gain, w_spatial, b_spatial_t, sinks, rel_table, bucket, exchange=None):
    def body(z_ref, kvp_ref, dm_ref, gain_ref, ws_ref, bt_ref, sink_ref, table_ref, bucket_ref,
             dz_ref, dws_ref, db_ref, dgain_ref, dsink_ref, drel_ref,
             bias_ref, wt_ref, wtt_ref, dbias_ref, dsv_ref, carry_ref):
        n = pl.program_id(0)

        @pl.when(n == 0)
        def _():
            _fill_bias(bucket_ref, table_ref, bias_ref)
            _fill_tril(ws_ref, wt_ref, wtt_ref)
            dbias_ref[...] = jnp.zeros_like(dbias_ref)
            dsv_ref[...] = jnp.zeros_like(dsv_ref)
            dws_ref[...] = jnp.zeros_like(dws_ref)
            dgain_ref[...] = jnp.zeros_like(dgain_ref)
            dsink_ref[...] = jnp.zeros_like(dsink_ref)

        rows = pl.ds(pl.multiple_of(n * CHUNK, CHUNK), CHUNK)

        zuv = z_ref[:, :1024]
        cdf, t = _gelu_parts(zuv)
        guv = zuv * cdf
        dgelu = cdf + zuv * (0.5 * (1.0 - t * t)) * (GELU_C * (1.0 + 3.0 * 0.044715 * (zuv * zuv)))
        for g in range(N_GROUP):
            lo, hi = 128 * g, 128 * (g + 1)
            u = guv[:, lo:hi]
            vg = guv[:, 512 + lo:512 + hi]
            rr = _rms_scale(vg)
            vhat = vg * rr
            gain = gain_ref[:, lo:hi]
            vnb = (vhat * gain).astype(BF16)
            sv = _dot(wt_ref[g], vnb) + bt_ref[:, g:g + 1]
            da = dm_ref[:, lo:hi]
            dsv = da * u
            dsvb = dsv.astype(BF16)
            dsv_ref[g] += dsv
            dws_ref[g] += _dot_nt(dsvb, vnb)
            dvn = _dot(wtt_ref[g], dsvb)
            dgain_ref[:, lo:hi] += jnp.sum(dvn * vhat, axis=0, keepdims=True)
            dvg = _rms_bwd(dvn * gain, vhat, rr)
            dz_ref[rows, lo:hi] = (da * sv * dgelu[:, lo:hi]).astype(BF16)
            dz_ref[rows, 512 + lo:512 + hi] = (dvg * dgelu[:, 512 + lo:512 + hi]).astype(BF16)

        k_same, k_swap, v_same, v_swap = _kv_layouts(kvp_ref[...], z_ref[:, 1536:1792])
        valid = _band_mask(n)
        lane_half = lax.broadcasted_iota(jnp.int32, (1, 128), 1) // 64
        zero = jnp.zeros((2 * CHUNK, 128), F32)
        dk_same, dk_swap, dv_same, dv_swap = zero, zero, zero, zero
        for pair in range(N_HEAD // 2):
            cols = slice(1024 + 128 * pair, 1024 + 128 * (pair + 1))
            qq = z_ref[:, cols]
            do_pair = dm_ref[:, 512 + 128 * pair:512 + 128 * (pair + 1)]
            dq = jnp.zeros((CHUNK, 128), F32)
            for pos in range(2):
                h = 2 * pair + pos
                _, _, same = _head_place(h)
                on_half = lane_half == pos
                qm = jnp.where(on_half, qq, 0.0).astype(BF16)
                k_use = k_same if same else k_swap
                v_use = v_same if same else v_swap
                p, p_sink = _softmax_sink(qm, k_use, bias_ref[h], sink_ref[h], valid)
                dom = jnp.where(on_half, do_pair, 0.0).astype(BF16)
                dp = _dot_nt(dom, v_use)
                dsum = jnp.sum(p * dp, axis=-1, keepdims=True)
                ds = p * (dp - dsum)
                dbias_ref[h] += ds
                dsink_ref[h:h + 1, :] += jnp.broadcast_to(jnp.sum(-p_sink * dsum, axis=0, keepdims=True), (1, 128))
                dsb = ds.astype(BF16)
                dq = dq + jnp.where(on_half, _dot(dsb, k_use), 0.0)
                dk_h = _dot_tn(dsb, qm)
                dv_h = _dot_tn(p.astype(BF16), dom)
                if same:
                    dk_same, dv_same = dk_same + dk_h, dv_same + dv_h
                else:
                    dk_swap, dv_swap = dk_swap + dk_h, dv_swap + dv_h
            dz_ref[rows, cols] = (dq * QK_SCALE).astype(BF16)
        dk = (dk_same + pltpu.roll(dk_swap, 64, axis=1)) * QK_SCALE
        dv = dv_same + pltpu.roll(dv_swap, 64, axis=1)
        dkv = jnp.concatenate([dk, dv], axis=1)

        @pl.when(n > 0)
        def _():
            prev_rows = pl.ds(pl.multiple_of((n - 1) * CHUNK, CHUNK), CHUNK)
            dz_ref[prev_rows, 1536:1792] = (carry_ref[...] + dkv[:CHUNK]).astype(BF16)

        carry_ref[...] = dkv[CHUNK:]

        @pl.when(n == N_BLOCK - 1)
        def _():
            dz_ref[rows, 1536:1792] = dkv[CHUNK:].astype(BF16)
            r = lax.broadcasted_iota(jnp.int32, (CHUNK, CHUNK), 0)
            c = lax.broadcasted_iota(jnp.int32, (CHUNK, CHUNK), 1)
            for g in range(N_GROUP):
                dws_ref[g] = jnp.where(c <= r, dws_ref[g], 0.0)
                db_ref[g] = jnp.sum(dsv_ref[g], axis=1, keepdims=True)
            bucket = bucket_ref[...]
            for h in range(N_HEAD):
                dbh = dbias_ref[h]
                per_bucket = [jnp.sum(jnp.where(bucket == b, dbh, 0.0), axis=0, keepdims=True) for b in range(N_BUCKET)]
                drel_ref[h] = jnp.sum(jnp.concatenate(per_bucket, axis=0), axis=1, keepdims=True)

    return _call(
        body, (z, z, dmix, v_gain, w_spatial, b_spatial_t, sinks, rel_table, bucket), grid=(N_BLOCK,), name="mixer_bwd",
        in_specs=[pl.BlockSpec((CHUNK, D_IN), lambda n: (n, 0)),
                  pl.BlockSpec((CHUNK, 256), lambda n: (jnp.maximum(n - 1, 0), 6)),
                  pl.BlockSpec((CHUNK, D), lambda n: (n, 0)),
                  pl.BlockSpec((1, 512), lambda n: (0, 0)),
                  pl.BlockSpec((N_GROUP, CHUNK, CHUNK), lambda n: (0, 0, 0)),
                  pl.BlockSpec((CHUNK, N_GROUP), lambda n: (0, 0)),
                  pl.BlockSpec(memory_space=pltpu.SMEM),
                  pl.BlockSpec(memory_space=pltpu.SMEM),
                  pl.BlockSpec((CHUNK, 2 * CHUNK), lambda n: (0, 0))],
        out_specs=[pl.BlockSpec((S, D_IN), lambda n: (0, 0)),
                   pl.BlockSpec((N_GROUP, CHUNK, CHUNK), lambda n: (0, 0, 0)),
                   pl.BlockSpec((N_GROUP, CHUNK, 1), lambda n: (0, 0, 0)),
                   pl.BlockSpec((1, 512), lambda n: (0, 0)),
                   pl.BlockSpec((N_HEAD, 128), lambda n: (0, 0)),
                   pl.BlockSpec((N_HEAD, N_BUCKET, 1), lambda n: (0, 0, 0))],
        out_shape=[jax.ShapeDtypeStruct((S, D_IN), BF16), jax.ShapeDtypeStruct((N_GROUP, CHUNK, CHUNK), F32),
                   jax.ShapeDtypeStruct((N_GROUP, CHUNK, 1), F32), jax.ShapeDtypeStruct((1, 512), F32),
                   jax.ShapeDtypeStruct((N_HEAD, 128), F32), jax.ShapeDtypeStruct((N_HEAD, N_BUCKET, 1), F32)],
        scratch_shapes=[pltpu.VMEM((N_HEAD, CHUNK, 2 * CHUNK), F32), pltpu.VMEM((N_GROUP, CHUNK, CHUNK), BF16),
                        pltpu.VMEM((N_GROUP, CHUNK, CHUNK), BF16), pltpu.VMEM((N_HEAD, CHUNK, 2 * CHUNK), F32),
                        pltpu.VMEM((N_GROUP, CHUNK, CHUNK), F32), pltpu.VMEM((CHUNK, 256), F32)],
        compiler_params=_params(48), exchange=exchange)


def _in_bwd_weight(dz, hn1, exchange=None):
    tm = 512

    def body(dz_ref, hn_ref, dw_ref):
        i = pl.program_id(0)
        dw = _dot_tn(dz_ref[...], hn_ref[...])

        @pl.when(i == 0)
        def _():
            dw_ref[...] = dw

        @pl.when(i > 0)
        def _():
            dw_ref[...] += dw

    return _call(
        body, (dz, hn1), grid=(S // tm,), name="in_bwd_weight",
        in_specs=[pl.BlockSpec((tm, D_IN), lambda i: (i, 0)), pl.BlockSpec((tm, D), lambda i: (i, 0))],
        out_specs=[pl.BlockSpec((D_IN, D), lambda i: (0, 0))],
        out_shape=[jax.ShapeDtypeStruct((D_IN, D), F32)],
        compiler_params=_params(40), exchange=exchange)


def _in_bwd_input(dz, w_in_t, x, dh1, gain1, exchange=None):
    tm = 512

    def body(dz_ref, w_ref, x_ref, dh1_ref, g_ref, dx_ref, dg_ref):
        i = pl.program_id(0)
        dhn = _dot(dz_ref[...], w_ref[...])
        xv = x_ref[...]
        r1 = _rms_scale(xv)
        xhat = xv * r1
        dx_ref[...] = dh1_ref[...] + _rms_bwd(dhn * g_ref[...], xhat, r1)
        dg = jnp.sum(dhn * xhat, axis=0, keepdims=True)

        @pl.when(i == 0)
        def _():
            dg_ref[...] = dg

        @pl.when(i > 0)
        def _():
            dg_ref[...] += dg

    return _call(
        body, (dz, w_in_t, x, dh1, gain1), grid=(S // tm,), name="in_bwd_input",
        in_specs=[pl.BlockSpec((tm, D_IN), lambda i: (i, 0)), pl.BlockSpec((D_IN, D), lambda i: (0, 0)),
                  pl.BlockSpec((tm, D), lambda i: (i, 0)), pl.BlockSpec((tm, D), lambda i: (i, 0)),
                  pl.BlockSpec((1, D), lambda i: (0, 0))],
        out_specs=[pl.BlockSpec((tm, D), lambda i: (i, 0)), pl.BlockSpec((1, D), lambda i: (0, 0))],
        out_shape=[jax.ShapeDtypeStruct((S, D), F32), jax.ShapeDtypeStruct((1, D), F32)],
        compiler_params=_params(48), exchange=exchange)


def _rel_bucket():
    a = jnp.arange(CHUNK)[:, None]
    j = jnp.arange(2 * CHUNK)[None, :]
    n = jnp.maximum(CHUNK + a - j, 0)
    max_exact = N_BUCKET // 2
    nf = jnp.maximum(n, 1).astype(jnp.float32)
    large = max_exact + (jnp.log(nf / max_exact) / math.log(CHUNK / max_exact) * (N_BUCKET - max_exact)).astype(jnp.int32)
    large = jnp.minimum(large, N_BUCKET - 1)
    return jnp.where(n < max_exact, n, large).astype(jnp.int32)


def _step(x, p, target, small, bufs, place):
    bucket = _rel_bucket()
    sinks = small["attn_sinks"].reshape(N_HEAD)
    b_t = jnp.transpose(small["b_spatial"].reshape(N_GROUP, CHUNK))
    ws = small["w_spatial"].reshape(N_GROUP, CHUNK, CHUNK)
    gain1, gain2 = small["norm1_gain"], small["norm2_gain"]
    v_gain = small["gmlp_v_gain"]
    final_gain = small["final_gain"].reshape(1, D)
    table = small["rel_bias_table"]
    bufs = dict(bufs)

    def gather(*names):
        return _Gather([bufs[n] for n in names])

    def took(names, got):
        bufs.update(zip(names, got))

    took(["w_in"], _gather_weights([bufs["w_in"]]))
    w_in_t = _whole(bufs["w_in"]).reshape(D_IN, D)
    (z, hn1), got = _in_proj(x, gain1, w_in_t, gather("w_out"))
    took(["w_out"], got)
    (mix,), got = _mixer_fwd(z, v_gain, ws, b_t, sinks, table, bucket, gather("w_ff1"))
    took(["w_ff1"], got)
    w_out = _whole(bufs["w_out"]).reshape(D, D)
    (h1, hn2, hn2_t), _ = _out_proj(x, mix, w_out, gain2)
    w_ff1 = _whole(bufs["w_ff1"])
    (r, a, a_t), got = _ffn_up(hn2, w_ff1, gather("w_ff2"))
    took(["w_ff2"], got)
    w_ff2 = _whole(bufs["w_ff2"])
    (h2,), got = _ffn_down(h1, a, w_ff2, gather("w_ple_gate", "w_ple_proj"))
    took(["w_ple_gate", "w_ple_proj"], got)
    dh2, d_gate, d_proj, d_final, sq, dh2b = _tail(h2, p, target, _whole(bufs["w_ple_gate"]).reshape(D, D),
                                                   _whole(bufs["w_ple_proj"]), final_gain)

    def pair_sums(halves, from_sibling):
        sums, landing = zip(*[_pair_sum(g, o, place) for g, o in zip(halves, from_sibling)])
        return list(sums), list(landing)

    landed = {}
    halves = [_halves(d_gate.reshape(N_CHIP, 256, D)), _halves(d_proj)]
    ex = _ChipExchange(*pair_sums(halves, _sibling_exchange(halves)))
    (df, d_ff1, d_ff2), got = _ffn_bwd_weights(dh2b, hn2_t, r, a_t, w_ff2, ex)
    landed.update(zip(["w_ple_gate", "w_ple_proj"], got))
    halves = [_halves(d_ff1), _halves(d_ff2)]
    (dh1, dmix, d_out, d_gain2), got = _ffn_bwd_input(df, w_ff1, dh2, h1, gain2, mix, w_out, _SiblingExchange(halves))
    ex = _ChipExchange(*pair_sums(halves, got))
    (dz, d_ws, d_b, d_vgain, d_sink, d_rel), got = _mixer_bwd(z, dmix, v_gain, ws, b_t, sinks, table, bucket, ex)
    landed.update(zip(["w_ff1", "w_ff2"], got))
    small_grads = {
        "gmlp_v_gain": d_vgain, "w_spatial": d_ws.reshape(1, N_GROUP, CHUNK, CHUNK),
        "b_spatial": d_b.reshape(1, N_GROUP, CHUNK), "attn_sinks": d_sink[:, 0].reshape(1, N_HEAD),
        "rel_bias_table": jnp.transpose(d_rel.reshape(N_HEAD, N_BUCKET)), "norm2_gain": d_gain2,
        "final_gain": d_final.reshape(D),
    }
    halves = [_halves(d_out.reshape(N_CHIP, 256, D))]
    (d_in_t,), got = _in_bwd_weight(dz, hn1, _ChipExchange(*pair_sums(halves, _sibling_exchange(halves))))
    landed.update(zip(["w_out"], got))
    halves = [_halves(d_in_t.reshape(N_CHIP, 448, D))]
    ex = _ChipExchange(*pair_sums(halves, _sibling_exchange(halves)))
    (dx, small_grads["norm1_gain"]), got = _in_bwd_input(dz, w_in_t, x, dh1, gain1, ex)
    landed.update(zip(["w_in"], got))
    return dx, landed, small_grads, sq


HBM_SPEC = pl.BlockSpec(memory_space=pltpu.HBM)
VMEM_SPEC = pl.BlockSpec(memory_space=pltpu.VMEM)


def _mesh_place():
    x, y, c = lax.axis_index("x"), lax.axis_index("y"), lax.axis_index("c")
    others = [(1 - x, y), (x, 1 - y), (1 - x, 1 - y)]
    return x, y, c, others


def _remote(src, dst, send_sem, recv_sem, device):
    return pltpu.make_async_remote_copy(src_ref=src, dst_ref=dst, send_sem=send_sem, recv_sem=recv_sem,
                                        device_id=device, device_id_type=MESH)


def _hbm_like(a, shape=None, dtype=None):
    return pltpu.HBM(a.shape if shape is None else shape, a.dtype if dtype is None else dtype)


def _gather_start(bufs, send_sems, recv_sems):
    x, y, c, others = _mesh_place()
    me = 2 * x + y
    for w, buf in enumerate(bufs):
        for k, (ox, oy) in enumerate(others):
            mine = buf.at[me, c]
            _remote(mine, mine, send_sems.at[w, k], recv_sems.at[w, k], (ox, oy, c)).start()


def _gather_finish(bufs, send_sems, recv_sems):
    x, y, c, others = _mesh_place()
    me = 2 * x + y
    sibling = (x, y, 1 - c)
    idx = [2 * ox + oy for ox, oy in others]
    for w, buf in enumerate(bufs):
        for k in range(3):
            landed = buf.at[idx[k], c]
            _remote(landed, landed, send_sems.at[w, k], recv_sems.at[w, k], sibling).wait_recv()
            _remote(landed, landed, send_sems.at[w, 3 + k], recv_sems.at[w, 3 + k], sibling).start()
    for w, buf in enumerate(bufs):
        for k in range(3):
            landed = buf.at[idx[k], 1 - c]
            _remote(landed, landed, send_sems.at[w, 3 + k], recv_sems.at[w, 3 + k], sibling).wait_recv()
    for w, buf in enumerate(bufs):
        for k in range(3):
            mine, passed = buf.at[me, c], buf.at[idx[k], c]
            _remote(mine, mine, send_sems.at[w, k], recv_sems.at[w, k], sibling).wait_send()
            _remote(passed, passed, send_sems.at[w, 3 + k], recv_sems.at[w, 3 + k], sibling).wait_send()


def _gather_sems(n):
    return [pltpu.SemaphoreType.DMA((n, 6)), pltpu.SemaphoreType.DMA((n, 6))]


def _gather_weights(bufs):
    n = len(bufs)

    def body(*refs):
        outs = refs[n:2 * n]
        send_sems, recv_sems = refs[2 * n:]
        _gather_start(outs, send_sems, recv_sems)
        _gather_finish(outs, send_sems, recv_sems)

    return pl.pallas_call(
        body, name="gather_weights",
        in_specs=[HBM_SPEC] * n, out_specs=[HBM_SPEC] * n,
        out_shape=[_hbm_like(b) for b in bufs],
        input_output_aliases={w: w for w in range(n)},
        scratch_shapes=_gather_sems(n),
    )(*bufs)


def _sibling_copies(grads, landing, send_sems, recv_sems):
    x, y, c, _ = _mesh_place()
    return [_remote(grads[w].at[j, 1 - c], landing[w].at[j], send_sems.at[w, j], recv_sems.at[w, j], (x, y, 1 - c))
            for w in range(len(grads)) for j in range(N_CHIP)]


def _sibling_exchange_start(grads, landing, send_sems, recv_sems):
    for cp in _sibling_copies(grads, landing, send_sems, recv_sems):
        cp.start()


def _sibling_exchange_finish(grads, landing, send_sems, recv_sems):
    copies = _sibling_copies(grads, landing, send_sems, recv_sems)
    for cp in copies:
        cp.wait_recv()
    for cp in copies:
        cp.wait_send()


def _sibling_exchange_sems(n):
    return [pltpu.SemaphoreType.DMA((n, N_CHIP)), pltpu.SemaphoreType.DMA((n, N_CHIP))]


def _sibling_exchange(grads):
    n = len(grads)

    def body(*refs):
        ins, outs = refs[:n], refs[n:2 * n]
        _sibling_exchange_start(ins, outs, *refs[2 * n:])
        _sibling_exchange_finish(ins, outs, *refs[2 * n:])

    return pl.pallas_call(
        body, name="sibling_exchange",
        in_specs=[HBM_SPEC] * n, out_specs=[HBM_SPEC] * n,
        out_shape=[_hbm_like(g, (N_CHIP,) + g.shape[2:]) for g in grads],
        scratch_shapes=_sibling_exchange_sems(n),
    )(*[_in_hbm(g) for g in grads])


def _chip_exchange_start(sums, landing, send_sems, recv_sems):
    x, y, c, others = _mesh_place()
    me = 2 * x + y
    for w in range(len(sums)):
        for k, (ox, oy) in enumerate(others):
            _remote(sums[w].at[2 * ox + oy], landing[w].at[me], send_sems.at[w, k], recv_sems.at[w, k],
                    (ox, oy, c)).start()


def _chip_exchange_finish(sums, landing, send_sems, recv_sems):
    x, y, c, others = _mesh_place()
    for w in range(len(sums)):
        for k, (ox, oy) in enumerate(others):
            piece = landing[w].at[2 * ox + oy]
            _remote(piece, piece, send_sems.at[w, k], recv_sems.at[w, k], (x, y, c)).wait_recv()
    for w in range(len(sums)):
        for k, (ox, oy) in enumerate(others):
            piece = sums[w].at[2 * ox + oy]
            _remote(piece, piece, send_sems.at[w, k], recv_sems.at[w, k], (x, y, c)).wait_send()


def _chip_exchange_sems(n):
    return [pltpu.SemaphoreType.DMA((n, 3)), pltpu.SemaphoreType.DMA((n, 3))]


def _chip_exchange(sums, landing):
    n = len(sums)

    def body(*refs):
        ins, outs = refs[:n], refs[2 * n:3 * n]
        send_sems, recv_sems = refs[3 * n:]
        _chip_exchange_start(ins, outs, send_sems, recv_sems)
        _chip_exchange_finish(ins, outs, send_sems, recv_sems)

    return pl.pallas_call(
        body, name="chip_exchange",
        in_specs=[HBM_SPEC] * (2 * n), out_specs=[HBM_SPEC] * n,
        out_shape=[_hbm_like(b) for b in landing],
        input_output_aliases={n + w: w for w in range(n)},
        scratch_shapes=_chip_exchange_sems(n),
    )(*sums, *landing)


def _sibling_allgather(bufs, also):
    n = len(bufs)
    k_in, k_out = len(also.operands), also.n_out

    def body(*refs):
        ex_ins, refs = refs[n:n + k_in], refs[n + k_in:]
        outs, refs = refs[:n], refs[n:]
        ex_outs, refs = refs[:k_out], refs[k_out:]
        send_sems, recv_sems, ex_sems = refs[0], refs[1], refs[2:]
        x, y, c, _ = _mesh_place()
        sibling = (x, y, 1 - c)
        also.start(ex_ins, ex_outs, ex_sems)
        sends = [_remote(outs[w].at[c], outs[w].at[c], send_sems.at[w], recv_sems.at[w], sibling) for w in range(n)]
        for cp in sends:
            cp.start()
        for w in range(n):
            landed = outs[w].at[1 - c]
            _remote(landed, landed, send_sems.at[w], recv_sems.at[w], sibling).wait_recv()
        for cp in sends:
            cp.wait_send()
        also.finish(ex_ins, ex_outs, ex_sems)

    res = pl.pallas_call(
        body, name="sibling_allgather",
        in_specs=[HBM_SPEC] * (n + k_in), out_specs=[HBM_SPEC] * (n + k_out),
        out_shape=[_hbm_like(b) for b in bufs] + also.out_shape,
        input_output_aliases={**{w: w for w in range(n)}, **{n + i: n + o for i, o in also.aliases.items()}},
        scratch_shapes=[pltpu.SemaphoreType.DMA((n,)), pltpu.SemaphoreType.DMA((n,))] + also.sems,
    )(*bufs, *[_in_hbm(o) for o in also.operands])
    return list(res[:n]), list(res[n:])


def _pair_sum(grad, other, place):
    _, _, h, cols = grad.shape
    tr = _row_tile(h)

    def body(place_ref, g_ref, o_ref, sums_ref, own_ref):
        s = (g_ref[0, 0] + o_ref[0]).astype(BF16)
        sums_ref[0] = s

        @pl.when(pl.program_id(1) == place_ref[0])
        def _():
            own_ref[0] = s

    return pl.pallas_call(
        body, name="pair_sum",
        grid_spec=pltpu.PrefetchScalarGridSpec(
            num_scalar_prefetch=1, grid=(h // tr, N_CHIP),
            in_specs=[pl.BlockSpec((1, 1, tr, cols), lambda r, j, place_ref: (j, place_ref[1], r, 0)),
                      pl.BlockSpec((1, tr, cols), lambda r, j, place_ref: (j, r, 0))],
            out_specs=[pl.BlockSpec((1, tr, cols), lambda r, j, place_ref: (j, r, 0)),
                       pl.BlockSpec((1, tr, cols), lambda r, j, place_ref: (place_ref[0], r, 0))]),
        out_shape=[pltpu.HBM((N_CHIP, h, cols), BF16)] * 2,
        compiler_params=_params(16, 2),
    )(place, _in_hbm(grad), _in_hbm(other))


def _chip_sum(parts, place):
    _, h, cols = parts.shape
    tr = _row_tile(h)

    def body(place_ref, p_ref, out_ref):
        out_ref[0] = ((p_ref[0].astype(F32) + p_ref[1].astype(F32)) + p_ref[2].astype(F32)) + p_ref[3].astype(F32)

    return pl.pallas_call(
        body, name="chip_sum",
        grid_spec=pltpu.PrefetchScalarGridSpec(
            num_scalar_prefetch=1, grid=(h // tr,),
            in_specs=[pl.BlockSpec((N_CHIP, tr, cols), lambda r, place_ref: (0, r, 0))],
            out_specs=pl.BlockSpec((1, tr, cols), lambda r, place_ref: (place_ref[1], r, 0))),
        out_shape=pltpu.HBM((2, h, cols), F32),
        compiler_params=_params(16),
    )(place, _in_hbm(parts))


def _adamw_math(w, g, m, v):
    m = ADAM_B1 * m + (1.0 - ADAM_B1) * g
    v = ADAM_B2 * v + (1.0 - ADAM_B2) * (g * g)
    m_hat = m / (1.0 - ADAM_B1 ** ADAM_STEP)
    v_hat = v / (1.0 - ADAM_B2 ** ADAM_STEP)
    delta = -ADAM_LR * (m_hat / (jnp.sqrt(v_hat) + ADAM_EPS) + ADAM_WD * w)
    return delta, m, v


def _adamw(w, g, m, v, exchange=None):
    rows, cols = w.shape
    tr = _row_tile(rows)

    def body(w_ref, g_ref, m_ref, v_ref, d_ref, nm_ref, nv_ref, g_out_ref):
        g = g_ref[...]
        d_ref[...], nm_ref[...], nv_ref[...] = _adamw_math(w_ref[...], g, m_ref[...], v_ref[...])
        g_out_ref[...] = g

    spec = pl.BlockSpec((tr, cols), lambda r: (r, 0))
    return _call(
        body, (w, g, m, v), grid=(rows // tr,), name="adamw",
        in_specs=[spec] * 4, out_specs=[spec] * 4,
        out_shape=[jax.ShapeDtypeStruct((rows, cols), F32)] * 4,
        compiler_params=_params(32), exchange=exchange)


SMALL_NAMES = ("norm1_gain", "gmlp_v_gain", "w_spatial", "b_spatial", "attn_sinks", "rel_bias_table", "norm2_gain",
               "final_gain")
PACK_TILE = 8 * 128


def _pack_small(arrays):
    parts = []
    for a in arrays:
        flat = a.reshape(-1)
        rows = -(-flat.shape[0] // PACK_TILE) * 8
        parts.append(jnp.pad(flat, (0, rows * 128 - flat.shape[0])).reshape(rows, 128))
    return jnp.concatenate(parts, axis=0)


def _unpack_small(packed, like):
    out, row = [], 0
    for a in like:
        size = math.prod(a.shape)
        rows = -(-size // PACK_TILE) * 8
        out.append(packed[row:row + rows].reshape(-1)[:size].reshape(a.shape))
        row += rows
    return out


def _small_update(gathered, w, m, v):
    rows = gathered.shape[1]

    def body(g_ref, w_ref, m_ref, v_ref, tot_ref, d_ref, nm_ref, nv_ref):
        total = g_ref[0].astype(F32)
        for dev in range(1, 8):
            total = total + g_ref[dev].astype(F32)
        tot_ref[...] = total
        d_ref[...], nm_ref[...], nv_ref[...] = _adamw_math(w_ref[...], total, m_ref[...], v_ref[...])

    return pl.pallas_call(
        body, name="small_update",
        in_specs=[VMEM_SPEC] * 4, out_specs=[VMEM_SPEC] * 4,
        out_shape=[jax.ShapeDtypeStruct((rows, 128), F32)] * 4,
        compiler_params=pltpu.CompilerParams(vmem_limit_bytes=24 * MIB),
    )(gathered, w, m, v)


def _halves(a):
    return a.reshape(a.shape[:-2] + (2, a.shape[-2] // 2, a.shape[-1]))


def _whole(a):
    return a.reshape(a.shape[:-3] + (2 * a.shape[-2], a.shape[-1]))


def kernel(x, p, norm1_gain, w_in, gmlp_v_gain, w_spatial, b_spatial, attn_sinks, rel_bias_table, w_out, norm2_gain, w_ff1, w_ff2, w_ple_proj, w_ple_gate, final_gain, loss_target, m_norm1_gain, m_w_in, m_gmlp_v_gain, m_w_spatial, m_b_spatial, m_attn_sinks, m_rel_bias_table, m_w_out, m_norm2_gain, m_w_ff1, m_w_ff2, m_w_ple_proj, m_w_ple_gate, m_final_gain, v_norm1_gain, v_w_in, v_gmlp_v_gain, v_w_spatial, v_b_spatial, v_attn_sinks, v_rel_bias_table, v_w_out, v_norm2_gain, v_w_ff1, v_w_ff2, v_w_ple_proj, v_w_ple_gate, v_final_gain):
    given = dict(locals())
    small = {n: given[n] for n in SMALL_NAMES}
    chip = 2 * lax.axis_index("x") + lax.axis_index("y")
    place = jnp.stack([chip, lax.axis_index("c")]).astype(jnp.int32)

    big_names = ("w_in", "w_out", "w_ff1", "w_ff2", "w_ple_proj", "w_ple_gate")
    shards = {n: given[n][0] for n in big_names}
    travel = dict(shards, w_in=jnp.transpose(shards["w_in"]))
    bufs = {n: _cast_shard(travel[n], place[:1]) for n in big_names}
    dx, landed, small_grads, sq = _step(x[0], p[0, 0], loss_target[0], small, bufs, place)

    out_grad, out_delta, out_m, out_v = {}, {}, {}, {}

    def update(n, g, exchange=None):
        to = jnp.transpose if n == "w_in" else (lambda a: a)
        (delta, new_m, new_v, g_out), got = _adamw(to(shards[n]), g, to(given["m_" + n][0]), to(given["v_" + n][0]),
                                                   exchange)
        out_grad[n], out_delta[n], out_m[n], out_v[n] = [to(a)[None] for a in (g_out, delta, new_m, new_v)]
        return got

    spare = jnp.zeros((8, 128), F32)
    small_packed = _pack_small([small_grads[n] for n in SMALL_NAMES] + [spare]).astype(BF16)
    reduced, (small_gathered, sq_gathered) = _sibling_allgather(
        [_chip_sum(landed[n], place) for n in big_names], _Both(_GatherAll(small_packed), _GatherAll(sq)))
    for n, r in zip(big_names, reduced):
        update(n, _whole(r))

    like = [given[n] for n in SMALL_NAMES] + [spare]
    packed = _small_update(small_gathered, *[_pack_small([given[pre + n] for n in SMALL_NAMES] + [spare])
                                             for pre in ("", "m_", "v_")])
    for res, out in zip(packed, (out_grad, out_delta, out_m, out_v)):
        out.update(zip(SMALL_NAMES, _unpack_small(res, like)))
    loss = 0.5 * jnp.sum(sq_gathered[:, 0, 0]) / D

    order = ("norm1_gain", "w_in", "gmlp_v_gain", "w_spatial", "b_spatial", "attn_sinks", "rel_bias_table", "w_out",
             "norm2_gain", "w_ff1", "w_ff2", "w_ple_proj", "w_ple_gate", "final_gain")
    return (loss, dx[None], *[out_grad[n] for n in order], *[out_delta[n] for n in order],
            *[out_m[n] for n in order], *[out_v[n] for n in order])
```

```python
import functools
import math

import jax
import jax.numpy as jnp
from jax import lax
from jax.experimental import pallas as pl
from jax.experimental.pallas import tpu as pltpu

S = 2048
D = 1024
D_IN = 1792
D_FF = 4096
PLE = 256
N_CHIP = 4
N_GROUP = 4
CHUNK = 128
N_HEAD = 8
N_BLOCK = S // CHUNK
N_BUCKET = 32
EPS = 1e-6
NEG_INF = -1e30
QK_SCALE = 0.125
GELU_C = math.sqrt(2.0 / math.pi)

ADAM_LR = 0.001
ADAM_B1 = 0.9
ADAM_B2 = 0.999
ADAM_EPS = 1e-08
ADAM_WD = 0.01
ADAM_STEP = 10

F32 = jnp.float32
BF16 = jnp.bfloat16
MIB = 1024 * 1024
MESH = pl.DeviceIdType.MESH

NT = (((1,), (1,)), ((), ()))
TN = (((0,), (0,)), ((), ()))


def _dot(a, b):
    return jnp.dot(a, b, preferred_element_type=F32)


def _dot_nt(a, b):
    return lax.dot_general(a, b, NT, preferred_element_type=F32)


def _dot_tn(a, b):
    return lax.dot_general(a, b, TN, preferred_element_type=F32)


def _params(vmem_mib, n_axes=1):
    return pltpu.CompilerParams(dimension_semantics=("arbitrary",) * n_axes, vmem_limit_bytes=vmem_mib * MIB)


def _rms_scale(v):
    return lax.rsqrt(jnp.mean(v * v, axis=-1, keepdims=True) + EPS)


def _rms_bwd(dy_gain, xhat, r):
    return r * (dy_gain - xhat * jnp.mean(dy_gain * xhat, axis=-1, keepdims=True))


class _Gather:
    def __init__(self, bufs):
        self.operands = list(bufs)
        self.n_out = len(self.operands)
        self.out_shape = [_hbm_like(b) for b in bufs]
        self.aliases = {w: w for w in range(self.n_out)}
        self.sems = _gather_sems(self.n_out)

    def start(self, ins, outs, sems):
        _gather_start(outs, *sems)

    def finish(self, ins, outs, sems):
        _gather_finish(outs, *sems)


class _Relay(_Gather):
    def start(self, ins, outs, sems):
        _relay_start(outs, *sems)

    def finish(self, ins, outs, sems):
        _relay_finish(outs, *sems)


class _ChipExchange:
    def __init__(self, sums, landing):
        self.n_out = len(landing)
        self.operands = list(sums) + list(landing)
        self.out_shape = [_hbm_like(b) for b in landing]
        self.aliases = {self.n_out + w: w for w in range(self.n_out)}
        self.sems = _chip_exchange_sems(self.n_out)

    def start(self, ins, outs, sems):
        _chip_exchange_start(ins[:self.n_out], outs, *sems)

    def finish(self, ins, outs, sems):
        _chip_exchange_finish(ins[:self.n_out], outs, *sems)


class _GatherAll:
    def __init__(self, packed):
        self.operands = [packed]
        self.n_out = 1
        self.out_shape = [_hbm_like(packed, (8,) + packed.shape)]
        self.aliases = {}
        self.sems = [pltpu.SemaphoreType.DMA((8,)), pltpu.SemaphoreType.DMA((8,))]

    def _copies(self, ins, outs, sems):
        x, y, c, _ = _mesh_place()
        me = 4 * x + 2 * y + c
        send_sems, recv_sems = sems
        copies = []
        for k in range(1, 8):
            peer = (1 - x if k // 4 else x, 1 - y if (k // 2) % 2 else y, 1 - c if k % 2 else c)
            src = 4 * peer[0] + 2 * peer[1] + peer[2]
            copies.append((_remote(ins[0], outs[0].at[me], send_sems.at[k], recv_sems.at[k], peer), outs[0].at[src]))
        own = pltpu.make_async_copy(ins[0], outs[0].at[me], send_sems.at[0])
        return own, copies

    def start(self, ins, outs, sems):
        own, copies = self._copies(ins, outs, sems)
        own.start()
        for cp, _ in copies:
            cp.start()

    def finish(self, ins, outs, sems):
        own, copies = self._copies(ins, outs, sems)
        x, y, c, _ = _mesh_place()
        for k, (cp, landed) in enumerate(copies):
            _remote(landed, landed, sems[0].at[k + 1], sems[1].at[k + 1], (x, y, c)).wait_recv()
        for cp, _ in copies:
            cp.wait_send()
        own.wait()


class _Both:
    def __init__(self, a, b):
        self.a, self.b = a, b
        self.operands = a.operands + b.operands
        self.n_out = a.n_out + b.n_out
        self.out_shape = a.out_shape + b.out_shape
        self.aliases = dict(a.aliases)
        self.aliases.update({len(a.operands) + i: a.n_out + o for i, o in b.aliases.items()})
        self.sems = a.sems + b.sems

    def _split(self, ins, outs, sems):
        ka, na, sa = len(self.a.operands), self.a.n_out, len(self.a.sems)
        return (ins[:ka], outs[:na], sems[:sa]), (ins[ka:], outs[na:], sems[sa:])

    def start(self, ins, outs, sems):
        for ex, args in zip((self.a, self.b), self._split(ins, outs, sems)):
            ex.start(*args)

    def finish(self, ins, outs, sems):
        for ex, args in zip((self.a, self.b), self._split(ins, outs, sems)):
            ex.finish(*args)


class _SiblingExchange:
    def __init__(self, grads):
        self.operands = list(grads)
        self.n_out = len(self.operands)
        self.out_shape = [_hbm_like(g, (N_CHIP,) + g.shape[2:]) for g in grads]
        self.aliases = {}
        self.sems = _sibling_exchange_sems(self.n_out)

    def start(self, ins, outs, sems):
        _sibling_exchange_start(ins, outs, *sems)

    def finish(self, ins, outs, sems):
        _sibling_exchange_finish(ins, outs, *sems)


def _call(body, operands, *, grid, in_specs, out_specs, out_shape, name, compiler_params, scratch_shapes=(),
          exchange=None):
    operands = [o if getattr(spec, "memory_space", None) == pltpu.SMEM else _in_hbm(o)
                for o, spec in zip(operands, in_specs)]
    out_shape = [pltpu.HBM(s.shape, s.dtype) for s in out_shape]
    if exchange is None:
        res = pl.pallas_call(body, grid=grid, in_specs=in_specs, out_specs=out_specs, out_shape=out_shape, name=name,
                             scratch_shapes=list(scratch_shapes), compiler_params=compiler_params)(*operands)
        return list(res), []
    n_in, n_out, n_scr = len(in_specs), len(out_specs), len(scratch_shapes)
    k_in, k_out = len(exchange.operands), exchange.n_out

    def fused(*refs):
        ins, refs = refs[:n_in], refs[n_in:]
        ex_ins, refs = refs[:k_in], refs[k_in:]
        outs, refs = refs[:n_out], refs[n_out:]
        ex_outs, refs = refs[:k_out], refs[k_out:]
        scratch, sems = refs[:n_scr], refs[n_scr:]
        ids = [pl.program_id(a) for a in range(len(grid))]
        first = functools.reduce(jnp.logical_and, [i == 0 for i in ids])
        last = functools.reduce(jnp.logical_and, [i == g - 1 for i, g in zip(ids, grid)])

        @pl.when(first)
        def _():
            exchange.start(ex_ins, ex_outs, sems)

        body(*ins, *outs, *scratch)

        @pl.when(last)
        def _():
            exchange.finish(ex_ins, ex_outs, sems)

    res = pl.pallas_call(
        fused, grid=grid, name=name,
        in_specs=list(in_specs) + [HBM_SPEC] * k_in, out_specs=list(out_specs) + [HBM_SPEC] * k_out,
        out_shape=list(out_shape) + exchange.out_shape,
        input_output_aliases={n_in + i: n_out + o for i, o in exchange.aliases.items()},
        scratch_shapes=list(scratch_shapes) + exchange.sems, compiler_params=compiler_params,
    )(*operands, *[_in_hbm(o) for o in exchange.operands])
    return list(res[:n_out]), list(res[n_out:])


def _in_hbm(a):
    return pltpu.with_memory_space_constraint(a, pltpu.HBM)


def _row_tile(h):
    return max(t for t in range(16, 257, 16) if h % t == 0)


def _cast_shard(a, chip):
    rows, cols = a.shape
    h = rows // 2
    tr = _row_tile(h)

    def body(chip_ref, a_ref, o_ref):
        o_ref[0, 0] = a_ref[0].astype(BF16)

    return pl.pallas_call(
        body, name="cast_shard",
        grid_spec=pltpu.PrefetchScalarGridSpec(
            num_scalar_prefetch=1, grid=(2, h // tr),
            in_specs=[pl.BlockSpec((1, tr, cols), lambda s, r, chip_ref: (s, r, 0))],
            out_specs=pl.BlockSpec((1, 1, tr, cols), lambda s, r, chip_ref: (chip_ref[0], s, r, 0))),
        out_shape=pltpu.HBM((N_CHIP, 2, h, cols), BF16),
        compiler_params=_params(16, 2),
    )(chip, _in_hbm(a.reshape(2, h, cols)))


def _in_proj(x, gain1, w_in_t, exchange=None):
    tm = 256

    def body(x_ref, g_ref, w_ref, z_ref, hn_ref):
        xv = x_ref[...]
        hn = (xv * _rms_scale(xv) * g_ref[...]).astype(BF16)
        hn_ref[...] = hn
        z_ref[...] = _dot_nt(hn, w_ref[...])

    return _call(
        body, (x, gain1, w_in_t), grid=(S // tm,), name="in_proj",
        in_specs=[pl.BlockSpec((tm, D), lambda i: (i, 0)), pl.BlockSpec((1, D), lambda i: (0, 0)),
                  pl.BlockSpec((D_IN, D), lambda i: (0, 0))],
        out_specs=[pl.BlockSpec((tm, D_IN), lambda i: (i, 0)), pl.BlockSpec((tm, D), lambda i: (i, 0))],
        out_shape=[jax.ShapeDtypeStruct((S, D_IN), F32), jax.ShapeDtypeStruct((S, D), BF16)],
        compiler_params=_params(40), exchange=exchange)


def _gelu_parts(v):
    t = jnp.tanh(GELU_C * (v + 0.044715 * (v * v * v)))
    cdf = 0.5 * (1.0 + t)
    return cdf, t


def _band_mask(n):
    a = lax.broadcasted_iota(jnp.int32, (CHUNK, 2 * CHUNK), 0)
    j = lax.broadcasted_iota(jnp.int32, (CHUNK, 2 * CHUNK), 1)
    dist = CHUNK + a - j
    valid = (dist >= 0) & (dist < CHUNK)
    return valid & ((n > 0) | (j >= CHUNK))


def _fill_bias(bucket_ref, table_ref, bias_ref):
    bucket = bucket_ref[...]
    for h in range(N_HEAD):
        acc = jnp.zeros((CHUNK, 2 * CHUNK), F32)
        for b in range(N_BUCKET):
            acc = jnp.where(bucket == b, table_ref[b, h], acc)
        bias_ref[h] = acc


def _fill_tril(ws_ref, wt_ref, wtt_ref=None):
    r = lax.broadcasted_iota(jnp.int32, (CHUNK, CHUNK), 0)
    c = lax.broadcasted_iota(jnp.int32, (CHUNK, CHUNK), 1)
    for g in range(N_GROUP):
        w = jnp.where(c <= r, ws_ref[g], 0.0)
        wt_ref[g] = w.astype(BF16)
        if wtt_ref is not None:
            wtt_ref[g] = w.T.astype(BF16)


def _kv_layouts(kv_prev, kv_cur):
    both = jnp.concatenate([kv_prev, kv_cur], axis=0)
    k = both[:, :128]
    v = both[:, 128:]
    return (k.astype(BF16), pltpu.roll(k, 64, axis=1).astype(BF16),
            v.astype(BF16), pltpu.roll(v, 64, axis=1).astype(BF16))


def _head_place(h):
    pair, pos, kvh = h // 2, h % 2, h // 4
    return pair, pos, kvh == pos


def _softmax_sink(qm, k_use, bias_h, sink, valid):
    s = _dot_nt(qm, k_use) * QK_SCALE + bias_h
    s = jnp.where(valid, s, NEG_INF)
    m = jnp.maximum(jnp.max(s, axis=-1, keepdims=True), sink)
    e = jnp.exp(s - m)
    es = jnp.exp(sink - m)
    denom = jnp.sum(e, axis=-1, keepdims=True) + es
    return e / denom, es / denom


def _mixer_fwd(z, v_gain, w_spatial, b_spatial_t, sinks, rel_table, bucket, exchange=None):
    def body(z_ref, kvp_ref, gain_ref, ws_ref, bt_ref, sink_ref, table_ref, bucket_ref, out_ref, bias_ref, wt_ref):
        n = pl.program_id(0)

        @pl.when(n == 0)
        def _():
            _fill_bias(bucket_ref, table_ref, bias_ref)
            _fill_tril(ws_ref, wt_ref)

        zuv = z_ref[:, :1024]
        cdf, _ = _gelu_parts(zuv)
        guv = zuv * cdf
        for g in range(N_GROUP):
            vg = guv[:, 512 + 128 * g:512 + 128 * (g + 1)]
            vn = vg * _rms_scale(vg) * gain_ref[:, 128 * g:128 * (g + 1)]
            sv = _dot(wt_ref[g], vn.astype(BF16)) + bt_ref[:, g:g + 1]
            out_ref[:, 128 * g:128 * (g + 1)] = (guv[:, 128 * g:128 * (g + 1)] * sv).astype(BF16)

        k_same, k_swap, v_same, v_swap = _kv_layouts(kvp_ref[...], z_ref[:, 1536:1792])
        valid = _band_mask(n)
        lane_half = lax.broadcasted_iota(jnp.int32, (1, 128), 1) // 64
        for pair in range(N_HEAD // 2):
            qq = z_ref[:, 1024 + 128 * pair:1024 + 128 * (pair + 1)]
            acc = jnp.zeros((CHUNK, 128), F32)
            for pos in range(2):
                h = 2 * pair + pos
                _, _, same = _head_place(h)
                qm = jnp.where(lane_half == pos, qq, 0.0).astype(BF16)
                p, _ = _softmax_sink(qm, k_same if same else k_swap, bias_ref[h], sink_ref[h], valid)
                vm = jnp.where(lane_half == pos, v_same if same else v_swap, jnp.zeros((), BF16))
                acc = acc + _dot(p.astype(BF16), vm)
            out_ref[:, 512 + 128 * pair:512 + 128 * (pair + 1)] = acc.astype(BF16)

    return _call(
        body, (z, z, v_gain, w_spatial, b_spatial_t, sinks, rel_table, bucket), grid=(N_BLOCK,), name="mixer_fwd",
        in_specs=[pl.BlockSpec((CHUNK, D_IN), lambda n: (n, 0)),
                  pl.BlockSpec((CHUNK, 256), lambda n: (jnp.maximum(n - 1, 0), 6)),
                  pl.BlockSpec((1, 512), lambda n: (0, 0)),
                  pl.BlockSpec((N_GROUP, CHUNK, CHUNK), lambda n: (0, 0, 0)),
                  pl.BlockSpec((CHUNK, N_GROUP), lambda n: (0, 0)),
                  pl.BlockSpec(memory_space=pltpu.SMEM),
                  pl.BlockSpec(memory_space=pltpu.SMEM),
                  pl.BlockSpec((CHUNK, 2 * CHUNK), lambda n: (0, 0))],
        out_specs=[pl.BlockSpec((CHUNK, D), lambda n: (n, 0))],
        out_shape=[jax.ShapeDtypeStruct((S, D), BF16)],
        scratch_shapes=[pltpu.VMEM((N_HEAD, CHUNK, 2 * CHUNK), F32), pltpu.VMEM((N_GROUP, CHUNK, CHUNK), BF16)],
        compiler_params=_params(32), exchange=exchange)


def _out_proj(x, mix, w_out, gain2, exchange=None):
    tm = 256

    def body(x_ref, mix_ref, w_ref, g_ref, h1_ref, hn_ref, hnt_ref):
        h1 = x_ref[...] + _dot(mix_ref[...], w_ref[...])
        h1_ref[...] = h1
        hn = h1 * _rms_scale(h1) * g_ref[...]
        hn_ref[...] = hn.astype(BF16)
        hnt_ref[...] = hn.T.astype(BF16)

    return _call(
        body, (x, mix, w_out, gain2), grid=(S // tm,), name="out_proj",
        in_specs=[pl.BlockSpec((tm, D), lambda i: (i, 0)), pl.BlockSpec((tm, D), lambda i: (i, 0)),
                  pl.BlockSpec((D, D), lambda i: (0, 0)), pl.BlockSpec((1, D), lambda i: (0, 0))],
        out_specs=[pl.BlockSpec((tm, D), lambda i: (i, 0)), pl.BlockSpec((tm, D), lambda i: (i, 0)),
                   pl.BlockSpec((D, tm), lambda i: (0, i))],
        out_shape=[jax.ShapeDtypeStruct((S, D), F32), jax.ShapeDtypeStruct((S, D), BF16),
                   jax.ShapeDtypeStruct((D, S), BF16)],
        compiler_params=_params(32), exchange=exchange)


def _ffn_up(hn2, w_ff1, exchange=None):
    tm = 512
    nj = D_FF // 1024

    def body(hn_ref, w1_ref, r_ref, a_ref, at_ref):
        r = jnp.maximum(_dot(hn_ref[...], w1_ref[0]), 0.0)
        r_ref[...] = r.astype(BF16)
        a = r * r
        a_ref[...] = a.astype(BF16)
        at_ref[...] = a.T.astype(BF16)

    return _call(
        body, (hn2, w_ff1), grid=(nj, S // tm), name="ffn_up",
        in_specs=[pl.BlockSpec((tm, D), lambda j, i: (i, 0)), pl.BlockSpec((1, D, 1024), lambda j, i: (j, 0, 0))],
        out_specs=[pl.BlockSpec((tm, 1024), lambda j, i: (i, j)), pl.BlockSpec((tm, 1024), lambda j, i: (i, j)),
                   pl.BlockSpec((1024, tm), lambda j, i: (j, i))],
        out_shape=[jax.ShapeDtypeStruct((S, D_FF), BF16), jax.ShapeDtypeStruct((S, D_FF), BF16),
                   jax.ShapeDtypeStruct((D_FF, S), BF16)],
        compiler_params=_params(40, 2), exchange=exchange)


def _ffn_down(h1, a, w_ff2, exchange=None):
    tm = 1024
    nj = D_FF // 1024

    def body(h1_ref, a_ref, w2_ref, h2_ref, acc_ref):
        j = pl.program_id(1)
        part = _dot(a_ref[...], w2_ref[0])

        @pl.when(j == 0)
        def _():
            acc_ref[...] = part

        @pl.when(j > 0)
        def _():
            acc_ref[...] += part

        @pl.when(j == nj - 1)
        def _():
            h2_ref[...] = h1_ref[...] + acc_ref[...]

    return _call(
        body, (h1, a, w_ff2), grid=(S // tm, nj), name="ffn_down",
        in_specs=[pl.BlockSpec((tm, D), lambda i, j: (i, 0)), pl.BlockSpec((tm, 1024), lambda i, j: (i, j)),
                  pl.BlockSpec((1, 1024, D), lambda i, j: (j, 0, 0))],
        out_specs=[pl.BlockSpec((tm, D), lambda i, j: (i, 0))],
        out_shape=[jax.ShapeDtypeStruct((S, D), F32)],
        scratch_shapes=[pltpu.VMEM((tm, D), F32)],
        compiler_params=_params(48, 2), exchange=exchange)


def _tail(h2, p, target, w_gate, w_proj, final_gain):
    tm = 256
    steps = S // tm

    def body(h2_ref, p_ref, t_ref, wg_ref, wp_ref, gf_ref, dh2_ref, dwg_ref, dwp_ref, dgf_ref, loss_ref, dh2b_ref,
             dwp_acc):
        i = pl.program_id(0)
        h2 = h2_ref[...]
        h2b = h2.astype(BF16)
        pb = p_ref[...].astype(BF16)
        gate = jax.nn.sigmoid(_dot(h2b, wg_ref[...]))
        pp = jnp.concatenate([_dot(pb, wp_ref[j]) for j in range(N_CHIP)], axis=1)
        h3 = h2 + gate * pp
        r3 = _rms_scale(h3)
        xhat = h3 * r3
        gf = gf_ref[...]
        err = xhat * gf - t_ref[...]
        dy = err * (1.0 / D)
        dh3 = _rms_bwd(dy * gf, xhat, r3)
        dgp = (dh3 * pp * gate * (1.0 - gate)).astype(BF16)
        dpp = (dh3 * gate).astype(BF16)
        dh2 = dh3 + _dot_nt(dgp, wg_ref[...])
        dh2_ref[...] = dh2
        dh2b_ref[...] = dh2.astype(BF16)
        dwg = _dot_tn(h2b, dgp)
        dwp = _dot_tn(pb, dpp)
        dgf = jnp.sum(dy * xhat, axis=0, keepdims=True)
        sq = jnp.sum(jnp.sum(err * err, axis=1, keepdims=True), axis=0, keepdims=True)

        @pl.when(i == 0)
        def _():
            dwg_ref[...] = dwg
            dwp_acc[...] = dwp
            dgf_ref[...] = dgf
            loss_ref[...] = jnp.broadcast_to(sq, (8, 128))

        @pl.when(i > 0)
        def _():
            dwg_ref[...] += dwg
            dwp_acc[...] += dwp
            dgf_ref[...] += dgf
            loss_ref[...] += jnp.broadcast_to(sq, (8, 128))

        @pl.when(i == steps - 1)
        def _():
            for j in range(N_CHIP):
                dwp_ref[j] = dwp_acc[:, 256 * j:256 * (j + 1)]

    return _call(
        body, (h2, p, target, w_gate, w_proj, final_gain), grid=(steps,), name="tail",
        in_specs=[pl.BlockSpec((tm, D), lambda i: (i, 0)), pl.BlockSpec((tm, PLE), lambda i: (i, 0)),
                  pl.BlockSpec((tm, D), lambda i: (i, 0)), pl.BlockSpec((D, D), lambda i: (0, 0)),
                  pl.BlockSpec((N_CHIP, PLE, 256), lambda i: (0, 0, 0)), pl.BlockSpec((1, D), lambda i: (0, 0))],
        out_specs=[pl.BlockSpec((tm, D), lambda i: (i, 0)), pl.BlockSpec((D, D), lambda i: (0, 0)),
                   pl.BlockSpec((N_CHIP, PLE, 256), lambda i: (0, 0, 0)), pl.BlockSpec((1, D), lambda i: (0, 0)),
                   pl.BlockSpec((8, 128), lambda i: (0, 0)), pl.BlockSpec((tm, D), lambda i: (i, 0))],
        out_shape=[jax.ShapeDtypeStruct((S, D), F32), jax.ShapeDtypeStruct((D, D), F32),
                   jax.ShapeDtypeStruct((N_CHIP, PLE, 256), F32), jax.ShapeDtypeStruct((1, D), F32),
                   jax.ShapeDtypeStruct((8, 128), F32), jax.ShapeDtypeStruct((S, D), BF16)],
        scratch_shapes=[pltpu.VMEM((PLE, D), F32)],
        compiler_params=_params(48))[0]


def _ffn_bwd_weights(dh2b, hn2_t, r, a_t, w_ff2, exchange=None):
    tm = 1024
    nj = D_FF // 1024

    def body(dh2_ref, hnt_ref, r_ref, at_ref, w2_ref, df_ref, dw1_ref, dw2_ref):
        i = pl.program_id(1)
        dh2b = dh2_ref[...]
        da = _dot_nt(dh2b, w2_ref[0])
        df = (da * (2.0 * r_ref[...].astype(F32))).astype(BF16)
        df_ref[...] = df
        dw1 = _dot(hnt_ref[...], df)
        dw2 = _dot(at_ref[...], dh2b)

        @pl.when(i == 0)
        def _():
            dw1_ref[0] = dw1
            dw2_ref[0] = dw2

        @pl.when(i > 0)
        def _():
            dw1_ref[0] += dw1
            dw2_ref[0] += dw2

    return _call(
        body, (dh2b, hn2_t, r, a_t, w_ff2), grid=(nj, S // tm), name="ffn_bwd_weights",
        in_specs=[pl.BlockSpec((tm, D), lambda j, i: (i, 0)), pl.BlockSpec((D, tm), lambda j, i: (0, i)),
                  pl.BlockSpec((tm, 1024), lambda j, i: (i, j)), pl.BlockSpec((1024, tm), lambda j, i: (j, i)),
                  pl.BlockSpec((1, 1024, D), lambda j, i: (j, 0, 0))],
        out_specs=[pl.BlockSpec((tm, 1024), lambda j, i: (i, j)), pl.BlockSpec((1, D, 1024), lambda j, i: (j, 0, 0)),
                   pl.BlockSpec((1, 1024, D), lambda j, i: (j, 0, 0))],
        out_shape=[jax.ShapeDtypeStruct((S, D_FF), BF16), jax.ShapeDtypeStruct((nj, D, 1024), F32),
                   jax.ShapeDtypeStruct((nj, 1024, D), F32)],
        compiler_params=_params(60, 2), exchange=exchange)


def _ffn_bwd_input(df, w_ff1, dh2, h1, gain2, mix, w_out, exchange=None):
    tm = 512
    nj = D_FF // 1024
    steps = S // tm

    def body(df_ref, w1_ref, dh2_ref, h1_ref, g_ref, mix_ref, wo_ref, dh1_ref, dmix_ref, dwo_ref, dg_ref, acc_ref):
        i = pl.program_id(0)
        j = pl.program_id(1)
        part = _dot_nt(df_ref[...], w1_ref[0])

        @pl.when(j == 0)
        def _():
            acc_ref[...] = part

        @pl.when(j > 0)
        def _():
            acc_ref[...] += part

        @pl.when(j == nj - 1)
        def _():
            dhn = acc_ref[...]
            h1 = h1_ref[...]
            r2 = _rms_scale(h1)
            xhat = h1 * r2
            dh1 = dh2_ref[...] + _rms_bwd(dhn * g_ref[...], xhat, r2)
            dh1_ref[...] = dh1
            dh1b = dh1.astype(BF16)
            dmix_ref[...] = _dot_nt(dh1b, wo_ref[...])
            dwo = _dot_tn(mix_ref[...], dh1b)
            dg = jnp.sum(dhn * xhat, axis=0, keepdims=True)

            @pl.when(i == 0)
            def _():
                dwo_ref[...] = dwo
                dg_ref[...] = dg

            @pl.when(i > 0)
            def _():
                dwo_ref[...] += dwo
                dg_ref[...] += dg

    return _call(
        body, (df, w_ff1, dh2, h1, gain2, mix, w_out), grid=(steps, nj), name="ffn_bwd_input",
        in_specs=[pl.BlockSpec((tm, 1024), lambda i, j: (i, j)), pl.BlockSpec((1, D, 1024), lambda i, j: (j, 0, 0)),
                  pl.BlockSpec((tm, D), lambda i, j: (i, 0)), pl.BlockSpec((tm, D), lambda i, j: (i, 0)),
                  pl.BlockSpec((1, D), lambda i, j: (0, 0)), pl.BlockSpec((tm, D), lambda i, j: (i, 0)),
                  pl.BlockSpec((D, D), lambda i, j: (0, 0))],
        out_specs=[pl.BlockSpec((tm, D), lambda i, j: (i, 0)), pl.BlockSpec((tm, D), lambda i, j: (i, 0)),
                   pl.BlockSpec((D, D), lambda i, j: (0, 0)), pl.BlockSpec((1, D), lambda i, j: (0, 0))],
        out_shape=[jax.ShapeDtypeStruct((S, D), F32), jax.ShapeDtypeStruct((S, D), F32),
                   jax.ShapeDtypeStruct((D, D), F32), jax.ShapeDtypeStruct((1, D), F32)],
        scratch_shapes=[pltpu.VMEM((tm, D), F32)],
        compiler_params=_params(56, 2), exchange=exchange)


def _mixer_bwd(z, dmix, v_gain, w_spatial, b_spatial_t, sinks, rel_table, bucket, exchange=None):
    def body(z_ref, kvp_ref, dm_ref, gain_ref, ws_ref, bt_ref, sink_ref, table_ref, bucket_ref,
             dz_ref, dws_ref, db_ref, dgain_ref, dsink_ref, drel_ref,
             bias_ref, wt_ref, wtt_ref, dbias_ref, dsv_ref, carry_ref):
        n = pl.program_id(0)

        @pl.when(n == 0)
        def _():
            _fill_bias(bucket_ref, table_ref, bias_ref)
            _fill_tril(ws_ref, wt_ref, wtt_ref)
            dbias_ref[...] = jnp.zeros_like(dbias_ref)
            dsv_ref[...] = jnp.zeros_like(dsv_ref)
            dws_ref[...] = jnp.zeros_like(dws_ref)
            dgain_ref[...] = jnp.zeros_like(dgain_ref)
            dsink_ref[...] = jnp.zeros_like(dsink_ref)

        rows = pl.ds(pl.multiple_of(n * CHUNK, CHUNK), CHUNK)

        zuv = z_ref[:, :1024]
        cdf, t = _gelu_parts(zuv)
        guv = zuv * cdf
        dgelu = cdf + zuv * (0.5 * (1.0 - t * t)) * (GELU_C * (1.0 + 3.0 * 0.044715 * (zuv * zuv)))
        for g in range(N_GROUP):
            lo, hi = 128 * g, 128 * (g + 1)
            u = guv[:, lo:hi]
            vg = guv[:, 512 + lo:512 + hi]
            rr = _rms_scale(vg)
            vhat = vg * rr
            gain = gain_ref[:, lo:hi]
            vnb = (vhat * gain).astype(BF16)
            sv = _dot(wt_ref[g], vnb) + bt_ref[:, g:g + 1]
            da = dm_ref[:, lo:hi]
            dsv = da * u
            dsvb = dsv.astype(BF16)
            dsv_ref[g] += dsv
            dws_ref[g] += _dot_nt(dsvb, vnb)
            dvn = _dot(wtt_ref[g], dsvb)
            dgain_ref[:, lo:hi] += jnp.sum(dvn * vhat, axis=0, keepdims=True)
            dvg = _rms_bwd(dvn * gain, vhat, rr)
            dz_ref[rows, lo:hi] = (da * sv * dgelu[:, lo:hi]).astype(BF16)
            dz_ref[rows, 512 + lo:512 + hi] = (dvg * dgelu[:, 512 + lo:512 + hi]).astype(BF16)

        k_same, k_swap, v_same, v_swap = _kv_layouts(kvp_ref[...], z_ref[:, 1536:1792])
        valid = _band_mask(n)
        lane_half = lax.broadcasted_iota(jnp.int32, (1, 128), 1) // 64
        zero = jnp.zeros((2 * CHUNK, 128), F32)
        dk_same, dk_swap, dv_same, dv_swap = zero, zero, zero, zero
        for pair in range(N_HEAD // 2):
            cols = slice(1024 + 128 * pair, 1024 + 128 * (pair + 1))
            qq = z_ref[:, cols]
            do_pair = dm_ref[:, 512 + 128 * pair:512 + 128 * (pair + 1)]
            dq = jnp.zeros((CHUNK, 128), F32)
            for pos in range(2):
                h = 2 * pair + pos
                _, _, same = _head_place(h)
                on_half = lane_half == pos
                qm = jnp.where(on_half, qq, 0.0).astype(BF16)
                k_use = k_same if same else k_swap
                v_use = v_same if same else v_swap
                p, p_sink = _softmax_sink(qm, k_use, bias_ref[h], sink_ref[h], valid)
                dom = jnp.where(on_half, do_pair, 0.0).astype(BF16)
                dp = _dot_nt(dom, v_use)
                dsum = jnp.sum(p * dp, axis=-1, keepdims=True)
                ds = p * (dp - dsum)
                dbias_ref[h] += ds
                dsink_ref[h:h + 1, :] += jnp.broadcast_to(jnp.sum(-p_sink * dsum, axis=0, keepdims=True), (1, 128))
                dsb = ds.astype(BF16)
                dq = dq + jnp.where(on_half, _dot(dsb, k_use), 0.0)
                dk_h = _dot_tn(dsb, qm)
                dv_h = _dot_tn(p.astype(BF16), dom)
                if same:
                    dk_same, dv_same = dk_same + dk_h, dv_same + dv_h
                else:
                    dk_swap, dv_swap = dk_swap + dk_h, dv_swap + dv_h
            dz_ref[rows, cols] = (dq * QK_SCALE).astype(BF16)
        dk = (dk_same + pltpu.roll(dk_swap, 64, axis=1)) * QK_SCALE
        dv = dv_same + pltpu.roll(dv_swap, 64, axis=1)
        dkv = jnp.concatenate([dk, dv], axis=1)

        @pl.when(n > 0)
        def _():
            prev_rows = pl.ds(pl.multiple_of((n - 1) * CHUNK, CHUNK), CHUNK)
            dz_ref[prev_rows, 1536:1792] = (carry_ref[...] + dkv[:CHUNK]).astype(BF16)

        carry_ref[...] = dkv[CHUNK:]

        @pl.when(n == N_BLOCK - 1)
        def _():
            dz_ref[rows, 1536:1792] = dkv[CHUNK:].astype(BF16)
            r = lax.broadcasted_iota(jnp.int32, (CHUNK, CHUNK), 0)
            c = lax.broadcasted_iota(jnp.int32, (CHUNK, CHUNK), 1)
            for g in range(N_GROUP):
                dws_ref[g] = jnp.where(c <= r, dws_ref[g], 0.0)
                db_ref[g] = jnp.sum(dsv_ref[g], axis=1, keepdims=True)
            bucket = bucket_ref[...]
            for h in range(N_HEAD):
                dbh = dbias_ref[h]
                per_bucket = [jnp.sum(jnp.where(bucket == b, dbh, 0.0), axis=0, keepdims=True) for b in range(N_BUCKET)]
                drel_ref[h] = jnp.sum(jnp.concatenate(per_bucket, axis=0), axis=1, keepdims=True)

    return _call(
        body, (z, z, dmix, v_gain, w_spatial, b_spatial_t, sinks, rel_table, bucket), grid=(N_BLOCK,), name="mixer_bwd",
        in_specs=[pl.BlockSpec((CHUNK, D_IN), lambda n: (n, 0)),
                  pl.BlockSpec((CHUNK, 256), lambda n: (jnp.maximum(n - 1, 0), 6)),
                  pl.BlockSpec((CHUNK, D), lambda n: (n, 0)),
                  pl.BlockSpec((1, 512), lambda n: (0, 0)),
                  pl.BlockSpec((N_GROUP, CHUNK, CHUNK), lambda n: (0, 0, 0)),
                  pl.BlockSpec((CHUNK, N_GROUP), lambda n: (0, 0)),
                  pl.BlockSpec(memory_space=pltpu.SMEM),
                  pl.BlockSpec(memory_space=pltpu.SMEM),
                  pl.BlockSpec((CHUNK, 2 * CHUNK), lambda n: (0, 0))],
        out_specs=[pl.BlockSpec((S, D_IN), lambda n: (0, 0)),
                   pl.BlockSpec((N_GROUP, CHUNK, CHUNK), lambda n: (0, 0, 0)),
                   pl.BlockSpec((N_GROUP, CHUNK, 1), lambda n: (0, 0, 0)),
                   pl.BlockSpec((1, 512), lambda n: (0, 0)),
                   pl.BlockSpec((N_HEAD, 128), lambda n: (0, 0)),
                   pl.BlockSpec((N_HEAD, N_BUCKET, 1), lambda n: (0, 0, 0))],
        out_shape=[jax.ShapeDtypeStruct((S, D_IN), BF16), jax.ShapeDtypeStruct((N_GROUP, CHUNK, CHUNK), F32),
                   jax.ShapeDtypeStruct((N_GROUP, CHUNK, 1), F32), jax.ShapeDtypeStruct((1, 512), F32),
                   jax.ShapeDtypeStruct((N_HEAD, 128), F32), jax.ShapeDtypeStruct((N_HEAD, N_BUCKET, 1), F32)],
        scratch_shapes=[pltpu.VMEM((N_HEAD, CHUNK, 2 * CHUNK), F32), pltpu.VMEM((N_GROUP, CHUNK, CHUNK), BF16),
                        pltpu.VMEM((N_GROUP, CHUNK, CHUNK), BF16), pltpu.VMEM((N_HEAD, CHUNK, 2 * CHUNK), F32),
                        pltpu.VMEM((N_GROUP, CHUNK, CHUNK), F32), pltpu.VMEM((CHUNK, 256), F32)],
        compiler_params=_params(48), exchange=exchange)


def _in_bwd_weight(dz, hn1, exchange=None):
    tm = 512

    def body(dz_ref, hn_ref, dw_ref):
        i = pl.program_id(0)
        dw = _dot_tn(dz_ref[...], hn_ref[...])

        @pl.when(i == 0)
        def _():
            dw_ref[...] = dw

        @pl.when(i > 0)
        def _():
            dw_ref[...] += dw

    return _call(
        body, (dz, hn1), grid=(S // tm,), name="in_bwd_weight",
        in_specs=[pl.BlockSpec((tm, D_IN), lambda i: (i, 0)), pl.BlockSpec((tm, D), lambda i: (i, 0))],
        out_specs=[pl.BlockSpec((D_IN, D), lambda i: (0, 0))],
        out_shape=[jax.ShapeDtypeStruct((D_IN, D), F32)],
        compiler_params=_params(40), exchange=exchange)


def _in_bwd_input(dz, w_in_t, x, dh1, gain1, exchange=None):
    tm = 512

    def body(dz_ref, w_ref, x_ref, dh1_ref, g_ref, dx_ref, dg_ref):
        i = pl.program_id(0)
        dhn = _dot(dz_ref[...], w_ref[...])
        xv = x_ref[...]
        r1 = _rms_scale(xv)
        xhat = xv * r1
        dx_ref[...] = dh1_ref[...] + _rms_bwd(dhn * g_ref[...], xhat, r1)
        dg = jnp.sum(dhn * xhat, axis=0, keepdims=True)

        @pl.when(i == 0)
        def _():
            dg_ref[...] = dg

        @pl.when(i > 0)
        def _():
            dg_ref[...] += dg

    return _call(
        body, (dz, w_in_t, x, dh1, gain1), grid=(S // tm,), name="in_bwd_input",
        in_specs=[pl.BlockSpec((tm, D_IN), lambda i: (i, 0)), pl.BlockSpec((D_IN, D), lambda i: (0, 0)),
                  pl.BlockSpec((tm, D), lambda i: (i, 0)), pl.BlockSpec((tm, D), lambda i: (i, 0)),
                  pl.BlockSpec((1, D), lambda i: (0, 0))],
        out_specs=[pl.BlockSpec((tm, D), lambda i: (i, 0)), pl.BlockSpec((1, D), lambda i: (0, 0))],
        out_shape=[jax.ShapeDtypeStruct((S, D), F32), jax.ShapeDtypeStruct((1, D), F32)],
        compiler_params=_params(48), exchange=exchange)


def _rel_bucket():
    a = jnp.arange(CHUNK)[:, None]
    j = jnp.arange(2 * CHUNK)[None, :]
    n = jnp.maximum(CHUNK + a - j, 0)
    max_exact = N_BUCKET // 2
    nf = jnp.maximum(n, 1).astype(jnp.float32)
    large = max_exact + (jnp.log(nf / max_exact) / math.log(CHUNK / max_exact) * (N_BUCKET - max_exact)).astype(jnp.int32)
    large = jnp.minimum(large, N_BUCKET - 1)
    return jnp.where(n < max_exact, n, large).astype(jnp.int32)


def _step(x, p, target, small, bufs, place):
    bucket = _rel_bucket()
    sinks = small["attn_sinks"].reshape(N_HEAD)
    b_t = jnp.transpose(small["b_spatial"].reshape(N_GROUP, CHUNK))
    ws = small["w_spatial"].reshape(N_GROUP, CHUNK, CHUNK)
    gain1, gain2 = small["norm1_gain"], small["norm2_gain"]
    v_gain = small["gmlp_v_gain"]
    final_gain = small["final_gain"].reshape(1, D)
    table = small["rel_bias_table"]
    bufs = dict(bufs)

    def travel(neighbours, relayed):
        names = list(neighbours) + list(relayed)
        steps = [_Gather([bufs[n] for n in neighbours])] if neighbours else []
        steps += [_Relay([bufs[n] for n in relayed])] if relayed else []
        return names, (steps[0] if len(steps) == 1 else _Both(*steps))

    def took(names, got):
        bufs.update(zip(names, got))

    took(["w_in"], _gather_weights([bufs["w_in"]]))
    w_in_t = _whole(bufs["w_in"]).reshape(D_IN, D)
    names, ex = travel(["w_out", "w_ff2"], [])
    (z, hn1), got = _in_proj(x, gain1, w_in_t, ex)
    took(names, got)
    names, ex = travel(["w_ff1"], ["w_out"])
    (mix,), got = _mixer_fwd(z, v_gain, ws, b_t, sinks, table, bucket, ex)
    took(names, got)
    w_out = _whole(bufs["w_out"]).reshape(D, D)
    names, ex = travel([], ["w_ff1", "w_ff2"])
    (h1, hn2, hn2_t), got = _out_proj(x, mix, w_out, gain2, ex)
    took(names, got)
    w_ff1 = _whole(bufs["w_ff1"])
    names, ex = travel(["w_ple_gate", "w_ple_proj"], [])
    (r, a, a_t), got = _ffn_up(hn2, w_ff1, ex)
    took(names, got)
    w_ff2 = _whole(bufs["w_ff2"])
    names, ex = travel([], ["w_ple_gate", "w_ple_proj"])
    (h2,), got = _ffn_down(h1, a, w_ff2, ex)
    took(names, got)
    dh2, d_gate, d_proj, d_final, sq, dh2b = _tail(h2, p, target, _whole(bufs["w_ple_gate"]).reshape(D, D),
                                                   _whole(bufs["w_ple_proj"]), final_gain)

    def pair_sums(halves, from_sibling):
        sums, landing = zip(*[_pair_sum(g, o, place) for g, o in zip(halves, from_sibling)])
        return list(sums), list(landing)

    landed = {}
    halves = [_halves(d_gate.reshape(N_CHIP, 256, D)), _halves(d_proj)]
    ex = _ChipExchange(*pair_sums(halves, _sibling_exchange(halves)))
    (df, d_ff1, d_ff2), got = _ffn_bwd_weights(dh2b, hn2_t, r, a_t, w_ff2, ex)
    landed.update(zip(["w_ple_gate", "w_ple_proj"], got))
    halves = [_halves(d_ff1), _halves(d_ff2)]
    (dh1, dmix, d_out, d_gain2), got = _ffn_bwd_input(df, w_ff1, dh2, h1, gain2, mix, w_out, _SiblingExchange(halves))
    ex = _ChipExchange(*pair_sums(halves, got))
    (dz, d_ws, d_b, d_vgain, d_sink, d_rel), got = _mixer_bwd(z, dmix, v_gain, ws, b_t, sinks, table, bucket, ex)
    landed.update(zip(["w_ff1", "w_ff2"], got))
    small_grads = {
        "gmlp_v_gain": d_vgain, "w_spatial": d_ws.reshape(1, N_GROUP, CHUNK, CHUNK),
        "b_spatial": d_b.reshape(1, N_GROUP, CHUNK), "attn_sinks": d_sink[:, 0].reshape(1, N_HEAD),
        "rel_bias_table": jnp.transpose(d_rel.reshape(N_HEAD, N_BUCKET)), "norm2_gain": d_gain2,
        "final_gain": d_final.reshape(D),
    }
    halves = [_halves(d_out.reshape(N_CHIP, 256, D))]
    (d_in_t,), got = _in_bwd_weight(dz, hn1, _ChipExchange(*pair_sums(halves, _sibling_exchange(halves))))
    landed.update(zip(["w_out"], got))
    halves = [_halves(d_in_t.reshape(N_CHIP, 448, D))]
    ex = _ChipExchange(*pair_sums(halves, _sibling_exchange(halves)))
    (dx, small_grads["norm1_gain"]), got = _in_bwd_input(dz, w_in_t, x, dh1, gain1, ex)
    landed.update(zip(["w_in"], got))
    return dx, landed, small_grads, sq


HBM_SPEC = pl.BlockSpec(memory_space=pltpu.HBM)
VMEM_SPEC = pl.BlockSpec(memory_space=pltpu.VMEM)


def _mesh_place():
    x, y, c = lax.axis_index("x"), lax.axis_index("y"), lax.axis_index("c")
    others = [(1 - x, y), (x, 1 - y), (1 - x, 1 - y)]
    return x, y, c, others


def _remote(src, dst, send_sem, recv_sem, device):
    return pltpu.make_async_remote_copy(src_ref=src, dst_ref=dst, send_sem=send_sem, recv_sem=recv_sem,
                                        device_id=device, device_id_type=MESH)


def _hbm_like(a, shape=None, dtype=None):
    return pltpu.HBM(a.shape if shape is None else shape, a.dtype if dtype is None else dtype)


def _gather_start(bufs, send_sems, recv_sems):
    x, y, c, others = _mesh_place()
    me = 2 * x + y
    for w, buf in enumerate(bufs):
        for k in NEIGHBOURS:
            mine = buf.at[me, c]
            _remote(mine, mine, send_sems.at[w, k], recv_sems.at[w, k], (*others[k], c)).start()


def _gather_finish(bufs, send_sems, recv_sems):
    x, y, c, others = _mesh_place()
    me = 2 * x + y
    sibling = (x, y, 1 - c)
    idx = [2 * ox + oy for ox, oy in others]
    chips = NEIGHBOURS
    for w, buf in enumerate(bufs):
        for k in chips:
            landed = buf.at[idx[k], c]
            _remote(landed, landed, send_sems.at[w, k], recv_sems.at[w, k], sibling).wait_recv()
            _remote(landed, landed, send_sems.at[w, 3 + k], recv_sems.at[w, 3 + k], sibling).start()
    for w, buf in enumerate(bufs):
        for k in chips:
            landed = buf.at[idx[k], 1 - c]
            _remote(landed, landed, send_sems.at[w, 3 + k], recv_sems.at[w, 3 + k], sibling).wait_recv()
    for w, buf in enumerate(bufs):
        for k in chips:
            mine, passed = buf.at[me, c], buf.at[idx[k], c]
            _remote(mine, mine, send_sems.at[w, k], recv_sems.at[w, k], sibling).wait_send()
            _remote(passed, passed, send_sems.at[w, 3 + k], recv_sems.at[w, 3 + k], sibling).wait_send()


NEIGHBOURS = (0, 1)
DIAGONAL = 2
RELAY_TOP, RELAY_BOTTOM = 6, 7


def _relay_copies(bufs, send_sems, recv_sems):
    x, y, c, others = _mesh_place()
    idx = [2 * ox + oy for ox, oy in others]
    sends, lands = [], []
    for w, buf in enumerate(bufs):
        rows = buf.shape[2] // 2
        upper, lower = pl.ds(0, rows), pl.ds(rows, rows)
        from_x, from_y = buf.at[idx[0], c, upper], buf.at[idx[1], c, lower]
        sends.append(_remote(from_x, from_x, send_sems.at[w, RELAY_TOP], recv_sems.at[w, RELAY_TOP], (*others[1], c)))
        sends.append(_remote(from_y, from_y, send_sems.at[w, RELAY_BOTTOM], recv_sems.at[w, RELAY_BOTTOM],
                             (*others[0], c)))
        lands.append((buf.at[idx[DIAGONAL], c, upper], w, RELAY_TOP))
        lands.append((buf.at[idx[DIAGONAL], c, lower], w, RELAY_BOTTOM))
    return sends, lands


def _relay_start(bufs, send_sems, recv_sems):
    for cp in _relay_copies(bufs, send_sems, recv_sems)[0]:
        cp.start()


def _relay_finish(bufs, send_sems, recv_sems):
    x, y, c, others = _mesh_place()
    sibling = (x, y, 1 - c)
    diag = 2 * others[DIAGONAL][0] + others[DIAGONAL][1]
    sends, lands = _relay_copies(bufs, send_sems, recv_sems)
    for piece, w, col in lands:
        _remote(piece, piece, send_sems.at[w, col], recv_sems.at[w, col], sibling).wait_recv()
    passed = []
    for w, buf in enumerate(bufs):
        cp = _remote(buf.at[diag, c], buf.at[diag, c], send_sems.at[w, 3 + DIAGONAL], recv_sems.at[w, 3 + DIAGONAL],
                     sibling)
        cp.start()
        passed.append(cp)
    for w, buf in enumerate(bufs):
        landed = buf.at[diag, 1 - c]
        _remote(landed, landed, send_sems.at[w, 3 + DIAGONAL], recv_sems.at[w, 3 + DIAGONAL], sibling).wait_recv()
    for cp in sends + passed:
        cp.wait_send()


def _gather_sems(n):
    return [pltpu.SemaphoreType.DMA((n, 8)), pltpu.SemaphoreType.DMA((n, 8))]


def _gather_weights(bufs):
    n = len(bufs)

    def body(*refs):
        outs = refs[n:2 * n]
        send_sems, recv_sems = refs[2 * n:]
        _gather_start(outs, send_sems, recv_sems)
        _gather_finish(outs, send_sems, recv_sems)
        _relay_start(outs, send_sems, recv_sems)
        _relay_finish(outs, send_sems, recv_sems)

    return pl.pallas_call(
        body, name="gather_weights",
        in_specs=[HBM_SPEC] * n, out_specs=[HBM_SPEC] * n,
        out_shape=[_hbm_like(b) for b in bufs],
        input_output_aliases={w: w for w in range(n)},
        scratch_shapes=_gather_sems(n),
    )(*bufs)


def _sibling_copies(grads, landing, send_sems, recv_sems):
    x, y, c, _ = _mesh_place()
    return [_remote(grads[w].at[j, 1 - c], landing[w].at[j], send_sems.at[w, j], recv_sems.at[w, j], (x, y, 1 - c))
            for w in range(len(grads)) for j in range(N_CHIP)]


def _sibling_exchange_start(grads, landing, send_sems, recv_sems):
    for cp in _sibling_copies(grads, landing, send_sems, recv_sems):
        cp.start()


def _sibling_exchange_finish(grads, landing, send_sems, recv_sems):
    copies = _sibling_copies(grads, landing, send_sems, recv_sems)
    for cp in copies:
        cp.wait_recv()
    for cp in copies:
        cp.wait_send()


def _sibling_exchange_sems(n):
    return [pltpu.SemaphoreType.DMA((n, N_CHIP)), pltpu.SemaphoreType.DMA((n, N_CHIP))]


def _sibling_exchange(grads):
    n = len(grads)

    def body(*refs):
        ins, outs = refs[:n], refs[n:2 * n]
        _sibling_exchange_start(ins, outs, *refs[2 * n:])
        _sibling_exchange_finish(ins, outs, *refs[2 * n:])

    return pl.pallas_call(
        body, name="sibling_exchange",
        in_specs=[HBM_SPEC] * n, out_specs=[HBM_SPEC] * n,
        out_shape=[_hbm_like(g, (N_CHIP,) + g.shape[2:]) for g in grads],
        scratch_shapes=_sibling_exchange_sems(n),
    )(*[_in_hbm(g) for g in grads])


def _chip_exchange_start(sums, landing, send_sems, recv_sems):
    x, y, c, others = _mesh_place()
    me = 2 * x + y
    for w in range(len(sums)):
        for k, (ox, oy) in enumerate(others):
            _remote(sums[w].at[2 * ox + oy], landing[w].at[me], send_sems.at[w, k], recv_sems.at[w, k],
                    (ox, oy, c)).start()


def _chip_exchange_finish(sums, landing, send_sems, recv_sems):
    x, y, c, others = _mesh_place()
    for w in range(len(sums)):
        for k, (ox, oy) in enumerate(others):
            piece = landing[w].at[2 * ox + oy]
            _remote(piece, piece, send_sems.at[w, k], recv_sems.at[w, k], (x, y, c)).wait_recv()
    for w in range(len(sums)):
        for k, (ox, oy) in enumerate(others):
            piece = sums[w].at[2 * ox + oy]
            _remote(piece, piece, send_sems.at[w, k], recv_sems.at[w, k], (x, y, c)).wait_send()


def _chip_exchange_sems(n):
    return [pltpu.SemaphoreType.DMA((n, 3)), pltpu.SemaphoreType.DMA((n, 3))]


def _chip_exchange(sums, landing):
    n = len(sums)

    def body(*refs):
        ins, outs = refs[:n], refs[2 * n:3 * n]
        send_sems, recv_sems = refs[3 * n:]
        _chip_exchange_start(ins, outs, send_sems, recv_sems)
        _chip_exchange_finish(ins, outs, send_sems, recv_sems)

    return pl.pallas_call(
        body, name="chip_exchange",
        in_specs=[HBM_SPEC] * (2 * n), out_specs=[HBM_SPEC] * n,
        out_shape=[_hbm_like(b) for b in landing],
        input_output_aliases={n + w: w for w in range(n)},
        scratch_shapes=_chip_exchange_sems(n),
    )(*sums, *landing)


def _sibling_allgather(bufs, also):
    n = len(bufs)
    k_in, k_out = len(also.operands), also.n_out

    def body(*refs):
        ex_ins, refs = refs[n:n + k_in], refs[n + k_in:]
        outs, refs = refs[:n], refs[n:]
        ex_outs, refs = refs[:k_out], refs[k_out:]
        send_sems, recv_sems, ex_sems = refs[0], refs[1], refs[2:]
        x, y, c, _ = _mesh_place()
        sibling = (x, y, 1 - c)
        also.start(ex_ins, ex_outs, ex_sems)
        sends = [_remote(outs[w].at[c], outs[w].at[c], send_sems.at[w], recv_sems.at[w], sibling) for w in range(n)]
        for cp in sends:
            cp.start()
        for w in range(n):
            landed = outs[w].at[1 - c]
            _remote(landed, landed, send_sems.at[w], recv_sems.at[w], sibling).wait_recv()
        for cp in sends:
            cp.wait_send()
        also.finish(ex_ins, ex_outs, ex_sems)

    res = pl.pallas_call(
        body, name="sibling_allgather",
        in_specs=[HBM_SPEC] * (n + k_in), out_specs=[HBM_SPEC] * (n + k_out),
        out_shape=[_hbm_like(b) for b in bufs] + also.out_shape,
        input_output_aliases={**{w: w for w in range(n)}, **{n + i: n + o for i, o in also.aliases.items()}},
        scratch_shapes=[pltpu.SemaphoreType.DMA((n,)), pltpu.SemaphoreType.DMA((n,))] + also.sems,
    )(*bufs, *[_in_hbm(o) for o in also.operands])
    return list(res[:n]), list(res[n:])


def _pair_sum(grad, other, place):
    _, _, h, cols = grad.shape
    tr = _row_tile(h)

    def body(place_ref, g_ref, o_ref, sums_ref, own_ref):
        s = (g_ref[0, 0] + o_ref[0]).astype(BF16)
        sums_ref[0] = s

        @pl.when(pl.program_id(1) == place_ref[0])
        def _():
            own_ref[0] = s

    return pl.pallas_call(
        body, name="pair_sum",
        grid_spec=pltpu.PrefetchScalarGridSpec(
            num_scalar_prefetch=1, grid=(h // tr, N_CHIP),
            in_specs=[pl.BlockSpec((1, 1, tr, cols), lambda r, j, place_ref: (j, place_ref[1], r, 0)),
                      pl.BlockSpec((1, tr, cols), lambda r, j, place_ref: (j, r, 0))],
            out_specs=[pl.BlockSpec((1, tr, cols), lambda r, j, place_ref: (j, r, 0)),
                       pl.BlockSpec((1, tr, cols), lambda r, j, place_ref: (place_ref[0], r, 0))]),
        out_shape=[pltpu.HBM((N_CHIP, h, cols), BF16)] * 2,
        compiler_params=_params(16, 2),
    )(place, _in_hbm(grad), _in_hbm(other))


def _chip_sum(parts, place):
    _, h, cols = parts.shape
    tr = _row_tile(h)

    def body(place_ref, p_ref, out_ref):
        out_ref[0] = ((p_ref[0].astype(F32) + p_ref[1].astype(F32)) + p_ref[2].astype(F32)) + p_ref[3].astype(F32)

    return pl.pallas_call(
        body, name="chip_sum",
        grid_spec=pltpu.PrefetchScalarGridSpec(
            num_scalar_prefetch=1, grid=(h // tr,),
            in_specs=[pl.BlockSpec((N_CHIP, tr, cols), lambda r, place_ref: (0, r, 0))],
            out_specs=pl.BlockSpec((1, tr, cols), lambda r, place_ref: (place_ref[1], r, 0))),
        out_shape=pltpu.HBM((2, h, cols), F32),
        compiler_params=_params(16),
    )(place, _in_hbm(parts))


def _adamw_math(w, g, m, v):
    m = ADAM_B1 * m + (1.0 - ADAM_B1) * g
    v = ADAM_B2 * v + (1.0 - ADAM_B2) * (g * g)
    m_hat = m / (1.0 - ADAM_B1 ** ADAM_STEP)
    v_hat = v / (1.0 - ADAM_B2 ** ADAM_STEP)
    delta = -ADAM_LR * (m_hat / (jnp.sqrt(v_hat) + ADAM_EPS) + ADAM_WD * w)
    return delta, m, v


def _adamw(w, g, m, v, exchange=None):
    rows, cols = w.shape
    tr = _row_tile(rows)

    def body(w_ref, g_ref, m_ref, v_ref, d_ref, nm_ref, nv_ref, g_out_ref):
        g = g_ref[...]
        d_ref[...], nm_ref[...], nv_ref[...] = _adamw_math(w_ref[...], g, m_ref[...], v_ref[...])
        g_out_ref[...] = g

    spec = pl.BlockSpec((tr, cols), lambda r: (r, 0))
    return _call(
        body, (w, g, m, v), grid=(rows // tr,), name="adamw",
        in_specs=[spec] * 4, out_specs=[spec] * 4,
        out_shape=[jax.ShapeDtypeStruct((rows, cols), F32)] * 4,
        compiler_params=_params(32), exchange=exchange)


SMALL_NAMES = ("norm1_gain", "gmlp_v_gain", "w_spatial", "b_spatial", "attn_sinks", "rel_bias_table", "norm2_gain",
               "final_gain")
PACK_TILE = 8 * 128


def _pack_small(arrays):
    parts = []
    for a in arrays:
        flat = a.reshape(-1)
        rows = -(-flat.shape[0] // PACK_TILE) * 8
        parts.append(jnp.pad(flat, (0, rows * 128 - flat.shape[0])).reshape(rows, 128))
    return jnp.concatenate(parts, axis=0)


def _unpack_small(packed, like):
    out, row = [], 0
    for a in like:
        size = math.prod(a.shape)
        rows = -(-size // PACK_TILE) * 8
        out.append(packed[row:row + rows].reshape(-1)[:size].reshape(a.shape))
        row += rows
    return out


def _small_update(gathered, w, m, v):
    rows = gathered.shape[1]

    def body(g_ref, w_ref, m_ref, v_ref, tot_ref, d_ref, nm_ref, nv_ref):
        total = g_ref[0].astype(F32)
        for dev in range(1, 8):
            total = total + g_ref[dev].astype(F32)
        tot_ref[...] = total
        d_ref[...], nm_ref[...], nv_ref[...] = _adamw_math(w_ref[...], total, m_ref[...], v_ref[...])

    return pl.pallas_call(
        body, name="small_update",
        in_specs=[VMEM_SPEC] * 4, out_specs=[VMEM_SPEC] * 4,
        out_shape=[jax.ShapeDtypeStruct((rows, 128), F32)] * 4,
        compiler_params=pltpu.CompilerParams(vmem_limit_bytes=24 * MIB),
    )(gathered, w, m, v)


def _halves(a):
    return a.reshape(a.shape[:-2] + (2, a.shape[-2] // 2, a.shape[-1]))


def _whole(a):
    return a.reshape(a.shape[:-3] + (2 * a.shape[-2], a.shape[-1]))


def kernel(x, p, norm1_gain, w_in, gmlp_v_gain, w_spatial, b_spatial, attn_sinks, rel_bias_table, w_out, norm2_gain, w_ff1, w_ff2, w_ple_proj, w_ple_gate, final_gain, loss_target, m_norm1_gain, m_w_in, m_gmlp_v_gain, m_w_spatial, m_b_spatial, m_attn_sinks, m_rel_bias_table, m_w_out, m_norm2_gain, m_w_ff1, m_w_ff2, m_w_ple_proj, m_w_ple_gate, m_final_gain, v_norm1_gain, v_w_in, v_gmlp_v_gain, v_w_spatial, v_b_spatial, v_attn_sinks, v_rel_bias_table, v_w_out, v_norm2_gain, v_w_ff1, v_w_ff2, v_w_ple_proj, v_w_ple_gate, v_final_gain):
    given = dict(locals())
    small = {n: given[n] for n in SMALL_NAMES}
    chip = 2 * lax.axis_index("x") + lax.axis_index("y")
    place = jnp.stack([chip, lax.axis_index("c")]).astype(jnp.int32)

    big_names = ("w_in", "w_out", "w_ff1", "w_ff2", "w_ple_proj", "w_ple_gate")
    shards = {n: given[n][0] for n in big_names}
    travel = dict(shards, w_in=jnp.transpose(shards["w_in"]))
    bufs = {n: _cast_shard(travel[n], place[:1]) for n in big_names}
    dx, landed, small_grads, sq = _step(x[0], p[0, 0], loss_target[0], small, bufs, place)

    out_grad, out_delta, out_m, out_v = {}, {}, {}, {}

    def update(n, g, exchange=None):
        to = jnp.transpose if n == "w_in" else (lambda a: a)
        (delta, new_m, new_v, g_out), got = _adamw(to(shards[n]), g, to(given["m_" + n][0]), to(given["v_" + n][0]),
                                                   exchange)
        out_grad[n], out_delta[n], out_m[n], out_v[n] = [to(a)[None] for a in (g_out, delta, new_m, new_v)]
        return got

    spare = jnp.zeros((8, 128), F32)
    small_packed = _pack_small([small_grads[n] for n in SMALL_NAMES] + [spare]).astype(BF16)
    reduced, (small_gathered, sq_gathered) = _sibling_allgather(
        [_chip_sum(landed[n], place) for n in big_names], _Both(_GatherAll(small_packed), _GatherAll(sq)))
    for n, r in zip(big_names, reduced):
        update(n, _whole(r))

    like = [given[n] for n in SMALL_NAMES] + [spare]
    packed = _small_update(small_gathered, *[_pack_small([given[pre + n] for n in SMALL_NAMES] + [spare])
                                             for pre in ("", "m_", "v_")])
    for res, out in zip(packed, (out_grad, out_delta, out_m, out_v)):
        out.update(zip(SMALL_NAMES, _unpack_small(res, like)))
    loss = 0.5 * jnp.sum(sq_gathered[:, 0, 0]) / D

    order = ("norm1_gain", "w_in", "gmlp_v_gain", "w_spatial", "b_spatial", "attn_sinks", "rel_bias_table", "w_out",
             "norm2_gain", "w_ff1", "w_ff2", "w_ple_proj", "w_ple_gate", "final_gain")
    return (loss, dx[None], *[out_grad[n] for n in order], *[out_delta[n] for n in order],
            *[out_m[n] for n in order], *[out_v[n] for n in order])
```

```python
import functools
import math

import jax
import jax.numpy as jnp
from jax import lax
from jax.experimental import pallas as pl
from jax.experimental.pallas import tpu as pltpu

S = 2048
D = 1024
D_IN = 1792
D_FF = 4096
PLE = 256
N_CHIP = 4
N_GROUP = 4
CHUNK = 128
N_HEAD = 8
N_BLOCK = S // CHUNK
N_BUCKET = 32
EPS = 1e-6
NEG_INF = -1e30
QK_SCALE = 0.125
GELU_C = math.sqrt(2.0 / math.pi)

ADAM_LR = 0.001
ADAM_B1 = 0.9
ADAM_B2 = 0.999
ADAM_EPS = 1e-08
ADAM_WD = 0.01
ADAM_STEP = 10

F32 = jnp.float32
BF16 = jnp.bfloat16
MIB = 1024 * 1024
MESH = pl.DeviceIdType.MESH

NT = (((1,), (1,)), ((), ()))
TN = (((0,), (0,)), ((), ()))


def _dot(a, b):
    return jnp.dot(a, b, preferred_element_type=F32)


def _dot_nt(a, b):
    return lax.dot_general(a, b, NT, preferred_element_type=F32)


def _dot_tn(a, b):
    return lax.dot_general(a, b, TN, preferred_element_type=F32)


def _params(vmem_mib, n_axes=1):
    return pltpu.CompilerParams(dimension_semantics=("arbitrary",) * n_axes, vmem_limit_bytes=vmem_mib * MIB)


def _rms_scale(v):
    return lax.rsqrt(jnp.mean(v * v, axis=-1, keepdims=True) + EPS)


def _rms_bwd(dy_gain, xhat, r):
    return r * (dy_gain - xhat * jnp.mean(dy_gain * xhat, axis=-1, keepdims=True))


class _Gather:
    def __init__(self, bufs):
        self.operands = list(bufs)
        self.n_out = len(self.operands)
        self.out_shape = [_hbm_like(b) for b in bufs]
        self.aliases = {w: w for w in range(self.n_out)}
        self.sems = _gather_sems(self.n_out)

    def start(self, ins, outs, sems):
        _gather_start(outs, *sems)

    def finish(self, ins, outs, sems):
        _gather_finish(outs, *sems)


class _ChipExchange:
    def __init__(self, sums, landing):
        self.n_out = len(landing)
        self.operands = list(sums) + list(landing)
        self.out_shape = [_hbm_like(b) for b in landing]
        self.aliases = {self.n_out + w: w for w in range(self.n_out)}
        self.sems = _chip_exchange_sems(self.n_out)

    def start(self, ins, outs, sems):
        _chip_exchange_start(ins[:self.n_out], outs, *sems)

    def finish(self, ins, outs, sems):
        _chip_exchange_finish(ins[:self.n_out], outs, *sems)


class _GatherAll:
    def __init__(self, packed):
        self.operands = [packed]
        self.n_out = 1
        self.out_shape = [_hbm_like(packed, (8,) + packed.shape)]
        self.aliases = {}
        self.sems = [pltpu.SemaphoreType.DMA((8,)), pltpu.SemaphoreType.DMA((8,))]

    def _copies(self, ins, outs, sems):
        x, y, c, _ = _mesh_place()
        me = 4 * x + 2 * y + c
        send_sems, recv_sems = sems
        copies = []
        for k in range(1, 8):
            peer = (1 - x if k // 4 else x, 1 - y if (k // 2) % 2 else y, 1 - c if k % 2 else c)
            src = 4 * peer[0] + 2 * peer[1] + peer[2]
            copies.append((_remote(ins[0], outs[0].at[me], send_sems.at[k], recv_sems.at[k], peer), outs[0].at[src]))
        own = pltpu.make_async_copy(ins[0], outs[0].at[me], send_sems.at[0])
        return own, copies

    def start(self, ins, outs, sems):
        own, copies = self._copies(ins, outs, sems)
        own.start()
        for cp, _ in copies:
            cp.start()

    def finish(self, ins, outs, sems):
        own, copies = self._copies(ins, outs, sems)
        x, y, c, _ = _mesh_place()
        for k, (cp, landed) in enumerate(copies):
            _remote(landed, landed, sems[0].at[k + 1], sems[1].at[k + 1], (x, y, c)).wait_recv()
        for cp, _ in copies:
            cp.wait_send()
        own.wait()


class _Both:
    def __init__(self, a, b):
        self.a, self.b = a, b
        self.operands = a.operands + b.operands
        self.n_out = a.n_out + b.n_out
        self.out_shape = a.out_shape + b.out_shape
        self.aliases = dict(a.aliases)
        self.aliases.update({len(a.operands) + i: a.n_out + o for i, o in b.aliases.items()})
        self.sems = a.sems + b.sems

    def _split(self, ins, outs, sems):
        ka, na, sa = len(self.a.operands), self.a.n_out, len(self.a.sems)
        return (ins[:ka], outs[:na], sems[:sa]), (ins[ka:], outs[na:], sems[sa:])

    def start(self, ins, outs, sems):
        for ex, args in zip((self.a, self.b), self._split(ins, outs, sems)):
            ex.start(*args)

    def finish(self, ins, outs, sems):
        for ex, args in zip((self.a, self.b), self._split(ins, outs, sems)):
            ex.finish(*args)


class _SiblingExchange:
    def __init__(self, grads):
        self.operands = list(grads)
        self.n_out = len(self.operands)
        self.out_shape = [_hbm_like(g, (N_CHIP,) + g.shape[2:]) for g in grads]
        self.aliases = {}
        self.sems = _sibling_exchange_sems(self.n_out)

    def start(self, ins, outs, sems):
        _sibling_exchange_start(ins, outs, *sems)

    def finish(self, ins, outs, sems):
        _sibling_exchange_finish(ins, outs, *sems)


def _call(body, operands, *, grid, in_specs, out_specs, out_shape, name, compiler_params, scratch_shapes=(),
          exchange=None):
    operands = [o if getattr(spec, "memory_space", None) == pltpu.SMEM else _in_hbm(o)
                for o, spec in zip(operands, in_specs)]
    out_shape = [pltpu.HBM(s.shape, s.dtype) for s in out_shape]
    if exchange is None:
        res = pl.pallas_call(body, grid=grid, in_specs=in_specs, out_specs=out_specs, out_shape=out_shape, name=name,
                             scratch_shapes=list(scratch_shapes), compiler_params=compiler_params)(*operands)
        return list(res), []
    n_in, n_out, n_scr = len(in_specs), len(out_specs), len(scratch_shapes)
    k_in, k_out = len(exchange.operands), exchange.n_out

    def fused(*refs):
        ins, refs = refs[:n_in], refs[n_in:]
        ex_ins, refs = refs[:k_in], refs[k_in:]
        outs, refs = refs[:n_out], refs[n_out:]
        ex_outs, refs = refs[:k_out], refs[k_out:]
        scratch, sems = refs[:n_scr], refs[n_scr:]
        ids = [pl.program_id(a) for a in range(len(grid))]
        first = functools.reduce(jnp.logical_and, [i == 0 for i in ids])
        last = functools.reduce(jnp.logical_and, [i == g - 1 for i, g in zip(ids, grid)])

        @pl.when(first)
        def _():
            exchange.start(ex_ins, ex_outs, sems)

        body(*ins, *outs, *scratch)

        @pl.when(last)
        def _():
            exchange.finish(ex_ins, ex_outs, sems)

    res = pl.pallas_call(
        fused, grid=grid, name=name,
        in_specs=list(in_specs) + [HBM_SPEC] * k_in, out_specs=list(out_specs) + [HBM_SPEC] * k_out,
        out_shape=list(out_shape) + exchange.out_shape,
        input_output_aliases={n_in + i: n_out + o for i, o in exchange.aliases.items()},
        scratch_shapes=list(scratch_shapes) + exchange.sems, compiler_params=compiler_params,
    )(*operands, *[_in_hbm(o) for o in exchange.operands])
    return list(res[:n_out]), list(res[n_out:])


def _in_hbm(a):
    return pltpu.with_memory_space_constraint(a, pltpu.HBM)


def _row_tile(h):
    return max(t for t in range(16, 257, 16) if h % t == 0)


def _cast_shard(a, chip):
    rows, cols = a.shape
    h = rows // 2
    tr = _row_tile(h)

    def body(chip_ref, a_ref, o_ref):
        o_ref[0, 0] = a_ref[0].astype(BF16)

    return pl.pallas_call(
        body, name="cast_shard",
        grid_spec=pltpu.PrefetchScalarGridSpec(
            num_scalar_prefetch=1, grid=(2, h // tr),
            in_specs=[pl.BlockSpec((1, tr, cols), lambda s, r, chip_ref: (s, r, 0))],
            out_specs=pl.BlockSpec((1, 1, tr, cols), lambda s, r, chip_ref: (chip_ref[0], s, r, 0))),
        out_shape=pltpu.HBM((N_CHIP, 2, h, cols), BF16),
        compiler_params=_params(16, 2),
    )(chip, _in_hbm(a.reshape(2, h, cols)))


def _in_proj(x, gain1, w_in_t, exchange=None):
    tm = 256

    def body(x_ref, g_ref, w_ref, z_ref, hn_ref):
        xv = x_ref[...]
        hn = (xv * _rms_scale(xv) * g_ref[...]).astype(BF16)
        hn_ref[...] = hn
        z_ref[...] = _dot_nt(hn, w_ref[...])

    return _call(
        body, (x, gain1, w_in_t), grid=(S // tm,), name="in_proj",
        in_specs=[pl.BlockSpec((tm, D), lambda i: (i, 0)), pl.BlockSpec((1, D), lambda i: (0, 0)),
                  pl.BlockSpec((D_IN, D), lambda i: (0, 0))],
        out_specs=[pl.BlockSpec((tm, D_IN), lambda i: (i, 0)), pl.BlockSpec((tm, D), lambda i: (i, 0))],
        out_shape=[jax.ShapeDtypeStruct((S, D_IN), F32), jax.ShapeDtypeStruct((S, D), BF16)],
        compiler_params=_params(40), exchange=exchange)


def _gelu_parts(v):
    t = jnp.tanh(GELU_C * (v + 0.044715 * (v * v * v)))
    cdf = 0.5 * (1.0 + t)
    return cdf, t


def _band_mask(n):
    a = lax.broadcasted_iota(jnp.int32, (CHUNK, 2 * CHUNK), 0)
    j = lax.broadcasted_iota(jnp.int32, (CHUNK, 2 * CHUNK), 1)
    dist = CHUNK + a - j
    valid = (dist >= 0) & (dist < CHUNK)
    return valid & ((n > 0) | (j >= CHUNK))


def _fill_bias(bucket_ref, table_ref, bias_ref):
    bucket = bucket_ref[...]
    for h in range(N_HEAD):
        acc = jnp.zeros((CHUNK, 2 * CHUNK), F32)
        for b in range(N_BUCKET):
            acc = jnp.where(bucket == b, table_ref[b, h], acc)
        bias_ref[h] = acc


def _fill_tril(ws_ref, wt_ref, wtt_ref=None):
    r = lax.broadcasted_iota(jnp.int32, (CHUNK, CHUNK), 0)
    c = lax.broadcasted_iota(jnp.int32, (CHUNK, CHUNK), 1)
    for g in range(N_GROUP):
        w = jnp.where(c <= r, ws_ref[g], 0.0)
        wt_ref[g] = w.astype(BF16)
        if wtt_ref is not None:
            wtt_ref[g] = w.T.astype(BF16)


def _kv_layouts(kv_prev, kv_cur):
    both = jnp.concatenate([kv_prev, kv_cur], axis=0)
    k = both[:, :128]
    v = both[:, 128:]
    return (k.astype(BF16), pltpu.roll(k, 64, axis=1).astype(BF16),
            v.astype(BF16), pltpu.roll(v, 64, axis=1).astype(BF16))


def _head_place(h):
    pair, pos, kvh = h // 2, h % 2, h // 4
    return pair, pos, kvh == pos


def _softmax_sink(qm, k_use, bias_h, sink, valid):
    s = _dot_nt(qm, k_use) * QK_SCALE + bias_h
    s = jnp.where(valid, s, NEG_INF)
    m = jnp.maximum(jnp.max(s, axis=-1, keepdims=True), sink)
    e = jnp.exp(s - m)
    es = jnp.exp(sink - m)
    denom = jnp.sum(e, axis=-1, keepdims=True) + es
    return e / denom, es / denom


def _mixer_fwd(z, v_gain, w_spatial, b_spatial_t, sinks, rel_table, bucket, exchange=None):
    def body(z_ref, kvp_ref, gain_ref, ws_ref, bt_ref, sink_ref, table_ref, bucket_ref, out_ref, bias_ref, wt_ref):
        n = pl.program_id(0)

        @pl.when(n == 0)
        def _():
            _fill_bias(bucket_ref, table_ref, bias_ref)
            _fill_tril(ws_ref, wt_ref)

        zuv = z_ref[:, :1024]
        cdf, _ = _gelu_parts(zuv)
        guv = zuv * cdf
        for g in range(N_GROUP):
            vg = guv[:, 512 + 128 * g:512 + 128 * (g + 1)]
            vn = vg * _rms_scale(vg) * gain_ref[:, 128 * g:128 * (g + 1)]
            sv = _dot(wt_ref[g], vn.astype(BF16)) + bt_ref[:, g:g + 1]
            out_ref[:, 128 * g:128 * (g + 1)] = (guv[:, 128 * g:128 * (g + 1)] * sv).astype(BF16)

        k_same, k_swap, v_same, v_swap = _kv_layouts(kvp_ref[...], z_ref[:, 1536:1792])
        valid = _band_mask(n)
        lane_half = lax.broadcasted_iota(jnp.int32, (1, 128), 1) // 64
        for pair in range(N_HEAD // 2):
            qq = z_ref[:, 1024 + 128 * pair:1024 + 128 * (pair + 1)]
            acc = jnp.zeros((CHUNK, 128), F32)
            for pos in range(2):
                h = 2 * pair + pos
                _, _, same = _head_place(h)
                qm = jnp.where(lane_half == pos, qq, 0.0).astype(BF16)
                p, _ = _softmax_sink(qm, k_same if same else k_swap, bias_ref[h], sink_ref[h], valid)
                vm = jnp.where(lane_half == pos, v_same if same else v_swap, jnp.zeros((), BF16))
                acc = acc + _dot(p.astype(BF16), vm)
            out_ref[:, 512 + 128 * pair:512 + 128 * (pair + 1)] = acc.astype(BF16)

    return _call(
        body, (z, z, v_gain, w_spatial, b_spatial_t, sinks, rel_table, bucket), grid=(N_BLOCK,), name="mixer_fwd",
        in_specs=[pl.BlockSpec((CHUNK, D_IN), lambda n: (n, 0)),
                  pl.BlockSpec((CHUNK, 256), lambda n: (jnp.maximum(n - 1, 0), 6)),
                  pl.BlockSpec((1, 512), lambda n: (0, 0)),
                  pl.BlockSpec((N_GROUP, CHUNK, CHUNK), lambda n: (0, 0, 0)),
                  pl.BlockSpec((CHUNK, N_GROUP), lambda n: (0, 0)),
                  pl.BlockSpec(memory_space=pltpu.SMEM),
                  pl.BlockSpec(memory_space=pltpu.SMEM),
                  pl.BlockSpec((CHUNK, 2 * CHUNK), lambda n: (0, 0))],
        out_specs=[pl.BlockSpec((CHUNK, D), lambda n: (n, 0))],
        out_shape=[jax.ShapeDtypeStruct((S, D), BF16)],
        scratch_shapes=[pltpu.VMEM((N_HEAD, CHUNK, 2 * CHUNK), F32), pltpu.VMEM((N_GROUP, CHUNK, CHUNK), BF16)],
        compiler_params=_params(32), exchange=exchange)


def _out_proj(x, mix, w_out, gain2, exchange=None):
    tm = 256

    def body(x_ref, mix_ref, w_ref, g_ref, h1_ref, hn_ref, hnt_ref):
        h1 = x_ref[...] + _dot(mix_ref[...], w_ref[...])
        h1_ref[...] = h1
        hn = h1 * _rms_scale(h1) * g_ref[...]
        hn_ref[...] = hn.astype(BF16)
        hnt_ref[...] = hn.T.astype(BF16)

    return _call(
        body, (x, mix, w_out, gain2), grid=(S // tm,), name="out_proj",
        in_specs=[pl.BlockSpec((tm, D), lambda i: (i, 0)), pl.BlockSpec((tm, D), lambda i: (i, 0)),
                  pl.BlockSpec((D, D), lambda i: (0, 0)), pl.BlockSpec((1, D), lambda i: (0, 0))],
        out_specs=[pl.BlockSpec((tm, D), lambda i: (i, 0)), pl.BlockSpec((tm, D), lambda i: (i, 0)),
                   pl.BlockSpec((D, tm), lambda i: (0, i))],
        out_shape=[jax.ShapeDtypeStruct((S, D), F32), jax.ShapeDtypeStruct((S, D), BF16),
                   jax.ShapeDtypeStruct((D, S), BF16)],
        compiler_params=_params(32), exchange=exchange)


def _ffn_up(hn2, w_ff1, exchange=None):
    tm = 512
    nj = D_FF // 1024

    def body(hn_ref, w1_ref, r_ref, a_ref, at_ref):
        r = jnp.maximum(_dot(hn_ref[...], w1_ref[0]), 0.0)
        r_ref[...] = r.astype(BF16)
        a = r * r
        a_ref[...] = a.astype(BF16)
        at_ref[...] = a.T.astype(BF16)

    return _call(
        body, (hn2, w_ff1), grid=(nj, S // tm), name="ffn_up",
        in_specs=[pl.BlockSpec((tm, D), lambda j, i: (i, 0)), pl.BlockSpec((1, D, 1024), lambda j, i: (j, 0, 0))],
        out_specs=[pl.BlockSpec((tm, 1024), lambda j, i: (i, j)), pl.BlockSpec((tm, 1024), lambda j, i: (i, j)),
                   pl.BlockSpec((1024, tm), lambda j, i: (j, i))],
        out_shape=[jax.ShapeDtypeStruct((S, D_FF), BF16), jax.ShapeDtypeStruct((S, D_FF), BF16),
                   jax.ShapeDtypeStruct((D_FF, S), BF16)],
        compiler_params=_params(40, 2), exchange=exchange)


def _ffn_down(h1, a, w_ff2, exchange=None):
    tm = 1024
    nj = D_FF // 1024

    def body(h1_ref, a_ref, w2_ref, h2_ref, acc_ref):
        j = pl.program_id(1)
        part = _dot(a_ref[...], w2_ref[0])

        @pl.when(j == 0)
        def _():
            acc_ref[...] = part

        @pl.when(j > 0)
        def _():
            acc_ref[...] += part

        @pl.when(j == nj - 1)
        def _():
            h2_ref[...] = h1_ref[...] + acc_ref[...]

    return _call(
        body, (h1, a, w_ff2), grid=(S // tm, nj), name="ffn_down",
        in_specs=[pl.BlockSpec((tm, D), lambda i, j: (i, 0)), pl.BlockSpec((tm, 1024), lambda i, j: (i, j)),
                  pl.BlockSpec((1, 1024, D), lambda i, j: (j, 0, 0))],
        out_specs=[pl.BlockSpec((tm, D), lambda i, j: (i, 0))],
        out_shape=[jax.ShapeDtypeStruct((S, D), F32)],
        scratch_shapes=[pltpu.VMEM((tm, D), F32)],
        compiler_params=_params(48, 2), exchange=exchange)


def _tail(h2, p, target, w_gate, w_proj, final_gain):
    tm = 256
    steps = S // tm

    def body(h2_ref, p_ref, t_ref, wg_ref, wp_ref, gf_ref, dh2_ref, dwg_ref, dwp_ref, dgf_ref, loss_ref, dh2b_ref,
             dwp_acc):
        i = pl.program_id(0)
        h2 = h2_ref[...]
        h2b = h2.astype(BF16)
        pb = p_ref[...].astype(BF16)
        gate = jax.nn.sigmoid(_dot(h2b, wg_ref[...]))
        pp = jnp.concatenate([_dot(pb, wp_ref[j]) for j in range(N_CHIP)], axis=1)
        h3 = h2 + gate * pp
        r3 = _rms_scale(h3)
        xhat = h3 * r3
        gf = gf_ref[...]
        err = xhat * gf - t_ref[...]
        dy = err * (1.0 / D)
        dh3 = _rms_bwd(dy * gf, xhat, r3)
        dgp = (dh3 * pp * gate * (1.0 - gate)).astype(BF16)
        dpp = (dh3 * gate).astype(BF16)
        dh2 = dh3 + _dot_nt(dgp, wg_ref[...])
        dh2_ref[...] = dh2
        dh2b_ref[...] = dh2.astype(BF16)
        dwg = _dot_tn(h2b, dgp)
        dwp = _dot_tn(pb, dpp)
        dgf = jnp.sum(dy * xhat, axis=0, keepdims=True)
        sq = jnp.sum(jnp.sum(err * err, axis=1, keepdims=True), axis=0, keepdims=True)

        @pl.when(i == 0)
        def _():
            dwg_ref[...] = dwg
            dwp_acc[...] = dwp
            dgf_ref[...] = dgf
            loss_ref[...] = jnp.broadcast_to(sq, (8, 128))

        @pl.when(i > 0)
        def _():
            dwg_ref[...] += dwg
            dwp_acc[...] += dwp
            dgf_ref[...] += dgf
            loss_ref[...] += jnp.broadcast_to(sq, (8, 128))

        @pl.when(i == steps - 1)
        def _():
            for j in range(N_CHIP):
                dwp_ref[j] = dwp_acc[:, 256 * j:256 * (j + 1)]

    return _call(
        body, (h2, p, target, w_gate, w_proj, final_gain), grid=(steps,), name="tail",
        in_specs=[pl.BlockSpec((tm, D), lambda i: (i, 0)), pl.BlockSpec((tm, PLE), lambda i: (i, 0)),
                  pl.BlockSpec((tm, D), lambda i: (i, 0)), pl.BlockSpec((D, D), lambda i: (0, 0)),
                  pl.BlockSpec((N_CHIP, PLE, 256), lambda i: (0, 0, 0)), pl.BlockSpec((1, D), lambda i: (0, 0))],
        out_specs=[pl.BlockSpec((tm, D), lambda i: (i, 0)), pl.BlockSpec((D, D), lambda i: (0, 0)),
                   pl.BlockSpec((N_CHIP, PLE, 256), lambda i: (0, 0, 0)), pl.BlockSpec((1, D), lambda i: (0, 0)),
                   pl.BlockSpec((8, 128), lambda i: (0, 0)), pl.BlockSpec((tm, D), lambda i: (i, 0))],
        out_shape=[jax.ShapeDtypeStruct((S, D), F32), jax.ShapeDtypeStruct((D, D), F32),
                   jax.ShapeDtypeStruct((N_CHIP, PLE, 256), F32), jax.ShapeDtypeStruct((1, D), F32),
                   jax.ShapeDtypeStruct((8, 128), F32), jax.ShapeDtypeStruct((S, D), BF16)],
        scratch_shapes=[pltpu.VMEM((PLE, D), F32)],
        compiler_params=_params(48))[0]


def _ffn_bwd_down(dh2b, r, a_t, w_ff2, exchange=None):
    tm = 1024
    nj = D_FF // 1024

    def body(dh2_ref, r_ref, at_ref, w2_ref, df_ref, dw2_ref):
        i = pl.program_id(1)
        dh2b = dh2_ref[...]
        da = _dot_nt(dh2b, w2_ref[0])
        df_ref[...] = (da * (2.0 * r_ref[...].astype(F32))).astype(BF16)
        dw2 = _dot(at_ref[...], dh2b)

        @pl.when(i == 0)
        def _():
            dw2_ref[0] = dw2

        @pl.when(i > 0)
        def _():
            dw2_ref[0] += dw2

    return _call(
        body, (dh2b, r, a_t, w_ff2), grid=(nj, S // tm), name="ffn_bwd_down",
        in_specs=[pl.BlockSpec((tm, D), lambda j, i: (i, 0)), pl.BlockSpec((tm, 1024), lambda j, i: (i, j)),
                  pl.BlockSpec((1024, tm), lambda j, i: (j, i)), pl.BlockSpec((1, 1024, D), lambda j, i: (j, 0, 0))],
        out_specs=[pl.BlockSpec((tm, 1024), lambda j, i: (i, j)), pl.BlockSpec((1, 1024, D), lambda j, i: (j, 0, 0))],
        out_shape=[jax.ShapeDtypeStruct((S, D_FF), BF16), jax.ShapeDtypeStruct((nj, 1024, D), F32)],
        compiler_params=_params(48, 2), exchange=exchange)


def _ffn_bwd_up(df, hn2_t, exchange=None):
    tm = 1024
    nj = D_FF // 1024

    def body(df_ref, hnt_ref, dw1_ref):
        i = pl.program_id(1)
        dw1 = _dot(hnt_ref[...], df_ref[...])

        @pl.when(i == 0)
        def _():
            dw1_ref[0] = dw1

        @pl.when(i > 0)
        def _():
            dw1_ref[0] += dw1

    return _call(
        body, (df, hn2_t), grid=(nj, S // tm), name="ffn_bwd_up",
        in_specs=[pl.BlockSpec((tm, 1024), lambda j, i: (i, j)), pl.BlockSpec((D, tm), lambda j, i: (0, i))],
        out_specs=[pl.BlockSpec((1, D, 1024), lambda j, i: (j, 0, 0))],
        out_shape=[jax.ShapeDtypeStruct((nj, D, 1024), F32)],
        compiler_params=_params(40, 2), exchange=exchange)


def _ffn_bwd_input(df, w_ff1, dh2, h1, gain2, mix, w_out, exchange=None):
    tm = 512
    nj = D_FF // 1024
    steps = S // tm

    def body(df_ref, w1_ref, dh2_ref, h1_ref, g_ref, mix_ref, wo_ref, dh1_ref, dmix_ref, dwo_ref, dg_ref, acc_ref):
        i = pl.program_id(0)
        j = pl.program_id(1)
        part = _dot_nt(df_ref[...], w1_ref[0])

        @pl.when(j == 0)
        def _():
            acc_ref[...] = part

        @pl.when(j > 0)
        def _():
            acc_ref[...] += part

        @pl.when(j == nj - 1)
        def _():
            dhn = acc_ref[...]
            h1 = h1_ref[...]
            r2 = _rms_scale(h1)
            xhat = h1 * r2
            dh1 = dh2_ref[...] + _rms_bwd(dhn * g_ref[...], xhat, r2)
            dh1_ref[...] = dh1
            dh1b = dh1.astype(BF16)
            dmix_ref[...] = _dot_nt(dh1b, wo_ref[...])
            dwo = _dot_tn(mix_ref[...], dh1b)
            dg = jnp.sum(dhn * xhat, axis=0, keepdims=True)

            @pl.when(i == 0)
            def _():
                dwo_ref[...] = dwo
                dg_ref[...] = dg

            @pl.when(i > 0)
            def _():
                dwo_ref[...] += dwo
                dg_ref[...] += dg

    return _call(
        body, (df, w_ff1, dh2, h1, gain2, mix, w_out), grid=(steps, nj), name="ffn_bwd_input",
        in_specs=[pl.BlockSpec((tm, 1024), lambda i, j: (i, j)), pl.BlockSpec((1, D, 1024), lambda i, j: (j, 0, 0)),
                  pl.BlockSpec((tm, D), lambda i, j: (i, 0)), pl.BlockSpec((tm, D), lambda i, j: (i, 0)),
                  pl.BlockSpec((1, D), lambda i, j: (0, 0)), pl.BlockSpec((tm, D), lambda i, j: (i, 0)),
                  pl.BlockSpec((D, D), lambda i, j: (0, 0))],
        out_specs=[pl.BlockSpec((tm, D), lambda i, j: (i, 0)), pl.BlockSpec((tm, D), lambda i, j: (i, 0)),
                   pl.BlockSpec((D, D), lambda i, j: (0, 0)), pl.BlockSpec((1, D), lambda i, j: (0, 0))],
        out_shape=[jax.ShapeDtypeStruct((S, D), F32), jax.ShapeDtypeStruct((S, D), F32),
                   jax.ShapeDtypeStruct((D, D), F32), jax.ShapeDtypeStruct((1, D), F32)],
        scratch_shapes=[pltpu.VMEM((tm, D), F32)],
        compiler_params=_params(56, 2), exchange=exchange)


def _mixer_bwd(z, dmix, v_gain, w_spatial, b_spatial_t, sinks, rel_table, bucket, exchange=None):
    def body(z_ref, kvp_ref, dm_ref, gain_ref, ws_ref, bt_ref, sink_ref, table_ref, bucket_ref,
             dz_ref, dws_ref, db_ref, dgain_ref, dsink_ref, drel_ref,
             bias_ref, wt_ref, wtt_ref, dbias_ref, dsv_ref, carry_ref):
        n = pl.program_id(0)

        @pl.when(n == 0)
        def _():
            _fill_bias(bucket_ref, table_ref, bias_ref)
            _fill_tril(ws_ref, wt_ref, wtt_ref)
            dbias_ref[...] = jnp.zeros_like(dbias_ref)
            dsv_ref[...] = jnp.zeros_like(dsv_ref)
            dws_ref[...] = jnp.zeros_like(dws_ref)
            dgain_ref[...] = jnp.zeros_like(dgain_ref)
            dsink_ref[...] = jnp.zeros_like(dsink_ref)

        rows = pl.ds(pl.multiple_of(n * CHUNK, CHUNK), CHUNK)

        zuv = z_ref[:, :1024]
        cdf, t = _gelu_parts(zuv)
        guv = zuv * cdf
        dgelu = cdf + zuv * (0.5 * (1.0 - t * t)) * (GELU_C * (1.0 + 3.0 * 0.044715 * (zuv * zuv)))
        for g in range(N_GROUP):
            lo, hi = 128 * g, 128 * (g + 1)
            u = guv[:, lo:hi]
            vg = guv[:, 512 + lo:512 + hi]
            rr = _rms_scale(vg)
            vhat = vg * rr
            gain = gain_ref[:, lo:hi]
            vnb = (vhat * gain).astype(BF16)
            sv = _dot(wt_ref[g], vnb) + bt_ref[:, g:g + 1]
            da = dm_ref[:, lo:hi]
            dsv = da * u
            dsvb = dsv.astype(BF16)
            dsv_ref[g] += dsv
            dws_ref[g] += _dot_nt(dsvb, vnb)
            dvn = _dot(wtt_ref[g], dsvb)
            dgain_ref[:, lo:hi] += jnp.sum(dvn * vhat, axis=0, keepdims=True)
            dvg = _rms_bwd(dvn * gain, vhat, rr)
            dz_ref[rows, lo:hi] = (da * sv * dgelu[:, lo:hi]).astype(BF16)
            dz_ref[rows, 512 + lo:512 + hi] = (dvg * dgelu[:, 512 + lo:512 + hi]).astype(BF16)

        k_same, k_swap, v_same, v_swap = _kv_layouts(kvp_ref[...], z_ref[:, 1536:1792])
        valid = _band_mask(n)
        lane_half = lax.broadcasted_iota(jnp.int32, (1, 128), 1) // 64
        zero = jnp.zeros((2 * CHUNK, 128), F32)
        dk_same, dk_swap, dv_same, dv_swap = zero, zero, zero, zero
        for pair in range(N_HEAD // 2):
            cols = slice(1024 + 128 * pair, 1024 + 128 * (pair + 1))
            qq = z_ref[:, cols]
            do_pair = dm_ref[:, 512 + 128 * pair:512 + 128 * (pair + 1)]
            dq = jnp.zeros((CHUNK, 128), F32)
            for pos in range(2):
                h = 2 * pair + pos
                _, _, same = _head_place(h)
                on_half = lane_half == pos
                qm = jnp.where(on_half, qq, 0.0).astype(BF16)
                k_use = k_same if same else k_swap
                v_use = v_same if same else v_swap
                p, p_sink = _softmax_sink(qm, k_use, bias_ref[h], sink_ref[h], valid)
                dom = jnp.where(on_half, do_pair, 0.0).astype(BF16)
                dp = _dot_nt(dom, v_use)
                dsum = jnp.sum(p * dp, axis=-1, keepdims=True)
                ds = p * (dp - dsum)
                dbias_ref[h] += ds
                dsink_ref[h:h + 1, :] += jnp.broadcast_to(jnp.sum(-p_sink * dsum, axis=0, keepdims=True), (1, 128))
                dsb = ds.astype(BF16)
                dq = dq + jnp.where(on_half, _dot(dsb, k_use), 0.0)
                dk_h = _dot_tn(dsb, qm)
                dv_h = _dot_tn(p.astype(BF16), dom)
                if same:
                    dk_same, dv_same = dk_same + dk_h, dv_same + dv_h
                else:
                    dk_swap, dv_swap = dk_swap + dk_h, dv_swap + dv_h
            dz_ref[rows, cols] = (dq * QK_SCALE).astype(BF16)
        dk = (dk_same + pltpu.roll(dk_swap, 64, axis=1)) * QK_SCALE
        dv = dv_same + pltpu.roll(dv_swap, 64, axis=1)
        dkv = jnp.concatenate([dk, dv], axis=1)

        @pl.when(n > 0)
        def _():
            prev_rows = pl.ds(pl.multiple_of((n - 1) * CHUNK, CHUNK), CHUNK)
            dz_ref[prev_rows, 1536:1792] = (carry_ref[...] + dkv[:CHUNK]).astype(BF16)

        carry_ref[...] = dkv[CHUNK:]

        @pl.when(n == N_BLOCK - 1)
        def _():
            dz_ref[rows, 1536:1792] = dkv[CHUNK:].astype(BF16)
            r = lax.broadcasted_iota(jnp.int32, (CHUNK, CHUNK), 0)
            c = lax.broadcasted_iota(jnp.int32, (CHUNK, CHUNK), 1)
            for g in range(N_GROUP):
                dws_ref[g] = jnp.where(c <= r, dws_ref[g], 0.0)
                db_ref[g] = jnp.sum(dsv_ref[g], axis=1, keepdims=True)
            bucket = bucket_ref[...]
            for h in range(N_HEAD):
                dbh = dbias_ref[h]
                per_bucket = [jnp.sum(jnp.where(bucket == b, dbh, 0.0), axis=0, keepdims=True) for b in range(N_BUCKET)]
                drel_ref[h] = jnp.sum(jnp.concatenate(per_bucket, axis=0), axis=1, keepdims=True)

    return _call(
        body, (z, z, dmix, v_gain, w_spatial, b_spatial_t, sinks, rel_table, bucket), grid=(N_BLOCK,), name="mixer_bwd",
        in_specs=[pl.BlockSpec((CHUNK, D_IN), lambda n: (n, 0)),
                  pl.BlockSpec((CHUNK, 256), lambda n: (jnp.maximum(n - 1, 0), 6)),
                  pl.BlockSpec((CHUNK, D), lambda n: (n, 0)),
                  pl.BlockSpec((1, 512), lambda n: (0, 0)),
                  pl.BlockSpec((N_GROUP, CHUNK, CHUNK), lambda n: (0, 0, 0)),
                  pl.BlockSpec((CHUNK, N_GROUP), lambda n: (0, 0)),
                  pl.BlockSpec(memory_space=pltpu.SMEM),
                  pl.BlockSpec(memory_space=pltpu.SMEM),
                  pl.BlockSpec((CHUNK, 2 * CHUNK), lambda n: (0, 0))],
        out_specs=[pl.BlockSpec((S, D_IN), lambda n: (0, 0)),
                   pl.BlockSpec((N_GROUP, CHUNK, CHUNK), lambda n: (0, 0, 0)),
                   pl.BlockSpec((N_GROUP, CHUNK, 1), lambda n: (0, 0, 0)),
                   pl.BlockSpec((1, 512), lambda n: (0, 0)),
                   pl.BlockSpec((N_HEAD, 128), lambda n: (0, 0)),
                   pl.BlockSpec((N_HEAD, N_BUCKET, 1), lambda n: (0, 0, 0))],
        out_shape=[jax.ShapeDtypeStruct((S, D_IN), BF16), jax.ShapeDtypeStruct((N_GROUP, CHUNK, CHUNK), F32),
                   jax.ShapeDtypeStruct((N_GROUP, CHUNK, 1), F32), jax.ShapeDtypeStruct((1, 512), F32),
                   jax.ShapeDtypeStruct((N_HEAD, 128), F32), jax.ShapeDtypeStruct((N_HEAD, N_BUCKET, 1), F32)],
        scratch_shapes=[pltpu.VMEM((N_HEAD, CHUNK, 2 * CHUNK), F32), pltpu.VMEM((N_GROUP, CHUNK, CHUNK), BF16),
                        pltpu.VMEM((N_GROUP, CHUNK, CHUNK), BF16), pltpu.VMEM((N_HEAD, CHUNK, 2 * CHUNK), F32),
                        pltpu.VMEM((N_GROUP, CHUNK, CHUNK), F32), pltpu.VMEM((CHUNK, 256), F32)],
        compiler_params=_params(48), exchange=exchange)


def _in_bwd_weight(dz, hn1, exchange=None):
    tm = 512

    def body(dz_ref, hn_ref, dw_ref):
        i = pl.program_id(0)
        dw = _dot_tn(dz_ref[...], hn_ref[...])

        @pl.when(i == 0)
        def _():
            dw_ref[...] = dw

        @pl.when(i > 0)
        def _():
            dw_ref[...] += dw

    return _call(
        body, (dz, hn1), grid=(S // tm,), name="in_bwd_weight",
        in_specs=[pl.BlockSpec((tm, D_IN), lambda i: (i, 0)), pl.BlockSpec((tm, D), lambda i: (i, 0))],
        out_specs=[pl.BlockSpec((D_IN, D), lambda i: (0, 0))],
        out_shape=[jax.ShapeDtypeStruct((D_IN, D), F32)],
        compiler_params=_params(40), exchange=exchange)


def _in_bwd_input(dz, w_in_t, x, dh1, gain1, exchange=None):
    tm = 512

    def body(dz_ref, w_ref, x_ref, dh1_ref, g_ref, dx_ref, dg_ref):
        i = pl.program_id(0)
        dhn = _dot(dz_ref[...], w_ref[...])
        xv = x_ref[...]
        r1 = _rms_scale(xv)
        xhat = xv * r1
        dx_ref[...] = dh1_ref[...] + _rms_bwd(dhn * g_ref[...], xhat, r1)
        dg = jnp.sum(dhn * xhat, axis=0, keepdims=True)

        @pl.when(i == 0)
        def _():
            dg_ref[...] = dg

        @pl.when(i > 0)
        def _():
            dg_ref[...] += dg

    return _call(
        body, (dz, w_in_t, x, dh1, gain1), grid=(S // tm,), name="in_bwd_input",
        in_specs=[pl.BlockSpec((tm, D_IN), lambda i: (i, 0)), pl.BlockSpec((D_IN, D), lambda i: (0, 0)),
                  pl.BlockSpec((tm, D), lambda i: (i, 0)), pl.BlockSpec((tm, D), lambda i: (i, 0)),
                  pl.BlockSpec((1, D), lambda i: (0, 0))],
        out_specs=[pl.BlockSpec((tm, D), lambda i: (i, 0)), pl.BlockSpec((1, D), lambda i: (0, 0))],
        out_shape=[jax.ShapeDtypeStruct((S, D), F32), jax.ShapeDtypeStruct((1, D), F32)],
        compiler_params=_params(48), exchange=exchange)


def _rel_bucket():
    a = jnp.arange(CHUNK)[:, None]
    j = jnp.arange(2 * CHUNK)[None, :]
    n = jnp.maximum(CHUNK + a - j, 0)
    max_exact = N_BUCKET // 2
    nf = jnp.maximum(n, 1).astype(jnp.float32)
    large = max_exact + (jnp.log(nf / max_exact) / math.log(CHUNK / max_exact) * (N_BUCKET - max_exact)).astype(jnp.int32)
    large = jnp.minimum(large, N_BUCKET - 1)
    return jnp.where(n < max_exact, n, large).astype(jnp.int32)


def _step(x, p, target, small, bufs, place):
    bucket = _rel_bucket()
    sinks = small["attn_sinks"].reshape(N_HEAD)
    b_t = jnp.transpose(small["b_spatial"].reshape(N_GROUP, CHUNK))
    ws = small["w_spatial"].reshape(N_GROUP, CHUNK, CHUNK)
    gain1, gain2 = small["norm1_gain"], small["norm2_gain"]
    v_gain = small["gmlp_v_gain"]
    final_gain = small["final_gain"].reshape(1, D)
    table = small["rel_bias_table"]
    bufs = dict(bufs)

    def gather(*names):
        return _Gather([bufs[n] for n in names])

    def took(names, got):
        bufs.update(zip(names, got))

    took(["w_in"], _gather_weights([bufs["w_in"]]))
    w_in_t = _whole(bufs["w_in"]).reshape(D_IN, D)
    (z, hn1), got = _in_proj(x, gain1, w_in_t, gather("w_out"))
    took(["w_out"], got)
    (mix,), got = _mixer_fwd(z, v_gain, ws, b_t, sinks, table, bucket, gather("w_ff1"))
    took(["w_ff1"], got)
    w_out = _whole(bufs["w_out"]).reshape(D, D)
    (h1, hn2, hn2_t), _ = _out_proj(x, mix, w_out, gain2)
    w_ff1 = _whole(bufs["w_ff1"])
    (r, a, a_t), got = _ffn_up(hn2, w_ff1, gather("w_ff2"))
    took(["w_ff2"], got)
    w_ff2 = _whole(bufs["w_ff2"])
    (h2,), got = _ffn_down(h1, a, w_ff2, gather("w_ple_gate", "w_ple_proj"))
    took(["w_ple_gate", "w_ple_proj"], got)
    dh2, d_gate, d_proj, d_final, sq, dh2b = _tail(h2, p, target, _whole(bufs["w_ple_gate"]).reshape(D, D),
                                                   _whole(bufs["w_ple_proj"]), final_gain)

    def pair_sums(halves, from_sibling):
        sums, landing = zip(*[_pair_sum(g, o, place) for g, o in zip(halves, from_sibling)])
        return list(sums), list(landing)

    landed = {}
    halves = [_halves(d_gate.reshape(N_CHIP, 256, D)), _halves(d_proj)]
    ex = _ChipExchange(*pair_sums(halves, _sibling_exchange(halves)))
    (df, d_ff2), got = _ffn_bwd_down(dh2b, r, a_t, w_ff2, ex)
    landed.update(zip(["w_ple_gate", "w_ple_proj"], got))
    halves_ff2 = [_halves(d_ff2)]
    (d_ff1,), got = _ffn_bwd_up(df, hn2_t, _SiblingExchange(halves_ff2))
    halves_ff1 = [_halves(d_ff1)]
    ex = _Both(_ChipExchange(*pair_sums(halves_ff2, got)), _SiblingExchange(halves_ff1))
    (dh1, dmix, d_out, d_gain2), got = _ffn_bwd_input(df, w_ff1, dh2, h1, gain2, mix, w_out, ex)
    landed["w_ff2"] = got[0]
    halves_out = [_halves(d_out.reshape(N_CHIP, 256, D))]
    ex = _ChipExchange(*pair_sums(halves_ff1 + halves_out, list(got[1:]) + list(_sibling_exchange(halves_out))))
    (dz, d_ws, d_b, d_vgain, d_sink, d_rel), got = _mixer_bwd(z, dmix, v_gain, ws, b_t, sinks, table, bucket, ex)
    landed.update(zip(["w_ff1", "w_out"], got))
    small_grads = {
        "gmlp_v_gain": d_vgain, "w_spatial": d_ws.reshape(1, N_GROUP, CHUNK, CHUNK),
        "b_spatial": d_b.reshape(1, N_GROUP, CHUNK), "attn_sinks": d_sink[:, 0].reshape(1, N_HEAD),
        "rel_bias_table": jnp.transpose(d_rel.reshape(N_HEAD, N_BUCKET)), "norm2_gain": d_gain2,
        "final_gain": d_final.reshape(D),
    }
    (d_in_t,), _ = _in_bwd_weight(dz, hn1)
    halves = [_halves(d_in_t.reshape(N_CHIP, 448, D))]
    ex = _ChipExchange(*pair_sums(halves, _sibling_exchange(halves)))
    (dx, small_grads["norm1_gain"]), got = _in_bwd_input(dz, w_in_t, x, dh1, gain1, ex)
    landed.update(zip(["w_in"], got))
    return dx, landed, small_grads, sq


HBM_SPEC = pl.BlockSpec(memory_space=pltpu.HBM)
VMEM_SPEC = pl.BlockSpec(memory_space=pltpu.VMEM)


def _mesh_place():
    x, y, c = lax.axis_index("x"), lax.axis_index("y"), lax.axis_index("c")
    others = [(1 - x, y), (x, 1 - y), (1 - x, 1 - y)]
    return x, y, c, others


def _remote(src, dst, send_sem, recv_sem, device):
    return pltpu.make_async_remote_copy(src_ref=src, dst_ref=dst, send_sem=send_sem, recv_sem=recv_sem,
                                        device_id=device, device_id_type=MESH)


def _hbm_like(a, shape=None, dtype=None):
    return pltpu.HBM(a.shape if shape is None else shape, a.dtype if dtype is None else dtype)


def _gather_start(bufs, send_sems, recv_sems):
    x, y, c, others = _mesh_place()
    me = 2 * x + y
    for w, buf in enumerate(bufs):
        for k in range(3):
            mine = buf.at[me, c]
            _remote(mine, mine, send_sems.at[w, k], recv_sems.at[w, k], (*others[k], c)).start()


def _gather_finish(bufs, send_sems, recv_sems):
    x, y, c, others = _mesh_place()
    me = 2 * x + y
    sibling = (x, y, 1 - c)
    idx = [2 * ox + oy for ox, oy in others]
    chips = range(3)
    for w, buf in enumerate(bufs):
        for k in chips:
            landed = buf.at[idx[k], c]
            _remote(landed, landed, send_sems.at[w, k], recv_sems.at[w, k], sibling).wait_recv()
            _remote(landed, landed, send_sems.at[w, 3 + k], recv_sems.at[w, 3 + k], sibling).start()
    for w, buf in enumerate(bufs):
        for k in chips:
            landed = buf.at[idx[k], 1 - c]
            _remote(landed, landed, send_sems.at[w, 3 + k], recv_sems.at[w, 3 + k], sibling).wait_recv()
    for w, buf in enumerate(bufs):
        for k in chips:
            mine, passed = buf.at[me, c], buf.at[idx[k], c]
            _remote(mine, mine, send_sems.at[w, k], recv_sems.at[w, k], sibling).wait_send()
            _remote(passed, passed, send_sems.at[w, 3 + k], recv_sems.at[w, 3 + k], sibling).wait_send()


def _gather_sems(n):
    return [pltpu.SemaphoreType.DMA((n, 6)), pltpu.SemaphoreType.DMA((n, 6))]


def _gather_weights(bufs):
    n = len(bufs)

    def body(*refs):
        outs = refs[n:2 * n]
        send_sems, recv_sems = refs[2 * n:]
        _gather_start(outs, send_sems, recv_sems)
        _gather_finish(outs, send_sems, recv_sems)

    return pl.pallas_call(
        body, name="gather_weights",
        in_specs=[HBM_SPEC] * n, out_specs=[HBM_SPEC] * n,
        out_shape=[_hbm_like(b) for b in bufs],
        input_output_aliases={w: w for w in range(n)},
        scratch_shapes=_gather_sems(n),
    )(*bufs)


def _sibling_copies(grads, landing, send_sems, recv_sems):
    x, y, c, _ = _mesh_place()
    return [_remote(grads[w].at[j, 1 - c], landing[w].at[j], send_sems.at[w, j], recv_sems.at[w, j], (x, y, 1 - c))
            for w in range(len(grads)) for j in range(N_CHIP)]


def _sibling_exchange_start(grads, landing, send_sems, recv_sems):
    for cp in _sibling_copies(grads, landing, send_sems, recv_sems):
        cp.start()


def _sibling_exchange_finish(grads, landing, send_sems, recv_sems):
    copies = _sibling_copies(grads, landing, send_sems, recv_sems)
    for cp in copies:
        cp.wait_recv()
    for cp in copies:
        cp.wait_send()


def _sibling_exchange_sems(n):
    return [pltpu.SemaphoreType.DMA((n, N_CHIP)), pltpu.SemaphoreType.DMA((n, N_CHIP))]


def _sibling_exchange(grads):
    n = len(grads)

    def body(*refs):
        ins, outs = refs[:n], refs[n:2 * n]
        _sibling_exchange_start(ins, outs, *refs[2 * n:])
        _sibling_exchange_finish(ins, outs, *refs[2 * n:])

    return pl.pallas_call(
        body, name="sibling_exchange",
        in_specs=[HBM_SPEC] * n, out_specs=[HBM_SPEC] * n,
        out_shape=[_hbm_like(g, (N_CHIP,) + g.shape[2:]) for g in grads],
        scratch_shapes=_sibling_exchange_sems(n),
    )(*[_in_hbm(g) for g in grads])


def _chip_exchange_start(sums, landing, send_sems, recv_sems):
    x, y, c, others = _mesh_place()
    me = 2 * x + y
    for w in range(len(sums)):
        for k, (ox, oy) in enumerate(others):
            _remote(sums[w].at[2 * ox + oy], landing[w].at[me], send_sems.at[w, k], recv_sems.at[w, k],
                    (ox, oy, c)).start()


def _chip_exchange_finish(sums, landing, send_sems, recv_sems):
    x, y, c, others = _mesh_place()
    for w in range(len(sums)):
        for k, (ox, oy) in enumerate(others):
            piece = landing[w].at[2 * ox + oy]
            _remote(piece, piece, send_sems.at[w, k], recv_sems.at[w, k], (x, y, c)).wait_recv()
    for w in range(len(sums)):
        for k, (ox, oy) in enumerate(others):
            piece = sums[w].at[2 * ox + oy]
            _remote(piece, piece, send_sems.at[w, k], recv_sems.at[w, k], (x, y, c)).wait_send()


def _chip_exchange_sems(n):
    return [pltpu.SemaphoreType.DMA((n, 3)), pltpu.SemaphoreType.DMA((n, 3))]


def _chip_exchange(sums, landing):
    n = len(sums)

    def body(*refs):
        ins, outs = refs[:n], refs[2 * n:3 * n]
        send_sems, recv_sems = refs[3 * n:]
        _chip_exchange_start(ins, outs, send_sems, recv_sems)
        _chip_exchange_finish(ins, outs, send_sems, recv_sems)

    return pl.pallas_call(
        body, name="chip_exchange",
        in_specs=[HBM_SPEC] * (2 * n), out_specs=[HBM_SPEC] * n,
        out_shape=[_hbm_like(b) for b in landing],
        input_output_aliases={n + w: w for w in range(n)},
        scratch_shapes=_chip_exchange_sems(n),
    )(*sums, *landing)


def _sibling_allgather(bufs, also):
    n = len(bufs)
    k_in, k_out = len(also.operands), also.n_out

    def body(*refs):
        ex_ins, refs = refs[n:n + k_in], refs[n + k_in:]
        outs, refs = refs[:n], refs[n:]
        ex_outs, refs = refs[:k_out], refs[k_out:]
        send_sems, recv_sems, ex_sems = refs[0], refs[1], refs[2:]
        x, y, c, _ = _mesh_place()
        sibling = (x, y, 1 - c)
        also.start(ex_ins, ex_outs, ex_sems)
        sends = [_remote(outs[w].at[c], outs[w].at[c], send_sems.at[w], recv_sems.at[w], sibling) for w in range(n)]
        for cp in sends:
            cp.start()
        for w in range(n):
            landed = outs[w].at[1 - c]
            _remote(landed, landed, send_sems.at[w], recv_sems.at[w], sibling).wait_recv()
        for cp in sends:
            cp.wait_send()
        also.finish(ex_ins, ex_outs, ex_sems)

    res = pl.pallas_call(
        body, name="sibling_allgather",
        in_specs=[HBM_SPEC] * (n + k_in), out_specs=[HBM_SPEC] * (n + k_out),
        out_shape=[_hbm_like(b) for b in bufs] + also.out_shape,
        input_output_aliases={**{w: w for w in range(n)}, **{n + i: n + o for i, o in also.aliases.items()}},
        scratch_shapes=[pltpu.SemaphoreType.DMA((n,)), pltpu.SemaphoreType.DMA((n,))] + also.sems,
    )(*bufs, *[_in_hbm(o) for o in also.operands])
    return list(res[:n]), list(res[n:])


def _pair_sum(grad, other, place):
    _, _, h, cols = grad.shape
    tr = _row_tile(h)

    def body(place_ref, g_ref, o_ref, sums_ref, own_ref):
        s = (g_ref[0, 0] + o_ref[0]).astype(BF16)
        sums_ref[0] = s

        @pl.when(pl.program_id(1) == place_ref[0])
        def _():
            own_ref[0] = s

    return pl.pallas_call(
        body, name="pair_sum",
        grid_spec=pltpu.PrefetchScalarGridSpec(
            num_scalar_prefetch=1, grid=(h // tr, N_CHIP),
            in_specs=[pl.BlockSpec((1, 1, tr, cols), lambda r, j, place_ref: (j, place_ref[1], r, 0)),
                      pl.BlockSpec((1, tr, cols), lambda r, j, place_ref: (j, r, 0))],
            out_specs=[pl.BlockSpec((1, tr, cols), lambda r, j, place_ref: (j, r, 0)),
                       pl.BlockSpec((1, tr, cols), lambda r, j, place_ref: (place_ref[0], r, 0))]),
        out_shape=[pltpu.HBM((N_CHIP, h, cols), BF16)] * 2,
        compiler_params=_params(16, 2),
    )(place, _in_hbm(grad), _in_hbm(other))


def _chip_sum(parts, place):
    _, h, cols = parts.shape
    tr = _row_tile(h)

    def body(place_ref, p_ref, out_ref):
        out_ref[0] = ((p_ref[0].astype(F32) + p_ref[1].astype(F32)) + p_ref[2].astype(F32)) + p_ref[3].astype(F32)

    return pl.pallas_call(
        body, name="chip_sum",
        grid_spec=pltpu.PrefetchScalarGridSpec(
            num_scalar_prefetch=1, grid=(h // tr,),
            in_specs=[pl.BlockSpec((N_CHIP, tr, cols), lambda r, place_ref: (0, r, 0))],
            out_specs=pl.BlockSpec((1, tr, cols), lambda r, place_ref: (place_ref[1], r, 0))),
        out_shape=pltpu.HBM((2, h, cols), F32),
        compiler_params=_params(16),
    )(place, _in_hbm(parts))


def _adamw_math(w, g, m, v):
    m = ADAM_B1 * m + (1.0 - ADAM_B1) * g
    v = ADAM_B2 * v + (1.0 - ADAM_B2) * (g * g)
    m_hat = m / (1.0 - ADAM_B1 ** ADAM_STEP)
    v_hat = v / (1.0 - ADAM_B2 ** ADAM_STEP)
    delta = -ADAM_LR * (m_hat / (jnp.sqrt(v_hat) + ADAM_EPS) + ADAM_WD * w)
    return delta, m, v


def _adamw(w, g, m, v, exchange=None):
    rows, cols = w.shape
    tr = _row_tile(rows)

    def body(w_ref, g_ref, m_ref, v_ref, d_ref, nm_ref, nv_ref, g_out_ref):
        g = g_ref[...]
        d_ref[...], nm_ref[...], nv_ref[...] = _adamw_math(w_ref[...], g, m_ref[...], v_ref[...])
        g_out_ref[...] = g

    spec = pl.BlockSpec((tr, cols), lambda r: (r, 0))
    return _call(
        body, (w, g, m, v), grid=(rows // tr,), name="adamw",
        in_specs=[spec] * 4, out_specs=[spec] * 4,
        out_shape=[jax.ShapeDtypeStruct((rows, cols), F32)] * 4,
        compiler_params=_params(32), exchange=exchange)


SMALL_NAMES = ("norm1_gain", "gmlp_v_gain", "w_spatial", "b_spatial", "attn_sinks", "rel_bias_table", "norm2_gain",
               "final_gain")
PACK_TILE = 8 * 128


def _pack_small(arrays):
    parts = []
    for a in arrays:
        flat = a.reshape(-1)
        rows = -(-flat.shape[0] // PACK_TILE) * 8
        parts.append(jnp.pad(flat, (0, rows * 128 - flat.shape[0])).reshape(rows, 128))
    return jnp.concatenate(parts, axis=0)


def _unpack_small(packed, like):
    out, row = [], 0
    for a in like:
        size = math.prod(a.shape)
        rows = -(-size // PACK_TILE) * 8
        out.append(packed[row:row + rows].reshape(-1)[:size].reshape(a.shape))
        row += rows
    return out


def _small_update(gathered, w, m, v):
    rows = gathered.shape[1]

    def body(g_ref, w_ref, m_ref, v_ref, tot_ref, d_ref, nm_ref, nv_ref):
        total = g_ref[0].astype(F32)
        for dev in range(1, 8):
            total = total + g_ref[dev].astype(F32)
        tot_ref[...] = total
        d_ref[...], nm_ref[...], nv_ref[...] = _adamw_math(w_ref[...], total, m_ref[...], v_ref[...])

    return pl.pallas_call(
        body, name="small_update",
        in_specs=[VMEM_SPEC] * 4, out_specs=[VMEM_SPEC] * 4,
        out_shape=[jax.ShapeDtypeStruct((rows, 128), F32)] * 4,
        compiler_params=pltpu.CompilerParams(vmem_limit_bytes=24 * MIB),
    )(gathered, w, m, v)


def _halves(a):
    return a.reshape(a.shape[:-2] + (2, a.shape[-2] // 2, a.shape[-1]))


def _whole(a):
    return a.reshape(a.shape[:-3] + (2 * a.shape[-2], a.shape[-1]))


def kernel(x, p, norm1_gain, w_in, gmlp_v_gain, w_spatial, b_spatial, attn_sinks, rel_bias_table, w_out, norm2_gain, w_ff1, w_ff2, w_ple_proj, w_ple_gate, final_gain, loss_target, m_norm1_gain, m_w_in, m_gmlp_v_gain, m_w_spatial, m_b_spatial, m_attn_sinks, m_rel_bias_table, m_w_out, m_norm2_gain, m_w_ff1, m_w_ff2, m_w_ple_proj, m_w_ple_gate, m_final_gain, v_norm1_gain, v_w_in, v_gmlp_v_gain, v_w_spatial, v_b_spatial, v_attn_sinks, v_rel_bias_table, v_w_out, v_norm2_gain, v_w_ff1, v_w_ff2, v_w_ple_proj, v_w_ple_gate, v_final_gain):
    given = dict(locals())
    small = {n: given[n] for n in SMALL_NAMES}
    chip = 2 * lax.axis_index("x") + lax.axis_index("y")
    place = jnp.stack([chip, lax.axis_index("c")]).astype(jnp.int32)

    big_names = ("w_in", "w_out", "w_ff1", "w_ff2", "w_ple_proj", "w_ple_gate")
    shards = {n: given[n][0] for n in big_names}
    travel = dict(shards, w_in=jnp.transpose(shards["w_in"]))
    bufs = {n: _cast_shard(travel[n], place[:1]) for n in big_names}
    dx, landed, small_grads, sq = _step(x[0], p[0, 0], loss_target[0], small, bufs, place)

    out_grad, out_delta, out_m, out_v = {}, {}, {}, {}

    def update(n, g, exchange=None):
        to = jnp.transpose if n == "w_in" else (lambda a: a)
        (delta, new_m, new_v, g_out), got = _adamw(to(shards[n]), g, to(given["m_" + n][0]), to(given["v_" + n][0]),
                                                   exchange)
        out_grad[n], out_delta[n], out_m[n], out_v[n] = [to(a)[None] for a in (g_out, delta, new_m, new_v)]
        return got

    spare = jnp.zeros((8, 128), F32)
    small_packed = _pack_small([small_grads[n] for n in SMALL_NAMES] + [spare]).astype(BF16)
    reduced, (small_gathered, sq_gathered) = _sibling_allgather(
        [_chip_sum(landed[n], place) for n in big_names], _Both(_GatherAll(small_packed), _GatherAll(sq)))
    for n, r in zip(big_names, reduced):
        update(n, _whole(r))

    like = [given[n] for n in SMALL_NAMES] + [spare]
    packed = _small_update(small_gathered, *[_pack_small([given[pre + n] for n in SMALL_NAMES] + [spare])
                                             for pre in ("", "m_", "v_")])
    for res, out in zip(packed, (out_grad, out_delta, out_m, out_v)):
        out.update(zip(SMALL_NAMES, _unpack_small(res, like)))
    loss = 0.5 * jnp.sum(sq_gathered[:, 0, 0]) / D

    order = ("norm1_gain", "w_in", "gmlp_v_gain", "w_spatial", "b_spatial", "attn_sinks", "rel_bias_table", "w_out",
             "norm2_gain", "w_ff1", "w_ff2", "w_ple_proj", "w_ple_gate", "final_gain")
    return (loss, dx[None], *[out_grad[n] for n in order], *[out_delta[n] for n in order],
            *[out_m[n] for n in order], *[out_v[n] for n in order])
```

```python
import functools
import math

import jax
import jax.numpy as jnp
from jax import lax
from jax.experimental import pallas as pl
from jax.experimental.pallas import tpu as pltpu

S = 2048
D = 1024
D_IN = 1792
D_FF = 4096
PLE = 256
N_CHIP = 4
N_GROUP = 4
CHUNK = 128
N_HEAD = 8
N_BLOCK = S // CHUNK
N_BUCKET = 32
EPS = 1e-6
NEG_INF = -1e30
QK_SCALE = 0.125
GELU_C = math.sqrt(2.0 / math.pi)

ADAM_LR = 0.001
ADAM_B1 = 0.9
ADAM_B2 = 0.999
ADAM_EPS = 1e-08
ADAM_WD = 0.01
ADAM_STEP = 10

F32 = jnp.float32
BF16 = jnp.bfloat16
MIB = 1024 * 1024
MESH = pl.DeviceIdType.MESH

NT = (((1,), (1,)), ((), ()))
TN = (((0,), (0,)), ((), ()))


def _dot(a, b):
    return jnp.dot(a, b, preferred_element_type=F32)


def _dot_nt(a, b):
    return lax.dot_general(a, b, NT, preferred_element_type=F32)


def _dot_tn(a, b):
    return lax.dot_general(a, b, TN, preferred_element_type=F32)


def _params(vmem_mib, n_axes=1):
    return pltpu.CompilerParams(dimension_semantics=("arbitrary",) * n_axes, vmem_limit_bytes=vmem_mib * MIB)


def _rms_scale(v):
    return lax.rsqrt(jnp.mean(v * v, axis=-1, keepdims=True) + EPS)


def _rms_bwd(dy_gain, xhat, r):
    return r * (dy_gain - xhat * jnp.mean(dy_gain * xhat, axis=-1, keepdims=True))


class _Gather:
    def __init__(self, bufs):
        self.operands = list(bufs)
        self.n_out = len(self.operands)
        self.out_shape = [_hbm_like(b) for b in bufs]
        self.aliases = {w: w for w in range(self.n_out)}
        self.sems = _gather_sems(self.n_out)

    def start(self, ins, outs, sems):
        _gather_start(outs, *sems)

    def finish(self, ins, outs, sems):
        _gather_finish(outs, *sems)


class _ChipExchange:
    def __init__(self, sums, landing):
        self.n_out = len(landing)
        self.operands = list(sums) + list(landing)
        self.out_shape = [_hbm_like(b) for b in landing]
        self.aliases = {self.n_out + w: w for w in range(self.n_out)}
        self.sems = _chip_exchange_sems(self.n_out)

    def start(self, ins, outs, sems):
        _chip_exchange_start(ins[:self.n_out], outs, *sems)

    def finish(self, ins, outs, sems):
        _chip_exchange_finish(ins[:self.n_out], outs, *sems)


class _GatherAll:
    def __init__(self, packed):
        self.operands = [packed]
        self.n_out = 1
        self.out_shape = [_hbm_like(packed, (8,) + packed.shape)]
        self.aliases = {}
        self.sems = [pltpu.SemaphoreType.DMA((8,)), pltpu.SemaphoreType.DMA((8,))]

    def _copies(self, ins, outs, sems):
        x, y, c, _ = _mesh_place()
        me = 4 * x + 2 * y + c
        send_sems, recv_sems = sems
        copies = []
        for k in range(1, 8):
            peer = (1 - x if k // 4 else x, 1 - y if (k // 2) % 2 else y, 1 - c if k % 2 else c)
            src = 4 * peer[0] + 2 * peer[1] + peer[2]
            copies.append((_remote(ins[0], outs[0].at[me], send_sems.at[k], recv_sems.at[k], peer), outs[0].at[src]))
        own = pltpu.make_async_copy(ins[0], outs[0].at[me], send_sems.at[0])
        return own, copies

    def start(self, ins, outs, sems):
        own, copies = self._copies(ins, outs, sems)
        own.start()
        for cp, _ in copies:
            cp.start()

    def finish(self, ins, outs, sems):
        own, copies = self._copies(ins, outs, sems)
        x, y, c, _ = _mesh_place()
        for k, (cp, landed) in enumerate(copies):
            _remote(landed, landed, sems[0].at[k + 1], sems[1].at[k + 1], (x, y, c)).wait_recv()
        for cp, _ in copies:
            cp.wait_send()
        own.wait()


class _Both:
    def __init__(self, a, b):
        self.a, self.b = a, b
        self.operands = a.operands + b.operands
        self.n_out = a.n_out + b.n_out
        self.out_shape = a.out_shape + b.out_shape
        self.aliases = dict(a.aliases)
        self.aliases.update({len(a.operands) + i: a.n_out + o for i, o in b.aliases.items()})
        self.sems = a.sems + b.sems

    def _split(self, ins, outs, sems):
        ka, na, sa = len(self.a.operands), self.a.n_out, len(self.a.sems)
        return (ins[:ka], outs[:na], sems[:sa]), (ins[ka:], outs[na:], sems[sa:])

    def start(self, ins, outs, sems):
        for ex, args in zip((self.a, self.b), self._split(ins, outs, sems)):
            ex.start(*args)

    def finish(self, ins, outs, sems):
        for ex, args in zip((self.a, self.b), self._split(ins, outs, sems)):
            ex.finish(*args)


class _SiblingExchange:
    def __init__(self, grads):
        self.operands = list(grads)
        self.n_out = len(self.operands)
        self.out_shape = [_hbm_like(g, (N_CHIP,) + g.shape[2:]) for g in grads]
        self.aliases = {}
        self.sems = _sibling_exchange_sems(self.n_out)

    def start(self, ins, outs, sems):
        _sibling_exchange_start(ins, outs, *sems)

    def finish(self, ins, outs, sems):
        _sibling_exchange_finish(ins, outs, *sems)


def _call(body, operands, *, grid, in_specs, out_specs, out_shape, name, compiler_params, scratch_shapes=(),
          exchange=None):
    operands = [o if getattr(spec, "memory_space", None) == pltpu.SMEM else _in_hbm(o)
                for o, spec in zip(operands, in_specs)]
    out_shape = [pltpu.HBM(s.shape, s.dtype) for s in out_shape]
    if exchange is None:
        res = pl.pallas_call(body, grid=grid, in_specs=in_specs, out_specs=out_specs, out_shape=out_shape, name=name,
                             scratch_shapes=list(scratch_shapes), compiler_params=compiler_params)(*operands)
        return list(res), []
    n_in, n_out, n_scr = len(in_specs), len(out_specs), len(scratch_shapes)
    k_in, k_out = len(exchange.operands), exchange.n_out

    def fused(*refs):
        ins, refs = refs[:n_in], refs[n_in:]
        ex_ins, refs = refs[:k_in], refs[k_in:]
        outs, refs = refs[:n_out], refs[n_out:]
        ex_outs, refs = refs[:k_out], refs[k_out:]
        scratch, sems = refs[:n_scr], refs[n_scr:]
        ids = [pl.program_id(a) for a in range(len(grid))]
        first = functools.reduce(jnp.logical_and, [i == 0 for i in ids])
        last = functools.reduce(jnp.logical_and, [i == g - 1 for i, g in zip(ids, grid)])

        @pl.when(first)
        def _():
            exchange.start(ex_ins, ex_outs, sems)

        body(*ins, *outs, *scratch)

        @pl.when(last)
        def _():
            exchange.finish(ex_ins, ex_outs, sems)

    res = pl.pallas_call(
        fused, grid=grid, name=name,
        in_specs=list(in_specs) + [HBM_SPEC] * k_in, out_specs=list(out_specs) + [HBM_SPEC] * k_out,
        out_shape=list(out_shape) + exchange.out_shape,
        input_output_aliases={n_in + i: n_out + o for i, o in exchange.aliases.items()},
        scratch_shapes=list(scratch_shapes) + exchange.sems, compiler_params=compiler_params,
    )(*operands, *[_in_hbm(o) for o in exchange.operands])
    return list(res[:n_out]), list(res[n_out:])


def _in_hbm(a):
    return pltpu.with_memory_space_constraint(a, pltpu.HBM)


def _row_tile(h):
    return max(t for t in range(16, 257, 16) if h % t == 0)


def _cast_shard(a, chip):
    rows, cols = a.shape
    h = rows // 2
    tr = _row_tile(h)

    def body(chip_ref, a_ref, o_ref):
        o_ref[0, 0] = a_ref[0].astype(BF16)

    return pl.pallas_call(
        body, name="cast_shard",
        grid_spec=pltpu.PrefetchScalarGridSpec(
            num_scalar_prefetch=1, grid=(2, h // tr),
            in_specs=[pl.BlockSpec((1, tr, cols), lambda s, r, chip_ref: (s, r, 0))],
            out_specs=pl.BlockSpec((1, 1, tr, cols), lambda s, r, chip_ref: (chip_ref[0], s, r, 0))),
        out_shape=pltpu.HBM((N_CHIP, 2, h, cols), BF16),
        compiler_params=_params(16, 2),
    )(chip, _in_hbm(a.reshape(2, h, cols)))


def _in_proj(x, gain1, w_in_t, exchange=None):
    tm = 256

    def body(x_ref, g_ref, w_ref, z_ref, hn_ref):
        xv = x_ref[...]
        hn = (xv * _rms_scale(xv) * g_ref[...]).astype(BF16)
        hn_ref[...] = hn
        z_ref[...] = _dot_nt(hn, w_ref[...])

    return _call(
        body, (x, gain1, w_in_t), grid=(S // tm,), name="in_proj",
        in_specs=[pl.BlockSpec((tm, D), lambda i: (i, 0)), pl.BlockSpec((1, D), lambda i: (0, 0)),
                  pl.BlockSpec((D_IN, D), lambda i: (0, 0))],
        out_specs=[pl.BlockSpec((tm, D_IN), lambda i: (i, 0)), pl.BlockSpec((tm, D), lambda i: (i, 0))],
        out_shape=[jax.ShapeDtypeStruct((S, D_IN), F32), jax.ShapeDtypeStruct((S, D), BF16)],
        compiler_params=_params(40), exchange=exchange)


def _gelu_parts(v):
    t = jnp.tanh(GELU_C * (v + 0.044715 * (v * v * v)))
    cdf = 0.5 * (1.0 + t)
    return cdf, t


def _band_mask(n):
    a = lax.broadcasted_iota(jnp.int32, (CHUNK, 2 * CHUNK), 0)
    j = lax.broadcasted_iota(jnp.int32, (CHUNK, 2 * CHUNK), 1)
    dist = CHUNK + a - j
    valid = (dist >= 0) & (dist < CHUNK)
    return valid & ((n > 0) | (j >= CHUNK))


def _fill_bias(bucket_ref, table_ref, bias_ref):
    bucket = bucket_ref[...]
    for h in range(N_HEAD):
        acc = jnp.zeros((CHUNK, 2 * CHUNK), F32)
        for b in range(N_BUCKET):
            acc = jnp.where(bucket == b, table_ref[b, h], acc)
        bias_ref[h] = acc


def _fill_tril(ws_ref, wt_ref, wtt_ref=None):
    r = lax.broadcasted_iota(jnp.int32, (CHUNK, CHUNK), 0)
    c = lax.broadcasted_iota(jnp.int32, (CHUNK, CHUNK), 1)
    for g in range(N_GROUP):
        w = jnp.where(c <= r, ws_ref[g], 0.0)
        wt_ref[g] = w.astype(BF16)
        if wtt_ref is not None:
            wtt_ref[g] = w.T.astype(BF16)


def _kv_layouts(kv_prev, kv_cur):
    both = jnp.concatenate([kv_prev, kv_cur], axis=0)
    k = both[:, :128]
    v = both[:, 128:]
    return (k.astype(BF16), pltpu.roll(k, 64, axis=1).astype(BF16),
            v.astype(BF16), pltpu.roll(v, 64, axis=1).astype(BF16))


def _head_place(h):
    pair, pos, kvh = h // 2, h % 2, h // 4
    return pair, pos, kvh == pos


def _softmax_sink(qm, k_use, bias_h, sink, valid):
    s = _dot_nt(qm, k_use) * QK_SCALE + bias_h
    s = jnp.where(valid, s, NEG_INF)
    m = jnp.maximum(jnp.max(s, axis=-1, keepdims=True), sink)
    e = jnp.exp(s - m)
    es = jnp.exp(sink - m)
    denom = jnp.sum(e, axis=-1, keepdims=True) + es
    return e / denom, es / denom


def _mixer_fwd(z, v_gain, w_spatial, b_spatial_t, sinks, rel_table, bucket, exchange=None):
    def body(z_ref, kvp_ref, gain_ref, ws_ref, bt_ref, sink_ref, table_ref, bucket_ref, out_ref, bias_ref, wt_ref):
        n = pl.program_id(0)

        @pl.when(n == 0)
        def _():
            _fill_bias(bucket_ref, table_ref, bias_ref)
            _fill_tril(ws_ref, wt_ref)

        zuv = z_ref[:, :1024]
        cdf, _ = _gelu_parts(zuv)
        guv = zuv * cdf
        for g in range(N_GROUP):
            vg = guv[:, 512 + 128 * g:512 + 128 * (g + 1)]
            vn = vg * _rms_scale(vg) * gain_ref[:, 128 * g:128 * (g + 1)]
            sv = _dot(wt_ref[g], vn.astype(BF16)) + bt_ref[:, g:g + 1]
            out_ref[:, 128 * g:128 * (g + 1)] = (guv[:, 128 * g:128 * (g + 1)] * sv).astype(BF16)

        k_same, k_swap, v_same, v_swap = _kv_layouts(kvp_ref[...], z_ref[:, 1536:1792])
        valid = _band_mask(n)
        lane_half = lax.broadcasted_iota(jnp.int32, (1, 128), 1) // 64
        for pair in range(N_HEAD // 2):
            qq = z_ref[:, 1024 + 128 * pair:1024 + 128 * (pair + 1)]
            acc = jnp.zeros((CHUNK, 128), F32)
            for pos in range(2):
                h = 2 * pair + pos
                _, _, same = _head_place(h)
                qm = jnp.where(lane_half == pos, qq, 0.0).astype(BF16)
                p, _ = _softmax_sink(qm, k_same if same else k_swap, bias_ref[h], sink_ref[h], valid)
                vm = jnp.where(lane_half == pos, v_same if same else v_swap, jnp.zeros((), BF16))
                acc = acc + _dot(p.astype(BF16), vm)
            out_ref[:, 512 + 128 * pair:512 + 128 * (pair + 1)] = acc.astype(BF16)

    return _call(
        body, (z, z, v_gain, w_spatial, b_spatial_t, sinks, rel_table, bucket), grid=(N_BLOCK,), name="mixer_fwd",
        in_specs=[pl.BlockSpec((CHUNK, D_IN), lambda n: (n, 0)),
                  pl.BlockSpec((CHUNK, 256), lambda n: (jnp.maximum(n - 1, 0), 6)),
                  pl.BlockSpec((1, 512), lambda n: (0, 0)),
                  pl.BlockSpec((N_GROUP, CHUNK, CHUNK), lambda n: (0, 0, 0)),
                  pl.BlockSpec((CHUNK, N_GROUP), lambda n: (0, 0)),
                  pl.BlockSpec(memory_space=pltpu.SMEM),
                  pl.BlockSpec(memory_space=pltpu.SMEM),
                  pl.BlockSpec((CHUNK, 2 * CHUNK), lambda n: (0, 0))],
        out_specs=[pl.BlockSpec((CHUNK, D), lambda n: (n, 0))],
        out_shape=[jax.ShapeDtypeStruct((S, D), BF16)],
        scratch_shapes=[pltpu.VMEM((N_HEAD, CHUNK, 2 * CHUNK), F32), pltpu.VMEM((N_GROUP, CHUNK, CHUNK), BF16)],
        compiler_params=_params(32), exchange=exchange)


def _out_proj(x, mix, w_out, gain2, exchange=None):
    tm = 256

    def body(x_ref, mix_ref, w_ref, g_ref, h1_ref, hn_ref, hnt_ref):
        h1 = x_ref[...] + _dot(mix_ref[...], w_ref[...])
        h1_ref[...] = h1
        hn = h1 * _rms_scale(h1) * g_ref[...]
        hn_ref[...] = hn.astype(BF16)
        hnt_ref[...] = hn.T.astype(BF16)

    return _call(
        body, (x, mix, w_out, gain2), grid=(S // tm,), name="out_proj",
        in_specs=[pl.BlockSpec((tm, D), lambda i: (i, 0)), pl.BlockSpec((tm, D), lambda i: (i, 0)),
                  pl.BlockSpec((D, D), lambda i: (0, 0)), pl.BlockSpec((1, D), lambda i: (0, 0))],
        out_specs=[pl.BlockSpec((tm, D), lambda i: (i, 0)), pl.BlockSpec((tm, D), lambda i: (i, 0)),
                   pl.BlockSpec((D, tm), lambda i: (0, i))],
        out_shape=[jax.ShapeDtypeStruct((S, D), F32), jax.ShapeDtypeStruct((S, D), BF16),
                   jax.ShapeDtypeStruct((D, S), BF16)],
        compiler_params=_params(32), exchange=exchange)


def _ffn_up(hn2, w_ff1, exchange=None):
    tm = 512
    nj = D_FF // 1024

    def body(hn_ref, w1_ref, r_ref, a_ref, at_ref):
        r = jnp.maximum(_dot(hn_ref[...], w1_ref[0]), 0.0)
        r_ref[...] = r.astype(BF16)
        a = r * r
        a_ref[...] = a.astype(BF16)
        at_ref[...] = a.T.astype(BF16)

    return _call(
        body, (hn2, w_ff1), grid=(nj, S // tm), name="ffn_up",
        in_specs=[pl.BlockSpec((tm, D), lambda j, i: (i, 0)), pl.BlockSpec((1, D, 1024), lambda j, i: (j, 0, 0))],
        out_specs=[pl.BlockSpec((tm, 1024), lambda j, i: (i, j)), pl.BlockSpec((tm, 1024), lambda j, i: (i, j)),
                   pl.BlockSpec((1024, tm), lambda j, i: (j, i))],
        out_shape=[jax.ShapeDtypeStruct((S, D_FF), BF16), jax.ShapeDtypeStruct((S, D_FF), BF16),
                   jax.ShapeDtypeStruct((D_FF, S), BF16)],
        compiler_params=_params(40, 2), exchange=exchange)


def _ffn_down(h1, a, w_ff2, exchange=None):
    tm = 1024
    nj = D_FF // 1024

    def body(h1_ref, a_ref, w2_ref, h2_ref, acc_ref):
        j = pl.program_id(1)
        part = _dot(a_ref[...], w2_ref[0])

        @pl.when(j == 0)
        def _():
            acc_ref[...] = part

        @pl.when(j > 0)
        def _():
            acc_ref[...] += part

        @pl.when(j == nj - 1)
        def _():
            h2_ref[...] = h1_ref[...] + acc_ref[...]

    return _call(
        body, (h1, a, w_ff2), grid=(S // tm, nj), name="ffn_down",
        in_specs=[pl.BlockSpec((tm, D), lambda i, j: (i, 0)), pl.BlockSpec((tm, 1024), lambda i, j: (i, j)),
                  pl.BlockSpec((1, 1024, D), lambda i, j: (j, 0, 0))],
        out_specs=[pl.BlockSpec((tm, D), lambda i, j: (i, 0))],
        out_shape=[jax.ShapeDtypeStruct((S, D), F32)],
        scratch_shapes=[pltpu.VMEM((tm, D), F32)],
        compiler_params=_params(48, 2), exchange=exchange)


def _tail(h2, p, target, w_gate, w_proj, final_gain):
    tm = 256
    steps = S // tm

    def body(h2_ref, p_ref, t_ref, wg_ref, wp_ref, gf_ref, dh2_ref, dwg_ref, dwp_ref, dgf_ref, loss_ref, dh2b_ref,
             dwp_acc):
        i = pl.program_id(0)
        h2 = h2_ref[...]
        h2b = h2.astype(BF16)
        pb = p_ref[...].astype(BF16)
        gate = jax.nn.sigmoid(_dot(h2b, wg_ref[...]))
        pp = jnp.concatenate([_dot(pb, wp_ref[j]) for j in range(N_CHIP)], axis=1)
        h3 = h2 + gate * pp
        r3 = _rms_scale(h3)
        xhat = h3 * r3
        gf = gf_ref[...]
        err = xhat * gf - t_ref[...]
        dy = err * (1.0 / D)
        dh3 = _rms_bwd(dy * gf, xhat, r3)
        dgp = (dh3 * pp * gate * (1.0 - gate)).astype(BF16)
        dpp = (dh3 * gate).astype(BF16)
        dh2 = dh3 + _dot_nt(dgp, wg_ref[...])
        dh2_ref[...] = dh2
        dh2b_ref[...] = dh2.astype(BF16)
        dwg = _dot_tn(h2b, dgp)
        dwp = _dot_tn(pb, dpp)
        dgf = jnp.sum(dy * xhat, axis=0, keepdims=True)
        sq = jnp.sum(jnp.sum(err * err, axis=1, keepdims=True), axis=0, keepdims=True)

        @pl.when(i == 0)
        def _():
            dwg_ref[...] = dwg
            dwp_acc[...] = dwp
            dgf_ref[...] = dgf
            loss_ref[...] = jnp.broadcast_to(sq, (8, 128))

        @pl.when(i > 0)
        def _():
            dwg_ref[...] += dwg
            dwp_acc[...] += dwp
            dgf_ref[...] += dgf
            loss_ref[...] += jnp.broadcast_to(sq, (8, 128))

        @pl.when(i == steps - 1)
        def _():
            for j in range(N_CHIP):
                dwp_ref[j] = dwp_acc[:, 256 * j:256 * (j + 1)]

    return _call(
        body, (h2, p, target, w_gate, w_proj, final_gain), grid=(steps,), name="tail",
        in_specs=[pl.BlockSpec((tm, D), lambda i: (i, 0)), pl.BlockSpec((tm, PLE), lambda i: (i, 0)),
                  pl.BlockSpec((tm, D), lambda i: (i, 0)), pl.BlockSpec((D, D), lambda i: (0, 0)),
                  pl.BlockSpec((N_CHIP, PLE, 256), lambda i: (0, 0, 0)), pl.BlockSpec((1, D), lambda i: (0, 0))],
        out_specs=[pl.BlockSpec((tm, D), lambda i: (i, 0)), pl.BlockSpec((D, D), lambda i: (0, 0)),
                   pl.BlockSpec((N_CHIP, PLE, 256), lambda i: (0, 0, 0)), pl.BlockSpec((1, D), lambda i: (0, 0)),
                   pl.BlockSpec((8, 128), lambda i: (0, 0)), pl.BlockSpec((tm, D), lambda i: (i, 0))],
        out_shape=[jax.ShapeDtypeStruct((S, D), F32), jax.ShapeDtypeStruct((D, D), F32),
                   jax.ShapeDtypeStruct((N_CHIP, PLE, 256), F32), jax.ShapeDtypeStruct((1, D), F32),
                   jax.ShapeDtypeStruct((8, 128), F32), jax.ShapeDtypeStruct((S, D), BF16)],
        scratch_shapes=[pltpu.VMEM((PLE, D), F32)],
        compiler_params=_params(48))[0]


def _ffn_bwd_down(dh2b, r, a_t, w_ff2, exchange=None):
    tm = 1024
    nj = D_FF // 1024

    def body(dh2_ref, r_ref, at_ref, w2_ref, df_ref, dw2_ref):
        i = pl.program_id(1)
        dh2b = dh2_ref[...]
        da = _dot_nt(dh2b, w2_ref[0])
        df_ref[...] = (da * (2.0 * r_ref[...].astype(F32))).astype(BF16)
        dw2 = _dot(at_ref[...], dh2b)

        @pl.when(i == 0)
        def _():
            dw2_ref[0] = dw2

        @pl.when(i > 0)
        def _():
            dw2_ref[0] += dw2

    return _call(
        body, (dh2b, r, a_t, w_ff2), grid=(nj, S // tm), name="ffn_bwd_down",
        in_specs=[pl.BlockSpec((tm, D), lambda j, i: (i, 0)), pl.BlockSpec((tm, 1024), lambda j, i: (i, j)),
                  pl.BlockSpec((1024, tm), lambda j, i: (j, i)), pl.BlockSpec((1, 1024, D), lambda j, i: (j, 0, 0))],
        out_specs=[pl.BlockSpec((tm, 1024), lambda j, i: (i, j)), pl.BlockSpec((1, 1024, D), lambda j, i: (j, 0, 0))],
        out_shape=[jax.ShapeDtypeStruct((S, D_FF), BF16), jax.ShapeDtypeStruct((nj, 1024, D), F32)],
        compiler_params=_params(48, 2), exchange=exchange)


def _ffn_bwd_up(df, hn2_t, exchange=None):
    tm = 1024
    nj = D_FF // 1024

    def body(df_ref, hnt_ref, dw1_ref):
        i = pl.program_id(1)
        dw1 = _dot(hnt_ref[...], df_ref[...])

        @pl.when(i == 0)
        def _():
            dw1_ref[0] = dw1

        @pl.when(i > 0)
        def _():
            dw1_ref[0] += dw1

    return _call(
        body, (df, hn2_t), grid=(nj, S // tm), name="ffn_bwd_up",
        in_specs=[pl.BlockSpec((tm, 1024), lambda j, i: (i, j)), pl.BlockSpec((D, tm), lambda j, i: (0, i))],
        out_specs=[pl.BlockSpec((1, D, 1024), lambda j, i: (j, 0, 0))],
        out_shape=[jax.ShapeDtypeStruct((nj, D, 1024), F32)],
        compiler_params=_params(40, 2), exchange=exchange)


def _ffn_bwd_input(df, w_ff1, dh2, h1, gain2, mix, w_out, exchange=None):
    tm = 512
    nj = D_FF // 1024
    steps = S // tm

    def body(df_ref, w1_ref, dh2_ref, h1_ref, g_ref, mix_ref, wo_ref, dh1_ref, dmix_ref, dwo_ref, dg_ref, acc_ref):
        i = pl.program_id(0)
        j = pl.program_id(1)
        part = _dot_nt(df_ref[...], w1_ref[0])

        @pl.when(j == 0)
        def _():
            acc_ref[...] = part

        @pl.when(j > 0)
        def _():
            acc_ref[...] += part

        @pl.when(j == nj - 1)
        def _():
            dhn = acc_ref[...]
            h1 = h1_ref[...]
            r2 = _rms_scale(h1)
            xhat = h1 * r2
            dh1 = dh2_ref[...] + _rms_bwd(dhn * g_ref[...], xhat, r2)
            dh1_ref[...] = dh1
            dh1b = dh1.astype(BF16)
            dmix_ref[...] = _dot_nt(dh1b, wo_ref[...])
            dwo = _dot_tn(mix_ref[...], dh1b)
            dg = jnp.sum(dhn * xhat, axis=0, keepdims=True)

            @pl.when(i == 0)
            def _():
                dwo_ref[...] = dwo
                dg_ref[...] = dg

            @pl.when(i > 0)
            def _():
                dwo_ref[...] += dwo
                dg_ref[...] += dg

    return _call(
        body, (df, w_ff1, dh2, h1, gain2, mix, w_out), grid=(steps, nj), name="ffn_bwd_input",
        in_specs=[pl.BlockSpec((tm, 1024), lambda i, j: (i, j)), pl.BlockSpec((1, D, 1024), lambda i, j: (j, 0, 0)),
                  pl.BlockSpec((tm, D), lambda i, j: (i, 0)), pl.BlockSpec((tm, D), lambda i, j: (i, 0)),
                  pl.BlockSpec((1, D), lambda i, j: (0, 0)), pl.BlockSpec((tm, D), lambda i, j: (i, 0)),
                  pl.BlockSpec((D, D), lambda i, j: (0, 0))],
        out_specs=[pl.BlockSpec((tm, D), lambda i, j: (i, 0)), pl.BlockSpec((tm, D), lambda i, j: (i, 0)),
                   pl.BlockSpec((D, D), lambda i, j: (0, 0)), pl.BlockSpec((1, D), lambda i, j: (0, 0))],
        out_shape=[jax.ShapeDtypeStruct((S, D), F32), jax.ShapeDtypeStruct((S, D), F32),
                   jax.ShapeDtypeStruct((D, D), F32), jax.ShapeDtypeStruct((1, D), F32)],
        scratch_shapes=[pltpu.VMEM((tm, D), F32)],
        compiler_params=_params(56, 2), exchange=exchange)


def _mixer_bwd(z, dmix, v_gain, w_spatial, b_spatial_t, sinks, rel_table, bucket, exchange=None):
    def body(z_ref, kvp_ref, dm_ref, gain_ref, ws_ref, bt_ref, sink_ref, table_ref, bucket_ref,
             dz_ref, dws_ref, db_ref, dgain_ref, dsink_ref, drel_ref,
             bias_ref, wt_ref, wtt_ref, dbias_ref, dsv_ref, carry_ref):
        n = pl.program_id(0)

        @pl.when(n == 0)
        def _():
            _fill_bias(bucket_ref, table_ref, bias_ref)
            _fill_tril(ws_ref, wt_ref, wtt_ref)
            dbias_ref[...] = jnp.zeros_like(dbias_ref)
            dsv_ref[...] = jnp.zeros_like(dsv_ref)
            dws_ref[...] = jnp.zeros_like(dws_ref)
            dgain_ref[...] = jnp.zeros_like(dgain_ref)
            dsink_ref[...] = jnp.zeros_like(dsink_ref)

        rows = pl.ds(pl.multiple_of(n * CHUNK, CHUNK), CHUNK)

        zuv = z_ref[:, :1024]
        cdf, t = _gelu_parts(zuv)
        guv = zuv * cdf
        dgelu = cdf + zuv * (0.5 * (1.0 - t * t)) * (GELU_C * (1.0 + 3.0 * 0.044715 * (zuv * zuv)))
        for g in range(N_GROUP):
            lo, hi = 128 * g, 128 * (g + 1)
            u = guv[:, lo:hi]
            vg = guv[:, 512 + lo:512 + hi]
            rr = _rms_scale(vg)
            vhat = vg * rr
            gain = gain_ref[:, lo:hi]
            vnb = (vhat * gain).astype(BF16)
            sv = _dot(wt_ref[g], vnb) + bt_ref[:, g:g + 1]
            da = dm_ref[:, lo:hi]
            dsv = da * u
            dsvb = dsv.astype(BF16)
            dsv_ref[g] += dsv
            dws_ref[g] += _dot_nt(dsvb, vnb)
            dvn = _dot(wtt_ref[g], dsvb)
            dgain_ref[:, lo:hi] += jnp.sum(dvn * vhat, axis=0, keepdims=True)
            dvg = _rms_bwd(dvn * gain, vhat, rr)
            dz_ref[rows, lo:hi] = (da * sv * dgelu[:, lo:hi]).astype(BF16)
            dz_ref[rows, 512 + lo:512 + hi] = (dvg * dgelu[:, 512 + lo:512 + hi]).astype(BF16)

        k_same, k_swap, v_same, v_swap = _kv_layouts(kvp_ref[...], z_ref[:, 1536:1792])
        valid = _band_mask(n)
        lane_half = lax.broadcasted_iota(jnp.int32, (1, 128), 1) // 64
        zero = jnp.zeros((2 * CHUNK, 128), F32)
        dk_same, dk_swap, dv_same, dv_swap = zero, zero, zero, zero
        for pair in range(N_HEAD // 2):
            cols = slice(1024 + 128 * pair, 1024 + 128 * (pair + 1))
            qq = z_ref[:, cols]
            do_pair = dm_ref[:, 512 + 128 * pair:512 + 128 * (pair + 1)]
            dq = jnp.zeros((CHUNK, 128), F32)
            for pos in range(2):
                h = 2 * pair + pos
                _, _, same = _head_place(h)
                on_half = lane_half == pos
                qm = jnp.where(on_half, qq, 0.0).astype(BF16)
                k_use = k_same if same else k_swap
                v_use = v_same if same else v_swap
                p, p_sink = _softmax_sink(qm, k_use, bias_ref[h], sink_ref[h], valid)
                dom = jnp.where(on_half, do_pair, 0.0).astype(BF16)
                dp = _dot_nt(dom, v_use)
                dsum = jnp.sum(p * dp, axis=-1, keepdims=True)
                ds = p * (dp - dsum)
                dbias_ref[h] += ds
                dsink_ref[h:h + 1, :] += jnp.broadcast_to(jnp.sum(-p_sink * dsum, axis=0, keepdims=True), (1, 128))
                dsb = ds.astype(BF16)
                dq = dq + jnp.where(on_half, _dot(dsb, k_use), 0.0)
                dk_h = _dot_tn(dsb, qm)
                dv_h = _dot_tn(p.astype(BF16), dom)
                if same:
                    dk_same, dv_same = dk_same + dk_h, dv_same + dv_h
                else:
                    dk_swap, dv_swap = dk_swap + dk_h, dv_swap + dv_h
            dz_ref[rows, cols] = (dq * QK_SCALE).astype(BF16)
        dk = (dk_same + pltpu.roll(dk_swap, 64, axis=1)) * QK_SCALE
        dv = dv_same + pltpu.roll(dv_swap, 64, axis=1)
        dkv = jnp.concatenate([dk, dv], axis=1)

        @pl.when(n > 0)
        def _():
            prev_rows = pl.ds(pl.multiple_of((n - 1) * CHUNK, CHUNK), CHUNK)
            dz_ref[prev_rows, 1536:1792] = (carry_ref[...] + dkv[:CHUNK]).astype(BF16)

        carry_ref[...] = dkv[CHUNK:]

        @pl.when(n == N_BLOCK - 1)
        def _():
            dz_ref[rows, 1536:1792] = dkv[CHUNK:].astype(BF16)
            r = lax.broadcasted_iota(jnp.int32, (CHUNK, CHUNK), 0)
            c = lax.broadcasted_iota(jnp.int32, (CHUNK, CHUNK), 1)
            for g in range(N_GROUP):
                dws_ref[g] = jnp.where(c <= r, dws_ref[g], 0.0)
                db_ref[g] = jnp.sum(dsv_ref[g], axis=1, keepdims=True)
            bucket = bucket_ref[...]
            for h in range(N_HEAD):
                dbh = dbias_ref[h]
                per_bucket = [jnp.sum(jnp.where(bucket == b, dbh, 0.0), axis=0, keepdims=True) for b in range(N_BUCKET)]
                drel_ref[h] = jnp.sum(jnp.concatenate(per_bucket, axis=0), axis=1, keepdims=True)

    return _call(
        body, (z, z, dmix, v_gain, w_spatial, b_spatial_t, sinks, rel_table, bucket), grid=(N_BLOCK,), name="mixer_bwd",
        in_specs=[pl.BlockSpec((CHUNK, D_IN), lambda n: (n, 0)),
                  pl.BlockSpec((CHUNK, 256), lambda n: (jnp.maximum(n - 1, 0), 6)),
                  pl.BlockSpec((CHUNK, D), lambda n: (n, 0)),
                  pl.BlockSpec((1, 512), lambda n: (0, 0)),
                  pl.BlockSpec((N_GROUP, CHUNK, CHUNK), lambda n: (0, 0, 0)),
                  pl.BlockSpec((CHUNK, N_GROUP), lambda n: (0, 0)),
                  pl.BlockSpec(memory_space=pltpu.SMEM),
                  pl.BlockSpec(memory_space=pltpu.SMEM),
                  pl.BlockSpec((CHUNK, 2 * CHUNK), lambda n: (0, 0))],
        out_specs=[pl.BlockSpec((S, D_IN), lambda n: (0, 0)),
                   pl.BlockSpec((N_GROUP, CHUNK, CHUNK), lambda n: (0, 0, 0)),
                   pl.BlockSpec((N_GROUP, CHUNK, 1), lambda n: (0, 0, 0)),
                   pl.BlockSpec((1, 512), lambda n: (0, 0)),
                   pl.BlockSpec((N_HEAD, 128), lambda n: (0, 0)),
                   pl.BlockSpec((N_HEAD, N_BUCKET, 1), lambda n: (0, 0, 0))],
        out_shape=[jax.ShapeDtypeStruct((S, D_IN), BF16), jax.ShapeDtypeStruct((N_GROUP, CHUNK, CHUNK), F32),
                   jax.ShapeDtypeStruct((N_GROUP, CHUNK, 1), F32), jax.ShapeDtypeStruct((1, 512), F32),
                   jax.ShapeDtypeStruct((N_HEAD, 128), F32), jax.ShapeDtypeStruct((N_HEAD, N_BUCKET, 1), F32)],
        scratch_shapes=[pltpu.VMEM((N_HEAD, CHUNK, 2 * CHUNK), F32), pltpu.VMEM((N_GROUP, CHUNK, CHUNK), BF16),
                        pltpu.VMEM((N_GROUP, CHUNK, CHUNK), BF16), pltpu.VMEM((N_HEAD, CHUNK, 2 * CHUNK), F32),
                        pltpu.VMEM((N_GROUP, CHUNK, CHUNK), F32), pltpu.VMEM((CHUNK, 256), F32)],
        compiler_params=_params(48), exchange=exchange)


def _in_bwd_weight(dz, hn1, exchange=None):
    tm = 512

    def body(dz_ref, hn_ref, dw_ref):
        i = pl.program_id(0)
        dw = _dot_tn(dz_ref[...], hn_ref[...])

        @pl.when(i == 0)
        def _():
            dw_ref[...] = dw

        @pl.when(i > 0)
        def _():
            dw_ref[...] += dw

    return _call(
        body, (dz, hn1), grid=(S // tm,), name="in_bwd_weight",
        in_specs=[pl.BlockSpec((tm, D_IN), lambda i: (i, 0)), pl.BlockSpec((tm, D), lambda i: (i, 0))],
        out_specs=[pl.BlockSpec((D_IN, D), lambda i: (0, 0))],
        out_shape=[jax.ShapeDtypeStruct((D_IN, D), F32)],
        compiler_params=_params(40), exchange=exchange)


def _in_bwd_input(dz, w_in_t, x, dh1, gain1, exchange=None):
    tm = 512

    def body(dz_ref, w_ref, x_ref, dh1_ref, g_ref, dx_ref, dg_ref):
        i = pl.program_id(0)
        dhn = _dot(dz_ref[...], w_ref[...])
        xv = x_ref[...]
        r1 = _rms_scale(xv)
        xhat = xv * r1
        dx_ref[...] = dh1_ref[...] + _rms_bwd(dhn * g_ref[...], xhat, r1)
        dg = jnp.sum(dhn * xhat, axis=0, keepdims=True)

        @pl.when(i == 0)
        def _():
            dg_ref[...] = dg

        @pl.when(i > 0)
        def _():
            dg_ref[...] += dg

    return _call(
        body, (dz, w_in_t, x, dh1, gain1), grid=(S // tm,), name="in_bwd_input",
        in_specs=[pl.BlockSpec((tm, D_IN), lambda i: (i, 0)), pl.BlockSpec((D_IN, D), lambda i: (0, 0)),
                  pl.BlockSpec((tm, D), lambda i: (i, 0)), pl.BlockSpec((tm, D), lambda i: (i, 0)),
                  pl.BlockSpec((1, D), lambda i: (0, 0))],
        out_specs=[pl.BlockSpec((tm, D), lambda i: (i, 0)), pl.BlockSpec((1, D), lambda i: (0, 0))],
        out_shape=[jax.ShapeDtypeStruct((S, D), F32), jax.ShapeDtypeStruct((1, D), F32)],
        compiler_params=_params(48), exchange=exchange)


def _rel_bucket():
    a = jnp.arange(CHUNK)[:, None]
    j = jnp.arange(2 * CHUNK)[None, :]
    n = jnp.maximum(CHUNK + a - j, 0)
    max_exact = N_BUCKET // 2
    nf = jnp.maximum(n, 1).astype(jnp.float32)
    large = max_exact + (jnp.log(nf / max_exact) / math.log(CHUNK / max_exact) * (N_BUCKET - max_exact)).astype(jnp.int32)
    large = jnp.minimum(large, N_BUCKET - 1)
    return jnp.where(n < max_exact, n, large).astype(jnp.int32)


def _step(x, p, target, small, bufs, place):
    bucket = _rel_bucket()
    sinks = small["attn_sinks"].reshape(N_HEAD)
    b_t = jnp.transpose(small["b_spatial"].reshape(N_GROUP, CHUNK))
    ws = small["w_spatial"].reshape(N_GROUP, CHUNK, CHUNK)
    gain1, gain2 = small["norm1_gain"], small["norm2_gain"]
    v_gain = small["gmlp_v_gain"]
    final_gain = small["final_gain"].reshape(1, D)
    table = small["rel_bias_table"]
    bufs = dict(bufs)

    def gather(*names):
        return _Gather([bufs[n] for n in names])

    def took(names, got):
        bufs.update(zip(names, got))

    took(["w_in"], _gather_weights([bufs["w_in"]]))
    w_in_t = _whole(bufs["w_in"]).reshape(D_IN, D)
    (z, hn1), got = _in_proj(x, gain1, w_in_t, gather("w_out"))
    took(["w_out"], got)
    (mix,), got = _mixer_fwd(z, v_gain, ws, b_t, sinks, table, bucket, gather("w_ff1"))
    took(["w_ff1"], got)
    w_out = _whole(bufs["w_out"]).reshape(D, D)
    (h1, hn2, hn2_t), _ = _out_proj(x, mix, w_out, gain2)
    w_ff1 = _whole(bufs["w_ff1"])
    (r, a, a_t), got = _ffn_up(hn2, w_ff1, gather("w_ff2"))
    took(["w_ff2"], got)
    w_ff2 = _whole(bufs["w_ff2"])
    (h2,), got = _ffn_down(h1, a, w_ff2, gather("w_ple_gate", "w_ple_proj"))
    took(["w_ple_gate", "w_ple_proj"], got)
    dh2, d_gate, d_proj, d_final, sq, dh2b = _tail(h2, p, target, _whole(bufs["w_ple_gate"]).reshape(D, D),
                                                   _whole(bufs["w_ple_proj"]), final_gain)

    def pair_sums(halves, from_sibling):
        sums, landing = zip(*[_pair_sum(g, o, place) for g, o in zip(halves, from_sibling)])
        return list(sums), list(landing)

    landed = {}
    halves = [_halves(d_gate.reshape(N_CHIP, 256, D)), _halves(d_proj)]
    (df, d_ff2), got = _ffn_bwd_down(dh2b, r, a_t, w_ff2, _SiblingExchange(halves))
    ex, halves = _ChipExchange(*pair_sums(halves, got)), [_halves(d_ff2)]
    (d_ff1,), got = _ffn_bwd_up(df, hn2_t, _Both(ex, _SiblingExchange(halves)))
    landed.update(zip(["w_ple_gate", "w_ple_proj"], got[:2]))
    ex, halves = _ChipExchange(*pair_sums(halves, got[2:])), [_halves(d_ff1)]
    (dh1, dmix, d_out, d_gain2), got = _ffn_bwd_input(df, w_ff1, dh2, h1, gain2, mix, w_out,
                                                      _Both(ex, _SiblingExchange(halves)))
    landed["w_ff2"] = got[0]
    ex, halves = _ChipExchange(*pair_sums(halves, got[1:])), [_halves(d_out.reshape(N_CHIP, 256, D))]
    (dz, d_ws, d_b, d_vgain, d_sink, d_rel), got = _mixer_bwd(z, dmix, v_gain, ws, b_t, sinks, table, bucket,
                                                             _Both(ex, _SiblingExchange(halves)))
    landed["w_ff1"] = got[0]
    small_grads = {
        "gmlp_v_gain": d_vgain, "w_spatial": d_ws.reshape(1, N_GROUP, CHUNK, CHUNK),
        "b_spatial": d_b.reshape(1, N_GROUP, CHUNK), "attn_sinks": d_sink[:, 0].reshape(1, N_HEAD),
        "rel_bias_table": jnp.transpose(d_rel.reshape(N_HEAD, N_BUCKET)), "norm2_gain": d_gain2,
        "final_gain": d_final.reshape(D),
    }
    (d_in_t,), got = _in_bwd_weight(dz, hn1, _ChipExchange(*pair_sums(halves, got[1:])))
    landed.update(zip(["w_out"], got))
    halves = [_halves(d_in_t.reshape(N_CHIP, 448, D))]
    ex = _ChipExchange(*pair_sums(halves, _sibling_exchange(halves)))
    (dx, small_grads["norm1_gain"]), got = _in_bwd_input(dz, w_in_t, x, dh1, gain1, ex)
    landed.update(zip(["w_in"], got))
    return dx, landed, small_grads, sq


HBM_SPEC = pl.BlockSpec(memory_space=pltpu.HBM)
VMEM_SPEC = pl.BlockSpec(memory_space=pltpu.VMEM)


def _mesh_place():
    x, y, c = lax.axis_index("x"), lax.axis_index("y"), lax.axis_index("c")
    others = [(1 - x, y), (x, 1 - y), (1 - x, 1 - y)]
    return x, y, c, others


def _remote(src, dst, send_sem, recv_sem, device):
    return pltpu.make_async_remote_copy(src_ref=src, dst_ref=dst, send_sem=send_sem, recv_sem=recv_sem,
                                        device_id=device, device_id_type=MESH)


def _hbm_like(a, shape=None, dtype=None):
    return pltpu.HBM(a.shape if shape is None else shape, a.dtype if dtype is None else dtype)


def _gather_start(bufs, send_sems, recv_sems):
    x, y, c, others = _mesh_place()
    me = 2 * x + y
    for w, buf in enumerate(bufs):
        for k in range(3):
            mine = buf.at[me, c]
            _remote(mine, mine, send_sems.at[w, k], recv_sems.at[w, k], (*others[k], c)).start()


def _gather_finish(bufs, send_sems, recv_sems):
    x, y, c, others = _mesh_place()
    me = 2 * x + y
    sibling = (x, y, 1 - c)
    idx = [2 * ox + oy for ox, oy in others]
    chips = range(3)
    for w, buf in enumerate(bufs):
        for k in chips:
            landed = buf.at[idx[k], c]
            _remote(landed, landed, send_sems.at[w, k], recv_sems.at[w, k], sibling).wait_recv()
            _remote(landed, landed, send_sems.at[w, 3 + k], recv_sems.at[w, 3 + k], sibling).start()
    for w, buf in enumerate(bufs):
        for k in chips:
            landed = buf.at[idx[k], 1 - c]
            _remote(landed, landed, send_sems.at[w, 3 + k], recv_sems.at[w, 3 + k], sibling).wait_recv()
    for w, buf in enumerate(bufs):
        for k in chips:
            mine, passed = buf.at[me, c], buf.at[idx[k], c]
            _remote(mine, mine, send_sems.at[w, k], recv_sems.at[w, k], sibling).wait_send()
            _remote(passed, passed, send_sems.at[w, 3 + k], recv_sems.at[w, 3 + k], sibling).wait_send()


def _gather_sems(n):
    return [pltpu.SemaphoreType.DMA((n, 6)), pltpu.SemaphoreType.DMA((n, 6))]


def _gather_weights(bufs):
    n = len(bufs)

    def body(*refs):
        outs = refs[n:2 * n]
        send_sems, recv_sems = refs[2 * n:]
        _gather_start(outs, send_sems, recv_sems)
        _gather_finish(outs, send_sems, recv_sems)

    return pl.pallas_call(
        body, name="gather_weights",
        in_specs=[HBM_SPEC] * n, out_specs=[HBM_SPEC] * n,
        out_shape=[_hbm_like(b) for b in bufs],
        input_output_aliases={w: w for w in range(n)},
        scratch_shapes=_gather_sems(n),
    )(*bufs)


def _sibling_copies(grads, landing, send_sems, recv_sems):
    x, y, c, _ = _mesh_place()
    return [_remote(grads[w].at[j, 1 - c], landing[w].at[j], send_sems.at[w, j], recv_sems.at[w, j], (x, y, 1 - c))
            for w in range(len(grads)) for j in range(N_CHIP)]


def _sibling_exchange_start(grads, landing, send_sems, recv_sems):
    for cp in _sibling_copies(grads, landing, send_sems, recv_sems):
        cp.start()


def _sibling_exchange_finish(grads, landing, send_sems, recv_sems):
    copies = _sibling_copies(grads, landing, send_sems, recv_sems)
    for cp in copies:
        cp.wait_recv()
    for cp in copies:
        cp.wait_send()


def _sibling_exchange_sems(n):
    return [pltpu.SemaphoreType.DMA((n, N_CHIP)), pltpu.SemaphoreType.DMA((n, N_CHIP))]


def _sibling_exchange(grads):
    n = len(grads)

    def body(*refs):
        ins, outs = refs[:n], refs[n:2 * n]
        _sibling_exchange_start(ins, outs, *refs[2 * n:])
        _sibling_exchange_finish(ins, outs, *refs[2 * n:])

    return pl.pallas_call(
        body, name="sibling_exchange",
        in_specs=[HBM_SPEC] * n, out_specs=[HBM_SPEC] * n,
        out_shape=[_hbm_like(g, (N_CHIP,) + g.shape[2:]) for g in grads],
        scratch_shapes=_sibling_exchange_sems(n),
    )(*[_in_hbm(g) for g in grads])


def _chip_exchange_start(sums, landing, send_sems, recv_sems):
    x, y, c, others = _mesh_place()
    me = 2 * x + y
    for w in range(len(sums)):
        for k, (ox, oy) in enumerate(others):
            _remote(sums[w].at[2 * ox + oy], landing[w].at[me], send_sems.at[w, k], recv_sems.at[w, k],
                    (ox, oy, c)).start()


def _chip_exchange_finish(sums, landing, send_sems, recv_sems):
    x, y, c, others = _mesh_place()
    for w in range(len(sums)):
        for k, (ox, oy) in enumerate(others):
            piece = landing[w].at[2 * ox + oy]
            _remote(piece, piece, send_sems.at[w, k], recv_sems.at[w, k], (x, y, c)).wait_recv()
    for w in range(len(sums)):
        for k, (ox, oy) in enumerate(others):
            piece = sums[w].at[2 * ox + oy]
            _remote(piece, piece, send_sems.at[w, k], recv_sems.at[w, k], (x, y, c)).wait_send()


def _chip_exchange_sems(n):
    return [pltpu.SemaphoreType.DMA((n, 3)), pltpu.SemaphoreType.DMA((n, 3))]


def _chip_exchange(sums, landing):
    n = len(sums)

    def body(*refs):
        ins, outs = refs[:n], refs[2 * n:3 * n]
        send_sems, recv_sems = refs[3 * n:]
        _chip_exchange_start(ins, outs, send_sems, recv_sems)
        _chip_exchange_finish(ins, outs, send_sems, recv_sems)

    return pl.pallas_call(
        body, name="chip_exchange",
        in_specs=[HBM_SPEC] * (2 * n), out_specs=[HBM_SPEC] * n,
        out_shape=[_hbm_like(b) for b in landing],
        input_output_aliases={n + w: w for w in range(n)},
        scratch_shapes=_chip_exchange_sems(n),
    )(*sums, *landing)


def _sibling_allgather(bufs, also):
    n = len(bufs)
    k_in, k_out = len(also.operands), also.n_out

    def body(*refs):
        ex_ins, refs = refs[n:n + k_in], refs[n + k_in:]
        outs, refs = refs[:n], refs[n:]
        ex_outs, refs = refs[:k_out], refs[k_out:]
        send_sems, recv_sems, ex_sems = refs[0], refs[1], refs[2:]
        x, y, c, _ = _mesh_place()
        sibling = (x, y, 1 - c)
        also.start(ex_ins, ex_outs, ex_sems)
        sends = [_remote(outs[w].at[c], outs[w].at[c], send_sems.at[w], recv_sems.at[w], sibling) for w in range(n)]
        for cp in sends:
            cp.start()
        for w in range(n):
            landed = outs[w].at[1 - c]
            _remote(landed, landed, send_sems.at[w], recv_sems.at[w], sibling).wait_recv()
        for cp in sends:
            cp.wait_send()
        also.finish(ex_ins, ex_outs, ex_sems)

    res = pl.pallas_call(
        body, name="sibling_allgather",
        in_specs=[HBM_SPEC] * (n + k_in), out_specs=[HBM_SPEC] * (n + k_out),
        out_shape=[_hbm_like(b) for b in bufs] + also.out_shape,
        input_output_aliases={**{w: w for w in range(n)}, **{n + i: n + o for i, o in also.aliases.items()}},
        scratch_shapes=[pltpu.SemaphoreType.DMA((n,)), pltpu.SemaphoreType.DMA((n,))] + also.sems,
    )(*bufs, *[_in_hbm(o) for o in also.operands])
    return list(res[:n]), list(res[n:])


def _pair_sum(grad, other, place):
    _, _, h, cols = grad.shape
    tr = _row_tile(h)

    def body(place_ref, g_ref, o_ref, sums_ref, own_ref):
        s = (g_ref[0, 0] + o_ref[0]).astype(BF16)
        sums_ref[0] = s

        @pl.when(pl.program_id(1) == place_ref[0])
        def _():
            own_ref[0] = s

    return pl.pallas_call(
        body, name="pair_sum",
        grid_spec=pltpu.PrefetchScalarGridSpec(
            num_scalar_prefetch=1, grid=(h // tr, N_CHIP),
            in_specs=[pl.BlockSpec((1, 1, tr, cols), lambda r, j, place_ref: (j, place_ref[1], r, 0)),
                      pl.BlockSpec((1, tr, cols), lambda r, j, place_ref: (j, r, 0))],
            out_specs=[pl.BlockSpec((1, tr, cols), lambda r, j, place_ref: (j, r, 0)),
                       pl.BlockSpec((1, tr, cols), lambda r, j, place_ref: (place_ref[0], r, 0))]),
        out_shape=[pltpu.HBM((N_CHIP, h, cols), BF16)] * 2,
        compiler_params=_params(16, 2),
    )(place, _in_hbm(grad), _in_hbm(other))


def _chip_sum(parts, place):
    _, h, cols = parts.shape
    tr = _row_tile(h)

    def body(place_ref, p_ref, out_ref):
        out_ref[0] = ((p_ref[0].astype(F32) + p_ref[1].astype(F32)) + p_ref[2].astype(F32)) + p_ref[3].astype(F32)

    return pl.pallas_call(
        body, name="chip_sum",
        grid_spec=pltpu.PrefetchScalarGridSpec(
            num_scalar_prefetch=1, grid=(h // tr,),
            in_specs=[pl.BlockSpec((N_CHIP, tr, cols), lambda r, place_ref: (0, r, 0))],
            out_specs=pl.BlockSpec((1, tr, cols), lambda r, place_ref: (place_ref[1], r, 0))),
        out_shape=pltpu.HBM((2, h, cols), F32),
        compiler_params=_params(16),
    )(place, _in_hbm(parts))


def _adamw_math(w, g, m, v):
    m = ADAM_B1 * m + (1.0 - ADAM_B1) * g
    v = ADAM_B2 * v + (1.0 - ADAM_B2) * (g * g)
    m_hat = m / (1.0 - ADAM_B1 ** ADAM_STEP)
    v_hat = v / (1.0 - ADAM_B2 ** ADAM_STEP)
    delta = -ADAM_LR * (m_hat / (jnp.sqrt(v_hat) + ADAM_EPS) + ADAM_WD * w)
    return delta, m, v


def _adamw(w, g, m, v, exchange=None):
    rows, cols = w.shape
    tr = _row_tile(rows)

    def body(w_ref, g_ref, m_ref, v_ref, d_ref, nm_ref, nv_ref, g_out_ref):
        g = g_ref[...]
        d_ref[...], nm_ref[...], nv_ref[...] = _adamw_math(w_ref[...], g, m_ref[...], v_ref[...])
        g_out_ref[...] = g

    spec = pl.BlockSpec((tr, cols), lambda r: (r, 0))
    return _call(
        body, (w, g, m, v), grid=(rows // tr,), name="adamw",
        in_specs=[spec] * 4, out_specs=[spec] * 4,
        out_shape=[jax.ShapeDtypeStruct((rows, cols), F32)] * 4,
        compiler_params=_params(32), exchange=exchange)


SMALL_NAMES = ("norm1_gain", "gmlp_v_gain", "w_spatial", "b_spatial", "attn_sinks", "rel_bias_table", "norm2_gain",
               "final_gain")
PACK_TILE = 8 * 128


def _pack_small(arrays):
    parts = []
    for a in arrays:
        flat = a.reshape(-1)
        rows = -(-flat.shape[0] // PACK_TILE) * 8
        parts.append(jnp.pad(flat, (0, rows * 128 - flat.shape[0])).reshape(rows, 128))
    return jnp.concatenate(parts, axis=0)


def _unpack_small(packed, like):
    out, row = [], 0
    for a in like:
        size = math.prod(a.shape)
        rows = -(-size // PACK_TILE) * 8
        out.append(packed[row:row + rows].reshape(-1)[:size].reshape(a.shape))
        row += rows
    return out


def _small_update(gathered, w, m, v):
    rows = gathered.shape[1]

    def body(g_ref, w_ref, m_ref, v_ref, tot_ref, d_ref, nm_ref, nv_ref):
        total = g_ref[0].astype(F32)
        for dev in range(1, 8):
            total = total + g_ref[dev].astype(F32)
        tot_ref[...] = total
        d_ref[...], nm_ref[...], nv_ref[...] = _adamw_math(w_ref[...], total, m_ref[...], v_ref[...])

    return pl.pallas_call(
        body, name="small_update",
        in_specs=[VMEM_SPEC] * 4, out_specs=[VMEM_SPEC] * 4,
        out_shape=[jax.ShapeDtypeStruct((rows, 128), F32)] * 4,
        compiler_params=pltpu.CompilerParams(vmem_limit_bytes=24 * MIB),
    )(gathered, w, m, v)


def _halves(a):
    return a.reshape(a.shape[:-2] + (2, a.shape[-2] // 2, a.shape[-1]))


def _whole(a):
    return a.reshape(a.shape[:-3] + (2 * a.shape[-2], a.shape[-1]))


def kernel(x, p, norm1_gain, w_in, gmlp_v_gain, w_spatial, b_spatial, attn_sinks, rel_bias_table, w_out, norm2_gain, w_ff1, w_ff2, w_ple_proj, w_ple_gate, final_gain, loss_target, m_norm1_gain, m_w_in, m_gmlp_v_gain, m_w_spatial, m_b_spatial, m_attn_sinks, m_rel_bias_table, m_w_out, m_norm2_gain, m_w_ff1, m_w_ff2, m_w_ple_proj, m_w_ple_gate, m_final_gain, v_norm1_gain, v_w_in, v_gmlp_v_gain, v_w_spatial, v_b_spatial, v_attn_sinks, v_rel_bias_table, v_w_out, v_norm2_gain, v_w_ff1, v_w_ff2, v_w_ple_proj, v_w_ple_gate, v_final_gain):
    given = dict(locals())
    small = {n: given[n] for n in SMALL_NAMES}
    chip = 2 * lax.axis_index("x") + lax.axis_index("y")
    place = jnp.stack([chip, lax.axis_index("c")]).astype(jnp.int32)

    big_names = ("w_in", "w_out", "w_ff1", "w_ff2", "w_ple_proj", "w_ple_gate")
    shards = {n: given[n][0] for n in big_names}
    travel = dict(shards, w_in=jnp.transpose(shards["w_in"]))
    bufs = {n: _cast_shard(travel[n], place[:1]) for n in big_names}
    dx, landed, small_grads, sq = _step(x[0], p[0, 0], loss_target[0], small, bufs, place)

    out_grad, out_delta, out_m, out_v = {}, {}, {}, {}

    def update(n, g, exchange=None):
        to = jnp.transpose if n == "w_in" else (lambda a: a)
        (delta, new_m, new_v, g_out), got = _adamw(to(shards[n]), g, to(given["m_" + n][0]), to(given["v_" + n][0]),
                                                   exchange)
        out_grad[n], out_delta[n], out_m[n], out_v[n] = [to(a)[None] for a in (g_out, delta, new_m, new_v)]
        return got

    spare = jnp.zeros((8, 128), F32)
    small_packed = _pack_small([small_grads[n] for n in SMALL_NAMES] + [spare]).astype(BF16)
    reduced, (small_gathered, sq_gathered) = _sibling_allgather(
        [_chip_sum(landed[n], place) for n in big_names], _Both(_GatherAll(small_packed), _GatherAll(sq)))
    for n, r in zip(big_names, reduced):
        update(n, _whole(r))

    like = [given[n] for n in SMALL_NAMES] + [spare]
    packed = _small_update(small_gathered, *[_pack_small([given[pre + n] for n in SMALL_NAMES] + [spare])
                                             for pre in ("", "m_", "v_")])
    for res, out in zip(packed, (out_grad, out_delta, out_m, out_v)):
        out.update(zip(SMALL_NAMES, _unpack_small(res, like)))
    loss = 0.5 * jnp.sum(sq_gathered[:, 0, 0]) / D

    order = ("norm1_gain", "w_in", "gmlp_v_gain", "w_spatial", "b_spatial", "attn_sinks", "rel_bias_table", "w_out",
             "norm2_gain", "w_ff1", "w_ff2", "w_ple_proj", "w_ple_gate", "final_gain")
    return (loss, dx[None], *[out_grad[n] for n in order], *[out_delta[n] for n in order],
            *[out_m[n] for n in order], *[out_v[n] for n in order])
```

```python
import functools
import math

import jax
import jax.numpy as jnp
from jax import lax
from jax.experimental import pallas as pl
from jax.experimental.pallas import tpu as pltpu

S = 2048
D = 1024
D_IN = 1792
D_FF = 4096
PLE = 256
N_CHIP = 4
N_GROUP = 4
CHUNK = 128
N_HEAD = 8
N_BLOCK = S // CHUNK
N_BUCKET = 32
EPS = 1e-6
NEG_INF = -1e30
QK_SCALE = 0.125
GELU_C = math.sqrt(2.0 / math.pi)

ADAM_LR = 0.001
ADAM_B1 = 0.9
ADAM_B2 = 0.999
ADAM_EPS = 1e-08
ADAM_WD = 0.01
ADAM_STEP = 10

F32 = jnp.float32
BF16 = jnp.bfloat16
MIB = 1024 * 1024
MESH = pl.DeviceIdType.MESH

NT = (((1,), (1,)), ((), ()))
TN = (((0,), (0,)), ((), ()))


def _dot(a, b):
    return jnp.dot(a, b, preferred_element_type=F32)


def _dot_nt(a, b):
    return lax.dot_general(a, b, NT, preferred_element_type=F32)


def _dot_tn(a, b):
    return lax.dot_general(a, b, TN, preferred_element_type=F32)


def _params(vmem_mib, n_axes=1):
    return pltpu.CompilerParams(dimension_semantics=("arbitrary",) * n_axes, vmem_limit_bytes=vmem_mib * MIB)


def _rms_scale(v):
    return lax.rsqrt(jnp.mean(v * v, axis=-1, keepdims=True) + EPS)


def _rms_bwd(dy_gain, xhat, r):
    return r * (dy_gain - xhat * jnp.mean(dy_gain * xhat, axis=-1, keepdims=True))


class _Gather:
    def __init__(self, bufs):
        self.operands = list(bufs)
        self.n_out = len(self.operands)
        self.out_shape = [_hbm_like(b) for b in bufs]
        self.aliases = {w: w for w in range(self.n_out)}
        self.sems = _gather_sems(self.n_out)

    def start(self, ins, outs, sems):
        _gather_start(outs, *sems)

    def finish(self, ins, outs, sems):
        _gather_finish(outs, *sems)


class _ChipExchange:
    def __init__(self, sums, landing):
        self.n_out = len(landing)
        self.operands = list(sums) + list(landing)
        self.out_shape = [_hbm_like(b) for b in landing]
        self.aliases = {self.n_out + w: w for w in range(self.n_out)}
        self.sems = _chip_exchange_sems(self.n_out)

    def start(self, ins, outs, sems):
        _chip_exchange_start(ins[:self.n_out], outs, *sems)

    def finish(self, ins, outs, sems):
        _chip_exchange_finish(ins[:self.n_out], outs, *sems)


class _GatherAll:
    def __init__(self, packed):
        self.operands = [packed]
        self.n_out = 1
        self.out_shape = [_hbm_like(packed, (8,) + packed.shape)]
        self.aliases = {}
        self.sems = [pltpu.SemaphoreType.DMA((8,)), pltpu.SemaphoreType.DMA((8,))]

    def _copies(self, ins, outs, sems):
        x, y, c, _ = _mesh_place()
        me = 4 * x + 2 * y + c
        send_sems, recv_sems = sems
        copies = []
        for k in range(1, 8):
            peer = (1 - x if k // 4 else x, 1 - y if (k // 2) % 2 else y, 1 - c if k % 2 else c)
            src = 4 * peer[0] + 2 * peer[1] + peer[2]
            copies.append((_remote(ins[0], outs[0].at[me], send_sems.at[k], recv_sems.at[k], peer), outs[0].at[src]))
        own = pltpu.make_async_copy(ins[0], outs[0].at[me], send_sems.at[0])
        return own, copies

    def start(self, ins, outs, sems):
        own, copies = self._copies(ins, outs, sems)
        own.start()
        for cp, _ in copies:
            cp.start()

    def finish(self, ins, outs, sems):
        own, copies = self._copies(ins, outs, sems)
        x, y, c, _ = _mesh_place()
        for k, (cp, landed) in enumerate(copies):
            _remote(landed, landed, sems[0].at[k + 1], sems[1].at[k + 1], (x, y, c)).wait_recv()
        for cp, _ in copies:
            cp.wait_send()
        own.wait()


class _Both:
    def __init__(self, a, b):
        self.a, self.b = a, b
        self.operands = a.operands + b.operands
        self.n_out = a.n_out + b.n_out
        self.out_shape = a.out_shape + b.out_shape
        self.aliases = dict(a.aliases)
        self.aliases.update({len(a.operands) + i: a.n_out + o for i, o in b.aliases.items()})
        self.sems = a.sems + b.sems

    def _split(self, ins, outs, sems):
        ka, na, sa = len(self.a.operands), self.a.n_out, len(self.a.sems)
        return (ins[:ka], outs[:na], sems[:sa]), (ins[ka:], outs[na:], sems[sa:])

    def start(self, ins, outs, sems):
        for ex, args in zip((self.a, self.b), self._split(ins, outs, sems)):
            ex.start(*args)

    def finish(self, ins, outs, sems):
        for ex, args in zip((self.a, self.b), self._split(ins, outs, sems)):
            ex.finish(*args)


class _SiblingExchange:
    def __init__(self, grads):
        self.operands = list(grads)
        self.n_out = len(self.operands)
        self.out_shape = [_hbm_like(g, (N_CHIP,) + g.shape[2:]) for g in grads]
        self.aliases = {}
        self.sems = _sibling_exchange_sems(self.n_out)

    def start(self, ins, outs, sems):
        _sibling_exchange_start(ins, outs, *sems)

    def finish(self, ins, outs, sems):
        _sibling_exchange_finish(ins, outs, *sems)


def _call(body, operands, *, grid, in_specs, out_specs, out_shape, name, compiler_params, scratch_shapes=(),
          exchange=None):
    operands = [o if getattr(spec, "memory_space", None) == pltpu.SMEM else _in_hbm(o)
                for o, spec in zip(operands, in_specs)]
    out_shape = [pltpu.HBM(s.shape, s.dtype) for s in out_shape]
    if exchange is None:
        res = pl.pallas_call(body, grid=grid, in_specs=in_specs, out_specs=out_specs, out_shape=out_shape, name=name,
                             scratch_shapes=list(scratch_shapes), compiler_params=compiler_params)(*operands)
        return list(res), []
    n_in, n_out, n_scr = len(in_specs), len(out_specs), len(scratch_shapes)
    k_in, k_out = len(exchange.operands), exchange.n_out

    def fused(*refs):
        ins, refs = refs[:n_in], refs[n_in:]
        ex_ins, refs = refs[:k_in], refs[k_in:]
        outs, refs = refs[:n_out], refs[n_out:]
        ex_outs, refs = refs[:k_out], refs[k_out:]
        scratch, sems = refs[:n_scr], refs[n_scr:]
        ids = [pl.program_id(a) for a in range(len(grid))]
        first = functools.reduce(jnp.logical_and, [i == 0 for i in ids])
        last = functools.reduce(jnp.logical_and, [i == g - 1 for i, g in zip(ids, grid)])

        @pl.when(first)
        def _():
            exchange.start(ex_ins, ex_outs, sems)

        body(*ins, *outs, *scratch)

        @pl.when(last)
        def _():
            exchange.finish(ex_ins, ex_outs, sems)

    res = pl.pallas_call(
        fused, grid=grid, name=name,
        in_specs=list(in_specs) + [HBM_SPEC] * k_in, out_specs=list(out_specs) + [HBM_SPEC] * k_out,
        out_shape=list(out_shape) + exchange.out_shape,
        input_output_aliases={n_in + i: n_out + o for i, o in exchange.aliases.items()},
        scratch_shapes=list(scratch_shapes) + exchange.sems, compiler_params=compiler_params,
    )(*operands, *[_in_hbm(o) for o in exchange.operands])
    return list(res[:n_out]), list(res[n_out:])


def _in_hbm(a):
    return pltpu.with_memory_space_constraint(a, pltpu.HBM)


def _row_tile(h):
    return max(t for t in range(16, 513, 16) if h % t == 0)


def _cast_shard(a, chip):
    rows, cols = a.shape
    h = rows // 2
    tr = _row_tile(h)

    def body(chip_ref, a_ref, o_ref):
        o_ref[0, 0] = a_ref[0].astype(BF16)

    return pl.pallas_call(
        body, name="cast_shard",
        grid_spec=pltpu.PrefetchScalarGridSpec(
            num_scalar_prefetch=1, grid=(2, h // tr),
            in_specs=[pl.BlockSpec((1, tr, cols), lambda s, r, chip_ref: (s, r, 0))],
            out_specs=pl.BlockSpec((1, 1, tr, cols), lambda s, r, chip_ref: (chip_ref[0], s, r, 0))),
        out_shape=pltpu.HBM((N_CHIP, 2, h, cols), BF16),
        compiler_params=_params(16, 2),
    )(chip, _in_hbm(a.reshape(2, h, cols)))


def _in_proj(x, gain1, w_in_t, exchange=None):
    tm = 256

    def body(x_ref, g_ref, w_ref, z_ref, hn_ref):
        xv = x_ref[...]
        hn = (xv * _rms_scale(xv) * g_ref[...]).astype(BF16)
        hn_ref[...] = hn
        z_ref[...] = _dot_nt(hn, w_ref[...])

    return _call(
        body, (x, gain1, w_in_t), grid=(S // tm,), name="in_proj",
        in_specs=[pl.BlockSpec((tm, D), lambda i: (i, 0)), pl.BlockSpec((1, D), lambda i: (0, 0)),
                  pl.BlockSpec((D_IN, D), lambda i: (0, 0))],
        out_specs=[pl.BlockSpec((tm, D_IN), lambda i: (i, 0)), pl.BlockSpec((tm, D), lambda i: (i, 0))],
        out_shape=[jax.ShapeDtypeStruct((S, D_IN), F32), jax.ShapeDtypeStruct((S, D), BF16)],
        compiler_params=_params(40), exchange=exchange)


def _gelu_parts(v):
    t = jnp.tanh(GELU_C * (v + 0.044715 * (v * v * v)))
    cdf = 0.5 * (1.0 + t)
    return cdf, t


def _band_mask(n):
    a = lax.broadcasted_iota(jnp.int32, (CHUNK, 2 * CHUNK), 0)
    j = lax.broadcasted_iota(jnp.int32, (CHUNK, 2 * CHUNK), 1)
    dist = CHUNK + a - j
    valid = (dist >= 0) & (dist < CHUNK)
    return valid & ((n > 0) | (j >= CHUNK))


def _fill_bias(bucket_ref, table_ref, bias_ref):
    bucket = bucket_ref[...]
    for h in range(N_HEAD):
        acc = jnp.zeros((CHUNK, 2 * CHUNK), F32)
        for b in range(N_BUCKET):
            acc = jnp.where(bucket == b, table_ref[b, h], acc)
        bias_ref[h] = acc


def _fill_tril(ws_ref, wt_ref, wtt_ref=None):
    r = lax.broadcasted_iota(jnp.int32, (CHUNK, CHUNK), 0)
    c = lax.broadcasted_iota(jnp.int32, (CHUNK, CHUNK), 1)
    for g in range(N_GROUP):
        w = jnp.where(c <= r, ws_ref[g], 0.0)
        wt_ref[g] = w.astype(BF16)
        if wtt_ref is not None:
            wtt_ref[g] = w.T.astype(BF16)


def _kv_layouts(kv_prev, kv_cur):
    both = jnp.concatenate([kv_prev, kv_cur], axis=0)
    k = both[:, :128]
    v = both[:, 128:]
    return (k.astype(BF16), pltpu.roll(k, 64, axis=1).astype(BF16),
            v.astype(BF16), pltpu.roll(v, 64, axis=1).astype(BF16))


def _head_place(h):
    pair, pos, kvh = h // 2, h % 2, h // 4
    return pair, pos, kvh == pos


def _softmax_sink(qm, k_use, bias_h, sink, valid):
    s = _dot_nt(qm, k_use) * QK_SCALE + bias_h
    s = jnp.where(valid, s, NEG_INF)
    m = jnp.maximum(jnp.max(s, axis=-1, keepdims=True), sink)
    e = jnp.exp(s - m)
    es = jnp.exp(sink - m)
    denom = jnp.sum(e, axis=-1, keepdims=True) + es
    return e / denom, es / denom


def _mixer_fwd(z, v_gain, w_spatial, b_spatial_t, sinks, rel_table, bucket, exchange=None):
    def body(z_ref, kvp_ref, gain_ref, ws_ref, bt_ref, sink_ref, table_ref, bucket_ref, out_ref, bias_ref, wt_ref):
        n = pl.program_id(0)

        @pl.when(n == 0)
        def _():
            _fill_bias(bucket_ref, table_ref, bias_ref)
            _fill_tril(ws_ref, wt_ref)

        zuv = z_ref[:, :1024]
        cdf, _ = _gelu_parts(zuv)
        guv = zuv * cdf
        for g in range(N_GROUP):
            vg = guv[:, 512 + 128 * g:512 + 128 * (g + 1)]
            vn = vg * _rms_scale(vg) * gain_ref[:, 128 * g:128 * (g + 1)]
            sv = _dot(wt_ref[g], vn.astype(BF16)) + bt_ref[:, g:g + 1]
            out_ref[:, 128 * g:128 * (g + 1)] = (guv[:, 128 * g:128 * (g + 1)] * sv).astype(BF16)

        k_same, k_swap, v_same, v_swap = _kv_layouts(kvp_ref[...], z_ref[:, 1536:1792])
        valid = _band_mask(n)
        lane_half = lax.broadcasted_iota(jnp.int32, (1, 128), 1) // 64
        for pair in range(N_HEAD // 2):
            qq = z_ref[:, 1024 + 128 * pair:1024 + 128 * (pair + 1)]
            acc = jnp.zeros((CHUNK, 128), F32)
            for pos in range(2):
                h = 2 * pair + pos
                _, _, same = _head_place(h)
                qm = jnp.where(lane_half == pos, qq, 0.0).astype(BF16)
                p, _ = _softmax_sink(qm, k_same if same else k_swap, bias_ref[h], sink_ref[h], valid)
                vm = jnp.where(lane_half == pos, v_same if same else v_swap, jnp.zeros((), BF16))
                acc = acc + _dot(p.astype(BF16), vm)
            out_ref[:, 512 + 128 * pair:512 + 128 * (pair + 1)] = acc.astype(BF16)

    return _call(
        body, (z, z, v_gain, w_spatial, b_spatial_t, sinks, rel_table, bucket), grid=(N_BLOCK,), name="mixer_fwd",
        in_specs=[pl.BlockSpec((CHUNK, D_IN), lambda n: (n, 0)),
                  pl.BlockSpec((CHUNK, 256), lambda n: (jnp.maximum(n - 1, 0), 6)),
                  pl.BlockSpec((1, 512), lambda n: (0, 0)),
                  pl.BlockSpec((N_GROUP, CHUNK, CHUNK), lambda n: (0, 0, 0)),
                  pl.BlockSpec((CHUNK, N_GROUP), lambda n: (0, 0)),
                  pl.BlockSpec(memory_space=pltpu.SMEM),
                  pl.BlockSpec(memory_space=pltpu.SMEM),
                  pl.BlockSpec((CHUNK, 2 * CHUNK), lambda n: (0, 0))],
        out_specs=[pl.BlockSpec((CHUNK, D), lambda n: (n, 0))],
        out_shape=[jax.ShapeDtypeStruct((S, D), BF16)],
        scratch_shapes=[pltpu.VMEM((N_HEAD, CHUNK, 2 * CHUNK), F32), pltpu.VMEM((N_GROUP, CHUNK, CHUNK), BF16)],
        compiler_params=_params(32), exchange=exchange)


def _out_proj(x, mix, w_out, gain2, exchange=None):
    tm = 256

    def body(x_ref, mix_ref, w_ref, g_ref, h1_ref, hn_ref, hnt_ref):
        h1 = x_ref[...] + _dot(mix_ref[...], w_ref[...])
        h1_ref[...] = h1
        hn = h1 * _rms_scale(h1) * g_ref[...]
        hn_ref[...] = hn.astype(BF16)
        hnt_ref[...] = hn.T.astype(BF16)

    return _call(
        body, (x, mix, w_out, gain2), grid=(S // tm,), name="out_proj",
        in_specs=[pl.BlockSpec((tm, D), lambda i: (i, 0)), pl.BlockSpec((tm, D), lambda i: (i, 0)),
                  pl.BlockSpec((D, D), lambda i: (0, 0)), pl.BlockSpec((1, D), lambda i: (0, 0))],
        out_specs=[pl.BlockSpec((tm, D), lambda i: (i, 0)), pl.BlockSpec((tm, D), lambda i: (i, 0)),
                   pl.BlockSpec((D, tm), lambda i: (0, i))],
        out_shape=[jax.ShapeDtypeStruct((S, D), F32), jax.ShapeDtypeStruct((S, D), BF16),
                   jax.ShapeDtypeStruct((D, S), BF16)],
        compiler_params=_params(32), exchange=exchange)


def _ffn_up(hn2, w_ff1, exchange=None):
    tm = 512
    nj = D_FF // 1024

    def body(hn_ref, w1_ref, r_ref, a_ref, at_ref):
        r = jnp.maximum(_dot(hn_ref[...], w1_ref[0]), 0.0)
        r_ref[...] = r.astype(BF16)
        a = r * r
        a_ref[...] = a.astype(BF16)
        at_ref[...] = a.T.astype(BF16)

    return _call(
        body, (hn2, w_ff1), grid=(nj, S // tm), name="ffn_up",
        in_specs=[pl.BlockSpec((tm, D), lambda j, i: (i, 0)), pl.BlockSpec((1, D, 1024), lambda j, i: (j, 0, 0))],
        out_specs=[pl.BlockSpec((tm, 1024), lambda j, i: (i, j)), pl.BlockSpec((tm, 1024), lambda j, i: (i, j)),
                   pl.BlockSpec((1024, tm), lambda j, i: (j, i))],
        out_shape=[jax.ShapeDtypeStruct((S, D_FF), BF16), jax.ShapeDtypeStruct((S, D_FF), BF16),
                   jax.ShapeDtypeStruct((D_FF, S), BF16)],
        compiler_params=_params(40, 2), exchange=exchange)


def _ffn_down(h1, a, w_ff2, exchange=None):
    tm = 1024
    nj = D_FF // 1024

    def body(h1_ref, a_ref, w2_ref, h2_ref, acc_ref):
        j = pl.program_id(1)
        part = _dot(a_ref[...], w2_ref[0])

        @pl.when(j == 0)
        def _():
            acc_ref[...] = part

        @pl.when(j > 0)
        def _():
            acc_ref[...] += part

        @pl.when(j == nj - 1)
        def _():
            h2_ref[...] = h1_ref[...] + acc_ref[...]

    return _call(
        body, (h1, a, w_ff2), grid=(S // tm, nj), name="ffn_down",
        in_specs=[pl.BlockSpec((tm, D), lambda i, j: (i, 0)), pl.BlockSpec((tm, 1024), lambda i, j: (i, j)),
                  pl.BlockSpec((1, 1024, D), lambda i, j: (j, 0, 0))],
        out_specs=[pl.BlockSpec((tm, D), lambda i, j: (i, 0))],
        out_shape=[jax.ShapeDtypeStruct((S, D), F32)],
        scratch_shapes=[pltpu.VMEM((tm, D), F32)],
        compiler_params=_params(48, 2), exchange=exchange)


def _tail(h2, p, target, w_gate, w_proj, final_gain):
    tm = 256
    steps = S // tm

    def body(h2_ref, p_ref, t_ref, wg_ref, wp_ref, gf_ref, dh2_ref, dwg_ref, dwp_ref, dgf_ref, loss_ref, dh2b_ref,
             dwp_acc):
        i = pl.program_id(0)
        h2 = h2_ref[...]
        h2b = h2.astype(BF16)
        pb = p_ref[...].astype(BF16)
        gate = jax.nn.sigmoid(_dot(h2b, wg_ref[...]))
        pp = jnp.concatenate([_dot(pb, wp_ref[j]) for j in range(N_CHIP)], axis=1)
        h3 = h2 + gate * pp
        r3 = _rms_scale(h3)
        xhat = h3 * r3
        gf = gf_ref[...]
        err = xhat * gf - t_ref[...]
        dy = err * (1.0 / D)
        dh3 = _rms_bwd(dy * gf, xhat, r3)
        dgp = (dh3 * pp * gate * (1.0 - gate)).astype(BF16)
        dpp = (dh3 * gate).astype(BF16)
        dh2 = dh3 + _dot_nt(dgp, wg_ref[...])
        dh2_ref[...] = dh2
        dh2b_ref[...] = dh2.astype(BF16)
        dwg = _dot_tn(h2b, dgp)
        dwp = _dot_tn(pb, dpp)
        dgf = jnp.sum(dy * xhat, axis=0, keepdims=True)
        sq = jnp.sum(jnp.sum(err * err, axis=1, keepdims=True), axis=0, keepdims=True)

        @pl.when(i == 0)
        def _():
            dwg_ref[...] = dwg
            dwp_acc[...] = dwp
            dgf_ref[...] = dgf
            loss_ref[...] = jnp.broadcast_to(sq, (8, 128))

        @pl.when(i > 0)
        def _():
            dwg_ref[...] += dwg
            dwp_acc[...] += dwp
            dgf_ref[...] += dgf
            loss_ref[...] += jnp.broadcast_to(sq, (8, 128))

        @pl.when(i == steps - 1)
        def _():
            for j in range(N_CHIP):
                dwp_ref[j] = dwp_acc[:, 256 * j:256 * (j + 1)]

    return _call(
        body, (h2, p, target, w_gate, w_proj, final_gain), grid=(steps,), name="tail",
        in_specs=[pl.BlockSpec((tm, D), lambda i: (i, 0)), pl.BlockSpec((tm, PLE), lambda i: (i, 0)),
                  pl.BlockSpec((tm, D), lambda i: (i, 0)), pl.BlockSpec((D, D), lambda i: (0, 0)),
                  pl.BlockSpec((N_CHIP, PLE, 256), lambda i: (0, 0, 0)), pl.BlockSpec((1, D), lambda i: (0, 0))],
        out_specs=[pl.BlockSpec((tm, D), lambda i: (i, 0)), pl.BlockSpec((D, D), lambda i: (0, 0)),
                   pl.BlockSpec((N_CHIP, PLE, 256), lambda i: (0, 0, 0)), pl.BlockSpec((1, D), lambda i: (0, 0)),
                   pl.BlockSpec((8, 128), lambda i: (0, 0)), pl.BlockSpec((tm, D), lambda i: (i, 0))],
        out_shape=[jax.ShapeDtypeStruct((S, D), F32), jax.ShapeDtypeStruct((D, D), F32),
                   jax.ShapeDtypeStruct((N_CHIP, PLE, 256), F32), jax.ShapeDtypeStruct((1, D), F32),
                   jax.ShapeDtypeStruct((8, 128), F32), jax.ShapeDtypeStruct((S, D), BF16)],
        scratch_shapes=[pltpu.VMEM((PLE, D), F32)],
        compiler_params=_params(48))[0]


def _ffn_bwd_down(dh2b, r, a_t, w_ff2, exchange=None):
    tm = 1024
    nj = D_FF // 1024

    def body(dh2_ref, r_ref, at_ref, w2_ref, df_ref, dw2_ref):
        i = pl.program_id(1)
        dh2b = dh2_ref[...]
        da = _dot_nt(dh2b, w2_ref[0])
        df_ref[...] = (da * (2.0 * r_ref[...].astype(F32))).astype(BF16)
        dw2 = _dot(at_ref[...], dh2b)

        @pl.when(i == 0)
        def _():
            dw2_ref[0] = dw2

        @pl.when(i > 0)
        def _():
            dw2_ref[0] += dw2

    return _call(
        body, (dh2b, r, a_t, w_ff2), grid=(nj, S // tm), name="ffn_bwd_down",
        in_specs=[pl.BlockSpec((tm, D), lambda j, i: (i, 0)), pl.BlockSpec((tm, 1024), lambda j, i: (i, j)),
                  pl.BlockSpec((1024, tm), lambda j, i: (j, i)), pl.BlockSpec((1, 1024, D), lambda j, i: (j, 0, 0))],
        out_specs=[pl.BlockSpec((tm, 1024), lambda j, i: (i, j)), pl.BlockSpec((1, 1024, D), lambda j, i: (j, 0, 0))],
        out_shape=[jax.ShapeDtypeStruct((S, D_FF), BF16), jax.ShapeDtypeStruct((nj, 1024, D), F32)],
        compiler_params=_params(48, 2), exchange=exchange)


def _ffn_bwd_up(df, hn2_t, exchange=None):
    tm = 1024
    nj = D_FF // 1024

    def body(df_ref, hnt_ref, dw1_ref):
        i = pl.program_id(1)
        dw1 = _dot(hnt_ref[...], df_ref[...])

        @pl.when(i == 0)
        def _():
            dw1_ref[0] = dw1

        @pl.when(i > 0)
        def _():
            dw1_ref[0] += dw1

    return _call(
        body, (df, hn2_t), grid=(nj, S // tm), name="ffn_bwd_up",
        in_specs=[pl.BlockSpec((tm, 1024), lambda j, i: (i, j)), pl.BlockSpec((D, tm), lambda j, i: (0, i))],
        out_specs=[pl.BlockSpec((1, D, 1024), lambda j, i: (j, 0, 0))],
        out_shape=[jax.ShapeDtypeStruct((nj, D, 1024), F32)],
        compiler_params=_params(40, 2), exchange=exchange)


def _ffn_bwd_input(df, w_ff1, dh2, h1, gain2, mix, w_out, exchange=None):
    tm = 512
    nj = D_FF // 1024
    steps = S // tm

    def body(df_ref, w1_ref, dh2_ref, h1_ref, g_ref, mix_ref, wo_ref, dh1_ref, dmix_ref, dwo_ref, dg_ref, acc_ref):
        i = pl.program_id(0)
        j = pl.program_id(1)
        part = _dot_nt(df_ref[...], w1_ref[0])

        @pl.when(j == 0)
        def _():
            acc_ref[...] = part

        @pl.when(j > 0)
        def _():
            acc_ref[...] += part

        @pl.when(j == nj - 1)
        def _():
            dhn = acc_ref[...]
            h1 = h1_ref[...]
            r2 = _rms_scale(h1)
            xhat = h1 * r2
            dh1 = dh2_ref[...] + _rms_bwd(dhn * g_ref[...], xhat, r2)
            dh1_ref[...] = dh1
            dh1b = dh1.astype(BF16)
            dmix_ref[...] = _dot_nt(dh1b, wo_ref[...])
            dwo = _dot_tn(mix_ref[...], dh1b)
            dg = jnp.sum(dhn * xhat, axis=0, keepdims=True)

            @pl.when(i == 0)
            def _():
                dwo_ref[...] = dwo
                dg_ref[...] = dg

            @pl.when(i > 0)
            def _():
                dwo_ref[...] += dwo
                dg_ref[...] += dg

    return _call(
        body, (df, w_ff1, dh2, h1, gain2, mix, w_out), grid=(steps, nj), name="ffn_bwd_input",
        in_specs=[pl.BlockSpec((tm, 1024), lambda i, j: (i, j)), pl.BlockSpec((1, D, 1024), lambda i, j: (j, 0, 0)),
                  pl.BlockSpec((tm, D), lambda i, j: (i, 0)), pl.BlockSpec((tm, D), lambda i, j: (i, 0)),
                  pl.BlockSpec((1, D), lambda i, j: (0, 0)), pl.BlockSpec((tm, D), lambda i, j: (i, 0)),
                  pl.BlockSpec((D, D), lambda i, j: (0, 0))],
        out_specs=[pl.BlockSpec((tm, D), lambda i, j: (i, 0)), pl.BlockSpec((tm, D), lambda i, j: (i, 0)),
                   pl.BlockSpec((D, D), lambda i, j: (0, 0)), pl.BlockSpec((1, D), lambda i, j: (0, 0))],
        out_shape=[jax.ShapeDtypeStruct((S, D), F32), jax.ShapeDtypeStruct((S, D), F32),
                   jax.ShapeDtypeStruct((D, D), F32), jax.ShapeDtypeStruct((1, D), F32)],
        scratch_shapes=[pltpu.VMEM((tm, D), F32)],
        compiler_params=_params(56, 2), exchange=exchange)


IN_GROUP = 4


def _mixer_bwd(z, dmix, v_gain, w_spatial, b_spatial_t, sinks, rel_table, bucket, hn1, exchange=None):
    def body(z_ref, kvp_ref, dm_ref, gain_ref, ws_ref, bt_ref, sink_ref, table_ref, bucket_ref, hn_ref,
             dz_ref, dws_ref, db_ref, dgain_ref, dsink_ref, drel_ref, dwin_ref,
             bias_ref, wt_ref, wtt_ref, dbias_ref, dsv_ref, carry_ref):
        n = pl.program_id(0)

        @pl.when(n == 0)
        def _():
            _fill_bias(bucket_ref, table_ref, bias_ref)
            _fill_tril(ws_ref, wt_ref, wtt_ref)
            dwin_ref[...] = jnp.zeros_like(dwin_ref)
            dbias_ref[...] = jnp.zeros_like(dbias_ref)
            dsv_ref[...] = jnp.zeros_like(dsv_ref)
            dws_ref[...] = jnp.zeros_like(dws_ref)
            dgain_ref[...] = jnp.zeros_like(dgain_ref)
            dsink_ref[...] = jnp.zeros_like(dsink_ref)

        rows = pl.ds(pl.multiple_of(n * CHUNK, CHUNK), CHUNK)

        zuv = z_ref[:, :1024]
        cdf, t = _gelu_parts(zuv)
        guv = zuv * cdf
        dgelu = cdf + zuv * (0.5 * (1.0 - t * t)) * (GELU_C * (1.0 + 3.0 * 0.044715 * (zuv * zuv)))
        for g in range(N_GROUP):
            lo, hi = 128 * g, 128 * (g + 1)
            u = guv[:, lo:hi]
            vg = guv[:, 512 + lo:512 + hi]
            rr = _rms_scale(vg)
            vhat = vg * rr
            gain = gain_ref[:, lo:hi]
            vnb = (vhat * gain).astype(BF16)
            sv = _dot(wt_ref[g], vnb) + bt_ref[:, g:g + 1]
            da = dm_ref[:, lo:hi]
            dsv = da * u
            dsvb = dsv.astype(BF16)
            dsv_ref[g] += dsv
            dws_ref[g] += _dot_nt(dsvb, vnb)
            dvn = _dot(wtt_ref[g], dsvb)
            dgain_ref[:, lo:hi] += jnp.sum(dvn * vhat, axis=0, keepdims=True)
            dvg = _rms_bwd(dvn * gain, vhat, rr)
            dz_ref[rows, lo:hi] = (da * sv * dgelu[:, lo:hi]).astype(BF16)
            dz_ref[rows, 512 + lo:512 + hi] = (dvg * dgelu[:, 512 + lo:512 + hi]).astype(BF16)

        k_same, k_swap, v_same, v_swap = _kv_layouts(kvp_ref[...], z_ref[:, 1536:1792])
        valid = _band_mask(n)
        lane_half = lax.broadcasted_iota(jnp.int32, (1, 128), 1) // 64
        zero = jnp.zeros((2 * CHUNK, 128), F32)
        dk_same, dk_swap, dv_same, dv_swap = zero, zero, zero, zero
        for pair in range(N_HEAD // 2):
            cols = slice(1024 + 128 * pair, 1024 + 128 * (pair + 1))
            qq = z_ref[:, cols]
            do_pair = dm_ref[:, 512 + 128 * pair:512 + 128 * (pair + 1)]
            dq = jnp.zeros((CHUNK, 128), F32)
            for pos in range(2):
                h = 2 * pair + pos
                _, _, same = _head_place(h)
                on_half = lane_half == pos
                qm = jnp.where(on_half, qq, 0.0).astype(BF16)
                k_use = k_same if same else k_swap
                v_use = v_same if same else v_swap
                p, p_sink = _softmax_sink(qm, k_use, bias_ref[h], sink_ref[h], valid)
                dom = jnp.where(on_half, do_pair, 0.0).astype(BF16)
                dp = _dot_nt(dom, v_use)
                dsum = jnp.sum(p * dp, axis=-1, keepdims=True)
                ds = p * (dp - dsum)
                dbias_ref[h] += ds
                dsink_ref[h:h + 1, :] += jnp.broadcast_to(jnp.sum(-p_sink * dsum, axis=0, keepdims=True), (1, 128))
                dsb = ds.astype(BF16)
                dq = dq + jnp.where(on_half, _dot(dsb, k_use), 0.0)
                dk_h = _dot_tn(dsb, qm)
                dv_h = _dot_tn(p.astype(BF16), dom)
                if same:
                    dk_same, dv_same = dk_same + dk_h, dv_same + dv_h
                else:
                    dk_swap, dv_swap = dk_swap + dk_h, dv_swap + dv_h
            dz_ref[rows, cols] = (dq * QK_SCALE).astype(BF16)
        dk = (dk_same + pltpu.roll(dk_swap, 64, axis=1)) * QK_SCALE
        dv = dv_same + pltpu.roll(dv_swap, 64, axis=1)
        dkv = jnp.concatenate([dk, dv], axis=1)

        @pl.when(n > 0)
        def _():
            prev_rows = pl.ds(pl.multiple_of((n - 1) * CHUNK, CHUNK), CHUNK)
            dz_ref[prev_rows, 1536:1792] = (carry_ref[...] + dkv[:CHUNK]).astype(BF16)

        carry_ref[...] = dkv[CHUNK:]

        @pl.when((n > 0) & (n % IN_GROUP == 0))
        def _():
            done = pl.ds(pl.multiple_of((n - IN_GROUP) * CHUNK, IN_GROUP * CHUNK), IN_GROUP * CHUNK)
            dwin_ref[...] += _dot_tn(dz_ref[done, :], hn_ref[...])

        @pl.when(n == N_BLOCK - 1)
        def _():
            dz_ref[rows, 1536:1792] = dkv[CHUNK:].astype(BF16)
            last = pl.ds((N_BLOCK - IN_GROUP) * CHUNK, IN_GROUP * CHUNK)
            dwin_ref[...] += _dot_tn(dz_ref[last, :], hn_ref[...])
            r = lax.broadcasted_iota(jnp.int32, (CHUNK, CHUNK), 0)
            c = lax.broadcasted_iota(jnp.int32, (CHUNK, CHUNK), 1)
            for g in range(N_GROUP):
                dws_ref[g] = jnp.where(c <= r, dws_ref[g], 0.0)
                db_ref[g] = jnp.sum(dsv_ref[g], axis=1, keepdims=True)
            bucket = bucket_ref[...]
            for h in range(N_HEAD):
                dbh = dbias_ref[h]
                per_bucket = [jnp.sum(jnp.where(bucket == b, dbh, 0.0), axis=0, keepdims=True) for b in range(N_BUCKET)]
                drel_ref[h] = jnp.sum(jnp.concatenate(per_bucket, axis=0), axis=1, keepdims=True)

    def hn_group(n):
        return jnp.where(n == N_BLOCK - 1, N_BLOCK // IN_GROUP - 1, jnp.maximum(n // IN_GROUP - 1, 0))

    return _call(
        body, (z, z, dmix, v_gain, w_spatial, b_spatial_t, sinks, rel_table, bucket, hn1), grid=(N_BLOCK,),
        name="mixer_bwd",
        in_specs=[pl.BlockSpec((CHUNK, D_IN), lambda n: (n, 0)),
                  pl.BlockSpec((CHUNK, 256), lambda n: (jnp.maximum(n - 1, 0), 6)),
                  pl.BlockSpec((CHUNK, D), lambda n: (n, 0)),
                  pl.BlockSpec((1, 512), lambda n: (0, 0)),
                  pl.BlockSpec((N_GROUP, CHUNK, CHUNK), lambda n: (0, 0, 0)),
                  pl.BlockSpec((CHUNK, N_GROUP), lambda n: (0, 0)),
                  pl.BlockSpec(memory_space=pltpu.SMEM),
                  pl.BlockSpec(memory_space=pltpu.SMEM),
                  pl.BlockSpec((CHUNK, 2 * CHUNK), lambda n: (0, 0)),
                  pl.BlockSpec((IN_GROUP * CHUNK, D), lambda n: (hn_group(n), 0))],
        out_specs=[pl.BlockSpec((S, D_IN), lambda n: (0, 0)),
                   pl.BlockSpec((N_GROUP, CHUNK, CHUNK), lambda n: (0, 0, 0)),
                   pl.BlockSpec((N_GROUP, CHUNK, 1), lambda n: (0, 0, 0)),
                   pl.BlockSpec((1, 512), lambda n: (0, 0)),
                   pl.BlockSpec((N_HEAD, 128), lambda n: (0, 0)),
                   pl.BlockSpec((N_HEAD, N_BUCKET, 1), lambda n: (0, 0, 0)),
                   pl.BlockSpec((D_IN, D), lambda n: (0, 0))],
        out_shape=[jax.ShapeDtypeStruct((S, D_IN), BF16), jax.ShapeDtypeStruct((N_GROUP, CHUNK, CHUNK), F32),
                   jax.ShapeDtypeStruct((N_GROUP, CHUNK, 1), F32), jax.ShapeDtypeStruct((1, 512), F32),
                   jax.ShapeDtypeStruct((N_HEAD, 128), F32), jax.ShapeDtypeStruct((N_HEAD, N_BUCKET, 1), F32),
                   jax.ShapeDtypeStruct((D_IN, D), F32)],
        scratch_shapes=[pltpu.VMEM((N_HEAD, CHUNK, 2 * CHUNK), F32), pltpu.VMEM((N_GROUP, CHUNK, CHUNK), BF16),
                        pltpu.VMEM((N_GROUP, CHUNK, CHUNK), BF16), pltpu.VMEM((N_HEAD, CHUNK, 2 * CHUNK), F32),
                        pltpu.VMEM((N_GROUP, CHUNK, CHUNK), F32), pltpu.VMEM((CHUNK, 256), F32)],
        compiler_params=_params(48), exchange=exchange)


def _in_bwd_input(dz, w_in_t, x, dh1, gain1, exchange=None):
    tm = 512

    def body(dz_ref, w_ref, x_ref, dh1_ref, g_ref, dx_ref, dg_ref):
        i = pl.program_id(0)
        dhn = _dot(dz_ref[...], w_ref[...])
        xv = x_ref[...]
        r1 = _rms_scale(xv)
        xhat = xv * r1
        dx_ref[...] = dh1_ref[...] + _rms_bwd(dhn * g_ref[...], xhat, r1)
        dg = jnp.sum(dhn * xhat, axis=0, keepdims=True)

        @pl.when(i == 0)
        def _():
            dg_ref[...] = dg

        @pl.when(i > 0)
        def _():
            dg_ref[...] += dg

    return _call(
        body, (dz, w_in_t, x, dh1, gain1), grid=(S // tm,), name="in_bwd_input",
        in_specs=[pl.BlockSpec((tm, D_IN), lambda i: (i, 0)), pl.BlockSpec((D_IN, D), lambda i: (0, 0)),
                  pl.BlockSpec((tm, D), lambda i: (i, 0)), pl.BlockSpec((tm, D), lambda i: (i, 0)),
                  pl.BlockSpec((1, D), lambda i: (0, 0))],
        out_specs=[pl.BlockSpec((tm, D), lambda i: (i, 0)), pl.BlockSpec((1, D), lambda i: (0, 0))],
        out_shape=[jax.ShapeDtypeStruct((S, D), F32), jax.ShapeDtypeStruct((1, D), F32)],
        compiler_params=_params(48), exchange=exchange)


def _rel_bucket():
    a = jnp.arange(CHUNK)[:, None]
    j = jnp.arange(2 * CHUNK)[None, :]
    n = jnp.maximum(CHUNK + a - j, 0)
    max_exact = N_BUCKET // 2
    nf = jnp.maximum(n, 1).astype(jnp.float32)
    large = max_exact + (jnp.log(nf / max_exact) / math.log(CHUNK / max_exact) * (N_BUCKET - max_exact)).astype(jnp.int32)
    large = jnp.minimum(large, N_BUCKET - 1)
    return jnp.where(n < max_exact, n, large).astype(jnp.int32)


def _step(x, p, target, small, bufs, place):
    bucket = _rel_bucket()
    sinks = small["attn_sinks"].reshape(N_HEAD)
    b_t = jnp.transpose(small["b_spatial"].reshape(N_GROUP, CHUNK))
    ws = small["w_spatial"].reshape(N_GROUP, CHUNK, CHUNK)
    gain1, gain2 = small["norm1_gain"], small["norm2_gain"]
    v_gain = small["gmlp_v_gain"]
    final_gain = small["final_gain"].reshape(1, D)
    table = small["rel_bias_table"]
    bufs = dict(bufs)

    def gather(*names):
        return _Gather([bufs[n] for n in names])

    def took(names, got):
        bufs.update(zip(names, got))

    took(["w_in"], _gather_weights([bufs["w_in"]]))
    w_in_t = _whole(bufs["w_in"]).reshape(D_IN, D)
    (z, hn1), got = _in_proj(x, gain1, w_in_t, gather("w_out"))
    took(["w_out"], got)
    (mix,), got = _mixer_fwd(z, v_gain, ws, b_t, sinks, table, bucket, gather("w_ff1"))
    took(["w_ff1"], got)
    w_out = _whole(bufs["w_out"]).reshape(D, D)
    (h1, hn2, hn2_t), _ = _out_proj(x, mix, w_out, gain2)
    w_ff1 = _whole(bufs["w_ff1"])
    (r, a, a_t), got = _ffn_up(hn2, w_ff1, gather("w_ff2"))
    took(["w_ff2"], got)
    w_ff2 = _whole(bufs["w_ff2"])
    (h2,), got = _ffn_down(h1, a, w_ff2, gather("w_ple_gate", "w_ple_proj"))
    took(["w_ple_gate", "w_ple_proj"], got)
    dh2, d_gate, d_proj, d_final, sq, dh2b = _tail(h2, p, target, _whole(bufs["w_ple_gate"]).reshape(D, D),
                                                   _whole(bufs["w_ple_proj"]), final_gain)

    def pair_sums(halves, from_sibling):
        sums, landing = zip(*[_pair_sum(g, o, place) for g, o in zip(halves, from_sibling)])
        return list(sums), list(landing)

    landed = {}
    halves = [_halves(d_gate.reshape(N_CHIP, 256, D)), _halves(d_proj)]
    (df, d_ff2), got = _ffn_bwd_down(dh2b, r, a_t, w_ff2, _SiblingExchange(halves))
    ex, halves = _ChipExchange(*pair_sums(halves, got)), [_halves(d_ff2)]
    (d_ff1,), got = _ffn_bwd_up(df, hn2_t, _Both(ex, _SiblingExchange(halves)))
    landed.update(zip(["w_ple_gate", "w_ple_proj"], got[:2]))
    ex, halves = _ChipExchange(*pair_sums(halves, got[2:])), [_halves(d_ff1)]
    (dh1, dmix, d_out, d_gain2), got = _ffn_bwd_input(df, w_ff1, dh2, h1, gain2, mix, w_out,
                                                      _Both(ex, _SiblingExchange(halves)))
    landed["w_ff2"] = got[0]
    ex, halves = _ChipExchange(*pair_sums(halves, got[1:])), [_halves(d_out.reshape(N_CHIP, 256, D))]
    (dz, d_ws, d_b, d_vgain, d_sink, d_rel, d_in_t), got = _mixer_bwd(z, dmix, v_gain, ws, b_t, sinks, table, bucket, hn1,
                                                                     _Both(ex, _SiblingExchange(halves)))
    landed["w_ff1"] = got[0]
    small_grads = {
        "gmlp_v_gain": d_vgain, "w_spatial": d_ws.reshape(1, N_GROUP, CHUNK, CHUNK),
        "b_spatial": d_b.reshape(1, N_GROUP, CHUNK), "attn_sinks": d_sink[:, 0].reshape(1, N_HEAD),
        "rel_bias_table": jnp.transpose(d_rel.reshape(N_HEAD, N_BUCKET)), "norm2_gain": d_gain2,
        "final_gain": d_final.reshape(D),
    }
    halves_in = [_halves(d_in_t.reshape(N_CHIP, 448, D))]
    ex = _ChipExchange(*pair_sums(halves + halves_in, list(got[1:]) + list(_sibling_exchange(halves_in))))
    (dx, small_grads["norm1_gain"]), got = _in_bwd_input(dz, w_in_t, x, dh1, gain1, ex)
    landed.update(zip(["w_out", "w_in"], got))
    return dx, landed, small_grads, sq


HBM_SPEC = pl.BlockSpec(memory_space=pltpu.HBM)
VMEM_SPEC = pl.BlockSpec(memory_space=pltpu.VMEM)


def _mesh_place():
    x, y, c = lax.axis_index("x"), lax.axis_index("y"), lax.axis_index("c")
    others = [(1 - x, y), (x, 1 - y), (1 - x, 1 - y)]
    return x, y, c, others


def _remote(src, dst, send_sem, recv_sem, device):
    return pltpu.make_async_remote_copy(src_ref=src, dst_ref=dst, send_sem=send_sem, recv_sem=recv_sem,
                                        device_id=device, device_id_type=MESH)


def _hbm_like(a, shape=None, dtype=None):
    return pltpu.HBM(a.shape if shape is None else shape, a.dtype if dtype is None else dtype)


def _gather_start(bufs, send_sems, recv_sems):
    x, y, c, others = _mesh_place()
    me = 2 * x + y
    for w, buf in enumerate(bufs):
        for k in range(3):
            mine = buf.at[me, c]
            _remote(mine, mine, send_sems.at[w, k], recv_sems.at[w, k], (*others[k], c)).start()


def _gather_finish(bufs, send_sems, recv_sems):
    x, y, c, others = _mesh_place()
    me = 2 * x + y
    sibling = (x, y, 1 - c)
    idx = [2 * ox + oy for ox, oy in others]
    chips = range(3)
    for w, buf in enumerate(bufs):
        for k in chips:
            landed = buf.at[idx[k], c]
            _remote(landed, landed, send_sems.at[w, k], recv_sems.at[w, k], sibling).wait_recv()
            _remote(landed, landed, send_sems.at[w, 3 + k], recv_sems.at[w, 3 + k], sibling).start()
    for w, buf in enumerate(bufs):
        for k in chips:
            landed = buf.at[idx[k], 1 - c]
            _remote(landed, landed, send_sems.at[w, 3 + k], recv_sems.at[w, 3 + k], sibling).wait_recv()
    for w, buf in enumerate(bufs):
        for k in chips:
            mine, passed = buf.at[me, c], buf.at[idx[k], c]
            _remote(mine, mine, send_sems.at[w, k], recv_sems.at[w, k], sibling).wait_send()
            _remote(passed, passed, send_sems.at[w, 3 + k], recv_sems.at[w, 3 + k], sibling).wait_send()


def _gather_sems(n):
    return [pltpu.SemaphoreType.DMA((n, 6)), pltpu.SemaphoreType.DMA((n, 6))]


def _gather_weights(bufs):
    n = len(bufs)

    def body(*refs):
        outs = refs[n:2 * n]
        send_sems, recv_sems = refs[2 * n:]
        _gather_start(outs, send_sems, recv_sems)
        _gather_finish(outs, send_sems, recv_sems)

    return pl.pallas_call(
        body, name="gather_weights",
        in_specs=[HBM_SPEC] * n, out_specs=[HBM_SPEC] * n,
        out_shape=[_hbm_like(b) for b in bufs],
        input_output_aliases={w: w for w in range(n)},
        scratch_shapes=_gather_sems(n),
    )(*bufs)


def _sibling_copies(grads, landing, send_sems, recv_sems):
    x, y, c, _ = _mesh_place()
    return [_remote(grads[w].at[j, 1 - c], landing[w].at[j], send_sems.at[w, j], recv_sems.at[w, j], (x, y, 1 - c))
            for w in range(len(grads)) for j in range(N_CHIP)]


def _sibling_exchange_start(grads, landing, send_sems, recv_sems):
    for cp in _sibling_copies(grads, landing, send_sems, recv_sems):
        cp.start()


def _sibling_exchange_finish(grads, landing, send_sems, recv_sems):
    copies = _sibling_copies(grads, landing, send_sems, recv_sems)
    for cp in copies:
        cp.wait_recv()
    for cp in copies:
        cp.wait_send()


def _sibling_exchange_sems(n):
    return [pltpu.SemaphoreType.DMA((n, N_CHIP)), pltpu.SemaphoreType.DMA((n, N_CHIP))]


def _sibling_exchange(grads):
    n = len(grads)

    def body(*refs):
        ins, outs = refs[:n], refs[n:2 * n]
        _sibling_exchange_start(ins, outs, *refs[2 * n:])
        _sibling_exchange_finish(ins, outs, *refs[2 * n:])

    return pl.pallas_call(
        body, name="sibling_exchange",
        in_specs=[HBM_SPEC] * n, out_specs=[HBM_SPEC] * n,
        out_shape=[_hbm_like(g, (N_CHIP,) + g.shape[2:]) for g in grads],
        scratch_shapes=_sibling_exchange_sems(n),
    )(*[_in_hbm(g) for g in grads])


def _chip_exchange_start(sums, landing, send_sems, recv_sems):
    x, y, c, others = _mesh_place()
    me = 2 * x + y
    for w in range(len(sums)):
        for k, (ox, oy) in enumerate(others):
            _remote(sums[w].at[2 * ox + oy], landing[w].at[me], send_sems.at[w, k], recv_sems.at[w, k],
                    (ox, oy, c)).start()


def _chip_exchange_finish(sums, landing, send_sems, recv_sems):
    x, y, c, others = _mesh_place()
    for w in range(len(sums)):
        for k, (ox, oy) in enumerate(others):
            piece = landing[w].at[2 * ox + oy]
            _remote(piece, piece, send_sems.at[w, k], recv_sems.at[w, k], (x, y, c)).wait_recv()
    for w in range(len(sums)):
        for k, (ox, oy) in enumerate(others):
            piece = sums[w].at[2 * ox + oy]
            _remote(piece, piece, send_sems.at[w, k], recv_sems.at[w, k], (x, y, c)).wait_send()


def _chip_exchange_sems(n):
    return [pltpu.SemaphoreType.DMA((n, 3)), pltpu.SemaphoreType.DMA((n, 3))]


def _sibling_allgather(bufs, also):
    n = len(bufs)
    k_in, k_out = len(also.operands), also.n_out

    def body(*refs):
        ex_ins, refs = refs[n:n + k_in], refs[n + k_in:]
        outs, refs = refs[:n], refs[n:]
        ex_outs, refs = refs[:k_out], refs[k_out:]
        send_sems, recv_sems, ex_sems = refs[0], refs[1], refs[2:]
        x, y, c, _ = _mesh_place()
        sibling = (x, y, 1 - c)
        also.start(ex_ins, ex_outs, ex_sems)
        sends = [_remote(outs[w].at[c], outs[w].at[c], send_sems.at[w], recv_sems.at[w], sibling) for w in range(n)]
        for cp in sends:
            cp.start()
        for w in range(n):
            landed = outs[w].at[1 - c]
            _remote(landed, landed, send_sems.at[w], recv_sems.at[w], sibling).wait_recv()
        for cp in sends:
            cp.wait_send()
        also.finish(ex_ins, ex_outs, ex_sems)

    res = pl.pallas_call(
        body, name="sibling_allgather",
        in_specs=[HBM_SPEC] * (n + k_in), out_specs=[HBM_SPEC] * (n + k_out),
        out_shape=[_hbm_like(b) for b in bufs] + also.out_shape,
        input_output_aliases={**{w: w for w in range(n)}, **{n + i: n + o for i, o in also.aliases.items()}},
        scratch_shapes=[pltpu.SemaphoreType.DMA((n,)), pltpu.SemaphoreType.DMA((n,))] + also.sems,
    )(*bufs, *[_in_hbm(o) for o in also.operands])
    return list(res[:n]), list(res[n:])


def _pair_sum(grad, other, place):
    _, _, h, cols = grad.shape
    tr = _row_tile(h)

    def body(place_ref, g_ref, o_ref, sums_ref, own_ref):
        s = (g_ref[0, 0] + o_ref[0]).astype(BF16)
        sums_ref[0] = s

        @pl.when(pl.program_id(1) == place_ref[0])
        def _():
            own_ref[0] = s

    return pl.pallas_call(
        body, name="pair_sum",
        grid_spec=pltpu.PrefetchScalarGridSpec(
            num_scalar_prefetch=1, grid=(h // tr, N_CHIP),
            in_specs=[pl.BlockSpec((1, 1, tr, cols), lambda r, j, place_ref: (j, place_ref[1], r, 0)),
                      pl.BlockSpec((1, tr, cols), lambda r, j, place_ref: (j, r, 0))],
            out_specs=[pl.BlockSpec((1, tr, cols), lambda r, j, place_ref: (j, r, 0)),
                       pl.BlockSpec((1, tr, cols), lambda r, j, place_ref: (place_ref[0], r, 0))]),
        out_shape=[pltpu.HBM((N_CHIP, h, cols), BF16)] * 2,
        compiler_params=_params(32, 2),
    )(place, _in_hbm(grad), _in_hbm(other))


def _chip_sum(parts, place):
    _, h, cols = parts.shape
    tr = _row_tile(h)

    def body(place_ref, p_ref, out_ref):
        out_ref[0] = ((p_ref[0].astype(F32) + p_ref[1].astype(F32)) + p_ref[2].astype(F32)) + p_ref[3].astype(F32)

    return pl.pallas_call(
        body, name="chip_sum",
        grid_spec=pltpu.PrefetchScalarGridSpec(
            num_scalar_prefetch=1, grid=(h // tr,),
            in_specs=[pl.BlockSpec((N_CHIP, tr, cols), lambda r, place_ref: (0, r, 0))],
            out_specs=pl.BlockSpec((1, tr, cols), lambda r, place_ref: (place_ref[1], r, 0))),
        out_shape=pltpu.HBM((2, h, cols), F32),
        compiler_params=_params(32),
    )(place, _in_hbm(parts))


def _adamw_math(w, g, m, v):
    m = ADAM_B1 * m + (1.0 - ADAM_B1) * g
    v = ADAM_B2 * v + (1.0 - ADAM_B2) * (g * g)
    m_hat = m / (1.0 - ADAM_B1 ** ADAM_STEP)
    v_hat = v / (1.0 - ADAM_B2 ** ADAM_STEP)
    delta = -ADAM_LR * (m_hat / (jnp.sqrt(v_hat) + ADAM_EPS) + ADAM_WD * w)
    return delta, m, v


def _adamw(w, g, m, v, exchange=None):
    rows, cols = w.shape
    tr = _row_tile(rows)

    def body(w_ref, g_ref, m_ref, v_ref, d_ref, nm_ref, nv_ref, g_out_ref):
        g = g_ref[...]
        d_ref[...], nm_ref[...], nv_ref[...] = _adamw_math(w_ref[...], g, m_ref[...], v_ref[...])
        g_out_ref[...] = g

    spec = pl.BlockSpec((tr, cols), lambda r: (r, 0))
    return _call(
        body, (w, g, m, v), grid=(rows // tr,), name="adamw",
        in_specs=[spec] * 4, out_specs=[spec] * 4,
        out_shape=[jax.ShapeDtypeStruct((rows, cols), F32)] * 4,
        compiler_params=_params(48), exchange=exchange)


SMALL_NAMES = ("norm1_gain", "gmlp_v_gain", "w_spatial", "b_spatial", "attn_sinks", "rel_bias_table", "norm2_gain",
               "final_gain")
PACK_TILE = 8 * 128


def _pack_small(arrays):
    parts = []
    for a in arrays:
        flat = a.reshape(-1)
        rows = -(-flat.shape[0] // PACK_TILE) * 8
        parts.append(jnp.pad(flat, (0, rows * 128 - flat.shape[0])).reshape(rows, 128))
    return jnp.concatenate(parts, axis=0)


def _unpack_small(packed, like):
    out, row = [], 0
    for a in like:
        size = math.prod(a.shape)
        rows = -(-size // PACK_TILE) * 8
        out.append(packed[row:row + rows].reshape(-1)[:size].reshape(a.shape))
        row += rows
    return out


def _small_update(gathered, w, m, v):
    rows = gathered.shape[1]

    def body(g_ref, w_ref, m_ref, v_ref, tot_ref, d_ref, nm_ref, nv_ref):
        total = g_ref[0].astype(F32)
        for dev in range(1, 8):
            total = total + g_ref[dev].astype(F32)
        tot_ref[...] = total
        d_ref[...], nm_ref[...], nv_ref[...] = _adamw_math(w_ref[...], total, m_ref[...], v_ref[...])

    return pl.pallas_call(
        body, name="small_update",
        in_specs=[VMEM_SPEC] * 4, out_specs=[VMEM_SPEC] * 4,
        out_shape=[jax.ShapeDtypeStruct((rows, 128), F32)] * 4,
        compiler_params=pltpu.CompilerParams(vmem_limit_bytes=24 * MIB),
    )(gathered, w, m, v)


def _halves(a):
    return a.reshape(a.shape[:-2] + (2, a.shape[-2] // 2, a.shape[-1]))


def _whole(a):
    return a.reshape(a.shape[:-3] + (2 * a.shape[-2], a.shape[-1]))


def kernel(x, p, norm1_gain, w_in, gmlp_v_gain, w_spatial, b_spatial, attn_sinks, rel_bias_table, w_out, norm2_gain, w_ff1, w_ff2, w_ple_proj, w_ple_gate, final_gain, loss_target, m_norm1_gain, m_w_in, m_gmlp_v_gain, m_w_spatial, m_b_spatial, m_attn_sinks, m_rel_bias_table, m_w_out, m_norm2_gain, m_w_ff1, m_w_ff2, m_w_ple_proj, m_w_ple_gate, m_final_gain, v_norm1_gain, v_w_in, v_gmlp_v_gain, v_w_spatial, v_b_spatial, v_attn_sinks, v_rel_bias_table, v_w_out, v_norm2_gain, v_w_ff1, v_w_ff2, v_w_ple_proj, v_w_ple_gate, v_final_gain):
    given = dict(locals())
    small = {n: given[n] for n in SMALL_NAMES}
    chip = 2 * lax.axis_index("x") + lax.axis_index("y")
    place = jnp.stack([chip, lax.axis_index("c")]).astype(jnp.int32)

    big_names = ("w_in", "w_out", "w_ff1", "w_ff2", "w_ple_proj", "w_ple_gate")
    shards = {n: given[n][0] for n in big_names}
    travel = dict(shards, w_in=jnp.transpose(shards["w_in"]))
    bufs = {n: _cast_shard(travel[n], place[:1]) for n in big_names}
    dx, landed, small_grads, sq = _step(x[0], p[0, 0], loss_target[0], small, bufs, place)

    out_grad, out_delta, out_m, out_v = {}, {}, {}, {}

    def update(n, g, exchange=None):
        to = jnp.transpose if n == "w_in" else (lambda a: a)
        (delta, new_m, new_v, g_out), got = _adamw(to(shards[n]), g, to(given["m_" + n][0]), to(given["v_" + n][0]),
                                                   exchange)
        out_grad[n], out_delta[n], out_m[n], out_v[n] = [to(a)[None] for a in (g_out, delta, new_m, new_v)]
        return got

    spare = jnp.zeros((8, 128), F32)
    small_packed = _pack_small([small_grads[n] for n in SMALL_NAMES] + [spare]).astype(BF16)
    reduced, (small_gathered, sq_gathered) = _sibling_allgather(
        [_chip_sum(landed[n], place) for n in big_names], _Both(_GatherAll(small_packed), _GatherAll(sq)))
    for n, r in zip(big_names, reduced):
        update(n, _whole(r))

    like = [given[n] for n in SMALL_NAMES] + [spare]
    packed = _small_update(small_gathered, *[_pack_small([given[pre + n] for n in SMALL_NAMES] + [spare])
                                             for pre in ("", "m_", "v_")])
    for res, out in zip(packed, (out_grad, out_delta, out_m, out_v)):
        out.update(zip(SMALL_NAMES, _unpack_small(res, like)))
    loss = 0.5 * jnp.sum(sq_gathered[:, 0, 0]) / D

    order = ("norm1_gain", "w_in", "gmlp_v_gain", "w_spatial", "b_spatial", "attn_sinks", "rel_bias_table", "w_out",
             "norm2_gain", "w_ff1", "w_ff2", "w_ple_proj", "w_ple_gate", "final_gain")
    return (loss, dx[None], *[out_grad[n] for n in order], *[out_delta[n] for n in order],
            *[out_m[n] for n in order], *[out_v[n] for n in order])
```

```python
import functools
import math

import jax
import jax.numpy as jnp
from jax import lax
from jax.experimental import pallas as pl
from jax.experimental.pallas import tpu as pltpu

S = 2048
D = 1024
D_IN = 1792
D_FF = 4096
PLE = 256
N_CHIP = 4
N_GROUP = 4
CHUNK = 128
N_HEAD = 8
N_BLOCK = S // CHUNK
N_BUCKET = 32
EPS = 1e-6
NEG_INF = -1e30
QK_SCALE = 0.125
GELU_C = math.sqrt(2.0 / math.pi)

ADAM_LR = 0.001
ADAM_B1 = 0.9
ADAM_B2 = 0.999
ADAM_EPS = 1e-08
ADAM_WD = 0.01
ADAM_STEP = 10

F32 = jnp.float32
BF16 = jnp.bfloat16
MIB = 1024 * 1024
MESH = pl.DeviceIdType.MESH

NT = (((1,), (1,)), ((), ()))
TN = (((0,), (0,)), ((), ()))


def _dot(a, b):
    return jnp.dot(a, b, preferred_element_type=F32)


def _dot_nt(a, b):
    return lax.dot_general(a, b, NT, preferred_element_type=F32)


def _dot_tn(a, b):
    return lax.dot_general(a, b, TN, preferred_element_type=F32)


def _params(vmem_mib, n_axes=1):
    return pltpu.CompilerParams(dimension_semantics=("arbitrary",) * n_axes, vmem_limit_bytes=vmem_mib * MIB)


def _rms_scale(v):
    return lax.rsqrt(jnp.mean(v * v, axis=-1, keepdims=True) + EPS)


def _rms_bwd(dy_gain, xhat, r):
    return r * (dy_gain - xhat * jnp.mean(dy_gain * xhat, axis=-1, keepdims=True))


class _Gather:
    def __init__(self, bufs):
        self.operands = list(bufs)
        self.n_out = len(self.operands)
        self.out_shape = [_hbm_like(b) for b in bufs]
        self.aliases = {w: w for w in range(self.n_out)}
        self.sems = _gather_sems(self.n_out)

    def start(self, ins, outs, sems):
        _gather_start(outs, *sems)

    def finish(self, ins, outs, sems):
        _gather_finish(outs, *sems)


class _RelayGather(_Gather):
    TOP, BOTTOM = 6, 7
    DIAGONAL_PASSED = 5

    def __init__(self, bufs):
        super().__init__(bufs)
        self.sems = [pltpu.SemaphoreType.DMA((self.n_out, 8)), pltpu.SemaphoreType.DMA((self.n_out, 8))]

    def _copies(self, bufs, send_sems, recv_sems):
        x, y, c, others = _mesh_place()
        me = 2 * x + y
        idx = [2 * ox + oy for ox, oy in others]
        sibling = (x, y, 1 - c)
        direct, passed, relayed = [], [], []
        for w, buf in enumerate(bufs):
            rows = buf.shape[2] // 2
            upper, lower = pl.ds(0, rows), pl.ds(rows, rows)
            for k in (0, 1):
                mine = buf.at[me, c]
                direct.append((_remote(mine, mine, send_sems.at[w, k], recv_sems.at[w, k], (*others[k], c)),
                               buf.at[idx[k], c], w, k))
            for k in (0, 1, 2):
                here = buf.at[idx[k], c]
                passed.append((_remote(here, here, send_sems.at[w, 3 + k], recv_sems.at[w, 3 + k], sibling),
                               buf.at[idx[k], 1 - c], w, 3 + k))
            from_x, from_y = buf.at[idx[0], c, upper], buf.at[idx[1], c, lower]
            relayed.append((_remote(from_x, from_x, send_sems.at[w, self.TOP], recv_sems.at[w, self.TOP],
                                    (*others[1], c)), buf.at[idx[2], c, upper], w, self.TOP))
            relayed.append((_remote(from_y, from_y, send_sems.at[w, self.BOTTOM], recv_sems.at[w, self.BOTTOM],
                                    (*others[0], c)), buf.at[idx[2], c, lower], w, self.BOTTOM))
        return direct, passed, relayed

    @staticmethod
    def _landed(piece, send_sems, recv_sems, w, col):
        x, y, c, _ = _mesh_place()
        _remote(piece, piece, send_sems.at[w, col], recv_sems.at[w, col], (x, y, c)).wait_recv()

    def start(self, ins, outs, sems):
        for cp, _, _, _ in self._copies(outs, *sems)[0]:
            cp.start()

    def middle(self, ins, outs, sems):
        direct, passed, relayed = self._copies(outs, *sems)
        for _, piece, w, col in direct:
            self._landed(piece, *sems, w, col)
        for cp, _, _, col in passed:
            if col != self.DIAGONAL_PASSED:
                cp.start()
        for cp, _, _, _ in relayed:
            cp.start()

    def finish(self, ins, outs, sems):
        direct, passed, relayed = self._copies(outs, *sems)
        for _, piece, w, col in relayed:
            self._landed(piece, *sems, w, col)
        for cp, _, _, col in passed:
            if col == self.DIAGONAL_PASSED:
                cp.start()
        for _, piece, w, col in passed:
            self._landed(piece, *sems, w, col)
        for cp, _, _, _ in direct + passed + relayed:
            cp.wait_send()


class _ChipExchange:
    def __init__(self, sums, landing):
        self.n_out = len(landing)
        self.operands = list(sums) + list(landing)
        self.out_shape = [_hbm_like(b) for b in landing]
        self.aliases = {self.n_out + w: w for w in range(self.n_out)}
        self.sems = _chip_exchange_sems(self.n_out)

    def start(self, ins, outs, sems):
        _chip_exchange_start(ins[:self.n_out], outs, *sems)

    def finish(self, ins, outs, sems):
        _chip_exchange_finish(ins[:self.n_out], outs, *sems)


class _GatherAll:
    def __init__(self, packed):
        self.operands = [packed]
        self.n_out = 1
        self.out_shape = [_hbm_like(packed, (8,) + packed.shape)]
        self.aliases = {}
        self.sems = [pltpu.SemaphoreType.DMA((8,)), pltpu.SemaphoreType.DMA((8,))]

    def _copies(self, ins, outs, sems):
        x, y, c, _ = _mesh_place()
        me = 4 * x + 2 * y + c
        send_sems, recv_sems = sems
        copies = []
        for k in range(1, 8):
            peer = (1 - x if k // 4 else x, 1 - y if (k // 2) % 2 else y, 1 - c if k % 2 else c)
            src = 4 * peer[0] + 2 * peer[1] + peer[2]
            copies.append((_remote(ins[0], outs[0].at[me], send_sems.at[k], recv_sems.at[k], peer), outs[0].at[src]))
        own = pltpu.make_async_copy(ins[0], outs[0].at[me], send_sems.at[0])
        return own, copies

    def start(self, ins, outs, sems):
        own, copies = self._copies(ins, outs, sems)
        own.start()
        for cp, _ in copies:
            cp.start()

    def finish(self, ins, outs, sems):
        own, copies = self._copies(ins, outs, sems)
        x, y, c, _ = _mesh_place()
        for k, (cp, landed) in enumerate(copies):
            _remote(landed, landed, sems[0].at[k + 1], sems[1].at[k + 1], (x, y, c)).wait_recv()
        for cp, _ in copies:
            cp.wait_send()
        own.wait()


class _Both:
    def __init__(self, a, b):
        self.a, self.b = a, b
        self.operands = a.operands + b.operands
        self.n_out = a.n_out + b.n_out
        self.out_shape = a.out_shape + b.out_shape
        self.aliases = dict(a.aliases)
        self.aliases.update({len(a.operands) + i: a.n_out + o for i, o in b.aliases.items()})
        self.sems = a.sems + b.sems

    def _split(self, ins, outs, sems):
        ka, na, sa = len(self.a.operands), self.a.n_out, len(self.a.sems)
        return (ins[:ka], outs[:na], sems[:sa]), (ins[ka:], outs[na:], sems[sa:])

    def start(self, ins, outs, sems):
        for ex, args in zip((self.a, self.b), self._split(ins, outs, sems)):
            ex.start(*args)

    def finish(self, ins, outs, sems):
        for ex, args in zip((self.a, self.b), self._split(ins, outs, sems)):
            ex.finish(*args)


class _SiblingExchange:
    def __init__(self, grads):
        self.operands = list(grads)
        self.n_out = len(self.operands)
        self.out_shape = [_hbm_like(g, (N_CHIP,) + g.shape[2:]) for g in grads]
        self.aliases = {}
        self.sems = _sibling_exchange_sems(self.n_out)

    def start(self, ins, outs, sems):
        _sibling_exchange_start(ins, outs, *sems)

    def finish(self, ins, outs, sems):
        _sibling_exchange_finish(ins, outs, *sems)


def _call(body, operands, *, grid, in_specs, out_specs, out_shape, name, compiler_params, scratch_shapes=(),
          exchange=None):
    operands = [o if getattr(spec, "memory_space", None) == pltpu.SMEM else _in_hbm(o)
                for o, spec in zip(operands, in_specs)]
    out_shape = [pltpu.HBM(s.shape, s.dtype) for s in out_shape]
    if exchange is None:
        res = pl.pallas_call(body, grid=grid, in_specs=in_specs, out_specs=out_specs, out_shape=out_shape, name=name,
                             scratch_shapes=list(scratch_shapes), compiler_params=compiler_params)(*operands)
        return list(res), []
    n_in, n_out, n_scr = len(in_specs), len(out_specs), len(scratch_shapes)
    k_in, k_out = len(exchange.operands), exchange.n_out

    def fused(*refs):
        ins, refs = refs[:n_in], refs[n_in:]
        ex_ins, refs = refs[:k_in], refs[k_in:]
        outs, refs = refs[:n_out], refs[n_out:]
        ex_outs, refs = refs[:k_out], refs[k_out:]
        scratch, sems = refs[:n_scr], refs[n_scr:]
        ids = [pl.program_id(a) for a in range(len(grid))]
        first = functools.reduce(jnp.logical_and, [i == 0 for i in ids])
        last = functools.reduce(jnp.logical_and, [i == g - 1 for i, g in zip(ids, grid)])

        @pl.when(first)
        def _():
            exchange.start(ex_ins, ex_outs, sems)

        if hasattr(exchange, "middle"):
            steps = math.prod(grid)
            at = (2 * steps) // 3
            place = [(at // math.prod(grid[a + 1:])) % grid[a] for a in range(len(grid))]

            @pl.when(functools.reduce(jnp.logical_and, [i == p for i, p in zip(ids, place)]))
            def _():
                exchange.middle(ex_ins, ex_outs, sems)

        body(*ins, *outs, *scratch)

        @pl.when(last)
        def _():
            exchange.finish(ex_ins, ex_outs, sems)

    res = pl.pallas_call(
        fused, grid=grid, name=name,
        in_specs=list(in_specs) + [HBM_SPEC] * k_in, out_specs=list(out_specs) + [HBM_SPEC] * k_out,
        out_shape=list(out_shape) + exchange.out_shape,
        input_output_aliases={n_in + i: n_out + o for i, o in exchange.aliases.items()},
        scratch_shapes=list(scratch_shapes) + exchange.sems, compiler_params=compiler_params,
    )(*operands, *[_in_hbm(o) for o in exchange.operands])
    return list(res[:n_out]), list(res[n_out:])


def _in_hbm(a):
    return pltpu.with_memory_space_constraint(a, pltpu.HBM)


def _row_tile(h):
    return max(t for t in range(16, 513, 16) if h % t == 0)


def _cast_shard(a, chip):
    rows, cols = a.shape
    h = rows // 2
    tr = _row_tile(h)

    def body(chip_ref, a_ref, o_ref):
        o_ref[0, 0] = a_ref[0].astype(BF16)

    return pl.pallas_call(
        body, name="cast_shard",
        grid_spec=pltpu.PrefetchScalarGridSpec(
            num_scalar_prefetch=1, grid=(2, h // tr),
            in_specs=[pl.BlockSpec((1, tr, cols), lambda s, r, chip_ref: (s, r, 0))],
            out_specs=pl.BlockSpec((1, 1, tr, cols), lambda s, r, chip_ref: (chip_ref[0], s, r, 0))),
        out_shape=pltpu.HBM((N_CHIP, 2, h, cols), BF16),
        compiler_params=_params(16, 2),
    )(chip, _in_hbm(a.reshape(2, h, cols)))


def _in_proj(x, gain1, w_in_t, exchange=None):
    tm = 256

    def body(x_ref, g_ref, w_ref, z_ref, hn_ref):
        xv = x_ref[...]
        hn = (xv * _rms_scale(xv) * g_ref[...]).astype(BF16)
        hn_ref[...] = hn
        z_ref[...] = _dot_nt(hn, w_ref[...])

    return _call(
        body, (x, gain1, w_in_t), grid=(S // tm,), name="in_proj",
        in_specs=[pl.BlockSpec((tm, D), lambda i: (i, 0)), pl.BlockSpec((1, D), lambda i: (0, 0)),
                  pl.BlockSpec((D_IN, D), lambda i: (0, 0))],
        out_specs=[pl.BlockSpec((tm, D_IN), lambda i: (i, 0)), pl.BlockSpec((tm, D), lambda i: (i, 0))],
        out_shape=[jax.ShapeDtypeStruct((S, D_IN), F32), jax.ShapeDtypeStruct((S, D), BF16)],
        compiler_params=_params(40), exchange=exchange)


def _gelu_parts(v):
    t = jnp.tanh(GELU_C * (v + 0.044715 * (v * v * v)))
    cdf = 0.5 * (1.0 + t)
    return cdf, t


def _band_mask(n):
    a = lax.broadcasted_iota(jnp.int32, (CHUNK, 2 * CHUNK), 0)
    j = lax.broadcasted_iota(jnp.int32, (CHUNK, 2 * CHUNK), 1)
    dist = CHUNK + a - j
    valid = (dist >= 0) & (dist < CHUNK)
    return valid & ((n > 0) | (j >= CHUNK))


def _fill_bias(bucket_ref, table_ref, bias_ref):
    bucket = bucket_ref[...]
    for h in range(N_HEAD):
        acc = jnp.zeros((CHUNK, 2 * CHUNK), F32)
        for b in range(N_BUCKET):
            acc = jnp.where(bucket == b, table_ref[b, h], acc)
        bias_ref[h] = acc


def _fill_tril(ws_ref, wt_ref, wtt_ref=None):
    r = lax.broadcasted_iota(jnp.int32, (CHUNK, CHUNK), 0)
    c = lax.broadcasted_iota(jnp.int32, (CHUNK, CHUNK), 1)
    for g in range(N_GROUP):
        w = jnp.where(c <= r, ws_ref[g], 0.0)
        wt_ref[g] = w.astype(BF16)
        if wtt_ref is not None:
            wtt_ref[g] = w.T.astype(BF16)


def _kv_layouts(kv_prev, kv_cur):
    both = jnp.concatenate([kv_prev, kv_cur], axis=0)
    k = both[:, :128]
    v = both[:, 128:]
    return (k.astype(BF16), pltpu.roll(k, 64, axis=1).astype(BF16),
            v.astype(BF16), pltpu.roll(v, 64, axis=1).astype(BF16))


def _head_place(h):
    pair, pos, kvh = h // 2, h % 2, h // 4
    return pair, pos, kvh == pos


def _softmax_sink(qm, k_use, bias_h, sink, valid):
    s = _dot_nt(qm, k_use) * QK_SCALE + bias_h
    s = jnp.where(valid, s, NEG_INF)
    m = jnp.maximum(jnp.max(s, axis=-1, keepdims=True), sink)
    e = jnp.exp(s - m)
    es = jnp.exp(sink - m)
    denom = jnp.sum(e, axis=-1, keepdims=True) + es
    return e / denom, es / denom


def _mixer_fwd(z, v_gain, w_spatial, b_spatial_t, sinks, rel_table, bucket, exchange=None):
    def body(z_ref, kvp_ref, gain_ref, ws_ref, bt_ref, sink_ref, table_ref, bucket_ref, out_ref, bias_ref, wt_ref):
        n = pl.program_id(0)

        @pl.when(n == 0)
        def _():
            _fill_bias(bucket_ref, table_ref, bias_ref)
            _fill_tril(ws_ref, wt_ref)

        zuv = z_ref[:, :1024]
        cdf, _ = _gelu_parts(zuv)
        guv = zuv * cdf
        for g in range(N_GROUP):
            vg = guv[:, 512 + 128 * g:512 + 128 * (g + 1)]
            vn = vg * _rms_scale(vg) * gain_ref[:, 128 * g:128 * (g + 1)]
            sv = _dot(wt_ref[g], vn.astype(BF16)) + bt_ref[:, g:g + 1]
            out_ref[:, 128 * g:128 * (g + 1)] = (guv[:, 128 * g:128 * (g + 1)] * sv).astype(BF16)

        k_same, k_swap, v_same, v_swap = _kv_layouts(kvp_ref[...], z_ref[:, 1536:1792])
        valid = _band_mask(n)
        lane_half = lax.broadcasted_iota(jnp.int32, (1, 128), 1) // 64
        for pair in range(N_HEAD // 2):
            qq = z_ref[:, 1024 + 128 * pair:1024 + 128 * (pair + 1)]
            acc = jnp.zeros((CHUNK, 128), F32)
            for pos in range(2):
                h = 2 * pair + pos
                _, _, same = _head_place(h)
                qm = jnp.where(lane_half == pos, qq, 0.0).astype(BF16)
                p, _ = _softmax_sink(qm, k_same if same else k_swap, bias_ref[h], sink_ref[h], valid)
                vm = jnp.where(lane_half == pos, v_same if same else v_swap, jnp.zeros((), BF16))
                acc = acc + _dot(p.astype(BF16), vm)
            out_ref[:, 512 + 128 * pair:512 + 128 * (pair + 1)] = acc.astype(BF16)

    return _call(
        body, (z, z, v_gain, w_spatial, b_spatial_t, sinks, rel_table, bucket), grid=(N_BLOCK,), name="mixer_fwd",
        in_specs=[pl.BlockSpec((CHUNK, D_IN), lambda n: (n, 0)),
                  pl.BlockSpec((CHUNK, 256), lambda n: (jnp.maximum(n - 1, 0), 6)),
                  pl.BlockSpec((1, 512), lambda n: (0, 0)),
                  pl.BlockSpec((N_GROUP, CHUNK, CHUNK), lambda n: (0, 0, 0)),
                  pl.BlockSpec((CHUNK, N_GROUP), lambda n: (0, 0)),
                  pl.BlockSpec(memory_space=pltpu.SMEM),
                  pl.BlockSpec(memory_space=pltpu.SMEM),
                  pl.BlockSpec((CHUNK, 2 * CHUNK), lambda n: (0, 0))],
        out_specs=[pl.BlockSpec((CHUNK, D), lambda n: (n, 0))],
        out_shape=[jax.ShapeDtypeStruct((S, D), BF16)],
        scratch_shapes=[pltpu.VMEM((N_HEAD, CHUNK, 2 * CHUNK), F32), pltpu.VMEM((N_GROUP, CHUNK, CHUNK), BF16)],
        compiler_params=_params(32), exchange=exchange)


def _out_proj(x, mix, w_out, gain2, exchange=None):
    tm = 256

    def body(x_ref, mix_ref, w_ref, g_ref, h1_ref, hn_ref, hnt_ref):
        h1 = x_ref[...] + _dot(mix_ref[...], w_ref[...])
        h1_ref[...] = h1
        hn = h1 * _rms_scale(h1) * g_ref[...]
        hn_ref[...] = hn.astype(BF16)
        hnt_ref[...] = hn.T.astype(BF16)

    return _call(
        body, (x, mix, w_out, gain2), grid=(S // tm,), name="out_proj",
        in_specs=[pl.BlockSpec((tm, D), lambda i: (i, 0)), pl.BlockSpec((tm, D), lambda i: (i, 0)),
                  pl.BlockSpec((D, D), lambda i: (0, 0)), pl.BlockSpec((1, D), lambda i: (0, 0))],
        out_specs=[pl.BlockSpec((tm, D), lambda i: (i, 0)), pl.BlockSpec((tm, D), lambda i: (i, 0)),
                   pl.BlockSpec((D, tm), lambda i: (0, i))],
        out_shape=[jax.ShapeDtypeStruct((S, D), F32), jax.ShapeDtypeStruct((S, D), BF16),
                   jax.ShapeDtypeStruct((D, S), BF16)],
        compiler_params=_params(32), exchange=exchange)


def _ffn_up(hn2, w_ff1, exchange=None):
    tm = 512
    nj = D_FF // 1024

    def body(hn_ref, w1_ref, r_ref, a_ref, at_ref):
        r = jnp.maximum(_dot(hn_ref[...], w1_ref[0]), 0.0)
        r_ref[...] = r.astype(BF16)
        a = r * r
        a_ref[...] = a.astype(BF16)
        at_ref[...] = a.T.astype(BF16)

    return _call(
        body, (hn2, w_ff1), grid=(nj, S // tm), name="ffn_up",
        in_specs=[pl.BlockSpec((tm, D), lambda j, i: (i, 0)), pl.BlockSpec((1, D, 1024), lambda j, i: (j, 0, 0))],
        out_specs=[pl.BlockSpec((tm, 1024), lambda j, i: (i, j)), pl.BlockSpec((tm, 1024), lambda j, i: (i, j)),
                   pl.BlockSpec((1024, tm), lambda j, i: (j, i))],
        out_shape=[jax.ShapeDtypeStruct((S, D_FF), BF16), jax.ShapeDtypeStruct((S, D_FF), BF16),
                   jax.ShapeDtypeStruct((D_FF, S), BF16)],
        compiler_params=_params(40, 2), exchange=exchange)


def _ffn_down(h1, a, w_ff2, exchange=None):
    tm = 1024
    nj = D_FF // 1024

    def body(h1_ref, a_ref, w2_ref, h2_ref, acc_ref):
        j = pl.program_id(1)
        part = _dot(a_ref[...], w2_ref[0])

        @pl.when(j == 0)
        def _():
            acc_ref[...] = part

        @pl.when(j > 0)
        def _():
            acc_ref[...] += part

        @pl.when(j == nj - 1)
        def _():
            h2_ref[...] = h1_ref[...] + acc_ref[...]

    return _call(
        body, (h1, a, w_ff2), grid=(S // tm, nj), name="ffn_down",
        in_specs=[pl.BlockSpec((tm, D), lambda i, j: (i, 0)), pl.BlockSpec((tm, 1024), lambda i, j: (i, j)),
                  pl.BlockSpec((1, 1024, D), lambda i, j: (j, 0, 0))],
        out_specs=[pl.BlockSpec((tm, D), lambda i, j: (i, 0))],
        out_shape=[jax.ShapeDtypeStruct((S, D), F32)],
        scratch_shapes=[pltpu.VMEM((tm, D), F32)],
        compiler_params=_params(48, 2), exchange=exchange)


def _tail(h2, p, target, w_gate, w_proj, final_gain):
    tm = 256
    steps = S // tm

    def body(h2_ref, p_ref, t_ref, wg_ref, wp_ref, gf_ref, dh2_ref, dwg_ref, dwp_ref, dgf_ref, loss_ref, dh2b_ref,
             dwp_acc):
        i = pl.program_id(0)
        h2 = h2_ref[...]
        h2b = h2.astype(BF16)
        pb = p_ref[...].astype(BF16)
        gate = jax.nn.sigmoid(_dot(h2b, wg_ref[...]))
        pp = jnp.concatenate([_dot(pb, wp_ref[j]) for j in range(N_CHIP)], axis=1)
        h3 = h2 + gate * pp
        r3 = _rms_scale(h3)
        xhat = h3 * r3
        gf = gf_ref[...]
        err = xhat * gf - t_ref[...]
        dy = err * (1.0 / D)
        dh3 = _rms_bwd(dy * gf, xhat, r3)
        dgp = (dh3 * pp * gate * (1.0 - gate)).astype(BF16)
        dpp = (dh3 * gate).astype(BF16)
        dh2 = dh3 + _dot_nt(dgp, wg_ref[...])
        dh2_ref[...] = dh2
        dh2b_ref[...] = dh2.astype(BF16)
        dwg = _dot_tn(h2b, dgp)
        dwp = _dot_tn(pb, dpp)
        dgf = jnp.sum(dy * xhat, axis=0, keepdims=True)
        sq = jnp.sum(jnp.sum(err * err, axis=1, keepdims=True), axis=0, keepdims=True)

        @pl.when(i == 0)
        def _():
            dwg_ref[...] = dwg
            dwp_acc[...] = dwp
            dgf_ref[...] = dgf
            loss_ref[...] = jnp.broadcast_to(sq, (8, 128))

        @pl.when(i > 0)
        def _():
            dwg_ref[...] += dwg
            dwp_acc[...] += dwp
            dgf_ref[...] += dgf
            loss_ref[...] += jnp.broadcast_to(sq, (8, 128))

        @pl.when(i == steps - 1)
        def _():
            for j in range(N_CHIP):
                dwp_ref[j] = dwp_acc[:, 256 * j:256 * (j + 1)]

    return _call(
        body, (h2, p, target, w_gate, w_proj, final_gain), grid=(steps,), name="tail",
        in_specs=[pl.BlockSpec((tm, D), lambda i: (i, 0)), pl.BlockSpec((tm, PLE), lambda i: (i, 0)),
                  pl.BlockSpec((tm, D), lambda i: (i, 0)), pl.BlockSpec((D, D), lambda i: (0, 0)),
                  pl.BlockSpec((N_CHIP, PLE, 256), lambda i: (0, 0, 0)), pl.BlockSpec((1, D), lambda i: (0, 0))],
        out_specs=[pl.BlockSpec((tm, D), lambda i: (i, 0)), pl.BlockSpec((D, D), lambda i: (0, 0)),
                   pl.BlockSpec((N_CHIP, PLE, 256), lambda i: (0, 0, 0)), pl.BlockSpec((1, D), lambda i: (0, 0)),
                   pl.BlockSpec((8, 128), lambda i: (0, 0)), pl.BlockSpec((tm, D), lambda i: (i, 0))],
        out_shape=[jax.ShapeDtypeStruct((S, D), F32), jax.ShapeDtypeStruct((D, D), F32),
                   jax.ShapeDtypeStruct((N_CHIP, PLE, 256), F32), jax.ShapeDtypeStruct((1, D), F32),
                   jax.ShapeDtypeStruct((8, 128), F32), jax.ShapeDtypeStruct((S, D), BF16)],
        scratch_shapes=[pltpu.VMEM((PLE, D), F32)],
        compiler_params=_params(48))[0]


def _ffn_bwd_down(dh2b, r, a_t, w_ff2, exchange=None):
    tm = 1024
    nj = D_FF // 1024

    def body(dh2_ref, r_ref, at_ref, w2_ref, df_ref, dw2_ref):
        i = pl.program_id(1)
        dh2b = dh2_ref[...]
        da = _dot_nt(dh2b, w2_ref[0])
        df_ref[...] = (da * (2.0 * r_ref[...].astype(F32))).astype(BF16)
        dw2 = _dot(at_ref[...], dh2b)

        @pl.when(i == 0)
        def _():
            dw2_ref[0] = dw2

        @pl.when(i > 0)
        def _():
            dw2_ref[0] += dw2

    return _call(
        body, (dh2b, r, a_t, w_ff2), grid=(nj, S // tm), name="ffn_bwd_down",
        in_specs=[pl.BlockSpec((tm, D), lambda j, i: (i, 0)), pl.BlockSpec((tm, 1024), lambda j, i: (i, j)),
                  pl.BlockSpec((1024, tm), lambda j, i: (j, i)), pl.BlockSpec((1, 1024, D), lambda j, i: (j, 0, 0))],
        out_specs=[pl.BlockSpec((tm, 1024), lambda j, i: (i, j)), pl.BlockSpec((1, 1024, D), lambda j, i: (j, 0, 0))],
        out_shape=[jax.ShapeDtypeStruct((S, D_FF), BF16), jax.ShapeDtypeStruct((nj, 1024, D), F32)],
        compiler_params=_params(48, 2), exchange=exchange)


def _ffn_bwd_up(df, hn2_t, exchange=None):
    tm = 1024
    nj = D_FF // 1024

    def body(df_ref, hnt_ref, dw1_ref):
        i = pl.program_id(1)
        dw1 = _dot(hnt_ref[...], df_ref[...])

        @pl.when(i == 0)
        def _():
            dw1_ref[0] = dw1

        @pl.when(i > 0)
        def _():
            dw1_ref[0] += dw1

    return _call(
        body, (df, hn2_t), grid=(nj, S // tm), name="ffn_bwd_up",
        in_specs=[pl.BlockSpec((tm, 1024), lambda j, i: (i, j)), pl.BlockSpec((D, tm), lambda j, i: (0, i))],
        out_specs=[pl.BlockSpec((1, D, 1024), lambda j, i: (j, 0, 0))],
        out_shape=[jax.ShapeDtypeStruct((nj, D, 1024), F32)],
        compiler_params=_params(40, 2), exchange=exchange)


def _ffn_bwd_input(df, w_ff1, dh2, h1, gain2, mix, w_out, exchange=None):
    tm = 512
    nj = D_FF // 1024
    steps = S // tm

    def body(df_ref, w1_ref, dh2_ref, h1_ref, g_ref, mix_ref, wo_ref, dh1_ref, dmix_ref, dwo_ref, dg_ref, acc_ref):
        i = pl.program_id(0)
        j = pl.program_id(1)
        part = _dot_nt(df_ref[...], w1_ref[0])

        @pl.when(j == 0)
        def _():
            acc_ref[...] = part

        @pl.when(j > 0)
        def _():
            acc_ref[...] += part

        @pl.when(j == nj - 1)
        def _():
            dhn = acc_ref[...]
            h1 = h1_ref[...]
            r2 = _rms_scale(h1)
            xhat = h1 * r2
            dh1 = dh2_ref[...] + _rms_bwd(dhn * g_ref[...], xhat, r2)
            dh1_ref[...] = dh1
            dh1b = dh1.astype(BF16)
            dmix_ref[...] = _dot_nt(dh1b, wo_ref[...])
            dwo = _dot_tn(mix_ref[...], dh1b)
            dg = jnp.sum(dhn * xhat, axis=0, keepdims=True)

            @pl.when(i == 0)
            def _():
                dwo_ref[...] = dwo
                dg_ref[...] = dg

            @pl.when(i > 0)
            def _():
                dwo_ref[...] += dwo
                dg_ref[...] += dg

    return _call(
        body, (df, w_ff1, dh2, h1, gain2, mix, w_out), grid=(steps, nj), name="ffn_bwd_input",
        in_specs=[pl.BlockSpec((tm, 1024), lambda i, j: (i, j)), pl.BlockSpec((1, D, 1024), lambda i, j: (j, 0, 0)),
                  pl.BlockSpec((tm, D), lambda i, j: (i, 0)), pl.BlockSpec((tm, D), lambda i, j: (i, 0)),
                  pl.BlockSpec((1, D), lambda i, j: (0, 0)), pl.BlockSpec((tm, D), lambda i, j: (i, 0)),
                  pl.BlockSpec((D, D), lambda i, j: (0, 0))],
        out_specs=[pl.BlockSpec((tm, D), lambda i, j: (i, 0)), pl.BlockSpec((tm, D), lambda i, j: (i, 0)),
                   pl.BlockSpec((D, D), lambda i, j: (0, 0)), pl.BlockSpec((1, D), lambda i, j: (0, 0))],
        out_shape=[jax.ShapeDtypeStruct((S, D), F32), jax.ShapeDtypeStruct((S, D), F32),
                   jax.ShapeDtypeStruct((D, D), F32), jax.ShapeDtypeStruct((1, D), F32)],
        scratch_shapes=[pltpu.VMEM((tm, D), F32)],
        compiler_params=_params(56, 2), exchange=exchange)


IN_GROUP = 4


def _mixer_bwd(z, dmix, v_gain, w_spatial, b_spatial_t, sinks, rel_table, bucket, hn1, exchange=None):
    def body(z_ref, kvp_ref, dm_ref, gain_ref, ws_ref, bt_ref, sink_ref, table_ref, bucket_ref, hn_ref,
             dz_ref, dws_ref, db_ref, dgain_ref, dsink_ref, drel_ref, dwin_ref,
             bias_ref, wt_ref, wtt_ref, dbias_ref, dsv_ref, carry_ref):
        n = pl.program_id(0)

        @pl.when(n == 0)
        def _():
            _fill_bias(bucket_ref, table_ref, bias_ref)
            _fill_tril(ws_ref, wt_ref, wtt_ref)
            dwin_ref[...] = jnp.zeros_like(dwin_ref)
            dbias_ref[...] = jnp.zeros_like(dbias_ref)
            dsv_ref[...] = jnp.zeros_like(dsv_ref)
            dws_ref[...] = jnp.zeros_like(dws_ref)
            dgain_ref[...] = jnp.zeros_like(dgain_ref)
            dsink_ref[...] = jnp.zeros_like(dsink_ref)

        rows = pl.ds(pl.multiple_of(n * CHUNK, CHUNK), CHUNK)

        zuv = z_ref[:, :1024]
        cdf, t = _gelu_parts(zuv)
        guv = zuv * cdf
        dgelu = cdf + zuv * (0.5 * (1.0 - t * t)) * (GELU_C * (1.0 + 3.0 * 0.044715 * (zuv * zuv)))
        for g in range(N_GROUP):
            lo, hi = 128 * g, 128 * (g + 1)
            u = guv[:, lo:hi]
            vg = guv[:, 512 + lo:512 + hi]
            rr = _rms_scale(vg)
            vhat = vg * rr
            gain = gain_ref[:, lo:hi]
            vnb = (vhat * gain).astype(BF16)
            sv = _dot(wt_ref[g], vnb) + bt_ref[:, g:g + 1]
            da = dm_ref[:, lo:hi]
            dsv = da * u
            dsvb = dsv.astype(BF16)
            dsv_ref[g] += dsv
            dws_ref[g] += _dot_nt(dsvb, vnb)
            dvn = _dot(wtt_ref[g], dsvb)
            dgain_ref[:, lo:hi] += jnp.sum(dvn * vhat, axis=0, keepdims=True)
            dvg = _rms_bwd(dvn * gain, vhat, rr)
            dz_ref[rows, lo:hi] = (da * sv * dgelu[:, lo:hi]).astype(BF16)
            dz_ref[rows, 512 + lo:512 + hi] = (dvg * dgelu[:, 512 + lo:512 + hi]).astype(BF16)

        k_same, k_swap, v_same, v_swap = _kv_layouts(kvp_ref[...], z_ref[:, 1536:1792])
        valid = _band_mask(n)
        lane_half = lax.broadcasted_iota(jnp.int32, (1, 128), 1) // 64
        zero = jnp.zeros((2 * CHUNK, 128), F32)
        dk_same, dk_swap, dv_same, dv_swap = zero, zero, zero, zero
        for pair in range(N_HEAD // 2):
            cols = slice(1024 + 128 * pair, 1024 + 128 * (pair + 1))
            qq = z_ref[:, cols]
            do_pair = dm_ref[:, 512 + 128 * pair:512 + 128 * (pair + 1)]
            dq = jnp.zeros((CHUNK, 128), F32)
            for pos in range(2):
                h = 2 * pair + pos
                _, _, same = _head_place(h)
                on_half = lane_half == pos
                qm = jnp.where(on_half, qq, 0.0).astype(BF16)
                k_use = k_same if same else k_swap
                v_use = v_same if same else v_swap
                p, p_sink = _softmax_sink(qm, k_use, bias_ref[h], sink_ref[h], valid)
                dom = jnp.where(on_half, do_pair, 0.0).astype(BF16)
                dp = _dot_nt(dom, v_use)
                dsum = jnp.sum(p * dp, axis=-1, keepdims=True)
                ds = p * (dp - dsum)
                dbias_ref[h] += ds
                dsink_ref[h:h + 1, :] += jnp.broadcast_to(jnp.sum(-p_sink * dsum, axis=0, keepdims=True), (1, 128))
                dsb = ds.astype(BF16)
                dq = dq + jnp.where(on_half, _dot(dsb, k_use), 0.0)
                dk_h = _dot_tn(dsb, qm)
                dv_h = _dot_tn(p.astype(BF16), dom)
                if same:
                    dk_same, dv_same = dk_same + dk_h, dv_same + dv_h
                else:
                    dk_swap, dv_swap = dk_swap + dk_h, dv_swap + dv_h
            dz_ref[rows, cols] = (dq * QK_SCALE).astype(BF16)
        dk = (dk_same + pltpu.roll(dk_swap, 64, axis=1)) * QK_SCALE
        dv = dv_same + pltpu.roll(dv_swap, 64, axis=1)
        dkv = jnp.concatenate([dk, dv], axis=1)

        @pl.when(n > 0)
        def _():
            prev_rows = pl.ds(pl.multiple_of((n - 1) * CHUNK, CHUNK), CHUNK)
            dz_ref[prev_rows, 1536:1792] = (carry_ref[...] + dkv[:CHUNK]).astype(BF16)

        carry_ref[...] = dkv[CHUNK:]

        @pl.when((n > 0) & (n % IN_GROUP == 0))
        def _():
            done = pl.ds(pl.multiple_of((n - IN_GROUP) * CHUNK, IN_GROUP * CHUNK), IN_GROUP * CHUNK)
            dwin_ref[...] += _dot_tn(dz_ref[done, :], hn_ref[...])

        @pl.when(n == N_BLOCK - 1)
        def _():
            dz_ref[rows, 1536:1792] = dkv[CHUNK:].astype(BF16)
            last = pl.ds((N_BLOCK - IN_GROUP) * CHUNK, IN_GROUP * CHUNK)
            dwin_ref[...] += _dot_tn(dz_ref[last, :], hn_ref[...])
            r = lax.broadcasted_iota(jnp.int32, (CHUNK, CHUNK), 0)
            c = lax.broadcasted_iota(jnp.int32, (CHUNK, CHUNK), 1)
            for g in range(N_GROUP):
                dws_ref[g] = jnp.where(c <= r, dws_ref[g], 0.0)
                db_ref[g] = jnp.sum(dsv_ref[g], axis=1, keepdims=True)
            bucket = bucket_ref[...]
            for h in range(N_HEAD):
                dbh = dbias_ref[h]
                per_bucket = [jnp.sum(jnp.where(bucket == b, dbh, 0.0), axis=0, keepdims=True) for b in range(N_BUCKET)]
                drel_ref[h] = jnp.sum(jnp.concatenate(per_bucket, axis=0), axis=1, keepdims=True)

    def hn_group(n):
        return jnp.where(n == N_BLOCK - 1, N_BLOCK // IN_GROUP - 1, jnp.maximum(n // IN_GROUP - 1, 0))

    return _call(
        body, (z, z, dmix, v_gain, w_spatial, b_spatial_t, sinks, rel_table, bucket, hn1), grid=(N_BLOCK,),
        name="mixer_bwd",
        in_specs=[pl.BlockSpec((CHUNK, D_IN), lambda n: (n, 0)),
                  pl.BlockSpec((CHUNK, 256), lambda n: (jnp.maximum(n - 1, 0), 6)),
                  pl.BlockSpec((CHUNK, D), lambda n: (n, 0)),
                  pl.BlockSpec((1, 512), lambda n: (0, 0)),
                  pl.BlockSpec((N_GROUP, CHUNK, CHUNK), lambda n: (0, 0, 0)),
                  pl.BlockSpec((CHUNK, N_GROUP), lambda n: (0, 0)),
                  pl.BlockSpec(memory_space=pltpu.SMEM),
                  pl.BlockSpec(memory_space=pltpu.SMEM),
                  pl.BlockSpec((CHUNK, 2 * CHUNK), lambda n: (0, 0)),
                  pl.BlockSpec((IN_GROUP * CHUNK, D), lambda n: (hn_group(n), 0))],
        out_specs=[pl.BlockSpec((S, D_IN), lambda n: (0, 0)),
                   pl.BlockSpec((N_GROUP, CHUNK, CHUNK), lambda n: (0, 0, 0)),
                   pl.BlockSpec((N_GROUP, CHUNK, 1), lambda n: (0, 0, 0)),
                   pl.BlockSpec((1, 512), lambda n: (0, 0)),
                   pl.BlockSpec((N_HEAD, 128), lambda n: (0, 0)),
                   pl.BlockSpec((N_HEAD, N_BUCKET, 1), lambda n: (0, 0, 0)),
                   pl.BlockSpec((D_IN, D), lambda n: (0, 0))],
        out_shape=[jax.ShapeDtypeStruct((S, D_IN), BF16), jax.ShapeDtypeStruct((N_GROUP, CHUNK, CHUNK), F32),
                   jax.ShapeDtypeStruct((N_GROUP, CHUNK, 1), F32), jax.ShapeDtypeStruct((1, 512), F32),
                   jax.ShapeDtypeStruct((N_HEAD, 128), F32), jax.ShapeDtypeStruct((N_HEAD, N_BUCKET, 1), F32),
                   jax.ShapeDtypeStruct((D_IN, D), F32)],
        scratch_shapes=[pltpu.VMEM((N_HEAD, CHUNK, 2 * CHUNK), F32), pltpu.VMEM((N_GROUP, CHUNK, CHUNK), BF16),
                        pltpu.VMEM((N_GROUP, CHUNK, CHUNK), BF16), pltpu.VMEM((N_HEAD, CHUNK, 2 * CHUNK), F32),
                        pltpu.VMEM((N_GROUP, CHUNK, CHUNK), F32), pltpu.VMEM((CHUNK, 256), F32)],
        compiler_params=_params(48), exchange=exchange)


def _in_bwd_input(dz, w_in_t, x, dh1, gain1, exchange=None):
    tm = 512

    def body(dz_ref, w_ref, x_ref, dh1_ref, g_ref, dx_ref, dg_ref):
        i = pl.program_id(0)
        dhn = _dot(dz_ref[...], w_ref[...])
        xv = x_ref[...]
        r1 = _rms_scale(xv)
        xhat = xv * r1
        dx_ref[...] = dh1_ref[...] + _rms_bwd(dhn * g_ref[...], xhat, r1)
        dg = jnp.sum(dhn * xhat, axis=0, keepdims=True)

        @pl.when(i == 0)
        def _():
            dg_ref[...] = dg

        @pl.when(i > 0)
        def _():
            dg_ref[...] += dg

    return _call(
        body, (dz, w_in_t, x, dh1, gain1), grid=(S // tm,), name="in_bwd_input",
        in_specs=[pl.BlockSpec((tm, D_IN), lambda i: (i, 0)), pl.BlockSpec((D_IN, D), lambda i: (0, 0)),
                  pl.BlockSpec((tm, D), lambda i: (i, 0)), pl.BlockSpec((tm, D), lambda i: (i, 0)),
                  pl.BlockSpec((1, D), lambda i: (0, 0))],
        out_specs=[pl.BlockSpec((tm, D), lambda i: (i, 0)), pl.BlockSpec((1, D), lambda i: (0, 0))],
        out_shape=[jax.ShapeDtypeStruct((S, D), F32), jax.ShapeDtypeStruct((1, D), F32)],
        compiler_params=_params(48), exchange=exchange)


def _rel_bucket():
    a = jnp.arange(CHUNK)[:, None]
    j = jnp.arange(2 * CHUNK)[None, :]
    n = jnp.maximum(CHUNK + a - j, 0)
    max_exact = N_BUCKET // 2
    nf = jnp.maximum(n, 1).astype(jnp.float32)
    large = max_exact + (jnp.log(nf / max_exact) / math.log(CHUNK / max_exact) * (N_BUCKET - max_exact)).astype(jnp.int32)
    large = jnp.minimum(large, N_BUCKET - 1)
    return jnp.where(n < max_exact, n, large).astype(jnp.int32)


def _step(x, p, target, small, bufs, place):
    bucket = _rel_bucket()
    sinks = small["attn_sinks"].reshape(N_HEAD)
    b_t = jnp.transpose(small["b_spatial"].reshape(N_GROUP, CHUNK))
    ws = small["w_spatial"].reshape(N_GROUP, CHUNK, CHUNK)
    gain1, gain2 = small["norm1_gain"], small["norm2_gain"]
    v_gain = small["gmlp_v_gain"]
    final_gain = small["final_gain"].reshape(1, D)
    table = small["rel_bias_table"]
    bufs = dict(bufs)

    def gather(*names):
        return _Gather([bufs[n] for n in names])

    def took(names, got):
        bufs.update(zip(names, got))

    took(["w_in"], _gather_weights([bufs["w_in"]]))
    w_in_t = _whole(bufs["w_in"]).reshape(D_IN, D)
    (z, hn1), got = _in_proj(x, gain1, w_in_t, gather("w_out"))
    took(["w_out"], got)
    (mix,), got = _mixer_fwd(z, v_gain, ws, b_t, sinks, table, bucket, _RelayGather([bufs["w_ff1"]]))
    took(["w_ff1"], got)
    w_out = _whole(bufs["w_out"]).reshape(D, D)
    (h1, hn2, hn2_t), _ = _out_proj(x, mix, w_out, gain2)
    w_ff1 = _whole(bufs["w_ff1"])
    (r, a, a_t), got = _ffn_up(hn2, w_ff1, _RelayGather([bufs["w_ff2"]]))
    took(["w_ff2"], got)
    w_ff2 = _whole(bufs["w_ff2"])
    (h2,), got = _ffn_down(h1, a, w_ff2, gather("w_ple_gate", "w_ple_proj"))
    took(["w_ple_gate", "w_ple_proj"], got)
    dh2, d_gate, d_proj, d_final, sq, dh2b = _tail(h2, p, target, _whole(bufs["w_ple_gate"]).reshape(D, D),
                                                   _whole(bufs["w_ple_proj"]), final_gain)

    def pair_sums(halves, from_sibling):
        sums, landing = zip(*[_pair_sum(g, o, place) for g, o in zip(halves, from_sibling)])
        return list(sums), list(landing)

    landed = {}
    halves = [_halves(d_gate.reshape(N_CHIP, 256, D)), _halves(d_proj)]
    (df, d_ff2), got = _ffn_bwd_down(dh2b, r, a_t, w_ff2, _SiblingExchange(halves))
    ex, halves = _ChipExchange(*pair_sums(halves, got)), [_halves(d_ff2)]
    (d_ff1,), got = _ffn_bwd_up(df, hn2_t, _Both(ex, _SiblingExchange(halves)))
    landed.update(zip(["w_ple_gate", "w_ple_proj"], got[:2]))
    ex, halves = _ChipExchange(*pair_sums(halves, got[2:])), [_halves(d_ff1)]
    (dh1, dmix, d_out, d_gain2), got = _ffn_bwd_input(df, w_ff1, dh2, h1, gain2, mix, w_out,
                                                      _Both(ex, _SiblingExchange(halves)))
    landed["w_ff2"] = got[0]
    ex, halves = _ChipExchange(*pair_sums(halves, got[1:])), [_halves(d_out.reshape(N_CHIP, 256, D))]
    (dz, d_ws, d_b, d_vgain, d_sink, d_rel, d_in_t), got = _mixer_bwd(z, dmix, v_gain, ws, b_t, sinks, table, bucket, hn1,
                                                                     _Both(ex, _SiblingExchange(halves)))
    landed["w_ff1"] = got[0]
    small_grads = {
        "gmlp_v_gain": d_vgain, "w_spatial": d_ws.reshape(1, N_GROUP, CHUNK, CHUNK),
        "b_spatial": d_b.reshape(1, N_GROUP, CHUNK), "attn_sinks": d_sink[:, 0].reshape(1, N_HEAD),
        "rel_bias_table": jnp.transpose(d_rel.reshape(N_HEAD, N_BUCKET)), "norm2_gain": d_gain2,
        "final_gain": d_final.reshape(D),
    }
    halves_in = [_halves(d_in_t.reshape(N_CHIP, 448, D))]
    ex = _ChipExchange(*pair_sums(halves + halves_in, list(got[1:]) + list(_sibling_exchange(halves_in))))
    (dx, small_grads["norm1_gain"]), got = _in_bwd_input(dz, w_in_t, x, dh1, gain1, ex)
    landed.update(zip(["w_out", "w_in"], got))
    return dx, landed, small_grads, sq


HBM_SPEC = pl.BlockSpec(memory_space=pltpu.HBM)
VMEM_SPEC = pl.BlockSpec(memory_space=pltpu.VMEM)


def _mesh_place():
    x, y, c = lax.axis_index("x"), lax.axis_index("y"), lax.axis_index("c")
    others = [(1 - x, y), (x, 1 - y), (1 - x, 1 - y)]
    return x, y, c, others


def _remote(src, dst, send_sem, recv_sem, device):
    return pltpu.make_async_remote_copy(src_ref=src, dst_ref=dst, send_sem=send_sem, recv_sem=recv_sem,
                                        device_id=device, device_id_type=MESH)


def _hbm_like(a, shape=None, dtype=None):
    return pltpu.HBM(a.shape if shape is None else shape, a.dtype if dtype is None else dtype)


def _gather_start(bufs, send_sems, recv_sems):
    x, y, c, others = _mesh_place()
    me = 2 * x + y
    for w, buf in enumerate(bufs):
        for k in range(3):
            mine = buf.at[me, c]
            _remote(mine, mine, send_sems.at[w, k], recv_sems.at[w, k], (*others[k], c)).start()


def _gather_finish(bufs, send_sems, recv_sems):
    x, y, c, others = _mesh_place()
    me = 2 * x + y
    sibling = (x, y, 1 - c)
    idx = [2 * ox + oy for ox, oy in others]
    chips = range(3)
    for w, buf in enumerate(bufs):
        for k in chips:
            landed = buf.at[idx[k], c]
            _remote(landed, landed, send_sems.at[w, k], recv_sems.at[w, k], sibling).wait_recv()
            _remote(landed, landed, send_sems.at[w, 3 + k], recv_sems.at[w, 3 + k], sibling).start()
    for w, buf in enumerate(bufs):
        for k in chips:
            landed = buf.at[idx[k], 1 - c]
            _remote(landed, landed, send_sems.at[w, 3 + k], recv_sems.at[w, 3 + k], sibling).wait_recv()
    for w, buf in enumerate(bufs):
        for k in chips:
            mine, passed = buf.at[me, c], buf.at[idx[k], c]
            _remote(mine, mine, send_sems.at[w, k], recv_sems.at[w, k], sibling).wait_send()
            _remote(passed, passed, send_sems.at[w, 3 + k], recv_sems.at[w, 3 + k], sibling).wait_send()


def _gather_sems(n):
    return [pltpu.SemaphoreType.DMA((n, 6)), pltpu.SemaphoreType.DMA((n, 6))]


def _gather_weights(bufs):
    n = len(bufs)

    def body(*refs):
        outs = refs[n:2 * n]
        send_sems, recv_sems = refs[2 * n:]
        _gather_start(outs, send_sems, recv_sems)
        _gather_finish(outs, send_sems, recv_sems)

    return pl.pallas_call(
        body, name="gather_weights",
        in_specs=[HBM_SPEC] * n, out_specs=[HBM_SPEC] * n,
        out_shape=[_hbm_like(b) for b in bufs],
        input_output_aliases={w: w for w in range(n)},
        scratch_shapes=_gather_sems(n),
    )(*bufs)


def _sibling_copies(grads, landing, send_sems, recv_sems):
    x, y, c, _ = _mesh_place()
    return [_remote(grads[w].at[j, 1 - c], landing[w].at[j], send_sems.at[w, j], recv_sems.at[w, j], (x, y, 1 - c))
            for w in range(len(grads)) for j in range(N_CHIP)]


def _sibling_exchange_start(grads, landing, send_sems, recv_sems):
    for cp in _sibling_copies(grads, landing, send_sems, recv_sems):
        cp.start()


def _sibling_exchange_finish(grads, landing, send_sems, recv_sems):
    copies = _sibling_copies(grads, landing, send_sems, recv_sems)
    for cp in copies:
        cp.wait_recv()
    for cp in copies:
        cp.wait_send()


def _sibling_exchange_sems(n):
    return [pltpu.SemaphoreType.DMA((n, N_CHIP)), pltpu.SemaphoreType.DMA((n, N_CHIP))]


def _sibling_exchange(grads):
    n = len(grads)

    def body(*refs):
        ins, outs = refs[:n], refs[n:2 * n]
        _sibling_exchange_start(ins, outs, *refs[2 * n:])
        _sibling_exchange_finish(ins, outs, *refs[2 * n:])

    return pl.pallas_call(
        body, name="sibling_exchange",
        in_specs=[HBM_SPEC] * n, out_specs=[HBM_SPEC] * n,
        out_shape=[_hbm_like(g, (N_CHIP,) + g.shape[2:]) for g in grads],
        scratch_shapes=_sibling_exchange_sems(n),
    )(*[_in_hbm(g) for g in grads])


def _chip_exchange_start(sums, landing, send_sems, recv_sems):
    x, y, c, others = _mesh_place()
    me = 2 * x + y
    for w in range(len(sums)):
        for k, (ox, oy) in enumerate(others):
            _remote(sums[w].at[2 * ox + oy], landing[w].at[me], send_sems.at[w, k], recv_sems.at[w, k],
                    (ox, oy, c)).start()


def _chip_exchange_finish(sums, landing, send_sems, recv_sems):
    x, y, c, others = _mesh_place()
    for w in range(len(sums)):
        for k, (ox, oy) in enumerate(others):
            piece = landing[w].at[2 * ox + oy]
            _remote(piece, piece, send_sems.at[w, k], recv_sems.at[w, k], (x, y, c)).wait_recv()
    for w in range(len(sums)):
        for k, (ox, oy) in enumerate(others):
            piece = sums[w].at[2 * ox + oy]
            _remote(piece, piece, send_sems.at[w, k], recv_sems.at[w, k], (x, y, c)).wait_send()


def _chip_exchange_sems(n):
    return [pltpu.SemaphoreType.DMA((n, 3)), pltpu.SemaphoreType.DMA((n, 3))]


def _sibling_allgather(bufs, also):
    n = len(bufs)
    k_in, k_out = len(also.operands), also.n_out

    def body(*refs):
        ex_ins, refs = refs[n:n + k_in], refs[n + k_in:]
        outs, refs = refs[:n], refs[n:]
        ex_outs, refs = refs[:k_out], refs[k_out:]
        send_sems, recv_sems, ex_sems = refs[0], refs[1], refs[2:]
        x, y, c, _ = _mesh_place()
        sibling = (x, y, 1 - c)
        also.start(ex_ins, ex_outs, ex_sems)
        sends = [_remote(outs[w].at[c], outs[w].at[c], send_sems.at[w], recv_sems.at[w], sibling) for w in range(n)]
        for cp in sends:
            cp.start()
        for w in range(n):
            landed = outs[w].at[1 - c]
            _remote(landed, landed, send_sems.at[w], recv_sems.at[w], sibling).wait_recv()
        for cp in sends:
            cp.wait_send()
        also.finish(ex_ins, ex_outs, ex_sems)

    res = pl.pallas_call(
        body, name="sibling_allgather",
        in_specs=[HBM_SPEC] * (n + k_in), out_specs=[HBM_SPEC] * (n + k_out),
        out_shape=[_hbm_like(b) for b in bufs] + also.out_shape,
        input_output_aliases={**{w: w for w in range(n)}, **{n + i: n + o for i, o in also.aliases.items()}},
        scratch_shapes=[pltpu.SemaphoreType.DMA((n,)), pltpu.SemaphoreType.DMA((n,))] + also.sems,
    )(*bufs, *[_in_hbm(o) for o in also.operands])
    return list(res[:n]), list(res[n:])


def _pair_sum(grad, other, place):
    _, _, h, cols = grad.shape
    tr = _row_tile(h)

    def body(place_ref, g_ref, o_ref, sums_ref, own_ref):
        s = (g_ref[0, 0] + o_ref[0]).astype(BF16)
        sums_ref[0] = s

        @pl.when(pl.program_id(1) == place_ref[0])
        def _():
            own_ref[0] = s

    return pl.pallas_call(
        body, name="pair_sum",
        grid_spec=pltpu.PrefetchScalarGridSpec(
            num_scalar_prefetch=1, grid=(h // tr, N_CHIP),
            in_specs=[pl.BlockSpec((1, 1, tr, cols), lambda r, j, place_ref: (j, place_ref[1], r, 0)),
                      pl.BlockSpec((1, tr, cols), lambda r, j, place_ref: (j, r, 0))],
            out_specs=[pl.BlockSpec((1, tr, cols), lambda r, j, place_ref: (j, r, 0)),
                       pl.BlockSpec((1, tr, cols), lambda r, j, place_ref: (place_ref[0], r, 0))]),
        out_shape=[pltpu.HBM((N_CHIP, h, cols), BF16)] * 2,
        compiler_params=_params(32, 2),
    )(place, _in_hbm(grad), _in_hbm(other))


def _chip_sum(parts, place):
    _, h, cols = parts.shape
    tr = _row_tile(h)

    def body(place_ref, p_ref, out_ref):
        out_ref[0] = ((p_ref[0].astype(F32) + p_ref[1].astype(F32)) + p_ref[2].astype(F32)) + p_ref[3].astype(F32)

    return pl.pallas_call(
        body, name="chip_sum",
        grid_spec=pltpu.PrefetchScalarGridSpec(
            num_scalar_prefetch=1, grid=(h // tr,),
            in_specs=[pl.BlockSpec((N_CHIP, tr, cols), lambda r, place_ref: (0, r, 0))],
            out_specs=pl.BlockSpec((1, tr, cols), lambda r, place_ref: (place_ref[1], r, 0))),
        out_shape=pltpu.HBM((2, h, cols), F32),
        compiler_params=_params(32),
    )(place, _in_hbm(parts))


def _adamw_math(w, g, m, v):
    m = ADAM_B1 * m + (1.0 - ADAM_B1) * g
    v = ADAM_B2 * v + (1.0 - ADAM_B2) * (g * g)
    m_hat = m / (1.0 - ADAM_B1 ** ADAM_STEP)
    v_hat = v / (1.0 - ADAM_B2 ** ADAM_STEP)
    delta = -ADAM_LR * (m_hat / (jnp.sqrt(v_hat) + ADAM_EPS) + ADAM_WD * w)
    return delta, m, v


def _adamw(w, g, m, v, exchange=None):
    rows, cols = w.shape
    tr = _row_tile(rows)

    def body(w_ref, g_ref, m_ref, v_ref, d_ref, nm_ref, nv_ref, g_out_ref):
        g = g_ref[...]
        d_ref[...], nm_ref[...], nv_ref[...] = _adamw_math(w_ref[...], g, m_ref[...], v_ref[...])
        g_out_ref[...] = g

    spec = pl.BlockSpec((tr, cols), lambda r: (r, 0))
    return _call(
        body, (w, g, m, v), grid=(rows // tr,), name="adamw",
        in_specs=[spec] * 4, out_specs=[spec] * 4,
        out_shape=[jax.ShapeDtypeStruct((rows, cols), F32)] * 4,
        compiler_params=_params(48), exchange=exchange)


SMALL_NAMES = ("norm1_gain", "gmlp_v_gain", "w_spatial", "b_spatial", "attn_sinks", "rel_bias_table", "norm2_gain",
               "final_gain")
PACK_TILE = 8 * 128


def _pack_small(arrays):
    parts = []
    for a in arrays:
        flat = a.reshape(-1)
        rows = -(-flat.shape[0] // PACK_TILE) * 8
        parts.append(jnp.pad(flat, (0, rows * 128 - flat.shape[0])).reshape(rows, 128))
    return jnp.concatenate(parts, axis=0)


def _unpack_small(packed, like):
    out, row = [], 0
    for a in like:
        size = math.prod(a.shape)
        rows = -(-size // PACK_TILE) * 8
        out.append(packed[row:row + rows].reshape(-1)[:size].reshape(a.shape))
        row += rows
    return out


def _small_update(gathered, w, m, v):
    rows = gathered.shape[1]

    def body(g_ref, w_ref, m_ref, v_ref, tot_ref, d_ref, nm_ref, nv_ref):
        total = g_ref[0].astype(F32)
        for dev in range(1, 8):
            total = total + g_ref[dev].astype(F32)
        tot_ref[...] = total
        d_ref[...], nm_ref[...], nv_ref[...] = _adamw_math(w_ref[...], total, m_ref[...], v_ref[...])

    return pl.pallas_call(
        body, name="small_update",
        in_specs=[VMEM_SPEC] * 4, out_specs=[VMEM_SPEC] * 4,
        out_shape=[jax.ShapeDtypeStruct((rows, 128), F32)] * 4,
        compiler_params=pltpu.CompilerParams(vmem_limit_bytes=24 * MIB),
    )(gathered, w, m, v)


def _halves(a):
    return a.reshape(a.shape[:-2] + (2, a.shape[-2] // 2, a.shape[-1]))


def _whole(a):
    return a.reshape(a.shape[:-3] + (2 * a.shape[-2], a.shape[-1]))


def kernel(x, p, norm1_gain, w_in, gmlp_v_gain, w_spatial, b_spatial, attn_sinks, rel_bias_table, w_out, norm2_gain, w_ff1, w_ff2, w_ple_proj, w_ple_gate, final_gain, loss_target, m_norm1_gain, m_w_in, m_gmlp_v_gain, m_w_spatial, m_b_spatial, m_attn_sinks, m_rel_bias_table, m_w_out, m_norm2_gain, m_w_ff1, m_w_ff2, m_w_ple_proj, m_w_ple_gate, m_final_gain, v_norm1_gain, v_w_in, v_gmlp_v_gain, v_w_spatial, v_b_spatial, v_attn_sinks, v_rel_bias_table, v_w_out, v_norm2_gain, v_w_ff1, v_w_ff2, v_w_ple_proj, v_w_ple_gate, v_final_gain):
    given = dict(locals())
    small = {n: given[n] for n in SMALL_NAMES}
    chip = 2 * lax.axis_index("x") + lax.axis_index("y")
    place = jnp.stack([chip, lax.axis_index("c")]).astype(jnp.int32)

    big_names = ("w_in", "w_out", "w_ff1", "w_ff2", "w_ple_proj", "w_ple_gate")
    shards = {n: given[n][0] for n in big_names}
    travel = dict(shards, w_in=jnp.transpose(shards["w_in"]))
    bufs = {n: _cast_shard(travel[n], place[:1]) for n in big_names}
    dx, landed, small_grads, sq = _step(x[0], p[0, 0], loss_target[0], small, bufs, place)

    out_grad, out_delta, out_m, out_v = {}, {}, {}, {}

    def update(n, g, exchange=None):
        to = jnp.transpose if n == "w_in" else (lambda a: a)
        (delta, new_m, new_v, g_out), got = _adamw(to(shards[n]), g, to(given["m_" + n][0]), to(given["v_" + n][0]),
                                                   exchange)
        out_grad[n], out_delta[n], out_m[n], out_v[n] = [to(a)[None] for a in (g_out, delta, new_m, new_v)]
        return got

    spare = jnp.zeros((8, 128), F32)
    small_packed = _pack_small([small_grads[n] for n in SMALL_NAMES] + [spare]).astype(BF16)
    reduced, (small_gathered, sq_gathered) = _sibling_allgather(
        [_chip_sum(landed[n], place) for n in big_names], _Both(_GatherAll(small_packed), _GatherAll(sq)))
    for n, r in zip(big_names, reduced):
        update(n, _whole(r))

    like = [given[n] for n in SMALL_NAMES] + [spare]
    packed = _small_update(small_gathered, *[_pack_small([given[pre + n] for n in SMALL_NAMES] + [spare])
                                             for pre in ("", "m_", "v_")])
    for res, out in zip(packed, (out_grad, out_delta, out_m, out_v)):
        out.update(zip(SMALL_NAMES, _unpack_small(res, like)))
    loss = 0.5 * jnp.sum(sq_gathered[:, 0, 0]) / D

    order = ("norm1_gain", "w_in", "gmlp_v_gain", "w_spatial", "b_spatial", "attn_sinks", "rel_bias_table", "w_out",
             "norm2_gain", "w_ff1", "w_ff2", "w_ple_proj", "w_ple_gate", "final_gain")
    return (loss, dx[None], *[out_grad[n] for n in order], *[out_delta[n] for n in order],
            *[out_m[n] for n in order], *[out_v[n] for n in order])
```

```python
import functools
import math

import jax
import jax.numpy as jnp
from jax import lax
from jax.experimental import pallas as pl
from jax.experimental.pallas import tpu as pltpu

S = 2048
D = 1024
D_IN = 1792
D_FF = 4096
PLE = 256
N_CHIP = 4
N_GROUP = 4
CHUNK = 128
N_HEAD = 8
N_BLOCK = S // CHUNK
N_BUCKET = 32
EPS = 1e-6
NEG_INF = -1e30
QK_SCALE = 0.125
GELU_C = math.sqrt(2.0 / math.pi)

ADAM_LR = 0.001
ADAM_B1 = 0.9
ADAM_B2 = 0.999
ADAM_EPS = 1e-08
ADAM_WD = 0.01
ADAM_STEP = 10

F32 = jnp.float32
BF16 = jnp.bfloat16
MIB = 1024 * 1024
MESH = pl.DeviceIdType.MESH

NT = (((1,), (1,)), ((), ()))
TN = (((0,), (0,)), ((), ()))


def _dot(a, b):
    return jnp.dot(a, b, preferred_element_type=F32)


def _dot_nt(a, b):
    return lax.dot_general(a, b, NT, preferred_element_type=F32)


def _dot_tn(a, b):
    return lax.dot_general(a, b, TN, preferred_element_type=F32)


def _params(vmem_mib, n_axes=1):
    return pltpu.CompilerParams(dimension_semantics=("arbitrary",) * n_axes, vmem_limit_bytes=vmem_mib * MIB)


def _rms_scale(v):
    return lax.rsqrt(jnp.mean(v * v, axis=-1, keepdims=True) + EPS)


def _rms_bwd(dy_gain, xhat, r):
    return r * (dy_gain - xhat * jnp.mean(dy_gain * xhat, axis=-1, keepdims=True))


class _Gather:
    def __init__(self, bufs):
        self.operands = list(bufs)
        self.n_out = len(self.operands)
        self.out_shape = [_hbm_like(b) for b in bufs]
        self.aliases = {w: w for w in range(self.n_out)}
        self.sems = _gather_sems(self.n_out)

    def start(self, ins, outs, sems):
        _gather_start(outs, *sems)

    def finish(self, ins, outs, sems):
        _gather_finish(outs, *sems)


class _RelayGather(_Gather):
    TOP, BOTTOM = 6, 7
    DIAGONAL_PASSED = 5

    def __init__(self, bufs):
        super().__init__(bufs)
        self.sems = [pltpu.SemaphoreType.DMA((self.n_out, 8)), pltpu.SemaphoreType.DMA((self.n_out, 8))]

    def _copies(self, bufs, send_sems, recv_sems):
        x, y, c, others = _mesh_place()
        me = 2 * x + y
        idx = [2 * ox + oy for ox, oy in others]
        sibling = (x, y, 1 - c)
        direct, passed, relayed = [], [], []
        for w, buf in enumerate(bufs):
            rows = buf.shape[2] // 2
            upper, lower = pl.ds(0, rows), pl.ds(rows, rows)
            for k in (0, 1):
                mine = buf.at[me, c]
                direct.append((_remote(mine, mine, send_sems.at[w, k], recv_sems.at[w, k], (*others[k], c)),
                               buf.at[idx[k], c], w, k))
            for k in (0, 1, 2):
                here = buf.at[idx[k], c]
                passed.append((_remote(here, here, send_sems.at[w, 3 + k], recv_sems.at[w, 3 + k], sibling),
                               buf.at[idx[k], 1 - c], w, 3 + k))
            from_x, from_y = buf.at[idx[0], c, upper], buf.at[idx[1], c, lower]
            relayed.append((_remote(from_x, from_x, send_sems.at[w, self.TOP], recv_sems.at[w, self.TOP],
                                    (*others[1], c)), buf.at[idx[2], c, upper], w, self.TOP))
            relayed.append((_remote(from_y, from_y, send_sems.at[w, self.BOTTOM], recv_sems.at[w, self.BOTTOM],
                                    (*others[0], c)), buf.at[idx[2], c, lower], w, self.BOTTOM))
        return direct, passed, relayed

    @staticmethod
    def _landed(piece, send_sems, recv_sems, w, col):
        x, y, c, _ = _mesh_place()
        _remote(piece, piece, send_sems.at[w, col], recv_sems.at[w, col], (x, y, c)).wait_recv()

    def start(self, ins, outs, sems):
        for cp, _, _, _ in self._copies(outs, *sems)[0]:
            cp.start()

    def middle(self, ins, outs, sems):
        direct, passed, relayed = self._copies(outs, *sems)
        for _, piece, w, col in direct:
            self._landed(piece, *sems, w, col)
        for cp, _, _, col in passed:
            if col != self.DIAGONAL_PASSED:
                cp.start()
        for cp, _, _, _ in relayed:
            cp.start()

    def finish(self, ins, outs, sems):
        direct, passed, relayed = self._copies(outs, *sems)
        for _, piece, w, col in relayed:
            self._landed(piece, *sems, w, col)
        for cp, _, _, col in passed:
            if col == self.DIAGONAL_PASSED:
                cp.start()
        for _, piece, w, col in passed:
            self._landed(piece, *sems, w, col)
        for cp, _, _, _ in direct + passed + relayed:
            cp.wait_send()


class _ChipExchange:
    def __init__(self, sums, landing):
        self.n_out = len(landing)
        self.operands = list(sums) + list(landing)
        self.out_shape = [_hbm_like(b) for b in landing]
        self.aliases = {self.n_out + w: w for w in range(self.n_out)}
        self.sems = _chip_exchange_sems(self.n_out)

    def start(self, ins, outs, sems):
        _chip_exchange_start(ins[:self.n_out], outs, *sems)

    def finish(self, ins, outs, sems):
        _chip_exchange_finish(ins[:self.n_out], outs, *sems)


class _GatherAll:
    def __init__(self, packed):
        self.operands = [packed]
        self.n_out = 1
        self.out_shape = [_hbm_like(packed, (8,) + packed.shape)]
        self.aliases = {}
        self.sems = [pltpu.SemaphoreType.DMA((8,)), pltpu.SemaphoreType.DMA((8,))]

    def _copies(self, ins, outs, sems):
        x, y, c, _ = _mesh_place()
        me = 4 * x + 2 * y + c
        send_sems, recv_sems = sems
        copies = []
        for k in range(1, 8):
            peer = (1 - x if k // 4 else x, 1 - y if (k // 2) % 2 else y, 1 - c if k % 2 else c)
            src = 4 * peer[0] + 2 * peer[1] + peer[2]
            copies.append((_remote(ins[0], outs[0].at[me], send_sems.at[k], recv_sems.at[k], peer), outs[0].at[src]))
        own = pltpu.make_async_copy(ins[0], outs[0].at[me], send_sems.at[0])
        return own, copies

    def start(self, ins, outs, sems):
        own, copies = self._copies(ins, outs, sems)
        own.start()
        for cp, _ in copies:
            cp.start()

    def finish(self, ins, outs, sems):
        own, copies = self._copies(ins, outs, sems)
        x, y, c, _ = _mesh_place()
        for k, (cp, landed) in enumerate(copies):
            _remote(landed, landed, sems[0].at[k + 1], sems[1].at[k + 1], (x, y, c)).wait_recv()
        for cp, _ in copies:
            cp.wait_send()
        own.wait()


class _Both:
    def __init__(self, a, b):
        self.a, self.b = a, b
        self.operands = a.operands + b.operands
        self.n_out = a.n_out + b.n_out
        self.out_shape = a.out_shape + b.out_shape
        self.aliases = dict(a.aliases)
        self.aliases.update({len(a.operands) + i: a.n_out + o for i, o in b.aliases.items()})
        self.sems = a.sems + b.sems

    def _split(self, ins, outs, sems):
        ka, na, sa = len(self.a.operands), self.a.n_out, len(self.a.sems)
        return (ins[:ka], outs[:na], sems[:sa]), (ins[ka:], outs[na:], sems[sa:])

    def start(self, ins, outs, sems):
        for ex, args in zip((self.a, self.b), self._split(ins, outs, sems)):
            ex.start(*args)

    def finish(self, ins, outs, sems):
        for ex, args in zip((self.a, self.b), self._split(ins, outs, sems)):
            ex.finish(*args)


class _SiblingExchange:
    def __init__(self, grads):
        self.operands = list(grads)
        self.n_out = len(self.operands)
        self.out_shape = [_hbm_like(g, (N_CHIP,) + g.shape[2:]) for g in grads]
        self.aliases = {}
        self.sems = _sibling_exchange_sems(self.n_out)

    def start(self, ins, outs, sems):
        _sibling_exchange_start(ins, outs, *sems)

    def finish(self, ins, outs, sems):
        _sibling_exchange_finish(ins, outs, *sems)


def _call(body, operands, *, grid, in_specs, out_specs, out_shape, name, compiler_params, scratch_shapes=(),
          exchange=None):
    operands = [o if getattr(spec, "memory_space", None) == pltpu.SMEM else _in_hbm(o)
                for o, spec in zip(operands, in_specs)]
    out_shape = [pltpu.HBM(s.shape, s.dtype) for s in out_shape]
    if exchange is None:
        res = pl.pallas_call(body, grid=grid, in_specs=in_specs, out_specs=out_specs, out_shape=out_shape, name=name,
                             scratch_shapes=list(scratch_shapes), compiler_params=compiler_params)(*operands)
        return list(res), []
    n_in, n_out, n_scr = len(in_specs), len(out_specs), len(scratch_shapes)
    k_in, k_out = len(exchange.operands), exchange.n_out

    def fused(*refs):
        ins, refs = refs[:n_in], refs[n_in:]
        ex_ins, refs = refs[:k_in], refs[k_in:]
        outs, refs = refs[:n_out], refs[n_out:]
        ex_outs, refs = refs[:k_out], refs[k_out:]
        scratch, sems = refs[:n_scr], refs[n_scr:]
        ids = [pl.program_id(a) for a in range(len(grid))]
        first = functools.reduce(jnp.logical_and, [i == 0 for i in ids])
        last = functools.reduce(jnp.logical_and, [i == g - 1 for i, g in zip(ids, grid)])

        @pl.when(first)
        def _():
            exchange.start(ex_ins, ex_outs, sems)

        if hasattr(exchange, "middle"):
            steps = math.prod(grid)
            at = (2 * steps) // 3
            place = [(at // math.prod(grid[a + 1:])) % grid[a] for a in range(len(grid))]

            @pl.when(functools.reduce(jnp.logical_and, [i == p for i, p in zip(ids, place)]))
            def _():
                exchange.middle(ex_ins, ex_outs, sems)

        body(*ins, *outs, *scratch)

        @pl.when(last)
        def _():
            exchange.finish(ex_ins, ex_outs, sems)

    res = pl.pallas_call(
        fused, grid=grid, name=name,
        in_specs=list(in_specs) + [HBM_SPEC] * k_in, out_specs=list(out_specs) + [HBM_SPEC] * k_out,
        out_shape=list(out_shape) + exchange.out_shape,
        input_output_aliases={n_in + i: n_out + o for i, o in exchange.aliases.items()},
        scratch_shapes=list(scratch_shapes) + exchange.sems, compiler_params=compiler_params,
    )(*operands, *[_in_hbm(o) for o in exchange.operands])
    return list(res[:n_out]), list(res[n_out:])


def _in_hbm(a):
    return pltpu.with_memory_space_constraint(a, pltpu.HBM)


def _row_tile(h):
    return max(t for t in range(16, 513, 16) if h % t == 0)


def _cast_shard(a, chip):
    rows, cols = a.shape
    h = rows // 2
    tr = _row_tile(h)

    def body(chip_ref, a_ref, o_ref):
        o_ref[0, 0] = a_ref[0].astype(BF16)

    return pl.pallas_call(
        body, name="cast_shard",
        grid_spec=pltpu.PrefetchScalarGridSpec(
            num_scalar_prefetch=1, grid=(2, h // tr),
            in_specs=[pl.BlockSpec((1, tr, cols), lambda s, r, chip_ref: (s, r, 0))],
            out_specs=pl.BlockSpec((1, 1, tr, cols), lambda s, r, chip_ref: (chip_ref[0], s, r, 0))),
        out_shape=pltpu.HBM((N_CHIP, 2, h, cols), BF16),
        compiler_params=_params(16, 2),
    )(chip, _in_hbm(a.reshape(2, h, cols)))


def _in_proj(x, gain1, w_in_t, exchange=None):
    tm = 256

    def body(x_ref, g_ref, w_ref, z_ref, hn_ref):
        xv = x_ref[...]
        hn = (xv * _rms_scale(xv) * g_ref[...]).astype(BF16)
        hn_ref[...] = hn
        z_ref[...] = _dot_nt(hn, w_ref[...])

    return _call(
        body, (x, gain1, w_in_t), grid=(S // tm,), name="in_proj",
        in_specs=[pl.BlockSpec((tm, D), lambda i: (i, 0)), pl.BlockSpec((1, D), lambda i: (0, 0)),
                  pl.BlockSpec((D_IN, D), lambda i: (0, 0))],
        out_specs=[pl.BlockSpec((tm, D_IN), lambda i: (i, 0)), pl.BlockSpec((tm, D), lambda i: (i, 0))],
        out_shape=[jax.ShapeDtypeStruct((S, D_IN), F32), jax.ShapeDtypeStruct((S, D), BF16)],
        compiler_params=_params(40), exchange=exchange)


def _gelu_parts(v):
    t = jnp.tanh(GELU_C * (v + 0.044715 * (v * v * v)))
    cdf = 0.5 * (1.0 + t)
    return cdf, t


def _band_mask(n):
    a = lax.broadcasted_iota(jnp.int32, (CHUNK, 2 * CHUNK), 0)
    j = lax.broadcasted_iota(jnp.int32, (CHUNK, 2 * CHUNK), 1)
    dist = CHUNK + a - j
    valid = (dist >= 0) & (dist < CHUNK)
    return valid & ((n > 0) | (j >= CHUNK))


def _fill_bias(bucket_ref, table_ref, bias_ref):
    bucket = bucket_ref[...]
    for h in range(N_HEAD):
        acc = jnp.zeros((CHUNK, 2 * CHUNK), F32)
        for b in range(N_BUCKET):
            acc = jnp.where(bucket == b, table_ref[b, h], acc)
        bias_ref[h] = acc


def _fill_tril(ws_ref, wt_ref, wtt_ref=None):
    r = lax.broadcasted_iota(jnp.int32, (CHUNK, CHUNK), 0)
    c = lax.broadcasted_iota(jnp.int32, (CHUNK, CHUNK), 1)
    for g in range(N_GROUP):
        w = jnp.where(c <= r, ws_ref[g], 0.0)
        wt_ref[g] = w.astype(BF16)
        if wtt_ref is not None:
            wtt_ref[g] = w.T.astype(BF16)


def _kv_layouts(kv_prev, kv_cur):
    both = jnp.concatenate([kv_prev, kv_cur], axis=0)
    k = both[:, :128]
    v = both[:, 128:]
    return (k.astype(BF16), pltpu.roll(k, 64, axis=1).astype(BF16),
            v.astype(BF16), pltpu.roll(v, 64, axis=1).astype(BF16))


def _head_place(h):
    pair, pos, kvh = h // 2, h % 2, h // 4
    return pair, pos, kvh == pos


def _softmax_sink(qm, k_use, bias_h, sink, valid):
    s = _dot_nt(qm, k_use) * QK_SCALE + bias_h
    s = jnp.where(valid, s, NEG_INF)
    m = jnp.maximum(jnp.max(s, axis=-1, keepdims=True), sink)
    e = jnp.exp(s - m)
    es = jnp.exp(sink - m)
    denom = jnp.sum(e, axis=-1, keepdims=True) + es
    return e / denom, es / denom


def _mixer_fwd(z, v_gain, w_spatial, b_spatial_t, sinks, rel_table, bucket, exchange=None):
    def body(z_ref, kvp_ref, gain_ref, ws_ref, bt_ref, sink_ref, table_ref, bucket_ref, out_ref, bias_ref, wt_ref):
        n = pl.program_id(0)

        @pl.when(n == 0)
        def _():
            _fill_bias(bucket_ref, table_ref, bias_ref)
            _fill_tril(ws_ref, wt_ref)

        zuv = z_ref[:, :1024]
        cdf, _ = _gelu_parts(zuv)
        guv = zuv * cdf
        for g in range(N_GROUP):
            vg = guv[:, 512 + 128 * g:512 + 128 * (g + 1)]
            vn = vg * _rms_scale(vg) * gain_ref[:, 128 * g:128 * (g + 1)]
            sv = _dot(wt_ref[g], vn.astype(BF16)) + bt_ref[:, g:g + 1]
            out_ref[:, 128 * g:128 * (g + 1)] = (guv[:, 128 * g:128 * (g + 1)] * sv).astype(BF16)

        k_same, k_swap, v_same, v_swap = _kv_layouts(kvp_ref[...], z_ref[:, 1536:1792])
        valid = _band_mask(n)
        lane_half = lax.broadcasted_iota(jnp.int32, (1, 128), 1) // 64
        for pair in range(N_HEAD // 2):
            qq = z_ref[:, 1024 + 128 * pair:1024 + 128 * (pair + 1)]
            acc = jnp.zeros((CHUNK, 128), F32)
            for pos in range(2):
                h = 2 * pair + pos
                _, _, same = _head_place(h)
                qm = jnp.where(lane_half == pos, qq, 0.0).astype(BF16)
                p, _ = _softmax_sink(qm, k_same if same else k_swap, bias_ref[h], sink_ref[h], valid)
                vm = jnp.where(lane_half == pos, v_same if same else v_swap, jnp.zeros((), BF16))
                acc = acc + _dot(p.astype(BF16), vm)
            out_ref[:, 512 + 128 * pair:512 + 128 * (pair + 1)] = acc.astype(BF16)

    return _call(
        body, (z, z, v_gain, w_spatial, b_spatial_t, sinks, rel_table, bucket), grid=(N_BLOCK,), name="mixer_fwd",
        in_specs=[pl.BlockSpec((CHUNK, D_IN), lambda n: (n, 0)),
                  pl.BlockSpec((CHUNK, 256), lambda n: (jnp.maximum(n - 1, 0), 6)),
                  pl.BlockSpec((1, 512), lambda n: (0, 0)),
                  pl.BlockSpec((N_GROUP, CHUNK, CHUNK), lambda n: (0, 0, 0)),
                  pl.BlockSpec((CHUNK, N_GROUP), lambda n: (0, 0)),
                  pl.BlockSpec(memory_space=pltpu.SMEM),
                  pl.BlockSpec(memory_space=pltpu.SMEM),
                  pl.BlockSpec((CHUNK, 2 * CHUNK), lambda n: (0, 0))],
        out_specs=[pl.BlockSpec((CHUNK, D), lambda n: (n, 0))],
        out_shape=[jax.ShapeDtypeStruct((S, D), BF16)],
        scratch_shapes=[pltpu.VMEM((N_HEAD, CHUNK, 2 * CHUNK), F32), pltpu.VMEM((N_GROUP, CHUNK, CHUNK), BF16)],
        compiler_params=_params(32), exchange=exchange)


def _out_proj(x, mix, w_out, gain2, exchange=None):
    tm = 256

    def body(x_ref, mix_ref, w_ref, g_ref, h1_ref, hn_ref, hnt_ref):
        h1 = x_ref[...] + _dot(mix_ref[...], w_ref[...])
        h1_ref[...] = h1
        hn = h1 * _rms_scale(h1) * g_ref[...]
        hn_ref[...] = hn.astype(BF16)
        hnt_ref[...] = hn.T.astype(BF16)

    return _call(
        body, (x, mix, w_out, gain2), grid=(S // tm,), name="out_proj",
        in_specs=[pl.BlockSpec((tm, D), lambda i: (i, 0)), pl.BlockSpec((tm, D), lambda i: (i, 0)),
                  pl.BlockSpec((D, D), lambda i: (0, 0)), pl.BlockSpec((1, D), lambda i: (0, 0))],
        out_specs=[pl.BlockSpec((tm, D), lambda i: (i, 0)), pl.BlockSpec((tm, D), lambda i: (i, 0)),
                   pl.BlockSpec((D, tm), lambda i: (0, i))],
        out_shape=[jax.ShapeDtypeStruct((S, D), F32), jax.ShapeDtypeStruct((S, D), BF16),
                   jax.ShapeDtypeStruct((D, S), BF16)],
        compiler_params=_params(32), exchange=exchange)


def _ffn_up(hn2, w_ff1, exchange=None):
    tm = 512
    nj = D_FF // 1024

    def body(hn_ref, w1_ref, r_ref, a_ref, at_ref):
        r = jnp.maximum(_dot(hn_ref[...], w1_ref[0]), 0.0)
        r_ref[...] = r.astype(BF16)
        a = r * r
        a_ref[...] = a.astype(BF16)
        at_ref[...] = a.T.astype(BF16)

    return _call(
        body, (hn2, w_ff1), grid=(nj, S // tm), name="ffn_up",
        in_specs=[pl.BlockSpec((tm, D), lambda j, i: (i, 0)), pl.BlockSpec((1, D, 1024), lambda j, i: (j, 0, 0))],
        out_specs=[pl.BlockSpec((tm, 1024), lambda j, i: (i, j)), pl.BlockSpec((tm, 1024), lambda j, i: (i, j)),
                   pl.BlockSpec((1024, tm), lambda j, i: (j, i))],
        out_shape=[jax.ShapeDtypeStruct((S, D_FF), BF16), jax.ShapeDtypeStruct((S, D_FF), BF16),
                   jax.ShapeDtypeStruct((D_FF, S), BF16)],
        compiler_params=_params(40, 2), exchange=exchange)


def _ffn_down(h1, a, w_ff2, exchange=None):
    tm = 1024
    nj = D_FF // 1024

    def body(h1_ref, a_ref, w2_ref, h2_ref, acc_ref):
        j = pl.program_id(1)
        part = _dot(a_ref[...], w2_ref[0])

        @pl.when(j == 0)
        def _():
            acc_ref[...] = part

        @pl.when(j > 0)
        def _():
            acc_ref[...] += part

        @pl.when(j == nj - 1)
        def _():
            h2_ref[...] = h1_ref[...] + acc_ref[...]

    return _call(
        body, (h1, a, w_ff2), grid=(S // tm, nj), name="ffn_down",
        in_specs=[pl.BlockSpec((tm, D), lambda i, j: (i, 0)), pl.BlockSpec((tm, 1024), lambda i, j: (i, j)),
                  pl.BlockSpec((1, 1024, D), lambda i, j: (j, 0, 0))],
        out_specs=[pl.BlockSpec((tm, D), lambda i, j: (i, 0))],
        out_shape=[jax.ShapeDtypeStruct((S, D), F32)],
        scratch_shapes=[pltpu.VMEM((tm, D), F32)],
        compiler_params=_params(48, 2), exchange=exchange)


def _tail(h2, p, target, w_gate, w_proj, final_gain):
    tm = 256
    steps = S // tm

    def body(h2_ref, p_ref, t_ref, wg_ref, wp_ref, gf_ref, dh2_ref, dwg_ref, dwp_ref, dgf_ref, loss_ref, dh2b_ref,
             dwp_acc):
        i = pl.program_id(0)
        h2 = h2_ref[...]
        h2b = h2.astype(BF16)
        pb = p_ref[...].astype(BF16)
        gate = jax.nn.sigmoid(_dot(h2b, wg_ref[...]))
        pp = jnp.concatenate([_dot(pb, wp_ref[j]) for j in range(N_CHIP)], axis=1)
        h3 = h2 + gate * pp
        r3 = _rms_scale(h3)
        xhat = h3 * r3
        gf = gf_ref[...]
        err = xhat * gf - t_ref[...]
        dy = err * (1.0 / D)
        dh3 = _rms_bwd(dy * gf, xhat, r3)
        dgp = (dh3 * pp * gate * (1.0 - gate)).astype(BF16)
        dpp = (dh3 * gate).astype(BF16)
        dh2 = dh3 + _dot_nt(dgp, wg_ref[...])
        dh2_ref[...] = dh2
        dh2b_ref[...] = dh2.astype(BF16)
        dwg = _dot_tn(h2b, dgp)
        dwp = _dot_tn(pb, dpp)
        dgf = jnp.sum(dy * xhat, axis=0, keepdims=True)
        sq = jnp.sum(jnp.sum(err * err, axis=1, keepdims=True), axis=0, keepdims=True)

        @pl.when(i == 0)
        def _():
            dwg_ref[...] = dwg
            dwp_acc[...] = dwp
            dgf_ref[...] = dgf
            loss_ref[...] = jnp.broadcast_to(sq, (8, 128))

        @pl.when(i > 0)
        def _():
            dwg_ref[...] += dwg
            dwp_acc[...] += dwp
            dgf_ref[...] += dgf
            loss_ref[...] += jnp.broadcast_to(sq, (8, 128))

        @pl.when(i == steps - 1)
        def _():
            for j in range(N_CHIP):
                dwp_ref[j] = dwp_acc[:, 256 * j:256 * (j + 1)]

    return _call(
        body, (h2, p, target, w_gate, w_proj, final_gain), grid=(steps,), name="tail",
        in_specs=[pl.BlockSpec((tm, D), lambda i: (i, 0)), pl.BlockSpec((tm, PLE), lambda i: (i, 0)),
                  pl.BlockSpec((tm, D), lambda i: (i, 0)), pl.BlockSpec((D, D), lambda i: (0, 0)),
                  pl.BlockSpec((N_CHIP, PLE, 256), lambda i: (0, 0, 0)), pl.BlockSpec((1, D), lambda i: (0, 0))],
        out_specs=[pl.BlockSpec((tm, D), lambda i: (i, 0)), pl.BlockSpec((D, D), lambda i: (0, 0)),
                   pl.BlockSpec((N_CHIP, PLE, 256), lambda i: (0, 0, 0)), pl.BlockSpec((1, D), lambda i: (0, 0)),
                   pl.BlockSpec((8, 128), lambda i: (0, 0)), pl.BlockSpec((tm, D), lambda i: (i, 0))],
        out_shape=[jax.ShapeDtypeStruct((S, D), F32), jax.ShapeDtypeStruct((D, D), F32),
                   jax.ShapeDtypeStruct((N_CHIP, PLE, 256), F32), jax.ShapeDtypeStruct((1, D), F32),
                   jax.ShapeDtypeStruct((8, 128), F32), jax.ShapeDtypeStruct((S, D), BF16)],
        scratch_shapes=[pltpu.VMEM((PLE, D), F32)],
        compiler_params=_params(48))[0]


def _ffn_bwd_down(dh2b, r, a_t, w_ff2, exchange=None):
    tm = 1024
    nj = D_FF // 1024

    def body(dh2_ref, r_ref, at_ref, w2_ref, df_ref, dw2_ref):
        i = pl.program_id(1)
        dh2b = dh2_ref[...]
        da = _dot_nt(dh2b, w2_ref[0])
        df_ref[...] = (da * (2.0 * r_ref[...].astype(F32))).astype(BF16)
        dw2 = _dot(at_ref[...], dh2b)

        @pl.when(i == 0)
        def _():
            dw2_ref[0] = dw2

        @pl.when(i > 0)
        def _():
            dw2_ref[0] += dw2

    return _call(
        body, (dh2b, r, a_t, w_ff2), grid=(nj, S // tm), name="ffn_bwd_down",
        in_specs=[pl.BlockSpec((tm, D), lambda j, i: (i, 0)), pl.BlockSpec((tm, 1024), lambda j, i: (i, j)),
                  pl.BlockSpec((1024, tm), lambda j, i: (j, i)), pl.BlockSpec((1, 1024, D), lambda j, i: (j, 0, 0))],
        out_specs=[pl.BlockSpec((tm, 1024), lambda j, i: (i, j)), pl.BlockSpec((1, 1024, D), lambda j, i: (j, 0, 0))],
        out_shape=[jax.ShapeDtypeStruct((S, D_FF), BF16), jax.ShapeDtypeStruct((nj, 1024, D), F32)],
        compiler_params=_params(48, 2), exchange=exchange)


def _ffn_bwd_up(df, hn2_t, exchange=None):
    tm = 1024
    nj = D_FF // 1024

    def body(df_ref, hnt_ref, dw1_ref):
        i = pl.program_id(1)
        dw1 = _dot(hnt_ref[...], df_ref[...])

        @pl.when(i == 0)
        def _():
            dw1_ref[0] = dw1

        @pl.when(i > 0)
        def _():
            dw1_ref[0] += dw1

    return _call(
        body, (df, hn2_t), grid=(nj, S // tm), name="ffn_bwd_up",
        in_specs=[pl.BlockSpec((tm, 1024), lambda j, i: (i, j)), pl.BlockSpec((D, tm), lambda j, i: (0, i))],
        out_specs=[pl.BlockSpec((1, D, 1024), lambda j, i: (j, 0, 0))],
        out_shape=[jax.ShapeDtypeStruct((nj, D, 1024), F32)],
        compiler_params=_params(40, 2), exchange=exchange)


def _ffn_bwd_input(df, w_ff1, dh2, h1, gain2, mix, w_out, exchange=None):
    tm = 512
    nj = D_FF // 1024
    steps = S // tm

    def body(df_ref, w1_ref, dh2_ref, h1_ref, g_ref, mix_ref, wo_ref, dh1_ref, dmix_ref, dwo_ref, dg_ref, acc_ref):
        i = pl.program_id(0)
        j = pl.program_id(1)
        part = _dot_nt(df_ref[...], w1_ref[0])

        @pl.when(j == 0)
        def _():
            acc_ref[...] = part

        @pl.when(j > 0)
        def _():
            acc_ref[...] += part

        @pl.when(j == nj - 1)
        def _():
            dhn = acc_ref[...]
            h1 = h1_ref[...]
            r2 = _rms_scale(h1)
            xhat = h1 * r2
            dh1 = dh2_ref[...] + _rms_bwd(dhn * g_ref[...], xhat, r2)
            dh1_ref[...] = dh1
            dh1b = dh1.astype(BF16)
            dmix_ref[...] = _dot_nt(dh1b, wo_ref[...])
            dwo = _dot_tn(mix_ref[...], dh1b)
            dg = jnp.sum(dhn * xhat, axis=0, keepdims=True)

            @pl.when(i == 0)
            def _():
                dwo_ref[...] = dwo
                dg_ref[...] = dg

            @pl.when(i > 0)
            def _():
                dwo_ref[...] += dwo
                dg_ref[...] += dg

    return _call(
        body, (df, w_ff1, dh2, h1, gain2, mix, w_out), grid=(steps, nj), name="ffn_bwd_input",
        in_specs=[pl.BlockSpec((tm, 1024), lambda i, j: (i, j)), pl.BlockSpec((1, D, 1024), lambda i, j: (j, 0, 0)),
                  pl.BlockSpec((tm, D), lambda i, j: (i, 0)), pl.BlockSpec((tm, D), lambda i, j: (i, 0)),
                  pl.BlockSpec((1, D), lambda i, j: (0, 0)), pl.BlockSpec((tm, D), lambda i, j: (i, 0)),
                  pl.BlockSpec((D, D), lambda i, j: (0, 0))],
        out_specs=[pl.BlockSpec((tm, D), lambda i, j: (i, 0)), pl.BlockSpec((tm, D), lambda i, j: (i, 0)),
                   pl.BlockSpec((D, D), lambda i, j: (0, 0)), pl.BlockSpec((1, D), lambda i, j: (0, 0))],
        out_shape=[jax.ShapeDtypeStruct((S, D), F32), jax.ShapeDtypeStruct((S, D), F32),
                   jax.ShapeDtypeStruct((D, D), F32), jax.ShapeDtypeStruct((1, D), F32)],
        scratch_shapes=[pltpu.VMEM((tm, D), F32)],
        compiler_params=_params(56, 2), exchange=exchange)


IN_GROUP = 8


def _mixer_bwd(z, dmix, v_gain, w_spatial, b_spatial_t, sinks, rel_table, bucket, hn1, exchange=None):
    def body(z_ref, kvp_ref, dm_ref, gain_ref, ws_ref, bt_ref, sink_ref, table_ref, bucket_ref, hn_ref,
             dz_ref, dws_ref, db_ref, dgain_ref, dsink_ref, drel_ref, dwin_ref,
             bias_ref, wt_ref, wtt_ref, dbias_ref, dsv_ref, carry_ref):
        n = pl.program_id(0)

        @pl.when(n == 0)
        def _():
            _fill_bias(bucket_ref, table_ref, bias_ref)
            _fill_tril(ws_ref, wt_ref, wtt_ref)
            dwin_ref[...] = jnp.zeros_like(dwin_ref)
            dbias_ref[...] = jnp.zeros_like(dbias_ref)
            dsv_ref[...] = jnp.zeros_like(dsv_ref)
            dws_ref[...] = jnp.zeros_like(dws_ref)
            dgain_ref[...] = jnp.zeros_like(dgain_ref)
            dsink_ref[...] = jnp.zeros_like(dsink_ref)

        rows = pl.ds(pl.multiple_of(n * CHUNK, CHUNK), CHUNK)

        zuv = z_ref[:, :1024]
        cdf, t = _gelu_parts(zuv)
        guv = zuv * cdf
        dgelu = cdf + zuv * (0.5 * (1.0 - t * t)) * (GELU_C * (1.0 + 3.0 * 0.044715 * (zuv * zuv)))
        for g in range(N_GROUP):
            lo, hi = 128 * g, 128 * (g + 1)
            u = guv[:, lo:hi]
            vg = guv[:, 512 + lo:512 + hi]
            rr = _rms_scale(vg)
            vhat = vg * rr
            gain = gain_ref[:, lo:hi]
            vnb = (vhat * gain).astype(BF16)
            sv = _dot(wt_ref[g], vnb) + bt_ref[:, g:g + 1]
            da = dm_ref[:, lo:hi]
            dsv = da * u
            dsvb = dsv.astype(BF16)
            dsv_ref[g] += dsv
            dws_ref[g] += _dot_nt(dsvb, vnb)
            dvn = _dot(wtt_ref[g], dsvb)
            dgain_ref[:, lo:hi] += jnp.sum(dvn * vhat, axis=0, keepdims=True)
            dvg = _rms_bwd(dvn * gain, vhat, rr)
            dz_ref[rows, lo:hi] = (da * sv * dgelu[:, lo:hi]).astype(BF16)
            dz_ref[rows, 512 + lo:512 + hi] = (dvg * dgelu[:, 512 + lo:512 + hi]).astype(BF16)

        k_same, k_swap, v_same, v_swap = _kv_layouts(kvp_ref[...], z_ref[:, 1536:1792])
        valid = _band_mask(n)
        lane_half = lax.broadcasted_iota(jnp.int32, (1, 128), 1) // 64
        zero = jnp.zeros((2 * CHUNK, 128), F32)
        dk_same, dk_swap, dv_same, dv_swap = zero, zero, zero, zero
        for pair in range(N_HEAD // 2):
            cols = slice(1024 + 128 * pair, 1024 + 128 * (pair + 1))
            qq = z_ref[:, cols]
            do_pair = dm_ref[:, 512 + 128 * pair:512 + 128 * (pair + 1)]
            dq = jnp.zeros((CHUNK, 128), F32)
            for pos in range(2):
                h = 2 * pair + pos
                _, _, same = _head_place(h)
                on_half = lane_half == pos
                qm = jnp.where(on_half, qq, 0.0).astype(BF16)
                k_use = k_same if same else k_swap
                v_use = v_same if same else v_swap
                p, p_sink = _softmax_sink(qm, k_use, bias_ref[h], sink_ref[h], valid)
                dom = jnp.where(on_half, do_pair, 0.0).astype(BF16)
                dp = _dot_nt(dom, v_use)
                dsum = jnp.sum(p * dp, axis=-1, keepdims=True)
                ds = p * (dp - dsum)
                dbias_ref[h] += ds
                dsink_ref[h:h + 1, :] += jnp.broadcast_to(jnp.sum(-p_sink * dsum, axis=0, keepdims=True), (1, 128))
                dsb = ds.astype(BF16)
                dq = dq + jnp.where(on_half, _dot(dsb, k_use), 0.0)
                dk_h = _dot_tn(dsb, qm)
                dv_h = _dot_tn(p.astype(BF16), dom)
                if same:
                    dk_same, dv_same = dk_same + dk_h, dv_same + dv_h
                else:
                    dk_swap, dv_swap = dk_swap + dk_h, dv_swap + dv_h
            dz_ref[rows, cols] = (dq * QK_SCALE).astype(BF16)
        dk = (dk_same + pltpu.roll(dk_swap, 64, axis=1)) * QK_SCALE
        dv = dv_same + pltpu.roll(dv_swap, 64, axis=1)
        dkv = jnp.concatenate([dk, dv], axis=1)

        @pl.when(n > 0)
        def _():
            prev_rows = pl.ds(pl.multiple_of((n - 1) * CHUNK, CHUNK), CHUNK)
            dz_ref[prev_rows, 1536:1792] = (carry_ref[...] + dkv[:CHUNK]).astype(BF16)

        carry_ref[...] = dkv[CHUNK:]

        @pl.when((n > 0) & (n % IN_GROUP == 0))
        def _():
            done = pl.ds(pl.multiple_of((n - IN_GROUP) * CHUNK, IN_GROUP * CHUNK), IN_GROUP * CHUNK)
            dwin_ref[...] += _dot_tn(dz_ref[done, :], hn_ref[...])

        @pl.when(n == N_BLOCK - 1)
        def _():
            dz_ref[rows, 1536:1792] = dkv[CHUNK:].astype(BF16)
            last = pl.ds((N_BLOCK - IN_GROUP) * CHUNK, IN_GROUP * CHUNK)
            dwin_ref[...] += _dot_tn(dz_ref[last, :], hn_ref[...])
            r = lax.broadcasted_iota(jnp.int32, (CHUNK, CHUNK), 0)
            c = lax.broadcasted_iota(jnp.int32, (CHUNK, CHUNK), 1)
            for g in range(N_GROUP):
                dws_ref[g] = jnp.where(c <= r, dws_ref[g], 0.0)
                db_ref[g] = jnp.sum(dsv_ref[g], axis=1, keepdims=True)
            bucket = bucket_ref[...]
            for h in range(N_HEAD):
                dbh = dbias_ref[h]
                per_bucket = [jnp.sum(jnp.where(bucket == b, dbh, 0.0), axis=0, keepdims=True) for b in range(N_BUCKET)]
                drel_ref[h] = jnp.sum(jnp.concatenate(per_bucket, axis=0), axis=1, keepdims=True)

    def hn_group(n):
        return jnp.where(n == N_BLOCK - 1, N_BLOCK // IN_GROUP - 1, jnp.maximum(n // IN_GROUP - 1, 0))

    return _call(
        body, (z, z, dmix, v_gain, w_spatial, b_spatial_t, sinks, rel_table, bucket, hn1), grid=(N_BLOCK,),
        name="mixer_bwd",
        in_specs=[pl.BlockSpec((CHUNK, D_IN), lambda n: (n, 0)),
                  pl.BlockSpec((CHUNK, 256), lambda n: (jnp.maximum(n - 1, 0), 6)),
                  pl.BlockSpec((CHUNK, D), lambda n: (n, 0)),
                  pl.BlockSpec((1, 512), lambda n: (0, 0)),
                  pl.BlockSpec((N_GROUP, CHUNK, CHUNK), lambda n: (0, 0, 0)),
                  pl.BlockSpec((CHUNK, N_GROUP), lambda n: (0, 0)),
                  pl.BlockSpec(memory_space=pltpu.SMEM),
                  pl.BlockSpec(memory_space=pltpu.SMEM),
                  pl.BlockSpec((CHUNK, 2 * CHUNK), lambda n: (0, 0)),
                  pl.BlockSpec((IN_GROUP * CHUNK, D), lambda n: (hn_group(n), 0))],
        out_specs=[pl.BlockSpec((S, D_IN), lambda n: (0, 0)),
                   pl.BlockSpec((N_GROUP, CHUNK, CHUNK), lambda n: (0, 0, 0)),
                   pl.BlockSpec((N_GROUP, CHUNK, 1), lambda n: (0, 0, 0)),
                   pl.BlockSpec((1, 512), lambda n: (0, 0)),
                   pl.BlockSpec((N_HEAD, 128), lambda n: (0, 0)),
                   pl.BlockSpec((N_HEAD, N_BUCKET, 1), lambda n: (0, 0, 0)),
                   pl.BlockSpec((D_IN, D), lambda n: (0, 0))],
        out_shape=[jax.ShapeDtypeStruct((S, D_IN), BF16), jax.ShapeDtypeStruct((N_GROUP, CHUNK, CHUNK), F32),
                   jax.ShapeDtypeStruct((N_GROUP, CHUNK, 1), F32), jax.ShapeDtypeStruct((1, 512), F32),
                   jax.ShapeDtypeStruct((N_HEAD, 128), F32), jax.ShapeDtypeStruct((N_HEAD, N_BUCKET, 1), F32),
                   jax.ShapeDtypeStruct((D_IN, D), F32)],
        scratch_shapes=[pltpu.VMEM((N_HEAD, CHUNK, 2 * CHUNK), F32), pltpu.VMEM((N_GROUP, CHUNK, CHUNK), BF16),
                        pltpu.VMEM((N_GROUP, CHUNK, CHUNK), BF16), pltpu.VMEM((N_HEAD, CHUNK, 2 * CHUNK), F32),
                        pltpu.VMEM((N_GROUP, CHUNK, CHUNK), F32), pltpu.VMEM((CHUNK, 256), F32)],
        compiler_params=_params(56), exchange=exchange)


def _in_bwd_input(dz, w_in_t, x, dh1, gain1, exchange=None):
    tm = 512

    def body(dz_ref, w_ref, x_ref, dh1_ref, g_ref, dx_ref, dg_ref):
        i = pl.program_id(0)
        dhn = _dot(dz_ref[...], w_ref[...])
        xv = x_ref[...]
        r1 = _rms_scale(xv)
        xhat = xv * r1
        dx_ref[...] = dh1_ref[...] + _rms_bwd(dhn * g_ref[...], xhat, r1)
        dg = jnp.sum(dhn * xhat, axis=0, keepdims=True)

        @pl.when(i == 0)
        def _():
            dg_ref[...] = dg

        @pl.when(i > 0)
        def _():
            dg_ref[...] += dg

    return _call(
        body, (dz, w_in_t, x, dh1, gain1), grid=(S // tm,), name="in_bwd_input",
        in_specs=[pl.BlockSpec((tm, D_IN), lambda i: (i, 0)), pl.BlockSpec((D_IN, D), lambda i: (0, 0)),
                  pl.BlockSpec((tm, D), lambda i: (i, 0)), pl.BlockSpec((tm, D), lambda i: (i, 0)),
                  pl.BlockSpec((1, D), lambda i: (0, 0))],
        out_specs=[pl.BlockSpec((tm, D), lambda i: (i, 0)), pl.BlockSpec((1, D), lambda i: (0, 0))],
        out_shape=[jax.ShapeDtypeStruct((S, D), F32), jax.ShapeDtypeStruct((1, D), F32)],
        compiler_params=_params(48), exchange=exchange)


def _rel_bucket():
    a = jnp.arange(CHUNK)[:, None]
    j = jnp.arange(2 * CHUNK)[None, :]
    n = jnp.maximum(CHUNK + a - j, 0)
    max_exact = N_BUCKET // 2
    nf = jnp.maximum(n, 1).astype(jnp.float32)
    large = max_exact + (jnp.log(nf / max_exact) / math.log(CHUNK / max_exact) * (N_BUCKET - max_exact)).astype(jnp.int32)
    large = jnp.minimum(large, N_BUCKET - 1)
    return jnp.where(n < max_exact, n, large).astype(jnp.int32)


def _step(x, p, target, small, bufs, place):
    bucket = _rel_bucket()
    sinks = small["attn_sinks"].reshape(N_HEAD)
    b_t = jnp.transpose(small["b_spatial"].reshape(N_GROUP, CHUNK))
    ws = small["w_spatial"].reshape(N_GROUP, CHUNK, CHUNK)
    gain1, gain2 = small["norm1_gain"], small["norm2_gain"]
    v_gain = small["gmlp_v_gain"]
    final_gain = small["final_gain"].reshape(1, D)
    table = small["rel_bias_table"]
    bufs = dict(bufs)

    def gather(*names):
        return _RelayGather([bufs[n] for n in names])

    def took(names, got):
        bufs.update(zip(names, got))

    took(["w_in"], _gather_weights([bufs["w_in"]]))
    w_in_t = _whole(bufs["w_in"]).reshape(D_IN, D)
    (z, hn1), got = _in_proj(x, gain1, w_in_t, gather("w_out"))
    took(["w_out"], got)
    (mix,), got = _mixer_fwd(z, v_gain, ws, b_t, sinks, table, bucket, gather("w_ff1"))
    took(["w_ff1"], got)
    w_out = _whole(bufs["w_out"]).reshape(D, D)
    (h1, hn2, hn2_t), _ = _out_proj(x, mix, w_out, gain2)
    w_ff1 = _whole(bufs["w_ff1"])
    (r, a, a_t), got = _ffn_up(hn2, w_ff1, gather("w_ff2"))
    took(["w_ff2"], got)
    w_ff2 = _whole(bufs["w_ff2"])
    (h2,), got = _ffn_down(h1, a, w_ff2, gather("w_ple_gate", "w_ple_proj"))
    took(["w_ple_gate", "w_ple_proj"], got)
    dh2, d_gate, d_proj, d_final, sq, dh2b = _tail(h2, p, target, _whole(bufs["w_ple_gate"]).reshape(D, D),
                                                   _whole(bufs["w_ple_proj"]), final_gain)

    def pair_sums(halves, from_sibling):
        sums, landing = zip(*[_pair_sum(g, o, place) for g, o in zip(halves, from_sibling)])
        return list(sums), list(landing)

    landed = {}
    halves = [_halves(d_gate.reshape(N_CHIP, 256, D)), _halves(d_proj)]
    (df, d_ff2), got = _ffn_bwd_down(dh2b, r, a_t, w_ff2, _SiblingExchange(halves))
    ex, halves = _ChipExchange(*pair_sums(halves, got)), [_halves(d_ff2)]
    (d_ff1,), got = _ffn_bwd_up(df, hn2_t, _Both(ex, _SiblingExchange(halves)))
    landed.update(zip(["w_ple_gate", "w_ple_proj"], got[:2]))
    ex, halves = _ChipExchange(*pair_sums(halves, got[2:])), [_halves(d_ff1)]
    (dh1, dmix, d_out, d_gain2), got = _ffn_bwd_input(df, w_ff1, dh2, h1, gain2, mix, w_out,
                                                      _Both(ex, _SiblingExchange(halves)))
    landed["w_ff2"] = got[0]
    ex, halves = _ChipExchange(*pair_sums(halves, got[1:])), [_halves(d_out.reshape(N_CHIP, 256, D))]
    (dz, d_ws, d_b, d_vgain, d_sink, d_rel, d_in_t), got = _mixer_bwd(z, dmix, v_gain, ws, b_t, sinks, table, bucket, hn1,
                                                                     _Both(ex, _SiblingExchange(halves)))
    landed["w_ff1"] = got[0]
    small_grads = {
        "gmlp_v_gain": d_vgain, "w_spatial": d_ws.reshape(1, N_GROUP, CHUNK, CHUNK),
        "b_spatial": d_b.reshape(1, N_GROUP, CHUNK), "attn_sinks": d_sink[:, 0].reshape(1, N_HEAD),
        "rel_bias_table": jnp.transpose(d_rel.reshape(N_HEAD, N_BUCKET)), "norm2_gain": d_gain2,
        "final_gain": d_final.reshape(D),
    }
    halves_in = [_halves(d_in_t.reshape(N_CHIP, 448, D))]
    ex = _ChipExchange(*pair_sums(halves + halves_in, list(got[1:]) + list(_sibling_exchange(halves_in))))
    (dx, small_grads["norm1_gain"]), got = _in_bwd_input(dz, w_in_t, x, dh1, gain1, ex)
    landed.update(zip(["w_out", "w_in"], got))
    return dx, landed, small_grads, sq


HBM_SPEC = pl.BlockSpec(memory_space=pltpu.HBM)
VMEM_SPEC = pl.BlockSpec(memory_space=pltpu.VMEM)


def _mesh_place():
    x, y, c = lax.axis_index("x"), lax.axis_index("y"), lax.axis_index("c")
    others = [(1 - x, y), (x, 1 - y), (1 - x, 1 - y)]
    return x, y, c, others


def _remote(src, dst, send_sem, recv_sem, device):
    return pltpu.make_async_remote_copy(src_ref=src, dst_ref=dst, send_sem=send_sem, recv_sem=recv_sem,
                                        device_id=device, device_id_type=MESH)


def _hbm_like(a, shape=None, dtype=None):
    return pltpu.HBM(a.shape if shape is None else shape, a.dtype if dtype is None else dtype)


def _gather_start(bufs, send_sems, recv_sems):
    x, y, c, others = _mesh_place()
    me = 2 * x + y
    for w, buf in enumerate(bufs):
        for k in range(3):
            mine = buf.at[me, c]
            _remote(mine, mine, send_sems.at[w, k], recv_sems.at[w, k], (*others[k], c)).start()


def _gather_finish(bufs, send_sems, recv_sems):
    x, y, c, others = _mesh_place()
    me = 2 * x + y
    sibling = (x, y, 1 - c)
    idx = [2 * ox + oy for ox, oy in others]
    chips = range(3)
    for w, buf in enumerate(bufs):
        for k in chips:
            landed = buf.at[idx[k], c]
            _remote(landed, landed, send_sems.at[w, k], recv_sems.at[w, k], sibling).wait_recv()
            _remote(landed, landed, send_sems.at[w, 3 + k], recv_sems.at[w, 3 + k], sibling).start()
    for w, buf in enumerate(bufs):
        for k in chips:
            landed = buf.at[idx[k], 1 - c]
            _remote(landed, landed, send_sems.at[w, 3 + k], recv_sems.at[w, 3 + k], sibling).wait_recv()
    for w, buf in enumerate(bufs):
        for k in chips:
            mine, passed = buf.at[me, c], buf.at[idx[k], c]
            _remote(mine, mine, send_sems.at[w, k], recv_sems.at[w, k], sibling).wait_send()
            _remote(passed, passed, send_sems.at[w, 3 + k], recv_sems.at[w, 3 + k], sibling).wait_send()


def _gather_sems(n):
    return [pltpu.SemaphoreType.DMA((n, 6)), pltpu.SemaphoreType.DMA((n, 6))]


def _gather_weights(bufs):
    n = len(bufs)

    def body(*refs):
        outs = refs[n:2 * n]
        send_sems, recv_sems = refs[2 * n:]
        _gather_start(outs, send_sems, recv_sems)
        _gather_finish(outs, send_sems, recv_sems)

    return pl.pallas_call(
        body, name="gather_weights",
        in_specs=[HBM_SPEC] * n, out_specs=[HBM_SPEC] * n,
        out_shape=[_hbm_like(b) for b in bufs],
        input_output_aliases={w: w for w in range(n)},
        scratch_shapes=_gather_sems(n),
    )(*bufs)


def _sibling_copies(grads, landing, send_sems, recv_sems):
    x, y, c, _ = _mesh_place()
    return [_remote(grads[w].at[j, 1 - c], landing[w].at[j], send_sems.at[w, j], recv_sems.at[w, j], (x, y, 1 - c))
            for w in range(len(grads)) for j in range(N_CHIP)]


def _sibling_exchange_start(grads, landing, send_sems, recv_sems):
    for cp in _sibling_copies(grads, landing, send_sems, recv_sems):
        cp.start()


def _sibling_exchange_finish(grads, landing, send_sems, recv_sems):
    copies = _sibling_copies(grads, landing, send_sems, recv_sems)
    for cp in copies:
        cp.wait_recv()
    for cp in copies:
        cp.wait_send()


def _sibling_exchange_sems(n):
    return [pltpu.SemaphoreType.DMA((n, N_CHIP)), pltpu.SemaphoreType.DMA((n, N_CHIP))]


def _sibling_exchange(grads):
    n = len(grads)

    def body(*refs):
        ins, outs = refs[:n], refs[n:2 * n]
        _sibling_exchange_start(ins, outs, *refs[2 * n:])
        _sibling_exchange_finish(ins, outs, *refs[2 * n:])

    return pl.pallas_call(
        body, name="sibling_exchange",
        in_specs=[HBM_SPEC] * n, out_specs=[HBM_SPEC] * n,
        out_shape=[_hbm_like(g, (N_CHIP,) + g.shape[2:]) for g in grads],
        scratch_shapes=_sibling_exchange_sems(n),
    )(*[_in_hbm(g) for g in grads])


def _chip_exchange_start(sums, landing, send_sems, recv_sems):
    x, y, c, others = _mesh_place()
    me = 2 * x + y
    for w in range(len(sums)):
        for k, (ox, oy) in enumerate(others):
            _remote(sums[w].at[2 * ox + oy], landing[w].at[me], send_sems.at[w, k], recv_sems.at[w, k],
                    (ox, oy, c)).start()


def _chip_exchange_finish(sums, landing, send_sems, recv_sems):
    x, y, c, others = _mesh_place()
    for w in range(len(sums)):
        for k, (ox, oy) in enumerate(others):
            piece = landing[w].at[2 * ox + oy]
            _remote(piece, piece, send_sems.at[w, k], recv_sems.at[w, k], (x, y, c)).wait_recv()
    for w in range(len(sums)):
        for k, (ox, oy) in enumerate(others):
            piece = sums[w].at[2 * ox + oy]
            _remote(piece, piece, send_sems.at[w, k], recv_sems.at[w, k], (x, y, c)).wait_send()


def _chip_exchange_sems(n):
    return [pltpu.SemaphoreType.DMA((n, 3)), pltpu.SemaphoreType.DMA((n, 3))]


def _sibling_allgather(bufs, also):
    n = len(bufs)
    k_in, k_out = len(also.operands), also.n_out

    def body(*refs):
        ex_ins, refs = refs[n:n + k_in], refs[n + k_in:]
        outs, refs = refs[:n], refs[n:]
        ex_outs, refs = refs[:k_out], refs[k_out:]
        send_sems, recv_sems, ex_sems = refs[0], refs[1], refs[2:]
        x, y, c, _ = _mesh_place()
        sibling = (x, y, 1 - c)
        also.start(ex_ins, ex_outs, ex_sems)
        sends = [_remote(outs[w].at[c], outs[w].at[c], send_sems.at[w], recv_sems.at[w], sibling) for w in range(n)]
        for cp in sends:
            cp.start()
        for w in range(n):
            landed = outs[w].at[1 - c]
            _remote(landed, landed, send_sems.at[w], recv_sems.at[w], sibling).wait_recv()
        for cp in sends:
            cp.wait_send()
        also.finish(ex_ins, ex_outs, ex_sems)

    res = pl.pallas_call(
        body, name="sibling_allgather",
        in_specs=[HBM_SPEC] * (n + k_in), out_specs=[HBM_SPEC] * (n + k_out),
        out_shape=[_hbm_like(b) for b in bufs] + also.out_shape,
        input_output_aliases={**{w: w for w in range(n)}, **{n + i: n + o for i, o in also.aliases.items()}},
        scratch_shapes=[pltpu.SemaphoreType.DMA((n,)), pltpu.SemaphoreType.DMA((n,))] + also.sems,
    )(*bufs, *[_in_hbm(o) for o in also.operands])
    return list(res[:n]), list(res[n:])


def _pair_sum(grad, other, place):
    _, _, h, cols = grad.shape
    tr = _row_tile(h)

    def body(place_ref, g_ref, o_ref, sums_ref, own_ref):
        s = (g_ref[0, 0] + o_ref[0]).astype(BF16)
        sums_ref[0] = s

        @pl.when(pl.program_id(1) == place_ref[0])
        def _():
            own_ref[0] = s

    return pl.pallas_call(
        body, name="pair_sum",
        grid_spec=pltpu.PrefetchScalarGridSpec(
            num_scalar_prefetch=1, grid=(h // tr, N_CHIP),
            in_specs=[pl.BlockSpec((1, 1, tr, cols), lambda r, j, place_ref: (j, place_ref[1], r, 0)),
                      pl.BlockSpec((1, tr, cols), lambda r, j, place_ref: (j, r, 0))],
            out_specs=[pl.BlockSpec((1, tr, cols), lambda r, j, place_ref: (j, r, 0)),
                       pl.BlockSpec((1, tr, cols), lambda r, j, place_ref: (place_ref[0], r, 0))]),
        out_shape=[pltpu.HBM((N_CHIP, h, cols), BF16)] * 2,
        compiler_params=_params(32, 2),
    )(place, _in_hbm(grad), _in_hbm(other))


def _chip_sum(parts, place):
    _, h, cols = parts.shape
    tr = _row_tile(h)

    def body(place_ref, p_ref, out_ref):
        out_ref[0] = ((p_ref[0].astype(F32) + p_ref[1].astype(F32)) + p_ref[2].astype(F32)) + p_ref[3].astype(F32)

    return pl.pallas_call(
        body, name="chip_sum",
        grid_spec=pltpu.PrefetchScalarGridSpec(
            num_scalar_prefetch=1, grid=(h // tr,),
            in_specs=[pl.BlockSpec((N_CHIP, tr, cols), lambda r, place_ref: (0, r, 0))],
            out_specs=pl.BlockSpec((1, tr, cols), lambda r, place_ref: (place_ref[1], r, 0))),
        out_shape=pltpu.HBM((2, h, cols), F32),
        compiler_params=_params(32),
    )(place, _in_hbm(parts))


def _adamw_math(w, g, m, v):
    m = ADAM_B1 * m + (1.0 - ADAM_B1) * g
    v = ADAM_B2 * v + (1.0 - ADAM_B2) * (g * g)
    m_hat = m / (1.0 - ADAM_B1 ** ADAM_STEP)
    v_hat = v / (1.0 - ADAM_B2 ** ADAM_STEP)
    delta = -ADAM_LR * (m_hat / (jnp.sqrt(v_hat) + ADAM_EPS) + ADAM_WD * w)
    return delta, m, v


def _adamw(w, g, m, v, exchange=None):
    rows, cols = w.shape
    tr = _row_tile(rows)

    def body(w_ref, g_ref, m_ref, v_ref, d_ref, nm_ref, nv_ref, g_out_ref):
        g = g_ref[...]
        d_ref[...], nm_ref[...], nv_ref[...] = _adamw_math(w_ref[...], g, m_ref[...], v_ref[...])
        g_out_ref[...] = g

    spec = pl.BlockSpec((tr, cols), lambda r: (r, 0))
    return _call(
        body, (w, g, m, v), grid=(rows // tr,), name="adamw",
        in_specs=[spec] * 4, out_specs=[spec] * 4,
        out_shape=[jax.ShapeDtypeStruct((rows, cols), F32)] * 4,
        compiler_params=_params(48), exchange=exchange)


SMALL_NAMES = ("norm1_gain", "gmlp_v_gain", "w_spatial", "b_spatial", "attn_sinks", "rel_bias_table", "norm2_gain",
               "final_gain")
PACK_TILE = 8 * 128


def _pack_small(arrays):
    parts = []
    for a in arrays:
        flat = a.reshape(-1)
        rows = -(-flat.shape[0] // PACK_TILE) * 8
        parts.append(jnp.pad(flat, (0, rows * 128 - flat.shape[0])).reshape(rows, 128))
    return jnp.concatenate(parts, axis=0)


def _unpack_small(packed, like):
    out, row = [], 0
    for a in like:
        size = math.prod(a.shape)
        rows = -(-size // PACK_TILE) * 8
        out.append(packed[row:row + rows].reshape(-1)[:size].reshape(a.shape))
        row += rows
    return out


def _small_update(gathered, w, m, v):
    rows = gathered.shape[1]

    def body(g_ref, w_ref, m_ref, v_ref, tot_ref, d_ref, nm_ref, nv_ref):
        total = g_ref[0].astype(F32)
        for dev in range(1, 8):
            total = total + g_ref[dev].astype(F32)
        tot_ref[...] = total
        d_ref[...], nm_ref[...], nv_ref[...] = _adamw_math(w_ref[...], total, m_ref[...], v_ref[...])

    return pl.pallas_call(
        body, name="small_update",
        in_specs=[VMEM_SPEC] * 4, out_specs=[VMEM_SPEC] * 4,
        out_shape=[jax.ShapeDtypeStruct((rows, 128), F32)] * 4,
        compiler_params=pltpu.CompilerParams(vmem_limit_bytes=24 * MIB),
    )(gathered, w, m, v)


def _halves(a):
    return a.reshape(a.shape[:-2] + (2, a.shape[-2] // 2, a.shape[-1]))


def _whole(a):
    return a.reshape(a.shape[:-3] + (2 * a.shape[-2], a.shape[-1]))


def kernel(x, p, norm1_gain, w_in, gmlp_v_gain, w_spatial, b_spatial, attn_sinks, rel_bias_table, w_out, norm2_gain, w_ff1, w_ff2, w_ple_proj, w_ple_gate, final_gain, loss_target, m_norm1_gain, m_w_in, m_gmlp_v_gain, m_w_spatial, m_b_spatial, m_attn_sinks, m_rel_bias_table, m_w_out, m_norm2_gain, m_w_ff1, m_w_ff2, m_w_ple_proj, m_w_ple_gate, m_final_gain, v_norm1_gain, v_w_in, v_gmlp_v_gain, v_w_spatial, v_b_spatial, v_attn_sinks, v_rel_bias_table, v_w_out, v_norm2_gain, v_w_ff1, v_w_ff2, v_w_ple_proj, v_w_ple_gate, v_final_gain):
    given = dict(locals())
    small = {n: given[n] for n in SMALL_NAMES}
    chip = 2 * lax.axis_index("x") + lax.axis_index("y")
    place = jnp.stack([chip, lax.axis_index("c")]).astype(jnp.int32)

    big_names = ("w_in", "w_out", "w_ff1", "w_ff2", "w_ple_proj", "w_ple_gate")
    shards = {n: given[n][0] for n in big_names}
    travel = dict(shards, w_in=jnp.transpose(shards["w_in"]))
    bufs = {n: _cast_shard(travel[n], place[:1]) for n in big_names}
    dx, landed, small_grads, sq = _step(x[0], p[0, 0], loss_target[0], small, bufs, place)

    out_grad, out_delta, out_m, out_v = {}, {}, {}, {}

    def update(n, g, exchange=None):
        to = jnp.transpose if n == "w_in" else (lambda a: a)
        (delta, new_m, new_v, g_out), got = _adamw(to(shards[n]), g, to(given["m_" + n][0]), to(given["v_" + n][0]),
                                                   exchange)
        out_grad[n], out_delta[n], out_m[n], out_v[n] = [to(a)[None] for a in (g_out, delta, new_m, new_v)]
        return got

    spare = jnp.zeros((8, 128), F32)
    small_packed = _pack_small([small_grads[n] for n in SMALL_NAMES] + [spare]).astype(BF16)
    reduced, (small_gathered, sq_gathered) = _sibling_allgather(
        [_chip_sum(landed[n], place) for n in big_names], _Both(_GatherAll(small_packed), _GatherAll(sq)))
    for n, r in zip(big_names, reduced):
        update(n, _whole(r))

    like = [given[n] for n in SMALL_NAMES] + [spare]
    packed = _small_update(small_gathered, *[_pack_small([given[pre + n] for n in SMALL_NAMES] + [spare])
                                             for pre in ("", "m_", "v_")])
    for res, out in zip(packed, (out_grad, out_delta, out_m, out_v)):
        out.update(zip(SMALL_NAMES, _unpack_small(res, like)))
    loss = 0.5 * jnp.sum(sq_gathered[:, 0, 0]) / D

    order = ("norm1_gain", "w_in", "gmlp_v_gain", "w_spatial", "b_spatial", "attn_sinks", "rel_bias_table", "w_out",
             "norm2_gain", "w_ff1", "w_ff2", "w_ple_proj", "w_ple_gate", "final_gain")
    return (loss, dx[None], *[out_grad[n] for n in order], *[out_delta[n] for n in order],
            *[out_m[n] for n in order], *[out_v[n] for n in order])
```

```python
import functools
import math

import jax
import jax.numpy as jnp
from jax import lax
from jax.experimental import pallas as pl
from jax.experimental.pallas import tpu as pltpu

S = 2048
D = 1024
D_IN = 1792
D_FF = 4096
PLE = 256
N_CHIP = 4
N_GROUP = 4
CHUNK = 128
N_HEAD = 8
N_BLOCK = S // CHUNK
N_BUCKET = 32
EPS = 1e-6
NEG_INF = -1e30
QK_SCALE = 0.125
GELU_C = math.sqrt(2.0 / math.pi)

ADAM_LR = 0.001
ADAM_B1 = 0.9
ADAM_B2 = 0.999
ADAM_EPS = 1e-08
ADAM_WD = 0.01
ADAM_STEP = 10

F32 = jnp.float32
BF16 = jnp.bfloat16
MIB = 1024 * 1024
MESH = pl.DeviceIdType.MESH

NT = (((1,), (1,)), ((), ()))
TN = (((0,), (0,)), ((), ()))


def _dot(a, b):
    return jnp.dot(a, b, preferred_element_type=F32)


def _dot_nt(a, b):
    return lax.dot_general(a, b, NT, preferred_element_type=F32)


def _dot_tn(a, b):
    return lax.dot_general(a, b, TN, preferred_element_type=F32)


def _params(vmem_mib, n_axes=1):
    return pltpu.CompilerParams(dimension_semantics=("arbitrary",) * n_axes, vmem_limit_bytes=vmem_mib * MIB)


def _rms_scale(v):
    return lax.rsqrt(jnp.mean(v * v, axis=-1, keepdims=True) + EPS)


def _rms_bwd(dy_gain, xhat, r):
    return r * (dy_gain - xhat * jnp.mean(dy_gain * xhat, axis=-1, keepdims=True))


class _Gather:
    def __init__(self, bufs):
        self.operands = list(bufs)
        self.n_out = len(self.operands)
        self.out_shape = [_hbm_like(b) for b in bufs]
        self.aliases = {w: w for w in range(self.n_out)}
        self.sems = _gather_sems(self.n_out)

    def start(self, ins, outs, sems):
        _gather_start(outs, *sems)

    def finish(self, ins, outs, sems):
        _gather_finish(outs, *sems)


class _RelayGather(_Gather):
    TOP, BOTTOM = 6, 7
    DIAGONAL_PASSED = 5

    def __init__(self, bufs):
        super().__init__(bufs)
        self.sems = [pltpu.SemaphoreType.DMA((self.n_out, 8)), pltpu.SemaphoreType.DMA((self.n_out, 8))]

    def _copies(self, bufs, send_sems, recv_sems):
        x, y, c, others = _mesh_place()
        me = 2 * x + y
        idx = [2 * ox + oy for ox, oy in others]
        sibling = (x, y, 1 - c)
        direct, passed, relayed = [], [], []
        for w, buf in enumerate(bufs):
            rows = buf.shape[2] // 2
            upper, lower = pl.ds(0, rows), pl.ds(rows, rows)
            for k in (0, 1):
                mine = buf.at[me, c]
                direct.append((_remote(mine, mine, send_sems.at[w, k], recv_sems.at[w, k], (*others[k], c)),
                               buf.at[idx[k], c], w, k))
            for k in (0, 1, 2):
                here = buf.at[idx[k], c]
                passed.append((_remote(here, here, send_sems.at[w, 3 + k], recv_sems.at[w, 3 + k], sibling),
                               buf.at[idx[k], 1 - c], w, 3 + k))
            from_x, from_y = buf.at[idx[0], c, upper], buf.at[idx[1], c, lower]
            relayed.append((_remote(from_x, from_x, send_sems.at[w, self.TOP], recv_sems.at[w, self.TOP],
                                    (*others[1], c)), buf.at[idx[2], c, upper], w, self.TOP))
            relayed.append((_remote(from_y, from_y, send_sems.at[w, self.BOTTOM], recv_sems.at[w, self.BOTTOM],
                                    (*others[0], c)), buf.at[idx[2], c, lower], w, self.BOTTOM))
        return direct, passed, relayed

    @staticmethod
    def _landed(piece, send_sems, recv_sems, w, col):
        x, y, c, _ = _mesh_place()
        _remote(piece, piece, send_sems.at[w, col], recv_sems.at[w, col], (x, y, c)).wait_recv()

    def start(self, ins, outs, sems):
        for cp, _, _, _ in self._copies(outs, *sems)[0]:
            cp.start()

    def middle(self, ins, outs, sems):
        direct, passed, relayed = self._copies(outs, *sems)
        for _, piece, w, col in direct:
            self._landed(piece, *sems, w, col)
        for cp, _, _, col in passed:
            if col != self.DIAGONAL_PASSED:
                cp.start()
        for cp, _, _, _ in relayed:
            cp.start()

    def finish(self, ins, outs, sems):
        direct, passed, relayed = self._copies(outs, *sems)
        for _, piece, w, col in relayed:
            self._landed(piece, *sems, w, col)
        for cp, _, _, col in passed:
            if col == self.DIAGONAL_PASSED:
                cp.start()
        for _, piece, w, col in passed:
            self._landed(piece, *sems, w, col)
        for cp, _, _, _ in direct + passed + relayed:
            cp.wait_send()


class _ChipExchange:
    def __init__(self, sums, landing):
        self.n_out = len(landing)
        self.operands = list(sums) + list(landing)
        self.out_shape = [_hbm_like(b) for b in landing]
        self.aliases = {self.n_out + w: w for w in range(self.n_out)}
        self.sems = _chip_exchange_sems(self.n_out)

    def start(self, ins, outs, sems):
        _chip_exchange_start(ins[:self.n_out], outs, *sems)

    def finish(self, ins, outs, sems):
        _chip_exchange_finish(ins[:self.n_out], outs, *sems)


class _GatherAll:
    def __init__(self, packed):
        self.operands = [packed]
        self.n_out = 1
        self.out_shape = [_hbm_like(packed, (8,) + packed.shape)]
        self.aliases = {}
        self.sems = [pltpu.SemaphoreType.DMA((8,)), pltpu.SemaphoreType.DMA((8,))]

    def _copies(self, ins, outs, sems):
        x, y, c, _ = _mesh_place()
        me = 4 * x + 2 * y + c
        send_sems, recv_sems = sems
        copies = []
        for k in range(1, 8):
            peer = (1 - x if k // 4 else x, 1 - y if (k // 2) % 2 else y, 1 - c if k % 2 else c)
            src = 4 * peer[0] + 2 * peer[1] + peer[2]
            copies.append((_remote(ins[0], outs[0].at[me], send_sems.at[k], recv_sems.at[k], peer), outs[0].at[src]))
        own = pltpu.make_async_copy(ins[0], outs[0].at[me], send_sems.at[0])
        return own, copies

    def start(self, ins, outs, sems):
        own, copies = self._copies(ins, outs, sems)
        own.start()
        for cp, _ in copies:
            cp.start()

    def finish(self, ins, outs, sems):
        own, copies = self._copies(ins, outs, sems)
        x, y, c, _ = _mesh_place()
        for k, (cp, landed) in enumerate(copies):
            _remote(landed, landed, sems[0].at[k + 1], sems[1].at[k + 1], (x, y, c)).wait_recv()
        for cp, _ in copies:
            cp.wait_send()
        own.wait()


class _Nothing:
    operands, n_out, out_shape, aliases, sems = [], 0, [], {}, []

    def start(self, ins, outs, sems):
        pass

    def finish(self, ins, outs, sems):
        pass


class _Both:
    def __init__(self, a, b):
        self.a, self.b = a, b
        self.operands = a.operands + b.operands
        self.n_out = a.n_out + b.n_out
        self.out_shape = a.out_shape + b.out_shape
        self.aliases = dict(a.aliases)
        self.aliases.update({len(a.operands) + i: a.n_out + o for i, o in b.aliases.items()})
        self.sems = a.sems + b.sems

    def _split(self, ins, outs, sems):
        ka, na, sa = len(self.a.operands), self.a.n_out, len(self.a.sems)
        return (ins[:ka], outs[:na], sems[:sa]), (ins[ka:], outs[na:], sems[sa:])

    def start(self, ins, outs, sems):
        for ex, args in zip((self.a, self.b), self._split(ins, outs, sems)):
            ex.start(*args)

    def finish(self, ins, outs, sems):
        for ex, args in zip((self.a, self.b), self._split(ins, outs, sems)):
            ex.finish(*args)


class _SiblingExchange:
    def __init__(self, grads):
        self.operands = list(grads)
        self.n_out = len(self.operands)
        self.out_shape = [_hbm_like(g, (N_CHIP,) + g.shape[2:]) for g in grads]
        self.aliases = {}
        self.sems = _sibling_exchange_sems(self.n_out)

    def start(self, ins, outs, sems):
        _sibling_exchange_start(ins, outs, *sems)

    def finish(self, ins, outs, sems):
        _sibling_exchange_finish(ins, outs, *sems)


def _call(body, operands, *, grid, in_specs, out_specs, out_shape, name, compiler_params, scratch_shapes=(),
          exchange=None):
    operands = [o if getattr(spec, "memory_space", None) == pltpu.SMEM else _in_hbm(o)
                for o, spec in zip(operands, in_specs)]
    out_shape = [pltpu.HBM(s.shape, s.dtype) for s in out_shape]
    if exchange is None:
        res = pl.pallas_call(body, grid=grid, in_specs=in_specs, out_specs=out_specs, out_shape=out_shape, name=name,
                             scratch_shapes=list(scratch_shapes), compiler_params=compiler_params)(*operands)
        return list(res), []
    n_in, n_out, n_scr = len(in_specs), len(out_specs), len(scratch_shapes)
    k_in, k_out = len(exchange.operands), exchange.n_out

    def fused(*refs):
        ins, refs = refs[:n_in], refs[n_in:]
        ex_ins, refs = refs[:k_in], refs[k_in:]
        outs, refs = refs[:n_out], refs[n_out:]
        ex_outs, refs = refs[:k_out], refs[k_out:]
        scratch, sems = refs[:n_scr], refs[n_scr:]
        ids = [pl.program_id(a) for a in range(len(grid))]
        first = functools.reduce(jnp.logical_and, [i == 0 for i in ids])
        last = functools.reduce(jnp.logical_and, [i == g - 1 for i, g in zip(ids, grid)])

        @pl.when(first)
        def _():
            exchange.start(ex_ins, ex_outs, sems)

        if hasattr(exchange, "middle"):
            steps = math.prod(grid)
            at = (2 * steps) // 3
            place = [(at // math.prod(grid[a + 1:])) % grid[a] for a in range(len(grid))]

            @pl.when(functools.reduce(jnp.logical_and, [i == p for i, p in zip(ids, place)]))
            def _():
                exchange.middle(ex_ins, ex_outs, sems)

        body(*ins, *outs, *scratch)

        @pl.when(last)
        def _():
            exchange.finish(ex_ins, ex_outs, sems)

    res = pl.pallas_call(
        fused, grid=grid, name=name,
        in_specs=list(in_specs) + [HBM_SPEC] * k_in, out_specs=list(out_specs) + [HBM_SPEC] * k_out,
        out_shape=list(out_shape) + exchange.out_shape,
        input_output_aliases={n_in + i: n_out + o for i, o in exchange.aliases.items()},
        scratch_shapes=list(scratch_shapes) + exchange.sems, compiler_params=compiler_params,
    )(*operands, *[_in_hbm(o) for o in exchange.operands])
    return list(res[:n_out]), list(res[n_out:])


def _in_hbm(a):
    return pltpu.with_memory_space_constraint(a, pltpu.HBM)


def _row_tile(h):
    return max(t for t in range(16, 513, 16) if h % t == 0)


def _cast_shard(a, chip):
    rows, cols = a.shape
    h = rows // 2
    tr = _row_tile(h)

    def body(chip_ref, a_ref, o_ref):
        o_ref[0, 0] = a_ref[0].astype(BF16)

    return pl.pallas_call(
        body, name="cast_shard",
        grid_spec=pltpu.PrefetchScalarGridSpec(
            num_scalar_prefetch=1, grid=(2, h // tr),
            in_specs=[pl.BlockSpec((1, tr, cols), lambda s, r, chip_ref: (s, r, 0))],
            out_specs=pl.BlockSpec((1, 1, tr, cols), lambda s, r, chip_ref: (chip_ref[0], s, r, 0))),
        out_shape=pltpu.HBM((N_CHIP, 2, h, cols), BF16),
        compiler_params=_params(16, 2),
    )(chip, _in_hbm(a.reshape(2, h, cols)))


def _in_proj(x, gain1, w_in_t, exchange=None):
    tm = 256

    def body(x_ref, g_ref, w_ref, z_ref, hn_ref):
        xv = x_ref[...]
        hn = (xv * _rms_scale(xv) * g_ref[...]).astype(BF16)
        hn_ref[...] = hn
        z_ref[...] = _dot_nt(hn, w_ref[...])

    return _call(
        body, (x, gain1, w_in_t), grid=(S // tm,), name="in_proj",
        in_specs=[pl.BlockSpec((tm, D), lambda i: (i, 0)), pl.BlockSpec((1, D), lambda i: (0, 0)),
                  pl.BlockSpec((D_IN, D), lambda i: (0, 0))],
        out_specs=[pl.BlockSpec((tm, D_IN), lambda i: (i, 0)), pl.BlockSpec((tm, D), lambda i: (i, 0))],
        out_shape=[jax.ShapeDtypeStruct((S, D_IN), F32), jax.ShapeDtypeStruct((S, D), BF16)],
        compiler_params=_params(40), exchange=exchange)


def _gelu_parts(v):
    t = jnp.tanh(GELU_C * (v + 0.044715 * (v * v * v)))
    cdf = 0.5 * (1.0 + t)
    return cdf, t


def _band_mask(n):
    a = lax.broadcasted_iota(jnp.int32, (CHUNK, 2 * CHUNK), 0)
    j = lax.broadcasted_iota(jnp.int32, (CHUNK, 2 * CHUNK), 1)
    dist = CHUNK + a - j
    valid = (dist >= 0) & (dist < CHUNK)
    return valid & ((n > 0) | (j >= CHUNK))


def _fill_bias(bucket_ref, table_ref, bias_ref):
    bucket = bucket_ref[...]
    for h in range(N_HEAD):
        acc = jnp.zeros((CHUNK, 2 * CHUNK), F32)
        for b in range(N_BUCKET):
            acc = jnp.where(bucket == b, table_ref[b, h], acc)
        bias_ref[h] = acc


def _fill_tril(ws_ref, wt_ref, wtt_ref=None):
    r = lax.broadcasted_iota(jnp.int32, (CHUNK, CHUNK), 0)
    c = lax.broadcasted_iota(jnp.int32, (CHUNK, CHUNK), 1)
    for g in range(N_GROUP):
        w = jnp.where(c <= r, ws_ref[g], 0.0)
        wt_ref[g] = w.astype(BF16)
        if wtt_ref is not None:
            wtt_ref[g] = w.T.astype(BF16)


def _kv_layouts(kv_prev, kv_cur):
    both = jnp.concatenate([kv_prev, kv_cur], axis=0)
    k = both[:, :128]
    v = both[:, 128:]
    return (k.astype(BF16), pltpu.roll(k, 64, axis=1).astype(BF16),
            v.astype(BF16), pltpu.roll(v, 64, axis=1).astype(BF16))


def _head_place(h):
    pair, pos, kvh = h // 2, h % 2, h // 4
    return pair, pos, kvh == pos


def _softmax_sink(qm, k_use, bias_h, sink, valid):
    s = _dot_nt(qm, k_use) * QK_SCALE + bias_h
    s = jnp.where(valid, s, NEG_INF)
    m = jnp.maximum(jnp.max(s, axis=-1, keepdims=True), sink)
    e = jnp.exp(s - m)
    es = jnp.exp(sink - m)
    denom = jnp.sum(e, axis=-1, keepdims=True) + es
    return e / denom, es / denom


def _mixer_fwd(z, v_gain, w_spatial, b_spatial_t, sinks, rel_table, bucket, exchange=None):
    def body(z_ref, kvp_ref, gain_ref, ws_ref, bt_ref, sink_ref, table_ref, bucket_ref, out_ref, bias_ref, wt_ref):
        n = pl.program_id(0)

        @pl.when(n == 0)
        def _():
            _fill_bias(bucket_ref, table_ref, bias_ref)
            _fill_tril(ws_ref, wt_ref)

        zuv = z_ref[:, :1024]
        cdf, _ = _gelu_parts(zuv)
        guv = zuv * cdf
        for g in range(N_GROUP):
            vg = guv[:, 512 + 128 * g:512 + 128 * (g + 1)]
            vn = vg * _rms_scale(vg) * gain_ref[:, 128 * g:128 * (g + 1)]
            sv = _dot(wt_ref[g], vn.astype(BF16)) + bt_ref[:, g:g + 1]
            out_ref[:, 128 * g:128 * (g + 1)] = (guv[:, 128 * g:128 * (g + 1)] * sv).astype(BF16)

        k_same, k_swap, v_same, v_swap = _kv_layouts(kvp_ref[...], z_ref[:, 1536:1792])
        valid = _band_mask(n)
        lane_half = lax.broadcasted_iota(jnp.int32, (1, 128), 1) // 64
        for pair in range(N_HEAD // 2):
            qq = z_ref[:, 1024 + 128 * pair:1024 + 128 * (pair + 1)]
            acc = jnp.zeros((CHUNK, 128), F32)
            for pos in range(2):
                h = 2 * pair + pos
                _, _, same = _head_place(h)
                qm = jnp.where(lane_half == pos, qq, 0.0).astype(BF16)
                p, _ = _softmax_sink(qm, k_same if same else k_swap, bias_ref[h], sink_ref[h], valid)
                vm = jnp.where(lane_half == pos, v_same if same else v_swap, jnp.zeros((), BF16))
                acc = acc + _dot(p.astype(BF16), vm)
            out_ref[:, 512 + 128 * pair:512 + 128 * (pair + 1)] = acc.astype(BF16)

    return _call(
        body, (z, z, v_gain, w_spatial, b_spatial_t, sinks, rel_table, bucket), grid=(N_BLOCK,), name="mixer_fwd",
        in_specs=[pl.BlockSpec((CHUNK, D_IN), lambda n: (n, 0)),
                  pl.BlockSpec((CHUNK, 256), lambda n: (jnp.maximum(n - 1, 0), 6)),
                  pl.BlockSpec((1, 512), lambda n: (0, 0)),
                  pl.BlockSpec((N_GROUP, CHUNK, CHUNK), lambda n: (0, 0, 0)),
                  pl.BlockSpec((CHUNK, N_GROUP), lambda n: (0, 0)),
                  pl.BlockSpec(memory_space=pltpu.SMEM),
                  pl.BlockSpec(memory_space=pltpu.SMEM),
                  pl.BlockSpec((CHUNK, 2 * CHUNK), lambda n: (0, 0))],
        out_specs=[pl.BlockSpec((CHUNK, D), lambda n: (n, 0))],
        out_shape=[jax.ShapeDtypeStruct((S, D), BF16)],
        scratch_shapes=[pltpu.VMEM((N_HEAD, CHUNK, 2 * CHUNK), F32), pltpu.VMEM((N_GROUP, CHUNK, CHUNK), BF16)],
        compiler_params=_params(32), exchange=exchange)


def _out_proj(x, mix, w_out, gain2, exchange=None):
    tm = 256

    def body(x_ref, mix_ref, w_ref, g_ref, h1_ref, hn_ref, hnt_ref):
        h1 = x_ref[...] + _dot(mix_ref[...], w_ref[...])
        h1_ref[...] = h1
        hn = h1 * _rms_scale(h1) * g_ref[...]
        hn_ref[...] = hn.astype(BF16)
        hnt_ref[...] = hn.T.astype(BF16)

    return _call(
        body, (x, mix, w_out, gain2), grid=(S // tm,), name="out_proj",
        in_specs=[pl.BlockSpec((tm, D), lambda i: (i, 0)), pl.BlockSpec((tm, D), lambda i: (i, 0)),
                  pl.BlockSpec((D, D), lambda i: (0, 0)), pl.BlockSpec((1, D), lambda i: (0, 0))],
        out_specs=[pl.BlockSpec((tm, D), lambda i: (i, 0)), pl.BlockSpec((tm, D), lambda i: (i, 0)),
                   pl.BlockSpec((D, tm), lambda i: (0, i))],
        out_shape=[jax.ShapeDtypeStruct((S, D), F32), jax.ShapeDtypeStruct((S, D), BF16),
                   jax.ShapeDtypeStruct((D, S), BF16)],
        compiler_params=_params(32), exchange=exchange)


def _ffn_up(hn2, w_ff1, exchange=None):
    tm = 512
    nj = D_FF // 1024

    def body(hn_ref, w1_ref, r_ref, a_ref, at_ref):
        r = jnp.maximum(_dot(hn_ref[...], w1_ref[0]), 0.0)
        r_ref[...] = r.astype(BF16)
        a = r * r
        a_ref[...] = a.astype(BF16)
        at_ref[...] = a.T.astype(BF16)

    return _call(
        body, (hn2, w_ff1), grid=(nj, S // tm), name="ffn_up",
        in_specs=[pl.BlockSpec((tm, D), lambda j, i: (i, 0)), pl.BlockSpec((1, D, 1024), lambda j, i: (j, 0, 0))],
        out_specs=[pl.BlockSpec((tm, 1024), lambda j, i: (i, j)), pl.BlockSpec((tm, 1024), lambda j, i: (i, j)),
                   pl.BlockSpec((1024, tm), lambda j, i: (j, i))],
        out_shape=[jax.ShapeDtypeStruct((S, D_FF), BF16), jax.ShapeDtypeStruct((S, D_FF), BF16),
                   jax.ShapeDtypeStruct((D_FF, S), BF16)],
        compiler_params=_params(40, 2), exchange=exchange)


def _ffn_down(h1, a, w_ff2, exchange=None):
    tm = 1024
    nj = D_FF // 1024

    def body(h1_ref, a_ref, w2_ref, h2_ref, acc_ref):
        j = pl.program_id(1)
        part = _dot(a_ref[...], w2_ref[0])

        @pl.when(j == 0)
        def _():
            acc_ref[...] = part

        @pl.when(j > 0)
        def _():
            acc_ref[...] += part

        @pl.when(j == nj - 1)
        def _():
            h2_ref[...] = h1_ref[...] + acc_ref[...]

    return _call(
        body, (h1, a, w_ff2), grid=(S // tm, nj), name="ffn_down",
        in_specs=[pl.BlockSpec((tm, D), lambda i, j: (i, 0)), pl.BlockSpec((tm, 1024), lambda i, j: (i, j)),
                  pl.BlockSpec((1, 1024, D), lambda i, j: (j, 0, 0))],
        out_specs=[pl.BlockSpec((tm, D), lambda i, j: (i, 0))],
        out_shape=[jax.ShapeDtypeStruct((S, D), F32)],
        scratch_shapes=[pltpu.VMEM((tm, D), F32)],
        compiler_params=_params(48, 2), exchange=exchange)


def _tail(h2, p, target, w_gate, w_proj, final_gain):
    tm = 256
    steps = S // tm

    def body(h2_ref, p_ref, t_ref, wg_ref, wp_ref, gf_ref, dh2_ref, dwg_ref, dwp_ref, dgf_ref, loss_ref, dh2b_ref,
             dwp_acc):
        i = pl.program_id(0)
        h2 = h2_ref[...]
        h2b = h2.astype(BF16)
        pb = p_ref[...].astype(BF16)
        gate = jax.nn.sigmoid(_dot(h2b, wg_ref[...]))
        pp = jnp.concatenate([_dot(pb, wp_ref[j]) for j in range(N_CHIP)], axis=1)
        h3 = h2 + gate * pp
        r3 = _rms_scale(h3)
        xhat = h3 * r3
        gf = gf_ref[...]
        err = xhat * gf - t_ref[...]
        dy = err * (1.0 / D)
        dh3 = _rms_bwd(dy * gf, xhat, r3)
        dgp = (dh3 * pp * gate * (1.0 - gate)).astype(BF16)
        dpp = (dh3 * gate).astype(BF16)
        dh2 = dh3 + _dot_nt(dgp, wg_ref[...])
        dh2_ref[...] = dh2
        dh2b_ref[...] = dh2.astype(BF16)
        dwg = _dot_tn(h2b, dgp)
        dwp = _dot_tn(pb, dpp)
        dgf = jnp.sum(dy * xhat, axis=0, keepdims=True)
        sq = jnp.sum(jnp.sum(err * err, axis=1, keepdims=True), axis=0, keepdims=True)

        @pl.when(i == 0)
        def _():
            dwg_ref[...] = dwg
            dwp_acc[...] = dwp
            dgf_ref[...] = dgf
            loss_ref[...] = jnp.broadcast_to(sq, (8, 128))

        @pl.when(i > 0)
        def _():
            dwg_ref[...] += dwg
            dwp_acc[...] += dwp
            dgf_ref[...] += dgf
            loss_ref[...] += jnp.broadcast_to(sq, (8, 128))

        @pl.when(i == steps - 1)
        def _():
            for j in range(N_CHIP):
                dwp_ref[j] = dwp_acc[:, 256 * j:256 * (j + 1)]

    return _call(
        body, (h2, p, target, w_gate, w_proj, final_gain), grid=(steps,), name="tail",
        in_specs=[pl.BlockSpec((tm, D), lambda i: (i, 0)), pl.BlockSpec((tm, PLE), lambda i: (i, 0)),
                  pl.BlockSpec((tm, D), lambda i: (i, 0)), pl.BlockSpec((D, D), lambda i: (0, 0)),
                  pl.BlockSpec((N_CHIP, PLE, 256), lambda i: (0, 0, 0)), pl.BlockSpec((1, D), lambda i: (0, 0))],
        out_specs=[pl.BlockSpec((tm, D), lambda i: (i, 0)), pl.BlockSpec((D, D), lambda i: (0, 0)),
                   pl.BlockSpec((N_CHIP, PLE, 256), lambda i: (0, 0, 0)), pl.BlockSpec((1, D), lambda i: (0, 0)),
                   pl.BlockSpec((8, 128), lambda i: (0, 0)), pl.BlockSpec((tm, D), lambda i: (i, 0))],
        out_shape=[jax.ShapeDtypeStruct((S, D), F32), jax.ShapeDtypeStruct((D, D), F32),
                   jax.ShapeDtypeStruct((N_CHIP, PLE, 256), F32), jax.ShapeDtypeStruct((1, D), F32),
                   jax.ShapeDtypeStruct((8, 128), F32), jax.ShapeDtypeStruct((S, D), BF16)],
        scratch_shapes=[pltpu.VMEM((PLE, D), F32)],
        compiler_params=_params(48))[0]


def _ffn_bwd_down(dh2b, r, a_t, w_ff2, exchange=None):
    tm = 1024
    nj = D_FF // 1024

    def body(dh2_ref, r_ref, at_ref, w2_ref, df_ref, dw2_ref):
        i = pl.program_id(1)
        dh2b = dh2_ref[...]
        da = _dot_nt(dh2b, w2_ref[0])
        df_ref[...] = (da * (2.0 * r_ref[...].astype(F32))).astype(BF16)
        dw2 = _dot(at_ref[...], dh2b)

        @pl.when(i == 0)
        def _():
            dw2_ref[0] = dw2

        @pl.when(i > 0)
        def _():
            dw2_ref[0] += dw2

    return _call(
        body, (dh2b, r, a_t, w_ff2), grid=(nj, S // tm), name="ffn_bwd_down",
        in_specs=[pl.BlockSpec((tm, D), lambda j, i: (i, 0)), pl.BlockSpec((tm, 1024), lambda j, i: (i, j)),
                  pl.BlockSpec((1024, tm), lambda j, i: (j, i)), pl.BlockSpec((1, 1024, D), lambda j, i: (j, 0, 0))],
        out_specs=[pl.BlockSpec((tm, 1024), lambda j, i: (i, j)), pl.BlockSpec((1, 1024, D), lambda j, i: (j, 0, 0))],
        out_shape=[jax.ShapeDtypeStruct((S, D_FF), BF16), jax.ShapeDtypeStruct((nj, 1024, D), F32)],
        compiler_params=_params(48, 2), exchange=exchange)


def _ffn_bwd_up(df, hn2_t, exchange=None):
    tm = 1024
    nj = D_FF // 1024

    def body(df_ref, hnt_ref, dw1_ref):
        i = pl.program_id(1)
        dw1 = _dot(hnt_ref[...], df_ref[...])

        @pl.when(i == 0)
        def _():
            dw1_ref[0] = dw1

        @pl.when(i > 0)
        def _():
            dw1_ref[0] += dw1

    return _call(
        body, (df, hn2_t), grid=(nj, S // tm), name="ffn_bwd_up",
        in_specs=[pl.BlockSpec((tm, 1024), lambda j, i: (i, j)), pl.BlockSpec((D, tm), lambda j, i: (0, i))],
        out_specs=[pl.BlockSpec((1, D, 1024), lambda j, i: (j, 0, 0))],
        out_shape=[jax.ShapeDtypeStruct((nj, D, 1024), F32)],
        compiler_params=_params(40, 2), exchange=exchange)


def _ffn_bwd_input(df, w_ff1, dh2, h1, gain2, mix, w_out, exchange=None):
    tm = 512
    nj = D_FF // 1024
    steps = S // tm

    def body(df_ref, w1_ref, dh2_ref, h1_ref, g_ref, mix_ref, wo_ref, dh1_ref, dmix_ref, dwo_ref, dg_ref, acc_ref):
        i = pl.program_id(0)
        j = pl.program_id(1)
        part = _dot_nt(df_ref[...], w1_ref[0])

        @pl.when(j == 0)
        def _():
            acc_ref[...] = part

        @pl.when(j > 0)
        def _():
            acc_ref[...] += part

        @pl.when(j == nj - 1)
        def _():
            dhn = acc_ref[...]
            h1 = h1_ref[...]
            r2 = _rms_scale(h1)
            xhat = h1 * r2
            dh1 = dh2_ref[...] + _rms_bwd(dhn * g_ref[...], xhat, r2)
            dh1_ref[...] = dh1
            dh1b = dh1.astype(BF16)
            dmix_ref[...] = _dot_nt(dh1b, wo_ref[...])
            dwo = _dot_tn(mix_ref[...], dh1b)
            dg = jnp.sum(dhn * xhat, axis=0, keepdims=True)

            @pl.when(i == 0)
            def _():
                dwo_ref[...] = dwo
                dg_ref[...] = dg

            @pl.when(i > 0)
            def _():
                dwo_ref[...] += dwo
                dg_ref[...] += dg

    return _call(
        body, (df, w_ff1, dh2, h1, gain2, mix, w_out), grid=(steps, nj), name="ffn_bwd_input",
        in_specs=[pl.BlockSpec((tm, 1024), lambda i, j: (i, j)), pl.BlockSpec((1, D, 1024), lambda i, j: (j, 0, 0)),
                  pl.BlockSpec((tm, D), lambda i, j: (i, 0)), pl.BlockSpec((tm, D), lambda i, j: (i, 0)),
                  pl.BlockSpec((1, D), lambda i, j: (0, 0)), pl.BlockSpec((tm, D), lambda i, j: (i, 0)),
                  pl.BlockSpec((D, D), lambda i, j: (0, 0))],
        out_specs=[pl.BlockSpec((tm, D), lambda i, j: (i, 0)), pl.BlockSpec((tm, D), lambda i, j: (i, 0)),
                   pl.BlockSpec((D, D), lambda i, j: (0, 0)), pl.BlockSpec((1, D), lambda i, j: (0, 0))],
        out_shape=[jax.ShapeDtypeStruct((S, D), F32), jax.ShapeDtypeStruct((S, D), F32),
                   jax.ShapeDtypeStruct((D, D), F32), jax.ShapeDtypeStruct((1, D), F32)],
        scratch_shapes=[pltpu.VMEM((tm, D), F32)],
        compiler_params=_params(56, 2), exchange=exchange)


IN_GROUP = 8


def _mixer_bwd(z, dmix, v_gain, w_spatial, b_spatial_t, sinks, rel_table, bucket, hn1, exchange=None):
    def body(z_ref, kvp_ref, dm_ref, gain_ref, ws_ref, bt_ref, sink_ref, table_ref, bucket_ref, hn_ref,
             dz_ref, dws_ref, db_ref, dgain_ref, dsink_ref, drel_ref, dwin_ref,
             bias_ref, wt_ref, wtt_ref, dbias_ref, dsv_ref, carry_ref):
        n = pl.program_id(0)

        @pl.when(n == 0)
        def _():
            _fill_bias(bucket_ref, table_ref, bias_ref)
            _fill_tril(ws_ref, wt_ref, wtt_ref)
            dwin_ref[...] = jnp.zeros_like(dwin_ref)
            dbias_ref[...] = jnp.zeros_like(dbias_ref)
            dsv_ref[...] = jnp.zeros_like(dsv_ref)
            dws_ref[...] = jnp.zeros_like(dws_ref)
            dgain_ref[...] = jnp.zeros_like(dgain_ref)
            dsink_ref[...] = jnp.zeros_like(dsink_ref)

        rows = pl.ds(pl.multiple_of(n * CHUNK, CHUNK), CHUNK)

        zuv = z_ref[:, :1024]
        cdf, t = _gelu_parts(zuv)
        guv = zuv * cdf
        dgelu = cdf + zuv * (0.5 * (1.0 - t * t)) * (GELU_C * (1.0 + 3.0 * 0.044715 * (zuv * zuv)))
        for g in range(N_GROUP):
            lo, hi = 128 * g, 128 * (g + 1)
            u = guv[:, lo:hi]
            vg = guv[:, 512 + lo:512 + hi]
            rr = _rms_scale(vg)
            vhat = vg * rr
            gain = gain_ref[:, lo:hi]
            vnb = (vhat * gain).astype(BF16)
            sv = _dot(wt_ref[g], vnb) + bt_ref[:, g:g + 1]
            da = dm_ref[:, lo:hi]
            dsv = da * u
            dsvb = dsv.astype(BF16)
            dsv_ref[g] += dsv
            dws_ref[g] += _dot_nt(dsvb, vnb)
            dvn = _dot(wtt_ref[g], dsvb)
            dgain_ref[:, lo:hi] += jnp.sum(dvn * vhat, axis=0, keepdims=True)
            dvg = _rms_bwd(dvn * gain, vhat, rr)
            dz_ref[rows, lo:hi] = (da * sv * dgelu[:, lo:hi]).astype(BF16)
            dz_ref[rows, 512 + lo:512 + hi] = (dvg * dgelu[:, 512 + lo:512 + hi]).astype(BF16)

        k_same, k_swap, v_same, v_swap = _kv_layouts(kvp_ref[...], z_ref[:, 1536:1792])
        valid = _band_mask(n)
        lane_half = lax.broadcasted_iota(jnp.int32, (1, 128), 1) // 64
        zero = jnp.zeros((2 * CHUNK, 128), F32)
        dk_same, dk_swap, dv_same, dv_swap = zero, zero, zero, zero
        for pair in range(N_HEAD // 2):
            cols = slice(1024 + 128 * pair, 1024 + 128 * (pair + 1))
            qq = z_ref[:, cols]
            do_pair = dm_ref[:, 512 + 128 * pair:512 + 128 * (pair + 1)]
            dq = jnp.zeros((CHUNK, 128), F32)
            for pos in range(2):
                h = 2 * pair + pos
                _, _, same = _head_place(h)
                on_half = lane_half == pos
                qm = jnp.where(on_half, qq, 0.0).astype(BF16)
                k_use = k_same if same else k_swap
                v_use = v_same if same else v_swap
                p, p_sink = _softmax_sink(qm, k_use, bias_ref[h], sink_ref[h], valid)
                dom = jnp.where(on_half, do_pair, 0.0).astype(BF16)
                dp = _dot_nt(dom, v_use)
                dsum = jnp.sum(p * dp, axis=-1, keepdims=True)
                ds = p * (dp - dsum)
                dbias_ref[h] += ds
                dsink_ref[h:h + 1, :] += jnp.broadcast_to(jnp.sum(-p_sink * dsum, axis=0, keepdims=True), (1, 128))
                dsb = ds.astype(BF16)
                dq = dq + jnp.where(on_half, _dot(dsb, k_use), 0.0)
                dk_h = _dot_tn(dsb, qm)
                dv_h = _dot_tn(p.astype(BF16), dom)
                if same:
                    dk_same, dv_same = dk_same + dk_h, dv_same + dv_h
                else:
                    dk_swap, dv_swap = dk_swap + dk_h, dv_swap + dv_h
            dz_ref[rows, cols] = (dq * QK_SCALE).astype(BF16)
        dk = (dk_same + pltpu.roll(dk_swap, 64, axis=1)) * QK_SCALE
        dv = dv_same + pltpu.roll(dv_swap, 64, axis=1)
        dkv = jnp.concatenate([dk, dv], axis=1)

        @pl.when(n > 0)
        def _():
            prev_rows = pl.ds(pl.multiple_of((n - 1) * CHUNK, CHUNK), CHUNK)
            dz_ref[prev_rows, 1536:1792] = (carry_ref[...] + dkv[:CHUNK]).astype(BF16)

        carry_ref[...] = dkv[CHUNK:]

        @pl.when((n > 0) & (n % IN_GROUP == 0))
        def _():
            done = pl.ds(pl.multiple_of((n - IN_GROUP) * CHUNK, IN_GROUP * CHUNK), IN_GROUP * CHUNK)
            dwin_ref[...] += _dot_tn(dz_ref[done, :], hn_ref[...])

        @pl.when(n == N_BLOCK - 1)
        def _():
            dz_ref[rows, 1536:1792] = dkv[CHUNK:].astype(BF16)
            last = pl.ds((N_BLOCK - IN_GROUP) * CHUNK, IN_GROUP * CHUNK)
            dwin_ref[...] += _dot_tn(dz_ref[last, :], hn_ref[...])
            r = lax.broadcasted_iota(jnp.int32, (CHUNK, CHUNK), 0)
            c = lax.broadcasted_iota(jnp.int32, (CHUNK, CHUNK), 1)
            for g in range(N_GROUP):
                dws_ref[g] = jnp.where(c <= r, dws_ref[g], 0.0)
                db_ref[g] = jnp.sum(dsv_ref[g], axis=1, keepdims=True)
            bucket = bucket_ref[...]
            for h in range(N_HEAD):
                dbh = dbias_ref[h]
                per_bucket = [jnp.sum(jnp.where(bucket == b, dbh, 0.0), axis=0, keepdims=True) for b in range(N_BUCKET)]
                drel_ref[h] = jnp.sum(jnp.concatenate(per_bucket, axis=0), axis=1, keepdims=True)

    def hn_group(n):
        return jnp.where(n == N_BLOCK - 1, N_BLOCK // IN_GROUP - 1, jnp.maximum(n // IN_GROUP - 1, 0))

    return _call(
        body, (z, z, dmix, v_gain, w_spatial, b_spatial_t, sinks, rel_table, bucket, hn1), grid=(N_BLOCK,),
        name="mixer_bwd",
        in_specs=[pl.BlockSpec((CHUNK, D_IN), lambda n: (n, 0)),
                  pl.BlockSpec((CHUNK, 256), lambda n: (jnp.maximum(n - 1, 0), 6)),
                  pl.BlockSpec((CHUNK, D), lambda n: (n, 0)),
                  pl.BlockSpec((1, 512), lambda n: (0, 0)),
                  pl.BlockSpec((N_GROUP, CHUNK, CHUNK), lambda n: (0, 0, 0)),
                  pl.BlockSpec((CHUNK, N_GROUP), lambda n: (0, 0)),
                  pl.BlockSpec(memory_space=pltpu.SMEM),
                  pl.BlockSpec(memory_space=pltpu.SMEM),
                  pl.BlockSpec((CHUNK, 2 * CHUNK), lambda n: (0, 0)),
                  pl.BlockSpec((IN_GROUP * CHUNK, D), lambda n: (hn_group(n), 0))],
        out_specs=[pl.BlockSpec((S, D_IN), lambda n: (0, 0)),
                   pl.BlockSpec((N_GROUP, CHUNK, CHUNK), lambda n: (0, 0, 0)),
                   pl.BlockSpec((N_GROUP, CHUNK, 1), lambda n: (0, 0, 0)),
                   pl.BlockSpec((1, 512), lambda n: (0, 0)),
                   pl.BlockSpec((N_HEAD, 128), lambda n: (0, 0)),
                   pl.BlockSpec((N_HEAD, N_BUCKET, 1), lambda n: (0, 0, 0)),
                   pl.BlockSpec((D_IN, D), lambda n: (0, 0))],
        out_shape=[jax.ShapeDtypeStruct((S, D_IN), BF16), jax.ShapeDtypeStruct((N_GROUP, CHUNK, CHUNK), F32),
                   jax.ShapeDtypeStruct((N_GROUP, CHUNK, 1), F32), jax.ShapeDtypeStruct((1, 512), F32),
                   jax.ShapeDtypeStruct((N_HEAD, 128), F32), jax.ShapeDtypeStruct((N_HEAD, N_BUCKET, 1), F32),
                   jax.ShapeDtypeStruct((D_IN, D), F32)],
        scratch_shapes=[pltpu.VMEM((N_HEAD, CHUNK, 2 * CHUNK), F32), pltpu.VMEM((N_GROUP, CHUNK, CHUNK), BF16),
                        pltpu.VMEM((N_GROUP, CHUNK, CHUNK), BF16), pltpu.VMEM((N_HEAD, CHUNK, 2 * CHUNK), F32),
                        pltpu.VMEM((N_GROUP, CHUNK, CHUNK), F32), pltpu.VMEM((CHUNK, 256), F32)],
        compiler_params=_params(56), exchange=exchange)


def _in_bwd_input(dz, w_in_t, x, dh1, gain1, exchange=None):
    tm = 512

    def body(dz_ref, w_ref, x_ref, dh1_ref, g_ref, dx_ref, dg_ref):
        i = pl.program_id(0)
        dhn = _dot(dz_ref[...], w_ref[...])
        xv = x_ref[...]
        r1 = _rms_scale(xv)
        xhat = xv * r1
        dx_ref[...] = dh1_ref[...] + _rms_bwd(dhn * g_ref[...], xhat, r1)
        dg = jnp.sum(dhn * xhat, axis=0, keepdims=True)

        @pl.when(i == 0)
        def _():
            dg_ref[...] = dg

        @pl.when(i > 0)
        def _():
            dg_ref[...] += dg

    return _call(
        body, (dz, w_in_t, x, dh1, gain1), grid=(S // tm,), name="in_bwd_input",
        in_specs=[pl.BlockSpec((tm, D_IN), lambda i: (i, 0)), pl.BlockSpec((D_IN, D), lambda i: (0, 0)),
                  pl.BlockSpec((tm, D), lambda i: (i, 0)), pl.BlockSpec((tm, D), lambda i: (i, 0)),
                  pl.BlockSpec((1, D), lambda i: (0, 0))],
        out_specs=[pl.BlockSpec((tm, D), lambda i: (i, 0)), pl.BlockSpec((1, D), lambda i: (0, 0))],
        out_shape=[jax.ShapeDtypeStruct((S, D), F32), jax.ShapeDtypeStruct((1, D), F32)],
        compiler_params=_params(48), exchange=exchange)


def _rel_bucket():
    a = jnp.arange(CHUNK)[:, None]
    j = jnp.arange(2 * CHUNK)[None, :]
    n = jnp.maximum(CHUNK + a - j, 0)
    max_exact = N_BUCKET // 2
    nf = jnp.maximum(n, 1).astype(jnp.float32)
    large = max_exact + (jnp.log(nf / max_exact) / math.log(CHUNK / max_exact) * (N_BUCKET - max_exact)).astype(jnp.int32)
    large = jnp.minimum(large, N_BUCKET - 1)
    return jnp.where(n < max_exact, n, large).astype(jnp.int32)


def _step(x, p, target, small, bufs, place):
    bucket = _rel_bucket()
    sinks = small["attn_sinks"].reshape(N_HEAD)
    b_t = jnp.transpose(small["b_spatial"].reshape(N_GROUP, CHUNK))
    ws = small["w_spatial"].reshape(N_GROUP, CHUNK, CHUNK)
    gain1, gain2 = small["norm1_gain"], small["norm2_gain"]
    v_gain = small["gmlp_v_gain"]
    final_gain = small["final_gain"].reshape(1, D)
    table = small["rel_bias_table"]
    bufs = dict(bufs)

    def gather(*names):
        return _RelayGather([bufs[n] for n in names])

    def took(names, got):
        bufs.update(zip(names, got))

    took(["w_in"], _gather_weights([bufs["w_in"]]))
    w_in_t = _whole(bufs["w_in"]).reshape(D_IN, D)
    (z, hn1), got = _in_proj(x, gain1, w_in_t, gather("w_out"))
    took(["w_out"], got)
    (mix,), got = _mixer_fwd(z, v_gain, ws, b_t, sinks, table, bucket, gather("w_ff1"))
    took(["w_ff1"], got)
    w_out = _whole(bufs["w_out"]).reshape(D, D)
    (h1, hn2, hn2_t), _ = _out_proj(x, mix, w_out, gain2)
    w_ff1 = _whole(bufs["w_ff1"])
    (r, a, a_t), got = _ffn_up(hn2, w_ff1, gather("w_ff2"))
    took(["w_ff2"], got)
    w_ff2 = _whole(bufs["w_ff2"])
    (h2,), got = _ffn_down(h1, a, w_ff2, gather("w_ple_gate", "w_ple_proj"))
    took(["w_ple_gate", "w_ple_proj"], got)
    dh2, d_gate, d_proj, d_final, sq, dh2b = _tail(h2, p, target, _whole(bufs["w_ple_gate"]).reshape(D, D),
                                                   _whole(bufs["w_ple_proj"]), final_gain)

    def pair_sums(halves, from_sibling):
        sums, landing = zip(*[_pair_sum(g, o, place) for g, o in zip(halves, from_sibling)])
        return list(sums), list(landing)

    landed = {}
    halves = [_halves(d_gate.reshape(N_CHIP, 256, D)), _halves(d_proj)]
    (df, d_ff2), got = _ffn_bwd_down(dh2b, r, a_t, w_ff2, _SiblingExchange(halves))
    ex, halves = _ChipExchange(*pair_sums(halves, got)), [_halves(d_ff2)]
    (d_ff1,), got = _ffn_bwd_up(df, hn2_t, _Both(ex, _SiblingExchange(halves)))
    landed.update(zip(["w_ple_gate", "w_ple_proj"], got[:2]))
    ex, halves = _ChipExchange(*pair_sums(halves, got[2:])), [_halves(d_ff1)]
    (dh1, dmix, d_out, d_gain2), got = _ffn_bwd_input(df, w_ff1, dh2, h1, gain2, mix, w_out,
                                                      _Both(ex, _SiblingExchange(halves)))
    landed["w_ff2"] = got[0]
    ex, halves = _ChipExchange(*pair_sums(halves, got[1:])), [_halves(d_out.reshape(N_CHIP, 256, D))]
    (dz, d_ws, d_b, d_vgain, d_sink, d_rel, d_in_t), got = _mixer_bwd(z, dmix, v_gain, ws, b_t, sinks, table, bucket, hn1,
                                                                     _Both(ex, _SiblingExchange(halves)))
    landed["w_ff1"] = got[0]
    small_grads = {
        "gmlp_v_gain": d_vgain, "w_spatial": d_ws.reshape(1, N_GROUP, CHUNK, CHUNK),
        "b_spatial": d_b.reshape(1, N_GROUP, CHUNK), "attn_sinks": d_sink[:, 0].reshape(1, N_HEAD),
        "rel_bias_table": jnp.transpose(d_rel.reshape(N_HEAD, N_BUCKET)), "norm2_gain": d_gain2,
        "final_gain": d_final.reshape(D),
    }
    ex, halves = _ChipExchange(*pair_sums(halves, got[1:])), [_halves(d_in_t.reshape(N_CHIP, 448, D))]
    (dx, small_grads["norm1_gain"]), got = _in_bwd_input(dz, w_in_t, x, dh1, gain1, _Both(ex, _SiblingExchange(halves)))
    landed["w_out"] = got[0]
    return dx, landed, _ChipExchange(*pair_sums(halves, got[1:])), small_grads, sq


HBM_SPEC = pl.BlockSpec(memory_space=pltpu.HBM)
VMEM_SPEC = pl.BlockSpec(memory_space=pltpu.VMEM)


def _mesh_place():
    x, y, c = lax.axis_index("x"), lax.axis_index("y"), lax.axis_index("c")
    others = [(1 - x, y), (x, 1 - y), (1 - x, 1 - y)]
    return x, y, c, others


def _remote(src, dst, send_sem, recv_sem, device):
    return pltpu.make_async_remote_copy(src_ref=src, dst_ref=dst, send_sem=send_sem, recv_sem=recv_sem,
                                        device_id=device, device_id_type=MESH)


def _hbm_like(a, shape=None, dtype=None):
    return pltpu.HBM(a.shape if shape is None else shape, a.dtype if dtype is None else dtype)


def _gather_start(bufs, send_sems, recv_sems):
    x, y, c, others = _mesh_place()
    me = 2 * x + y
    for w, buf in enumerate(bufs):
        for k in range(3):
            mine = buf.at[me, c]
            _remote(mine, mine, send_sems.at[w, k], recv_sems.at[w, k], (*others[k], c)).start()


def _gather_finish(bufs, send_sems, recv_sems):
    x, y, c, others = _mesh_place()
    me = 2 * x + y
    sibling = (x, y, 1 - c)
    idx = [2 * ox + oy for ox, oy in others]
    chips = range(3)
    for w, buf in enumerate(bufs):
        for k in chips:
            landed = buf.at[idx[k], c]
            _remote(landed, landed, send_sems.at[w, k], recv_sems.at[w, k], sibling).wait_recv()
            _remote(landed, landed, send_sems.at[w, 3 + k], recv_sems.at[w, 3 + k], sibling).start()
    for w, buf in enumerate(bufs):
        for k in chips:
            landed = buf.at[idx[k], 1 - c]
            _remote(landed, landed, send_sems.at[w, 3 + k], recv_sems.at[w, 3 + k], sibling).wait_recv()
    for w, buf in enumerate(bufs):
        for k in chips:
            mine, passed = buf.at[me, c], buf.at[idx[k], c]
            _remote(mine, mine, send_sems.at[w, k], recv_sems.at[w, k], sibling).wait_send()
            _remote(passed, passed, send_sems.at[w, 3 + k], recv_sems.at[w, 3 + k], sibling).wait_send()


def _gather_sems(n):
    return [pltpu.SemaphoreType.DMA((n, 6)), pltpu.SemaphoreType.DMA((n, 6))]


def _gather_weights(bufs):
    n = len(bufs)

    def body(*refs):
        outs = refs[n:2 * n]
        send_sems, recv_sems = refs[2 * n:]
        _gather_start(outs, send_sems, recv_sems)
        _gather_finish(outs, send_sems, recv_sems)

    return pl.pallas_call(
        body, name="gather_weights",
        in_specs=[HBM_SPEC] * n, out_specs=[HBM_SPEC] * n,
        out_shape=[_hbm_like(b) for b in bufs],
        input_output_aliases={w: w for w in range(n)},
        scratch_shapes=_gather_sems(n),
    )(*bufs)


def _sibling_copies(grads, landing, send_sems, recv_sems):
    x, y, c, _ = _mesh_place()
    return [_remote(grads[w].at[j, 1 - c], landing[w].at[j], send_sems.at[w, j], recv_sems.at[w, j], (x, y, 1 - c))
            for w in range(len(grads)) for j in range(N_CHIP)]


def _sibling_exchange_start(grads, landing, send_sems, recv_sems):
    for cp in _sibling_copies(grads, landing, send_sems, recv_sems):
        cp.start()


def _sibling_exchange_finish(grads, landing, send_sems, recv_sems):
    copies = _sibling_copies(grads, landing, send_sems, recv_sems)
    for cp in copies:
        cp.wait_recv()
    for cp in copies:
        cp.wait_send()


def _sibling_exchange_sems(n):
    return [pltpu.SemaphoreType.DMA((n, N_CHIP)), pltpu.SemaphoreType.DMA((n, N_CHIP))]


def _sibling_exchange(grads):
    n = len(grads)

    def body(*refs):
        ins, outs = refs[:n], refs[n:2 * n]
        _sibling_exchange_start(ins, outs, *refs[2 * n:])
        _sibling_exchange_finish(ins, outs, *refs[2 * n:])

    return pl.pallas_call(
        body, name="sibling_exchange",
        in_specs=[HBM_SPEC] * n, out_specs=[HBM_SPEC] * n,
        out_shape=[_hbm_like(g, (N_CHIP,) + g.shape[2:]) for g in grads],
        scratch_shapes=_sibling_exchange_sems(n),
    )(*[_in_hbm(g) for g in grads])


def _chip_exchange_start(sums, landing, send_sems, recv_sems):
    x, y, c, others = _mesh_place()
    me = 2 * x + y
    for w in range(len(sums)):
        for k, (ox, oy) in enumerate(others):
            _remote(sums[w].at[2 * ox + oy], landing[w].at[me], send_sems.at[w, k], recv_sems.at[w, k],
                    (ox, oy, c)).start()


def _chip_exchange_finish(sums, landing, send_sems, recv_sems):
    x, y, c, others = _mesh_place()
    for w in range(len(sums)):
        for k, (ox, oy) in enumerate(others):
            piece = landing[w].at[2 * ox + oy]
            _remote(piece, piece, send_sems.at[w, k], recv_sems.at[w, k], (x, y, c)).wait_recv()
    for w in range(len(sums)):
        for k, (ox, oy) in enumerate(others):
            piece = sums[w].at[2 * ox + oy]
            _remote(piece, piece, send_sems.at[w, k], recv_sems.at[w, k], (x, y, c)).wait_send()


def _chip_exchange_sems(n):
    return [pltpu.SemaphoreType.DMA((n, 3)), pltpu.SemaphoreType.DMA((n, 3))]


def _sibling_allgather(bufs, also):
    n = len(bufs)
    k_in, k_out = len(also.operands), also.n_out

    def body(*refs):
        ex_ins, refs = refs[n:n + k_in], refs[n + k_in:]
        outs, refs = refs[:n], refs[n:]
        ex_outs, refs = refs[:k_out], refs[k_out:]
        send_sems, recv_sems, ex_sems = refs[0], refs[1], refs[2:]
        x, y, c, _ = _mesh_place()
        sibling = (x, y, 1 - c)
        also.start(ex_ins, ex_outs, ex_sems)
        sends = [_remote(outs[w].at[c], outs[w].at[c], send_sems.at[w], recv_sems.at[w], sibling) for w in range(n)]
        for cp in sends:
            cp.start()
        for w in range(n):
            landed = outs[w].at[1 - c]
            _remote(landed, landed, send_sems.at[w], recv_sems.at[w], sibling).wait_recv()
        for cp in sends:
            cp.wait_send()
        also.finish(ex_ins, ex_outs, ex_sems)

    res = pl.pallas_call(
        body, name="sibling_allgather",
        in_specs=[HBM_SPEC] * (n + k_in), out_specs=[HBM_SPEC] * (n + k_out),
        out_shape=[_hbm_like(b) for b in bufs] + also.out_shape,
        input_output_aliases={**{w: w for w in range(n)}, **{n + i: n + o for i, o in also.aliases.items()}},
        scratch_shapes=[pltpu.SemaphoreType.DMA((n,)), pltpu.SemaphoreType.DMA((n,))] + also.sems,
    )(*bufs, *[_in_hbm(o) for o in also.operands])
    return list(res[:n]), list(res[n:])


def _pair_sum(grad, other, place):
    _, _, h, cols = grad.shape
    tr = _row_tile(h)

    def body(place_ref, g_ref, o_ref, sums_ref, own_ref):
        s = (g_ref[0, 0] + o_ref[0]).astype(BF16)
        sums_ref[0] = s

        @pl.when(pl.program_id(1) == place_ref[0])
        def _():
            own_ref[0] = s

    return pl.pallas_call(
        body, name="pair_sum",
        grid_spec=pltpu.PrefetchScalarGridSpec(
            num_scalar_prefetch=1, grid=(h // tr, N_CHIP),
            in_specs=[pl.BlockSpec((1, 1, tr, cols), lambda r, j, place_ref: (j, place_ref[1], r, 0)),
                      pl.BlockSpec((1, tr, cols), lambda r, j, place_ref: (j, r, 0))],
            out_specs=[pl.BlockSpec((1, tr, cols), lambda r, j, place_ref: (j, r, 0)),
                       pl.BlockSpec((1, tr, cols), lambda r, j, place_ref: (place_ref[0], r, 0))]),
        out_shape=[pltpu.HBM((N_CHIP, h, cols), BF16)] * 2,
        compiler_params=_params(32, 2),
    )(place, _in_hbm(grad), _in_hbm(other))


def _chip_sum(parts, place):
    _, h, cols = parts.shape
    tr = _row_tile(h)

    def body(place_ref, p_ref, out_ref):
        out_ref[0] = ((p_ref[0].astype(F32) + p_ref[1].astype(F32)) + p_ref[2].astype(F32)) + p_ref[3].astype(F32)

    return pl.pallas_call(
        body, name="chip_sum",
        grid_spec=pltpu.PrefetchScalarGridSpec(
            num_scalar_prefetch=1, grid=(h // tr,),
            in_specs=[pl.BlockSpec((N_CHIP, tr, cols), lambda r, place_ref: (0, r, 0))],
            out_specs=pl.BlockSpec((1, tr, cols), lambda r, place_ref: (place_ref[1], r, 0))),
        out_shape=pltpu.HBM((2, h, cols), F32),
        compiler_params=_params(32),
    )(place, _in_hbm(parts))


def _adamw_math(w, g, m, v):
    m = ADAM_B1 * m + (1.0 - ADAM_B1) * g
    v = ADAM_B2 * v + (1.0 - ADAM_B2) * (g * g)
    m_hat = m / (1.0 - ADAM_B1 ** ADAM_STEP)
    v_hat = v / (1.0 - ADAM_B2 ** ADAM_STEP)
    delta = -ADAM_LR * (m_hat / (jnp.sqrt(v_hat) + ADAM_EPS) + ADAM_WD * w)
    return delta, m, v


def _adamw(w, g, m, v, exchange=None):
    rows, cols = w.shape
    tr = _row_tile(rows)

    def body(w_ref, g_ref, m_ref, v_ref, d_ref, nm_ref, nv_ref, g_out_ref):
        g = g_ref[...]
        d_ref[...], nm_ref[...], nv_ref[...] = _adamw_math(w_ref[...], g, m_ref[...], v_ref[...])
        g_out_ref[...] = g

    spec = pl.BlockSpec((tr, cols), lambda r: (r, 0))
    return _call(
        body, (w, g, m, v), grid=(rows // tr,), name="adamw",
        in_specs=[spec] * 4, out_specs=[spec] * 4,
        out_shape=[jax.ShapeDtypeStruct((rows, cols), F32)] * 4,
        compiler_params=_params(48), exchange=exchange)


SMALL_NAMES = ("norm1_gain", "gmlp_v_gain", "w_spatial", "b_spatial", "attn_sinks", "rel_bias_table", "norm2_gain",
               "final_gain")
PACK_TILE = 8 * 128


def _pack_small(arrays):
    parts = []
    for a in arrays:
        flat = a.reshape(-1)
        rows = -(-flat.shape[0] // PACK_TILE) * 8
        parts.append(jnp.pad(flat, (0, rows * 128 - flat.shape[0])).reshape(rows, 128))
    return jnp.concatenate(parts, axis=0)


def _unpack_small(packed, like):
    out, row = [], 0
    for a in like:
        size = math.prod(a.shape)
        rows = -(-size // PACK_TILE) * 8
        out.append(packed[row:row + rows].reshape(-1)[:size].reshape(a.shape))
        row += rows
    return out


def _small_update(gathered, w, m, v):
    rows = gathered.shape[1]

    def body(g_ref, w_ref, m_ref, v_ref, tot_ref, d_ref, nm_ref, nv_ref):
        total = g_ref[0].astype(F32)
        for dev in range(1, 8):
            total = total + g_ref[dev].astype(F32)
        tot_ref[...] = total
        d_ref[...], nm_ref[...], nv_ref[...] = _adamw_math(w_ref[...], total, m_ref[...], v_ref[...])

    return pl.pallas_call(
        body, name="small_update",
        in_specs=[VMEM_SPEC] * 4, out_specs=[VMEM_SPEC] * 4,
        out_shape=[jax.ShapeDtypeStruct((rows, 128), F32)] * 4,
        compiler_params=pltpu.CompilerParams(vmem_limit_bytes=24 * MIB),
    )(gathered, w, m, v)


def _halves(a):
    return a.reshape(a.shape[:-2] + (2, a.shape[-2] // 2, a.shape[-1]))


def _whole(a):
    return a.reshape(a.shape[:-3] + (2 * a.shape[-2], a.shape[-1]))


def kernel(x, p, norm1_gain, w_in, gmlp_v_gain, w_spatial, b_spatial, attn_sinks, rel_bias_table, w_out, norm2_gain, w_ff1, w_ff2, w_ple_proj, w_ple_gate, final_gain, loss_target, m_norm1_gain, m_w_in, m_gmlp_v_gain, m_w_spatial, m_b_spatial, m_attn_sinks, m_rel_bias_table, m_w_out, m_norm2_gain, m_w_ff1, m_w_ff2, m_w_ple_proj, m_w_ple_gate, m_final_gain, v_norm1_gain, v_w_in, v_gmlp_v_gain, v_w_spatial, v_b_spatial, v_attn_sinks, v_rel_bias_table, v_w_out, v_norm2_gain, v_w_ff1, v_w_ff2, v_w_ple_proj, v_w_ple_gate, v_final_gain):
    given = dict(locals())
    small = {n: given[n] for n in SMALL_NAMES}
    chip = 2 * lax.axis_index("x") + lax.axis_index("y")
    place = jnp.stack([chip, lax.axis_index("c")]).astype(jnp.int32)

    big_names = ("w_in", "w_out", "w_ff1", "w_ff2", "w_ple_proj", "w_ple_gate")
    shards = {n: given[n][0] for n in big_names}
    travel = dict(shards, w_in=jnp.transpose(shards["w_in"]))
    bufs = {n: _cast_shard(travel[n], place[:1]) for n in big_names}
    dx, landed, exchange_in, small_grads, sq = _step(x[0], p[0, 0], loss_target[0], small, bufs, place)

    out_grad, out_delta, out_m, out_v = {}, {}, {}, {}

    def update(n, g, exchange=None):
        to = jnp.transpose if n == "w_in" else (lambda a: a)
        (delta, new_m, new_v, g_out), got = _adamw(to(shards[n]), g, to(given["m_" + n][0]), to(given["v_" + n][0]),
                                                   exchange)
        out_grad[n], out_delta[n], out_m[n], out_v[n] = [to(a)[None] for a in (g_out, delta, new_m, new_v)]
        return got

    spare = jnp.zeros((8, 128), F32)
    small_packed = _pack_small([small_grads[n] for n in SMALL_NAMES] + [spare]).astype(BF16)
    early = [n for n in big_names if n != "w_in"]
    reduced, (small_gathered, sq_gathered, landed_in) = _sibling_allgather(
        [_chip_sum(landed[n], place) for n in early], _Both(_Both(_GatherAll(small_packed), _GatherAll(sq)), exchange_in))
    for n, r in zip(early, reduced):
        update(n, _whole(r))
    (reduced_in,), _ = _sibling_allgather([_chip_sum(landed_in, place)], _Nothing())
    update("w_in", _whole(reduced_in))

    like = [given[n] for n in SMALL_NAMES] + [spare]
    packed = _small_update(small_gathered, *[_pack_small([given[pre + n] for n in SMALL_NAMES] + [spare])
                                             for pre in ("", "m_", "v_")])
    for res, out in zip(packed, (out_grad, out_delta, out_m, out_v)):
        out.update(zip(SMALL_NAMES, _unpack_small(res, like)))
    loss = 0.5 * jnp.sum(sq_gathered[:, 0, 0]) / D

    order = ("norm1_gain", "w_in", "gmlp_v_gain", "w_spatial", "b_spatial", "attn_sinks", "rel_bias_table", "w_out",
             "norm2_gain", "w_ff1", "w_ff2", "w_ple_proj", "w_ple_gate", "final_gain")
    return (loss, dx[None], *[out_grad[n] for n in order], *[out_delta[n] for n in order],
            *[out_m[n] for n in order], *[out_v[n] for n in order])
```

```python
import functools
import math

import jax
import jax.numpy as jnp
from jax import lax
from jax.experimental import pallas as pl
from jax.experimental.pallas import tpu as pltpu

S = 2048
D = 1024
D_IN = 1792
D_FF = 4096
PLE = 256
N_CHIP = 4
N_GROUP = 4
CHUNK = 128
N_HEAD = 8
N_BLOCK = S // CHUNK
N_BUCKET = 32
EPS = 1e-6
NEG_INF = -1e30
QK_SCALE = 0.125
GELU_C = math.sqrt(2.0 / math.pi)

ADAM_LR = 0.001
ADAM_B1 = 0.9
ADAM_B2 = 0.999
ADAM_EPS = 1e-08
ADAM_WD = 0.01
ADAM_STEP = 10

F32 = jnp.float32
BF16 = jnp.bfloat16
MIB = 1024 * 1024
MESH = pl.DeviceIdType.MESH

NT = (((1,), (1,)), ((), ()))
TN = (((0,), (0,)), ((), ()))


def _dot(a, b):
    return jnp.dot(a, b, preferred_element_type=F32)


def _dot_nt(a, b):
    return lax.dot_general(a, b, NT, preferred_element_type=F32)


def _dot_tn(a, b):
    return lax.dot_general(a, b, TN, preferred_element_type=F32)


def _params(vmem_mib, n_axes=1):
    return pltpu.CompilerParams(dimension_semantics=("arbitrary",) * n_axes, vmem_limit_bytes=vmem_mib * MIB)


def _rms_scale(v):
    return lax.rsqrt(jnp.mean(v * v, axis=-1, keepdims=True) + EPS)


def _rms_bwd(dy_gain, xhat, r):
    return r * (dy_gain - xhat * jnp.mean(dy_gain * xhat, axis=-1, keepdims=True))


class _Gather:
    def __init__(self, bufs):
        self.operands = list(bufs)
        self.n_out = len(self.operands)
        self.out_shape = [_hbm_like(b) for b in bufs]
        self.aliases = {w: w for w in range(self.n_out)}
        self.sems = _gather_sems(self.n_out)

    def start(self, ins, outs, sems):
        _gather_start(outs, *sems)

    def finish(self, ins, outs, sems):
        _gather_finish(outs, *sems)


class _RelayGather(_Gather):
    TOP, BOTTOM = 6, 7
    DIAGONAL_PASSED = 5

    def __init__(self, bufs):
        super().__init__(bufs)
        self.sems = [pltpu.SemaphoreType.DMA((self.n_out, 8)), pltpu.SemaphoreType.DMA((self.n_out, 8))]

    def _copies(self, bufs, send_sems, recv_sems):
        x, y, c, others = _mesh_place()
        me = 2 * x + y
        idx = [2 * ox + oy for ox, oy in others]
        sibling = (x, y, 1 - c)
        direct, passed, relayed = [], [], []
        for w, buf in enumerate(bufs):
            rows = buf.shape[2] // 2
            upper, lower = pl.ds(0, rows), pl.ds(rows, rows)
            for k in (0, 1):
                mine = buf.at[me, c]
                direct.append((_remote(mine, mine, send_sems.at[w, k], recv_sems.at[w, k], (*others[k], c)),
                               buf.at[idx[k], c], w, k))
            for k in (0, 1, 2):
                here = buf.at[idx[k], c]
                passed.append((_remote(here, here, send_sems.at[w, 3 + k], recv_sems.at[w, 3 + k], sibling),
                               buf.at[idx[k], 1 - c], w, 3 + k))
            from_x, from_y = buf.at[idx[0], c, upper], buf.at[idx[1], c, lower]
            relayed.append((_remote(from_x, from_x, send_sems.at[w, self.TOP], recv_sems.at[w, self.TOP],
                                    (*others[1], c)), buf.at[idx[2], c, upper], w, self.TOP))
            relayed.append((_remote(from_y, from_y, send_sems.at[w, self.BOTTOM], recv_sems.at[w, self.BOTTOM],
                                    (*others[0], c)), buf.at[idx[2], c, lower], w, self.BOTTOM))
        return direct, passed, relayed

    @staticmethod
    def _landed(piece, send_sems, recv_sems, w, col):
        x, y, c, _ = _mesh_place()
        _remote(piece, piece, send_sems.at[w, col], recv_sems.at[w, col], (x, y, c)).wait_recv()

    def start(self, ins, outs, sems):
        for cp, _, _, _ in self._copies(outs, *sems)[0]:
            cp.start()

    def middle(self, ins, outs, sems):
        direct, passed, relayed = self._copies(outs, *sems)
        for _, piece, w, col in direct:
            self._landed(piece, *sems, w, col)
        for cp, _, _, col in passed:
            if col != self.DIAGONAL_PASSED:
                cp.start()
        for cp, _, _, _ in relayed:
            cp.start()

    def finish(self, ins, outs, sems):
        direct, passed, relayed = self._copies(outs, *sems)
        for _, piece, w, col in relayed:
            self._landed(piece, *sems, w, col)
        for cp, _, _, col in passed:
            if col == self.DIAGONAL_PASSED:
                cp.start()
        for _, piece, w, col in passed:
            self._landed(piece, *sems, w, col)
        for cp, _, _, _ in direct + passed + relayed:
            cp.wait_send()


class _ChipExchange:
    def __init__(self, sums, landing):
        self.n_out = len(landing)
        self.operands = list(sums) + list(landing)
        self.out_shape = [_hbm_like(b) for b in landing]
        self.aliases = {self.n_out + w: w for w in range(self.n_out)}
        self.sems = _chip_exchange_sems(self.n_out)

    def start(self, ins, outs, sems):
        _chip_exchange_start(ins[:self.n_out], outs, *sems)

    def finish(self, ins, outs, sems):
        _chip_exchange_finish(ins[:self.n_out], outs, *sems)


class _GatherAll:
    def __init__(self, packed):
        self.operands = [packed]
        self.n_out = 1
        self.out_shape = [_hbm_like(packed, (8,) + packed.shape)]
        self.aliases = {}
        self.sems = [pltpu.SemaphoreType.DMA((8,)), pltpu.SemaphoreType.DMA((8,))]

    def _copies(self, ins, outs, sems):
        x, y, c, _ = _mesh_place()
        me = 4 * x + 2 * y + c
        send_sems, recv_sems = sems
        copies = []
        for k in range(1, 8):
            peer = (1 - x if k // 4 else x, 1 - y if (k // 2) % 2 else y, 1 - c if k % 2 else c)
            src = 4 * peer[0] + 2 * peer[1] + peer[2]
            copies.append((_remote(ins[0], outs[0].at[me], send_sems.at[k], recv_sems.at[k], peer), outs[0].at[src]))
        own = pltpu.make_async_copy(ins[0], outs[0].at[me], send_sems.at[0])
        return own, copies

    def start(self, ins, outs, sems):
        own, copies = self._copies(ins, outs, sems)
        own.start()
        for cp, _ in copies:
            cp.start()

    def finish(self, ins, outs, sems):
        own, copies = self._copies(ins, outs, sems)
        x, y, c, _ = _mesh_place()
        for k, (cp, landed) in enumerate(copies):
            _remote(landed, landed, sems[0].at[k + 1], sems[1].at[k + 1], (x, y, c)).wait_recv()
        for cp, _ in copies:
            cp.wait_send()
        own.wait()


class _Nothing:
    operands, n_out, out_shape, aliases, sems = [], 0, [], {}, []

    def start(self, ins, outs, sems):
        pass

    def finish(self, ins, outs, sems):
        pass


class _Both:
    def __init__(self, a, b):
        self.a, self.b = a, b
        self.operands = a.operands + b.operands
        self.n_out = a.n_out + b.n_out
        self.out_shape = a.out_shape + b.out_shape
        self.aliases = dict(a.aliases)
        self.aliases.update({len(a.operands) + i: a.n_out + o for i, o in b.aliases.items()})
        self.sems = a.sems + b.sems

    def _split(self, ins, outs, sems):
        ka, na, sa = len(self.a.operands), self.a.n_out, len(self.a.sems)
        return (ins[:ka], outs[:na], sems[:sa]), (ins[ka:], outs[na:], sems[sa:])

    def start(self, ins, outs, sems):
        for ex, args in zip((self.a, self.b), self._split(ins, outs, sems)):
            ex.start(*args)

    def finish(self, ins, outs, sems):
        for ex, args in zip((self.a, self.b), self._split(ins, outs, sems)):
            ex.finish(*args)


class _SiblingExchange:
    def __init__(self, grads):
        self.operands = list(grads)
        self.n_out = len(self.operands)
        self.out_shape = [_hbm_like(g, (N_CHIP,) + g.shape[2:]) for g in grads]
        self.aliases = {}
        self.sems = _sibling_exchange_sems(self.n_out)

    def start(self, ins, outs, sems):
        _sibling_exchange_start(ins, outs, *sems)

    def finish(self, ins, outs, sems):
        _sibling_exchange_finish(ins, outs, *sems)


def _call(body, operands, *, grid, in_specs, out_specs, out_shape, name, compiler_params, scratch_shapes=(),
          exchange=None):
    operands = [o if getattr(spec, "memory_space", None) == pltpu.SMEM else _in_hbm(o)
                for o, spec in zip(operands, in_specs)]
    out_shape = [pltpu.HBM(s.shape, s.dtype) for s in out_shape]
    if exchange is None:
        res = pl.pallas_call(body, grid=grid, in_specs=in_specs, out_specs=out_specs, out_shape=out_shape, name=name,
                             scratch_shapes=list(scratch_shapes), compiler_params=compiler_params)(*operands)
        return list(res), []
    n_in, n_out, n_scr = len(in_specs), len(out_specs), len(scratch_shapes)
    k_in, k_out = len(exchange.operands), exchange.n_out

    def fused(*refs):
        ins, refs = refs[:n_in], refs[n_in:]
        ex_ins, refs = refs[:k_in], refs[k_in:]
        outs, refs = refs[:n_out], refs[n_out:]
        ex_outs, refs = refs[:k_out], refs[k_out:]
        scratch, sems = refs[:n_scr], refs[n_scr:]
        ids = [pl.program_id(a) for a in range(len(grid))]
        first = functools.reduce(jnp.logical_and, [i == 0 for i in ids])
        last = functools.reduce(jnp.logical_and, [i == g - 1 for i, g in zip(ids, grid)])

        @pl.when(first)
        def _():
            exchange.start(ex_ins, ex_outs, sems)

        if hasattr(exchange, "middle"):
            steps = math.prod(grid)
            at = (2 * steps) // 3
            place = [(at // math.prod(grid[a + 1:])) % grid[a] for a in range(len(grid))]

            @pl.when(functools.reduce(jnp.logical_and, [i == p for i, p in zip(ids, place)]))
            def _():
                exchange.middle(ex_ins, ex_outs, sems)

        body(*ins, *outs, *scratch)

        @pl.when(last)
        def _():
            exchange.finish(ex_ins, ex_outs, sems)

    res = pl.pallas_call(
        fused, grid=grid, name=name,
        in_specs=list(in_specs) + [HBM_SPEC] * k_in, out_specs=list(out_specs) + [HBM_SPEC] * k_out,
        out_shape=list(out_shape) + exchange.out_shape,
        input_output_aliases={n_in + i: n_out + o for i, o in exchange.aliases.items()},
        scratch_shapes=list(scratch_shapes) + exchange.sems, compiler_params=compiler_params,
    )(*operands, *[_in_hbm(o) for o in exchange.operands])
    return list(res[:n_out]), list(res[n_out:])


def _in_hbm(a):
    return pltpu.with_memory_space_constraint(a, pltpu.HBM)


def _row_tile(h):
    return max(t for t in range(16, 513, 16) if h % t == 0)


def _cast_shard(a, chip):
    rows, cols = a.shape
    h = rows // 2
    tr = _row_tile(h)

    def body(chip_ref, a_ref, o_ref):
        o_ref[0, 0] = a_ref[0].astype(BF16)

    return pl.pallas_call(
        body, name="cast_shard",
        grid_spec=pltpu.PrefetchScalarGridSpec(
            num_scalar_prefetch=1, grid=(2, h // tr),
            in_specs=[pl.BlockSpec((1, tr, cols), lambda s, r, chip_ref: (s, r, 0))],
            out_specs=pl.BlockSpec((1, 1, tr, cols), lambda s, r, chip_ref: (chip_ref[0], s, r, 0))),
        out_shape=pltpu.HBM((N_CHIP, 2, h, cols), BF16),
        compiler_params=_params(16, 2),
    )(chip, _in_hbm(a.reshape(2, h, cols)))


def _in_proj(x, gain1, w_in_t, exchange=None):
    tm = 256

    def body(x_ref, g_ref, w_ref, z_ref, hn_ref):
        xv = x_ref[...]
        hn = (xv * _rms_scale(xv) * g_ref[...]).astype(BF16)
        hn_ref[...] = hn
        z_ref[...] = _dot_nt(hn, w_ref[...])

    return _call(
        body, (x, gain1, w_in_t), grid=(S // tm,), name="in_proj",
        in_specs=[pl.BlockSpec((tm, D), lambda i: (i, 0)), pl.BlockSpec((1, D), lambda i: (0, 0)),
                  pl.BlockSpec((D_IN, D), lambda i: (0, 0))],
        out_specs=[pl.BlockSpec((tm, D_IN), lambda i: (i, 0)), pl.BlockSpec((tm, D), lambda i: (i, 0))],
        out_shape=[jax.ShapeDtypeStruct((S, D_IN), F32), jax.ShapeDtypeStruct((S, D), BF16)],
        compiler_params=_params(40), exchange=exchange)


def _gelu_parts(v):
    t = jnp.tanh(GELU_C * (v + 0.044715 * (v * v * v)))
    cdf = 0.5 * (1.0 + t)
    return cdf, t


def _band_mask(n):
    a = lax.broadcasted_iota(jnp.int32, (CHUNK, 2 * CHUNK), 0)
    j = lax.broadcasted_iota(jnp.int32, (CHUNK, 2 * CHUNK), 1)
    dist = CHUNK + a - j
    valid = (dist >= 0) & (dist < CHUNK)
    return valid & ((n > 0) | (j >= CHUNK))


def _fill_bias(bucket_ref, table_ref, bias_ref):
    bucket = bucket_ref[...]
    for h in range(N_HEAD):
        acc = jnp.zeros((CHUNK, 2 * CHUNK), F32)
        for b in range(N_BUCKET):
            acc = jnp.where(bucket == b, table_ref[b, h], acc)
        bias_ref[h] = acc


def _fill_tril(ws_ref, wt_ref, wtt_ref=None):
    r = lax.broadcasted_iota(jnp.int32, (CHUNK, CHUNK), 0)
    c = lax.broadcasted_iota(jnp.int32, (CHUNK, CHUNK), 1)
    for g in range(N_GROUP):
        w = jnp.where(c <= r, ws_ref[g], 0.0)
        wt_ref[g] = w.astype(BF16)
        if wtt_ref is not None:
            wtt_ref[g] = w.T.astype(BF16)


def _kv_layouts(kv_prev, kv_cur):
    both = jnp.concatenate([kv_prev, kv_cur], axis=0)
    k = both[:, :128]
    v = both[:, 128:]
    return (k.astype(BF16), pltpu.roll(k, 64, axis=1).astype(BF16),
            v.astype(BF16), pltpu.roll(v, 64, axis=1).astype(BF16))


def _head_place(h):
    pair, pos, kvh = h // 2, h % 2, h // 4
    return pair, pos, kvh == pos


def _softmax_sink(qm, k_use, bias_h, sink, valid):
    s = _dot_nt(qm, k_use) * QK_SCALE + bias_h
    s = jnp.where(valid, s, NEG_INF)
    m = jnp.maximum(jnp.max(s, axis=-1, keepdims=True), sink)
    e = jnp.exp(s - m)
    es = jnp.exp(sink - m)
    denom = jnp.sum(e, axis=-1, keepdims=True) + es
    return e / denom, es / denom


def _mixer_fwd(z, v_gain, w_spatial, b_spatial_t, sinks, rel_table, bucket, exchange=None):
    def body(z_ref, kvp_ref, gain_ref, ws_ref, bt_ref, sink_ref, table_ref, bucket_ref, out_ref, bias_ref, wt_ref):
        n = pl.program_id(0)

        @pl.when(n == 0)
        def _():
            _fill_bias(bucket_ref, table_ref, bias_ref)
            _fill_tril(ws_ref, wt_ref)

        zuv = z_ref[:, :1024]
        cdf, _ = _gelu_parts(zuv)
        guv = zuv * cdf
        for g in range(N_GROUP):
            vg = guv[:, 512 + 128 * g:512 + 128 * (g + 1)]
            vn = vg * _rms_scale(vg) * gain_ref[:, 128 * g:128 * (g + 1)]
            sv = _dot(wt_ref[g], vn.astype(BF16)) + bt_ref[:, g:g + 1]
            out_ref[:, 128 * g:128 * (g + 1)] = (guv[:, 128 * g:128 * (g + 1)] * sv).astype(BF16)

        k_same, k_swap, v_same, v_swap = _kv_layouts(kvp_ref[...], z_ref[:, 1536:1792])
        valid = _band_mask(n)
        lane_half = lax.broadcasted_iota(jnp.int32, (1, 128), 1) // 64
        for pair in range(N_HEAD // 2):
            qq = z_ref[:, 1024 + 128 * pair:1024 + 128 * (pair + 1)]
            acc = jnp.zeros((CHUNK, 128), F32)
            for pos in range(2):
                h = 2 * pair + pos
                _, _, same = _head_place(h)
                qm = jnp.where(lane_half == pos, qq, 0.0).astype(BF16)
                p, _ = _softmax_sink(qm, k_same if same else k_swap, bias_ref[h], sink_ref[h], valid)
                vm = jnp.where(lane_half == pos, v_same if same else v_swap, jnp.zeros((), BF16))
                acc = acc + _dot(p.astype(BF16), vm)
            out_ref[:, 512 + 128 * pair:512 + 128 * (pair + 1)] = acc.astype(BF16)

    return _call(
        body, (z, z, v_gain, w_spatial, b_spatial_t, sinks, rel_table, bucket), grid=(N_BLOCK,), name="mixer_fwd",
        in_specs=[pl.BlockSpec((CHUNK, D_IN), lambda n: (n, 0)),
                  pl.BlockSpec((CHUNK, 256), lambda n: (jnp.maximum(n - 1, 0), 6)),
                  pl.BlockSpec((1, 512), lambda n: (0, 0)),
                  pl.BlockSpec((N_GROUP, CHUNK, CHUNK), lambda n: (0, 0, 0)),
                  pl.BlockSpec((CHUNK, N_GROUP), lambda n: (0, 0)),
                  pl.BlockSpec(memory_space=pltpu.SMEM),
                  pl.BlockSpec(memory_space=pltpu.SMEM),
                  pl.BlockSpec((CHUNK, 2 * CHUNK), lambda n: (0, 0))],
        out_specs=[pl.BlockSpec((CHUNK, D), lambda n: (n, 0))],
        out_shape=[jax.ShapeDtypeStruct((S, D), BF16)],
        scratch_shapes=[pltpu.VMEM((N_HEAD, CHUNK, 2 * CHUNK), F32), pltpu.VMEM((N_GROUP, CHUNK, CHUNK), BF16)],
        compiler_params=_params(32), exchange=exchange)


def _out_proj(x, mix, w_out, gain2, exchange=None):
    tm = 256

    def body(x_ref, mix_ref, w_ref, g_ref, h1_ref, hn_ref, hnt_ref):
        h1 = x_ref[...] + _dot(mix_ref[...], w_ref[...])
        h1_ref[...] = h1
        hn = h1 * _rms_scale(h1) * g_ref[...]
        hn_ref[...] = hn.astype(BF16)
        hnt_ref[...] = hn.T.astype(BF16)

    return _call(
        body, (x, mix, w_out, gain2), grid=(S // tm,), name="out_proj",
        in_specs=[pl.BlockSpec((tm, D), lambda i: (i, 0)), pl.BlockSpec((tm, D), lambda i: (i, 0)),
                  pl.BlockSpec((D, D), lambda i: (0, 0)), pl.BlockSpec((1, D), lambda i: (0, 0))],
        out_specs=[pl.BlockSpec((tm, D), lambda i: (i, 0)), pl.BlockSpec((tm, D), lambda i: (i, 0)),
                   pl.BlockSpec((D, tm), lambda i: (0, i))],
        out_shape=[jax.ShapeDtypeStruct((S, D), F32), jax.ShapeDtypeStruct((S, D), BF16),
                   jax.ShapeDtypeStruct((D, S), BF16)],
        compiler_params=_params(32), exchange=exchange)


def _ffn_up(hn2, w_ff1, exchange=None):
    tm = 512
    nj = D_FF // 1024

    def body(hn_ref, w1_ref, r_ref, a_ref, at_ref):
        r = jnp.maximum(_dot(hn_ref[...], w1_ref[0]), 0.0)
        r_ref[...] = r.astype(BF16)
        a = r * r
        a_ref[...] = a.astype(BF16)
        at_ref[...] = a.T.astype(BF16)

    return _call(
        body, (hn2, w_ff1), grid=(nj, S // tm), name="ffn_up",
        in_specs=[pl.BlockSpec((tm, D), lambda j, i: (i, 0)), pl.BlockSpec((1, D, 1024), lambda j, i: (j, 0, 0))],
        out_specs=[pl.BlockSpec((tm, 1024), lambda j, i: (i, j)), pl.BlockSpec((tm, 1024), lambda j, i: (i, j)),
                   pl.BlockSpec((1024, tm), lambda j, i: (j, i))],
        out_shape=[jax.ShapeDtypeStruct((S, D_FF), BF16), jax.ShapeDtypeStruct((S, D_FF), BF16),
                   jax.ShapeDtypeStruct((D_FF, S), BF16)],
        compiler_params=_params(40, 2), exchange=exchange)


def _ffn_down(h1, a, w_ff2, exchange=None):
    tm = 1024
    nj = D_FF // 1024

    def body(h1_ref, a_ref, w2_ref, h2_ref, acc_ref):
        j = pl.program_id(1)
        part = _dot(a_ref[...], w2_ref[0])

        @pl.when(j == 0)
        def _():
            acc_ref[...] = part

        @pl.when(j > 0)
        def _():
            acc_ref[...] += part

        @pl.when(j == nj - 1)
        def _():
            h2_ref[...] = h1_ref[...] + acc_ref[...]

    return _call(
        body, (h1, a, w_ff2), grid=(S // tm, nj), name="ffn_down",
        in_specs=[pl.BlockSpec((tm, D), lambda i, j: (i, 0)), pl.BlockSpec((tm, 1024), lambda i, j: (i, j)),
                  pl.BlockSpec((1, 1024, D), lambda i, j: (j, 0, 0))],
        out_specs=[pl.BlockSpec((tm, D), lambda i, j: (i, 0))],
        out_shape=[jax.ShapeDtypeStruct((S, D), F32)],
        scratch_shapes=[pltpu.VMEM((tm, D), F32)],
        compiler_params=_params(48, 2), exchange=exchange)


def _tail(h2, p, target, w_gate, w_proj, final_gain):
    tm = 256
    steps = S // tm

    def body(h2_ref, p_ref, t_ref, wg_ref, wp_ref, gf_ref, dh2_ref, dwg_ref, dwp_ref, dgf_ref, loss_ref, dh2b_ref,
             dwp_acc):
        i = pl.program_id(0)
        h2 = h2_ref[...]
        h2b = h2.astype(BF16)
        pb = p_ref[...].astype(BF16)
        gate = jax.nn.sigmoid(_dot(h2b, wg_ref[...]))
        pp = jnp.concatenate([_dot(pb, wp_ref[j]) for j in range(N_CHIP)], axis=1)
        h3 = h2 + gate * pp
        r3 = _rms_scale(h3)
        xhat = h3 * r3
        gf = gf_ref[...]
        err = xhat * gf - t_ref[...]
        dy = err * (1.0 / D)
        dh3 = _rms_bwd(dy * gf, xhat, r3)
        dgp = (dh3 * pp * gate * (1.0 - gate)).astype(BF16)
        dpp = (dh3 * gate).astype(BF16)
        dh2 = dh3 + _dot_nt(dgp, wg_ref[...])
        dh2_ref[...] = dh2
        dh2b_ref[...] = dh2.astype(BF16)
        dwg = _dot_tn(h2b, dgp)
        dwp = _dot_tn(pb, dpp)
        dgf = jnp.sum(dy * xhat, axis=0, keepdims=True)
        sq = jnp.sum(jnp.sum(err * err, axis=1, keepdims=True), axis=0, keepdims=True)

        @pl.when(i == 0)
        def _():
            dwg_ref[...] = dwg
            dwp_acc[...] = dwp
            dgf_ref[...] = dgf
            loss_ref[...] = jnp.broadcast_to(sq, (8, 128))

        @pl.when(i > 0)
        def _():
            dwg_ref[...] += dwg
            dwp_acc[...] += dwp
            dgf_ref[...] += dgf
            loss_ref[...] += jnp.broadcast_to(sq, (8, 128))

        @pl.when(i == steps - 1)
        def _():
            for j in range(N_CHIP):
                dwp_ref[j] = dwp_acc[:, 256 * j:256 * (j + 1)]

    return _call(
        body, (h2, p, target, w_gate, w_proj, final_gain), grid=(steps,), name="tail",
        in_specs=[pl.BlockSpec((tm, D), lambda i: (i, 0)), pl.BlockSpec((tm, PLE), lambda i: (i, 0)),
                  pl.BlockSpec((tm, D), lambda i: (i, 0)), pl.BlockSpec((D, D), lambda i: (0, 0)),
                  pl.BlockSpec((N_CHIP, PLE, 256), lambda i: (0, 0, 0)), pl.BlockSpec((1, D), lambda i: (0, 0))],
        out_specs=[pl.BlockSpec((tm, D), lambda i: (i, 0)), pl.BlockSpec((D, D), lambda i: (0, 0)),
                   pl.BlockSpec((N_CHIP, PLE, 256), lambda i: (0, 0, 0)), pl.BlockSpec((1, D), lambda i: (0, 0)),
                   pl.BlockSpec((8, 128), lambda i: (0, 0)), pl.BlockSpec((tm, D), lambda i: (i, 0))],
        out_shape=[jax.ShapeDtypeStruct((S, D), F32), jax.ShapeDtypeStruct((D, D), F32),
                   jax.ShapeDtypeStruct((N_CHIP, PLE, 256), F32), jax.ShapeDtypeStruct((1, D), F32),
                   jax.ShapeDtypeStruct((8, 128), F32), jax.ShapeDtypeStruct((S, D), BF16)],
        scratch_shapes=[pltpu.VMEM((PLE, D), F32)],
        compiler_params=_params(48))[0]


def _ffn_bwd_down(dh2b, r, a_t, w_ff2, exchange=None):
    tm = 1024
    nj = D_FF // 1024

    def body(dh2_ref, r_ref, at_ref, w2_ref, df_ref, dw2_ref):
        i = pl.program_id(1)
        dh2b = dh2_ref[...]
        da = _dot_nt(dh2b, w2_ref[0])
        df_ref[...] = (da * (2.0 * r_ref[...].astype(F32))).astype(BF16)
        dw2 = _dot(at_ref[...], dh2b)

        @pl.when(i == 0)
        def _():
            dw2_ref[0] = dw2

        @pl.when(i > 0)
        def _():
            dw2_ref[0] += dw2

    return _call(
        body, (dh2b, r, a_t, w_ff2), grid=(nj, S // tm), name="ffn_bwd_down",
        in_specs=[pl.BlockSpec((tm, D), lambda j, i: (i, 0)), pl.BlockSpec((tm, 1024), lambda j, i: (i, j)),
                  pl.BlockSpec((1024, tm), lambda j, i: (j, i)), pl.BlockSpec((1, 1024, D), lambda j, i: (j, 0, 0))],
        out_specs=[pl.BlockSpec((tm, 1024), lambda j, i: (i, j)), pl.BlockSpec((1, 1024, D), lambda j, i: (j, 0, 0))],
        out_shape=[jax.ShapeDtypeStruct((S, D_FF), BF16), jax.ShapeDtypeStruct((nj, 1024, D), F32)],
        compiler_params=_params(48, 2), exchange=exchange)


def _ffn_bwd_up(df, hn2_t, exchange=None):
    tm = 1024
    nj = D_FF // 1024

    def body(df_ref, hnt_ref, dw1_ref):
        i = pl.program_id(1)
        dw1 = _dot(hnt_ref[...], df_ref[...])

        @pl.when(i == 0)
        def _():
            dw1_ref[0] = dw1

        @pl.when(i > 0)
        def _():
            dw1_ref[0] += dw1

    return _call(
        body, (df, hn2_t), grid=(nj, S // tm), name="ffn_bwd_up",
        in_specs=[pl.BlockSpec((tm, 1024), lambda j, i: (i, j)), pl.BlockSpec((D, tm), lambda j, i: (0, i))],
        out_specs=[pl.BlockSpec((1, D, 1024), lambda j, i: (j, 0, 0))],
        out_shape=[jax.ShapeDtypeStruct((nj, D, 1024), F32)],
        compiler_params=_params(40, 2), exchange=exchange)


def _ffn_bwd_input(df, w_ff1, dh2, h1, gain2, mix, w_out, exchange=None):
    tm = 512
    nj = D_FF // 1024
    steps = S // tm

    def body(df_ref, w1_ref, dh2_ref, h1_ref, g_ref, mix_ref, wo_ref, dh1_ref, dmix_ref, dwo_ref, dg_ref, acc_ref):
        i = pl.program_id(0)
        j = pl.program_id(1)
        part = _dot_nt(df_ref[...], w1_ref[0])

        @pl.when(j == 0)
        def _():
            acc_ref[...] = part

        @pl.when(j > 0)
        def _():
            acc_ref[...] += part

        @pl.when(j == nj - 1)
        def _():
            dhn = acc_ref[...]
            h1 = h1_ref[...]
            r2 = _rms_scale(h1)
            xhat = h1 * r2
            dh1 = dh2_ref[...] + _rms_bwd(dhn * g_ref[...], xhat, r2)
            dh1_ref[...] = dh1
            dh1b = dh1.astype(BF16)
            dmix_ref[...] = _dot_nt(dh1b, wo_ref[...])
            dwo = _dot_tn(mix_ref[...], dh1b)
            dg = jnp.sum(dhn * xhat, axis=0, keepdims=True)

            @pl.when(i == 0)
            def _():
                dwo_ref[...] = dwo
                dg_ref[...] = dg

            @pl.when(i > 0)
            def _():
                dwo_ref[...] += dwo
                dg_ref[...] += dg

    return _call(
        body, (df, w_ff1, dh2, h1, gain2, mix, w_out), grid=(steps, nj), name="ffn_bwd_input",
        in_specs=[pl.BlockSpec((tm, 1024), lambda i, j: (i, j)), pl.BlockSpec((1, D, 1024), lambda i, j: (j, 0, 0)),
                  pl.BlockSpec((tm, D), lambda i, j: (i, 0)), pl.BlockSpec((tm, D), lambda i, j: (i, 0)),
                  pl.BlockSpec((1, D), lambda i, j: (0, 0)), pl.BlockSpec((tm, D), lambda i, j: (i, 0)),
                  pl.BlockSpec((D, D), lambda i, j: (0, 0))],
        out_specs=[pl.BlockSpec((tm, D), lambda i, j: (i, 0)), pl.BlockSpec((tm, D), lambda i, j: (i, 0)),
                   pl.BlockSpec((D, D), lambda i, j: (0, 0)), pl.BlockSpec((1, D), lambda i, j: (0, 0))],
        out_shape=[jax.ShapeDtypeStruct((S, D), F32), jax.ShapeDtypeStruct((S, D), F32),
                   jax.ShapeDtypeStruct((D, D), F32), jax.ShapeDtypeStruct((1, D), F32)],
        scratch_shapes=[pltpu.VMEM((tm, D), F32)],
        compiler_params=_params(56, 2), exchange=exchange)


IN_GROUP = 8


def _mixer_bwd(z, dmix, v_gain, w_spatial, b_spatial_t, sinks, rel_table, bucket, hn1, exchange=None):
    def body(z_ref, kvp_ref, dm_ref, gain_ref, ws_ref, bt_ref, sink_ref, table_ref, bucket_ref, hn_ref,
             dz_ref, dws_ref, db_ref, dgain_ref, dsink_ref, drel_ref, dwin_ref,
             bias_ref, wt_ref, wtt_ref, dbias_ref, dsv_ref, carry_ref):
        n = pl.program_id(0)

        @pl.when(n == 0)
        def _():
            _fill_bias(bucket_ref, table_ref, bias_ref)
            _fill_tril(ws_ref, wt_ref, wtt_ref)
            dwin_ref[...] = jnp.zeros_like(dwin_ref)
            dbias_ref[...] = jnp.zeros_like(dbias_ref)
            dsv_ref[...] = jnp.zeros_like(dsv_ref)
            dws_ref[...] = jnp.zeros_like(dws_ref)
            dgain_ref[...] = jnp.zeros_like(dgain_ref)
            dsink_ref[...] = jnp.zeros_like(dsink_ref)

        rows = pl.ds(pl.multiple_of(n * CHUNK, CHUNK), CHUNK)

        zuv = z_ref[:, :1024]
        cdf, t = _gelu_parts(zuv)
        guv = zuv * cdf
        dgelu = cdf + zuv * (0.5 * (1.0 - t * t)) * (GELU_C * (1.0 + 3.0 * 0.044715 * (zuv * zuv)))
        for g in range(N_GROUP):
            lo, hi = 128 * g, 128 * (g + 1)
            u = guv[:, lo:hi]
            vg = guv[:, 512 + lo:512 + hi]
            rr = _rms_scale(vg)
            vhat = vg * rr
            gain = gain_ref[:, lo:hi]
            vnb = (vhat * gain).astype(BF16)
            sv = _dot(wt_ref[g], vnb) + bt_ref[:, g:g + 1]
            da = dm_ref[:, lo:hi]
            dsv = da * u
            dsvb = dsv.astype(BF16)
            dsv_ref[g] += dsv
            dws_ref[g] += _dot_nt(dsvb, vnb)
            dvn = _dot(wtt_ref[g], dsvb)
            dgain_ref[:, lo:hi] += jnp.sum(dvn * vhat, axis=0, keepdims=True)
            dvg = _rms_bwd(dvn * gain, vhat, rr)
            dz_ref[rows, lo:hi] = (da * sv * dgelu[:, lo:hi]).astype(BF16)
            dz_ref[rows, 512 + lo:512 + hi] = (dvg * dgelu[:, 512 + lo:512 + hi]).astype(BF16)

        k_same, k_swap, v_same, v_swap = _kv_layouts(kvp_ref[...], z_ref[:, 1536:1792])
        valid = _band_mask(n)
        lane_half = lax.broadcasted_iota(jnp.int32, (1, 128), 1) // 64
        zero = jnp.zeros((2 * CHUNK, 128), F32)
        dk_same, dk_swap, dv_same, dv_swap = zero, zero, zero, zero
        for pair in range(N_HEAD // 2):
            cols = slice(1024 + 128 * pair, 1024 + 128 * (pair + 1))
            qq = z_ref[:, cols]
            do_pair = dm_ref[:, 512 + 128 * pair:512 + 128 * (pair + 1)]
            dq = jnp.zeros((CHUNK, 128), F32)
            for pos in range(2):
                h = 2 * pair + pos
                _, _, same = _head_place(h)
                on_half = lane_half == pos
                qm = jnp.where(on_half, qq, 0.0).astype(BF16)
                k_use = k_same if same else k_swap
                v_use = v_same if same else v_swap
                p, p_sink = _softmax_sink(qm, k_use, bias_ref[h], sink_ref[h], valid)
                dom = jnp.where(on_half, do_pair, 0.0).astype(BF16)
                dp = _dot_nt(dom, v_use)
                dsum = jnp.sum(p * dp, axis=-1, keepdims=True)
                ds = p * (dp - dsum)
                dbias_ref[h] += ds
                dsink_ref[h:h + 1, :] += jnp.broadcast_to(jnp.sum(-p_sink * dsum, axis=0, keepdims=True), (1, 128))
                dsb = ds.astype(BF16)
                dq = dq + jnp.where(on_half, _dot(dsb, k_use), 0.0)
                dk_h = _dot_tn(dsb, qm)
                dv_h = _dot_tn(p.astype(BF16), dom)
                if same:
                    dk_same, dv_same = dk_same + dk_h, dv_same + dv_h
                else:
                    dk_swap, dv_swap = dk_swap + dk_h, dv_swap + dv_h
            dz_ref[rows, cols] = (dq * QK_SCALE).astype(BF16)
        dk = (dk_same + pltpu.roll(dk_swap, 64, axis=1)) * QK_SCALE
        dv = dv_same + pltpu.roll(dv_swap, 64, axis=1)
        dkv = jnp.concatenate([dk, dv], axis=1)

        @pl.when(n > 0)
        def _():
            prev_rows = pl.ds(pl.multiple_of((n - 1) * CHUNK, CHUNK), CHUNK)
            dz_ref[prev_rows, 1536:1792] = (carry_ref[...] + dkv[:CHUNK]).astype(BF16)

        carry_ref[...] = dkv[CHUNK:]

        @pl.when((n > 0) & (n % IN_GROUP == 0))
        def _():
            done = pl.ds(pl.multiple_of((n - IN_GROUP) * CHUNK, IN_GROUP * CHUNK), IN_GROUP * CHUNK)
            dwin_ref[...] += _dot_tn(dz_ref[done, :], hn_ref[...])

        @pl.when(n == N_BLOCK - 1)
        def _():
            dz_ref[rows, 1536:1792] = dkv[CHUNK:].astype(BF16)
            last = pl.ds((N_BLOCK - IN_GROUP) * CHUNK, IN_GROUP * CHUNK)
            dwin_ref[...] += _dot_tn(dz_ref[last, :], hn_ref[...])
            r = lax.broadcasted_iota(jnp.int32, (CHUNK, CHUNK), 0)
            c = lax.broadcasted_iota(jnp.int32, (CHUNK, CHUNK), 1)
            for g in range(N_GROUP):
                dws_ref[g] = jnp.where(c <= r, dws_ref[g], 0.0)
                db_ref[g] = jnp.sum(dsv_ref[g], axis=1, keepdims=True)
            bucket = bucket_ref[...]
            for h in range(N_HEAD):
                dbh = dbias_ref[h]
                per_bucket = [jnp.sum(jnp.where(bucket == b, dbh, 0.0), axis=0, keepdims=True) for b in range(N_BUCKET)]
                drel_ref[h] = jnp.sum(jnp.concatenate(per_bucket, axis=0), axis=1, keepdims=True)

    def hn_group(n):
        return jnp.where(n == N_BLOCK - 1, N_BLOCK // IN_GROUP - 1, jnp.maximum(n // IN_GROUP - 1, 0))

    return _call(
        body, (z, z, dmix, v_gain, w_spatial, b_spatial_t, sinks, rel_table, bucket, hn1), grid=(N_BLOCK,),
        name="mixer_bwd",
        in_specs=[pl.BlockSpec((CHUNK, D_IN), lambda n: (n, 0)),
                  pl.BlockSpec((CHUNK, 256), lambda n: (jnp.maximum(n - 1, 0), 6)),
                  pl.BlockSpec((CHUNK, D), lambda n: (n, 0)),
                  pl.BlockSpec((1, 512), lambda n: (0, 0)),
                  pl.BlockSpec((N_GROUP, CHUNK, CHUNK), lambda n: (0, 0, 0)),
                  pl.BlockSpec((CHUNK, N_GROUP), lambda n: (0, 0)),
                  pl.BlockSpec(memory_space=pltpu.SMEM),
                  pl.BlockSpec(memory_space=pltpu.SMEM),
                  pl.BlockSpec((CHUNK, 2 * CHUNK), lambda n: (0, 0)),
                  pl.BlockSpec((IN_GROUP * CHUNK, D), lambda n: (hn_group(n), 0))],
        out_specs=[pl.BlockSpec((S, D_IN), lambda n: (0, 0)),
                   pl.BlockSpec((N_GROUP, CHUNK, CHUNK), lambda n: (0, 0, 0)),
                   pl.BlockSpec((N_GROUP, CHUNK, 1), lambda n: (0, 0, 0)),
                   pl.BlockSpec((1, 512), lambda n: (0, 0)),
                   pl.BlockSpec((N_HEAD, 128), lambda n: (0, 0)),
                   pl.BlockSpec((N_HEAD, N_BUCKET, 1), lambda n: (0, 0, 0)),
                   pl.BlockSpec((D_IN, D), lambda n: (0, 0))],
        out_shape=[jax.ShapeDtypeStruct((S, D_IN), BF16), jax.ShapeDtypeStruct((N_GROUP, CHUNK, CHUNK), F32),
                   jax.ShapeDtypeStruct((N_GROUP, CHUNK, 1), F32), jax.ShapeDtypeStruct((1, 512), F32),
                   jax.ShapeDtypeStruct((N_HEAD, 128), F32), jax.ShapeDtypeStruct((N_HEAD, N_BUCKET, 1), F32),
                   jax.ShapeDtypeStruct((D_IN, D), F32)],
        scratch_shapes=[pltpu.VMEM((N_HEAD, CHUNK, 2 * CHUNK), F32), pltpu.VMEM((N_GROUP, CHUNK, CHUNK), BF16),
                        pltpu.VMEM((N_GROUP, CHUNK, CHUNK), BF16), pltpu.VMEM((N_HEAD, CHUNK, 2 * CHUNK), F32),
                        pltpu.VMEM((N_GROUP, CHUNK, CHUNK), F32), pltpu.VMEM((CHUNK, 256), F32)],
        compiler_params=_params(56), exchange=exchange)


def _in_bwd_input(dz, w_in_t, x, dh1, gain1, exchange=None):
    tm = 512

    def body(dz_ref, w_ref, x_ref, dh1_ref, g_ref, dx_ref, dg_ref):
        i = pl.program_id(0)
        dhn = _dot(dz_ref[...], w_ref[...])
        xv = x_ref[...]
        r1 = _rms_scale(xv)
        xhat = xv * r1
        dx_ref[...] = dh1_ref[...] + _rms_bwd(dhn * g_ref[...], xhat, r1)
        dg = jnp.sum(dhn * xhat, axis=0, keepdims=True)

        @pl.when(i == 0)
        def _():
            dg_ref[...] = dg

        @pl.when(i > 0)
        def _():
            dg_ref[...] += dg

    return _call(
        body, (dz, w_in_t, x, dh1, gain1), grid=(S // tm,), name="in_bwd_input",
        in_specs=[pl.BlockSpec((tm, D_IN), lambda i: (i, 0)), pl.BlockSpec((D_IN, D), lambda i: (0, 0)),
                  pl.BlockSpec((tm, D), lambda i: (i, 0)), pl.BlockSpec((tm, D), lambda i: (i, 0)),
                  pl.BlockSpec((1, D), lambda i: (0, 0))],
        out_specs=[pl.BlockSpec((tm, D), lambda i: (i, 0)), pl.BlockSpec((1, D), lambda i: (0, 0))],
        out_shape=[jax.ShapeDtypeStruct((S, D), F32), jax.ShapeDtypeStruct((1, D), F32)],
        compiler_params=_params(48), exchange=exchange)


def _rel_bucket():
    a = jnp.arange(CHUNK)[:, None]
    j = jnp.arange(2 * CHUNK)[None, :]
    n = jnp.maximum(CHUNK + a - j, 0)
    max_exact = N_BUCKET // 2
    nf = jnp.maximum(n, 1).astype(jnp.float32)
    large = max_exact + (jnp.log(nf / max_exact) / math.log(CHUNK / max_exact) * (N_BUCKET - max_exact)).astype(jnp.int32)
    large = jnp.minimum(large, N_BUCKET - 1)
    return jnp.where(n < max_exact, n, large).astype(jnp.int32)


def _step(x, p, target, small, bufs, place):
    bucket = _rel_bucket()
    sinks = small["attn_sinks"].reshape(N_HEAD)
    b_t = jnp.transpose(small["b_spatial"].reshape(N_GROUP, CHUNK))
    ws = small["w_spatial"].reshape(N_GROUP, CHUNK, CHUNK)
    gain1, gain2 = small["norm1_gain"], small["norm2_gain"]
    v_gain = small["gmlp_v_gain"]
    final_gain = small["final_gain"].reshape(1, D)
    table = small["rel_bias_table"]
    bufs = dict(bufs)

    def gather(*names):
        return _RelayGather([bufs[n] for n in names])

    def took(names, got):
        bufs.update(zip(names, got))

    took(["w_in"], _gather_weights([bufs["w_in"]]))
    w_in_t = _whole(bufs["w_in"]).reshape(D_IN, D)
    (z, hn1), got = _in_proj(x, gain1, w_in_t, gather("w_out"))
    took(["w_out"], got)
    (mix,), got = _mixer_fwd(z, v_gain, ws, b_t, sinks, table, bucket, gather("w_ff1"))
    took(["w_ff1"], got)
    w_out = _whole(bufs["w_out"]).reshape(D, D)
    (h1, hn2, hn2_t), _ = _out_proj(x, mix, w_out, gain2)
    w_ff1 = _whole(bufs["w_ff1"])
    (r, a, a_t), got = _ffn_up(hn2, w_ff1, gather("w_ff2"))
    took(["w_ff2"], got)
    w_ff2 = _whole(bufs["w_ff2"])
    (h2,), got = _ffn_down(h1, a, w_ff2, gather("w_ple_gate", "w_ple_proj"))
    took(["w_ple_gate", "w_ple_proj"], got)
    dh2, d_gate, d_proj, d_final, sq, dh2b = _tail(h2, p, target, _whole(bufs["w_ple_gate"]).reshape(D, D),
                                                   _whole(bufs["w_ple_proj"]), final_gain)

    def pair_sums(halves, from_sibling):
        sums, landing = zip(*[_pair_sum(g, o, place) for g, o in zip(halves, from_sibling)])
        return list(sums), list(landing)

    landed = {}
    halves = [_halves(d_gate.reshape(N_CHIP, 256, D)), _halves(d_proj)]
    (df, d_ff2), got = _ffn_bwd_down(dh2b, r, a_t, w_ff2, _SiblingExchange(halves))
    ex, halves = _ChipExchange(*pair_sums(halves, got)), [_halves(d_ff2)]
    (d_ff1,), got = _ffn_bwd_up(df, hn2_t, _Both(ex, _SiblingExchange(halves)))
    landed.update(zip(["w_ple_gate", "w_ple_proj"], got[:2]))
    ex, halves = _ChipExchange(*pair_sums(halves, got[2:])), [_halves(d_ff1)]
    (dh1, dmix, d_out, d_gain2), got = _ffn_bwd_input(df, w_ff1, dh2, h1, gain2, mix, w_out,
                                                      _Both(ex, _SiblingExchange(halves)))
    landed["w_ff2"] = got[0]
    ex, halves = _ChipExchange(*pair_sums(halves, got[1:])), [_halves(d_out.reshape(N_CHIP, 256, D))]
    (dz, d_ws, d_b, d_vgain, d_sink, d_rel, d_in_t), got = _mixer_bwd(z, dmix, v_gain, ws, b_t, sinks, table, bucket, hn1,
                                                                     _Both(ex, _SiblingExchange(halves)))
    landed["w_ff1"] = got[0]
    small_grads = {
        "gmlp_v_gain": d_vgain, "w_spatial": d_ws.reshape(1, N_GROUP, CHUNK, CHUNK),
        "b_spatial": d_b.reshape(1, N_GROUP, CHUNK), "attn_sinks": d_sink[:, 0].reshape(1, N_HEAD),
        "rel_bias_table": jnp.transpose(d_rel.reshape(N_HEAD, N_BUCKET)), "norm2_gain": d_gain2,
        "final_gain": d_final.reshape(D),
    }
    ex, halves = _ChipExchange(*pair_sums(halves, got[1:])), [_halves(d_in_t.reshape(N_CHIP, 448, D))]
    (dx, small_grads["norm1_gain"]), got = _in_bwd_input(dz, w_in_t, x, dh1, gain1, _Both(ex, _SiblingExchange(halves)))
    landed["w_out"] = got[0]
    return dx, landed, _ChipExchange(*pair_sums(halves, got[1:])), small_grads, sq


HBM_SPEC = pl.BlockSpec(memory_space=pltpu.HBM)
VMEM_SPEC = pl.BlockSpec(memory_space=pltpu.VMEM)


def _mesh_place():
    x, y, c = lax.axis_index("x"), lax.axis_index("y"), lax.axis_index("c")
    others = [(1 - x, y), (x, 1 - y), (1 - x, 1 - y)]
    return x, y, c, others


def _remote(src, dst, send_sem, recv_sem, device):
    return pltpu.make_async_remote_copy(src_ref=src, dst_ref=dst, send_sem=send_sem, recv_sem=recv_sem,
                                        device_id=device, device_id_type=MESH)


def _hbm_like(a, shape=None, dtype=None):
    return pltpu.HBM(a.shape if shape is None else shape, a.dtype if dtype is None else dtype)


def _gather_start(bufs, send_sems, recv_sems):
    x, y, c, others = _mesh_place()
    me = 2 * x + y
    for w, buf in enumerate(bufs):
        for k in range(3):
            mine = buf.at[me, c]
            _remote(mine, mine, send_sems.at[w, k], recv_sems.at[w, k], (*others[k], c)).start()


def _gather_finish(bufs, send_sems, recv_sems):
    x, y, c, others = _mesh_place()
    me = 2 * x + y
    sibling = (x, y, 1 - c)
    idx = [2 * ox + oy for ox, oy in others]
    chips = range(3)
    for w, buf in enumerate(bufs):
        for k in chips:
            landed = buf.at[idx[k], c]
            _remote(landed, landed, send_sems.at[w, k], recv_sems.at[w, k], sibling).wait_recv()
            _remote(landed, landed, send_sems.at[w, 3 + k], recv_sems.at[w, 3 + k], sibling).start()
    for w, buf in enumerate(bufs):
        for k in chips:
            landed = buf.at[idx[k], 1 - c]
            _remote(landed, landed, send_sems.at[w, 3 + k], recv_sems.at[w, 3 + k], sibling).wait_recv()
    for w, buf in enumerate(bufs):
        for k in chips:
            mine, passed = buf.at[me, c], buf.at[idx[k], c]
            _remote(mine, mine, send_sems.at[w, k], recv_sems.at[w, k], sibling).wait_send()
            _remote(passed, passed, send_sems.at[w, 3 + k], recv_sems.at[w, 3 + k], sibling).wait_send()


def _gather_sems(n):
    return [pltpu.SemaphoreType.DMA((n, 6)), pltpu.SemaphoreType.DMA((n, 6))]


def _gather_weights(bufs):
    n = len(bufs)

    def body(*refs):
        outs = refs[n:2 * n]
        send_sems, recv_sems = refs[2 * n:]
        _gather_start(outs, send_sems, recv_sems)
        _gather_finish(outs, send_sems, recv_sems)

    return pl.pallas_call(
        body, name="gather_weights",
        in_specs=[HBM_SPEC] * n, out_specs=[HBM_SPEC] * n,
        out_shape=[_hbm_like(b) for b in bufs],
        input_output_aliases={w: w for w in range(n)},
        scratch_shapes=_gather_sems(n),
    )(*bufs)


def _sibling_copies(grads, landing, send_sems, recv_sems):
    x, y, c, _ = _mesh_place()
    return [_remote(grads[w].at[j, 1 - c], landing[w].at[j], send_sems.at[w, j], recv_sems.at[w, j], (x, y, 1 - c))
            for w in range(len(grads)) for j in range(N_CHIP)]


def _sibling_exchange_start(grads, landing, send_sems, recv_sems):
    for cp in _sibling_copies(grads, landing, send_sems, recv_sems):
        cp.start()


def _sibling_exchange_finish(grads, landing, send_sems, recv_sems):
    copies = _sibling_copies(grads, landing, send_sems, recv_sems)
    for cp in copies:
        cp.wait_recv()
    for cp in copies:
        cp.wait_send()


def _sibling_exchange_sems(n):
    return [pltpu.SemaphoreType.DMA((n, N_CHIP)), pltpu.SemaphoreType.DMA((n, N_CHIP))]


def _sibling_exchange(grads):
    n = len(grads)

    def body(*refs):
        ins, outs = refs[:n], refs[n:2 * n]
        _sibling_exchange_start(ins, outs, *refs[2 * n:])
        _sibling_exchange_finish(ins, outs, *refs[2 * n:])

    return pl.pallas_call(
        body, name="sibling_exchange",
        in_specs=[HBM_SPEC] * n, out_specs=[HBM_SPEC] * n,
        out_shape=[_hbm_like(g, (N_CHIP,) + g.shape[2:]) for g in grads],
        scratch_shapes=_sibling_exchange_sems(n),
    )(*[_in_hbm(g) for g in grads])


def _chip_exchange_start(sums, landing, send_sems, recv_sems):
    x, y, c, others = _mesh_place()
    me = 2 * x + y
    for w in range(len(sums)):
        for k, (ox, oy) in enumerate(others):
            _remote(sums[w].at[2 * ox + oy], landing[w].at[me], send_sems.at[w, k], recv_sems.at[w, k],
                    (ox, oy, c)).start()


def _chip_exchange_finish(sums, landing, send_sems, recv_sems):
    x, y, c, others = _mesh_place()
    for w in range(len(sums)):
        for k, (ox, oy) in enumerate(others):
            piece = landing[w].at[2 * ox + oy]
            _remote(piece, piece, send_sems.at[w, k], recv_sems.at[w, k], (x, y, c)).wait_recv()
    for w in range(len(sums)):
        for k, (ox, oy) in enumerate(others):
            piece = sums[w].at[2 * ox + oy]
            _remote(piece, piece, send_sems.at[w, k], recv_sems.at[w, k], (x, y, c)).wait_send()


def _chip_exchange_sems(n):
    return [pltpu.SemaphoreType.DMA((n, 3)), pltpu.SemaphoreType.DMA((n, 3))]


def _sibling_allgather(bufs, also):
    n = len(bufs)
    k_in, k_out = len(also.operands), also.n_out

    def body(*refs):
        ex_ins, refs = refs[n:n + k_in], refs[n + k_in:]
        outs, refs = refs[:n], refs[n:]
        ex_outs, refs = refs[:k_out], refs[k_out:]
        send_sems, recv_sems, ex_sems = refs[0], refs[1], refs[2:]
        x, y, c, _ = _mesh_place()
        sibling = (x, y, 1 - c)
        also.start(ex_ins, ex_outs, ex_sems)
        sends = [_remote(outs[w].at[c], outs[w].at[c], send_sems.at[w], recv_sems.at[w], sibling) for w in range(n)]
        for cp in sends:
            cp.start()
        for w in range(n):
            landed = outs[w].at[1 - c]
            _remote(landed, landed, send_sems.at[w], recv_sems.at[w], sibling).wait_recv()
        for cp in sends:
            cp.wait_send()
        also.finish(ex_ins, ex_outs, ex_sems)

    res = pl.pallas_call(
        body, name="sibling_allgather",
        in_specs=[HBM_SPEC] * (n + k_in), out_specs=[HBM_SPEC] * (n + k_out),
        out_shape=[_hbm_like(b) for b in bufs] + also.out_shape,
        input_output_aliases={**{w: w for w in range(n)}, **{n + i: n + o for i, o in also.aliases.items()}},
        scratch_shapes=[pltpu.SemaphoreType.DMA((n,)), pltpu.SemaphoreType.DMA((n,))] + also.sems,
    )(*bufs, *[_in_hbm(o) for o in also.operands])
    return list(res[:n]), list(res[n:])


def _pair_sum(grad, other, place):
    _, _, h, cols = grad.shape
    tr = _row_tile(h)

    def body(place_ref, g_ref, o_ref, sums_ref, own_ref):
        s = (g_ref[0, 0] + o_ref[0]).astype(BF16)
        sums_ref[0] = s

        @pl.when(pl.program_id(1) == place_ref[0])
        def _():
            own_ref[0] = s

    return pl.pallas_call(
        body, name="pair_sum",
        grid_spec=pltpu.PrefetchScalarGridSpec(
            num_scalar_prefetch=1, grid=(h // tr, N_CHIP),
            in_specs=[pl.BlockSpec((1, 1, tr, cols), lambda r, j, place_ref: (j, place_ref[1], r, 0)),
                      pl.BlockSpec((1, tr, cols), lambda r, j, place_ref: (j, r, 0))],
            out_specs=[pl.BlockSpec((1, tr, cols), lambda r, j, place_ref: (j, r, 0)),
                       pl.BlockSpec((1, tr, cols), lambda r, j, place_ref: (place_ref[0], r, 0))]),
        out_shape=[pltpu.HBM((N_CHIP, h, cols), BF16)] * 2,
        compiler_params=_params(32, 2),
    )(place, _in_hbm(grad), _in_hbm(other))


def _chip_sum(parts, place):
    _, h, cols = parts.shape
    tr = _row_tile(h)

    def body(place_ref, p_ref, out_ref):
        out_ref[0] = ((p_ref[0].astype(F32) + p_ref[1].astype(F32)) + p_ref[2].astype(F32)) + p_ref[3].astype(F32)

    return pl.pallas_call(
        body, name="chip_sum",
        grid_spec=pltpu.PrefetchScalarGridSpec(
            num_scalar_prefetch=1, grid=(h // tr,),
            in_specs=[pl.BlockSpec((N_CHIP, tr, cols), lambda r, place_ref: (0, r, 0))],
            out_specs=pl.BlockSpec((1, tr, cols), lambda r, place_ref: (place_ref[1], r, 0))),
        out_shape=pltpu.HBM((2, h, cols), F32),
        compiler_params=_params(32),
    )(place, _in_hbm(parts))


def _adamw_math(w, g, m, v):
    m = ADAM_B1 * m + (1.0 - ADAM_B1) * g
    v = ADAM_B2 * v + (1.0 - ADAM_B2) * (g * g)
    m_hat = m / (1.0 - ADAM_B1 ** ADAM_STEP)
    v_hat = v / (1.0 - ADAM_B2 ** ADAM_STEP)
    delta = -ADAM_LR * (m_hat / (jnp.sqrt(v_hat) + ADAM_EPS) + ADAM_WD * w)
    return delta, m, v


def _adamw(w, g, m, v, exchange=None):
    rows, cols = w.shape
    tr = _row_tile(rows)

    def body(w_ref, g_ref, m_ref, v_ref, d_ref, nm_ref, nv_ref, g_out_ref):
        g = g_ref[...]
        d_ref[...], nm_ref[...], nv_ref[...] = _adamw_math(w_ref[...], g, m_ref[...], v_ref[...])
        g_out_ref[...] = g

    spec = pl.BlockSpec((tr, cols), lambda r: (r, 0))
    return _call(
        body, (w, g, m, v), grid=(rows // tr,), name="adamw",
        in_specs=[spec] * 4, out_specs=[spec] * 4,
        out_shape=[jax.ShapeDtypeStruct((rows, cols), F32)] * 4,
        compiler_params=_params(48), exchange=exchange)


SMALL_NAMES = ("norm1_gain", "gmlp_v_gain", "w_spatial", "b_spatial", "attn_sinks", "rel_bias_table", "norm2_gain",
               "final_gain")
PACK_TILE = 8 * 128


def _pack_small(arrays):
    parts = []
    for a in arrays:
        flat = a.reshape(-1)
        rows = -(-flat.shape[0] // PACK_TILE) * 8
        parts.append(jnp.pad(flat, (0, rows * 128 - flat.shape[0])).reshape(rows, 128))
    return jnp.concatenate(parts, axis=0)


def _unpack_small(packed, like):
    out, row = [], 0
    for a in like:
        size = math.prod(a.shape)
        rows = -(-size // PACK_TILE) * 8
        out.append(packed[row:row + rows].reshape(-1)[:size].reshape(a.shape))
        row += rows
    return out


def _small_update(gathered, w, m, v):
    rows = gathered.shape[1]

    def body(g_ref, w_ref, m_ref, v_ref, tot_ref, d_ref, nm_ref, nv_ref):
        total = g_ref[0].astype(F32)
        for dev in range(1, 8):
            total = total + g_ref[dev].astype(F32)
        tot_ref[...] = total
        d_ref[...], nm_ref[...], nv_ref[...] = _adamw_math(w_ref[...], total, m_ref[...], v_ref[...])

    return pl.pallas_call(
        body, name="small_update",
        in_specs=[VMEM_SPEC] * 4, out_specs=[VMEM_SPEC] * 4,
        out_shape=[jax.ShapeDtypeStruct((rows, 128), F32)] * 4,
        compiler_params=pltpu.CompilerParams(vmem_limit_bytes=24 * MIB),
    )(gathered, w, m, v)


def _halves(a):
    return a.reshape(a.shape[:-2] + (2, a.shape[-2] // 2, a.shape[-1]))


def _whole(a):
    return a.reshape(a.shape[:-3] + (2 * a.shape[-2], a.shape[-1]))


def kernel(x, p, norm1_gain, w_in, gmlp_v_gain, w_spatial, b_spatial, attn_sinks, rel_bias_table, w_out, norm2_gain, w_ff1, w_ff2, w_ple_proj, w_ple_gate, final_gain, loss_target, m_norm1_gain, m_w_in, m_gmlp_v_gain, m_w_spatial, m_b_spatial, m_attn_sinks, m_rel_bias_table, m_w_out, m_norm2_gain, m_w_ff1, m_w_ff2, m_w_ple_proj, m_w_ple_gate, m_final_gain, v_norm1_gain, v_w_in, v_gmlp_v_gain, v_w_spatial, v_b_spatial, v_attn_sinks, v_rel_bias_table, v_w_out, v_norm2_gain, v_w_ff1, v_w_ff2, v_w_ple_proj, v_w_ple_gate, v_final_gain):
    given = dict(locals())
    small = {n: given[n] for n in SMALL_NAMES}
    chip = 2 * lax.axis_index("x") + lax.axis_index("y")
    place = jnp.stack([chip, lax.axis_index("c")]).astype(jnp.int32)

    big_names = ("w_in", "w_out", "w_ff1", "w_ff2", "w_ple_proj", "w_ple_gate")
    shards = {n: given[n][0] for n in big_names}
    travel = dict(shards, w_in=jnp.transpose(shards["w_in"]))
    bufs = {n: _cast_shard(travel[n], place[:1]) for n in big_names}
    dx, landed, exchange_in, small_grads, sq = _step(x[0], p[0, 0], loss_target[0], small, bufs, place)

    out_grad, out_delta, out_m, out_v = {}, {}, {}, {}

    def update(n, g, exchange=None):
        to = jnp.transpose if n == "w_in" else (lambda a: a)
        (delta, new_m, new_v, g_out), got = _adamw(to(shards[n]), g, to(given["m_" + n][0]), to(given["v_" + n][0]),
                                                   exchange)
        out_grad[n], out_delta[n], out_m[n], out_v[n] = [to(a)[None] for a in (g_out, delta, new_m, new_v)]
        return got

    spare = jnp.zeros((8, 128), F32)
    small_packed = _pack_small([small_grads[n] for n in SMALL_NAMES] + [spare]).astype(BF16)
    early = [n for n in big_names if n != "w_in"]
    reduced, (small_gathered, sq_gathered) = _sibling_allgather(
        [_chip_sum(landed[n], place) for n in early], _Both(_GatherAll(small_packed), _GatherAll(sq)))
    reduced = dict(zip(early, reduced))
    landed_in, = update("w_ff1", _whole(reduced["w_ff1"]), exchange_in)
    for n in early:
        if n != "w_ff1":
            update(n, _whole(reduced[n]))
    (reduced_in,), _ = _sibling_allgather([_chip_sum(landed_in, place)], _Nothing())
    update("w_in", _whole(reduced_in))

    like = [given[n] for n in SMALL_NAMES] + [spare]
    packed = _small_update(small_gathered, *[_pack_small([given[pre + n] for n in SMALL_NAMES] + [spare])
                                             for pre in ("", "m_", "v_")])
    for res, out in zip(packed, (out_grad, out_delta, out_m, out_v)):
        out.update(zip(SMALL_NAMES, _unpack_small(res, like)))
    loss = 0.5 * jnp.sum(sq_gathered[:, 0, 0]) / D

    order = ("norm1_gain", "w_in", "gmlp_v_gain", "w_spatial", "b_spatial", "attn_sinks", "rel_bias_table", "w_out",
             "norm2_gain", "w_ff1", "w_ff2", "w_ple_proj", "w_ple_gate", "final_gain")
    return (loss, dx[None], *[out_grad[n] for n in order], *[out_delta[n] for n in order],
            *[out_m[n] for n in order], *[out_v[n] for n in order])
```

```python
import functools
import math

import jax
import jax.numpy as jnp
from jax import lax
from jax.experimental import pallas as pl
from jax.experimental.pallas import tpu as pltpu

S = 2048
D = 1024
D_IN = 1792
D_FF = 4096
PLE = 256
N_CHIP = 4
N_GROUP = 4
CHUNK = 128
N_HEAD = 8
N_BLOCK = S // CHUNK
N_BUCKET = 32
EPS = 1e-6
NEG_INF = -1e30
QK_SCALE = 0.125
GELU_C = math.sqrt(2.0 / math.pi)

ADAM_LR = 0.001
ADAM_B1 = 0.9
ADAM_B2 = 0.999
ADAM_EPS = 1e-08
ADAM_WD = 0.01
ADAM_STEP = 10

F32 = jnp.float32
BF16 = jnp.bfloat16
MIB = 1024 * 1024
MESH = pl.DeviceIdType.MESH

NT = (((1,), (1,)), ((), ()))
TN = (((0,), (0,)), ((), ()))


def _dot(a, b):
    return jnp.dot(a, b, preferred_element_type=F32)


def _dot_nt(a, b):
    return lax.dot_general(a, b, NT, preferred_element_type=F32)


def _dot_tn(a, b):
    return lax.dot_general(a, b, TN, preferred_element_type=F32)


def _params(vmem_mib, n_axes=1):
    return pltpu.CompilerParams(dimension_semantics=("arbitrary",) * n_axes, vmem_limit_bytes=vmem_mib * MIB)


def _rms_scale(v):
    return lax.rsqrt(jnp.mean(v * v, axis=-1, keepdims=True) + EPS)


def _rms_bwd(dy_gain, xhat, r):
    return r * (dy_gain - xhat * jnp.mean(dy_gain * xhat, axis=-1, keepdims=True))


class _Gather:
    def __init__(self, bufs):
        self.operands = list(bufs)
        self.n_out = len(self.operands)
        self.out_shape = [_hbm_like(b) for b in bufs]
        self.aliases = {w: w for w in range(self.n_out)}
        self.sems = _gather_sems(self.n_out)

    def start(self, ins, outs, sems):
        _gather_start(outs, *sems)

    def finish(self, ins, outs, sems):
        _gather_finish(outs, *sems)


class _RelayGather(_Gather):
    TOP, BOTTOM = 6, 7
    DIAGONAL_PASSED = 5
    MIDDLE_AT, LATE_AT = (5, 8), (7, 8)

    def __init__(self, bufs):
        super().__init__(bufs)
        self.sems = [pltpu.SemaphoreType.DMA((self.n_out, 8)), pltpu.SemaphoreType.DMA((self.n_out, 8))]

    def _copies(self, bufs, send_sems, recv_sems):
        x, y, c, others = _mesh_place()
        me = 2 * x + y
        idx = [2 * ox + oy for ox, oy in others]
        sibling = (x, y, 1 - c)
        direct, passed, relayed = [], [], []
        for w, buf in enumerate(bufs):
            rows = buf.shape[2] // 2
            upper, lower = pl.ds(0, rows), pl.ds(rows, rows)
            for k in (0, 1):
                mine = buf.at[me, c]
                direct.append((_remote(mine, mine, send_sems.at[w, k], recv_sems.at[w, k], (*others[k], c)),
                               buf.at[idx[k], c], w, k))
            for k in (0, 1, 2):
                here = buf.at[idx[k], c]
                passed.append((_remote(here, here, send_sems.at[w, 3 + k], recv_sems.at[w, 3 + k], sibling),
                               buf.at[idx[k], 1 - c], w, 3 + k))
            from_x, from_y = buf.at[idx[0], c, upper], buf.at[idx[1], c, lower]
            relayed.append((_remote(from_x, from_x, send_sems.at[w, self.TOP], recv_sems.at[w, self.TOP],
                                    (*others[1], c)), buf.at[idx[2], c, upper], w, self.TOP))
            relayed.append((_remote(from_y, from_y, send_sems.at[w, self.BOTTOM], recv_sems.at[w, self.BOTTOM],
                                    (*others[0], c)), buf.at[idx[2], c, lower], w, self.BOTTOM))
        return direct, passed, relayed

    @staticmethod
    def _landed(piece, send_sems, recv_sems, w, col):
        x, y, c, _ = _mesh_place()
        _remote(piece, piece, send_sems.at[w, col], recv_sems.at[w, col], (x, y, c)).wait_recv()

    def start(self, ins, outs, sems):
        for cp, _, _, _ in self._copies(outs, *sems)[0]:
            cp.start()

    def middle(self, ins, outs, sems):
        direct, passed, relayed = self._copies(outs, *sems)
        for _, piece, w, col in direct:
            self._landed(piece, *sems, w, col)
        for cp, _, _, col in passed:
            if col != self.DIAGONAL_PASSED:
                cp.start()
        for cp, _, _, _ in relayed:
            cp.start()

    def late(self, ins, outs, sems):
        direct, passed, relayed = self._copies(outs, *sems)
        for _, piece, w, col in relayed:
            self._landed(piece, *sems, w, col)
        for cp, _, _, col in passed:
            if col == self.DIAGONAL_PASSED:
                cp.start()

    def finish(self, ins, outs, sems):
        direct, passed, relayed = self._copies(outs, *sems)
        for _, piece, w, col in passed:
            self._landed(piece, *sems, w, col)
        for cp, _, _, _ in direct + passed + relayed:
            cp.wait_send()


class _ChipExchange:
    def __init__(self, sums, landing):
        self.n_out = len(landing)
        self.operands = list(sums) + list(landing)
        self.out_shape = [_hbm_like(b) for b in landing]
        self.aliases = {self.n_out + w: w for w in range(self.n_out)}
        self.sems = _chip_exchange_sems(self.n_out)

    def start(self, ins, outs, sems):
        _chip_exchange_start(ins[:self.n_out], outs, *sems)

    def finish(self, ins, outs, sems):
        _chip_exchange_finish(ins[:self.n_out], outs, *sems)


class _GatherAll:
    def __init__(self, packed):
        self.operands = [packed]
        self.n_out = 1
        self.out_shape = [_hbm_like(packed, (8,) + packed.shape)]
        self.aliases = {}
        self.sems = [pltpu.SemaphoreType.DMA((8,)), pltpu.SemaphoreType.DMA((8,))]

    def _copies(self, ins, outs, sems):
        x, y, c, _ = _mesh_place()
        me = 4 * x + 2 * y + c
        send_sems, recv_sems = sems
        copies = []
        for k in range(1, 8):
            peer = (1 - x if k // 4 else x, 1 - y if (k // 2) % 2 else y, 1 - c if k % 2 else c)
            src = 4 * peer[0] + 2 * peer[1] + peer[2]
            copies.append((_remote(ins[0], outs[0].at[me], send_sems.at[k], recv_sems.at[k], peer), outs[0].at[src]))
        own = pltpu.make_async_copy(ins[0], outs[0].at[me], send_sems.at[0])
        return own, copies

    def start(self, ins, outs, sems):
        own, copies = self._copies(ins, outs, sems)
        own.start()
        for cp, _ in copies:
            cp.start()

    def finish(self, ins, outs, sems):
        own, copies = self._copies(ins, outs, sems)
        x, y, c, _ = _mesh_place()
        for k, (cp, landed) in enumerate(copies):
            _remote(landed, landed, sems[0].at[k + 1], sems[1].at[k + 1], (x, y, c)).wait_recv()
        for cp, _ in copies:
            cp.wait_send()
        own.wait()


class _Nothing:
    operands, n_out, out_shape, aliases, sems = [], 0, [], {}, []

    def start(self, ins, outs, sems):
        pass

    def finish(self, ins, outs, sems):
        pass


class _Both:
    def __init__(self, a, b):
        self.a, self.b = a, b
        self.operands = a.operands + b.operands
        self.n_out = a.n_out + b.n_out
        self.out_shape = a.out_shape + b.out_shape
        self.aliases = dict(a.aliases)
        self.aliases.update({len(a.operands) + i: a.n_out + o for i, o in b.aliases.items()})
        self.sems = a.sems + b.sems

    def _split(self, ins, outs, sems):
        ka, na, sa = len(self.a.operands), self.a.n_out, len(self.a.sems)
        return (ins[:ka], outs[:na], sems[:sa]), (ins[ka:], outs[na:], sems[sa:])

    def start(self, ins, outs, sems):
        for ex, args in zip((self.a, self.b), self._split(ins, outs, sems)):
            ex.start(*args)

    def finish(self, ins, outs, sems):
        for ex, args in zip((self.a, self.b), self._split(ins, outs, sems)):
            ex.finish(*args)


class _SiblingExchange:
    def __init__(self, grads):
        self.operands = list(grads)
        self.n_out = len(self.operands)
        self.out_shape = [_hbm_like(g, (N_CHIP,) + g.shape[2:]) for g in grads]
        self.aliases = {}
        self.sems = _sibling_exchange_sems(self.n_out)

    def start(self, ins, outs, sems):
        _sibling_exchange_start(ins, outs, *sems)

    def finish(self, ins, outs, sems):
        _sibling_exchange_finish(ins, outs, *sems)


def _call(body, operands, *, grid, in_specs, out_specs, out_shape, name, compiler_params, scratch_shapes=(),
          exchange=None):
    operands = [o if getattr(spec, "memory_space", None) == pltpu.SMEM else _in_hbm(o)
                for o, spec in zip(operands, in_specs)]
    out_shape = [pltpu.HBM(s.shape, s.dtype) for s in out_shape]
    if exchange is None:
        res = pl.pallas_call(body, grid=grid, in_specs=in_specs, out_specs=out_specs, out_shape=out_shape, name=name,
                             scratch_shapes=list(scratch_shapes), compiler_params=compiler_params)(*operands)
        return list(res), []
    n_in, n_out, n_scr = len(in_specs), len(out_specs), len(scratch_shapes)
    k_in, k_out = len(exchange.operands), exchange.n_out

    def fused(*refs):
        ins, refs = refs[:n_in], refs[n_in:]
        ex_ins, refs = refs[:k_in], refs[k_in:]
        outs, refs = refs[:n_out], refs[n_out:]
        ex_outs, refs = refs[:k_out], refs[k_out:]
        scratch, sems = refs[:n_scr], refs[n_scr:]
        ids = [pl.program_id(a) for a in range(len(grid))]
        first = functools.reduce(jnp.logical_and, [i == 0 for i in ids])
        last = functools.reduce(jnp.logical_and, [i == g - 1 for i, g in zip(ids, grid)])

        @pl.when(first)
        def _():
            exchange.start(ex_ins, ex_outs, sems)

        def at_step(numerator, denominator):
            at = (numerator * math.prod(grid)) // denominator
            place = [(at // math.prod(grid[a + 1:])) % grid[a] for a in range(len(grid))]
            return functools.reduce(jnp.logical_and, [i == p for i, p in zip(ids, place)])

        if hasattr(exchange, "middle"):
            @pl.when(at_step(*exchange.MIDDLE_AT))
            def _():
                exchange.middle(ex_ins, ex_outs, sems)

            @pl.when(at_step(*exchange.LATE_AT))
            def _():
                exchange.late(ex_ins, ex_outs, sems)

        body(*ins, *outs, *scratch)

        @pl.when(last)
        def _():
            exchange.finish(ex_ins, ex_outs, sems)

    res = pl.pallas_call(
        fused, grid=grid, name=name,
        in_specs=list(in_specs) + [HBM_SPEC] * k_in, out_specs=list(out_specs) + [HBM_SPEC] * k_out,
        out_shape=list(out_shape) + exchange.out_shape,
        input_output_aliases={n_in + i: n_out + o for i, o in exchange.aliases.items()},
        scratch_shapes=list(scratch_shapes) + exchange.sems, compiler_params=compiler_params,
    )(*operands, *[_in_hbm(o) for o in exchange.operands])
    return list(res[:n_out]), list(res[n_out:])


def _in_hbm(a):
    return pltpu.with_memory_space_constraint(a, pltpu.HBM)


def _row_tile(h):
    return max(t for t in range(16, 513, 16) if h % t == 0)


def _cast_shard(a, chip):
    rows, cols = a.shape
    h = rows // 2
    tr = _row_tile(h)

    def body(chip_ref, a_ref, o_ref):
        o_ref[0, 0] = a_ref[0].astype(BF16)

    return pl.pallas_call(
        body, name="cast_shard",
        grid_spec=pltpu.PrefetchScalarGridSpec(
            num_scalar_prefetch=1, grid=(2, h // tr),
            in_specs=[pl.BlockSpec((1, tr, cols), lambda s, r, chip_ref: (s, r, 0))],
            out_specs=pl.BlockSpec((1, 1, tr, cols), lambda s, r, chip_ref: (chip_ref[0], s, r, 0))),
        out_shape=pltpu.HBM((N_CHIP, 2, h, cols), BF16),
        compiler_params=_params(16, 2),
    )(chip, _in_hbm(a.reshape(2, h, cols)))


def _in_proj(x, gain1, w_in_t, exchange=None):
    tm = 256

    def body(x_ref, g_ref, w_ref, z_ref, hn_ref):
        xv = x_ref[...]
        hn = (xv * _rms_scale(xv) * g_ref[...]).astype(BF16)
        hn_ref[...] = hn
        z_ref[...] = _dot_nt(hn, w_ref[...])

    return _call(
        body, (x, gain1, w_in_t), grid=(S // tm,), name="in_proj",
        in_specs=[pl.BlockSpec((tm, D), lambda i: (i, 0)), pl.BlockSpec((1, D), lambda i: (0, 0)),
                  pl.BlockSpec((D_IN, D), lambda i: (0, 0))],
        out_specs=[pl.BlockSpec((tm, D_IN), lambda i: (i, 0)), pl.BlockSpec((tm, D), lambda i: (i, 0))],
        out_shape=[jax.ShapeDtypeStruct((S, D_IN), F32), jax.ShapeDtypeStruct((S, D), BF16)],
        compiler_params=_params(40), exchange=exchange)


def _gelu_parts(v):
    t = jnp.tanh(GELU_C * (v + 0.044715 * (v * v * v)))
    cdf = 0.5 * (1.0 + t)
    return cdf, t


def _band_mask(n):
    a = lax.broadcasted_iota(jnp.int32, (CHUNK, 2 * CHUNK), 0)
    j = lax.broadcasted_iota(jnp.int32, (CHUNK, 2 * CHUNK), 1)
    dist = CHUNK + a - j
    valid = (dist >= 0) & (dist < CHUNK)
    return valid & ((n > 0) | (j >= CHUNK))


def _fill_bias(bucket_ref, table_ref, bias_ref):
    bucket = bucket_ref[...]
    for h in range(N_HEAD):
        acc = jnp.zeros((CHUNK, 2 * CHUNK), F32)
        for b in range(N_BUCKET):
            acc = jnp.where(bucket == b, table_ref[b, h], acc)
        bias_ref[h] = acc


def _fill_tril(ws_ref, wt_ref, wtt_ref=None):
    r = lax.broadcasted_iota(jnp.int32, (CHUNK, CHUNK), 0)
    c = lax.broadcasted_iota(jnp.int32, (CHUNK, CHUNK), 1)
    for g in range(N_GROUP):
        w = jnp.where(c <= r, ws_ref[g], 0.0)
        wt_ref[g] = w.astype(BF16)
        if wtt_ref is not None:
            wtt_ref[g] = w.T.astype(BF16)


def _kv_layouts(kv_prev, kv_cur):
    both = jnp.concatenate([kv_prev, kv_cur], axis=0)
    k = both[:, :128]
    v = both[:, 128:]
    return (k.astype(BF16), pltpu.roll(k, 64, axis=1).astype(BF16),
            v.astype(BF16), pltpu.roll(v, 64, axis=1).astype(BF16))


def _head_place(h):
    pair, pos, kvh = h // 2, h % 2, h // 4
    return pair, pos, kvh == pos


def _softmax_sink(qm, k_use, bias_h, sink, valid):
    s = _dot_nt(qm, k_use) * QK_SCALE + bias_h
    s = jnp.where(valid, s, NEG_INF)
    m = jnp.maximum(jnp.max(s, axis=-1, keepdims=True), sink)
    e = jnp.exp(s - m)
    es = jnp.exp(sink - m)
    inv = 1.0 / (jnp.sum(e, axis=-1, keepdims=True) + es)
    return e * inv, es * inv


def _mixer_fwd(z, v_gain, w_spatial, b_spatial_t, sinks, rel_table, bucket, exchange=None):
    def body(z_ref, kvp_ref, gain_ref, ws_ref, bt_ref, sink_ref, table_ref, bucket_ref, out_ref, bias_ref, wt_ref):
        n = pl.program_id(0)

        @pl.when(n == 0)
        def _():
            _fill_bias(bucket_ref, table_ref, bias_ref)
            _fill_tril(ws_ref, wt_ref)

        zuv = z_ref[:, :1024]
        cdf, _ = _gelu_parts(zuv)
        guv = zuv * cdf
        for g in range(N_GROUP):
            vg = guv[:, 512 + 128 * g:512 + 128 * (g + 1)]
            vn = vg * _rms_scale(vg) * gain_ref[:, 128 * g:128 * (g + 1)]
            sv = _dot(wt_ref[g], vn.astype(BF16)) + bt_ref[:, g:g + 1]
            out_ref[:, 128 * g:128 * (g + 1)] = (guv[:, 128 * g:128 * (g + 1)] * sv).astype(BF16)

        k_same, k_swap, v_same, v_swap = _kv_layouts(kvp_ref[...], z_ref[:, 1536:1792])
        valid = _band_mask(n)
        lane_half = lax.broadcasted_iota(jnp.int32, (1, 128), 1) // 64
        for pair in range(N_HEAD // 2):
            qq = z_ref[:, 1024 + 128 * pair:1024 + 128 * (pair + 1)]
            acc = jnp.zeros((CHUNK, 128), F32)
            for pos in range(2):
                h = 2 * pair + pos
                _, _, same = _head_place(h)
                qm = jnp.where(lane_half == pos, qq, 0.0).astype(BF16)
                p, _ = _softmax_sink(qm, k_same if same else k_swap, bias_ref[h], sink_ref[h], valid)
                vm = jnp.where(lane_half == pos, v_same if same else v_swap, jnp.zeros((), BF16))
                acc = acc + _dot(p.astype(BF16), vm)
            out_ref[:, 512 + 128 * pair:512 + 128 * (pair + 1)] = acc.astype(BF16)

    return _call(
        body, (z, z, v_gain, w_spatial, b_spatial_t, sinks, rel_table, bucket), grid=(N_BLOCK,), name="mixer_fwd",
        in_specs=[pl.BlockSpec((CHUNK, D_IN), lambda n: (n, 0)),
                  pl.BlockSpec((CHUNK, 256), lambda n: (jnp.maximum(n - 1, 0), 6)),
                  pl.BlockSpec((1, 512), lambda n: (0, 0)),
                  pl.BlockSpec((N_GROUP, CHUNK, CHUNK), lambda n: (0, 0, 0)),
                  pl.BlockSpec((CHUNK, N_GROUP), lambda n: (0, 0)),
                  pl.BlockSpec(memory_space=pltpu.SMEM),
                  pl.BlockSpec(memory_space=pltpu.SMEM),
                  pl.BlockSpec((CHUNK, 2 * CHUNK), lambda n: (0, 0))],
        out_specs=[pl.BlockSpec((CHUNK, D), lambda n: (n, 0))],
        out_shape=[jax.ShapeDtypeStruct((S, D), BF16)],
        scratch_shapes=[pltpu.VMEM((N_HEAD, CHUNK, 2 * CHUNK), F32), pltpu.VMEM((N_GROUP, CHUNK, CHUNK), BF16)],
        compiler_params=_params(32), exchange=exchange)


def _out_proj(x, mix, w_out, gain2, exchange=None):
    tm = 256

    def body(x_ref, mix_ref, w_ref, g_ref, h1_ref, hn_ref, hnt_ref):
        h1 = x_ref[...] + _dot(mix_ref[...], w_ref[...])
        h1_ref[...] = h1
        hn = h1 * _rms_scale(h1) * g_ref[...]
        hn_ref[...] = hn.astype(BF16)
        hnt_ref[...] = hn.T.astype(BF16)

    return _call(
        body, (x, mix, w_out, gain2), grid=(S // tm,), name="out_proj",
        in_specs=[pl.BlockSpec((tm, D), lambda i: (i, 0)), pl.BlockSpec((tm, D), lambda i: (i, 0)),
                  pl.BlockSpec((D, D), lambda i: (0, 0)), pl.BlockSpec((1, D), lambda i: (0, 0))],
        out_specs=[pl.BlockSpec((tm, D), lambda i: (i, 0)), pl.BlockSpec((tm, D), lambda i: (i, 0)),
                   pl.BlockSpec((D, tm), lambda i: (0, i))],
        out_shape=[jax.ShapeDtypeStruct((S, D), F32), jax.ShapeDtypeStruct((S, D), BF16),
                   jax.ShapeDtypeStruct((D, S), BF16)],
        compiler_params=_params(32), exchange=exchange)


def _ffn_up(hn2, w_ff1, exchange=None):
    tm = 512
    nj = D_FF // 1024

    def body(hn_ref, w1_ref, r_ref, a_ref, at_ref):
        r = jnp.maximum(_dot(hn_ref[...], w1_ref[0]), 0.0)
        r_ref[...] = r.astype(BF16)
        a = r * r
        a_ref[...] = a.astype(BF16)
        at_ref[...] = a.T.astype(BF16)

    return _call(
        body, (hn2, w_ff1), grid=(nj, S // tm), name="ffn_up",
        in_specs=[pl.BlockSpec((tm, D), lambda j, i: (i, 0)), pl.BlockSpec((1, D, 1024), lambda j, i: (j, 0, 0))],
        out_specs=[pl.BlockSpec((tm, 1024), lambda j, i: (i, j)), pl.BlockSpec((tm, 1024), lambda j, i: (i, j)),
                   pl.BlockSpec((1024, tm), lambda j, i: (j, i))],
        out_shape=[jax.ShapeDtypeStruct((S, D_FF), BF16), jax.ShapeDtypeStruct((S, D_FF), BF16),
                   jax.ShapeDtypeStruct((D_FF, S), BF16)],
        compiler_params=_params(40, 2), exchange=exchange)


def _ffn_down(h1, a, w_ff2, exchange=None):
    tm = 1024
    nj = D_FF // 1024

    def body(h1_ref, a_ref, w2_ref, h2_ref, acc_ref):
        j = pl.program_id(1)
        part = _dot(a_ref[...], w2_ref[0])

        @pl.when(j == 0)
        def _():
            acc_ref[...] = part

        @pl.when(j > 0)
        def _():
            acc_ref[...] += part

        @pl.when(j == nj - 1)
        def _():
            h2_ref[...] = h1_ref[...] + acc_ref[...]

    return _call(
        body, (h1, a, w_ff2), grid=(S // tm, nj), name="ffn_down",
        in_specs=[pl.BlockSpec((tm, D), lambda i, j: (i, 0)), pl.BlockSpec((tm, 1024), lambda i, j: (i, j)),
                  pl.BlockSpec((1, 1024, D), lambda i, j: (j, 0, 0))],
        out_specs=[pl.BlockSpec((tm, D), lambda i, j: (i, 0))],
        out_shape=[jax.ShapeDtypeStruct((S, D), F32)],
        scratch_shapes=[pltpu.VMEM((tm, D), F32)],
        compiler_params=_params(48, 2), exchange=exchange)


def _tail(h2, p, target, w_gate, w_proj, final_gain):
    tm = 256
    steps = S // tm

    def body(h2_ref, p_ref, t_ref, wg_ref, wp_ref, gf_ref, dh2_ref, dwg_ref, dwp_ref, dgf_ref, loss_ref, dh2b_ref,
             dwp_acc):
        i = pl.program_id(0)
        h2 = h2_ref[...]
        h2b = h2.astype(BF16)
        pb = p_ref[...].astype(BF16)
        gate = jax.nn.sigmoid(_dot(h2b, wg_ref[...]))
        pp = jnp.concatenate([_dot(pb, wp_ref[j]) for j in range(N_CHIP)], axis=1)
        h3 = h2 + gate * pp
        r3 = _rms_scale(h3)
        xhat = h3 * r3
        gf = gf_ref[...]
        err = xhat * gf - t_ref[...]
        dy = err * (1.0 / D)
        dh3 = _rms_bwd(dy * gf, xhat, r3)
        dgp = (dh3 * pp * gate * (1.0 - gate)).astype(BF16)
        dpp = (dh3 * gate).astype(BF16)
        dh2 = dh3 + _dot_nt(dgp, wg_ref[...])
        dh2_ref[...] = dh2
        dh2b_ref[...] = dh2.astype(BF16)
        dwg = _dot_tn(h2b, dgp)
        dwp = _dot_tn(pb, dpp)
        dgf = jnp.sum(dy * xhat, axis=0, keepdims=True)
        sq = jnp.sum(jnp.sum(err * err, axis=1, keepdims=True), axis=0, keepdims=True)

        @pl.when(i == 0)
        def _():
            dwg_ref[...] = dwg
            dwp_acc[...] = dwp
            dgf_ref[...] = dgf
            loss_ref[...] = jnp.broadcast_to(sq, (8, 128))

        @pl.when(i > 0)
        def _():
            dwg_ref[...] += dwg
            dwp_acc[...] += dwp
            dgf_ref[...] += dgf
            loss_ref[...] += jnp.broadcast_to(sq, (8, 128))

        @pl.when(i == steps - 1)
        def _():
            for j in range(N_CHIP):
                dwp_ref[j] = dwp_acc[:, 256 * j:256 * (j + 1)]

    return _call(
        body, (h2, p, target, w_gate, w_proj, final_gain), grid=(steps,), name="tail",
        in_specs=[pl.BlockSpec((tm, D), lambda i: (i, 0)), pl.BlockSpec((tm, PLE), lambda i: (i, 0)),
                  pl.BlockSpec((tm, D), lambda i: (i, 0)), pl.BlockSpec((D, D), lambda i: (0, 0)),
                  pl.BlockSpec((N_CHIP, PLE, 256), lambda i: (0, 0, 0)), pl.BlockSpec((1, D), lambda i: (0, 0))],
        out_specs=[pl.BlockSpec((tm, D), lambda i: (i, 0)), pl.BlockSpec((D, D), lambda i: (0, 0)),
                   pl.BlockSpec((N_CHIP, PLE, 256), lambda i: (0, 0, 0)), pl.BlockSpec((1, D), lambda i: (0, 0)),
                   pl.BlockSpec((8, 128), lambda i: (0, 0)), pl.BlockSpec((tm, D), lambda i: (i, 0))],
        out_shape=[jax.ShapeDtypeStruct((S, D), F32), jax.ShapeDtypeStruct((D, D), F32),
                   jax.ShapeDtypeStruct((N_CHIP, PLE, 256), F32), jax.ShapeDtypeStruct((1, D), F32),
                   jax.ShapeDtypeStruct((8, 128), F32), jax.ShapeDtypeStruct((S, D), BF16)],
        scratch_shapes=[pltpu.VMEM((PLE, D), F32)],
        compiler_params=_params(48))[0]


def _ffn_bwd_down(dh2b, r, a_t, w_ff2, exchange=None):
    tm = 1024
    nj = D_FF // 1024

    def body(dh2_ref, r_ref, at_ref, w2_ref, df_ref, dw2_ref):
        i = pl.program_id(1)
        dh2b = dh2_ref[...]
        da = _dot_nt(dh2b, w2_ref[0])
        df_ref[...] = (da * (2.0 * r_ref[...].astype(F32))).astype(BF16)
        dw2 = _dot(at_ref[...], dh2b)

        @pl.when(i == 0)
        def _():
            dw2_ref[0] = dw2

        @pl.when(i > 0)
        def _():
            dw2_ref[0] += dw2

    return _call(
        body, (dh2b, r, a_t, w_ff2), grid=(nj, S // tm), name="ffn_bwd_down",
        in_specs=[pl.BlockSpec((tm, D), lambda j, i: (i, 0)), pl.BlockSpec((tm, 1024), lambda j, i: (i, j)),
                  pl.BlockSpec((1024, tm), lambda j, i: (j, i)), pl.BlockSpec((1, 1024, D), lambda j, i: (j, 0, 0))],
        out_specs=[pl.BlockSpec((tm, 1024), lambda j, i: (i, j)), pl.BlockSpec((1, 1024, D), lambda j, i: (j, 0, 0))],
        out_shape=[jax.ShapeDtypeStruct((S, D_FF), BF16), jax.ShapeDtypeStruct((nj, 1024, D), F32)],
        compiler_params=_params(48, 2), exchange=exchange)


def _ffn_bwd_up(df, hn2_t, exchange=None):
    tm = 1024
    nj = D_FF // 1024

    def body(df_ref, hnt_ref, dw1_ref):
        i = pl.program_id(1)
        dw1 = _dot(hnt_ref[...], df_ref[...])

        @pl.when(i == 0)
        def _():
            dw1_ref[0] = dw1

        @pl.when(i > 0)
        def _():
            dw1_ref[0] += dw1

    return _call(
        body, (df, hn2_t), grid=(nj, S // tm), name="ffn_bwd_up",
        in_specs=[pl.BlockSpec((tm, 1024), lambda j, i: (i, j)), pl.BlockSpec((D, tm), lambda j, i: (0, i))],
        out_specs=[pl.BlockSpec((1, D, 1024), lambda j, i: (j, 0, 0))],
        out_shape=[jax.ShapeDtypeStruct((nj, D, 1024), F32)],
        compiler_params=_params(40, 2), exchange=exchange)


def _ffn_bwd_input(df, w_ff1, dh2, h1, gain2, mix, w_out, exchange=None):
    tm = 512
    nj = D_FF // 1024
    steps = S // tm

    def body(df_ref, w1_ref, dh2_ref, h1_ref, g_ref, mix_ref, wo_ref, dh1_ref, dmix_ref, dwo_ref, dg_ref, acc_ref):
        i = pl.program_id(0)
        j = pl.program_id(1)
        part = _dot_nt(df_ref[...], w1_ref[0])

        @pl.when(j == 0)
        def _():
            acc_ref[...] = part

        @pl.when(j > 0)
        def _():
            acc_ref[...] += part

        @pl.when(j == nj - 1)
        def _():
            dhn = acc_ref[...]
            h1 = h1_ref[...]
            r2 = _rms_scale(h1)
            xhat = h1 * r2
            dh1 = dh2_ref[...] + _rms_bwd(dhn * g_ref[...], xhat, r2)
            dh1_ref[...] = dh1
            dh1b = dh1.astype(BF16)
            dmix_ref[...] = _dot_nt(dh1b, wo_ref[...])
            dwo = _dot_tn(mix_ref[...], dh1b)
            dg = jnp.sum(dhn * xhat, axis=0, keepdims=True)

            @pl.when(i == 0)
            def _():
                dwo_ref[...] = dwo
                dg_ref[...] = dg

            @pl.when(i > 0)
            def _():
                dwo_ref[...] += dwo
                dg_ref[...] += dg

    return _call(
        body, (df, w_ff1, dh2, h1, gain2, mix, w_out), grid=(steps, nj), name="ffn_bwd_input",
        in_specs=[pl.BlockSpec((tm, 1024), lambda i, j: (i, j)), pl.BlockSpec((1, D, 1024), lambda i, j: (j, 0, 0)),
                  pl.BlockSpec((tm, D), lambda i, j: (i, 0)), pl.BlockSpec((tm, D), lambda i, j: (i, 0)),
                  pl.BlockSpec((1, D), lambda i, j: (0, 0)), pl.BlockSpec((tm, D), lambda i, j: (i, 0)),
                  pl.BlockSpec((D, D), lambda i, j: (0, 0))],
        out_specs=[pl.BlockSpec((tm, D), lambda i, j: (i, 0)), pl.BlockSpec((tm, D), lambda i, j: (i, 0)),
                   pl.BlockSpec((D, D), lambda i, j: (0, 0)), pl.BlockSpec((1, D), lambda i, j: (0, 0))],
        out_shape=[jax.ShapeDtypeStruct((S, D), F32), jax.ShapeDtypeStruct((S, D), F32),
                   jax.ShapeDtypeStruct((D, D), F32), jax.ShapeDtypeStruct((1, D), F32)],
        scratch_shapes=[pltpu.VMEM((tm, D), F32)],
        compiler_params=_params(56, 2), exchange=exchange)


IN_GROUP = 8


def _mixer_bwd(z, dmix, v_gain, w_spatial, b_spatial_t, sinks, rel_table, bucket, hn1, exchange=None):
    def body(z_ref, kvp_ref, dm_ref, gain_ref, ws_ref, bt_ref, sink_ref, table_ref, bucket_ref, hn_ref,
             dz_ref, dws_ref, db_ref, dgain_ref, dsink_ref, drel_ref, dwin_ref,
             bias_ref, wt_ref, wtt_ref, dbias_ref, dsv_ref, carry_ref):
        n = pl.program_id(0)

        @pl.when(n == 0)
        def _():
            _fill_bias(bucket_ref, table_ref, bias_ref)
            _fill_tril(ws_ref, wt_ref, wtt_ref)
            dwin_ref[...] = jnp.zeros_like(dwin_ref)
            dbias_ref[...] = jnp.zeros_like(dbias_ref)
            dsv_ref[...] = jnp.zeros_like(dsv_ref)
            dws_ref[...] = jnp.zeros_like(dws_ref)
            dgain_ref[...] = jnp.zeros_like(dgain_ref)
            dsink_ref[...] = jnp.zeros_like(dsink_ref)

        rows = pl.ds(pl.multiple_of(n * CHUNK, CHUNK), CHUNK)

        zuv = z_ref[:, :1024]
        cdf, t = _gelu_parts(zuv)
        guv = zuv * cdf
        dgelu = cdf + zuv * (0.5 * (1.0 - t * t)) * (GELU_C * (1.0 + 3.0 * 0.044715 * (zuv * zuv)))
        for g in range(N_GROUP):
            lo, hi = 128 * g, 128 * (g + 1)
            u = guv[:, lo:hi]
            vg = guv[:, 512 + lo:512 + hi]
            rr = _rms_scale(vg)
            vhat = vg * rr
            gain = gain_ref[:, lo:hi]
            vnb = (vhat * gain).astype(BF16)
            sv = _dot(wt_ref[g], vnb) + bt_ref[:, g:g + 1]
            da = dm_ref[:, lo:hi]
            dsv = da * u
            dsvb = dsv.astype(BF16)
            dsv_ref[g] += dsv
            dws_ref[g] += _dot_nt(dsvb, vnb)
            dvn = _dot(wtt_ref[g], dsvb)
            dgain_ref[:, lo:hi] += jnp.sum(dvn * vhat, axis=0, keepdims=True)
            dvg = _rms_bwd(dvn * gain, vhat, rr)
            dz_ref[rows, lo:hi] = (da * sv * dgelu[:, lo:hi]).astype(BF16)
            dz_ref[rows, 512 + lo:512 + hi] = (dvg * dgelu[:, 512 + lo:512 + hi]).astype(BF16)

        k_same, k_swap, v_same, v_swap = _kv_layouts(kvp_ref[...], z_ref[:, 1536:1792])
        valid = _band_mask(n)
        lane_half = lax.broadcasted_iota(jnp.int32, (1, 128), 1) // 64
        zero = jnp.zeros((2 * CHUNK, 128), F32)
        dk_same, dk_swap, dv_same, dv_swap = zero, zero, zero, zero
        for pair in range(N_HEAD // 2):
            cols = slice(1024 + 128 * pair, 1024 + 128 * (pair + 1))
            qq = z_ref[:, cols]
            do_pair = dm_ref[:, 512 + 128 * pair:512 + 128 * (pair + 1)]
            dq = jnp.zeros((CHUNK, 128), F32)
            for pos in range(2):
                h = 2 * pair + pos
                _, _, same = _head_place(h)
                on_half = lane_half == pos
                qm = jnp.where(on_half, qq, 0.0).astype(BF16)
                k_use = k_same if same else k_swap
                v_use = v_same if same else v_swap
                p, p_sink = _softmax_sink(qm, k_use, bias_ref[h], sink_ref[h], valid)
                dom = jnp.where(on_half, do_pair, 0.0).astype(BF16)
                dp = _dot_nt(dom, v_use)
                dsum = jnp.sum(p * dp, axis=-1, keepdims=True)
                ds = p * (dp - dsum)
                dbias_ref[h] += ds
                dsink_ref[h:h + 1, :] += jnp.broadcast_to(jnp.sum(-p_sink * dsum, axis=0, keepdims=True), (1, 128))
                dsb = ds.astype(BF16)
                dq = dq + jnp.where(on_half, _dot(dsb, k_use), 0.0)
                dk_h = _dot_tn(dsb, qm)
                dv_h = _dot_tn(p.astype(BF16), dom)
                if same:
                    dk_same, dv_same = dk_same + dk_h, dv_same + dv_h
                else:
                    dk_swap, dv_swap = dk_swap + dk_h, dv_swap + dv_h
            dz_ref[rows, cols] = (dq * QK_SCALE).astype(BF16)
        dk = (dk_same + pltpu.roll(dk_swap, 64, axis=1)) * QK_SCALE
        dv = dv_same + pltpu.roll(dv_swap, 64, axis=1)
        dkv = jnp.concatenate([dk, dv], axis=1)

        @pl.when(n > 0)
        def _():
            prev_rows = pl.ds(pl.multiple_of((n - 1) * CHUNK, CHUNK), CHUNK)
            dz_ref[prev_rows, 1536:1792] = (carry_ref[...] + dkv[:CHUNK]).astype(BF16)

        carry_ref[...] = dkv[CHUNK:]

        @pl.when((n > 0) & (n % IN_GROUP == 0))
        def _():
            done = pl.ds(pl.multiple_of((n - IN_GROUP) * CHUNK, IN_GROUP * CHUNK), IN_GROUP * CHUNK)
            dwin_ref[...] += _dot_tn(dz_ref[done, :], hn_ref[...])

        @pl.when(n == N_BLOCK - 1)
        def _():
            dz_ref[rows, 1536:1792] = dkv[CHUNK:].astype(BF16)
            last = pl.ds((N_BLOCK - IN_GROUP) * CHUNK, IN_GROUP * CHUNK)
            dwin_ref[...] += _dot_tn(dz_ref[last, :], hn_ref[...])
            r = lax.broadcasted_iota(jnp.int32, (CHUNK, CHUNK), 0)
            c = lax.broadcasted_iota(jnp.int32, (CHUNK, CHUNK), 1)
            for g in range(N_GROUP):
                dws_ref[g] = jnp.where(c <= r, dws_ref[g], 0.0)
                db_ref[g] = jnp.sum(dsv_ref[g], axis=1, keepdims=True)
            bucket = bucket_ref[...]
            for h in range(N_HEAD):
                dbh = dbias_ref[h]
                per_bucket = [jnp.sum(jnp.where(bucket == b, dbh, 0.0), axis=0, keepdims=True) for b in range(N_BUCKET)]
                drel_ref[h] = jnp.sum(jnp.concatenate(per_bucket, axis=0), axis=1, keepdims=True)

    def hn_group(n):
        return jnp.where(n == N_BLOCK - 1, N_BLOCK // IN_GROUP - 1, jnp.maximum(n // IN_GROUP - 1, 0))

    return _call(
        body, (z, z, dmix, v_gain, w_spatial, b_spatial_t, sinks, rel_table, bucket, hn1), grid=(N_BLOCK,),
        name="mixer_bwd",
        in_specs=[pl.BlockSpec((CHUNK, D_IN), lambda n: (n, 0)),
                  pl.BlockSpec((CHUNK, 256), lambda n: (jnp.maximum(n - 1, 0), 6)),
                  pl.BlockSpec((CHUNK, D), lambda n: (n, 0)),
                  pl.BlockSpec((1, 512), lambda n: (0, 0)),
                  pl.BlockSpec((N_GROUP, CHUNK, CHUNK), lambda n: (0, 0, 0)),
                  pl.BlockSpec((CHUNK, N_GROUP), lambda n: (0, 0)),
                  pl.BlockSpec(memory_space=pltpu.SMEM),
                  pl.BlockSpec(memory_space=pltpu.SMEM),
                  pl.BlockSpec((CHUNK, 2 * CHUNK), lambda n: (0, 0)),
                  pl.BlockSpec((IN_GROUP * CHUNK, D), lambda n: (hn_group(n), 0))],
        out_specs=[pl.BlockSpec((S, D_IN), lambda n: (0, 0)),
                   pl.BlockSpec((N_GROUP, CHUNK, CHUNK), lambda n: (0, 0, 0)),
                   pl.BlockSpec((N_GROUP, CHUNK, 1), lambda n: (0, 0, 0)),
                   pl.BlockSpec((1, 512), lambda n: (0, 0)),
                   pl.BlockSpec((N_HEAD, 128), lambda n: (0, 0)),
                   pl.BlockSpec((N_HEAD, N_BUCKET, 1), lambda n: (0, 0, 0)),
                   pl.BlockSpec((D_IN, D), lambda n: (0, 0))],
        out_shape=[jax.ShapeDtypeStruct((S, D_IN), BF16), jax.ShapeDtypeStruct((N_GROUP, CHUNK, CHUNK), F32),
                   jax.ShapeDtypeStruct((N_GROUP, CHUNK, 1), F32), jax.ShapeDtypeStruct((1, 512), F32),
                   jax.ShapeDtypeStruct((N_HEAD, 128), F32), jax.ShapeDtypeStruct((N_HEAD, N_BUCKET, 1), F32),
                   jax.ShapeDtypeStruct((D_IN, D), F32)],
        scratch_shapes=[pltpu.VMEM((N_HEAD, CHUNK, 2 * CHUNK), F32), pltpu.VMEM((N_GROUP, CHUNK, CHUNK), BF16),
                        pltpu.VMEM((N_GROUP, CHUNK, CHUNK), BF16), pltpu.VMEM((N_HEAD, CHUNK, 2 * CHUNK), F32),
                        pltpu.VMEM((N_GROUP, CHUNK, CHUNK), F32), pltpu.VMEM((CHUNK, 256), F32)],
        compiler_params=_params(56), exchange=exchange)


def _in_bwd_input(dz, w_in_t, x, dh1, gain1, exchange=None):
    tm = 512

    def body(dz_ref, w_ref, x_ref, dh1_ref, g_ref, dx_ref, dg_ref):
        i = pl.program_id(0)
        dhn = _dot(dz_ref[...], w_ref[...])
        xv = x_ref[...]
        r1 = _rms_scale(xv)
        xhat = xv * r1
        dx_ref[...] = dh1_ref[...] + _rms_bwd(dhn * g_ref[...], xhat, r1)
        dg = jnp.sum(dhn * xhat, axis=0, keepdims=True)

        @pl.when(i == 0)
        def _():
            dg_ref[...] = dg

        @pl.when(i > 0)
        def _():
            dg_ref[...] += dg

    return _call(
        body, (dz, w_in_t, x, dh1, gain1), grid=(S // tm,), name="in_bwd_input",
        in_specs=[pl.BlockSpec((tm, D_IN), lambda i: (i, 0)), pl.BlockSpec((D_IN, D), lambda i: (0, 0)),
                  pl.BlockSpec((tm, D), lambda i: (i, 0)), pl.BlockSpec((tm, D), lambda i: (i, 0)),
                  pl.BlockSpec((1, D), lambda i: (0, 0))],
        out_specs=[pl.BlockSpec((tm, D), lambda i: (i, 0)), pl.BlockSpec((1, D), lambda i: (0, 0))],
        out_shape=[jax.ShapeDtypeStruct((S, D), F32), jax.ShapeDtypeStruct((1, D), F32)],
        compiler_params=_params(48), exchange=exchange)


def _rel_bucket():
    a = jnp.arange(CHUNK)[:, None]
    j = jnp.arange(2 * CHUNK)[None, :]
    n = jnp.maximum(CHUNK + a - j, 0)
    max_exact = N_BUCKET // 2
    nf = jnp.maximum(n, 1).astype(jnp.float32)
    large = max_exact + (jnp.log(nf / max_exact) / math.log(CHUNK / max_exact) * (N_BUCKET - max_exact)).astype(jnp.int32)
    large = jnp.minimum(large, N_BUCKET - 1)
    return jnp.where(n < max_exact, n, large).astype(jnp.int32)


def _step(x, p, target, small, bufs, place):
    bucket = _rel_bucket()
    sinks = small["attn_sinks"].reshape(N_HEAD)
    b_t = jnp.transpose(small["b_spatial"].reshape(N_GROUP, CHUNK))
    ws = small["w_spatial"].reshape(N_GROUP, CHUNK, CHUNK)
    gain1, gain2 = small["norm1_gain"], small["norm2_gain"]
    v_gain = small["gmlp_v_gain"]
    final_gain = small["final_gain"].reshape(1, D)
    table = small["rel_bias_table"]
    bufs = dict(bufs)

    def gather(*names):
        return _RelayGather([bufs[n] for n in names])

    def took(names, got):
        bufs.update(zip(names, got))

    took(["w_in"], _gather_weights([bufs["w_in"]]))
    w_in_t = _whole(bufs["w_in"]).reshape(D_IN, D)
    (z, hn1), got = _in_proj(x, gain1, w_in_t, gather("w_out"))
    took(["w_out"], got)
    (mix,), got = _mixer_fwd(z, v_gain, ws, b_t, sinks, table, bucket, gather("w_ff1"))
    took(["w_ff1"], got)
    w_out = _whole(bufs["w_out"]).reshape(D, D)
    (h1, hn2, hn2_t), _ = _out_proj(x, mix, w_out, gain2)
    w_ff1 = _whole(bufs["w_ff1"])
    (r, a, a_t), got = _ffn_up(hn2, w_ff1, gather("w_ff2"))
    took(["w_ff2"], got)
    w_ff2 = _whole(bufs["w_ff2"])
    (h2,), got = _ffn_down(h1, a, w_ff2, gather("w_ple_gate", "w_ple_proj"))
    took(["w_ple_gate", "w_ple_proj"], got)
    dh2, d_gate, d_proj, d_final, sq, dh2b = _tail(h2, p, target, _whole(bufs["w_ple_gate"]).reshape(D, D),
                                                   _whole(bufs["w_ple_proj"]), final_gain)

    def pair_sums(halves, from_sibling):
        sums, landing = zip(*[_pair_sum(g, o, place) for g, o in zip(halves, from_sibling)])
        return list(sums), list(landing)

    landed = {}
    halves = [_halves(d_gate.reshape(N_CHIP, 256, D)), _halves(d_proj)]
    (df, d_ff2), got = _ffn_bwd_down(dh2b, r, a_t, w_ff2, _SiblingExchange(halves))
    ex, halves = _ChipExchange(*pair_sums(halves, got)), [_halves(d_ff2)]
    (d_ff1,), got = _ffn_bwd_up(df, hn2_t, _Both(ex, _SiblingExchange(halves)))
    landed.update(zip(["w_ple_gate", "w_ple_proj"], got[:2]))
    ex, halves = _ChipExchange(*pair_sums(halves, got[2:])), [_halves(d_ff1)]
    (dh1, dmix, d_out, d_gain2), got = _ffn_bwd_input(df, w_ff1, dh2, h1, gain2, mix, w_out,
                                                      _Both(ex, _SiblingExchange(halves)))
    landed["w_ff2"] = got[0]
    ex, halves = _ChipExchange(*pair_sums(halves, got[1:])), [_halves(d_out.reshape(N_CHIP, 256, D))]
    (dz, d_ws, d_b, d_vgain, d_sink, d_rel, d_in_t), got = _mixer_bwd(z, dmix, v_gain, ws, b_t, sinks, table, bucket, hn1,
                                                                     _Both(ex, _SiblingExchange(halves)))
    landed["w_ff1"] = got[0]
    small_grads = {
        "gmlp_v_gain": d_vgain, "w_spatial": d_ws.reshape(1, N_GROUP, CHUNK, CHUNK),
        "b_spatial": d_b.reshape(1, N_GROUP, CHUNK), "attn_sinks": d_sink[:, 0].reshape(1, N_HEAD),
        "rel_bias_table": jnp.transpose(d_rel.reshape(N_HEAD, N_BUCKET)), "norm2_gain": d_gain2,
        "final_gain": d_final.reshape(D),
    }
    ex, halves = _ChipExchange(*pair_sums(halves, got[1:])), [_halves(d_in_t.reshape(N_CHIP, 448, D))]
    (dx, small_grads["norm1_gain"]), got = _in_bwd_input(dz, w_in_t, x, dh1, gain1, _Both(ex, _SiblingExchange(halves)))
    landed["w_out"] = got[0]
    return dx, landed, _ChipExchange(*pair_sums(halves, got[1:])), small_grads, sq


HBM_SPEC = pl.BlockSpec(memory_space=pltpu.HBM)
VMEM_SPEC = pl.BlockSpec(memory_space=pltpu.VMEM)


def _mesh_place():
    x, y, c = lax.axis_index("x"), lax.axis_index("y"), lax.axis_index("c")
    others = [(1 - x, y), (x, 1 - y), (1 - x, 1 - y)]
    return x, y, c, others


def _remote(src, dst, send_sem, recv_sem, device):
    return pltpu.make_async_remote_copy(src_ref=src, dst_ref=dst, send_sem=send_sem, recv_sem=recv_sem,
                                        device_id=device, device_id_type=MESH)


def _hbm_like(a, shape=None, dtype=None):
    return pltpu.HBM(a.shape if shape is None else shape, a.dtype if dtype is None else dtype)


def _gather_start(bufs, send_sems, recv_sems):
    x, y, c, others = _mesh_place()
    me = 2 * x + y
    for w, buf in enumerate(bufs):
        for k in range(3):
            mine = buf.at[me, c]
            _remote(mine, mine, send_sems.at[w, k], recv_sems.at[w, k], (*others[k], c)).start()


def _gather_finish(bufs, send_sems, recv_sems):
    x, y, c, others = _mesh_place()
    me = 2 * x + y
    sibling = (x, y, 1 - c)
    idx = [2 * ox + oy for ox, oy in others]
    chips = range(3)
    for w, buf in enumerate(bufs):
        for k in chips:
            landed = buf.at[idx[k], c]
            _remote(landed, landed, send_sems.at[w, k], recv_sems.at[w, k], sibling).wait_recv()
            _remote(landed, landed, send_sems.at[w, 3 + k], recv_sems.at[w, 3 + k], sibling).start()
    for w, buf in enumerate(bufs):
        for k in chips:
            landed = buf.at[idx[k], 1 - c]
            _remote(landed, landed, send_sems.at[w, 3 + k], recv_sems.at[w, 3 + k], sibling).wait_recv()
    for w, buf in enumerate(bufs):
        for k in chips:
            mine, passed = buf.at[me, c], buf.at[idx[k], c]
            _remote(mine, mine, send_sems.at[w, k], recv_sems.at[w, k], sibling).wait_send()
            _remote(passed, passed, send_sems.at[w, 3 + k], recv_sems.at[w, 3 + k], sibling).wait_send()


def _gather_sems(n):
    return [pltpu.SemaphoreType.DMA((n, 6)), pltpu.SemaphoreType.DMA((n, 6))]


def _gather_weights(bufs):
    n = len(bufs)

    def body(*refs):
        outs = refs[n:2 * n]
        send_sems, recv_sems = refs[2 * n:]
        _gather_start(outs, send_sems, recv_sems)
        _gather_finish(outs, send_sems, recv_sems)

    return pl.pallas_call(
        body, name="gather_weights",
        in_specs=[HBM_SPEC] * n, out_specs=[HBM_SPEC] * n,
        out_shape=[_hbm_like(b) for b in bufs],
        input_output_aliases={w: w for w in range(n)},
        scratch_shapes=_gather_sems(n),
    )(*bufs)


def _sibling_copies(grads, landing, send_sems, recv_sems):
    x, y, c, _ = _mesh_place()
    return [_remote(grads[w].at[j, 1 - c], landing[w].at[j], send_sems.at[w, j], recv_sems.at[w, j], (x, y, 1 - c))
            for w in range(len(grads)) for j in range(N_CHIP)]


def _sibling_exchange_start(grads, landing, send_sems, recv_sems):
    for cp in _sibling_copies(grads, landing, send_sems, recv_sems):
        cp.start()


def _sibling_exchange_finish(grads, landing, send_sems, recv_sems):
    copies = _sibling_copies(grads, landing, send_sems, recv_sems)
    for cp in copies:
        cp.wait_recv()
    for cp in copies:
        cp.wait_send()


def _sibling_exchange_sems(n):
    return [pltpu.SemaphoreType.DMA((n, N_CHIP)), pltpu.SemaphoreType.DMA((n, N_CHIP))]


def _sibling_exchange(grads):
    n = len(grads)

    def body(*refs):
        ins, outs = refs[:n], refs[n:2 * n]
        _sibling_exchange_start(ins, outs, *refs[2 * n:])
        _sibling_exchange_finish(ins, outs, *refs[2 * n:])

    return pl.pallas_call(
        body, name="sibling_exchange",
        in_specs=[HBM_SPEC] * n, out_specs=[HBM_SPEC] * n,
        out_shape=[_hbm_like(g, (N_CHIP,) + g.shape[2:]) for g in grads],
        scratch_shapes=_sibling_exchange_sems(n),
    )(*[_in_hbm(g) for g in grads])


def _chip_exchange_start(sums, landing, send_sems, recv_sems):
    x, y, c, others = _mesh_place()
    me = 2 * x + y
    for w in range(len(sums)):
        for k, (ox, oy) in enumerate(others):
            _remote(sums[w].at[2 * ox + oy], landing[w].at[me], send_sems.at[w, k], recv_sems.at[w, k],
                    (ox, oy, c)).start()


def _chip_exchange_finish(sums, landing, send_sems, recv_sems):
    x, y, c, others = _mesh_place()
    for w in range(len(sums)):
        for k, (ox, oy) in enumerate(others):
            piece = landing[w].at[2 * ox + oy]
            _remote(piece, piece, send_sems.at[w, k], recv_sems.at[w, k], (x, y, c)).wait_recv()
    for w in range(len(sums)):
        for k, (ox, oy) in enumerate(others):
            piece = sums[w].at[2 * ox + oy]
            _remote(piece, piece, send_sems.at[w, k], recv_sems.at[w, k], (x, y, c)).wait_send()


def _chip_exchange_sems(n):
    return [pltpu.SemaphoreType.DMA((n, 3)), pltpu.SemaphoreType.DMA((n, 3))]


def _sibling_allgather(bufs, also):
    n = len(bufs)
    k_in, k_out = len(also.operands), also.n_out

    def body(*refs):
        ex_ins, refs = refs[n:n + k_in], refs[n + k_in:]
        outs, refs = refs[:n], refs[n:]
        ex_outs, refs = refs[:k_out], refs[k_out:]
        send_sems, recv_sems, ex_sems = refs[0], refs[1], refs[2:]
        x, y, c, _ = _mesh_place()
        sibling = (x, y, 1 - c)
        also.start(ex_ins, ex_outs, ex_sems)
        sends = [_remote(outs[w].at[c], outs[w].at[c], send_sems.at[w], recv_sems.at[w], sibling) for w in range(n)]
        for cp in sends:
            cp.start()
        for w in range(n):
            landed = outs[w].at[1 - c]
            _remote(landed, landed, send_sems.at[w], recv_sems.at[w], sibling).wait_recv()
        for cp in sends:
            cp.wait_send()
        also.finish(ex_ins, ex_outs, ex_sems)

    res = pl.pallas_call(
        body, name="sibling_allgather",
        in_specs=[HBM_SPEC] * (n + k_in), out_specs=[HBM_SPEC] * (n + k_out),
        out_shape=[_hbm_like(b) for b in bufs] + also.out_shape,
        input_output_aliases={**{w: w for w in range(n)}, **{n + i: n + o for i, o in also.aliases.items()}},
        scratch_shapes=[pltpu.SemaphoreType.DMA((n,)), pltpu.SemaphoreType.DMA((n,))] + also.sems,
    )(*bufs, *[_in_hbm(o) for o in also.operands])
    return list(res[:n]), list(res[n:])


def _pair_sum(grad, other, place):
    _, _, h, cols = grad.shape
    tr = _row_tile(h)

    def body(place_ref, g_ref, o_ref, sums_ref, own_ref):
        s = (g_ref[0, 0] + o_ref[0]).astype(BF16)
        sums_ref[0] = s

        @pl.when(pl.program_id(1) == place_ref[0])
        def _():
            own_ref[0] = s

    return pl.pallas_call(
        body, name="pair_sum",
        grid_spec=pltpu.PrefetchScalarGridSpec(
            num_scalar_prefetch=1, grid=(h // tr, N_CHIP),
            in_specs=[pl.BlockSpec((1, 1, tr, cols), lambda r, j, place_ref: (j, place_ref[1], r, 0)),
                      pl.BlockSpec((1, tr, cols), lambda r, j, place_ref: (j, r, 0))],
            out_specs=[pl.BlockSpec((1, tr, cols), lambda r, j, place_ref: (j, r, 0)),
                       pl.BlockSpec((1, tr, cols), lambda r, j, place_ref: (place_ref[0], r, 0))]),
        out_shape=[pltpu.HBM((N_CHIP, h, cols), BF16)] * 2,
        compiler_params=_params(32, 2),
    )(place, _in_hbm(grad), _in_hbm(other))


def _chip_sum(parts, place):
    _, h, cols = parts.shape
    tr = _row_tile(h)

    def body(place_ref, p_ref, out_ref):
        out_ref[0] = ((p_ref[0].astype(F32) + p_ref[1].astype(F32)) + p_ref[2].astype(F32)) + p_ref[3].astype(F32)

    return pl.pallas_call(
        body, name="chip_sum",
        grid_spec=pltpu.PrefetchScalarGridSpec(
            num_scalar_prefetch=1, grid=(h // tr,),
            in_specs=[pl.BlockSpec((N_CHIP, tr, cols), lambda r, place_ref: (0, r, 0))],
            out_specs=pl.BlockSpec((1, tr, cols), lambda r, place_ref: (place_ref[1], r, 0))),
        out_shape=pltpu.HBM((2, h, cols), F32),
        compiler_params=_params(32),
    )(place, _in_hbm(parts))


def _adamw_math(w, g, m, v):
    m = ADAM_B1 * m + (1.0 - ADAM_B1) * g
    v = ADAM_B2 * v + (1.0 - ADAM_B2) * (g * g)
    m_hat = m / (1.0 - ADAM_B1 ** ADAM_STEP)
    v_hat = v / (1.0 - ADAM_B2 ** ADAM_STEP)
    delta = -ADAM_LR * (m_hat / (jnp.sqrt(v_hat) + ADAM_EPS) + ADAM_WD * w)
    return delta, m, v


def _adamw(w, g, m, v, exchange=None):
    rows, cols = w.shape
    tr = _row_tile(rows)

    def body(w_ref, g_ref, m_ref, v_ref, d_ref, nm_ref, nv_ref, g_out_ref):
        g = g_ref[...]
        d_ref[...], nm_ref[...], nv_ref[...] = _adamw_math(w_ref[...], g, m_ref[...], v_ref[...])
        g_out_ref[...] = g

    spec = pl.BlockSpec((tr, cols), lambda r: (r, 0))
    return _call(
        body, (w, g, m, v), grid=(rows // tr,), name="adamw",
        in_specs=[spec] * 4, out_specs=[spec] * 4,
        out_shape=[jax.ShapeDtypeStruct((rows, cols), F32)] * 4,
        compiler_params=_params(48), exchange=exchange)


SMALL_NAMES = ("norm1_gain", "gmlp_v_gain", "w_spatial", "b_spatial", "attn_sinks", "rel_bias_table", "norm2_gain",
               "final_gain")
PACK_TILE = 8 * 128


def _pack_small(arrays):
    parts = []
    for a in arrays:
        flat = a.reshape(-1)
        rows = -(-flat.shape[0] // PACK_TILE) * 8
        parts.append(jnp.pad(flat, (0, rows * 128 - flat.shape[0])).reshape(rows, 128))
    return jnp.concatenate(parts, axis=0)


def _unpack_small(packed, like):
    out, row = [], 0
    for a in like:
        size = math.prod(a.shape)
        rows = -(-size // PACK_TILE) * 8
        out.append(packed[row:row + rows].reshape(-1)[:size].reshape(a.shape))
        row += rows
    return out


def _small_update(gathered, w, m, v):
    rows = gathered.shape[1]

    def body(g_ref, w_ref, m_ref, v_ref, tot_ref, d_ref, nm_ref, nv_ref):
        total = g_ref[0].astype(F32)
        for dev in range(1, 8):
            total = total + g_ref[dev].astype(F32)
        tot_ref[...] = total
        d_ref[...], nm_ref[...], nv_ref[...] = _adamw_math(w_ref[...], total, m_ref[...], v_ref[...])

    return pl.pallas_call(
        body, name="small_update",
        in_specs=[VMEM_SPEC] * 4, out_specs=[VMEM_SPEC] * 4,
        out_shape=[jax.ShapeDtypeStruct((rows, 128), F32)] * 4,
        compiler_params=pltpu.CompilerParams(vmem_limit_bytes=24 * MIB),
    )(gathered, w, m, v)


def _halves(a):
    return a.reshape(a.shape[:-2] + (2, a.shape[-2] // 2, a.shape[-1]))


def _whole(a):
    return a.reshape(a.shape[:-3] + (2 * a.shape[-2], a.shape[-1]))


def kernel(x, p, norm1_gain, w_in, gmlp_v_gain, w_spatial, b_spatial, attn_sinks, rel_bias_table, w_out, norm2_gain, w_ff1, w_ff2, w_ple_proj, w_ple_gate, final_gain, loss_target, m_norm1_gain, m_w_in, m_gmlp_v_gain, m_w_spatial, m_b_spatial, m_attn_sinks, m_rel_bias_table, m_w_out, m_norm2_gain, m_w_ff1, m_w_ff2, m_w_ple_proj, m_w_ple_gate, m_final_gain, v_norm1_gain, v_w_in, v_gmlp_v_gain, v_w_spatial, v_b_spatial, v_attn_sinks, v_rel_bias_table, v_w_out, v_norm2_gain, v_w_ff1, v_w_ff2, v_w_ple_proj, v_w_ple_gate, v_final_gain):
    given = dict(locals())
    small = {n: given[n] for n in SMALL_NAMES}
    chip = 2 * lax.axis_index("x") + lax.axis_index("y")
    place = jnp.stack([chip, lax.axis_index("c")]).astype(jnp.int32)

    big_names = ("w_in", "w_out", "w_ff1", "w_ff2", "w_ple_proj", "w_ple_gate")
    shards = {n: given[n][0] for n in big_names}
    travel = dict(shards, w_in=jnp.transpose(shards["w_in"]))
    bufs = {n: _cast_shard(travel[n], place[:1]) for n in big_names}
    dx, landed, exchange_in, small_grads, sq = _step(x[0], p[0, 0], loss_target[0], small, bufs, place)

    out_grad, out_delta, out_m, out_v = {}, {}, {}, {}

    def update(n, g, exchange=None):
        to = jnp.transpose if n == "w_in" else (lambda a: a)
        (delta, new_m, new_v, g_out), got = _adamw(to(shards[n]), g, to(given["m_" + n][0]), to(given["v_" + n][0]),
                                                   exchange)
        out_grad[n], out_delta[n], out_m[n], out_v[n] = [to(a)[None] for a in (g_out, delta, new_m, new_v)]
        return got

    spare = jnp.zeros((8, 128), F32)
    small_packed = _pack_small([small_grads[n] for n in SMALL_NAMES] + [spare]).astype(BF16)
    early = [n for n in big_names if n != "w_in"]
    reduced, (small_gathered, sq_gathered, landed_in) = _sibling_allgather(
        [_chip_sum(landed[n], place) for n in early], _Both(_Both(_GatherAll(small_packed), _GatherAll(sq)), exchange_in))
    for n, r in zip(early, reduced):
        update(n, _whole(r))
    (reduced_in,), _ = _sibling_allgather([_chip_sum(landed_in, place)], _Nothing())
    update("w_in", _whole(reduced_in))

    like = [given[n] for n in SMALL_NAMES] + [spare]
    packed = _small_update(small_gathered, *[_pack_small([given[pre + n] for n in SMALL_NAMES] + [spare])
                                             for pre in ("", "m_", "v_")])
    for res, out in zip(packed, (out_grad, out_delta, out_m, out_v)):
        out.update(zip(SMALL_NAMES, _unpack_small(res, like)))
    loss = 0.5 * jnp.sum(sq_gathered[:, 0, 0]) / D

    order = ("norm1_gain", "w_in", "gmlp_v_gain", "w_spatial", "b_spatial", "attn_sinks", "rel_bias_table", "w_out",
             "norm2_gain", "w_ff1", "w_ff2", "w_ple_proj", "w_ple_gate", "final_gain")
    return (loss, dx[None], *[out_grad[n] for n in order], *[out_delta[n] for n in order],
            *[out_m[n] for n in order], *[out_v[n] for n in order])
```

```python
import functools
import math

import jax
import jax.numpy as jnp
from jax import lax
from jax.experimental import pallas as pl
from jax.experimental.pallas import tpu as pltpu

S = 2048
D = 1024
D_IN = 1792
D_FF = 4096
PLE = 256
N_CHIP = 4
N_GROUP = 4
CHUNK = 128
N_HEAD = 8
N_BLOCK = S // CHUNK
N_BUCKET = 32
EPS = 1e-6
NEG_INF = -1e30
QK_SCALE = 0.125
GELU_C = math.sqrt(2.0 / math.pi)

ADAM_LR = 0.001
ADAM_B1 = 0.9
ADAM_B2 = 0.999
ADAM_EPS = 1e-08
ADAM_WD = 0.01
ADAM_STEP = 10

F32 = jnp.float32
BF16 = jnp.bfloat16
MIB = 1024 * 1024
MESH = pl.DeviceIdType.MESH

NT = (((1,), (1,)), ((), ()))
TN = (((0,), (0,)), ((), ()))


def _dot(a, b):
    return jnp.dot(a, b, preferred_element_type=F32)


def _dot_nt(a, b):
    return lax.dot_general(a, b, NT, preferred_element_type=F32)


def _dot_tn(a, b):
    return lax.dot_general(a, b, TN, preferred_element_type=F32)


def _params(vmem_mib, n_axes=1):
    return pltpu.CompilerParams(dimension_semantics=("arbitrary",) * n_axes, vmem_limit_bytes=vmem_mib * MIB)


def _rms_scale(v):
    return lax.rsqrt(jnp.mean(v * v, axis=-1, keepdims=True) + EPS)


def _rms_bwd(dy_gain, xhat, r):
    return r * (dy_gain - xhat * jnp.mean(dy_gain * xhat, axis=-1, keepdims=True))


class _Gather:
    def __init__(self, bufs):
        self.operands = list(bufs)
        self.n_out = len(self.operands)
        self.out_shape = [_hbm_like(b) for b in bufs]
        self.aliases = {w: w for w in range(self.n_out)}
        self.sems = _gather_sems(self.n_out)

    def start(self, ins, outs, sems):
        _gather_start(outs, *sems)

    def finish(self, ins, outs, sems):
        _gather_finish(outs, *sems)


class _RelayGather(_Gather):
    TOP, BOTTOM = 6, 7
    DIAGONAL_PASSED = 5
    MIDDLE_AT, LATE_AT = (5, 8), (7, 8)

    def __init__(self, bufs):
        super().__init__(bufs)
        self.sems = [pltpu.SemaphoreType.DMA((self.n_out, 8)), pltpu.SemaphoreType.DMA((self.n_out, 8))]

    def _copies(self, bufs, send_sems, recv_sems):
        x, y, c, others = _mesh_place()
        me = 2 * x + y
        idx = [2 * ox + oy for ox, oy in others]
        sibling = (x, y, 1 - c)
        direct, passed, relayed = [], [], []
        for w, buf in enumerate(bufs):
            rows = buf.shape[2] // 2
            upper, lower = pl.ds(0, rows), pl.ds(rows, rows)
            for k in (0, 1):
                mine = buf.at[me, c]
                direct.append((_remote(mine, mine, send_sems.at[w, k], recv_sems.at[w, k], (*others[k], c)),
                               buf.at[idx[k], c], w, k))
            for k in (0, 1, 2):
                here = buf.at[idx[k], c]
                passed.append((_remote(here, here, send_sems.at[w, 3 + k], recv_sems.at[w, 3 + k], sibling),
                               buf.at[idx[k], 1 - c], w, 3 + k))
            from_x, from_y = buf.at[idx[0], c, upper], buf.at[idx[1], c, lower]
            relayed.append((_remote(from_x, from_x, send_sems.at[w, self.TOP], recv_sems.at[w, self.TOP],
                                    (*others[1], c)), buf.at[idx[2], c, upper], w, self.TOP))
            relayed.append((_remote(from_y, from_y, send_sems.at[w, self.BOTTOM], recv_sems.at[w, self.BOTTOM],
                                    (*others[0], c)), buf.at[idx[2], c, lower], w, self.BOTTOM))
        return direct, passed, relayed

    @staticmethod
    def _landed(piece, send_sems, recv_sems, w, col):
        x, y, c, _ = _mesh_place()
        _remote(piece, piece, send_sems.at[w, col], recv_sems.at[w, col], (x, y, c)).wait_recv()

    def start(self, ins, outs, sems):
        for cp, _, _, _ in self._copies(outs, *sems)[0]:
            cp.start()

    def middle(self, ins, outs, sems):
        direct, passed, relayed = self._copies(outs, *sems)
        for _, piece, w, col in direct:
            self._landed(piece, *sems, w, col)
        for cp, _, _, col in passed:
            if col != self.DIAGONAL_PASSED:
                cp.start()
        for cp, _, _, _ in relayed:
            cp.start()

    def late(self, ins, outs, sems):
        direct, passed, relayed = self._copies(outs, *sems)
        for _, piece, w, col in relayed:
            self._landed(piece, *sems, w, col)
        for cp, _, _, col in passed:
            if col == self.DIAGONAL_PASSED:
                cp.start()

    def finish(self, ins, outs, sems):
        direct, passed, relayed = self._copies(outs, *sems)
        for _, piece, w, col in passed:
            self._landed(piece, *sems, w, col)
        for cp, _, _, _ in direct + passed + relayed:
            cp.wait_send()


class _ChipExchange:
    def __init__(self, sums, landing):
        self.n_out = len(landing)
        self.operands = list(sums) + list(landing)
        self.out_shape = [_hbm_like(b) for b in landing]
        self.aliases = {self.n_out + w: w for w in range(self.n_out)}
        self.sems = _chip_exchange_sems(self.n_out)

    def start(self, ins, outs, sems):
        _chip_exchange_start(ins[:self.n_out], outs, *sems)

    def finish(self, ins, outs, sems):
        _chip_exchange_finish(ins[:self.n_out], outs, *sems)


class _GatherAll:
    def __init__(self, packed):
        self.operands = [packed]
        self.n_out = 1
        self.out_shape = [_hbm_like(packed, (8,) + packed.shape)]
        self.aliases = {}
        self.sems = [pltpu.SemaphoreType.DMA((8,)), pltpu.SemaphoreType.DMA((8,))]

    def _copies(self, ins, outs, sems):
        x, y, c, _ = _mesh_place()
        me = 4 * x + 2 * y + c
        send_sems, recv_sems = sems
        copies = []
        for k in range(1, 8):
            peer = (1 - x if k // 4 else x, 1 - y if (k // 2) % 2 else y, 1 - c if k % 2 else c)
            src = 4 * peer[0] + 2 * peer[1] + peer[2]
            copies.append((_remote(ins[0], outs[0].at[me], send_sems.at[k], recv_sems.at[k], peer), outs[0].at[src]))
        own = pltpu.make_async_copy(ins[0], outs[0].at[me], send_sems.at[0])
        return own, copies

    def start(self, ins, outs, sems):
        own, copies = self._copies(ins, outs, sems)
        own.start()
        for cp, _ in copies:
            cp.start()

    def finish(self, ins, outs, sems):
        own, copies = self._copies(ins, outs, sems)
        x, y, c, _ = _mesh_place()
        for k, (cp, landed) in enumerate(copies):
            _remote(landed, landed, sems[0].at[k + 1], sems[1].at[k + 1], (x, y, c)).wait_recv()
        for cp, _ in copies:
            cp.wait_send()
        own.wait()


class _Nothing:
    operands, n_out, out_shape, aliases, sems = [], 0, [], {}, []

    def start(self, ins, outs, sems):
        pass

    def finish(self, ins, outs, sems):
        pass


class _Both:
    def __init__(self, a, b):
        self.a, self.b = a, b
        self.operands = a.operands + b.operands
        self.n_out = a.n_out + b.n_out
        self.out_shape = a.out_shape + b.out_shape
        self.aliases = dict(a.aliases)
        self.aliases.update({len(a.operands) + i: a.n_out + o for i, o in b.aliases.items()})
        self.sems = a.sems + b.sems

    def _split(self, ins, outs, sems):
        ka, na, sa = len(self.a.operands), self.a.n_out, len(self.a.sems)
        return (ins[:ka], outs[:na], sems[:sa]), (ins[ka:], outs[na:], sems[sa:])

    def start(self, ins, outs, sems):
        for ex, args in zip((self.a, self.b), self._split(ins, outs, sems)):
            ex.start(*args)

    def finish(self, ins, outs, sems):
        for ex, args in zip((self.a, self.b), self._split(ins, outs, sems)):
            ex.finish(*args)


class _SiblingExchange:
    def __init__(self, grads):
        self.operands = list(grads)
        self.n_out = len(self.operands)
        self.out_shape = [_hbm_like(g, (N_CHIP,) + g.shape[2:]) for g in grads]
        self.aliases = {}
        self.sems = _sibling_exchange_sems(self.n_out)

    def start(self, ins, outs, sems):
        _sibling_exchange_start(ins, outs, *sems)

    def finish(self, ins, outs, sems):
        _sibling_exchange_finish(ins, outs, *sems)


def _call(body, operands, *, grid, in_specs, out_specs, out_shape, name, compiler_params, scratch_shapes=(),
          exchange=None):
    operands = [o if getattr(spec, "memory_space", None) == pltpu.SMEM else _in_hbm(o)
                for o, spec in zip(operands, in_specs)]
    out_shape = [pltpu.HBM(s.shape, s.dtype) for s in out_shape]
    if exchange is None:
        res = pl.pallas_call(body, grid=grid, in_specs=in_specs, out_specs=out_specs, out_shape=out_shape, name=name,
                             scratch_shapes=list(scratch_shapes), compiler_params=compiler_params)(*operands)
        return list(res), []
    n_in, n_out, n_scr = len(in_specs), len(out_specs), len(scratch_shapes)
    k_in, k_out = len(exchange.operands), exchange.n_out

    def fused(*refs):
        ins, refs = refs[:n_in], refs[n_in:]
        ex_ins, refs = refs[:k_in], refs[k_in:]
        outs, refs = refs[:n_out], refs[n_out:]
        ex_outs, refs = refs[:k_out], refs[k_out:]
        scratch, sems = refs[:n_scr], refs[n_scr:]
        ids = [pl.program_id(a) for a in range(len(grid))]
        first = functools.reduce(jnp.logical_and, [i == 0 for i in ids])
        last = functools.reduce(jnp.logical_and, [i == g - 1 for i, g in zip(ids, grid)])

        @pl.when(first)
        def _():
            exchange.start(ex_ins, ex_outs, sems)

        def at_step(numerator, denominator):
            at = (numerator * math.prod(grid)) // denominator
            place = [(at // math.prod(grid[a + 1:])) % grid[a] for a in range(len(grid))]
            return functools.reduce(jnp.logical_and, [i == p for i, p in zip(ids, place)])

        if hasattr(exchange, "middle"):
            @pl.when(at_step(*exchange.MIDDLE_AT))
            def _():
                exchange.middle(ex_ins, ex_outs, sems)

            @pl.when(at_step(*exchange.LATE_AT))
            def _():
                exchange.late(ex_ins, ex_outs, sems)

        body(*ins, *outs, *scratch)

        @pl.when(last)
        def _():
            exchange.finish(ex_ins, ex_outs, sems)

    res = pl.pallas_call(
        fused, grid=grid, name=name,
        in_specs=list(in_specs) + [HBM_SPEC] * k_in, out_specs=list(out_specs) + [HBM_SPEC] * k_out,
        out_shape=list(out_shape) + exchange.out_shape,
        input_output_aliases={n_in + i: n_out + o for i, o in exchange.aliases.items()},
        scratch_shapes=list(scratch_shapes) + exchange.sems, compiler_params=compiler_params,
    )(*operands, *[_in_hbm(o) for o in exchange.operands])
    return list(res[:n_out]), list(res[n_out:])


def _in_hbm(a):
    return pltpu.with_memory_space_constraint(a, pltpu.HBM)


def _row_tile(h):
    return max(t for t in range(16, 513, 16) if h % t == 0)


def _cast_shard(a, chip):
    rows, cols = a.shape
    h = rows // 2
    tr = _row_tile(h)

    def body(chip_ref, a_ref, o_ref):
        o_ref[0, 0] = a_ref[0].astype(BF16)

    return pl.pallas_call(
        body, name="cast_shard",
        grid_spec=pltpu.PrefetchScalarGridSpec(
            num_scalar_prefetch=1, grid=(2, h // tr),
            in_specs=[pl.BlockSpec((1, tr, cols), lambda s, r, chip_ref: (s, r, 0))],
            out_specs=pl.BlockSpec((1, 1, tr, cols), lambda s, r, chip_ref: (chip_ref[0], s, r, 0))),
        out_shape=pltpu.HBM((N_CHIP, 2, h, cols), BF16),
        compiler_params=_params(16, 2),
    )(chip, _in_hbm(a.reshape(2, h, cols)))


def _in_proj(x, gain1, w_in_t, exchange=None):
    tm = 256

    def body(x_ref, g_ref, w_ref, z_ref, hn_ref):
        xv = x_ref[...]
        hn = (xv * _rms_scale(xv) * g_ref[...]).astype(BF16)
        hn_ref[...] = hn
        z_ref[...] = _dot_nt(hn, w_ref[...])

    return _call(
        body, (x, gain1, w_in_t), grid=(S // tm,), name="in_proj",
        in_specs=[pl.BlockSpec((tm, D), lambda i: (i, 0)), pl.BlockSpec((1, D), lambda i: (0, 0)),
                  pl.BlockSpec((D_IN, D), lambda i: (0, 0))],
        out_specs=[pl.BlockSpec((tm, D_IN), lambda i: (i, 0)), pl.BlockSpec((tm, D), lambda i: (i, 0))],
        out_shape=[jax.ShapeDtypeStruct((S, D_IN), F32), jax.ShapeDtypeStruct((S, D), BF16)],
        compiler_params=_params(40), exchange=exchange)


def _gelu_parts(v):
    t = jnp.tanh(GELU_C * (v + 0.044715 * (v * v * v)))
    cdf = 0.5 * (1.0 + t)
    return cdf, t


def _band_mask(n):
    a = lax.broadcasted_iota(jnp.int32, (CHUNK, 2 * CHUNK), 0)
    j = lax.broadcasted_iota(jnp.int32, (CHUNK, 2 * CHUNK), 1)
    dist = CHUNK + a - j
    valid = (dist >= 0) & (dist < CHUNK)
    return valid & ((n > 0) | (j >= CHUNK))


def _fill_bias(bucket_ref, table_ref, bias_ref):
    bucket = bucket_ref[...]
    for h in range(N_HEAD):
        acc = jnp.zeros((CHUNK, 2 * CHUNK), F32)
        for b in range(N_BUCKET):
            acc = jnp.where(bucket == b, table_ref[b, h], acc)
        bias_ref[h] = acc


def _fill_tril(ws_ref, wt_ref, wtt_ref=None):
    r = lax.broadcasted_iota(jnp.int32, (CHUNK, CHUNK), 0)
    c = lax.broadcasted_iota(jnp.int32, (CHUNK, CHUNK), 1)
    for g in range(N_GROUP):
        w = jnp.where(c <= r, ws_ref[g], 0.0)
        wt_ref[g] = w.astype(BF16)
        if wtt_ref is not None:
            wtt_ref[g] = w.T.astype(BF16)


def _kv_layouts(kv_prev, kv_cur):
    both = jnp.concatenate([kv_prev, kv_cur], axis=0)
    k = both[:, :128]
    v = both[:, 128:]
    return (k.astype(BF16), pltpu.roll(k, 64, axis=1).astype(BF16),
            v.astype(BF16), pltpu.roll(v, 64, axis=1).astype(BF16))


def _head_place(h):
    pair, pos, kvh = h // 2, h % 2, h // 4
    return pair, pos, kvh == pos


def _softmax_sink(qm, k_use, bias_h, sink, valid):
    s = _dot_nt(qm, k_use) * QK_SCALE + bias_h
    s = jnp.where(valid, s, NEG_INF)
    m = jnp.maximum(jnp.max(s, axis=-1, keepdims=True), sink)
    e = jnp.exp(s - m)
    es = jnp.exp(sink - m)
    inv = 1.0 / (jnp.sum(e, axis=-1, keepdims=True) + es)
    return e * inv, es * inv


def _mixer_fwd(z, v_gain, w_spatial, b_spatial_t, sinks, rel_table, bucket, exchange=None):
    def body(z_ref, kvp_ref, gain_ref, ws_ref, bt_ref, sink_ref, table_ref, bucket_ref, out_ref, bias_ref, wt_ref):
        n = pl.program_id(0)

        @pl.when(n == 0)
        def _():
            _fill_bias(bucket_ref, table_ref, bias_ref)
            _fill_tril(ws_ref, wt_ref)

        zuv = z_ref[:, :1024]
        cdf, _ = _gelu_parts(zuv)
        guv = zuv * cdf
        for g in range(N_GROUP):
            vg = guv[:, 512 + 128 * g:512 + 128 * (g + 1)]
            vn = vg * _rms_scale(vg) * gain_ref[:, 128 * g:128 * (g + 1)]
            sv = _dot(wt_ref[g], vn.astype(BF16)) + bt_ref[:, g:g + 1]
            out_ref[:, 128 * g:128 * (g + 1)] = (guv[:, 128 * g:128 * (g + 1)] * sv).astype(BF16)

        k_same, k_swap, v_same, v_swap = _kv_layouts(kvp_ref[...], z_ref[:, 1536:1792])
        valid = _band_mask(n)
        lane_half = lax.broadcasted_iota(jnp.int32, (1, 128), 1) // 64
        for pair in range(N_HEAD // 2):
            qq = z_ref[:, 1024 + 128 * pair:1024 + 128 * (pair + 1)]
            acc = jnp.zeros((CHUNK, 128), F32)
            for pos in range(2):
                h = 2 * pair + pos
                _, _, same = _head_place(h)
                qm = jnp.where(lane_half == pos, qq, 0.0).astype(BF16)
                p, _ = _softmax_sink(qm, k_same if same else k_swap, bias_ref[h], sink_ref[h], valid)
                vm = jnp.where(lane_half == pos, v_same if same else v_swap, jnp.zeros((), BF16))
                acc = acc + _dot(p.astype(BF16), vm)
            out_ref[:, 512 + 128 * pair:512 + 128 * (pair + 1)] = acc.astype(BF16)

    return _call(
        body, (z, z, v_gain, w_spatial, b_spatial_t, sinks, rel_table, bucket), grid=(N_BLOCK,), name="mixer_fwd",
        in_specs=[pl.BlockSpec((CHUNK, D_IN), lambda n: (n, 0)),
                  pl.BlockSpec((CHUNK, 256), lambda n: (jnp.maximum(n - 1, 0), 6)),
                  pl.BlockSpec((1, 512), lambda n: (0, 0)),
                  pl.BlockSpec((N_GROUP, CHUNK, CHUNK), lambda n: (0, 0, 0)),
                  pl.BlockSpec((CHUNK, N_GROUP), lambda n: (0, 0)),
                  pl.BlockSpec(memory_space=pltpu.SMEM),
                  pl.BlockSpec(memory_space=pltpu.SMEM),
                  pl.BlockSpec((CHUNK, 2 * CHUNK), lambda n: (0, 0))],
        out_specs=[pl.BlockSpec((CHUNK, D), lambda n: (n, 0))],
        out_shape=[jax.ShapeDtypeStruct((S, D), BF16)],
        scratch_shapes=[pltpu.VMEM((N_HEAD, CHUNK, 2 * CHUNK), F32), pltpu.VMEM((N_GROUP, CHUNK, CHUNK), BF16)],
        compiler_params=_params(32), exchange=exchange)


def _out_proj(x, mix, w_out, gain2, exchange=None):
    tm = 256

    def body(x_ref, mix_ref, w_ref, g_ref, h1_ref, hn_ref, hnt_ref):
        h1 = x_ref[...] + _dot(mix_ref[...], w_ref[...])
        h1_ref[...] = h1
        hn = h1 * _rms_scale(h1) * g_ref[...]
        hn_ref[...] = hn.astype(BF16)
        hnt_ref[...] = hn.T.astype(BF16)

    return _call(
        body, (x, mix, w_out, gain2), grid=(S // tm,), name="out_proj",
        in_specs=[pl.BlockSpec((tm, D), lambda i: (i, 0)), pl.BlockSpec((tm, D), lambda i: (i, 0)),
                  pl.BlockSpec((D, D), lambda i: (0, 0)), pl.BlockSpec((1, D), lambda i: (0, 0))],
        out_specs=[pl.BlockSpec((tm, D), lambda i: (i, 0)), pl.BlockSpec((tm, D), lambda i: (i, 0)),
                   pl.BlockSpec((D, tm), lambda i: (0, i))],
        out_shape=[jax.ShapeDtypeStruct((S, D), F32), jax.ShapeDtypeStruct((S, D), BF16),
                   jax.ShapeDtypeStruct((D, S), BF16)],
        compiler_params=_params(32), exchange=exchange)


def _ffn_up(hn2, w_ff1, exchange=None):
    tm = 512
    nj = D_FF // 1024

    def body(hn_ref, w1_ref, r_ref, a_ref, at_ref):
        r = jnp.maximum(_dot(hn_ref[...], w1_ref[0]), 0.0)
        r_ref[...] = r.astype(BF16)
        a = r * r
        a_ref[...] = a.astype(BF16)
        at_ref[...] = a.T.astype(BF16)

    return _call(
        body, (hn2, w_ff1), grid=(nj, S // tm), name="ffn_up",
        in_specs=[pl.BlockSpec((tm, D), lambda j, i: (i, 0)), pl.BlockSpec((1, D, 1024), lambda j, i: (j, 0, 0))],
        out_specs=[pl.BlockSpec((tm, 1024), lambda j, i: (i, j)), pl.BlockSpec((tm, 1024), lambda j, i: (i, j)),
                   pl.BlockSpec((1024, tm), lambda j, i: (j, i))],
        out_shape=[jax.ShapeDtypeStruct((S, D_FF), BF16), jax.ShapeDtypeStruct((S, D_FF), BF16),
                   jax.ShapeDtypeStruct((D_FF, S), BF16)],
        compiler_params=_params(40, 2), exchange=exchange)


def _ffn_down(h1, a, w_ff2, exchange=None):
    tm = 1024
    nj = D_FF // 1024

    def body(h1_ref, a_ref, w2_ref, h2_ref, acc_ref):
        j = pl.program_id(1)
        part = _dot(a_ref[...], w2_ref[0])

        @pl.when(j == 0)
        def _():
            acc_ref[...] = part

        @pl.when(j > 0)
        def _():
            acc_ref[...] += part

        @pl.when(j == nj - 1)
        def _():
            h2_ref[...] = h1_ref[...] + acc_ref[...]

    return _call(
        body, (h1, a, w_ff2), grid=(S // tm, nj), name="ffn_down",
        in_specs=[pl.BlockSpec((tm, D), lambda i, j: (i, 0)), pl.BlockSpec((tm, 1024), lambda i, j: (i, j)),
                  pl.BlockSpec((1, 1024, D), lambda i, j: (j, 0, 0))],
        out_specs=[pl.BlockSpec((tm, D), lambda i, j: (i, 0))],
        out_shape=[jax.ShapeDtypeStruct((S, D), F32)],
        scratch_shapes=[pltpu.VMEM((tm, D), F32)],
        compiler_params=_params(48, 2), exchange=exchange)


def _tail(h2, p, target, w_gate, w_proj, final_gain):
    tm = 256
    steps = S // tm

    def body(h2_ref, p_ref, t_ref, wg_ref, wp_ref, gf_ref, dh2_ref, dwg_ref, dwp_ref, dgf_ref, loss_ref, dh2b_ref,
             dwp_acc):
        i = pl.program_id(0)
        h2 = h2_ref[...]
        h2b = h2.astype(BF16)
        pb = p_ref[...].astype(BF16)
        gate = jax.nn.sigmoid(_dot(h2b, wg_ref[...]))
        pp = jnp.concatenate([_dot(pb, wp_ref[j]) for j in range(N_CHIP)], axis=1)
        h3 = h2 + gate * pp
        r3 = _rms_scale(h3)
        xhat = h3 * r3
        gf = gf_ref[...]
        err = xhat * gf - t_ref[...]
        dy = err * (1.0 / D)
        dh3 = _rms_bwd(dy * gf, xhat, r3)
        dgp = (dh3 * pp * gate * (1.0 - gate)).astype(BF16)
        dpp = (dh3 * gate).astype(BF16)
        dh2 = dh3 + _dot_nt(dgp, wg_ref[...])
        dh2_ref[...] = dh2
        dh2b_ref[...] = dh2.astype(BF16)
        dwg = _dot_tn(h2b, dgp)
        dwp = _dot_tn(pb, dpp)
        dgf = jnp.sum(dy * xhat, axis=0, keepdims=True)
        sq = jnp.sum(jnp.sum(err * err, axis=1, keepdims=True), axis=0, keepdims=True)

        @pl.when(i == 0)
        def _():
            dwg_ref[...] = dwg
            dwp_acc[...] = dwp
            dgf_ref[...] = dgf
            loss_ref[...] = jnp.broadcast_to(sq, (8, 128))

        @pl.when(i > 0)
        def _():
            dwg_ref[...] += dwg
            dwp_acc[...] += dwp
            dgf_ref[...] += dgf
            loss_ref[...] += jnp.broadcast_to(sq, (8, 128))

        @pl.when(i == steps - 1)
        def _():
            for j in range(N_CHIP):
                dwp_ref[j] = dwp_acc[:, 256 * j:256 * (j + 1)]

    return _call(
        body, (h2, p, target, w_gate, w_proj, final_gain), grid=(steps,), name="tail",
        in_specs=[pl.BlockSpec((tm, D), lambda i: (i, 0)), pl.BlockSpec((tm, PLE), lambda i: (i, 0)),
                  pl.BlockSpec((tm, D), lambda i: (i, 0)), pl.BlockSpec((D, D), lambda i: (0, 0)),
                  pl.BlockSpec((N_CHIP, PLE, 256), lambda i: (0, 0, 0)), pl.BlockSpec((1, D), lambda i: (0, 0))],
        out_specs=[pl.BlockSpec((tm, D), lambda i: (i, 0)), pl.BlockSpec((D, D), lambda i: (0, 0)),
                   pl.BlockSpec((N_CHIP, PLE, 256), lambda i: (0, 0, 0)), pl.BlockSpec((1, D), lambda i: (0, 0)),
                   pl.BlockSpec((8, 128), lambda i: (0, 0)), pl.BlockSpec((tm, D), lambda i: (i, 0))],
        out_shape=[jax.ShapeDtypeStruct((S, D), F32), jax.ShapeDtypeStruct((D, D), F32),
                   jax.ShapeDtypeStruct((N_CHIP, PLE, 256), F32), jax.ShapeDtypeStruct((1, D), F32),
                   jax.ShapeDtypeStruct((8, 128), F32), jax.ShapeDtypeStruct((S, D), BF16)],
        scratch_shapes=[pltpu.VMEM((PLE, D), F32)],
        compiler_params=_params(48))[0]


def _ffn_bwd_act(dh2b, r, w_ff2, exchange=None):
    tm = 1024
    nj = D_FF // 1024

    def body(dh2_ref, r_ref, w2_ref, df_ref):
        da = _dot_nt(dh2_ref[...], w2_ref[0])
        df_ref[...] = (da * (2.0 * r_ref[...].astype(F32))).astype(BF16)

    return _call(
        body, (dh2b, r, w_ff2), grid=(nj, S // tm), name="ffn_bwd_act",
        in_specs=[pl.BlockSpec((tm, D), lambda j, i: (i, 0)), pl.BlockSpec((tm, 1024), lambda j, i: (i, j)),
                  pl.BlockSpec((1, 1024, D), lambda j, i: (j, 0, 0))],
        out_specs=[pl.BlockSpec((tm, 1024), lambda j, i: (i, j))],
        out_shape=[jax.ShapeDtypeStruct((S, D_FF), BF16)],
        compiler_params=_params(40, 2), exchange=exchange)


def _ffn_bwd_weight(lhs_t, rhs, lhs_blocked, name, exchange=None):
    nj = D_FF // 1024

    def body(lhs_ref, rhs_ref, dw_ref):
        dw_ref[0] = _dot(lhs_ref[...], rhs_ref[...])

    if lhs_blocked:
        in_specs = [pl.BlockSpec((1024, S), lambda j: (j, 0)), pl.BlockSpec((S, D), lambda j: (0, 0))]
    else:
        in_specs = [pl.BlockSpec((D, S), lambda j: (0, 0)), pl.BlockSpec((S, 1024), lambda j: (0, j))]
    return _call(
        body, (lhs_t, rhs), grid=(nj,), name=name, in_specs=in_specs,
        out_specs=[pl.BlockSpec((1, 1024, 1024), lambda j: (j, 0, 0))],
        out_shape=[jax.ShapeDtypeStruct((nj, 1024, 1024), F32)],
        compiler_params=_params(40), exchange=exchange)


def _ffn_bwd_input(df, w_ff1, dh2, h1, gain2, mix, w_out, exchange=None):
    tm = 512
    nj = D_FF // 1024
    steps = S // tm

    def body(df_ref, w1_ref, dh2_ref, h1_ref, g_ref, mix_ref, wo_ref, dh1_ref, dmix_ref, dwo_ref, dg_ref, acc_ref):
        i = pl.program_id(0)
        j = pl.program_id(1)
        part = _dot_nt(df_ref[...], w1_ref[0])

        @pl.when(j == 0)
        def _():
            acc_ref[...] = part

        @pl.when(j > 0)
        def _():
            acc_ref[...] += part

        @pl.when(j == nj - 1)
        def _():
            dhn = acc_ref[...]
            h1 = h1_ref[...]
            r2 = _rms_scale(h1)
            xhat = h1 * r2
            dh1 = dh2_ref[...] + _rms_bwd(dhn * g_ref[...], xhat, r2)
            dh1_ref[...] = dh1
            dh1b = dh1.astype(BF16)
            dmix_ref[...] = _dot_nt(dh1b, wo_ref[...])
            dwo = _dot_tn(mix_ref[...], dh1b)
            dg = jnp.sum(dhn * xhat, axis=0, keepdims=True)

            @pl.when(i == 0)
            def _():
                dwo_ref[...] = dwo
                dg_ref[...] = dg

            @pl.when(i > 0)
            def _():
                dwo_ref[...] += dwo
                dg_ref[...] += dg

    return _call(
        body, (df, w_ff1, dh2, h1, gain2, mix, w_out), grid=(steps, nj), name="ffn_bwd_input",
        in_specs=[pl.BlockSpec((tm, 1024), lambda i, j: (i, j)), pl.BlockSpec((1, D, 1024), lambda i, j: (j, 0, 0)),
                  pl.BlockSpec((tm, D), lambda i, j: (i, 0)), pl.BlockSpec((tm, D), lambda i, j: (i, 0)),
                  pl.BlockSpec((1, D), lambda i, j: (0, 0)), pl.BlockSpec((tm, D), lambda i, j: (i, 0)),
                  pl.BlockSpec((D, D), lambda i, j: (0, 0))],
        out_specs=[pl.BlockSpec((tm, D), lambda i, j: (i, 0)), pl.BlockSpec((tm, D), lambda i, j: (i, 0)),
                   pl.BlockSpec((D, D), lambda i, j: (0, 0)), pl.BlockSpec((1, D), lambda i, j: (0, 0))],
        out_shape=[jax.ShapeDtypeStruct((S, D), F32), jax.ShapeDtypeStruct((S, D), F32),
                   jax.ShapeDtypeStruct((D, D), F32), jax.ShapeDtypeStruct((1, D), F32)],
        scratch_shapes=[pltpu.VMEM((tm, D), F32)],
        compiler_params=_params(56, 2), exchange=exchange)


IN_GROUP = 8


def _mixer_bwd(z, dmix, v_gain, w_spatial, b_spatial_t, sinks, rel_table, bucket, hn1, exchange=None):
    def body(z_ref, kvp_ref, dm_ref, gain_ref, ws_ref, bt_ref, sink_ref, table_ref, bucket_ref, hn_ref,
             dz_ref, dws_ref, db_ref, dgain_ref, dsink_ref, drel_ref, dwin_ref,
             bias_ref, wt_ref, wtt_ref, dbias_ref, dsv_ref, carry_ref):
        n = pl.program_id(0)

        @pl.when(n == 0)
        def _():
            _fill_bias(bucket_ref, table_ref, bias_ref)
            _fill_tril(ws_ref, wt_ref, wtt_ref)
            dwin_ref[...] = jnp.zeros_like(dwin_ref)
            dbias_ref[...] = jnp.zeros_like(dbias_ref)
            dsv_ref[...] = jnp.zeros_like(dsv_ref)
            dws_ref[...] = jnp.zeros_like(dws_ref)
            dgain_ref[...] = jnp.zeros_like(dgain_ref)
            dsink_ref[...] = jnp.zeros_like(dsink_ref)

        rows = pl.ds(pl.multiple_of(n * CHUNK, CHUNK), CHUNK)

        zuv = z_ref[:, :1024]
        cdf, t = _gelu_parts(zuv)
        guv = zuv * cdf
        dgelu = cdf + zuv * (0.5 * (1.0 - t * t)) * (GELU_C * (1.0 + 3.0 * 0.044715 * (zuv * zuv)))
        for g in range(N_GROUP):
            lo, hi = 128 * g, 128 * (g + 1)
            u = guv[:, lo:hi]
            vg = guv[:, 512 + lo:512 + hi]
            rr = _rms_scale(vg)
            vhat = vg * rr
            gain = gain_ref[:, lo:hi]
            vnb = (vhat * gain).astype(BF16)
            sv = _dot(wt_ref[g], vnb) + bt_ref[:, g:g + 1]
            da = dm_ref[:, lo:hi]
            dsv = da * u
            dsvb = dsv.astype(BF16)
            dsv_ref[g] += dsv
            dws_ref[g] += _dot_nt(dsvb, vnb)
            dvn = _dot(wtt_ref[g], dsvb)
            dgain_ref[:, lo:hi] += jnp.sum(dvn * vhat, axis=0, keepdims=True)
            dvg = _rms_bwd(dvn * gain, vhat, rr)
            dz_ref[rows, lo:hi] = (da * sv * dgelu[:, lo:hi]).astype(BF16)
            dz_ref[rows, 512 + lo:512 + hi] = (dvg * dgelu[:, 512 + lo:512 + hi]).astype(BF16)

        k_same, k_swap, v_same, v_swap = _kv_layouts(kvp_ref[...], z_ref[:, 1536:1792])
        valid = _band_mask(n)
        lane_half = lax.broadcasted_iota(jnp.int32, (1, 128), 1) // 64
        zero = jnp.zeros((2 * CHUNK, 128), F32)
        dk_same, dk_swap, dv_same, dv_swap = zero, zero, zero, zero
        for pair in range(N_HEAD // 2):
            cols = slice(1024 + 128 * pair, 1024 + 128 * (pair + 1))
            qq = z_ref[:, cols]
            do_pair = dm_ref[:, 512 + 128 * pair:512 + 128 * (pair + 1)]
            dq = jnp.zeros((CHUNK, 128), F32)
            for pos in range(2):
                h = 2 * pair + pos
                _, _, same = _head_place(h)
                on_half = lane_half == pos
                qm = jnp.where(on_half, qq, 0.0).astype(BF16)
                k_use = k_same if same else k_swap
                v_use = v_same if same else v_swap
                p, p_sink = _softmax_sink(qm, k_use, bias_ref[h], sink_ref[h], valid)
                dom = jnp.where(on_half, do_pair, 0.0).astype(BF16)
                dp = _dot_nt(dom, v_use)
                dsum = jnp.sum(p * dp, axis=-1, keepdims=True)
                ds = p * (dp - dsum)
                dbias_ref[h] += ds
                dsink_ref[h:h + 1, :] += jnp.broadcast_to(jnp.sum(-p_sink * dsum, axis=0, keepdims=True), (1, 128))
                dsb = ds.astype(BF16)
                dq = dq + jnp.where(on_half, _dot(dsb, k_use), 0.0)
                dk_h = _dot_tn(dsb, qm)
                dv_h = _dot_tn(p.astype(BF16), dom)
                if same:
                    dk_same, dv_same = dk_same + dk_h, dv_same + dv_h
                else:
                    dk_swap, dv_swap = dk_swap + dk_h, dv_swap + dv_h
            dz_ref[rows, cols] = (dq * QK_SCALE).astype(BF16)
        dk = (dk_same + pltpu.roll(dk_swap, 64, axis=1)) * QK_SCALE
        dv = dv_same + pltpu.roll(dv_swap, 64, axis=1)
        dkv = jnp.concatenate([dk, dv], axis=1)

        @pl.when(n > 0)
        def _():
            prev_rows = pl.ds(pl.multiple_of((n - 1) * CHUNK, CHUNK), CHUNK)
            dz_ref[prev_rows, 1536:1792] = (carry_ref[...] + dkv[:CHUNK]).astype(BF16)

        carry_ref[...] = dkv[CHUNK:]

        @pl.when((n > 0) & (n % IN_GROUP == 0))
        def _():
            done = pl.ds(pl.multiple_of((n - IN_GROUP) * CHUNK, IN_GROUP * CHUNK), IN_GROUP * CHUNK)
            dwin_ref[...] += _dot_tn(dz_ref[done, :], hn_ref[...])

        @pl.when(n == N_BLOCK - 1)
        def _():
            dz_ref[rows, 1536:1792] = dkv[CHUNK:].astype(BF16)
            last = pl.ds((N_BLOCK - IN_GROUP) * CHUNK, IN_GROUP * CHUNK)
            dwin_ref[...] += _dot_tn(dz_ref[last, :], hn_ref[...])
            r = lax.broadcasted_iota(jnp.int32, (CHUNK, CHUNK), 0)
            c = lax.broadcasted_iota(jnp.int32, (CHUNK, CHUNK), 1)
            for g in range(N_GROUP):
                dws_ref[g] = jnp.where(c <= r, dws_ref[g], 0.0)
                db_ref[g] = jnp.sum(dsv_ref[g], axis=1, keepdims=True)
            bucket = bucket_ref[...]
            for h in range(N_HEAD):
                dbh = dbias_ref[h]
                per_bucket = [jnp.sum(jnp.where(bucket == b, dbh, 0.0), axis=0, keepdims=True) for b in range(N_BUCKET)]
                drel_ref[h] = jnp.sum(jnp.concatenate(per_bucket, axis=0), axis=1, keepdims=True)

    def hn_group(n):
        return jnp.where(n == N_BLOCK - 1, N_BLOCK // IN_GROUP - 1, jnp.maximum(n // IN_GROUP - 1, 0))

    return _call(
        body, (z, z, dmix, v_gain, w_spatial, b_spatial_t, sinks, rel_table, bucket, hn1), grid=(N_BLOCK,),
        name="mixer_bwd",
        in_specs=[pl.BlockSpec((CHUNK, D_IN), lambda n: (n, 0)),
                  pl.BlockSpec((CHUNK, 256), lambda n: (jnp.maximum(n - 1, 0), 6)),
                  pl.BlockSpec((CHUNK, D), lambda n: (n, 0)),
                  pl.BlockSpec((1, 512), lambda n: (0, 0)),
                  pl.BlockSpec((N_GROUP, CHUNK, CHUNK), lambda n: (0, 0, 0)),
                  pl.BlockSpec((CHUNK, N_GROUP), lambda n: (0, 0)),
                  pl.BlockSpec(memory_space=pltpu.SMEM),
                  pl.BlockSpec(memory_space=pltpu.SMEM),
                  pl.BlockSpec((CHUNK, 2 * CHUNK), lambda n: (0, 0)),
                  pl.BlockSpec((IN_GROUP * CHUNK, D), lambda n: (hn_group(n), 0))],
        out_specs=[pl.BlockSpec((S, D_IN), lambda n: (0, 0)),
                   pl.BlockSpec((N_GROUP, CHUNK, CHUNK), lambda n: (0, 0, 0)),
                   pl.BlockSpec((N_GROUP, CHUNK, 1), lambda n: (0, 0, 0)),
                   pl.BlockSpec((1, 512), lambda n: (0, 0)),
                   pl.BlockSpec((N_HEAD, 128), lambda n: (0, 0)),
                   pl.BlockSpec((N_HEAD, N_BUCKET, 1), lambda n: (0, 0, 0)),
                   pl.BlockSpec((D_IN, D), lambda n: (0, 0))],
        out_shape=[jax.ShapeDtypeStruct((S, D_IN), BF16), jax.ShapeDtypeStruct((N_GROUP, CHUNK, CHUNK), F32),
                   jax.ShapeDtypeStruct((N_GROUP, CHUNK, 1), F32), jax.ShapeDtypeStruct((1, 512), F32),
                   jax.ShapeDtypeStruct((N_HEAD, 128), F32), jax.ShapeDtypeStruct((N_HEAD, N_BUCKET, 1), F32),
                   jax.ShapeDtypeStruct((D_IN, D), F32)],
        scratch_shapes=[pltpu.VMEM((N_HEAD, CHUNK, 2 * CHUNK), F32), pltpu.VMEM((N_GROUP, CHUNK, CHUNK), BF16),
                        pltpu.VMEM((N_GROUP, CHUNK, CHUNK), BF16), pltpu.VMEM((N_HEAD, CHUNK, 2 * CHUNK), F32),
                        pltpu.VMEM((N_GROUP, CHUNK, CHUNK), F32), pltpu.VMEM((CHUNK, 256), F32)],
        compiler_params=_params(56), exchange=exchange)


def _in_bwd_input(dz, w_in_t, x, dh1, gain1, exchange=None):
    tm = 512

    def body(dz_ref, w_ref, x_ref, dh1_ref, g_ref, dx_ref, dg_ref):
        i = pl.program_id(0)
        dhn = _dot(dz_ref[...], w_ref[...])
        xv = x_ref[...]
        r1 = _rms_scale(xv)
        xhat = xv * r1
        dx_ref[...] = dh1_ref[...] + _rms_bwd(dhn * g_ref[...], xhat, r1)
        dg = jnp.sum(dhn * xhat, axis=0, keepdims=True)

        @pl.when(i == 0)
        def _():
            dg_ref[...] = dg

        @pl.when(i > 0)
        def _():
            dg_ref[...] += dg

    return _call(
        body, (dz, w_in_t, x, dh1, gain1), grid=(S // tm,), name="in_bwd_input",
        in_specs=[pl.BlockSpec((tm, D_IN), lambda i: (i, 0)), pl.BlockSpec((D_IN, D), lambda i: (0, 0)),
                  pl.BlockSpec((tm, D), lambda i: (i, 0)), pl.BlockSpec((tm, D), lambda i: (i, 0)),
                  pl.BlockSpec((1, D), lambda i: (0, 0))],
        out_specs=[pl.BlockSpec((tm, D), lambda i: (i, 0)), pl.BlockSpec((1, D), lambda i: (0, 0))],
        out_shape=[jax.ShapeDtypeStruct((S, D), F32), jax.ShapeDtypeStruct((1, D), F32)],
        compiler_params=_params(48), exchange=exchange)


def _rel_bucket():
    a = jnp.arange(CHUNK)[:, None]
    j = jnp.arange(2 * CHUNK)[None, :]
    n = jnp.maximum(CHUNK + a - j, 0)
    max_exact = N_BUCKET // 2
    nf = jnp.maximum(n, 1).astype(jnp.float32)
    large = max_exact + (jnp.log(nf / max_exact) / math.log(CHUNK / max_exact) * (N_BUCKET - max_exact)).astype(jnp.int32)
    large = jnp.minimum(large, N_BUCKET - 1)
    return jnp.where(n < max_exact, n, large).astype(jnp.int32)


def _step(x, p, target, small, bufs, place):
    bucket = _rel_bucket()
    sinks = small["attn_sinks"].reshape(N_HEAD)
    b_t = jnp.transpose(small["b_spatial"].reshape(N_GROUP, CHUNK))
    ws = small["w_spatial"].reshape(N_GROUP, CHUNK, CHUNK)
    gain1, gain2 = small["norm1_gain"], small["norm2_gain"]
    v_gain = small["gmlp_v_gain"]
    final_gain = small["final_gain"].reshape(1, D)
    table = small["rel_bias_table"]
    bufs = dict(bufs)

    def gather(*names):
        return _RelayGather([bufs[n] for n in names])

    def took(names, got):
        bufs.update(zip(names, got))

    took(["w_in"], _gather_weights([bufs["w_in"]]))
    w_in_t = _whole(bufs["w_in"]).reshape(D_IN, D)
    (z, hn1), got = _in_proj(x, gain1, w_in_t, gather("w_out"))
    took(["w_out"], got)
    (mix,), got = _mixer_fwd(z, v_gain, ws, b_t, sinks, table, bucket, gather("w_ff1"))
    took(["w_ff1"], got)
    w_out = _whole(bufs["w_out"]).reshape(D, D)
    (h1, hn2, hn2_t), _ = _out_proj(x, mix, w_out, gain2)
    w_ff1 = _whole(bufs["w_ff1"])
    (r, a, a_t), got = _ffn_up(hn2, w_ff1, gather("w_ff2"))
    took(["w_ff2"], got)
    w_ff2 = _whole(bufs["w_ff2"])
    (h2,), got = _ffn_down(h1, a, w_ff2, gather("w_ple_gate", "w_ple_proj"))
    took(["w_ple_gate", "w_ple_proj"], got)
    dh2, d_gate, d_proj, d_final, sq, dh2b = _tail(h2, p, target, _whole(bufs["w_ple_gate"]).reshape(D, D),
                                                   _whole(bufs["w_ple_proj"]), final_gain)

    def pair_sums(halves, from_sibling):
        sums, landing = zip(*[_pair_sum(g, o, place) for g, o in zip(halves, from_sibling)])
        return list(sums), list(landing)

    landed = {}
    halves = [_halves(d_gate.reshape(N_CHIP, 256, D)), _halves(d_proj)]
    (df,), got = _ffn_bwd_act(dh2b, r, w_ff2, _SiblingExchange(halves))
    (d_ff2,), got = _ffn_bwd_weight(a_t, dh2b, True, "ffn_bwd_down", _ChipExchange(*pair_sums(halves, got)))
    landed.update(zip(["w_ple_gate", "w_ple_proj"], got))
    halves = [_halves(d_ff2)]
    (d_ff1,), got = _ffn_bwd_weight(hn2_t, df, False, "ffn_bwd_up", _SiblingExchange(halves))
    ex, halves = _ChipExchange(*pair_sums(halves, got)), [_halves(d_ff1)]
    (dh1, dmix, d_out, d_gain2), got = _ffn_bwd_input(df, w_ff1, dh2, h1, gain2, mix, w_out,
                                                      _Both(ex, _SiblingExchange(halves)))
    landed["w_ff2"] = got[0]
    ex, halves = _ChipExchange(*pair_sums(halves, got[1:])), [_halves(d_out.reshape(N_CHIP, 256, D))]
    (dz, d_ws, d_b, d_vgain, d_sink, d_rel, d_in_t), got = _mixer_bwd(z, dmix, v_gain, ws, b_t, sinks, table, bucket, hn1,
                                                                     _Both(ex, _SiblingExchange(halves)))
    landed["w_ff1"] = got[0]
    small_grads = {
        "gmlp_v_gain": d_vgain, "w_spatial": d_ws.reshape(1, N_GROUP, CHUNK, CHUNK),
        "b_spatial": d_b.reshape(1, N_GROUP, CHUNK), "attn_sinks": d_sink[:, 0].reshape(1, N_HEAD),
        "rel_bias_table": jnp.transpose(d_rel.reshape(N_HEAD, N_BUCKET)), "norm2_gain": d_gain2,
        "final_gain": d_final.reshape(D),
    }
    ex, halves = _ChipExchange(*pair_sums(halves, got[1:])), [_halves(d_in_t.reshape(N_CHIP, 448, D))]
    (dx, small_grads["norm1_gain"]), got = _in_bwd_input(dz, w_in_t, x, dh1, gain1, _Both(ex, _SiblingExchange(halves)))
    landed["w_out"] = got[0]
    return dx, landed, _ChipExchange(*pair_sums(halves, got[1:])), small_grads, sq


HBM_SPEC = pl.BlockSpec(memory_space=pltpu.HBM)
VMEM_SPEC = pl.BlockSpec(memory_space=pltpu.VMEM)


def _mesh_place():
    x, y, c = lax.axis_index("x"), lax.axis_index("y"), lax.axis_index("c")
    others = [(1 - x, y), (x, 1 - y), (1 - x, 1 - y)]
    return x, y, c, others


def _remote(src, dst, send_sem, recv_sem, device):
    return pltpu.make_async_remote_copy(src_ref=src, dst_ref=dst, send_sem=send_sem, recv_sem=recv_sem,
                                        device_id=device, device_id_type=MESH)


def _hbm_like(a, shape=None, dtype=None):
    return pltpu.HBM(a.shape if shape is None else shape, a.dtype if dtype is None else dtype)


def _gather_start(bufs, send_sems, recv_sems):
    x, y, c, others = _mesh_place()
    me = 2 * x + y
    for w, buf in enumerate(bufs):
        for k in range(3):
            mine = buf.at[me, c]
            _remote(mine, mine, send_sems.at[w, k], recv_sems.at[w, k], (*others[k], c)).start()


def _gather_finish(bufs, send_sems, recv_sems):
    x, y, c, others = _mesh_place()
    me = 2 * x + y
    sibling = (x, y, 1 - c)
    idx = [2 * ox + oy for ox, oy in others]
    chips = range(3)
    for w, buf in enumerate(bufs):
        for k in chips:
            landed = buf.at[idx[k], c]
            _remote(landed, landed, send_sems.at[w, k], recv_sems.at[w, k], sibling).wait_recv()
            _remote(landed, landed, send_sems.at[w, 3 + k], recv_sems.at[w, 3 + k], sibling).start()
    for w, buf in enumerate(bufs):
        for k in chips:
            landed = buf.at[idx[k], 1 - c]
            _remote(landed, landed, send_sems.at[w, 3 + k], recv_sems.at[w, 3 + k], sibling).wait_recv()
    for w, buf in enumerate(bufs):
        for k in chips:
            mine, passed = buf.at[me, c], buf.at[idx[k], c]
            _remote(mine, mine, send_sems.at[w, k], recv_sems.at[w, k], sibling).wait_send()
            _remote(passed, passed, send_sems.at[w, 3 + k], recv_sems.at[w, 3 + k], sibling).wait_send()


def _gather_sems(n):
    return [pltpu.SemaphoreType.DMA((n, 6)), pltpu.SemaphoreType.DMA((n, 6))]


def _gather_weights(bufs):
    n = len(bufs)

    def body(*refs):
        outs = refs[n:2 * n]
        send_sems, recv_sems = refs[2 * n:]
        _gather_start(outs, send_sems, recv_sems)
        _gather_finish(outs, send_sems, recv_sems)

    return pl.pallas_call(
        body, name="gather_weights",
        in_specs=[HBM_SPEC] * n, out_specs=[HBM_SPEC] * n,
        out_shape=[_hbm_like(b) for b in bufs],
        input_output_aliases={w: w for w in range(n)},
        scratch_shapes=_gather_sems(n),
    )(*bufs)


def _sibling_copies(grads, landing, send_sems, recv_sems):
    x, y, c, _ = _mesh_place()
    return [_remote(grads[w].at[j, 1 - c], landing[w].at[j], send_sems.at[w, j], recv_sems.at[w, j], (x, y, 1 - c))
            for w in range(len(grads)) for j in range(N_CHIP)]


def _sibling_exchange_start(grads, landing, send_sems, recv_sems):
    for cp in _sibling_copies(grads, landing, send_sems, recv_sems):
        cp.start()


def _sibling_exchange_finish(grads, landing, send_sems, recv_sems):
    copies = _sibling_copies(grads, landing, send_sems, recv_sems)
    for cp in copies:
        cp.wait_recv()
    for cp in copies:
        cp.wait_send()


def _sibling_exchange_sems(n):
    return [pltpu.SemaphoreType.DMA((n, N_CHIP)), pltpu.SemaphoreType.DMA((n, N_CHIP))]


def _sibling_exchange(grads):
    n = len(grads)

    def body(*refs):
        ins, outs = refs[:n], refs[n:2 * n]
        _sibling_exchange_start(ins, outs, *refs[2 * n:])
        _sibling_exchange_finish(ins, outs, *refs[2 * n:])

    return pl.pallas_call(
        body, name="sibling_exchange",
        in_specs=[HBM_SPEC] * n, out_specs=[HBM_SPEC] * n,
        out_shape=[_hbm_like(g, (N_CHIP,) + g.shape[2:]) for g in grads],
        scratch_shapes=_sibling_exchange_sems(n),
    )(*[_in_hbm(g) for g in grads])


def _chip_exchange_start(sums, landing, send_sems, recv_sems):
    x, y, c, others = _mesh_place()
    me = 2 * x + y
    for w in range(len(sums)):
        for k, (ox, oy) in enumerate(others):
            _remote(sums[w].at[2 * ox + oy], landing[w].at[me], send_sems.at[w, k], recv_sems.at[w, k],
                    (ox, oy, c)).start()


def _chip_exchange_finish(sums, landing, send_sems, recv_sems):
    x, y, c, others = _mesh_place()
    for w in range(len(sums)):
        for k, (ox, oy) in enumerate(others):
            piece = landing[w].at[2 * ox + oy]
            _remote(piece, piece, send_sems.at[w, k], recv_sems.at[w, k], (x, y, c)).wait_recv()
    for w in range(len(sums)):
        for k, (ox, oy) in enumerate(others):
            piece = sums[w].at[2 * ox + oy]
            _remote(piece, piece, send_sems.at[w, k], recv_sems.at[w, k], (x, y, c)).wait_send()


def _chip_exchange_sems(n):
    return [pltpu.SemaphoreType.DMA((n, 3)), pltpu.SemaphoreType.DMA((n, 3))]


def _sibling_allgather(bufs, also):
    n = len(bufs)
    k_in, k_out = len(also.operands), also.n_out

    def body(*refs):
        ex_ins, refs = refs[n:n + k_in], refs[n + k_in:]
        outs, refs = refs[:n], refs[n:]
        ex_outs, refs = refs[:k_out], refs[k_out:]
        send_sems, recv_sems, ex_sems = refs[0], refs[1], refs[2:]
        x, y, c, _ = _mesh_place()
        sibling = (x, y, 1 - c)
        also.start(ex_ins, ex_outs, ex_sems)
        sends = [_remote(outs[w].at[c], outs[w].at[c], send_sems.at[w], recv_sems.at[w], sibling) for w in range(n)]
        for cp in sends:
            cp.start()
        for w in range(n):
            landed = outs[w].at[1 - c]
            _remote(landed, landed, send_sems.at[w], recv_sems.at[w], sibling).wait_recv()
        for cp in sends:
            cp.wait_send()
        also.finish(ex_ins, ex_outs, ex_sems)

    res = pl.pallas_call(
        body, name="sibling_allgather",
        in_specs=[HBM_SPEC] * (n + k_in), out_specs=[HBM_SPEC] * (n + k_out),
        out_shape=[_hbm_like(b) for b in bufs] + also.out_shape,
        input_output_aliases={**{w: w for w in range(n)}, **{n + i: n + o for i, o in also.aliases.items()}},
        scratch_shapes=[pltpu.SemaphoreType.DMA((n,)), pltpu.SemaphoreType.DMA((n,))] + also.sems,
    )(*bufs, *[_in_hbm(o) for o in also.operands])
    return list(res[:n]), list(res[n:])


def _pair_sum(grad, other, place):
    _, _, h, cols = grad.shape
    tr = _row_tile(h)

    def body(place_ref, g_ref, o_ref, sums_ref, own_ref):
        s = (g_ref[0, 0] + o_ref[0]).astype(BF16)
        sums_ref[0] = s

        @pl.when(pl.program_id(1) == place_ref[0])
        def _():
            own_ref[0] = s

    return pl.pallas_call(
        body, name="pair_sum",
        grid_spec=pltpu.PrefetchScalarGridSpec(
            num_scalar_prefetch=1, grid=(h // tr, N_CHIP),
            in_specs=[pl.BlockSpec((1, 1, tr, cols), lambda r, j, place_ref: (j, place_ref[1], r, 0)),
                      pl.BlockSpec((1, tr, cols), lambda r, j, place_ref: (j, r, 0))],
            out_specs=[pl.BlockSpec((1, tr, cols), lambda r, j, place_ref: (j, r, 0)),
                       pl.BlockSpec((1, tr, cols), lambda r, j, place_ref: (place_ref[0], r, 0))]),
        out_shape=[pltpu.HBM((N_CHIP, h, cols), BF16)] * 2,
        compiler_params=_params(32, 2),
    )(place, _in_hbm(grad), _in_hbm(other))


def _chip_sum(parts, place):
    _, h, cols = parts.shape
    tr = _row_tile(h)

    def body(place_ref, p_ref, out_ref):
        out_ref[0] = ((p_ref[0].astype(F32) + p_ref[1].astype(F32)) + p_ref[2].astype(F32)) + p_ref[3].astype(F32)

    return pl.pallas_call(
        body, name="chip_sum",
        grid_spec=pltpu.PrefetchScalarGridSpec(
            num_scalar_prefetch=1, grid=(h // tr,),
            in_specs=[pl.BlockSpec((N_CHIP, tr, cols), lambda r, place_ref: (0, r, 0))],
            out_specs=pl.BlockSpec((1, tr, cols), lambda r, place_ref: (place_ref[1], r, 0))),
        out_shape=pltpu.HBM((2, h, cols), F32),
        compiler_params=_params(32),
    )(place, _in_hbm(parts))


def _adamw_math(w, g, m, v):
    m = ADAM_B1 * m + (1.0 - ADAM_B1) * g
    v = ADAM_B2 * v + (1.0 - ADAM_B2) * (g * g)
    m_hat = m / (1.0 - ADAM_B1 ** ADAM_STEP)
    v_hat = v / (1.0 - ADAM_B2 ** ADAM_STEP)
    delta = -ADAM_LR * (m_hat / (jnp.sqrt(v_hat) + ADAM_EPS) + ADAM_WD * w)
    return delta, m, v


def _adamw(w, g, m, v, exchange=None):
    rows, cols = w.shape
    tr = _row_tile(rows)

    def body(w_ref, g_ref, m_ref, v_ref, d_ref, nm_ref, nv_ref, g_out_ref):
        g = g_ref[...]
        d_ref[...], nm_ref[...], nv_ref[...] = _adamw_math(w_ref[...], g, m_ref[...], v_ref[...])
        g_out_ref[...] = g

    spec = pl.BlockSpec((tr, cols), lambda r: (r, 0))
    return _call(
        body, (w, g, m, v), grid=(rows // tr,), name="adamw",
        in_specs=[spec] * 4, out_specs=[spec] * 4,
        out_shape=[jax.ShapeDtypeStruct((rows, cols), F32)] * 4,
        compiler_params=_params(48), exchange=exchange)


SMALL_NAMES = ("norm1_gain", "gmlp_v_gain", "w_spatial", "b_spatial", "attn_sinks", "rel_bias_table", "norm2_gain",
               "final_gain")
PACK_TILE = 8 * 128


def _pack_small(arrays):
    parts = []
    for a in arrays:
        flat = a.reshape(-1)
        rows = -(-flat.shape[0] // PACK_TILE) * 8
        parts.append(jnp.pad(flat, (0, rows * 128 - flat.shape[0])).reshape(rows, 128))
    return jnp.concatenate(parts, axis=0)


def _unpack_small(packed, like):
    out, row = [], 0
    for a in like:
        size = math.prod(a.shape)
        rows = -(-size // PACK_TILE) * 8
        out.append(packed[row:row + rows].reshape(-1)[:size].reshape(a.shape))
        row += rows
    return out


def _small_update(gathered, w, m, v):
    rows = gathered.shape[1]

    def body(g_ref, w_ref, m_ref, v_ref, tot_ref, d_ref, nm_ref, nv_ref):
        total = g_ref[0].astype(F32)
        for dev in range(1, 8):
            total = total + g_ref[dev].astype(F32)
        tot_ref[...] = total
        d_ref[...], nm_ref[...], nv_ref[...] = _adamw_math(w_ref[...], total, m_ref[...], v_ref[...])

    return pl.pallas_call(
        body, name="small_update",
        in_specs=[VMEM_SPEC] * 4, out_specs=[VMEM_SPEC] * 4,
        out_shape=[jax.ShapeDtypeStruct((rows, 128), F32)] * 4,
        compiler_params=pltpu.CompilerParams(vmem_limit_bytes=24 * MIB),
    )(gathered, w, m, v)


def _halves(a):
    return a.reshape(a.shape[:-2] + (2, a.shape[-2] // 2, a.shape[-1]))


def _whole(a):
    return a.reshape(a.shape[:-3] + (2 * a.shape[-2], a.shape[-1]))


def kernel(x, p, norm1_gain, w_in, gmlp_v_gain, w_spatial, b_spatial, attn_sinks, rel_bias_table, w_out, norm2_gain, w_ff1, w_ff2, w_ple_proj, w_ple_gate, final_gain, loss_target, m_norm1_gain, m_w_in, m_gmlp_v_gain, m_w_spatial, m_b_spatial, m_attn_sinks, m_rel_bias_table, m_w_out, m_norm2_gain, m_w_ff1, m_w_ff2, m_w_ple_proj, m_w_ple_gate, m_final_gain, v_norm1_gain, v_w_in, v_gmlp_v_gain, v_w_spatial, v_b_spatial, v_attn_sinks, v_rel_bias_table, v_w_out, v_norm2_gain, v_w_ff1, v_w_ff2, v_w_ple_proj, v_w_ple_gate, v_final_gain):
    given = dict(locals())
    small = {n: given[n] for n in SMALL_NAMES}
    chip = 2 * lax.axis_index("x") + lax.axis_index("y")
    place = jnp.stack([chip, lax.axis_index("c")]).astype(jnp.int32)

    big_names = ("w_in", "w_out", "w_ff1", "w_ff2", "w_ple_proj", "w_ple_gate")
    shards = {n: given[n][0] for n in big_names}
    travel = dict(shards, w_in=jnp.transpose(shards["w_in"]))
    bufs = {n: _cast_shard(travel[n], place[:1]) for n in big_names}
    dx, landed, exchange_in, small_grads, sq = _step(x[0], p[0, 0], loss_target[0], small, bufs, place)

    out_grad, out_delta, out_m, out_v = {}, {}, {}, {}

    def update(n, g, exchange=None):
        to = jnp.transpose if n == "w_in" else (lambda a: a)
        (delta, new_m, new_v, g_out), got = _adamw(to(shards[n]), g, to(given["m_" + n][0]), to(given["v_" + n][0]),
                                                   exchange)
        out_grad[n], out_delta[n], out_m[n], out_v[n] = [to(a)[None] for a in (g_out, delta, new_m, new_v)]
        return got

    spare = jnp.zeros((8, 128), F32)
    small_packed = _pack_small([small_grads[n] for n in SMALL_NAMES] + [spare]).astype(BF16)
    early = [n for n in big_names if n != "w_in"]
    reduced, (small_gathered, sq_gathered, landed_in) = _sibling_allgather(
        [_chip_sum(landed[n], place) for n in early], _Both(_Both(_GatherAll(small_packed), _GatherAll(sq)), exchange_in))
    for n, r in zip(early, reduced):
        update(n, _whole(r))
    (reduced_in,), _ = _sibling_allgather([_chip_sum(landed_in, place)], _Nothing())
    update("w_in", _whole(reduced_in))

    like = [given[n] for n in SMALL_NAMES] + [spare]
    packed = _small_update(small_gathered, *[_pack_small([given[pre + n] for n in SMALL_NAMES] + [spare])
                                             for pre in ("", "m_", "v_")])
    for res, out in zip(packed, (out_grad, out_delta, out_m, out_v)):
        out.update(zip(SMALL_NAMES, _unpack_small(res, like)))
    loss = 0.5 * jnp.sum(sq_gathered[:, 0, 0]) / D

    order = ("norm1_gain", "w_in", "gmlp_v_gain", "w_spatial", "b_spatial", "attn_sinks", "rel_bias_table", "w_out",
             "norm2_gain", "w_ff1", "w_ff2", "w_ple_proj", "w_ple_gate", "final_gain")
    return (loss, dx[None], *[out_grad[n] for n in order], *[out_delta[n] for n in order],
            *[out_m[n] for n in order], *[out_v[n] for n in order])
```

```python
import functools
import math

import jax
import jax.numpy as jnp
from jax import lax
from jax.experimental import pallas as pl
from jax.experimental.pallas import tpu as pltpu

S = 2048
D = 1024
D_IN = 1792
D_FF = 4096
PLE = 256
N_CHIP = 4
N_GROUP = 4
CHUNK = 128
N_HEAD = 8
N_BLOCK = S // CHUNK
N_BUCKET = 32
EPS = 1e-6
NEG_INF = -1e30
QK_SCALE = 0.125
GELU_C = math.sqrt(2.0 / math.pi)

ADAM_LR = 0.001
ADAM_B1 = 0.9
ADAM_B2 = 0.999
ADAM_EPS = 1e-08
ADAM_WD = 0.01
ADAM_STEP = 10

F32 = jnp.float32
BF16 = jnp.bfloat16
MIB = 1024 * 1024
MESH = pl.DeviceIdType.MESH

NT = (((1,), (1,)), ((), ()))
TN = (((0,), (0,)), ((), ()))


def _dot(a, b):
    return jnp.dot(a, b, preferred_element_type=F32)


def _dot_nt(a, b):
    return lax.dot_general(a, b, NT, preferred_element_type=F32)


def _dot_tn(a, b):
    return lax.dot_general(a, b, TN, preferred_element_type=F32)


def _params(vmem_mib, n_axes=1):
    return pltpu.CompilerParams(dimension_semantics=("arbitrary",) * n_axes, vmem_limit_bytes=vmem_mib * MIB)


def _rms_scale(v):
    return lax.rsqrt(jnp.mean(v * v, axis=-1, keepdims=True) + EPS)


def _rms_bwd(dy_gain, xhat, r):
    return r * (dy_gain - xhat * jnp.mean(dy_gain * xhat, axis=-1, keepdims=True))


class _Gather:
    def __init__(self, bufs):
        self.operands = list(bufs)
        self.n_out = len(self.operands)
        self.out_shape = [_hbm_like(b) for b in bufs]
        self.aliases = {w: w for w in range(self.n_out)}
        self.sems = _gather_sems(self.n_out)

    def start(self, ins, outs, sems):
        _gather_start(outs, *sems)

    def finish(self, ins, outs, sems):
        _gather_finish(outs, *sems)


class _RelayGather(_Gather):
    TOP, BOTTOM = 6, 7
    DIAGONAL_PASSED = 5
    MIDDLE_AT, LATE_AT = (5, 8), (7, 8)

    def __init__(self, bufs):
        super().__init__(bufs)
        self.sems = [pltpu.SemaphoreType.DMA((self.n_out, 8)), pltpu.SemaphoreType.DMA((self.n_out, 8))]

    def _copies(self, bufs, send_sems, recv_sems):
        x, y, c, others = _mesh_place()
        me = 2 * x + y
        idx = [2 * ox + oy for ox, oy in others]
        sibling = (x, y, 1 - c)
        direct, passed, relayed = [], [], []
        for w, buf in enumerate(bufs):
            rows = buf.shape[2] // 2
            upper, lower = pl.ds(0, rows), pl.ds(rows, rows)
            for k in (0, 1):
                mine = buf.at[me, c]
                direct.append((_remote(mine, mine, send_sems.at[w, k], recv_sems.at[w, k], (*others[k], c)),
                               buf.at[idx[k], c], w, k))
            for k in (0, 1, 2):
                here = buf.at[idx[k], c]
                passed.append((_remote(here, here, send_sems.at[w, 3 + k], recv_sems.at[w, 3 + k], sibling),
                               buf.at[idx[k], 1 - c], w, 3 + k))
            from_x, from_y = buf.at[idx[0], c, upper], buf.at[idx[1], c, lower]
            relayed.append((_remote(from_x, from_x, send_sems.at[w, self.TOP], recv_sems.at[w, self.TOP],
                                    (*others[1], c)), buf.at[idx[2], c, upper], w, self.TOP))
            relayed.append((_remote(from_y, from_y, send_sems.at[w, self.BOTTOM], recv_sems.at[w, self.BOTTOM],
                                    (*others[0], c)), buf.at[idx[2], c, lower], w, self.BOTTOM))
        return direct, passed, relayed

    @staticmethod
    def _landed(piece, send_sems, recv_sems, w, col):
        x, y, c, _ = _mesh_place()
        _remote(piece, piece, send_sems.at[w, col], recv_sems.at[w, col], (x, y, c)).wait_recv()

    def start(self, ins, outs, sems):
        for cp, _, _, _ in self._copies(outs, *sems)[0]:
            cp.start()

    def middle(self, ins, outs, sems):
        direct, passed, relayed = self._copies(outs, *sems)
        for _, piece, w, col in direct:
            self._landed(piece, *sems, w, col)
        for cp, _, _, col in passed:
            if col != self.DIAGONAL_PASSED:
                cp.start()
        for cp, _, _, _ in relayed:
            cp.start()

    def late(self, ins, outs, sems):
        direct, passed, relayed = self._copies(outs, *sems)
        for _, piece, w, col in relayed:
            self._landed(piece, *sems, w, col)
        for cp, _, _, col in passed:
            if col == self.DIAGONAL_PASSED:
                cp.start()

    def finish(self, ins, outs, sems):
        direct, passed, relayed = self._copies(outs, *sems)
        for _, piece, w, col in passed:
            self._landed(piece, *sems, w, col)
        for cp, _, _, _ in direct + passed + relayed:
            cp.wait_send()


class _ChipExchange:
    def __init__(self, sums, landing):
        self.n_out = len(landing)
        self.operands = list(sums) + list(landing)
        self.out_shape = [_hbm_like(b) for b in landing]
        self.aliases = {self.n_out + w: w for w in range(self.n_out)}
        self.sems = _chip_exchange_sems(self.n_out)

    def start(self, ins, outs, sems):
        _chip_exchange_start(ins[:self.n_out], outs, *sems)

    def finish(self, ins, outs, sems):
        _chip_exchange_finish(ins[:self.n_out], outs, *sems)


class _GatherAll:
    def __init__(self, packed):
        self.operands = [packed]
        self.n_out = 1
        self.out_shape = [_hbm_like(packed, (8,) + packed.shape)]
        self.aliases = {}
        self.sems = [pltpu.SemaphoreType.DMA((8,)), pltpu.SemaphoreType.DMA((8,))]

    def _copies(self, ins, outs, sems):
        x, y, c, _ = _mesh_place()
        me = 4 * x + 2 * y + c
        send_sems, recv_sems = sems
        copies = []
        for k in range(1, 8):
            peer = (1 - x if k // 4 else x, 1 - y if (k // 2) % 2 else y, 1 - c if k % 2 else c)
            src = 4 * peer[0] + 2 * peer[1] + peer[2]
            copies.append((_remote(ins[0], outs[0].at[me], send_sems.at[k], recv_sems.at[k], peer), outs[0].at[src]))
        own = pltpu.make_async_copy(ins[0], outs[0].at[me], send_sems.at[0])
        return own, copies

    def start(self, ins, outs, sems):
        own, copies = self._copies(ins, outs, sems)
        own.start()
        for cp, _ in copies:
            cp.start()

    def finish(self, ins, outs, sems):
        own, copies = self._copies(ins, outs, sems)
        x, y, c, _ = _mesh_place()
        for k, (cp, landed) in enumerate(copies):
            _remote(landed, landed, sems[0].at[k + 1], sems[1].at[k + 1], (x, y, c)).wait_recv()
        for cp, _ in copies:
            cp.wait_send()
        own.wait()


class _Nothing:
    operands, n_out, out_shape, aliases, sems = [], 0, [], {}, []

    def start(self, ins, outs, sems):
        pass

    def finish(self, ins, outs, sems):
        pass


class _Both:
    def __init__(self, a, b):
        self.a, self.b = a, b
        self.operands = a.operands + b.operands
        self.n_out = a.n_out + b.n_out
        self.out_shape = a.out_shape + b.out_shape
        self.aliases = dict(a.aliases)
        self.aliases.update({len(a.operands) + i: a.n_out + o for i, o in b.aliases.items()})
        self.sems = a.sems + b.sems

    def _split(self, ins, outs, sems):
        ka, na, sa = len(self.a.operands), self.a.n_out, len(self.a.sems)
        return (ins[:ka], outs[:na], sems[:sa]), (ins[ka:], outs[na:], sems[sa:])

    def start(self, ins, outs, sems):
        for ex, args in zip((self.a, self.b), self._split(ins, outs, sems)):
            ex.start(*args)

    def finish(self, ins, outs, sems):
        for ex, args in zip((self.a, self.b), self._split(ins, outs, sems)):
            ex.finish(*args)


class _SiblingExchange:
    def __init__(self, grads):
        self.operands = list(grads)
        self.n_out = len(self.operands)
        self.out_shape = [_hbm_like(g, (N_CHIP,) + g.shape[2:]) for g in grads]
        self.aliases = {}
        self.sems = _sibling_exchange_sems(self.n_out)

    def start(self, ins, outs, sems):
        _sibling_exchange_start(ins, outs, *sems)

    def finish(self, ins, outs, sems):
        _sibling_exchange_finish(ins, outs, *sems)


def _call(body, operands, *, grid, in_specs, out_specs, out_shape, name, compiler_params, scratch_shapes=(),
          exchange=None):
    operands = [o if getattr(spec, "memory_space", None) == pltpu.SMEM else _in_hbm(o)
                for o, spec in zip(operands, in_specs)]
    out_shape = [pltpu.HBM(s.shape, s.dtype) for s in out_shape]
    if exchange is None:
        res = pl.pallas_call(body, grid=grid, in_specs=in_specs, out_specs=out_specs, out_shape=out_shape, name=name,
                             scratch_shapes=list(scratch_shapes), compiler_params=compiler_params)(*operands)
        return list(res), []
    n_in, n_out, n_scr = len(in_specs), len(out_specs), len(scratch_shapes)
    k_in, k_out = len(exchange.operands), exchange.n_out

    def fused(*refs):
        ins, refs = refs[:n_in], refs[n_in:]
        ex_ins, refs = refs[:k_in], refs[k_in:]
        outs, refs = refs[:n_out], refs[n_out:]
        ex_outs, refs = refs[:k_out], refs[k_out:]
        scratch, sems = refs[:n_scr], refs[n_scr:]
        ids = [pl.program_id(a) for a in range(len(grid))]
        first = functools.reduce(jnp.logical_and, [i == 0 for i in ids])
        last = functools.reduce(jnp.logical_and, [i == g - 1 for i, g in zip(ids, grid)])

        @pl.when(first)
        def _():
            exchange.start(ex_ins, ex_outs, sems)

        def at_step(numerator, denominator):
            at = (numerator * math.prod(grid)) // denominator
            place = [(at // math.prod(grid[a + 1:])) % grid[a] for a in range(len(grid))]
            return functools.reduce(jnp.logical_and, [i == p for i, p in zip(ids, place)])

        if hasattr(exchange, "middle"):
            @pl.when(at_step(*exchange.MIDDLE_AT))
            def _():
                exchange.middle(ex_ins, ex_outs, sems)

            @pl.when(at_step(*exchange.LATE_AT))
            def _():
                exchange.late(ex_ins, ex_outs, sems)

        body(*ins, *outs, *scratch)

        @pl.when(last)
        def _():
            exchange.finish(ex_ins, ex_outs, sems)

    res = pl.pallas_call(
        fused, grid=grid, name=name,
        in_specs=list(in_specs) + [HBM_SPEC] * k_in, out_specs=list(out_specs) + [HBM_SPEC] * k_out,
        out_shape=list(out_shape) + exchange.out_shape,
        input_output_aliases={n_in + i: n_out + o for i, o in exchange.aliases.items()},
        scratch_shapes=list(scratch_shapes) + exchange.sems, compiler_params=compiler_params,
    )(*operands, *[_in_hbm(o) for o in exchange.operands])
    return list(res[:n_out]), list(res[n_out:])


def _in_hbm(a):
    return pltpu.with_memory_space_constraint(a, pltpu.HBM)


def _row_tile(h):
    return max(t for t in range(16, 513, 16) if h % t == 0)


def _cast_shard(a, chip):
    rows, cols = a.shape
    h = rows // 2
    tr = _row_tile(h)

    def body(chip_ref, a_ref, o_ref):
        o_ref[0, 0] = a_ref[0].astype(BF16)

    return pl.pallas_call(
        body, name="cast_shard",
        grid_spec=pltpu.PrefetchScalarGridSpec(
            num_scalar_prefetch=1, grid=(2, h // tr),
            in_specs=[pl.BlockSpec((1, tr, cols), lambda s, r, chip_ref: (s, r, 0))],
            out_specs=pl.BlockSpec((1, 1, tr, cols), lambda s, r, chip_ref: (chip_ref[0], s, r, 0))),
        out_shape=pltpu.HBM((N_CHIP, 2, h, cols), BF16),
        compiler_params=_params(16, 2),
    )(chip, _in_hbm(a.reshape(2, h, cols)))


def _cast_shards_beside_gather(arrays, chip, gathered):
    n, k = len(arrays), len(gathered)
    shapes = [(a.shape[0] // 2, a.shape[1]) for a in arrays]

    def body(chip_ref, *refs):
        ins, refs = refs[:n], refs[n + k:]
        outs, refs = refs[:n], refs[n:]
        bufs, sems = refs[:k], refs[k:]
        half = pl.program_id(0)

        @pl.when(half == 0)
        def _():
            _gather_start(bufs, *sems)

        for a_ref, o_ref in zip(ins, outs):
            o_ref[0, 0] = a_ref[0].astype(BF16)

        @pl.when(half == 1)
        def _():
            _gather_finish(bufs, *sems)

    res = pl.pallas_call(
        body, name="cast_shards",
        grid_spec=pltpu.PrefetchScalarGridSpec(
            num_scalar_prefetch=1, grid=(2,),
            in_specs=[pl.BlockSpec((1, h, c), lambda s, chip_ref: (s, 0, 0)) for h, c in shapes] + [HBM_SPEC] * k,
            out_specs=[pl.BlockSpec((1, 1, h, c), lambda s, chip_ref: (chip_ref[0], s, 0, 0)) for h, c in shapes]
            + [HBM_SPEC] * k,
            scratch_shapes=_gather_sems(k)),
        out_shape=[pltpu.HBM((N_CHIP, 2, h, c), BF16) for h, c in shapes] + [_hbm_like(b) for b in gathered],
        input_output_aliases={1 + n + i: n + i for i in range(k)},
        compiler_params=_params(32),
    )(chip, *[_in_hbm(a.reshape(2, h, c)) for a, (h, c) in zip(arrays, shapes)], *gathered)
    return list(res[:n]), list(res[n:])


def _in_proj(x, gain1, w_in_t, exchange=None):
    tm = 256

    def body(x_ref, g_ref, w_ref, z_ref, hn_ref):
        xv = x_ref[...]
        hn = (xv * _rms_scale(xv) * g_ref[...]).astype(BF16)
        hn_ref[...] = hn
        z_ref[...] = _dot_nt(hn, w_ref[...])

    return _call(
        body, (x, gain1, w_in_t), grid=(S // tm,), name="in_proj",
        in_specs=[pl.BlockSpec((tm, D), lambda i: (i, 0)), pl.BlockSpec((1, D), lambda i: (0, 0)),
                  pl.BlockSpec((D_IN, D), lambda i: (0, 0))],
        out_specs=[pl.BlockSpec((tm, D_IN), lambda i: (i, 0)), pl.BlockSpec((tm, D), lambda i: (i, 0))],
        out_shape=[jax.ShapeDtypeStruct((S, D_IN), F32), jax.ShapeDtypeStruct((S, D), BF16)],
        compiler_params=_params(40), exchange=exchange)


def _gelu_parts(v):
    t = jnp.tanh(GELU_C * (v + 0.044715 * (v * v * v)))
    cdf = 0.5 * (1.0 + t)
    return cdf, t


def _band_mask(n):
    a = lax.broadcasted_iota(jnp.int32, (CHUNK, 2 * CHUNK), 0)
    j = lax.broadcasted_iota(jnp.int32, (CHUNK, 2 * CHUNK), 1)
    dist = CHUNK + a - j
    valid = (dist >= 0) & (dist < CHUNK)
    return valid & ((n > 0) | (j >= CHUNK))


def _fill_bias(bucket_ref, table_ref, bias_ref):
    bucket = bucket_ref[...]
    for h in range(N_HEAD):
        acc = jnp.zeros((CHUNK, 2 * CHUNK), F32)
        for b in range(N_BUCKET):
            acc = jnp.where(bucket == b, table_ref[b, h], acc)
        bias_ref[h] = acc


def _fill_tril(ws_ref, wt_ref, wtt_ref=None):
    r = lax.broadcasted_iota(jnp.int32, (CHUNK, CHUNK), 0)
    c = lax.broadcasted_iota(jnp.int32, (CHUNK, CHUNK), 1)
    for g in range(N_GROUP):
        w = jnp.where(c <= r, ws_ref[g], 0.0)
        wt_ref[g] = w.astype(BF16)
        if wtt_ref is not None:
            wtt_ref[g] = w.T.astype(BF16)


def _kv_layouts(kv_prev, kv_cur):
    both = jnp.concatenate([kv_prev, kv_cur], axis=0)
    k = both[:, :128]
    v = both[:, 128:]
    return (k.astype(BF16), pltpu.roll(k, 64, axis=1).astype(BF16),
            v.astype(BF16), pltpu.roll(v, 64, axis=1).astype(BF16))


def _head_place(h):
    pair, pos, kvh = h // 2, h % 2, h // 4
    return pair, pos, kvh == pos


def _softmax_sink(qm, k_use, bias_h, sink, valid):
    s = _dot_nt(qm, k_use) * QK_SCALE + bias_h
    s = jnp.where(valid, s, NEG_INF)
    m = jnp.maximum(jnp.max(s, axis=-1, keepdims=True), sink)
    e = jnp.exp(s - m)
    es = jnp.exp(sink - m)
    inv = 1.0 / (jnp.sum(e, axis=-1, keepdims=True) + es)
    return e * inv, es * inv


def _mixer_fwd(z, v_gain, w_spatial, b_spatial_t, sinks, rel_table, bucket, exchange=None):
    def body(z_ref, kvp_ref, gain_ref, ws_ref, bt_ref, sink_ref, table_ref, bucket_ref, out_ref, bias_ref, wt_ref):
        n = pl.program_id(0)

        @pl.when(n == 0)
        def _():
            _fill_bias(bucket_ref, table_ref, bias_ref)
            _fill_tril(ws_ref, wt_ref)

        zuv = z_ref[:, :1024]
        cdf, _ = _gelu_parts(zuv)
        guv = zuv * cdf
        for g in range(N_GROUP):
            vg = guv[:, 512 + 128 * g:512 + 128 * (g + 1)]
            vn = vg * _rms_scale(vg) * gain_ref[:, 128 * g:128 * (g + 1)]
            sv = _dot(wt_ref[g], vn.astype(BF16)) + bt_ref[:, g:g + 1]
            out_ref[:, 128 * g:128 * (g + 1)] = (guv[:, 128 * g:128 * (g + 1)] * sv).astype(BF16)

        k_same, k_swap, v_same, v_swap = _kv_layouts(kvp_ref[...], z_ref[:, 1536:1792])
        valid = _band_mask(n)
        lane_half = lax.broadcasted_iota(jnp.int32, (1, 128), 1) // 64
        for pair in range(N_HEAD // 2):
            qq = z_ref[:, 1024 + 128 * pair:1024 + 128 * (pair + 1)]
            acc = jnp.zeros((CHUNK, 128), F32)
            for pos in range(2):
                h = 2 * pair + pos
                _, _, same = _head_place(h)
                qm = jnp.where(lane_half == pos, qq, 0.0).astype(BF16)
                p, _ = _softmax_sink(qm, k_same if same else k_swap, bias_ref[h], sink_ref[h], valid)
                vm = jnp.where(lane_half == pos, v_same if same else v_swap, jnp.zeros((), BF16))
                acc = acc + _dot(p.astype(BF16), vm)
            out_ref[:, 512 + 128 * pair:512 + 128 * (pair + 1)] = acc.astype(BF16)

    return _call(
        body, (z, z, v_gain, w_spatial, b_spatial_t, sinks, rel_table, bucket), grid=(N_BLOCK,), name="mixer_fwd",
        in_specs=[pl.BlockSpec((CHUNK, D_IN), lambda n: (n, 0)),
                  pl.BlockSpec((CHUNK, 256), lambda n: (jnp.maximum(n - 1, 0), 6)),
                  pl.BlockSpec((1, 512), lambda n: (0, 0)),
                  pl.BlockSpec((N_GROUP, CHUNK, CHUNK), lambda n: (0, 0, 0)),
                  pl.BlockSpec((CHUNK, N_GROUP), lambda n: (0, 0)),
                  pl.BlockSpec(memory_space=pltpu.SMEM),
                  pl.BlockSpec(memory_space=pltpu.SMEM),
                  pl.BlockSpec((CHUNK, 2 * CHUNK), lambda n: (0, 0))],
        out_specs=[pl.BlockSpec((CHUNK, D), lambda n: (n, 0))],
        out_shape=[jax.ShapeDtypeStruct((S, D), BF16)],
        scratch_shapes=[pltpu.VMEM((N_HEAD, CHUNK, 2 * CHUNK), F32), pltpu.VMEM((N_GROUP, CHUNK, CHUNK), BF16)],
        compiler_params=_params(32), exchange=exchange)


def _out_proj(x, mix, w_out, gain2, exchange=None):
    tm = 256

    def body(x_ref, mix_ref, w_ref, g_ref, h1_ref, hn_ref, hnt_ref):
        h1 = x_ref[...] + _dot(mix_ref[...], w_ref[...])
        h1_ref[...] = h1
        hn = h1 * _rms_scale(h1) * g_ref[...]
        hn_ref[...] = hn.astype(BF16)
        hnt_ref[...] = hn.T.astype(BF16)

    return _call(
        body, (x, mix, w_out, gain2), grid=(S // tm,), name="out_proj",
        in_specs=[pl.BlockSpec((tm, D), lambda i: (i, 0)), pl.BlockSpec((tm, D), lambda i: (i, 0)),
                  pl.BlockSpec((D, D), lambda i: (0, 0)), pl.BlockSpec((1, D), lambda i: (0, 0))],
        out_specs=[pl.BlockSpec((tm, D), lambda i: (i, 0)), pl.BlockSpec((tm, D), lambda i: (i, 0)),
                   pl.BlockSpec((D, tm), lambda i: (0, i))],
        out_shape=[jax.ShapeDtypeStruct((S, D), F32), jax.ShapeDtypeStruct((S, D), BF16),
                   jax.ShapeDtypeStruct((D, S), BF16)],
        compiler_params=_params(32), exchange=exchange)


def _ffn_up(hn2, w_ff1, exchange=None):
    tm = 512
    nj = D_FF // 1024

    def body(hn_ref, w1_ref, r_ref, a_ref, at_ref):
        r = jnp.maximum(_dot(hn_ref[...], w1_ref[0]), 0.0)
        r_ref[...] = r.astype(BF16)
        a = r * r
        a_ref[...] = a.astype(BF16)
        at_ref[...] = a.T.astype(BF16)

    return _call(
        body, (hn2, w_ff1), grid=(nj, S // tm), name="ffn_up",
        in_specs=[pl.BlockSpec((tm, D), lambda j, i: (i, 0)), pl.BlockSpec((1, D, 1024), lambda j, i: (j, 0, 0))],
        out_specs=[pl.BlockSpec((tm, 1024), lambda j, i: (i, j)), pl.BlockSpec((tm, 1024), lambda j, i: (i, j)),
                   pl.BlockSpec((1024, tm), lambda j, i: (j, i))],
        out_shape=[jax.ShapeDtypeStruct((S, D_FF), BF16), jax.ShapeDtypeStruct((S, D_FF), BF16),
                   jax.ShapeDtypeStruct((D_FF, S), BF16)],
        compiler_params=_params(40, 2), exchange=exchange)


def _ffn_down(h1, a, w_ff2, exchange=None):
    tm = 1024
    nj = D_FF // 1024

    def body(h1_ref, a_ref, w2_ref, h2_ref, acc_ref):
        j = pl.program_id(1)
        part = _dot(a_ref[...], w2_ref[0])

        @pl.when(j == 0)
        def _():
            acc_ref[...] = part

        @pl.when(j > 0)
        def _():
            acc_ref[...] += part

        @pl.when(j == nj - 1)
        def _():
            h2_ref[...] = h1_ref[...] + acc_ref[...]

    return _call(
        body, (h1, a, w_ff2), grid=(S // tm, nj), name="ffn_down",
        in_specs=[pl.BlockSpec((tm, D), lambda i, j: (i, 0)), pl.BlockSpec((tm, 1024), lambda i, j: (i, j)),
                  pl.BlockSpec((1, 1024, D), lambda i, j: (j, 0, 0))],
        out_specs=[pl.BlockSpec((tm, D), lambda i, j: (i, 0))],
        out_shape=[jax.ShapeDtypeStruct((S, D), F32)],
        scratch_shapes=[pltpu.VMEM((tm, D), F32)],
        compiler_params=_params(48, 2), exchange=exchange)


def _tail(h2, p, target, w_gate, w_proj, final_gain):
    tm = 256
    steps = S // tm

    def body(h2_ref, p_ref, t_ref, wg_ref, wp_ref, gf_ref, dh2_ref, dwg_ref, dwp_ref, dgf_ref, loss_ref, dh2b_ref,
             dwp_acc):
        i = pl.program_id(0)
        h2 = h2_ref[...]
        h2b = h2.astype(BF16)
        pb = p_ref[...].astype(BF16)
        gate = jax.nn.sigmoid(_dot(h2b, wg_ref[...]))
        pp = jnp.concatenate([_dot(pb, wp_ref[j]) for j in range(N_CHIP)], axis=1)
        h3 = h2 + gate * pp
        r3 = _rms_scale(h3)
        xhat = h3 * r3
        gf = gf_ref[...]
        err = xhat * gf - t_ref[...]
        dy = err * (1.0 / D)
        dh3 = _rms_bwd(dy * gf, xhat, r3)
        dgp = (dh3 * pp * gate * (1.0 - gate)).astype(BF16)
        dpp = (dh3 * gate).astype(BF16)
        dh2 = dh3 + _dot_nt(dgp, wg_ref[...])
        dh2_ref[...] = dh2
        dh2b_ref[...] = dh2.astype(BF16)
        dwg = _dot_tn(h2b, dgp)
        dwp = _dot_tn(pb, dpp)
        dgf = jnp.sum(dy * xhat, axis=0, keepdims=True)
        sq = jnp.sum(jnp.sum(err * err, axis=1, keepdims=True), axis=0, keepdims=True)

        @pl.when(i == 0)
        def _():
            dwg_ref[...] = dwg
            dwp_acc[...] = dwp
            dgf_ref[...] = dgf
            loss_ref[...] = jnp.broadcast_to(sq, (8, 128))

        @pl.when(i > 0)
        def _():
            dwg_ref[...] += dwg
            dwp_acc[...] += dwp
            dgf_ref[...] += dgf
            loss_ref[...] += jnp.broadcast_to(sq, (8, 128))

        @pl.when(i == steps - 1)
        def _():
            for j in range(N_CHIP):
                dwp_ref[j] = dwp_acc[:, 256 * j:256 * (j + 1)]

    return _call(
        body, (h2, p, target, w_gate, w_proj, final_gain), grid=(steps,), name="tail",
        in_specs=[pl.BlockSpec((tm, D), lambda i: (i, 0)), pl.BlockSpec((tm, PLE), lambda i: (i, 0)),
                  pl.BlockSpec((tm, D), lambda i: (i, 0)), pl.BlockSpec((D, D), lambda i: (0, 0)),
                  pl.BlockSpec((N_CHIP, PLE, 256), lambda i: (0, 0, 0)), pl.BlockSpec((1, D), lambda i: (0, 0))],
        out_specs=[pl.BlockSpec((tm, D), lambda i: (i, 0)), pl.BlockSpec((D, D), lambda i: (0, 0)),
                   pl.BlockSpec((N_CHIP, PLE, 256), lambda i: (0, 0, 0)), pl.BlockSpec((1, D), lambda i: (0, 0)),
                   pl.BlockSpec((8, 128), lambda i: (0, 0)), pl.BlockSpec((tm, D), lambda i: (i, 0))],
        out_shape=[jax.ShapeDtypeStruct((S, D), F32), jax.ShapeDtypeStruct((D, D), F32),
                   jax.ShapeDtypeStruct((N_CHIP, PLE, 256), F32), jax.ShapeDtypeStruct((1, D), F32),
                   jax.ShapeDtypeStruct((8, 128), F32), jax.ShapeDtypeStruct((S, D), BF16)],
        scratch_shapes=[pltpu.VMEM((PLE, D), F32)],
        compiler_params=_params(48))[0]


def _ffn_bwd_down(dh2b, r, a_t, w_ff2, exchange=None):
    tm = 1024
    nj = D_FF // 1024

    def body(dh2_ref, r_ref, at_ref, w2_ref, df_ref, dw2_ref):
        i = pl.program_id(1)
        dh2b = dh2_ref[...]
        da = _dot_nt(dh2b, w2_ref[0])
        df_ref[...] = (da * (2.0 * r_ref[...].astype(F32))).astype(BF16)
        dw2 = _dot(at_ref[...], dh2b)

        @pl.when(i == 0)
        def _():
            dw2_ref[0] = dw2

        @pl.when(i > 0)
        def _():
            dw2_ref[0] += dw2

    return _call(
        body, (dh2b, r, a_t, w_ff2), grid=(nj, S // tm), name="ffn_bwd_down",
        in_specs=[pl.BlockSpec((tm, D), lambda j, i: (i, 0)), pl.BlockSpec((tm, 1024), lambda j, i: (i, j)),
                  pl.BlockSpec((1024, tm), lambda j, i: (j, i)), pl.BlockSpec((1, 1024, D), lambda j, i: (j, 0, 0))],
        out_specs=[pl.BlockSpec((tm, 1024), lambda j, i: (i, j)), pl.BlockSpec((1, 1024, D), lambda j, i: (j, 0, 0))],
        out_shape=[jax.ShapeDtypeStruct((S, D_FF), BF16), jax.ShapeDtypeStruct((nj, 1024, D), F32)],
        compiler_params=_params(48, 2), exchange=exchange)


def _ffn_bwd_up(df, hn2_t, exchange=None):
    tm = 1024
    nj = D_FF // 1024

    def body(df_ref, hnt_ref, dw1_ref):
        i = pl.program_id(1)
        dw1 = _dot(hnt_ref[...], df_ref[...])

        @pl.when(i == 0)
        def _():
            dw1_ref[0] = dw1

        @pl.when(i > 0)
        def _():
            dw1_ref[0] += dw1

    return _call(
        body, (df, hn2_t), grid=(nj, S // tm), name="ffn_bwd_up",
        in_specs=[pl.BlockSpec((tm, 1024), lambda j, i: (i, j)), pl.BlockSpec((D, tm), lambda j, i: (0, i))],
        out_specs=[pl.BlockSpec((1, D, 1024), lambda j, i: (j, 0, 0))],
        out_shape=[jax.ShapeDtypeStruct((nj, D, 1024), F32)],
        compiler_params=_params(40, 2), exchange=exchange)


def _ffn_bwd_input(df, w_ff1, dh2, h1, gain2, mix, w_out, exchange=None):
    tm = 512
    nj = D_FF // 1024
    steps = S // tm

    def body(df_ref, w1_ref, dh2_ref, h1_ref, g_ref, mix_ref, wo_ref, dh1_ref, dmix_ref, dwo_ref, dg_ref, acc_ref):
        i = pl.program_id(0)
        j = pl.program_id(1)
        part = _dot_nt(df_ref[...], w1_ref[0])

        @pl.when(j == 0)
        def _():
            acc_ref[...] = part

        @pl.when(j > 0)
        def _():
            acc_ref[...] += part

        @pl.when(j == nj - 1)
        def _():
            dhn = acc_ref[...]
            h1 = h1_ref[...]
            r2 = _rms_scale(h1)
            xhat = h1 * r2
            dh1 = dh2_ref[...] + _rms_bwd(dhn * g_ref[...], xhat, r2)
            dh1_ref[...] = dh1
            dh1b = dh1.astype(BF16)
            dmix_ref[...] = _dot_nt(dh1b, wo_ref[...])
            dwo = _dot_tn(mix_ref[...], dh1b)
            dg = jnp.sum(dhn * xhat, axis=0, keepdims=True)

            @pl.when(i == 0)
            def _():
                dwo_ref[...] = dwo
                dg_ref[...] = dg

            @pl.when(i > 0)
            def _():
                dwo_ref[...] += dwo
                dg_ref[...] += dg

    return _call(
        body, (df, w_ff1, dh2, h1, gain2, mix, w_out), grid=(steps, nj), name="ffn_bwd_input",
        in_specs=[pl.BlockSpec((tm, 1024), lambda i, j: (i, j)), pl.BlockSpec((1, D, 1024), lambda i, j: (j, 0, 0)),
                  pl.BlockSpec((tm, D), lambda i, j: (i, 0)), pl.BlockSpec((tm, D), lambda i, j: (i, 0)),
                  pl.BlockSpec((1, D), lambda i, j: (0, 0)), pl.BlockSpec((tm, D), lambda i, j: (i, 0)),
                  pl.BlockSpec((D, D), lambda i, j: (0, 0))],
        out_specs=[pl.BlockSpec((tm, D), lambda i, j: (i, 0)), pl.BlockSpec((tm, D), lambda i, j: (i, 0)),
                   pl.BlockSpec((D, D), lambda i, j: (0, 0)), pl.BlockSpec((1, D), lambda i, j: (0, 0))],
        out_shape=[jax.ShapeDtypeStruct((S, D), F32), jax.ShapeDtypeStruct((S, D), F32),
                   jax.ShapeDtypeStruct((D, D), F32), jax.ShapeDtypeStruct((1, D), F32)],
        scratch_shapes=[pltpu.VMEM((tm, D), F32)],
        compiler_params=_params(56, 2), exchange=exchange)


IN_GROUP = 8


def _mixer_bwd(z, dmix, v_gain, w_spatial, b_spatial_t, sinks, rel_table, bucket, hn1, exchange=None):
    def body(z_ref, kvp_ref, dm_ref, gain_ref, ws_ref, bt_ref, sink_ref, table_ref, bucket_ref, hn_ref,
             dz_ref, dws_ref, db_ref, dgain_ref, dsink_ref, drel_ref, dwin_ref,
             bias_ref, wt_ref, wtt_ref, dbias_ref, dsv_ref, carry_ref):
        n = pl.program_id(0)

        @pl.when(n == 0)
        def _():
            _fill_bias(bucket_ref, table_ref, bias_ref)
            _fill_tril(ws_ref, wt_ref, wtt_ref)
            dwin_ref[...] = jnp.zeros_like(dwin_ref)
            dbias_ref[...] = jnp.zeros_like(dbias_ref)
            dsv_ref[...] = jnp.zeros_like(dsv_ref)
            dws_ref[...] = jnp.zeros_like(dws_ref)
            dgain_ref[...] = jnp.zeros_like(dgain_ref)
            dsink_ref[...] = jnp.zeros_like(dsink_ref)

        rows = pl.ds(pl.multiple_of(n * CHUNK, CHUNK), CHUNK)

        zuv = z_ref[:, :1024]
        cdf, t = _gelu_parts(zuv)
        guv = zuv * cdf
        dgelu = cdf + zuv * (0.5 * (1.0 - t * t)) * (GELU_C * (1.0 + 3.0 * 0.044715 * (zuv * zuv)))
        for g in range(N_GROUP):
            lo, hi = 128 * g, 128 * (g + 1)
            u = guv[:, lo:hi]
            vg = guv[:, 512 + lo:512 + hi]
            rr = _rms_scale(vg)
            vhat = vg * rr
            gain = gain_ref[:, lo:hi]
            vnb = (vhat * gain).astype(BF16)
            sv = _dot(wt_ref[g], vnb) + bt_ref[:, g:g + 1]
            da = dm_ref[:, lo:hi]
            dsv = da * u
            dsvb = dsv.astype(BF16)
            dsv_ref[g] += dsv
            dws_ref[g] += _dot_nt(dsvb, vnb)
            dvn = _dot(wtt_ref[g], dsvb)
            dgain_ref[:, lo:hi] += jnp.sum(dvn * vhat, axis=0, keepdims=True)
            dvg = _rms_bwd(dvn * gain, vhat, rr)
            dz_ref[rows, lo:hi] = (da * sv * dgelu[:, lo:hi]).astype(BF16)
            dz_ref[rows, 512 + lo:512 + hi] = (dvg * dgelu[:, 512 + lo:512 + hi]).astype(BF16)

        k_same, k_swap, v_same, v_swap = _kv_layouts(kvp_ref[...], z_ref[:, 1536:1792])
        valid = _band_mask(n)
        lane_half = lax.broadcasted_iota(jnp.int32, (1, 128), 1) // 64
        zero = jnp.zeros((2 * CHUNK, 128), F32)
        dk_same, dk_swap, dv_same, dv_swap = zero, zero, zero, zero
        for pair in range(N_HEAD // 2):
            cols = slice(1024 + 128 * pair, 1024 + 128 * (pair + 1))
            qq = z_ref[:, cols]
            do_pair = dm_ref[:, 512 + 128 * pair:512 + 128 * (pair + 1)]
            dq = jnp.zeros((CHUNK, 128), F32)
            for pos in range(2):
                h = 2 * pair + pos
                _, _, same = _head_place(h)
                on_half = lane_half == pos
                qm = jnp.where(on_half, qq, 0.0).astype(BF16)
                k_use = k_same if same else k_swap
                v_use = v_same if same else v_swap
                p, p_sink = _softmax_sink(qm, k_use, bias_ref[h], sink_ref[h], valid)
                dom = jnp.where(on_half, do_pair, 0.0).astype(BF16)
                dp = _dot_nt(dom, v_use)
                dsum = jnp.sum(p * dp, axis=-1, keepdims=True)
                ds = p * (dp - dsum)
                dbias_ref[h] += ds
                dsink_ref[h:h + 1, :] += jnp.broadcast_to(jnp.sum(-p_sink * dsum, axis=0, keepdims=True), (1, 128))
                dsb = ds.astype(BF16)
                dq = dq + jnp.where(on_half, _dot(dsb, k_use), 0.0)
                dk_h = _dot_tn(dsb, qm)
                dv_h = _dot_tn(p.astype(BF16), dom)
                if same:
                    dk_same, dv_same = dk_same + dk_h, dv_same + dv_h
                else:
                    dk_swap, dv_swap = dk_swap + dk_h, dv_swap + dv_h
            dz_ref[rows, cols] = (dq * QK_SCALE).astype(BF16)
        dk = (dk_same + pltpu.roll(dk_swap, 64, axis=1)) * QK_SCALE
        dv = dv_same + pltpu.roll(dv_swap, 64, axis=1)
        dkv = jnp.concatenate([dk, dv], axis=1)

        @pl.when(n > 0)
        def _():
            prev_rows = pl.ds(pl.multiple_of((n - 1) * CHUNK, CHUNK), CHUNK)
            dz_ref[prev_rows, 1536:1792] = (carry_ref[...] + dkv[:CHUNK]).astype(BF16)

        carry_ref[...] = dkv[CHUNK:]

        @pl.when((n > 0) & (n % IN_GROUP == 0))
        def _():
            done = pl.ds(pl.multiple_of((n - IN_GROUP) * CHUNK, IN_GROUP * CHUNK), IN_GROUP * CHUNK)
            dwin_ref[...] += _dot_tn(dz_ref[done, :], hn_ref[...])

        @pl.when(n == N_BLOCK - 1)
        def _():
            dz_ref[rows, 1536:1792] = dkv[CHUNK:].astype(BF16)
            last = pl.ds((N_BLOCK - IN_GROUP) * CHUNK, IN_GROUP * CHUNK)
            dwin_ref[...] += _dot_tn(dz_ref[last, :], hn_ref[...])
            r = lax.broadcasted_iota(jnp.int32, (CHUNK, CHUNK), 0)
            c = lax.broadcasted_iota(jnp.int32, (CHUNK, CHUNK), 1)
            for g in range(N_GROUP):
                dws_ref[g] = jnp.where(c <= r, dws_ref[g], 0.0)
                db_ref[g] = jnp.sum(dsv_ref[g], axis=1, keepdims=True)
            bucket = bucket_ref[...]
            for h in range(N_HEAD):
                dbh = dbias_ref[h]
                per_bucket = [jnp.sum(jnp.where(bucket == b, dbh, 0.0), axis=0, keepdims=True) for b in range(N_BUCKET)]
                drel_ref[h] = jnp.sum(jnp.concatenate(per_bucket, axis=0), axis=1, keepdims=True)

    def hn_group(n):
        return jnp.where(n == N_BLOCK - 1, N_BLOCK // IN_GROUP - 1, jnp.maximum(n // IN_GROUP - 1, 0))

    return _call(
        body, (z, z, dmix, v_gain, w_spatial, b_spatial_t, sinks, rel_table, bucket, hn1), grid=(N_BLOCK,),
        name="mixer_bwd",
        in_specs=[pl.BlockSpec((CHUNK, D_IN), lambda n: (n, 0)),
                  pl.BlockSpec((CHUNK, 256), lambda n: (jnp.maximum(n - 1, 0), 6)),
                  pl.BlockSpec((CHUNK, D), lambda n: (n, 0)),
                  pl.BlockSpec((1, 512), lambda n: (0, 0)),
                  pl.BlockSpec((N_GROUP, CHUNK, CHUNK), lambda n: (0, 0, 0)),
                  pl.BlockSpec((CHUNK, N_GROUP), lambda n: (0, 0)),
                  pl.BlockSpec(memory_space=pltpu.SMEM),
                  pl.BlockSpec(memory_space=pltpu.SMEM),
                  pl.BlockSpec((CHUNK, 2 * CHUNK), lambda n: (0, 0)),
                  pl.BlockSpec((IN_GROUP * CHUNK, D), lambda n: (hn_group(n), 0))],
        out_specs=[pl.BlockSpec((S, D_IN), lambda n: (0, 0)),
                   pl.BlockSpec((N_GROUP, CHUNK, CHUNK), lambda n: (0, 0, 0)),
                   pl.BlockSpec((N_GROUP, CHUNK, 1), lambda n: (0, 0, 0)),
                   pl.BlockSpec((1, 512), lambda n: (0, 0)),
                   pl.BlockSpec((N_HEAD, 128), lambda n: (0, 0)),
                   pl.BlockSpec((N_HEAD, N_BUCKET, 1), lambda n: (0, 0, 0)),
                   pl.BlockSpec((D_IN, D), lambda n: (0, 0))],
        out_shape=[jax.ShapeDtypeStruct((S, D_IN), BF16), jax.ShapeDtypeStruct((N_GROUP, CHUNK, CHUNK), F32),
                   jax.ShapeDtypeStruct((N_GROUP, CHUNK, 1), F32), jax.ShapeDtypeStruct((1, 512), F32),
                   jax.ShapeDtypeStruct((N_HEAD, 128), F32), jax.ShapeDtypeStruct((N_HEAD, N_BUCKET, 1), F32),
                   jax.ShapeDtypeStruct((D_IN, D), F32)],
        scratch_shapes=[pltpu.VMEM((N_HEAD, CHUNK, 2 * CHUNK), F32), pltpu.VMEM((N_GROUP, CHUNK, CHUNK), BF16),
                        pltpu.VMEM((N_GROUP, CHUNK, CHUNK), BF16), pltpu.VMEM((N_HEAD, CHUNK, 2 * CHUNK), F32),
                        pltpu.VMEM((N_GROUP, CHUNK, CHUNK), F32), pltpu.VMEM((CHUNK, 256), F32)],
        compiler_params=_params(56), exchange=exchange)


def _in_bwd_input(dz, w_in_t, x, dh1, gain1, exchange=None):
    tm = 512

    def body(dz_ref, w_ref, x_ref, dh1_ref, g_ref, dx_ref, dg_ref):
        i = pl.program_id(0)
        dhn = _dot(dz_ref[...], w_ref[...])
        xv = x_ref[...]
        r1 = _rms_scale(xv)
        xhat = xv * r1
        dx_ref[...] = dh1_ref[...] + _rms_bwd(dhn * g_ref[...], xhat, r1)
        dg = jnp.sum(dhn * xhat, axis=0, keepdims=True)

        @pl.when(i == 0)
        def _():
            dg_ref[...] = dg

        @pl.when(i > 0)
        def _():
            dg_ref[...] += dg

    return _call(
        body, (dz, w_in_t, x, dh1, gain1), grid=(S // tm,), name="in_bwd_input",
        in_specs=[pl.BlockSpec((tm, D_IN), lambda i: (i, 0)), pl.BlockSpec((D_IN, D), lambda i: (0, 0)),
                  pl.BlockSpec((tm, D), lambda i: (i, 0)), pl.BlockSpec((tm, D), lambda i: (i, 0)),
                  pl.BlockSpec((1, D), lambda i: (0, 0))],
        out_specs=[pl.BlockSpec((tm, D), lambda i: (i, 0)), pl.BlockSpec((1, D), lambda i: (0, 0))],
        out_shape=[jax.ShapeDtypeStruct((S, D), F32), jax.ShapeDtypeStruct((1, D), F32)],
        compiler_params=_params(48), exchange=exchange)


def _rel_bucket():
    a = jnp.arange(CHUNK)[:, None]
    j = jnp.arange(2 * CHUNK)[None, :]
    n = jnp.maximum(CHUNK + a - j, 0)
    max_exact = N_BUCKET // 2
    nf = jnp.maximum(n, 1).astype(jnp.float32)
    large = max_exact + (jnp.log(nf / max_exact) / math.log(CHUNK / max_exact) * (N_BUCKET - max_exact)).astype(jnp.int32)
    large = jnp.minimum(large, N_BUCKET - 1)
    return jnp.where(n < max_exact, n, large).astype(jnp.int32)


def _step(x, p, target, small, bufs, place):
    bucket = _rel_bucket()
    sinks = small["attn_sinks"].reshape(N_HEAD)
    b_t = jnp.transpose(small["b_spatial"].reshape(N_GROUP, CHUNK))
    ws = small["w_spatial"].reshape(N_GROUP, CHUNK, CHUNK)
    gain1, gain2 = small["norm1_gain"], small["norm2_gain"]
    v_gain = small["gmlp_v_gain"]
    final_gain = small["final_gain"].reshape(1, D)
    table = small["rel_bias_table"]
    bufs = dict(bufs)

    def gather(*names):
        return _RelayGather([bufs[n] for n in names])

    def took(names, got):
        bufs.update(zip(names, got))

    w_in_t = _whole(bufs["w_in"]).reshape(D_IN, D)
    (z, hn1), got = _in_proj(x, gain1, w_in_t, gather("w_out"))
    took(["w_out"], got)
    (mix,), got = _mixer_fwd(z, v_gain, ws, b_t, sinks, table, bucket, gather("w_ff1"))
    took(["w_ff1"], got)
    w_out = _whole(bufs["w_out"]).reshape(D, D)
    (h1, hn2, hn2_t), _ = _out_proj(x, mix, w_out, gain2)
    w_ff1 = _whole(bufs["w_ff1"])
    (r, a, a_t), got = _ffn_up(hn2, w_ff1, gather("w_ff2"))
    took(["w_ff2"], got)
    w_ff2 = _whole(bufs["w_ff2"])
    (h2,), got = _ffn_down(h1, a, w_ff2, gather("w_ple_gate", "w_ple_proj"))
    took(["w_ple_gate", "w_ple_proj"], got)
    dh2, d_gate, d_proj, d_final, sq, dh2b = _tail(h2, p, target, _whole(bufs["w_ple_gate"]).reshape(D, D),
                                                   _whole(bufs["w_ple_proj"]), final_gain)

    def pair_sums(halves, from_sibling):
        sums, landing = zip(*[_pair_sum(g, o, place) for g, o in zip(halves, from_sibling)])
        return list(sums), list(landing)

    landed = {}
    halves = [_halves(d_gate.reshape(N_CHIP, 256, D)), _halves(d_proj)]
    (df, d_ff2), got = _ffn_bwd_down(dh2b, r, a_t, w_ff2, _SiblingExchange(halves))
    ex, halves = _ChipExchange(*pair_sums(halves, got)), [_halves(d_ff2)]
    (d_ff1,), got = _ffn_bwd_up(df, hn2_t, _Both(ex, _SiblingExchange(halves)))
    landed.update(zip(["w_ple_gate", "w_ple_proj"], got[:2]))
    ex, halves = _ChipExchange(*pair_sums(halves, got[2:])), [_halves(d_ff1)]
    (dh1, dmix, d_out, d_gain2), got = _ffn_bwd_input(df, w_ff1, dh2, h1, gain2, mix, w_out,
                                                      _Both(ex, _SiblingExchange(halves)))
    landed["w_ff2"] = got[0]
    ex, halves = _ChipExchange(*pair_sums(halves, got[1:])), [_halves(d_out.reshape(N_CHIP, 256, D))]
    (dz, d_ws, d_b, d_vgain, d_sink, d_rel, d_in_t), got = _mixer_bwd(z, dmix, v_gain, ws, b_t, sinks, table, bucket, hn1,
                                                                     _Both(ex, _SiblingExchange(halves)))
    landed["w_ff1"] = got[0]
    small_grads = {
        "gmlp_v_gain": d_vgain, "w_spatial": d_ws.reshape(1, N_GROUP, CHUNK, CHUNK),
        "b_spatial": d_b.reshape(1, N_GROUP, CHUNK), "attn_sinks": d_sink[:, 0].reshape(1, N_HEAD),
        "rel_bias_table": jnp.transpose(d_rel.reshape(N_HEAD, N_BUCKET)), "norm2_gain": d_gain2,
        "final_gain": d_final.reshape(D),
    }
    ex, halves = _ChipExchange(*pair_sums(halves, got[1:])), [_halves(d_in_t.reshape(N_CHIP, 448, D))]
    (dx, small_grads["norm1_gain"]), got = _in_bwd_input(dz, w_in_t, x, dh1, gain1, _Both(ex, _SiblingExchange(halves)))
    landed["w_out"] = got[0]
    return dx, landed, _ChipExchange(*pair_sums(halves, got[1:])), small_grads, sq


HBM_SPEC = pl.BlockSpec(memory_space=pltpu.HBM)
VMEM_SPEC = pl.BlockSpec(memory_space=pltpu.VMEM)


def _mesh_place():
    x, y, c = lax.axis_index("x"), lax.axis_index("y"), lax.axis_index("c")
    others = [(1 - x, y), (x, 1 - y), (1 - x, 1 - y)]
    return x, y, c, others


def _remote(src, dst, send_sem, recv_sem, device):
    return pltpu.make_async_remote_copy(src_ref=src, dst_ref=dst, send_sem=send_sem, recv_sem=recv_sem,
                                        device_id=device, device_id_type=MESH)


def _hbm_like(a, shape=None, dtype=None):
    return pltpu.HBM(a.shape if shape is None else shape, a.dtype if dtype is None else dtype)


def _gather_start(bufs, send_sems, recv_sems):
    x, y, c, others = _mesh_place()
    me = 2 * x + y
    for w, buf in enumerate(bufs):
        for k in range(3):
            mine = buf.at[me, c]
            _remote(mine, mine, send_sems.at[w, k], recv_sems.at[w, k], (*others[k], c)).start()


def _gather_finish(bufs, send_sems, recv_sems):
    x, y, c, others = _mesh_place()
    me = 2 * x + y
    sibling = (x, y, 1 - c)
    idx = [2 * ox + oy for ox, oy in others]
    chips = range(3)
    for w, buf in enumerate(bufs):
        for k in chips:
            landed = buf.at[idx[k], c]
            _remote(landed, landed, send_sems.at[w, k], recv_sems.at[w, k], sibling).wait_recv()
            _remote(landed, landed, send_sems.at[w, 3 + k], recv_sems.at[w, 3 + k], sibling).start()
    for w, buf in enumerate(bufs):
        for k in chips:
            landed = buf.at[idx[k], 1 - c]
            _remote(landed, landed, send_sems.at[w, 3 + k], recv_sems.at[w, 3 + k], sibling).wait_recv()
    for w, buf in enumerate(bufs):
        for k in chips:
            mine, passed = buf.at[me, c], buf.at[idx[k], c]
            _remote(mine, mine, send_sems.at[w, k], recv_sems.at[w, k], sibling).wait_send()
            _remote(passed, passed, send_sems.at[w, 3 + k], recv_sems.at[w, 3 + k], sibling).wait_send()


def _gather_sems(n):
    return [pltpu.SemaphoreType.DMA((n, 6)), pltpu.SemaphoreType.DMA((n, 6))]


def _sibling_copies(grads, landing, send_sems, recv_sems):
    x, y, c, _ = _mesh_place()
    return [_remote(grads[w].at[j, 1 - c], landing[w].at[j], send_sems.at[w, j], recv_sems.at[w, j], (x, y, 1 - c))
            for w in range(len(grads)) for j in range(N_CHIP)]


def _sibling_exchange_start(grads, landing, send_sems, recv_sems):
    for cp in _sibling_copies(grads, landing, send_sems, recv_sems):
        cp.start()


def _sibling_exchange_finish(grads, landing, send_sems, recv_sems):
    copies = _sibling_copies(grads, landing, send_sems, recv_sems)
    for cp in copies:
        cp.wait_recv()
    for cp in copies:
        cp.wait_send()


def _sibling_exchange_sems(n):
    return [pltpu.SemaphoreType.DMA((n, N_CHIP)), pltpu.SemaphoreType.DMA((n, N_CHIP))]


def _sibling_exchange(grads):
    n = len(grads)

    def body(*refs):
        ins, outs = refs[:n], refs[n:2 * n]
        _sibling_exchange_start(ins, outs, *refs[2 * n:])
        _sibling_exchange_finish(ins, outs, *refs[2 * n:])

    return pl.pallas_call(
        body, name="sibling_exchange",
        in_specs=[HBM_SPEC] * n, out_specs=[HBM_SPEC] * n,
        out_shape=[_hbm_like(g, (N_CHIP,) + g.shape[2:]) for g in grads],
        scratch_shapes=_sibling_exchange_sems(n),
    )(*[_in_hbm(g) for g in grads])


def _chip_exchange_start(sums, landing, send_sems, recv_sems):
    x, y, c, others = _mesh_place()
    me = 2 * x + y
    for w in range(len(sums)):
        for k, (ox, oy) in enumerate(others):
            _remote(sums[w].at[2 * ox + oy], landing[w].at[me], send_sems.at[w, k], recv_sems.at[w, k],
                    (ox, oy, c)).start()


def _chip_exchange_finish(sums, landing, send_sems, recv_sems):
    x, y, c, others = _mesh_place()
    for w in range(len(sums)):
        for k, (ox, oy) in enumerate(others):
            piece = landing[w].at[2 * ox + oy]
            _remote(piece, piece, send_sems.at[w, k], recv_sems.at[w, k], (x, y, c)).wait_recv()
    for w in range(len(sums)):
        for k, (ox, oy) in enumerate(others):
            piece = sums[w].at[2 * ox + oy]
            _remote(piece, piece, send_sems.at[w, k], recv_sems.at[w, k], (x, y, c)).wait_send()


def _chip_exchange_sems(n):
    return [pltpu.SemaphoreType.DMA((n, 3)), pltpu.SemaphoreType.DMA((n, 3))]


def _sibling_allgather(bufs, also):
    n = len(bufs)
    k_in, k_out = len(also.operands), also.n_out

    def body(*refs):
        ex_ins, refs = refs[n:n + k_in], refs[n + k_in:]
        outs, refs = refs[:n], refs[n:]
        ex_outs, refs = refs[:k_out], refs[k_out:]
        send_sems, recv_sems, ex_sems = refs[0], refs[1], refs[2:]
        x, y, c, _ = _mesh_place()
        sibling = (x, y, 1 - c)
        also.start(ex_ins, ex_outs, ex_sems)
        sends = [_remote(outs[w].at[c], outs[w].at[c], send_sems.at[w], recv_sems.at[w], sibling) for w in range(n)]
        for cp in sends:
            cp.start()
        for w in range(n):
            landed = outs[w].at[1 - c]
            _remote(landed, landed, send_sems.at[w], recv_sems.at[w], sibling).wait_recv()
        for cp in sends:
            cp.wait_send()
        also.finish(ex_ins, ex_outs, ex_sems)

    res = pl.pallas_call(
        body, name="sibling_allgather",
        in_specs=[HBM_SPEC] * (n + k_in), out_specs=[HBM_SPEC] * (n + k_out),
        out_shape=[_hbm_like(b) for b in bufs] + also.out_shape,
        input_output_aliases={**{w: w for w in range(n)}, **{n + i: n + o for i, o in also.aliases.items()}},
        scratch_shapes=[pltpu.SemaphoreType.DMA((n,)), pltpu.SemaphoreType.DMA((n,))] + also.sems,
    )(*bufs, *[_in_hbm(o) for o in also.operands])
    return list(res[:n]), list(res[n:])


def _pair_sum(grad, other, place):
    _, _, h, cols = grad.shape
    tr = _row_tile(h)

    def body(place_ref, g_ref, o_ref, sums_ref, own_ref):
        s = (g_ref[0, 0] + o_ref[0]).astype(BF16)
        sums_ref[0] = s

        @pl.when(pl.program_id(1) == place_ref[0])
        def _():
            own_ref[0] = s

    return pl.pallas_call(
        body, name="pair_sum",
        grid_spec=pltpu.PrefetchScalarGridSpec(
            num_scalar_prefetch=1, grid=(h // tr, N_CHIP),
            in_specs=[pl.BlockSpec((1, 1, tr, cols), lambda r, j, place_ref: (j, place_ref[1], r, 0)),
                      pl.BlockSpec((1, tr, cols), lambda r, j, place_ref: (j, r, 0))],
            out_specs=[pl.BlockSpec((1, tr, cols), lambda r, j, place_ref: (j, r, 0)),
                       pl.BlockSpec((1, tr, cols), lambda r, j, place_ref: (place_ref[0], r, 0))]),
        out_shape=[pltpu.HBM((N_CHIP, h, cols), BF16)] * 2,
        compiler_params=_params(32, 2),
    )(place, _in_hbm(grad), _in_hbm(other))


def _chip_sum(parts, place):
    _, h, cols = parts.shape
    tr = _row_tile(h)

    def body(place_ref, p_ref, out_ref):
        out_ref[0] = ((p_ref[0].astype(F32) + p_ref[1].astype(F32)) + p_ref[2].astype(F32)) + p_ref[3].astype(F32)

    return pl.pallas_call(
        body, name="chip_sum",
        grid_spec=pltpu.PrefetchScalarGridSpec(
            num_scalar_prefetch=1, grid=(h // tr,),
            in_specs=[pl.BlockSpec((N_CHIP, tr, cols), lambda r, place_ref: (0, r, 0))],
            out_specs=pl.BlockSpec((1, tr, cols), lambda r, place_ref: (place_ref[1], r, 0))),
        out_shape=pltpu.HBM((2, h, cols), F32),
        compiler_params=_params(32),
    )(place, _in_hbm(parts))


def _adamw_math(w, g, m, v):
    m = ADAM_B1 * m + (1.0 - ADAM_B1) * g
    v = ADAM_B2 * v + (1.0 - ADAM_B2) * (g * g)
    m_hat = m / (1.0 - ADAM_B1 ** ADAM_STEP)
    v_hat = v / (1.0 - ADAM_B2 ** ADAM_STEP)
    delta = -ADAM_LR * (m_hat / (jnp.sqrt(v_hat) + ADAM_EPS) + ADAM_WD * w)
    return delta, m, v


def _adamw(w, g, m, v, exchange=None):
    rows, cols = w.shape
    tr = _row_tile(rows)

    def body(w_ref, g_ref, m_ref, v_ref, d_ref, nm_ref, nv_ref, g_out_ref):
        g = g_ref[...]
        d_ref[...], nm_ref[...], nv_ref[...] = _adamw_math(w_ref[...], g, m_ref[...], v_ref[...])
        g_out_ref[...] = g

    spec = pl.BlockSpec((tr, cols), lambda r: (r, 0))
    return _call(
        body, (w, g, m, v), grid=(rows // tr,), name="adamw",
        in_specs=[spec] * 4, out_specs=[spec] * 4,
        out_shape=[jax.ShapeDtypeStruct((rows, cols), F32)] * 4,
        compiler_params=_params(48), exchange=exchange)


SMALL_NAMES = ("norm1_gain", "gmlp_v_gain", "w_spatial", "b_spatial", "attn_sinks", "rel_bias_table", "norm2_gain",
               "final_gain")
PACK_TILE = 8 * 128


def _pack_small(arrays):
    parts = []
    for a in arrays:
        flat = a.reshape(-1)
        rows = -(-flat.shape[0] // PACK_TILE) * 8
        parts.append(jnp.pad(flat, (0, rows * 128 - flat.shape[0])).reshape(rows, 128))
    return jnp.concatenate(parts, axis=0)


def _unpack_small(packed, like):
    out, row = [], 0
    for a in like:
        size = math.prod(a.shape)
        rows = -(-size // PACK_TILE) * 8
        out.append(packed[row:row + rows].reshape(-1)[:size].reshape(a.shape))
        row += rows
    return out


def _small_update(gathered, w, m, v):
    rows = gathered.shape[1]

    def body(g_ref, w_ref, m_ref, v_ref, tot_ref, d_ref, nm_ref, nv_ref):
        total = g_ref[0].astype(F32)
        for dev in range(1, 8):
            total = total + g_ref[dev].astype(F32)
        tot_ref[...] = total
        d_ref[...], nm_ref[...], nv_ref[...] = _adamw_math(w_ref[...], total, m_ref[...], v_ref[...])

    return pl.pallas_call(
        body, name="small_update",
        in_specs=[VMEM_SPEC] * 4, out_specs=[VMEM_SPEC] * 4,
        out_shape=[jax.ShapeDtypeStruct((rows, 128), F32)] * 4,
        compiler_params=pltpu.CompilerParams(vmem_limit_bytes=24 * MIB),
    )(gathered, w, m, v)


def _halves(a):
    return a.reshape(a.shape[:-2] + (2, a.shape[-2] // 2, a.shape[-1]))


def _whole(a):
    return a.reshape(a.shape[:-3] + (2 * a.shape[-2], a.shape[-1]))


def kernel(x, p, norm1_gain, w_in, gmlp_v_gain, w_spatial, b_spatial, attn_sinks, rel_bias_table, w_out, norm2_gain, w_ff1, w_ff2, w_ple_proj, w_ple_gate, final_gain, loss_target, m_norm1_gain, m_w_in, m_gmlp_v_gain, m_w_spatial, m_b_spatial, m_attn_sinks, m_rel_bias_table, m_w_out, m_norm2_gain, m_w_ff1, m_w_ff2, m_w_ple_proj, m_w_ple_gate, m_final_gain, v_norm1_gain, v_w_in, v_gmlp_v_gain, v_w_spatial, v_b_spatial, v_attn_sinks, v_rel_bias_table, v_w_out, v_norm2_gain, v_w_ff1, v_w_ff2, v_w_ple_proj, v_w_ple_gate, v_final_gain):
    given = dict(locals())
    small = {n: given[n] for n in SMALL_NAMES}
    chip = 2 * lax.axis_index("x") + lax.axis_index("y")
    place = jnp.stack([chip, lax.axis_index("c")]).astype(jnp.int32)

    big_names = ("w_in", "w_out", "w_ff1", "w_ff2", "w_ple_proj", "w_ple_gate")
    shards = {n: given[n][0] for n in big_names}
    travel = dict(shards, w_in=jnp.transpose(shards["w_in"]))
    rest = [n for n in big_names if n != "w_in"]
    cast, gathered = _cast_shards_beside_gather([travel[n] for n in rest], place[:1],
                                                [_cast_shard(travel["w_in"], place[:1])])
    bufs = dict(zip(rest + ["w_in"], cast + gathered))
    dx, landed, exchange_in, small_grads, sq = _step(x[0], p[0, 0], loss_target[0], small, bufs, place)

    out_grad, out_delta, out_m, out_v = {}, {}, {}, {}

    def update(n, g, exchange=None):
        to = jnp.transpose if n == "w_in" else (lambda a: a)
        (delta, new_m, new_v, g_out), got = _adamw(to(shards[n]), g, to(given["m_" + n][0]), to(given["v_" + n][0]),
                                                   exchange)
        out_grad[n], out_delta[n], out_m[n], out_v[n] = [to(a)[None] for a in (g_out, delta, new_m, new_v)]
        return got

    spare = jnp.zeros((8, 128), F32)
    small_packed = _pack_small([small_grads[n] for n in SMALL_NAMES] + [spare]).astype(BF16)
    early = [n for n in big_names if n != "w_in"]
    reduced, (small_gathered, sq_gathered, landed_in) = _sibling_allgather(
        [_chip_sum(landed[n], place) for n in early], _Both(_Both(_GatherAll(small_packed), _GatherAll(sq)), exchange_in))
    for n, r in zip(early, reduced):
        update(n, _whole(r))
    (reduced_in,), _ = _sibling_allgather([_chip_sum(landed_in, place)], _Nothing())
    update("w_in", _whole(reduced_in))

    like = [given[n] for n in SMALL_NAMES] + [spare]
    packed = _small_update(small_gathered, *[_pack_small([given[pre + n] for n in SMALL_NAMES] + [spare])
                                             for pre in ("", "m_", "v_")])
    for res, out in zip(packed, (out_grad, out_delta, out_m, out_v)):
        out.update(zip(SMALL_NAMES, _unpack_small(res, like)))
    loss = 0.5 * jnp.sum(sq_gathered[:, 0, 0]) / D

    order = ("norm1_gain", "w_in", "gmlp_v_gain", "w_spatial", "b_spatial", "attn_sinks", "rel_bias_table", "w_out",
             "norm2_gain", "w_ff1", "w_ff2", "w_ple_proj", "w_ple_gate", "final_gain")
    return (loss, dx[None], *[out_grad[n] for n in order], *[out_delta[n] for n in order],
            *[out_m[n] for n in order], *[out_v[n] for n in order])
```

```python
import functools
import math

import jax
import jax.numpy as jnp
from jax import lax
from jax.experimental import pallas as pl
from jax.experimental.pallas import tpu as pltpu

S = 2048
D = 1024
D_IN = 1792
D_FF = 4096
PLE = 256
N_CHIP = 4
N_GROUP = 4
CHUNK = 128
N_HEAD = 8
N_BLOCK = S // CHUNK
N_BUCKET = 32
EPS = 1e-6
NEG_INF = -1e30
QK_SCALE = 0.125
GELU_C = math.sqrt(2.0 / math.pi)

ADAM_LR = 0.001
ADAM_B1 = 0.9
ADAM_B2 = 0.999
ADAM_EPS = 1e-08
ADAM_WD = 0.01
ADAM_STEP = 10

F32 = jnp.float32
BF16 = jnp.bfloat16
MIB = 1024 * 1024
MESH = pl.DeviceIdType.MESH

NT = (((1,), (1,)), ((), ()))
TN = (((0,), (0,)), ((), ()))


def _dot(a, b):
    return jnp.dot(a, b, preferred_element_type=F32)


def _dot_nt(a, b):
    return lax.dot_general(a, b, NT, preferred_element_type=F32)


def _dot_tn(a, b):
    return lax.dot_general(a, b, TN, preferred_element_type=F32)


def _params(vmem_mib, n_axes=1):
    return pltpu.CompilerParams(dimension_semantics=("arbitrary",) * n_axes, vmem_limit_bytes=vmem_mib * MIB)


def _rms_scale(v):
    return lax.rsqrt(jnp.mean(v * v, axis=-1, keepdims=True) + EPS)


def _rms_bwd(dy_gain, xhat, r):
    return r * (dy_gain - xhat * jnp.mean(dy_gain * xhat, axis=-1, keepdims=True))


class _Gather:
    def __init__(self, bufs):
        self.operands = list(bufs)
        self.n_out = len(self.operands)
        self.out_shape = [_hbm_like(b) for b in bufs]
        self.aliases = {w: w for w in range(self.n_out)}
        self.sems = _gather_sems(self.n_out)

    def start(self, ins, outs, sems):
        _gather_start(outs, *sems)

    def finish(self, ins, outs, sems):
        _gather_finish(outs, *sems)


class _RelayGather(_Gather):
    TOP, BOTTOM = 6, 7
    DIAGONAL_PASSED = 5
    MIDDLE_AT, LATE_AT = (5, 8), (7, 8)

    def __init__(self, bufs):
        super().__init__(bufs)
        self.sems = [pltpu.SemaphoreType.DMA((self.n_out, 8)), pltpu.SemaphoreType.DMA((self.n_out, 8))]

    def _copies(self, bufs, send_sems, recv_sems):
        x, y, c, others = _mesh_place()
        me = 2 * x + y
        idx = [2 * ox + oy for ox, oy in others]
        sibling = (x, y, 1 - c)
        direct, passed, relayed = [], [], []
        for w, buf in enumerate(bufs):
            rows = buf.shape[2] // 2
            upper, lower = pl.ds(0, rows), pl.ds(rows, rows)
            for k in (0, 1):
                mine = buf.at[me, c]
                direct.append((_remote(mine, mine, send_sems.at[w, k], recv_sems.at[w, k], (*others[k], c)),
                               buf.at[idx[k], c], w, k))
            for k in (0, 1, 2):
                here = buf.at[idx[k], c]
                passed.append((_remote(here, here, send_sems.at[w, 3 + k], recv_sems.at[w, 3 + k], sibling),
                               buf.at[idx[k], 1 - c], w, 3 + k))
            from_x, from_y = buf.at[idx[0], c, upper], buf.at[idx[1], c, lower]
            relayed.append((_remote(from_x, from_x, send_sems.at[w, self.TOP], recv_sems.at[w, self.TOP],
                                    (*others[1], c)), buf.at[idx[2], c, upper], w, self.TOP))
            relayed.append((_remote(from_y, from_y, send_sems.at[w, self.BOTTOM], recv_sems.at[w, self.BOTTOM],
                                    (*others[0], c)), buf.at[idx[2], c, lower], w, self.BOTTOM))
        return direct, passed, relayed

    @staticmethod
    def _landed(piece, send_sems, recv_sems, w, col):
        x, y, c, _ = _mesh_place()
        _remote(piece, piece, send_sems.at[w, col], recv_sems.at[w, col], (x, y, c)).wait_recv()

    def start(self, ins, outs, sems):
        for cp, _, _, _ in self._copies(outs, *sems)[0]:
            cp.start()

    def middle(self, ins, outs, sems):
        direct, passed, relayed = self._copies(outs, *sems)
        for _, piece, w, col in direct:
            self._landed(piece, *sems, w, col)
        for cp, _, _, col in passed:
            if col != self.DIAGONAL_PASSED:
                cp.start()
        for cp, _, _, _ in relayed:
            cp.start()

    def late(self, ins, outs, sems):
        direct, passed, relayed = self._copies(outs, *sems)
        for _, piece, w, col in relayed:
            self._landed(piece, *sems, w, col)
        for cp, _, _, col in passed:
            if col == self.DIAGONAL_PASSED:
                cp.start()

    def finish(self, ins, outs, sems):
        direct, passed, relayed = self._copies(outs, *sems)
        for _, piece, w, col in passed:
            self._landed(piece, *sems, w, col)
        for cp, _, _, _ in direct + passed + relayed:
            cp.wait_send()


class _ChipExchange:
    def __init__(self, sums, landing):
        self.n_out = len(landing)
        self.operands = list(sums) + list(landing)
        self.out_shape = [_hbm_like(b) for b in landing]
        self.aliases = {self.n_out + w: w for w in range(self.n_out)}
        self.sems = _chip_exchange_sems(self.n_out)

    def start(self, ins, outs, sems):
        _chip_exchange_start(ins[:self.n_out], outs, *sems)

    def finish(self, ins, outs, sems):
        _chip_exchange_finish(ins[:self.n_out], outs, *sems)


class _GatherAll:
    def __init__(self, packed):
        self.operands = [packed]
        self.n_out = 1
        self.out_shape = [_hbm_like(packed, (8,) + packed.shape)]
        self.aliases = {}
        self.sems = [pltpu.SemaphoreType.DMA((8,)), pltpu.SemaphoreType.DMA((8,))]

    def _copies(self, ins, outs, sems):
        x, y, c, _ = _mesh_place()
        me = 4 * x + 2 * y + c
        send_sems, recv_sems = sems
        copies = []
        for k in range(1, 8):
            peer = (1 - x if k // 4 else x, 1 - y if (k // 2) % 2 else y, 1 - c if k % 2 else c)
            src = 4 * peer[0] + 2 * peer[1] + peer[2]
            copies.append((_remote(ins[0], outs[0].at[me], send_sems.at[k], recv_sems.at[k], peer), outs[0].at[src]))
        own = pltpu.make_async_copy(ins[0], outs[0].at[me], send_sems.at[0])
        return own, copies

    def start(self, ins, outs, sems):
        own, copies = self._copies(ins, outs, sems)
        own.start()
        for cp, _ in copies:
            cp.start()

    def finish(self, ins, outs, sems):
        own, copies = self._copies(ins, outs, sems)
        x, y, c, _ = _mesh_place()
        for k, (cp, landed) in enumerate(copies):
            _remote(landed, landed, sems[0].at[k + 1], sems[1].at[k + 1], (x, y, c)).wait_recv()
        for cp, _ in copies:
            cp.wait_send()
        own.wait()


class _Nothing:
    operands, n_out, out_shape, aliases, sems = [], 0, [], {}, []

    def start(self, ins, outs, sems):
        pass

    def finish(self, ins, outs, sems):
        pass


class _Both:
    def __init__(self, a, b):
        self.a, self.b = a, b
        self.operands = a.operands + b.operands
        self.n_out = a.n_out + b.n_out
        self.out_shape = a.out_shape + b.out_shape
        self.aliases = dict(a.aliases)
        self.aliases.update({len(a.operands) + i: a.n_out + o for i, o in b.aliases.items()})
        self.sems = a.sems + b.sems

    def _split(self, ins, outs, sems):
        ka, na, sa = len(self.a.operands), self.a.n_out, len(self.a.sems)
        return (ins[:ka], outs[:na], sems[:sa]), (ins[ka:], outs[na:], sems[sa:])

    def start(self, ins, outs, sems):
        for ex, args in zip((self.a, self.b), self._split(ins, outs, sems)):
            ex.start(*args)

    def finish(self, ins, outs, sems):
        for ex, args in zip((self.a, self.b), self._split(ins, outs, sems)):
            ex.finish(*args)


class _SiblingExchange:
    def __init__(self, grads):
        self.operands = list(grads)
        self.n_out = len(self.operands)
        self.out_shape = [_hbm_like(g, (N_CHIP,) + g.shape[2:]) for g in grads]
        self.aliases = {}
        self.sems = _sibling_exchange_sems(self.n_out)

    def start(self, ins, outs, sems):
        _sibling_exchange_start(ins, outs, *sems)

    def finish(self, ins, outs, sems):
        _sibling_exchange_finish(ins, outs, *sems)


def _call(body, operands, *, grid, in_specs, out_specs, out_shape, name, compiler_params, scratch_shapes=(),
          exchange=None):
    operands = [o if getattr(spec, "memory_space", None) == pltpu.SMEM else _in_hbm(o)
                for o, spec in zip(operands, in_specs)]
    out_shape = [pltpu.HBM(s.shape, s.dtype) for s in out_shape]
    if exchange is None:
        res = pl.pallas_call(body, grid=grid, in_specs=in_specs, out_specs=out_specs, out_shape=out_shape, name=name,
                             scratch_shapes=list(scratch_shapes), compiler_params=compiler_params)(*operands)
        return list(res), []
    n_in, n_out, n_scr = len(in_specs), len(out_specs), len(scratch_shapes)
    k_in, k_out = len(exchange.operands), exchange.n_out

    def fused(*refs):
        ins, refs = refs[:n_in], refs[n_in:]
        ex_ins, refs = refs[:k_in], refs[k_in:]
        outs, refs = refs[:n_out], refs[n_out:]
        ex_outs, refs = refs[:k_out], refs[k_out:]
        scratch, sems = refs[:n_scr], refs[n_scr:]
        ids = [pl.program_id(a) for a in range(len(grid))]
        first = functools.reduce(jnp.logical_and, [i == 0 for i in ids])
        last = functools.reduce(jnp.logical_and, [i == g - 1 for i, g in zip(ids, grid)])

        @pl.when(first)
        def _():
            exchange.start(ex_ins, ex_outs, sems)

        def at_step(numerator, denominator):
            at = (numerator * math.prod(grid)) // denominator
            place = [(at // math.prod(grid[a + 1:])) % grid[a] for a in range(len(grid))]
            return functools.reduce(jnp.logical_and, [i == p for i, p in zip(ids, place)])

        if hasattr(exchange, "middle"):
            @pl.when(at_step(*exchange.MIDDLE_AT))
            def _():
                exchange.middle(ex_ins, ex_outs, sems)

            @pl.when(at_step(*exchange.LATE_AT))
            def _():
                exchange.late(ex_ins, ex_outs, sems)

        body(*ins, *outs, *scratch)

        @pl.when(last)
        def _():
            exchange.finish(ex_ins, ex_outs, sems)

    res = pl.pallas_call(
        fused, grid=grid, name=name,
        in_specs=list(in_specs) + [HBM_SPEC] * k_in, out_specs=list(out_specs) + [HBM_SPEC] * k_out,
        out_shape=list(out_shape) + exchange.out_shape,
        input_output_aliases={n_in + i: n_out + o for i, o in exchange.aliases.items()},
        scratch_shapes=list(scratch_shapes) + exchange.sems, compiler_params=compiler_params,
    )(*operands, *[_in_hbm(o) for o in exchange.operands])
    return list(res[:n_out]), list(res[n_out:])


def _in_hbm(a):
    return pltpu.with_memory_space_constraint(a, pltpu.HBM)


def _row_tile(h):
    return max(t for t in range(16, 513, 16) if h % t == 0)


def _cast_shard(a, chip):
    rows, cols = a.shape
    h = rows // 2
    tr = _row_tile(h)

    def body(chip_ref, a_ref, o_ref):
        o_ref[0, 0] = a_ref[0].astype(BF16)

    return pl.pallas_call(
        body, name="cast_shard",
        grid_spec=pltpu.PrefetchScalarGridSpec(
            num_scalar_prefetch=1, grid=(2, h // tr),
            in_specs=[pl.BlockSpec((1, tr, cols), lambda s, r, chip_ref: (s, r, 0))],
            out_specs=pl.BlockSpec((1, 1, tr, cols), lambda s, r, chip_ref: (chip_ref[0], s, r, 0))),
        out_shape=pltpu.HBM((N_CHIP, 2, h, cols), BF16),
        compiler_params=_params(16, 2),
    )(chip, _in_hbm(a.reshape(2, h, cols)))


def _cast_shards_beside_gather(arrays, chip, gathered):
    n, k = len(arrays), len(gathered)
    shapes = [(a.shape[0] // 2, a.shape[1]) for a in arrays]

    def body(chip_ref, *refs):
        ins, refs = refs[:n], refs[n + k:]
        outs, refs = refs[:n], refs[n:]
        bufs, sems = refs[:k], refs[k:]
        half = pl.program_id(0)

        @pl.when(half == 0)
        def _():
            _gather_start(bufs, *sems)

        for a_ref, o_ref in zip(ins, outs):
            o_ref[0, 0] = a_ref[0].astype(BF16)

        @pl.when(half == 1)
        def _():
            _gather_finish(bufs, *sems)

    res = pl.pallas_call(
        body, name="cast_shards",
        grid_spec=pltpu.PrefetchScalarGridSpec(
            num_scalar_prefetch=1, grid=(2,),
            in_specs=[pl.BlockSpec((1, h, c), lambda s, chip_ref: (s, 0, 0)) for h, c in shapes] + [HBM_SPEC] * k,
            out_specs=[pl.BlockSpec((1, 1, h, c), lambda s, chip_ref: (chip_ref[0], s, 0, 0)) for h, c in shapes]
            + [HBM_SPEC] * k,
            scratch_shapes=_gather_sems(k)),
        out_shape=[pltpu.HBM((N_CHIP, 2, h, c), BF16) for h, c in shapes] + [_hbm_like(b) for b in gathered],
        input_output_aliases={1 + n + i: n + i for i in range(k)},
        compiler_params=_params(32),
    )(chip, *[_in_hbm(a.reshape(2, h, c)) for a, (h, c) in zip(arrays, shapes)], *gathered)
    return list(res[:n]), list(res[n:])


def _in_proj(x, gain1, w_in_t, exchange=None):
    tm = 256

    def body(x_ref, g_ref, w_ref, z_ref, hn_ref):
        xv = x_ref[...]
        hn = (xv * _rms_scale(xv) * g_ref[...]).astype(BF16)
        hn_ref[...] = hn
        z_ref[...] = _dot_nt(hn, w_ref[...])

    return _call(
        body, (x, gain1, w_in_t), grid=(S // tm,), name="in_proj",
        in_specs=[pl.BlockSpec((tm, D), lambda i: (i, 0)), pl.BlockSpec((1, D), lambda i: (0, 0)),
                  pl.BlockSpec((D_IN, D), lambda i: (0, 0))],
        out_specs=[pl.BlockSpec((tm, D_IN), lambda i: (i, 0)), pl.BlockSpec((tm, D), lambda i: (i, 0))],
        out_shape=[jax.ShapeDtypeStruct((S, D_IN), F32), jax.ShapeDtypeStruct((S, D), BF16)],
        compiler_params=_params(40), exchange=exchange)


def _gelu_parts(v):
    t = jnp.tanh(GELU_C * (v + 0.044715 * (v * v * v)))
    cdf = 0.5 * (1.0 + t)
    return cdf, t


def _band_mask(n):
    a = lax.broadcasted_iota(jnp.int32, (CHUNK, 2 * CHUNK), 0)
    j = lax.broadcasted_iota(jnp.int32, (CHUNK, 2 * CHUNK), 1)
    dist = CHUNK + a - j
    valid = (dist >= 0) & (dist < CHUNK)
    return valid & ((n > 0) | (j >= CHUNK))


def _fill_bias(bucket_ref, table_ref, bias_ref):
    bucket = bucket_ref[...]
    for h in range(N_HEAD):
        acc = jnp.zeros((CHUNK, 2 * CHUNK), F32)
        for b in range(N_BUCKET):
            acc = jnp.where(bucket == b, table_ref[b, h], acc)
        bias_ref[h] = acc


def _fill_tril(ws_ref, wt_ref, wtt_ref=None):
    r = lax.broadcasted_iota(jnp.int32, (CHUNK, CHUNK), 0)
    c = lax.broadcasted_iota(jnp.int32, (CHUNK, CHUNK), 1)
    for g in range(N_GROUP):
        w = jnp.where(c <= r, ws_ref[g], 0.0)
        wt_ref[g] = w.astype(BF16)
        if wtt_ref is not None:
            wtt_ref[g] = w.T.astype(BF16)


def _kv_layouts(kv_prev, kv_cur):
    both = jnp.concatenate([kv_prev, kv_cur], axis=0)
    k = both[:, :128]
    v = both[:, 128:]
    return (k.astype(BF16), pltpu.roll(k, 64, axis=1).astype(BF16),
            v.astype(BF16), pltpu.roll(v, 64, axis=1).astype(BF16))


def _head_place(h):
    pair, pos, kvh = h // 2, h % 2, h // 4
    return pair, pos, kvh == pos


def _softmax_sink(qm, k_use, bias_h, sink, valid):
    s = _dot_nt(qm, k_use) * QK_SCALE + bias_h
    s = jnp.where(valid, s, NEG_INF)
    m = jnp.maximum(jnp.max(s, axis=-1, keepdims=True), sink)
    e = jnp.exp(s - m)
    es = jnp.exp(sink - m)
    inv = 1.0 / (jnp.sum(e, axis=-1, keepdims=True) + es)
    return e * inv, es * inv


def _mixer_fwd(z, v_gain, w_spatial, b_spatial_t, sinks, rel_table, bucket, exchange=None):
    def body(z_ref, kvp_ref, gain_ref, ws_ref, bt_ref, sink_ref, table_ref, bucket_ref, out_ref, probs_ref, share_ref,
             bias_ref, wt_ref):
        n = pl.program_id(0)

        @pl.when(n == 0)
        def _():
            _fill_bias(bucket_ref, table_ref, bias_ref)
            _fill_tril(ws_ref, wt_ref)

        zuv = z_ref[:, :1024]
        cdf, _ = _gelu_parts(zuv)
        guv = zuv * cdf
        for g in range(N_GROUP):
            vg = guv[:, 512 + 128 * g:512 + 128 * (g + 1)]
            vn = vg * _rms_scale(vg) * gain_ref[:, 128 * g:128 * (g + 1)]
            sv = _dot(wt_ref[g], vn.astype(BF16)) + bt_ref[:, g:g + 1]
            out_ref[:, 128 * g:128 * (g + 1)] = (guv[:, 128 * g:128 * (g + 1)] * sv).astype(BF16)

        k_same, k_swap, v_same, v_swap = _kv_layouts(kvp_ref[...], z_ref[:, 1536:1792])
        valid = _band_mask(n)
        lane = lax.broadcasted_iota(jnp.int32, (1, 128), 1)
        lane_half = lane // 64
        shares = jnp.zeros((CHUNK, 128), F32)
        for pair in range(N_HEAD // 2):
            qq = z_ref[:, 1024 + 128 * pair:1024 + 128 * (pair + 1)]
            acc = jnp.zeros((CHUNK, 128), F32)
            for pos in range(2):
                h = 2 * pair + pos
                _, _, same = _head_place(h)
                qm = jnp.where(lane_half == pos, qq, 0.0).astype(BF16)
                p, p_sink = _softmax_sink(qm, k_same if same else k_swap, bias_ref[h], sink_ref[h], valid)
                pb = p.astype(BF16)
                probs_ref[0, h] = pb
                shares = jnp.where(lane == h, p_sink, shares)
                vm = jnp.where(lane_half == pos, v_same if same else v_swap, jnp.zeros((), BF16))
                acc = acc + _dot(pb, vm)
            out_ref[:, 512 + 128 * pair:512 + 128 * (pair + 1)] = acc.astype(BF16)
        share_ref[...] = shares

    return _call(
        body, (z, z, v_gain, w_spatial, b_spatial_t, sinks, rel_table, bucket), grid=(N_BLOCK,), name="mixer_fwd",
        in_specs=[pl.BlockSpec((CHUNK, D_IN), lambda n: (n, 0)),
                  pl.BlockSpec((CHUNK, 256), lambda n: (jnp.maximum(n - 1, 0), 6)),
                  pl.BlockSpec((1, 512), lambda n: (0, 0)),
                  pl.BlockSpec((N_GROUP, CHUNK, CHUNK), lambda n: (0, 0, 0)),
                  pl.BlockSpec((CHUNK, N_GROUP), lambda n: (0, 0)),
                  pl.BlockSpec(memory_space=pltpu.SMEM),
                  pl.BlockSpec(memory_space=pltpu.SMEM),
                  pl.BlockSpec((CHUNK, 2 * CHUNK), lambda n: (0, 0))],
        out_specs=[pl.BlockSpec((CHUNK, D), lambda n: (n, 0)),
                   pl.BlockSpec((1, N_HEAD, CHUNK, 2 * CHUNK), lambda n: (n, 0, 0, 0)),
                   pl.BlockSpec((CHUNK, 128), lambda n: (n, 0))],
        out_shape=[jax.ShapeDtypeStruct((S, D), BF16), jax.ShapeDtypeStruct((N_BLOCK, N_HEAD, CHUNK, 2 * CHUNK), BF16),
                   jax.ShapeDtypeStruct((S, 128), F32)],
        scratch_shapes=[pltpu.VMEM((N_HEAD, CHUNK, 2 * CHUNK), F32), pltpu.VMEM((N_GROUP, CHUNK, CHUNK), BF16)],
        compiler_params=_params(32), exchange=exchange)


def _out_proj(x, mix, w_out, gain2, exchange=None):
    tm = 256

    def body(x_ref, mix_ref, w_ref, g_ref, h1_ref, hn_ref, hnt_ref):
        h1 = x_ref[...] + _dot(mix_ref[...], w_ref[...])
        h1_ref[...] = h1
        hn = h1 * _rms_scale(h1) * g_ref[...]
        hn_ref[...] = hn.astype(BF16)
        hnt_ref[...] = hn.T.astype(BF16)

    return _call(
        body, (x, mix, w_out, gain2), grid=(S // tm,), name="out_proj",
        in_specs=[pl.BlockSpec((tm, D), lambda i: (i, 0)), pl.BlockSpec((tm, D), lambda i: (i, 0)),
                  pl.BlockSpec((D, D), lambda i: (0, 0)), pl.BlockSpec((1, D), lambda i: (0, 0))],
        out_specs=[pl.BlockSpec((tm, D), lambda i: (i, 0)), pl.BlockSpec((tm, D), lambda i: (i, 0)),
                   pl.BlockSpec((D, tm), lambda i: (0, i))],
        out_shape=[jax.ShapeDtypeStruct((S, D), F32), jax.ShapeDtypeStruct((S, D), BF16),
                   jax.ShapeDtypeStruct((D, S), BF16)],
        compiler_params=_params(32), exchange=exchange)


def _ffn_up(hn2, w_ff1, exchange=None):
    tm = 512
    nj = D_FF // 1024

    def body(hn_ref, w1_ref, r_ref, a_ref, at_ref):
        r = jnp.maximum(_dot(hn_ref[...], w1_ref[0]), 0.0)
        r_ref[...] = r.astype(BF16)
        a = r * r
        a_ref[...] = a.astype(BF16)
        at_ref[...] = a.T.astype(BF16)

    return _call(
        body, (hn2, w_ff1), grid=(nj, S // tm), name="ffn_up",
        in_specs=[pl.BlockSpec((tm, D), lambda j, i: (i, 0)), pl.BlockSpec((1, D, 1024), lambda j, i: (j, 0, 0))],
        out_specs=[pl.BlockSpec((tm, 1024), lambda j, i: (i, j)), pl.BlockSpec((tm, 1024), lambda j, i: (i, j)),
                   pl.BlockSpec((1024, tm), lambda j, i: (j, i))],
        out_shape=[jax.ShapeDtypeStruct((S, D_FF), BF16), jax.ShapeDtypeStruct((S, D_FF), BF16),
                   jax.ShapeDtypeStruct((D_FF, S), BF16)],
        compiler_params=_params(40, 2), exchange=exchange)


def _ffn_down(h1, a, w_ff2, exchange=None):
    tm = 1024
    nj = D_FF // 1024

    def body(h1_ref, a_ref, w2_ref, h2_ref, acc_ref):
        j = pl.program_id(1)
        part = _dot(a_ref[...], w2_ref[0])

        @pl.when(j == 0)
        def _():
            acc_ref[...] = part

        @pl.when(j > 0)
        def _():
            acc_ref[...] += part

        @pl.when(j == nj - 1)
        def _():
            h2_ref[...] = h1_ref[...] + acc_ref[...]

    return _call(
        body, (h1, a, w_ff2), grid=(S // tm, nj), name="ffn_down",
        in_specs=[pl.BlockSpec((tm, D), lambda i, j: (i, 0)), pl.BlockSpec((tm, 1024), lambda i, j: (i, j)),
                  pl.BlockSpec((1, 1024, D), lambda i, j: (j, 0, 0))],
        out_specs=[pl.BlockSpec((tm, D), lambda i, j: (i, 0))],
        out_shape=[jax.ShapeDtypeStruct((S, D), F32)],
        scratch_shapes=[pltpu.VMEM((tm, D), F32)],
        compiler_params=_params(48, 2), exchange=exchange)


def _tail(h2, p, target, w_gate, w_proj, final_gain):
    tm = 256
    steps = S // tm

    def body(h2_ref, p_ref, t_ref, wg_ref, wp_ref, gf_ref, dh2_ref, dwg_ref, dwp_ref, dgf_ref, loss_ref, dh2b_ref,
             dwp_acc):
        i = pl.program_id(0)
        h2 = h2_ref[...]
        h2b = h2.astype(BF16)
        pb = p_ref[...].astype(BF16)
        gate = jax.nn.sigmoid(_dot(h2b, wg_ref[...]))
        pp = jnp.concatenate([_dot(pb, wp_ref[j]) for j in range(N_CHIP)], axis=1)
        h3 = h2 + gate * pp
        r3 = _rms_scale(h3)
        xhat = h3 * r3
        gf = gf_ref[...]
        err = xhat * gf - t_ref[...]
        dy = err * (1.0 / D)
        dh3 = _rms_bwd(dy * gf, xhat, r3)
        dgp = (dh3 * pp * gate * (1.0 - gate)).astype(BF16)
        dpp = (dh3 * gate).astype(BF16)
        dh2 = dh3 + _dot_nt(dgp, wg_ref[...])
        dh2_ref[...] = dh2
        dh2b_ref[...] = dh2.astype(BF16)
        dwg = _dot_tn(h2b, dgp)
        dwp = _dot_tn(pb, dpp)
        dgf = jnp.sum(dy * xhat, axis=0, keepdims=True)
        sq = jnp.sum(jnp.sum(err * err, axis=1, keepdims=True), axis=0, keepdims=True)

        @pl.when(i == 0)
        def _():
            dwg_ref[...] = dwg
            dwp_acc[...] = dwp
            dgf_ref[...] = dgf
            loss_ref[...] = jnp.broadcast_to(sq, (8, 128))

        @pl.when(i > 0)
        def _():
            dwg_ref[...] += dwg
            dwp_acc[...] += dwp
            dgf_ref[...] += dgf
            loss_ref[...] += jnp.broadcast_to(sq, (8, 128))

        @pl.when(i == steps - 1)
        def _():
            for j in range(N_CHIP):
                dwp_ref[j] = dwp_acc[:, 256 * j:256 * (j + 1)]

    return _call(
        body, (h2, p, target, w_gate, w_proj, final_gain), grid=(steps,), name="tail",
        in_specs=[pl.BlockSpec((tm, D), lambda i: (i, 0)), pl.BlockSpec((tm, PLE), lambda i: (i, 0)),
                  pl.BlockSpec((tm, D), lambda i: (i, 0)), pl.BlockSpec((D, D), lambda i: (0, 0)),
                  pl.BlockSpec((N_CHIP, PLE, 256), lambda i: (0, 0, 0)), pl.BlockSpec((1, D), lambda i: (0, 0))],
        out_specs=[pl.BlockSpec((tm, D), lambda i: (i, 0)), pl.BlockSpec((D, D), lambda i: (0, 0)),
                   pl.BlockSpec((N_CHIP, PLE, 256), lambda i: (0, 0, 0)), pl.BlockSpec((1, D), lambda i: (0, 0)),
                   pl.BlockSpec((8, 128), lambda i: (0, 0)), pl.BlockSpec((tm, D), lambda i: (i, 0))],
        out_shape=[jax.ShapeDtypeStruct((S, D), F32), jax.ShapeDtypeStruct((D, D), F32),
                   jax.ShapeDtypeStruct((N_CHIP, PLE, 256), F32), jax.ShapeDtypeStruct((1, D), F32),
                   jax.ShapeDtypeStruct((8, 128), F32), jax.ShapeDtypeStruct((S, D), BF16)],
        scratch_shapes=[pltpu.VMEM((PLE, D), F32)],
        compiler_params=_params(48))[0]


def _ffn_bwd_down(dh2b, r, a_t, w_ff2, exchange=None):
    tm = 1024
    nj = D_FF // 1024

    def body(dh2_ref, r_ref, at_ref, w2_ref, df_ref, dw2_ref):
        i = pl.program_id(1)
        dh2b = dh2_ref[...]
        da = _dot_nt(dh2b, w2_ref[0])
        df_ref[...] = (da * (2.0 * r_ref[...].astype(F32))).astype(BF16)
        dw2 = _dot(at_ref[...], dh2b)

        @pl.when(i == 0)
        def _():
            dw2_ref[0] = dw2

        @pl.when(i > 0)
        def _():
            dw2_ref[0] += dw2

    return _call(
        body, (dh2b, r, a_t, w_ff2), grid=(nj, S // tm), name="ffn_bwd_down",
        in_specs=[pl.BlockSpec((tm, D), lambda j, i: (i, 0)), pl.BlockSpec((tm, 1024), lambda j, i: (i, j)),
                  pl.BlockSpec((1024, tm), lambda j, i: (j, i)), pl.BlockSpec((1, 1024, D), lambda j, i: (j, 0, 0))],
        out_specs=[pl.BlockSpec((tm, 1024), lambda j, i: (i, j)), pl.BlockSpec((1, 1024, D), lambda j, i: (j, 0, 0))],
        out_shape=[jax.ShapeDtypeStruct((S, D_FF), BF16), jax.ShapeDtypeStruct((nj, 1024, D), F32)],
        compiler_params=_params(48, 2), exchange=exchange)


def _ffn_bwd_up(df, hn2_t, exchange=None):
    tm = 1024
    nj = D_FF // 1024

    def body(df_ref, hnt_ref, dw1_ref):
        i = pl.program_id(1)
        dw1 = _dot(hnt_ref[...], df_ref[...])

        @pl.when(i == 0)
        def _():
            dw1_ref[0] = dw1

        @pl.when(i > 0)
        def _():
            dw1_ref[0] += dw1

    return _call(
        body, (df, hn2_t), grid=(nj, S // tm), name="ffn_bwd_up",
        in_specs=[pl.BlockSpec((tm, 1024), lambda j, i: (i, j)), pl.BlockSpec((D, tm), lambda j, i: (0, i))],
        out_specs=[pl.BlockSpec((1, D, 1024), lambda j, i: (j, 0, 0))],
        out_shape=[jax.ShapeDtypeStruct((nj, D, 1024), F32)],
        compiler_params=_params(40, 2), exchange=exchange)


def _ffn_bwd_input(df, w_ff1, dh2, h1, gain2, mix, w_out, exchange=None):
    tm = 512
    nj = D_FF // 1024
    steps = S // tm

    def body(df_ref, w1_ref, dh2_ref, h1_ref, g_ref, mix_ref, wo_ref, dh1_ref, dmix_ref, dwo_ref, dg_ref, acc_ref):
        i = pl.program_id(0)
        j = pl.program_id(1)
        part = _dot_nt(df_ref[...], w1_ref[0])

        @pl.when(j == 0)
        def _():
            acc_ref[...] = part

        @pl.when(j > 0)
        def _():
            acc_ref[...] += part

        @pl.when(j == nj - 1)
        def _():
            dhn = acc_ref[...]
            h1 = h1_ref[...]
            r2 = _rms_scale(h1)
            xhat = h1 * r2
            dh1 = dh2_ref[...] + _rms_bwd(dhn * g_ref[...], xhat, r2)
            dh1_ref[...] = dh1
            dh1b = dh1.astype(BF16)
            dmix_ref[...] = _dot_nt(dh1b, wo_ref[...])
            dwo = _dot_tn(mix_ref[...], dh1b)
            dg = jnp.sum(dhn * xhat, axis=0, keepdims=True)

            @pl.when(i == 0)
            def _():
                dwo_ref[...] = dwo
                dg_ref[...] = dg

            @pl.when(i > 0)
            def _():
                dwo_ref[...] += dwo
                dg_ref[...] += dg

    return _call(
        body, (df, w_ff1, dh2, h1, gain2, mix, w_out), grid=(steps, nj), name="ffn_bwd_input",
        in_specs=[pl.BlockSpec((tm, 1024), lambda i, j: (i, j)), pl.BlockSpec((1, D, 1024), lambda i, j: (j, 0, 0)),
                  pl.BlockSpec((tm, D), lambda i, j: (i, 0)), pl.BlockSpec((tm, D), lambda i, j: (i, 0)),
                  pl.BlockSpec((1, D), lambda i, j: (0, 0)), pl.BlockSpec((tm, D), lambda i, j: (i, 0)),
                  pl.BlockSpec((D, D), lambda i, j: (0, 0))],
        out_specs=[pl.BlockSpec((tm, D), lambda i, j: (i, 0)), pl.BlockSpec((tm, D), lambda i, j: (i, 0)),
                   pl.BlockSpec((D, D), lambda i, j: (0, 0)), pl.BlockSpec((1, D), lambda i, j: (0, 0))],
        out_shape=[jax.ShapeDtypeStruct((S, D), F32), jax.ShapeDtypeStruct((S, D), F32),
                   jax.ShapeDtypeStruct((D, D), F32), jax.ShapeDtypeStruct((1, D), F32)],
        scratch_shapes=[pltpu.VMEM((tm, D), F32)],
        compiler_params=_params(56, 2), exchange=exchange)


IN_GROUP = 8


def _mixer_bwd(z, dmix, v_gain, w_spatial, b_spatial_t, probs, shares, bucket, hn1, exchange=None):
    def body(z_ref, kvp_ref, dm_ref, gain_ref, ws_ref, bt_ref, probs_ref, share_ref, bucket_ref, hn_ref,
             dz_ref, dws_ref, db_ref, dgain_ref, dsink_ref, drel_ref, dwin_ref,
             wt_ref, wtt_ref, dbias_ref, dsv_ref, carry_ref):
        n = pl.program_id(0)

        @pl.when(n == 0)
        def _():
            _fill_tril(ws_ref, wt_ref, wtt_ref)
            dwin_ref[...] = jnp.zeros_like(dwin_ref)
            dbias_ref[...] = jnp.zeros_like(dbias_ref)
            dsv_ref[...] = jnp.zeros_like(dsv_ref)
            dws_ref[...] = jnp.zeros_like(dws_ref)
            dgain_ref[...] = jnp.zeros_like(dgain_ref)
            dsink_ref[...] = jnp.zeros_like(dsink_ref)

        rows = pl.ds(pl.multiple_of(n * CHUNK, CHUNK), CHUNK)

        zuv = z_ref[:, :1024]
        cdf, t = _gelu_parts(zuv)
        guv = zuv * cdf
        dgelu = cdf + zuv * (0.5 * (1.0 - t * t)) * (GELU_C * (1.0 + 3.0 * 0.044715 * (zuv * zuv)))
        for g in range(N_GROUP):
            lo, hi = 128 * g, 128 * (g + 1)
            u = guv[:, lo:hi]
            vg = guv[:, 512 + lo:512 + hi]
            rr = _rms_scale(vg)
            vhat = vg * rr
            gain = gain_ref[:, lo:hi]
            vnb = (vhat * gain).astype(BF16)
            sv = _dot(wt_ref[g], vnb) + bt_ref[:, g:g + 1]
            da = dm_ref[:, lo:hi]
            dsv = da * u
            dsvb = dsv.astype(BF16)
            dsv_ref[g] += dsv
            dws_ref[g] += _dot_nt(dsvb, vnb)
            dvn = _dot(wtt_ref[g], dsvb)
            dgain_ref[:, lo:hi] += jnp.sum(dvn * vhat, axis=0, keepdims=True)
            dvg = _rms_bwd(dvn * gain, vhat, rr)
            dz_ref[rows, lo:hi] = (da * sv * dgelu[:, lo:hi]).astype(BF16)
            dz_ref[rows, 512 + lo:512 + hi] = (dvg * dgelu[:, 512 + lo:512 + hi]).astype(BF16)

        k_same, k_swap, v_same, v_swap = _kv_layouts(kvp_ref[...], z_ref[:, 1536:1792])
        lane_half = lax.broadcasted_iota(jnp.int32, (1, 128), 1) // 64
        zero = jnp.zeros((2 * CHUNK, 128), F32)
        dk_same, dk_swap, dv_same, dv_swap = zero, zero, zero, zero
        for pair in range(N_HEAD // 2):
            cols = slice(1024 + 128 * pair, 1024 + 128 * (pair + 1))
            qq = z_ref[:, cols]
            do_pair = dm_ref[:, 512 + 128 * pair:512 + 128 * (pair + 1)]
            dq = jnp.zeros((CHUNK, 128), F32)
            for pos in range(2):
                h = 2 * pair + pos
                _, _, same = _head_place(h)
                on_half = lane_half == pos
                qm = jnp.where(on_half, qq, 0.0).astype(BF16)
                k_use = k_same if same else k_swap
                v_use = v_same if same else v_swap
                pb = probs_ref[0, h]
                p = pb.astype(F32)
                p_sink = share_ref[:, h:h + 1]
                dom = jnp.where(on_half, do_pair, 0.0).astype(BF16)
                dp = _dot_nt(dom, v_use)
                dsum = jnp.sum(p * dp, axis=-1, keepdims=True)
                ds = p * (dp - dsum)
                dbias_ref[h] += ds
                dsink_ref[h:h + 1, :] += jnp.broadcast_to(jnp.sum(-p_sink * dsum, axis=0, keepdims=True), (1, 128))
                dsb = ds.astype(BF16)
                dq = dq + jnp.where(on_half, _dot(dsb, k_use), 0.0)
                dk_h = _dot_tn(dsb, qm)
                dv_h = _dot_tn(pb, dom)
                if same:
                    dk_same, dv_same = dk_same + dk_h, dv_same + dv_h
                else:
                    dk_swap, dv_swap = dk_swap + dk_h, dv_swap + dv_h
            dz_ref[rows, cols] = (dq * QK_SCALE).astype(BF16)
        dk = (dk_same + pltpu.roll(dk_swap, 64, axis=1)) * QK_SCALE
        dv = dv_same + pltpu.roll(dv_swap, 64, axis=1)
        dkv = jnp.concatenate([dk, dv], axis=1)

        @pl.when(n > 0)
        def _():
            prev_rows = pl.ds(pl.multiple_of((n - 1) * CHUNK, CHUNK), CHUNK)
            dz_ref[prev_rows, 1536:1792] = (carry_ref[...] + dkv[:CHUNK]).astype(BF16)

        carry_ref[...] = dkv[CHUNK:]

        @pl.when((n > 0) & (n % IN_GROUP == 0))
        def _():
            done = pl.ds(pl.multiple_of((n - IN_GROUP) * CHUNK, IN_GROUP * CHUNK), IN_GROUP * CHUNK)
            dwin_ref[...] += _dot_tn(dz_ref[done, :], hn_ref[...])

        @pl.when(n == N_BLOCK - 1)
        def _():
            dz_ref[rows, 1536:1792] = dkv[CHUNK:].astype(BF16)
            last = pl.ds((N_BLOCK - IN_GROUP) * CHUNK, IN_GROUP * CHUNK)
            dwin_ref[...] += _dot_tn(dz_ref[last, :], hn_ref[...])
            r = lax.broadcasted_iota(jnp.int32, (CHUNK, CHUNK), 0)
            c = lax.broadcasted_iota(jnp.int32, (CHUNK, CHUNK), 1)
            for g in range(N_GROUP):
                dws_ref[g] = jnp.where(c <= r, dws_ref[g], 0.0)
                db_ref[g] = jnp.sum(dsv_ref[g], axis=1, keepdims=True)
            bucket = bucket_ref[...]
            for h in range(N_HEAD):
                dbh = dbias_ref[h]
                per_bucket = [jnp.sum(jnp.where(bucket == b, dbh, 0.0), axis=0, keepdims=True) for b in range(N_BUCKET)]
                drel_ref[h] = jnp.sum(jnp.concatenate(per_bucket, axis=0), axis=1, keepdims=True)

    def hn_group(n):
        return jnp.where(n == N_BLOCK - 1, N_BLOCK // IN_GROUP - 1, jnp.maximum(n // IN_GROUP - 1, 0))

    return _call(
        body, (z, z, dmix, v_gain, w_spatial, b_spatial_t, probs, shares, bucket, hn1), grid=(N_BLOCK,),
        name="mixer_bwd",
        in_specs=[pl.BlockSpec((CHUNK, D_IN), lambda n: (n, 0)),
                  pl.BlockSpec((CHUNK, 256), lambda n: (jnp.maximum(n - 1, 0), 6)),
                  pl.BlockSpec((CHUNK, D), lambda n: (n, 0)),
                  pl.BlockSpec((1, 512), lambda n: (0, 0)),
                  pl.BlockSpec((N_GROUP, CHUNK, CHUNK), lambda n: (0, 0, 0)),
                  pl.BlockSpec((CHUNK, N_GROUP), lambda n: (0, 0)),
                  pl.BlockSpec((1, N_HEAD, CHUNK, 2 * CHUNK), lambda n: (n, 0, 0, 0)),
                  pl.BlockSpec((CHUNK, 128), lambda n: (n, 0)),
                  pl.BlockSpec((CHUNK, 2 * CHUNK), lambda n: (0, 0)),
                  pl.BlockSpec((IN_GROUP * CHUNK, D), lambda n: (hn_group(n), 0))],
        out_specs=[pl.BlockSpec((S, D_IN), lambda n: (0, 0)),
                   pl.BlockSpec((N_GROUP, CHUNK, CHUNK), lambda n: (0, 0, 0)),
                   pl.BlockSpec((N_GROUP, CHUNK, 1), lambda n: (0, 0, 0)),
                   pl.BlockSpec((1, 512), lambda n: (0, 0)),
                   pl.BlockSpec((N_HEAD, 128), lambda n: (0, 0)),
                   pl.BlockSpec((N_HEAD, N_BUCKET, 1), lambda n: (0, 0, 0)),
                   pl.BlockSpec((D_IN, D), lambda n: (0, 0))],
        out_shape=[jax.ShapeDtypeStruct((S, D_IN), BF16), jax.ShapeDtypeStruct((N_GROUP, CHUNK, CHUNK), F32),
                   jax.ShapeDtypeStruct((N_GROUP, CHUNK, 1), F32), jax.ShapeDtypeStruct((1, 512), F32),
                   jax.ShapeDtypeStruct((N_HEAD, 128), F32), jax.ShapeDtypeStruct((N_HEAD, N_BUCKET, 1), F32),
                   jax.ShapeDtypeStruct((D_IN, D), F32)],
        scratch_shapes=[pltpu.VMEM((N_GROUP, CHUNK, CHUNK), BF16),
                        pltpu.VMEM((N_GROUP, CHUNK, CHUNK), BF16), pltpu.VMEM((N_HEAD, CHUNK, 2 * CHUNK), F32),
                        pltpu.VMEM((N_GROUP, CHUNK, CHUNK), F32), pltpu.VMEM((CHUNK, 256), F32)],
        compiler_params=_params(56), exchange=exchange)


def _in_bwd_input(dz, w_in_t, x, dh1, gain1, exchange=None):
    tm = 512

    def body(dz_ref, w_ref, x_ref, dh1_ref, g_ref, dx_ref, dg_ref):
        i = pl.program_id(0)
        dhn = _dot(dz_ref[...], w_ref[...])
        xv = x_ref[...]
        r1 = _rms_scale(xv)
        xhat = xv * r1
        dx_ref[...] = dh1_ref[...] + _rms_bwd(dhn * g_ref[...], xhat, r1)
        dg = jnp.sum(dhn * xhat, axis=0, keepdims=True)

        @pl.when(i == 0)
        def _():
            dg_ref[...] = dg

        @pl.when(i > 0)
        def _():
            dg_ref[...] += dg

    return _call(
        body, (dz, w_in_t, x, dh1, gain1), grid=(S // tm,), name="in_bwd_input",
        in_specs=[pl.BlockSpec((tm, D_IN), lambda i: (i, 0)), pl.BlockSpec((D_IN, D), lambda i: (0, 0)),
                  pl.BlockSpec((tm, D), lambda i: (i, 0)), pl.BlockSpec((tm, D), lambda i: (i, 0)),
                  pl.BlockSpec((1, D), lambda i: (0, 0))],
        out_specs=[pl.BlockSpec((tm, D), lambda i: (i, 0)), pl.BlockSpec((1, D), lambda i: (0, 0))],
        out_shape=[jax.ShapeDtypeStruct((S, D), F32), jax.ShapeDtypeStruct((1, D), F32)],
        compiler_params=_params(48), exchange=exchange)


def _rel_bucket():
    a = jnp.arange(CHUNK)[:, None]
    j = jnp.arange(2 * CHUNK)[None, :]
    n = jnp.maximum(CHUNK + a - j, 0)
    max_exact = N_BUCKET // 2
    nf = jnp.maximum(n, 1).astype(jnp.float32)
    large = max_exact + (jnp.log(nf / max_exact) / math.log(CHUNK / max_exact) * (N_BUCKET - max_exact)).astype(jnp.int32)
    large = jnp.minimum(large, N_BUCKET - 1)
    return jnp.where(n < max_exact, n, large).astype(jnp.int32)


def _step(x, p, target, small, bufs, place):
    bucket = _rel_bucket()
    sinks = small["attn_sinks"].reshape(N_HEAD)
    b_t = jnp.transpose(small["b_spatial"].reshape(N_GROUP, CHUNK))
    ws = small["w_spatial"].reshape(N_GROUP, CHUNK, CHUNK)
    gain1, gain2 = small["norm1_gain"], small["norm2_gain"]
    v_gain = small["gmlp_v_gain"]
    final_gain = small["final_gain"].reshape(1, D)
    table = small["rel_bias_table"]
    bufs = dict(bufs)

    def gather(*names):
        return _RelayGather([bufs[n] for n in names])

    def took(names, got):
        bufs.update(zip(names, got))

    w_in_t = _whole(bufs["w_in"]).reshape(D_IN, D)
    (z, hn1), got = _in_proj(x, gain1, w_in_t, gather("w_out"))
    took(["w_out"], got)
    (mix, probs, shares), got = _mixer_fwd(z, v_gain, ws, b_t, sinks, table, bucket, gather("w_ff1"))
    took(["w_ff1"], got)
    w_out = _whole(bufs["w_out"]).reshape(D, D)
    (h1, hn2, hn2_t), _ = _out_proj(x, mix, w_out, gain2)
    w_ff1 = _whole(bufs["w_ff1"])
    (r, a, a_t), got = _ffn_up(hn2, w_ff1, gather("w_ff2"))
    took(["w_ff2"], got)
    w_ff2 = _whole(bufs["w_ff2"])
    (h2,), got = _ffn_down(h1, a, w_ff2, gather("w_ple_gate", "w_ple_proj"))
    took(["w_ple_gate", "w_ple_proj"], got)
    dh2, d_gate, d_proj, d_final, sq, dh2b = _tail(h2, p, target, _whole(bufs["w_ple_gate"]).reshape(D, D),
                                                   _whole(bufs["w_ple_proj"]), final_gain)

    def pair_sums(halves, from_sibling):
        sums, landing = zip(*[_pair_sum(g, o, place) for g, o in zip(halves, from_sibling)])
        return list(sums), list(landing)

    landed = {}
    halves = [_halves(d_gate.reshape(N_CHIP, 256, D)), _halves(d_proj)]
    (df, d_ff2), got = _ffn_bwd_down(dh2b, r, a_t, w_ff2, _SiblingExchange(halves))
    ex, halves = _ChipExchange(*pair_sums(halves, got)), [_halves(d_ff2)]
    (d_ff1,), got = _ffn_bwd_up(df, hn2_t, _Both(ex, _SiblingExchange(halves)))
    landed.update(zip(["w_ple_gate", "w_ple_proj"], got[:2]))
    ex, halves = _ChipExchange(*pair_sums(halves, got[2:])), [_halves(d_ff1)]
    (dh1, dmix, d_out, d_gain2), got = _ffn_bwd_input(df, w_ff1, dh2, h1, gain2, mix, w_out,
                                                      _Both(ex, _SiblingExchange(halves)))
    landed["w_ff2"] = got[0]
    ex, halves = _ChipExchange(*pair_sums(halves, got[1:])), [_halves(d_out.reshape(N_CHIP, 256, D))]
    (dz, d_ws, d_b, d_vgain, d_sink, d_rel, d_in_t), got = _mixer_bwd(z, dmix, v_gain, ws, b_t, probs, shares, bucket, hn1,
                                                                     _Both(ex, _SiblingExchange(halves)))
    landed["w_ff1"] = got[0]
    small_grads = {
        "gmlp_v_gain": d_vgain, "w_spatial": d_ws.reshape(1, N_GROUP, CHUNK, CHUNK),
        "b_spatial": d_b.reshape(1, N_GROUP, CHUNK), "attn_sinks": d_sink[:, 0].reshape(1, N_HEAD),
        "rel_bias_table": jnp.transpose(d_rel.reshape(N_HEAD, N_BUCKET)), "norm2_gain": d_gain2,
        "final_gain": d_final.reshape(D),
    }
    ex, halves = _ChipExchange(*pair_sums(halves, got[1:])), [_halves(d_in_t.reshape(N_CHIP, 448, D))]
    (dx, small_grads["norm1_gain"]), got = _in_bwd_input(dz, w_in_t, x, dh1, gain1, _Both(ex, _SiblingExchange(halves)))
    landed["w_out"] = got[0]
    return dx, landed, _ChipExchange(*pair_sums(halves, got[1:])), small_grads, sq


HBM_SPEC = pl.BlockSpec(memory_space=pltpu.HBM)
VMEM_SPEC = pl.BlockSpec(memory_space=pltpu.VMEM)


def _mesh_place():
    x, y, c = lax.axis_index("x"), lax.axis_index("y"), lax.axis_index("c")
    others = [(1 - x, y), (x, 1 - y), (1 - x, 1 - y)]
    return x, y, c, others


def _remote(src, dst, send_sem, recv_sem, device):
    return pltpu.make_async_remote_copy(src_ref=src, dst_ref=dst, send_sem=send_sem, recv_sem=recv_sem,
                                        device_id=device, device_id_type=MESH)


def _hbm_like(a, shape=None, dtype=None):
    return pltpu.HBM(a.shape if shape is None else shape, a.dtype if dtype is None else dtype)


def _gather_start(bufs, send_sems, recv_sems):
    x, y, c, others = _mesh_place()
    me = 2 * x + y
    for w, buf in enumerate(bufs):
        for k in range(3):
            mine = buf.at[me, c]
            _remote(mine, mine, send_sems.at[w, k], recv_sems.at[w, k], (*others[k], c)).start()


def _gather_finish(bufs, send_sems, recv_sems):
    x, y, c, others = _mesh_place()
    me = 2 * x + y
    sibling = (x, y, 1 - c)
    idx = [2 * ox + oy for ox, oy in others]
    chips = range(3)
    for w, buf in enumerate(bufs):
        for k in chips:
            landed = buf.at[idx[k], c]
            _remote(landed, landed, send_sems.at[w, k], recv_sems.at[w, k], sibling).wait_recv()
            _remote(landed, landed, send_sems.at[w, 3 + k], recv_sems.at[w, 3 + k], sibling).start()
    for w, buf in enumerate(bufs):
        for k in chips:
            landed = buf.at[idx[k], 1 - c]
            _remote(landed, landed, send_sems.at[w, 3 + k], recv_sems.at[w, 3 + k], sibling).wait_recv()
    for w, buf in enumerate(bufs):
        for k in chips:
            mine, passed = buf.at[me, c], buf.at[idx[k], c]
            _remote(mine, mine, send_sems.at[w, k], recv_sems.at[w, k], sibling).wait_send()
            _remote(passed, passed, send_sems.at[w, 3 + k], recv_sems.at[w, 3 + k], sibling).wait_send()


def _gather_sems(n):
    return [pltpu.SemaphoreType.DMA((n, 6)), pltpu.SemaphoreType.DMA((n, 6))]


def _sibling_copies(grads, landing, send_sems, recv_sems):
    x, y, c, _ = _mesh_place()
    return [_remote(grads[w].at[j, 1 - c], landing[w].at[j], send_sems.at[w, j], recv_sems.at[w, j], (x, y, 1 - c))
            for w in range(len(grads)) for j in range(N_CHIP)]


def _sibling_exchange_start(grads, landing, send_sems, recv_sems):
    for cp in _sibling_copies(grads, landing, send_sems, recv_sems):
        cp.start()


def _sibling_exchange_finish(grads, landing, send_sems, recv_sems):
    copies = _sibling_copies(grads, landing, send_sems, recv_sems)
    for cp in copies:
        cp.wait_recv()
    for cp in copies:
        cp.wait_send()


def _sibling_exchange_sems(n):
    return [pltpu.SemaphoreType.DMA((n, N_CHIP)), pltpu.SemaphoreType.DMA((n, N_CHIP))]


def _sibling_exchange(grads):
    n = len(grads)

    def body(*refs):
        ins, outs = refs[:n], refs[n:2 * n]
        _sibling_exchange_start(ins, outs, *refs[2 * n:])
        _sibling_exchange_finish(ins, outs, *refs[2 * n:])

    return pl.pallas_call(
        body, name="sibling_exchange",
        in_specs=[HBM_SPEC] * n, out_specs=[HBM_SPEC] * n,
        out_shape=[_hbm_like(g, (N_CHIP,) + g.shape[2:]) for g in grads],
        scratch_shapes=_sibling_exchange_sems(n),
    )(*[_in_hbm(g) for g in grads])


def _chip_exchange_start(sums, landing, send_sems, recv_sems):
    x, y, c, others = _mesh_place()
    me = 2 * x + y
    for w in range(len(sums)):
        for k, (ox, oy) in enumerate(others):
            _remote(sums[w].at[2 * ox + oy], landing[w].at[me], send_sems.at[w, k], recv_sems.at[w, k],
                    (ox, oy, c)).start()


def _chip_exchange_finish(sums, landing, send_sems, recv_sems):
    x, y, c, others = _mesh_place()
    for w in range(len(sums)):
        for k, (ox, oy) in enumerate(others):
            piece = landing[w].at[2 * ox + oy]
            _remote(piece, piece, send_sems.at[w, k], recv_sems.at[w, k], (x, y, c)).wait_recv()
    for w in range(len(sums)):
        for k, (ox, oy) in enumerate(others):
            piece = sums[w].at[2 * ox + oy]
            _remote(piece, piece, send_sems.at[w, k], recv_sems.at[w, k], (x, y, c)).wait_send()


def _chip_exchange_sems(n):
    return [pltpu.SemaphoreType.DMA((n, 3)), pltpu.SemaphoreType.DMA((n, 3))]


def _sibling_allgather(bufs, also):
    n = len(bufs)
    k_in, k_out = len(also.operands), also.n_out

    def body(*refs):
        ex_ins, refs = refs[n:n + k_in], refs[n + k_in:]
        outs, refs = refs[:n], refs[n:]
        ex_outs, refs = refs[:k_out], refs[k_out:]
        send_sems, recv_sems, ex_sems = refs[0], refs[1], refs[2:]
        x, y, c, _ = _mesh_place()
        sibling = (x, y, 1 - c)
        also.start(ex_ins, ex_outs, ex_sems)
        sends = [_remote(outs[w].at[c], outs[w].at[c], send_sems.at[w], recv_sems.at[w], sibling) for w in range(n)]
        for cp in sends:
            cp.start()
        for w in range(n):
            landed = outs[w].at[1 - c]
            _remote(landed, landed, send_sems.at[w], recv_sems.at[w], sibling).wait_recv()
        for cp in sends:
            cp.wait_send()
        also.finish(ex_ins, ex_outs, ex_sems)

    res = pl.pallas_call(
        body, name="sibling_allgather",
        in_specs=[HBM_SPEC] * (n + k_in), out_specs=[HBM_SPEC] * (n + k_out),
        out_shape=[_hbm_like(b) for b in bufs] + also.out_shape,
        input_output_aliases={**{w: w for w in range(n)}, **{n + i: n + o for i, o in also.aliases.items()}},
        scratch_shapes=[pltpu.SemaphoreType.DMA((n,)), pltpu.SemaphoreType.DMA((n,))] + also.sems,
    )(*bufs, *[_in_hbm(o) for o in also.operands])
    return list(res[:n]), list(res[n:])


def _pair_sum(grad, other, place):
    _, _, h, cols = grad.shape
    tr = _row_tile(h)

    def body(place_ref, g_ref, o_ref, sums_ref, own_ref):
        s = (g_ref[0, 0] + o_ref[0]).astype(BF16)
        sums_ref[0] = s

        @pl.when(pl.program_id(1) == place_ref[0])
        def _():
            own_ref[0] = s

    return pl.pallas_call(
        body, name="pair_sum",
        grid_spec=pltpu.PrefetchScalarGridSpec(
            num_scalar_prefetch=1, grid=(h // tr, N_CHIP),
            in_specs=[pl.BlockSpec((1, 1, tr, cols), lambda r, j, place_ref: (j, place_ref[1], r, 0)),
                      pl.BlockSpec((1, tr, cols), lambda r, j, place_ref: (j, r, 0))],
            out_specs=[pl.BlockSpec((1, tr, cols), lambda r, j, place_ref: (j, r, 0)),
                       pl.BlockSpec((1, tr, cols), lambda r, j, place_ref: (place_ref[0], r, 0))]),
        out_shape=[pltpu.HBM((N_CHIP, h, cols), BF16)] * 2,
        compiler_params=_params(32, 2),
    )(place, _in_hbm(grad), _in_hbm(other))


def _chip_sum(parts, place):
    _, h, cols = parts.shape
    tr = _row_tile(h)

    def body(place_ref, p_ref, out_ref):
        out_ref[0] = ((p_ref[0].astype(F32) + p_ref[1].astype(F32)) + p_ref[2].astype(F32)) + p_ref[3].astype(F32)

    return pl.pallas_call(
        body, name="chip_sum",
        grid_spec=pltpu.PrefetchScalarGridSpec(
            num_scalar_prefetch=1, grid=(h // tr,),
            in_specs=[pl.BlockSpec((N_CHIP, tr, cols), lambda r, place_ref: (0, r, 0))],
            out_specs=pl.BlockSpec((1, tr, cols), lambda r, place_ref: (place_ref[1], r, 0))),
        out_shape=pltpu.HBM((2, h, cols), F32),
        compiler_params=_params(32),
    )(place, _in_hbm(parts))


def _adamw_math(w, g, m, v):
    m = ADAM_B1 * m + (1.0 - ADAM_B1) * g
    v = ADAM_B2 * v + (1.0 - ADAM_B2) * (g * g)
    m_hat = m / (1.0 - ADAM_B1 ** ADAM_STEP)
    v_hat = v / (1.0 - ADAM_B2 ** ADAM_STEP)
    delta = -ADAM_LR * (m_hat / (jnp.sqrt(v_hat) + ADAM_EPS) + ADAM_WD * w)
    return delta, m, v


def _adamw(w, g, m, v, exchange=None):
    rows, cols = w.shape
    tr = _row_tile(rows)

    def body(w_ref, g_ref, m_ref, v_ref, d_ref, nm_ref, nv_ref, g_out_ref):
        g = g_ref[...]
        d_ref[...], nm_ref[...], nv_ref[...] = _adamw_math(w_ref[...], g, m_ref[...], v_ref[...])
        g_out_ref[...] = g

    spec = pl.BlockSpec((tr, cols), lambda r: (r, 0))
    return _call(
        body, (w, g, m, v), grid=(rows // tr,), name="adamw",
        in_specs=[spec] * 4, out_specs=[spec] * 4,
        out_shape=[jax.ShapeDtypeStruct((rows, cols), F32)] * 4,
        compiler_params=_params(48), exchange=exchange)


SMALL_NAMES = ("norm1_gain", "gmlp_v_gain", "w_spatial", "b_spatial", "attn_sinks", "rel_bias_table", "norm2_gain",
               "final_gain")
PACK_TILE = 8 * 128


def _pack_small(arrays):
    parts = []
    for a in arrays:
        flat = a.reshape(-1)
        rows = -(-flat.shape[0] // PACK_TILE) * 8
        parts.append(jnp.pad(flat, (0, rows * 128 - flat.shape[0])).reshape(rows, 128))
    return jnp.concatenate(parts, axis=0)


def _unpack_small(packed, like):
    out, row = [], 0
    for a in like:
        size = math.prod(a.shape)
        rows = -(-size // PACK_TILE) * 8
        out.append(packed[row:row + rows].reshape(-1)[:size].reshape(a.shape))
        row += rows
    return out


def _small_update(gathered, w, m, v):
    rows = gathered.shape[1]

    def body(g_ref, w_ref, m_ref, v_ref, tot_ref, d_ref, nm_ref, nv_ref):
        total = g_ref[0].astype(F32)
        for dev in range(1, 8):
            total = total + g_ref[dev].astype(F32)
        tot_ref[...] = total
        d_ref[...], nm_ref[...], nv_ref[...] = _adamw_math(w_ref[...], total, m_ref[...], v_ref[...])

    return pl.pallas_call(
        body, name="small_update",
        in_specs=[VMEM_SPEC] * 4, out_specs=[VMEM_SPEC] * 4,
        out_shape=[jax.ShapeDtypeStruct((rows, 128), F32)] * 4,
        compiler_params=pltpu.CompilerParams(vmem_limit_bytes=24 * MIB),
    )(gathered, w, m, v)


def _halves(a):
    return a.reshape(a.shape[:-2] + (2, a.shape[-2] // 2, a.shape[-1]))


def _whole(a):
    return a.reshape(a.shape[:-3] + (2 * a.shape[-2], a.shape[-1]))


def kernel(x, p, norm1_gain, w_in, gmlp_v_gain, w_spatial, b_spatial, attn_sinks, rel_bias_table, w_out, norm2_gain, w_ff1, w_ff2, w_ple_proj, w_ple_gate, final_gain, loss_target, m_norm1_gain, m_w_in, m_gmlp_v_gain, m_w_spatial, m_b_spatial, m_attn_sinks, m_rel_bias_table, m_w_out, m_norm2_gain, m_w_ff1, m_w_ff2, m_w_ple_proj, m_w_ple_gate, m_final_gain, v_norm1_gain, v_w_in, v_gmlp_v_gain, v_w_spatial, v_b_spatial, v_attn_sinks, v_rel_bias_table, v_w_out, v_norm2_gain, v_w_ff1, v_w_ff2, v_w_ple_proj, v_w_ple_gate, v_final_gain):
    given = dict(locals())
    small = {n: given[n] for n in SMALL_NAMES}
    chip = 2 * lax.axis_index("x") + lax.axis_index("y")
    place = jnp.stack([chip, lax.axis_index("c")]).astype(jnp.int32)

    big_names = ("w_in", "w_out", "w_ff1", "w_ff2", "w_ple_proj", "w_ple_gate")
    shards = {n: given[n][0] for n in big_names}
    travel = dict(shards, w_in=jnp.transpose(shards["w_in"]))
    rest = [n for n in big_names if n != "w_in"]
    cast, gathered = _cast_shards_beside_gather([travel[n] for n in rest], place[:1],
                                                [_cast_shard(travel["w_in"], place[:1])])
    bufs = dict(zip(rest + ["w_in"], cast + gathered))
    dx, landed, exchange_in, small_grads, sq = _step(x[0], p[0, 0], loss_target[0], small, bufs, place)

    out_grad, out_delta, out_m, out_v = {}, {}, {}, {}

    def update(n, g, exchange=None):
        to = jnp.transpose if n == "w_in" else (lambda a: a)
        (delta, new_m, new_v, g_out), got = _adamw(to(shards[n]), g, to(given["m_" + n][0]), to(given["v_" + n][0]),
                                                   exchange)
        out_grad[n], out_delta[n], out_m[n], out_v[n] = [to(a)[None] for a in (g_out, delta, new_m, new_v)]
        return got

    spare = jnp.zeros((8, 128), F32)
    small_packed = _pack_small([small_grads[n] for n in SMALL_NAMES] + [spare]).astype(BF16)
    early = [n for n in big_names if n != "w_in"]
    reduced, (small_gathered, sq_gathered, landed_in) = _sibling_allgather(
        [_chip_sum(landed[n], place) for n in early], _Both(_Both(_GatherAll(small_packed), _GatherAll(sq)), exchange_in))
    for n, r in zip(early, reduced):
        update(n, _whole(r))
    (reduced_in,), _ = _sibling_allgather([_chip_sum(landed_in, place)], _Nothing())
    update("w_in", _whole(reduced_in))

    like = [given[n] for n in SMALL_NAMES] + [spare]
    packed = _small_update(small_gathered, *[_pack_small([given[pre + n] for n in SMALL_NAMES] + [spare])
                                             for pre in ("", "m_", "v_")])
    for res, out in zip(packed, (out_grad, out_delta, out_m, out_v)):
        out.update(zip(SMALL_NAMES, _unpack_small(res, like)))
    loss = 0.5 * jnp.sum(sq_gathered[:, 0, 0]) / D

    order = ("norm1_gain", "w_in", "gmlp_v_gain", "w_spatial", "b_spatial", "attn_sinks", "rel_bias_table", "w_out",
             "norm2_gain", "w_ff1", "w_ff2", "w_ple_proj", "w_ple_gate", "final_gain")
    return (loss, dx[None], *[out_grad[n] for n in order], *[out_delta[n] for n in order],
            *[out_m[n] for n in order], *[out_v[n] for n in order])
```

```python
import functools
import math

import jax
import jax.numpy as jnp
from jax import lax
from jax.experimental import pallas as pl
from jax.experimental.pallas import tpu as pltpu

S = 2048
D = 1024
D_IN = 1792
D_FF = 4096
PLE = 256
N_CHIP = 4
N_GROUP = 4
CHUNK = 128
N_HEAD = 8
N_BLOCK = S // CHUNK
N_BUCKET = 32
EPS = 1e-6
NEG_INF = -1e30
QK_SCALE = 0.125
GELU_C = math.sqrt(2.0 / math.pi)

ADAM_LR = 0.001
ADAM_B1 = 0.9
ADAM_B2 = 0.999
ADAM_EPS = 1e-08
ADAM_WD = 0.01
ADAM_STEP = 10

F32 = jnp.float32
BF16 = jnp.bfloat16
MIB = 1024 * 1024
MESH = pl.DeviceIdType.MESH

NT = (((1,), (1,)), ((), ()))
TN = (((0,), (0,)), ((), ()))


def _dot(a, b):
    return jnp.dot(a, b, preferred_element_type=F32)


def _dot_nt(a, b):
    return lax.dot_general(a, b, NT, preferred_element_type=F32)


def _dot_tn(a, b):
    return lax.dot_general(a, b, TN, preferred_element_type=F32)


def _params(vmem_mib, n_axes=1):
    return pltpu.CompilerParams(dimension_semantics=("arbitrary",) * n_axes, vmem_limit_bytes=vmem_mib * MIB)


def _rms_scale(v):
    return lax.rsqrt(jnp.mean(v * v, axis=-1, keepdims=True) + EPS)


def _rms_bwd(dy_gain, xhat, r):
    return r * (dy_gain - xhat * jnp.mean(dy_gain * xhat, axis=-1, keepdims=True))


class _Gather:
    def __init__(self, bufs):
        self.operands = list(bufs)
        self.n_out = len(self.operands)
        self.out_shape = [_hbm_like(b) for b in bufs]
        self.aliases = {w: w for w in range(self.n_out)}
        self.sems = _gather_sems(self.n_out)

    def start(self, ins, outs, sems):
        _gather_start(outs, *sems)

    def finish(self, ins, outs, sems):
        _gather_finish(outs, *sems)


class _RelayGather(_Gather):
    TOP, BOTTOM = 6, 7
    DIAGONAL_PASSED = 5
    MIDDLE_AT, LATE_AT = (5, 8), (7, 8)

    def __init__(self, bufs):
        super().__init__(bufs)
        self.sems = [pltpu.SemaphoreType.DMA((self.n_out, 8)), pltpu.SemaphoreType.DMA((self.n_out, 8))]

    def _copies(self, bufs, send_sems, recv_sems):
        x, y, c, others = _mesh_place()
        me = 2 * x + y
        idx = [2 * ox + oy for ox, oy in others]
        sibling = (x, y, 1 - c)
        direct, passed, relayed = [], [], []
        for w, buf in enumerate(bufs):
            rows = buf.shape[2] // 2
            upper, lower = pl.ds(0, rows), pl.ds(rows, rows)
            for k in (0, 1):
                mine = buf.at[me, c]
                direct.append((_remote(mine, mine, send_sems.at[w, k], recv_sems.at[w, k], (*others[k], c)),
                               buf.at[idx[k], c], w, k))
            for k in (0, 1, 2):
                here = buf.at[idx[k], c]
                passed.append((_remote(here, here, send_sems.at[w, 3 + k], recv_sems.at[w, 3 + k], sibling),
                               buf.at[idx[k], 1 - c], w, 3 + k))
            from_x, from_y = buf.at[idx[0], c, upper], buf.at[idx[1], c, lower]
            relayed.append((_remote(from_x, from_x, send_sems.at[w, self.TOP], recv_sems.at[w, self.TOP],
                                    (*others[1], c)), buf.at[idx[2], c, upper], w, self.TOP))
            relayed.append((_remote(from_y, from_y, send_sems.at[w, self.BOTTOM], recv_sems.at[w, self.BOTTOM],
                                    (*others[0], c)), buf.at[idx[2], c, lower], w, self.BOTTOM))
        return direct, passed, relayed

    @staticmethod
    def _landed(piece, send_sems, recv_sems, w, col):
        x, y, c, _ = _mesh_place()
        _remote(piece, piece, send_sems.at[w, col], recv_sems.at[w, col], (x, y, c)).wait_recv()

    def start(self, ins, outs, sems):
        for cp, _, _, _ in self._copies(outs, *sems)[0]:
            cp.start()

    def middle(self, ins, outs, sems):
        direct, passed, relayed = self._copies(outs, *sems)
        for _, piece, w, col in direct:
            self._landed(piece, *sems, w, col)
        for cp, _, _, col in passed:
            if col != self.DIAGONAL_PASSED:
                cp.start()
        for cp, _, _, _ in relayed:
            cp.start()

    def late(self, ins, outs, sems):
        direct, passed, relayed = self._copies(outs, *sems)
        for _, piece, w, col in relayed:
            self._landed(piece, *sems, w, col)
        for cp, _, _, col in passed:
            if col == self.DIAGONAL_PASSED:
                cp.start()

    def finish(self, ins, outs, sems):
        direct, passed, relayed = self._copies(outs, *sems)
        for _, piece, w, col in passed:
            self._landed(piece, *sems, w, col)
        for cp, _, _, _ in direct + passed + relayed:
            cp.wait_send()


class _ChipExchange:
    def __init__(self, sums, landing):
        self.n_out = len(landing)
        self.operands = list(sums) + list(landing)
        self.out_shape = [_hbm_like(b) for b in landing]
        self.aliases = {self.n_out + w: w for w in range(self.n_out)}
        self.sems = _chip_exchange_sems(self.n_out)

    def start(self, ins, outs, sems):
        _chip_exchange_start(ins[:self.n_out], outs, *sems)

    def finish(self, ins, outs, sems):
        _chip_exchange_finish(ins[:self.n_out], outs, *sems)


class _GatherAll:
    def __init__(self, packed):
        self.operands = [packed]
        self.n_out = 1
        self.out_shape = [_hbm_like(packed, (8,) + packed.shape)]
        self.aliases = {}
        self.sems = [pltpu.SemaphoreType.DMA((8,)), pltpu.SemaphoreType.DMA((8,))]

    def _copies(self, ins, outs, sems):
        x, y, c, _ = _mesh_place()
        me = 4 * x + 2 * y + c
        send_sems, recv_sems = sems
        copies = []
        for k in range(1, 8):
            peer = (1 - x if k // 4 else x, 1 - y if (k // 2) % 2 else y, 1 - c if k % 2 else c)
            src = 4 * peer[0] + 2 * peer[1] + peer[2]
            copies.append((_remote(ins[0], outs[0].at[me], send_sems.at[k], recv_sems.at[k], peer), outs[0].at[src]))
        own = pltpu.make_async_copy(ins[0], outs[0].at[me], send_sems.at[0])
        return own, copies

    def start(self, ins, outs, sems):
        own, copies = self._copies(ins, outs, sems)
        own.start()
        for cp, _ in copies:
            cp.start()

    def finish(self, ins, outs, sems):
        own, copies = self._copies(ins, outs, sems)
        x, y, c, _ = _mesh_place()
        for k, (cp, landed) in enumerate(copies):
            _remote(landed, landed, sems[0].at[k + 1], sems[1].at[k + 1], (x, y, c)).wait_recv()
        for cp, _ in copies:
            cp.wait_send()
        own.wait()


class _Nothing:
    operands, n_out, out_shape, aliases, sems = [], 0, [], {}, []

    def start(self, ins, outs, sems):
        pass

    def finish(self, ins, outs, sems):
        pass


class _Both:
    def __init__(self, a, b):
        self.a, self.b = a, b
        self.operands = a.operands + b.operands
        self.n_out = a.n_out + b.n_out
        self.out_shape = a.out_shape + b.out_shape
        self.aliases = dict(a.aliases)
        self.aliases.update({len(a.operands) + i: a.n_out + o for i, o in b.aliases.items()})
        self.sems = a.sems + b.sems

    def _split(self, ins, outs, sems):
        ka, na, sa = len(self.a.operands), self.a.n_out, len(self.a.sems)
        return (ins[:ka], outs[:na], sems[:sa]), (ins[ka:], outs[na:], sems[sa:])

    def start(self, ins, outs, sems):
        for ex, args in zip((self.a, self.b), self._split(ins, outs, sems)):
            ex.start(*args)

    def finish(self, ins, outs, sems):
        for ex, args in zip((self.a, self.b), self._split(ins, outs, sems)):
            ex.finish(*args)


class _SiblingExchange:
    def __init__(self, grads):
        self.operands = list(grads)
        self.n_out = len(self.operands)
        self.out_shape = [_hbm_like(g, (N_CHIP,) + g.shape[2:]) for g in grads]
        self.aliases = {}
        self.sems = _sibling_exchange_sems(self.n_out)

    def start(self, ins, outs, sems):
        _sibling_exchange_start(ins, outs, *sems)

    def finish(self, ins, outs, sems):
        _sibling_exchange_finish(ins, outs, *sems)


def _call(body, operands, *, grid, in_specs, out_specs, out_shape, name, compiler_params, scratch_shapes=(),
          exchange=None):
    operands = [o if getattr(spec, "memory_space", None) == pltpu.SMEM else _in_hbm(o)
                for o, spec in zip(operands, in_specs)]
    out_shape = [pltpu.HBM(s.shape, s.dtype) for s in out_shape]
    if exchange is None:
        res = pl.pallas_call(body, grid=grid, in_specs=in_specs, out_specs=out_specs, out_shape=out_shape, name=name,
                             scratch_shapes=list(scratch_shapes), compiler_params=compiler_params)(*operands)
        return list(res), []
    n_in, n_out, n_scr = len(in_specs), len(out_specs), len(scratch_shapes)
    k_in, k_out = len(exchange.operands), exchange.n_out

    def fused(*refs):
        ins, refs = refs[:n_in], refs[n_in:]
        ex_ins, refs = refs[:k_in], refs[k_in:]
        outs, refs = refs[:n_out], refs[n_out:]
        ex_outs, refs = refs[:k_out], refs[k_out:]
        scratch, sems = refs[:n_scr], refs[n_scr:]
        ids = [pl.program_id(a) for a in range(len(grid))]
        first = functools.reduce(jnp.logical_and, [i == 0 for i in ids])
        last = functools.reduce(jnp.logical_and, [i == g - 1 for i, g in zip(ids, grid)])

        @pl.when(first)
        def _():
            exchange.start(ex_ins, ex_outs, sems)

        def at_step(numerator, denominator):
            at = (numerator * math.prod(grid)) // denominator
            place = [(at // math.prod(grid[a + 1:])) % grid[a] for a in range(len(grid))]
            return functools.reduce(jnp.logical_and, [i == p for i, p in zip(ids, place)])

        if hasattr(exchange, "middle"):
            @pl.when(at_step(*exchange.MIDDLE_AT))
            def _():
                exchange.middle(ex_ins, ex_outs, sems)

            @pl.when(at_step(*exchange.LATE_AT))
            def _():
                exchange.late(ex_ins, ex_outs, sems)

        body(*ins, *outs, *scratch)

        @pl.when(last)
        def _():
            exchange.finish(ex_ins, ex_outs, sems)

    res = pl.pallas_call(
        fused, grid=grid, name=name,
        in_specs=list(in_specs) + [HBM_SPEC] * k_in, out_specs=list(out_specs) + [HBM_SPEC] * k_out,
        out_shape=list(out_shape) + exchange.out_shape,
        input_output_aliases={n_in + i: n_out + o for i, o in exchange.aliases.items()},
        scratch_shapes=list(scratch_shapes) + exchange.sems, compiler_params=compiler_params,
    )(*operands, *[_in_hbm(o) for o in exchange.operands])
    return list(res[:n_out]), list(res[n_out:])


def _in_hbm(a):
    return pltpu.with_memory_space_constraint(a, pltpu.HBM)


def _row_tile(h):
    return max(t for t in range(16, 513, 16) if h % t == 0)


def _cast_shard(a, chip):
    rows, cols = a.shape
    h = rows // 2
    tr = _row_tile(h)

    def body(chip_ref, a_ref, o_ref):
        o_ref[0, 0] = a_ref[0].astype(BF16)

    return pl.pallas_call(
        body, name="cast_shard",
        grid_spec=pltpu.PrefetchScalarGridSpec(
            num_scalar_prefetch=1, grid=(2, h // tr),
            in_specs=[pl.BlockSpec((1, tr, cols), lambda s, r, chip_ref: (s, r, 0))],
            out_specs=pl.BlockSpec((1, 1, tr, cols), lambda s, r, chip_ref: (chip_ref[0], s, r, 0))),
        out_shape=pltpu.HBM((N_CHIP, 2, h, cols), BF16),
        compiler_params=_params(16, 2),
    )(chip, _in_hbm(a.reshape(2, h, cols)))


def _cast_shards_beside_gather(arrays, chip, gathered):
    n, k = len(arrays), len(gathered)
    shapes = [(a.shape[0] // 2, a.shape[1]) for a in arrays]

    def body(chip_ref, *refs):
        ins, refs = refs[:n], refs[n + k:]
        outs, refs = refs[:n], refs[n:]
        bufs, sems = refs[:k], refs[k:]
        half = pl.program_id(0)

        @pl.when(half == 0)
        def _():
            _gather_start(bufs, *sems)

        for a_ref, o_ref in zip(ins, outs):
            o_ref[0, 0] = a_ref[0].astype(BF16)

        @pl.when(half == 1)
        def _():
            _gather_finish(bufs, *sems)

    res = pl.pallas_call(
        body, name="cast_shards",
        grid_spec=pltpu.PrefetchScalarGridSpec(
            num_scalar_prefetch=1, grid=(2,),
            in_specs=[pl.BlockSpec((1, h, c), lambda s, chip_ref: (s, 0, 0)) for h, c in shapes] + [HBM_SPEC] * k,
            out_specs=[pl.BlockSpec((1, 1, h, c), lambda s, chip_ref: (chip_ref[0], s, 0, 0)) for h, c in shapes]
            + [HBM_SPEC] * k,
            scratch_shapes=_gather_sems(k)),
        out_shape=[pltpu.HBM((N_CHIP, 2, h, c), BF16) for h, c in shapes] + [_hbm_like(b) for b in gathered],
        input_output_aliases={1 + n + i: n + i for i in range(k)},
        compiler_params=_params(32),
    )(chip, *[_in_hbm(a.reshape(2, h, c)) for a, (h, c) in zip(arrays, shapes)], *gathered)
    return list(res[:n]), list(res[n:])


def _in_proj(x, gain1, w_in_t, exchange=None):
    tm = 256

    def body(x_ref, g_ref, w_ref, z_ref, hn_ref):
        xv = x_ref[...]
        hn = (xv * _rms_scale(xv) * g_ref[...]).astype(BF16)
        hn_ref[...] = hn
        z_ref[...] = _dot_nt(hn, w_ref[...])

    return _call(
        body, (x, gain1, w_in_t), grid=(S // tm,), name="in_proj",
        in_specs=[pl.BlockSpec((tm, D), lambda i: (i, 0)), pl.BlockSpec((1, D), lambda i: (0, 0)),
                  pl.BlockSpec((D_IN, D), lambda i: (0, 0))],
        out_specs=[pl.BlockSpec((tm, D_IN), lambda i: (i, 0)), pl.BlockSpec((tm, D), lambda i: (i, 0))],
        out_shape=[jax.ShapeDtypeStruct((S, D_IN), F32), jax.ShapeDtypeStruct((S, D), BF16)],
        compiler_params=_params(40), exchange=exchange)


def _gelu_parts(v):
    t = jnp.tanh(GELU_C * (v + 0.044715 * (v * v * v)))
    cdf = 0.5 * (1.0 + t)
    return cdf, t


def _band_mask(n):
    a = lax.broadcasted_iota(jnp.int32, (CHUNK, 2 * CHUNK), 0)
    j = lax.broadcasted_iota(jnp.int32, (CHUNK, 2 * CHUNK), 1)
    dist = CHUNK + a - j
    valid = (dist >= 0) & (dist < CHUNK)
    return valid & ((n > 0) | (j >= CHUNK))


def _fill_bias(bucket_ref, table_ref, bias_ref):
    bucket = bucket_ref[...]
    for h in range(N_HEAD):
        acc = jnp.zeros((CHUNK, 2 * CHUNK), F32)
        for b in range(N_BUCKET):
            acc = jnp.where(bucket == b, table_ref[b, h], acc)
        bias_ref[h] = acc


def _fill_tril(ws_ref, wt_ref, wtt_ref=None):
    r = lax.broadcasted_iota(jnp.int32, (CHUNK, CHUNK), 0)
    c = lax.broadcasted_iota(jnp.int32, (CHUNK, CHUNK), 1)
    for g in range(N_GROUP):
        w = jnp.where(c <= r, ws_ref[g], 0.0)
        wt_ref[g] = w.astype(BF16)
        if wtt_ref is not None:
            wtt_ref[g] = w.T.astype(BF16)


def _kv_layouts(kv_prev, kv_cur):
    both = jnp.concatenate([kv_prev, kv_cur], axis=0)
    k = both[:, :128]
    v = both[:, 128:]
    return (k.astype(BF16), pltpu.roll(k, 64, axis=1).astype(BF16),
            v.astype(BF16), pltpu.roll(v, 64, axis=1).astype(BF16))


def _head_place(h):
    pair, pos, kvh = h // 2, h % 2, h // 4
    return pair, pos, kvh == pos


def _softmax_sink(qm, k_use, bias_h, sink, valid):
    s = _dot_nt(qm, k_use) * QK_SCALE + bias_h
    s = jnp.where(valid, s, NEG_INF)
    m = jnp.maximum(jnp.max(s, axis=-1, keepdims=True), sink)
    e = jnp.exp(s - m)
    es = jnp.exp(sink - m)
    inv = 1.0 / (jnp.sum(e, axis=-1, keepdims=True) + es)
    return e * inv, es * inv


def _mixer_fwd(z, v_gain, w_spatial, b_spatial_t, sinks, rel_table, bucket, exchange=None):
    def body(z_ref, kvp_ref, gain_ref, ws_ref, bt_ref, sink_ref, table_ref, bucket_ref, out_ref, probs_ref, probs_t_ref,
             share_ref, guv_ref, dgelu_ref, bias_ref, wt_ref):
        n = pl.program_id(0)

        @pl.when(n == 0)
        def _():
            _fill_bias(bucket_ref, table_ref, bias_ref)
            _fill_tril(ws_ref, wt_ref)

        zuv = z_ref[:, :1024]
        cdf, t = _gelu_parts(zuv)
        guv = zuv * cdf
        guv_ref[...] = guv
        dgelu_ref[...] = cdf + zuv * (0.5 * (1.0 - t * t)) * (GELU_C * (1.0 + 3.0 * 0.044715 * (zuv * zuv)))
        for g in range(N_GROUP):
            vg = guv[:, 512 + 128 * g:512 + 128 * (g + 1)]
            vn = vg * _rms_scale(vg) * gain_ref[:, 128 * g:128 * (g + 1)]
            sv = _dot(wt_ref[g], vn.astype(BF16)) + bt_ref[:, g:g + 1]
            out_ref[:, 128 * g:128 * (g + 1)] = (guv[:, 128 * g:128 * (g + 1)] * sv).astype(BF16)

        k_same, k_swap, v_same, v_swap = _kv_layouts(kvp_ref[...], z_ref[:, 1536:1792])
        valid = _band_mask(n)
        lane = lax.broadcasted_iota(jnp.int32, (1, 128), 1)
        lane_half = lane // 64
        shares = jnp.zeros((CHUNK, 128), F32)
        for pair in range(N_HEAD // 2):
            qq = z_ref[:, 1024 + 128 * pair:1024 + 128 * (pair + 1)]
            acc = jnp.zeros((CHUNK, 128), F32)
            for pos in range(2):
                h = 2 * pair + pos
                _, _, same = _head_place(h)
                qm = jnp.where(lane_half == pos, qq, 0.0).astype(BF16)
                p, p_sink = _softmax_sink(qm, k_same if same else k_swap, bias_ref[h], sink_ref[h], valid)
                pb = p.astype(BF16)
                probs_ref[0, h] = pb
                probs_t_ref[0, h] = p.T.astype(BF16)
                shares = jnp.where(lane == h, p_sink, shares)
                vm = jnp.where(lane_half == pos, v_same if same else v_swap, jnp.zeros((), BF16))
                acc = acc + _dot(pb, vm)
            out_ref[:, 512 + 128 * pair:512 + 128 * (pair + 1)] = acc.astype(BF16)
        share_ref[...] = shares

    return _call(
        body, (z, z, v_gain, w_spatial, b_spatial_t, sinks, rel_table, bucket), grid=(N_BLOCK,), name="mixer_fwd",
        in_specs=[pl.BlockSpec((CHUNK, D_IN), lambda n: (n, 0)),
                  pl.BlockSpec((CHUNK, 256), lambda n: (jnp.maximum(n - 1, 0), 6)),
                  pl.BlockSpec((1, 512), lambda n: (0, 0)),
                  pl.BlockSpec((N_GROUP, CHUNK, CHUNK), lambda n: (0, 0, 0)),
                  pl.BlockSpec((CHUNK, N_GROUP), lambda n: (0, 0)),
                  pl.BlockSpec(memory_space=pltpu.SMEM),
                  pl.BlockSpec(memory_space=pltpu.SMEM),
                  pl.BlockSpec((CHUNK, 2 * CHUNK), lambda n: (0, 0))],
        out_specs=[pl.BlockSpec((CHUNK, D), lambda n: (n, 0)),
                   pl.BlockSpec((1, N_HEAD, CHUNK, 2 * CHUNK), lambda n: (n, 0, 0, 0)),
                   pl.BlockSpec((1, N_HEAD, 2 * CHUNK, CHUNK), lambda n: (n, 0, 0, 0)),
                   pl.BlockSpec((CHUNK, 128), lambda n: (n, 0)),
                   pl.BlockSpec((CHUNK, 1024), lambda n: (n, 0)), pl.BlockSpec((CHUNK, 1024), lambda n: (n, 0))],
        out_shape=[jax.ShapeDtypeStruct((S, D), BF16), jax.ShapeDtypeStruct((N_BLOCK, N_HEAD, CHUNK, 2 * CHUNK), BF16),
                   jax.ShapeDtypeStruct((N_BLOCK, N_HEAD, 2 * CHUNK, CHUNK), BF16), jax.ShapeDtypeStruct((S, 128), F32),
                   jax.ShapeDtypeStruct((S, 1024), F32), jax.ShapeDtypeStruct((S, 1024), F32)],
        scratch_shapes=[pltpu.VMEM((N_HEAD, CHUNK, 2 * CHUNK), F32), pltpu.VMEM((N_GROUP, CHUNK, CHUNK), BF16)],
        compiler_params=_params(32), exchange=exchange)


def _out_proj(x, mix, w_out, gain2, exchange=None):
    tm = 256

    def body(x_ref, mix_ref, w_ref, g_ref, h1_ref, hn_ref, hnt_ref):
        h1 = x_ref[...] + _dot(mix_ref[...], w_ref[...])
        h1_ref[...] = h1
        hn = h1 * _rms_scale(h1) * g_ref[...]
        hn_ref[...] = hn.astype(BF16)
        hnt_ref[...] = hn.T.astype(BF16)

    return _call(
        body, (x, mix, w_out, gain2), grid=(S // tm,), name="out_proj",
        in_specs=[pl.BlockSpec((tm, D), lambda i: (i, 0)), pl.BlockSpec((tm, D), lambda i: (i, 0)),
                  pl.BlockSpec((D, D), lambda i: (0, 0)), pl.BlockSpec((1, D), lambda i: (0, 0))],
        out_specs=[pl.BlockSpec((tm, D), lambda i: (i, 0)), pl.BlockSpec((tm, D), lambda i: (i, 0)),
                   pl.BlockSpec((D, tm), lambda i: (0, i))],
        out_shape=[jax.ShapeDtypeStruct((S, D), F32), jax.ShapeDtypeStruct((S, D), BF16),
                   jax.ShapeDtypeStruct((D, S), BF16)],
        compiler_params=_params(32), exchange=exchange)


def _ffn_up(hn2, w_ff1, exchange=None):
    tm = 512
    nj = D_FF // 1024

    def body(hn_ref, w1_ref, r_ref, a_ref, at_ref):
        r = jnp.maximum(_dot(hn_ref[...], w1_ref[0]), 0.0)
        r_ref[...] = r.astype(BF16)
        a = r * r
        a_ref[...] = a.astype(BF16)
        at_ref[...] = a.T.astype(BF16)

    return _call(
        body, (hn2, w_ff1), grid=(nj, S // tm), name="ffn_up",
        in_specs=[pl.BlockSpec((tm, D), lambda j, i: (i, 0)), pl.BlockSpec((1, D, 1024), lambda j, i: (j, 0, 0))],
        out_specs=[pl.BlockSpec((tm, 1024), lambda j, i: (i, j)), pl.BlockSpec((tm, 1024), lambda j, i: (i, j)),
                   pl.BlockSpec((1024, tm), lambda j, i: (j, i))],
        out_shape=[jax.ShapeDtypeStruct((S, D_FF), BF16), jax.ShapeDtypeStruct((S, D_FF), BF16),
                   jax.ShapeDtypeStruct((D_FF, S), BF16)],
        compiler_params=_params(40, 2), exchange=exchange)


def _ffn_down(h1, a, w_ff2, exchange=None):
    tm = 1024
    nj = D_FF // 1024

    def body(h1_ref, a_ref, w2_ref, h2_ref, acc_ref):
        j = pl.program_id(1)
        part = _dot(a_ref[...], w2_ref[0])

        @pl.when(j == 0)
        def _():
            acc_ref[...] = part

        @pl.when(j > 0)
        def _():
            acc_ref[...] += part

        @pl.when(j == nj - 1)
        def _():
            h2_ref[...] = h1_ref[...] + acc_ref[...]

    return _call(
        body, (h1, a, w_ff2), grid=(S // tm, nj), name="ffn_down",
        in_specs=[pl.BlockSpec((tm, D), lambda i, j: (i, 0)), pl.BlockSpec((tm, 1024), lambda i, j: (i, j)),
                  pl.BlockSpec((1, 1024, D), lambda i, j: (j, 0, 0))],
        out_specs=[pl.BlockSpec((tm, D), lambda i, j: (i, 0))],
        out_shape=[jax.ShapeDtypeStruct((S, D), F32)],
        scratch_shapes=[pltpu.VMEM((tm, D), F32)],
        compiler_params=_params(48, 2), exchange=exchange)


def _tail(h2, p, target, w_gate, w_proj, final_gain):
    tm = 256
    steps = S // tm

    def body(h2_ref, p_ref, t_ref, wg_ref, wp_ref, gf_ref, dh2_ref, dwg_ref, dwp_ref, dgf_ref, loss_ref, dh2b_ref,
             dwp_acc):
        i = pl.program_id(0)
        h2 = h2_ref[...]
        h2b = h2.astype(BF16)
        pb = p_ref[...].astype(BF16)
        gate = jax.nn.sigmoid(_dot(h2b, wg_ref[...]))
        pp = jnp.concatenate([_dot(pb, wp_ref[j]) for j in range(N_CHIP)], axis=1)
        h3 = h2 + gate * pp
        r3 = _rms_scale(h3)
        xhat = h3 * r3
        gf = gf_ref[...]
        err = xhat * gf - t_ref[...]
        dy = err * (1.0 / D)
        dh3 = _rms_bwd(dy * gf, xhat, r3)
        dgp = (dh3 * pp * gate * (1.0 - gate)).astype(BF16)
        dpp = (dh3 * gate).astype(BF16)
        dh2 = dh3 + _dot_nt(dgp, wg_ref[...])
        dh2_ref[...] = dh2
        dh2b_ref[...] = dh2.astype(BF16)
        dwg = _dot_tn(h2b, dgp)
        dwp = _dot_tn(pb, dpp)
        dgf = jnp.sum(dy * xhat, axis=0, keepdims=True)
        sq = jnp.sum(jnp.sum(err * err, axis=1, keepdims=True), axis=0, keepdims=True)

        @pl.when(i == 0)
        def _():
            dwg_ref[...] = dwg
            dwp_acc[...] = dwp
            dgf_ref[...] = dgf
            loss_ref[...] = jnp.broadcast_to(sq, (8, 128))

        @pl.when(i > 0)
        def _():
            dwg_ref[...] += dwg
            dwp_acc[...] += dwp
            dgf_ref[...] += dgf
            loss_ref[...] += jnp.broadcast_to(sq, (8, 128))

        @pl.when(i == steps - 1)
        def _():
            for j in range(N_CHIP):
                dwp_ref[j] = dwp_acc[:, 256 * j:256 * (j + 1)]

    return _call(
        body, (h2, p, target, w_gate, w_proj, final_gain), grid=(steps,), name="tail",
        in_specs=[pl.BlockSpec((tm, D), lambda i: (i, 0)), pl.BlockSpec((tm, PLE), lambda i: (i, 0)),
                  pl.BlockSpec((tm, D), lambda i: (i, 0)), pl.BlockSpec((D, D), lambda i: (0, 0)),
                  pl.BlockSpec((N_CHIP, PLE, 256), lambda i: (0, 0, 0)), pl.BlockSpec((1, D), lambda i: (0, 0))],
        out_specs=[pl.BlockSpec((tm, D), lambda i: (i, 0)), pl.BlockSpec((D, D), lambda i: (0, 0)),
                   pl.BlockSpec((N_CHIP, PLE, 256), lambda i: (0, 0, 0)), pl.BlockSpec((1, D), lambda i: (0, 0)),
                   pl.BlockSpec((8, 128), lambda i: (0, 0)), pl.BlockSpec((tm, D), lambda i: (i, 0))],
        out_shape=[jax.ShapeDtypeStruct((S, D), F32), jax.ShapeDtypeStruct((D, D), F32),
                   jax.ShapeDtypeStruct((N_CHIP, PLE, 256), F32), jax.ShapeDtypeStruct((1, D), F32),
                   jax.ShapeDtypeStruct((8, 128), F32), jax.ShapeDtypeStruct((S, D), BF16)],
        scratch_shapes=[pltpu.VMEM((PLE, D), F32)],
        compiler_params=_params(48))[0]


def _ffn_bwd_down(dh2b, r, a_t, w_ff2, exchange=None):
    tm = 1024
    nj = D_FF // 1024

    def body(dh2_ref, r_ref, at_ref, w2_ref, df_ref, dw2_ref):
        i = pl.program_id(1)
        dh2b = dh2_ref[...]
        da = _dot_nt(dh2b, w2_ref[0])
        df_ref[...] = (da * (2.0 * r_ref[...].astype(F32))).astype(BF16)
        dw2 = _dot(at_ref[...], dh2b)

        @pl.when(i == 0)
        def _():
            dw2_ref[0] = dw2

        @pl.when(i > 0)
        def _():
            dw2_ref[0] += dw2

    return _call(
        body, (dh2b, r, a_t, w_ff2), grid=(nj, S // tm), name="ffn_bwd_down",
        in_specs=[pl.BlockSpec((tm, D), lambda j, i: (i, 0)), pl.BlockSpec((tm, 1024), lambda j, i: (i, j)),
                  pl.BlockSpec((1024, tm), lambda j, i: (j, i)), pl.BlockSpec((1, 1024, D), lambda j, i: (j, 0, 0))],
        out_specs=[pl.BlockSpec((tm, 1024), lambda j, i: (i, j)), pl.BlockSpec((1, 1024, D), lambda j, i: (j, 0, 0))],
        out_shape=[jax.ShapeDtypeStruct((S, D_FF), BF16), jax.ShapeDtypeStruct((nj, 1024, D), F32)],
        compiler_params=_params(48, 2), exchange=exchange)


def _ffn_bwd_up(df, hn2_t, exchange=None):
    tm = 1024
    nj = D_FF // 1024

    def body(df_ref, hnt_ref, dw1_ref):
        i = pl.program_id(1)
        dw1 = _dot(hnt_ref[...], df_ref[...])

        @pl.when(i == 0)
        def _():
            dw1_ref[0] = dw1

        @pl.when(i > 0)
        def _():
            dw1_ref[0] += dw1

    return _call(
        body, (df, hn2_t), grid=(nj, S // tm), name="ffn_bwd_up",
        in_specs=[pl.BlockSpec((tm, 1024), lambda j, i: (i, j)), pl.BlockSpec((D, tm), lambda j, i: (0, i))],
        out_specs=[pl.BlockSpec((1, D, 1024), lambda j, i: (j, 0, 0))],
        out_shape=[jax.ShapeDtypeStruct((nj, D, 1024), F32)],
        compiler_params=_params(40, 2), exchange=exchange)


def _ffn_bwd_input(df, w_ff1, dh2, h1, gain2, mix, w_out, exchange=None):
    tm = 512
    nj = D_FF // 1024
    steps = S // tm

    def body(df_ref, w1_ref, dh2_ref, h1_ref, g_ref, mix_ref, wo_ref, dh1_ref, dmix_ref, dwo_ref, dg_ref, acc_ref):
        i = pl.program_id(0)
        j = pl.program_id(1)
        part = _dot_nt(df_ref[...], w1_ref[0])

        @pl.when(j == 0)
        def _():
            acc_ref[...] = part

        @pl.when(j > 0)
        def _():
            acc_ref[...] += part

        @pl.when(j == nj - 1)
        def _():
            dhn = acc_ref[...]
            h1 = h1_ref[...]
            r2 = _rms_scale(h1)
            xhat = h1 * r2
            dh1 = dh2_ref[...] + _rms_bwd(dhn * g_ref[...], xhat, r2)
            dh1_ref[...] = dh1
            dh1b = dh1.astype(BF16)
            dmix_ref[...] = _dot_nt(dh1b, wo_ref[...])
            dwo = _dot_tn(mix_ref[...], dh1b)
            dg = jnp.sum(dhn * xhat, axis=0, keepdims=True)

            @pl.when(i == 0)
            def _():
                dwo_ref[...] = dwo
                dg_ref[...] = dg

            @pl.when(i > 0)
            def _():
                dwo_ref[...] += dwo
                dg_ref[...] += dg

    return _call(
        body, (df, w_ff1, dh2, h1, gain2, mix, w_out), grid=(steps, nj), name="ffn_bwd_input",
        in_specs=[pl.BlockSpec((tm, 1024), lambda i, j: (i, j)), pl.BlockSpec((1, D, 1024), lambda i, j: (j, 0, 0)),
                  pl.BlockSpec((tm, D), lambda i, j: (i, 0)), pl.BlockSpec((tm, D), lambda i, j: (i, 0)),
                  pl.BlockSpec((1, D), lambda i, j: (0, 0)), pl.BlockSpec((tm, D), lambda i, j: (i, 0)),
                  pl.BlockSpec((D, D), lambda i, j: (0, 0))],
        out_specs=[pl.BlockSpec((tm, D), lambda i, j: (i, 0)), pl.BlockSpec((tm, D), lambda i, j: (i, 0)),
                   pl.BlockSpec((D, D), lambda i, j: (0, 0)), pl.BlockSpec((1, D), lambda i, j: (0, 0))],
        out_shape=[jax.ShapeDtypeStruct((S, D), F32), jax.ShapeDtypeStruct((S, D), F32),
                   jax.ShapeDtypeStruct((D, D), F32), jax.ShapeDtypeStruct((1, D), F32)],
        scratch_shapes=[pltpu.VMEM((tm, D), F32)],
        compiler_params=_params(56, 2), exchange=exchange)


IN_GROUP = 8


def _mixer_bwd(z, dmix, v_gain, w_spatial, b_spatial_t, saved, bucket, hn1, exchange=None):
    def body(z_ref, kvp_ref, dm_ref, gain_ref, ws_ref, bt_ref, probs_ref, probs_t_ref, share_ref, guv_ref, dgelu_ref,
             bucket_ref, hn_ref,
             dz_ref, dws_ref, db_ref, dgain_ref, dsink_ref, drel_ref, dwin_ref,
             wt_ref, wtt_ref, dbias_ref, dsv_ref, carry_ref):
        n = pl.program_id(0)

        @pl.when(n == 0)
        def _():
            _fill_tril(ws_ref, wt_ref, wtt_ref)
            dwin_ref[...] = jnp.zeros_like(dwin_ref)
            dbias_ref[...] = jnp.zeros_like(dbias_ref)
            dsv_ref[...] = jnp.zeros_like(dsv_ref)
            dws_ref[...] = jnp.zeros_like(dws_ref)
            dgain_ref[...] = jnp.zeros_like(dgain_ref)
            dsink_ref[...] = jnp.zeros_like(dsink_ref)

        rows = pl.ds(pl.multiple_of(n * CHUNK, CHUNK), CHUNK)

        guv = guv_ref[...]
        dgelu = dgelu_ref[...]
        for g in range(N_GROUP):
            lo, hi = 128 * g, 128 * (g + 1)
            u = guv[:, lo:hi]
            vg = guv[:, 512 + lo:512 + hi]
            rr = _rms_scale(vg)
            vhat = vg * rr
            gain = gain_ref[:, lo:hi]
            vnb = (vhat * gain).astype(BF16)
            sv = _dot(wt_ref[g], vnb) + bt_ref[:, g:g + 1]
            da = dm_ref[:, lo:hi]
            dsv = da * u
            dsvb = dsv.astype(BF16)
            dsv_ref[g] += dsv
            dws_ref[g] += _dot_nt(dsvb, vnb)
            dvn = _dot(wtt_ref[g], dsvb)
            dgain_ref[:, lo:hi] += jnp.sum(dvn * vhat, axis=0, keepdims=True)
            dvg = _rms_bwd(dvn * gain, vhat, rr)
            dz_ref[rows, lo:hi] = (da * sv * dgelu[:, lo:hi]).astype(BF16)
            dz_ref[rows, 512 + lo:512 + hi] = (dvg * dgelu[:, 512 + lo:512 + hi]).astype(BF16)

        k_same, k_swap, v_same, v_swap = _kv_layouts(kvp_ref[...], z_ref[:, 1536:1792])
        lane_half = lax.broadcasted_iota(jnp.int32, (1, 128), 1) // 64
        zero = jnp.zeros((2 * CHUNK, 128), F32)
        dk_same, dk_swap, dv_same, dv_swap = zero, zero, zero, zero
        for pair in range(N_HEAD // 2):
            cols = slice(1024 + 128 * pair, 1024 + 128 * (pair + 1))
            qq = z_ref[:, cols]
            do_pair = dm_ref[:, 512 + 128 * pair:512 + 128 * (pair + 1)]
            dq = jnp.zeros((CHUNK, 128), F32)
            for pos in range(2):
                h = 2 * pair + pos
                _, _, same = _head_place(h)
                on_half = lane_half == pos
                qm = jnp.where(on_half, qq, 0.0).astype(BF16)
                k_use = k_same if same else k_swap
                v_use = v_same if same else v_swap
                pb = probs_ref[0, h]
                p = pb.astype(F32)
                p_sink = share_ref[:, h:h + 1]
                dom = jnp.where(on_half, do_pair, 0.0).astype(BF16)
                dp = _dot_nt(dom, v_use)
                dsum = jnp.sum(p * dp, axis=-1, keepdims=True)
                ds = p * (dp - dsum)
                dbias_ref[h] += ds
                dsink_ref[h:h + 1, :] += jnp.broadcast_to(jnp.sum(-p_sink * dsum, axis=0, keepdims=True), (1, 128))
                dsb = ds.astype(BF16)
                dq = dq + jnp.where(on_half, _dot(dsb, k_use), 0.0)
                dk_h = _dot_tn(dsb, qm)
                dv_h = _dot(probs_t_ref[0, h], dom)
                if same:
                    dk_same, dv_same = dk_same + dk_h, dv_same + dv_h
                else:
                    dk_swap, dv_swap = dk_swap + dk_h, dv_swap + dv_h
            dz_ref[rows, cols] = (dq * QK_SCALE).astype(BF16)
        dk = (dk_same + pltpu.roll(dk_swap, 64, axis=1)) * QK_SCALE
        dv = dv_same + pltpu.roll(dv_swap, 64, axis=1)
        dkv = jnp.concatenate([dk, dv], axis=1)

        @pl.when(n > 0)
        def _():
            prev_rows = pl.ds(pl.multiple_of((n - 1) * CHUNK, CHUNK), CHUNK)
            dz_ref[prev_rows, 1536:1792] = (carry_ref[...] + dkv[:CHUNK]).astype(BF16)

        carry_ref[...] = dkv[CHUNK:]

        @pl.when((n > 0) & (n % IN_GROUP == 0))
        def _():
            done = pl.ds(pl.multiple_of((n - IN_GROUP) * CHUNK, IN_GROUP * CHUNK), IN_GROUP * CHUNK)
            dwin_ref[...] += _dot_tn(dz_ref[done, :], hn_ref[...])

        @pl.when(n == N_BLOCK - 1)
        def _():
            dz_ref[rows, 1536:1792] = dkv[CHUNK:].astype(BF16)
            last = pl.ds((N_BLOCK - IN_GROUP) * CHUNK, IN_GROUP * CHUNK)
            dwin_ref[...] += _dot_tn(dz_ref[last, :], hn_ref[...])
            r = lax.broadcasted_iota(jnp.int32, (CHUNK, CHUNK), 0)
            c = lax.broadcasted_iota(jnp.int32, (CHUNK, CHUNK), 1)
            for g in range(N_GROUP):
                dws_ref[g] = jnp.where(c <= r, dws_ref[g], 0.0)
                db_ref[g] = jnp.sum(dsv_ref[g], axis=1, keepdims=True)
            bucket = bucket_ref[...]
            for h in range(N_HEAD):
                dbh = dbias_ref[h]
                per_bucket = [jnp.sum(jnp.where(bucket == b, dbh, 0.0), axis=0, keepdims=True) for b in range(N_BUCKET)]
                drel_ref[h] = jnp.sum(jnp.concatenate(per_bucket, axis=0), axis=1, keepdims=True)

    def hn_group(n):
        return jnp.where(n == N_BLOCK - 1, N_BLOCK // IN_GROUP - 1, jnp.maximum(n // IN_GROUP - 1, 0))

    return _call(
        body, (z, z, dmix, v_gain, w_spatial, b_spatial_t, *saved, bucket, hn1), grid=(N_BLOCK,),
        name="mixer_bwd",
        in_specs=[pl.BlockSpec((CHUNK, D_IN), lambda n: (n, 0)),
                  pl.BlockSpec((CHUNK, 256), lambda n: (jnp.maximum(n - 1, 0), 6)),
                  pl.BlockSpec((CHUNK, D), lambda n: (n, 0)),
                  pl.BlockSpec((1, 512), lambda n: (0, 0)),
                  pl.BlockSpec((N_GROUP, CHUNK, CHUNK), lambda n: (0, 0, 0)),
                  pl.BlockSpec((CHUNK, N_GROUP), lambda n: (0, 0)),
                  pl.BlockSpec((1, N_HEAD, CHUNK, 2 * CHUNK), lambda n: (n, 0, 0, 0)),
                  pl.BlockSpec((1, N_HEAD, 2 * CHUNK, CHUNK), lambda n: (n, 0, 0, 0)),
                  pl.BlockSpec((CHUNK, 128), lambda n: (n, 0)),
                  pl.BlockSpec((CHUNK, 1024), lambda n: (n, 0)), pl.BlockSpec((CHUNK, 1024), lambda n: (n, 0)),
                  pl.BlockSpec((CHUNK, 2 * CHUNK), lambda n: (0, 0)),
                  pl.BlockSpec((IN_GROUP * CHUNK, D), lambda n: (hn_group(n), 0))],
        out_specs=[pl.BlockSpec((S, D_IN), lambda n: (0, 0)),
                   pl.BlockSpec((N_GROUP, CHUNK, CHUNK), lambda n: (0, 0, 0)),
                   pl.BlockSpec((N_GROUP, CHUNK, 1), lambda n: (0, 0, 0)),
                   pl.BlockSpec((1, 512), lambda n: (0, 0)),
                   pl.BlockSpec((N_HEAD, 128), lambda n: (0, 0)),
                   pl.BlockSpec((N_HEAD, N_BUCKET, 1), lambda n: (0, 0, 0)),
                   pl.BlockSpec((D_IN, D), lambda n: (0, 0))],
        out_shape=[jax.ShapeDtypeStruct((S, D_IN), BF16), jax.ShapeDtypeStruct((N_GROUP, CHUNK, CHUNK), F32),
                   jax.ShapeDtypeStruct((N_GROUP, CHUNK, 1), F32), jax.ShapeDtypeStruct((1, 512), F32),
                   jax.ShapeDtypeStruct((N_HEAD, 128), F32), jax.ShapeDtypeStruct((N_HEAD, N_BUCKET, 1), F32),
                   jax.ShapeDtypeStruct((D_IN, D), F32)],
        scratch_shapes=[pltpu.VMEM((N_GROUP, CHUNK, CHUNK), BF16),
                        pltpu.VMEM((N_GROUP, CHUNK, CHUNK), BF16), pltpu.VMEM((N_HEAD, CHUNK, 2 * CHUNK), F32),
                        pltpu.VMEM((N_GROUP, CHUNK, CHUNK), F32), pltpu.VMEM((CHUNK, 256), F32)],
        compiler_params=_params(56), exchange=exchange)


def _in_bwd_input(dz, w_in_t, x, dh1, gain1, exchange=None):
    tm = 512

    def body(dz_ref, w_ref, x_ref, dh1_ref, g_ref, dx_ref, dg_ref):
        i = pl.program_id(0)
        dhn = _dot(dz_ref[...], w_ref[...])
        xv = x_ref[...]
        r1 = _rms_scale(xv)
        xhat = xv * r1
        dx_ref[...] = dh1_ref[...] + _rms_bwd(dhn * g_ref[...], xhat, r1)
        dg = jnp.sum(dhn * xhat, axis=0, keepdims=True)

        @pl.when(i == 0)
        def _():
            dg_ref[...] = dg

        @pl.when(i > 0)
        def _():
            dg_ref[...] += dg

    return _call(
        body, (dz, w_in_t, x, dh1, gain1), grid=(S // tm,), name="in_bwd_input",
        in_specs=[pl.BlockSpec((tm, D_IN), lambda i: (i, 0)), pl.BlockSpec((D_IN, D), lambda i: (0, 0)),
                  pl.BlockSpec((tm, D), lambda i: (i, 0)), pl.BlockSpec((tm, D), lambda i: (i, 0)),
                  pl.BlockSpec((1, D), lambda i: (0, 0))],
        out_specs=[pl.BlockSpec((tm, D), lambda i: (i, 0)), pl.BlockSpec((1, D), lambda i: (0, 0))],
        out_shape=[jax.ShapeDtypeStruct((S, D), F32), jax.ShapeDtypeStruct((1, D), F32)],
        compiler_params=_params(48), exchange=exchange)


def _rel_bucket():
    a = jnp.arange(CHUNK)[:, None]
    j = jnp.arange(2 * CHUNK)[None, :]
    n = jnp.maximum(CHUNK + a - j, 0)
    max_exact = N_BUCKET // 2
    nf = jnp.maximum(n, 1).astype(jnp.float32)
    large = max_exact + (jnp.log(nf / max_exact) / math.log(CHUNK / max_exact) * (N_BUCKET - max_exact)).astype(jnp.int32)
    large = jnp.minimum(large, N_BUCKET - 1)
    return jnp.where(n < max_exact, n, large).astype(jnp.int32)


def _step(x, p, target, small, bufs, place):
    bucket = _rel_bucket()
    sinks = small["attn_sinks"].reshape(N_HEAD)
    b_t = jnp.transpose(small["b_spatial"].reshape(N_GROUP, CHUNK))
    ws = small["w_spatial"].reshape(N_GROUP, CHUNK, CHUNK)
    gain1, gain2 = small["norm1_gain"], small["norm2_gain"]
    v_gain = small["gmlp_v_gain"]
    final_gain = small["final_gain"].reshape(1, D)
    table = small["rel_bias_table"]
    bufs = dict(bufs)

    def gather(*names):
        return _RelayGather([bufs[n] for n in names])

    def took(names, got):
        bufs.update(zip(names, got))

    w_in_t = _whole(bufs["w_in"]).reshape(D_IN, D)
    (z, hn1), got = _in_proj(x, gain1, w_in_t, gather("w_out"))
    took(["w_out"], got)
    (mix, *saved), got = _mixer_fwd(z, v_gain, ws, b_t, sinks, table, bucket, gather("w_ff1"))
    took(["w_ff1"], got)
    w_out = _whole(bufs["w_out"]).reshape(D, D)
    (h1, hn2, hn2_t), _ = _out_proj(x, mix, w_out, gain2)
    w_ff1 = _whole(bufs["w_ff1"])
    (r, a, a_t), got = _ffn_up(hn2, w_ff1, gather("w_ff2"))
    took(["w_ff2"], got)
    w_ff2 = _whole(bufs["w_ff2"])
    (h2,), got = _ffn_down(h1, a, w_ff2, gather("w_ple_gate", "w_ple_proj"))
    took(["w_ple_gate", "w_ple_proj"], got)
    dh2, d_gate, d_proj, d_final, sq, dh2b = _tail(h2, p, target, _whole(bufs["w_ple_gate"]).reshape(D, D),
                                                   _whole(bufs["w_ple_proj"]), final_gain)

    def pair_sums(halves, from_sibling):
        sums, landing = zip(*[_pair_sum(g, o, place) for g, o in zip(halves, from_sibling)])
        return list(sums), list(landing)

    landed = {}
    halves = [_halves(d_gate.reshape(N_CHIP, 256, D)), _halves(d_proj)]
    (df, d_ff2), got = _ffn_bwd_down(dh2b, r, a_t, w_ff2, _SiblingExchange(halves))
    ex, halves = _ChipExchange(*pair_sums(halves, got)), [_halves(d_ff2)]
    (d_ff1,), got = _ffn_bwd_up(df, hn2_t, _Both(ex, _SiblingExchange(halves)))
    landed.update(zip(["w_ple_gate", "w_ple_proj"], got[:2]))
    ex, halves = _ChipExchange(*pair_sums(halves, got[2:])), [_halves(d_ff1)]
    (dh1, dmix, d_out, d_gain2), got = _ffn_bwd_input(df, w_ff1, dh2, h1, gain2, mix, w_out,
                                                      _Both(ex, _SiblingExchange(halves)))
    landed["w_ff2"] = got[0]
    ex, halves = _ChipExchange(*pair_sums(halves, got[1:])), [_halves(d_out.reshape(N_CHIP, 256, D))]
    (dz, d_ws, d_b, d_vgain, d_sink, d_rel, d_in_t), got = _mixer_bwd(z, dmix, v_gain, ws, b_t, saved, bucket, hn1,
                                                                     _Both(ex, _SiblingExchange(halves)))
    landed["w_ff1"] = got[0]
    small_grads = {
        "gmlp_v_gain": d_vgain, "w_spatial": d_ws.reshape(1, N_GROUP, CHUNK, CHUNK),
        "b_spatial": d_b.reshape(1, N_GROUP, CHUNK), "attn_sinks": d_sink[:, 0].reshape(1, N_HEAD),
        "rel_bias_table": jnp.transpose(d_rel.reshape(N_HEAD, N_BUCKET)), "norm2_gain": d_gain2,
        "final_gain": d_final.reshape(D),
    }
    ex, halves = _ChipExchange(*pair_sums(halves, got[1:])), [_halves(d_in_t.reshape(N_CHIP, 448, D))]
    (dx, small_grads["norm1_gain"]), got = _in_bwd_input(dz, w_in_t, x, dh1, gain1, _Both(ex, _SiblingExchange(halves)))
    landed["w_out"] = got[0]
    return dx, landed, _ChipExchange(*pair_sums(halves, got[1:])), small_grads, sq


HBM_SPEC = pl.BlockSpec(memory_space=pltpu.HBM)
VMEM_SPEC = pl.BlockSpec(memory_space=pltpu.VMEM)


def _mesh_place():
    x, y, c = lax.axis_index("x"), lax.axis_index("y"), lax.axis_index("c")
    others = [(1 - x, y), (x, 1 - y), (1 - x, 1 - y)]
    return x, y, c, others


def _remote(src, dst, send_sem, recv_sem, device):
    return pltpu.make_async_remote_copy(src_ref=src, dst_ref=dst, send_sem=send_sem, recv_sem=recv_sem,
                                        device_id=device, device_id_type=MESH)


def _hbm_like(a, shape=None, dtype=None):
    return pltpu.HBM(a.shape if shape is None else shape, a.dtype if dtype is None else dtype)


def _gather_start(bufs, send_sems, recv_sems):
    x, y, c, others = _mesh_place()
    me = 2 * x + y
    for w, buf in enumerate(bufs):
        for k in range(3):
            mine = buf.at[me, c]
            _remote(mine, mine, send_sems.at[w, k], recv_sems.at[w, k], (*others[k], c)).start()


def _gather_finish(bufs, send_sems, recv_sems):
    x, y, c, others = _mesh_place()
    me = 2 * x + y
    sibling = (x, y, 1 - c)
    idx = [2 * ox + oy for ox, oy in others]
    chips = range(3)
    for w, buf in enumerate(bufs):
        for k in chips:
            landed = buf.at[idx[k], c]
            _remote(landed, landed, send_sems.at[w, k], recv_sems.at[w, k], sibling).wait_recv()
            _remote(landed, landed, send_sems.at[w, 3 + k], recv_sems.at[w, 3 + k], sibling).start()
    for w, buf in enumerate(bufs):
        for k in chips:
            landed = buf.at[idx[k], 1 - c]
            _remote(landed, landed, send_sems.at[w, 3 + k], recv_sems.at[w, 3 + k], sibling).wait_recv()
    for w, buf in enumerate(bufs):
        for k in chips:
            mine, passed = buf.at[me, c], buf.at[idx[k], c]
            _remote(mine, mine, send_sems.at[w, k], recv_sems.at[w, k], sibling).wait_send()
            _remote(passed, passed, send_sems.at[w, 3 + k], recv_sems.at[w, 3 + k], sibling).wait_send()


def _gather_sems(n):
    return [pltpu.SemaphoreType.DMA((n, 6)), pltpu.SemaphoreType.DMA((n, 6))]


def _sibling_copies(grads, landing, send_sems, recv_sems):
    x, y, c, _ = _mesh_place()
    return [_remote(grads[w].at[j, 1 - c], landing[w].at[j], send_sems.at[w, j], recv_sems.at[w, j], (x, y, 1 - c))
            for w in range(len(grads)) for j in range(N_CHIP)]


def _sibling_exchange_start(grads, landing, send_sems, recv_sems):
    for cp in _sibling_copies(grads, landing, send_sems, recv_sems):
        cp.start()


def _sibling_exchange_finish(grads, landing, send_sems, recv_sems):
    copies = _sibling_copies(grads, landing, send_sems, recv_sems)
    for cp in copies:
        cp.wait_recv()
    for cp in copies:
        cp.wait_send()


def _sibling_exchange_sems(n):
    return [pltpu.SemaphoreType.DMA((n, N_CHIP)), pltpu.SemaphoreType.DMA((n, N_CHIP))]


def _sibling_exchange(grads):
    n = len(grads)

    def body(*refs):
        ins, outs = refs[:n], refs[n:2 * n]
        _sibling_exchange_start(ins, outs, *refs[2 * n:])
        _sibling_exchange_finish(ins, outs, *refs[2 * n:])

    return pl.pallas_call(
        body, name="sibling_exchange",
        in_specs=[HBM_SPEC] * n, out_specs=[HBM_SPEC] * n,
        out_shape=[_hbm_like(g, (N_CHIP,) + g.shape[2:]) for g in grads],
        scratch_shapes=_sibling_exchange_sems(n),
    )(*[_in_hbm(g) for g in grads])


def _chip_exchange_start(sums, landing, send_sems, recv_sems):
    x, y, c, others = _mesh_place()
    me = 2 * x + y
    for w in range(len(sums)):
        for k, (ox, oy) in enumerate(others):
            _remote(sums[w].at[2 * ox + oy], landing[w].at[me], send_sems.at[w, k], recv_sems.at[w, k],
                    (ox, oy, c)).start()


def _chip_exchange_finish(sums, landing, send_sems, recv_sems):
    x, y, c, others = _mesh_place()
    for w in range(len(sums)):
        for k, (ox, oy) in enumerate(others):
            piece = landing[w].at[2 * ox + oy]
            _remote(piece, piece, send_sems.at[w, k], recv_sems.at[w, k], (x, y, c)).wait_recv()
    for w in range(len(sums)):
        for k, (ox, oy) in enumerate(others):
            piece = sums[w].at[2 * ox + oy]
            _remote(piece, piece, send_sems.at[w, k], recv_sems.at[w, k], (x, y, c)).wait_send()


def _chip_exchange_sems(n):
    return [pltpu.SemaphoreType.DMA((n, 3)), pltpu.SemaphoreType.DMA((n, 3))]


def _sibling_allgather(bufs, also):
    n = len(bufs)
    k_in, k_out = len(also.operands), also.n_out

    def body(*refs):
        ex_ins, refs = refs[n:n + k_in], refs[n + k_in:]
        outs, refs = refs[:n], refs[n:]
        ex_outs, refs = refs[:k_out], refs[k_out:]
        send_sems, recv_sems, ex_sems = refs[0], refs[1], refs[2:]
        x, y, c, _ = _mesh_place()
        sibling = (x, y, 1 - c)
        also.start(ex_ins, ex_outs, ex_sems)
        sends = [_remote(outs[w].at[c], outs[w].at[c], send_sems.at[w], recv_sems.at[w], sibling) for w in range(n)]
        for cp in sends:
            cp.start()
        for w in range(n):
            landed = outs[w].at[1 - c]
            _remote(landed, landed, send_sems.at[w], recv_sems.at[w], sibling).wait_recv()
        for cp in sends:
            cp.wait_send()
        also.finish(ex_ins, ex_outs, ex_sems)

    res = pl.pallas_call(
        body, name="sibling_allgather",
        in_specs=[HBM_SPEC] * (n + k_in), out_specs=[HBM_SPEC] * (n + k_out),
        out_shape=[_hbm_like(b) for b in bufs] + also.out_shape,
        input_output_aliases={**{w: w for w in range(n)}, **{n + i: n + o for i, o in also.aliases.items()}},
        scratch_shapes=[pltpu.SemaphoreType.DMA((n,)), pltpu.SemaphoreType.DMA((n,))] + also.sems,
    )(*bufs, *[_in_hbm(o) for o in also.operands])
    return list(res[:n]), list(res[n:])


def _pair_sum(grad, other, place):
    _, _, h, cols = grad.shape
    tr = _row_tile(h)

    def body(place_ref, g_ref, o_ref, sums_ref, own_ref):
        s = (g_ref[0, 0] + o_ref[0]).astype(BF16)
        sums_ref[0] = s

        @pl.when(pl.program_id(1) == place_ref[0])
        def _():
            own_ref[0] = s

    return pl.pallas_call(
        body, name="pair_sum",
        grid_spec=pltpu.PrefetchScalarGridSpec(
            num_scalar_prefetch=1, grid=(h // tr, N_CHIP),
            in_specs=[pl.BlockSpec((1, 1, tr, cols), lambda r, j, place_ref: (j, place_ref[1], r, 0)),
                      pl.BlockSpec((1, tr, cols), lambda r, j, place_ref: (j, r, 0))],
            out_specs=[pl.BlockSpec((1, tr, cols), lambda r, j, place_ref: (j, r, 0)),
                       pl.BlockSpec((1, tr, cols), lambda r, j, place_ref: (place_ref[0], r, 0))]),
        out_shape=[pltpu.HBM((N_CHIP, h, cols), BF16)] * 2,
        compiler_params=_params(32, 2),
    )(place, _in_hbm(grad), _in_hbm(other))


def _chip_sum(parts, place):
    _, h, cols = parts.shape
    tr = _row_tile(h)

    def body(place_ref, p_ref, out_ref):
        out_ref[0] = ((p_ref[0].astype(F32) + p_ref[1].astype(F32)) + p_ref[2].astype(F32)) + p_ref[3].astype(F32)

    return pl.pallas_call(
        body, name="chip_sum",
        grid_spec=pltpu.PrefetchScalarGridSpec(
            num_scalar_prefetch=1, grid=(h // tr,),
            in_specs=[pl.BlockSpec((N_CHIP, tr, cols), lambda r, place_ref: (0, r, 0))],
            out_specs=pl.BlockSpec((1, tr, cols), lambda r, place_ref: (place_ref[1], r, 0))),
        out_shape=pltpu.HBM((2, h, cols), F32),
        compiler_params=_params(32),
    )(place, _in_hbm(parts))


def _adamw_math(w, g, m, v):
    m = ADAM_B1 * m + (1.0 - ADAM_B1) * g
    v = ADAM_B2 * v + (1.0 - ADAM_B2) * (g * g)
    m_hat = m / (1.0 - ADAM_B1 ** ADAM_STEP)
    v_hat = v / (1.0 - ADAM_B2 ** ADAM_STEP)
    delta = -ADAM_LR * (m_hat / (jnp.sqrt(v_hat) + ADAM_EPS) + ADAM_WD * w)
    return delta, m, v


def _adamw(w, g, m, v, exchange=None):
    rows, cols = w.shape
    tr = _row_tile(rows)

    def body(w_ref, g_ref, m_ref, v_ref, d_ref, nm_ref, nv_ref, g_out_ref):
        g = g_ref[...]
        d_ref[...], nm_ref[...], nv_ref[...] = _adamw_math(w_ref[...], g, m_ref[...], v_ref[...])
        g_out_ref[...] = g

    spec = pl.BlockSpec((tr, cols), lambda r: (r, 0))
    return _call(
        body, (w, g, m, v), grid=(rows // tr,), name="adamw",
        in_specs=[spec] * 4, out_specs=[spec] * 4,
        out_shape=[jax.ShapeDtypeStruct((rows, cols), F32)] * 4,
        compiler_params=_params(48), exchange=exchange)


SMALL_NAMES = ("norm1_gain", "gmlp_v_gain", "w_spatial", "b_spatial", "attn_sinks", "rel_bias_table", "norm2_gain",
               "final_gain")
PACK_TILE = 8 * 128


def _pack_small(arrays):
    parts = []
    for a in arrays:
        flat = a.reshape(-1)
        rows = -(-flat.shape[0] // PACK_TILE) * 8
        parts.append(jnp.pad(flat, (0, rows * 128 - flat.shape[0])).reshape(rows, 128))
    return jnp.concatenate(parts, axis=0)


def _unpack_small(packed, like):
    out, row = [], 0
    for a in like:
        size = math.prod(a.shape)
        rows = -(-size // PACK_TILE) * 8
        out.append(packed[row:row + rows].reshape(-1)[:size].reshape(a.shape))
        row += rows
    return out


def _small_update(gathered, w, m, v):
    rows = gathered.shape[1]

    def body(g_ref, w_ref, m_ref, v_ref, tot_ref, d_ref, nm_ref, nv_ref):
        total = g_ref[0].astype(F32)
        for dev in range(1, 8):
            total = total + g_ref[dev].astype(F32)
        tot_ref[...] = total
        d_ref[...], nm_ref[...], nv_ref[...] = _adamw_math(w_ref[...], total, m_ref[...], v_ref[...])

    return pl.pallas_call(
        body, name="small_update",
        in_specs=[VMEM_SPEC] * 4, out_specs=[VMEM_SPEC] * 4,
        out_shape=[jax.ShapeDtypeStruct((rows, 128), F32)] * 4,
        compiler_params=pltpu.CompilerParams(vmem_limit_bytes=24 * MIB),
    )(gathered, w, m, v)


def _halves(a):
    return a.reshape(a.shape[:-2] + (2, a.shape[-2] // 2, a.shape[-1]))


def _whole(a):
    return a.reshape(a.shape[:-3] + (2 * a.shape[-2], a.shape[-1]))


def kernel(x, p, norm1_gain, w_in, gmlp_v_gain, w_spatial, b_spatial, attn_sinks, rel_bias_table, w_out, norm2_gain, w_ff1, w_ff2, w_ple_proj, w_ple_gate, final_gain, loss_target, m_norm1_gain, m_w_in, m_gmlp_v_gain, m_w_spatial, m_b_spatial, m_attn_sinks, m_rel_bias_table, m_w_out, m_norm2_gain, m_w_ff1, m_w_ff2, m_w_ple_proj, m_w_ple_gate, m_final_gain, v_norm1_gain, v_w_in, v_gmlp_v_gain, v_w_spatial, v_b_spatial, v_attn_sinks, v_rel_bias_table, v_w_out, v_norm2_gain, v_w_ff1, v_w_ff2, v_w_ple_proj, v_w_ple_gate, v_final_gain):
    given = dict(locals())
    small = {n: given[n] for n in SMALL_NAMES}
    chip = 2 * lax.axis_index("x") + lax.axis_index("y")
    place = jnp.stack([chip, lax.axis_index("c")]).astype(jnp.int32)

    big_names = ("w_in", "w_out", "w_ff1", "w_ff2", "w_ple_proj", "w_ple_gate")
    shards = {n: given[n][0] for n in big_names}
    travel = dict(shards, w_in=jnp.transpose(shards["w_in"]))
    rest = [n for n in big_names if n != "w_in"]
    cast, gathered = _cast_shards_beside_gather([travel[n] for n in rest], place[:1],
                                                [_cast_shard(travel["w_in"], place[:1])])
    bufs = dict(zip(rest + ["w_in"], cast + gathered))
    dx, landed, exchange_in, small_grads, sq = _step(x[0], p[0, 0], loss_target[0], small, bufs, place)

    out_grad, out_delta, out_m, out_v = {}, {}, {}, {}

    def update(n, g, exchange=None):
        to = jnp.transpose if n == "w_in" else (lambda a: a)
        (delta, new_m, new_v, g_out), got = _adamw(to(shards[n]), g, to(given["m_" + n][0]), to(given["v_" + n][0]),
                                                   exchange)
        out_grad[n], out_delta[n], out_m[n], out_v[n] = [to(a)[None] for a in (g_out, delta, new_m, new_v)]
        return got

    spare = jnp.zeros((8, 128), F32)
    small_packed = _pack_small([small_grads[n] for n in SMALL_NAMES] + [spare]).astype(BF16)
    early = [n for n in big_names if n != "w_in"]
    reduced, (small_gathered, sq_gathered, landed_in) = _sibling_allgather(
        [_chip_sum(landed[n], place) for n in early], _Both(_Both(_GatherAll(small_packed), _GatherAll(sq)), exchange_in))
    for n, r in zip(early, reduced):
        update(n, _whole(r))
    (reduced_in,), _ = _sibling_allgather([_chip_sum(landed_in, place)], _Nothing())
    update("w_in", _whole(reduced_in))

    like = [given[n] for n in SMALL_NAMES] + [spare]
    packed = _small_update(small_gathered, *[_pack_small([given[pre + n] for n in SMALL_NAMES] + [spare])
                                             for pre in ("", "m_", "v_")])
    for res, out in zip(packed, (out_grad, out_delta, out_m, out_v)):
        out.update(zip(SMALL_NAMES, _unpack_small(res, like)))
    loss = 0.5 * jnp.sum(sq_gathered[:, 0, 0]) / D

    order = ("norm1_gain", "w_in", "gmlp_v_gain", "w_spatial", "b_spatial", "attn_sinks", "rel_bias_table", "w_out",
             "norm2_gain", "w_ff1", "w_ff2", "w_ple_proj", "w_ple_gate", "final_gain")
    return (loss, dx[None], *[out_grad[n] for n in order], *[out_delta[n] for n in order],
            *[out_m[n] for n in order], *[out_v[n] for n in order])
```

```python
import functools
import math

import jax
import jax.numpy as jnp
from jax import lax
from jax.experimental import pallas as pl
from jax.experimental.pallas import tpu as pltpu

S = 2048
D = 1024
D_IN = 1792
D_FF = 4096
PLE = 256
N_CHIP = 4
N_GROUP = 4
CHUNK = 128
N_HEAD = 8
N_BLOCK = S // CHUNK
N_BUCKET = 32
EPS = 1e-6
NEG_INF = -1e30
QK_SCALE = 0.125
GELU_C = math.sqrt(2.0 / math.pi)

ADAM_LR = 0.001
ADAM_B1 = 0.9
ADAM_B2 = 0.999
ADAM_EPS = 1e-08
ADAM_WD = 0.01
ADAM_STEP = 10

F32 = jnp.float32
BF16 = jnp.bfloat16
MIB = 1024 * 1024
MESH = pl.DeviceIdType.MESH

NT = (((1,), (1,)), ((), ()))
TN = (((0,), (0,)), ((), ()))


def _dot(a, b):
    return jnp.dot(a, b, preferred_element_type=F32)


def _dot_nt(a, b):
    return lax.dot_general(a, b, NT, preferred_element_type=F32)


def _dot_tn(a, b):
    return lax.dot_general(a, b, TN, preferred_element_type=F32)


def _params(vmem_mib, n_axes=1):
    return pltpu.CompilerParams(dimension_semantics=("arbitrary",) * n_axes, vmem_limit_bytes=vmem_mib * MIB)


def _rms_scale(v):
    return lax.rsqrt(jnp.mean(v * v, axis=-1, keepdims=True) + EPS)


def _rms_bwd(dy_gain, xhat, r):
    return r * (dy_gain - xhat * jnp.mean(dy_gain * xhat, axis=-1, keepdims=True))


class _Gather:
    def __init__(self, bufs):
        self.operands = list(bufs)
        self.n_out = len(self.operands)
        self.out_shape = [_hbm_like(b) for b in bufs]
        self.aliases = {w: w for w in range(self.n_out)}
        self.sems = _gather_sems(self.n_out)

    def start(self, ins, outs, sems):
        _gather_start(outs, *sems)

    def finish(self, ins, outs, sems):
        _gather_finish(outs, *sems)


class _RelayGather(_Gather):
    TOP, BOTTOM = 6, 7
    DIAGONAL_PASSED = 5
    MIDDLE_AT, LATE_AT = (5, 8), (7, 8)

    def __init__(self, bufs):
        super().__init__(bufs)
        self.sems = [pltpu.SemaphoreType.DMA((self.n_out, 8)), pltpu.SemaphoreType.DMA((self.n_out, 8))]

    def _copies(self, bufs, send_sems, recv_sems):
        x, y, c, others = _mesh_place()
        me = 2 * x + y
        idx = [2 * ox + oy for ox, oy in others]
        sibling = (x, y, 1 - c)
        direct, passed, relayed = [], [], []
        for w, buf in enumerate(bufs):
            rows = buf.shape[2] // 2
            upper, lower = pl.ds(0, rows), pl.ds(rows, rows)
            for k in (0, 1):
                mine = buf.at[me, c]
                direct.append((_remote(mine, mine, send_sems.at[w, k], recv_sems.at[w, k], (*others[k], c)),
                               buf.at[idx[k], c], w, k))
            for k in (0, 1, 2):
                here = buf.at[idx[k], c]
                passed.append((_remote(here, here, send_sems.at[w, 3 + k], recv_sems.at[w, 3 + k], sibling),
                               buf.at[idx[k], 1 - c], w, 3 + k))
            from_x, from_y = buf.at[idx[0], c, upper], buf.at[idx[1], c, lower]
            relayed.append((_remote(from_x, from_x, send_sems.at[w, self.TOP], recv_sems.at[w, self.TOP],
                                    (*others[1], c)), buf.at[idx[2], c, upper], w, self.TOP))
            relayed.append((_remote(from_y, from_y, send_sems.at[w, self.BOTTOM], recv_sems.at[w, self.BOTTOM],
                                    (*others[0], c)), buf.at[idx[2], c, lower], w, self.BOTTOM))
        return direct, passed, relayed

    @staticmethod
    def _landed(piece, send_sems, recv_sems, w, col):
        x, y, c, _ = _mesh_place()
        _remote(piece, piece, send_sems.at[w, col], recv_sems.at[w, col], (x, y, c)).wait_recv()

    def start(self, ins, outs, sems):
        for cp, _, _, _ in self._copies(outs, *sems)[0]:
            cp.start()

    def middle(self, ins, outs, sems):
        direct, passed, relayed = self._copies(outs, *sems)
        for _, piece, w, col in direct:
            self._landed(piece, *sems, w, col)
        for cp, _, _, col in passed:
            if col != self.DIAGONAL_PASSED:
                cp.start()
        for cp, _, _, _ in relayed:
            cp.start()

    def late(self, ins, outs, sems):
        direct, passed, relayed = self._copies(outs, *sems)
        for _, piece, w, col in relayed:
            self._landed(piece, *sems, w, col)
        for cp, _, _, col in passed:
            if col == self.DIAGONAL_PASSED:
                cp.start()

    def finish(self, ins, outs, sems):
        direct, passed, relayed = self._copies(outs, *sems)
        for _, piece, w, col in passed:
            self._landed(piece, *sems, w, col)
        for cp, _, _, _ in direct + passed + relayed:
            cp.wait_send()


class _ChipExchange:
    def __init__(self, sums, landing):
        self.n_out = len(landing)
        self.operands = list(sums) + list(landing)
        self.out_shape = [_hbm_like(b) for b in landing]
        self.aliases = {self.n_out + w: w for w in range(self.n_out)}
        self.sems = _chip_exchange_sems(self.n_out)

    def start(self, ins, outs, sems):
        _chip_exchange_start(ins[:self.n_out], outs, *sems)

    def finish(self, ins, outs, sems):
        _chip_exchange_finish(ins[:self.n_out], outs, *sems)


class _GatherAll:
    def __init__(self, packed):
        self.operands = [packed]
        self.n_out = 1
        self.out_shape = [_hbm_like(packed, (8,) + packed.shape)]
        self.aliases = {}
        self.sems = [pltpu.SemaphoreType.DMA((8,)), pltpu.SemaphoreType.DMA((8,))]

    def _copies(self, ins, outs, sems):
        x, y, c, _ = _mesh_place()
        me = 4 * x + 2 * y + c
        send_sems, recv_sems = sems
        copies = []
        for k in range(1, 8):
            peer = (1 - x if k // 4 else x, 1 - y if (k // 2) % 2 else y, 1 - c if k % 2 else c)
            src = 4 * peer[0] + 2 * peer[1] + peer[2]
            copies.append((_remote(ins[0], outs[0].at[me], send_sems.at[k], recv_sems.at[k], peer), outs[0].at[src]))
        own = pltpu.make_async_copy(ins[0], outs[0].at[me], send_sems.at[0])
        return own, copies

    def start(self, ins, outs, sems):
        own, copies = self._copies(ins, outs, sems)
        own.start()
        for cp, _ in copies:
            cp.start()

    def finish(self, ins, outs, sems):
        own, copies = self._copies(ins, outs, sems)
        x, y, c, _ = _mesh_place()
        for k, (cp, landed) in enumerate(copies):
            _remote(landed, landed, sems[0].at[k + 1], sems[1].at[k + 1], (x, y, c)).wait_recv()
        for cp, _ in copies:
            cp.wait_send()
        own.wait()


class _Nothing:
    operands, n_out, out_shape, aliases, sems = [], 0, [], {}, []

    def start(self, ins, outs, sems):
        pass

    def finish(self, ins, outs, sems):
        pass


class _Both:
    def __init__(self, a, b):
        self.a, self.b = a, b
        self.operands = a.operands + b.operands
        self.n_out = a.n_out + b.n_out
        self.out_shape = a.out_shape + b.out_shape
        self.aliases = dict(a.aliases)
        self.aliases.update({len(a.operands) + i: a.n_out + o for i, o in b.aliases.items()})
        self.sems = a.sems + b.sems

    def _split(self, ins, outs, sems):
        ka, na, sa = len(self.a.operands), self.a.n_out, len(self.a.sems)
        return (ins[:ka], outs[:na], sems[:sa]), (ins[ka:], outs[na:], sems[sa:])

    def start(self, ins, outs, sems):
        for ex, args in zip((self.a, self.b), self._split(ins, outs, sems)):
            ex.start(*args)

    def finish(self, ins, outs, sems):
        for ex, args in zip((self.a, self.b), self._split(ins, outs, sems)):
            ex.finish(*args)


class _SiblingExchange:
    def __init__(self, grads):
        self.operands = list(grads)
        self.n_out = len(self.operands)
        self.out_shape = [_hbm_like(g, (N_CHIP,) + g.shape[2:]) for g in grads]
        self.aliases = {}
        self.sems = _sibling_exchange_sems(self.n_out)

    def start(self, ins, outs, sems):
        _sibling_exchange_start(ins, outs, *sems)

    def finish(self, ins, outs, sems):
        _sibling_exchange_finish(ins, outs, *sems)


def _call(body, operands, *, grid, in_specs, out_specs, out_shape, name, compiler_params, scratch_shapes=(),
          exchange=None):
    operands = [o if getattr(spec, "memory_space", None) == pltpu.SMEM else _in_hbm(o)
                for o, spec in zip(operands, in_specs)]
    out_shape = [pltpu.HBM(s.shape, s.dtype) for s in out_shape]
    if exchange is None:
        res = pl.pallas_call(body, grid=grid, in_specs=in_specs, out_specs=out_specs, out_shape=out_shape, name=name,
                             scratch_shapes=list(scratch_shapes), compiler_params=compiler_params)(*operands)
        return list(res), []
    n_in, n_out, n_scr = len(in_specs), len(out_specs), len(scratch_shapes)
    k_in, k_out = len(exchange.operands), exchange.n_out

    def fused(*refs):
        ins, refs = refs[:n_in], refs[n_in:]
        ex_ins, refs = refs[:k_in], refs[k_in:]
        outs, refs = refs[:n_out], refs[n_out:]
        ex_outs, refs = refs[:k_out], refs[k_out:]
        scratch, sems = refs[:n_scr], refs[n_scr:]
        ids = [pl.program_id(a) for a in range(len(grid))]
        first = functools.reduce(jnp.logical_and, [i == 0 for i in ids])
        last = functools.reduce(jnp.logical_and, [i == g - 1 for i, g in zip(ids, grid)])

        @pl.when(first)
        def _():
            exchange.start(ex_ins, ex_outs, sems)

        def at_step(numerator, denominator):
            at = (numerator * math.prod(grid)) // denominator
            place = [(at // math.prod(grid[a + 1:])) % grid[a] for a in range(len(grid))]
            return functools.reduce(jnp.logical_and, [i == p for i, p in zip(ids, place)])

        if hasattr(exchange, "middle"):
            @pl.when(at_step(*exchange.MIDDLE_AT))
            def _():
                exchange.middle(ex_ins, ex_outs, sems)

            @pl.when(at_step(*exchange.LATE_AT))
            def _():
                exchange.late(ex_ins, ex_outs, sems)

        body(*ins, *outs, *scratch)

        @pl.when(last)
        def _():
            exchange.finish(ex_ins, ex_outs, sems)

    res = pl.pallas_call(
        fused, grid=grid, name=name,
        in_specs=list(in_specs) + [HBM_SPEC] * k_in, out_specs=list(out_specs) + [HBM_SPEC] * k_out,
        out_shape=list(out_shape) + exchange.out_shape,
        input_output_aliases={n_in + i: n_out + o for i, o in exchange.aliases.items()},
        scratch_shapes=list(scratch_shapes) + exchange.sems, compiler_params=compiler_params,
    )(*operands, *[_in_hbm(o) for o in exchange.operands])
    return list(res[:n_out]), list(res[n_out:])


def _in_hbm(a):
    return pltpu.with_memory_space_constraint(a, pltpu.HBM)


def _row_tile(h):
    return max(t for t in range(16, 513, 16) if h % t == 0)


def _cast_shard(a, chip):
    rows, cols = a.shape
    h = rows // 2
    tr = _row_tile(h)

    def body(chip_ref, a_ref, o_ref):
        o_ref[0, 0] = a_ref[0].astype(BF16)

    return pl.pallas_call(
        body, name="cast_shard",
        grid_spec=pltpu.PrefetchScalarGridSpec(
            num_scalar_prefetch=1, grid=(2, h // tr),
            in_specs=[pl.BlockSpec((1, tr, cols), lambda s, r, chip_ref: (s, r, 0))],
            out_specs=pl.BlockSpec((1, 1, tr, cols), lambda s, r, chip_ref: (chip_ref[0], s, r, 0))),
        out_shape=pltpu.HBM((N_CHIP, 2, h, cols), BF16),
        compiler_params=_params(16, 2),
    )(chip, _in_hbm(a.reshape(2, h, cols)))


def _cast_shards_beside_gather(arrays, chip, gathered):
    n, k = len(arrays), len(gathered)
    shapes = [(a.shape[0] // 2, a.shape[1]) for a in arrays]

    def body(chip_ref, *refs):
        ins, refs = refs[:n], refs[n + k:]
        outs, refs = refs[:n], refs[n:]
        bufs, sems = refs[:k], refs[k:]
        half = pl.program_id(0)

        @pl.when(half == 0)
        def _():
            _gather_start(bufs, *sems)

        for a_ref, o_ref in zip(ins, outs):
            o_ref[0, 0] = a_ref[0].astype(BF16)

        @pl.when(half == 1)
        def _():
            _gather_finish(bufs, *sems)

    res = pl.pallas_call(
        body, name="cast_shards",
        grid_spec=pltpu.PrefetchScalarGridSpec(
            num_scalar_prefetch=1, grid=(2,),
            in_specs=[pl.BlockSpec((1, h, c), lambda s, chip_ref: (s, 0, 0)) for h, c in shapes] + [HBM_SPEC] * k,
            out_specs=[pl.BlockSpec((1, 1, h, c), lambda s, chip_ref: (chip_ref[0], s, 0, 0)) for h, c in shapes]
            + [HBM_SPEC] * k,
            scratch_shapes=_gather_sems(k)),
        out_shape=[pltpu.HBM((N_CHIP, 2, h, c), BF16) for h, c in shapes] + [_hbm_like(b) for b in gathered],
        input_output_aliases={1 + n + i: n + i for i in range(k)},
        compiler_params=_params(32),
    )(chip, *[_in_hbm(a.reshape(2, h, c)) for a, (h, c) in zip(arrays, shapes)], *gathered)
    return list(res[:n]), list(res[n:])


def _in_proj(x, gain1, w_in_t, exchange=None):
    tm = 256

    def body(x_ref, g_ref, w_ref, z_ref, hn_ref):
        xv = x_ref[...]
        hn = (xv * _rms_scale(xv) * g_ref[...]).astype(BF16)
        hn_ref[...] = hn
        z_ref[...] = _dot_nt(hn, w_ref[...])

    return _call(
        body, (x, gain1, w_in_t), grid=(S // tm,), name="in_proj",
        in_specs=[pl.BlockSpec((tm, D), lambda i: (i, 0)), pl.BlockSpec((1, D), lambda i: (0, 0)),
                  pl.BlockSpec((D_IN, D), lambda i: (0, 0))],
        out_specs=[pl.BlockSpec((tm, D_IN), lambda i: (i, 0)), pl.BlockSpec((tm, D), lambda i: (i, 0))],
        out_shape=[jax.ShapeDtypeStruct((S, D_IN), F32), jax.ShapeDtypeStruct((S, D), BF16)],
        compiler_params=_params(40), exchange=exchange)


def _gelu_parts(v):
    t = jnp.tanh(GELU_C * (v + 0.044715 * (v * v * v)))
    cdf = 0.5 * (1.0 + t)
    return cdf, t


def _band_mask(n):
    a = lax.broadcasted_iota(jnp.int32, (CHUNK, 2 * CHUNK), 0)
    j = lax.broadcasted_iota(jnp.int32, (CHUNK, 2 * CHUNK), 1)
    dist = CHUNK + a - j
    valid = (dist >= 0) & (dist < CHUNK)
    return valid & ((n > 0) | (j >= CHUNK))


def _fill_bias(bucket_ref, table_ref, bias_ref):
    bucket = bucket_ref[...]
    for h in range(N_HEAD):
        acc = jnp.zeros((CHUNK, 2 * CHUNK), F32)
        for b in range(N_BUCKET):
            acc = jnp.where(bucket == b, table_ref[b, h], acc)
        bias_ref[h] = acc


def _fill_tril(ws_ref, wt_ref, wtt_ref=None):
    r = lax.broadcasted_iota(jnp.int32, (CHUNK, CHUNK), 0)
    c = lax.broadcasted_iota(jnp.int32, (CHUNK, CHUNK), 1)
    for g in range(N_GROUP):
        w = jnp.where(c <= r, ws_ref[g], 0.0)
        wt_ref[g] = w.astype(BF16)
        if wtt_ref is not None:
            wtt_ref[g] = w.T.astype(BF16)


def _kv_layouts(kv_prev, kv_cur):
    both = jnp.concatenate([kv_prev, kv_cur], axis=0)
    k = both[:, :128]
    v = both[:, 128:]
    return (k.astype(BF16), pltpu.roll(k, 64, axis=1).astype(BF16),
            v.astype(BF16), pltpu.roll(v, 64, axis=1).astype(BF16))


def _head_place(h):
    pair, pos, kvh = h // 2, h % 2, h // 4
    return pair, pos, kvh == pos


def _softmax_sink(qm, k_use, bias_h, sink, valid):
    s = _dot_nt(qm, k_use) * QK_SCALE + bias_h
    s = jnp.where(valid, s, NEG_INF)
    m = jnp.maximum(jnp.max(s, axis=-1, keepdims=True), sink)
    e = jnp.exp(s - m)
    es = jnp.exp(sink - m)
    inv = 1.0 / (jnp.sum(e, axis=-1, keepdims=True) + es)
    return e * inv, es * inv


def _mixer_fwd(z, v_gain, w_spatial, b_spatial_t, sinks, rel_table, bucket, exchange=None):
    def body(z_ref, kvp_ref, gain_ref, ws_ref, bt_ref, sink_ref, table_ref, bucket_ref, out_ref, probs_ref, probs_t_ref,
             share_ref, guv_ref, dgelu_ref, bias_ref, wt_ref):
        n = pl.program_id(0)

        @pl.when(n == 0)
        def _():
            _fill_bias(bucket_ref, table_ref, bias_ref)
            _fill_tril(ws_ref, wt_ref)

        zuv = z_ref[:, :1024]
        cdf, t = _gelu_parts(zuv)
        guv = zuv * cdf
        guv_ref[...] = guv
        dgelu_ref[...] = cdf + zuv * (0.5 * (1.0 - t * t)) * (GELU_C * (1.0 + 3.0 * 0.044715 * (zuv * zuv)))
        for g in range(N_GROUP):
            vg = guv[:, 512 + 128 * g:512 + 128 * (g + 1)]
            vn = vg * _rms_scale(vg) * gain_ref[:, 128 * g:128 * (g + 1)]
            sv = _dot(wt_ref[g], vn.astype(BF16)) + bt_ref[:, g:g + 1]
            out_ref[:, 128 * g:128 * (g + 1)] = (guv[:, 128 * g:128 * (g + 1)] * sv).astype(BF16)

        k_same, k_swap, v_same, v_swap = _kv_layouts(kvp_ref[...], z_ref[:, 1536:1792])
        valid = _band_mask(n)
        lane = lax.broadcasted_iota(jnp.int32, (1, 128), 1)
        lane_half = lane // 64
        shares = jnp.zeros((CHUNK, 128), F32)
        for pair in range(N_HEAD // 2):
            qq = z_ref[:, 1024 + 128 * pair:1024 + 128 * (pair + 1)]
            acc = jnp.zeros((CHUNK, 128), F32)
            for pos in range(2):
                h = 2 * pair + pos
                _, _, same = _head_place(h)
                qm = jnp.where(lane_half == pos, qq, 0.0).astype(BF16)
                p, p_sink = _softmax_sink(qm, k_same if same else k_swap, bias_ref[h], sink_ref[h], valid)
                pb = p.astype(BF16)
                probs_ref[0, h] = pb
                probs_t_ref[0, h] = p.T.astype(BF16)
                shares = jnp.where(lane == h, p_sink, shares)
                vm = jnp.where(lane_half == pos, v_same if same else v_swap, jnp.zeros((), BF16))
                acc = acc + _dot(pb, vm)
            out_ref[:, 512 + 128 * pair:512 + 128 * (pair + 1)] = acc.astype(BF16)
        share_ref[...] = shares

    return _call(
        body, (z, z, v_gain, w_spatial, b_spatial_t, sinks, rel_table, bucket), grid=(N_BLOCK,), name="mixer_fwd",
        in_specs=[pl.BlockSpec((CHUNK, D_IN), lambda n: (n, 0)),
                  pl.BlockSpec((CHUNK, 256), lambda n: (jnp.maximum(n - 1, 0), 6)),
                  pl.BlockSpec((1, 512), lambda n: (0, 0)),
                  pl.BlockSpec((N_GROUP, CHUNK, CHUNK), lambda n: (0, 0, 0)),
                  pl.BlockSpec((CHUNK, N_GROUP), lambda n: (0, 0)),
                  pl.BlockSpec(memory_space=pltpu.SMEM),
                  pl.BlockSpec(memory_space=pltpu.SMEM),
                  pl.BlockSpec((CHUNK, 2 * CHUNK), lambda n: (0, 0))],
        out_specs=[pl.BlockSpec((CHUNK, D), lambda n: (n, 0)),
                   pl.BlockSpec((1, N_HEAD, CHUNK, 2 * CHUNK), lambda n: (n, 0, 0, 0)),
                   pl.BlockSpec((1, N_HEAD, 2 * CHUNK, CHUNK), lambda n: (n, 0, 0, 0)),
                   pl.BlockSpec((CHUNK, 128), lambda n: (n, 0)),
                   pl.BlockSpec((CHUNK, 1024), lambda n: (n, 0)), pl.BlockSpec((CHUNK, 1024), lambda n: (n, 0))],
        out_shape=[jax.ShapeDtypeStruct((S, D), BF16), jax.ShapeDtypeStruct((N_BLOCK, N_HEAD, CHUNK, 2 * CHUNK), BF16),
                   jax.ShapeDtypeStruct((N_BLOCK, N_HEAD, 2 * CHUNK, CHUNK), BF16), jax.ShapeDtypeStruct((S, 128), F32),
                   jax.ShapeDtypeStruct((S, 1024), F32), jax.ShapeDtypeStruct((S, 1024), F32)],
        scratch_shapes=[pltpu.VMEM((N_HEAD, CHUNK, 2 * CHUNK), F32), pltpu.VMEM((N_GROUP, CHUNK, CHUNK), BF16)],
        compiler_params=_params(32), exchange=exchange)


def _out_proj(x, mix, w_out, gain2, exchange=None):
    tm = 256

    def body(x_ref, mix_ref, w_ref, g_ref, h1_ref, hn_ref, hnt_ref):
        h1 = x_ref[...] + _dot(mix_ref[...], w_ref[...])
        h1_ref[...] = h1
        hn = h1 * _rms_scale(h1) * g_ref[...]
        hn_ref[...] = hn.astype(BF16)
        hnt_ref[...] = hn.T.astype(BF16)

    return _call(
        body, (x, mix, w_out, gain2), grid=(S // tm,), name="out_proj",
        in_specs=[pl.BlockSpec((tm, D), lambda i: (i, 0)), pl.BlockSpec((tm, D), lambda i: (i, 0)),
                  pl.BlockSpec((D, D), lambda i: (0, 0)), pl.BlockSpec((1, D), lambda i: (0, 0))],
        out_specs=[pl.BlockSpec((tm, D), lambda i: (i, 0)), pl.BlockSpec((tm, D), lambda i: (i, 0)),
                   pl.BlockSpec((D, tm), lambda i: (0, i))],
        out_shape=[jax.ShapeDtypeStruct((S, D), F32), jax.ShapeDtypeStruct((S, D), BF16),
                   jax.ShapeDtypeStruct((D, S), BF16)],
        compiler_params=_params(32), exchange=exchange)


def _ffn_up(hn2, w_ff1, exchange=None):
    tm = 512
    nj = D_FF // 1024

    def body(hn_ref, w1_ref, r_ref, a_ref, at_ref):
        r = jnp.maximum(_dot(hn_ref[...], w1_ref[0]), 0.0)
        r_ref[...] = r.astype(BF16)
        a = r * r
        a_ref[...] = a.astype(BF16)
        at_ref[...] = a.T.astype(BF16)

    return _call(
        body, (hn2, w_ff1), grid=(nj, S // tm), name="ffn_up",
        in_specs=[pl.BlockSpec((tm, D), lambda j, i: (i, 0)), pl.BlockSpec((1, D, 1024), lambda j, i: (j, 0, 0))],
        out_specs=[pl.BlockSpec((tm, 1024), lambda j, i: (i, j)), pl.BlockSpec((tm, 1024), lambda j, i: (i, j)),
                   pl.BlockSpec((1024, tm), lambda j, i: (j, i))],
        out_shape=[jax.ShapeDtypeStruct((S, D_FF), BF16), jax.ShapeDtypeStruct((S, D_FF), BF16),
                   jax.ShapeDtypeStruct((D_FF, S), BF16)],
        compiler_params=_params(40, 2), exchange=exchange)


def _ffn_down(h1, a, w_ff2, exchange=None):
    tm = 1024
    nj = D_FF // 1024

    def body(h1_ref, a_ref, w2_ref, h2_ref, acc_ref):
        j = pl.program_id(1)
        part = _dot(a_ref[...], w2_ref[0])

        @pl.when(j == 0)
        def _():
            acc_ref[...] = part

        @pl.when(j > 0)
        def _():
            acc_ref[...] += part

        @pl.when(j == nj - 1)
        def _():
            h2_ref[...] = h1_ref[...] + acc_ref[...]

    return _call(
        body, (h1, a, w_ff2), grid=(S // tm, nj), name="ffn_down",
        in_specs=[pl.BlockSpec((tm, D), lambda i, j: (i, 0)), pl.BlockSpec((tm, 1024), lambda i, j: (i, j)),
                  pl.BlockSpec((1, 1024, D), lambda i, j: (j, 0, 0))],
        out_specs=[pl.BlockSpec((tm, D), lambda i, j: (i, 0))],
        out_shape=[jax.ShapeDtypeStruct((S, D), F32)],
        scratch_shapes=[pltpu.VMEM((tm, D), F32)],
        compiler_params=_params(48, 2), exchange=exchange)


def _tail(h2, p, target, w_gate, w_proj, final_gain):
    tm = 256
    steps = S // tm

    def body(h2_ref, p_ref, t_ref, wg_ref, wp_ref, gf_ref, dh2_ref, dwg_ref, dwp_ref, dgf_ref, loss_ref, dh2b_ref,
             dwp_acc):
        i = pl.program_id(0)
        h2 = h2_ref[...]
        h2b = h2.astype(BF16)
        pb = p_ref[...].astype(BF16)
        gate = jax.nn.sigmoid(_dot(h2b, wg_ref[...]))
        pp = jnp.concatenate([_dot(pb, wp_ref[j]) for j in range(N_CHIP)], axis=1)
        h3 = h2 + gate * pp
        r3 = _rms_scale(h3)
        xhat = h3 * r3
        gf = gf_ref[...]
        err = xhat * gf - t_ref[...]
        dy = err * (1.0 / D)
        dh3 = _rms_bwd(dy * gf, xhat, r3)
        dgp = (dh3 * pp * gate * (1.0 - gate)).astype(BF16)
        dpp = (dh3 * gate).astype(BF16)
        dh2 = dh3 + _dot_nt(dgp, wg_ref[...])
        dh2_ref[...] = dh2
        dh2b_ref[...] = dh2.astype(BF16)
        dwg = _dot_tn(h2b, dgp)
        dwp = _dot_tn(pb, dpp)
        dgf = jnp.sum(dy * xhat, axis=0, keepdims=True)
        sq = jnp.sum(jnp.sum(err * err, axis=1, keepdims=True), axis=0, keepdims=True)

        @pl.when(i == 0)
        def _():
            dwg_ref[...] = dwg
            dwp_acc[...] = dwp
            dgf_ref[...] = dgf
            loss_ref[...] = jnp.broadcast_to(sq, (8, 128))

        @pl.when(i > 0)
        def _():
            dwg_ref[...] += dwg
            dwp_acc[...] += dwp
            dgf_ref[...] += dgf
            loss_ref[...] += jnp.broadcast_to(sq, (8, 128))

        @pl.when(i == steps - 1)
        def _():
            for j in range(N_CHIP):
                dwp_ref[j] = dwp_acc[:, 256 * j:256 * (j + 1)]

    return _call(
        body, (h2, p, target, w_gate, w_proj, final_gain), grid=(steps,), name="tail",
        in_specs=[pl.BlockSpec((tm, D), lambda i: (i, 0)), pl.BlockSpec((tm, PLE), lambda i: (i, 0)),
                  pl.BlockSpec((tm, D), lambda i: (i, 0)), pl.BlockSpec((D, D), lambda i: (0, 0)),
                  pl.BlockSpec((N_CHIP, PLE, 256), lambda i: (0, 0, 0)), pl.BlockSpec((1, D), lambda i: (0, 0))],
        out_specs=[pl.BlockSpec((tm, D), lambda i: (i, 0)), pl.BlockSpec((D, D), lambda i: (0, 0)),
                   pl.BlockSpec((N_CHIP, PLE, 256), lambda i: (0, 0, 0)), pl.BlockSpec((1, D), lambda i: (0, 0)),
                   pl.BlockSpec((8, 128), lambda i: (0, 0)), pl.BlockSpec((tm, D), lambda i: (i, 0))],
        out_shape=[jax.ShapeDtypeStruct((S, D), F32), jax.ShapeDtypeStruct((D, D), F32),
                   jax.ShapeDtypeStruct((N_CHIP, PLE, 256), F32), jax.ShapeDtypeStruct((1, D), F32),
                   jax.ShapeDtypeStruct((8, 128), F32), jax.ShapeDtypeStruct((S, D), BF16)],
        scratch_shapes=[pltpu.VMEM((PLE, D), F32)],
        compiler_params=_params(48))[0]


def _ffn_bwd_down(dh2b, r, a_t, w_ff2, exchange=None):
    tm = 1024
    nj = D_FF // 1024

    def body(dh2_ref, r_ref, at_ref, w2_ref, df_ref, dw2_ref):
        i = pl.program_id(1)
        dh2b = dh2_ref[...]
        da = _dot_nt(dh2b, w2_ref[0])
        df_ref[...] = (da * (2.0 * r_ref[...].astype(F32))).astype(BF16)
        dw2 = _dot(at_ref[...], dh2b)

        @pl.when(i == 0)
        def _():
            dw2_ref[0] = dw2

        @pl.when(i > 0)
        def _():
            dw2_ref[0] += dw2

    return _call(
        body, (dh2b, r, a_t, w_ff2), grid=(nj, S // tm), name="ffn_bwd_down",
        in_specs=[pl.BlockSpec((tm, D), lambda j, i: (i, 0)), pl.BlockSpec((tm, 1024), lambda j, i: (i, j)),
                  pl.BlockSpec((1024, tm), lambda j, i: (j, i)), pl.BlockSpec((1, 1024, D), lambda j, i: (j, 0, 0))],
        out_specs=[pl.BlockSpec((tm, 1024), lambda j, i: (i, j)), pl.BlockSpec((1, 1024, D), lambda j, i: (j, 0, 0))],
        out_shape=[jax.ShapeDtypeStruct((S, D_FF), BF16), jax.ShapeDtypeStruct((nj, 1024, D), F32)],
        compiler_params=_params(48, 2), exchange=exchange)


def _ffn_bwd_up(df, hn2_t, exchange=None):
    tm = 1024
    nj = D_FF // 1024

    def body(df_ref, hnt_ref, dw1_ref):
        i = pl.program_id(1)
        dw1 = _dot(hnt_ref[...], df_ref[...])

        @pl.when(i == 0)
        def _():
            dw1_ref[0] = dw1

        @pl.when(i > 0)
        def _():
            dw1_ref[0] += dw1

    return _call(
        body, (df, hn2_t), grid=(nj, S // tm), name="ffn_bwd_up",
        in_specs=[pl.BlockSpec((tm, 1024), lambda j, i: (i, j)), pl.BlockSpec((D, tm), lambda j, i: (0, i))],
        out_specs=[pl.BlockSpec((1, D, 1024), lambda j, i: (j, 0, 0))],
        out_shape=[jax.ShapeDtypeStruct((nj, D, 1024), F32)],
        compiler_params=_params(40, 2), exchange=exchange)


def _ffn_bwd_input(df, w_ff1, dh2, h1, gain2, mix, w_out, exchange=None):
    tm = 512
    nj = D_FF // 1024
    steps = S // tm

    def body(df_ref, w1_ref, dh2_ref, h1_ref, g_ref, mix_ref, wo_ref, dh1_ref, dmix_ref, dwo_ref, dg_ref, acc_ref):
        i = pl.program_id(0)
        j = pl.program_id(1)
        part = _dot_nt(df_ref[...], w1_ref[j])

        @pl.when(j == 0)
        def _():
            acc_ref[...] = part

        @pl.when(j > 0)
        def _():
            acc_ref[...] += part

        @pl.when(j == nj - 1)
        def _():
            dhn = acc_ref[...]
            h1 = h1_ref[...]
            r2 = _rms_scale(h1)
            xhat = h1 * r2
            dh1 = dh2_ref[...] + _rms_bwd(dhn * g_ref[...], xhat, r2)
            dh1_ref[...] = dh1
            dh1b = dh1.astype(BF16)
            dmix_ref[...] = _dot_nt(dh1b, wo_ref[...])
            dwo = _dot_tn(mix_ref[...], dh1b)
            dg = jnp.sum(dhn * xhat, axis=0, keepdims=True)

            @pl.when(i == 0)
            def _():
                dwo_ref[...] = dwo
                dg_ref[...] = dg

            @pl.when(i > 0)
            def _():
                dwo_ref[...] += dwo
                dg_ref[...] += dg

    return _call(
        body, (df, w_ff1, dh2, h1, gain2, mix, w_out), grid=(steps, nj), name="ffn_bwd_input",
        in_specs=[pl.BlockSpec((tm, 1024), lambda i, j: (i, j)),
                  pl.BlockSpec((nj, D, 1024), lambda i, j: (0, 0, 0), pipeline_mode=pl.Buffered(1)),
                  pl.BlockSpec((tm, D), lambda i, j: (i, 0)), pl.BlockSpec((tm, D), lambda i, j: (i, 0)),
                  pl.BlockSpec((1, D), lambda i, j: (0, 0)), pl.BlockSpec((tm, D), lambda i, j: (i, 0)),
                  pl.BlockSpec((D, D), lambda i, j: (0, 0), pipeline_mode=pl.Buffered(1))],
        out_specs=[pl.BlockSpec((tm, D), lambda i, j: (i, 0)), pl.BlockSpec((tm, D), lambda i, j: (i, 0)),
                   pl.BlockSpec((D, D), lambda i, j: (0, 0)), pl.BlockSpec((1, D), lambda i, j: (0, 0))],
        out_shape=[jax.ShapeDtypeStruct((S, D), F32), jax.ShapeDtypeStruct((S, D), F32),
                   jax.ShapeDtypeStruct((D, D), F32), jax.ShapeDtypeStruct((1, D), F32)],
        scratch_shapes=[pltpu.VMEM((tm, D), F32)],
        compiler_params=_params(56, 2), exchange=exchange)


IN_GROUP = 8


def _mixer_bwd(z, dmix, v_gain, w_spatial, b_spatial_t, saved, bucket, hn1, exchange=None):
    def body(z_ref, kvp_ref, dm_ref, gain_ref, ws_ref, bt_ref, probs_ref, probs_t_ref, share_ref, guv_ref, dgelu_ref,
             bucket_ref, hn_ref,
             dz_ref, dws_ref, db_ref, dgain_ref, dsink_ref, drel_ref, dwin_ref,
             wt_ref, wtt_ref, dbias_ref, dsv_ref, carry_ref):
        n = pl.program_id(0)

        @pl.when(n == 0)
        def _():
            _fill_tril(ws_ref, wt_ref, wtt_ref)
            dwin_ref[...] = jnp.zeros_like(dwin_ref)
            dbias_ref[...] = jnp.zeros_like(dbias_ref)
            dsv_ref[...] = jnp.zeros_like(dsv_ref)
            dws_ref[...] = jnp.zeros_like(dws_ref)
            dgain_ref[...] = jnp.zeros_like(dgain_ref)
            dsink_ref[...] = jnp.zeros_like(dsink_ref)

        rows = pl.ds(pl.multiple_of(n * CHUNK, CHUNK), CHUNK)

        guv = guv_ref[...]
        dgelu = dgelu_ref[...]
        for g in range(N_GROUP):
            lo, hi = 128 * g, 128 * (g + 1)
            u = guv[:, lo:hi]
            vg = guv[:, 512 + lo:512 + hi]
            rr = _rms_scale(vg)
            vhat = vg * rr
            gain = gain_ref[:, lo:hi]
            vnb = (vhat * gain).astype(BF16)
            sv = _dot(wt_ref[g], vnb) + bt_ref[:, g:g + 1]
            da = dm_ref[:, lo:hi]
            dsv = da * u
            dsvb = dsv.astype(BF16)
            dsv_ref[g] += dsv
            dws_ref[g] += _dot_nt(dsvb, vnb)
            dvn = _dot(wtt_ref[g], dsvb)
            dgain_ref[:, lo:hi] += jnp.sum(dvn * vhat, axis=0, keepdims=True)
            dvg = _rms_bwd(dvn * gain, vhat, rr)
            dz_ref[rows, lo:hi] = (da * sv * dgelu[:, lo:hi]).astype(BF16)
            dz_ref[rows, 512 + lo:512 + hi] = (dvg * dgelu[:, 512 + lo:512 + hi]).astype(BF16)

        k_same, k_swap, v_same, v_swap = _kv_layouts(kvp_ref[...], z_ref[:, 1536:1792])
        lane_half = lax.broadcasted_iota(jnp.int32, (1, 128), 1) // 64
        zero = jnp.zeros((2 * CHUNK, 128), F32)
        dk_same, dk_swap, dv_same, dv_swap = zero, zero, zero, zero
        for pair in range(N_HEAD // 2):
            cols = slice(1024 + 128 * pair, 1024 + 128 * (pair + 1))
            qq = z_ref[:, cols]
            do_pair = dm_ref[:, 512 + 128 * pair:512 + 128 * (pair + 1)]
            dq = jnp.zeros((CHUNK, 128), F32)
            for pos in range(2):
                h = 2 * pair + pos
                _, _, same = _head_place(h)
                on_half = lane_half == pos
                qm = jnp.where(on_half, qq, 0.0).astype(BF16)
                k_use = k_same if same else k_swap
                v_use = v_same if same else v_swap
                pb = probs_ref[0, h]
                p = pb.astype(F32)
                p_sink = share_ref[:, h:h + 1]
                dom = jnp.where(on_half, do_pair, 0.0).astype(BF16)
                dp = _dot_nt(dom, v_use)
                dsum = jnp.sum(p * dp, axis=-1, keepdims=True)
                ds = p * (dp - dsum)
                dbias_ref[h] += ds
                dsink_ref[h:h + 1, :] += jnp.broadcast_to(jnp.sum(-p_sink * dsum, axis=0, keepdims=True), (1, 128))
                dsb = ds.astype(BF16)
                dq = dq + jnp.where(on_half, _dot(dsb, k_use), 0.0)
                dk_h = _dot_tn(dsb, qm)
                dv_h = _dot(probs_t_ref[0, h], dom)
                if same:
                    dk_same, dv_same = dk_same + dk_h, dv_same + dv_h
                else:
                    dk_swap, dv_swap = dk_swap + dk_h, dv_swap + dv_h
            dz_ref[rows, cols] = (dq * QK_SCALE).astype(BF16)
        dk = (dk_same + pltpu.roll(dk_swap, 64, axis=1)) * QK_SCALE
        dv = dv_same + pltpu.roll(dv_swap, 64, axis=1)
        dkv = jnp.concatenate([dk, dv], axis=1)

        @pl.when(n > 0)
        def _():
            prev_rows = pl.ds(pl.multiple_of((n - 1) * CHUNK, CHUNK), CHUNK)
            dz_ref[prev_rows, 1536:1792] = (carry_ref[...] + dkv[:CHUNK]).astype(BF16)

        carry_ref[...] = dkv[CHUNK:]

        @pl.when((n > 0) & (n % IN_GROUP == 0))
        def _():
            done = pl.ds(pl.multiple_of((n - IN_GROUP) * CHUNK, IN_GROUP * CHUNK), IN_GROUP * CHUNK)
            dwin_ref[...] += _dot_tn(dz_ref[done, :], hn_ref[...])

        @pl.when(n == N_BLOCK - 1)
        def _():
            dz_ref[rows, 1536:1792] = dkv[CHUNK:].astype(BF16)
            last = pl.ds((N_BLOCK - IN_GROUP) * CHUNK, IN_GROUP * CHUNK)
            dwin_ref[...] += _dot_tn(dz_ref[last, :], hn_ref[...])
            r = lax.broadcasted_iota(jnp.int32, (CHUNK, CHUNK), 0)
            c = lax.broadcasted_iota(jnp.int32, (CHUNK, CHUNK), 1)
            for g in range(N_GROUP):
                dws_ref[g] = jnp.where(c <= r, dws_ref[g], 0.0)
                db_ref[g] = jnp.sum(dsv_ref[g], axis=1, keepdims=True)
            bucket = bucket_ref[...]
            for h in range(N_HEAD):
                dbh = dbias_ref[h]
                per_bucket = [jnp.sum(jnp.where(bucket == b, dbh, 0.0), axis=0, keepdims=True) for b in range(N_BUCKET)]
                drel_ref[h] = jnp.sum(jnp.concatenate(per_bucket, axis=0), axis=1, keepdims=True)

    def hn_group(n):
        return jnp.where(n == N_BLOCK - 1, N_BLOCK // IN_GROUP - 1, jnp.maximum(n // IN_GROUP - 1, 0))

    return _call(
        body, (z, z, dmix, v_gain, w_spatial, b_spatial_t, *saved, bucket, hn1), grid=(N_BLOCK,),
        name="mixer_bwd",
        in_specs=[pl.BlockSpec((CHUNK, D_IN), lambda n: (n, 0)),
                  pl.BlockSpec((CHUNK, 256), lambda n: (jnp.maximum(n - 1, 0), 6)),
                  pl.BlockSpec((CHUNK, D), lambda n: (n, 0)),
                  pl.BlockSpec((1, 512), lambda n: (0, 0)),
                  pl.BlockSpec((N_GROUP, CHUNK, CHUNK), lambda n: (0, 0, 0)),
                  pl.BlockSpec((CHUNK, N_GROUP), lambda n: (0, 0)),
                  pl.BlockSpec((1, N_HEAD, CHUNK, 2 * CHUNK), lambda n: (n, 0, 0, 0)),
                  pl.BlockSpec((1, N_HEAD, 2 * CHUNK, CHUNK), lambda n: (n, 0, 0, 0)),
                  pl.BlockSpec((CHUNK, 128), lambda n: (n, 0)),
                  pl.BlockSpec((CHUNK, 1024), lambda n: (n, 0)), pl.BlockSpec((CHUNK, 1024), lambda n: (n, 0)),
                  pl.BlockSpec((CHUNK, 2 * CHUNK), lambda n: (0, 0)),
                  pl.BlockSpec((IN_GROUP * CHUNK, D), lambda n: (hn_group(n), 0))],
        out_specs=[pl.BlockSpec((S, D_IN), lambda n: (0, 0)),
                   pl.BlockSpec((N_GROUP, CHUNK, CHUNK), lambda n: (0, 0, 0)),
                   pl.BlockSpec((N_GROUP, CHUNK, 1), lambda n: (0, 0, 0)),
                   pl.BlockSpec((1, 512), lambda n: (0, 0)),
                   pl.BlockSpec((N_HEAD, 128), lambda n: (0, 0)),
                   pl.BlockSpec((N_HEAD, N_BUCKET, 1), lambda n: (0, 0, 0)),
                   pl.BlockSpec((D_IN, D), lambda n: (0, 0))],
        out_shape=[jax.ShapeDtypeStruct((S, D_IN), BF16), jax.ShapeDtypeStruct((N_GROUP, CHUNK, CHUNK), F32),
                   jax.ShapeDtypeStruct((N_GROUP, CHUNK, 1), F32), jax.ShapeDtypeStruct((1, 512), F32),
                   jax.ShapeDtypeStruct((N_HEAD, 128), F32), jax.ShapeDtypeStruct((N_HEAD, N_BUCKET, 1), F32),
                   jax.ShapeDtypeStruct((D_IN, D), F32)],
        scratch_shapes=[pltpu.VMEM((N_GROUP, CHUNK, CHUNK), BF16),
                        pltpu.VMEM((N_GROUP, CHUNK, CHUNK), BF16), pltpu.VMEM((N_HEAD, CHUNK, 2 * CHUNK), F32),
                        pltpu.VMEM((N_GROUP, CHUNK, CHUNK), F32), pltpu.VMEM((CHUNK, 256), F32)],
        compiler_params=_params(56), exchange=exchange)


def _in_bwd_input(dz, w_in_t, x, dh1, gain1, exchange=None):
    tm = 512

    def body(dz_ref, w_ref, x_ref, dh1_ref, g_ref, dx_ref, dg_ref):
        i = pl.program_id(0)
        dhn = _dot(dz_ref[...], w_ref[...])
        xv = x_ref[...]
        r1 = _rms_scale(xv)
        xhat = xv * r1
        dx_ref[...] = dh1_ref[...] + _rms_bwd(dhn * g_ref[...], xhat, r1)
        dg = jnp.sum(dhn * xhat, axis=0, keepdims=True)

        @pl.when(i == 0)
        def _():
            dg_ref[...] = dg

        @pl.when(i > 0)
        def _():
            dg_ref[...] += dg

    return _call(
        body, (dz, w_in_t, x, dh1, gain1), grid=(S // tm,), name="in_bwd_input",
        in_specs=[pl.BlockSpec((tm, D_IN), lambda i: (i, 0)), pl.BlockSpec((D_IN, D), lambda i: (0, 0)),
                  pl.BlockSpec((tm, D), lambda i: (i, 0)), pl.BlockSpec((tm, D), lambda i: (i, 0)),
                  pl.BlockSpec((1, D), lambda i: (0, 0))],
        out_specs=[pl.BlockSpec((tm, D), lambda i: (i, 0)), pl.BlockSpec((1, D), lambda i: (0, 0))],
        out_shape=[jax.ShapeDtypeStruct((S, D), F32), jax.ShapeDtypeStruct((1, D), F32)],
        compiler_params=_params(48), exchange=exchange)


def _rel_bucket():
    a = jnp.arange(CHUNK)[:, None]
    j = jnp.arange(2 * CHUNK)[None, :]
    n = jnp.maximum(CHUNK + a - j, 0)
    max_exact = N_BUCKET // 2
    nf = jnp.maximum(n, 1).astype(jnp.float32)
    large = max_exact + (jnp.log(nf / max_exact) / math.log(CHUNK / max_exact) * (N_BUCKET - max_exact)).astype(jnp.int32)
    large = jnp.minimum(large, N_BUCKET - 1)
    return jnp.where(n < max_exact, n, large).astype(jnp.int32)


def _step(x, p, target, small, bufs, place):
    bucket = _rel_bucket()
    sinks = small["attn_sinks"].reshape(N_HEAD)
    b_t = jnp.transpose(small["b_spatial"].reshape(N_GROUP, CHUNK))
    ws = small["w_spatial"].reshape(N_GROUP, CHUNK, CHUNK)
    gain1, gain2 = small["norm1_gain"], small["norm2_gain"]
    v_gain = small["gmlp_v_gain"]
    final_gain = small["final_gain"].reshape(1, D)
    table = small["rel_bias_table"]
    bufs = dict(bufs)

    def gather(*names):
        return _RelayGather([bufs[n] for n in names])

    def took(names, got):
        bufs.update(zip(names, got))

    w_in_t = _whole(bufs["w_in"]).reshape(D_IN, D)
    (z, hn1), got = _in_proj(x, gain1, w_in_t, gather("w_out"))
    took(["w_out"], got)
    (mix, *saved), got = _mixer_fwd(z, v_gain, ws, b_t, sinks, table, bucket, gather("w_ff1"))
    took(["w_ff1"], got)
    w_out = _whole(bufs["w_out"]).reshape(D, D)
    (h1, hn2, hn2_t), _ = _out_proj(x, mix, w_out, gain2)
    w_ff1 = _whole(bufs["w_ff1"])
    (r, a, a_t), got = _ffn_up(hn2, w_ff1, gather("w_ff2"))
    took(["w_ff2"], got)
    w_ff2 = _whole(bufs["w_ff2"])
    (h2,), got = _ffn_down(h1, a, w_ff2, gather("w_ple_gate", "w_ple_proj"))
    took(["w_ple_gate", "w_ple_proj"], got)
    dh2, d_gate, d_proj, d_final, sq, dh2b = _tail(h2, p, target, _whole(bufs["w_ple_gate"]).reshape(D, D),
                                                   _whole(bufs["w_ple_proj"]), final_gain)

    def pair_sums(halves, from_sibling):
        sums, landing = zip(*[_pair_sum(g, o, place) for g, o in zip(halves, from_sibling)])
        return list(sums), list(landing)

    landed = {}
    halves = [_halves(d_gate.reshape(N_CHIP, 256, D)), _halves(d_proj)]
    (df, d_ff2), got = _ffn_bwd_down(dh2b, r, a_t, w_ff2, _SiblingExchange(halves))
    ex, halves = _ChipExchange(*pair_sums(halves, got)), [_halves(d_ff2)]
    (d_ff1,), got = _ffn_bwd_up(df, hn2_t, _Both(ex, _SiblingExchange(halves)))
    landed.update(zip(["w_ple_gate", "w_ple_proj"], got[:2]))
    ex, halves = _ChipExchange(*pair_sums(halves, got[2:])), [_halves(d_ff1)]
    (dh1, dmix, d_out, d_gain2), got = _ffn_bwd_input(df, w_ff1, dh2, h1, gain2, mix, w_out,
                                                      _Both(ex, _SiblingExchange(halves)))
    landed["w_ff2"] = got[0]
    ex, halves = _ChipExchange(*pair_sums(halves, got[1:])), [_halves(d_out.reshape(N_CHIP, 256, D))]
    (dz, d_ws, d_b, d_vgain, d_sink, d_rel, d_in_t), got = _mixer_bwd(z, dmix, v_gain, ws, b_t, saved, bucket, hn1,
                                                                     _Both(ex, _SiblingExchange(halves)))
    landed["w_ff1"] = got[0]
    small_grads = {
        "gmlp_v_gain": d_vgain, "w_spatial": d_ws.reshape(1, N_GROUP, CHUNK, CHUNK),
        "b_spatial": d_b.reshape(1, N_GROUP, CHUNK), "attn_sinks": d_sink[:, 0].reshape(1, N_HEAD),
        "rel_bias_table": jnp.transpose(d_rel.reshape(N_HEAD, N_BUCKET)), "norm2_gain": d_gain2,
        "final_gain": d_final.reshape(D),
    }
    ex, halves = _ChipExchange(*pair_sums(halves, got[1:])), [_halves(d_in_t.reshape(N_CHIP, 448, D))]
    (dx, small_grads["norm1_gain"]), got = _in_bwd_input(dz, w_in_t, x, dh1, gain1, _Both(ex, _SiblingExchange(halves)))
    landed["w_out"] = got[0]
    return dx, landed, _ChipExchange(*pair_sums(halves, got[1:])), small_grads, sq


HBM_SPEC = pl.BlockSpec(memory_space=pltpu.HBM)
VMEM_SPEC = pl.BlockSpec(memory_space=pltpu.VMEM)


def _mesh_place():
    x, y, c = lax.axis_index("x"), lax.axis_index("y"), lax.axis_index("c")
    others = [(1 - x, y), (x, 1 - y), (1 - x, 1 - y)]
    return x, y, c, others


def _remote(src, dst, send_sem, recv_sem, device):
    return pltpu.make_async_remote_copy(src_ref=src, dst_ref=dst, send_sem=send_sem, recv_sem=recv_sem,
                                        device_id=device, device_id_type=MESH)


def _hbm_like(a, shape=None, dtype=None):
    return pltpu.HBM(a.shape if shape is None else shape, a.dtype if dtype is None else dtype)


def _gather_start(bufs, send_sems, recv_sems):
    x, y, c, others = _mesh_place()
    me = 2 * x + y
    for w, buf in enumerate(bufs):
        for k in range(3):
            mine = buf.at[me, c]
            _remote(mine, mine, send_sems.at[w, k], recv_sems.at[w, k], (*others[k], c)).start()


def _gather_finish(bufs, send_sems, recv_sems):
    x, y, c, others = _mesh_place()
    me = 2 * x + y
    sibling = (x, y, 1 - c)
    idx = [2 * ox + oy for ox, oy in others]
    chips = range(3)
    for w, buf in enumerate(bufs):
        for k in chips:
            landed = buf.at[idx[k], c]
            _remote(landed, landed, send_sems.at[w, k], recv_sems.at[w, k], sibling).wait_recv()
            _remote(landed, landed, send_sems.at[w, 3 + k], recv_sems.at[w, 3 + k], sibling).start()
    for w, buf in enumerate(bufs):
        for k in chips:
            landed = buf.at[idx[k], 1 - c]
            _remote(landed, landed, send_sems.at[w, 3 + k], recv_sems.at[w, 3 + k], sibling).wait_recv()
    for w, buf in enumerate(bufs):
        for k in chips:
            mine, passed = buf.at[me, c], buf.at[idx[k], c]
            _remote(mine, mine, send_sems.at[w, k], recv_sems.at[w, k], sibling).wait_send()
            _remote(passed, passed, send_sems.at[w, 3 + k], recv_sems.at[w, 3 + k], sibling).wait_send()


def _gather_sems(n):
    return [pltpu.SemaphoreType.DMA((n, 6)), pltpu.SemaphoreType.DMA((n, 6))]


def _sibling_copies(grads, landing, send_sems, recv_sems):
    x, y, c, _ = _mesh_place()
    return [_remote(grads[w].at[j, 1 - c], landing[w].at[j], send_sems.at[w, j], recv_sems.at[w, j], (x, y, 1 - c))
            for w in range(len(grads)) for j in range(N_CHIP)]


def _sibling_exchange_start(grads, landing, send_sems, recv_sems):
    for cp in _sibling_copies(grads, landing, send_sems, recv_sems):
        cp.start()


def _sibling_exchange_finish(grads, landing, send_sems, recv_sems):
    copies = _sibling_copies(grads, landing, send_sems, recv_sems)
    for cp in copies:
        cp.wait_recv()
    for cp in copies:
        cp.wait_send()


def _sibling_exchange_sems(n):
    return [pltpu.SemaphoreType.DMA((n, N_CHIP)), pltpu.SemaphoreType.DMA((n, N_CHIP))]


def _sibling_exchange(grads):
    n = len(grads)

    def body(*refs):
        ins, outs = refs[:n], refs[n:2 * n]
        _sibling_exchange_start(ins, outs, *refs[2 * n:])
        _sibling_exchange_finish(ins, outs, *refs[2 * n:])

    return pl.pallas_call(
        body, name="sibling_exchange",
        in_specs=[HBM_SPEC] * n, out_specs=[HBM_SPEC] * n,
        out_shape=[_hbm_like(g, (N_CHIP,) + g.shape[2:]) for g in grads],
        scratch_shapes=_sibling_exchange_sems(n),
    )(*[_in_hbm(g) for g in grads])


def _chip_exchange_start(sums, landing, send_sems, recv_sems):
    x, y, c, others = _mesh_place()
    me = 2 * x + y
    for w in range(len(sums)):
        for k, (ox, oy) in enumerate(others):
            _remote(sums[w].at[2 * ox + oy], landing[w].at[me], send_sems.at[w, k], recv_sems.at[w, k],
                    (ox, oy, c)).start()


def _chip_exchange_finish(sums, landing, send_sems, recv_sems):
    x, y, c, others = _mesh_place()
    for w in range(len(sums)):
        for k, (ox, oy) in enumerate(others):
            piece = landing[w].at[2 * ox + oy]
            _remote(piece, piece, send_sems.at[w, k], recv_sems.at[w, k], (x, y, c)).wait_recv()
    for w in range(len(sums)):
        for k, (ox, oy) in enumerate(others):
            piece = sums[w].at[2 * ox + oy]
            _remote(piece, piece, send_sems.at[w, k], recv_sems.at[w, k], (x, y, c)).wait_send()


def _chip_exchange_sems(n):
    return [pltpu.SemaphoreType.DMA((n, 3)), pltpu.SemaphoreType.DMA((n, 3))]


def _sibling_allgather(bufs, also):
    n = len(bufs)
    k_in, k_out = len(also.operands), also.n_out

    def body(*refs):
        ex_ins, refs = refs[n:n + k_in], refs[n + k_in:]
        outs, refs = refs[:n], refs[n:]
        ex_outs, refs = refs[:k_out], refs[k_out:]
        send_sems, recv_sems, ex_sems = refs[0], refs[1], refs[2:]
        x, y, c, _ = _mesh_place()
        sibling = (x, y, 1 - c)
        also.start(ex_ins, ex_outs, ex_sems)
        sends = [_remote(outs[w].at[c], outs[w].at[c], send_sems.at[w], recv_sems.at[w], sibling) for w in range(n)]
        for cp in sends:
            cp.start()
        for w in range(n):
            landed = outs[w].at[1 - c]
            _remote(landed, landed, send_sems.at[w], recv_sems.at[w], sibling).wait_recv()
        for cp in sends:
            cp.wait_send()
        also.finish(ex_ins, ex_outs, ex_sems)

    res = pl.pallas_call(
        body, name="sibling_allgather",
        in_specs=[HBM_SPEC] * (n + k_in), out_specs=[HBM_SPEC] * (n + k_out),
        out_shape=[_hbm_like(b) for b in bufs] + also.out_shape,
        input_output_aliases={**{w: w for w in range(n)}, **{n + i: n + o for i, o in also.aliases.items()}},
        scratch_shapes=[pltpu.SemaphoreType.DMA((n,)), pltpu.SemaphoreType.DMA((n,))] + also.sems,
    )(*bufs, *[_in_hbm(o) for o in also.operands])
    return list(res[:n]), list(res[n:])


def _pair_sum(grad, other, place):
    _, _, h, cols = grad.shape
    tr = _row_tile(h)

    def body(place_ref, g_ref, o_ref, sums_ref, own_ref):
        s = (g_ref[0, 0] + o_ref[0]).astype(BF16)
        sums_ref[0] = s

        @pl.when(pl.program_id(1) == place_ref[0])
        def _():
            own_ref[0] = s

    return pl.pallas_call(
        body, name="pair_sum",
        grid_spec=pltpu.PrefetchScalarGridSpec(
            num_scalar_prefetch=1, grid=(h // tr, N_CHIP),
            in_specs=[pl.BlockSpec((1, 1, tr, cols), lambda r, j, place_ref: (j, place_ref[1], r, 0)),
                      pl.BlockSpec((1, tr, cols), lambda r, j, place_ref: (j, r, 0))],
            out_specs=[pl.BlockSpec((1, tr, cols), lambda r, j, place_ref: (j, r, 0)),
                       pl.BlockSpec((1, tr, cols), lambda r, j, place_ref: (place_ref[0], r, 0))]),
        out_shape=[pltpu.HBM((N_CHIP, h, cols), BF16)] * 2,
        compiler_params=_params(32, 2),
    )(place, _in_hbm(grad), _in_hbm(other))


def _chip_sum(parts, place):
    _, h, cols = parts.shape
    tr = _row_tile(h)

    def body(place_ref, p_ref, out_ref):
        out_ref[0] = ((p_ref[0].astype(F32) + p_ref[1].astype(F32)) + p_ref[2].astype(F32)) + p_ref[3].astype(F32)

    return pl.pallas_call(
        body, name="chip_sum",
        grid_spec=pltpu.PrefetchScalarGridSpec(
            num_scalar_prefetch=1, grid=(h // tr,),
            in_specs=[pl.BlockSpec((N_CHIP, tr, cols), lambda r, place_ref: (0, r, 0))],
            out_specs=pl.BlockSpec((1, tr, cols), lambda r, place_ref: (place_ref[1], r, 0))),
        out_shape=pltpu.HBM((2, h, cols), F32),
        compiler_params=_params(32),
    )(place, _in_hbm(parts))


def _adamw_math(w, g, m, v):
    m = ADAM_B1 * m + (1.0 - ADAM_B1) * g
    v = ADAM_B2 * v + (1.0 - ADAM_B2) * (g * g)
    m_hat = m / (1.0 - ADAM_B1 ** ADAM_STEP)
    v_hat = v / (1.0 - ADAM_B2 ** ADAM_STEP)
    delta = -ADAM_LR * (m_hat / (jnp.sqrt(v_hat) + ADAM_EPS) + ADAM_WD * w)
    return delta, m, v


def _adamw(w, g, m, v, exchange=None):
    rows, cols = w.shape
    tr = _row_tile(rows)

    def body(w_ref, g_ref, m_ref, v_ref, d_ref, nm_ref, nv_ref, g_out_ref):
        g = g_ref[...]
        d_ref[...], nm_ref[...], nv_ref[...] = _adamw_math(w_ref[...], g, m_ref[...], v_ref[...])
        g_out_ref[...] = g

    spec = pl.BlockSpec((tr, cols), lambda r: (r, 0))
    return _call(
        body, (w, g, m, v), grid=(rows // tr,), name="adamw",
        in_specs=[spec] * 4, out_specs=[spec] * 4,
        out_shape=[jax.ShapeDtypeStruct((rows, cols), F32)] * 4,
        compiler_params=_params(48), exchange=exchange)


SMALL_NAMES = ("norm1_gain", "gmlp_v_gain", "w_spatial", "b_spatial", "attn_sinks", "rel_bias_table", "norm2_gain",
               "final_gain")
PACK_TILE = 8 * 128


def _pack_small(arrays):
    parts = []
    for a in arrays:
        flat = a.reshape(-1)
        rows = -(-flat.shape[0] // PACK_TILE) * 8
        parts.append(jnp.pad(flat, (0, rows * 128 - flat.shape[0])).reshape(rows, 128))
    return jnp.concatenate(parts, axis=0)


def _unpack_small(packed, like):
    out, row = [], 0
    for a in like:
        size = math.prod(a.shape)
        rows = -(-size // PACK_TILE) * 8
        out.append(packed[row:row + rows].reshape(-1)[:size].reshape(a.shape))
        row += rows
    return out


def _small_update(gathered, w, m, v):
    rows = gathered.shape[1]

    def body(g_ref, w_ref, m_ref, v_ref, tot_ref, d_ref, nm_ref, nv_ref):
        total = g_ref[0].astype(F32)
        for dev in range(1, 8):
            total = total + g_ref[dev].astype(F32)
        tot_ref[...] = total
        d_ref[...], nm_ref[...], nv_ref[...] = _adamw_math(w_ref[...], total, m_ref[...], v_ref[...])

    return pl.pallas_call(
        body, name="small_update",
        in_specs=[VMEM_SPEC] * 4, out_specs=[VMEM_SPEC] * 4,
        out_shape=[jax.ShapeDtypeStruct((rows, 128), F32)] * 4,
        compiler_params=pltpu.CompilerParams(vmem_limit_bytes=24 * MIB),
    )(gathered, w, m, v)


def _halves(a):
    return a.reshape(a.shape[:-2] + (2, a.shape[-2] // 2, a.shape[-1]))


def _whole(a):
    return a.reshape(a.shape[:-3] + (2 * a.shape[-2], a.shape[-1]))


def kernel(x, p, norm1_gain, w_in, gmlp_v_gain, w_spatial, b_spatial, attn_sinks, rel_bias_table, w_out, norm2_gain, w_ff1, w_ff2, w_ple_proj, w_ple_gate, final_gain, loss_target, m_norm1_gain, m_w_in, m_gmlp_v_gain, m_w_spatial, m_b_spatial, m_attn_sinks, m_rel_bias_table, m_w_out, m_norm2_gain, m_w_ff1, m_w_ff2, m_w_ple_proj, m_w_ple_gate, m_final_gain, v_norm1_gain, v_w_in, v_gmlp_v_gain, v_w_spatial, v_b_spatial, v_attn_sinks, v_rel_bias_table, v_w_out, v_norm2_gain, v_w_ff1, v_w_ff2, v_w_ple_proj, v_w_ple_gate, v_final_gain):
    given = dict(locals())
    small = {n: given[n] for n in SMALL_NAMES}
    chip = 2 * lax.axis_index("x") + lax.axis_index("y")
    place = jnp.stack([chip, lax.axis_index("c")]).astype(jnp.int32)

    big_names = ("w_in", "w_out", "w_ff1", "w_ff2", "w_ple_proj", "w_ple_gate")
    shards = {n: given[n][0] for n in big_names}
    travel = dict(shards, w_in=jnp.transpose(shards["w_in"]))
    rest = [n for n in big_names if n != "w_in"]
    cast, gathered = _cast_shards_beside_gather([travel[n] for n in rest], place[:1],
                                                [_cast_shard(travel["w_in"], place[:1])])
    bufs = dict(zip(rest + ["w_in"], cast + gathered))
    dx, landed, exchange_in, small_grads, sq = _step(x[0], p[0, 0], loss_target[0], small, bufs, place)

    out_grad, out_delta, out_m, out_v = {}, {}, {}, {}

    def update(n, g, exchange=None):
        to = jnp.transpose if n == "w_in" else (lambda a: a)
        (delta, new_m, new_v, g_out), got = _adamw(to(shards[n]), g, to(given["m_" + n][0]), to(given["v_" + n][0]),
                                                   exchange)
        out_grad[n], out_delta[n], out_m[n], out_v[n] = [to(a)[None] for a in (g_out, delta, new_m, new_v)]
        return got

    spare = jnp.zeros((8, 128), F32)
    small_packed = _pack_small([small_grads[n] for n in SMALL_NAMES] + [spare]).astype(BF16)
    early = [n for n in big_names if n != "w_in"]
    reduced, (small_gathered, sq_gathered, landed_in) = _sibling_allgather(
        [_chip_sum(landed[n], place) for n in early], _Both(_Both(_GatherAll(small_packed), _GatherAll(sq)), exchange_in))
    for n, r in zip(early, reduced):
        update(n, _whole(r))
    (reduced_in,), _ = _sibling_allgather([_chip_sum(landed_in, place)], _Nothing())
    update("w_in", _whole(reduced_in))

    like = [given[n] for n in SMALL_NAMES] + [spare]
    packed = _small_update(small_gathered, *[_pack_small([given[pre + n] for n in SMALL_NAMES] + [spare])
                                             for pre in ("", "m_", "v_")])
    for res, out in zip(packed, (out_grad, out_delta, out_m, out_v)):
        out.update(zip(SMALL_NAMES, _unpack_small(res, like)))
    loss = 0.5 * jnp.sum(sq_gathered[:, 0, 0]) / D

    order = ("norm1_gain", "w_in", "gmlp_v_gain", "w_spatial", "b_spatial", "attn_sinks", "rel_bias_table", "w_out",
             "norm2_gain", "w_ff1", "w_ff2", "w_ple_proj", "w_ple_gate", "final_gain")
    return (loss, dx[None], *[out_grad[n] for n in order], *[out_delta[n] for n in order],
            *[out_m[n] for n in order], *[out_v[n] for n in order])
```

```python
import functools
import math

import jax
import jax.numpy as jnp
from jax import lax
from jax.experimental import pallas as pl
from jax.experimental.pallas import tpu as pltpu

S = 2048
D = 1024
D_IN = 1792
D_FF = 4096
PLE = 256
N_CHIP = 4
N_GROUP = 4
CHUNK = 128
N_HEAD = 8
N_BLOCK = S // CHUNK
N_BUCKET = 32
EPS = 1e-6
NEG_INF = -1e30
QK_SCALE = 0.125
GELU_C = math.sqrt(2.0 / math.pi)

ADAM_LR = 0.001
ADAM_B1 = 0.9
ADAM_B2 = 0.999
ADAM_EPS = 1e-08
ADAM_WD = 0.01
ADAM_STEP = 10

F32 = jnp.float32
BF16 = jnp.bfloat16
MIB = 1024 * 1024
MESH = pl.DeviceIdType.MESH

NT = (((1,), (1,)), ((), ()))
TN = (((0,), (0,)), ((), ()))


def _dot(a, b):
    return jnp.dot(a, b, preferred_element_type=F32)


def _dot_nt(a, b):
    return lax.dot_general(a, b, NT, preferred_element_type=F32)


def _dot_tn(a, b):
    return lax.dot_general(a, b, TN, preferred_element_type=F32)


def _params(vmem_mib, n_axes=1):
    return pltpu.CompilerParams(dimension_semantics=("arbitrary",) * n_axes, vmem_limit_bytes=vmem_mib * MIB)


def _rms_scale(v):
    return lax.rsqrt(jnp.mean(v * v, axis=-1, keepdims=True) + EPS)


def _rms_bwd(dy_gain, xhat, r):
    return r * (dy_gain - xhat * jnp.mean(dy_gain * xhat, axis=-1, keepdims=True))


class _Gather:
    def __init__(self, bufs):
        self.operands = list(bufs)
        self.n_out = len(self.operands)
        self.out_shape = [_hbm_like(b) for b in bufs]
        self.aliases = {w: w for w in range(self.n_out)}
        self.sems = _gather_sems(self.n_out)

    def start(self, ins, outs, sems):
        _gather_start(outs, *sems)

    def finish(self, ins, outs, sems):
        _gather_finish(outs, *sems)


class _RelayGather(_Gather):
    TOP, BOTTOM = 6, 7
    DIAGONAL_PASSED = 5
    MIDDLE_AT, LATE_AT = (5, 8), (7, 8)

    def __init__(self, bufs):
        super().__init__(bufs)
        self.sems = [pltpu.SemaphoreType.DMA((self.n_out, 8)), pltpu.SemaphoreType.DMA((self.n_out, 8))]

    def _copies(self, bufs, send_sems, recv_sems):
        x, y, c, others = _mesh_place()
        me = 2 * x + y
        idx = [2 * ox + oy for ox, oy in others]
        sibling = (x, y, 1 - c)
        direct, passed, relayed = [], [], []
        for w, buf in enumerate(bufs):
            rows = buf.shape[2] // 2
            upper, lower = pl.ds(0, rows), pl.ds(rows, rows)
            for k in (0, 1):
                mine = buf.at[me, c]
                direct.append((_remote(mine, mine, send_sems.at[w, k], recv_sems.at[w, k], (*others[k], c)),
                               buf.at[idx[k], c], w, k))
            for k in (0, 1, 2):
                here = buf.at[idx[k], c]
                passed.append((_remote(here, here, send_sems.at[w, 3 + k], recv_sems.at[w, 3 + k], sibling),
                               buf.at[idx[k], 1 - c], w, 3 + k))
            from_x, from_y = buf.at[idx[0], c, upper], buf.at[idx[1], c, lower]
            relayed.append((_remote(from_x, from_x, send_sems.at[w, self.TOP], recv_sems.at[w, self.TOP],
                                    (*others[1], c)), buf.at[idx[2], c, upper], w, self.TOP))
            relayed.append((_remote(from_y, from_y, send_sems.at[w, self.BOTTOM], recv_sems.at[w, self.BOTTOM],
                                    (*others[0], c)), buf.at[idx[2], c, lower], w, self.BOTTOM))
        return direct, passed, relayed

    @staticmethod
    def _landed(piece, send_sems, recv_sems, w, col):
        x, y, c, _ = _mesh_place()
        _remote(piece, piece, send_sems.at[w, col], recv_sems.at[w, col], (x, y, c)).wait_recv()

    def start(self, ins, outs, sems):
        for cp, _, _, _ in self._copies(outs, *sems)[0]:
            cp.start()

    def middle(self, ins, outs, sems):
        direct, passed, relayed = self._copies(outs, *sems)
        for _, piece, w, col in direct:
            self._landed(piece, *sems, w, col)
        for cp, _, _, col in passed:
            if col != self.DIAGONAL_PASSED:
                cp.start()
        for cp, _, _, _ in relayed:
            cp.start()

    def late(self, ins, outs, sems):
        direct, passed, relayed = self._copies(outs, *sems)
        for _, piece, w, col in relayed:
            self._landed(piece, *sems, w, col)
        for cp, _, _, col in passed:
            if col == self.DIAGONAL_PASSED:
                cp.start()

    def finish(self, ins, outs, sems):
        direct, passed, relayed = self._copies(outs, *sems)
        for _, piece, w, col in passed:
            self._landed(piece, *sems, w, col)
        for cp, _, _, _ in direct + passed + relayed:
            cp.wait_send()


class _ChipExchange:
    def __init__(self, sums, landing, chips=(0, 1, 2)):
        self.n_out = len(landing)
        self.chips = chips
        self.operands = list(sums) + list(landing)
        self.out_shape = [_hbm_like(b) for b in landing]
        self.aliases = {self.n_out + w: w for w in range(self.n_out)}
        self.sems = _chip_exchange_sems(self.n_out)

    def start(self, ins, outs, sems):
        _chip_exchange_start(ins[:self.n_out], outs, *sems, chips=self.chips)

    def finish(self, ins, outs, sems):
        _chip_exchange_finish(ins[:self.n_out], outs, *sems, chips=self.chips)


class _GatherAll:
    def __init__(self, packed):
        self.operands = [packed]
        self.n_out = 1
        self.out_shape = [_hbm_like(packed, (8,) + packed.shape)]
        self.aliases = {}
        self.sems = [pltpu.SemaphoreType.DMA((8,)), pltpu.SemaphoreType.DMA((8,))]

    def _copies(self, ins, outs, sems):
        x, y, c, _ = _mesh_place()
        me = 4 * x + 2 * y + c
        send_sems, recv_sems = sems
        copies = []
        for k in range(1, 8):
            peer = (1 - x if k // 4 else x, 1 - y if (k // 2) % 2 else y, 1 - c if k % 2 else c)
            src = 4 * peer[0] + 2 * peer[1] + peer[2]
            copies.append((_remote(ins[0], outs[0].at[me], send_sems.at[k], recv_sems.at[k], peer), outs[0].at[src]))
        own = pltpu.make_async_copy(ins[0], outs[0].at[me], send_sems.at[0])
        return own, copies

    def start(self, ins, outs, sems):
        own, copies = self._copies(ins, outs, sems)
        own.start()
        for cp, _ in copies:
            cp.start()

    def finish(self, ins, outs, sems):
        own, copies = self._copies(ins, outs, sems)
        x, y, c, _ = _mesh_place()
        for k, (cp, landed) in enumerate(copies):
            _remote(landed, landed, sems[0].at[k + 1], sems[1].at[k + 1], (x, y, c)).wait_recv()
        for cp, _ in copies:
            cp.wait_send()
        own.wait()


class _Nothing:
    operands, n_out, out_shape, aliases, sems = [], 0, [], {}, []

    def start(self, ins, outs, sems):
        pass

    def finish(self, ins, outs, sems):
        pass


class _Both:
    def __init__(self, a, b):
        self.a, self.b = a, b
        self.operands = a.operands + b.operands
        self.n_out = a.n_out + b.n_out
        self.out_shape = a.out_shape + b.out_shape
        self.aliases = dict(a.aliases)
        self.aliases.update({len(a.operands) + i: a.n_out + o for i, o in b.aliases.items()})
        self.sems = a.sems + b.sems

    def _split(self, ins, outs, sems):
        ka, na, sa = len(self.a.operands), self.a.n_out, len(self.a.sems)
        return (ins[:ka], outs[:na], sems[:sa]), (ins[ka:], outs[na:], sems[sa:])

    def start(self, ins, outs, sems):
        for ex, args in zip((self.a, self.b), self._split(ins, outs, sems)):
            ex.start(*args)

    def finish(self, ins, outs, sems):
        for ex, args in zip((self.a, self.b), self._split(ins, outs, sems)):
            ex.finish(*args)


class _SiblingExchange:
    def __init__(self, grads):
        self.operands = list(grads)
        self.n_out = len(self.operands)
        self.out_shape = [_hbm_like(g, (N_CHIP,) + g.shape[2:]) for g in grads]
        self.aliases = {}
        self.sems = _sibling_exchange_sems(self.n_out)

    def start(self, ins, outs, sems):
        _sibling_exchange_start(ins, outs, *sems)

    def finish(self, ins, outs, sems):
        _sibling_exchange_finish(ins, outs, *sems)


def _call(body, operands, *, grid, in_specs, out_specs, out_shape, name, compiler_params, scratch_shapes=(),
          exchange=None):
    operands = [o if getattr(spec, "memory_space", None) == pltpu.SMEM else _in_hbm(o)
                for o, spec in zip(operands, in_specs)]
    out_shape = [pltpu.HBM(s.shape, s.dtype) for s in out_shape]
    if exchange is None:
        res = pl.pallas_call(body, grid=grid, in_specs=in_specs, out_specs=out_specs, out_shape=out_shape, name=name,
                             scratch_shapes=list(scratch_shapes), compiler_params=compiler_params)(*operands)
        return list(res), []
    n_in, n_out, n_scr = len(in_specs), len(out_specs), len(scratch_shapes)
    k_in, k_out = len(exchange.operands), exchange.n_out

    def fused(*refs):
        ins, refs = refs[:n_in], refs[n_in:]
        ex_ins, refs = refs[:k_in], refs[k_in:]
        outs, refs = refs[:n_out], refs[n_out:]
        ex_outs, refs = refs[:k_out], refs[k_out:]
        scratch, sems = refs[:n_scr], refs[n_scr:]
        ids = [pl.program_id(a) for a in range(len(grid))]
        first = functools.reduce(jnp.logical_and, [i == 0 for i in ids])
        last = functools.reduce(jnp.logical_and, [i == g - 1 for i, g in zip(ids, grid)])

        @pl.when(first)
        def _():
            exchange.start(ex_ins, ex_outs, sems)

        def at_step(numerator, denominator):
            at = (numerator * math.prod(grid)) // denominator
            place = [(at // math.prod(grid[a + 1:])) % grid[a] for a in range(len(grid))]
            return functools.reduce(jnp.logical_and, [i == p for i, p in zip(ids, place)])

        if hasattr(exchange, "middle"):
            @pl.when(at_step(*exchange.MIDDLE_AT))
            def _():
                exchange.middle(ex_ins, ex_outs, sems)

            @pl.when(at_step(*exchange.LATE_AT))
            def _():
                exchange.late(ex_ins, ex_outs, sems)

        body(*ins, *outs, *scratch)

        @pl.when(last)
        def _():
            exchange.finish(ex_ins, ex_outs, sems)

    res = pl.pallas_call(
        fused, grid=grid, name=name,
        in_specs=list(in_specs) + [HBM_SPEC] * k_in, out_specs=list(out_specs) + [HBM_SPEC] * k_out,
        out_shape=list(out_shape) + exchange.out_shape,
        input_output_aliases={n_in + i: n_out + o for i, o in exchange.aliases.items()},
        scratch_shapes=list(scratch_shapes) + exchange.sems, compiler_params=compiler_params,
    )(*operands, *[_in_hbm(o) for o in exchange.operands])
    return list(res[:n_out]), list(res[n_out:])


def _in_hbm(a):
    return pltpu.with_memory_space_constraint(a, pltpu.HBM)


def _row_tile(h):
    return max(t for t in range(16, 513, 16) if h % t == 0)


def _cast_shard(a, chip):
    rows, cols = a.shape
    h = rows // 2
    tr = _row_tile(h)

    def body(chip_ref, a_ref, o_ref):
        o_ref[0, 0] = a_ref[0].astype(BF16)

    return pl.pallas_call(
        body, name="cast_shard",
        grid_spec=pltpu.PrefetchScalarGridSpec(
            num_scalar_prefetch=1, grid=(2, h // tr),
            in_specs=[pl.BlockSpec((1, tr, cols), lambda s, r, chip_ref: (s, r, 0))],
            out_specs=pl.BlockSpec((1, 1, tr, cols), lambda s, r, chip_ref: (chip_ref[0], s, r, 0))),
        out_shape=pltpu.HBM((N_CHIP, 2, h, cols), BF16),
        compiler_params=_params(16, 2),
    )(chip, _in_hbm(a.reshape(2, h, cols)))


def _cast_shards_beside_gather(arrays, chip, gathered):
    n, k = len(arrays), len(gathered)
    shapes = [(a.shape[0] // 2, a.shape[1]) for a in arrays]

    def body(chip_ref, *refs):
        ins, refs = refs[:n], refs[n + k:]
        outs, refs = refs[:n], refs[n:]
        bufs, sems = refs[:k], refs[k:]
        half = pl.program_id(0)

        @pl.when(half == 0)
        def _():
            _gather_start(bufs, *sems)

        for a_ref, o_ref in zip(ins, outs):
            o_ref[0, 0] = a_ref[0].astype(BF16)

        @pl.when(half == 1)
        def _():
            _gather_finish(bufs, *sems)

    res = pl.pallas_call(
        body, name="cast_shards",
        grid_spec=pltpu.PrefetchScalarGridSpec(
            num_scalar_prefetch=1, grid=(2,),
            in_specs=[pl.BlockSpec((1, h, c), lambda s, chip_ref: (s, 0, 0)) for h, c in shapes] + [HBM_SPEC] * k,
            out_specs=[pl.BlockSpec((1, 1, h, c), lambda s, chip_ref: (chip_ref[0], s, 0, 0)) for h, c in shapes]
            + [HBM_SPEC] * k,
            scratch_shapes=_gather_sems(k)),
        out_shape=[pltpu.HBM((N_CHIP, 2, h, c), BF16) for h, c in shapes] + [_hbm_like(b) for b in gathered],
        input_output_aliases={1 + n + i: n + i for i in range(k)},
        compiler_params=_params(32),
    )(chip, *[_in_hbm(a.reshape(2, h, c)) for a, (h, c) in zip(arrays, shapes)], *gathered)
    return list(res[:n]), list(res[n:])


def _in_proj(x, gain1, w_in_t, exchange=None):
    tm = 256

    def body(x_ref, g_ref, w_ref, z_ref, hn_ref):
        xv = x_ref[...]
        hn = (xv * _rms_scale(xv) * g_ref[...]).astype(BF16)
        hn_ref[...] = hn
        z_ref[...] = _dot_nt(hn, w_ref[...])

    return _call(
        body, (x, gain1, w_in_t), grid=(S // tm,), name="in_proj",
        in_specs=[pl.BlockSpec((tm, D), lambda i: (i, 0)), pl.BlockSpec((1, D), lambda i: (0, 0)),
                  pl.BlockSpec((D_IN, D), lambda i: (0, 0))],
        out_specs=[pl.BlockSpec((tm, D_IN), lambda i: (i, 0)), pl.BlockSpec((tm, D), lambda i: (i, 0))],
        out_shape=[jax.ShapeDtypeStruct((S, D_IN), F32), jax.ShapeDtypeStruct((S, D), BF16)],
        compiler_params=_params(40), exchange=exchange)


def _gelu_parts(v):
    t = jnp.tanh(GELU_C * (v + 0.044715 * (v * v * v)))
    cdf = 0.5 * (1.0 + t)
    return cdf, t


def _band_mask(n):
    a = lax.broadcasted_iota(jnp.int32, (CHUNK, 2 * CHUNK), 0)
    j = lax.broadcasted_iota(jnp.int32, (CHUNK, 2 * CHUNK), 1)
    dist = CHUNK + a - j
    valid = (dist >= 0) & (dist < CHUNK)
    return valid & ((n > 0) | (j >= CHUNK))


def _fill_bias(bucket_ref, table_ref, bias_ref):
    bucket = bucket_ref[...]
    for h in range(N_HEAD):
        acc = jnp.zeros((CHUNK, 2 * CHUNK), F32)
        for b in range(N_BUCKET):
            acc = jnp.where(bucket == b, table_ref[b, h], acc)
        bias_ref[h] = acc


def _fill_tril(ws_ref, wt_ref, wtt_ref=None):
    r = lax.broadcasted_iota(jnp.int32, (CHUNK, CHUNK), 0)
    c = lax.broadcasted_iota(jnp.int32, (CHUNK, CHUNK), 1)
    for g in range(N_GROUP):
        w = jnp.where(c <= r, ws_ref[g], 0.0)
        wt_ref[g] = w.astype(BF16)
        if wtt_ref is not None:
            wtt_ref[g] = w.T.astype(BF16)


def _kv_layouts(kv_prev, kv_cur):
    both = jnp.concatenate([kv_prev, kv_cur], axis=0)
    k = both[:, :128]
    v = both[:, 128:]
    return (k.astype(BF16), pltpu.roll(k, 64, axis=1).astype(BF16),
            v.astype(BF16), pltpu.roll(v, 64, axis=1).astype(BF16))


def _head_place(h):
    pair, pos, kvh = h // 2, h % 2, h // 4
    return pair, pos, kvh == pos


def _softmax_sink(qm, k_use, bias_h, sink, valid):
    s = _dot_nt(qm, k_use) * QK_SCALE + bias_h
    s = jnp.where(valid, s, NEG_INF)
    m = jnp.maximum(jnp.max(s, axis=-1, keepdims=True), sink)
    e = jnp.exp(s - m)
    es = jnp.exp(sink - m)
    inv = 1.0 / (jnp.sum(e, axis=-1, keepdims=True) + es)
    return e * inv, es * inv


def _mixer_fwd(z, v_gain, w_spatial, b_spatial_t, sinks, rel_table, bucket, exchange=None):
    def body(z_ref, kvp_ref, gain_ref, ws_ref, bt_ref, sink_ref, table_ref, bucket_ref, out_ref, probs_ref, probs_t_ref,
             share_ref, guv_ref, dgelu_ref, bias_ref, wt_ref):
        n = pl.program_id(0)

        @pl.when(n == 0)
        def _():
            _fill_bias(bucket_ref, table_ref, bias_ref)
            _fill_tril(ws_ref, wt_ref)

        zuv = z_ref[:, :1024]
        cdf, t = _gelu_parts(zuv)
        guv = zuv * cdf
        guv_ref[...] = guv
        dgelu_ref[...] = cdf + zuv * (0.5 * (1.0 - t * t)) * (GELU_C * (1.0 + 3.0 * 0.044715 * (zuv * zuv)))
        for g in range(N_GROUP):
            vg = guv[:, 512 + 128 * g:512 + 128 * (g + 1)]
            vn = vg * _rms_scale(vg) * gain_ref[:, 128 * g:128 * (g + 1)]
            sv = _dot(wt_ref[g], vn.astype(BF16)) + bt_ref[:, g:g + 1]
            out_ref[:, 128 * g:128 * (g + 1)] = (guv[:, 128 * g:128 * (g + 1)] * sv).astype(BF16)

        k_same, k_swap, v_same, v_swap = _kv_layouts(kvp_ref[...], z_ref[:, 1536:1792])
        valid = _band_mask(n)
        lane = lax.broadcasted_iota(jnp.int32, (1, 128), 1)
        lane_half = lane // 64
        shares = jnp.zeros((CHUNK, 128), F32)
        for pair in range(N_HEAD // 2):
            qq = z_ref[:, 1024 + 128 * pair:1024 + 128 * (pair + 1)]
            acc = jnp.zeros((CHUNK, 128), F32)
            for pos in range(2):
                h = 2 * pair + pos
                _, _, same = _head_place(h)
                qm = jnp.where(lane_half == pos, qq, 0.0).astype(BF16)
                p, p_sink = _softmax_sink(qm, k_same if same else k_swap, bias_ref[h], sink_ref[h], valid)
                pb = p.astype(BF16)
                probs_ref[0, h] = pb
                probs_t_ref[0, h] = p.T.astype(BF16)
                shares = jnp.where(lane == h, p_sink, shares)
                vm = jnp.where(lane_half == pos, v_same if same else v_swap, jnp.zeros((), BF16))
                acc = acc + _dot(pb, vm)
            out_ref[:, 512 + 128 * pair:512 + 128 * (pair + 1)] = acc.astype(BF16)
        share_ref[...] = shares

    return _call(
        body, (z, z, v_gain, w_spatial, b_spatial_t, sinks, rel_table, bucket), grid=(N_BLOCK,), name="mixer_fwd",
        in_specs=[pl.BlockSpec((CHUNK, D_IN), lambda n: (n, 0)),
                  pl.BlockSpec((CHUNK, 256), lambda n: (jnp.maximum(n - 1, 0), 6)),
                  pl.BlockSpec((1, 512), lambda n: (0, 0)),
                  pl.BlockSpec((N_GROUP, CHUNK, CHUNK), lambda n: (0, 0, 0)),
                  pl.BlockSpec((CHUNK, N_GROUP), lambda n: (0, 0)),
                  pl.BlockSpec(memory_space=pltpu.SMEM),
                  pl.BlockSpec(memory_space=pltpu.SMEM),
                  pl.BlockSpec((CHUNK, 2 * CHUNK), lambda n: (0, 0))],
        out_specs=[pl.BlockSpec((CHUNK, D), lambda n: (n, 0)),
                   pl.BlockSpec((1, N_HEAD, CHUNK, 2 * CHUNK), lambda n: (n, 0, 0, 0)),
                   pl.BlockSpec((1, N_HEAD, 2 * CHUNK, CHUNK), lambda n: (n, 0, 0, 0)),
                   pl.BlockSpec((CHUNK, 128), lambda n: (n, 0)),
                   pl.BlockSpec((CHUNK, 1024), lambda n: (n, 0)), pl.BlockSpec((CHUNK, 1024), lambda n: (n, 0))],
        out_shape=[jax.ShapeDtypeStruct((S, D), BF16), jax.ShapeDtypeStruct((N_BLOCK, N_HEAD, CHUNK, 2 * CHUNK), BF16),
                   jax.ShapeDtypeStruct((N_BLOCK, N_HEAD, 2 * CHUNK, CHUNK), BF16), jax.ShapeDtypeStruct((S, 128), F32),
                   jax.ShapeDtypeStruct((S, 1024), F32), jax.ShapeDtypeStruct((S, 1024), F32)],
        scratch_shapes=[pltpu.VMEM((N_HEAD, CHUNK, 2 * CHUNK), F32), pltpu.VMEM((N_GROUP, CHUNK, CHUNK), BF16)],
        compiler_params=_params(32), exchange=exchange)


def _out_proj(x, mix, w_out, gain2, exchange=None):
    tm = 256

    def body(x_ref, mix_ref, w_ref, g_ref, h1_ref, hn_ref, hnt_ref):
        h1 = x_ref[...] + _dot(mix_ref[...], w_ref[...])
        h1_ref[...] = h1
        hn = h1 * _rms_scale(h1) * g_ref[...]
        hn_ref[...] = hn.astype(BF16)
        hnt_ref[...] = hn.T.astype(BF16)

    return _call(
        body, (x, mix, w_out, gain2), grid=(S // tm,), name="out_proj",
        in_specs=[pl.BlockSpec((tm, D), lambda i: (i, 0)), pl.BlockSpec((tm, D), lambda i: (i, 0)),
                  pl.BlockSpec((D, D), lambda i: (0, 0)), pl.BlockSpec((1, D), lambda i: (0, 0))],
        out_specs=[pl.BlockSpec((tm, D), lambda i: (i, 0)), pl.BlockSpec((tm, D), lambda i: (i, 0)),
                   pl.BlockSpec((D, tm), lambda i: (0, i))],
        out_shape=[jax.ShapeDtypeStruct((S, D), F32), jax.ShapeDtypeStruct((S, D), BF16),
                   jax.ShapeDtypeStruct((D, S), BF16)],
        compiler_params=_params(32), exchange=exchange)


def _ffn_up(hn2, w_ff1, exchange=None):
    tm = 512
    nj = D_FF // 1024

    def body(hn_ref, w1_ref, r_ref, a_ref, at_ref):
        r = jnp.maximum(_dot(hn_ref[...], w1_ref[0]), 0.0)
        r_ref[...] = r.astype(BF16)
        a = r * r
        a_ref[...] = a.astype(BF16)
        at_ref[...] = a.T.astype(BF16)

    return _call(
        body, (hn2, w_ff1), grid=(nj, S // tm), name="ffn_up",
        in_specs=[pl.BlockSpec((tm, D), lambda j, i: (i, 0)), pl.BlockSpec((1, D, 1024), lambda j, i: (j, 0, 0))],
        out_specs=[pl.BlockSpec((tm, 1024), lambda j, i: (i, j)), pl.BlockSpec((tm, 1024), lambda j, i: (i, j)),
                   pl.BlockSpec((1024, tm), lambda j, i: (j, i))],
        out_shape=[jax.ShapeDtypeStruct((S, D_FF), BF16), jax.ShapeDtypeStruct((S, D_FF), BF16),
                   jax.ShapeDtypeStruct((D_FF, S), BF16)],
        compiler_params=_params(40, 2), exchange=exchange)


def _ffn_down(h1, a, w_ff2, exchange=None):
    tm = 1024
    nj = D_FF // 1024

    def body(h1_ref, a_ref, w2_ref, h2_ref, acc_ref):
        j = pl.program_id(1)
        part = _dot(a_ref[...], w2_ref[0])

        @pl.when(j == 0)
        def _():
            acc_ref[...] = part

        @pl.when(j > 0)
        def _():
            acc_ref[...] += part

        @pl.when(j == nj - 1)
        def _():
            h2_ref[...] = h1_ref[...] + acc_ref[...]

    return _call(
        body, (h1, a, w_ff2), grid=(S // tm, nj), name="ffn_down",
        in_specs=[pl.BlockSpec((tm, D), lambda i, j: (i, 0)), pl.BlockSpec((tm, 1024), lambda i, j: (i, j)),
                  pl.BlockSpec((1, 1024, D), lambda i, j: (j, 0, 0))],
        out_specs=[pl.BlockSpec((tm, D), lambda i, j: (i, 0))],
        out_shape=[jax.ShapeDtypeStruct((S, D), F32)],
        scratch_shapes=[pltpu.VMEM((tm, D), F32)],
        compiler_params=_params(48, 2), exchange=exchange)


def _tail(h2, p, target, w_gate, w_proj, final_gain):
    tm = 256
    steps = S // tm

    def body(h2_ref, p_ref, t_ref, wg_ref, wp_ref, gf_ref, dh2_ref, dwg_ref, dwp_ref, dgf_ref, loss_ref, dh2b_ref,
             dwp_acc):
        i = pl.program_id(0)
        h2 = h2_ref[...]
        h2b = h2.astype(BF16)
        pb = p_ref[...].astype(BF16)
        gate = jax.nn.sigmoid(_dot(h2b, wg_ref[...]))
        pp = jnp.concatenate([_dot(pb, wp_ref[j]) for j in range(N_CHIP)], axis=1)
        h3 = h2 + gate * pp
        r3 = _rms_scale(h3)
        xhat = h3 * r3
        gf = gf_ref[...]
        err = xhat * gf - t_ref[...]
        dy = err * (1.0 / D)
        dh3 = _rms_bwd(dy * gf, xhat, r3)
        dgp = (dh3 * pp * gate * (1.0 - gate)).astype(BF16)
        dpp = (dh3 * gate).astype(BF16)
        dh2 = dh3 + _dot_nt(dgp, wg_ref[...])
        dh2_ref[...] = dh2
        dh2b_ref[...] = dh2.astype(BF16)
        dwg = _dot_tn(h2b, dgp)
        dwp = _dot_tn(pb, dpp)
        dgf = jnp.sum(dy * xhat, axis=0, keepdims=True)
        sq = jnp.sum(jnp.sum(err * err, axis=1, keepdims=True), axis=0, keepdims=True)

        @pl.when(i == 0)
        def _():
            dwg_ref[...] = dwg
            dwp_acc[...] = dwp
            dgf_ref[...] = dgf
            loss_ref[...] = jnp.broadcast_to(sq, (8, 128))

        @pl.when(i > 0)
        def _():
            dwg_ref[...] += dwg
            dwp_acc[...] += dwp
            dgf_ref[...] += dgf
            loss_ref[...] += jnp.broadcast_to(sq, (8, 128))

        @pl.when(i == steps - 1)
        def _():
            for j in range(N_CHIP):
                dwp_ref[j] = dwp_acc[:, 256 * j:256 * (j + 1)]

    return _call(
        body, (h2, p, target, w_gate, w_proj, final_gain), grid=(steps,), name="tail",
        in_specs=[pl.BlockSpec((tm, D), lambda i: (i, 0)), pl.BlockSpec((tm, PLE), lambda i: (i, 0)),
                  pl.BlockSpec((tm, D), lambda i: (i, 0)), pl.BlockSpec((D, D), lambda i: (0, 0)),
                  pl.BlockSpec((N_CHIP, PLE, 256), lambda i: (0, 0, 0)), pl.BlockSpec((1, D), lambda i: (0, 0))],
        out_specs=[pl.BlockSpec((tm, D), lambda i: (i, 0)), pl.BlockSpec((D, D), lambda i: (0, 0)),
                   pl.BlockSpec((N_CHIP, PLE, 256), lambda i: (0, 0, 0)), pl.BlockSpec((1, D), lambda i: (0, 0)),
                   pl.BlockSpec((8, 128), lambda i: (0, 0)), pl.BlockSpec((tm, D), lambda i: (i, 0))],
        out_shape=[jax.ShapeDtypeStruct((S, D), F32), jax.ShapeDtypeStruct((D, D), F32),
                   jax.ShapeDtypeStruct((N_CHIP, PLE, 256), F32), jax.ShapeDtypeStruct((1, D), F32),
                   jax.ShapeDtypeStruct((8, 128), F32), jax.ShapeDtypeStruct((S, D), BF16)],
        scratch_shapes=[pltpu.VMEM((PLE, D), F32)],
        compiler_params=_params(48))[0]


def _ffn_bwd_down(dh2b, r, a_t, w_ff2, exchange=None):
    tm = 1024
    nj = D_FF // 1024

    def body(dh2_ref, r_ref, at_ref, w2_ref, df_ref, dw2_ref):
        i = pl.program_id(1)
        dh2b = dh2_ref[...]
        da = _dot_nt(dh2b, w2_ref[0])
        df_ref[...] = (da * (2.0 * r_ref[...].astype(F32))).astype(BF16)
        dw2 = _dot(at_ref[...], dh2b)

        @pl.when(i == 0)
        def _():
            dw2_ref[0] = dw2

        @pl.when(i > 0)
        def _():
            dw2_ref[0] += dw2

    return _call(
        body, (dh2b, r, a_t, w_ff2), grid=(nj, S // tm), name="ffn_bwd_down",
        in_specs=[pl.BlockSpec((tm, D), lambda j, i: (i, 0)), pl.BlockSpec((tm, 1024), lambda j, i: (i, j)),
                  pl.BlockSpec((1024, tm), lambda j, i: (j, i)), pl.BlockSpec((1, 1024, D), lambda j, i: (j, 0, 0))],
        out_specs=[pl.BlockSpec((tm, 1024), lambda j, i: (i, j)), pl.BlockSpec((1, 1024, D), lambda j, i: (j, 0, 0))],
        out_shape=[jax.ShapeDtypeStruct((S, D_FF), BF16), jax.ShapeDtypeStruct((nj, 1024, D), F32)],
        compiler_params=_params(48, 2), exchange=exchange)


def _ffn_bwd_up(df, hn2_t, exchange=None):
    tm = 1024
    nj = D_FF // 1024

    def body(df_ref, hnt_ref, dw1_ref):
        i = pl.program_id(1)
        dw1 = _dot(hnt_ref[...], df_ref[...])

        @pl.when(i == 0)
        def _():
            dw1_ref[0] = dw1

        @pl.when(i > 0)
        def _():
            dw1_ref[0] += dw1

    return _call(
        body, (df, hn2_t), grid=(nj, S // tm), name="ffn_bwd_up",
        in_specs=[pl.BlockSpec((tm, 1024), lambda j, i: (i, j)), pl.BlockSpec((D, tm), lambda j, i: (0, i))],
        out_specs=[pl.BlockSpec((1, D, 1024), lambda j, i: (j, 0, 0))],
        out_shape=[jax.ShapeDtypeStruct((nj, D, 1024), F32)],
        compiler_params=_params(40, 2), exchange=exchange)


def _ffn_bwd_input(df, w_ff1, dh2, h1, gain2, mix, w_out, exchange=None):
    tm = 512
    nj = D_FF // 1024
    steps = S // tm

    def body(df_ref, w1_ref, dh2_ref, h1_ref, g_ref, mix_ref, wo_ref, dh1_ref, dmix_ref, dwo_ref, dg_ref, acc_ref):
        i = pl.program_id(0)
        j = pl.program_id(1)
        part = _dot_nt(df_ref[...], w1_ref[j])

        @pl.when(j == 0)
        def _():
            acc_ref[...] = part

        @pl.when(j > 0)
        def _():
            acc_ref[...] += part

        @pl.when(j == nj - 1)
        def _():
            dhn = acc_ref[...]
            h1 = h1_ref[...]
            r2 = _rms_scale(h1)
            xhat = h1 * r2
            dh1 = dh2_ref[...] + _rms_bwd(dhn * g_ref[...], xhat, r2)
            dh1_ref[...] = dh1
            dh1b = dh1.astype(BF16)
            dmix_ref[...] = _dot_nt(dh1b, wo_ref[...])
            dwo = _dot_tn(mix_ref[...], dh1b)
            dg = jnp.sum(dhn * xhat, axis=0, keepdims=True)

            @pl.when(i == 0)
            def _():
                dwo_ref[...] = dwo
                dg_ref[...] = dg

            @pl.when(i > 0)
            def _():
                dwo_ref[...] += dwo
                dg_ref[...] += dg

    return _call(
        body, (df, w_ff1, dh2, h1, gain2, mix, w_out), grid=(steps, nj), name="ffn_bwd_input",
        in_specs=[pl.BlockSpec((tm, 1024), lambda i, j: (i, j)),
                  pl.BlockSpec((nj, D, 1024), lambda i, j: (0, 0, 0), pipeline_mode=pl.Buffered(1)),
                  pl.BlockSpec((tm, D), lambda i, j: (i, 0)), pl.BlockSpec((tm, D), lambda i, j: (i, 0)),
                  pl.BlockSpec((1, D), lambda i, j: (0, 0)), pl.BlockSpec((tm, D), lambda i, j: (i, 0)),
                  pl.BlockSpec((D, D), lambda i, j: (0, 0), pipeline_mode=pl.Buffered(1))],
        out_specs=[pl.BlockSpec((tm, D), lambda i, j: (i, 0)), pl.BlockSpec((tm, D), lambda i, j: (i, 0)),
                   pl.BlockSpec((D, D), lambda i, j: (0, 0)), pl.BlockSpec((1, D), lambda i, j: (0, 0))],
        out_shape=[jax.ShapeDtypeStruct((S, D), F32), jax.ShapeDtypeStruct((S, D), F32),
                   jax.ShapeDtypeStruct((D, D), F32), jax.ShapeDtypeStruct((1, D), F32)],
        scratch_shapes=[pltpu.VMEM((tm, D), F32)],
        compiler_params=_params(56, 2), exchange=exchange)


IN_GROUP = 8


def _mixer_bwd(z, dmix, v_gain, w_spatial, b_spatial_t, saved, bucket, hn1, exchange=None):
    def body(z_ref, kvp_ref, dm_ref, gain_ref, ws_ref, bt_ref, probs_ref, probs_t_ref, share_ref, guv_ref, dgelu_ref,
             bucket_ref, hn_ref,
             dz_ref, dws_ref, db_ref, dgain_ref, dsink_ref, drel_ref, dwin_ref,
             wt_ref, wtt_ref, dbias_ref, dsv_ref, carry_ref):
        n = pl.program_id(0)

        @pl.when(n == 0)
        def _():
            _fill_tril(ws_ref, wt_ref, wtt_ref)
            dwin_ref[...] = jnp.zeros_like(dwin_ref)
            dbias_ref[...] = jnp.zeros_like(dbias_ref)
            dsv_ref[...] = jnp.zeros_like(dsv_ref)
            dws_ref[...] = jnp.zeros_like(dws_ref)
            dgain_ref[...] = jnp.zeros_like(dgain_ref)
            dsink_ref[...] = jnp.zeros_like(dsink_ref)

        rows = pl.ds(pl.multiple_of(n * CHUNK, CHUNK), CHUNK)

        guv = guv_ref[...]
        dgelu = dgelu_ref[...]
        for g in range(N_GROUP):
            lo, hi = 128 * g, 128 * (g + 1)
            u = guv[:, lo:hi]
            vg = guv[:, 512 + lo:512 + hi]
            rr = _rms_scale(vg)
            vhat = vg * rr
            gain = gain_ref[:, lo:hi]
            vnb = (vhat * gain).astype(BF16)
            sv = _dot(wt_ref[g], vnb) + bt_ref[:, g:g + 1]
            da = dm_ref[:, lo:hi]
            dsv = da * u
            dsvb = dsv.astype(BF16)
            dsv_ref[g] += dsv
            dws_ref[g] += _dot_nt(dsvb, vnb)
            dvn = _dot(wtt_ref[g], dsvb)
            dgain_ref[:, lo:hi] += jnp.sum(dvn * vhat, axis=0, keepdims=True)
            dvg = _rms_bwd(dvn * gain, vhat, rr)
            dz_ref[rows, lo:hi] = (da * sv * dgelu[:, lo:hi]).astype(BF16)
            dz_ref[rows, 512 + lo:512 + hi] = (dvg * dgelu[:, 512 + lo:512 + hi]).astype(BF16)

        k_same, k_swap, v_same, v_swap = _kv_layouts(kvp_ref[...], z_ref[:, 1536:1792])
        lane_half = lax.broadcasted_iota(jnp.int32, (1, 128), 1) // 64
        zero = jnp.zeros((2 * CHUNK, 128), F32)
        dk_same, dk_swap, dv_same, dv_swap = zero, zero, zero, zero
        for pair in range(N_HEAD // 2):
            cols = slice(1024 + 128 * pair, 1024 + 128 * (pair + 1))
            qq = z_ref[:, cols]
            do_pair = dm_ref[:, 512 + 128 * pair:512 + 128 * (pair + 1)]
            dq = jnp.zeros((CHUNK, 128), F32)
            for pos in range(2):
                h = 2 * pair + pos
                _, _, same = _head_place(h)
                on_half = lane_half == pos
                qm = jnp.where(on_half, qq, 0.0).astype(BF16)
                k_use = k_same if same else k_swap
                v_use = v_same if same else v_swap
                pb = probs_ref[0, h]
                p = pb.astype(F32)
                p_sink = share_ref[:, h:h + 1]
                dom = jnp.where(on_half, do_pair, 0.0).astype(BF16)
                dp = _dot_nt(dom, v_use)
                dsum = jnp.sum(p * dp, axis=-1, keepdims=True)
                ds = p * (dp - dsum)
                dbias_ref[h] += ds
                dsink_ref[h:h + 1, :] += jnp.broadcast_to(jnp.sum(-p_sink * dsum, axis=0, keepdims=True), (1, 128))
                dsb = ds.astype(BF16)
                dq = dq + jnp.where(on_half, _dot(dsb, k_use), 0.0)
                dk_h = _dot_tn(dsb, qm)
                dv_h = _dot(probs_t_ref[0, h], dom)
                if same:
                    dk_same, dv_same = dk_same + dk_h, dv_same + dv_h
                else:
                    dk_swap, dv_swap = dk_swap + dk_h, dv_swap + dv_h
            dz_ref[rows, cols] = (dq * QK_SCALE).astype(BF16)
        dk = (dk_same + pltpu.roll(dk_swap, 64, axis=1)) * QK_SCALE
        dv = dv_same + pltpu.roll(dv_swap, 64, axis=1)
        dkv = jnp.concatenate([dk, dv], axis=1)

        @pl.when(n > 0)
        def _():
            prev_rows = pl.ds(pl.multiple_of((n - 1) * CHUNK, CHUNK), CHUNK)
            dz_ref[prev_rows, 1536:1792] = (carry_ref[...] + dkv[:CHUNK]).astype(BF16)

        carry_ref[...] = dkv[CHUNK:]

        @pl.when((n > 0) & (n % IN_GROUP == 0))
        def _():
            done = pl.ds(pl.multiple_of((n - IN_GROUP) * CHUNK, IN_GROUP * CHUNK), IN_GROUP * CHUNK)
            dwin_ref[...] += _dot_tn(dz_ref[done, :], hn_ref[...])

        @pl.when(n == N_BLOCK - 1)
        def _():
            dz_ref[rows, 1536:1792] = dkv[CHUNK:].astype(BF16)
            last = pl.ds((N_BLOCK - IN_GROUP) * CHUNK, IN_GROUP * CHUNK)
            dwin_ref[...] += _dot_tn(dz_ref[last, :], hn_ref[...])
            r = lax.broadcasted_iota(jnp.int32, (CHUNK, CHUNK), 0)
            c = lax.broadcasted_iota(jnp.int32, (CHUNK, CHUNK), 1)
            for g in range(N_GROUP):
                dws_ref[g] = jnp.where(c <= r, dws_ref[g], 0.0)
                db_ref[g] = jnp.sum(dsv_ref[g], axis=1, keepdims=True)
            bucket = bucket_ref[...]
            for h in range(N_HEAD):
                dbh = dbias_ref[h]
                per_bucket = [jnp.sum(jnp.where(bucket == b, dbh, 0.0), axis=0, keepdims=True) for b in range(N_BUCKET)]
                drel_ref[h] = jnp.sum(jnp.concatenate(per_bucket, axis=0), axis=1, keepdims=True)

    def hn_group(n):
        return jnp.where(n == N_BLOCK - 1, N_BLOCK // IN_GROUP - 1, jnp.maximum(n // IN_GROUP - 1, 0))

    return _call(
        body, (z, z, dmix, v_gain, w_spatial, b_spatial_t, *saved, bucket, hn1), grid=(N_BLOCK,),
        name="mixer_bwd",
        in_specs=[pl.BlockSpec((CHUNK, D_IN), lambda n: (n, 0)),
                  pl.BlockSpec((CHUNK, 256), lambda n: (jnp.maximum(n - 1, 0), 6)),
                  pl.BlockSpec((CHUNK, D), lambda n: (n, 0)),
                  pl.BlockSpec((1, 512), lambda n: (0, 0)),
                  pl.BlockSpec((N_GROUP, CHUNK, CHUNK), lambda n: (0, 0, 0)),
                  pl.BlockSpec((CHUNK, N_GROUP), lambda n: (0, 0)),
                  pl.BlockSpec((1, N_HEAD, CHUNK, 2 * CHUNK), lambda n: (n, 0, 0, 0)),
                  pl.BlockSpec((1, N_HEAD, 2 * CHUNK, CHUNK), lambda n: (n, 0, 0, 0)),
                  pl.BlockSpec((CHUNK, 128), lambda n: (n, 0)),
                  pl.BlockSpec((CHUNK, 1024), lambda n: (n, 0)), pl.BlockSpec((CHUNK, 1024), lambda n: (n, 0)),
                  pl.BlockSpec((CHUNK, 2 * CHUNK), lambda n: (0, 0)),
                  pl.BlockSpec((IN_GROUP * CHUNK, D), lambda n: (hn_group(n), 0))],
        out_specs=[pl.BlockSpec((S, D_IN), lambda n: (0, 0)),
                   pl.BlockSpec((N_GROUP, CHUNK, CHUNK), lambda n: (0, 0, 0)),
                   pl.BlockSpec((N_GROUP, CHUNK, 1), lambda n: (0, 0, 0)),
                   pl.BlockSpec((1, 512), lambda n: (0, 0)),
                   pl.BlockSpec((N_HEAD, 128), lambda n: (0, 0)),
                   pl.BlockSpec((N_HEAD, N_BUCKET, 1), lambda n: (0, 0, 0)),
                   pl.BlockSpec((D_IN, D), lambda n: (0, 0))],
        out_shape=[jax.ShapeDtypeStruct((S, D_IN), BF16), jax.ShapeDtypeStruct((N_GROUP, CHUNK, CHUNK), F32),
                   jax.ShapeDtypeStruct((N_GROUP, CHUNK, 1), F32), jax.ShapeDtypeStruct((1, 512), F32),
                   jax.ShapeDtypeStruct((N_HEAD, 128), F32), jax.ShapeDtypeStruct((N_HEAD, N_BUCKET, 1), F32),
                   jax.ShapeDtypeStruct((D_IN, D), F32)],
        scratch_shapes=[pltpu.VMEM((N_GROUP, CHUNK, CHUNK), BF16),
                        pltpu.VMEM((N_GROUP, CHUNK, CHUNK), BF16), pltpu.VMEM((N_HEAD, CHUNK, 2 * CHUNK), F32),
                        pltpu.VMEM((N_GROUP, CHUNK, CHUNK), F32), pltpu.VMEM((CHUNK, 256), F32)],
        compiler_params=_params(56), exchange=exchange)


def _in_bwd_input(dz, w_in_t, x, dh1, gain1, exchange=None):
    tm = 512

    def body(dz_ref, w_ref, x_ref, dh1_ref, g_ref, dx_ref, dg_ref):
        i = pl.program_id(0)
        dhn = _dot(dz_ref[...], w_ref[...])
        xv = x_ref[...]
        r1 = _rms_scale(xv)
        xhat = xv * r1
        dx_ref[...] = dh1_ref[...] + _rms_bwd(dhn * g_ref[...], xhat, r1)
        dg = jnp.sum(dhn * xhat, axis=0, keepdims=True)

        @pl.when(i == 0)
        def _():
            dg_ref[...] = dg

        @pl.when(i > 0)
        def _():
            dg_ref[...] += dg

    return _call(
        body, (dz, w_in_t, x, dh1, gain1), grid=(S // tm,), name="in_bwd_input",
        in_specs=[pl.BlockSpec((tm, D_IN), lambda i: (i, 0)), pl.BlockSpec((D_IN, D), lambda i: (0, 0)),
                  pl.BlockSpec((tm, D), lambda i: (i, 0)), pl.BlockSpec((tm, D), lambda i: (i, 0)),
                  pl.BlockSpec((1, D), lambda i: (0, 0))],
        out_specs=[pl.BlockSpec((tm, D), lambda i: (i, 0)), pl.BlockSpec((1, D), lambda i: (0, 0))],
        out_shape=[jax.ShapeDtypeStruct((S, D), F32), jax.ShapeDtypeStruct((1, D), F32)],
        compiler_params=_params(48), exchange=exchange)


def _rel_bucket():
    a = jnp.arange(CHUNK)[:, None]
    j = jnp.arange(2 * CHUNK)[None, :]
    n = jnp.maximum(CHUNK + a - j, 0)
    max_exact = N_BUCKET // 2
    nf = jnp.maximum(n, 1).astype(jnp.float32)
    large = max_exact + (jnp.log(nf / max_exact) / math.log(CHUNK / max_exact) * (N_BUCKET - max_exact)).astype(jnp.int32)
    large = jnp.minimum(large, N_BUCKET - 1)
    return jnp.where(n < max_exact, n, large).astype(jnp.int32)


def _step(x, p, target, small, bufs, place):
    bucket = _rel_bucket()
    sinks = small["attn_sinks"].reshape(N_HEAD)
    b_t = jnp.transpose(small["b_spatial"].reshape(N_GROUP, CHUNK))
    ws = small["w_spatial"].reshape(N_GROUP, CHUNK, CHUNK)
    gain1, gain2 = small["norm1_gain"], small["norm2_gain"]
    v_gain = small["gmlp_v_gain"]
    final_gain = small["final_gain"].reshape(1, D)
    table = small["rel_bias_table"]
    bufs = dict(bufs)

    def gather(*names):
        return _RelayGather([bufs[n] for n in names])

    def took(names, got):
        bufs.update(zip(names, got))

    w_in_t = _whole(bufs["w_in"]).reshape(D_IN, D)
    (z, hn1), got = _in_proj(x, gain1, w_in_t, gather("w_out"))
    took(["w_out"], got)
    (mix, *saved), got = _mixer_fwd(z, v_gain, ws, b_t, sinks, table, bucket, gather("w_ff1"))
    took(["w_ff1"], got)
    w_out = _whole(bufs["w_out"]).reshape(D, D)
    (h1, hn2, hn2_t), _ = _out_proj(x, mix, w_out, gain2)
    w_ff1 = _whole(bufs["w_ff1"])
    (r, a, a_t), got = _ffn_up(hn2, w_ff1, gather("w_ff2"))
    took(["w_ff2"], got)
    w_ff2 = _whole(bufs["w_ff2"])
    (h2,), got = _ffn_down(h1, a, w_ff2, gather("w_ple_gate", "w_ple_proj"))
    took(["w_ple_gate", "w_ple_proj"], got)
    dh2, d_gate, d_proj, d_final, sq, dh2b = _tail(h2, p, target, _whole(bufs["w_ple_gate"]).reshape(D, D),
                                                   _whole(bufs["w_ple_proj"]), final_gain)

    def pair_sums(halves, from_sibling):
        sums, landing = zip(*[_pair_sum(g, o, place) for g, o in zip(halves, from_sibling)])
        return list(sums), list(landing)

    landed = {}
    halves = [_halves(d_gate.reshape(N_CHIP, 256, D)), _halves(d_proj)]
    (df, d_ff2), got = _ffn_bwd_down(dh2b, r, a_t, w_ff2, _SiblingExchange(halves))
    ex, halves = _ChipExchange(*pair_sums(halves, got)), [_halves(d_ff2)]
    (d_ff1,), got = _ffn_bwd_up(df, hn2_t, _Both(ex, _SiblingExchange(halves)))
    landed.update(zip(["w_ple_gate", "w_ple_proj"], got[:2]))
    sums_ff2, landing_ff2 = pair_sums(halves, got[2:])
    halves = [_halves(d_ff1)]
    (dh1, dmix, d_out, d_gain2), got = _ffn_bwd_input(
        df, w_ff1, dh2, h1, gain2, mix, w_out,
        _Both(_ChipExchange(sums_ff2, landing_ff2, chips=(0, 1)), _SiblingExchange(halves)))
    ex_ff2 = _ChipExchange(sums_ff2, got[:1], chips=(2,))
    ex, halves = _ChipExchange(*pair_sums(halves, got[1:])), [_halves(d_out.reshape(N_CHIP, 256, D))]
    (dz, d_ws, d_b, d_vgain, d_sink, d_rel, d_in_t), got = _mixer_bwd(z, dmix, v_gain, ws, b_t, saved, bucket, hn1,
                                                                     _Both(_Both(ex, ex_ff2), _SiblingExchange(halves)))
    landed["w_ff1"], landed["w_ff2"] = got[0], got[1]
    small_grads = {
        "gmlp_v_gain": d_vgain, "w_spatial": d_ws.reshape(1, N_GROUP, CHUNK, CHUNK),
        "b_spatial": d_b.reshape(1, N_GROUP, CHUNK), "attn_sinks": d_sink[:, 0].reshape(1, N_HEAD),
        "rel_bias_table": jnp.transpose(d_rel.reshape(N_HEAD, N_BUCKET)), "norm2_gain": d_gain2,
        "final_gain": d_final.reshape(D),
    }
    ex, halves = _ChipExchange(*pair_sums(halves, got[2:])), [_halves(d_in_t.reshape(N_CHIP, 448, D))]
    (dx, small_grads["norm1_gain"]), got = _in_bwd_input(dz, w_in_t, x, dh1, gain1, _Both(ex, _SiblingExchange(halves)))
    landed["w_out"] = got[0]
    return dx, landed, _ChipExchange(*pair_sums(halves, got[1:])), small_grads, sq


HBM_SPEC = pl.BlockSpec(memory_space=pltpu.HBM)
VMEM_SPEC = pl.BlockSpec(memory_space=pltpu.VMEM)


def _mesh_place():
    x, y, c = lax.axis_index("x"), lax.axis_index("y"), lax.axis_index("c")
    others = [(1 - x, y), (x, 1 - y), (1 - x, 1 - y)]
    return x, y, c, others


def _remote(src, dst, send_sem, recv_sem, device):
    return pltpu.make_async_remote_copy(src_ref=src, dst_ref=dst, send_sem=send_sem, recv_sem=recv_sem,
                                        device_id=device, device_id_type=MESH)


def _hbm_like(a, shape=None, dtype=None):
    return pltpu.HBM(a.shape if shape is None else shape, a.dtype if dtype is None else dtype)


def _gather_start(bufs, send_sems, recv_sems):
    x, y, c, others = _mesh_place()
    me = 2 * x + y
    for w, buf in enumerate(bufs):
        for k in range(3):
            mine = buf.at[me, c]
            _remote(mine, mine, send_sems.at[w, k], recv_sems.at[w, k], (*others[k], c)).start()


def _gather_finish(bufs, send_sems, recv_sems):
    x, y, c, others = _mesh_place()
    me = 2 * x + y
    sibling = (x, y, 1 - c)
    idx = [2 * ox + oy for ox, oy in others]
    chips = range(3)
    for w, buf in enumerate(bufs):
        for k in chips:
            landed = buf.at[idx[k], c]
            _remote(landed, landed, send_sems.at[w, k], recv_sems.at[w, k], sibling).wait_recv()
            _remote(landed, landed, send_sems.at[w, 3 + k], recv_sems.at[w, 3 + k], sibling).start()
    for w, buf in enumerate(bufs):
        for k in chips:
            landed = buf.at[idx[k], 1 - c]
            _remote(landed, landed, send_sems.at[w, 3 + k], recv_sems.at[w, 3 + k], sibling).wait_recv()
    for w, buf in enumerate(bufs):
        for k in chips:
            mine, passed = buf.at[me, c], buf.at[idx[k], c]
            _remote(mine, mine, send_sems.at[w, k], recv_sems.at[w, k], sibling).wait_send()
            _remote(passed, passed, send_sems.at[w, 3 + k], recv_sems.at[w, 3 + k], sibling).wait_send()


def _gather_sems(n):
    return [pltpu.SemaphoreType.DMA((n, 6)), pltpu.SemaphoreType.DMA((n, 6))]


def _sibling_copies(grads, landing, send_sems, recv_sems):
    x, y, c, _ = _mesh_place()
    return [_remote(grads[w].at[j, 1 - c], landing[w].at[j], send_sems.at[w, j], recv_sems.at[w, j], (x, y, 1 - c))
            for w in range(len(grads)) for j in range(N_CHIP)]


def _sibling_exchange_start(grads, landing, send_sems, recv_sems):
    for cp in _sibling_copies(grads, landing, send_sems, recv_sems):
        cp.start()


def _sibling_exchange_finish(grads, landing, send_sems, recv_sems):
    copies = _sibling_copies(grads, landing, send_sems, recv_sems)
    for cp in copies:
        cp.wait_recv()
    for cp in copies:
        cp.wait_send()


def _sibling_exchange_sems(n):
    return [pltpu.SemaphoreType.DMA((n, N_CHIP)), pltpu.SemaphoreType.DMA((n, N_CHIP))]


def _sibling_exchange(grads):
    n = len(grads)

    def body(*refs):
        ins, outs = refs[:n], refs[n:2 * n]
        _sibling_exchange_start(ins, outs, *refs[2 * n:])
        _sibling_exchange_finish(ins, outs, *refs[2 * n:])

    return pl.pallas_call(
        body, name="sibling_exchange",
        in_specs=[HBM_SPEC] * n, out_specs=[HBM_SPEC] * n,
        out_shape=[_hbm_like(g, (N_CHIP,) + g.shape[2:]) for g in grads],
        scratch_shapes=_sibling_exchange_sems(n),
    )(*[_in_hbm(g) for g in grads])


ALL_CHIPS = (0, 1, 2)


def _chip_exchange_start(sums, landing, send_sems, recv_sems, chips=ALL_CHIPS):
    x, y, c, others = _mesh_place()
    me = 2 * x + y
    for w in range(len(sums)):
        for k in chips:
            ox, oy = others[k]
            _remote(sums[w].at[2 * ox + oy], landing[w].at[me], send_sems.at[w, k], recv_sems.at[w, k],
                    (ox, oy, c)).start()


def _chip_exchange_finish(sums, landing, send_sems, recv_sems, chips=ALL_CHIPS):
    x, y, c, others = _mesh_place()
    for w in range(len(sums)):
        for k in chips:
            piece = landing[w].at[2 * others[k][0] + others[k][1]]
            _remote(piece, piece, send_sems.at[w, k], recv_sems.at[w, k], (x, y, c)).wait_recv()
    for w in range(len(sums)):
        for k in chips:
            piece = sums[w].at[2 * others[k][0] + others[k][1]]
            _remote(piece, piece, send_sems.at[w, k], recv_sems.at[w, k], (x, y, c)).wait_send()


def _chip_exchange_sems(n):
    return [pltpu.SemaphoreType.DMA((n, 3)), pltpu.SemaphoreType.DMA((n, 3))]


def _sibling_allgather(bufs, also):
    n = len(bufs)
    k_in, k_out = len(also.operands), also.n_out

    def body(*refs):
        ex_ins, refs = refs[n:n + k_in], refs[n + k_in:]
        outs, refs = refs[:n], refs[n:]
        ex_outs, refs = refs[:k_out], refs[k_out:]
        send_sems, recv_sems, ex_sems = refs[0], refs[1], refs[2:]
        x, y, c, _ = _mesh_place()
        sibling = (x, y, 1 - c)
        also.start(ex_ins, ex_outs, ex_sems)
        sends = [_remote(outs[w].at[c], outs[w].at[c], send_sems.at[w], recv_sems.at[w], sibling) for w in range(n)]
        for cp in sends:
            cp.start()
        for w in range(n):
            landed = outs[w].at[1 - c]
            _remote(landed, landed, send_sems.at[w], recv_sems.at[w], sibling).wait_recv()
        for cp in sends:
            cp.wait_send()
        also.finish(ex_ins, ex_outs, ex_sems)

    res = pl.pallas_call(
        body, name="sibling_allgather",
        in_specs=[HBM_SPEC] * (n + k_in), out_specs=[HBM_SPEC] * (n + k_out),
        out_shape=[_hbm_like(b) for b in bufs] + also.out_shape,
        input_output_aliases={**{w: w for w in range(n)}, **{n + i: n + o for i, o in also.aliases.items()}},
        scratch_shapes=[pltpu.SemaphoreType.DMA((n,)), pltpu.SemaphoreType.DMA((n,))] + also.sems,
    )(*bufs, *[_in_hbm(o) for o in also.operands])
    return list(res[:n]), list(res[n:])


def _pair_sum(grad, other, place):
    _, _, h, cols = grad.shape
    tr = _row_tile(h)

    def body(place_ref, g_ref, o_ref, sums_ref, own_ref):
        s = (g_ref[0, 0] + o_ref[0]).astype(BF16)
        sums_ref[0] = s

        @pl.when(pl.program_id(1) == place_ref[0])
        def _():
            own_ref[0] = s

    return pl.pallas_call(
        body, name="pair_sum",
        grid_spec=pltpu.PrefetchScalarGridSpec(
            num_scalar_prefetch=1, grid=(h // tr, N_CHIP),
            in_specs=[pl.BlockSpec((1, 1, tr, cols), lambda r, j, place_ref: (j, place_ref[1], r, 0)),
                      pl.BlockSpec((1, tr, cols), lambda r, j, place_ref: (j, r, 0))],
            out_specs=[pl.BlockSpec((1, tr, cols), lambda r, j, place_ref: (j, r, 0)),
                       pl.BlockSpec((1, tr, cols), lambda r, j, place_ref: (place_ref[0], r, 0))]),
        out_shape=[pltpu.HBM((N_CHIP, h, cols), BF16)] * 2,
        compiler_params=_params(32, 2),
    )(place, _in_hbm(grad), _in_hbm(other))


def _chip_sum(parts, place):
    _, h, cols = parts.shape
    tr = _row_tile(h)

    def body(place_ref, p_ref, out_ref):
        out_ref[0] = ((p_ref[0].astype(F32) + p_ref[1].astype(F32)) + p_ref[2].astype(F32)) + p_ref[3].astype(F32)

    return pl.pallas_call(
        body, name="chip_sum",
        grid_spec=pltpu.PrefetchScalarGridSpec(
            num_scalar_prefetch=1, grid=(h // tr,),
            in_specs=[pl.BlockSpec((N_CHIP, tr, cols), lambda r, place_ref: (0, r, 0))],
            out_specs=pl.BlockSpec((1, tr, cols), lambda r, place_ref: (place_ref[1], r, 0))),
        out_shape=pltpu.HBM((2, h, cols), F32),
        compiler_params=_params(32),
    )(place, _in_hbm(parts))


def _adamw_math(w, g, m, v):
    m = ADAM_B1 * m + (1.0 - ADAM_B1) * g
    v = ADAM_B2 * v + (1.0 - ADAM_B2) * (g * g)
    m_hat = m / (1.0 - ADAM_B1 ** ADAM_STEP)
    v_hat = v / (1.0 - ADAM_B2 ** ADAM_STEP)
    delta = -ADAM_LR * (m_hat / (jnp.sqrt(v_hat) + ADAM_EPS) + ADAM_WD * w)
    return delta, m, v


def _adamw(w, g, m, v, exchange=None):
    rows, cols = w.shape
    tr = _row_tile(rows)

    def body(w_ref, g_ref, m_ref, v_ref, d_ref, nm_ref, nv_ref, g_out_ref):
        g = g_ref[...]
        d_ref[...], nm_ref[...], nv_ref[...] = _adamw_math(w_ref[...], g, m_ref[...], v_ref[...])
        g_out_ref[...] = g

    spec = pl.BlockSpec((tr, cols), lambda r: (r, 0))
    return _call(
        body, (w, g, m, v), grid=(rows // tr,), name="adamw",
        in_specs=[spec] * 4, out_specs=[spec] * 4,
        out_shape=[jax.ShapeDtypeStruct((rows, cols), F32)] * 4,
        compiler_params=_params(48), exchange=exchange)


SMALL_NAMES = ("norm1_gain", "gmlp_v_gain", "w_spatial", "b_spatial", "attn_sinks", "rel_bias_table", "norm2_gain",
               "final_gain")
PACK_TILE = 8 * 128


def _pack_small(arrays):
    parts = []
    for a in arrays:
        flat = a.reshape(-1)
        rows = -(-flat.shape[0] // PACK_TILE) * 8
        parts.append(jnp.pad(flat, (0, rows * 128 - flat.shape[0])).reshape(rows, 128))
    return jnp.concatenate(parts, axis=0)


def _unpack_small(packed, like):
    out, row = [], 0
    for a in like:
        size = math.prod(a.shape)
        rows = -(-size // PACK_TILE) * 8
        out.append(packed[row:row + rows].reshape(-1)[:size].reshape(a.shape))
        row += rows
    return out


def _small_update(gathered, w, m, v):
    rows = gathered.shape[1]

    def body(g_ref, w_ref, m_ref, v_ref, tot_ref, d_ref, nm_ref, nv_ref):
        total = g_ref[0].astype(F32)
        for dev in range(1, 8):
            total = total + g_ref[dev].astype(F32)
        tot_ref[...] = total
        d_ref[...], nm_ref[...], nv_ref[...] = _adamw_math(w_ref[...], total, m_ref[...], v_ref[...])

    return pl.pallas_call(
        body, name="small_update",
        in_specs=[VMEM_SPEC] * 4, out_specs=[VMEM_SPEC] * 4,
        out_shape=[jax.ShapeDtypeStruct((rows, 128), F32)] * 4,
        compiler_params=pltpu.CompilerParams(vmem_limit_bytes=24 * MIB),
    )(gathered, w, m, v)


def _halves(a):
    return a.reshape(a.shape[:-2] + (2, a.shape[-2] // 2, a.shape[-1]))


def _whole(a):
    return a.reshape(a.shape[:-3] + (2 * a.shape[-2], a.shape[-1]))


def kernel(x, p, norm1_gain, w_in, gmlp_v_gain, w_spatial, b_spatial, attn_sinks, rel_bias_table, w_out, norm2_gain, w_ff1, w_ff2, w_ple_proj, w_ple_gate, final_gain, loss_target, m_norm1_gain, m_w_in, m_gmlp_v_gain, m_w_spatial, m_b_spatial, m_attn_sinks, m_rel_bias_table, m_w_out, m_norm2_gain, m_w_ff1, m_w_ff2, m_w_ple_proj, m_w_ple_gate, m_final_gain, v_norm1_gain, v_w_in, v_gmlp_v_gain, v_w_spatial, v_b_spatial, v_attn_sinks, v_rel_bias_table, v_w_out, v_norm2_gain, v_w_ff1, v_w_ff2, v_w_ple_proj, v_w_ple_gate, v_final_gain):
    given = dict(locals())
    small = {n: given[n] for n in SMALL_NAMES}
    chip = 2 * lax.axis_index("x") + lax.axis_index("y")
    place = jnp.stack([chip, lax.axis_index("c")]).astype(jnp.int32)

    big_names = ("w_in", "w_out", "w_ff1", "w_ff2", "w_ple_proj", "w_ple_gate")
    shards = {n: given[n][0] for n in big_names}
    travel = dict(shards, w_in=jnp.transpose(shards["w_in"]))
    rest = [n for n in big_names if n != "w_in"]
    cast, gathered = _cast_shards_beside_gather([travel[n] for n in rest], place[:1],
                                                [_cast_shard(travel["w_in"], place[:1])])
    bufs = dict(zip(rest + ["w_in"], cast + gathered))
    dx, landed, exchange_in, small_grads, sq = _step(x[0], p[0, 0], loss_target[0], small, bufs, place)

    out_grad, out_delta, out_m, out_v = {}, {}, {}, {}

    def update(n, g, exchange=None):
        to = jnp.transpose if n == "w_in" else (lambda a: a)
        (delta, new_m, new_v, g_out), got = _adamw(to(shards[n]), g, to(given["m_" + n][0]), to(given["v_" + n][0]),
                                                   exchange)
        out_grad[n], out_delta[n], out_m[n], out_v[n] = [to(a)[None] for a in (g_out, delta, new_m, new_v)]
        return got

    spare = jnp.zeros((8, 128), F32)
    small_packed = _pack_small([small_grads[n] for n in SMALL_NAMES] + [spare]).astype(BF16)
    early = [n for n in big_names if n != "w_in"]
    reduced, (small_gathered, sq_gathered, landed_in) = _sibling_allgather(
        [_chip_sum(landed[n], place) for n in early], _Both(_Both(_GatherAll(small_packed), _GatherAll(sq)), exchange_in))
    for n, r in zip(early, reduced):
        update(n, _whole(r))
    (reduced_in,), _ = _sibling_allgather([_chip_sum(landed_in, place)], _Nothing())
    update("w_in", _whole(reduced_in))

    like = [given[n] for n in SMALL_NAMES] + [spare]
    packed = _small_update(small_gathered, *[_pack_small([given[pre + n] for n in SMALL_NAMES] + [spare])
                                             for pre in ("", "m_", "v_")])
    for res, out in zip(packed, (out_grad, out_delta, out_m, out_v)):
        out.update(zip(SMALL_NAMES, _unpack_small(res, like)))
    loss = 0.5 * jnp.sum(sq_gathered[:, 0, 0]) / D

    order = ("norm1_gain", "w_in", "gmlp_v_gain", "w_spatial", "b_spatial", "attn_sinks", "rel_bias_table", "w_out",
             "norm2_gain", "w_ff1", "w_ff2", "w_ple_proj", "w_ple_gate", "final_gain")
    return (loss, dx[None], *[out_grad[n] for n in order], *[out_delta[n] for n in order],
            *[out_m[n] for n in order], *[out_v[n] for n in order])
```

```python
import functools
import math

import jax
import jax.numpy as jnp
from jax import lax
from jax.experimental import pallas as pl
from jax.experimental.pallas import tpu as pltpu

S = 2048
D = 1024
D_IN = 1792
D_FF = 4096
PLE = 256
N_CHIP = 4
N_GROUP = 4
CHUNK = 128
N_HEAD = 8
N_BLOCK = S // CHUNK
N_BUCKET = 32
EPS = 1e-6
NEG_INF = -1e30
QK_SCALE = 0.125
GELU_C = math.sqrt(2.0 / math.pi)

ADAM_LR = 0.001
ADAM_B1 = 0.9
ADAM_B2 = 0.999
ADAM_EPS = 1e-08
ADAM_WD = 0.01
ADAM_STEP = 10

F32 = jnp.float32
BF16 = jnp.bfloat16
MIB = 1024 * 1024
MESH = pl.DeviceIdType.MESH

NT = (((1,), (1,)), ((), ()))
TN = (((0,), (0,)), ((), ()))


def _dot(a, b):
    return jnp.dot(a, b, preferred_element_type=F32)


def _dot_nt(a, b):
    return lax.dot_general(a, b, NT, preferred_element_type=F32)


def _dot_tn(a, b):
    return lax.dot_general(a, b, TN, preferred_element_type=F32)


def _params(vmem_mib, n_axes=1):
    return pltpu.CompilerParams(dimension_semantics=("arbitrary",) * n_axes, vmem_limit_bytes=vmem_mib * MIB)


def _rms_scale(v):
    return lax.rsqrt(jnp.mean(v * v, axis=-1, keepdims=True) + EPS)


def _rms_bwd(dy_gain, xhat, r):
    return r * (dy_gain - xhat * jnp.mean(dy_gain * xhat, axis=-1, keepdims=True))


class _Gather:
    def __init__(self, bufs):
        self.operands = list(bufs)
        self.n_out = len(self.operands)
        self.out_shape = [_hbm_like(b) for b in bufs]
        self.aliases = {w: w for w in range(self.n_out)}
        self.sems = _gather_sems(self.n_out)

    def start(self, ins, outs, sems):
        _gather_start(outs, *sems)

    def finish(self, ins, outs, sems):
        _gather_finish(outs, *sems)


class _RelayGather(_Gather):
    TOP, BOTTOM = 6, 7
    DIAGONAL_PASSED = 5
    MIDDLE_AT, LATE_AT = (5, 8), (7, 8)

    def __init__(self, bufs):
        super().__init__(bufs)
        self.sems = [pltpu.SemaphoreType.DMA((self.n_out, 8)), pltpu.SemaphoreType.DMA((self.n_out, 8))]

    def _copies(self, bufs, send_sems, recv_sems):
        x, y, c, others = _mesh_place()
        me = 2 * x + y
        idx = [2 * ox + oy for ox, oy in others]
        sibling = (x, y, 1 - c)
        direct, passed, relayed = [], [], []
        for w, buf in enumerate(bufs):
            rows = buf.shape[2] // 2
            upper, lower = pl.ds(0, rows), pl.ds(rows, rows)
            for k in (0, 1):
                mine = buf.at[me, c]
                direct.append((_remote(mine, mine, send_sems.at[w, k], recv_sems.at[w, k], (*others[k], c)),
                               buf.at[idx[k], c], w, k))
            for k in (0, 1, 2):
                here = buf.at[idx[k], c]
                passed.append((_remote(here, here, send_sems.at[w, 3 + k], recv_sems.at[w, 3 + k], sibling),
                               buf.at[idx[k], 1 - c], w, 3 + k))
            from_x, from_y = buf.at[idx[0], c, upper], buf.at[idx[1], c, lower]
            relayed.append((_remote(from_x, from_x, send_sems.at[w, self.TOP], recv_sems.at[w, self.TOP],
                                    (*others[1], c)), buf.at[idx[2], c, upper], w, self.TOP))
            relayed.append((_remote(from_y, from_y, send_sems.at[w, self.BOTTOM], recv_sems.at[w, self.BOTTOM],
                                    (*others[0], c)), buf.at[idx[2], c, lower], w, self.BOTTOM))
        return direct, passed, relayed

    @staticmethod
    def _landed(piece, send_sems, recv_sems, w, col):
        x, y, c, _ = _mesh_place()
        _remote(piece, piece, send_sems.at[w, col], recv_sems.at[w, col], (x, y, c)).wait_recv()

    def start(self, ins, outs, sems):
        for cp, _, _, _ in self._copies(outs, *sems)[0]:
            cp.start()

    def middle(self, ins, outs, sems):
        direct, passed, relayed = self._copies(outs, *sems)
        for _, piece, w, col in direct:
            self._landed(piece, *sems, w, col)
        for cp, _, _, col in passed:
            if col != self.DIAGONAL_PASSED:
                cp.start()
        for cp, _, _, _ in relayed:
            cp.start()

    def late(self, ins, outs, sems):
        direct, passed, relayed = self._copies(outs, *sems)
        for _, piece, w, col in relayed:
            self._landed(piece, *sems, w, col)
        for cp, _, _, col in passed:
            if col == self.DIAGONAL_PASSED:
                cp.start()

    def finish(self, ins, outs, sems):
        direct, passed, relayed = self._copies(outs, *sems)
        for _, piece, w, col in passed:
            self._landed(piece, *sems, w, col)
        for cp, _, _, _ in direct + passed + relayed:
            cp.wait_send()


class _ChipExchange:
    def __init__(self, sums, landing):
        self.n_out = len(landing)
        self.operands = list(sums) + list(landing)
        self.out_shape = [_hbm_like(b) for b in landing]
        self.aliases = {self.n_out + w: w for w in range(self.n_out)}
        self.sems = _chip_exchange_sems(self.n_out)

    def start(self, ins, outs, sems):
        _chip_exchange_start(ins[:self.n_out], outs, *sems)

    def finish(self, ins, outs, sems):
        _chip_exchange_finish(ins[:self.n_out], outs, *sems)


class _GatherAll:
    def __init__(self, packed):
        self.operands = [packed]
        self.n_out = 1
        self.out_shape = [_hbm_like(packed, (8,) + packed.shape)]
        self.aliases = {}
        self.sems = [pltpu.SemaphoreType.DMA((8,)), pltpu.SemaphoreType.DMA((8,))]

    def _copies(self, ins, outs, sems):
        x, y, c, _ = _mesh_place()
        me = 4 * x + 2 * y + c
        send_sems, recv_sems = sems
        copies = []
        for k in range(1, 8):
            peer = (1 - x if k // 4 else x, 1 - y if (k // 2) % 2 else y, 1 - c if k % 2 else c)
            src = 4 * peer[0] + 2 * peer[1] + peer[2]
            copies.append((_remote(ins[0], outs[0].at[me], send_sems.at[k], recv_sems.at[k], peer), outs[0].at[src]))
        own = pltpu.make_async_copy(ins[0], outs[0].at[me], send_sems.at[0])
        return own, copies

    def start(self, ins, outs, sems):
        own, copies = self._copies(ins, outs, sems)
        own.start()
        for cp, _ in copies:
            cp.start()

    def finish(self, ins, outs, sems):
        own, copies = self._copies(ins, outs, sems)
        x, y, c, _ = _mesh_place()
        for k, (cp, landed) in enumerate(copies):
            _remote(landed, landed, sems[0].at[k + 1], sems[1].at[k + 1], (x, y, c)).wait_recv()
        for cp, _ in copies:
            cp.wait_send()
        own.wait()


class _Nothing:
    operands, n_out, out_shape, aliases, sems = [], 0, [], {}, []

    def start(self, ins, outs, sems):
        pass

    def finish(self, ins, outs, sems):
        pass


class _Both:
    def __init__(self, a, b):
        self.a, self.b = a, b
        self.operands = a.operands + b.operands
        self.n_out = a.n_out + b.n_out
        self.out_shape = a.out_shape + b.out_shape
        self.aliases = dict(a.aliases)
        self.aliases.update({len(a.operands) + i: a.n_out + o for i, o in b.aliases.items()})
        self.sems = a.sems + b.sems

    def _split(self, ins, outs, sems):
        ka, na, sa = len(self.a.operands), self.a.n_out, len(self.a.sems)
        return (ins[:ka], outs[:na], sems[:sa]), (ins[ka:], outs[na:], sems[sa:])

    def start(self, ins, outs, sems):
        for ex, args in zip((self.a, self.b), self._split(ins, outs, sems)):
            ex.start(*args)

    def finish(self, ins, outs, sems):
        for ex, args in zip((self.a, self.b), self._split(ins, outs, sems)):
            ex.finish(*args)


class _SiblingExchange:
    def __init__(self, grads):
        self.operands = list(grads)
        self.n_out = len(self.operands)
        self.out_shape = [_hbm_like(g, (N_CHIP,) + g.shape[2:]) for g in grads]
        self.aliases = {}
        self.sems = _sibling_exchange_sems(self.n_out)

    def start(self, ins, outs, sems):
        _sibling_exchange_start(ins, outs, *sems)

    def finish(self, ins, outs, sems):
        _sibling_exchange_finish(ins, outs, *sems)


def _call(body, operands, *, grid, in_specs, out_specs, out_shape, name, compiler_params, scratch_shapes=(),
          exchange=None):
    operands = [o if getattr(spec, "memory_space", None) == pltpu.SMEM else _in_hbm(o)
                for o, spec in zip(operands, in_specs)]
    out_shape = [pltpu.HBM(s.shape, s.dtype) for s in out_shape]
    if exchange is None:
        res = pl.pallas_call(body, grid=grid, in_specs=in_specs, out_specs=out_specs, out_shape=out_shape, name=name,
                             scratch_shapes=list(scratch_shapes), compiler_params=compiler_params)(*operands)
        return list(res), []
    n_in, n_out, n_scr = len(in_specs), len(out_specs), len(scratch_shapes)
    k_in, k_out = len(exchange.operands), exchange.n_out

    def fused(*refs):
        ins, refs = refs[:n_in], refs[n_in:]
        ex_ins, refs = refs[:k_in], refs[k_in:]
        outs, refs = refs[:n_out], refs[n_out:]
        ex_outs, refs = refs[:k_out], refs[k_out:]
        scratch, sems = refs[:n_scr], refs[n_scr:]
        ids = [pl.program_id(a) for a in range(len(grid))]
        first = functools.reduce(jnp.logical_and, [i == 0 for i in ids])
        last = functools.reduce(jnp.logical_and, [i == g - 1 for i, g in zip(ids, grid)])

        @pl.when(first)
        def _():
            exchange.start(ex_ins, ex_outs, sems)

        def at_step(numerator, denominator):
            at = (numerator * math.prod(grid)) // denominator
            place = [(at // math.prod(grid[a + 1:])) % grid[a] for a in range(len(grid))]
            return functools.reduce(jnp.logical_and, [i == p for i, p in zip(ids, place)])

        if hasattr(exchange, "middle"):
            @pl.when(at_step(*exchange.MIDDLE_AT))
            def _():
                exchange.middle(ex_ins, ex_outs, sems)

            @pl.when(at_step(*exchange.LATE_AT))
            def _():
                exchange.late(ex_ins, ex_outs, sems)

        body(*ins, *outs, *scratch)

        @pl.when(last)
        def _():
            exchange.finish(ex_ins, ex_outs, sems)

    res = pl.pallas_call(
        fused, grid=grid, name=name,
        in_specs=list(in_specs) + [HBM_SPEC] * k_in, out_specs=list(out_specs) + [HBM_SPEC] * k_out,
        out_shape=list(out_shape) + exchange.out_shape,
        input_output_aliases={n_in + i: n_out + o for i, o in exchange.aliases.items()},
        scratch_shapes=list(scratch_shapes) + exchange.sems, compiler_params=compiler_params,
    )(*operands, *[_in_hbm(o) for o in exchange.operands])
    return list(res[:n_out]), list(res[n_out:])


def _in_hbm(a):
    return pltpu.with_memory_space_constraint(a, pltpu.HBM)


def _row_tile(h):
    return max(t for t in range(16, 513, 16) if h % t == 0)


def _cast_shard(a, chip):
    rows, cols = a.shape
    h = rows // 2
    tr = _row_tile(h)

    def body(chip_ref, a_ref, o_ref):
        o_ref[0, 0] = a_ref[0].astype(BF16)

    return pl.pallas_call(
        body, name="cast_shard",
        grid_spec=pltpu.PrefetchScalarGridSpec(
            num_scalar_prefetch=1, grid=(2, h // tr),
            in_specs=[pl.BlockSpec((1, tr, cols), lambda s, r, chip_ref: (s, r, 0))],
            out_specs=pl.BlockSpec((1, 1, tr, cols), lambda s, r, chip_ref: (chip_ref[0], s, r, 0))),
        out_shape=pltpu.HBM((N_CHIP, 2, h, cols), BF16),
        compiler_params=_params(16, 2),
    )(chip, _in_hbm(a.reshape(2, h, cols)))


def _cast_shards_beside_gather(arrays, chip, gathered):
    n, k = len(arrays), len(gathered)
    shapes = [(a.shape[0] // 2, a.shape[1]) for a in arrays]

    def body(chip_ref, *refs):
        ins, refs = refs[:n], refs[n + k:]
        outs, refs = refs[:n], refs[n:]
        bufs, sems = refs[:k], refs[k:]
        half = pl.program_id(0)

        @pl.when(half == 0)
        def _():
            _gather_start(bufs, *sems)

        for a_ref, o_ref in zip(ins, outs):
            o_ref[0, 0] = a_ref[0].astype(BF16)

        @pl.when(half == 1)
        def _():
            _gather_finish(bufs, *sems)

    res = pl.pallas_call(
        body, name="cast_shards",
        grid_spec=pltpu.PrefetchScalarGridSpec(
            num_scalar_prefetch=1, grid=(2,),
            in_specs=[pl.BlockSpec((1, h, c), lambda s, chip_ref: (s, 0, 0)) for h, c in shapes] + [HBM_SPEC] * k,
            out_specs=[pl.BlockSpec((1, 1, h, c), lambda s, chip_ref: (chip_ref[0], s, 0, 0)) for h, c in shapes]
            + [HBM_SPEC] * k,
            scratch_shapes=_gather_sems(k)),
        out_shape=[pltpu.HBM((N_CHIP, 2, h, c), BF16) for h, c in shapes] + [_hbm_like(b) for b in gathered],
        input_output_aliases={1 + n + i: n + i for i in range(k)},
        compiler_params=_params(32),
    )(chip, *[_in_hbm(a.reshape(2, h, c)) for a, (h, c) in zip(arrays, shapes)], *gathered)
    return list(res[:n]), list(res[n:])


def _in_proj(x, gain1, w_in_t, exchange=None):
    tm = 256

    def body(x_ref, g_ref, w_ref, z_ref, hn_ref):
        xv = x_ref[...]
        hn = (xv * _rms_scale(xv) * g_ref[...]).astype(BF16)
        hn_ref[...] = hn
        z_ref[...] = _dot_nt(hn, w_ref[...])

    return _call(
        body, (x, gain1, w_in_t), grid=(S // tm,), name="in_proj",
        in_specs=[pl.BlockSpec((tm, D), lambda i: (i, 0)), pl.BlockSpec((1, D), lambda i: (0, 0)),
                  pl.BlockSpec((D_IN, D), lambda i: (0, 0))],
        out_specs=[pl.BlockSpec((tm, D_IN), lambda i: (i, 0)), pl.BlockSpec((tm, D), lambda i: (i, 0))],
        out_shape=[jax.ShapeDtypeStruct((S, D_IN), F32), jax.ShapeDtypeStruct((S, D), BF16)],
        compiler_params=_params(40), exchange=exchange)


def _gelu_parts(v):
    t = jnp.tanh(GELU_C * (v + 0.044715 * (v * v * v)))
    cdf = 0.5 * (1.0 + t)
    return cdf, t


def _band_mask(n):
    a = lax.broadcasted_iota(jnp.int32, (CHUNK, 2 * CHUNK), 0)
    j = lax.broadcasted_iota(jnp.int32, (CHUNK, 2 * CHUNK), 1)
    dist = CHUNK + a - j
    valid = (dist >= 0) & (dist < CHUNK)
    return valid & ((n > 0) | (j >= CHUNK))


def _fill_bias(bucket_ref, table_ref, bias_ref):
    bucket = bucket_ref[...]
    for h in range(N_HEAD):
        acc = jnp.zeros((CHUNK, 2 * CHUNK), F32)
        for b in range(N_BUCKET):
            acc = jnp.where(bucket == b, table_ref[b, h], acc)
        bias_ref[h] = acc


def _fill_tril(ws_ref, wt_ref, wtt_ref=None):
    r = lax.broadcasted_iota(jnp.int32, (CHUNK, CHUNK), 0)
    c = lax.broadcasted_iota(jnp.int32, (CHUNK, CHUNK), 1)
    for g in range(N_GROUP):
        w = jnp.where(c <= r, ws_ref[g], 0.0)
        wt_ref[g] = w.astype(BF16)
        if wtt_ref is not None:
            wtt_ref[g] = w.T.astype(BF16)


def _kv_layouts(kv_prev, kv_cur):
    both = jnp.concatenate([kv_prev, kv_cur], axis=0)
    k = both[:, :128]
    v = both[:, 128:]
    return (k.astype(BF16), pltpu.roll(k, 64, axis=1).astype(BF16),
            v.astype(BF16), pltpu.roll(v, 64, axis=1).astype(BF16))


def _head_place(h):
    pair, pos, kvh = h // 2, h % 2, h // 4
    return pair, pos, kvh == pos


def _softmax_sink(qm, k_use, bias_h, sink, valid):
    s = _dot_nt(qm, k_use) * QK_SCALE + bias_h
    s = jnp.where(valid, s, NEG_INF)
    m = jnp.maximum(jnp.max(s, axis=-1, keepdims=True), sink)
    e = jnp.exp(s - m)
    es = jnp.exp(sink - m)
    inv = 1.0 / (jnp.sum(e, axis=-1, keepdims=True) + es)
    return e * inv, es * inv


def _mixer_fwd(z, v_gain, w_spatial, b_spatial_t, sinks, rel_table, bucket, exchange=None):
    def body(z_ref, kvp_ref, gain_ref, ws_ref, bt_ref, sink_ref, table_ref, bucket_ref, out_ref, probs_ref, probs_t_ref,
             share_ref, guv_ref, dgelu_ref, bias_ref, wt_ref):
        n = pl.program_id(0)

        @pl.when(n == 0)
        def _():
            _fill_bias(bucket_ref, table_ref, bias_ref)
            _fill_tril(ws_ref, wt_ref)

        zuv = z_ref[:, :1024]
        cdf, t = _gelu_parts(zuv)
        guv = zuv * cdf
        guv_ref[...] = guv
        dgelu_ref[...] = cdf + zuv * (0.5 * (1.0 - t * t)) * (GELU_C * (1.0 + 3.0 * 0.044715 * (zuv * zuv)))
        for g in range(N_GROUP):
            vg = guv[:, 512 + 128 * g:512 + 128 * (g + 1)]
            vn = vg * _rms_scale(vg) * gain_ref[:, 128 * g:128 * (g + 1)]
            sv = _dot(wt_ref[g], vn.astype(BF16)) + bt_ref[:, g:g + 1]
            out_ref[:, 128 * g:128 * (g + 1)] = (guv[:, 128 * g:128 * (g + 1)] * sv).astype(BF16)

        k_same, k_swap, v_same, v_swap = _kv_layouts(kvp_ref[...], z_ref[:, 1536:1792])
        valid = _band_mask(n)
        lane = lax.broadcasted_iota(jnp.int32, (1, 128), 1)
        lane_half = lane // 64
        shares = jnp.zeros((CHUNK, 128), F32)
        for pair in range(N_HEAD // 2):
            qq = z_ref[:, 1024 + 128 * pair:1024 + 128 * (pair + 1)]
            acc = jnp.zeros((CHUNK, 128), F32)
            for pos in range(2):
                h = 2 * pair + pos
                _, _, same = _head_place(h)
                qm = jnp.where(lane_half == pos, qq, 0.0).astype(BF16)
                p, p_sink = _softmax_sink(qm, k_same if same else k_swap, bias_ref[h], sink_ref[h], valid)
                pb = p.astype(BF16)
                probs_ref[0, h] = pb
                probs_t_ref[0, h] = p.T.astype(BF16)
                shares = jnp.where(lane == h, p_sink, shares)
                vm = jnp.where(lane_half == pos, v_same if same else v_swap, jnp.zeros((), BF16))
                acc = acc + _dot(pb, vm)
            out_ref[:, 512 + 128 * pair:512 + 128 * (pair + 1)] = acc.astype(BF16)
        share_ref[...] = shares

    return _call(
        body, (z, z, v_gain, w_spatial, b_spatial_t, sinks, rel_table, bucket), grid=(N_BLOCK,), name="mixer_fwd",
        in_specs=[pl.BlockSpec((CHUNK, D_IN), lambda n: (n, 0)),
                  pl.BlockSpec((CHUNK, 256), lambda n: (jnp.maximum(n - 1, 0), 6)),
                  pl.BlockSpec((1, 512), lambda n: (0, 0)),
                  pl.BlockSpec((N_GROUP, CHUNK, CHUNK), lambda n: (0, 0, 0)),
                  pl.BlockSpec((CHUNK, N_GROUP), lambda n: (0, 0)),
                  pl.BlockSpec(memory_space=pltpu.SMEM),
                  pl.BlockSpec(memory_space=pltpu.SMEM),
                  pl.BlockSpec((CHUNK, 2 * CHUNK), lambda n: (0, 0))],
        out_specs=[pl.BlockSpec((CHUNK, D), lambda n: (n, 0)),
                   pl.BlockSpec((1, N_HEAD, CHUNK, 2 * CHUNK), lambda n: (n, 0, 0, 0)),
                   pl.BlockSpec((1, N_HEAD, 2 * CHUNK, CHUNK), lambda n: (n, 0, 0, 0)),
                   pl.BlockSpec((CHUNK, 128), lambda n: (n, 0)),
                   pl.BlockSpec((CHUNK, 1024), lambda n: (n, 0)), pl.BlockSpec((CHUNK, 1024), lambda n: (n, 0))],
        out_shape=[jax.ShapeDtypeStruct((S, D), BF16), jax.ShapeDtypeStruct((N_BLOCK, N_HEAD, CHUNK, 2 * CHUNK), BF16),
                   jax.ShapeDtypeStruct((N_BLOCK, N_HEAD, 2 * CHUNK, CHUNK), BF16), jax.ShapeDtypeStruct((S, 128), F32),
                   jax.ShapeDtypeStruct((S, 1024), F32), jax.ShapeDtypeStruct((S, 1024), F32)],
        scratch_shapes=[pltpu.VMEM((N_HEAD, CHUNK, 2 * CHUNK), F32), pltpu.VMEM((N_GROUP, CHUNK, CHUNK), BF16)],
        compiler_params=_params(32), exchange=exchange)


def _out_proj(x, mix, w_out, gain2, exchange=None):
    tm = 256

    def body(x_ref, mix_ref, w_ref, g_ref, h1_ref, hn_ref, hnt_ref):
        h1 = x_ref[...] + _dot(mix_ref[...], w_ref[...])
        h1_ref[...] = h1
        hn = h1 * _rms_scale(h1) * g_ref[...]
        hn_ref[...] = hn.astype(BF16)
        hnt_ref[...] = hn.T.astype(BF16)

    return _call(
        body, (x, mix, w_out, gain2), grid=(S // tm,), name="out_proj",
        in_specs=[pl.BlockSpec((tm, D), lambda i: (i, 0)), pl.BlockSpec((tm, D), lambda i: (i, 0)),
                  pl.BlockSpec((D, D), lambda i: (0, 0)), pl.BlockSpec((1, D), lambda i: (0, 0))],
        out_specs=[pl.BlockSpec((tm, D), lambda i: (i, 0)), pl.BlockSpec((tm, D), lambda i: (i, 0)),
                   pl.BlockSpec((D, tm), lambda i: (0, i))],
        out_shape=[jax.ShapeDtypeStruct((S, D), F32), jax.ShapeDtypeStruct((S, D), BF16),
                   jax.ShapeDtypeStruct((D, S), BF16)],
        compiler_params=_params(32), exchange=exchange)


def _ffn_up(hn2, w_ff1, exchange=None):
    tm = 512
    nj = D_FF // 1024

    def body(hn_ref, w1_ref, r_ref, a_ref, at_ref):
        r = jnp.maximum(_dot(hn_ref[...], w1_ref[0]), 0.0)
        r_ref[...] = r.astype(BF16)
        a = r * r
        a_ref[...] = a.astype(BF16)
        at_ref[...] = a.T.astype(BF16)

    return _call(
        body, (hn2, w_ff1), grid=(nj, S // tm), name="ffn_up",
        in_specs=[pl.BlockSpec((tm, D), lambda j, i: (i, 0)), pl.BlockSpec((1, D, 1024), lambda j, i: (j, 0, 0))],
        out_specs=[pl.BlockSpec((tm, 1024), lambda j, i: (i, j)), pl.BlockSpec((tm, 1024), lambda j, i: (i, j)),
                   pl.BlockSpec((1024, tm), lambda j, i: (j, i))],
        out_shape=[jax.ShapeDtypeStruct((S, D_FF), BF16), jax.ShapeDtypeStruct((S, D_FF), BF16),
                   jax.ShapeDtypeStruct((D_FF, S), BF16)],
        compiler_params=_params(40, 2), exchange=exchange)


def _ffn_down(h1, a, w_ff2, exchange=None):
    tm = 1024
    nj = D_FF // 1024

    def body(h1_ref, a_ref, w2_ref, h2_ref, acc_ref):
        j = pl.program_id(1)
        part = _dot(a_ref[...], w2_ref[0])

        @pl.when(j == 0)
        def _():
            acc_ref[...] = part

        @pl.when(j > 0)
        def _():
            acc_ref[...] += part

        @pl.when(j == nj - 1)
        def _():
            h2_ref[...] = h1_ref[...] + acc_ref[...]

    return _call(
        body, (h1, a, w_ff2), grid=(S // tm, nj), name="ffn_down",
        in_specs=[pl.BlockSpec((tm, D), lambda i, j: (i, 0)), pl.BlockSpec((tm, 1024), lambda i, j: (i, j)),
                  pl.BlockSpec((1, 1024, D), lambda i, j: (j, 0, 0))],
        out_specs=[pl.BlockSpec((tm, D), lambda i, j: (i, 0))],
        out_shape=[jax.ShapeDtypeStruct((S, D), F32)],
        scratch_shapes=[pltpu.VMEM((tm, D), F32)],
        compiler_params=_params(48, 2), exchange=exchange)


def _tail(h2, p, target, w_gate, w_proj, final_gain):
    tm = 256
    steps = S // tm

    def body(h2_ref, p_ref, t_ref, wg_ref, wp_ref, gf_ref, dh2_ref, dwg_ref, dwp_ref, dgf_ref, loss_ref, dh2b_ref,
             dwp_acc):
        i = pl.program_id(0)
        h2 = h2_ref[...]
        h2b = h2.astype(BF16)
        pb = p_ref[...].astype(BF16)
        gate = jax.nn.sigmoid(_dot(h2b, wg_ref[...]))
        pp = jnp.concatenate([_dot(pb, wp_ref[j]) for j in range(N_CHIP)], axis=1)
        h3 = h2 + gate * pp
        r3 = _rms_scale(h3)
        xhat = h3 * r3
        gf = gf_ref[...]
        err = xhat * gf - t_ref[...]
        dy = err * (1.0 / D)
        dh3 = _rms_bwd(dy * gf, xhat, r3)
        dgp = (dh3 * pp * gate * (1.0 - gate)).astype(BF16)
        dpp = (dh3 * gate).astype(BF16)
        dh2 = dh3 + _dot_nt(dgp, wg_ref[...])
        dh2_ref[...] = dh2
        dh2b_ref[...] = dh2.astype(BF16)
        dwg = _dot_tn(h2b, dgp)
        dwp = _dot_tn(pb, dpp)
        dgf = jnp.sum(dy * xhat, axis=0, keepdims=True)
        sq = jnp.sum(jnp.sum(err * err, axis=1, keepdims=True), axis=0, keepdims=True)

        @pl.when(i == 0)
        def _():
            dwg_ref[...] = dwg
            dwp_acc[...] = dwp
            dgf_ref[...] = dgf
            loss_ref[...] = jnp.broadcast_to(sq, (8, 128))

        @pl.when(i > 0)
        def _():
            dwg_ref[...] += dwg
            dwp_acc[...] += dwp
            dgf_ref[...] += dgf
            loss_ref[...] += jnp.broadcast_to(sq, (8, 128))

        @pl.when(i == steps - 1)
        def _():
            for j in range(N_CHIP):
                dwp_ref[j] = dwp_acc[:, 256 * j:256 * (j + 1)]

    return _call(
        body, (h2, p, target, w_gate, w_proj, final_gain), grid=(steps,), name="tail",
        in_specs=[pl.BlockSpec((tm, D), lambda i: (i, 0)), pl.BlockSpec((tm, PLE), lambda i: (i, 0)),
                  pl.BlockSpec((tm, D), lambda i: (i, 0)), pl.BlockSpec((D, D), lambda i: (0, 0)),
                  pl.BlockSpec((N_CHIP, PLE, 256), lambda i: (0, 0, 0)), pl.BlockSpec((1, D), lambda i: (0, 0))],
        out_specs=[pl.BlockSpec((tm, D), lambda i: (i, 0)), pl.BlockSpec((D, D), lambda i: (0, 0)),
                   pl.BlockSpec((N_CHIP, PLE, 256), lambda i: (0, 0, 0)), pl.BlockSpec((1, D), lambda i: (0, 0)),
                   pl.BlockSpec((8, 128), lambda i: (0, 0)), pl.BlockSpec((tm, D), lambda i: (i, 0))],
        out_shape=[jax.ShapeDtypeStruct((S, D), F32), jax.ShapeDtypeStruct((D, D), F32),
                   jax.ShapeDtypeStruct((N_CHIP, PLE, 256), F32), jax.ShapeDtypeStruct((1, D), F32),
                   jax.ShapeDtypeStruct((8, 128), F32), jax.ShapeDtypeStruct((S, D), BF16)],
        scratch_shapes=[pltpu.VMEM((PLE, D), F32)],
        compiler_params=_params(48))[0]


def _ffn_bwd_down(dh2b, r, a_t, w_ff2, exchange=None):
    tm = 1024
    nj = D_FF // 1024

    def body(dh2_ref, r_ref, at_ref, w2_ref, df_ref, dw2_ref):
        i = pl.program_id(1)
        dh2b = dh2_ref[...]
        da = _dot_nt(dh2b, w2_ref[0])
        df_ref[...] = (da * (2.0 * r_ref[...].astype(F32))).astype(BF16)
        dw2 = _dot(at_ref[...], dh2b)

        @pl.when(i == 0)
        def _():
            dw2_ref[0] = dw2

        @pl.when(i > 0)
        def _():
            dw2_ref[0] += dw2

    return _call(
        body, (dh2b, r, a_t, w_ff2), grid=(nj, S // tm), name="ffn_bwd_down",
        in_specs=[pl.BlockSpec((tm, D), lambda j, i: (i, 0)), pl.BlockSpec((tm, 1024), lambda j, i: (i, j)),
                  pl.BlockSpec((1024, tm), lambda j, i: (j, i)), pl.BlockSpec((1, 1024, D), lambda j, i: (j, 0, 0))],
        out_specs=[pl.BlockSpec((tm, 1024), lambda j, i: (i, j)), pl.BlockSpec((1, 1024, D), lambda j, i: (j, 0, 0))],
        out_shape=[jax.ShapeDtypeStruct((S, D_FF), BF16), jax.ShapeDtypeStruct((nj, 1024, D), F32)],
        compiler_params=_params(48, 2), exchange=exchange)


def _ffn_bwd_up(df, hn2_t, exchange=None):
    tm = 2048
    nj = D_FF // 1024

    def body(df_ref, hnt_ref, dw1_ref):
        i = pl.program_id(1)
        dw1 = _dot(hnt_ref[...], df_ref[...])

        @pl.when(i == 0)
        def _():
            dw1_ref[0] = dw1

        @pl.when(i > 0)
        def _():
            dw1_ref[0] += dw1

    return _call(
        body, (df, hn2_t), grid=(nj, S // tm), name="ffn_bwd_up",
        in_specs=[pl.BlockSpec((tm, 1024), lambda j, i: (i, j)), pl.BlockSpec((D, tm), lambda j, i: (0, i))],
        out_specs=[pl.BlockSpec((1, D, 1024), lambda j, i: (j, 0, 0))],
        out_shape=[jax.ShapeDtypeStruct((nj, D, 1024), F32)],
        compiler_params=_params(40, 2), exchange=exchange)


def _ffn_bwd_input(df, w_ff1, dh2, h1, gain2, mix, w_out, exchange=None):
    tm = 512
    nj = D_FF // 1024
    steps = S // tm

    def body(df_ref, w1_ref, dh2_ref, h1_ref, g_ref, mix_ref, wo_ref, dh1_ref, dmix_ref, dwo_ref, dg_ref, acc_ref):
        i = pl.program_id(0)
        j = pl.program_id(1)
        part = _dot_nt(df_ref[...], w1_ref[j])

        @pl.when(j == 0)
        def _():
            acc_ref[...] = part

        @pl.when(j > 0)
        def _():
            acc_ref[...] += part

        @pl.when(j == nj - 1)
        def _():
            dhn = acc_ref[...]
            h1 = h1_ref[...]
            r2 = _rms_scale(h1)
            xhat = h1 * r2
            dh1 = dh2_ref[...] + _rms_bwd(dhn * g_ref[...], xhat, r2)
            dh1_ref[...] = dh1
            dh1b = dh1.astype(BF16)
            dmix_ref[...] = _dot_nt(dh1b, wo_ref[...])
            dwo = _dot_tn(mix_ref[...], dh1b)
            dg = jnp.sum(dhn * xhat, axis=0, keepdims=True)

            @pl.when(i == 0)
            def _():
                dwo_ref[...] = dwo
                dg_ref[...] = dg

            @pl.when(i > 0)
            def _():
                dwo_ref[...] += dwo
                dg_ref[...] += dg

    return _call(
        body, (df, w_ff1, dh2, h1, gain2, mix, w_out), grid=(steps, nj), name="ffn_bwd_input",
        in_specs=[pl.BlockSpec((tm, 1024), lambda i, j: (i, j)),
                  pl.BlockSpec((nj, D, 1024), lambda i, j: (0, 0, 0), pipeline_mode=pl.Buffered(1)),
                  pl.BlockSpec((tm, D), lambda i, j: (i, 0)), pl.BlockSpec((tm, D), lambda i, j: (i, 0)),
                  pl.BlockSpec((1, D), lambda i, j: (0, 0)), pl.BlockSpec((tm, D), lambda i, j: (i, 0)),
                  pl.BlockSpec((D, D), lambda i, j: (0, 0), pipeline_mode=pl.Buffered(1))],
        out_specs=[pl.BlockSpec((tm, D), lambda i, j: (i, 0)), pl.BlockSpec((tm, D), lambda i, j: (i, 0)),
                   pl.BlockSpec((D, D), lambda i, j: (0, 0)), pl.BlockSpec((1, D), lambda i, j: (0, 0))],
        out_shape=[jax.ShapeDtypeStruct((S, D), F32), jax.ShapeDtypeStruct((S, D), F32),
                   jax.ShapeDtypeStruct((D, D), F32), jax.ShapeDtypeStruct((1, D), F32)],
        scratch_shapes=[pltpu.VMEM((tm, D), F32)],
        compiler_params=_params(56, 2), exchange=exchange)


IN_GROUP = 8


def _mixer_bwd(z, dmix, v_gain, w_spatial, b_spatial_t, saved, bucket, hn1, exchange=None):
    def body(z_ref, kvp_ref, dm_ref, gain_ref, ws_ref, bt_ref, probs_ref, probs_t_ref, share_ref, guv_ref, dgelu_ref,
             bucket_ref, hn_ref,
             dz_ref, dws_ref, db_ref, dgain_ref, dsink_ref, drel_ref, dwin_ref,
             wt_ref, wtt_ref, dbias_ref, dsv_ref, carry_ref):
        n = pl.program_id(0)

        @pl.when(n == 0)
        def _():
            _fill_tril(ws_ref, wt_ref, wtt_ref)
            dwin_ref[...] = jnp.zeros_like(dwin_ref)
            dbias_ref[...] = jnp.zeros_like(dbias_ref)
            dsv_ref[...] = jnp.zeros_like(dsv_ref)
            dws_ref[...] = jnp.zeros_like(dws_ref)
            dgain_ref[...] = jnp.zeros_like(dgain_ref)
            dsink_ref[...] = jnp.zeros_like(dsink_ref)

        rows = pl.ds(pl.multiple_of(n * CHUNK, CHUNK), CHUNK)

        guv = guv_ref[...]
        dgelu = dgelu_ref[...]
        for g in range(N_GROUP):
            lo, hi = 128 * g, 128 * (g + 1)
            u = guv[:, lo:hi]
            vg = guv[:, 512 + lo:512 + hi]
            rr = _rms_scale(vg)
            vhat = vg * rr
            gain = gain_ref[:, lo:hi]
            vnb = (vhat * gain).astype(BF16)
            sv = _dot(wt_ref[g], vnb) + bt_ref[:, g:g + 1]
            da = dm_ref[:, lo:hi]
            dsv = da * u
            dsvb = dsv.astype(BF16)
            dsv_ref[g] += dsv
            dws_ref[g] += _dot_nt(dsvb, vnb)
            dvn = _dot(wtt_ref[g], dsvb)
            dgain_ref[:, lo:hi] += jnp.sum(dvn * vhat, axis=0, keepdims=True)
            dvg = _rms_bwd(dvn * gain, vhat, rr)
            dz_ref[rows, lo:hi] = (da * sv * dgelu[:, lo:hi]).astype(BF16)
            dz_ref[rows, 512 + lo:512 + hi] = (dvg * dgelu[:, 512 + lo:512 + hi]).astype(BF16)

        k_same, k_swap, v_same, v_swap = _kv_layouts(kvp_ref[...], z_ref[:, 1536:1792])
        lane_half = lax.broadcasted_iota(jnp.int32, (1, 128), 1) // 64
        zero = jnp.zeros((2 * CHUNK, 128), F32)
        dk_same, dk_swap, dv_same, dv_swap = zero, zero, zero, zero
        for pair in range(N_HEAD // 2):
            cols = slice(1024 + 128 * pair, 1024 + 128 * (pair + 1))
            qq = z_ref[:, cols]
            do_pair = dm_ref[:, 512 + 128 * pair:512 + 128 * (pair + 1)]
            dq = jnp.zeros((CHUNK, 128), F32)
            for pos in range(2):
                h = 2 * pair + pos
                _, _, same = _head_place(h)
                on_half = lane_half == pos
                qm = jnp.where(on_half, qq, 0.0).astype(BF16)
                k_use = k_same if same else k_swap
                v_use = v_same if same else v_swap
                pb = probs_ref[0, h]
                p = pb.astype(F32)
                p_sink = share_ref[:, h:h + 1]
                dom = jnp.where(on_half, do_pair, 0.0).astype(BF16)
                dp = _dot_nt(dom, v_use)
                dsum = jnp.sum(p * dp, axis=-1, keepdims=True)
                ds = p * (dp - dsum)
                dbias_ref[h] += ds
                dsink_ref[h:h + 1, :] += jnp.broadcast_to(jnp.sum(-p_sink * dsum, axis=0, keepdims=True), (1, 128))
                dsb = ds.astype(BF16)
                dq = dq + jnp.where(on_half, _dot(dsb, k_use), 0.0)
                dk_h = _dot_tn(dsb, qm)
                dv_h = _dot(probs_t_ref[0, h], dom)
                if same:
                    dk_same, dv_same = dk_same + dk_h, dv_same + dv_h
                else:
                    dk_swap, dv_swap = dk_swap + dk_h, dv_swap + dv_h
            dz_ref[rows, cols] = (dq * QK_SCALE).astype(BF16)
        dk = (dk_same + pltpu.roll(dk_swap, 64, axis=1)) * QK_SCALE
        dv = dv_same + pltpu.roll(dv_swap, 64, axis=1)
        dkv = jnp.concatenate([dk, dv], axis=1)

        @pl.when(n > 0)
        def _():
            prev_rows = pl.ds(pl.multiple_of((n - 1) * CHUNK, CHUNK), CHUNK)
            dz_ref[prev_rows, 1536:1792] = (carry_ref[...] + dkv[:CHUNK]).astype(BF16)

        carry_ref[...] = dkv[CHUNK:]

        @pl.when((n > 0) & (n % IN_GROUP == 0))
        def _():
            done = pl.ds(pl.multiple_of((n - IN_GROUP) * CHUNK, IN_GROUP * CHUNK), IN_GROUP * CHUNK)
            dwin_ref[...] += _dot_tn(dz_ref[done, :], hn_ref[...])

        @pl.when(n == N_BLOCK - 1)
        def _():
            dz_ref[rows, 1536:1792] = dkv[CHUNK:].astype(BF16)
            last = pl.ds((N_BLOCK - IN_GROUP) * CHUNK, IN_GROUP * CHUNK)
            dwin_ref[...] += _dot_tn(dz_ref[last, :], hn_ref[...])
            r = lax.broadcasted_iota(jnp.int32, (CHUNK, CHUNK), 0)
            c = lax.broadcasted_iota(jnp.int32, (CHUNK, CHUNK), 1)
            for g in range(N_GROUP):
                dws_ref[g] = jnp.where(c <= r, dws_ref[g], 0.0)
                db_ref[g] = jnp.sum(dsv_ref[g], axis=1, keepdims=True)
            bucket = bucket_ref[...]
            for h in range(N_HEAD):
                dbh = dbias_ref[h]
                per_bucket = [jnp.sum(jnp.where(bucket == b, dbh, 0.0), axis=0, keepdims=True) for b in range(N_BUCKET)]
                drel_ref[h] = jnp.sum(jnp.concatenate(per_bucket, axis=0), axis=1, keepdims=True)

    def hn_group(n):
        return jnp.where(n == N_BLOCK - 1, N_BLOCK // IN_GROUP - 1, jnp.maximum(n // IN_GROUP - 1, 0))

    return _call(
        body, (z, z, dmix, v_gain, w_spatial, b_spatial_t, *saved, bucket, hn1), grid=(N_BLOCK,),
        name="mixer_bwd",
        in_specs=[pl.BlockSpec((CHUNK, D_IN), lambda n: (n, 0)),
                  pl.BlockSpec((CHUNK, 256), lambda n: (jnp.maximum(n - 1, 0), 6)),
                  pl.BlockSpec((CHUNK, D), lambda n: (n, 0)),
                  pl.BlockSpec((1, 512), lambda n: (0, 0)),
                  pl.BlockSpec((N_GROUP, CHUNK, CHUNK), lambda n: (0, 0, 0)),
                  pl.BlockSpec((CHUNK, N_GROUP), lambda n: (0, 0)),
                  pl.BlockSpec((1, N_HEAD, CHUNK, 2 * CHUNK), lambda n: (n, 0, 0, 0)),
                  pl.BlockSpec((1, N_HEAD, 2 * CHUNK, CHUNK), lambda n: (n, 0, 0, 0)),
                  pl.BlockSpec((CHUNK, 128), lambda n: (n, 0)),
                  pl.BlockSpec((CHUNK, 1024), lambda n: (n, 0)), pl.BlockSpec((CHUNK, 1024), lambda n: (n, 0)),
                  pl.BlockSpec((CHUNK, 2 * CHUNK), lambda n: (0, 0)),
                  pl.BlockSpec((IN_GROUP * CHUNK, D), lambda n: (hn_group(n), 0))],
        out_specs=[pl.BlockSpec((S, D_IN), lambda n: (0, 0)),
                   pl.BlockSpec((N_GROUP, CHUNK, CHUNK), lambda n: (0, 0, 0)),
                   pl.BlockSpec((N_GROUP, CHUNK, 1), lambda n: (0, 0, 0)),
                   pl.BlockSpec((1, 512), lambda n: (0, 0)),
                   pl.BlockSpec((N_HEAD, 128), lambda n: (0, 0)),
                   pl.BlockSpec((N_HEAD, N_BUCKET, 1), lambda n: (0, 0, 0)),
                   pl.BlockSpec((D_IN, D), lambda n: (0, 0))],
        out_shape=[jax.ShapeDtypeStruct((S, D_IN), BF16), jax.ShapeDtypeStruct((N_GROUP, CHUNK, CHUNK), F32),
                   jax.ShapeDtypeStruct((N_GROUP, CHUNK, 1), F32), jax.ShapeDtypeStruct((1, 512), F32),
                   jax.ShapeDtypeStruct((N_HEAD, 128), F32), jax.ShapeDtypeStruct((N_HEAD, N_BUCKET, 1), F32),
                   jax.ShapeDtypeStruct((D_IN, D), F32)],
        scratch_shapes=[pltpu.VMEM((N_GROUP, CHUNK, CHUNK), BF16),
                        pltpu.VMEM((N_GROUP, CHUNK, CHUNK), BF16), pltpu.VMEM((N_HEAD, CHUNK, 2 * CHUNK), F32),
                        pltpu.VMEM((N_GROUP, CHUNK, CHUNK), F32), pltpu.VMEM((CHUNK, 256), F32)],
        compiler_params=_params(56), exchange=exchange)


def _in_bwd_input(dz, w_in_t, x, dh1, gain1, exchange=None):
    tm = 512

    def body(dz_ref, w_ref, x_ref, dh1_ref, g_ref, dx_ref, dg_ref):
        i = pl.program_id(0)
        dhn = _dot(dz_ref[...], w_ref[...])
        xv = x_ref[...]
        r1 = _rms_scale(xv)
        xhat = xv * r1
        dx_ref[...] = dh1_ref[...] + _rms_bwd(dhn * g_ref[...], xhat, r1)
        dg = jnp.sum(dhn * xhat, axis=0, keepdims=True)

        @pl.when(i == 0)
        def _():
            dg_ref[...] = dg

        @pl.when(i > 0)
        def _():
            dg_ref[...] += dg

    return _call(
        body, (dz, w_in_t, x, dh1, gain1), grid=(S // tm,), name="in_bwd_input",
        in_specs=[pl.BlockSpec((tm, D_IN), lambda i: (i, 0)), pl.BlockSpec((D_IN, D), lambda i: (0, 0)),
                  pl.BlockSpec((tm, D), lambda i: (i, 0)), pl.BlockSpec((tm, D), lambda i: (i, 0)),
                  pl.BlockSpec((1, D), lambda i: (0, 0))],
        out_specs=[pl.BlockSpec((tm, D), lambda i: (i, 0)), pl.BlockSpec((1, D), lambda i: (0, 0))],
        out_shape=[jax.ShapeDtypeStruct((S, D), F32), jax.ShapeDtypeStruct((1, D), F32)],
        compiler_params=_params(48), exchange=exchange)


def _rel_bucket():
    a = jnp.arange(CHUNK)[:, None]
    j = jnp.arange(2 * CHUNK)[None, :]
    n = jnp.maximum(CHUNK + a - j, 0)
    max_exact = N_BUCKET // 2
    nf = jnp.maximum(n, 1).astype(jnp.float32)
    large = max_exact + (jnp.log(nf / max_exact) / math.log(CHUNK / max_exact) * (N_BUCKET - max_exact)).astype(jnp.int32)
    large = jnp.minimum(large, N_BUCKET - 1)
    return jnp.where(n < max_exact, n, large).astype(jnp.int32)


def _step(x, p, target, small, bufs, place):
    bucket = _rel_bucket()
    sinks = small["attn_sinks"].reshape(N_HEAD)
    b_t = jnp.transpose(small["b_spatial"].reshape(N_GROUP, CHUNK))
    ws = small["w_spatial"].reshape(N_GROUP, CHUNK, CHUNK)
    gain1, gain2 = small["norm1_gain"], small["norm2_gain"]
    v_gain = small["gmlp_v_gain"]
    final_gain = small["final_gain"].reshape(1, D)
    table = small["rel_bias_table"]
    bufs = dict(bufs)

    def gather(*names):
        return _RelayGather([bufs[n] for n in names])

    def took(names, got):
        bufs.update(zip(names, got))

    w_in_t = _whole(bufs["w_in"]).reshape(D_IN, D)
    (z, hn1), got = _in_proj(x, gain1, w_in_t, gather("w_out"))
    took(["w_out"], got)
    (mix, *saved), got = _mixer_fwd(z, v_gain, ws, b_t, sinks, table, bucket, gather("w_ff1"))
    took(["w_ff1"], got)
    w_out = _whole(bufs["w_out"]).reshape(D, D)
    (h1, hn2, hn2_t), _ = _out_proj(x, mix, w_out, gain2)
    w_ff1 = _whole(bufs["w_ff1"])
    (r, a, a_t), got = _ffn_up(hn2, w_ff1, gather("w_ff2"))
    took(["w_ff2"], got)
    w_ff2 = _whole(bufs["w_ff2"])
    (h2,), got = _ffn_down(h1, a, w_ff2, gather("w_ple_gate", "w_ple_proj"))
    took(["w_ple_gate", "w_ple_proj"], got)
    dh2, d_gate, d_proj, d_final, sq, dh2b = _tail(h2, p, target, _whole(bufs["w_ple_gate"]).reshape(D, D),
                                                   _whole(bufs["w_ple_proj"]), final_gain)

    def pair_sums(halves, from_sibling):
        sums, landing = zip(*[_pair_sum(g, o, place) for g, o in zip(halves, from_sibling)])
        return list(sums), list(landing)

    landed = {}
    halves = [_halves(d_gate.reshape(N_CHIP, 256, D)), _halves(d_proj)]
    (df, d_ff2), got = _ffn_bwd_down(dh2b, r, a_t, w_ff2, _SiblingExchange(halves))
    ex, halves = _ChipExchange(*pair_sums(halves, got)), [_halves(d_ff2)]
    (d_ff1,), got = _ffn_bwd_up(df, hn2_t, _Both(ex, _SiblingExchange(halves)))
    landed.update(zip(["w_ple_gate", "w_ple_proj"], got[:2]))
    ex, halves = _ChipExchange(*pair_sums(halves, got[2:])), [_halves(d_ff1)]
    (dh1, dmix, d_out, d_gain2), got = _ffn_bwd_input(df, w_ff1, dh2, h1, gain2, mix, w_out,
                                                      _Both(ex, _SiblingExchange(halves)))
    landed["w_ff2"] = got[0]
    ex, halves = _ChipExchange(*pair_sums(halves, got[1:])), [_halves(d_out.reshape(N_CHIP, 256, D))]
    (dz, d_ws, d_b, d_vgain, d_sink, d_rel, d_in_t), got = _mixer_bwd(z, dmix, v_gain, ws, b_t, saved, bucket, hn1,
                                                                     _Both(ex, _SiblingExchange(halves)))
    landed["w_ff1"] = got[0]
    small_grads = {
        "gmlp_v_gain": d_vgain, "w_spatial": d_ws.reshape(1, N_GROUP, CHUNK, CHUNK),
        "b_spatial": d_b.reshape(1, N_GROUP, CHUNK), "attn_sinks": d_sink[:, 0].reshape(1, N_HEAD),
        "rel_bias_table": jnp.transpose(d_rel.reshape(N_HEAD, N_BUCKET)), "norm2_gain": d_gain2,
        "final_gain": d_final.reshape(D),
    }
    ex, halves = _ChipExchange(*pair_sums(halves, got[1:])), [_halves(d_in_t.reshape(N_CHIP, 448, D))]
    (dx, small_grads["norm1_gain"]), got = _in_bwd_input(dz, w_in_t, x, dh1, gain1, _Both(ex, _SiblingExchange(halves)))
    landed["w_out"] = got[0]
    return dx, landed, _ChipExchange(*pair_sums(halves, got[1:])), small_grads, sq


HBM_SPEC = pl.BlockSpec(memory_space=pltpu.HBM)
VMEM_SPEC = pl.BlockSpec(memory_space=pltpu.VMEM)


def _mesh_place():
    x, y, c = lax.axis_index("x"), lax.axis_index("y"), lax.axis_index("c")
    others = [(1 - x, y), (x, 1 - y), (1 - x, 1 - y)]
    return x, y, c, others


def _remote(src, dst, send_sem, recv_sem, device):
    return pltpu.make_async_remote_copy(src_ref=src, dst_ref=dst, send_sem=send_sem, recv_sem=recv_sem,
                                        device_id=device, device_id_type=MESH)


def _hbm_like(a, shape=None, dtype=None):
    return pltpu.HBM(a.shape if shape is None else shape, a.dtype if dtype is None else dtype)


def _gather_start(bufs, send_sems, recv_sems):
    x, y, c, others = _mesh_place()
    me = 2 * x + y
    for w, buf in enumerate(bufs):
        for k in range(3):
            mine = buf.at[me, c]
            _remote(mine, mine, send_sems.at[w, k], recv_sems.at[w, k], (*others[k], c)).start()


def _gather_finish(bufs, send_sems, recv_sems):
    x, y, c, others = _mesh_place()
    me = 2 * x + y
    sibling = (x, y, 1 - c)
    idx = [2 * ox + oy for ox, oy in others]
    chips = range(3)
    for w, buf in enumerate(bufs):
        for k in chips:
            landed = buf.at[idx[k], c]
            _remote(landed, landed, send_sems.at[w, k], recv_sems.at[w, k], sibling).wait_recv()
            _remote(landed, landed, send_sems.at[w, 3 + k], recv_sems.at[w, 3 + k], sibling).start()
    for w, buf in enumerate(bufs):
        for k in chips:
            landed = buf.at[idx[k], 1 - c]
            _remote(landed, landed, send_sems.at[w, 3 + k], recv_sems.at[w, 3 + k], sibling).wait_recv()
    for w, buf in enumerate(bufs):
        for k in chips:
            mine, passed = buf.at[me, c], buf.at[idx[k], c]
            _remote(mine, mine, send_sems.at[w, k], recv_sems.at[w, k], sibling).wait_send()
            _remote(passed, passed, send_sems.at[w, 3 + k], recv_sems.at[w, 3 + k], sibling).wait_send()


def _gather_sems(n):
    return [pltpu.SemaphoreType.DMA((n, 6)), pltpu.SemaphoreType.DMA((n, 6))]


def _sibling_copies(grads, landing, send_sems, recv_sems):
    x, y, c, _ = _mesh_place()
    return [_remote(grads[w].at[j, 1 - c], landing[w].at[j], send_sems.at[w, j], recv_sems.at[w, j], (x, y, 1 - c))
            for w in range(len(grads)) for j in range(N_CHIP)]


def _sibling_exchange_start(grads, landing, send_sems, recv_sems):
    for cp in _sibling_copies(grads, landing, send_sems, recv_sems):
        cp.start()


def _sibling_exchange_finish(grads, landing, send_sems, recv_sems):
    copies = _sibling_copies(grads, landing, send_sems, recv_sems)
    for cp in copies:
        cp.wait_recv()
    for cp in copies:
        cp.wait_send()


def _sibling_exchange_sems(n):
    return [pltpu.SemaphoreType.DMA((n, N_CHIP)), pltpu.SemaphoreType.DMA((n, N_CHIP))]


def _sibling_exchange(grads):
    n = len(grads)

    def body(*refs):
        ins, outs = refs[:n], refs[n:2 * n]
        _sibling_exchange_start(ins, outs, *refs[2 * n:])
        _sibling_exchange_finish(ins, outs, *refs[2 * n:])

    return pl.pallas_call(
        body, name="sibling_exchange",
        in_specs=[HBM_SPEC] * n, out_specs=[HBM_SPEC] * n,
        out_shape=[_hbm_like(g, (N_CHIP,) + g.shape[2:]) for g in grads],
        scratch_shapes=_sibling_exchange_sems(n),
    )(*[_in_hbm(g) for g in grads])


def _chip_exchange_start(sums, landing, send_sems, recv_sems):
    x, y, c, others = _mesh_place()
    me = 2 * x + y
    for w in range(len(sums)):
        for k, (ox, oy) in enumerate(others):
            _remote(sums[w].at[2 * ox + oy], landing[w].at[me], send_sems.at[w, k], recv_sems.at[w, k],
                    (ox, oy, c)).start()


def _chip_exchange_finish(sums, landing, send_sems, recv_sems):
    x, y, c, others = _mesh_place()
    for w in range(len(sums)):
        for k, (ox, oy) in enumerate(others):
            piece = landing[w].at[2 * ox + oy]
            _remote(piece, piece, send_sems.at[w, k], recv_sems.at[w, k], (x, y, c)).wait_recv()
    for w in range(len(sums)):
        for k, (ox, oy) in enumerate(others):
            piece = sums[w].at[2 * ox + oy]
            _remote(piece, piece, send_sems.at[w, k], recv_sems.at[w, k], (x, y, c)).wait_send()


def _chip_exchange_sems(n):
    return [pltpu.SemaphoreType.DMA((n, 3)), pltpu.SemaphoreType.DMA((n, 3))]


def _sibling_allgather(bufs, also):
    n = len(bufs)
    k_in, k_out = len(also.operands), also.n_out

    def body(*refs):
        ex_ins, refs = refs[n:n + k_in], refs[n + k_in:]
        outs, refs = refs[:n], refs[n:]
        ex_outs, refs = refs[:k_out], refs[k_out:]
        send_sems, recv_sems, ex_sems = refs[0], refs[1], refs[2:]
        x, y, c, _ = _mesh_place()
        sibling = (x, y, 1 - c)
        also.start(ex_ins, ex_outs, ex_sems)
        sends = [_remote(outs[w].at[c], outs[w].at[c], send_sems.at[w], recv_sems.at[w], sibling) for w in range(n)]
        for cp in sends:
            cp.start()
        for w in range(n):
            landed = outs[w].at[1 - c]
            _remote(landed, landed, send_sems.at[w], recv_sems.at[w], sibling).wait_recv()
        for cp in sends:
            cp.wait_send()
        also.finish(ex_ins, ex_outs, ex_sems)

    res = pl.pallas_call(
        body, name="sibling_allgather",
        in_specs=[HBM_SPEC] * (n + k_in), out_specs=[HBM_SPEC] * (n + k_out),
        out_shape=[_hbm_like(b) for b in bufs] + also.out_shape,
        input_output_aliases={**{w: w for w in range(n)}, **{n + i: n + o for i, o in also.aliases.items()}},
        scratch_shapes=[pltpu.SemaphoreType.DMA((n,)), pltpu.SemaphoreType.DMA((n,))] + also.sems,
    )(*bufs, *[_in_hbm(o) for o in also.operands])
    return list(res[:n]), list(res[n:])


def _pair_sum(grad, other, place):
    _, _, h, cols = grad.shape
    tr = _row_tile(h)

    def body(place_ref, g_ref, o_ref, sums_ref, own_ref):
        s = (g_ref[0, 0] + o_ref[0]).astype(BF16)
        sums_ref[0] = s

        @pl.when(pl.program_id(1) == place_ref[0])
        def _():
            own_ref[0] = s

    return pl.pallas_call(
        body, name="pair_sum",
        grid_spec=pltpu.PrefetchScalarGridSpec(
            num_scalar_prefetch=1, grid=(h // tr, N_CHIP),
            in_specs=[pl.BlockSpec((1, 1, tr, cols), lambda r, j, place_ref: (j, place_ref[1], r, 0)),
                      pl.BlockSpec((1, tr, cols), lambda r, j, place_ref: (j, r, 0))],
            out_specs=[pl.BlockSpec((1, tr, cols), lambda r, j, place_ref: (j, r, 0)),
                       pl.BlockSpec((1, tr, cols), lambda r, j, place_ref: (place_ref[0], r, 0))]),
        out_shape=[pltpu.HBM((N_CHIP, h, cols), BF16)] * 2,
        compiler_params=_params(32, 2),
    )(place, _in_hbm(grad), _in_hbm(other))


def _chip_sum(parts, place):
    _, h, cols = parts.shape
    tr = _row_tile(h)

    def body(place_ref, p_ref, out_ref):
        out_ref[0] = ((p_ref[0].astype(F32) + p_ref[1].astype(F32)) + p_ref[2].astype(F32)) + p_ref[3].astype(F32)

    return pl.pallas_call(
        body, name="chip_sum",
        grid_spec=pltpu.PrefetchScalarGridSpec(
            num_scalar_prefetch=1, grid=(h // tr,),
            in_specs=[pl.BlockSpec((N_CHIP, tr, cols), lambda r, place_ref: (0, r, 0))],
            out_specs=pl.BlockSpec((1, tr, cols), lambda r, place_ref: (place_ref[1], r, 0))),
        out_shape=pltpu.HBM((2, h, cols), F32),
        compiler_params=_params(32),
    )(place, _in_hbm(parts))


def _adamw_math(w, g, m, v):
    m = ADAM_B1 * m + (1.0 - ADAM_B1) * g
    v = ADAM_B2 * v + (1.0 - ADAM_B2) * (g * g)
    m_hat = m / (1.0 - ADAM_B1 ** ADAM_STEP)
    v_hat = v / (1.0 - ADAM_B2 ** ADAM_STEP)
    delta = -ADAM_LR * (m_hat / (jnp.sqrt(v_hat) + ADAM_EPS) + ADAM_WD * w)
    return delta, m, v


def _adamw(w, g, m, v, exchange=None):
    rows, cols = w.shape
    tr = _row_tile(rows)

    def body(w_ref, g_ref, m_ref, v_ref, d_ref, nm_ref, nv_ref, g_out_ref):
        g = g_ref[...]
        d_ref[...], nm_ref[...], nv_ref[...] = _adamw_math(w_ref[...], g, m_ref[...], v_ref[...])
        g_out_ref[...] = g

    spec = pl.BlockSpec((tr, cols), lambda r: (r, 0))
    return _call(
        body, (w, g, m, v), grid=(rows // tr,), name="adamw",
        in_specs=[spec] * 4, out_specs=[spec] * 4,
        out_shape=[jax.ShapeDtypeStruct((rows, cols), F32)] * 4,
        compiler_params=_params(48), exchange=exchange)


SMALL_NAMES = ("norm1_gain", "gmlp_v_gain", "w_spatial", "b_spatial", "attn_sinks", "rel_bias_table", "norm2_gain",
               "final_gain")
PACK_TILE = 8 * 128


def _pack_small(arrays):
    parts = []
    for a in arrays:
        flat = a.reshape(-1)
        rows = -(-flat.shape[0] // PACK_TILE) * 8
        parts.append(jnp.pad(flat, (0, rows * 128 - flat.shape[0])).reshape(rows, 128))
    return jnp.concatenate(parts, axis=0)


def _unpack_small(packed, like):
    out, row = [], 0
    for a in like:
        size = math.prod(a.shape)
        rows = -(-size // PACK_TILE) * 8
        out.append(packed[row:row + rows].reshape(-1)[:size].reshape(a.shape))
        row += rows
    return out


def _small_update(gathered, w, m, v):
    rows = gathered.shape[1]

    def body(g_ref, w_ref, m_ref, v_ref, tot_ref, d_ref, nm_ref, nv_ref):
        total = g_ref[0].astype(F32)
        for dev in range(1, 8):
            total = total + g_ref[dev].astype(F32)
        tot_ref[...] = total
        d_ref[...], nm_ref[...], nv_ref[...] = _adamw_math(w_ref[...], total, m_ref[...], v_ref[...])

    return pl.pallas_call(
        body, name="small_update",
        in_specs=[VMEM_SPEC] * 4, out_specs=[VMEM_SPEC] * 4,
        out_shape=[jax.ShapeDtypeStruct((rows, 128), F32)] * 4,
        compiler_params=pltpu.CompilerParams(vmem_limit_bytes=24 * MIB),
    )(gathered, w, m, v)


def _halves(a):
    return a.reshape(a.shape[:-2] + (2, a.shape[-2] // 2, a.shape[-1]))


def _whole(a):
    return a.reshape(a.shape[:-3] + (2 * a.shape[-2], a.shape[-1]))


def kernel(x, p, norm1_gain, w_in, gmlp_v_gain, w_spatial, b_spatial, attn_sinks, rel_bias_table, w_out, norm2_gain, w_ff1, w_ff2, w_ple_proj, w_ple_gate, final_gain, loss_target, m_norm1_gain, m_w_in, m_gmlp_v_gain, m_w_spatial, m_b_spatial, m_attn_sinks, m_rel_bias_table, m_w_out, m_norm2_gain, m_w_ff1, m_w_ff2, m_w_ple_proj, m_w_ple_gate, m_final_gain, v_norm1_gain, v_w_in, v_gmlp_v_gain, v_w_spatial, v_b_spatial, v_attn_sinks, v_rel_bias_table, v_w_out, v_norm2_gain, v_w_ff1, v_w_ff2, v_w_ple_proj, v_w_ple_gate, v_final_gain):
    given = dict(locals())
    small = {n: given[n] for n in SMALL_NAMES}
    chip = 2 * lax.axis_index("x") + lax.axis_index("y")
    place = jnp.stack([chip, lax.axis_index("c")]).astype(jnp.int32)

    big_names = ("w_in", "w_out", "w_ff1", "w_ff2", "w_ple_proj", "w_ple_gate")
    shards = {n: given[n][0] for n in big_names}
    travel = dict(shards, w_in=jnp.transpose(shards["w_in"]))
    rest = [n for n in big_names if n != "w_in"]
    cast, gathered = _cast_shards_beside_gather([travel[n] for n in rest], place[:1],
                                                [_cast_shard(travel["w_in"], place[:1])])
    bufs = dict(zip(rest + ["w_in"], cast + gathered))
    dx, landed, exchange_in, small_grads, sq = _step(x[0], p[0, 0], loss_target[0], small, bufs, place)

    out_grad, out_delta, out_m, out_v = {}, {}, {}, {}

    def update(n, g, exchange=None):
        to = jnp.transpose if n == "w_in" else (lambda a: a)
        (delta, new_m, new_v, g_out), got = _adamw(to(shards[n]), g, to(given["m_" + n][0]), to(given["v_" + n][0]),
                                                   exchange)
        out_grad[n], out_delta[n], out_m[n], out_v[n] = [to(a)[None] for a in (g_out, delta, new_m, new_v)]
        return got

    spare = jnp.zeros((8, 128), F32)
    small_packed = _pack_small([small_grads[n] for n in SMALL_NAMES] + [spare]).astype(BF16)
    early = [n for n in big_names if n != "w_in"]
    reduced, (small_gathered, sq_gathered, landed_in) = _sibling_allgather(
        [_chip_sum(landed[n], place) for n in early], _Both(_Both(_GatherAll(small_packed), _GatherAll(sq)), exchange_in))
    for n, r in zip(early, reduced):
        update(n, _whole(r))
    (reduced_in,), _ = _sibling_allgather([_chip_sum(landed_in, place)], _Nothing())
    update("w_in", _whole(reduced_in))

    like = [given[n] for n in SMALL_NAMES] + [spare]
    packed = _small_update(small_gathered, *[_pack_small([given[pre + n] for n in SMALL_NAMES] + [spare])
                                             for pre in ("", "m_", "v_")])
    for res, out in zip(packed, (out_grad, out_delta, out_m, out_v)):
        out.update(zip(SMALL_NAMES, _unpack_small(res, like)))
    loss = 0.5 * jnp.sum(sq_gathered[:, 0, 0]) / D

    order = ("norm1_gain", "w_in", "gmlp_v_gain", "w_spatial", "b_spatial", "attn_sinks", "rel_bias_table", "w_out",
             "norm2_gain", "w_ff1", "w_ff2", "w_ple_proj", "w_ple_gate", "final_gain")
    return (loss, dx[None], *[out_grad[n] for n in order], *[out_delta[n] for n in order],
            *[out_m[n] for n in order], *[out_v[n] for n in order])
```

```python
import functools
import math

import jax
import jax.numpy as jnp
from jax import lax
from jax.experimental import pallas as pl
from jax.experimental.pallas import tpu as pltpu

S = 2048
D = 1024
D_IN = 1792
D_FF = 4096
PLE = 256
N_CHIP = 4
N_GROUP = 4
CHUNK = 128
N_HEAD = 8
N_BLOCK = S // CHUNK
N_BUCKET = 32
EPS = 1e-6
NEG_INF = -1e30
QK_SCALE = 0.125
GELU_C = math.sqrt(2.0 / math.pi)

ADAM_LR = 0.001
ADAM_B1 = 0.9
ADAM_B2 = 0.999
ADAM_EPS = 1e-08
ADAM_WD = 0.01
ADAM_STEP = 10

F32 = jnp.float32
BF16 = jnp.bfloat16
MIB = 1024 * 1024
MESH = pl.DeviceIdType.MESH

NT = (((1,), (1,)), ((), ()))
TN = (((0,), (0,)), ((), ()))


def _dot(a, b):
    return jnp.dot(a, b, preferred_element_type=F32)


def _dot_nt(a, b):
    return lax.dot_general(a, b, NT, preferred_element_type=F32)


def _dot_tn(a, b):
    return lax.dot_general(a, b, TN, preferred_element_type=F32)


def _params(vmem_mib, n_axes=1):
    return pltpu.CompilerParams(dimension_semantics=("arbitrary",) * n_axes, vmem_limit_bytes=vmem_mib * MIB)


def _rms_scale(v):
    return lax.rsqrt(jnp.mean(v * v, axis=-1, keepdims=True) + EPS)


def _rms_bwd(dy_gain, xhat, r):
    return r * (dy_gain - xhat * jnp.mean(dy_gain * xhat, axis=-1, keepdims=True))


class _Gather:
    def __init__(self, bufs):
        self.operands = list(bufs)
        self.n_out = len(self.operands)
        self.out_shape = [_hbm_like(b) for b in bufs]
        self.aliases = {w: w for w in range(self.n_out)}
        self.sems = _gather_sems(self.n_out)

    def start(self, ins, outs, sems):
        _gather_start(outs, *sems)

    def finish(self, ins, outs, sems):
        _gather_finish(outs, *sems)


class _RelayGather(_Gather):
    TOP, BOTTOM = 6, 7
    DIAGONAL_PASSED = 5
    MIDDLE_AT, LATE_AT = (5, 8), (7, 8)

    def __init__(self, bufs):
        super().__init__(bufs)
        self.sems = [pltpu.SemaphoreType.DMA((self.n_out, 8)), pltpu.SemaphoreType.DMA((self.n_out, 8))]

    def _copies(self, bufs, send_sems, recv_sems):
        x, y, c, others = _mesh_place()
        me = 2 * x + y
        idx = [2 * ox + oy for ox, oy in others]
        sibling = (x, y, 1 - c)
        direct, passed, relayed = [], [], []
        for w, buf in enumerate(bufs):
            rows = buf.shape[2] // 2
            upper, lower = pl.ds(0, rows), pl.ds(rows, rows)
            for k in (0, 1):
                mine = buf.at[me, c]
                direct.append((_remote(mine, mine, send_sems.at[w, k], recv_sems.at[w, k], (*others[k], c)),
                               buf.at[idx[k], c], w, k))
            for k in (0, 1, 2):
                here = buf.at[idx[k], c]
                passed.append((_remote(here, here, send_sems.at[w, 3 + k], recv_sems.at[w, 3 + k], sibling),
                               buf.at[idx[k], 1 - c], w, 3 + k))
            from_x, from_y = buf.at[idx[0], c, upper], buf.at[idx[1], c, lower]
            relayed.append((_remote(from_x, from_x, send_sems.at[w, self.TOP], recv_sems.at[w, self.TOP],
                                    (*others[1], c)), buf.at[idx[2], c, upper], w, self.TOP))
            relayed.append((_remote(from_y, from_y, send_sems.at[w, self.BOTTOM], recv_sems.at[w, self.BOTTOM],
                                    (*others[0], c)), buf.at[idx[2], c, lower], w, self.BOTTOM))
        return direct, passed, relayed

    @staticmethod
    def _landed(piece, send_sems, recv_sems, w, col):
        x, y, c, _ = _mesh_place()
        _remote(piece, piece, send_sems.at[w, col], recv_sems.at[w, col], (x, y, c)).wait_recv()

    def start(self, ins, outs, sems):
        for cp, _, _, _ in self._copies(outs, *sems)[0]:
            cp.start()

    def middle(self, ins, outs, sems):
        direct, passed, relayed = self._copies(outs, *sems)
        for _, piece, w, col in direct:
            self._landed(piece, *sems, w, col)
        for cp, _, _, col in passed:
            if col != self.DIAGONAL_PASSED:
                cp.start()
        for cp, _, _, _ in relayed:
            cp.start()

    def late(self, ins, outs, sems):
        direct, passed, relayed = self._copies(outs, *sems)
        for _, piece, w, col in relayed:
            self._landed(piece, *sems, w, col)
        for cp, _, _, col in passed:
            if col == self.DIAGONAL_PASSED:
                cp.start()

    def finish(self, ins, outs, sems):
        direct, passed, relayed = self._copies(outs, *sems)
        for _, piece, w, col in passed:
            self._landed(piece, *sems, w, col)
        for cp, _, _, _ in direct + passed + relayed:
            cp.wait_send()


class _ChipExchange:
    def __init__(self, sums, landing):
        self.n_out = len(landing)
        self.operands = list(sums) + list(landing)
        self.out_shape = [_hbm_like(b) for b in landing]
        self.aliases = {self.n_out + w: w for w in range(self.n_out)}
        self.sems = _chip_exchange_sems(self.n_out)

    def start(self, ins, outs, sems):
        _chip_exchange_start(ins[:self.n_out], outs, *sems)

    def finish(self, ins, outs, sems):
        _chip_exchange_finish(ins[:self.n_out], outs, *sems)


class _GatherAll:
    def __init__(self, packed):
        self.operands = [packed]
        self.n_out = 1
        self.out_shape = [_hbm_like(packed, (8,) + packed.shape)]
        self.aliases = {}
        self.sems = [pltpu.SemaphoreType.DMA((8,)), pltpu.SemaphoreType.DMA((8,))]

    def _copies(self, ins, outs, sems):
        x, y, c, _ = _mesh_place()
        me = 4 * x + 2 * y + c
        send_sems, recv_sems = sems
        copies = []
        for k in range(1, 8):
            peer = (1 - x if k // 4 else x, 1 - y if (k // 2) % 2 else y, 1 - c if k % 2 else c)
            src = 4 * peer[0] + 2 * peer[1] + peer[2]
            copies.append((_remote(ins[0], outs[0].at[me], send_sems.at[k], recv_sems.at[k], peer), outs[0].at[src]))
        own = pltpu.make_async_copy(ins[0], outs[0].at[me], send_sems.at[0])
        return own, copies

    def start(self, ins, outs, sems):
        own, copies = self._copies(ins, outs, sems)
        own.start()
        for cp, _ in copies:
            cp.start()

    def finish(self, ins, outs, sems):
        own, copies = self._copies(ins, outs, sems)
        x, y, c, _ = _mesh_place()
        for k, (cp, landed) in enumerate(copies):
            _remote(landed, landed, sems[0].at[k + 1], sems[1].at[k + 1], (x, y, c)).wait_recv()
        for cp, _ in copies:
            cp.wait_send()
        own.wait()


class _Nothing:
    operands, n_out, out_shape, aliases, sems = [], 0, [], {}, []

    def start(self, ins, outs, sems):
        pass

    def finish(self, ins, outs, sems):
        pass


class _Both:
    def __init__(self, a, b):
        self.a, self.b = a, b
        self.operands = a.operands + b.operands
        self.n_out = a.n_out + b.n_out
        self.out_shape = a.out_shape + b.out_shape
        self.aliases = dict(a.aliases)
        self.aliases.update({len(a.operands) + i: a.n_out + o for i, o in b.aliases.items()})
        self.sems = a.sems + b.sems

    def _split(self, ins, outs, sems):
        ka, na, sa = len(self.a.operands), self.a.n_out, len(self.a.sems)
        return (ins[:ka], outs[:na], sems[:sa]), (ins[ka:], outs[na:], sems[sa:])

    def start(self, ins, outs, sems):
        for ex, args in zip((self.a, self.b), self._split(ins, outs, sems)):
            ex.start(*args)

    def finish(self, ins, outs, sems):
        for ex, args in zip((self.a, self.b), self._split(ins, outs, sems)):
            ex.finish(*args)


class _SiblingExchange:
    def __init__(self, grads):
        self.operands = list(grads)
        self.n_out = len(self.operands)
        self.out_shape = [_hbm_like(g, (N_CHIP,) + g.shape[2:]) for g in grads]
        self.aliases = {}
        self.sems = _sibling_exchange_sems(self.n_out)

    def start(self, ins, outs, sems):
        _sibling_exchange_start(ins, outs, *sems)

    def finish(self, ins, outs, sems):
        _sibling_exchange_finish(ins, outs, *sems)


def _call(body, operands, *, grid, in_specs, out_specs, out_shape, name, compiler_params, scratch_shapes=(),
          exchange=None):
    operands = [o if getattr(spec, "memory_space", None) == pltpu.SMEM else _in_hbm(o)
                for o, spec in zip(operands, in_specs)]
    out_shape = [pltpu.HBM(s.shape, s.dtype) for s in out_shape]
    if exchange is None:
        res = pl.pallas_call(body, grid=grid, in_specs=in_specs, out_specs=out_specs, out_shape=out_shape, name=name,
                             scratch_shapes=list(scratch_shapes), compiler_params=compiler_params)(*operands)
        return list(res), []
    n_in, n_out, n_scr = len(in_specs), len(out_specs), len(scratch_shapes)
    k_in, k_out = len(exchange.operands), exchange.n_out

    def fused(*refs):
        ins, refs = refs[:n_in], refs[n_in:]
        ex_ins, refs = refs[:k_in], refs[k_in:]
        outs, refs = refs[:n_out], refs[n_out:]
        ex_outs, refs = refs[:k_out], refs[k_out:]
        scratch, sems = refs[:n_scr], refs[n_scr:]
        ids = [pl.program_id(a) for a in range(len(grid))]
        first = functools.reduce(jnp.logical_and, [i == 0 for i in ids])
        last = functools.reduce(jnp.logical_and, [i == g - 1 for i, g in zip(ids, grid)])

        @pl.when(first)
        def _():
            exchange.start(ex_ins, ex_outs, sems)

        def at_step(numerator, denominator):
            at = (numerator * math.prod(grid)) // denominator
            place = [(at // math.prod(grid[a + 1:])) % grid[a] for a in range(len(grid))]
            return functools.reduce(jnp.logical_and, [i == p for i, p in zip(ids, place)])

        if hasattr(exchange, "middle"):
            @pl.when(at_step(*exchange.MIDDLE_AT))
            def _():
                exchange.middle(ex_ins, ex_outs, sems)

            @pl.when(at_step(*exchange.LATE_AT))
            def _():
                exchange.late(ex_ins, ex_outs, sems)

        body(*ins, *outs, *scratch)

        @pl.when(last)
        def _():
            exchange.finish(ex_ins, ex_outs, sems)

    res = pl.pallas_call(
        fused, grid=grid, name=name,
        in_specs=list(in_specs) + [HBM_SPEC] * k_in, out_specs=list(out_specs) + [HBM_SPEC] * k_out,
        out_shape=list(out_shape) + exchange.out_shape,
        input_output_aliases={n_in + i: n_out + o for i, o in exchange.aliases.items()},
        scratch_shapes=list(scratch_shapes) + exchange.sems, compiler_params=compiler_params,
    )(*operands, *[_in_hbm(o) for o in exchange.operands])
    return list(res[:n_out]), list(res[n_out:])


def _in_hbm(a):
    return pltpu.with_memory_space_constraint(a, pltpu.HBM)


def _row_tile(h):
    return max(t for t in range(16, 513, 16) if h % t == 0)


def _cast_shard(a, chip):
    rows, cols = a.shape
    h = rows // 2
    tr = _row_tile(h)

    def body(chip_ref, a_ref, o_ref):
        o_ref[0, 0] = a_ref[0].astype(BF16)

    return pl.pallas_call(
        body, name="cast_shard",
        grid_spec=pltpu.PrefetchScalarGridSpec(
            num_scalar_prefetch=1, grid=(2, h // tr),
            in_specs=[pl.BlockSpec((1, tr, cols), lambda s, r, chip_ref: (s, r, 0))],
            out_specs=pl.BlockSpec((1, 1, tr, cols), lambda s, r, chip_ref: (chip_ref[0], s, r, 0))),
        out_shape=pltpu.HBM((N_CHIP, 2, h, cols), BF16),
        compiler_params=_params(16, 2),
    )(chip, _in_hbm(a.reshape(2, h, cols)))


def _cast_shards_beside_gather(arrays, chip, gathered):
    n, k = len(arrays), len(gathered)
    shapes = [(a.shape[0] // 2, a.shape[1]) for a in arrays]

    def body(chip_ref, *refs):
        ins, refs = refs[:n], refs[n + k:]
        outs, refs = refs[:n], refs[n:]
        bufs, sems = refs[:k], refs[k:]
        half = pl.program_id(0)

        @pl.when(half == 0)
        def _():
            _gather_start(bufs, *sems)

        for a_ref, o_ref in zip(ins, outs):
            o_ref[0, 0] = a_ref[0].astype(BF16)

        @pl.when(half == 1)
        def _():
            _gather_finish(bufs, *sems)

    res = pl.pallas_call(
        body, name="cast_shards",
        grid_spec=pltpu.PrefetchScalarGridSpec(
            num_scalar_prefetch=1, grid=(2,),
            in_specs=[pl.BlockSpec((1, h, c), lambda s, chip_ref: (s, 0, 0)) for h, c in shapes] + [HBM_SPEC] * k,
            out_specs=[pl.BlockSpec((1, 1, h, c), lambda s, chip_ref: (chip_ref[0], s, 0, 0)) for h, c in shapes]
            + [HBM_SPEC] * k,
            scratch_shapes=_gather_sems(k)),
        out_shape=[pltpu.HBM((N_CHIP, 2, h, c), BF16) for h, c in shapes] + [_hbm_like(b) for b in gathered],
        input_output_aliases={1 + n + i: n + i for i in range(k)},
        compiler_params=_params(32),
    )(chip, *[_in_hbm(a.reshape(2, h, c)) for a, (h, c) in zip(arrays, shapes)], *gathered)
    return list(res[:n]), list(res[n:])


def _in_proj(x, gain1, w_in_t, exchange=None):
    tm = 256

    def body(x_ref, g_ref, w_ref, z_ref, hn_ref):
        xv = x_ref[...]
        hn = (xv * _rms_scale(xv) * g_ref[...]).astype(BF16)
        hn_ref[...] = hn
        z_ref[...] = _dot_nt(hn, w_ref[...])

    return _call(
        body, (x, gain1, w_in_t), grid=(S // tm,), name="in_proj",
        in_specs=[pl.BlockSpec((tm, D), lambda i: (i, 0)), pl.BlockSpec((1, D), lambda i: (0, 0)),
                  pl.BlockSpec((D_IN, D), lambda i: (0, 0))],
        out_specs=[pl.BlockSpec((tm, D_IN), lambda i: (i, 0)), pl.BlockSpec((tm, D), lambda i: (i, 0))],
        out_shape=[jax.ShapeDtypeStruct((S, D_IN), F32), jax.ShapeDtypeStruct((S, D), BF16)],
        compiler_params=_params(40), exchange=exchange)


def _gelu_parts(v):
    t = jnp.tanh(GELU_C * (v + 0.044715 * (v * v * v)))
    cdf = 0.5 * (1.0 + t)
    return cdf, t


def _band_mask(n):
    a = lax.broadcasted_iota(jnp.int32, (CHUNK, 2 * CHUNK), 0)
    j = lax.broadcasted_iota(jnp.int32, (CHUNK, 2 * CHUNK), 1)
    dist = CHUNK + a - j
    valid = (dist >= 0) & (dist < CHUNK)
    return valid & ((n > 0) | (j >= CHUNK))


def _fill_bias(bucket_ref, table_ref, bias_ref):
    bucket = bucket_ref[...]
    for h in range(N_HEAD):
        acc = jnp.zeros((CHUNK, 2 * CHUNK), F32)
        for b in range(N_BUCKET):
            acc = jnp.where(bucket == b, table_ref[b, h], acc)
        bias_ref[h] = acc


def _fill_tril(ws_ref, wt_ref, wtt_ref=None):
    r = lax.broadcasted_iota(jnp.int32, (CHUNK, CHUNK), 0)
    c = lax.broadcasted_iota(jnp.int32, (CHUNK, CHUNK), 1)
    for g in range(N_GROUP):
        w = jnp.where(c <= r, ws_ref[g], 0.0)
        wt_ref[g] = w.astype(BF16)
        if wtt_ref is not None:
            wtt_ref[g] = w.T.astype(BF16)


def _kv_layouts(kv_prev, kv_cur):
    both = jnp.concatenate([kv_prev, kv_cur], axis=0)
    k = both[:, :128]
    v = both[:, 128:]
    return (k.astype(BF16), pltpu.roll(k, 64, axis=1).astype(BF16),
            v.astype(BF16), pltpu.roll(v, 64, axis=1).astype(BF16))


def _head_place(h):
    pair, pos, kvh = h // 2, h % 2, h // 4
    return pair, pos, kvh == pos


def _softmax_sink(qm, k_use, bias_h, sink, valid):
    s = _dot_nt(qm, k_use) * QK_SCALE + bias_h
    s = jnp.where(valid, s, NEG_INF)
    m = jnp.maximum(jnp.max(s, axis=-1, keepdims=True), sink)
    e = jnp.exp(s - m)
    es = jnp.exp(sink - m)
    inv = 1.0 / (jnp.sum(e, axis=-1, keepdims=True) + es)
    return e * inv, es * inv


def _mixer_fwd(z, v_gain, w_spatial, b_spatial_t, sinks, rel_table, bucket, exchange=None):
    def body(z_ref, kvp_ref, gain_ref, ws_ref, bt_ref, sink_ref, table_ref, bucket_ref, out_ref, probs_ref, probs_t_ref,
             share_ref, guv_ref, dgelu_ref, bias_ref, wt_ref):
        n = pl.program_id(0)

        @pl.when(n == 0)
        def _():
            _fill_bias(bucket_ref, table_ref, bias_ref)
            _fill_tril(ws_ref, wt_ref)

        zuv = z_ref[:, :1024]
        cdf, t = _gelu_parts(zuv)
        guv = zuv * cdf
        guv_ref[...] = guv
        dgelu_ref[...] = cdf + zuv * (0.5 * (1.0 - t * t)) * (GELU_C * (1.0 + 3.0 * 0.044715 * (zuv * zuv)))
        for g in range(N_GROUP):
            vg = guv[:, 512 + 128 * g:512 + 128 * (g + 1)]
            vn = vg * _rms_scale(vg) * gain_ref[:, 128 * g:128 * (g + 1)]
            sv = _dot(wt_ref[g], vn.astype(BF16)) + bt_ref[:, g:g + 1]
            out_ref[:, 128 * g:128 * (g + 1)] = (guv[:, 128 * g:128 * (g + 1)] * sv).astype(BF16)

        k_same, k_swap, v_same, v_swap = _kv_layouts(kvp_ref[...], z_ref[:, 1536:1792])
        valid = _band_mask(n)
        lane = lax.broadcasted_iota(jnp.int32, (1, 128), 1)
        lane_half = lane // 64
        shares = jnp.zeros((CHUNK, 128), F32)
        for pair in range(N_HEAD // 2):
            qq = z_ref[:, 1024 + 128 * pair:1024 + 128 * (pair + 1)]
            acc = jnp.zeros((CHUNK, 128), F32)
            for pos in range(2):
                h = 2 * pair + pos
                _, _, same = _head_place(h)
                qm = jnp.where(lane_half == pos, qq, 0.0).astype(BF16)
                p, p_sink = _softmax_sink(qm, k_same if same else k_swap, bias_ref[h], sink_ref[h], valid)
                pb = p.astype(BF16)
                probs_ref[0, h] = pb
                probs_t_ref[0, h] = p.T.astype(BF16)
                shares = jnp.where(lane == h, p_sink, shares)
                vm = jnp.where(lane_half == pos, v_same if same else v_swap, jnp.zeros((), BF16))
                acc = acc + _dot(pb, vm)
            out_ref[:, 512 + 128 * pair:512 + 128 * (pair + 1)] = acc.astype(BF16)
        share_ref[...] = shares

    return _call(
        body, (z, z, v_gain, w_spatial, b_spatial_t, sinks, rel_table, bucket), grid=(N_BLOCK,), name="mixer_fwd",
        in_specs=[pl.BlockSpec((CHUNK, D_IN), lambda n: (n, 0)),
                  pl.BlockSpec((CHUNK, 256), lambda n: (jnp.maximum(n - 1, 0), 6)),
                  pl.BlockSpec((1, 512), lambda n: (0, 0)),
                  pl.BlockSpec((N_GROUP, CHUNK, CHUNK), lambda n: (0, 0, 0)),
                  pl.BlockSpec((CHUNK, N_GROUP), lambda n: (0, 0)),
                  pl.BlockSpec(memory_space=pltpu.SMEM),
                  pl.BlockSpec(memory_space=pltpu.SMEM),
                  pl.BlockSpec((CHUNK, 2 * CHUNK), lambda n: (0, 0))],
        out_specs=[pl.BlockSpec((CHUNK, D), lambda n: (n, 0)),
                   pl.BlockSpec((1, N_HEAD, CHUNK, 2 * CHUNK), lambda n: (n, 0, 0, 0)),
                   pl.BlockSpec((1, N_HEAD, 2 * CHUNK, CHUNK), lambda n: (n, 0, 0, 0)),
                   pl.BlockSpec((CHUNK, 128), lambda n: (n, 0)),
                   pl.BlockSpec((CHUNK, 1024), lambda n: (n, 0)), pl.BlockSpec((CHUNK, 1024), lambda n: (n, 0))],
        out_shape=[jax.ShapeDtypeStruct((S, D), BF16), jax.ShapeDtypeStruct((N_BLOCK, N_HEAD, CHUNK, 2 * CHUNK), BF16),
                   jax.ShapeDtypeStruct((N_BLOCK, N_HEAD, 2 * CHUNK, CHUNK), BF16), jax.ShapeDtypeStruct((S, 128), F32),
                   jax.ShapeDtypeStruct((S, 1024), F32), jax.ShapeDtypeStruct((S, 1024), F32)],
        scratch_shapes=[pltpu.VMEM((N_HEAD, CHUNK, 2 * CHUNK), F32), pltpu.VMEM((N_GROUP, CHUNK, CHUNK), BF16)],
        compiler_params=_params(32), exchange=exchange)


def _out_proj(x, mix, w_out, gain2, exchange=None):
    tm = 256

    def body(x_ref, mix_ref, w_ref, g_ref, h1_ref, hn_ref, hnt_ref):
        h1 = x_ref[...] + _dot(mix_ref[...], w_ref[...])
        h1_ref[...] = h1
        hn = h1 * _rms_scale(h1) * g_ref[...]
        hn_ref[...] = hn.astype(BF16)
        hnt_ref[...] = hn.T.astype(BF16)

    return _call(
        body, (x, mix, w_out, gain2), grid=(S // tm,), name="out_proj",
        in_specs=[pl.BlockSpec((tm, D), lambda i: (i, 0)), pl.BlockSpec((tm, D), lambda i: (i, 0)),
                  pl.BlockSpec((D, D), lambda i: (0, 0)), pl.BlockSpec((1, D), lambda i: (0, 0))],
        out_specs=[pl.BlockSpec((tm, D), lambda i: (i, 0)), pl.BlockSpec((tm, D), lambda i: (i, 0)),
                   pl.BlockSpec((D, tm), lambda i: (0, i))],
        out_shape=[jax.ShapeDtypeStruct((S, D), F32), jax.ShapeDtypeStruct((S, D), BF16),
                   jax.ShapeDtypeStruct((D, S), BF16)],
        compiler_params=_params(32), exchange=exchange)


def _ffn_up(hn2, w_ff1, exchange=None):
    tm = 512
    nj = D_FF // 1024

    def body(hn_ref, w1_ref, r_ref, a_ref, at_ref):
        r = jnp.maximum(_dot(hn_ref[...], w1_ref[0]), 0.0)
        r_ref[...] = r.astype(BF16)
        a = r * r
        a_ref[...] = a.astype(BF16)
        at_ref[...] = a.T.astype(BF16)

    return _call(
        body, (hn2, w_ff1), grid=(nj, S // tm), name="ffn_up",
        in_specs=[pl.BlockSpec((tm, D), lambda j, i: (i, 0)), pl.BlockSpec((1, D, 1024), lambda j, i: (j, 0, 0))],
        out_specs=[pl.BlockSpec((tm, 1024), lambda j, i: (i, j)), pl.BlockSpec((tm, 1024), lambda j, i: (i, j)),
                   pl.BlockSpec((1024, tm), lambda j, i: (j, i))],
        out_shape=[jax.ShapeDtypeStruct((S, D_FF), BF16), jax.ShapeDtypeStruct((S, D_FF), BF16),
                   jax.ShapeDtypeStruct((D_FF, S), BF16)],
        compiler_params=_params(40, 2), exchange=exchange)


def _ffn_down(h1, a, w_ff2, exchange=None):
    tm = 512
    nj = D_FF // 1024

    def body(h1_ref, a_ref, w2_ref, h2_ref):
        h2_ref[...] = h1_ref[...] + _dot(a_ref[...], w2_ref[...].reshape(D_FF, D))

    return _call(
        body, (h1, a, w_ff2), grid=(S // tm,), name="ffn_down",
        in_specs=[pl.BlockSpec((tm, D), lambda i: (i, 0)), pl.BlockSpec((tm, D_FF), lambda i: (i, 0)),
                  pl.BlockSpec((nj, 1024, D), lambda i: (0, 0, 0), pipeline_mode=pl.Buffered(1))],
        out_specs=[pl.BlockSpec((tm, D), lambda i: (i, 0))],
        out_shape=[jax.ShapeDtypeStruct((S, D), F32)],
        compiler_params=_params(40), exchange=exchange)


def _tail(h2, p, target, w_gate, w_proj, final_gain):
    tm = 256
    steps = S // tm

    def body(h2_ref, p_ref, t_ref, wg_ref, wp_ref, gf_ref, dh2_ref, dwg_ref, dwp_ref, dgf_ref, loss_ref, dh2b_ref,
             dwp_acc):
        i = pl.program_id(0)
        h2 = h2_ref[...]
        h2b = h2.astype(BF16)
        pb = p_ref[...].astype(BF16)
        gate = jax.nn.sigmoid(_dot(h2b, wg_ref[...]))
        pp = jnp.concatenate([_dot(pb, wp_ref[j]) for j in range(N_CHIP)], axis=1)
        h3 = h2 + gate * pp
        r3 = _rms_scale(h3)
        xhat = h3 * r3
        gf = gf_ref[...]
        err = xhat * gf - t_ref[...]
        dy = err * (1.0 / D)
        dh3 = _rms_bwd(dy * gf, xhat, r3)
        dgp = (dh3 * pp * gate * (1.0 - gate)).astype(BF16)
        dpp = (dh3 * gate).astype(BF16)
        dh2 = dh3 + _dot_nt(dgp, wg_ref[...])
        dh2_ref[...] = dh2
        dh2b_ref[...] = dh2.astype(BF16)
        dwg = _dot_tn(h2b, dgp)
        dwp = _dot_tn(pb, dpp)
        dgf = jnp.sum(dy * xhat, axis=0, keepdims=True)
        sq = jnp.sum(jnp.sum(err * err, axis=1, keepdims=True), axis=0, keepdims=True)

        @pl.when(i == 0)
        def _():
            dwg_ref[...] = dwg
            dwp_acc[...] = dwp
            dgf_ref[...] = dgf
            loss_ref[...] = jnp.broadcast_to(sq, (8, 128))

        @pl.when(i > 0)
        def _():
            dwg_ref[...] += dwg
            dwp_acc[...] += dwp
            dgf_ref[...] += dgf
            loss_ref[...] += jnp.broadcast_to(sq, (8, 128))

        @pl.when(i == steps - 1)
        def _():
            for j in range(N_CHIP):
                dwp_ref[j] = dwp_acc[:, 256 * j:256 * (j + 1)]

    return _call(
        body, (h2, p, target, w_gate, w_proj, final_gain), grid=(steps,), name="tail",
        in_specs=[pl.BlockSpec((tm, D), lambda i: (i, 0)), pl.BlockSpec((tm, PLE), lambda i: (i, 0)),
                  pl.BlockSpec((tm, D), lambda i: (i, 0)), pl.BlockSpec((D, D), lambda i: (0, 0)),
                  pl.BlockSpec((N_CHIP, PLE, 256), lambda i: (0, 0, 0)), pl.BlockSpec((1, D), lambda i: (0, 0))],
        out_specs=[pl.BlockSpec((tm, D), lambda i: (i, 0)), pl.BlockSpec((D, D), lambda i: (0, 0)),
                   pl.BlockSpec((N_CHIP, PLE, 256), lambda i: (0, 0, 0)), pl.BlockSpec((1, D), lambda i: (0, 0)),
                   pl.BlockSpec((8, 128), lambda i: (0, 0)), pl.BlockSpec((tm, D), lambda i: (i, 0))],
        out_shape=[jax.ShapeDtypeStruct((S, D), F32), jax.ShapeDtypeStruct((D, D), F32),
                   jax.ShapeDtypeStruct((N_CHIP, PLE, 256), F32), jax.ShapeDtypeStruct((1, D), F32),
                   jax.ShapeDtypeStruct((8, 128), F32), jax.ShapeDtypeStruct((S, D), BF16)],
        scratch_shapes=[pltpu.VMEM((PLE, D), F32)],
        compiler_params=_params(48))[0]


def _ffn_bwd_down(dh2b, r, a_t, w_ff2, exchange=None):
    tm = 1024
    nj = D_FF // 1024

    def body(dh2_ref, r_ref, at_ref, w2_ref, df_ref, dw2_ref):
        i = pl.program_id(1)
        dh2b = dh2_ref[...]
        da = _dot_nt(dh2b, w2_ref[0])
        df_ref[...] = (da * (2.0 * r_ref[...].astype(F32))).astype(BF16)
        dw2 = _dot(at_ref[...], dh2b)

        @pl.when(i == 0)
        def _():
            dw2_ref[0] = dw2

        @pl.when(i > 0)
        def _():
            dw2_ref[0] += dw2

    return _call(
        body, (dh2b, r, a_t, w_ff2), grid=(nj, S // tm), name="ffn_bwd_down",
        in_specs=[pl.BlockSpec((tm, D), lambda j, i: (i, 0)), pl.BlockSpec((tm, 1024), lambda j, i: (i, j)),
                  pl.BlockSpec((1024, tm), lambda j, i: (j, i)), pl.BlockSpec((1, 1024, D), lambda j, i: (j, 0, 0))],
        out_specs=[pl.BlockSpec((tm, 1024), lambda j, i: (i, j)), pl.BlockSpec((1, 1024, D), lambda j, i: (j, 0, 0))],
        out_shape=[jax.ShapeDtypeStruct((S, D_FF), BF16), jax.ShapeDtypeStruct((nj, 1024, D), F32)],
        compiler_params=_params(48, 2), exchange=exchange)


def _ffn_bwd_up(df, hn2_t, exchange=None):
    tm = 2048
    nj = D_FF // 1024

    def body(df_ref, hnt_ref, dw1_ref):
        i = pl.program_id(1)
        dw1 = _dot(hnt_ref[...], df_ref[...])

        @pl.when(i == 0)
        def _():
            dw1_ref[0] = dw1

        @pl.when(i > 0)
        def _():
            dw1_ref[0] += dw1

    return _call(
        body, (df, hn2_t), grid=(nj, S // tm), name="ffn_bwd_up",
        in_specs=[pl.BlockSpec((tm, 1024), lambda j, i: (i, j)), pl.BlockSpec((D, tm), lambda j, i: (0, i))],
        out_specs=[pl.BlockSpec((1, D, 1024), lambda j, i: (j, 0, 0))],
        out_shape=[jax.ShapeDtypeStruct((nj, D, 1024), F32)],
        compiler_params=_params(40, 2), exchange=exchange)


def _ffn_bwd_input(df, w_ff1, dh2, h1, gain2, mix, w_out, exchange=None):
    tm = 512
    nj = D_FF // 1024
    steps = S // tm

    def body(df_ref, w1_ref, dh2_ref, h1_ref, g_ref, mix_ref, wo_ref, dh1_ref, dmix_ref, dwo_ref, dg_ref):
        i = pl.program_id(0)
        dhn = _dot_nt(df_ref[:, :1024], w1_ref[0])
        for j in range(1, nj):
            dhn = dhn + _dot_nt(df_ref[:, 1024 * j:1024 * (j + 1)], w1_ref[j])
        h1 = h1_ref[...]
        r2 = _rms_scale(h1)
        xhat = h1 * r2
        dh1 = dh2_ref[...] + _rms_bwd(dhn * g_ref[...], xhat, r2)
        dh1_ref[...] = dh1
        dh1b = dh1.astype(BF16)
        dmix_ref[...] = _dot_nt(dh1b, wo_ref[...])
        dwo = _dot_tn(mix_ref[...], dh1b)
        dg = jnp.sum(dhn * xhat, axis=0, keepdims=True)

        @pl.when(i == 0)
        def _():
            dwo_ref[...] = dwo
            dg_ref[...] = dg

        @pl.when(i > 0)
        def _():
            dwo_ref[...] += dwo
            dg_ref[...] += dg

    return _call(
        body, (df, w_ff1, dh2, h1, gain2, mix, w_out), grid=(steps,), name="ffn_bwd_input",
        in_specs=[pl.BlockSpec((tm, D_FF), lambda i: (i, 0)),
                  pl.BlockSpec((nj, D, 1024), lambda i: (0, 0, 0), pipeline_mode=pl.Buffered(1)),
                  pl.BlockSpec((tm, D), lambda i: (i, 0)), pl.BlockSpec((tm, D), lambda i: (i, 0)),
                  pl.BlockSpec((1, D), lambda i: (0, 0)), pl.BlockSpec((tm, D), lambda i: (i, 0)),
                  pl.BlockSpec((D, D), lambda i: (0, 0), pipeline_mode=pl.Buffered(1))],
        out_specs=[pl.BlockSpec((tm, D), lambda i: (i, 0)), pl.BlockSpec((tm, D), lambda i: (i, 0)),
                   pl.BlockSpec((D, D), lambda i: (0, 0)), pl.BlockSpec((1, D), lambda i: (0, 0))],
        out_shape=[jax.ShapeDtypeStruct((S, D), F32), jax.ShapeDtypeStruct((S, D), F32),
                   jax.ShapeDtypeStruct((D, D), F32), jax.ShapeDtypeStruct((1, D), F32)],
        compiler_params=_params(56), exchange=exchange)


IN_GROUP = 8


def _mixer_bwd(z, dmix, v_gain, w_spatial, b_spatial_t, saved, bucket, hn1, exchange=None):
    def body(z_ref, kvp_ref, dm_ref, gain_ref, ws_ref, bt_ref, probs_ref, probs_t_ref, share_ref, guv_ref, dgelu_ref,
             bucket_ref, hn_ref,
             dz_ref, dws_ref, db_ref, dgain_ref, dsink_ref, drel_ref, dwin_ref,
             wt_ref, wtt_ref, dbias_ref, dsv_ref, carry_ref):
        n = pl.program_id(0)

        @pl.when(n == 0)
        def _():
            _fill_tril(ws_ref, wt_ref, wtt_ref)
            dwin_ref[...] = jnp.zeros_like(dwin_ref)
            dbias_ref[...] = jnp.zeros_like(dbias_ref)
            dsv_ref[...] = jnp.zeros_like(dsv_ref)
            dws_ref[...] = jnp.zeros_like(dws_ref)
            dgain_ref[...] = jnp.zeros_like(dgain_ref)
            dsink_ref[...] = jnp.zeros_like(dsink_ref)

        rows = pl.ds(pl.multiple_of(n * CHUNK, CHUNK), CHUNK)

        guv = guv_ref[...]
        dgelu = dgelu_ref[...]
        for g in range(N_GROUP):
            lo, hi = 128 * g, 128 * (g + 1)
            u = guv[:, lo:hi]
            vg = guv[:, 512 + lo:512 + hi]
            rr = _rms_scale(vg)
            vhat = vg * rr
            gain = gain_ref[:, lo:hi]
            vnb = (vhat * gain).astype(BF16)
            sv = _dot(wt_ref[g], vnb) + bt_ref[:, g:g + 1]
            da = dm_ref[:, lo:hi]
            dsv = da * u
            dsvb = dsv.astype(BF16)
            dsv_ref[g] += dsv
            dws_ref[g] += _dot_nt(dsvb, vnb)
            dvn = _dot(wtt_ref[g], dsvb)
            dgain_ref[:, lo:hi] += jnp.sum(dvn * vhat, axis=0, keepdims=True)
            dvg = _rms_bwd(dvn * gain, vhat, rr)
            dz_ref[rows, lo:hi] = (da * sv * dgelu[:, lo:hi]).astype(BF16)
            dz_ref[rows, 512 + lo:512 + hi] = (dvg * dgelu[:, 512 + lo:512 + hi]).astype(BF16)

        k_same, k_swap, v_same, v_swap = _kv_layouts(kvp_ref[...], z_ref[:, 1536:1792])
        lane_half = lax.broadcasted_iota(jnp.int32, (1, 128), 1) // 64
        zero = jnp.zeros((2 * CHUNK, 128), F32)
        dk_same, dk_swap, dv_same, dv_swap = zero, zero, zero, zero
        for pair in range(N_HEAD // 2):
            cols = slice(1024 + 128 * pair, 1024 + 128 * (pair + 1))
            qq = z_ref[:, cols]
            do_pair = dm_ref[:, 512 + 128 * pair:512 + 128 * (pair + 1)]
            dq = jnp.zeros((CHUNK, 128), F32)
            for pos in range(2):
                h = 2 * pair + pos
                _, _, same = _head_place(h)
                on_half = lane_half == pos
                qm = jnp.where(on_half, qq, 0.0).astype(BF16)
                k_use = k_same if same else k_swap
                v_use = v_same if same else v_swap
                pb = probs_ref[0, h]
                p = pb.astype(F32)
                p_sink = share_ref[:, h:h + 1]
                dom = jnp.where(on_half, do_pair, 0.0).astype(BF16)
                dp = _dot_nt(dom, v_use)
                dsum = jnp.sum(p * dp, axis=-1, keepdims=True)
                ds = p * (dp - dsum)
                dbias_ref[h] += ds
                dsink_ref[h:h + 1, :] += jnp.broadcast_to(jnp.sum(-p_sink * dsum, axis=0, keepdims=True), (1, 128))
                dsb = ds.astype(BF16)
                dq = dq + jnp.where(on_half, _dot(dsb, k_use), 0.0)
                dk_h = _dot_tn(dsb, qm)
                dv_h = _dot(probs_t_ref[0, h], dom)
                if same:
                    dk_same, dv_same = dk_same + dk_h, dv_same + dv_h
                else:
                    dk_swap, dv_swap = dk_swap + dk_h, dv_swap + dv_h
            dz_ref[rows, cols] = (dq * QK_SCALE).astype(BF16)
        dk = (dk_same + pltpu.roll(dk_swap, 64, axis=1)) * QK_SCALE
        dv = dv_same + pltpu.roll(dv_swap, 64, axis=1)
        dkv = jnp.concatenate([dk, dv], axis=1)

        @pl.when(n > 0)
        def _():
            prev_rows = pl.ds(pl.multiple_of((n - 1) * CHUNK, CHUNK), CHUNK)
            dz_ref[prev_rows, 1536:1792] = (carry_ref[...] + dkv[:CHUNK]).astype(BF16)

        carry_ref[...] = dkv[CHUNK:]

        @pl.when((n > 0) & (n % IN_GROUP == 0))
        def _():
            done = pl.ds(pl.multiple_of((n - IN_GROUP) * CHUNK, IN_GROUP * CHUNK), IN_GROUP * CHUNK)
            dwin_ref[...] += _dot_tn(dz_ref[done, :], hn_ref[...])

        @pl.when(n == N_BLOCK - 1)
        def _():
            dz_ref[rows, 1536:1792] = dkv[CHUNK:].astype(BF16)
            last = pl.ds((N_BLOCK - IN_GROUP) * CHUNK, IN_GROUP * CHUNK)
            dwin_ref[...] += _dot_tn(dz_ref[last, :], hn_ref[...])
            r = lax.broadcasted_iota(jnp.int32, (CHUNK, CHUNK), 0)
            c = lax.broadcasted_iota(jnp.int32, (CHUNK, CHUNK), 1)
            for g in range(N_GROUP):
                dws_ref[g] = jnp.where(c <= r, dws_ref[g], 0.0)
                db_ref[g] = jnp.sum(dsv_ref[g], axis=1, keepdims=True)
            bucket = bucket_ref[...]
            for h in range(N_HEAD):
                dbh = dbias_ref[h]
                per_bucket = [jnp.sum(jnp.where(bucket == b, dbh, 0.0), axis=0, keepdims=True) for b in range(N_BUCKET)]
                drel_ref[h] = jnp.sum(jnp.concatenate(per_bucket, axis=0), axis=1, keepdims=True)

    def hn_group(n):
        return jnp.where(n == N_BLOCK - 1, N_BLOCK // IN_GROUP - 1, jnp.maximum(n // IN_GROUP - 1, 0))

    return _call(
        body, (z, z, dmix, v_gain, w_spatial, b_spatial_t, *saved, bucket, hn1), grid=(N_BLOCK,),
        name="mixer_bwd",
        in_specs=[pl.BlockSpec((CHUNK, D_IN), lambda n: (n, 0)),
                  pl.BlockSpec((CHUNK, 256), lambda n: (jnp.maximum(n - 1, 0), 6)),
                  pl.BlockSpec((CHUNK, D), lambda n: (n, 0)),
                  pl.BlockSpec((1, 512), lambda n: (0, 0)),
                  pl.BlockSpec((N_GROUP, CHUNK, CHUNK), lambda n: (0, 0, 0)),
                  pl.BlockSpec((CHUNK, N_GROUP), lambda n: (0, 0)),
                  pl.BlockSpec((1, N_HEAD, CHUNK, 2 * CHUNK), lambda n: (n, 0, 0, 0)),
                  pl.BlockSpec((1, N_HEAD, 2 * CHUNK, CHUNK), lambda n: (n, 0, 0, 0)),
                  pl.BlockSpec((CHUNK, 128), lambda n: (n, 0)),
                  pl.BlockSpec((CHUNK, 1024), lambda n: (n, 0)), pl.BlockSpec((CHUNK, 1024), lambda n: (n, 0)),
                  pl.BlockSpec((CHUNK, 2 * CHUNK), lambda n: (0, 0)),
                  pl.BlockSpec((IN_GROUP * CHUNK, D), lambda n: (hn_group(n), 0))],
        out_specs=[pl.BlockSpec((S, D_IN), lambda n: (0, 0)),
                   pl.BlockSpec((N_GROUP, CHUNK, CHUNK), lambda n: (0, 0, 0)),
                   pl.BlockSpec((N_GROUP, CHUNK, 1), lambda n: (0, 0, 0)),
                   pl.BlockSpec((1, 512), lambda n: (0, 0)),
                   pl.BlockSpec((N_HEAD, 128), lambda n: (0, 0)),
                   pl.BlockSpec((N_HEAD, N_BUCKET, 1), lambda n: (0, 0, 0)),
                   pl.BlockSpec((D_IN, D), lambda n: (0, 0))],
        out_shape=[jax.ShapeDtypeStruct((S, D_IN), BF16), jax.ShapeDtypeStruct((N_GROUP, CHUNK, CHUNK), F32),
                   jax.ShapeDtypeStruct((N_GROUP, CHUNK, 1), F32), jax.ShapeDtypeStruct((1, 512), F32),
                   jax.ShapeDtypeStruct((N_HEAD, 128), F32), jax.ShapeDtypeStruct((N_HEAD, N_BUCKET, 1), F32),
                   jax.ShapeDtypeStruct((D_IN, D), F32)],
        scratch_shapes=[pltpu.VMEM((N_GROUP, CHUNK, CHUNK), BF16),
                        pltpu.VMEM((N_GROUP, CHUNK, CHUNK), BF16), pltpu.VMEM((N_HEAD, CHUNK, 2 * CHUNK), F32),
                        pltpu.VMEM((N_GROUP, CHUNK, CHUNK), F32), pltpu.VMEM((CHUNK, 256), F32)],
        compiler_params=_params(56), exchange=exchange)


def _in_bwd_input(dz, w_in_t, x, dh1, gain1, exchange=None):
    tm = 512

    def body(dz_ref, w_ref, x_ref, dh1_ref, g_ref, dx_ref, dg_ref):
        i = pl.program_id(0)
        dhn = _dot(dz_ref[...], w_ref[...])
        xv = x_ref[...]
        r1 = _rms_scale(xv)
        xhat = xv * r1
        dx_ref[...] = dh1_ref[...] + _rms_bwd(dhn * g_ref[...], xhat, r1)
        dg = jnp.sum(dhn * xhat, axis=0, keepdims=True)

        @pl.when(i == 0)
        def _():
            dg_ref[...] = dg

        @pl.when(i > 0)
        def _():
            dg_ref[...] += dg

    return _call(
        body, (dz, w_in_t, x, dh1, gain1), grid=(S // tm,), name="in_bwd_input",
        in_specs=[pl.BlockSpec((tm, D_IN), lambda i: (i, 0)), pl.BlockSpec((D_IN, D), lambda i: (0, 0)),
                  pl.BlockSpec((tm, D), lambda i: (i, 0)), pl.BlockSpec((tm, D), lambda i: (i, 0)),
                  pl.BlockSpec((1, D), lambda i: (0, 0))],
        out_specs=[pl.BlockSpec((tm, D), lambda i: (i, 0)), pl.BlockSpec((1, D), lambda i: (0, 0))],
        out_shape=[jax.ShapeDtypeStruct((S, D), F32), jax.ShapeDtypeStruct((1, D), F32)],
        compiler_params=_params(48), exchange=exchange)


def _rel_bucket():
    a = jnp.arange(CHUNK)[:, None]
    j = jnp.arange(2 * CHUNK)[None, :]
    n = jnp.maximum(CHUNK + a - j, 0)
    max_exact = N_BUCKET // 2
    nf = jnp.maximum(n, 1).astype(jnp.float32)
    large = max_exact + (jnp.log(nf / max_exact) / math.log(CHUNK / max_exact) * (N_BUCKET - max_exact)).astype(jnp.int32)
    large = jnp.minimum(large, N_BUCKET - 1)
    return jnp.where(n < max_exact, n, large).astype(jnp.int32)


def _step(x, p, target, small, bufs, place):
    bucket = _rel_bucket()
    sinks = small["attn_sinks"].reshape(N_HEAD)
    b_t = jnp.transpose(small["b_spatial"].reshape(N_GROUP, CHUNK))
    ws = small["w_spatial"].reshape(N_GROUP, CHUNK, CHUNK)
    gain1, gain2 = small["norm1_gain"], small["norm2_gain"]
    v_gain = small["gmlp_v_gain"]
    final_gain = small["final_gain"].reshape(1, D)
    table = small["rel_bias_table"]
    bufs = dict(bufs)

    def gather(*names):
        return _RelayGather([bufs[n] for n in names])

    def took(names, got):
        bufs.update(zip(names, got))

    w_in_t = _whole(bufs["w_in"]).reshape(D_IN, D)
    (z, hn1), got = _in_proj(x, gain1, w_in_t, gather("w_out"))
    took(["w_out"], got)
    (mix, *saved), got = _mixer_fwd(z, v_gain, ws, b_t, sinks, table, bucket, gather("w_ff1"))
    took(["w_ff1"], got)
    w_out = _whole(bufs["w_out"]).reshape(D, D)
    (h1, hn2, hn2_t), _ = _out_proj(x, mix, w_out, gain2)
    w_ff1 = _whole(bufs["w_ff1"])
    (r, a, a_t), got = _ffn_up(hn2, w_ff1, gather("w_ff2"))
    took(["w_ff2"], got)
    w_ff2 = _whole(bufs["w_ff2"])
    (h2,), got = _ffn_down(h1, a, w_ff2, gather("w_ple_gate", "w_ple_proj"))
    took(["w_ple_gate", "w_ple_proj"], got)
    dh2, d_gate, d_proj, d_final, sq, dh2b = _tail(h2, p, target, _whole(bufs["w_ple_gate"]).reshape(D, D),
                                                   _whole(bufs["w_ple_proj"]), final_gain)

    def pair_sums(halves, from_sibling):
        sums, landing = zip(*[_pair_sum(g, o, place) for g, o in zip(halves, from_sibling)])
        return list(sums), list(landing)

    landed = {}
    halves = [_halves(d_gate.reshape(N_CHIP, 256, D)), _halves(d_proj)]
    (df, d_ff2), got = _ffn_bwd_down(dh2b, r, a_t, w_ff2, _SiblingExchange(halves))
    ex, halves = _ChipExchange(*pair_sums(halves, got)), [_halves(d_ff2)]
    (d_ff1,), got = _ffn_bwd_up(df, hn2_t, _Both(ex, _SiblingExchange(halves)))
    landed.update(zip(["w_ple_gate", "w_ple_proj"], got[:2]))
    ex, halves = _ChipExchange(*pair_sums(halves, got[2:])), [_halves(d_ff1)]
    (dh1, dmix, d_out, d_gain2), got = _ffn_bwd_input(df, w_ff1, dh2, h1, gain2, mix, w_out,
                                                      _Both(ex, _SiblingExchange(halves)))
    landed["w_ff2"] = got[0]
    ex, halves = _ChipExchange(*pair_sums(halves, got[1:])), [_halves(d_out.reshape(N_CHIP, 256, D))]
    (dz, d_ws, d_b, d_vgain, d_sink, d_rel, d_in_t), got = _mixer_bwd(z, dmix, v_gain, ws, b_t, saved, bucket, hn1,
                                                                     _Both(ex, _SiblingExchange(halves)))
    landed["w_ff1"] = got[0]
    small_grads = {
        "gmlp_v_gain": d_vgain, "w_spatial": d_ws.reshape(1, N_GROUP, CHUNK, CHUNK),
        "b_spatial": d_b.reshape(1, N_GROUP, CHUNK), "attn_sinks": d_sink[:, 0].reshape(1, N_HEAD),
        "rel_bias_table": jnp.transpose(d_rel.reshape(N_HEAD, N_BUCKET)), "norm2_gain": d_gain2,
        "final_gain": d_final.reshape(D),
    }
    ex, halves = _ChipExchange(*pair_sums(halves, got[1:])), [_halves(d_in_t.reshape(N_CHIP, 448, D))]
    (dx, small_grads["norm1_gain"]), got = _in_bwd_input(dz, w_in_t, x, dh1, gain1, _Both(ex, _SiblingExchange(halves)))
    landed["w_out"] = got[0]
    return dx, landed, _ChipExchange(*pair_sums(halves, got[1:])), small_grads, sq


HBM_SPEC = pl.BlockSpec(memory_space=pltpu.HBM)
VMEM_SPEC = pl.BlockSpec(memory_space=pltpu.VMEM)


def _mesh_place():
    x, y, c = lax.axis_index("x"), lax.axis_index("y"), lax.axis_index("c")
    others = [(1 - x, y), (x, 1 - y), (1 - x, 1 - y)]
    return x, y, c, others


def _remote(src, dst, send_sem, recv_sem, device):
    return pltpu.make_async_remote_copy(src_ref=src, dst_ref=dst, send_sem=send_sem, recv_sem=recv_sem,
                                        device_id=device, device_id_type=MESH)


def _hbm_like(a, shape=None, dtype=None):
    return pltpu.HBM(a.shape if shape is None else shape, a.dtype if dtype is None else dtype)


def _gather_start(bufs, send_sems, recv_sems):
    x, y, c, others = _mesh_place()
    me = 2 * x + y
    for w, buf in enumerate(bufs):
        for k in range(3):
            mine = buf.at[me, c]
            _remote(mine, mine, send_sems.at[w, k], recv_sems.at[w, k], (*others[k], c)).start()


def _gather_finish(bufs, send_sems, recv_sems):
    x, y, c, others = _mesh_place()
    me = 2 * x + y
    sibling = (x, y, 1 - c)
    idx = [2 * ox + oy for ox, oy in others]
    chips = range(3)
    for w, buf in enumerate(bufs):
        for k in chips:
            landed = buf.at[idx[k], c]
            _remote(landed, landed, send_sems.at[w, k], recv_sems.at[w, k], sibling).wait_recv()
            _remote(landed, landed, send_sems.at[w, 3 + k], recv_sems.at[w, 3 + k], sibling).start()
    for w, buf in enumerate(bufs):
        for k in chips:
            landed = buf.at[idx[k], 1 - c]
            _remote(landed, landed, send_sems.at[w, 3 + k], recv_sems.at[w, 3 + k], sibling).wait_recv()
    for w, buf in enumerate(bufs):
        for k in chips:
            mine, passed = buf.at[me, c], buf.at[idx[k], c]
            _remote(mine, mine, send_sems.at[w, k], recv_sems.at[w, k], sibling).wait_send()
            _remote(passed, passed, send_sems.at[w, 3 + k], recv_sems.at[w, 3 + k], sibling).wait_send()


def _gather_sems(n):
    return [pltpu.SemaphoreType.DMA((n, 6)), pltpu.SemaphoreType.DMA((n, 6))]


def _sibling_copies(grads, landing, send_sems, recv_sems):
    x, y, c, _ = _mesh_place()
    return [_remote(grads[w].at[j, 1 - c], landing[w].at[j], send_sems.at[w, j], recv_sems.at[w, j], (x, y, 1 - c))
            for w in range(len(grads)) for j in range(N_CHIP)]


def _sibling_exchange_start(grads, landing, send_sems, recv_sems):
    for cp in _sibling_copies(grads, landing, send_sems, recv_sems):
        cp.start()


def _sibling_exchange_finish(grads, landing, send_sems, recv_sems):
    copies = _sibling_copies(grads, landing, send_sems, recv_sems)
    for cp in copies:
        cp.wait_recv()
    for cp in copies:
        cp.wait_send()


def _sibling_exchange_sems(n):
    return [pltpu.SemaphoreType.DMA((n, N_CHIP)), pltpu.SemaphoreType.DMA((n, N_CHIP))]


def _sibling_exchange(grads):
    n = len(grads)

    def body(*refs):
        ins, outs = refs[:n], refs[n:2 * n]
        _sibling_exchange_start(ins, outs, *refs[2 * n:])
        _sibling_exchange_finish(ins, outs, *refs[2 * n:])

    return pl.pallas_call(
        body, name="sibling_exchange",
        in_specs=[HBM_SPEC] * n, out_specs=[HBM_SPEC] * n,
        out_shape=[_hbm_like(g, (N_CHIP,) + g.shape[2:]) for g in grads],
        scratch_shapes=_sibling_exchange_sems(n),
    )(*[_in_hbm(g) for g in grads])


def _chip_exchange_start(sums, landing, send_sems, recv_sems):
    x, y, c, others = _mesh_place()
    me = 2 * x + y
    for w in range(len(sums)):
        for k, (ox, oy) in enumerate(others):
            _remote(sums[w].at[2 * ox + oy], landing[w].at[me], send_sems.at[w, k], recv_sems.at[w, k],
                    (ox, oy, c)).start()


def _chip_exchange_finish(sums, landing, send_sems, recv_sems):
    x, y, c, others = _mesh_place()
    for w in range(len(sums)):
        for k, (ox, oy) in enumerate(others):
            piece = landing[w].at[2 * ox + oy]
            _remote(piece, piece, send_sems.at[w, k], recv_sems.at[w, k], (x, y, c)).wait_recv()
    for w in range(len(sums)):
        for k, (ox, oy) in enumerate(others):
            piece = sums[w].at[2 * ox + oy]
            _remote(piece, piece, send_sems.at[w, k], recv_sems.at[w, k], (x, y, c)).wait_send()


def _chip_exchange_sems(n):
    return [pltpu.SemaphoreType.DMA((n, 3)), pltpu.SemaphoreType.DMA((n, 3))]


def _sibling_allgather(bufs, also):
    n = len(bufs)
    k_in, k_out = len(also.operands), also.n_out

    def body(*refs):
        ex_ins, refs = refs[n:n + k_in], refs[n + k_in:]
        outs, refs = refs[:n], refs[n:]
        ex_outs, refs = refs[:k_out], refs[k_out:]
        send_sems, recv_sems, ex_sems = refs[0], refs[1], refs[2:]
        x, y, c, _ = _mesh_place()
        sibling = (x, y, 1 - c)
        also.start(ex_ins, ex_outs, ex_sems)
        sends = [_remote(outs[w].at[c], outs[w].at[c], send_sems.at[w], recv_sems.at[w], sibling) for w in range(n)]
        for cp in sends:
            cp.start()
        for w in range(n):
            landed = outs[w].at[1 - c]
            _remote(landed, landed, send_sems.at[w], recv_sems.at[w], sibling).wait_recv()
        for cp in sends:
            cp.wait_send()
        also.finish(ex_ins, ex_outs, ex_sems)

    res = pl.pallas_call(
        body, name="sibling_allgather",
        in_specs=[HBM_SPEC] * (n + k_in), out_specs=[HBM_SPEC] * (n + k_out),
        out_shape=[_hbm_like(b) for b in bufs] + also.out_shape,
        input_output_aliases={**{w: w for w in range(n)}, **{n + i: n + o for i, o in also.aliases.items()}},
        scratch_shapes=[pltpu.SemaphoreType.DMA((n,)), pltpu.SemaphoreType.DMA((n,))] + also.sems,
    )(*bufs, *[_in_hbm(o) for o in also.operands])
    return list(res[:n]), list(res[n:])


def _pair_sum(grad, other, place):
    _, _, h, cols = grad.shape
    tr = _row_tile(h)

    def body(place_ref, g_ref, o_ref, sums_ref, own_ref):
        s = (g_ref[0, 0] + o_ref[0]).astype(BF16)
        sums_ref[0] = s

        @pl.when(pl.program_id(1) == place_ref[0])
        def _():
            own_ref[0] = s

    return pl.pallas_call(
        body, name="pair_sum",
        grid_spec=pltpu.PrefetchScalarGridSpec(
            num_scalar_prefetch=1, grid=(h // tr, N_CHIP),
            in_specs=[pl.BlockSpec((1, 1, tr, cols), lambda r, j, place_ref: (j, place_ref[1], r, 0)),
                      pl.BlockSpec((1, tr, cols), lambda r, j, place_ref: (j, r, 0))],
            out_specs=[pl.BlockSpec((1, tr, cols), lambda r, j, place_ref: (j, r, 0)),
                       pl.BlockSpec((1, tr, cols), lambda r, j, place_ref: (place_ref[0], r, 0))]),
        out_shape=[pltpu.HBM((N_CHIP, h, cols), BF16)] * 2,
        compiler_params=_params(32, 2),
    )(place, _in_hbm(grad), _in_hbm(other))


def _chip_sum(parts, place):
    _, h, cols = parts.shape
    tr = _row_tile(h)

    def body(place_ref, p_ref, out_ref):
        out_ref[0] = ((p_ref[0].astype(F32) + p_ref[1].astype(F32)) + p_ref[2].astype(F32)) + p_ref[3].astype(F32)

    return pl.pallas_call(
        body, name="chip_sum",
        grid_spec=pltpu.PrefetchScalarGridSpec(
            num_scalar_prefetch=1, grid=(h // tr,),
            in_specs=[pl.BlockSpec((N_CHIP, tr, cols), lambda r, place_ref: (0, r, 0))],
            out_specs=pl.BlockSpec((1, tr, cols), lambda r, place_ref: (place_ref[1], r, 0))),
        out_shape=pltpu.HBM((2, h, cols), F32),
        compiler_params=_params(32),
    )(place, _in_hbm(parts))


def _adamw_math(w, g, m, v):
    m = ADAM_B1 * m + (1.0 - ADAM_B1) * g
    v = ADAM_B2 * v + (1.0 - ADAM_B2) * (g * g)
    m_hat = m / (1.0 - ADAM_B1 ** ADAM_STEP)
    v_hat = v / (1.0 - ADAM_B2 ** ADAM_STEP)
    delta = -ADAM_LR * (m_hat / (jnp.sqrt(v_hat) + ADAM_EPS) + ADAM_WD * w)
    return delta, m, v


def _adamw(w, g, m, v, exchange=None):
    rows, cols = w.shape
    tr = _row_tile(rows)

    def body(w_ref, g_ref, m_ref, v_ref, d_ref, nm_ref, nv_ref, g_out_ref):
        g = g_ref[...]
        d_ref[...], nm_ref[...], nv_ref[...] = _adamw_math(w_ref[...], g, m_ref[...], v_ref[...])
        g_out_ref[...] = g

    spec = pl.BlockSpec((tr, cols), lambda r: (r, 0))
    return _call(
        body, (w, g, m, v), grid=(rows // tr,), name="adamw",
        in_specs=[spec] * 4, out_specs=[spec] * 4,
        out_shape=[jax.ShapeDtypeStruct((rows, cols), F32)] * 4,
        compiler_params=_params(48), exchange=exchange)


SMALL_NAMES = ("norm1_gain", "gmlp_v_gain", "w_spatial", "b_spatial", "attn_sinks", "rel_bias_table", "norm2_gain",
               "final_gain")
PACK_TILE = 8 * 128


def _pack_small(arrays):
    parts = []
    for a in arrays:
        flat = a.reshape(-1)
        rows = -(-flat.shape[0] // PACK_TILE) * 8
        parts.append(jnp.pad(flat, (0, rows * 128 - flat.shape[0])).reshape(rows, 128))
    return jnp.concatenate(parts, axis=0)


def _unpack_small(packed, like):
    out, row = [], 0
    for a in like:
        size = math.prod(a.shape)
        rows = -(-size // PACK_TILE) * 8
        out.append(packed[row:row + rows].reshape(-1)[:size].reshape(a.shape))
        row += rows
    return out


def _small_update(gathered, w, m, v):
    rows = gathered.shape[1]

    def body(g_ref, w_ref, m_ref, v_ref, tot_ref, d_ref, nm_ref, nv_ref):
        total = g_ref[0].astype(F32)
        for dev in range(1, 8):
            total = total + g_ref[dev].astype(F32)
        tot_ref[...] = total
        d_ref[...], nm_ref[...], nv_ref[...] = _adamw_math(w_ref[...], total, m_ref[...], v_ref[...])

    return pl.pallas_call(
        body, name="small_update",
        in_specs=[VMEM_SPEC] * 4, out_specs=[VMEM_SPEC] * 4,
        out_shape=[jax.ShapeDtypeStruct((rows, 128), F32)] * 4,
        compiler_params=pltpu.CompilerParams(vmem_limit_bytes=24 * MIB),
    )(gathered, w, m, v)


def _halves(a):
    return a.reshape(a.shape[:-2] + (2, a.shape[-2] // 2, a.shape[-1]))


def _whole(a):
    return a.reshape(a.shape[:-3] + (2 * a.shape[-2], a.shape[-1]))


def kernel(x, p, norm1_gain, w_in, gmlp_v_gain, w_spatial, b_spatial, attn_sinks, rel_bias_table, w_out, norm2_gain, w_ff1, w_ff2, w_ple_proj, w_ple_gate, final_gain, loss_target, m_norm1_gain, m_w_in, m_gmlp_v_gain, m_w_spatial, m_b_spatial, m_attn_sinks, m_rel_bias_table, m_w_out, m_norm2_gain, m_w_ff1, m_w_ff2, m_w_ple_proj, m_w_ple_gate, m_final_gain, v_norm1_gain, v_w_in, v_gmlp_v_gain, v_w_spatial, v_b_spatial, v_attn_sinks, v_rel_bias_table, v_w_out, v_norm2_gain, v_w_ff1, v_w_ff2, v_w_ple_proj, v_w_ple_gate, v_final_gain):
    given = dict(locals())
    small = {n: given[n] for n in SMALL_NAMES}
    chip = 2 * lax.axis_index("x") + lax.axis_index("y")
    place = jnp.stack([chip, lax.axis_index("c")]).astype(jnp.int32)

    big_names = ("w_in", "w_out", "w_ff1", "w_ff2", "w_ple_proj", "w_ple_gate")
    shards = {n: given[n][0] for n in big_names}
    travel = dict(shards, w_in=jnp.transpose(shards["w_in"]))
    rest = [n for n in big_names if n != "w_in"]
    cast, gathered = _cast_shards_beside_gather([travel[n] for n in rest], place[:1],
                                                [_cast_shard(travel["w_in"], place[:1])])
    bufs = dict(zip(rest + ["w_in"], cast + gathered))
    dx, landed, exchange_in, small_grads, sq = _step(x[0], p[0, 0], loss_target[0], small, bufs, place)

    out_grad, out_delta, out_m, out_v = {}, {}, {}, {}

    def update(n, g, exchange=None):
        to = jnp.transpose if n == "w_in" else (lambda a: a)
        (delta, new_m, new_v, g_out), got = _adamw(to(shards[n]), g, to(given["m_" + n][0]), to(given["v_" + n][0]),
                                                   exchange)
        out_grad[n], out_delta[n], out_m[n], out_v[n] = [to(a)[None] for a in (g_out, delta, new_m, new_v)]
        return got

    spare = jnp.zeros((8, 128), F32)
    small_packed = _pack_small([small_grads[n] for n in SMALL_NAMES] + [spare]).astype(BF16)
    early = [n for n in big_names if n != "w_in"]
    reduced, (small_gathered, sq_gathered, landed_in) = _sibling_allgather(
        [_chip_sum(landed[n], place) for n in early], _Both(_Both(_GatherAll(small_packed), _GatherAll(sq)), exchange_in))
    for n, r in zip(early, reduced):
        update(n, _whole(r))
    (reduced_in,), _ = _sibling_allgather([_chip_sum(landed_in, place)], _Nothing())
    update("w_in", _whole(reduced_in))

    like = [given[n] for n in SMALL_NAMES] + [spare]
    packed = _small_update(small_gathered, *[_pack_small([given[pre + n] for n in SMALL_NAMES] + [spare])
                                             for pre in ("", "m_", "v_")])
    for res, out in zip(packed, (out_grad, out_delta, out_m, out_v)):
        out.update(zip(SMALL_NAMES, _unpack_small(res, like)))
    loss = 0.5 * jnp.sum(sq_gathered[:, 0, 0]) / D

    order = ("norm1_gain", "w_in", "gmlp_v_gain", "w_spatial", "b_spatial", "attn_sinks", "rel_bias_table", "w_out",
             "norm2_gain", "w_ff1", "w_ff2", "w_ple_proj", "w_ple_gate", "final_gain")
    return (loss, dx[None], *[out_grad[n] for n in order], *[out_delta[n] for n in order],
            *[out_m[n] for n in order], *[out_v[n] for n in order])
```

```python
import functools
import math

import jax
import jax.numpy as jnp
from jax import lax
from jax.experimental import pallas as pl
from jax.experimental.pallas import tpu as pltpu

S = 2048
D = 1024
D_IN = 1792
D_FF = 4096
PLE = 256
N_CHIP = 4
N_GROUP = 4
CHUNK = 128
N_HEAD = 8
N_BLOCK = S // CHUNK
N_BUCKET = 32
EPS = 1e-6
NEG_INF = -1e30
QK_SCALE = 0.125
GELU_C = math.sqrt(2.0 / math.pi)

ADAM_LR = 0.001
ADAM_B1 = 0.9
ADAM_B2 = 0.999
ADAM_EPS = 1e-08
ADAM_WD = 0.01
ADAM_STEP = 10

F32 = jnp.float32
BF16 = jnp.bfloat16
MIB = 1024 * 1024
MESH = pl.DeviceIdType.MESH

NT = (((1,), (1,)), ((), ()))
TN = (((0,), (0,)), ((), ()))


def _dot(a, b):
    return jnp.dot(a, b, preferred_element_type=F32)


def _dot_nt(a, b):
    return lax.dot_general(a, b, NT, preferred_element_type=F32)


def _dot_tn(a, b):
    return lax.dot_general(a, b, TN, preferred_element_type=F32)


def _params(vmem_mib, n_axes=1):
    return pltpu.CompilerParams(dimension_semantics=("arbitrary",) * n_axes, vmem_limit_bytes=vmem_mib * MIB)


def _rms_scale(v):
    return lax.rsqrt(jnp.mean(v * v, axis=-1, keepdims=True) + EPS)


def _rms_bwd(dy_gain, xhat, r):
    return r * (dy_gain - xhat * jnp.mean(dy_gain * xhat, axis=-1, keepdims=True))


class _Gather:
    def __init__(self, bufs):
        self.operands = list(bufs)
        self.n_out = len(self.operands)
        self.out_shape = [_hbm_like(b) for b in bufs]
        self.aliases = {w: w for w in range(self.n_out)}
        self.sems = _gather_sems(self.n_out)

    def start(self, ins, outs, sems):
        _gather_start(outs, *sems)

    def finish(self, ins, outs, sems):
        _gather_finish(outs, *sems)


class _RelayGather(_Gather):
    TOP, BOTTOM = 6, 7
    DIAGONAL_PASSED = 5
    MIDDLE_AT, LATE_AT = (5, 8), (7, 8)

    def __init__(self, bufs):
        super().__init__(bufs)
        self.sems = [pltpu.SemaphoreType.DMA((self.n_out, 8)), pltpu.SemaphoreType.DMA((self.n_out, 8))]

    def _copies(self, bufs, send_sems, recv_sems):
        x, y, c, others = _mesh_place()
        me = 2 * x + y
        idx = [2 * ox + oy for ox, oy in others]
        sibling = (x, y, 1 - c)
        direct, passed, relayed = [], [], []
        for w, buf in enumerate(bufs):
            rows = buf.shape[2] // 2
            upper, lower = pl.ds(0, rows), pl.ds(rows, rows)
            for k in (0, 1):
                mine = buf.at[me, c]
                direct.append((_remote(mine, mine, send_sems.at[w, k], recv_sems.at[w, k], (*others[k], c)),
                               buf.at[idx[k], c], w, k))
            for k in (0, 1, 2):
                here = buf.at[idx[k], c]
                passed.append((_remote(here, here, send_sems.at[w, 3 + k], recv_sems.at[w, 3 + k], sibling),
                               buf.at[idx[k], 1 - c], w, 3 + k))
            from_x, from_y = buf.at[idx[0], c, upper], buf.at[idx[1], c, lower]
            relayed.append((_remote(from_x, from_x, send_sems.at[w, self.TOP], recv_sems.at[w, self.TOP],
                                    (*others[1], c)), buf.at[idx[2], c, upper], w, self.TOP))
            relayed.append((_remote(from_y, from_y, send_sems.at[w, self.BOTTOM], recv_sems.at[w, self.BOTTOM],
                                    (*others[0], c)), buf.at[idx[2], c, lower], w, self.BOTTOM))
        return direct, passed, relayed

    @staticmethod
    def _landed(piece, send_sems, recv_sems, w, col):
        x, y, c, _ = _mesh_place()
        _remote(piece, piece, send_sems.at[w, col], recv_sems.at[w, col], (x, y, c)).wait_recv()

    def start(self, ins, outs, sems):
        for cp, _, _, _ in self._copies(outs, *sems)[0]:
            cp.start()

    def middle(self, ins, outs, sems):
        direct, passed, relayed = self._copies(outs, *sems)
        for _, piece, w, col in direct:
            self._landed(piece, *sems, w, col)
        for cp, _, _, col in passed:
            if col != self.DIAGONAL_PASSED:
                cp.start()
        for cp, _, _, _ in relayed:
            cp.start()

    def late(self, ins, outs, sems):
        direct, passed, relayed = self._copies(outs, *sems)
        for _, piece, w, col in relayed:
            self._landed(piece, *sems, w, col)
        for cp, _, _, col in passed:
            if col == self.DIAGONAL_PASSED:
                cp.start()

    def finish(self, ins, outs, sems):
        direct, passed, relayed = self._copies(outs, *sems)
        for _, piece, w, col in passed:
            self._landed(piece, *sems, w, col)
        for cp, _, _, _ in direct + passed + relayed:
            cp.wait_send()


class _ChipExchange:
    def __init__(self, sums, landing):
        self.n_out = len(landing)
        self.operands = list(sums) + list(landing)
        self.out_shape = [_hbm_like(b) for b in landing]
        self.aliases = {self.n_out + w: w for w in range(self.n_out)}
        self.sems = _chip_exchange_sems(self.n_out)

    def start(self, ins, outs, sems):
        _chip_exchange_start(ins[:self.n_out], outs, *sems)

    def finish(self, ins, outs, sems):
        _chip_exchange_finish(ins[:self.n_out], outs, *sems)


class _GatherAll:
    def __init__(self, packed):
        self.operands = [packed]
        self.n_out = 1
        self.out_shape = [_hbm_like(packed, (8,) + packed.shape)]
        self.aliases = {}
        self.sems = [pltpu.SemaphoreType.DMA((8,)), pltpu.SemaphoreType.DMA((8,))]

    def _copies(self, ins, outs, sems):
        x, y, c, _ = _mesh_place()
        me = 4 * x + 2 * y + c
        send_sems, recv_sems = sems
        copies = []
        for k in range(1, 8):
            peer = (1 - x if k // 4 else x, 1 - y if (k // 2) % 2 else y, 1 - c if k % 2 else c)
            src = 4 * peer[0] + 2 * peer[1] + peer[2]
            copies.append((_remote(ins[0], outs[0].at[me], send_sems.at[k], recv_sems.at[k], peer), outs[0].at[src]))
        own = pltpu.make_async_copy(ins[0], outs[0].at[me], send_sems.at[0])
        return own, copies

    def start(self, ins, outs, sems):
        own, copies = self._copies(ins, outs, sems)
        own.start()
        for cp, _ in copies:
            cp.start()

    def finish(self, ins, outs, sems):
        own, copies = self._copies(ins, outs, sems)
        x, y, c, _ = _mesh_place()
        for k, (cp, landed) in enumerate(copies):
            _remote(landed, landed, sems[0].at[k + 1], sems[1].at[k + 1], (x, y, c)).wait_recv()
        for cp, _ in copies:
            cp.wait_send()
        own.wait()


class _Nothing:
    operands, n_out, out_shape, aliases, sems = [], 0, [], {}, []

    def start(self, ins, outs, sems):
        pass

    def finish(self, ins, outs, sems):
        pass


class _Both:
    def __init__(self, a, b):
        self.a, self.b = a, b
        self.operands = a.operands + b.operands
        self.n_out = a.n_out + b.n_out
        self.out_shape = a.out_shape + b.out_shape
        self.aliases = dict(a.aliases)
        self.aliases.update({len(a.operands) + i: a.n_out + o for i, o in b.aliases.items()})
        self.sems = a.sems + b.sems

    def _split(self, ins, outs, sems):
        ka, na, sa = len(self.a.operands), self.a.n_out, len(self.a.sems)
        return (ins[:ka], outs[:na], sems[:sa]), (ins[ka:], outs[na:], sems[sa:])

    def start(self, ins, outs, sems):
        for ex, args in zip((self.a, self.b), self._split(ins, outs, sems)):
            ex.start(*args)

    def finish(self, ins, outs, sems):
        for ex, args in zip((self.a, self.b), self._split(ins, outs, sems)):
            ex.finish(*args)


class _SiblingExchange:
    def __init__(self, grads):
        self.operands = list(grads)
        self.n_out = len(self.operands)
        self.out_shape = [_hbm_like(g, (N_CHIP,) + g.shape[2:]) for g in grads]
        self.aliases = {}
        self.sems = _sibling_exchange_sems(self.n_out)

    def start(self, ins, outs, sems):
        _sibling_exchange_start(ins, outs, *sems)

    def finish(self, ins, outs, sems):
        _sibling_exchange_finish(ins, outs, *sems)


def _call(body, operands, *, grid, in_specs, out_specs, out_shape, name, compiler_params, scratch_shapes=(),
          exchange=None):
    operands = [o if getattr(spec, "memory_space", None) == pltpu.SMEM else _in_hbm(o)
                for o, spec in zip(operands, in_specs)]
    out_shape = [pltpu.HBM(s.shape, s.dtype) for s in out_shape]
    if exchange is None:
        res = pl.pallas_call(body, grid=grid, in_specs=in_specs, out_specs=out_specs, out_shape=out_shape, name=name,
                             scratch_shapes=list(scratch_shapes), compiler_params=compiler_params)(*operands)
        return list(res), []
    n_in, n_out, n_scr = len(in_specs), len(out_specs), len(scratch_shapes)
    k_in, k_out = len(exchange.operands), exchange.n_out

    def fused(*refs):
        ins, refs = refs[:n_in], refs[n_in:]
        ex_ins, refs = refs[:k_in], refs[k_in:]
        outs, refs = refs[:n_out], refs[n_out:]
        ex_outs, refs = refs[:k_out], refs[k_out:]
        scratch, sems = refs[:n_scr], refs[n_scr:]
        ids = [pl.program_id(a) for a in range(len(grid))]
        first = functools.reduce(jnp.logical_and, [i == 0 for i in ids])
        last = functools.reduce(jnp.logical_and, [i == g - 1 for i, g in zip(ids, grid)])

        @pl.when(first)
        def _():
            exchange.start(ex_ins, ex_outs, sems)

        def at_step(numerator, denominator):
            at = (numerator * math.prod(grid)) // denominator
            place = [(at // math.prod(grid[a + 1:])) % grid[a] for a in range(len(grid))]
            return functools.reduce(jnp.logical_and, [i == p for i, p in zip(ids, place)])

        if hasattr(exchange, "middle"):
            @pl.when(at_step(*exchange.MIDDLE_AT))
            def _():
                exchange.middle(ex_ins, ex_outs, sems)

            @pl.when(at_step(*exchange.LATE_AT))
            def _():
                exchange.late(ex_ins, ex_outs, sems)

        body(*ins, *outs, *scratch)

        @pl.when(last)
        def _():
            exchange.finish(ex_ins, ex_outs, sems)

    res = pl.pallas_call(
        fused, grid=grid, name=name,
        in_specs=list(in_specs) + [HBM_SPEC] * k_in, out_specs=list(out_specs) + [HBM_SPEC] * k_out,
        out_shape=list(out_shape) + exchange.out_shape,
        input_output_aliases={n_in + i: n_out + o for i, o in exchange.aliases.items()},
        scratch_shapes=list(scratch_shapes) + exchange.sems, compiler_params=compiler_params,
    )(*operands, *[_in_hbm(o) for o in exchange.operands])
    return list(res[:n_out]), list(res[n_out:])


def _in_hbm(a):
    return pltpu.with_memory_space_constraint(a, pltpu.HBM)


def _row_tile(h):
    return max(t for t in range(16, 513, 16) if h % t == 0)


def _cast_shard(a, chip):
    rows, cols = a.shape
    h = rows // 2
    tr = _row_tile(h)

    def body(chip_ref, a_ref, o_ref):
        o_ref[0, 0] = a_ref[0].astype(BF16)

    return pl.pallas_call(
        body, name="cast_shard",
        grid_spec=pltpu.PrefetchScalarGridSpec(
            num_scalar_prefetch=1, grid=(2, h // tr),
            in_specs=[pl.BlockSpec((1, tr, cols), lambda s, r, chip_ref: (s, r, 0))],
            out_specs=pl.BlockSpec((1, 1, tr, cols), lambda s, r, chip_ref: (chip_ref[0], s, r, 0))),
        out_shape=pltpu.HBM((N_CHIP, 2, h, cols), BF16),
        compiler_params=_params(16, 2),
    )(chip, _in_hbm(a.reshape(2, h, cols)))


def _cast_shards_beside_gather(arrays, chip, gathered):
    n, k = len(arrays), len(gathered)
    shapes = [(a.shape[0] // 2, a.shape[1]) for a in arrays]

    def body(chip_ref, *refs):
        ins, refs = refs[:n], refs[n + k:]
        outs, refs = refs[:n], refs[n:]
        bufs, sems = refs[:k], refs[k:]
        half = pl.program_id(0)

        @pl.when(half == 0)
        def _():
            _gather_start(bufs, *sems)

        for a_ref, o_ref in zip(ins, outs):
            o_ref[0, 0] = a_ref[0].astype(BF16)

        @pl.when(half == 1)
        def _():
            _gather_finish(bufs, *sems)

    res = pl.pallas_call(
        body, name="cast_shards",
        grid_spec=pltpu.PrefetchScalarGridSpec(
            num_scalar_prefetch=1, grid=(2,),
            in_specs=[pl.BlockSpec((1, h, c), lambda s, chip_ref: (s, 0, 0)) for h, c in shapes] + [HBM_SPEC] * k,
            out_specs=[pl.BlockSpec((1, 1, h, c), lambda s, chip_ref: (chip_ref[0], s, 0, 0)) for h, c in shapes]
            + [HBM_SPEC] * k,
            scratch_shapes=_gather_sems(k)),
        out_shape=[pltpu.HBM((N_CHIP, 2, h, c), BF16) for h, c in shapes] + [_hbm_like(b) for b in gathered],
        input_output_aliases={1 + n + i: n + i for i in range(k)},
        compiler_params=_params(32),
    )(chip, *[_in_hbm(a.reshape(2, h, c)) for a, (h, c) in zip(arrays, shapes)], *gathered)
    return list(res[:n]), list(res[n:])


def _in_proj(x, gain1, w_in_t, exchange=None):
    tm = 256

    def body(x_ref, g_ref, w_ref, z_ref, hn_ref):
        xv = x_ref[...]
        hn = (xv * _rms_scale(xv) * g_ref[...]).astype(BF16)
        hn_ref[...] = hn
        z_ref[...] = _dot_nt(hn, w_ref[...])

    return _call(
        body, (x, gain1, w_in_t), grid=(S // tm,), name="in_proj",
        in_specs=[pl.BlockSpec((tm, D), lambda i: (i, 0)), pl.BlockSpec((1, D), lambda i: (0, 0)),
                  pl.BlockSpec((D_IN, D), lambda i: (0, 0))],
        out_specs=[pl.BlockSpec((tm, D_IN), lambda i: (i, 0)), pl.BlockSpec((tm, D), lambda i: (i, 0))],
        out_shape=[jax.ShapeDtypeStruct((S, D_IN), F32), jax.ShapeDtypeStruct((S, D), BF16)],
        compiler_params=_params(40), exchange=exchange)


def _gelu_parts(v):
    t = jnp.tanh(GELU_C * (v + 0.044715 * (v * v * v)))
    cdf = 0.5 * (1.0 + t)
    return cdf, t


def _band_mask(n):
    a = lax.broadcasted_iota(jnp.int32, (CHUNK, 2 * CHUNK), 0)
    j = lax.broadcasted_iota(jnp.int32, (CHUNK, 2 * CHUNK), 1)
    dist = CHUNK + a - j
    valid = (dist >= 0) & (dist < CHUNK)
    return valid & ((n > 0) | (j >= CHUNK))


def _fill_bias(bucket_ref, table_ref, bias_ref):
    bucket = bucket_ref[...]
    for h in range(N_HEAD):
        acc = jnp.zeros((CHUNK, 2 * CHUNK), F32)
        for b in range(N_BUCKET):
            acc = jnp.where(bucket == b, table_ref[b, h], acc)
        bias_ref[h] = acc


def _fill_tril(ws_ref, wt_ref, wtt_ref=None):
    r = lax.broadcasted_iota(jnp.int32, (CHUNK, CHUNK), 0)
    c = lax.broadcasted_iota(jnp.int32, (CHUNK, CHUNK), 1)
    for g in range(N_GROUP):
        w = jnp.where(c <= r, ws_ref[g], 0.0)
        wt_ref[g] = w.astype(BF16)
        if wtt_ref is not None:
            wtt_ref[g] = w.T.astype(BF16)


def _kv_layouts(kv_prev, kv_cur):
    both = jnp.concatenate([kv_prev, kv_cur], axis=0)
    k = both[:, :128]
    v = both[:, 128:]
    return (k.astype(BF16), pltpu.roll(k, 64, axis=1).astype(BF16),
            v.astype(BF16), pltpu.roll(v, 64, axis=1).astype(BF16))


def _head_place(h):
    pair, pos, kvh = h // 2, h % 2, h // 4
    return pair, pos, kvh == pos


def _softmax_sink(qm, k_use, bias_h, sink, valid):
    s = _dot_nt(qm, k_use) * QK_SCALE + bias_h
    s = jnp.where(valid, s, NEG_INF)
    m = jnp.maximum(jnp.max(s, axis=-1, keepdims=True), sink)
    e = jnp.exp(s - m)
    es = jnp.exp(sink - m)
    inv = 1.0 / (jnp.sum(e, axis=-1, keepdims=True) + es)
    return e * inv, es * inv


def _mixer_fwd(z, v_gain, w_spatial, b_spatial_t, sinks, rel_table, bucket, exchange=None):
    def body(z_ref, kvp_ref, gain_ref, ws_ref, bt_ref, sink_ref, table_ref, bucket_ref, out_ref, probs_ref, probs_t_ref,
             share_ref, guv_ref, dgelu_ref, bias_ref, wt_ref):
        n = pl.program_id(0)

        @pl.when(n == 0)
        def _():
            _fill_bias(bucket_ref, table_ref, bias_ref)
            _fill_tril(ws_ref, wt_ref)

        zuv = z_ref[:, :1024]
        cdf, t = _gelu_parts(zuv)
        guv = zuv * cdf
        guv_ref[...] = guv
        dgelu_ref[...] = cdf + zuv * (0.5 * (1.0 - t * t)) * (GELU_C * (1.0 + 3.0 * 0.044715 * (zuv * zuv)))
        for g in range(N_GROUP):
            vg = guv[:, 512 + 128 * g:512 + 128 * (g + 1)]
            vn = vg * _rms_scale(vg) * gain_ref[:, 128 * g:128 * (g + 1)]
            sv = _dot(wt_ref[g], vn.astype(BF16)) + bt_ref[:, g:g + 1]
            out_ref[:, 128 * g:128 * (g + 1)] = (guv[:, 128 * g:128 * (g + 1)] * sv).astype(BF16)

        k_same, k_swap, v_same, v_swap = _kv_layouts(kvp_ref[...], z_ref[:, 1536:1792])
        valid = _band_mask(n)
        lane = lax.broadcasted_iota(jnp.int32, (1, 128), 1)
        lane_half = lane // 64
        shares = jnp.zeros((CHUNK, 128), F32)
        for pair in range(N_HEAD // 2):
            qq = z_ref[:, 1024 + 128 * pair:1024 + 128 * (pair + 1)]
            acc = jnp.zeros((CHUNK, 128), F32)
            for pos in range(2):
                h = 2 * pair + pos
                _, _, same = _head_place(h)
                qm = jnp.where(lane_half == pos, qq, 0.0).astype(BF16)
                p, p_sink = _softmax_sink(qm, k_same if same else k_swap, bias_ref[h], sink_ref[h], valid)
                pb = p.astype(BF16)
                probs_ref[0, h] = pb
                probs_t_ref[0, h] = p.T.astype(BF16)
                shares = jnp.where(lane == h, p_sink, shares)
                vm = jnp.where(lane_half == pos, v_same if same else v_swap, jnp.zeros((), BF16))
                acc = acc + _dot(pb, vm)
            out_ref[:, 512 + 128 * pair:512 + 128 * (pair + 1)] = acc.astype(BF16)
        share_ref[...] = shares

    return _call(
        body, (z, z, v_gain, w_spatial, b_spatial_t, sinks, rel_table, bucket), grid=(N_BLOCK,), name="mixer_fwd",
        in_specs=[pl.BlockSpec((CHUNK, D_IN), lambda n: (n, 0)),
                  pl.BlockSpec((CHUNK, 256), lambda n: (jnp.maximum(n - 1, 0), 6)),
                  pl.BlockSpec((1, 512), lambda n: (0, 0)),
                  pl.BlockSpec((N_GROUP, CHUNK, CHUNK), lambda n: (0, 0, 0)),
                  pl.BlockSpec((CHUNK, N_GROUP), lambda n: (0, 0)),
                  pl.BlockSpec(memory_space=pltpu.SMEM),
                  pl.BlockSpec(memory_space=pltpu.SMEM),
                  pl.BlockSpec((CHUNK, 2 * CHUNK), lambda n: (0, 0))],
        out_specs=[pl.BlockSpec((CHUNK, D), lambda n: (n, 0)),
                   pl.BlockSpec((1, N_HEAD, CHUNK, 2 * CHUNK), lambda n: (n, 0, 0, 0)),
                   pl.BlockSpec((1, N_HEAD, 2 * CHUNK, CHUNK), lambda n: (n, 0, 0, 0)),
                   pl.BlockSpec((CHUNK, 128), lambda n: (n, 0)),
                   pl.BlockSpec((CHUNK, 1024), lambda n: (n, 0)), pl.BlockSpec((CHUNK, 1024), lambda n: (n, 0))],
        out_shape=[jax.ShapeDtypeStruct((S, D), BF16), jax.ShapeDtypeStruct((N_BLOCK, N_HEAD, CHUNK, 2 * CHUNK), BF16),
                   jax.ShapeDtypeStruct((N_BLOCK, N_HEAD, 2 * CHUNK, CHUNK), BF16), jax.ShapeDtypeStruct((S, 128), F32),
                   jax.ShapeDtypeStruct((S, 1024), F32), jax.ShapeDtypeStruct((S, 1024), F32)],
        scratch_shapes=[pltpu.VMEM((N_HEAD, CHUNK, 2 * CHUNK), F32), pltpu.VMEM((N_GROUP, CHUNK, CHUNK), BF16)],
        compiler_params=_params(32), exchange=exchange)


def _out_proj(x, mix, w_out, gain2, exchange=None):
    tm = 256

    def body(x_ref, mix_ref, w_ref, g_ref, h1_ref, hn_ref, hnt_ref):
        h1 = x_ref[...] + _dot(mix_ref[...], w_ref[...])
        h1_ref[...] = h1
        hn = h1 * _rms_scale(h1) * g_ref[...]
        hn_ref[...] = hn.astype(BF16)
        hnt_ref[...] = hn.T.astype(BF16)

    return _call(
        body, (x, mix, w_out, gain2), grid=(S // tm,), name="out_proj",
        in_specs=[pl.BlockSpec((tm, D), lambda i: (i, 0)), pl.BlockSpec((tm, D), lambda i: (i, 0)),
                  pl.BlockSpec((D, D), lambda i: (0, 0)), pl.BlockSpec((1, D), lambda i: (0, 0))],
        out_specs=[pl.BlockSpec((tm, D), lambda i: (i, 0)), pl.BlockSpec((tm, D), lambda i: (i, 0)),
                   pl.BlockSpec((D, tm), lambda i: (0, i))],
        out_shape=[jax.ShapeDtypeStruct((S, D), F32), jax.ShapeDtypeStruct((S, D), BF16),
                   jax.ShapeDtypeStruct((D, S), BF16)],
        compiler_params=_params(32), exchange=exchange)


def _ffn_up(hn2, w_ff1, exchange=None):
    tm = 512
    nj = D_FF // 1024

    def body(hn_ref, w1_ref, r_ref, a_ref, at_ref):
        r = jnp.maximum(_dot(hn_ref[...], w1_ref[0]), 0.0)
        r_ref[...] = r.astype(BF16)
        a = r * r
        a_ref[...] = a.astype(BF16)
        at_ref[...] = a.T.astype(BF16)

    return _call(
        body, (hn2, w_ff1), grid=(nj, S // tm), name="ffn_up",
        in_specs=[pl.BlockSpec((tm, D), lambda j, i: (i, 0)), pl.BlockSpec((1, D, 1024), lambda j, i: (j, 0, 0))],
        out_specs=[pl.BlockSpec((tm, 1024), lambda j, i: (i, j)), pl.BlockSpec((tm, 1024), lambda j, i: (i, j)),
                   pl.BlockSpec((1024, tm), lambda j, i: (j, i))],
        out_shape=[jax.ShapeDtypeStruct((S, D_FF), BF16), jax.ShapeDtypeStruct((S, D_FF), BF16),
                   jax.ShapeDtypeStruct((D_FF, S), BF16)],
        compiler_params=_params(40, 2), exchange=exchange)


def _ffn_down(h1, a, w_ff2, exchange=None):
    tm = 512
    nj = D_FF // 1024

    def body(h1_ref, a_ref, w2_ref, h2_ref):
        h2_ref[...] = h1_ref[...] + _dot(a_ref[...], w2_ref[...].reshape(D_FF, D))

    return _call(
        body, (h1, a, w_ff2), grid=(S // tm,), name="ffn_down",
        in_specs=[pl.BlockSpec((tm, D), lambda i: (i, 0)), pl.BlockSpec((tm, D_FF), lambda i: (i, 0)),
                  pl.BlockSpec((nj, 1024, D), lambda i: (0, 0, 0), pipeline_mode=pl.Buffered(1))],
        out_specs=[pl.BlockSpec((tm, D), lambda i: (i, 0))],
        out_shape=[jax.ShapeDtypeStruct((S, D), F32)],
        compiler_params=_params(40), exchange=exchange)


def _tail(h2, p, target, w_gate, w_proj, final_gain):
    tm = 256
    steps = S // tm

    def body(h2_ref, p_ref, t_ref, wg_ref, wp_ref, gf_ref, dh2_ref, dwg_ref, dwp_ref, dgf_ref, loss_ref, dh2b_ref,
             dwp_acc):
        i = pl.program_id(0)
        h2 = h2_ref[...]
        h2b = h2.astype(BF16)
        pb = p_ref[...].astype(BF16)
        gate = jax.nn.sigmoid(_dot(h2b, wg_ref[...]))
        pp = jnp.concatenate([_dot(pb, wp_ref[j]) for j in range(N_CHIP)], axis=1)
        h3 = h2 + gate * pp
        r3 = _rms_scale(h3)
        xhat = h3 * r3
        gf = gf_ref[...]
        err = xhat * gf - t_ref[...]
        dy = err * (1.0 / D)
        dh3 = _rms_bwd(dy * gf, xhat, r3)
        dgp = (dh3 * pp * gate * (1.0 - gate)).astype(BF16)
        dpp = (dh3 * gate).astype(BF16)
        dh2 = dh3 + _dot_nt(dgp, wg_ref[...])
        dh2_ref[...] = dh2
        dh2b_ref[...] = dh2.astype(BF16)
        dwg = _dot_tn(h2b, dgp)
        dwp = _dot_tn(pb, dpp)
        dgf = jnp.sum(dy * xhat, axis=0, keepdims=True)
        sq = jnp.sum(jnp.sum(err * err, axis=1, keepdims=True), axis=0, keepdims=True)

        @pl.when(i == 0)
        def _():
            dwg_ref[...] = dwg
            dwp_acc[...] = dwp
            dgf_ref[...] = dgf
            loss_ref[...] = jnp.broadcast_to(sq, (8, 128))

        @pl.when(i > 0)
        def _():
            dwg_ref[...] += dwg
            dwp_acc[...] += dwp
            dgf_ref[...] += dgf
            loss_ref[...] += jnp.broadcast_to(sq, (8, 128))

        @pl.when(i == steps - 1)
        def _():
            for j in range(N_CHIP):
                dwp_ref[j] = dwp_acc[:, 256 * j:256 * (j + 1)]

    return _call(
        body, (h2, p, target, w_gate, w_proj, final_gain), grid=(steps,), name="tail",
        in_specs=[pl.BlockSpec((tm, D), lambda i: (i, 0)), pl.BlockSpec((tm, PLE), lambda i: (i, 0)),
                  pl.BlockSpec((tm, D), lambda i: (i, 0)), pl.BlockSpec((D, D), lambda i: (0, 0)),
                  pl.BlockSpec((N_CHIP, PLE, 256), lambda i: (0, 0, 0)), pl.BlockSpec((1, D), lambda i: (0, 0))],
        out_specs=[pl.BlockSpec((tm, D), lambda i: (i, 0)), pl.BlockSpec((D, D), lambda i: (0, 0)),
                   pl.BlockSpec((N_CHIP, PLE, 256), lambda i: (0, 0, 0)), pl.BlockSpec((1, D), lambda i: (0, 0)),
                   pl.BlockSpec((8, 128), lambda i: (0, 0)), pl.BlockSpec((tm, D), lambda i: (i, 0))],
        out_shape=[jax.ShapeDtypeStruct((S, D), F32), jax.ShapeDtypeStruct((D, D), F32),
                   jax.ShapeDtypeStruct((N_CHIP, PLE, 256), F32), jax.ShapeDtypeStruct((1, D), F32),
                   jax.ShapeDtypeStruct((8, 128), F32), jax.ShapeDtypeStruct((S, D), BF16)],
        scratch_shapes=[pltpu.VMEM((PLE, D), F32)],
        compiler_params=_params(48))[0]


def _ffn_bwd_down(dh2b, r, a_t, w_ff2, exchange=None):
    tm = 1024
    nj = D_FF // 1024

    def body(dh2_ref, r_ref, at_ref, w2_ref, df_ref, dw2_ref):
        i = pl.program_id(1)
        dh2b = dh2_ref[...]
        da = _dot_nt(dh2b, w2_ref[0])
        df_ref[...] = (da * (2.0 * r_ref[...].astype(F32))).astype(BF16)
        dw2 = _dot(at_ref[...], dh2b)

        @pl.when(i == 0)
        def _():
            dw2_ref[0] = dw2

        @pl.when(i > 0)
        def _():
            dw2_ref[0] += dw2

    return _call(
        body, (dh2b, r, a_t, w_ff2), grid=(nj, S // tm), name="ffn_bwd_down",
        in_specs=[pl.BlockSpec((tm, D), lambda j, i: (i, 0)), pl.BlockSpec((tm, 1024), lambda j, i: (i, j)),
                  pl.BlockSpec((1024, tm), lambda j, i: (j, i)), pl.BlockSpec((1, 1024, D), lambda j, i: (j, 0, 0))],
        out_specs=[pl.BlockSpec((tm, 1024), lambda j, i: (i, j)), pl.BlockSpec((1, 1024, D), lambda j, i: (j, 0, 0))],
        out_shape=[jax.ShapeDtypeStruct((S, D_FF), BF16), jax.ShapeDtypeStruct((nj, 1024, D), F32)],
        compiler_params=_params(48, 2), exchange=exchange)


def _ffn_bwd_up(df, hn2_t, exchange=None):
    tm = 2048
    nj = D_FF // 1024

    def body(df_ref, hnt_ref, dw1_ref):
        i = pl.program_id(1)
        dw1 = _dot(hnt_ref[...], df_ref[...])

        @pl.when(i == 0)
        def _():
            dw1_ref[0] = dw1

        @pl.when(i > 0)
        def _():
            dw1_ref[0] += dw1

    return _call(
        body, (df, hn2_t), grid=(nj, S // tm), name="ffn_bwd_up",
        in_specs=[pl.BlockSpec((tm, 1024), lambda j, i: (i, j)), pl.BlockSpec((D, tm), lambda j, i: (0, i))],
        out_specs=[pl.BlockSpec((1, D, 1024), lambda j, i: (j, 0, 0))],
        out_shape=[jax.ShapeDtypeStruct((nj, D, 1024), F32)],
        compiler_params=_params(40, 2), exchange=exchange)


def _ffn_bwd_input(df, w_ff1, dh2, h1, gain2, mix, w_out, exchange=None):
    tm = 512
    nj = D_FF // 1024
    steps = S // tm

    def body(df_ref, w1_ref, dh2_ref, h1_ref, g_ref, mix_ref, wo_ref, dh1_ref, dmix_ref, dwo_ref, dg_ref, acc_ref):
        i = pl.program_id(0)
        j = pl.program_id(1)
        part = _dot_nt(df_ref[...], w1_ref[j])

        @pl.when(j == 0)
        def _():
            acc_ref[...] = part

        @pl.when(j > 0)
        def _():
            acc_ref[...] += part

        @pl.when(j == nj - 1)
        def _():
            dhn = acc_ref[...]
            h1 = h1_ref[...]
            r2 = _rms_scale(h1)
            xhat = h1 * r2
            dh1 = dh2_ref[...] + _rms_bwd(dhn * g_ref[...], xhat, r2)
            dh1_ref[...] = dh1
            dh1b = dh1.astype(BF16)
            dmix_ref[...] = _dot_nt(dh1b, wo_ref[...])
            dwo = _dot_tn(mix_ref[...], dh1b)
            dg = jnp.sum(dhn * xhat, axis=0, keepdims=True)

            @pl.when(i == 0)
            def _():
                dwo_ref[...] = dwo
                dg_ref[...] = dg

            @pl.when(i > 0)
            def _():
                dwo_ref[...] += dwo
                dg_ref[...] += dg

    return _call(
        body, (df, w_ff1, dh2, h1, gain2, mix, w_out), grid=(steps, nj), name="ffn_bwd_input",
        in_specs=[pl.BlockSpec((tm, 1024), lambda i, j: (i, j)),
                  pl.BlockSpec((nj, D, 1024), lambda i, j: (0, 0, 0), pipeline_mode=pl.Buffered(1)),
                  pl.BlockSpec((tm, D), lambda i, j: (i, 0)), pl.BlockSpec((tm, D), lambda i, j: (i, 0)),
                  pl.BlockSpec((1, D), lambda i, j: (0, 0)), pl.BlockSpec((tm, D), lambda i, j: (i, 0)),
                  pl.BlockSpec((D, D), lambda i, j: (0, 0), pipeline_mode=pl.Buffered(1))],
        out_specs=[pl.BlockSpec((tm, D), lambda i, j: (i, 0)), pl.BlockSpec((tm, D), lambda i, j: (i, 0)),
                   pl.BlockSpec((D, D), lambda i, j: (0, 0)), pl.BlockSpec((1, D), lambda i, j: (0, 0))],
        out_shape=[jax.ShapeDtypeStruct((S, D), F32), jax.ShapeDtypeStruct((S, D), F32),
                   jax.ShapeDtypeStruct((D, D), F32), jax.ShapeDtypeStruct((1, D), F32)],
        scratch_shapes=[pltpu.VMEM((tm, D), F32)],
        compiler_params=_params(56, 2), exchange=exchange)


IN_GROUP = 8


def _mixer_bwd(z, dmix, v_gain, w_spatial, b_spatial_t, saved, bucket, hn1, exchange=None):
    def body(z_ref, kvp_ref, dm_ref, gain_ref, ws_ref, bt_ref, probs_ref, probs_t_ref, share_ref, guv_ref, dgelu_ref,
             bucket_ref, hn_ref,
             dz_ref, dws_ref, db_ref, dgain_ref, dsink_ref, drel_ref, dwin_ref,
             wt_ref, wtt_ref, dbias_ref, dsv_ref, carry_ref):
        n = pl.program_id(0)

        @pl.when(n == 0)
        def _():
            _fill_tril(ws_ref, wt_ref, wtt_ref)
            dwin_ref[...] = jnp.zeros_like(dwin_ref)
            dbias_ref[...] = jnp.zeros_like(dbias_ref)
            dsv_ref[...] = jnp.zeros_like(dsv_ref)
            dws_ref[...] = jnp.zeros_like(dws_ref)
            dgain_ref[...] = jnp.zeros_like(dgain_ref)
            dsink_ref[...] = jnp.zeros_like(dsink_ref)

        rows = pl.ds(pl.multiple_of(n * CHUNK, CHUNK), CHUNK)

        guv = guv_ref[...]
        dgelu = dgelu_ref[...]
        for g in range(N_GROUP):
            lo, hi = 128 * g, 128 * (g + 1)
            u = guv[:, lo:hi]
            vg = guv[:, 512 + lo:512 + hi]
            rr = _rms_scale(vg)
            vhat = vg * rr
            gain = gain_ref[:, lo:hi]
            vnb = (vhat * gain).astype(BF16)
            sv = _dot(wt_ref[g], vnb) + bt_ref[:, g:g + 1]
            da = dm_ref[:, lo:hi]
            dsv = da * u
            dsvb = dsv.astype(BF16)
            dsv_ref[g] += dsv
            dws_ref[g] += _dot_nt(dsvb, vnb)
            dvn = _dot(wtt_ref[g], dsvb)
            dgain_ref[:, lo:hi] += jnp.sum(dvn * vhat, axis=0, keepdims=True)
            dvg = _rms_bwd(dvn * gain, vhat, rr)
            dz_ref[rows, lo:hi] = (da * sv * dgelu[:, lo:hi]).astype(BF16)
            dz_ref[rows, 512 + lo:512 + hi] = (dvg * dgelu[:, 512 + lo:512 + hi]).astype(BF16)

        k_same, k_swap, v_same, v_swap = _kv_layouts(kvp_ref[...], z_ref[:, 1536:1792])
        lane_half = lax.broadcasted_iota(jnp.int32, (1, 128), 1) // 64
        zero = jnp.zeros((2 * CHUNK, 128), F32)
        dk_same, dk_swap, dv_same, dv_swap = zero, zero, zero, zero
        for pair in range(N_HEAD // 2):
            cols = slice(1024 + 128 * pair, 1024 + 128 * (pair + 1))
            qq = z_ref[:, cols]
            do_pair = dm_ref[:, 512 + 128 * pair:512 + 128 * (pair + 1)]
            dq = jnp.zeros((CHUNK, 128), F32)
            for pos in range(2):
                h = 2 * pair + pos
                _, _, same = _head_place(h)
                on_half = lane_half == pos
                qm = jnp.where(on_half, qq, 0.0).astype(BF16)
                k_use = k_same if same else k_swap
                v_use = v_same if same else v_swap
                pb = probs_ref[0, h]
                p = pb.astype(F32)
                p_sink = share_ref[:, h:h + 1]
                dom = jnp.where(on_half, do_pair, 0.0).astype(BF16)
                dp = _dot_nt(dom, v_use)
                dsum = jnp.sum(p * dp, axis=-1, keepdims=True)
                ds = p * (dp - dsum)
                dbias_ref[h] += ds
                dsink_ref[h:h + 1, :] += jnp.broadcast_to(jnp.sum(-p_sink * dsum, axis=0, keepdims=True), (1, 128))
                dsb = ds.astype(BF16)
                dq = dq + jnp.where(on_half, _dot(dsb, k_use), 0.0)
                dk_h = _dot_tn(dsb, qm)
                dv_h = _dot(probs_t_ref[0, h], dom)
                if same:
                    dk_same, dv_same = dk_same + dk_h, dv_same + dv_h
                else:
                    dk_swap, dv_swap = dk_swap + dk_h, dv_swap + dv_h
            dz_ref[rows, cols] = (dq * QK_SCALE).astype(BF16)
        dk = (dk_same + pltpu.roll(dk_swap, 64, axis=1)) * QK_SCALE
        dv = dv_same + pltpu.roll(dv_swap, 64, axis=1)
        dkv = jnp.concatenate([dk, dv], axis=1)

        @pl.when(n > 0)
        def _():
            prev_rows = pl.ds(pl.multiple_of((n - 1) * CHUNK, CHUNK), CHUNK)
            dz_ref[prev_rows, 1536:1792] = (carry_ref[...] + dkv[:CHUNK]).astype(BF16)

        carry_ref[...] = dkv[CHUNK:]

        @pl.when((n > 0) & (n % IN_GROUP == 0))
        def _():
            done = pl.ds(pl.multiple_of((n - IN_GROUP) * CHUNK, IN_GROUP * CHUNK), IN_GROUP * CHUNK)
            dwin_ref[...] += _dot_tn(dz_ref[done, :], hn_ref[...])

        @pl.when(n == N_BLOCK - 1)
        def _():
            dz_ref[rows, 1536:1792] = dkv[CHUNK:].astype(BF16)
            last = pl.ds((N_BLOCK - IN_GROUP) * CHUNK, IN_GROUP * CHUNK)
            dwin_ref[...] += _dot_tn(dz_ref[last, :], hn_ref[...])
            r = lax.broadcasted_iota(jnp.int32, (CHUNK, CHUNK), 0)
            c = lax.broadcasted_iota(jnp.int32, (CHUNK, CHUNK), 1)
            for g in range(N_GROUP):
                dws_ref[g] = jnp.where(c <= r, dws_ref[g], 0.0)
                db_ref[g] = jnp.sum(dsv_ref[g], axis=1, keepdims=True)
            bucket = bucket_ref[...]
            for h in range(N_HEAD):
                dbh = dbias_ref[h]
                per_bucket = [jnp.sum(jnp.where(bucket == b, dbh, 0.0), axis=0, keepdims=True) for b in range(N_BUCKET)]
                drel_ref[h] = jnp.sum(jnp.concatenate(per_bucket, axis=0), axis=1, keepdims=True)

    def hn_group(n):
        return jnp.where(n == N_BLOCK - 1, N_BLOCK // IN_GROUP - 1, jnp.maximum(n // IN_GROUP - 1, 0))

    return _call(
        body, (z, z, dmix, v_gain, w_spatial, b_spatial_t, *saved, bucket, hn1), grid=(N_BLOCK,),
        name="mixer_bwd",
        in_specs=[pl.BlockSpec((CHUNK, D_IN), lambda n: (n, 0)),
                  pl.BlockSpec((CHUNK, 256), lambda n: (jnp.maximum(n - 1, 0), 6)),
                  pl.BlockSpec((CHUNK, D), lambda n: (n, 0)),
                  pl.BlockSpec((1, 512), lambda n: (0, 0)),
                  pl.BlockSpec((N_GROUP, CHUNK, CHUNK), lambda n: (0, 0, 0)),
                  pl.BlockSpec((CHUNK, N_GROUP), lambda n: (0, 0)),
                  pl.BlockSpec((1, N_HEAD, CHUNK, 2 * CHUNK), lambda n: (n, 0, 0, 0)),
                  pl.BlockSpec((1, N_HEAD, 2 * CHUNK, CHUNK), lambda n: (n, 0, 0, 0)),
                  pl.BlockSpec((CHUNK, 128), lambda n: (n, 0)),
                  pl.BlockSpec((CHUNK, 1024), lambda n: (n, 0)), pl.BlockSpec((CHUNK, 1024), lambda n: (n, 0)),
                  pl.BlockSpec((CHUNK, 2 * CHUNK), lambda n: (0, 0)),
                  pl.BlockSpec((IN_GROUP * CHUNK, D), lambda n: (hn_group(n), 0))],
        out_specs=[pl.BlockSpec((S, D_IN), lambda n: (0, 0)),
                   pl.BlockSpec((N_GROUP, CHUNK, CHUNK), lambda n: (0, 0, 0)),
                   pl.BlockSpec((N_GROUP, CHUNK, 1), lambda n: (0, 0, 0)),
                   pl.BlockSpec((1, 512), lambda n: (0, 0)),
                   pl.BlockSpec((N_HEAD, 128), lambda n: (0, 0)),
                   pl.BlockSpec((N_HEAD, N_BUCKET, 1), lambda n: (0, 0, 0)),
                   pl.BlockSpec((D_IN, D), lambda n: (0, 0))],
        out_shape=[jax.ShapeDtypeStruct((S, D_IN), BF16), jax.ShapeDtypeStruct((N_GROUP, CHUNK, CHUNK), F32),
                   jax.ShapeDtypeStruct((N_GROUP, CHUNK, 1), F32), jax.ShapeDtypeStruct((1, 512), F32),
                   jax.ShapeDtypeStruct((N_HEAD, 128), F32), jax.ShapeDtypeStruct((N_HEAD, N_BUCKET, 1), F32),
                   jax.ShapeDtypeStruct((D_IN, D), F32)],
        scratch_shapes=[pltpu.VMEM((N_GROUP, CHUNK, CHUNK), BF16),
                        pltpu.VMEM((N_GROUP, CHUNK, CHUNK), BF16), pltpu.VMEM((N_HEAD, CHUNK, 2 * CHUNK), F32),
                        pltpu.VMEM((N_GROUP, CHUNK, CHUNK), F32), pltpu.VMEM((CHUNK, 256), F32)],
        compiler_params=_params(56), exchange=exchange)


def _in_bwd_input(dz, w_in_t, x, dh1, gain1, exchange=None):
    tm = 512

    def body(dz_ref, w_ref, x_ref, dh1_ref, g_ref, dx_ref, dg_ref):
        i = pl.program_id(0)
        dhn = _dot(dz_ref[...], w_ref[...])
        xv = x_ref[...]
        r1 = _rms_scale(xv)
        xhat = xv * r1
        dx_ref[...] = dh1_ref[...] + _rms_bwd(dhn * g_ref[...], xhat, r1)
        dg = jnp.sum(dhn * xhat, axis=0, keepdims=True)

        @pl.when(i == 0)
        def _():
            dg_ref[...] = dg

        @pl.when(i > 0)
        def _():
            dg_ref[...] += dg

    return _call(
        body, (dz, w_in_t, x, dh1, gain1), grid=(S // tm,), name="in_bwd_input",
        in_specs=[pl.BlockSpec((tm, D_IN), lambda i: (i, 0)), pl.BlockSpec((D_IN, D), lambda i: (0, 0)),
                  pl.BlockSpec((tm, D), lambda i: (i, 0)), pl.BlockSpec((tm, D), lambda i: (i, 0)),
                  pl.BlockSpec((1, D), lambda i: (0, 0))],
        out_specs=[pl.BlockSpec((tm, D), lambda i: (i, 0)), pl.BlockSpec((1, D), lambda i: (0, 0))],
        out_shape=[jax.ShapeDtypeStruct((S, D), F32), jax.ShapeDtypeStruct((1, D), F32)],
        compiler_params=_params(48), exchange=exchange)


def _rel_bucket():
    a = jnp.arange(CHUNK)[:, None]
    j = jnp.arange(2 * CHUNK)[None, :]
    n = jnp.maximum(CHUNK + a - j, 0)
    max_exact = N_BUCKET // 2
    nf = jnp.maximum(n, 1).astype(jnp.float32)
    large = max_exact + (jnp.log(nf / max_exact) / math.log(CHUNK / max_exact) * (N_BUCKET - max_exact)).astype(jnp.int32)
    large = jnp.minimum(large, N_BUCKET - 1)
    return jnp.where(n < max_exact, n, large).astype(jnp.int32)


def _step(x, p, target, small, bufs, place):
    bucket = _rel_bucket()
    sinks = small["attn_sinks"].reshape(N_HEAD)
    b_t = jnp.transpose(small["b_spatial"].reshape(N_GROUP, CHUNK))
    ws = small["w_spatial"].reshape(N_GROUP, CHUNK, CHUNK)
    gain1, gain2 = small["norm1_gain"], small["norm2_gain"]
    v_gain = small["gmlp_v_gain"]
    final_gain = small["final_gain"].reshape(1, D)
    table = small["rel_bias_table"]
    bufs = dict(bufs)

    def gather(*names):
        return _RelayGather([bufs[n] for n in names])

    def took(names, got):
        bufs.update(zip(names, got))

    w_in_t = _whole(bufs["w_in"]).reshape(D_IN, D)
    (z, hn1), got = _in_proj(x, gain1, w_in_t, gather("w_out"))
    took(["w_out"], got)
    (mix, *saved), got = _mixer_fwd(z, v_gain, ws, b_t, sinks, table, bucket, gather("w_ff1"))
    took(["w_ff1"], got)
    w_out = _whole(bufs["w_out"]).reshape(D, D)
    (h1, hn2, hn2_t), _ = _out_proj(x, mix, w_out, gain2)
    w_ff1 = _whole(bufs["w_ff1"])
    (r, a, a_t), got = _ffn_up(hn2, w_ff1, gather("w_ff2"))
    took(["w_ff2"], got)
    w_ff2 = _whole(bufs["w_ff2"])
    (h2,), got = _ffn_down(h1, a, w_ff2, gather("w_ple_gate", "w_ple_proj"))
    took(["w_ple_gate", "w_ple_proj"], got)
    dh2, d_gate, d_proj, d_final, sq, dh2b = _tail(h2, p, target, _whole(bufs["w_ple_gate"]).reshape(D, D),
                                                   _whole(bufs["w_ple_proj"]), final_gain)

    def pair_sums(halves, from_sibling):
        sums, landing = zip(*[_pair_sum(g, o, place) for g, o in zip(halves, from_sibling)])
        return list(sums), list(landing)

    landed = {}
    halves = [_halves(d_gate.reshape(N_CHIP, 256, D)), _halves(d_proj)]
    (df, d_ff2), got = _ffn_bwd_down(dh2b, r, a_t, w_ff2, _SiblingExchange(halves))
    ex, halves = _ChipExchange(*pair_sums(halves, got)), [_halves(d_ff2)]
    (d_ff1,), got = _ffn_bwd_up(df, hn2_t, _Both(ex, _SiblingExchange(halves)))
    landed.update(zip(["w_ple_gate", "w_ple_proj"], got[:2]))
    ex, halves = _ChipExchange(*pair_sums(halves, got[2:])), [_halves(d_ff1)]
    (dh1, dmix, d_out, d_gain2), got = _ffn_bwd_input(df, w_ff1, dh2, h1, gain2, mix, w_out,
                                                      _Both(ex, _SiblingExchange(halves)))
    landed["w_ff2"] = got[0]
    ex, halves = _ChipExchange(*pair_sums(halves, got[1:])), [_halves(d_out.reshape(N_CHIP, 256, D))]
    (dz, d_ws, d_b, d_vgain, d_sink, d_rel, d_in_t), got = _mixer_bwd(z, dmix, v_gain, ws, b_t, saved, bucket, hn1,
                                                                     _Both(ex, _SiblingExchange(halves)))
    landed["w_ff1"] = got[0]
    small_grads = {
        "gmlp_v_gain": d_vgain, "w_spatial": d_ws.reshape(1, N_GROUP, CHUNK, CHUNK),
        "b_spatial": d_b.reshape(1, N_GROUP, CHUNK), "attn_sinks": d_sink[:, 0].reshape(1, N_HEAD),
        "rel_bias_table": jnp.transpose(d_rel.reshape(N_HEAD, N_BUCKET)), "norm2_gain": d_gain2,
        "final_gain": d_final.reshape(D),
    }
    ex, halves = _ChipExchange(*pair_sums(halves, got[1:])), [_halves(d_in_t.reshape(N_CHIP, 448, D))]
    (dx, small_grads["norm1_gain"]), got = _in_bwd_input(dz, w_in_t, x, dh1, gain1, _Both(ex, _SiblingExchange(halves)))
    landed["w_out"] = got[0]
    return dx, landed, _ChipExchange(*pair_sums(halves, got[1:])), small_grads, sq


HBM_SPEC = pl.BlockSpec(memory_space=pltpu.HBM)
VMEM_SPEC = pl.BlockSpec(memory_space=pltpu.VMEM)


def _mesh_place():
    x, y, c = lax.axis_index("x"), lax.axis_index("y"), lax.axis_index("c")
    others = [(1 - x, y), (x, 1 - y), (1 - x, 1 - y)]
    return x, y, c, others


def _remote(src, dst, send_sem, recv_sem, device):
    return pltpu.make_async_remote_copy(src_ref=src, dst_ref=dst, send_sem=send_sem, recv_sem=recv_sem,
                                        device_id=device, device_id_type=MESH)


def _hbm_like(a, shape=None, dtype=None):
    return pltpu.HBM(a.shape if shape is None else shape, a.dtype if dtype is None else dtype)


def _gather_start(bufs, send_sems, recv_sems):
    x, y, c, others = _mesh_place()
    me = 2 * x + y
    for w, buf in enumerate(bufs):
        for k in range(3):
            mine = buf.at[me, c]
            _remote(mine, mine, send_sems.at[w, k], recv_sems.at[w, k], (*others[k], c)).start()


def _gather_finish(bufs, send_sems, recv_sems):
    x, y, c, others = _mesh_place()
    me = 2 * x + y
    sibling = (x, y, 1 - c)
    idx = [2 * ox + oy for ox, oy in others]
    chips = range(3)
    for w, buf in enumerate(bufs):
        for k in chips:
            landed = buf.at[idx[k], c]
            _remote(landed, landed, send_sems.at[w, k], recv_sems.at[w, k], sibling).wait_recv()
            _remote(landed, landed, send_sems.at[w, 3 + k], recv_sems.at[w, 3 + k], sibling).start()
    for w, buf in enumerate(bufs):
        for k in chips:
            landed = buf.at[idx[k], 1 - c]
            _remote(landed, landed, send_sems.at[w, 3 + k], recv_sems.at[w, 3 + k], sibling).wait_recv()
    for w, buf in enumerate(bufs):
        for k in chips:
            mine, passed = buf.at[me, c], buf.at[idx[k], c]
            _remote(mine, mine, send_sems.at[w, k], recv_sems.at[w, k], sibling).wait_send()
            _remote(passed, passed, send_sems.at[w, 3 + k], recv_sems.at[w, 3 + k], sibling).wait_send()


def _gather_sems(n):
    return [pltpu.SemaphoreType.DMA((n, 6)), pltpu.SemaphoreType.DMA((n, 6))]


def _sibling_copies(grads, landing, send_sems, recv_sems):
    x, y, c, _ = _mesh_place()
    return [_remote(grads[w].at[j, 1 - c], landing[w].at[j], send_sems.at[w, j], recv_sems.at[w, j], (x, y, 1 - c))
            for w in range(len(grads)) for j in range(N_CHIP)]


def _sibling_exchange_start(grads, landing, send_sems, recv_sems):
    for cp in _sibling_copies(grads, landing, send_sems, recv_sems):
        cp.start()


def _sibling_exchange_finish(grads, landing, send_sems, recv_sems):
    copies = _sibling_copies(grads, landing, send_sems, recv_sems)
    for cp in copies:
        cp.wait_recv()
    for cp in copies:
        cp.wait_send()


def _sibling_exchange_sems(n):
    return [pltpu.SemaphoreType.DMA((n, N_CHIP)), pltpu.SemaphoreType.DMA((n, N_CHIP))]


def _sibling_exchange(grads):
    n = len(grads)

    def body(*refs):
        ins, outs = refs[:n], refs[n:2 * n]
        _sibling_exchange_start(ins, outs, *refs[2 * n:])
        _sibling_exchange_finish(ins, outs, *refs[2 * n:])

    return pl.pallas_call(
        body, name="sibling_exchange",
        in_specs=[HBM_SPEC] * n, out_specs=[HBM_SPEC] * n,
        out_shape=[_hbm_like(g, (N_CHIP,) + g.shape[2:]) for g in grads],
        scratch_shapes=_sibling_exchange_sems(n),
    )(*[_in_hbm(g) for g in grads])


def _chip_exchange_start(sums, landing, send_sems, recv_sems):
    x, y, c, others = _mesh_place()
    me = 2 * x + y
    for w in range(len(sums)):
        for k, (ox, oy) in enumerate(others):
            _remote(sums[w].at[2 * ox + oy], landing[w].at[me], send_sems.at[w, k], recv_sems.at[w, k],
                    (ox, oy, c)).start()


def _chip_exchange_finish(sums, landing, send_sems, recv_sems):
    x, y, c, others = _mesh_place()
    for w in range(len(sums)):
        for k, (ox, oy) in enumerate(others):
            piece = landing[w].at[2 * ox + oy]
            _remote(piece, piece, send_sems.at[w, k], recv_sems.at[w, k], (x, y, c)).wait_recv()
    for w in range(len(sums)):
        for k, (ox, oy) in enumerate(others):
            piece = sums[w].at[2 * ox + oy]
            _remote(piece, piece, send_sems.at[w, k], recv_sems.at[w, k], (x, y, c)).wait_send()


def _chip_exchange_sems(n):
    return [pltpu.SemaphoreType.DMA((n, 3)), pltpu.SemaphoreType.DMA((n, 3))]


def _sibling_allgather(bufs, also):
    n = len(bufs)
    k_in, k_out = len(also.operands), also.n_out

    def body(*refs):
        ex_ins, refs = refs[n:n + k_in], refs[n + k_in:]
        outs, refs = refs[:n], refs[n:]
        ex_outs, refs = refs[:k_out], refs[k_out:]
        send_sems, recv_sems, ex_sems = refs[0], refs[1], refs[2:]
        x, y, c, _ = _mesh_place()
        sibling = (x, y, 1 - c)
        also.start(ex_ins, ex_outs, ex_sems)
        sends = [_remote(outs[w].at[c], outs[w].at[c], send_sems.at[w], recv_sems.at[w], sibling) for w in range(n)]
        for cp in sends:
            cp.start()
        for w in range(n):
            landed = outs[w].at[1 - c]
            _remote(landed, landed, send_sems.at[w], recv_sems.at[w], sibling).wait_recv()
        for cp in sends:
            cp.wait_send()
        also.finish(ex_ins, ex_outs, ex_sems)

    res = pl.pallas_call(
        body, name="sibling_allgather",
        in_specs=[HBM_SPEC] * (n + k_in), out_specs=[HBM_SPEC] * (n + k_out),
        out_shape=[_hbm_like(b) for b in bufs] + also.out_shape,
        input_output_aliases={**{w: w for w in range(n)}, **{n + i: n + o for i, o in also.aliases.items()}},
        scratch_shapes=[pltpu.SemaphoreType.DMA((n,)), pltpu.SemaphoreType.DMA((n,))] + also.sems,
    )(*bufs, *[_in_hbm(o) for o in also.operands])
    return list(res[:n]), list(res[n:])


def _pair_sum(grad, other, place):
    _, _, h, cols = grad.shape
    tr = _row_tile(h)

    def body(place_ref, g_ref, o_ref, sums_ref, own_ref):
        s = (g_ref[0, 0] + o_ref[0]).astype(BF16)
        sums_ref[0] = s

        @pl.when(pl.program_id(1) == place_ref[0])
        def _():
            own_ref[0] = s

    return pl.pallas_call(
        body, name="pair_sum",
        grid_spec=pltpu.PrefetchScalarGridSpec(
            num_scalar_prefetch=1, grid=(h // tr, N_CHIP),
            in_specs=[pl.BlockSpec((1, 1, tr, cols), lambda r, j, place_ref: (j, place_ref[1], r, 0)),
                      pl.BlockSpec((1, tr, cols), lambda r, j, place_ref: (j, r, 0))],
            out_specs=[pl.BlockSpec((1, tr, cols), lambda r, j, place_ref: (j, r, 0)),
                       pl.BlockSpec((1, tr, cols), lambda r, j, place_ref: (place_ref[0], r, 0))]),
        out_shape=[pltpu.HBM((N_CHIP, h, cols), BF16)] * 2,
        compiler_params=_params(32, 2),
    )(place, _in_hbm(grad), _in_hbm(other))


def _chip_sum(parts, place):
    _, h, cols = parts.shape
    tr = _row_tile(h)

    def body(place_ref, p_ref, out_ref):
        out_ref[0] = ((p_ref[0].astype(F32) + p_ref[1].astype(F32)) + p_ref[2].astype(F32)) + p_ref[3].astype(F32)

    return pl.pallas_call(
        body, name="chip_sum",
        grid_spec=pltpu.PrefetchScalarGridSpec(
            num_scalar_prefetch=1, grid=(h // tr,),
            in_specs=[pl.BlockSpec((N_CHIP, tr, cols), lambda r, place_ref: (0, r, 0))],
            out_specs=pl.BlockSpec((1, tr, cols), lambda r, place_ref: (place_ref[1], r, 0))),
        out_shape=pltpu.HBM((2, h, cols), F32),
        compiler_params=_params(32),
    )(place, _in_hbm(parts))


def _adamw_math(w, g, m, v):
    m = ADAM_B1 * m + (1.0 - ADAM_B1) * g
    v = ADAM_B2 * v + (1.0 - ADAM_B2) * (g * g)
    m_hat = m / (1.0 - ADAM_B1 ** ADAM_STEP)
    v_hat = v / (1.0 - ADAM_B2 ** ADAM_STEP)
    delta = -ADAM_LR * (m_hat / (jnp.sqrt(v_hat) + ADAM_EPS) + ADAM_WD * w)
    return delta, m, v


def _adamw(w, g, m, v, exchange=None):
    rows, cols = w.shape
    tr = _row_tile(rows)

    def body(w_ref, g_ref, m_ref, v_ref, d_ref, nm_ref, nv_ref, g_out_ref):
        g = g_ref[...]
        d_ref[...], nm_ref[...], nv_ref[...] = _adamw_math(w_ref[...], g, m_ref[...], v_ref[...])
        g_out_ref[...] = g

    spec = pl.BlockSpec((tr, cols), lambda r: (r, 0))
    return _call(
        body, (w, g, m, v), grid=(rows // tr,), name="adamw",
        in_specs=[spec] * 4, out_specs=[spec] * 4,
        out_shape=[jax.ShapeDtypeStruct((rows, cols), F32)] * 4,
        compiler_params=_params(48), exchange=exchange)


SMALL_NAMES = ("norm1_gain", "gmlp_v_gain", "w_spatial", "b_spatial", "attn_sinks", "rel_bias_table", "norm2_gain",
               "final_gain")
PACK_TILE = 8 * 128


def _pack_small(arrays):
    parts = []
    for a in arrays:
        flat = a.reshape(-1)
        rows = -(-flat.shape[0] // PACK_TILE) * 8
        parts.append(jnp.pad(flat, (0, rows * 128 - flat.shape[0])).reshape(rows, 128))
    return jnp.concatenate(parts, axis=0)


def _unpack_small(packed, like):
    out, row = [], 0
    for a in like:
        size = math.prod(a.shape)
        rows = -(-size // PACK_TILE) * 8
        out.append(packed[row:row + rows].reshape(-1)[:size].reshape(a.shape))
        row += rows
    return out


def _small_update(gathered, w, m, v):
    rows = gathered.shape[1]

    def body(g_ref, w_ref, m_ref, v_ref, tot_ref, d_ref, nm_ref, nv_ref):
        total = g_ref[0].astype(F32)
        for dev in range(1, 8):
            total = total + g_ref[dev].astype(F32)
        tot_ref[...] = total
        d_ref[...], nm_ref[...], nv_ref[...] = _adamw_math(w_ref[...], total, m_ref[...], v_ref[...])

    return pl.pallas_call(
        body, name="small_update",
        in_specs=[VMEM_SPEC] * 4, out_specs=[VMEM_SPEC] * 4,
        out_shape=[jax.ShapeDtypeStruct((rows, 128), F32)] * 4,
        compiler_params=pltpu.CompilerParams(vmem_limit_bytes=24 * MIB),
    )(gathered, w, m, v)


def _halves(a):
    return a.reshape(a.shape[:-2] + (2, a.shape[-2] // 2, a.shape[-1]))


def _whole(a):
    return a.reshape(a.shape[:-3] + (2 * a.shape[-2], a.shape[-1]))


def kernel(x, p, norm1_gain, w_in, gmlp_v_gain, w_spatial, b_spatial, attn_sinks, rel_bias_table, w_out, norm2_gain, w_ff1, w_ff2, w_ple_proj, w_ple_gate, final_gain, loss_target, m_norm1_gain, m_w_in, m_gmlp_v_gain, m_w_spatial, m_b_spatial, m_attn_sinks, m_rel_bias_table, m_w_out, m_norm2_gain, m_w_ff1, m_w_ff2, m_w_ple_proj, m_w_ple_gate, m_final_gain, v_norm1_gain, v_w_in, v_gmlp_v_gain, v_w_spatial, v_b_spatial, v_attn_sinks, v_rel_bias_table, v_w_out, v_norm2_gain, v_w_ff1, v_w_ff2, v_w_ple_proj, v_w_ple_gate, v_final_gain):
    given = dict(locals())
    small = {n: given[n] for n in SMALL_NAMES}
    chip = 2 * lax.axis_index("x") + lax.axis_index("y")
    place = jnp.stack([chip, lax.axis_index("c")]).astype(jnp.int32)

    big_names = ("w_in", "w_out", "w_ff1", "w_ff2", "w_ple_proj", "w_ple_gate")
    shards = {n: given[n][0] for n in big_names}
    travel = dict(shards, w_in=jnp.transpose(shards["w_in"]))
    rest = [n for n in big_names if n != "w_in"]
    cast, gathered = _cast_shards_beside_gather([travel[n] for n in rest], place[:1],
                                                [_cast_shard(travel["w_in"], place[:1])])
    bufs = dict(zip(rest + ["w_in"], cast + gathered))
    dx, landed, exchange_in, small_grads, sq = _step(x[0], p[0, 0], loss_target[0], small, bufs, place)

    out_grad, out_delta, out_m, out_v = {}, {}, {}, {}

    def update(n, g, exchange=None):
        to = jnp.transpose if n == "w_in" else (lambda a: a)
        (delta, new_m, new_v, g_out), got = _adamw(to(shards[n]), g, to(given["m_" + n][0]), to(given["v_" + n][0]),
                                                   exchange)
        out_grad[n], out_delta[n], out_m[n], out_v[n] = [to(a)[None] for a in (g_out, delta, new_m, new_v)]
        return got

    spare = jnp.zeros((8, 128), F32)
    small_packed = _pack_small([small_grads[n] for n in SMALL_NAMES] + [spare]).astype(BF16)
    early = [n for n in big_names if n != "w_in"]
    reduced, (small_gathered, sq_gathered, landed_in) = _sibling_allgather(
        [_chip_sum(landed[n], place) for n in early], _Both(_Both(_GatherAll(small_packed), _GatherAll(sq)), exchange_in))
    for n, r in zip(early, reduced):
        update(n, _whole(r))
    (reduced_in,), _ = _sibling_allgather([_chip_sum(landed_in, place)], _Nothing())
    update("w_in", _whole(reduced_in))

    like = [given[n] for n in SMALL_NAMES] + [spare]
    packed = _small_update(small_gathered, *[_pack_small([given[pre + n] for n in SMALL_NAMES] + [spare])
                                             for pre in ("", "m_", "v_")])
    for res, out in zip(packed, (out_grad, out_delta, out_m, out_v)):
        out.update(zip(SMALL_NAMES, _unpack_small(res, like)))
    loss = 0.5 * jnp.sum(sq_gathered[:, 0, 0]) / D

    order = ("norm1_gain", "w_in", "gmlp_v_gain", "w_spatial", "b_spatial", "attn_sinks", "rel_bias_table", "w_out",
             "norm2_gain", "w_ff1", "w_ff2", "w_ple_proj", "w_ple_gate", "final_gain")
    return (loss, dx[None], *[out_grad[n] for n in order], *[out_delta[n] for n in order],
            *[out_m[n] for n in order], *[out_v[n] for n in order])
```

```python
import functools
import math

import jax
import jax.numpy as jnp
from jax import lax
from jax.experimental import pallas as pl
from jax.experimental.pallas import tpu as pltpu

S = 2048
D = 1024
D_IN = 1792
D_FF = 4096
PLE = 256
N_CHIP = 4
N_GROUP = 4
CHUNK = 128
N_HEAD = 8
N_BLOCK = S // CHUNK
N_BUCKET = 32
EPS = 1e-6
NEG_INF = -1e30
QK_SCALE = 0.125
GELU_C = math.sqrt(2.0 / math.pi)

ADAM_LR = 0.001
ADAM_B1 = 0.9
ADAM_B2 = 0.999
ADAM_EPS = 1e-08
ADAM_WD = 0.01
ADAM_STEP = 10

F32 = jnp.float32
BF16 = jnp.bfloat16
MIB = 1024 * 1024
MESH = pl.DeviceIdType.MESH

NT = (((1,), (1,)), ((), ()))
TN = (((0,), (0,)), ((), ()))


def _dot(a, b):
    return jnp.dot(a, b, preferred_element_type=F32)


def _dot_nt(a, b):
    return lax.dot_general(a, b, NT, preferred_element_type=F32)


def _dot_tn(a, b):
    return lax.dot_general(a, b, TN, preferred_element_type=F32)


def _params(vmem_mib, n_axes=1):
    return pltpu.CompilerParams(dimension_semantics=("arbitrary",) * n_axes, vmem_limit_bytes=vmem_mib * MIB)


def _rms_scale(v):
    return lax.rsqrt(jnp.mean(v * v, axis=-1, keepdims=True) + EPS)


def _rms_bwd(dy_gain, xhat, r):
    return r * (dy_gain - xhat * jnp.mean(dy_gain * xhat, axis=-1, keepdims=True))


class _Gather:
    def __init__(self, bufs):
        self.operands = list(bufs)
        self.n_out = len(self.operands)
        self.out_shape = [_hbm_like(b) for b in bufs]
        self.aliases = {w: w for w in range(self.n_out)}
        self.sems = _gather_sems(self.n_out)

    def start(self, ins, outs, sems):
        _gather_start(outs, *sems)

    def finish(self, ins, outs, sems):
        _gather_finish(outs, *sems)


class _RelayGather(_Gather):
    TOP, BOTTOM = 6, 7
    DIAGONAL_PASSED = 5
    MIDDLE_AT, LATE_AT = (5, 8), (7, 8)

    def __init__(self, bufs):
        super().__init__(bufs)
        self.sems = [pltpu.SemaphoreType.DMA((self.n_out, 8)), pltpu.SemaphoreType.DMA((self.n_out, 8))]

    def _copies(self, bufs, send_sems, recv_sems):
        x, y, c, others = _mesh_place()
        me = 2 * x + y
        idx = [2 * ox + oy for ox, oy in others]
        sibling = (x, y, 1 - c)
        direct, passed, relayed = [], [], []
        for w, buf in enumerate(bufs):
            rows = buf.shape[2] // 2
            upper, lower = pl.ds(0, rows), pl.ds(rows, rows)
            for k in (0, 1):
                mine = buf.at[me, c]
                direct.append((_remote(mine, mine, send_sems.at[w, k], recv_sems.at[w, k], (*others[k], c)),
                               buf.at[idx[k], c], w, k))
            for k in (0, 1, 2):
                here = buf.at[idx[k], c]
                passed.append((_remote(here, here, send_sems.at[w, 3 + k], recv_sems.at[w, 3 + k], sibling),
                               buf.at[idx[k], 1 - c], w, 3 + k))
            from_x, from_y = buf.at[idx[0], c, upper], buf.at[idx[1], c, lower]
            relayed.append((_remote(from_x, from_x, send_sems.at[w, self.TOP], recv_sems.at[w, self.TOP],
                                    (*others[1], c)), buf.at[idx[2], c, upper], w, self.TOP))
            relayed.append((_remote(from_y, from_y, send_sems.at[w, self.BOTTOM], recv_sems.at[w, self.BOTTOM],
                                    (*others[0], c)), buf.at[idx[2], c, lower], w, self.BOTTOM))
        return direct, passed, relayed

    @staticmethod
    def _landed(piece, send_sems, recv_sems, w, col):
        x, y, c, _ = _mesh_place()
        _remote(piece, piece, send_sems.at[w, col], recv_sems.at[w, col], (x, y, c)).wait_recv()

    def start(self, ins, outs, sems):
        for cp, _, _, _ in self._copies(outs, *sems)[0]:
            cp.start()

    def middle(self, ins, outs, sems):
        direct, passed, relayed = self._copies(outs, *sems)
        for _, piece, w, col in direct:
            self._landed(piece, *sems, w, col)
        for cp, _, _, col in passed:
            if col != self.DIAGONAL_PASSED:
                cp.start()
        for cp, _, _, _ in relayed:
            cp.start()

    def late(self, ins, outs, sems):
        direct, passed, relayed = self._copies(outs, *sems)
        for _, piece, w, col in relayed:
            self._landed(piece, *sems, w, col)
        for cp, _, _, col in passed:
            if col == self.DIAGONAL_PASSED:
                cp.start()

    def finish(self, ins, outs, sems):
        direct, passed, relayed = self._copies(outs, *sems)
        for _, piece, w, col in passed:
            self._landed(piece, *sems, w, col)
        for cp, _, _, _ in direct + passed + relayed:
            cp.wait_send()


class _ChipExchange:
    def __init__(self, sums, landing):
        self.n_out = len(landing)
        self.operands = list(sums) + list(landing)
        self.out_shape = [_hbm_like(b) for b in landing]
        self.aliases = {self.n_out + w: w for w in range(self.n_out)}
        self.sems = _chip_exchange_sems(self.n_out)

    def start(self, ins, outs, sems):
        _chip_exchange_start(ins[:self.n_out], outs, *sems)

    def finish(self, ins, outs, sems):
        _chip_exchange_finish(ins[:self.n_out], outs, *sems)


class _GatherAll:
    def __init__(self, packed):
        self.operands = [packed]
        self.n_out = 1
        self.out_shape = [_hbm_like(packed, (8,) + packed.shape)]
        self.aliases = {}
        self.sems = [pltpu.SemaphoreType.DMA((8,)), pltpu.SemaphoreType.DMA((8,))]

    def _copies(self, ins, outs, sems):
        x, y, c, _ = _mesh_place()
        me = 4 * x + 2 * y + c
        send_sems, recv_sems = sems
        copies = []
        for k in range(1, 8):
            peer = (1 - x if k // 4 else x, 1 - y if (k // 2) % 2 else y, 1 - c if k % 2 else c)
            src = 4 * peer[0] + 2 * peer[1] + peer[2]
            copies.append((_remote(ins[0], outs[0].at[me], send_sems.at[k], recv_sems.at[k], peer), outs[0].at[src]))
        own = pltpu.make_async_copy(ins[0], outs[0].at[me], send_sems.at[0])
        return own, copies

    def start(self, ins, outs, sems):
        own, copies = self._copies(ins, outs, sems)
        own.start()
        for cp, _ in copies:
            cp.start()

    def finish(self, ins, outs, sems):
        own, copies = self._copies(ins, outs, sems)
        x, y, c, _ = _mesh_place()
        for k, (cp, landed) in enumerate(copies):
            _remote(landed, landed, sems[0].at[k + 1], sems[1].at[k + 1], (x, y, c)).wait_recv()
        for cp, _ in copies:
            cp.wait_send()
        own.wait()


class _Nothing:
    operands, n_out, out_shape, aliases, sems = [], 0, [], {}, []

    def start(self, ins, outs, sems):
        pass

    def finish(self, ins, outs, sems):
        pass


class _Both:
    def __init__(self, a, b):
        self.a, self.b = a, b
        self.operands = a.operands + b.operands
        self.n_out = a.n_out + b.n_out
        self.out_shape = a.out_shape + b.out_shape
        self.aliases = dict(a.aliases)
        self.aliases.update({len(a.operands) + i: a.n_out + o for i, o in b.aliases.items()})
        self.sems = a.sems + b.sems

    def _split(self, ins, outs, sems):
        ka, na, sa = len(self.a.operands), self.a.n_out, len(self.a.sems)
        return (ins[:ka], outs[:na], sems[:sa]), (ins[ka:], outs[na:], sems[sa:])

    def start(self, ins, outs, sems):
        for ex, args in zip((self.a, self.b), self._split(ins, outs, sems)):
            ex.start(*args)

    def finish(self, ins, outs, sems):
        for ex, args in zip((self.a, self.b), self._split(ins, outs, sems)):
            ex.finish(*args)


class _SiblingExchange:
    def __init__(self, grads):
        self.operands = list(grads)
        self.n_out = len(self.operands)
        self.out_shape = [_hbm_like(g, (N_CHIP,) + g.shape[2:]) for g in grads]
        self.aliases = {}
        self.sems = _sibling_exchange_sems(self.n_out)

    def start(self, ins, outs, sems):
        _sibling_exchange_start(ins, outs, *sems)

    def finish(self, ins, outs, sems):
        _sibling_exchange_finish(ins, outs, *sems)


def _call(body, operands, *, grid, in_specs, out_specs, out_shape, name, compiler_params, scratch_shapes=(),
          exchange=None):
    operands = [o if getattr(spec, "memory_space", None) == pltpu.SMEM else _in_hbm(o)
                for o, spec in zip(operands, in_specs)]
    out_shape = [pltpu.HBM(s.shape, s.dtype) for s in out_shape]
    if exchange is None:
        res = pl.pallas_call(body, grid=grid, in_specs=in_specs, out_specs=out_specs, out_shape=out_shape, name=name,
                             scratch_shapes=list(scratch_shapes), compiler_params=compiler_params)(*operands)
        return list(res), []
    n_in, n_out, n_scr = len(in_specs), len(out_specs), len(scratch_shapes)
    k_in, k_out = len(exchange.operands), exchange.n_out

    def fused(*refs):
        ins, refs = refs[:n_in], refs[n_in:]
        ex_ins, refs = refs[:k_in], refs[k_in:]
        outs, refs = refs[:n_out], refs[n_out:]
        ex_outs, refs = refs[:k_out], refs[k_out:]
        scratch, sems = refs[:n_scr], refs[n_scr:]
        ids = [pl.program_id(a) for a in range(len(grid))]
        first = functools.reduce(jnp.logical_and, [i == 0 for i in ids])
        last = functools.reduce(jnp.logical_and, [i == g - 1 for i, g in zip(ids, grid)])

        @pl.when(first)
        def _():
            exchange.start(ex_ins, ex_outs, sems)

        def at_step(numerator, denominator):
            at = (numerator * math.prod(grid)) // denominator
            place = [(at // math.prod(grid[a + 1:])) % grid[a] for a in range(len(grid))]
            return functools.reduce(jnp.logical_and, [i == p for i, p in zip(ids, place)])

        if hasattr(exchange, "middle"):
            @pl.when(at_step(*exchange.MIDDLE_AT))
            def _():
                exchange.middle(ex_ins, ex_outs, sems)

            @pl.when(at_step(*exchange.LATE_AT))
            def _():
                exchange.late(ex_ins, ex_outs, sems)

        body(*ins, *outs, *scratch)

        @pl.when(last)
        def _():
            exchange.finish(ex_ins, ex_outs, sems)

    res = pl.pallas_call(
        fused, grid=grid, name=name,
        in_specs=list(in_specs) + [HBM_SPEC] * k_in, out_specs=list(out_specs) + [HBM_SPEC] * k_out,
        out_shape=list(out_shape) + exchange.out_shape,
        input_output_aliases={n_in + i: n_out + o for i, o in exchange.aliases.items()},
        scratch_shapes=list(scratch_shapes) + exchange.sems, compiler_params=compiler_params,
    )(*operands, *[_in_hbm(o) for o in exchange.operands])
    return list(res[:n_out]), list(res[n_out:])


def _in_hbm(a):
    return pltpu.with_memory_space_constraint(a, pltpu.HBM)


def _row_tile(h):
    return max(t for t in range(16, 513, 16) if h % t == 0)


def _cast_shard(a, chip):
    rows, cols = a.shape
    h = rows // 2
    tr = _row_tile(h)

    def body(chip_ref, a_ref, o_ref):
        o_ref[0, 0] = a_ref[0].astype(BF16)

    return pl.pallas_call(
        body, name="cast_shard",
        grid_spec=pltpu.PrefetchScalarGridSpec(
            num_scalar_prefetch=1, grid=(2, h // tr),
            in_specs=[pl.BlockSpec((1, tr, cols), lambda s, r, chip_ref: (s, r, 0))],
            out_specs=pl.BlockSpec((1, 1, tr, cols), lambda s, r, chip_ref: (chip_ref[0], s, r, 0))),
        out_shape=pltpu.HBM((N_CHIP, 2, h, cols), BF16),
        compiler_params=_params(16, 2),
    )(chip, _in_hbm(a.reshape(2, h, cols)))


def _cast_shards_beside_gather(arrays, chip, gathered):
    n, k = len(arrays), len(gathered)
    shapes = [(a.shape[0] // 2, a.shape[1]) for a in arrays]

    def body(chip_ref, *refs):
        ins, refs = refs[:n], refs[n + k:]
        outs, refs = refs[:n], refs[n:]
        bufs, sems = refs[:k], refs[k:]
        half = pl.program_id(0)

        @pl.when(half == 0)
        def _():
            _gather_start(bufs, *sems)

        for a_ref, o_ref in zip(ins, outs):
            o_ref[0, 0] = a_ref[0].astype(BF16)

        @pl.when(half == 1)
        def _():
            _gather_finish(bufs, *sems)

    res = pl.pallas_call(
        body, name="cast_shards",
        grid_spec=pltpu.PrefetchScalarGridSpec(
            num_scalar_prefetch=1, grid=(2,),
            in_specs=[pl.BlockSpec((1, h, c), lambda s, chip_ref: (s, 0, 0)) for h, c in shapes] + [HBM_SPEC] * k,
            out_specs=[pl.BlockSpec((1, 1, h, c), lambda s, chip_ref: (chip_ref[0], s, 0, 0)) for h, c in shapes]
            + [HBM_SPEC] * k,
            scratch_shapes=_gather_sems(k)),
        out_shape=[pltpu.HBM((N_CHIP, 2, h, c), BF16) for h, c in shapes] + [_hbm_like(b) for b in gathered],
        input_output_aliases={1 + n + i: n + i for i in range(k)},
        compiler_params=_params(32),
    )(chip, *[_in_hbm(a.reshape(2, h, c)) for a, (h, c) in zip(arrays, shapes)], *gathered)
    return list(res[:n]), list(res[n:])


def _in_proj(x, gain1, w_in_t, exchange=None):
    tm = 256

    def body(x_ref, g_ref, w_ref, z_ref, hn_ref):
        xv = x_ref[...]
        hn = (xv * _rms_scale(xv) * g_ref[...]).astype(BF16)
        hn_ref[...] = hn
        z_ref[...] = _dot_nt(hn, w_ref[...])

    return _call(
        body, (x, gain1, w_in_t), grid=(S // tm,), name="in_proj",
        in_specs=[pl.BlockSpec((tm, D), lambda i: (i, 0)), pl.BlockSpec((1, D), lambda i: (0, 0)),
                  pl.BlockSpec((D_IN, D), lambda i: (0, 0))],
        out_specs=[pl.BlockSpec((tm, D_IN), lambda i: (i, 0)), pl.BlockSpec((tm, D), lambda i: (i, 0))],
        out_shape=[jax.ShapeDtypeStruct((S, D_IN), F32), jax.ShapeDtypeStruct((S, D), BF16)],
        compiler_params=_params(40), exchange=exchange)


def _gelu_parts(v):
    t = jnp.tanh(GELU_C * (v + 0.044715 * (v * v * v)))
    cdf = 0.5 * (1.0 + t)
    return cdf, t


def _band_mask(n):
    a = lax.broadcasted_iota(jnp.int32, (CHUNK, 2 * CHUNK), 0)
    j = lax.broadcasted_iota(jnp.int32, (CHUNK, 2 * CHUNK), 1)
    dist = CHUNK + a - j
    valid = (dist >= 0) & (dist < CHUNK)
    return valid & ((n > 0) | (j >= CHUNK))


def _fill_bias(bucket_ref, table_ref, bias_ref):
    bucket = bucket_ref[...]
    for h in range(N_HEAD):
        acc = jnp.zeros((CHUNK, 2 * CHUNK), F32)
        for b in range(N_BUCKET):
            acc = jnp.where(bucket == b, table_ref[b, h], acc)
        bias_ref[h] = acc


def _fill_tril(ws_ref, wt_ref, wtt_ref=None):
    r = lax.broadcasted_iota(jnp.int32, (CHUNK, CHUNK), 0)
    c = lax.broadcasted_iota(jnp.int32, (CHUNK, CHUNK), 1)
    for g in range(N_GROUP):
        w = jnp.where(c <= r, ws_ref[g], 0.0)
        wt_ref[g] = w.astype(BF16)
        if wtt_ref is not None:
            wtt_ref[g] = w.T.astype(BF16)


def _kv_layouts(kv_prev, kv_cur):
    both = jnp.concatenate([kv_prev, kv_cur], axis=0)
    k = both[:, :128]
    v = both[:, 128:]
    return (k.astype(BF16), pltpu.roll(k, 64, axis=1).astype(BF16),
            v.astype(BF16), pltpu.roll(v, 64, axis=1).astype(BF16))


def _head_place(h):
    pair, pos, kvh = h // 2, h % 2, h // 4
    return pair, pos, kvh == pos


def _softmax_sink(qm, k_use, bias_h, sink, valid):
    s = _dot_nt(qm, k_use) * QK_SCALE + bias_h
    s = jnp.where(valid, s, NEG_INF)
    m = jnp.maximum(jnp.max(s, axis=-1, keepdims=True), sink)
    e = jnp.exp(s - m)
    es = jnp.exp(sink - m)
    inv = 1.0 / (jnp.sum(e, axis=-1, keepdims=True) + es)
    return e * inv, es * inv


def _mixer_fwd(z, v_gain, w_spatial, b_spatial_t, sinks, rel_table, bucket, exchange=None):
    def body(z_ref, kvp_ref, gain_ref, ws_ref, bt_ref, sink_ref, table_ref, bucket_ref, out_ref, probs_ref, probs_t_ref,
             share_ref, guv_ref, dgelu_ref, bias_ref, wt_ref):
        n = pl.program_id(0)

        @pl.when(n == 0)
        def _():
            _fill_bias(bucket_ref, table_ref, bias_ref)
            _fill_tril(ws_ref, wt_ref)

        zuv = z_ref[:, :1024]
        cdf, t = _gelu_parts(zuv)
        guv = zuv * cdf
        guv_ref[...] = guv
        dgelu_ref[...] = cdf + zuv * (0.5 * (1.0 - t * t)) * (GELU_C * (1.0 + 3.0 * 0.044715 * (zuv * zuv)))
        for g in range(N_GROUP):
            vg = guv[:, 512 + 128 * g:512 + 128 * (g + 1)]
            vn = vg * _rms_scale(vg) * gain_ref[:, 128 * g:128 * (g + 1)]
            sv = _dot(wt_ref[g], vn.astype(BF16)) + bt_ref[:, g:g + 1]
            out_ref[:, 128 * g:128 * (g + 1)] = (guv[:, 128 * g:128 * (g + 1)] * sv).astype(BF16)

        k_same, k_swap, v_same, v_swap = _kv_layouts(kvp_ref[...], z_ref[:, 1536:1792])
        valid = _band_mask(n)
        lane = lax.broadcasted_iota(jnp.int32, (1, 128), 1)
        lane_half = lane // 64
        shares = jnp.zeros((CHUNK, 128), F32)
        for pair in range(N_HEAD // 2):
            qq = z_ref[:, 1024 + 128 * pair:1024 + 128 * (pair + 1)]
            acc = jnp.zeros((CHUNK, 128), F32)
            for pos in range(2):
                h = 2 * pair + pos
                _, _, same = _head_place(h)
                qm = jnp.where(lane_half == pos, qq, 0.0).astype(BF16)
                p, p_sink = _softmax_sink(qm, k_same if same else k_swap, bias_ref[h], sink_ref[h], valid)
                pb = p.astype(BF16)
                probs_ref[0, h] = pb
                probs_t_ref[0, h] = p.T.astype(BF16)
                shares = jnp.where(lane == h, p_sink, shares)
                vm = jnp.where(lane_half == pos, v_same if same else v_swap, jnp.zeros((), BF16))
                acc = acc + _dot(pb, vm)
            out_ref[:, 512 + 128 * pair:512 + 128 * (pair + 1)] = acc.astype(BF16)
        share_ref[...] = shares

    return _call(
        body, (z, z, v_gain, w_spatial, b_spatial_t, sinks, rel_table, bucket), grid=(N_BLOCK,), name="mixer_fwd",
        in_specs=[pl.BlockSpec((CHUNK, D_IN), lambda n: (n, 0)),
                  pl.BlockSpec((CHUNK, 256), lambda n: (jnp.maximum(n - 1, 0), 6)),
                  pl.BlockSpec((1, 512), lambda n: (0, 0)),
                  pl.BlockSpec((N_GROUP, CHUNK, CHUNK), lambda n: (0, 0, 0)),
                  pl.BlockSpec((CHUNK, N_GROUP), lambda n: (0, 0)),
                  pl.BlockSpec(memory_space=pltpu.SMEM),
                  pl.BlockSpec(memory_space=pltpu.SMEM),
                  pl.BlockSpec((CHUNK, 2 * CHUNK), lambda n: (0, 0))],
        out_specs=[pl.BlockSpec((CHUNK, D), lambda n: (n, 0)),
                   pl.BlockSpec((1, N_HEAD, CHUNK, 2 * CHUNK), lambda n: (n, 0, 0, 0)),
                   pl.BlockSpec((1, N_HEAD, 2 * CHUNK, CHUNK), lambda n: (n, 0, 0, 0)),
                   pl.BlockSpec((CHUNK, 128), lambda n: (n, 0)),
                   pl.BlockSpec((CHUNK, 1024), lambda n: (n, 0)), pl.BlockSpec((CHUNK, 1024), lambda n: (n, 0))],
        out_shape=[jax.ShapeDtypeStruct((S, D), BF16), jax.ShapeDtypeStruct((N_BLOCK, N_HEAD, CHUNK, 2 * CHUNK), BF16),
                   jax.ShapeDtypeStruct((N_BLOCK, N_HEAD, 2 * CHUNK, CHUNK), BF16), jax.ShapeDtypeStruct((S, 128), F32),
                   jax.ShapeDtypeStruct((S, 1024), F32), jax.ShapeDtypeStruct((S, 1024), F32)],
        scratch_shapes=[pltpu.VMEM((N_HEAD, CHUNK, 2 * CHUNK), F32), pltpu.VMEM((N_GROUP, CHUNK, CHUNK), BF16)],
        compiler_params=_params(32), exchange=exchange)


def _out_proj(x, mix, w_out, gain2, exchange=None):
    tm = 256

    def body(x_ref, mix_ref, w_ref, g_ref, h1_ref, hn_ref, hnt_ref):
        h1 = x_ref[...] + _dot(mix_ref[...], w_ref[...])
        h1_ref[...] = h1
        hn = h1 * _rms_scale(h1) * g_ref[...]
        hn_ref[...] = hn.astype(BF16)
        hnt_ref[...] = hn.T.astype(BF16)

    return _call(
        body, (x, mix, w_out, gain2), grid=(S // tm,), name="out_proj",
        in_specs=[pl.BlockSpec((tm, D), lambda i: (i, 0)), pl.BlockSpec((tm, D), lambda i: (i, 0)),
                  pl.BlockSpec((D, D), lambda i: (0, 0)), pl.BlockSpec((1, D), lambda i: (0, 0))],
        out_specs=[pl.BlockSpec((tm, D), lambda i: (i, 0)), pl.BlockSpec((tm, D), lambda i: (i, 0)),
                   pl.BlockSpec((D, tm), lambda i: (0, i))],
        out_shape=[jax.ShapeDtypeStruct((S, D), F32), jax.ShapeDtypeStruct((S, D), BF16),
                   jax.ShapeDtypeStruct((D, S), BF16)],
        compiler_params=_params(32), exchange=exchange)


def _ffn_up(hn2, w_ff1, exchange=None):
    tm = 512
    nj = D_FF // 1024

    def body(hn_ref, w1_ref, r_ref, a_ref, at_ref):
        r = jnp.maximum(_dot(hn_ref[...], w1_ref[0]), 0.0)
        r_ref[...] = r.astype(BF16)
        a = r * r
        a_ref[...] = a.astype(BF16)
        at_ref[...] = a.T.astype(BF16)

    return _call(
        body, (hn2, w_ff1), grid=(nj, S // tm), name="ffn_up",
        in_specs=[pl.BlockSpec((tm, D), lambda j, i: (i, 0)), pl.BlockSpec((1, D, 1024), lambda j, i: (j, 0, 0))],
        out_specs=[pl.BlockSpec((tm, 1024), lambda j, i: (i, j)), pl.BlockSpec((tm, 1024), lambda j, i: (i, j)),
                   pl.BlockSpec((1024, tm), lambda j, i: (j, i))],
        out_shape=[jax.ShapeDtypeStruct((S, D_FF), BF16), jax.ShapeDtypeStruct((S, D_FF), BF16),
                   jax.ShapeDtypeStruct((D_FF, S), BF16)],
        compiler_params=_params(40, 2), exchange=exchange)


def _ffn_down(h1, a, w_ff2, exchange=None):
    tm = 512
    nj = D_FF // 1024

    def body(h1_ref, a_ref, w2_ref, h2_ref):
        h2_ref[...] = h1_ref[...] + _dot(a_ref[...], w2_ref[...].reshape(D_FF, D))

    return _call(
        body, (h1, a, w_ff2), grid=(S // tm,), name="ffn_down",
        in_specs=[pl.BlockSpec((tm, D), lambda i: (i, 0)), pl.BlockSpec((tm, D_FF), lambda i: (i, 0)),
                  pl.BlockSpec((nj, 1024, D), lambda i: (0, 0, 0), pipeline_mode=pl.Buffered(1))],
        out_specs=[pl.BlockSpec((tm, D), lambda i: (i, 0))],
        out_shape=[jax.ShapeDtypeStruct((S, D), F32)],
        compiler_params=_params(40), exchange=exchange)


def _tail(h2, p, target, w_gate, w_proj, final_gain):
    tm = 256
    steps = S // tm

    def body(h2_ref, p_ref, t_ref, wg_ref, wp_ref, gf_ref, dh2_ref, dwg_ref, dwp_ref, dgf_ref, loss_ref, dh2b_ref,
             dwp_acc):
        i = pl.program_id(0)
        h2 = h2_ref[...]
        h2b = h2.astype(BF16)
        pb = p_ref[...].astype(BF16)
        gate = jax.nn.sigmoid(_dot(h2b, wg_ref[...]))
        pp = jnp.concatenate([_dot(pb, wp_ref[j]) for j in range(N_CHIP)], axis=1)
        h3 = h2 + gate * pp
        r3 = _rms_scale(h3)
        xhat = h3 * r3
        gf = gf_ref[...]
        err = xhat * gf - t_ref[...]
        dy = err * (1.0 / D)
        dh3 = _rms_bwd(dy * gf, xhat, r3)
        dgp = (dh3 * pp * gate * (1.0 - gate)).astype(BF16)
        dpp = (dh3 * gate).astype(BF16)
        dh2 = dh3 + _dot_nt(dgp, wg_ref[...])
        dh2_ref[...] = dh2
        dh2b_ref[...] = dh2.astype(BF16)
        dwg = _dot_tn(h2b, dgp)
        dwp = _dot_tn(pb, dpp)
        dgf = jnp.sum(dy * xhat, axis=0, keepdims=True)
        sq = jnp.sum(jnp.sum(err * err, axis=1, keepdims=True), axis=0, keepdims=True)

        @pl.when(i == 0)
        def _():
            dwg_ref[...] = dwg
            dwp_acc[...] = dwp
            dgf_ref[...] = dgf
            loss_ref[...] = jnp.broadcast_to(sq, (8, 128))

        @pl.when(i > 0)
        def _():
            dwg_ref[...] += dwg
            dwp_acc[...] += dwp
            dgf_ref[...] += dgf
            loss_ref[...] += jnp.broadcast_to(sq, (8, 128))

        @pl.when(i == steps - 1)
        def _():
            for j in range(N_CHIP):
                dwp_ref[j] = dwp_acc[:, 256 * j:256 * (j + 1)]

    return _call(
        body, (h2, p, target, w_gate, w_proj, final_gain), grid=(steps,), name="tail",
        in_specs=[pl.BlockSpec((tm, D), lambda i: (i, 0)), pl.BlockSpec((tm, PLE), lambda i: (i, 0)),
                  pl.BlockSpec((tm, D), lambda i: (i, 0)), pl.BlockSpec((D, D), lambda i: (0, 0)),
                  pl.BlockSpec((N_CHIP, PLE, 256), lambda i: (0, 0, 0)), pl.BlockSpec((1, D), lambda i: (0, 0))],
        out_specs=[pl.BlockSpec((tm, D), lambda i: (i, 0)), pl.BlockSpec((D, D), lambda i: (0, 0)),
                   pl.BlockSpec((N_CHIP, PLE, 256), lambda i: (0, 0, 0)), pl.BlockSpec((1, D), lambda i: (0, 0)),
                   pl.BlockSpec((8, 128), lambda i: (0, 0)), pl.BlockSpec((tm, D), lambda i: (i, 0))],
        out_shape=[jax.ShapeDtypeStruct((S, D), F32), jax.ShapeDtypeStruct((D, D), F32),
                   jax.ShapeDtypeStruct((N_CHIP, PLE, 256), F32), jax.ShapeDtypeStruct((1, D), F32),
                   jax.ShapeDtypeStruct((8, 128), F32), jax.ShapeDtypeStruct((S, D), BF16)],
        scratch_shapes=[pltpu.VMEM((PLE, D), F32)],
        compiler_params=_params(48))[0]


def _ffn_bwd_down(dh2b, r, a_t, w_ff2, exchange=None):
    tm = 1024
    nj = D_FF // 1024

    def body(dh2_ref, r_ref, at_ref, w2_ref, df_ref, dw2_ref):
        i = pl.program_id(1)
        dh2b = dh2_ref[...]
        da = _dot_nt(dh2b, w2_ref[0])
        df_ref[...] = (da * (2.0 * r_ref[...].astype(F32))).astype(BF16)
        dw2 = _dot(at_ref[...], dh2b)

        @pl.when(i == 0)
        def _():
            dw2_ref[0] = dw2

        @pl.when(i > 0)
        def _():
            dw2_ref[0] += dw2

    return _call(
        body, (dh2b, r, a_t, w_ff2), grid=(nj, S // tm), name="ffn_bwd_down",
        in_specs=[pl.BlockSpec((tm, D), lambda j, i: (i, 0)), pl.BlockSpec((tm, 1024), lambda j, i: (i, j)),
                  pl.BlockSpec((1024, tm), lambda j, i: (j, i)), pl.BlockSpec((1, 1024, D), lambda j, i: (j, 0, 0))],
        out_specs=[pl.BlockSpec((tm, 1024), lambda j, i: (i, j)), pl.BlockSpec((1, 1024, D), lambda j, i: (j, 0, 0))],
        out_shape=[jax.ShapeDtypeStruct((S, D_FF), BF16), jax.ShapeDtypeStruct((nj, 1024, D), F32)],
        compiler_params=_params(48, 2), exchange=exchange)


def _ffn_bwd_up(df, hn2_t, exchange=None):
    tm = 2048
    nj = D_FF // 1024

    def body(df_ref, hnt_ref, dw1_ref):
        i = pl.program_id(1)
        dw1 = _dot(hnt_ref[...], df_ref[...])

        @pl.when(i == 0)
        def _():
            dw1_ref[0] = dw1

        @pl.when(i > 0)
        def _():
            dw1_ref[0] += dw1

    return _call(
        body, (df, hn2_t), grid=(nj, S // tm), name="ffn_bwd_up",
        in_specs=[pl.BlockSpec((tm, 1024), lambda j, i: (i, j)), pl.BlockSpec((D, tm), lambda j, i: (0, i))],
        out_specs=[pl.BlockSpec((1, D, 1024), lambda j, i: (j, 0, 0))],
        out_shape=[jax.ShapeDtypeStruct((nj, D, 1024), F32)],
        compiler_params=_params(40, 2), exchange=exchange)


def _ffn_bwd_input(df, w_ff1, dh2, h1, gain2, mix, w_out, exchange=None):
    tm = 512
    nj = D_FF // 1024
    steps = S // tm

    def body(df_ref, w1_ref, dh2_ref, h1_ref, g_ref, mix_ref, wo_ref, dh1_ref, dmix_ref, dwo_ref, dg_ref, acc_ref):
        i = pl.program_id(0)
        j = pl.program_id(1)
        part = _dot_nt(df_ref[...], w1_ref[j])

        @pl.when(j == 0)
        def _():
            acc_ref[...] = part

        @pl.when(j > 0)
        def _():
            acc_ref[...] += part

        @pl.when(j == nj - 1)
        def _():
            dhn = acc_ref[...]
            h1 = h1_ref[...]
            r2 = _rms_scale(h1)
            xhat = h1 * r2
            dh1 = dh2_ref[...] + _rms_bwd(dhn * g_ref[...], xhat, r2)
            dh1_ref[...] = dh1
            dh1b = dh1.astype(BF16)
            dmix_ref[...] = _dot_nt(dh1b, wo_ref[...])
            dwo = _dot_tn(mix_ref[...], dh1b)
            dg = jnp.sum(dhn * xhat, axis=0, keepdims=True)

            @pl.when(i == 0)
            def _():
                dwo_ref[...] = dwo
                dg_ref[...] = dg

            @pl.when(i > 0)
            def _():
                dwo_ref[...] += dwo
                dg_ref[...] += dg

    return _call(
        body, (df, w_ff1, dh2, h1, gain2, mix, w_out), grid=(steps, nj), name="ffn_bwd_input",
        in_specs=[pl.BlockSpec((tm, 1024), lambda i, j: (i, j)),
                  pl.BlockSpec((nj, D, 1024), lambda i, j: (0, 0, 0), pipeline_mode=pl.Buffered(1)),
                  pl.BlockSpec((tm, D), lambda i, j: (i, 0)), pl.BlockSpec((tm, D), lambda i, j: (i, 0)),
                  pl.BlockSpec((1, D), lambda i, j: (0, 0)), pl.BlockSpec((tm, D), lambda i, j: (i, 0)),
                  pl.BlockSpec((D, D), lambda i, j: (0, 0), pipeline_mode=pl.Buffered(1))],
        out_specs=[pl.BlockSpec((tm, D), lambda i, j: (i, 0)), pl.BlockSpec((tm, D), lambda i, j: (i, 0)),
                   pl.BlockSpec((D, D), lambda i, j: (0, 0)), pl.BlockSpec((1, D), lambda i, j: (0, 0))],
        out_shape=[jax.ShapeDtypeStruct((S, D), F32), jax.ShapeDtypeStruct((S, D), F32),
                   jax.ShapeDtypeStruct((D, D), F32), jax.ShapeDtypeStruct((1, D), F32)],
        scratch_shapes=[pltpu.VMEM((tm, D), F32)],
        compiler_params=_params(56, 2), exchange=exchange)


IN_GROUP = 16


def _mixer_bwd(z, dmix, v_gain, w_spatial, b_spatial_t, saved, bucket, hn1, exchange=None):
    def body(z_ref, kvp_ref, dm_ref, gain_ref, ws_ref, bt_ref, probs_ref, probs_t_ref, share_ref, guv_ref, dgelu_ref,
             bucket_ref, hn_ref,
             dz_ref, dws_ref, db_ref, dgain_ref, dsink_ref, drel_ref, dwin_ref,
             wt_ref, wtt_ref, dbias_ref, dsv_ref, carry_ref):
        n = pl.program_id(0)

        @pl.when(n == 0)
        def _():
            _fill_tril(ws_ref, wt_ref, wtt_ref)
            dwin_ref[...] = jnp.zeros_like(dwin_ref)
            dbias_ref[...] = jnp.zeros_like(dbias_ref)
            dsv_ref[...] = jnp.zeros_like(dsv_ref)
            dws_ref[...] = jnp.zeros_like(dws_ref)
            dgain_ref[...] = jnp.zeros_like(dgain_ref)
            dsink_ref[...] = jnp.zeros_like(dsink_ref)

        rows = pl.ds(pl.multiple_of(n * CHUNK, CHUNK), CHUNK)

        guv = guv_ref[...]
        dgelu = dgelu_ref[...]
        for g in range(N_GROUP):
            lo, hi = 128 * g, 128 * (g + 1)
            u = guv[:, lo:hi]
            vg = guv[:, 512 + lo:512 + hi]
            rr = _rms_scale(vg)
            vhat = vg * rr
            gain = gain_ref[:, lo:hi]
            vnb = (vhat * gain).astype(BF16)
            sv = _dot(wt_ref[g], vnb) + bt_ref[:, g:g + 1]
            da = dm_ref[:, lo:hi]
            dsv = da * u
            dsvb = dsv.astype(BF16)
            dsv_ref[g] += dsv
            dws_ref[g] += _dot_nt(dsvb, vnb)
            dvn = _dot(wtt_ref[g], dsvb)
            dgain_ref[:, lo:hi] += jnp.sum(dvn * vhat, axis=0, keepdims=True)
            dvg = _rms_bwd(dvn * gain, vhat, rr)
            dz_ref[rows, lo:hi] = (da * sv * dgelu[:, lo:hi]).astype(BF16)
            dz_ref[rows, 512 + lo:512 + hi] = (dvg * dgelu[:, 512 + lo:512 + hi]).astype(BF16)

        k_same, k_swap, v_same, v_swap = _kv_layouts(kvp_ref[...], z_ref[:, 1536:1792])
        lane_half = lax.broadcasted_iota(jnp.int32, (1, 128), 1) // 64
        zero = jnp.zeros((2 * CHUNK, 128), F32)
        dk_same, dk_swap, dv_same, dv_swap = zero, zero, zero, zero
        for pair in range(N_HEAD // 2):
            cols = slice(1024 + 128 * pair, 1024 + 128 * (pair + 1))
            qq = z_ref[:, cols]
            do_pair = dm_ref[:, 512 + 128 * pair:512 + 128 * (pair + 1)]
            dq = jnp.zeros((CHUNK, 128), F32)
            for pos in range(2):
                h = 2 * pair + pos
                _, _, same = _head_place(h)
                on_half = lane_half == pos
                qm = jnp.where(on_half, qq, 0.0).astype(BF16)
                k_use = k_same if same else k_swap
                v_use = v_same if same else v_swap
                pb = probs_ref[0, h]
                p = pb.astype(F32)
                p_sink = share_ref[:, h:h + 1]
                dom = jnp.where(on_half, do_pair, 0.0).astype(BF16)
                dp = _dot_nt(dom, v_use)
                dsum = jnp.sum(p * dp, axis=-1, keepdims=True)
                ds = p * (dp - dsum)
                dbias_ref[h] += ds
                dsink_ref[h:h + 1, :] += jnp.broadcast_to(jnp.sum(-p_sink * dsum, axis=0, keepdims=True), (1, 128))
                dsb = ds.astype(BF16)
                dq = dq + jnp.where(on_half, _dot(dsb, k_use), 0.0)
                dk_h = _dot_tn(dsb, qm)
                dv_h = _dot(probs_t_ref[0, h], dom)
                if same:
                    dk_same, dv_same = dk_same + dk_h, dv_same + dv_h
                else:
                    dk_swap, dv_swap = dk_swap + dk_h, dv_swap + dv_h
            dz_ref[rows, cols] = (dq * QK_SCALE).astype(BF16)
        dk = (dk_same + pltpu.roll(dk_swap, 64, axis=1)) * QK_SCALE
        dv = dv_same + pltpu.roll(dv_swap, 64, axis=1)
        dkv = jnp.concatenate([dk, dv], axis=1)

        @pl.when(n > 0)
        def _():
            prev_rows = pl.ds(pl.multiple_of((n - 1) * CHUNK, CHUNK), CHUNK)
            dz_ref[prev_rows, 1536:1792] = (carry_ref[...] + dkv[:CHUNK]).astype(BF16)

        carry_ref[...] = dkv[CHUNK:]

        @pl.when((n > 0) & (n % IN_GROUP == 0))
        def _():
            done = pl.ds(pl.multiple_of((n - IN_GROUP) * CHUNK, IN_GROUP * CHUNK), IN_GROUP * CHUNK)
            dwin_ref[...] += _dot_tn(dz_ref[done, :], hn_ref[...])

        @pl.when(n == N_BLOCK - 1)
        def _():
            dz_ref[rows, 1536:1792] = dkv[CHUNK:].astype(BF16)
            last = pl.ds((N_BLOCK - IN_GROUP) * CHUNK, IN_GROUP * CHUNK)
            dwin_ref[...] += _dot_tn(dz_ref[last, :], hn_ref[...])
            r = lax.broadcasted_iota(jnp.int32, (CHUNK, CHUNK), 0)
            c = lax.broadcasted_iota(jnp.int32, (CHUNK, CHUNK), 1)
            for g in range(N_GROUP):
                dws_ref[g] = jnp.where(c <= r, dws_ref[g], 0.0)
                db_ref[g] = jnp.sum(dsv_ref[g], axis=1, keepdims=True)
            bucket = bucket_ref[...]
            for h in range(N_HEAD):
                dbh = dbias_ref[h]
                per_bucket = [jnp.sum(jnp.where(bucket == b, dbh, 0.0), axis=0, keepdims=True) for b in range(N_BUCKET)]
                drel_ref[h] = jnp.sum(jnp.concatenate(per_bucket, axis=0), axis=1, keepdims=True)

    def hn_group(n):
        return jnp.where(n == N_BLOCK - 1, N_BLOCK // IN_GROUP - 1, jnp.maximum(n // IN_GROUP - 1, 0))

    return _call(
        body, (z, z, dmix, v_gain, w_spatial, b_spatial_t, *saved, bucket, hn1), grid=(N_BLOCK,),
        name="mixer_bwd",
        in_specs=[pl.BlockSpec((CHUNK, D_IN), lambda n: (n, 0)),
                  pl.BlockSpec((CHUNK, 256), lambda n: (jnp.maximum(n - 1, 0), 6)),
                  pl.BlockSpec((CHUNK, D), lambda n: (n, 0)),
                  pl.BlockSpec((1, 512), lambda n: (0, 0)),
                  pl.BlockSpec((N_GROUP, CHUNK, CHUNK), lambda n: (0, 0, 0)),
                  pl.BlockSpec((CHUNK, N_GROUP), lambda n: (0, 0)),
                  pl.BlockSpec((1, N_HEAD, CHUNK, 2 * CHUNK), lambda n: (n, 0, 0, 0)),
                  pl.BlockSpec((1, N_HEAD, 2 * CHUNK, CHUNK), lambda n: (n, 0, 0, 0)),
                  pl.BlockSpec((CHUNK, 128), lambda n: (n, 0)),
                  pl.BlockSpec((CHUNK, 1024), lambda n: (n, 0)), pl.BlockSpec((CHUNK, 1024), lambda n: (n, 0)),
                  pl.BlockSpec((CHUNK, 2 * CHUNK), lambda n: (0, 0)),
                  pl.BlockSpec((IN_GROUP * CHUNK, D), lambda n: (hn_group(n), 0))],
        out_specs=[pl.BlockSpec((S, D_IN), lambda n: (0, 0)),
                   pl.BlockSpec((N_GROUP, CHUNK, CHUNK), lambda n: (0, 0, 0)),
                   pl.BlockSpec((N_GROUP, CHUNK, 1), lambda n: (0, 0, 0)),
                   pl.BlockSpec((1, 512), lambda n: (0, 0)),
                   pl.BlockSpec((N_HEAD, 128), lambda n: (0, 0)),
                   pl.BlockSpec((N_HEAD, N_BUCKET, 1), lambda n: (0, 0, 0)),
                   pl.BlockSpec((D_IN, D), lambda n: (0, 0))],
        out_shape=[jax.ShapeDtypeStruct((S, D_IN), BF16), jax.ShapeDtypeStruct((N_GROUP, CHUNK, CHUNK), F32),
                   jax.ShapeDtypeStruct((N_GROUP, CHUNK, 1), F32), jax.ShapeDtypeStruct((1, 512), F32),
                   jax.ShapeDtypeStruct((N_HEAD, 128), F32), jax.ShapeDtypeStruct((N_HEAD, N_BUCKET, 1), F32),
                   jax.ShapeDtypeStruct((D_IN, D), F32)],
        scratch_shapes=[pltpu.VMEM((N_GROUP, CHUNK, CHUNK), BF16),
                        pltpu.VMEM((N_GROUP, CHUNK, CHUNK), BF16), pltpu.VMEM((N_HEAD, CHUNK, 2 * CHUNK), F32),
                        pltpu.VMEM((N_GROUP, CHUNK, CHUNK), F32), pltpu.VMEM((CHUNK, 256), F32)],
        compiler_params=_params(56), exchange=exchange)


def _in_bwd_input(dz, w_in_t, x, dh1, gain1, exchange=None):
    tm = 512

    def body(dz_ref, w_ref, x_ref, dh1_ref, g_ref, dx_ref, dg_ref):
        i = pl.program_id(0)
        dhn = _dot(dz_ref[...], w_ref[...])
        xv = x_ref[...]
        r1 = _rms_scale(xv)
        xhat = xv * r1
        dx_ref[...] = dh1_ref[...] + _rms_bwd(dhn * g_ref[...], xhat, r1)
        dg = jnp.sum(dhn * xhat, axis=0, keepdims=True)

        @pl.when(i == 0)
        def _():
            dg_ref[...] = dg

        @pl.when(i > 0)
        def _():
            dg_ref[...] += dg

    return _call(
        body, (dz, w_in_t, x, dh1, gain1), grid=(S // tm,), name="in_bwd_input",
        in_specs=[pl.BlockSpec((tm, D_IN), lambda i: (i, 0)), pl.BlockSpec((D_IN, D), lambda i: (0, 0)),
                  pl.BlockSpec((tm, D), lambda i: (i, 0)), pl.BlockSpec((tm, D), lambda i: (i, 0)),
                  pl.BlockSpec((1, D), lambda i: (0, 0))],
        out_specs=[pl.BlockSpec((tm, D), lambda i: (i, 0)), pl.BlockSpec((1, D), lambda i: (0, 0))],
        out_shape=[jax.ShapeDtypeStruct((S, D), F32), jax.ShapeDtypeStruct((1, D), F32)],
        compiler_params=_params(48), exchange=exchange)


def _rel_bucket():
    a = jnp.arange(CHUNK)[:, None]
    j = jnp.arange(2 * CHUNK)[None, :]
    n = jnp.maximum(CHUNK + a - j, 0)
    max_exact = N_BUCKET // 2
    nf = jnp.maximum(n, 1).astype(jnp.float32)
    large = max_exact + (jnp.log(nf / max_exact) / math.log(CHUNK / max_exact) * (N_BUCKET - max_exact)).astype(jnp.int32)
    large = jnp.minimum(large, N_BUCKET - 1)
    return jnp.where(n < max_exact, n, large).astype(jnp.int32)


def _step(x, p, target, small, bufs, place):
    bucket = _rel_bucket()
    sinks = small["attn_sinks"].reshape(N_HEAD)
    b_t = jnp.transpose(small["b_spatial"].reshape(N_GROUP, CHUNK))
    ws = small["w_spatial"].reshape(N_GROUP, CHUNK, CHUNK)
    gain1, gain2 = small["norm1_gain"], small["norm2_gain"]
    v_gain = small["gmlp_v_gain"]
    final_gain = small["final_gain"].reshape(1, D)
    table = small["rel_bias_table"]
    bufs = dict(bufs)

    def gather(*names):
        return _RelayGather([bufs[n] for n in names])

    def took(names, got):
        bufs.update(zip(names, got))

    w_in_t = _whole(bufs["w_in"]).reshape(D_IN, D)
    (z, hn1), got = _in_proj(x, gain1, w_in_t, gather("w_out"))
    took(["w_out"], got)
    (mix, *saved), got = _mixer_fwd(z, v_gain, ws, b_t, sinks, table, bucket, gather("w_ff1"))
    took(["w_ff1"], got)
    w_out = _whole(bufs["w_out"]).reshape(D, D)
    (h1, hn2, hn2_t), _ = _out_proj(x, mix, w_out, gain2)
    w_ff1 = _whole(bufs["w_ff1"])
    (r, a, a_t), got = _ffn_up(hn2, w_ff1, gather("w_ff2"))
    took(["w_ff2"], got)
    w_ff2 = _whole(bufs["w_ff2"])
    (h2,), got = _ffn_down(h1, a, w_ff2, gather("w_ple_gate", "w_ple_proj"))
    took(["w_ple_gate", "w_ple_proj"], got)
    dh2, d_gate, d_proj, d_final, sq, dh2b = _tail(h2, p, target, _whole(bufs["w_ple_gate"]).reshape(D, D),
                                                   _whole(bufs["w_ple_proj"]), final_gain)

    def pair_sums(halves, from_sibling):
        sums, landing = zip(*[_pair_sum(g, o, place) for g, o in zip(halves, from_sibling)])
        return list(sums), list(landing)

    landed = {}
    halves = [_halves(d_gate.reshape(N_CHIP, 256, D)), _halves(d_proj)]
    (df, d_ff2), got = _ffn_bwd_down(dh2b, r, a_t, w_ff2, _SiblingExchange(halves))
    ex, halves = _ChipExchange(*pair_sums(halves, got)), [_halves(d_ff2)]
    (d_ff1,), got = _ffn_bwd_up(df, hn2_t, _Both(ex, _SiblingExchange(halves)))
    landed.update(zip(["w_ple_gate", "w_ple_proj"], got[:2]))
    ex, halves = _ChipExchange(*pair_sums(halves, got[2:])), [_halves(d_ff1)]
    (dh1, dmix, d_out, d_gain2), got = _ffn_bwd_input(df, w_ff1, dh2, h1, gain2, mix, w_out,
                                                      _Both(ex, _SiblingExchange(halves)))
    landed["w_ff2"] = got[0]
    ex, halves = _ChipExchange(*pair_sums(halves, got[1:])), [_halves(d_out.reshape(N_CHIP, 256, D))]
    (dz, d_ws, d_b, d_vgain, d_sink, d_rel, d_in_t), got = _mixer_bwd(z, dmix, v_gain, ws, b_t, saved, bucket, hn1,
                                                                     _Both(ex, _SiblingExchange(halves)))
    landed["w_ff1"] = got[0]
    small_grads = {
        "gmlp_v_gain": d_vgain, "w_spatial": d_ws.reshape(1, N_GROUP, CHUNK, CHUNK),
        "b_spatial": d_b.reshape(1, N_GROUP, CHUNK), "attn_sinks": d_sink[:, 0].reshape(1, N_HEAD),
        "rel_bias_table": jnp.transpose(d_rel.reshape(N_HEAD, N_BUCKET)), "norm2_gain": d_gain2,
        "final_gain": d_final.reshape(D),
    }
    ex, halves = _ChipExchange(*pair_sums(halves, got[1:])), [_halves(d_in_t.reshape(N_CHIP, 448, D))]
    (dx, small_grads["norm1_gain"]), got = _in_bwd_input(dz, w_in_t, x, dh1, gain1, _Both(ex, _SiblingExchange(halves)))
    landed["w_out"] = got[0]
    return dx, landed, _ChipExchange(*pair_sums(halves, got[1:])), small_grads, sq


HBM_SPEC = pl.BlockSpec(memory_space=pltpu.HBM)
VMEM_SPEC = pl.BlockSpec(memory_space=pltpu.VMEM)


def _mesh_place():
    x, y, c = lax.axis_index("x"), lax.axis_index("y"), lax.axis_index("c")
    others = [(1 - x, y), (x, 1 - y), (1 - x, 1 - y)]
    return x, y, c, others


def _remote(src, dst, send_sem, recv_sem, device):
    return pltpu.make_async_remote_copy(src_ref=src, dst_ref=dst, send_sem=send_sem, recv_sem=recv_sem,
                                        device_id=device, device_id_type=MESH)


def _hbm_like(a, shape=None, dtype=None):
    return pltpu.HBM(a.shape if shape is None else shape, a.dtype if dtype is None else dtype)


def _gather_start(bufs, send_sems, recv_sems):
    x, y, c, others = _mesh_place()
    me = 2 * x + y
    for w, buf in enumerate(bufs):
        for k in range(3):
            mine = buf.at[me, c]
            _remote(mine, mine, send_sems.at[w, k], recv_sems.at[w, k], (*others[k], c)).start()


def _gather_finish(bufs, send_sems, recv_sems):
    x, y, c, others = _mesh_place()
    me = 2 * x + y
    sibling = (x, y, 1 - c)
    idx = [2 * ox + oy for ox, oy in others]
    chips = range(3)
    for w, buf in enumerate(bufs):
        for k in chips:
            landed = buf.at[idx[k], c]
            _remote(landed, landed, send_sems.at[w, k], recv_sems.at[w, k], sibling).wait_recv()
            _remote(landed, landed, send_sems.at[w, 3 + k], recv_sems.at[w, 3 + k], sibling).start()
    for w, buf in enumerate(bufs):
        for k in chips:
            landed = buf.at[idx[k], 1 - c]
            _remote(landed, landed, send_sems.at[w, 3 + k], recv_sems.at[w, 3 + k], sibling).wait_recv()
    for w, buf in enumerate(bufs):
        for k in chips:
            mine, passed = buf.at[me, c], buf.at[idx[k], c]
            _remote(mine, mine, send_sems.at[w, k], recv_sems.at[w, k], sibling).wait_send()
            _remote(passed, passed, send_sems.at[w, 3 + k], recv_sems.at[w, 3 + k], sibling).wait_send()


def _gather_sems(n):
    return [pltpu.SemaphoreType.DMA((n, 6)), pltpu.SemaphoreType.DMA((n, 6))]


def _sibling_copies(grads, landing, send_sems, recv_sems):
    x, y, c, _ = _mesh_place()
    return [_remote(grads[w].at[j, 1 - c], landing[w].at[j], send_sems.at[w, j], recv_sems.at[w, j], (x, y, 1 - c))
            for w in range(len(grads)) for j in range(N_CHIP)]


def _sibling_exchange_start(grads, landing, send_sems, recv_sems):
    for cp in _sibling_copies(grads, landing, send_sems, recv_sems):
        cp.start()


def _sibling_exchange_finish(grads, landing, send_sems, recv_sems):
    copies = _sibling_copies(grads, landing, send_sems, recv_sems)
    for cp in copies:
        cp.wait_recv()
    for cp in copies:
        cp.wait_send()


def _sibling_exchange_sems(n):
    return [pltpu.SemaphoreType.DMA((n, N_CHIP)), pltpu.SemaphoreType.DMA((n, N_CHIP))]


def _sibling_exchange(grads):
    n = len(grads)

    def body(*refs):
        ins, outs = refs[:n], refs[n:2 * n]
        _sibling_exchange_start(ins, outs, *refs[2 * n:])
        _sibling_exchange_finish(ins, outs, *refs[2 * n:])

    return pl.pallas_call(
        body, name="sibling_exchange",
        in_specs=[HBM_SPEC] * n, out_specs=[HBM_SPEC] * n,
        out_shape=[_hbm_like(g, (N_CHIP,) + g.shape[2:]) for g in grads],
        scratch_shapes=_sibling_exchange_sems(n),
    )(*[_in_hbm(g) for g in grads])


def _chip_exchange_start(sums, landing, send_sems, recv_sems):
    x, y, c, others = _mesh_place()
    me = 2 * x + y
    for w in range(len(sums)):
        for k, (ox, oy) in enumerate(others):
            _remote(sums[w].at[2 * ox + oy], landing[w].at[me], send_sems.at[w, k], recv_sems.at[w, k],
                    (ox, oy, c)).start()


def _chip_exchange_finish(sums, landing, send_sems, recv_sems):
    x, y, c, others = _mesh_place()
    for w in range(len(sums)):
        for k, (ox, oy) in enumerate(others):
            piece = landing[w].at[2 * ox + oy]
            _remote(piece, piece, send_sems.at[w, k], recv_sems.at[w, k], (x, y, c)).wait_recv()
    for w in range(len(sums)):
        for k, (ox, oy) in enumerate(others):
            piece = sums[w].at[2 * ox + oy]
            _remote(piece, piece, send_sems.at[w, k], recv_sems.at[w, k], (x, y, c)).wait_send()


def _chip_exchange_sems(n):
    return [pltpu.SemaphoreType.DMA((n, 3)), pltpu.SemaphoreType.DMA((n, 3))]


def _sibling_allgather(bufs, also):
    n = len(bufs)
    k_in, k_out = len(also.operands), also.n_out

    def body(*refs):
        ex_ins, refs = refs[n:n + k_in], refs[n + k_in:]
        outs, refs = refs[:n], refs[n:]
        ex_outs, refs = refs[:k_out], refs[k_out:]
        send_sems, recv_sems, ex_sems = refs[0], refs[1], refs[2:]
        x, y, c, _ = _mesh_place()
        sibling = (x, y, 1 - c)
        also.start(ex_ins, ex_outs, ex_sems)
        sends = [_remote(outs[w].at[c], outs[w].at[c], send_sems.at[w], recv_sems.at[w], sibling) for w in range(n)]
        for cp in sends:
            cp.start()
        for w in range(n):
            landed = outs[w].at[1 - c]
            _remote(landed, landed, send_sems.at[w], recv_sems.at[w], sibling).wait_recv()
        for cp in sends:
            cp.wait_send()
        also.finish(ex_ins, ex_outs, ex_sems)

    res = pl.pallas_call(
        body, name="sibling_allgather",
        in_specs=[HBM_SPEC] * (n + k_in), out_specs=[HBM_SPEC] * (n + k_out),
        out_shape=[_hbm_like(b) for b in bufs] + also.out_shape,
        input_output_aliases={**{w: w for w in range(n)}, **{n + i: n + o for i, o in also.aliases.items()}},
        scratch_shapes=[pltpu.SemaphoreType.DMA((n,)), pltpu.SemaphoreType.DMA((n,))] + also.sems,
    )(*bufs, *[_in_hbm(o) for o in also.operands])
    return list(res[:n]), list(res[n:])


def _pair_sum(grad, other, place):
    _, _, h, cols = grad.shape
    tr = _row_tile(h)

    def body(place_ref, g_ref, o_ref, sums_ref, own_ref):
        s = (g_ref[0, 0] + o_ref[0]).astype(BF16)
        sums_ref[0] = s

        @pl.when(pl.program_id(1) == place_ref[0])
        def _():
            own_ref[0] = s

    return pl.pallas_call(
        body, name="pair_sum",
        grid_spec=pltpu.PrefetchScalarGridSpec(
            num_scalar_prefetch=1, grid=(h // tr, N_CHIP),
            in_specs=[pl.BlockSpec((1, 1, tr, cols), lambda r, j, place_ref: (j, place_ref[1], r, 0)),
                      pl.BlockSpec((1, tr, cols), lambda r, j, place_ref: (j, r, 0))],
            out_specs=[pl.BlockSpec((1, tr, cols), lambda r, j, place_ref: (j, r, 0)),
                       pl.BlockSpec((1, tr, cols), lambda r, j, place_ref: (place_ref[0], r, 0))]),
        out_shape=[pltpu.HBM((N_CHIP, h, cols), BF16)] * 2,
        compiler_params=_params(32, 2),
    )(place, _in_hbm(grad), _in_hbm(other))


def _chip_sum(parts, place):
    _, h, cols = parts.shape
    tr = _row_tile(h)

    def body(place_ref, p_ref, out_ref):
        out_ref[0] = ((p_ref[0].astype(F32) + p_ref[1].astype(F32)) + p_ref[2].astype(F32)) + p_ref[3].astype(F32)

    return pl.pallas_call(
        body, name="chip_sum",
        grid_spec=pltpu.PrefetchScalarGridSpec(
            num_scalar_prefetch=1, grid=(h // tr,),
            in_specs=[pl.BlockSpec((N_CHIP, tr, cols), lambda r, place_ref: (0, r, 0))],
            out_specs=pl.BlockSpec((1, tr, cols), lambda r, place_ref: (place_ref[1], r, 0))),
        out_shape=pltpu.HBM((2, h, cols), F32),
        compiler_params=_params(32),
    )(place, _in_hbm(parts))


def _adamw_math(w, g, m, v):
    m = ADAM_B1 * m + (1.0 - ADAM_B1) * g
    v = ADAM_B2 * v + (1.0 - ADAM_B2) * (g * g)
    m_hat = m / (1.0 - ADAM_B1 ** ADAM_STEP)
    v_hat = v / (1.0 - ADAM_B2 ** ADAM_STEP)
    delta = -ADAM_LR * (m_hat / (jnp.sqrt(v_hat) + ADAM_EPS) + ADAM_WD * w)
    return delta, m, v


def _adamw(w, g, m, v, exchange=None):
    rows, cols = w.shape
    tr = _row_tile(rows)

    def body(w_ref, g_ref, m_ref, v_ref, d_ref, nm_ref, nv_ref, g_out_ref):
        g = g_ref[...]
        d_ref[...], nm_ref[...], nv_ref[...] = _adamw_math(w_ref[...], g, m_ref[...], v_ref[...])
        g_out_ref[...] = g

    spec = pl.BlockSpec((tr, cols), lambda r: (r, 0))
    return _call(
        body, (w, g, m, v), grid=(rows // tr,), name="adamw",
        in_specs=[spec] * 4, out_specs=[spec] * 4,
        out_shape=[jax.ShapeDtypeStruct((rows, cols), F32)] * 4,
        compiler_params=_params(48), exchange=exchange)


SMALL_NAMES = ("norm1_gain", "gmlp_v_gain", "w_spatial", "b_spatial", "attn_sinks", "rel_bias_table", "norm2_gain",
               "final_gain")
PACK_TILE = 8 * 128


def _pack_small(arrays):
    parts = []
    for a in arrays:
        flat = a.reshape(-1)
        rows = -(-flat.shape[0] // PACK_TILE) * 8
        parts.append(jnp.pad(flat, (0, rows * 128 - flat.shape[0])).reshape(rows, 128))
    return jnp.concatenate(parts, axis=0)


def _unpack_small(packed, like):
    out, row = [], 0
    for a in like:
        size = math.prod(a.shape)
        rows = -(-size // PACK_TILE) * 8
        out.append(packed[row:row + rows].reshape(-1)[:size].reshape(a.shape))
        row += rows
    return out


def _small_update(gathered, w, m, v):
    rows = gathered.shape[1]

    def body(g_ref, w_ref, m_ref, v_ref, tot_ref, d_ref, nm_ref, nv_ref):
        total = g_ref[0].astype(F32)
        for dev in range(1, 8):
            total = total + g_ref[dev].astype(F32)
        tot_ref[...] = total
        d_ref[...], nm_ref[...], nv_ref[...] = _adamw_math(w_ref[...], total, m_ref[...], v_ref[...])

    return pl.pallas_call(
        body, name="small_update",
        in_specs=[VMEM_SPEC] * 4, out_specs=[VMEM_SPEC] * 4,
        out_shape=[jax.ShapeDtypeStruct((rows, 128), F32)] * 4,
        compiler_params=pltpu.CompilerParams(vmem_limit_bytes=24 * MIB),
    )(gathered, w, m, v)


def _halves(a):
    return a.reshape(a.shape[:-2] + (2, a.shape[-2] // 2, a.shape[-1]))


def _whole(a):
    return a.reshape(a.shape[:-3] + (2 * a.shape[-2], a.shape[-1]))


def kernel(x, p, norm1_gain, w_in, gmlp_v_gain, w_spatial, b_spatial, attn_sinks, rel_bias_table, w_out, norm2_gain, w_ff1, w_ff2, w_ple_proj, w_ple_gate, final_gain, loss_target, m_norm1_gain, m_w_in, m_gmlp_v_gain, m_w_spatial, m_b_spatial, m_attn_sinks, m_rel_bias_table, m_w_out, m_norm2_gain, m_w_ff1, m_w_ff2, m_w_ple_proj, m_w_ple_gate, m_final_gain, v_norm1_gain, v_w_in, v_gmlp_v_gain, v_w_spatial, v_b_spatial, v_attn_sinks, v_rel_bias_table, v_w_out, v_norm2_gain, v_w_ff1, v_w_ff2, v_w_ple_proj, v_w_ple_gate, v_final_gain):
    given = dict(locals())
    small = {n: given[n] for n in SMALL_NAMES}
    chip = 2 * lax.axis_index("x") + lax.axis_index("y")
    place = jnp.stack([chip, lax.axis_index("c")]).astype(jnp.int32)

    big_names = ("w_in", "w_out", "w_ff1", "w_ff2", "w_ple_proj", "w_ple_gate")
    shards = {n: given[n][0] for n in big_names}
    travel = dict(shards, w_in=jnp.transpose(shards["w_in"]))
    rest = [n for n in big_names if n != "w_in"]
    cast, gathered = _cast_shards_beside_gather([travel[n] for n in rest], place[:1],
                                                [_cast_shard(travel["w_in"], place[:1])])
    bufs = dict(zip(rest + ["w_in"], cast + gathered))
    dx, landed, exchange_in, small_grads, sq = _step(x[0], p[0, 0], loss_target[0], small, bufs, place)

    out_grad, out_delta, out_m, out_v = {}, {}, {}, {}

    def update(n, g, exchange=None):
        to = jnp.transpose if n == "w_in" else (lambda a: a)
        (delta, new_m, new_v, g_out), got = _adamw(to(shards[n]), g, to(given["m_" + n][0]), to(given["v_" + n][0]),
                                                   exchange)
        out_grad[n], out_delta[n], out_m[n], out_v[n] = [to(a)[None] for a in (g_out, delta, new_m, new_v)]
        return got

    spare = jnp.zeros((8, 128), F32)
    small_packed = _pack_small([small_grads[n] for n in SMALL_NAMES] + [spare]).astype(BF16)
    early = [n for n in big_names if n != "w_in"]
    reduced, (small_gathered, sq_gathered, landed_in) = _sibling_allgather(
        [_chip_sum(landed[n], place) for n in early], _Both(_Both(_GatherAll(small_packed), _GatherAll(sq)), exchange_in))
    for n, r in zip(early, reduced):
        update(n, _whole(r))
    (reduced_in,), _ = _sibling_allgather([_chip_sum(landed_in, place)], _Nothing())
    update("w_in", _whole(reduced_in))

    like = [given[n] for n in SMALL_NAMES] + [spare]
    packed = _small_update(small_gathered, *[_pack_small([given[pre + n] for n in SMALL_NAMES] + [spare])
                                             for pre in ("", "m_", "v_")])
    for res, out in zip(packed, (out_grad, out_delta, out_m, out_v)):
        out.update(zip(SMALL_NAMES, _unpack_small(res, like)))
    loss = 0.5 * jnp.sum(sq_gathered[:, 0, 0]) / D

    order = ("norm1_gain", "w_in", "gmlp_v_gain", "w_spatial", "b_spatial", "attn_sinks", "rel_bias_table", "w_out",
             "norm2_gain", "w_ff1", "w_ff2", "w_ple_proj", "w_ple_gate", "final_gain")
    return (loss, dx[None], *[out_grad[n] for n in order], *[out_delta[n] for n in order],
            *[out_m[n] for n in order], *[out_v[n] for n in order])
```

```python
import functools
import math

import jax
import jax.numpy as jnp
from jax import lax
from jax.experimental import pallas as pl
from jax.experimental.pallas import tpu as pltpu

S = 2048
D = 1024
D_IN = 1792
D_FF = 4096
PLE = 256
N_CHIP = 4
N_GROUP = 4
CHUNK = 128
N_HEAD = 8
N_BLOCK = S // CHUNK
N_BUCKET = 32
EPS = 1e-6
NEG_INF = -1e30
QK_SCALE = 0.125
GELU_C = math.sqrt(2.0 / math.pi)

ADAM_LR = 0.001
ADAM_B1 = 0.9
ADAM_B2 = 0.999
ADAM_EPS = 1e-08
ADAM_WD = 0.01
ADAM_STEP = 10

F32 = jnp.float32
BF16 = jnp.bfloat16
MIB = 1024 * 1024
MESH = pl.DeviceIdType.MESH

NT = (((1,), (1,)), ((), ()))
TN = (((0,), (0,)), ((), ()))


def _dot(a, b):
    return jnp.dot(a, b, preferred_element_type=F32)


def _dot_nt(a, b):
    return lax.dot_general(a, b, NT, preferred_element_type=F32)


def _dot_tn(a, b):
    return lax.dot_general(a, b, TN, preferred_element_type=F32)


def _params(vmem_mib, n_axes=1):
    return pltpu.CompilerParams(dimension_semantics=("arbitrary",) * n_axes, vmem_limit_bytes=vmem_mib * MIB)


def _rms_scale(v):
    return lax.rsqrt(jnp.mean(v * v, axis=-1, keepdims=True) + EPS)


def _rms_bwd(dy_gain, xhat, r):
    return r * (dy_gain - xhat * jnp.mean(dy_gain * xhat, axis=-1, keepdims=True))


class _Gather:
    def __init__(self, bufs):
        self.operands = list(bufs)
        self.n_out = len(self.operands)
        self.out_shape = [_hbm_like(b) for b in bufs]
        self.aliases = {w: w for w in range(self.n_out)}
        self.sems = _gather_sems(self.n_out)

    def start(self, ins, outs, sems):
        _gather_start(outs, *sems)

    def finish(self, ins, outs, sems):
        _gather_finish(outs, *sems)


class _RelayGather(_Gather):
    TOP, BOTTOM = 6, 7
    DIAGONAL_PASSED = 5
    MIDDLE_AT, LATE_AT = (5, 8), (7, 8)

    def __init__(self, bufs):
        super().__init__(bufs)
        self.sems = [pltpu.SemaphoreType.DMA((self.n_out, 8)), pltpu.SemaphoreType.DMA((self.n_out, 8))]

    def _copies(self, bufs, send_sems, recv_sems):
        x, y, c, others = _mesh_place()
        me = 2 * x + y
        idx = [2 * ox + oy for ox, oy in others]
        sibling = (x, y, 1 - c)
        direct, passed, relayed = [], [], []
        for w, buf in enumerate(bufs):
            rows = buf.shape[2] // 2
            upper, lower = pl.ds(0, rows), pl.ds(rows, rows)
            for k in (0, 1):
                mine = buf.at[me, c]
                direct.append((_remote(mine, mine, send_sems.at[w, k], recv_sems.at[w, k], (*others[k], c)),
                               buf.at[idx[k], c], w, k))
            for k in (0, 1, 2):
                here = buf.at[idx[k], c]
                passed.append((_remote(here, here, send_sems.at[w, 3 + k], recv_sems.at[w, 3 + k], sibling),
                               buf.at[idx[k], 1 - c], w, 3 + k))
            from_x, from_y = buf.at[idx[0], c, upper], buf.at[idx[1], c, lower]
            relayed.append((_remote(from_x, from_x, send_sems.at[w, self.TOP], recv_sems.at[w, self.TOP],
                                    (*others[1], c)), buf.at[idx[2], c, upper], w, self.TOP))
            relayed.append((_remote(from_y, from_y, send_sems.at[w, self.BOTTOM], recv_sems.at[w, self.BOTTOM],
                                    (*others[0], c)), buf.at[idx[2], c, lower], w, self.BOTTOM))
        return direct, passed, relayed

    @staticmethod
    def _landed(piece, send_sems, recv_sems, w, col):
        x, y, c, _ = _mesh_place()
        _remote(piece, piece, send_sems.at[w, col], recv_sems.at[w, col], (x, y, c)).wait_recv()

    def start(self, ins, outs, sems):
        for cp, _, _, _ in self._copies(outs, *sems)[0]:
            cp.start()

    def middle(self, ins, outs, sems):
        direct, passed, relayed = self._copies(outs, *sems)
        for _, piece, w, col in direct:
            self._landed(piece, *sems, w, col)
        for cp, _, _, col in passed:
            if col != self.DIAGONAL_PASSED:
                cp.start()
        for cp, _, _, _ in relayed:
            cp.start()

    def late(self, ins, outs, sems):
        direct, passed, relayed = self._copies(outs, *sems)
        for _, piece, w, col in relayed:
            self._landed(piece, *sems, w, col)
        for cp, _, _, col in passed:
            if col == self.DIAGONAL_PASSED:
                cp.start()

    def finish(self, ins, outs, sems):
        direct, passed, relayed = self._copies(outs, *sems)
        for _, piece, w, col in passed:
            self._landed(piece, *sems, w, col)
        for cp, _, _, _ in direct + passed + relayed:
            cp.wait_send()


class _ChipExchange:
    def __init__(self, sums, landing):
        self.n_out = len(landing)
        self.operands = list(sums) + list(landing)
        self.out_shape = [_hbm_like(b) for b in landing]
        self.aliases = {self.n_out + w: w for w in range(self.n_out)}
        self.sems = _chip_exchange_sems(self.n_out)

    def start(self, ins, outs, sems):
        _chip_exchange_start(ins[:self.n_out], outs, *sems)

    def finish(self, ins, outs, sems):
        _chip_exchange_finish(ins[:self.n_out], outs, *sems)


class _GatherAll:
    def __init__(self, packed):
        self.operands = [packed]
        self.n_out = 1
        self.out_shape = [_hbm_like(packed, (8,) + packed.shape)]
        self.aliases = {}
        self.sems = [pltpu.SemaphoreType.DMA((8,)), pltpu.SemaphoreType.DMA((8,))]

    def _copies(self, ins, outs, sems):
        x, y, c, _ = _mesh_place()
        me = 4 * x + 2 * y + c
        send_sems, recv_sems = sems
        copies = []
        for k in range(1, 8):
            peer = (1 - x if k // 4 else x, 1 - y if (k // 2) % 2 else y, 1 - c if k % 2 else c)
            src = 4 * peer[0] + 2 * peer[1] + peer[2]
            copies.append((_remote(ins[0], outs[0].at[me], send_sems.at[k], recv_sems.at[k], peer), outs[0].at[src]))
        own = pltpu.make_async_copy(ins[0], outs[0].at[me], send_sems.at[0])
        return own, copies

    def start(self, ins, outs, sems):
        own, copies = self._copies(ins, outs, sems)
        own.start()
        for cp, _ in copies:
            cp.start()

    def finish(self, ins, outs, sems):
        own, copies = self._copies(ins, outs, sems)
        x, y, c, _ = _mesh_place()
        for k, (cp, landed) in enumerate(copies):
            _remote(landed, landed, sems[0].at[k + 1], sems[1].at[k + 1], (x, y, c)).wait_recv()
        for cp, _ in copies:
            cp.wait_send()
        own.wait()


class _Nothing:
    operands, n_out, out_shape, aliases, sems = [], 0, [], {}, []

    def start(self, ins, outs, sems):
        pass

    def finish(self, ins, outs, sems):
        pass


class _Both:
    def __init__(self, a, b):
        self.a, self.b = a, b
        self.operands = a.operands + b.operands
        self.n_out = a.n_out + b.n_out
        self.out_shape = a.out_shape + b.out_shape
        self.aliases = dict(a.aliases)
        self.aliases.update({len(a.operands) + i: a.n_out + o for i, o in b.aliases.items()})
        self.sems = a.sems + b.sems

    def _split(self, ins, outs, sems):
        ka, na, sa = len(self.a.operands), self.a.n_out, len(self.a.sems)
        return (ins[:ka], outs[:na], sems[:sa]), (ins[ka:], outs[na:], sems[sa:])

    def start(self, ins, outs, sems):
        for ex, args in zip((self.a, self.b), self._split(ins, outs, sems)):
            ex.start(*args)

    def finish(self, ins, outs, sems):
        for ex, args in zip((self.a, self.b), self._split(ins, outs, sems)):
            ex.finish(*args)


class _SiblingExchange:
    def __init__(self, grads):
        self.operands = list(grads)
        self.n_out = len(self.operands)
        self.out_shape = [_hbm_like(g, (N_CHIP,) + g.shape[2:]) for g in grads]
        self.aliases = {}
        self.sems = _sibling_exchange_sems(self.n_out)

    def start(self, ins, outs, sems):
        _sibling_exchange_start(ins, outs, *sems)

    def finish(self, ins, outs, sems):
        _sibling_exchange_finish(ins, outs, *sems)


def _call(body, operands, *, grid, in_specs, out_specs, out_shape, name, compiler_params, scratch_shapes=(),
          exchange=None):
    operands = [o if getattr(spec, "memory_space", None) == pltpu.SMEM else _in_hbm(o)
                for o, spec in zip(operands, in_specs)]
    out_shape = [pltpu.HBM(s.shape, s.dtype) for s in out_shape]
    if exchange is None:
        res = pl.pallas_call(body, grid=grid, in_specs=in_specs, out_specs=out_specs, out_shape=out_shape, name=name,
                             scratch_shapes=list(scratch_shapes), compiler_params=compiler_params)(*operands)
        return list(res), []
    n_in, n_out, n_scr = len(in_specs), len(out_specs), len(scratch_shapes)
    k_in, k_out = len(exchange.operands), exchange.n_out

    def fused(*refs):
        ins, refs = refs[:n_in], refs[n_in:]
        ex_ins, refs = refs[:k_in], refs[k_in:]
        outs, refs = refs[:n_out], refs[n_out:]
        ex_outs, refs = refs[:k_out], refs[k_out:]
        scratch, sems = refs[:n_scr], refs[n_scr:]
        ids = [pl.program_id(a) for a in range(len(grid))]
        first = functools.reduce(jnp.logical_and, [i == 0 for i in ids])
        last = functools.reduce(jnp.logical_and, [i == g - 1 for i, g in zip(ids, grid)])

        @pl.when(first)
        def _():
            exchange.start(ex_ins, ex_outs, sems)

        def at_step(numerator, denominator):
            at = (numerator * math.prod(grid)) // denominator
            place = [(at // math.prod(grid[a + 1:])) % grid[a] for a in range(len(grid))]
            return functools.reduce(jnp.logical_and, [i == p for i, p in zip(ids, place)])

        if hasattr(exchange, "middle"):
            @pl.when(at_step(*exchange.MIDDLE_AT))
            def _():
                exchange.middle(ex_ins, ex_outs, sems)

            @pl.when(at_step(*exchange.LATE_AT))
            def _():
                exchange.late(ex_ins, ex_outs, sems)

        body(*ins, *outs, *scratch)

        @pl.when(last)
        def _():
            exchange.finish(ex_ins, ex_outs, sems)

    res = pl.pallas_call(
        fused, grid=grid, name=name,
        in_specs=list(in_specs) + [HBM_SPEC] * k_in, out_specs=list(out_specs) + [HBM_SPEC] * k_out,
        out_shape=list(out_shape) + exchange.out_shape,
        input_output_aliases={n_in + i: n_out + o for i, o in exchange.aliases.items()},
        scratch_shapes=list(scratch_shapes) + exchange.sems, compiler_params=compiler_params,
    )(*operands, *[_in_hbm(o) for o in exchange.operands])
    return list(res[:n_out]), list(res[n_out:])


def _in_hbm(a):
    return pltpu.with_memory_space_constraint(a, pltpu.HBM)


def _row_tile(h):
    return max(t for t in range(16, 513, 16) if h % t == 0)


def _cast_shard(a, chip):
    rows, cols = a.shape
    h = rows // 2
    tr = _row_tile(h)

    def body(chip_ref, a_ref, o_ref):
        o_ref[0, 0] = a_ref[0].astype(BF16)

    return pl.pallas_call(
        body, name="cast_shard",
        grid_spec=pltpu.PrefetchScalarGridSpec(
            num_scalar_prefetch=1, grid=(2, h // tr),
            in_specs=[pl.BlockSpec((1, tr, cols), lambda s, r, chip_ref: (s, r, 0))],
            out_specs=pl.BlockSpec((1, 1, tr, cols), lambda s, r, chip_ref: (chip_ref[0], s, r, 0))),
        out_shape=pltpu.HBM((N_CHIP, 2, h, cols), BF16),
        compiler_params=_params(16, 2),
    )(chip, _in_hbm(a.reshape(2, h, cols)))


def _cast_shards_beside_gather(arrays, chip, gathered):
    n, k = len(arrays), len(gathered)
    shapes = [(a.shape[0] // 2, a.shape[1]) for a in arrays]

    def body(chip_ref, *refs):
        ins, refs = refs[:n], refs[n + k:]
        outs, refs = refs[:n], refs[n:]
        bufs, sems = refs[:k], refs[k:]
        half = pl.program_id(0)

        @pl.when(half == 0)
        def _():
            _gather_start(bufs, *sems)

        for a_ref, o_ref in zip(ins, outs):
            o_ref[0, 0] = a_ref[0].astype(BF16)

        @pl.when(half == 1)
        def _():
            _gather_finish(bufs, *sems)

    res = pl.pallas_call(
        body, name="cast_shards",
        grid_spec=pltpu.PrefetchScalarGridSpec(
            num_scalar_prefetch=1, grid=(2,),
            in_specs=[pl.BlockSpec((1, h, c), lambda s, chip_ref: (s, 0, 0)) for h, c in shapes] + [HBM_SPEC] * k,
            out_specs=[pl.BlockSpec((1, 1, h, c), lambda s, chip_ref: (chip_ref[0], s, 0, 0)) for h, c in shapes]
            + [HBM_SPEC] * k,
            scratch_shapes=_gather_sems(k)),
        out_shape=[pltpu.HBM((N_CHIP, 2, h, c), BF16) for h, c in shapes] + [_hbm_like(b) for b in gathered],
        input_output_aliases={1 + n + i: n + i for i in range(k)},
        compiler_params=_params(32),
    )(chip, *[_in_hbm(a.reshape(2, h, c)) for a, (h, c) in zip(arrays, shapes)], *gathered)
    return list(res[:n]), list(res[n:])


def _in_proj(x, gain1, w_in_t, exchange=None):
    tm = 256

    def body(x_ref, g_ref, w_ref, z_ref, hn_ref):
        xv = x_ref[...]
        hn = (xv * _rms_scale(xv) * g_ref[...]).astype(BF16)
        hn_ref[...] = hn
        z_ref[...] = _dot_nt(hn, w_ref[...])

    return _call(
        body, (x, gain1, w_in_t), grid=(S // tm,), name="in_proj",
        in_specs=[pl.BlockSpec((tm, D), lambda i: (i, 0)), pl.BlockSpec((1, D), lambda i: (0, 0)),
                  pl.BlockSpec((D_IN, D), lambda i: (0, 0))],
        out_specs=[pl.BlockSpec((tm, D_IN), lambda i: (i, 0)), pl.BlockSpec((tm, D), lambda i: (i, 0))],
        out_shape=[jax.ShapeDtypeStruct((S, D_IN), F32), jax.ShapeDtypeStruct((S, D), BF16)],
        compiler_params=_params(40), exchange=exchange)


def _gelu_parts(v):
    t = jnp.tanh(GELU_C * (v + 0.044715 * (v * v * v)))
    cdf = 0.5 * (1.0 + t)
    return cdf, t


def _band_mask(n):
    a = lax.broadcasted_iota(jnp.int32, (CHUNK, 2 * CHUNK), 0)
    j = lax.broadcasted_iota(jnp.int32, (CHUNK, 2 * CHUNK), 1)
    dist = CHUNK + a - j
    valid = (dist >= 0) & (dist < CHUNK)
    return valid & ((n > 0) | (j >= CHUNK))


def _fill_bias(bucket_ref, table_ref, bias_ref):
    bucket = bucket_ref[...]
    for h in range(N_HEAD):
        acc = jnp.zeros((CHUNK, 2 * CHUNK), F32)
        for b in range(N_BUCKET):
            acc = jnp.where(bucket == b, table_ref[b, h], acc)
        bias_ref[h] = acc


def _fill_tril(ws_ref, wt_ref, wtt_ref=None):
    r = lax.broadcasted_iota(jnp.int32, (CHUNK, CHUNK), 0)
    c = lax.broadcasted_iota(jnp.int32, (CHUNK, CHUNK), 1)
    for g in range(N_GROUP):
        w = jnp.where(c <= r, ws_ref[g], 0.0)
        wt_ref[g] = w.astype(BF16)
        if wtt_ref is not None:
            wtt_ref[g] = w.T.astype(BF16)


def _kv_layouts(kv_prev, kv_cur):
    both = jnp.concatenate([kv_prev, kv_cur], axis=0)
    k = both[:, :128]
    v = both[:, 128:]
    return (k.astype(BF16), pltpu.roll(k, 64, axis=1).astype(BF16),
            v.astype(BF16), pltpu.roll(v, 64, axis=1).astype(BF16))


def _head_place(h):
    pair, pos, kvh = h // 2, h % 2, h // 4
    return pair, pos, kvh == pos


def _softmax_sink(qm, k_use, bias_h, sink, valid):
    s = _dot_nt(qm, k_use) * QK_SCALE + bias_h
    s = jnp.where(valid, s, NEG_INF)
    m = jnp.maximum(jnp.max(s, axis=-1, keepdims=True), sink)
    e = jnp.exp(s - m)
    es = jnp.exp(sink - m)
    inv = 1.0 / (jnp.sum(e, axis=-1, keepdims=True) + es)
    return e * inv, es * inv


def _mixer_fwd(z, v_gain, w_spatial, b_spatial_t, sinks, rel_table, bucket, exchange=None):
    def body(z_ref, kvp_ref, gain_ref, ws_ref, bt_ref, sink_ref, table_ref, bucket_ref, out_ref, probs_ref, probs_t_ref,
             share_ref, guv_ref, dgelu_ref, bias_ref, wt_ref):
        n = pl.program_id(0)

        @pl.when(n == 0)
        def _():
            _fill_bias(bucket_ref, table_ref, bias_ref)
            _fill_tril(ws_ref, wt_ref)

        zuv = z_ref[:, :1024]
        cdf, t = _gelu_parts(zuv)
        guv = zuv * cdf
        guv_ref[...] = guv
        dgelu_ref[...] = cdf + zuv * (0.5 * (1.0 - t * t)) * (GELU_C * (1.0 + 3.0 * 0.044715 * (zuv * zuv)))
        for g in range(N_GROUP):
            vg = guv[:, 512 + 128 * g:512 + 128 * (g + 1)]
            vn = vg * _rms_scale(vg) * gain_ref[:, 128 * g:128 * (g + 1)]
            sv = _dot(wt_ref[g], vn.astype(BF16)) + bt_ref[:, g:g + 1]
            out_ref[:, 128 * g:128 * (g + 1)] = (guv[:, 128 * g:128 * (g + 1)] * sv).astype(BF16)

        k_same, k_swap, v_same, v_swap = _kv_layouts(kvp_ref[...], z_ref[:, 1536:1792])
        valid = _band_mask(n)
        lane = lax.broadcasted_iota(jnp.int32, (1, 128), 1)
        lane_half = lane // 64
        shares = jnp.zeros((CHUNK, 128), F32)
        for pair in range(N_HEAD // 2):
            qq = z_ref[:, 1024 + 128 * pair:1024 + 128 * (pair + 1)]
            acc = jnp.zeros((CHUNK, 128), F32)
            for pos in range(2):
                h = 2 * pair + pos
                _, _, same = _head_place(h)
                qm = jnp.where(lane_half == pos, qq, 0.0).astype(BF16)
                p, p_sink = _softmax_sink(qm, k_same if same else k_swap, bias_ref[h], sink_ref[h], valid)
                pb = p.astype(BF16)
                probs_ref[0, h] = pb
                probs_t_ref[0, h] = p.T.astype(BF16)
                shares = jnp.where(lane == h, p_sink, shares)
                vm = jnp.where(lane_half == pos, v_same if same else v_swap, jnp.zeros((), BF16))
                acc = acc + _dot(pb, vm)
            out_ref[:, 512 + 128 * pair:512 + 128 * (pair + 1)] = acc.astype(BF16)
        share_ref[...] = shares

    return _call(
        body, (z, z, v_gain, w_spatial, b_spatial_t, sinks, rel_table, bucket), grid=(N_BLOCK,), name="mixer_fwd",
        in_specs=[pl.BlockSpec((CHUNK, D_IN), lambda n: (n, 0)),
                  pl.BlockSpec((CHUNK, 256), lambda n: (jnp.maximum(n - 1, 0), 6)),
                  pl.BlockSpec((1, 512), lambda n: (0, 0)),
                  pl.BlockSpec((N_GROUP, CHUNK, CHUNK), lambda n: (0, 0, 0)),
                  pl.BlockSpec((CHUNK, N_GROUP), lambda n: (0, 0)),
                  pl.BlockSpec(memory_space=pltpu.SMEM),
                  pl.BlockSpec(memory_space=pltpu.SMEM),
                  pl.BlockSpec((CHUNK, 2 * CHUNK), lambda n: (0, 0))],
        out_specs=[pl.BlockSpec((CHUNK, D), lambda n: (n, 0)),
                   pl.BlockSpec((1, N_HEAD, CHUNK, 2 * CHUNK), lambda n: (n, 0, 0, 0)),
                   pl.BlockSpec((1, N_HEAD, 2 * CHUNK, CHUNK), lambda n: (n, 0, 0, 0)),
                   pl.BlockSpec((CHUNK, 128), lambda n: (n, 0)),
                   pl.BlockSpec((CHUNK, 1024), lambda n: (n, 0)), pl.BlockSpec((CHUNK, 1024), lambda n: (n, 0))],
        out_shape=[jax.ShapeDtypeStruct((S, D), BF16), jax.ShapeDtypeStruct((N_BLOCK, N_HEAD, CHUNK, 2 * CHUNK), BF16),
                   jax.ShapeDtypeStruct((N_BLOCK, N_HEAD, 2 * CHUNK, CHUNK), BF16), jax.ShapeDtypeStruct((S, 128), F32),
                   jax.ShapeDtypeStruct((S, 1024), F32), jax.ShapeDtypeStruct((S, 1024), F32)],
        scratch_shapes=[pltpu.VMEM((N_HEAD, CHUNK, 2 * CHUNK), F32), pltpu.VMEM((N_GROUP, CHUNK, CHUNK), BF16)],
        compiler_params=_params(32), exchange=exchange)


def _out_proj(x, mix, w_out, gain2, exchange=None):
    tm = 256

    def body(x_ref, mix_ref, w_ref, g_ref, h1_ref, hn_ref, hnt_ref):
        h1 = x_ref[...] + _dot(mix_ref[...], w_ref[...])
        h1_ref[...] = h1
        hn = h1 * _rms_scale(h1) * g_ref[...]
        hn_ref[...] = hn.astype(BF16)
        hnt_ref[...] = hn.T.astype(BF16)

    return _call(
        body, (x, mix, w_out, gain2), grid=(S // tm,), name="out_proj",
        in_specs=[pl.BlockSpec((tm, D), lambda i: (i, 0)), pl.BlockSpec((tm, D), lambda i: (i, 0)),
                  pl.BlockSpec((D, D), lambda i: (0, 0)), pl.BlockSpec((1, D), lambda i: (0, 0))],
        out_specs=[pl.BlockSpec((tm, D), lambda i: (i, 0)), pl.BlockSpec((tm, D), lambda i: (i, 0)),
                   pl.BlockSpec((D, tm), lambda i: (0, i))],
        out_shape=[jax.ShapeDtypeStruct((S, D), F32), jax.ShapeDtypeStruct((S, D), BF16),
                   jax.ShapeDtypeStruct((D, S), BF16)],
        compiler_params=_params(32), exchange=exchange)


def _ffn_up(hn2, w_ff1, exchange=None):
    tm = 512
    nj = D_FF // 1024

    def body(hn_ref, w1_ref, r_ref, a_ref, at_ref):
        r = jnp.maximum(_dot(hn_ref[...], w1_ref[0]), 0.0)
        r_ref[...] = r.astype(BF16)
        a = r * r
        a_ref[...] = a.astype(BF16)
        at_ref[...] = a.T.astype(BF16)

    return _call(
        body, (hn2, w_ff1), grid=(nj, S // tm), name="ffn_up",
        in_specs=[pl.BlockSpec((tm, D), lambda j, i: (i, 0)), pl.BlockSpec((1, D, 1024), lambda j, i: (j, 0, 0))],
        out_specs=[pl.BlockSpec((tm, 1024), lambda j, i: (i, j)), pl.BlockSpec((tm, 1024), lambda j, i: (i, j)),
                   pl.BlockSpec((1024, tm), lambda j, i: (j, i))],
        out_shape=[jax.ShapeDtypeStruct((S, D_FF), BF16), jax.ShapeDtypeStruct((S, D_FF), BF16),
                   jax.ShapeDtypeStruct((D_FF, S), BF16)],
        compiler_params=_params(40, 2), exchange=exchange)


def _ffn_down(h1, a, w_ff2, exchange=None):
    tm = 512
    nj = D_FF // 1024

    def body(h1_ref, a_ref, w2_ref, h2_ref):
        h2_ref[...] = h1_ref[...] + _dot(a_ref[...], w2_ref[...].reshape(D_FF, D))

    return _call(
        body, (h1, a, w_ff2), grid=(S // tm,), name="ffn_down",
        in_specs=[pl.BlockSpec((tm, D), lambda i: (i, 0)), pl.BlockSpec((tm, D_FF), lambda i: (i, 0)),
                  pl.BlockSpec((nj, 1024, D), lambda i: (0, 0, 0), pipeline_mode=pl.Buffered(1))],
        out_specs=[pl.BlockSpec((tm, D), lambda i: (i, 0))],
        out_shape=[jax.ShapeDtypeStruct((S, D), F32)],
        compiler_params=_params(40), exchange=exchange)


def _tail(h2, p, target, w_gate, w_proj, final_gain):
    tm = 256
    steps = S // tm

    def body(h2_ref, p_ref, t_ref, wg_ref, wp_ref, gf_ref, dh2_ref, dwg_ref, dwp_ref, dgf_ref, loss_ref, dh2b_ref,
             dwp_acc):
        i = pl.program_id(0)
        h2 = h2_ref[...]
        h2b = h2.astype(BF16)
        pb = p_ref[...].astype(BF16)
        gate = jax.nn.sigmoid(_dot(h2b, wg_ref[...]))
        pp = jnp.concatenate([_dot(pb, wp_ref[j]) for j in range(N_CHIP)], axis=1)
        h3 = h2 + gate * pp
        r3 = _rms_scale(h3)
        xhat = h3 * r3
        gf = gf_ref[...]
        err = xhat * gf - t_ref[...]
        dy = err * (1.0 / D)
        dh3 = _rms_bwd(dy * gf, xhat, r3)
        dgp = (dh3 * pp * gate * (1.0 - gate)).astype(BF16)
        dpp = (dh3 * gate).astype(BF16)
        dh2 = dh3 + _dot_nt(dgp, wg_ref[...])
        dh2_ref[...] = dh2
        dh2b_ref[...] = dh2.astype(BF16)
        dwg = _dot_tn(h2b, dgp)
        dwp = _dot_tn(pb, dpp)
        dgf = jnp.sum(dy * xhat, axis=0, keepdims=True)
        sq = jnp.sum(jnp.sum(err * err, axis=1, keepdims=True), axis=0, keepdims=True)

        @pl.when(i == 0)
        def _():
            dwg_ref[...] = dwg
            dwp_acc[...] = dwp
            dgf_ref[...] = dgf
            loss_ref[...] = jnp.broadcast_to(sq, (8, 128))

        @pl.when(i > 0)
        def _():
            dwg_ref[...] += dwg
            dwp_acc[...] += dwp
            dgf_ref[...] += dgf
            loss_ref[...] += jnp.broadcast_to(sq, (8, 128))

        @pl.when(i == steps - 1)
        def _():
            for j in range(N_CHIP):
                dwp_ref[j] = dwp_acc[:, 256 * j:256 * (j + 1)]

    return _call(
        body, (h2, p, target, w_gate, w_proj, final_gain), grid=(steps,), name="tail",
        in_specs=[pl.BlockSpec((tm, D), lambda i: (i, 0)), pl.BlockSpec((tm, PLE), lambda i: (i, 0)),
                  pl.BlockSpec((tm, D), lambda i: (i, 0)), pl.BlockSpec((D, D), lambda i: (0, 0)),
                  pl.BlockSpec((N_CHIP, PLE, 256), lambda i: (0, 0, 0)), pl.BlockSpec((1, D), lambda i: (0, 0))],
        out_specs=[pl.BlockSpec((tm, D), lambda i: (i, 0)), pl.BlockSpec((D, D), lambda i: (0, 0)),
                   pl.BlockSpec((N_CHIP, PLE, 256), lambda i: (0, 0, 0)), pl.BlockSpec((1, D), lambda i: (0, 0)),
                   pl.BlockSpec((8, 128), lambda i: (0, 0)), pl.BlockSpec((tm, D), lambda i: (i, 0))],
        out_shape=[jax.ShapeDtypeStruct((S, D), F32), jax.ShapeDtypeStruct((D, D), F32),
                   jax.ShapeDtypeStruct((N_CHIP, PLE, 256), F32), jax.ShapeDtypeStruct((1, D), F32),
                   jax.ShapeDtypeStruct((8, 128), F32), jax.ShapeDtypeStruct((S, D), BF16)],
        scratch_shapes=[pltpu.VMEM((PLE, D), F32)],
        compiler_params=_params(48))[0]


def _ffn_bwd_down(dh2b, r, a_t, w_ff2, exchange=None):
    tm = S
    rows = 1024
    nj = D_FF // 1024

    def body(dh2_ref, r_ref, at_ref, w2_ref, df_ref, dw2_ref):
        for k in range(tm // rows):
            part = pl.ds(k * rows, rows)
            da = _dot_nt(dh2_ref[part, :], w2_ref[0])
            df_ref[part, :] = (da * (2.0 * r_ref[part, :].astype(F32))).astype(BF16)
        dw2_ref[0] = _dot(at_ref[...], dh2_ref[...])

    return _call(
        body, (dh2b, r, a_t, w_ff2), grid=(nj, S // tm), name="ffn_bwd_down",
        in_specs=[pl.BlockSpec((tm, D), lambda j, i: (i, 0), pipeline_mode=pl.Buffered(1)),
                  pl.BlockSpec((tm, 1024), lambda j, i: (i, j)),
                  pl.BlockSpec((1024, tm), lambda j, i: (j, i)), pl.BlockSpec((1, 1024, D), lambda j, i: (j, 0, 0))],
        out_specs=[pl.BlockSpec((tm, 1024), lambda j, i: (i, j)), pl.BlockSpec((1, 1024, D), lambda j, i: (j, 0, 0))],
        out_shape=[jax.ShapeDtypeStruct((S, D_FF), BF16), jax.ShapeDtypeStruct((nj, 1024, D), F32)],
        compiler_params=_params(56, 2), exchange=exchange)


def _ffn_bwd_up(df, hn2_t, exchange=None):
    tm = 2048
    nj = D_FF // 1024

    def body(df_ref, hnt_ref, dw1_ref):
        i = pl.program_id(1)
        dw1 = _dot(hnt_ref[...], df_ref[...])

        @pl.when(i == 0)
        def _():
            dw1_ref[0] = dw1

        @pl.when(i > 0)
        def _():
            dw1_ref[0] += dw1

    return _call(
        body, (df, hn2_t), grid=(nj, S // tm), name="ffn_bwd_up",
        in_specs=[pl.BlockSpec((tm, 1024), lambda j, i: (i, j)), pl.BlockSpec((D, tm), lambda j, i: (0, i))],
        out_specs=[pl.BlockSpec((1, D, 1024), lambda j, i: (j, 0, 0))],
        out_shape=[jax.ShapeDtypeStruct((nj, D, 1024), F32)],
        compiler_params=_params(40, 2), exchange=exchange)


def _ffn_bwd_input(df, w_ff1, dh2, h1, gain2, mix, w_out, exchange=None):
    tm = 512
    nj = D_FF // 1024
    steps = S // tm

    def body(df_ref, w1_ref, dh2_ref, h1_ref, g_ref, mix_ref, wo_ref, dh1_ref, dmix_ref, dwo_ref, dg_ref, acc_ref):
        i = pl.program_id(0)
        j = pl.program_id(1)
        part = _dot_nt(df_ref[...], w1_ref[j])

        @pl.when(j == 0)
        def _():
            acc_ref[...] = part

        @pl.when(j > 0)
        def _():
            acc_ref[...] += part

        @pl.when(j == nj - 1)
        def _():
            dhn = acc_ref[...]
            h1 = h1_ref[...]
            r2 = _rms_scale(h1)
            xhat = h1 * r2
            dh1 = dh2_ref[...] + _rms_bwd(dhn * g_ref[...], xhat, r2)
            dh1_ref[...] = dh1
            dh1b = dh1.astype(BF16)
            dmix_ref[...] = _dot_nt(dh1b, wo_ref[...])
            dwo = _dot_tn(mix_ref[...], dh1b)
            dg = jnp.sum(dhn * xhat, axis=0, keepdims=True)

            @pl.when(i == 0)
            def _():
                dwo_ref[...] = dwo
                dg_ref[...] = dg

            @pl.when(i > 0)
            def _():
                dwo_ref[...] += dwo
                dg_ref[...] += dg

    return _call(
        body, (df, w_ff1, dh2, h1, gain2, mix, w_out), grid=(steps, nj), name="ffn_bwd_input",
        in_specs=[pl.BlockSpec((tm, 1024), lambda i, j: (i, j)),
                  pl.BlockSpec((nj, D, 1024), lambda i, j: (0, 0, 0), pipeline_mode=pl.Buffered(1)),
                  pl.BlockSpec((tm, D), lambda i, j: (i, 0)), pl.BlockSpec((tm, D), lambda i, j: (i, 0)),
                  pl.BlockSpec((1, D), lambda i, j: (0, 0)), pl.BlockSpec((tm, D), lambda i, j: (i, 0)),
                  pl.BlockSpec((D, D), lambda i, j: (0, 0), pipeline_mode=pl.Buffered(1))],
        out_specs=[pl.BlockSpec((tm, D), lambda i, j: (i, 0)), pl.BlockSpec((tm, D), lambda i, j: (i, 0)),
                   pl.BlockSpec((D, D), lambda i, j: (0, 0)), pl.BlockSpec((1, D), lambda i, j: (0, 0))],
        out_shape=[jax.ShapeDtypeStruct((S, D), F32), jax.ShapeDtypeStruct((S, D), F32),
                   jax.ShapeDtypeStruct((D, D), F32), jax.ShapeDtypeStruct((1, D), F32)],
        scratch_shapes=[pltpu.VMEM((tm, D), F32)],
        compiler_params=_params(56, 2), exchange=exchange)


IN_GROUP = 8


def _mixer_bwd(z, dmix, v_gain, w_spatial, b_spatial_t, saved, bucket, hn1, exchange=None):
    def body(z_ref, kvp_ref, dm_ref, gain_ref, ws_ref, bt_ref, probs_ref, probs_t_ref, share_ref, guv_ref, dgelu_ref,
             bucket_ref, hn_ref,
             dz_ref, dws_ref, db_ref, dgain_ref, dsink_ref, drel_ref, dwin_ref,
             wt_ref, wtt_ref, dbias_ref, dsv_ref, carry_ref):
        n = pl.program_id(0)

        @pl.when(n == 0)
        def _():
            _fill_tril(ws_ref, wt_ref, wtt_ref)
            dwin_ref[...] = jnp.zeros_like(dwin_ref)
            dbias_ref[...] = jnp.zeros_like(dbias_ref)
            dsv_ref[...] = jnp.zeros_like(dsv_ref)
            dws_ref[...] = jnp.zeros_like(dws_ref)
            dgain_ref[...] = jnp.zeros_like(dgain_ref)
            dsink_ref[...] = jnp.zeros_like(dsink_ref)

        rows = pl.ds(pl.multiple_of(n * CHUNK, CHUNK), CHUNK)

        guv = guv_ref[...]
        dgelu = dgelu_ref[...]
        for g in range(N_GROUP):
            lo, hi = 128 * g, 128 * (g + 1)
            u = guv[:, lo:hi]
            vg = guv[:, 512 + lo:512 + hi]
            rr = _rms_scale(vg)
            vhat = vg * rr
            gain = gain_ref[:, lo:hi]
            vnb = (vhat * gain).astype(BF16)
            sv = _dot(wt_ref[g], vnb) + bt_ref[:, g:g + 1]
            da = dm_ref[:, lo:hi]
            dsv = da * u
            dsvb = dsv.astype(BF16)
            dsv_ref[g] += dsv
            dws_ref[g] += _dot_nt(dsvb, vnb)
            dvn = _dot(wtt_ref[g], dsvb)
            dgain_ref[:, lo:hi] += jnp.sum(dvn * vhat, axis=0, keepdims=True)
            dvg = _rms_bwd(dvn * gain, vhat, rr)
            dz_ref[rows, lo:hi] = (da * sv * dgelu[:, lo:hi]).astype(BF16)
            dz_ref[rows, 512 + lo:512 + hi] = (dvg * dgelu[:, 512 + lo:512 + hi]).astype(BF16)

        k_same, k_swap, v_same, v_swap = _kv_layouts(kvp_ref[...], z_ref[:, 1536:1792])
        lane_half = lax.broadcasted_iota(jnp.int32, (1, 128), 1) // 64
        zero = jnp.zeros((2 * CHUNK, 128), F32)
        dk_same, dk_swap, dv_same, dv_swap = zero, zero, zero, zero
        for pair in range(N_HEAD // 2):
            cols = slice(1024 + 128 * pair, 1024 + 128 * (pair + 1))
            qq = z_ref[:, cols]
            do_pair = dm_ref[:, 512 + 128 * pair:512 + 128 * (pair + 1)]
            dq = jnp.zeros((CHUNK, 128), F32)
            for pos in range(2):
                h = 2 * pair + pos
                _, _, same = _head_place(h)
                on_half = lane_half == pos
                qm = jnp.where(on_half, qq, 0.0).astype(BF16)
                k_use = k_same if same else k_swap
                v_use = v_same if same else v_swap
                pb = probs_ref[0, h]
                p = pb.astype(F32)
                p_sink = share_ref[:, h:h + 1]
                dom = jnp.where(on_half, do_pair, 0.0).astype(BF16)
                dp = _dot_nt(dom, v_use)
                dsum = jnp.sum(p * dp, axis=-1, keepdims=True)
                ds = p * (dp - dsum)
                dbias_ref[h] += ds
                dsink_ref[h:h + 1, :] += jnp.broadcast_to(jnp.sum(-p_sink * dsum, axis=0, keepdims=True), (1, 128))
                dsb = ds.astype(BF16)
                dq = dq + jnp.where(on_half, _dot(dsb, k_use), 0.0)
                dk_h = _dot_tn(dsb, qm)
                dv_h = _dot(probs_t_ref[0, h], dom)
                if same:
                    dk_same, dv_same = dk_same + dk_h, dv_same + dv_h
                else:
                    dk_swap, dv_swap = dk_swap + dk_h, dv_swap + dv_h
            dz_ref[rows, cols] = (dq * QK_SCALE).astype(BF16)
        dk = (dk_same + pltpu.roll(dk_swap, 64, axis=1)) * QK_SCALE
        dv = dv_same + pltpu.roll(dv_swap, 64, axis=1)
        dkv = jnp.concatenate([dk, dv], axis=1)

        @pl.when(n > 0)
        def _():
            prev_rows = pl.ds(pl.multiple_of((n - 1) * CHUNK, CHUNK), CHUNK)
            dz_ref[prev_rows, 1536:1792] = (carry_ref[...] + dkv[:CHUNK]).astype(BF16)

        carry_ref[...] = dkv[CHUNK:]

        @pl.when((n > 0) & (n % IN_GROUP == 0))
        def _():
            done = pl.ds(pl.multiple_of((n - IN_GROUP) * CHUNK, IN_GROUP * CHUNK), IN_GROUP * CHUNK)
            dwin_ref[...] += _dot_tn(dz_ref[done, :], hn_ref[...])

        @pl.when(n == N_BLOCK - 1)
        def _():
            dz_ref[rows, 1536:1792] = dkv[CHUNK:].astype(BF16)
            last = pl.ds((N_BLOCK - IN_GROUP) * CHUNK, IN_GROUP * CHUNK)
            dwin_ref[...] += _dot_tn(dz_ref[last, :], hn_ref[...])
            r = lax.broadcasted_iota(jnp.int32, (CHUNK, CHUNK), 0)
            c = lax.broadcasted_iota(jnp.int32, (CHUNK, CHUNK), 1)
            for g in range(N_GROUP):
                dws_ref[g] = jnp.where(c <= r, dws_ref[g], 0.0)
                db_ref[g] = jnp.sum(dsv_ref[g], axis=1, keepdims=True)
            bucket = bucket_ref[...]
            for h in range(N_HEAD):
                dbh = dbias_ref[h]
                per_bucket = [jnp.sum(jnp.where(bucket == b, dbh, 0.0), axis=0, keepdims=True) for b in range(N_BUCKET)]
                drel_ref[h] = jnp.sum(jnp.concatenate(per_bucket, axis=0), axis=1, keepdims=True)

    def hn_group(n):
        return jnp.where(n == N_BLOCK - 1, N_BLOCK // IN_GROUP - 1, jnp.maximum(n // IN_GROUP - 1, 0))

    return _call(
        body, (z, z, dmix, v_gain, w_spatial, b_spatial_t, *saved, bucket, hn1), grid=(N_BLOCK,),
        name="mixer_bwd",
        in_specs=[pl.BlockSpec((CHUNK, D_IN), lambda n: (n, 0)),
                  pl.BlockSpec((CHUNK, 256), lambda n: (jnp.maximum(n - 1, 0), 6)),
                  pl.BlockSpec((CHUNK, D), lambda n: (n, 0)),
                  pl.BlockSpec((1, 512), lambda n: (0, 0)),
                  pl.BlockSpec((N_GROUP, CHUNK, CHUNK), lambda n: (0, 0, 0)),
                  pl.BlockSpec((CHUNK, N_GROUP), lambda n: (0, 0)),
                  pl.BlockSpec((1, N_HEAD, CHUNK, 2 * CHUNK), lambda n: (n, 0, 0, 0)),
                  pl.BlockSpec((1, N_HEAD, 2 * CHUNK, CHUNK), lambda n: (n, 0, 0, 0)),
                  pl.BlockSpec((CHUNK, 128), lambda n: (n, 0)),
                  pl.BlockSpec((CHUNK, 1024), lambda n: (n, 0)), pl.BlockSpec((CHUNK, 1024), lambda n: (n, 0)),
                  pl.BlockSpec((CHUNK, 2 * CHUNK), lambda n: (0, 0)),
                  pl.BlockSpec((IN_GROUP * CHUNK, D), lambda n: (hn_group(n), 0))],
        out_specs=[pl.BlockSpec((S, D_IN), lambda n: (0, 0)),
                   pl.BlockSpec((N_GROUP, CHUNK, CHUNK), lambda n: (0, 0, 0)),
                   pl.BlockSpec((N_GROUP, CHUNK, 1), lambda n: (0, 0, 0)),
                   pl.BlockSpec((1, 512), lambda n: (0, 0)),
                   pl.BlockSpec((N_HEAD, 128), lambda n: (0, 0)),
                   pl.BlockSpec((N_HEAD, N_BUCKET, 1), lambda n: (0, 0, 0)),
                   pl.BlockSpec((D_IN, D), lambda n: (0, 0))],
        out_shape=[jax.ShapeDtypeStruct((S, D_IN), BF16), jax.ShapeDtypeStruct((N_GROUP, CHUNK, CHUNK), F32),
                   jax.ShapeDtypeStruct((N_GROUP, CHUNK, 1), F32), jax.ShapeDtypeStruct((1, 512), F32),
                   jax.ShapeDtypeStruct((N_HEAD, 128), F32), jax.ShapeDtypeStruct((N_HEAD, N_BUCKET, 1), F32),
                   jax.ShapeDtypeStruct((D_IN, D), F32)],
        scratch_shapes=[pltpu.VMEM((N_GROUP, CHUNK, CHUNK), BF16),
                        pltpu.VMEM((N_GROUP, CHUNK, CHUNK), BF16), pltpu.VMEM((N_HEAD, CHUNK, 2 * CHUNK), F32),
                        pltpu.VMEM((N_GROUP, CHUNK, CHUNK), F32), pltpu.VMEM((CHUNK, 256), F32)],
        compiler_params=_params(56), exchange=exchange)


def _in_bwd_input(dz, w_in_t, x, dh1, gain1, exchange=None):
    tm = 512

    def body(dz_ref, w_ref, x_ref, dh1_ref, g_ref, dx_ref, dg_ref):
        i = pl.program_id(0)
        dhn = _dot(dz_ref[...], w_ref[...])
        xv = x_ref[...]
        r1 = _rms_scale(xv)
        xhat = xv * r1
        dx_ref[...] = dh1_ref[...] + _rms_bwd(dhn * g_ref[...], xhat, r1)
        dg = jnp.sum(dhn * xhat, axis=0, keepdims=True)

        @pl.when(i == 0)
        def _():
            dg_ref[...] = dg

        @pl.when(i > 0)
        def _():
            dg_ref[...] += dg

    return _call(
        body, (dz, w_in_t, x, dh1, gain1), grid=(S // tm,), name="in_bwd_input",
        in_specs=[pl.BlockSpec((tm, D_IN), lambda i: (i, 0)), pl.BlockSpec((D_IN, D), lambda i: (0, 0)),
                  pl.BlockSpec((tm, D), lambda i: (i, 0)), pl.BlockSpec((tm, D), lambda i: (i, 0)),
                  pl.BlockSpec((1, D), lambda i: (0, 0))],
        out_specs=[pl.BlockSpec((tm, D), lambda i: (i, 0)), pl.BlockSpec((1, D), lambda i: (0, 0))],
        out_shape=[jax.ShapeDtypeStruct((S, D), F32), jax.ShapeDtypeStruct((1, D), F32)],
        compiler_params=_params(48), exchange=exchange)


def _rel_bucket():
    a = jnp.arange(CHUNK)[:, None]
    j = jnp.arange(2 * CHUNK)[None, :]
    n = jnp.maximum(CHUNK + a - j, 0)
    max_exact = N_BUCKET // 2
    nf = jnp.maximum(n, 1).astype(jnp.float32)
    large = max_exact + (jnp.log(nf / max_exact) / math.log(CHUNK / max_exact) * (N_BUCKET - max_exact)).astype(jnp.int32)
    large = jnp.minimum(large, N_BUCKET - 1)
    return jnp.where(n < max_exact, n, large).astype(jnp.int32)


def _step(x, p, target, small, bufs, place):
    bucket = _rel_bucket()
    sinks = small["attn_sinks"].reshape(N_HEAD)
    b_t = jnp.transpose(small["b_spatial"].reshape(N_GROUP, CHUNK))
    ws = small["w_spatial"].reshape(N_GROUP, CHUNK, CHUNK)
    gain1, gain2 = small["norm1_gain"], small["norm2_gain"]
    v_gain = small["gmlp_v_gain"]
    final_gain = small["final_gain"].reshape(1, D)
    table = small["rel_bias_table"]
    bufs = dict(bufs)

    def gather(*names):
        return _RelayGather([bufs[n] for n in names])

    def took(names, got):
        bufs.update(zip(names, got))

    w_in_t = _whole(bufs["w_in"]).reshape(D_IN, D)
    (z, hn1), got = _in_proj(x, gain1, w_in_t, gather("w_out"))
    took(["w_out"], got)
    (mix, *saved), got = _mixer_fwd(z, v_gain, ws, b_t, sinks, table, bucket, gather("w_ff1"))
    took(["w_ff1"], got)
    w_out = _whole(bufs["w_out"]).reshape(D, D)
    (h1, hn2, hn2_t), _ = _out_proj(x, mix, w_out, gain2)
    w_ff1 = _whole(bufs["w_ff1"])
    (r, a, a_t), got = _ffn_up(hn2, w_ff1, gather("w_ff2"))
    took(["w_ff2"], got)
    w_ff2 = _whole(bufs["w_ff2"])
    (h2,), got = _ffn_down(h1, a, w_ff2, gather("w_ple_gate", "w_ple_proj"))
    took(["w_ple_gate", "w_ple_proj"], got)
    dh2, d_gate, d_proj, d_final, sq, dh2b = _tail(h2, p, target, _whole(bufs["w_ple_gate"]).reshape(D, D),
                                                   _whole(bufs["w_ple_proj"]), final_gain)

    def pair_sums(halves, from_sibling):
        sums, landing = zip(*[_pair_sum(g, o, place) for g, o in zip(halves, from_sibling)])
        return list(sums), list(landing)

    landed = {}
    halves = [_halves(d_gate.reshape(N_CHIP, 256, D)), _halves(d_proj)]
    (df, d_ff2), got = _ffn_bwd_down(dh2b, r, a_t, w_ff2, _SiblingExchange(halves))
    ex, halves = _ChipExchange(*pair_sums(halves, got)), [_halves(d_ff2)]
    (d_ff1,), got = _ffn_bwd_up(df, hn2_t, _Both(ex, _SiblingExchange(halves)))
    landed.update(zip(["w_ple_gate", "w_ple_proj"], got[:2]))
    ex, halves = _ChipExchange(*pair_sums(halves, got[2:])), [_halves(d_ff1)]
    (dh1, dmix, d_out, d_gain2), got = _ffn_bwd_input(df, w_ff1, dh2, h1, gain2, mix, w_out,
                                                      _Both(ex, _SiblingExchange(halves)))
    landed["w_ff2"] = got[0]
    ex, halves = _ChipExchange(*pair_sums(halves, got[1:])), [_halves(d_out.reshape(N_CHIP, 256, D))]
    (dz, d_ws, d_b, d_vgain, d_sink, d_rel, d_in_t), got = _mixer_bwd(z, dmix, v_gain, ws, b_t, saved, bucket, hn1,
                                                                     _Both(ex, _SiblingExchange(halves)))
    landed["w_ff1"] = got[0]
    small_grads = {
        "gmlp_v_gain": d_vgain, "w_spatial": d_ws.reshape(1, N_GROUP, CHUNK, CHUNK),
        "b_spatial": d_b.reshape(1, N_GROUP, CHUNK), "attn_sinks": d_sink[:, 0].reshape(1, N_HEAD),
        "rel_bias_table": jnp.transpose(d_rel.reshape(N_HEAD, N_BUCKET)), "norm2_gain": d_gain2,
        "final_gain": d_final.reshape(D),
    }
    ex, halves = _ChipExchange(*pair_sums(halves, got[1:])), [_halves(d_in_t.reshape(N_CHIP, 448, D))]
    (dx, small_grads["norm1_gain"]), got = _in_bwd_input(dz, w_in_t, x, dh1, gain1, _Both(ex, _SiblingExchange(halves)))
    landed["w_out"] = got[0]
    return dx, landed, _ChipExchange(*pair_sums(halves, got[1:])), small_grads, sq


HBM_SPEC = pl.BlockSpec(memory_space=pltpu.HBM)
VMEM_SPEC = pl.BlockSpec(memory_space=pltpu.VMEM)


def _mesh_place():
    x, y, c = lax.axis_index("x"), lax.axis_index("y"), lax.axis_index("c")
    others = [(1 - x, y), (x, 1 - y), (1 - x, 1 - y)]
    return x, y, c, others


def _remote(src, dst, send_sem, recv_sem, device):
    return pltpu.make_async_remote_copy(src_ref=src, dst_ref=dst, send_sem=send_sem, recv_sem=recv_sem,
                                        device_id=device, device_id_type=MESH)


def _hbm_like(a, shape=None, dtype=None):
    return pltpu.HBM(a.shape if shape is None else shape, a.dtype if dtype is None else dtype)


def _gather_start(bufs, send_sems, recv_sems):
    x, y, c, others = _mesh_place()
    me = 2 * x + y
    for w, buf in enumerate(bufs):
        for k in range(3):
            mine = buf.at[me, c]
            _remote(mine, mine, send_sems.at[w, k], recv_sems.at[w, k], (*others[k], c)).start()


def _gather_finish(bufs, send_sems, recv_sems):
    x, y, c, others = _mesh_place()
    me = 2 * x + y
    sibling = (x, y, 1 - c)
    idx = [2 * ox + oy for ox, oy in others]
    chips = range(3)
    for w, buf in enumerate(bufs):
        for k in chips:
            landed = buf.at[idx[k], c]
            _remote(landed, landed, send_sems.at[w, k], recv_sems.at[w, k], sibling).wait_recv()
            _remote(landed, landed, send_sems.at[w, 3 + k], recv_sems.at[w, 3 + k], sibling).start()
    for w, buf in enumerate(bufs):
        for k in chips:
            landed = buf.at[idx[k], 1 - c]
            _remote(landed, landed, send_sems.at[w, 3 + k], recv_sems.at[w, 3 + k], sibling).wait_recv()
    for w, buf in enumerate(bufs):
        for k in chips:
            mine, passed = buf.at[me, c], buf.at[idx[k], c]
            _remote(mine, mine, send_sems.at[w, k], recv_sems.at[w, k], sibling).wait_send()
            _remote(passed, passed, send_sems.at[w, 3 + k], recv_sems.at[w, 3 + k], sibling).wait_send()


def _gather_sems(n):
    return [pltpu.SemaphoreType.DMA((n, 6)), pltpu.SemaphoreType.DMA((n, 6))]


def _sibling_copies(grads, landing, send_sems, recv_sems):
    x, y, c, _ = _mesh_place()
    return [_remote(grads[w].at[j, 1 - c], landing[w].at[j], send_sems.at[w, j], recv_sems.at[w, j], (x, y, 1 - c))
            for w in range(len(grads)) for j in range(N_CHIP)]


def _sibling_exchange_start(grads, landing, send_sems, recv_sems):
    for cp in _sibling_copies(grads, landing, send_sems, recv_sems):
        cp.start()


def _sibling_exchange_finish(grads, landing, send_sems, recv_sems):
    copies = _sibling_copies(grads, landing, send_sems, recv_sems)
    for cp in copies:
        cp.wait_recv()
    for cp in copies:
        cp.wait_send()


def _sibling_exchange_sems(n):
    return [pltpu.SemaphoreType.DMA((n, N_CHIP)), pltpu.SemaphoreType.DMA((n, N_CHIP))]


def _sibling_exchange(grads):
    n = len(grads)

    def body(*refs):
        ins, outs = refs[:n], refs[n:2 * n]
        _sibling_exchange_start(ins, outs, *refs[2 * n:])
        _sibling_exchange_finish(ins, outs, *refs[2 * n:])

    return pl.pallas_call(
        body, name="sibling_exchange",
        in_specs=[HBM_SPEC] * n, out_specs=[HBM_SPEC] * n,
        out_shape=[_hbm_like(g, (N_CHIP,) + g.shape[2:]) for g in grads],
        scratch_shapes=_sibling_exchange_sems(n),
    )(*[_in_hbm(g) for g in grads])


def _chip_exchange_start(sums, landing, send_sems, recv_sems):
    x, y, c, others = _mesh_place()
    me = 2 * x + y
    for w in range(len(sums)):
        for k, (ox, oy) in enumerate(others):
            _remote(sums[w].at[2 * ox + oy], landing[w].at[me], send_sems.at[w, k], recv_sems.at[w, k],
                    (ox, oy, c)).start()


def _chip_exchange_finish(sums, landing, send_sems, recv_sems):
    x, y, c, others = _mesh_place()
    for w in range(len(sums)):
        for k, (ox, oy) in enumerate(others):
            piece = landing[w].at[2 * ox + oy]
            _remote(piece, piece, send_sems.at[w, k], recv_sems.at[w, k], (x, y, c)).wait_recv()
    for w in range(len(sums)):
        for k, (ox, oy) in enumerate(others):
            piece = sums[w].at[2 * ox + oy]
            _remote(piece, piece, send_sems.at[w, k], recv_sems.at[w, k], (x, y, c)).wait_send()


def _chip_exchange_sems(n):
    return [pltpu.SemaphoreType.DMA((n, 3)), pltpu.SemaphoreType.DMA((n, 3))]


def _sibling_allgather(bufs, also):
    n = len(bufs)
    k_in, k_out = len(also.operands), also.n_out

    def body(*refs):
        ex_ins, refs = refs[n:n + k_in], refs[n + k_in:]
        outs, refs = refs[:n], refs[n:]
        ex_outs, refs = refs[:k_out], refs[k_out:]
        send_sems, recv_sems, ex_sems = refs[0], refs[1], refs[2:]
        x, y, c, _ = _mesh_place()
        sibling = (x, y, 1 - c)
        also.start(ex_ins, ex_outs, ex_sems)
        sends = [_remote(outs[w].at[c], outs[w].at[c], send_sems.at[w], recv_sems.at[w], sibling) for w in range(n)]
        for cp in sends:
            cp.start()
        for w in range(n):
            landed = outs[w].at[1 - c]
            _remote(landed, landed, send_sems.at[w], recv_sems.at[w], sibling).wait_recv()
        for cp in sends:
            cp.wait_send()
        also.finish(ex_ins, ex_outs, ex_sems)

    res = pl.pallas_call(
        body, name="sibling_allgather",
        in_specs=[HBM_SPEC] * (n + k_in), out_specs=[HBM_SPEC] * (n + k_out),
        out_shape=[_hbm_like(b) for b in bufs] + also.out_shape,
        input_output_aliases={**{w: w for w in range(n)}, **{n + i: n + o for i, o in also.aliases.items()}},
        scratch_shapes=[pltpu.SemaphoreType.DMA((n,)), pltpu.SemaphoreType.DMA((n,))] + also.sems,
    )(*bufs, *[_in_hbm(o) for o in also.operands])
    return list(res[:n]), list(res[n:])


def _pair_sum(grad, other, place):
    _, _, h, cols = grad.shape
    tr = _row_tile(h)

    def body(place_ref, g_ref, o_ref, sums_ref, own_ref):
        s = (g_ref[0, 0] + o_ref[0]).astype(BF16)
        sums_ref[0] = s

        @pl.when(pl.program_id(1) == place_ref[0])
        def _():
            own_ref[0] = s

    return pl.pallas_call(
        body, name="pair_sum",
        grid_spec=pltpu.PrefetchScalarGridSpec(
            num_scalar_prefetch=1, grid=(h // tr, N_CHIP),
            in_specs=[pl.BlockSpec((1, 1, tr, cols), lambda r, j, place_ref: (j, place_ref[1], r, 0)),
                      pl.BlockSpec((1, tr, cols), lambda r, j, place_ref: (j, r, 0))],
            out_specs=[pl.BlockSpec((1, tr, cols), lambda r, j, place_ref: (j, r, 0)),
                       pl.BlockSpec((1, tr, cols), lambda r, j, place_ref: (place_ref[0], r, 0))]),
        out_shape=[pltpu.HBM((N_CHIP, h, cols), BF16)] * 2,
        compiler_params=_params(32, 2),
    )(place, _in_hbm(grad), _in_hbm(other))


def _chip_sum(parts, place):
    _, h, cols = parts.shape
    tr = _row_tile(h)

    def body(place_ref, p_ref, out_ref):
        out_ref[0] = ((p_ref[0].astype(F32) + p_ref[1].astype(F32)) + p_ref[2].astype(F32)) + p_ref[3].astype(F32)

    return pl.pallas_call(
        body, name="chip_sum",
        grid_spec=pltpu.PrefetchScalarGridSpec(
            num_scalar_prefetch=1, grid=(h // tr,),
            in_specs=[pl.BlockSpec((N_CHIP, tr, cols), lambda r, place_ref: (0, r, 0))],
            out_specs=pl.BlockSpec((1, tr, cols), lambda r, place_ref: (place_ref[1], r, 0))),
        out_shape=pltpu.HBM((2, h, cols), F32),
        compiler_params=_params(32),
    )(place, _in_hbm(parts))


def _adamw_math(w, g, m, v):
    m = ADAM_B1 * m + (1.0 - ADAM_B1) * g
    v = ADAM_B2 * v + (1.0 - ADAM_B2) * (g * g)
    m_hat = m / (1.0 - ADAM_B1 ** ADAM_STEP)
    v_hat = v / (1.0 - ADAM_B2 ** ADAM_STEP)
    delta = -ADAM_LR * (m_hat / (jnp.sqrt(v_hat) + ADAM_EPS) + ADAM_WD * w)
    return delta, m, v


def _adamw(w, g, m, v, exchange=None):
    rows, cols = w.shape
    tr = _row_tile(rows)

    def body(w_ref, g_ref, m_ref, v_ref, d_ref, nm_ref, nv_ref, g_out_ref):
        g = g_ref[...]
        d_ref[...], nm_ref[...], nv_ref[...] = _adamw_math(w_ref[...], g, m_ref[...], v_ref[...])
        g_out_ref[...] = g

    spec = pl.BlockSpec((tr, cols), lambda r: (r, 0))
    return _call(
        body, (w, g, m, v), grid=(rows // tr,), name="adamw",
        in_specs=[spec] * 4, out_specs=[spec] * 4,
        out_shape=[jax.ShapeDtypeStruct((rows, cols), F32)] * 4,
        compiler_params=_params(48), exchange=exchange)


SMALL_NAMES = ("norm1_gain", "gmlp_v_gain", "w_spatial", "b_spatial", "attn_sinks", "rel_bias_table", "norm2_gain",
               "final_gain")
PACK_TILE = 8 * 128


def _pack_small(arrays):
    parts = []
    for a in arrays:
        flat = a.reshape(-1)
        rows = -(-flat.shape[0] // PACK_TILE) * 8
        parts.append(jnp.pad(flat, (0, rows * 128 - flat.shape[0])).reshape(rows, 128))
    return jnp.concatenate(parts, axis=0)


def _unpack_small(packed, like):
    out, row = [], 0
    for a in like:
        size = math.prod(a.shape)
        rows = -(-size // PACK_TILE) * 8
        out.append(packed[row:row + rows].reshape(-1)[:size].reshape(a.shape))
        row += rows
    return out


def _small_update(gathered, w, m, v):
    rows = gathered.shape[1]

    def body(g_ref, w_ref, m_ref, v_ref, tot_ref, d_ref, nm_ref, nv_ref):
        total = g_ref[0].astype(F32)
        for dev in range(1, 8):
            total = total + g_ref[dev].astype(F32)
        tot_ref[...] = total
        d_ref[...], nm_ref[...], nv_ref[...] = _adamw_math(w_ref[...], total, m_ref[...], v_ref[...])

    return pl.pallas_call(
        body, name="small_update",
        in_specs=[VMEM_SPEC] * 4, out_specs=[VMEM_SPEC] * 4,
        out_shape=[jax.ShapeDtypeStruct((rows, 128), F32)] * 4,
        compiler_params=pltpu.CompilerParams(vmem_limit_bytes=24 * MIB),
    )(gathered, w, m, v)


def _halves(a):
    return a.reshape(a.shape[:-2] + (2, a.shape[-2] // 2, a.shape[-1]))


def _whole(a):
    return a.reshape(a.shape[:-3] + (2 * a.shape[-2], a.shape[-1]))


def kernel(x, p, norm1_gain, w_in, gmlp_v_gain, w_spatial, b_spatial, attn_sinks, rel_bias_table, w_out, norm2_gain, w_ff1, w_ff2, w_ple_proj, w_ple_gate, final_gain, loss_target, m_norm1_gain, m_w_in, m_gmlp_v_gain, m_w_spatial, m_b_spatial, m_attn_sinks, m_rel_bias_table, m_w_out, m_norm2_gain, m_w_ff1, m_w_ff2, m_w_ple_proj, m_w_ple_gate, m_final_gain, v_norm1_gain, v_w_in, v_gmlp_v_gain, v_w_spatial, v_b_spatial, v_attn_sinks, v_rel_bias_table, v_w_out, v_norm2_gain, v_w_ff1, v_w_ff2, v_w_ple_proj, v_w_ple_gate, v_final_gain):
    given = dict(locals())
    small = {n: given[n] for n in SMALL_NAMES}
    chip = 2 * lax.axis_index("x") + lax.axis_index("y")
    place = jnp.stack([chip, lax.axis_index("c")]).astype(jnp.int32)

    big_names = ("w_in", "w_out", "w_ff1", "w_ff2", "w_ple_proj", "w_ple_gate")
    shards = {n: given[n][0] for n in big_names}
    travel = dict(shards, w_in=jnp.transpose(shards["w_in"]))
    rest = [n for n in big_names if n != "w_in"]
    cast, gathered = _cast_shards_beside_gather([travel[n] for n in rest], place[:1],
                                                [_cast_shard(travel["w_in"], place[:1])])
    bufs = dict(zip(rest + ["w_in"], cast + gathered))
    dx, landed, exchange_in, small_grads, sq = _step(x[0], p[0, 0], loss_target[0], small, bufs, place)

    out_grad, out_delta, out_m, out_v = {}, {}, {}, {}

    def update(n, g, exchange=None):
        to = jnp.transpose if n == "w_in" else (lambda a: a)
        (delta, new_m, new_v, g_out), got = _adamw(to(shards[n]), g, to(given["m_" + n][0]), to(given["v_" + n][0]),
                                                   exchange)
        out_grad[n], out_delta[n], out_m[n], out_v[n] = [to(a)[None] for a in (g_out, delta, new_m, new_v)]
        return got

    spare = jnp.zeros((8, 128), F32)
    small_packed = _pack_small([small_grads[n] for n in SMALL_NAMES] + [spare]).astype(BF16)
    early = [n for n in big_names if n != "w_in"]
    reduced, (small_gathered, sq_gathered, landed_in) = _sibling_allgather(
        [_chip_sum(landed[n], place) for n in early], _Both(_Both(_GatherAll(small_packed), _GatherAll(sq)), exchange_in))
    for n, r in zip(early, reduced):
        update(n, _whole(r))
    (reduced_in,), _ = _sibling_allgather([_chip_sum(landed_in, place)], _Nothing())
    update("w_in", _whole(reduced_in))

    like = [given[n] for n in SMALL_NAMES] + [spare]
    packed = _small_update(small_gathered, *[_pack_small([given[pre + n] for n in SMALL_NAMES] + [spare])
                                             for pre in ("", "m_", "v_")])
    for res, out in zip(packed, (out_grad, out_delta, out_m, out_v)):
        out.update(zip(SMALL_NAMES, _unpack_small(res, like)))
    loss = 0.5 * jnp.sum(sq_gathered[:, 0, 0]) / D

    order = ("norm1_gain", "w_in", "gmlp_v_gain", "w_spatial", "b_spatial", "attn_sinks", "rel_bias_table", "w_out",
             "norm2_gain", "w_ff1", "w_ff2", "w_ple_proj", "w_ple_gate", "final_gain")
    return (loss, dx[None], *[out_grad[n] for n in order], *[out_delta[n] for n in order],
            *[out_m[n] for n in order], *[out_v[n] for n in order])
```

```python
import functools
import math

import jax
import jax.numpy as jnp
from jax import lax
from jax.experimental import pallas as pl
from jax.experimental.pallas import tpu as pltpu

S = 2048
D = 1024
D_IN = 1792
D_FF = 4096
PLE = 256
N_CHIP = 4
N_GROUP = 4
CHUNK = 128
N_HEAD = 8
N_BLOCK = S // CHUNK
N_BUCKET = 32
EPS = 1e-6
NEG_INF = -1e30
QK_SCALE = 0.125
GELU_C = math.sqrt(2.0 / math.pi)

ADAM_LR = 0.001
ADAM_B1 = 0.9
ADAM_B2 = 0.999
ADAM_EPS = 1e-08
ADAM_WD = 0.01
ADAM_STEP = 10

F32 = jnp.float32
BF16 = jnp.bfloat16
MIB = 1024 * 1024
MESH = pl.DeviceIdType.MESH

NT = (((1,), (1,)), ((), ()))
TN = (((0,), (0,)), ((), ()))


def _dot(a, b):
    return jnp.dot(a, b, preferred_element_type=F32)


def _dot_nt(a, b):
    return lax.dot_general(a, b, NT, preferred_element_type=F32)


def _dot_tn(a, b):
    return lax.dot_general(a, b, TN, preferred_element_type=F32)


def _params(vmem_mib, n_axes=1):
    return pltpu.CompilerParams(dimension_semantics=("arbitrary",) * n_axes, vmem_limit_bytes=vmem_mib * MIB)


def _rms_scale(v):
    return lax.rsqrt(jnp.mean(v * v, axis=-1, keepdims=True) + EPS)


def _rms_bwd(dy_gain, xhat, r):
    return r * (dy_gain - xhat * jnp.mean(dy_gain * xhat, axis=-1, keepdims=True))


class _Gather:
    def __init__(self, bufs):
        self.operands = list(bufs)
        self.n_out = len(self.operands)
        self.out_shape = [_hbm_like(b) for b in bufs]
        self.aliases = {w: w for w in range(self.n_out)}
        self.sems = _gather_sems(self.n_out)

    def start(self, ins, outs, sems):
        _gather_start(outs, *sems)

    def finish(self, ins, outs, sems):
        _gather_finish(outs, *sems)


class _RelayGather(_Gather):
    TOP, BOTTOM = 6, 7
    DIAGONAL_PASSED = 5
    MIDDLE_AT, LATE_AT = (5, 8), (7, 8)

    def __init__(self, bufs):
        super().__init__(bufs)
        self.sems = [pltpu.SemaphoreType.DMA((self.n_out, 8)), pltpu.SemaphoreType.DMA((self.n_out, 8))]

    def _copies(self, bufs, send_sems, recv_sems):
        x, y, c, others = _mesh_place()
        me = 2 * x + y
        idx = [2 * ox + oy for ox, oy in others]
        sibling = (x, y, 1 - c)
        direct, passed, relayed = [], [], []
        for w, buf in enumerate(bufs):
            rows = buf.shape[2] // 2
            upper, lower = pl.ds(0, rows), pl.ds(rows, rows)
            for k in (0, 1):
                mine = buf.at[me, c]
                direct.append((_remote(mine, mine, send_sems.at[w, k], recv_sems.at[w, k], (*others[k], c)),
                               buf.at[idx[k], c], w, k))
            for k in (0, 1, 2):
                here = buf.at[idx[k], c]
                passed.append((_remote(here, here, send_sems.at[w, 3 + k], recv_sems.at[w, 3 + k], sibling),
                               buf.at[idx[k], 1 - c], w, 3 + k))
            from_x, from_y = buf.at[idx[0], c, upper], buf.at[idx[1], c, lower]
            relayed.append((_remote(from_x, from_x, send_sems.at[w, self.TOP], recv_sems.at[w, self.TOP],
                                    (*others[1], c)), buf.at[idx[2], c, upper], w, self.TOP))
            relayed.append((_remote(from_y, from_y, send_sems.at[w, self.BOTTOM], recv_sems.at[w, self.BOTTOM],
                                    (*others[0], c)), buf.at[idx[2], c, lower], w, self.BOTTOM))
        return direct, passed, relayed

    @staticmethod
    def _landed(piece, send_sems, recv_sems, w, col):
        x, y, c, _ = _mesh_place()
        _remote(piece, piece, send_sems.at[w, col], recv_sems.at[w, col], (x, y, c)).wait_recv()

    def start(self, ins, outs, sems):
        for cp, _, _, _ in self._copies(outs, *sems)[0]:
            cp.start()

    def middle(self, ins, outs, sems):
        direct, passed, relayed = self._copies(outs, *sems)
        for _, piece, w, col in direct:
            self._landed(piece, *sems, w, col)
        for cp, _, _, col in passed:
            if col != self.DIAGONAL_PASSED:
                cp.start()
        for cp, _, _, _ in relayed:
            cp.start()

    def late(self, ins, outs, sems):
        direct, passed, relayed = self._copies(outs, *sems)
        for _, piece, w, col in relayed:
            self._landed(piece, *sems, w, col)
        for cp, _, _, col in passed:
            if col == self.DIAGONAL_PASSED:
                cp.start()

    def finish(self, ins, outs, sems):
        direct, passed, relayed = self._copies(outs, *sems)
        for _, piece, w, col in passed:
            self._landed(piece, *sems, w, col)
        for cp, _, _, _ in direct + passed + relayed:
            cp.wait_send()


class _ChipExchange:
    def __init__(self, sums, landing):
        self.n_out = len(landing)
        self.operands = list(sums) + list(landing)
        self.out_shape = [_hbm_like(b) for b in landing]
        self.aliases = {self.n_out + w: w for w in range(self.n_out)}
        self.sems = _chip_exchange_sems(self.n_out)

    def start(self, ins, outs, sems):
        _chip_exchange_start(ins[:self.n_out], outs, *sems)

    def finish(self, ins, outs, sems):
        _chip_exchange_finish(ins[:self.n_out], outs, *sems)


class _GatherAll:
    def __init__(self, packed):
        self.operands = [packed]
        self.n_out = 1
        self.out_shape = [_hbm_like(packed, (8,) + packed.shape)]
        self.aliases = {}
        self.sems = [pltpu.SemaphoreType.DMA((8,)), pltpu.SemaphoreType.DMA((8,))]

    def _copies(self, ins, outs, sems):
        x, y, c, _ = _mesh_place()
        me = 4 * x + 2 * y + c
        send_sems, recv_sems = sems
        copies = []
        for k in range(1, 8):
            peer = (1 - x if k // 4 else x, 1 - y if (k // 2) % 2 else y, 1 - c if k % 2 else c)
            src = 4 * peer[0] + 2 * peer[1] + peer[2]
            copies.append((_remote(ins[0], outs[0].at[me], send_sems.at[k], recv_sems.at[k], peer), outs[0].at[src]))
        own = pltpu.make_async_copy(ins[0], outs[0].at[me], send_sems.at[0])
        return own, copies

    def start(self, ins, outs, sems):
        own, copies = self._copies(ins, outs, sems)
        own.start()
        for cp, _ in copies:
            cp.start()

    def finish(self, ins, outs, sems):
        own, copies = self._copies(ins, outs, sems)
        x, y, c, _ = _mesh_place()
        for k, (cp, landed) in enumerate(copies):
            _remote(landed, landed, sems[0].at[k + 1], sems[1].at[k + 1], (x, y, c)).wait_recv()
        for cp, _ in copies:
            cp.wait_send()
        own.wait()


class _Nothing:
    operands, n_out, out_shape, aliases, sems = [], 0, [], {}, []

    def start(self, ins, outs, sems):
        pass

    def finish(self, ins, outs, sems):
        pass


class _Both:
    def __init__(self, a, b):
        self.a, self.b = a, b
        self.operands = a.operands + b.operands
        self.n_out = a.n_out + b.n_out
        self.out_shape = a.out_shape + b.out_shape
        self.aliases = dict(a.aliases)
        self.aliases.update({len(a.operands) + i: a.n_out + o for i, o in b.aliases.items()})
        self.sems = a.sems + b.sems

    def _split(self, ins, outs, sems):
        ka, na, sa = len(self.a.operands), self.a.n_out, len(self.a.sems)
        return (ins[:ka], outs[:na], sems[:sa]), (ins[ka:], outs[na:], sems[sa:])

    def start(self, ins, outs, sems):
        for ex, args in zip((self.a, self.b), self._split(ins, outs, sems)):
            ex.start(*args)

    def finish(self, ins, outs, sems):
        for ex, args in zip((self.a, self.b), self._split(ins, outs, sems)):
            ex.finish(*args)


class _SiblingExchange:
    def __init__(self, grads):
        self.operands = list(grads)
        self.n_out = len(self.operands)
        self.out_shape = [_hbm_like(g, (N_CHIP,) + g.shape[2:]) for g in grads]
        self.aliases = {}
        self.sems = _sibling_exchange_sems(self.n_out)

    def start(self, ins, outs, sems):
        _sibling_exchange_start(ins, outs, *sems)

    def finish(self, ins, outs, sems):
        _sibling_exchange_finish(ins, outs, *sems)


def _call(body, operands, *, grid, in_specs, out_specs, out_shape, name, compiler_params, scratch_shapes=(),
          exchange=None):
    operands = [o if getattr(spec, "memory_space", None) == pltpu.SMEM else _in_hbm(o)
                for o, spec in zip(operands, in_specs)]
    out_shape = [pltpu.HBM(s.shape, s.dtype) for s in out_shape]
    if exchange is None:
        res = pl.pallas_call(body, grid=grid, in_specs=in_specs, out_specs=out_specs, out_shape=out_shape, name=name,
                             scratch_shapes=list(scratch_shapes), compiler_params=compiler_params)(*operands)
        return list(res), []
    n_in, n_out, n_scr = len(in_specs), len(out_specs), len(scratch_shapes)
    k_in, k_out = len(exchange.operands), exchange.n_out

    def fused(*refs):
        ins, refs = refs[:n_in], refs[n_in:]
        ex_ins, refs = refs[:k_in], refs[k_in:]
        outs, refs = refs[:n_out], refs[n_out:]
        ex_outs, refs = refs[:k_out], refs[k_out:]
        scratch, sems = refs[:n_scr], refs[n_scr:]
        ids = [pl.program_id(a) for a in range(len(grid))]
        first = functools.reduce(jnp.logical_and, [i == 0 for i in ids])
        last = functools.reduce(jnp.logical_and, [i == g - 1 for i, g in zip(ids, grid)])

        @pl.when(first)
        def _():
            exchange.start(ex_ins, ex_outs, sems)

        def at_step(numerator, denominator):
            at = (numerator * math.prod(grid)) // denominator
            place = [(at // math.prod(grid[a + 1:])) % grid[a] for a in range(len(grid))]
            return functools.reduce(jnp.logical_and, [i == p for i, p in zip(ids, place)])

        if hasattr(exchange, "middle"):
            @pl.when(at_step(*exchange.MIDDLE_AT))
            def _():
                exchange.middle(ex_ins, ex_outs, sems)

            @pl.when(at_step(*exchange.LATE_AT))
            def _():
                exchange.late(ex_ins, ex_outs, sems)

        body(*ins, *outs, *scratch)

        @pl.when(last)
        def _():
            exchange.finish(ex_ins, ex_outs, sems)

    res = pl.pallas_call(
        fused, grid=grid, name=name,
        in_specs=list(in_specs) + [HBM_SPEC] * k_in, out_specs=list(out_specs) + [HBM_SPEC] * k_out,
        out_shape=list(out_shape) + exchange.out_shape,
        input_output_aliases={n_in + i: n_out + o for i, o in exchange.aliases.items()},
        scratch_shapes=list(scratch_shapes) + exchange.sems, compiler_params=compiler_params,
    )(*operands, *[_in_hbm(o) for o in exchange.operands])
    return list(res[:n_out]), list(res[n_out:])


def _in_hbm(a):
    return pltpu.with_memory_space_constraint(a, pltpu.HBM)


def _row_tile(h):
    return max(t for t in range(16, 513, 16) if h % t == 0)


def _cast_shard(a, chip):
    rows, cols = a.shape
    h = rows // 2
    tr = _row_tile(h)

    def body(chip_ref, a_ref, o_ref):
        o_ref[0, 0] = a_ref[0].astype(BF16)

    return pl.pallas_call(
        body, name="cast_shard",
        grid_spec=pltpu.PrefetchScalarGridSpec(
            num_scalar_prefetch=1, grid=(2, h // tr),
            in_specs=[pl.BlockSpec((1, tr, cols), lambda s, r, chip_ref: (s, r, 0))],
            out_specs=pl.BlockSpec((1, 1, tr, cols), lambda s, r, chip_ref: (chip_ref[0], s, r, 0))),
        out_shape=pltpu.HBM((N_CHIP, 2, h, cols), BF16),
        compiler_params=_params(16, 2),
    )(chip, _in_hbm(a.reshape(2, h, cols)))


def _cast_shards_beside_gather(arrays, chip, gathered):
    n, k = len(arrays), len(gathered)
    shapes = [(a.shape[0] // 2, a.shape[1]) for a in arrays]

    def body(chip_ref, *refs):
        ins, refs = refs[:n], refs[n + k:]
        outs, refs = refs[:n], refs[n:]
        bufs, sems = refs[:k], refs[k:]
        half = pl.program_id(0)

        @pl.when(half == 0)
        def _():
            _gather_start(bufs, *sems)

        for a_ref, o_ref in zip(ins, outs):
            o_ref[0, 0] = a_ref[0].astype(BF16)

        @pl.when(half == 1)
        def _():
            _gather_finish(bufs, *sems)

    res = pl.pallas_call(
        body, name="cast_shards",
        grid_spec=pltpu.PrefetchScalarGridSpec(
            num_scalar_prefetch=1, grid=(2,),
            in_specs=[pl.BlockSpec((1, h, c), lambda s, chip_ref: (s, 0, 0)) for h, c in shapes] + [HBM_SPEC] * k,
            out_specs=[pl.BlockSpec((1, 1, h, c), lambda s, chip_ref: (chip_ref[0], s, 0, 0)) for h, c in shapes]
            + [HBM_SPEC] * k,
            scratch_shapes=_gather_sems(k)),
        out_shape=[pltpu.HBM((N_CHIP, 2, h, c), BF16) for h, c in shapes] + [_hbm_like(b) for b in gathered],
        input_output_aliases={1 + n + i: n + i for i in range(k)},
        compiler_params=_params(32),
    )(chip, *[_in_hbm(a.reshape(2, h, c)) for a, (h, c) in zip(arrays, shapes)], *gathered)
    return list(res[:n]), list(res[n:])


def _in_proj(x, gain1, w_in_t, exchange=None):
    tm = 512

    def body(x_ref, g_ref, w_ref, z_ref, hn_ref):
        xv = x_ref[...]
        hn = (xv * _rms_scale(xv) * g_ref[...]).astype(BF16)
        hn_ref[...] = hn
        z_ref[...] = _dot_nt(hn, w_ref[...])

    return _call(
        body, (x, gain1, w_in_t), grid=(S // tm,), name="in_proj",
        in_specs=[pl.BlockSpec((tm, D), lambda i: (i, 0)), pl.BlockSpec((1, D), lambda i: (0, 0)),
                  pl.BlockSpec((D_IN, D), lambda i: (0, 0))],
        out_specs=[pl.BlockSpec((tm, D_IN), lambda i: (i, 0)), pl.BlockSpec((tm, D), lambda i: (i, 0))],
        out_shape=[jax.ShapeDtypeStruct((S, D_IN), F32), jax.ShapeDtypeStruct((S, D), BF16)],
        compiler_params=_params(40), exchange=exchange)


def _gelu_parts(v):
    t = jnp.tanh(GELU_C * (v + 0.044715 * (v * v * v)))
    cdf = 0.5 * (1.0 + t)
    return cdf, t


def _band_mask(n):
    a = lax.broadcasted_iota(jnp.int32, (CHUNK, 2 * CHUNK), 0)
    j = lax.broadcasted_iota(jnp.int32, (CHUNK, 2 * CHUNK), 1)
    dist = CHUNK + a - j
    valid = (dist >= 0) & (dist < CHUNK)
    return valid & ((n > 0) | (j >= CHUNK))


def _fill_bias(bucket_ref, table_ref, bias_ref):
    bucket = bucket_ref[...]
    for h in range(N_HEAD):
        acc = jnp.zeros((CHUNK, 2 * CHUNK), F32)
        for b in range(N_BUCKET):
            acc = jnp.where(bucket == b, table_ref[b, h], acc)
        bias_ref[h] = acc


def _fill_tril(ws_ref, wt_ref, wtt_ref=None):
    r = lax.broadcasted_iota(jnp.int32, (CHUNK, CHUNK), 0)
    c = lax.broadcasted_iota(jnp.int32, (CHUNK, CHUNK), 1)
    for g in range(N_GROUP):
        w = jnp.where(c <= r, ws_ref[g], 0.0)
        wt_ref[g] = w.astype(BF16)
        if wtt_ref is not None:
            wtt_ref[g] = w.T.astype(BF16)


def _kv_layouts(kv_prev, kv_cur):
    both = jnp.concatenate([kv_prev, kv_cur], axis=0)
    k = both[:, :128]
    v = both[:, 128:]
    return (k.astype(BF16), pltpu.roll(k, 64, axis=1).astype(BF16),
            v.astype(BF16), pltpu.roll(v, 64, axis=1).astype(BF16))


def _head_place(h):
    pair, pos, kvh = h // 2, h % 2, h // 4
    return pair, pos, kvh == pos


def _softmax_sink(qm, k_use, bias_h, sink, valid):
    s = _dot_nt(qm, k_use) * QK_SCALE + bias_h
    s = jnp.where(valid, s, NEG_INF)
    m = jnp.maximum(jnp.max(s, axis=-1, keepdims=True), sink)
    e = jnp.exp(s - m)
    es = jnp.exp(sink - m)
    inv = 1.0 / (jnp.sum(e, axis=-1, keepdims=True) + es)
    return e * inv, es * inv


def _mixer_fwd(z, v_gain, w_spatial, b_spatial_t, sinks, rel_table, bucket, exchange=None):
    def body(z_ref, kvp_ref, gain_ref, ws_ref, bt_ref, sink_ref, table_ref, bucket_ref, out_ref, probs_ref, probs_t_ref,
             share_ref, guv_ref, dgelu_ref, bias_ref, wt_ref):
        n = pl.program_id(0)

        @pl.when(n == 0)
        def _():
            _fill_bias(bucket_ref, table_ref, bias_ref)
            _fill_tril(ws_ref, wt_ref)

        zuv = z_ref[:, :1024]
        cdf, t = _gelu_parts(zuv)
        guv = zuv * cdf
        guv_ref[...] = guv
        dgelu_ref[...] = cdf + zuv * (0.5 * (1.0 - t * t)) * (GELU_C * (1.0 + 3.0 * 0.044715 * (zuv * zuv)))
        for g in range(N_GROUP):
            vg = guv[:, 512 + 128 * g:512 + 128 * (g + 1)]
            vn = vg * _rms_scale(vg) * gain_ref[:, 128 * g:128 * (g + 1)]
            sv = _dot(wt_ref[g], vn.astype(BF16)) + bt_ref[:, g:g + 1]
            out_ref[:, 128 * g:128 * (g + 1)] = (guv[:, 128 * g:128 * (g + 1)] * sv).astype(BF16)

        k_same, k_swap, v_same, v_swap = _kv_layouts(kvp_ref[...], z_ref[:, 1536:1792])
        valid = _band_mask(n)
        lane = lax.broadcasted_iota(jnp.int32, (1, 128), 1)
        lane_half = lane // 64
        shares = jnp.zeros((CHUNK, 128), F32)
        for pair in range(N_HEAD // 2):
            qq = z_ref[:, 1024 + 128 * pair:1024 + 128 * (pair + 1)]
            acc = jnp.zeros((CHUNK, 128), F32)
            for pos in range(2):
                h = 2 * pair + pos
                _, _, same = _head_place(h)
                qm = jnp.where(lane_half == pos, qq, 0.0).astype(BF16)
                p, p_sink = _softmax_sink(qm, k_same if same else k_swap, bias_ref[h], sink_ref[h], valid)
                pb = p.astype(BF16)
                probs_ref[0, h] = pb
                probs_t_ref[0, h] = p.T.astype(BF16)
                shares = jnp.where(lane == h, p_sink, shares)
                vm = jnp.where(lane_half == pos, v_same if same else v_swap, jnp.zeros((), BF16))
                acc = acc + _dot(pb, vm)
            out_ref[:, 512 + 128 * pair:512 + 128 * (pair + 1)] = acc.astype(BF16)
        share_ref[...] = shares

    return _call(
        body, (z, z, v_gain, w_spatial, b_spatial_t, sinks, rel_table, bucket), grid=(N_BLOCK,), name="mixer_fwd",
        in_specs=[pl.BlockSpec((CHUNK, D_IN), lambda n: (n, 0)),
                  pl.BlockSpec((CHUNK, 256), lambda n: (jnp.maximum(n - 1, 0), 6)),
                  pl.BlockSpec((1, 512), lambda n: (0, 0)),
                  pl.BlockSpec((N_GROUP, CHUNK, CHUNK), lambda n: (0, 0, 0)),
                  pl.BlockSpec((CHUNK, N_GROUP), lambda n: (0, 0)),
                  pl.BlockSpec(memory_space=pltpu.SMEM),
                  pl.BlockSpec(memory_space=pltpu.SMEM),
                  pl.BlockSpec((CHUNK, 2 * CHUNK), lambda n: (0, 0))],
        out_specs=[pl.BlockSpec((CHUNK, D), lambda n: (n, 0)),
                   pl.BlockSpec((1, N_HEAD, CHUNK, 2 * CHUNK), lambda n: (n, 0, 0, 0)),
                   pl.BlockSpec((1, N_HEAD, 2 * CHUNK, CHUNK), lambda n: (n, 0, 0, 0)),
                   pl.BlockSpec((CHUNK, 128), lambda n: (n, 0)),
                   pl.BlockSpec((CHUNK, 1024), lambda n: (n, 0)), pl.BlockSpec((CHUNK, 1024), lambda n: (n, 0))],
        out_shape=[jax.ShapeDtypeStruct((S, D), BF16), jax.ShapeDtypeStruct((N_BLOCK, N_HEAD, CHUNK, 2 * CHUNK), BF16),
                   jax.ShapeDtypeStruct((N_BLOCK, N_HEAD, 2 * CHUNK, CHUNK), BF16), jax.ShapeDtypeStruct((S, 128), F32),
                   jax.ShapeDtypeStruct((S, 1024), F32), jax.ShapeDtypeStruct((S, 1024), F32)],
        scratch_shapes=[pltpu.VMEM((N_HEAD, CHUNK, 2 * CHUNK), F32), pltpu.VMEM((N_GROUP, CHUNK, CHUNK), BF16)],
        compiler_params=_params(32), exchange=exchange)


def _out_proj(x, mix, w_out, gain2, exchange=None):
    tm = 256

    def body(x_ref, mix_ref, w_ref, g_ref, h1_ref, hn_ref, hnt_ref):
        h1 = x_ref[...] + _dot(mix_ref[...], w_ref[...])
        h1_ref[...] = h1
        hn = h1 * _rms_scale(h1) * g_ref[...]
        hn_ref[...] = hn.astype(BF16)
        hnt_ref[...] = hn.T.astype(BF16)

    return _call(
        body, (x, mix, w_out, gain2), grid=(S // tm,), name="out_proj",
        in_specs=[pl.BlockSpec((tm, D), lambda i: (i, 0)), pl.BlockSpec((tm, D), lambda i: (i, 0)),
                  pl.BlockSpec((D, D), lambda i: (0, 0)), pl.BlockSpec((1, D), lambda i: (0, 0))],
        out_specs=[pl.BlockSpec((tm, D), lambda i: (i, 0)), pl.BlockSpec((tm, D), lambda i: (i, 0)),
                   pl.BlockSpec((D, tm), lambda i: (0, i))],
        out_shape=[jax.ShapeDtypeStruct((S, D), F32), jax.ShapeDtypeStruct((S, D), BF16),
                   jax.ShapeDtypeStruct((D, S), BF16)],
        compiler_params=_params(32), exchange=exchange)


def _ffn_up(hn2, w_ff1, exchange=None):
    tm = 512
    nj = D_FF // 1024

    def body(hn_ref, w1_ref, r_ref, a_ref, at_ref):
        r = jnp.maximum(_dot(hn_ref[...], w1_ref[0]), 0.0)
        r_ref[...] = r.astype(BF16)
        a = r * r
        a_ref[...] = a.astype(BF16)
        at_ref[...] = a.T.astype(BF16)

    return _call(
        body, (hn2, w_ff1), grid=(nj, S // tm), name="ffn_up",
        in_specs=[pl.BlockSpec((tm, D), lambda j, i: (i, 0)), pl.BlockSpec((1, D, 1024), lambda j, i: (j, 0, 0))],
        out_specs=[pl.BlockSpec((tm, 1024), lambda j, i: (i, j)), pl.BlockSpec((tm, 1024), lambda j, i: (i, j)),
                   pl.BlockSpec((1024, tm), lambda j, i: (j, i))],
        out_shape=[jax.ShapeDtypeStruct((S, D_FF), BF16), jax.ShapeDtypeStruct((S, D_FF), BF16),
                   jax.ShapeDtypeStruct((D_FF, S), BF16)],
        compiler_params=_params(40, 2), exchange=exchange)


def _ffn_down(h1, a, w_ff2, exchange=None):
    tm = 512
    nj = D_FF // 1024

    def body(h1_ref, a_ref, w2_ref, h2_ref):
        h2_ref[...] = h1_ref[...] + _dot(a_ref[...], w2_ref[...].reshape(D_FF, D))

    return _call(
        body, (h1, a, w_ff2), grid=(S // tm,), name="ffn_down",
        in_specs=[pl.BlockSpec((tm, D), lambda i: (i, 0)), pl.BlockSpec((tm, D_FF), lambda i: (i, 0)),
                  pl.BlockSpec((nj, 1024, D), lambda i: (0, 0, 0), pipeline_mode=pl.Buffered(1))],
        out_specs=[pl.BlockSpec((tm, D), lambda i: (i, 0))],
        out_shape=[jax.ShapeDtypeStruct((S, D), F32)],
        compiler_params=_params(40), exchange=exchange)


def _tail(h2, p, target, w_gate, w_proj, final_gain):
    tm = 256
    steps = S // tm

    def body(h2_ref, p_ref, t_ref, wg_ref, wp_ref, gf_ref, dh2_ref, dwg_ref, dwp_ref, dgf_ref, loss_ref, dh2b_ref,
             dwp_acc):
        i = pl.program_id(0)
        h2 = h2_ref[...]
        h2b = h2.astype(BF16)
        pb = p_ref[...].astype(BF16)
        gate = jax.nn.sigmoid(_dot(h2b, wg_ref[...]))
        pp = jnp.concatenate([_dot(pb, wp_ref[j]) for j in range(N_CHIP)], axis=1)
        h3 = h2 + gate * pp
        r3 = _rms_scale(h3)
        xhat = h3 * r3
        gf = gf_ref[...]
        err = xhat * gf - t_ref[...]
        dy = err * (1.0 / D)
        dh3 = _rms_bwd(dy * gf, xhat, r3)
        dgp = (dh3 * pp * gate * (1.0 - gate)).astype(BF16)
        dpp = (dh3 * gate).astype(BF16)
        dh2 = dh3 + _dot_nt(dgp, wg_ref[...])
        dh2_ref[...] = dh2
        dh2b_ref[...] = dh2.astype(BF16)
        dwg = _dot_tn(h2b, dgp)
        dwp = _dot_tn(pb, dpp)
        dgf = jnp.sum(dy * xhat, axis=0, keepdims=True)
        sq = jnp.sum(jnp.sum(err * err, axis=1, keepdims=True), axis=0, keepdims=True)

        @pl.when(i == 0)
        def _():
            dwg_ref[...] = dwg
            dwp_acc[...] = dwp
            dgf_ref[...] = dgf
            loss_ref[...] = jnp.broadcast_to(sq, (8, 128))

        @pl.when(i > 0)
        def _():
            dwg_ref[...] += dwg
            dwp_acc[...] += dwp
            dgf_ref[...] += dgf
            loss_ref[...] += jnp.broadcast_to(sq, (8, 128))

        @pl.when(i == steps - 1)
        def _():
            for j in range(N_CHIP):
                dwp_ref[j] = dwp_acc[:, 256 * j:256 * (j + 1)]

    return _call(
        body, (h2, p, target, w_gate, w_proj, final_gain), grid=(steps,), name="tail",
        in_specs=[pl.BlockSpec((tm, D), lambda i: (i, 0)), pl.BlockSpec((tm, PLE), lambda i: (i, 0)),
                  pl.BlockSpec((tm, D), lambda i: (i, 0)), pl.BlockSpec((D, D), lambda i: (0, 0)),
                  pl.BlockSpec((N_CHIP, PLE, 256), lambda i: (0, 0, 0)), pl.BlockSpec((1, D), lambda i: (0, 0))],
        out_specs=[pl.BlockSpec((tm, D), lambda i: (i, 0)), pl.BlockSpec((D, D), lambda i: (0, 0)),
                   pl.BlockSpec((N_CHIP, PLE, 256), lambda i: (0, 0, 0)), pl.BlockSpec((1, D), lambda i: (0, 0)),
                   pl.BlockSpec((8, 128), lambda i: (0, 0)), pl.BlockSpec((tm, D), lambda i: (i, 0))],
        out_shape=[jax.ShapeDtypeStruct((S, D), F32), jax.ShapeDtypeStruct((D, D), F32),
                   jax.ShapeDtypeStruct((N_CHIP, PLE, 256), F32), jax.ShapeDtypeStruct((1, D), F32),
                   jax.ShapeDtypeStruct((8, 128), F32), jax.ShapeDtypeStruct((S, D), BF16)],
        scratch_shapes=[pltpu.VMEM((PLE, D), F32)],
        compiler_params=_params(48))[0]


def _ffn_bwd_down(dh2b, r, a_t, w_ff2, exchange=None):
    tm = 1024
    nj = D_FF // 1024

    def body(dh2_ref, r_ref, at_ref, w2_ref, df_ref, dw2_ref):
        i = pl.program_id(1)
        dh2b = dh2_ref[...]
        da = _dot_nt(dh2b, w2_ref[0])
        df_ref[...] = (da * (2.0 * r_ref[...].astype(F32))).astype(BF16)
        dw2 = _dot(at_ref[...], dh2b)

        @pl.when(i == 0)
        def _():
            dw2_ref[0] = dw2

        @pl.when(i > 0)
        def _():
            dw2_ref[0] += dw2

    return _call(
        body, (dh2b, r, a_t, w_ff2), grid=(nj, S // tm), name="ffn_bwd_down",
        in_specs=[pl.BlockSpec((tm, D), lambda j, i: (i, 0)), pl.BlockSpec((tm, 1024), lambda j, i: (i, j)),
                  pl.BlockSpec((1024, tm), lambda j, i: (j, i)), pl.BlockSpec((1, 1024, D), lambda j, i: (j, 0, 0))],
        out_specs=[pl.BlockSpec((tm, 1024), lambda j, i: (i, j)), pl.BlockSpec((1, 1024, D), lambda j, i: (j, 0, 0))],
        out_shape=[jax.ShapeDtypeStruct((S, D_FF), BF16), jax.ShapeDtypeStruct((nj, 1024, D), F32)],
        compiler_params=_params(48, 2), exchange=exchange)


def _ffn_bwd_up(df, hn2_t, exchange=None):
    tm = 2048
    nj = D_FF // 1024

    def body(df_ref, hnt_ref, dw1_ref):
        i = pl.program_id(1)
        dw1 = _dot(hnt_ref[...], df_ref[...])

        @pl.when(i == 0)
        def _():
            dw1_ref[0] = dw1

        @pl.when(i > 0)
        def _():
            dw1_ref[0] += dw1

    return _call(
        body, (df, hn2_t), grid=(nj, S // tm), name="ffn_bwd_up",
        in_specs=[pl.BlockSpec((tm, 1024), lambda j, i: (i, j)), pl.BlockSpec((D, tm), lambda j, i: (0, i))],
        out_specs=[pl.BlockSpec((1, D, 1024), lambda j, i: (j, 0, 0))],
        out_shape=[jax.ShapeDtypeStruct((nj, D, 1024), F32)],
        compiler_params=_params(40, 2), exchange=exchange)


def _ffn_bwd_input(df, w_ff1, dh2, h1, gain2, mix, w_out, exchange=None):
    tm = 512
    nj = D_FF // 1024
    steps = S // tm

    def body(df_ref, w1_ref, dh2_ref, h1_ref, g_ref, mix_ref, wo_ref, dh1_ref, dmix_ref, dwo_ref, dg_ref, acc_ref):
        i = pl.program_id(0)
        j = pl.program_id(1)
        part = _dot_nt(df_ref[...], w1_ref[j])

        @pl.when(j == 0)
        def _():
            acc_ref[...] = part

        @pl.when(j > 0)
        def _():
            acc_ref[...] += part

        @pl.when(j == nj - 1)
        def _():
            dhn = acc_ref[...]
            h1 = h1_ref[...]
            r2 = _rms_scale(h1)
            xhat = h1 * r2
            dh1 = dh2_ref[...] + _rms_bwd(dhn * g_ref[...], xhat, r2)
            dh1_ref[...] = dh1
            dh1b = dh1.astype(BF16)
            dmix_ref[...] = _dot_nt(dh1b, wo_ref[...])
            dwo = _dot_tn(mix_ref[...], dh1b)
            dg = jnp.sum(dhn * xhat, axis=0, keepdims=True)

            @pl.when(i == 0)
            def _():
                dwo_ref[...] = dwo
                dg_ref[...] = dg

            @pl.when(i > 0)
            def _():
                dwo_ref[...] += dwo
                dg_ref[...] += dg

    return _call(
        body, (df, w_ff1, dh2, h1, gain2, mix, w_out), grid=(steps, nj), name="ffn_bwd_input",
        in_specs=[pl.BlockSpec((tm, 1024), lambda i, j: (i, j)),
                  pl.BlockSpec((nj, D, 1024), lambda i, j: (0, 0, 0), pipeline_mode=pl.Buffered(1)),
                  pl.BlockSpec((tm, D), lambda i, j: (i, 0)), pl.BlockSpec((tm, D), lambda i, j: (i, 0)),
                  pl.BlockSpec((1, D), lambda i, j: (0, 0)), pl.BlockSpec((tm, D), lambda i, j: (i, 0)),
                  pl.BlockSpec((D, D), lambda i, j: (0, 0), pipeline_mode=pl.Buffered(1))],
        out_specs=[pl.BlockSpec((tm, D), lambda i, j: (i, 0)), pl.BlockSpec((tm, D), lambda i, j: (i, 0)),
                   pl.BlockSpec((D, D), lambda i, j: (0, 0)), pl.BlockSpec((1, D), lambda i, j: (0, 0))],
        out_shape=[jax.ShapeDtypeStruct((S, D), F32), jax.ShapeDtypeStruct((S, D), F32),
                   jax.ShapeDtypeStruct((D, D), F32), jax.ShapeDtypeStruct((1, D), F32)],
        scratch_shapes=[pltpu.VMEM((tm, D), F32)],
        compiler_params=_params(56, 2), exchange=exchange)


IN_GROUP = 8


def _mixer_bwd(z, dmix, v_gain, w_spatial, b_spatial_t, saved, bucket, hn1, exchange=None):
    def body(z_ref, kvp_ref, dm_ref, gain_ref, ws_ref, bt_ref, probs_ref, probs_t_ref, share_ref, guv_ref, dgelu_ref,
             bucket_ref, hn_ref,
             dz_ref, dws_ref, db_ref, dgain_ref, dsink_ref, drel_ref, dwin_ref,
             wt_ref, wtt_ref, dbias_ref, dsv_ref, carry_ref):
        n = pl.program_id(0)

        @pl.when(n == 0)
        def _():
            _fill_tril(ws_ref, wt_ref, wtt_ref)
            dwin_ref[...] = jnp.zeros_like(dwin_ref)
            dbias_ref[...] = jnp.zeros_like(dbias_ref)
            dsv_ref[...] = jnp.zeros_like(dsv_ref)
            dws_ref[...] = jnp.zeros_like(dws_ref)
            dgain_ref[...] = jnp.zeros_like(dgain_ref)
            dsink_ref[...] = jnp.zeros_like(dsink_ref)

        rows = pl.ds(pl.multiple_of(n * CHUNK, CHUNK), CHUNK)

        guv = guv_ref[...]
        dgelu = dgelu_ref[...]
        for g in range(N_GROUP):
            lo, hi = 128 * g, 128 * (g + 1)
            u = guv[:, lo:hi]
            vg = guv[:, 512 + lo:512 + hi]
            rr = _rms_scale(vg)
            vhat = vg * rr
            gain = gain_ref[:, lo:hi]
            vnb = (vhat * gain).astype(BF16)
            sv = _dot(wt_ref[g], vnb) + bt_ref[:, g:g + 1]
            da = dm_ref[:, lo:hi]
            dsv = da * u
            dsvb = dsv.astype(BF16)
            dsv_ref[g] += dsv
            dws_ref[g] += _dot_nt(dsvb, vnb)
            dvn = _dot(wtt_ref[g], dsvb)
            dgain_ref[:, lo:hi] += jnp.sum(dvn * vhat, axis=0, keepdims=True)
            dvg = _rms_bwd(dvn * gain, vhat, rr)
            dz_ref[rows, lo:hi] = (da * sv * dgelu[:, lo:hi]).astype(BF16)
            dz_ref[rows, 512 + lo:512 + hi] = (dvg * dgelu[:, 512 + lo:512 + hi]).astype(BF16)

        k_same, k_swap, v_same, v_swap = _kv_layouts(kvp_ref[...], z_ref[:, 1536:1792])
        lane_half = lax.broadcasted_iota(jnp.int32, (1, 128), 1) // 64
        zero = jnp.zeros((2 * CHUNK, 128), F32)
        dk_same, dk_swap, dv_same, dv_swap = zero, zero, zero, zero
        for pair in range(N_HEAD // 2):
            cols = slice(1024 + 128 * pair, 1024 + 128 * (pair + 1))
            qq = z_ref[:, cols]
            do_pair = dm_ref[:, 512 + 128 * pair:512 + 128 * (pair + 1)]
            dq = jnp.zeros((CHUNK, 128), F32)
            for pos in range(2):
                h = 2 * pair + pos
                _, _, same = _head_place(h)
                on_half = lane_half == pos
                qm = jnp.where(on_half, qq, 0.0).astype(BF16)
                k_use = k_same if same else k_swap
                v_use = v_same if same else v_swap
                pb = probs_ref[0, h]
                p = pb.astype(F32)
                p_sink = share_ref[:, h:h + 1]
                dom = jnp.where(on_half, do_pair, 0.0).astype(BF16)
                dp = _dot_nt(dom, v_use)
                dsum = jnp.sum(p * dp, axis=-1, keepdims=True)
                ds = p * (dp - dsum)
                dbias_ref[h] += ds
                dsink_ref[h:h + 1, :] += jnp.broadcast_to(jnp.sum(-p_sink * dsum, axis=0, keepdims=True), (1, 128))
                dsb = ds.astype(BF16)
                dq = dq + jnp.where(on_half, _dot(dsb, k_use), 0.0)
                dk_h = _dot_tn(dsb, qm)
                dv_h = _dot(probs_t_ref[0, h], dom)
                if same:
                    dk_same, dv_same = dk_same + dk_h, dv_same + dv_h
                else:
                    dk_swap, dv_swap = dk_swap + dk_h, dv_swap + dv_h
            dz_ref[rows, cols] = (dq * QK_SCALE).astype(BF16)
        dk = (dk_same + pltpu.roll(dk_swap, 64, axis=1)) * QK_SCALE
        dv = dv_same + pltpu.roll(dv_swap, 64, axis=1)
        dkv = jnp.concatenate([dk, dv], axis=1)

        @pl.when(n > 0)
        def _():
            prev_rows = pl.ds(pl.multiple_of((n - 1) * CHUNK, CHUNK), CHUNK)
            dz_ref[prev_rows, 1536:1792] = (carry_ref[...] + dkv[:CHUNK]).astype(BF16)

        carry_ref[...] = dkv[CHUNK:]

        @pl.when((n > 0) & (n % IN_GROUP == 0))
        def _():
            done = pl.ds(pl.multiple_of((n - IN_GROUP) * CHUNK, IN_GROUP * CHUNK), IN_GROUP * CHUNK)
            dwin_ref[...] += _dot_tn(dz_ref[done, :], hn_ref[...])

        @pl.when(n == N_BLOCK - 1)
        def _():
            dz_ref[rows, 1536:1792] = dkv[CHUNK:].astype(BF16)
            last = pl.ds((N_BLOCK - IN_GROUP) * CHUNK, IN_GROUP * CHUNK)
            dwin_ref[...] += _dot_tn(dz_ref[last, :], hn_ref[...])
            r = lax.broadcasted_iota(jnp.int32, (CHUNK, CHUNK), 0)
            c = lax.broadcasted_iota(jnp.int32, (CHUNK, CHUNK), 1)
            for g in range(N_GROUP):
                dws_ref[g] = jnp.where(c <= r, dws_ref[g], 0.0)
                db_ref[g] = jnp.sum(dsv_ref[g], axis=1, keepdims=True)
            bucket = bucket_ref[...]
            for h in range(N_HEAD):
                dbh = dbias_ref[h]
                per_bucket = [jnp.sum(jnp.where(bucket == b, dbh, 0.0), axis=0, keepdims=True) for b in range(N_BUCKET)]
                drel_ref[h] = jnp.sum(jnp.concatenate(per_bucket, axis=0), axis=1, keepdims=True)

    def hn_group(n):
        return jnp.where(n == N_BLOCK - 1, N_BLOCK // IN_GROUP - 1, jnp.maximum(n // IN_GROUP - 1, 0))

    return _call(
        body, (z, z, dmix, v_gain, w_spatial, b_spatial_t, *saved, bucket, hn1), grid=(N_BLOCK,),
        name="mixer_bwd",
        in_specs=[pl.BlockSpec((CHUNK, D_IN), lambda n: (n, 0)),
                  pl.BlockSpec((CHUNK, 256), lambda n: (jnp.maximum(n - 1, 0), 6)),
                  pl.BlockSpec((CHUNK, D), lambda n: (n, 0)),
                  pl.BlockSpec((1, 512), lambda n: (0, 0)),
                  pl.BlockSpec((N_GROUP, CHUNK, CHUNK), lambda n: (0, 0, 0)),
                  pl.BlockSpec((CHUNK, N_GROUP), lambda n: (0, 0)),
                  pl.BlockSpec((1, N_HEAD, CHUNK, 2 * CHUNK), lambda n: (n, 0, 0, 0)),
                  pl.BlockSpec((1, N_HEAD, 2 * CHUNK, CHUNK), lambda n: (n, 0, 0, 0)),
                  pl.BlockSpec((CHUNK, 128), lambda n: (n, 0)),
                  pl.BlockSpec((CHUNK, 1024), lambda n: (n, 0)), pl.BlockSpec((CHUNK, 1024), lambda n: (n, 0)),
                  pl.BlockSpec((CHUNK, 2 * CHUNK), lambda n: (0, 0)),
                  pl.BlockSpec((IN_GROUP * CHUNK, D), lambda n: (hn_group(n), 0))],
        out_specs=[pl.BlockSpec((S, D_IN), lambda n: (0, 0)),
                   pl.BlockSpec((N_GROUP, CHUNK, CHUNK), lambda n: (0, 0, 0)),
                   pl.BlockSpec((N_GROUP, CHUNK, 1), lambda n: (0, 0, 0)),
                   pl.BlockSpec((1, 512), lambda n: (0, 0)),
                   pl.BlockSpec((N_HEAD, 128), lambda n: (0, 0)),
                   pl.BlockSpec((N_HEAD, N_BUCKET, 1), lambda n: (0, 0, 0)),
                   pl.BlockSpec((D_IN, D), lambda n: (0, 0))],
        out_shape=[jax.ShapeDtypeStruct((S, D_IN), BF16), jax.ShapeDtypeStruct((N_GROUP, CHUNK, CHUNK), F32),
                   jax.ShapeDtypeStruct((N_GROUP, CHUNK, 1), F32), jax.ShapeDtypeStruct((1, 512), F32),
                   jax.ShapeDtypeStruct((N_HEAD, 128), F32), jax.ShapeDtypeStruct((N_HEAD, N_BUCKET, 1), F32),
                   jax.ShapeDtypeStruct((D_IN, D), F32)],
        scratch_shapes=[pltpu.VMEM((N_GROUP, CHUNK, CHUNK), BF16),
                        pltpu.VMEM((N_GROUP, CHUNK, CHUNK), BF16), pltpu.VMEM((N_HEAD, CHUNK, 2 * CHUNK), F32),
                        pltpu.VMEM((N_GROUP, CHUNK, CHUNK), F32), pltpu.VMEM((CHUNK, 256), F32)],
        compiler_params=_params(56), exchange=exchange)


def _in_bwd_input(dz, w_in_t, x, dh1, gain1, exchange=None):
    tm = 512

    def body(dz_ref, w_ref, x_ref, dh1_ref, g_ref, dx_ref, dg_ref):
        i = pl.program_id(0)
        dhn = _dot(dz_ref[...], w_ref[...])
        xv = x_ref[...]
        r1 = _rms_scale(xv)
        xhat = xv * r1
        dx_ref[...] = dh1_ref[...] + _rms_bwd(dhn * g_ref[...], xhat, r1)
        dg = jnp.sum(dhn * xhat, axis=0, keepdims=True)

        @pl.when(i == 0)
        def _():
            dg_ref[...] = dg

        @pl.when(i > 0)
        def _():
            dg_ref[...] += dg

    return _call(
        body, (dz, w_in_t, x, dh1, gain1), grid=(S // tm,), name="in_bwd_input",
        in_specs=[pl.BlockSpec((tm, D_IN), lambda i: (i, 0)), pl.BlockSpec((D_IN, D), lambda i: (0, 0)),
                  pl.BlockSpec((tm, D), lambda i: (i, 0)), pl.BlockSpec((tm, D), lambda i: (i, 0)),
                  pl.BlockSpec((1, D), lambda i: (0, 0))],
        out_specs=[pl.BlockSpec((tm, D), lambda i: (i, 0)), pl.BlockSpec((1, D), lambda i: (0, 0))],
        out_shape=[jax.ShapeDtypeStruct((S, D), F32), jax.ShapeDtypeStruct((1, D), F32)],
        compiler_params=_params(48), exchange=exchange)


def _rel_bucket():
    a = jnp.arange(CHUNK)[:, None]
    j = jnp.arange(2 * CHUNK)[None, :]
    n = jnp.maximum(CHUNK + a - j, 0)
    max_exact = N_BUCKET // 2
    nf = jnp.maximum(n, 1).astype(jnp.float32)
    large = max_exact + (jnp.log(nf / max_exact) / math.log(CHUNK / max_exact) * (N_BUCKET - max_exact)).astype(jnp.int32)
    large = jnp.minimum(large, N_BUCKET - 1)
    return jnp.where(n < max_exact, n, large).astype(jnp.int32)


def _step(x, p, target, small, bufs, place):
    bucket = _rel_bucket()
    sinks = small["attn_sinks"].reshape(N_HEAD)
    b_t = jnp.transpose(small["b_spatial"].reshape(N_GROUP, CHUNK))
    ws = small["w_spatial"].reshape(N_GROUP, CHUNK, CHUNK)
    gain1, gain2 = small["norm1_gain"], small["norm2_gain"]
    v_gain = small["gmlp_v_gain"]
    final_gain = small["final_gain"].reshape(1, D)
    table = small["rel_bias_table"]
    bufs = dict(bufs)

    def gather(*names):
        return _RelayGather([bufs[n] for n in names])

    def took(names, got):
        bufs.update(zip(names, got))

    w_in_t = _whole(bufs["w_in"]).reshape(D_IN, D)
    (z, hn1), got = _in_proj(x, gain1, w_in_t, gather("w_out"))
    took(["w_out"], got)
    (mix, *saved), got = _mixer_fwd(z, v_gain, ws, b_t, sinks, table, bucket, gather("w_ff1"))
    took(["w_ff1"], got)
    w_out = _whole(bufs["w_out"]).reshape(D, D)
    (h1, hn2, hn2_t), _ = _out_proj(x, mix, w_out, gain2)
    w_ff1 = _whole(bufs["w_ff1"])
    (r, a, a_t), got = _ffn_up(hn2, w_ff1, gather("w_ff2"))
    took(["w_ff2"], got)
    w_ff2 = _whole(bufs["w_ff2"])
    (h2,), got = _ffn_down(h1, a, w_ff2, gather("w_ple_gate", "w_ple_proj"))
    took(["w_ple_gate", "w_ple_proj"], got)
    dh2, d_gate, d_proj, d_final, sq, dh2b = _tail(h2, p, target, _whole(bufs["w_ple_gate"]).reshape(D, D),
                                                   _whole(bufs["w_ple_proj"]), final_gain)

    def pair_sums(halves, from_sibling):
        sums, landing = zip(*[_pair_sum(g, o, place) for g, o in zip(halves, from_sibling)])
        return list(sums), list(landing)

    landed = {}
    halves = [_halves(d_gate.reshape(N_CHIP, 256, D)), _halves(d_proj)]
    (df, d_ff2), got = _ffn_bwd_down(dh2b, r, a_t, w_ff2, _SiblingExchange(halves))
    ex, halves = _ChipExchange(*pair_sums(halves, got)), [_halves(d_ff2)]
    (d_ff1,), got = _ffn_bwd_up(df, hn2_t, _Both(ex, _SiblingExchange(halves)))
    landed.update(zip(["w_ple_gate", "w_ple_proj"], got[:2]))
    ex, halves = _ChipExchange(*pair_sums(halves, got[2:])), [_halves(d_ff1)]
    (dh1, dmix, d_out, d_gain2), got = _ffn_bwd_input(df, w_ff1, dh2, h1, gain2, mix, w_out,
                                                      _Both(ex, _SiblingExchange(halves)))
    landed["w_ff2"] = got[0]
    ex, halves = _ChipExchange(*pair_sums(halves, got[1:])), [_halves(d_out.reshape(N_CHIP, 256, D))]
    (dz, d_ws, d_b, d_vgain, d_sink, d_rel, d_in_t), got = _mixer_bwd(z, dmix, v_gain, ws, b_t, saved, bucket, hn1,
                                                                     _Both(ex, _SiblingExchange(halves)))
    landed["w_ff1"] = got[0]
    small_grads = {
        "gmlp_v_gain": d_vgain, "w_spatial": d_ws.reshape(1, N_GROUP, CHUNK, CHUNK),
        "b_spatial": d_b.reshape(1, N_GROUP, CHUNK), "attn_sinks": d_sink[:, 0].reshape(1, N_HEAD),
        "rel_bias_table": jnp.transpose(d_rel.reshape(N_HEAD, N_BUCKET)), "norm2_gain": d_gain2,
        "final_gain": d_final.reshape(D),
    }
    ex, halves = _ChipExchange(*pair_sums(halves, got[1:])), [_halves(d_in_t.reshape(N_CHIP, 448, D))]
    (dx, small_grads["norm1_gain"]), got = _in_bwd_input(dz, w_in_t, x, dh1, gain1, _Both(ex, _SiblingExchange(halves)))
    landed["w_out"] = got[0]
    return dx, landed, _ChipExchange(*pair_sums(halves, got[1:])), small_grads, sq


HBM_SPEC = pl.BlockSpec(memory_space=pltpu.HBM)
VMEM_SPEC = pl.BlockSpec(memory_space=pltpu.VMEM)


def _mesh_place():
    x, y, c = lax.axis_index("x"), lax.axis_index("y"), lax.axis_index("c")
    others = [(1 - x, y), (x, 1 - y), (1 - x, 1 - y)]
    return x, y, c, others


def _remote(src, dst, send_sem, recv_sem, device):
    return pltpu.make_async_remote_copy(src_ref=src, dst_ref=dst, send_sem=send_sem, recv_sem=recv_sem,
                                        device_id=device, device_id_type=MESH)


def _hbm_like(a, shape=None, dtype=None):
    return pltpu.HBM(a.shape if shape is None else shape, a.dtype if dtype is None else dtype)


def _gather_start(bufs, send_sems, recv_sems):
    x, y, c, others = _mesh_place()
    me = 2 * x + y
    for w, buf in enumerate(bufs):
        for k in range(3):
            mine = buf.at[me, c]
            _remote(mine, mine, send_sems.at[w, k], recv_sems.at[w, k], (*others[k], c)).start()


def _gather_finish(bufs, send_sems, recv_sems):
    x, y, c, others = _mesh_place()
    me = 2 * x + y
    sibling = (x, y, 1 - c)
    idx = [2 * ox + oy for ox, oy in others]
    chips = range(3)
    for w, buf in enumerate(bufs):
        for k in chips:
            landed = buf.at[idx[k], c]
            _remote(landed, landed, send_sems.at[w, k], recv_sems.at[w, k], sibling).wait_recv()
            _remote(landed, landed, send_sems.at[w, 3 + k], recv_sems.at[w, 3 + k], sibling).start()
    for w, buf in enumerate(bufs):
        for k in chips:
            landed = buf.at[idx[k], 1 - c]
            _remote(landed, landed, send_sems.at[w, 3 + k], recv_sems.at[w, 3 + k], sibling).wait_recv()
    for w, buf in enumerate(bufs):
        for k in chips:
            mine, passed = buf.at[me, c], buf.at[idx[k], c]
            _remote(mine, mine, send_sems.at[w, k], recv_sems.at[w, k], sibling).wait_send()
            _remote(passed, passed, send_sems.at[w, 3 + k], recv_sems.at[w, 3 + k], sibling).wait_send()


def _gather_sems(n):
    return [pltpu.SemaphoreType.DMA((n, 6)), pltpu.SemaphoreType.DMA((n, 6))]


def _sibling_copies(grads, landing, send_sems, recv_sems):
    x, y, c, _ = _mesh_place()
    return [_remote(grads[w].at[j, 1 - c], landing[w].at[j], send_sems.at[w, j], recv_sems.at[w, j], (x, y, 1 - c))
            for w in range(len(grads)) for j in range(N_CHIP)]


def _sibling_exchange_start(grads, landing, send_sems, recv_sems):
    for cp in _sibling_copies(grads, landing, send_sems, recv_sems):
        cp.start()


def _sibling_exchange_finish(grads, landing, send_sems, recv_sems):
    copies = _sibling_copies(grads, landing, send_sems, recv_sems)
    for cp in copies:
        cp.wait_recv()
    for cp in copies:
        cp.wait_send()


def _sibling_exchange_sems(n):
    return [pltpu.SemaphoreType.DMA((n, N_CHIP)), pltpu.SemaphoreType.DMA((n, N_CHIP))]


def _sibling_exchange(grads):
    n = len(grads)

    def body(*refs):
        ins, outs = refs[:n], refs[n:2 * n]
        _sibling_exchange_start(ins, outs, *refs[2 * n:])
        _sibling_exchange_finish(ins, outs, *refs[2 * n:])

    return pl.pallas_call(
        body, name="sibling_exchange",
        in_specs=[HBM_SPEC] * n, out_specs=[HBM_SPEC] * n,
        out_shape=[_hbm_like(g, (N_CHIP,) + g.shape[2:]) for g in grads],
        scratch_shapes=_sibling_exchange_sems(n),
    )(*[_in_hbm(g) for g in grads])


def _chip_exchange_start(sums, landing, send_sems, recv_sems):
    x, y, c, others = _mesh_place()
    me = 2 * x + y
    for w in range(len(sums)):
        for k, (ox, oy) in enumerate(others):
            _remote(sums[w].at[2 * ox + oy], landing[w].at[me], send_sems.at[w, k], recv_sems.at[w, k],
                    (ox, oy, c)).start()


def _chip_exchange_finish(sums, landing, send_sems, recv_sems):
    x, y, c, others = _mesh_place()
    for w in range(len(sums)):
        for k, (ox, oy) in enumerate(others):
            piece = landing[w].at[2 * ox + oy]
            _remote(piece, piece, send_sems.at[w, k], recv_sems.at[w, k], (x, y, c)).wait_recv()
    for w in range(len(sums)):
        for k, (ox, oy) in enumerate(others):
            piece = sums[w].at[2 * ox + oy]
            _remote(piece, piece, send_sems.at[w, k], recv_sems.at[w, k], (x, y, c)).wait_send()


def _chip_exchange_sems(n):
    return [pltpu.SemaphoreType.DMA((n, 3)), pltpu.SemaphoreType.DMA((n, 3))]


def _sibling_allgather(bufs, also):
    n = len(bufs)
    k_in, k_out = len(also.operands), also.n_out

    def body(*refs):
        ex_ins, refs = refs[n:n + k_in], refs[n + k_in:]
        outs, refs = refs[:n], refs[n:]
        ex_outs, refs = refs[:k_out], refs[k_out:]
        send_sems, recv_sems, ex_sems = refs[0], refs[1], refs[2:]
        x, y, c, _ = _mesh_place()
        sibling = (x, y, 1 - c)
        also.start(ex_ins, ex_outs, ex_sems)
        sends = [_remote(outs[w].at[c], outs[w].at[c], send_sems.at[w], recv_sems.at[w], sibling) for w in range(n)]
        for cp in sends:
            cp.start()
        for w in range(n):
            landed = outs[w].at[1 - c]
            _remote(landed, landed, send_sems.at[w], recv_sems.at[w], sibling).wait_recv()
        for cp in sends:
            cp.wait_send()
        also.finish(ex_ins, ex_outs, ex_sems)

    res = pl.pallas_call(
        body, name="sibling_allgather",
        in_specs=[HBM_SPEC] * (n + k_in), out_specs=[HBM_SPEC] * (n + k_out),
        out_shape=[_hbm_like(b) for b in bufs] + also.out_shape,
        input_output_aliases={**{w: w for w in range(n)}, **{n + i: n + o for i, o in also.aliases.items()}},
        scratch_shapes=[pltpu.SemaphoreType.DMA((n,)), pltpu.SemaphoreType.DMA((n,))] + also.sems,
    )(*bufs, *[_in_hbm(o) for o in also.operands])
    return list(res[:n]), list(res[n:])


def _pair_sum(grad, other, place):
    _, _, h, cols = grad.shape
    tr = _row_tile(h)

    def body(place_ref, g_ref, o_ref, sums_ref, own_ref):
        s = (g_ref[0, 0] + o_ref[0]).astype(BF16)
        sums_ref[0] = s

        @pl.when(pl.program_id(1) == place_ref[0])
        def _():
            own_ref[0] = s

    return pl.pallas_call(
        body, name="pair_sum",
        grid_spec=pltpu.PrefetchScalarGridSpec(
            num_scalar_prefetch=1, grid=(h // tr, N_CHIP),
            in_specs=[pl.BlockSpec((1, 1, tr, cols), lambda r, j, place_ref: (j, place_ref[1], r, 0)),
                      pl.BlockSpec((1, tr, cols), lambda r, j, place_ref: (j, r, 0))],
            out_specs=[pl.BlockSpec((1, tr, cols), lambda r, j, place_ref: (j, r, 0)),
                       pl.BlockSpec((1, tr, cols), lambda r, j, place_ref: (place_ref[0], r, 0))]),
        out_shape=[pltpu.HBM((N_CHIP, h, cols), BF16)] * 2,
        compiler_params=_params(32, 2),
    )(place, _in_hbm(grad), _in_hbm(other))


def _chip_sum(parts, place):
    _, h, cols = parts.shape
    tr = _row_tile(h)

    def body(place_ref, p_ref, out_ref):
        out_ref[0] = ((p_ref[0].astype(F32) + p_ref[1].astype(F32)) + p_ref[2].astype(F32)) + p_ref[3].astype(F32)

    return pl.pallas_call(
        body, name="chip_sum",
        grid_spec=pltpu.PrefetchScalarGridSpec(
            num_scalar_prefetch=1, grid=(h // tr,),
            in_specs=[pl.BlockSpec((N_CHIP, tr, cols), lambda r, place_ref: (0, r, 0))],
            out_specs=pl.BlockSpec((1, tr, cols), lambda r, place_ref: (place_ref[1], r, 0))),
        out_shape=pltpu.HBM((2, h, cols), F32),
        compiler_params=_params(32),
    )(place, _in_hbm(parts))


def _adamw_math(w, g, m, v):
    m = ADAM_B1 * m + (1.0 - ADAM_B1) * g
    v = ADAM_B2 * v + (1.0 - ADAM_B2) * (g * g)
    m_hat = m / (1.0 - ADAM_B1 ** ADAM_STEP)
    v_hat = v / (1.0 - ADAM_B2 ** ADAM_STEP)
    delta = -ADAM_LR * (m_hat / (jnp.sqrt(v_hat) + ADAM_EPS) + ADAM_WD * w)
    return delta, m, v


def _adamw(w, g, m, v, exchange=None):
    rows, cols = w.shape
    tr = _row_tile(rows)

    def body(w_ref, g_ref, m_ref, v_ref, d_ref, nm_ref, nv_ref, g_out_ref):
        g = g_ref[...]
        d_ref[...], nm_ref[...], nv_ref[...] = _adamw_math(w_ref[...], g, m_ref[...], v_ref[...])
        g_out_ref[...] = g

    spec = pl.BlockSpec((tr, cols), lambda r: (r, 0))
    return _call(
        body, (w, g, m, v), grid=(rows // tr,), name="adamw",
        in_specs=[spec] * 4, out_specs=[spec] * 4,
        out_shape=[jax.ShapeDtypeStruct((rows, cols), F32)] * 4,
        compiler_params=_params(48), exchange=exchange)


SMALL_NAMES = ("norm1_gain", "gmlp_v_gain", "w_spatial", "b_spatial", "attn_sinks", "rel_bias_table", "norm2_gain",
               "final_gain")
PACK_TILE = 8 * 128


def _pack_small(arrays):
    parts = []
    for a in arrays:
        flat = a.reshape(-1)
        rows = -(-flat.shape[0] // PACK_TILE) * 8
        parts.append(jnp.pad(flat, (0, rows * 128 - flat.shape[0])).reshape(rows, 128))
    return jnp.concatenate(parts, axis=0)


def _unpack_small(packed, like):
    out, row = [], 0
    for a in like:
        size = math.prod(a.shape)
        rows = -(-size // PACK_TILE) * 8
        out.append(packed[row:row + rows].reshape(-1)[:size].reshape(a.shape))
        row += rows
    return out


def _small_update(gathered, w, m, v):
    rows = gathered.shape[1]

    def body(g_ref, w_ref, m_ref, v_ref, tot_ref, d_ref, nm_ref, nv_ref):
        total = g_ref[0].astype(F32)
        for dev in range(1, 8):
            total = total + g_ref[dev].astype(F32)
        tot_ref[...] = total
        d_ref[...], nm_ref[...], nv_ref[...] = _adamw_math(w_ref[...], total, m_ref[...], v_ref[...])

    return pl.pallas_call(
        body, name="small_update",
        in_specs=[VMEM_SPEC] * 4, out_specs=[VMEM_SPEC] * 4,
        out_shape=[jax.ShapeDtypeStruct((rows, 128), F32)] * 4,
        compiler_params=pltpu.CompilerParams(vmem_limit_bytes=24 * MIB),
    )(gathered, w, m, v)


def _halves(a):
    return a.reshape(a.shape[:-2] + (2, a.shape[-2] // 2, a.shape[-1]))


def _whole(a):
    return a.reshape(a.shape[:-3] + (2 * a.shape[-2], a.shape[-1]))


def kernel(x, p, norm1_gain, w_in, gmlp_v_gain, w_spatial, b_spatial, attn_sinks, rel_bias_table, w_out, norm2_gain, w_ff1, w_ff2, w_ple_proj, w_ple_gate, final_gain, loss_target, m_norm1_gain, m_w_in, m_gmlp_v_gain, m_w_spatial, m_b_spatial, m_attn_sinks, m_rel_bias_table, m_w_out, m_norm2_gain, m_w_ff1, m_w_ff2, m_w_ple_proj, m_w_ple_gate, m_final_gain, v_norm1_gain, v_w_in, v_gmlp_v_gain, v_w_spatial, v_b_spatial, v_attn_sinks, v_rel_bias_table, v_w_out, v_norm2_gain, v_w_ff1, v_w_ff2, v_w_ple_proj, v_w_ple_gate, v_final_gain):
    given = dict(locals())
    small = {n: given[n] for n in SMALL_NAMES}
    chip = 2 * lax.axis_index("x") + lax.axis_index("y")
    place = jnp.stack([chip, lax.axis_index("c")]).astype(jnp.int32)

    big_names = ("w_in", "w_out", "w_ff1", "w_ff2", "w_ple_proj", "w_ple_gate")
    shards = {n: given[n][0] for n in big_names}
    travel = dict(shards, w_in=jnp.transpose(shards["w_in"]))
    rest = [n for n in big_names if n != "w_in"]
    cast, gathered = _cast_shards_beside_gather([travel[n] for n in rest], place[:1],
                                                [_cast_shard(travel["w_in"], place[:1])])
    bufs = dict(zip(rest + ["w_in"], cast + gathered))
    dx, landed, exchange_in, small_grads, sq = _step(x[0], p[0, 0], loss_target[0], small, bufs, place)

    out_grad, out_delta, out_m, out_v = {}, {}, {}, {}

    def update(n, g, exchange=None):
        to = jnp.transpose if n == "w_in" else (lambda a: a)
        (delta, new_m, new_v, g_out), got = _adamw(to(shards[n]), g, to(given["m_" + n][0]), to(given["v_" + n][0]),
                                                   exchange)
        out_grad[n], out_delta[n], out_m[n], out_v[n] = [to(a)[None] for a in (g_out, delta, new_m, new_v)]
        return got

    spare = jnp.zeros((8, 128), F32)
    small_packed = _pack_small([small_grads[n] for n in SMALL_NAMES] + [spare]).astype(BF16)
    early = [n for n in big_names if n != "w_in"]
    reduced, (small_gathered, sq_gathered, landed_in) = _sibling_allgather(
        [_chip_sum(landed[n], place) for n in early], _Both(_Both(_GatherAll(small_packed), _GatherAll(sq)), exchange_in))
    for n, r in zip(early, reduced):
        update(n, _whole(r))
    (reduced_in,), _ = _sibling_allgather([_chip_sum(landed_in, place)], _Nothing())
    update("w_in", _whole(reduced_in))

    like = [given[n] for n in SMALL_NAMES] + [spare]
    packed = _small_update(small_gathered, *[_pack_small([given[pre + n] for n in SMALL_NAMES] + [spare])
                                             for pre in ("", "m_", "v_")])
    for res, out in zip(packed, (out_grad, out_delta, out_m, out_v)):
        out.update(zip(SMALL_NAMES, _unpack_small(res, like)))
    loss = 0.5 * jnp.sum(sq_gathered[:, 0, 0]) / D

    order = ("norm1_gain", "w_in", "gmlp_v_gain", "w_spatial", "b_spatial", "attn_sinks", "rel_bias_table", "w_out",
             "norm2_gain", "w_ff1", "w_ff2", "w_ple_proj", "w_ple_gate", "final_gain")
    return (loss, dx[None], *[out_grad[n] for n in order], *[out_delta[n] for n in order],
            *[out_m[n] for n in order], *[out_v[n] for n in order])
```

```python
import functools
import math

import jax
import jax.numpy as jnp
from jax import lax
from jax.experimental import pallas as pl
from jax.experimental.pallas import tpu as pltpu

S = 2048
D = 1024
D_IN = 1792
D_FF = 4096
PLE = 256
N_CHIP = 4
N_GROUP = 4
CHUNK = 128
N_HEAD = 8
N_BLOCK = S // CHUNK
N_BUCKET = 32
EPS = 1e-6
NEG_INF = -1e30
QK_SCALE = 0.125
GELU_C = math.sqrt(2.0 / math.pi)

ADAM_LR = 0.001
ADAM_B1 = 0.9
ADAM_B2 = 0.999
ADAM_EPS = 1e-08
ADAM_WD = 0.01
ADAM_STEP = 10

F32 = jnp.float32
BF16 = jnp.bfloat16
MIB = 1024 * 1024
MESH = pl.DeviceIdType.MESH

NT = (((1,), (1,)), ((), ()))
TN = (((0,), (0,)), ((), ()))


def _dot(a, b):
    return jnp.dot(a, b, preferred_element_type=F32)


def _dot_nt(a, b):
    return lax.dot_general(a, b, NT, preferred_element_type=F32)


def _dot_tn(a, b):
    return lax.dot_general(a, b, TN, preferred_element_type=F32)


def _params(vmem_mib, n_axes=1):
    return pltpu.CompilerParams(dimension_semantics=("arbitrary",) * n_axes, vmem_limit_bytes=vmem_mib * MIB)


def _rms_scale(v):
    return lax.rsqrt(jnp.mean(v * v, axis=-1, keepdims=True) + EPS)


def _rms_bwd(dy_gain, xhat, r):
    return r * (dy_gain - xhat * jnp.mean(dy_gain * xhat, axis=-1, keepdims=True))


class _Gather:
    def __init__(self, bufs):
        self.operands = list(bufs)
        self.n_out = len(self.operands)
        self.out_shape = [_hbm_like(b) for b in bufs]
        self.aliases = {w: w for w in range(self.n_out)}
        self.sems = _gather_sems(self.n_out)

    def start(self, ins, outs, sems):
        _gather_start(outs, *sems)

    def finish(self, ins, outs, sems):
        _gather_finish(outs, *sems)


class _RelayGather(_Gather):
    TOP, BOTTOM = 6, 7
    DIAGONAL_PASSED = 5
    MIDDLE_AT, LATE_AT = (5, 8), (7, 8)

    def __init__(self, bufs):
        super().__init__(bufs)
        self.sems = [pltpu.SemaphoreType.DMA((self.n_out, 8)), pltpu.SemaphoreType.DMA((self.n_out, 8))]

    def _copies(self, bufs, send_sems, recv_sems):
        x, y, c, others = _mesh_place()
        me = 2 * x + y
        idx = [2 * ox + oy for ox, oy in others]
        sibling = (x, y, 1 - c)
        direct, passed, relayed = [], [], []
        for w, buf in enumerate(bufs):
            rows = buf.shape[2] // 2
            upper, lower = pl.ds(0, rows), pl.ds(rows, rows)
            for k in (0, 1):
                mine = buf.at[me, c]
                direct.append((_remote(mine, mine, send_sems.at[w, k], recv_sems.at[w, k], (*others[k], c)),
                               buf.at[idx[k], c], w, k))
            for k in (0, 1, 2):
                here = buf.at[idx[k], c]
                passed.append((_remote(here, here, send_sems.at[w, 3 + k], recv_sems.at[w, 3 + k], sibling),
                               buf.at[idx[k], 1 - c], w, 3 + k))
            from_x, from_y = buf.at[idx[0], c, upper], buf.at[idx[1], c, lower]
            relayed.append((_remote(from_x, from_x, send_sems.at[w, self.TOP], recv_sems.at[w, self.TOP],
                                    (*others[1], c)), buf.at[idx[2], c, upper], w, self.TOP))
            relayed.append((_remote(from_y, from_y, send_sems.at[w, self.BOTTOM], recv_sems.at[w, self.BOTTOM],
                                    (*others[0], c)), buf.at[idx[2], c, lower], w, self.BOTTOM))
        return direct, passed, relayed

    @staticmethod
    def _landed(piece, send_sems, recv_sems, w, col):
        x, y, c, _ = _mesh_place()
        _remote(piece, piece, send_sems.at[w, col], recv_sems.at[w, col], (x, y, c)).wait_recv()

    def start(self, ins, outs, sems):
        for cp, _, _, _ in self._copies(outs, *sems)[0]:
            cp.start()

    def middle(self, ins, outs, sems):
        direct, passed, relayed = self._copies(outs, *sems)
        for _, piece, w, col in direct:
            self._landed(piece, *sems, w, col)
        for cp, _, _, col in passed:
            if col != self.DIAGONAL_PASSED:
                cp.start()
        for cp, _, _, _ in relayed:
            cp.start()

    def late(self, ins, outs, sems):
        direct, passed, relayed = self._copies(outs, *sems)
        for _, piece, w, col in relayed:
            self._landed(piece, *sems, w, col)
        for cp, _, _, col in passed:
            if col == self.DIAGONAL_PASSED:
                cp.start()

    def finish(self, ins, outs, sems):
        direct, passed, relayed = self._copies(outs, *sems)
        for _, piece, w, col in passed:
            self._landed(piece, *sems, w, col)
        for cp, _, _, _ in direct + passed + relayed:
            cp.wait_send()


class _ChipExchange:
    def __init__(self, sums, landing):
        self.n_out = len(landing)
        self.operands = list(sums) + list(landing)
        self.out_shape = [_hbm_like(b) for b in landing]
        self.aliases = {self.n_out + w: w for w in range(self.n_out)}
        self.sems = _chip_exchange_sems(self.n_out)

    def start(self, ins, outs, sems):
        _chip_exchange_start(ins[:self.n_out], outs, *sems)

    def finish(self, ins, outs, sems):
        _chip_exchange_finish(ins[:self.n_out], outs, *sems)


class _GatherAll:
    def __init__(self, packed):
        self.operands = [packed]
        self.n_out = 1
        self.out_shape = [_hbm_like(packed, (8,) + packed.shape)]
        self.aliases = {}
        self.sems = [pltpu.SemaphoreType.DMA((8,)), pltpu.SemaphoreType.DMA((8,))]

    def _copies(self, ins, outs, sems):
        x, y, c, _ = _mesh_place()
        me = 4 * x + 2 * y + c
        send_sems, recv_sems = sems
        copies = []
        for k in range(1, 8):
            peer = (1 - x if k // 4 else x, 1 - y if (k // 2) % 2 else y, 1 - c if k % 2 else c)
            src = 4 * peer[0] + 2 * peer[1] + peer[2]
            copies.append((_remote(ins[0], outs[0].at[me], send_sems.at[k], recv_sems.at[k], peer), outs[0].at[src]))
        own = pltpu.make_async_copy(ins[0], outs[0].at[me], send_sems.at[0])
        return own, copies

    def start(self, ins, outs, sems):
        own, copies = self._copies(ins, outs, sems)
        own.start()
        for cp, _ in copies:
            cp.start()

    def finish(self, ins, outs, sems):
        own, copies = self._copies(ins, outs, sems)
        x, y, c, _ = _mesh_place()
        for k, (cp, landed) in enumerate(copies):
            _remote(landed, landed, sems[0].at[k + 1], sems[1].at[k + 1], (x, y, c)).wait_recv()
        for cp, _ in copies:
            cp.wait_send()
        own.wait()


class _Nothing:
    operands, n_out, out_shape, aliases, sems = [], 0, [], {}, []

    def start(self, ins, outs, sems):
        pass

    def finish(self, ins, outs, sems):
        pass


class _Both:
    def __init__(self, a, b):
        self.a, self.b = a, b
        self.operands = a.operands + b.operands
        self.n_out = a.n_out + b.n_out
        self.out_shape = a.out_shape + b.out_shape
        self.aliases = dict(a.aliases)
        self.aliases.update({len(a.operands) + i: a.n_out + o for i, o in b.aliases.items()})
        self.sems = a.sems + b.sems

    def _split(self, ins, outs, sems):
        ka, na, sa = len(self.a.operands), self.a.n_out, len(self.a.sems)
        return (ins[:ka], outs[:na], sems[:sa]), (ins[ka:], outs[na:], sems[sa:])

    def start(self, ins, outs, sems):
        for ex, args in zip((self.a, self.b), self._split(ins, outs, sems)):
            ex.start(*args)

    def finish(self, ins, outs, sems):
        for ex, args in zip((self.a, self.b), self._split(ins, outs, sems)):
            ex.finish(*args)


class _SiblingExchange:
    def __init__(self, grads):
        self.operands = list(grads)
        self.n_out = len(self.operands)
        self.out_shape = [_hbm_like(g, (N_CHIP,) + g.shape[2:]) for g in grads]
        self.aliases = {}
        self.sems = _sibling_exchange_sems(self.n_out)

    def start(self, ins, outs, sems):
        _sibling_exchange_start(ins, outs, *sems)

    def finish(self, ins, outs, sems):
        _sibling_exchange_finish(ins, outs, *sems)


def _call(body, operands, *, grid, in_specs, out_specs, out_shape, name, compiler_params, scratch_shapes=(),
          exchange=None):
    operands = [o if getattr(spec, "memory_space", None) == pltpu.SMEM else _in_hbm(o)
                for o, spec in zip(operands, in_specs)]
    out_shape = [pltpu.HBM(s.shape, s.dtype) for s in out_shape]
    if exchange is None:
        res = pl.pallas_call(body, grid=grid, in_specs=in_specs, out_specs=out_specs, out_shape=out_shape, name=name,
                             scratch_shapes=list(scratch_shapes), compiler_params=compiler_params)(*operands)
        return list(res), []
    n_in, n_out, n_scr = len(in_specs), len(out_specs), len(scratch_shapes)
    k_in, k_out = len(exchange.operands), exchange.n_out

    def fused(*refs):
        ins, refs = refs[:n_in], refs[n_in:]
        ex_ins, refs = refs[:k_in], refs[k_in:]
        outs, refs = refs[:n_out], refs[n_out:]
        ex_outs, refs = refs[:k_out], refs[k_out:]
        scratch, sems = refs[:n_scr], refs[n_scr:]
        ids = [pl.program_id(a) for a in range(len(grid))]
        first = functools.reduce(jnp.logical_and, [i == 0 for i in ids])
        last = functools.reduce(jnp.logical_and, [i == g - 1 for i, g in zip(ids, grid)])

        @pl.when(first)
        def _():
            exchange.start(ex_ins, ex_outs, sems)

        def at_step(numerator, denominator):
            at = (numerator * math.prod(grid)) // denominator
            place = [(at // math.prod(grid[a + 1:])) % grid[a] for a in range(len(grid))]
            return functools.reduce(jnp.logical_and, [i == p for i, p in zip(ids, place)])

        if hasattr(exchange, "middle"):
            @pl.when(at_step(*exchange.MIDDLE_AT))
            def _():
                exchange.middle(ex_ins, ex_outs, sems)

            @pl.when(at_step(*exchange.LATE_AT))
            def _():
                exchange.late(ex_ins, ex_outs, sems)

        body(*ins, *outs, *scratch)

        @pl.when(last)
        def _():
            exchange.finish(ex_ins, ex_outs, sems)

    res = pl.pallas_call(
        fused, grid=grid, name=name,
        in_specs=list(in_specs) + [HBM_SPEC] * k_in, out_specs=list(out_specs) + [HBM_SPEC] * k_out,
        out_shape=list(out_shape) + exchange.out_shape,
        input_output_aliases={n_in + i: n_out + o for i, o in exchange.aliases.items()},
        scratch_shapes=list(scratch_shapes) + exchange.sems, compiler_params=compiler_params,
    )(*operands, *[_in_hbm(o) for o in exchange.operands])
    return list(res[:n_out]), list(res[n_out:])


def _in_hbm(a):
    return pltpu.with_memory_space_constraint(a, pltpu.HBM)


def _row_tile(h, most=512):
    return max(t for t in range(16, most + 1, 16) if h % t == 0)


def _cast_shard(a, chip):
    rows, cols = a.shape
    h = rows // 2
    tr = _row_tile(h)

    def body(chip_ref, a_ref, o_ref):
        o_ref[0, 0] = a_ref[0].astype(BF16)

    return pl.pallas_call(
        body, name="cast_shard",
        grid_spec=pltpu.PrefetchScalarGridSpec(
            num_scalar_prefetch=1, grid=(2, h // tr),
            in_specs=[pl.BlockSpec((1, tr, cols), lambda s, r, chip_ref: (s, r, 0))],
            out_specs=pl.BlockSpec((1, 1, tr, cols), lambda s, r, chip_ref: (chip_ref[0], s, r, 0))),
        out_shape=pltpu.HBM((N_CHIP, 2, h, cols), BF16),
        compiler_params=_params(16, 2),
    )(chip, _in_hbm(a.reshape(2, h, cols)))


def _cast_shards_beside_gather(arrays, chip, gathered):
    n, k = len(arrays), len(gathered)
    shapes = [(a.shape[0] // 2, a.shape[1]) for a in arrays]

    def body(chip_ref, *refs):
        ins, refs = refs[:n], refs[n + k:]
        outs, refs = refs[:n], refs[n:]
        bufs, sems = refs[:k], refs[k:]
        half = pl.program_id(0)

        @pl.when(half == 0)
        def _():
            _gather_start(bufs, *sems)

        for a_ref, o_ref in zip(ins, outs):
            o_ref[0, 0] = a_ref[0].astype(BF16)

        @pl.when(half == 1)
        def _():
            _gather_finish(bufs, *sems)

    res = pl.pallas_call(
        body, name="cast_shards",
        grid_spec=pltpu.PrefetchScalarGridSpec(
            num_scalar_prefetch=1, grid=(2,),
            in_specs=[pl.BlockSpec((1, h, c), lambda s, chip_ref: (s, 0, 0)) for h, c in shapes] + [HBM_SPEC] * k,
            out_specs=[pl.BlockSpec((1, 1, h, c), lambda s, chip_ref: (chip_ref[0], s, 0, 0)) for h, c in shapes]
            + [HBM_SPEC] * k,
            scratch_shapes=_gather_sems(k)),
        out_shape=[pltpu.HBM((N_CHIP, 2, h, c), BF16) for h, c in shapes] + [_hbm_like(b) for b in gathered],
        input_output_aliases={1 + n + i: n + i for i in range(k)},
        compiler_params=_params(32),
    )(chip, *[_in_hbm(a.reshape(2, h, c)) for a, (h, c) in zip(arrays, shapes)], *gathered)
    return list(res[:n]), list(res[n:])


def _in_proj(x, gain1, w_in_t, exchange=None):
    tm = 256

    def body(x_ref, g_ref, w_ref, z_ref, hn_ref):
        xv = x_ref[...]
        hn = (xv * _rms_scale(xv) * g_ref[...]).astype(BF16)
        hn_ref[...] = hn
        z_ref[...] = _dot_nt(hn, w_ref[...])

    return _call(
        body, (x, gain1, w_in_t), grid=(S // tm,), name="in_proj",
        in_specs=[pl.BlockSpec((tm, D), lambda i: (i, 0)), pl.BlockSpec((1, D), lambda i: (0, 0)),
                  pl.BlockSpec((D_IN, D), lambda i: (0, 0))],
        out_specs=[pl.BlockSpec((tm, D_IN), lambda i: (i, 0)), pl.BlockSpec((tm, D), lambda i: (i, 0))],
        out_shape=[jax.ShapeDtypeStruct((S, D_IN), F32), jax.ShapeDtypeStruct((S, D), BF16)],
        compiler_params=_params(40), exchange=exchange)


def _gelu_parts(v):
    t = jnp.tanh(GELU_C * (v + 0.044715 * (v * v * v)))
    cdf = 0.5 * (1.0 + t)
    return cdf, t


def _band_mask(n):
    a = lax.broadcasted_iota(jnp.int32, (CHUNK, 2 * CHUNK), 0)
    j = lax.broadcasted_iota(jnp.int32, (CHUNK, 2 * CHUNK), 1)
    dist = CHUNK + a - j
    valid = (dist >= 0) & (dist < CHUNK)
    return valid & ((n > 0) | (j >= CHUNK))


def _fill_bias(bucket_ref, table_ref, bias_ref):
    bucket = bucket_ref[...]
    for h in range(N_HEAD):
        acc = jnp.zeros((CHUNK, 2 * CHUNK), F32)
        for b in range(N_BUCKET):
            acc = jnp.where(bucket == b, table_ref[b, h], acc)
        bias_ref[h] = acc


def _fill_tril(ws_ref, wt_ref, wtt_ref=None):
    r = lax.broadcasted_iota(jnp.int32, (CHUNK, CHUNK), 0)
    c = lax.broadcasted_iota(jnp.int32, (CHUNK, CHUNK), 1)
    for g in range(N_GROUP):
        w = jnp.where(c <= r, ws_ref[g], 0.0)
        wt_ref[g] = w.astype(BF16)
        if wtt_ref is not None:
            wtt_ref[g] = w.T.astype(BF16)


def _kv_layouts(kv_prev, kv_cur):
    both = jnp.concatenate([kv_prev, kv_cur], axis=0)
    k = both[:, :128]
    v = both[:, 128:]
    return (k.astype(BF16), pltpu.roll(k, 64, axis=1).astype(BF16),
            v.astype(BF16), pltpu.roll(v, 64, axis=1).astype(BF16))


def _head_place(h):
    pair, pos, kvh = h // 2, h % 2, h // 4
    return pair, pos, kvh == pos


def _softmax_sink(qm, k_use, bias_h, sink, valid):
    s = _dot_nt(qm, k_use) * QK_SCALE + bias_h
    s = jnp.where(valid, s, NEG_INF)
    m = jnp.maximum(jnp.max(s, axis=-1, keepdims=True), sink)
    e = jnp.exp(s - m)
    es = jnp.exp(sink - m)
    inv = 1.0 / (jnp.sum(e, axis=-1, keepdims=True) + es)
    return e * inv, es * inv


def _mixer_fwd(z, v_gain, w_spatial, b_spatial_t, sinks, rel_table, bucket, exchange=None):
    def body(z_ref, kvp_ref, gain_ref, ws_ref, bt_ref, sink_ref, table_ref, bucket_ref, out_ref, probs_ref, probs_t_ref,
             share_ref, guv_ref, dgelu_ref, bias_ref, wt_ref):
        n = pl.program_id(0)

        @pl.when(n == 0)
        def _():
            _fill_bias(bucket_ref, table_ref, bias_ref)
            _fill_tril(ws_ref, wt_ref)

        zuv = z_ref[:, :1024]
        cdf, t = _gelu_parts(zuv)
        guv = zuv * cdf
        guv_ref[...] = guv
        dgelu_ref[...] = cdf + zuv * (0.5 * (1.0 - t * t)) * (GELU_C * (1.0 + 3.0 * 0.044715 * (zuv * zuv)))
        for g in range(N_GROUP):
            vg = guv[:, 512 + 128 * g:512 + 128 * (g + 1)]
            vn = vg * _rms_scale(vg) * gain_ref[:, 128 * g:128 * (g + 1)]
            sv = _dot(wt_ref[g], vn.astype(BF16)) + bt_ref[:, g:g + 1]
            out_ref[:, 128 * g:128 * (g + 1)] = (guv[:, 128 * g:128 * (g + 1)] * sv).astype(BF16)

        k_same, k_swap, v_same, v_swap = _kv_layouts(kvp_ref[...], z_ref[:, 1536:1792])
        valid = _band_mask(n)
        lane = lax.broadcasted_iota(jnp.int32, (1, 128), 1)
        lane_half = lane // 64
        shares = jnp.zeros((CHUNK, 128), F32)
        for pair in range(N_HEAD // 2):
            qq = z_ref[:, 1024 + 128 * pair:1024 + 128 * (pair + 1)]
            acc = jnp.zeros((CHUNK, 128), F32)
            for pos in range(2):
                h = 2 * pair + pos
                _, _, same = _head_place(h)
                qm = jnp.where(lane_half == pos, qq, 0.0).astype(BF16)
                p, p_sink = _softmax_sink(qm, k_same if same else k_swap, bias_ref[h], sink_ref[h], valid)
                pb = p.astype(BF16)
                probs_ref[0, h] = pb
                probs_t_ref[0, h] = p.T.astype(BF16)
                shares = jnp.where(lane == h, p_sink, shares)
                vm = jnp.where(lane_half == pos, v_same if same else v_swap, jnp.zeros((), BF16))
                acc = acc + _dot(pb, vm)
            out_ref[:, 512 + 128 * pair:512 + 128 * (pair + 1)] = acc.astype(BF16)
        share_ref[...] = shares

    return _call(
        body, (z, z, v_gain, w_spatial, b_spatial_t, sinks, rel_table, bucket), grid=(N_BLOCK,), name="mixer_fwd",
        in_specs=[pl.BlockSpec((CHUNK, D_IN), lambda n: (n, 0)),
                  pl.BlockSpec((CHUNK, 256), lambda n: (jnp.maximum(n - 1, 0), 6)),
                  pl.BlockSpec((1, 512), lambda n: (0, 0)),
                  pl.BlockSpec((N_GROUP, CHUNK, CHUNK), lambda n: (0, 0, 0)),
                  pl.BlockSpec((CHUNK, N_GROUP), lambda n: (0, 0)),
                  pl.BlockSpec(memory_space=pltpu.SMEM),
                  pl.BlockSpec(memory_space=pltpu.SMEM),
                  pl.BlockSpec((CHUNK, 2 * CHUNK), lambda n: (0, 0))],
        out_specs=[pl.BlockSpec((CHUNK, D), lambda n: (n, 0)),
                   pl.BlockSpec((1, N_HEAD, CHUNK, 2 * CHUNK), lambda n: (n, 0, 0, 0)),
                   pl.BlockSpec((1, N_HEAD, 2 * CHUNK, CHUNK), lambda n: (n, 0, 0, 0)),
                   pl.BlockSpec((CHUNK, 128), lambda n: (n, 0)),
                   pl.BlockSpec((CHUNK, 1024), lambda n: (n, 0)), pl.BlockSpec((CHUNK, 1024), lambda n: (n, 0))],
        out_shape=[jax.ShapeDtypeStruct((S, D), BF16), jax.ShapeDtypeStruct((N_BLOCK, N_HEAD, CHUNK, 2 * CHUNK), BF16),
                   jax.ShapeDtypeStruct((N_BLOCK, N_HEAD, 2 * CHUNK, CHUNK), BF16), jax.ShapeDtypeStruct((S, 128), F32),
                   jax.ShapeDtypeStruct((S, 1024), F32), jax.ShapeDtypeStruct((S, 1024), F32)],
        scratch_shapes=[pltpu.VMEM((N_HEAD, CHUNK, 2 * CHUNK), F32), pltpu.VMEM((N_GROUP, CHUNK, CHUNK), BF16)],
        compiler_params=_params(32), exchange=exchange)


def _out_proj(x, mix, w_out, gain2, exchange=None):
    tm = 256

    def body(x_ref, mix_ref, w_ref, g_ref, h1_ref, hn_ref, hnt_ref):
        h1 = x_ref[...] + _dot(mix_ref[...], w_ref[...])
        h1_ref[...] = h1
        hn = h1 * _rms_scale(h1) * g_ref[...]
        hn_ref[...] = hn.astype(BF16)
        hnt_ref[...] = hn.T.astype(BF16)

    return _call(
        body, (x, mix, w_out, gain2), grid=(S // tm,), name="out_proj",
        in_specs=[pl.BlockSpec((tm, D), lambda i: (i, 0)), pl.BlockSpec((tm, D), lambda i: (i, 0)),
                  pl.BlockSpec((D, D), lambda i: (0, 0)), pl.BlockSpec((1, D), lambda i: (0, 0))],
        out_specs=[pl.BlockSpec((tm, D), lambda i: (i, 0)), pl.BlockSpec((tm, D), lambda i: (i, 0)),
                   pl.BlockSpec((D, tm), lambda i: (0, i))],
        out_shape=[jax.ShapeDtypeStruct((S, D), F32), jax.ShapeDtypeStruct((S, D), BF16),
                   jax.ShapeDtypeStruct((D, S), BF16)],
        compiler_params=_params(32), exchange=exchange)


def _ffn_up(hn2, w_ff1, exchange=None):
    tm = 512
    nj = D_FF // 1024

    def body(hn_ref, w1_ref, r_ref, a_ref, at_ref):
        r = jnp.maximum(_dot(hn_ref[...], w1_ref[0]), 0.0)
        r_ref[...] = r.astype(BF16)
        a = r * r
        a_ref[...] = a.astype(BF16)
        at_ref[...] = a.T.astype(BF16)

    return _call(
        body, (hn2, w_ff1), grid=(nj, S // tm), name="ffn_up",
        in_specs=[pl.BlockSpec((tm, D), lambda j, i: (i, 0)), pl.BlockSpec((1, D, 1024), lambda j, i: (j, 0, 0))],
        out_specs=[pl.BlockSpec((tm, 1024), lambda j, i: (i, j)), pl.BlockSpec((tm, 1024), lambda j, i: (i, j)),
                   pl.BlockSpec((1024, tm), lambda j, i: (j, i))],
        out_shape=[jax.ShapeDtypeStruct((S, D_FF), BF16), jax.ShapeDtypeStruct((S, D_FF), BF16),
                   jax.ShapeDtypeStruct((D_FF, S), BF16)],
        compiler_params=_params(40, 2), exchange=exchange)


def _ffn_down(h1, a, w_ff2, exchange=None):
    tm = 512
    nj = D_FF // 1024

    def body(h1_ref, a_ref, w2_ref, h2_ref):
        h2_ref[...] = h1_ref[...] + _dot(a_ref[...], w2_ref[...].reshape(D_FF, D))

    return _call(
        body, (h1, a, w_ff2), grid=(S // tm,), name="ffn_down",
        in_specs=[pl.BlockSpec((tm, D), lambda i: (i, 0)), pl.BlockSpec((tm, D_FF), lambda i: (i, 0)),
                  pl.BlockSpec((nj, 1024, D), lambda i: (0, 0, 0), pipeline_mode=pl.Buffered(1))],
        out_specs=[pl.BlockSpec((tm, D), lambda i: (i, 0))],
        out_shape=[jax.ShapeDtypeStruct((S, D), F32)],
        compiler_params=_params(40), exchange=exchange)


def _tail(h2, p, target, w_gate, w_proj, final_gain):
    tm = 256
    steps = S // tm

    def body(h2_ref, p_ref, t_ref, wg_ref, wp_ref, gf_ref, dh2_ref, dwg_ref, dwp_ref, dgf_ref, loss_ref, dh2b_ref,
             dwp_acc):
        i = pl.program_id(0)
        h2 = h2_ref[...]
        h2b = h2.astype(BF16)
        pb = p_ref[...].astype(BF16)
        gate = jax.nn.sigmoid(_dot(h2b, wg_ref[...]))
        pp = jnp.concatenate([_dot(pb, wp_ref[j]) for j in range(N_CHIP)], axis=1)
        h3 = h2 + gate * pp
        r3 = _rms_scale(h3)
        xhat = h3 * r3
        gf = gf_ref[...]
        err = xhat * gf - t_ref[...]
        dy = err * (1.0 / D)
        dh3 = _rms_bwd(dy * gf, xhat, r3)
        dgp = (dh3 * pp * gate * (1.0 - gate)).astype(BF16)
        dpp = (dh3 * gate).astype(BF16)
        dh2 = dh3 + _dot_nt(dgp, wg_ref[...])
        dh2_ref[...] = dh2
        dh2b_ref[...] = dh2.astype(BF16)
        dwg = _dot_tn(h2b, dgp)
        dwp = _dot_tn(pb, dpp)
        dgf = jnp.sum(dy * xhat, axis=0, keepdims=True)
        sq = jnp.sum(jnp.sum(err * err, axis=1, keepdims=True), axis=0, keepdims=True)

        @pl.when(i == 0)
        def _():
            dwg_ref[...] = dwg
            dwp_acc[...] = dwp
            dgf_ref[...] = dgf
            loss_ref[...] = jnp.broadcast_to(sq, (8, 128))

        @pl.when(i > 0)
        def _():
            dwg_ref[...] += dwg
            dwp_acc[...] += dwp
            dgf_ref[...] += dgf
            loss_ref[...] += jnp.broadcast_to(sq, (8, 128))

        @pl.when(i == steps - 1)
        def _():
            for j in range(N_CHIP):
                dwp_ref[j] = dwp_acc[:, 256 * j:256 * (j + 1)]

    return _call(
        body, (h2, p, target, w_gate, w_proj, final_gain), grid=(steps,), name="tail",
        in_specs=[pl.BlockSpec((tm, D), lambda i: (i, 0)), pl.BlockSpec((tm, PLE), lambda i: (i, 0)),
                  pl.BlockSpec((tm, D), lambda i: (i, 0)), pl.BlockSpec((D, D), lambda i: (0, 0)),
                  pl.BlockSpec((N_CHIP, PLE, 256), lambda i: (0, 0, 0)), pl.BlockSpec((1, D), lambda i: (0, 0))],
        out_specs=[pl.BlockSpec((tm, D), lambda i: (i, 0)), pl.BlockSpec((D, D), lambda i: (0, 0)),
                   pl.BlockSpec((N_CHIP, PLE, 256), lambda i: (0, 0, 0)), pl.BlockSpec((1, D), lambda i: (0, 0)),
                   pl.BlockSpec((8, 128), lambda i: (0, 0)), pl.BlockSpec((tm, D), lambda i: (i, 0))],
        out_shape=[jax.ShapeDtypeStruct((S, D), F32), jax.ShapeDtypeStruct((D, D), F32),
                   jax.ShapeDtypeStruct((N_CHIP, PLE, 256), F32), jax.ShapeDtypeStruct((1, D), F32),
                   jax.ShapeDtypeStruct((8, 128), F32), jax.ShapeDtypeStruct((S, D), BF16)],
        scratch_shapes=[pltpu.VMEM((PLE, D), F32)],
        compiler_params=_params(48))[0]


def _ffn_bwd_down(dh2b, r, a_t, w_ff2, exchange=None):
    tm = 1024
    nj = D_FF // 1024

    def body(dh2_ref, r_ref, at_ref, w2_ref, df_ref, dw2_ref):
        i = pl.program_id(1)
        dh2b = dh2_ref[...]
        da = _dot_nt(dh2b, w2_ref[0])
        df_ref[...] = (da * (2.0 * r_ref[...].astype(F32))).astype(BF16)
        dw2 = _dot(at_ref[...], dh2b)

        @pl.when(i == 0)
        def _():
            dw2_ref[0] = dw2

        @pl.when(i > 0)
        def _():
            dw2_ref[0] += dw2

    return _call(
        body, (dh2b, r, a_t, w_ff2), grid=(nj, S // tm), name="ffn_bwd_down",
        in_specs=[pl.BlockSpec((tm, D), lambda j, i: (i, 0)), pl.BlockSpec((tm, 1024), lambda j, i: (i, j)),
                  pl.BlockSpec((1024, tm), lambda j, i: (j, i)), pl.BlockSpec((1, 1024, D), lambda j, i: (j, 0, 0))],
        out_specs=[pl.BlockSpec((tm, 1024), lambda j, i: (i, j)), pl.BlockSpec((1, 1024, D), lambda j, i: (j, 0, 0))],
        out_shape=[jax.ShapeDtypeStruct((S, D_FF), BF16), jax.ShapeDtypeStruct((nj, 1024, D), F32)],
        compiler_params=_params(48, 2), exchange=exchange)


def _ffn_bwd_up(df, hn2_t, exchange=None):
    tm = 2048
    nj = D_FF // 1024

    def body(df_ref, hnt_ref, dw1_ref):
        i = pl.program_id(1)
        dw1 = _dot(hnt_ref[...], df_ref[...])

        @pl.when(i == 0)
        def _():
            dw1_ref[0] = dw1

        @pl.when(i > 0)
        def _():
            dw1_ref[0] += dw1

    return _call(
        body, (df, hn2_t), grid=(nj, S // tm), name="ffn_bwd_up",
        in_specs=[pl.BlockSpec((tm, 1024), lambda j, i: (i, j)), pl.BlockSpec((D, tm), lambda j, i: (0, i))],
        out_specs=[pl.BlockSpec((1, D, 1024), lambda j, i: (j, 0, 0))],
        out_shape=[jax.ShapeDtypeStruct((nj, D, 1024), F32)],
        compiler_params=_params(40, 2), exchange=exchange)


def _ffn_bwd_input(df, w_ff1, dh2, h1, gain2, mix, w_out, exchange=None):
    tm = 512
    nj = D_FF // 1024
    steps = S // tm

    def body(df_ref, w1_ref, dh2_ref, h1_ref, g_ref, mix_ref, wo_ref, dh1_ref, dmix_ref, dwo_ref, dg_ref, acc_ref):
        i = pl.program_id(0)
        j = pl.program_id(1)
        part = _dot_nt(df_ref[...], w1_ref[j])

        @pl.when(j == 0)
        def _():
            acc_ref[...] = part

        @pl.when(j > 0)
        def _():
            acc_ref[...] += part

        @pl.when(j == nj - 1)
        def _():
            dhn = acc_ref[...]
            h1 = h1_ref[...]
            r2 = _rms_scale(h1)
            xhat = h1 * r2
            dh1 = dh2_ref[...] + _rms_bwd(dhn * g_ref[...], xhat, r2)
            dh1_ref[...] = dh1
            dh1b = dh1.astype(BF16)
            dmix_ref[...] = _dot_nt(dh1b, wo_ref[...])
            dwo = _dot_tn(mix_ref[...], dh1b)
            dg = jnp.sum(dhn * xhat, axis=0, keepdims=True)

            @pl.when(i == 0)
            def _():
                dwo_ref[...] = dwo
                dg_ref[...] = dg

            @pl.when(i > 0)
            def _():
                dwo_ref[...] += dwo
                dg_ref[...] += dg

    return _call(
        body, (df, w_ff1, dh2, h1, gain2, mix, w_out), grid=(steps, nj), name="ffn_bwd_input",
        in_specs=[pl.BlockSpec((tm, 1024), lambda i, j: (i, j)),
                  pl.BlockSpec((nj, D, 1024), lambda i, j: (0, 0, 0), pipeline_mode=pl.Buffered(1)),
                  pl.BlockSpec((tm, D), lambda i, j: (i, 0)), pl.BlockSpec((tm, D), lambda i, j: (i, 0)),
                  pl.BlockSpec((1, D), lambda i, j: (0, 0)), pl.BlockSpec((tm, D), lambda i, j: (i, 0)),
                  pl.BlockSpec((D, D), lambda i, j: (0, 0), pipeline_mode=pl.Buffered(1))],
        out_specs=[pl.BlockSpec((tm, D), lambda i, j: (i, 0)), pl.BlockSpec((tm, D), lambda i, j: (i, 0)),
                   pl.BlockSpec((D, D), lambda i, j: (0, 0)), pl.BlockSpec((1, D), lambda i, j: (0, 0))],
        out_shape=[jax.ShapeDtypeStruct((S, D), F32), jax.ShapeDtypeStruct((S, D), F32),
                   jax.ShapeDtypeStruct((D, D), F32), jax.ShapeDtypeStruct((1, D), F32)],
        scratch_shapes=[pltpu.VMEM((tm, D), F32)],
        compiler_params=_params(56, 2), exchange=exchange)


IN_GROUP = 8


def _mixer_bwd(z, dmix, v_gain, w_spatial, b_spatial_t, saved, bucket, hn1, exchange=None):
    def body(z_ref, kvp_ref, dm_ref, gain_ref, ws_ref, bt_ref, probs_ref, probs_t_ref, share_ref, guv_ref, dgelu_ref,
             bucket_ref, hn_ref,
             dz_ref, dws_ref, db_ref, dgain_ref, dsink_ref, drel_ref, dwin_ref,
             wt_ref, wtt_ref, dbias_ref, dsv_ref, carry_ref):
        n = pl.program_id(0)

        @pl.when(n == 0)
        def _():
            _fill_tril(ws_ref, wt_ref, wtt_ref)
            dwin_ref[...] = jnp.zeros_like(dwin_ref)
            dbias_ref[...] = jnp.zeros_like(dbias_ref)
            dsv_ref[...] = jnp.zeros_like(dsv_ref)
            dws_ref[...] = jnp.zeros_like(dws_ref)
            dgain_ref[...] = jnp.zeros_like(dgain_ref)
            dsink_ref[...] = jnp.zeros_like(dsink_ref)

        rows = pl.ds(pl.multiple_of(n * CHUNK, CHUNK), CHUNK)

        guv = guv_ref[...]
        dgelu = dgelu_ref[...]
        for g in range(N_GROUP):
            lo, hi = 128 * g, 128 * (g + 1)
            u = guv[:, lo:hi]
            vg = guv[:, 512 + lo:512 + hi]
            rr = _rms_scale(vg)
            vhat = vg * rr
            gain = gain_ref[:, lo:hi]
            vnb = (vhat * gain).astype(BF16)
            sv = _dot(wt_ref[g], vnb) + bt_ref[:, g:g + 1]
            da = dm_ref[:, lo:hi]
            dsv = da * u
            dsvb = dsv.astype(BF16)
            dsv_ref[g] += dsv
            dws_ref[g] += _dot_nt(dsvb, vnb)
            dvn = _dot(wtt_ref[g], dsvb)
            dgain_ref[:, lo:hi] += jnp.sum(dvn * vhat, axis=0, keepdims=True)
            dvg = _rms_bwd(dvn * gain, vhat, rr)
            dz_ref[rows, lo:hi] = (da * sv * dgelu[:, lo:hi]).astype(BF16)
            dz_ref[rows, 512 + lo:512 + hi] = (dvg * dgelu[:, 512 + lo:512 + hi]).astype(BF16)

        k_same, k_swap, v_same, v_swap = _kv_layouts(kvp_ref[...], z_ref[:, 1536:1792])
        lane_half = lax.broadcasted_iota(jnp.int32, (1, 128), 1) // 64
        zero = jnp.zeros((2 * CHUNK, 128), F32)
        dk_same, dk_swap, dv_same, dv_swap = zero, zero, zero, zero
        for pair in range(N_HEAD // 2):
            cols = slice(1024 + 128 * pair, 1024 + 128 * (pair + 1))
            qq = z_ref[:, cols]
            do_pair = dm_ref[:, 512 + 128 * pair:512 + 128 * (pair + 1)]
            dq = jnp.zeros((CHUNK, 128), F32)
            for pos in range(2):
                h = 2 * pair + pos
                _, _, same = _head_place(h)
                on_half = lane_half == pos
                qm = jnp.where(on_half, qq, 0.0).astype(BF16)
                k_use = k_same if same else k_swap
                v_use = v_same if same else v_swap
                pb = probs_ref[0, h]
                p = pb.astype(F32)
                p_sink = share_ref[:, h:h + 1]
                dom = jnp.where(on_half, do_pair, 0.0).astype(BF16)
                dp = _dot_nt(dom, v_use)
                dsum = jnp.sum(p * dp, axis=-1, keepdims=True)
                ds = p * (dp - dsum)
                dbias_ref[h] += ds
                dsink_ref[h:h + 1, :] += jnp.broadcast_to(jnp.sum(-p_sink * dsum, axis=0, keepdims=True), (1, 128))
                dsb = ds.astype(BF16)
                dq = dq + jnp.where(on_half, _dot(dsb, k_use), 0.0)
                dk_h = _dot_tn(dsb, qm)
                dv_h = _dot(probs_t_ref[0, h], dom)
                if same:
                    dk_same, dv_same = dk_same + dk_h, dv_same + dv_h
                else:
                    dk_swap, dv_swap = dk_swap + dk_h, dv_swap + dv_h
            dz_ref[rows, cols] = (dq * QK_SCALE).astype(BF16)
        dk = (dk_same + pltpu.roll(dk_swap, 64, axis=1)) * QK_SCALE
        dv = dv_same + pltpu.roll(dv_swap, 64, axis=1)
        dkv = jnp.concatenate([dk, dv], axis=1)

        @pl.when(n > 0)
        def _():
            prev_rows = pl.ds(pl.multiple_of((n - 1) * CHUNK, CHUNK), CHUNK)
            dz_ref[prev_rows, 1536:1792] = (carry_ref[...] + dkv[:CHUNK]).astype(BF16)

        carry_ref[...] = dkv[CHUNK:]

        @pl.when((n > 0) & (n % IN_GROUP == 0))
        def _():
            done = pl.ds(pl.multiple_of((n - IN_GROUP) * CHUNK, IN_GROUP * CHUNK), IN_GROUP * CHUNK)
            dwin_ref[...] += _dot_tn(dz_ref[done, :], hn_ref[...])

        @pl.when(n == N_BLOCK - 1)
        def _():
            dz_ref[rows, 1536:1792] = dkv[CHUNK:].astype(BF16)
            last = pl.ds((N_BLOCK - IN_GROUP) * CHUNK, IN_GROUP * CHUNK)
            dwin_ref[...] += _dot_tn(dz_ref[last, :], hn_ref[...])
            r = lax.broadcasted_iota(jnp.int32, (CHUNK, CHUNK), 0)
            c = lax.broadcasted_iota(jnp.int32, (CHUNK, CHUNK), 1)
            for g in range(N_GROUP):
                dws_ref[g] = jnp.where(c <= r, dws_ref[g], 0.0)
                db_ref[g] = jnp.sum(dsv_ref[g], axis=1, keepdims=True)
            bucket = bucket_ref[...]
            for h in range(N_HEAD):
                dbh = dbias_ref[h]
                per_bucket = [jnp.sum(jnp.where(bucket == b, dbh, 0.0), axis=0, keepdims=True) for b in range(N_BUCKET)]
                drel_ref[h] = jnp.sum(jnp.concatenate(per_bucket, axis=0), axis=1, keepdims=True)

    def hn_group(n):
        return jnp.where(n == N_BLOCK - 1, N_BLOCK // IN_GROUP - 1, jnp.maximum(n // IN_GROUP - 1, 0))

    return _call(
        body, (z, z, dmix, v_gain, w_spatial, b_spatial_t, *saved, bucket, hn1), grid=(N_BLOCK,),
        name="mixer_bwd",
        in_specs=[pl.BlockSpec((CHUNK, D_IN), lambda n: (n, 0)),
                  pl.BlockSpec((CHUNK, 256), lambda n: (jnp.maximum(n - 1, 0), 6)),
                  pl.BlockSpec((CHUNK, D), lambda n: (n, 0)),
                  pl.BlockSpec((1, 512), lambda n: (0, 0)),
                  pl.BlockSpec((N_GROUP, CHUNK, CHUNK), lambda n: (0, 0, 0)),
                  pl.BlockSpec((CHUNK, N_GROUP), lambda n: (0, 0)),
                  pl.BlockSpec((1, N_HEAD, CHUNK, 2 * CHUNK), lambda n: (n, 0, 0, 0)),
                  pl.BlockSpec((1, N_HEAD, 2 * CHUNK, CHUNK), lambda n: (n, 0, 0, 0)),
                  pl.BlockSpec((CHUNK, 128), lambda n: (n, 0)),
                  pl.BlockSpec((CHUNK, 1024), lambda n: (n, 0)), pl.BlockSpec((CHUNK, 1024), lambda n: (n, 0)),
                  pl.BlockSpec((CHUNK, 2 * CHUNK), lambda n: (0, 0)),
                  pl.BlockSpec((IN_GROUP * CHUNK, D), lambda n: (hn_group(n), 0))],
        out_specs=[pl.BlockSpec((S, D_IN), lambda n: (0, 0)),
                   pl.BlockSpec((N_GROUP, CHUNK, CHUNK), lambda n: (0, 0, 0)),
                   pl.BlockSpec((N_GROUP, CHUNK, 1), lambda n: (0, 0, 0)),
                   pl.BlockSpec((1, 512), lambda n: (0, 0)),
                   pl.BlockSpec((N_HEAD, 128), lambda n: (0, 0)),
                   pl.BlockSpec((N_HEAD, N_BUCKET, 1), lambda n: (0, 0, 0)),
                   pl.BlockSpec((D_IN, D), lambda n: (0, 0))],
        out_shape=[jax.ShapeDtypeStruct((S, D_IN), BF16), jax.ShapeDtypeStruct((N_GROUP, CHUNK, CHUNK), F32),
                   jax.ShapeDtypeStruct((N_GROUP, CHUNK, 1), F32), jax.ShapeDtypeStruct((1, 512), F32),
                   jax.ShapeDtypeStruct((N_HEAD, 128), F32), jax.ShapeDtypeStruct((N_HEAD, N_BUCKET, 1), F32),
                   jax.ShapeDtypeStruct((D_IN, D), F32)],
        scratch_shapes=[pltpu.VMEM((N_GROUP, CHUNK, CHUNK), BF16),
                        pltpu.VMEM((N_GROUP, CHUNK, CHUNK), BF16), pltpu.VMEM((N_HEAD, CHUNK, 2 * CHUNK), F32),
                        pltpu.VMEM((N_GROUP, CHUNK, CHUNK), F32), pltpu.VMEM((CHUNK, 256), F32)],
        compiler_params=_params(56), exchange=exchange)


def _in_bwd_input(dz, w_in_t, x, dh1, gain1, exchange=None):
    tm = 512

    def body(dz_ref, w_ref, x_ref, dh1_ref, g_ref, dx_ref, dg_ref):
        i = pl.program_id(0)
        dhn = _dot(dz_ref[...], w_ref[...])
        xv = x_ref[...]
        r1 = _rms_scale(xv)
        xhat = xv * r1
        dx_ref[...] = dh1_ref[...] + _rms_bwd(dhn * g_ref[...], xhat, r1)
        dg = jnp.sum(dhn * xhat, axis=0, keepdims=True)

        @pl.when(i == 0)
        def _():
            dg_ref[...] = dg

        @pl.when(i > 0)
        def _():
            dg_ref[...] += dg

    return _call(
        body, (dz, w_in_t, x, dh1, gain1), grid=(S // tm,), name="in_bwd_input",
        in_specs=[pl.BlockSpec((tm, D_IN), lambda i: (i, 0)), pl.BlockSpec((D_IN, D), lambda i: (0, 0)),
                  pl.BlockSpec((tm, D), lambda i: (i, 0)), pl.BlockSpec((tm, D), lambda i: (i, 0)),
                  pl.BlockSpec((1, D), lambda i: (0, 0))],
        out_specs=[pl.BlockSpec((tm, D), lambda i: (i, 0)), pl.BlockSpec((1, D), lambda i: (0, 0))],
        out_shape=[jax.ShapeDtypeStruct((S, D), F32), jax.ShapeDtypeStruct((1, D), F32)],
        compiler_params=_params(48), exchange=exchange)


def _rel_bucket():
    a = jnp.arange(CHUNK)[:, None]
    j = jnp.arange(2 * CHUNK)[None, :]
    n = jnp.maximum(CHUNK + a - j, 0)
    max_exact = N_BUCKET // 2
    nf = jnp.maximum(n, 1).astype(jnp.float32)
    large = max_exact + (jnp.log(nf / max_exact) / math.log(CHUNK / max_exact) * (N_BUCKET - max_exact)).astype(jnp.int32)
    large = jnp.minimum(large, N_BUCKET - 1)
    return jnp.where(n < max_exact, n, large).astype(jnp.int32)


def _step(x, p, target, small, bufs, place):
    bucket = _rel_bucket()
    sinks = small["attn_sinks"].reshape(N_HEAD)
    b_t = jnp.transpose(small["b_spatial"].reshape(N_GROUP, CHUNK))
    ws = small["w_spatial"].reshape(N_GROUP, CHUNK, CHUNK)
    gain1, gain2 = small["norm1_gain"], small["norm2_gain"]
    v_gain = small["gmlp_v_gain"]
    final_gain = small["final_gain"].reshape(1, D)
    table = small["rel_bias_table"]
    bufs = dict(bufs)

    def gather(*names):
        return _RelayGather([bufs[n] for n in names])

    def took(names, got):
        bufs.update(zip(names, got))

    w_in_t = _whole(bufs["w_in"]).reshape(D_IN, D)
    (z, hn1), got = _in_proj(x, gain1, w_in_t, gather("w_out"))
    took(["w_out"], got)
    (mix, *saved), got = _mixer_fwd(z, v_gain, ws, b_t, sinks, table, bucket, gather("w_ff1"))
    took(["w_ff1"], got)
    w_out = _whole(bufs["w_out"]).reshape(D, D)
    (h1, hn2, hn2_t), _ = _out_proj(x, mix, w_out, gain2)
    w_ff1 = _whole(bufs["w_ff1"])
    (r, a, a_t), got = _ffn_up(hn2, w_ff1, gather("w_ff2"))
    took(["w_ff2"], got)
    w_ff2 = _whole(bufs["w_ff2"])
    (h2,), got = _ffn_down(h1, a, w_ff2, gather("w_ple_gate", "w_ple_proj"))
    took(["w_ple_gate", "w_ple_proj"], got)
    dh2, d_gate, d_proj, d_final, sq, dh2b = _tail(h2, p, target, _whole(bufs["w_ple_gate"]).reshape(D, D),
                                                   _whole(bufs["w_ple_proj"]), final_gain)

    def pair_sums(halves, from_sibling):
        sums, landing = zip(*[_pair_sum(g, o, place) for g, o in zip(halves, from_sibling)])
        return list(sums), list(landing)

    landed = {}
    halves = [_halves(d_gate.reshape(N_CHIP, 256, D)), _halves(d_proj)]
    (df, d_ff2), got = _ffn_bwd_down(dh2b, r, a_t, w_ff2, _SiblingExchange(halves))
    ex, halves = _ChipExchange(*pair_sums(halves, got)), [_halves(d_ff2)]
    (d_ff1,), got = _ffn_bwd_up(df, hn2_t, _Both(ex, _SiblingExchange(halves)))
    landed.update(zip(["w_ple_gate", "w_ple_proj"], got[:2]))
    ex, halves = _ChipExchange(*pair_sums(halves, got[2:])), [_halves(d_ff1)]
    (dh1, dmix, d_out, d_gain2), got = _ffn_bwd_input(df, w_ff1, dh2, h1, gain2, mix, w_out,
                                                      _Both(ex, _SiblingExchange(halves)))
    landed["w_ff2"] = got[0]
    ex, halves = _ChipExchange(*pair_sums(halves, got[1:])), [_halves(d_out.reshape(N_CHIP, 256, D))]
    (dz, d_ws, d_b, d_vgain, d_sink, d_rel, d_in_t), got = _mixer_bwd(z, dmix, v_gain, ws, b_t, saved, bucket, hn1,
                                                                     _Both(ex, _SiblingExchange(halves)))
    landed["w_ff1"] = got[0]
    small_grads = {
        "gmlp_v_gain": d_vgain, "w_spatial": d_ws.reshape(1, N_GROUP, CHUNK, CHUNK),
        "b_spatial": d_b.reshape(1, N_GROUP, CHUNK), "attn_sinks": d_sink[:, 0].reshape(1, N_HEAD),
        "rel_bias_table": jnp.transpose(d_rel.reshape(N_HEAD, N_BUCKET)), "norm2_gain": d_gain2,
        "final_gain": d_final.reshape(D),
    }
    ex, halves = _ChipExchange(*pair_sums(halves, got[1:])), [_halves(d_in_t.reshape(N_CHIP, 448, D))]
    (dx, small_grads["norm1_gain"]), got = _in_bwd_input(dz, w_in_t, x, dh1, gain1, _Both(ex, _SiblingExchange(halves)))
    landed["w_out"] = got[0]
    return dx, landed, _ChipExchange(*pair_sums(halves, got[1:])), small_grads, sq


HBM_SPEC = pl.BlockSpec(memory_space=pltpu.HBM)
VMEM_SPEC = pl.BlockSpec(memory_space=pltpu.VMEM)


def _mesh_place():
    x, y, c = lax.axis_index("x"), lax.axis_index("y"), lax.axis_index("c")
    others = [(1 - x, y), (x, 1 - y), (1 - x, 1 - y)]
    return x, y, c, others


def _remote(src, dst, send_sem, recv_sem, device):
    return pltpu.make_async_remote_copy(src_ref=src, dst_ref=dst, send_sem=send_sem, recv_sem=recv_sem,
                                        device_id=device, device_id_type=MESH)


def _hbm_like(a, shape=None, dtype=None):
    return pltpu.HBM(a.shape if shape is None else shape, a.dtype if dtype is None else dtype)


def _gather_start(bufs, send_sems, recv_sems):
    x, y, c, others = _mesh_place()
    me = 2 * x + y
    for w, buf in enumerate(bufs):
        for k in range(3):
            mine = buf.at[me, c]
            _remote(mine, mine, send_sems.at[w, k], recv_sems.at[w, k], (*others[k], c)).start()


def _gather_finish(bufs, send_sems, recv_sems):
    x, y, c, others = _mesh_place()
    me = 2 * x + y
    sibling = (x, y, 1 - c)
    idx = [2 * ox + oy for ox, oy in others]
    chips = range(3)
    for w, buf in enumerate(bufs):
        for k in chips:
            landed = buf.at[idx[k], c]
            _remote(landed, landed, send_sems.at[w, k], recv_sems.at[w, k], sibling).wait_recv()
            _remote(landed, landed, send_sems.at[w, 3 + k], recv_sems.at[w, 3 + k], sibling).start()
    for w, buf in enumerate(bufs):
        for k in chips:
            landed = buf.at[idx[k], 1 - c]
            _remote(landed, landed, send_sems.at[w, 3 + k], recv_sems.at[w, 3 + k], sibling).wait_recv()
    for w, buf in enumerate(bufs):
        for k in chips:
            mine, passed = buf.at[me, c], buf.at[idx[k], c]
            _remote(mine, mine, send_sems.at[w, k], recv_sems.at[w, k], sibling).wait_send()
            _remote(passed, passed, send_sems.at[w, 3 + k], recv_sems.at[w, 3 + k], sibling).wait_send()


def _gather_sems(n):
    return [pltpu.SemaphoreType.DMA((n, 6)), pltpu.SemaphoreType.DMA((n, 6))]


def _sibling_copies(grads, landing, send_sems, recv_sems):
    x, y, c, _ = _mesh_place()
    return [_remote(grads[w].at[j, 1 - c], landing[w].at[j], send_sems.at[w, j], recv_sems.at[w, j], (x, y, 1 - c))
            for w in range(len(grads)) for j in range(N_CHIP)]


def _sibling_exchange_start(grads, landing, send_sems, recv_sems):
    for cp in _sibling_copies(grads, landing, send_sems, recv_sems):
        cp.start()


def _sibling_exchange_finish(grads, landing, send_sems, recv_sems):
    copies = _sibling_copies(grads, landing, send_sems, recv_sems)
    for cp in copies:
        cp.wait_recv()
    for cp in copies:
        cp.wait_send()


def _sibling_exchange_sems(n):
    return [pltpu.SemaphoreType.DMA((n, N_CHIP)), pltpu.SemaphoreType.DMA((n, N_CHIP))]


def _sibling_exchange(grads):
    n = len(grads)

    def body(*refs):
        ins, outs = refs[:n], refs[n:2 * n]
        _sibling_exchange_start(ins, outs, *refs[2 * n:])
        _sibling_exchange_finish(ins, outs, *refs[2 * n:])

    return pl.pallas_call(
        body, name="sibling_exchange",
        in_specs=[HBM_SPEC] * n, out_specs=[HBM_SPEC] * n,
        out_shape=[_hbm_like(g, (N_CHIP,) + g.shape[2:]) for g in grads],
        scratch_shapes=_sibling_exchange_sems(n),
    )(*[_in_hbm(g) for g in grads])


def _chip_exchange_start(sums, landing, send_sems, recv_sems):
    x, y, c, others = _mesh_place()
    me = 2 * x + y
    for w in range(len(sums)):
        for k, (ox, oy) in enumerate(others):
            _remote(sums[w].at[2 * ox + oy], landing[w].at[me], send_sems.at[w, k], recv_sems.at[w, k],
                    (ox, oy, c)).start()


def _chip_exchange_finish(sums, landing, send_sems, recv_sems):
    x, y, c, others = _mesh_place()
    for w in range(len(sums)):
        for k, (ox, oy) in enumerate(others):
            piece = landing[w].at[2 * ox + oy]
            _remote(piece, piece, send_sems.at[w, k], recv_sems.at[w, k], (x, y, c)).wait_recv()
    for w in range(len(sums)):
        for k, (ox, oy) in enumerate(others):
            piece = sums[w].at[2 * ox + oy]
            _remote(piece, piece, send_sems.at[w, k], recv_sems.at[w, k], (x, y, c)).wait_send()


def _chip_exchange_sems(n):
    return [pltpu.SemaphoreType.DMA((n, 3)), pltpu.SemaphoreType.DMA((n, 3))]


def _sibling_allgather(bufs, also):
    n = len(bufs)
    k_in, k_out = len(also.operands), also.n_out

    def body(*refs):
        ex_ins, refs = refs[n:n + k_in], refs[n + k_in:]
        outs, refs = refs[:n], refs[n:]
        ex_outs, refs = refs[:k_out], refs[k_out:]
        send_sems, recv_sems, ex_sems = refs[0], refs[1], refs[2:]
        x, y, c, _ = _mesh_place()
        sibling = (x, y, 1 - c)
        also.start(ex_ins, ex_outs, ex_sems)
        sends = [_remote(outs[w].at[c], outs[w].at[c], send_sems.at[w], recv_sems.at[w], sibling) for w in range(n)]
        for cp in sends:
            cp.start()
        for w in range(n):
            landed = outs[w].at[1 - c]
            _remote(landed, landed, send_sems.at[w], recv_sems.at[w], sibling).wait_recv()
        for cp in sends:
            cp.wait_send()
        also.finish(ex_ins, ex_outs, ex_sems)

    res = pl.pallas_call(
        body, name="sibling_allgather",
        in_specs=[HBM_SPEC] * (n + k_in), out_specs=[HBM_SPEC] * (n + k_out),
        out_shape=[_hbm_like(b) for b in bufs] + also.out_shape,
        input_output_aliases={**{w: w for w in range(n)}, **{n + i: n + o for i, o in also.aliases.items()}},
        scratch_shapes=[pltpu.SemaphoreType.DMA((n,)), pltpu.SemaphoreType.DMA((n,))] + also.sems,
    )(*bufs, *[_in_hbm(o) for o in also.operands])
    return list(res[:n]), list(res[n:])


def _pair_sum(grad, other, place):
    _, _, h, cols = grad.shape
    tr = _row_tile(h)

    def body(place_ref, g_ref, o_ref, sums_ref, own_ref):
        s = (g_ref[0, 0] + o_ref[0]).astype(BF16)
        sums_ref[0] = s

        @pl.when(pl.program_id(1) == place_ref[0])
        def _():
            own_ref[0] = s

    return pl.pallas_call(
        body, name="pair_sum",
        grid_spec=pltpu.PrefetchScalarGridSpec(
            num_scalar_prefetch=1, grid=(h // tr, N_CHIP),
            in_specs=[pl.BlockSpec((1, 1, tr, cols), lambda r, j, place_ref: (j, place_ref[1], r, 0)),
                      pl.BlockSpec((1, tr, cols), lambda r, j, place_ref: (j, r, 0))],
            out_specs=[pl.BlockSpec((1, tr, cols), lambda r, j, place_ref: (j, r, 0)),
                       pl.BlockSpec((1, tr, cols), lambda r, j, place_ref: (place_ref[0], r, 0))]),
        out_shape=[pltpu.HBM((N_CHIP, h, cols), BF16)] * 2,
        compiler_params=_params(32, 2),
    )(place, _in_hbm(grad), _in_hbm(other))


def _chip_sum(parts, place):
    _, h, cols = parts.shape
    tr = _row_tile(h, max(h // 4, 16))

    def body(place_ref, p_ref, out_ref):
        out_ref[0] = ((p_ref[0].astype(F32) + p_ref[1].astype(F32)) + p_ref[2].astype(F32)) + p_ref[3].astype(F32)

    return pl.pallas_call(
        body, name="chip_sum",
        grid_spec=pltpu.PrefetchScalarGridSpec(
            num_scalar_prefetch=1, grid=(h // tr,),
            in_specs=[pl.BlockSpec((N_CHIP, tr, cols), lambda r, place_ref: (0, r, 0))],
            out_specs=pl.BlockSpec((1, tr, cols), lambda r, place_ref: (place_ref[1], r, 0))),
        out_shape=pltpu.HBM((2, h, cols), F32),
        compiler_params=_params(32),
    )(place, _in_hbm(parts))


def _adamw_math(w, g, m, v):
    m = ADAM_B1 * m + (1.0 - ADAM_B1) * g
    v = ADAM_B2 * v + (1.0 - ADAM_B2) * (g * g)
    m_hat = m / (1.0 - ADAM_B1 ** ADAM_STEP)
    v_hat = v / (1.0 - ADAM_B2 ** ADAM_STEP)
    delta = -ADAM_LR * (m_hat / (jnp.sqrt(v_hat) + ADAM_EPS) + ADAM_WD * w)
    return delta, m, v


def _adamw(w, g, m, v, exchange=None):
    rows, cols = w.shape
    tr = _row_tile(rows)

    def body(w_ref, g_ref, m_ref, v_ref, d_ref, nm_ref, nv_ref, g_out_ref):
        g = g_ref[...]
        d_ref[...], nm_ref[...], nv_ref[...] = _adamw_math(w_ref[...], g, m_ref[...], v_ref[...])
        g_out_ref[...] = g

    spec = pl.BlockSpec((tr, cols), lambda r: (r, 0))
    return _call(
        body, (w, g, m, v), grid=(rows // tr,), name="adamw",
        in_specs=[spec] * 4, out_specs=[spec] * 4,
        out_shape=[jax.ShapeDtypeStruct((rows, cols), F32)] * 4,
        compiler_params=_params(48), exchange=exchange)


SMALL_NAMES = ("norm1_gain", "gmlp_v_gain", "w_spatial", "b_spatial", "attn_sinks", "rel_bias_table", "norm2_gain",
               "final_gain")
PACK_TILE = 8 * 128


def _pack_small(arrays):
    parts = []
    for a in arrays:
        flat = a.reshape(-1)
        rows = -(-flat.shape[0] // PACK_TILE) * 8
        parts.append(jnp.pad(flat, (0, rows * 128 - flat.shape[0])).reshape(rows, 128))
    return jnp.concatenate(parts, axis=0)


def _unpack_small(packed, like):
    out, row = [], 0
    for a in like:
        size = math.prod(a.shape)
        rows = -(-size // PACK_TILE) * 8
        out.append(packed[row:row + rows].reshape(-1)[:size].reshape(a.shape))
        row += rows
    return out


def _small_update(gathered, w, m, v):
    rows = gathered.shape[1]

    def body(g_ref, w_ref, m_ref, v_ref, tot_ref, d_ref, nm_ref, nv_ref):
        total = g_ref[0].astype(F32)
        for dev in range(1, 8):
            total = total + g_ref[dev].astype(F32)
        tot_ref[...] = total
        d_ref[...], nm_ref[...], nv_ref[...] = _adamw_math(w_ref[...], total, m_ref[...], v_ref[...])

    return pl.pallas_call(
        body, name="small_update",
        in_specs=[VMEM_SPEC] * 4, out_specs=[VMEM_SPEC] * 4,
        out_shape=[jax.ShapeDtypeStruct((rows, 128), F32)] * 4,
        compiler_params=pltpu.CompilerParams(vmem_limit_bytes=24 * MIB),
    )(gathered, w, m, v)


def _halves(a):
    return a.reshape(a.shape[:-2] + (2, a.shape[-2] // 2, a.shape[-1]))


def _whole(a):
    return a.reshape(a.shape[:-3] + (2 * a.shape[-2], a.shape[-1]))


def kernel(x, p, norm1_gain, w_in, gmlp_v_gain, w_spatial, b_spatial, attn_sinks, rel_bias_table, w_out, norm2_gain, w_ff1, w_ff2, w_ple_proj, w_ple_gate, final_gain, loss_target, m_norm1_gain, m_w_in, m_gmlp_v_gain, m_w_spatial, m_b_spatial, m_attn_sinks, m_rel_bias_table, m_w_out, m_norm2_gain, m_w_ff1, m_w_ff2, m_w_ple_proj, m_w_ple_gate, m_final_gain, v_norm1_gain, v_w_in, v_gmlp_v_gain, v_w_spatial, v_b_spatial, v_attn_sinks, v_rel_bias_table, v_w_out, v_norm2_gain, v_w_ff1, v_w_ff2, v_w_ple_proj, v_w_ple_gate, v_final_gain):
    given = dict(locals())
    small = {n: given[n] for n in SMALL_NAMES}
    chip = 2 * lax.axis_index("x") + lax.axis_index("y")
    place = jnp.stack([chip, lax.axis_index("c")]).astype(jnp.int32)

    big_names = ("w_in", "w_out", "w_ff1", "w_ff2", "w_ple_proj", "w_ple_gate")
    shards = {n: given[n][0] for n in big_names}
    travel = dict(shards, w_in=jnp.transpose(shards["w_in"]))
    rest = [n for n in big_names if n != "w_in"]
    cast, gathered = _cast_shards_beside_gather([travel[n] for n in rest], place[:1],
                                                [_cast_shard(travel["w_in"], place[:1])])
    bufs = dict(zip(rest + ["w_in"], cast + gathered))
    dx, landed, exchange_in, small_grads, sq = _step(x[0], p[0, 0], loss_target[0], small, bufs, place)

    out_grad, out_delta, out_m, out_v = {}, {}, {}, {}

    def update(n, g, exchange=None):
        to = jnp.transpose if n == "w_in" else (lambda a: a)
        (delta, new_m, new_v, g_out), got = _adamw(to(shards[n]), g, to(given["m_" + n][0]), to(given["v_" + n][0]),
                                                   exchange)
        out_grad[n], out_delta[n], out_m[n], out_v[n] = [to(a)[None] for a in (g_out, delta, new_m, new_v)]
        return got

    spare = jnp.zeros((8, 128), F32)
    small_packed = _pack_small([small_grads[n] for n in SMALL_NAMES] + [spare]).astype(BF16)
    early = [n for n in big_names if n != "w_in"]
    reduced, (small_gathered, sq_gathered, landed_in) = _sibling_allgather(
        [_chip_sum(landed[n], place) for n in early], _Both(_Both(_GatherAll(small_packed), _GatherAll(sq)), exchange_in))
    for n, r in zip(early, reduced):
        update(n, _whole(r))
    (reduced_in,), _ = _sibling_allgather([_chip_sum(landed_in, place)], _Nothing())
    update("w_in", _whole(reduced_in))

    like = [given[n] for n in SMALL_NAMES] + [spare]
    packed = _small_update(small_gathered, *[_pack_small([given[pre + n] for n in SMALL_NAMES] + [spare])
                                             for pre in ("", "m_", "v_")])
    for res, out in zip(packed, (out_grad, out_delta, out_m, out_v)):
        out.update(zip(SMALL_NAMES, _unpack_small(res, like)))
    loss = 0.5 * jnp.sum(sq_gathered[:, 0, 0]) / D

    order = ("norm1_gain", "w_in", "gmlp_v_gain", "w_spatial", "b_spatial", "attn_sinks", "rel_bias_table", "w_out",
             "norm2_gain", "w_ff1", "w_ff2", "w_ple_proj", "w_ple_gate", "final_gain")
    return (loss, dx[None], *[out_grad[n] for n in order], *[out_delta[n] for n in order],
            *[out_m[n] for n in order], *[out_v[n] for n in order])
```

```python
import functools
import math

import jax
import jax.numpy as jnp
from jax import lax
from jax.experimental import pallas as pl
from jax.experimental.pallas import tpu as pltpu

S = 2048
D = 1024
D_IN = 1792
D_FF = 4096
PLE = 256
N_CHIP = 4
N_GROUP = 4
CHUNK = 128
N_HEAD = 8
N_BLOCK = S // CHUNK
N_BUCKET = 32
EPS = 1e-6
NEG_INF = -1e30
QK_SCALE = 0.125
GELU_C = math.sqrt(2.0 / math.pi)

ADAM_LR = 0.001
ADAM_B1 = 0.9
ADAM_B2 = 0.999
ADAM_EPS = 1e-08
ADAM_WD = 0.01
ADAM_STEP = 10

F32 = jnp.float32
BF16 = jnp.bfloat16
MIB = 1024 * 1024
MESH = pl.DeviceIdType.MESH

NT = (((1,), (1,)), ((), ()))
TN = (((0,), (0,)), ((), ()))


def _dot(a, b):
    return jnp.dot(a, b, preferred_element_type=F32)


def _dot_nt(a, b):
    return lax.dot_general(a, b, NT, preferred_element_type=F32)


def _dot_tn(a, b):
    return lax.dot_general(a, b, TN, preferred_element_type=F32)


def _params(vmem_mib, n_axes=1):
    return pltpu.CompilerParams(dimension_semantics=("arbitrary",) * n_axes, vmem_limit_bytes=vmem_mib * MIB)


def _rms_scale(v):
    return lax.rsqrt(jnp.mean(v * v, axis=-1, keepdims=True) + EPS)


def _rms_bwd(dy_gain, xhat, r):
    return r * (dy_gain - xhat * jnp.mean(dy_gain * xhat, axis=-1, keepdims=True))


class _Gather:
    def __init__(self, bufs):
        self.operands = list(bufs)
        self.n_out = len(self.operands)
        self.out_shape = [_hbm_like(b) for b in bufs]
        self.aliases = {w: w for w in range(self.n_out)}
        self.sems = _gather_sems(self.n_out)

    def start(self, ins, outs, sems):
        _gather_start(outs, *sems)

    def finish(self, ins, outs, sems):
        _gather_finish(outs, *sems)


class _RelayGather(_Gather):
    TOP, BOTTOM = 6, 7
    DIAGONAL_PASSED = 5
    MIDDLE_AT, LATE_AT = (5, 8), (7, 8)

    def __init__(self, bufs):
        super().__init__(bufs)
        self.sems = [pltpu.SemaphoreType.DMA((self.n_out, 8)), pltpu.SemaphoreType.DMA((self.n_out, 8))]

    def _copies(self, bufs, send_sems, recv_sems):
        x, y, c, others = _mesh_place()
        me = 2 * x + y
        idx = [2 * ox + oy for ox, oy in others]
        sibling = (x, y, 1 - c)
        direct, passed, relayed = [], [], []
        for w, buf in enumerate(bufs):
            rows = buf.shape[2] // 2
            upper, lower = pl.ds(0, rows), pl.ds(rows, rows)
            for k in (0, 1):
                mine = buf.at[me, c]
                direct.append((_remote(mine, mine, send_sems.at[w, k], recv_sems.at[w, k], (*others[k], c)),
                               buf.at[idx[k], c], w, k))
            for k in (0, 1, 2):
                here = buf.at[idx[k], c]
                passed.append((_remote(here, here, send_sems.at[w, 3 + k], recv_sems.at[w, 3 + k], sibling),
                               buf.at[idx[k], 1 - c], w, 3 + k))
            from_x, from_y = buf.at[idx[0], c, upper], buf.at[idx[1], c, lower]
            relayed.append((_remote(from_x, from_x, send_sems.at[w, self.TOP], recv_sems.at[w, self.TOP],
                                    (*others[1], c)), buf.at[idx[2], c, upper], w, self.TOP))
            relayed.append((_remote(from_y, from_y, send_sems.at[w, self.BOTTOM], recv_sems.at[w, self.BOTTOM],
                                    (*others[0], c)), buf.at[idx[2], c, lower], w, self.BOTTOM))
        return direct, passed, relayed

    @staticmethod
    def _landed(piece, send_sems, recv_sems, w, col):
        x, y, c, _ = _mesh_place()
        _remote(piece, piece, send_sems.at[w, col], recv_sems.at[w, col], (x, y, c)).wait_recv()

    def start(self, ins, outs, sems):
        for cp, _, _, _ in self._copies(outs, *sems)[0]:
            cp.start()

    def middle(self, ins, outs, sems):
        direct, passed, relayed = self._copies(outs, *sems)
        for _, piece, w, col in direct:
            self._landed(piece, *sems, w, col)
        for cp, _, _, col in passed:
            if col != self.DIAGONAL_PASSED:
                cp.start()
        for cp, _, _, _ in relayed:
            cp.start()

    def late(self, ins, outs, sems):
        direct, passed, relayed = self._copies(outs, *sems)
        for _, piece, w, col in relayed:
            self._landed(piece, *sems, w, col)
        for cp, _, _, col in passed:
            if col == self.DIAGONAL_PASSED:
                cp.start()

    def finish(self, ins, outs, sems):
        direct, passed, relayed = self._copies(outs, *sems)
        for _, piece, w, col in passed:
            self._landed(piece, *sems, w, col)
        for cp, _, _, _ in direct + passed + relayed:
            cp.wait_send()


class _ChipExchange:
    def __init__(self, sums, landing):
        self.n_out = len(landing)
        self.operands = list(sums) + list(landing)
        self.out_shape = [_hbm_like(b) for b in landing]
        self.aliases = {self.n_out + w: w for w in range(self.n_out)}
        self.sems = _chip_exchange_sems(self.n_out)

    def start(self, ins, outs, sems):
        _chip_exchange_start(ins[:self.n_out], outs, *sems)

    def finish(self, ins, outs, sems):
        _chip_exchange_finish(ins[:self.n_out], outs, *sems)


class _GatherAll:
    def __init__(self, packed):
        self.operands = [packed]
        self.n_out = 1
        self.out_shape = [_hbm_like(packed, (8,) + packed.shape)]
        self.aliases = {}
        self.sems = [pltpu.SemaphoreType.DMA((8,)), pltpu.SemaphoreType.DMA((8,))]

    def _copies(self, ins, outs, sems):
        x, y, c, _ = _mesh_place()
        me = 4 * x + 2 * y + c
        send_sems, recv_sems = sems
        copies = []
        for k in range(1, 8):
            peer = (1 - x if k // 4 else x, 1 - y if (k // 2) % 2 else y, 1 - c if k % 2 else c)
            src = 4 * peer[0] + 2 * peer[1] + peer[2]
            copies.append((_remote(ins[0], outs[0].at[me], send_sems.at[k], recv_sems.at[k], peer), outs[0].at[src]))
        own = pltpu.make_async_copy(ins[0], outs[0].at[me], send_sems.at[0])
        return own, copies

    def start(self, ins, outs, sems):
        own, copies = self._copies(ins, outs, sems)
        own.start()
        for cp, _ in copies:
            cp.start()

    def finish(self, ins, outs, sems):
        own, copies = self._copies(ins, outs, sems)
        x, y, c, _ = _mesh_place()
        for k, (cp, landed) in enumerate(copies):
            _remote(landed, landed, sems[0].at[k + 1], sems[1].at[k + 1], (x, y, c)).wait_recv()
        for cp, _ in copies:
            cp.wait_send()
        own.wait()


class _Nothing:
    operands, n_out, out_shape, aliases, sems = [], 0, [], {}, []

    def start(self, ins, outs, sems):
        pass

    def finish(self, ins, outs, sems):
        pass


class _Both:
    def __init__(self, a, b):
        self.a, self.b = a, b
        self.operands = a.operands + b.operands
        self.n_out = a.n_out + b.n_out
        self.out_shape = a.out_shape + b.out_shape
        self.aliases = dict(a.aliases)
        self.aliases.update({len(a.operands) + i: a.n_out + o for i, o in b.aliases.items()})
        self.sems = a.sems + b.sems

    def _split(self, ins, outs, sems):
        ka, na, sa = len(self.a.operands), self.a.n_out, len(self.a.sems)
        return (ins[:ka], outs[:na], sems[:sa]), (ins[ka:], outs[na:], sems[sa:])

    def start(self, ins, outs, sems):
        for ex, args in zip((self.a, self.b), self._split(ins, outs, sems)):
            ex.start(*args)

    def finish(self, ins, outs, sems):
        for ex, args in zip((self.a, self.b), self._split(ins, outs, sems)):
            ex.finish(*args)


class _SiblingExchange:
    def __init__(self, grads):
        self.operands = list(grads)
        self.n_out = len(self.operands)
        self.out_shape = [_hbm_like(g, (N_CHIP,) + g.shape[2:]) for g in grads]
        self.aliases = {}
        self.sems = _sibling_exchange_sems(self.n_out)

    def start(self, ins, outs, sems):
        _sibling_exchange_start(ins, outs, *sems)

    def finish(self, ins, outs, sems):
        _sibling_exchange_finish(ins, outs, *sems)


def _call(body, operands, *, grid, in_specs, out_specs, out_shape, name, compiler_params, scratch_shapes=(),
          exchange=None):
    operands = [o if getattr(spec, "memory_space", None) == pltpu.SMEM else _in_hbm(o)
                for o, spec in zip(operands, in_specs)]
    out_shape = [pltpu.HBM(s.shape, s.dtype) for s in out_shape]
    if exchange is None:
        res = pl.pallas_call(body, grid=grid, in_specs=in_specs, out_specs=out_specs, out_shape=out_shape, name=name,
                             scratch_shapes=list(scratch_shapes), compiler_params=compiler_params)(*operands)
        return list(res), []
    n_in, n_out, n_scr = len(in_specs), len(out_specs), len(scratch_shapes)
    k_in, k_out = len(exchange.operands), exchange.n_out

    def fused(*refs):
        ins, refs = refs[:n_in], refs[n_in:]
        ex_ins, refs = refs[:k_in], refs[k_in:]
        outs, refs = refs[:n_out], refs[n_out:]
        ex_outs, refs = refs[:k_out], refs[k_out:]
        scratch, sems = refs[:n_scr], refs[n_scr:]
        ids = [pl.program_id(a) for a in range(len(grid))]
        first = functools.reduce(jnp.logical_and, [i == 0 for i in ids])
        last = functools.reduce(jnp.logical_and, [i == g - 1 for i, g in zip(ids, grid)])

        @pl.when(first)
        def _():
            exchange.start(ex_ins, ex_outs, sems)

        def at_step(numerator, denominator):
            at = (numerator * math.prod(grid)) // denominator
            place = [(at // math.prod(grid[a + 1:])) % grid[a] for a in range(len(grid))]
            return functools.reduce(jnp.logical_and, [i == p for i, p in zip(ids, place)])

        if hasattr(exchange, "middle"):
            @pl.when(at_step(*exchange.MIDDLE_AT))
            def _():
                exchange.middle(ex_ins, ex_outs, sems)

            @pl.when(at_step(*exchange.LATE_AT))
            def _():
                exchange.late(ex_ins, ex_outs, sems)

        body(*ins, *outs, *scratch)

        @pl.when(last)
        def _():
            exchange.finish(ex_ins, ex_outs, sems)

    res = pl.pallas_call(
        fused, grid=grid, name=name,
        in_specs=list(in_specs) + [HBM_SPEC] * k_in, out_specs=list(out_specs) + [HBM_SPEC] * k_out,
        out_shape=list(out_shape) + exchange.out_shape,
        input_output_aliases={n_in + i: n_out + o for i, o in exchange.aliases.items()},
        scratch_shapes=list(scratch_shapes) + exchange.sems, compiler_params=compiler_params,
    )(*operands, *[_in_hbm(o) for o in exchange.operands])
    return list(res[:n_out]), list(res[n_out:])


def _in_hbm(a):
    return pltpu.with_memory_space_constraint(a, pltpu.HBM)


def _row_tile(h):
    return max(t for t in range(16, 513, 16) if h % t == 0)


def _cast_shard(a, chip):
    rows, cols = a.shape
    h = rows // 2
    tr = _row_tile(h)

    def body(chip_ref, a_ref, o_ref):
        o_ref[0, 0] = a_ref[0].astype(BF16)

    return pl.pallas_call(
        body, name="cast_shard",
        grid_spec=pltpu.PrefetchScalarGridSpec(
            num_scalar_prefetch=1, grid=(2, h // tr),
            in_specs=[pl.BlockSpec((1, tr, cols), lambda s, r, chip_ref: (s, r, 0))],
            out_specs=pl.BlockSpec((1, 1, tr, cols), lambda s, r, chip_ref: (chip_ref[0], s, r, 0))),
        out_shape=pltpu.HBM((N_CHIP, 2, h, cols), BF16),
        compiler_params=_params(16, 2),
    )(chip, _in_hbm(a.reshape(2, h, cols)))


def _cast_shards_beside_gather(arrays, chip, gathered):
    n, k = len(arrays), len(gathered)
    shapes = [(a.shape[0] // 2, a.shape[1]) for a in arrays]

    def body(chip_ref, *refs):
        ins, refs = refs[:n], refs[n + k:]
        outs, refs = refs[:n], refs[n:]
        bufs, sems = refs[:k], refs[k:]
        half = pl.program_id(0)

        @pl.when(half == 0)
        def _():
            _gather_start(bufs, *sems)

        for a_ref, o_ref in zip(ins, outs):
            o_ref[0, 0] = a_ref[0].astype(BF16)

        @pl.when(half == 1)
        def _():
            _gather_finish(bufs, *sems)

    res = pl.pallas_call(
        body, name="cast_shards",
        grid_spec=pltpu.PrefetchScalarGridSpec(
            num_scalar_prefetch=1, grid=(2,),
            in_specs=[pl.BlockSpec((1, h, c), lambda s, chip_ref: (s, 0, 0)) for h, c in shapes] + [HBM_SPEC] * k,
            out_specs=[pl.BlockSpec((1, 1, h, c), lambda s, chip_ref: (chip_ref[0], s, 0, 0)) for h, c in shapes]
            + [HBM_SPEC] * k,
            scratch_shapes=_gather_sems(k)),
        out_shape=[pltpu.HBM((N_CHIP, 2, h, c), BF16) for h, c in shapes] + [_hbm_like(b) for b in gathered],
        input_output_aliases={1 + n + i: n + i for i in range(k)},
        compiler_params=_params(32),
    )(chip, *[_in_hbm(a.reshape(2, h, c)) for a, (h, c) in zip(arrays, shapes)], *gathered)
    return list(res[:n]), list(res[n:])


def _in_proj(x, gain1, w_in_t, exchange=None):
    tm = 256

    def body(x_ref, g_ref, w_ref, z_ref, hn_ref):
        xv = x_ref[...]
        hn = (xv * _rms_scale(xv) * g_ref[...]).astype(BF16)
        hn_ref[...] = hn
        z_ref[...] = _dot_nt(hn, w_ref[...])

    return _call(
        body, (x, gain1, w_in_t), grid=(S // tm,), name="in_proj",
        in_specs=[pl.BlockSpec((tm, D), lambda i: (i, 0)), pl.BlockSpec((1, D), lambda i: (0, 0)),
                  pl.BlockSpec((D_IN, D), lambda i: (0, 0))],
        out_specs=[pl.BlockSpec((tm, D_IN), lambda i: (i, 0)), pl.BlockSpec((tm, D), lambda i: (i, 0))],
        out_shape=[jax.ShapeDtypeStruct((S, D_IN), F32), jax.ShapeDtypeStruct((S, D), BF16)],
        compiler_params=_params(40), exchange=exchange)


def _gelu_parts(v):
    t = jnp.tanh(GELU_C * (v + 0.044715 * (v * v * v)))
    cdf = 0.5 * (1.0 + t)
    return cdf, t


def _band_mask(n):
    a = lax.broadcasted_iota(jnp.int32, (CHUNK, 2 * CHUNK), 0)
    j = lax.broadcasted_iota(jnp.int32, (CHUNK, 2 * CHUNK), 1)
    dist = CHUNK + a - j
    valid = (dist >= 0) & (dist < CHUNK)
    return valid & ((n > 0) | (j >= CHUNK))


def _fill_bias(bucket_ref, table_ref, bias_ref):
    bucket = bucket_ref[...]
    for h in range(N_HEAD):
        acc = jnp.zeros((CHUNK, 2 * CHUNK), F32)
        for b in range(N_BUCKET):
            acc = jnp.where(bucket == b, table_ref[b, h], acc)
        bias_ref[h] = acc


def _fill_tril(ws_ref, wt_ref, wtt_ref=None):
    r = lax.broadcasted_iota(jnp.int32, (CHUNK, CHUNK), 0)
    c = lax.broadcasted_iota(jnp.int32, (CHUNK, CHUNK), 1)
    for g in range(N_GROUP):
        w = jnp.where(c <= r, ws_ref[g], 0.0)
        wt_ref[g] = w.astype(BF16)
        if wtt_ref is not None:
            wtt_ref[g] = w.T.astype(BF16)


def _kv_layouts(kv_prev, kv_cur):
    both = jnp.concatenate([kv_prev, kv_cur], axis=0)
    k = both[:, :128]
    v = both[:, 128:]
    return (k.astype(BF16), pltpu.roll(k, 64, axis=1).astype(BF16),
            v.astype(BF16), pltpu.roll(v, 64, axis=1).astype(BF16))


def _head_place(h):
    pair, pos, kvh = h // 2, h % 2, h // 4
    return pair, pos, kvh == pos


def _softmax_sink(qm, k_use, bias_h, sink, valid):
    s = _dot_nt(qm, k_use) * QK_SCALE + bias_h
    s = jnp.where(valid, s, NEG_INF)
    m = jnp.maximum(jnp.max(s, axis=-1, keepdims=True), sink)
    e = jnp.exp(s - m)
    es = jnp.exp(sink - m)
    inv = 1.0 / (jnp.sum(e, axis=-1, keepdims=True) + es)
    return e * inv, es * inv


def _mixer_fwd(z, v_gain, w_spatial, b_spatial_t, sinks, rel_table, bucket, exchange=None):
    def body(z_ref, kvp_ref, gain_ref, ws_ref, bt_ref, sink_ref, table_ref, bucket_ref, out_ref, probs_ref, probs_t_ref,
             share_ref, guv_ref, dgelu_ref, bias_ref, wt_ref):
        n = pl.program_id(0)

        @pl.when(n == 0)
        def _():
            _fill_bias(bucket_ref, table_ref, bias_ref)
            _fill_tril(ws_ref, wt_ref)

        zuv = z_ref[:, :1024]
        cdf, t = _gelu_parts(zuv)
        guv = zuv * cdf
        guv_ref[...] = guv
        dgelu_ref[...] = cdf + zuv * (0.5 * (1.0 - t * t)) * (GELU_C * (1.0 + 3.0 * 0.044715 * (zuv * zuv)))
        for g in range(N_GROUP):
            vg = guv[:, 512 + 128 * g:512 + 128 * (g + 1)]
            vn = vg * _rms_scale(vg) * gain_ref[:, 128 * g:128 * (g + 1)]
            sv = _dot(wt_ref[g], vn.astype(BF16)) + bt_ref[:, g:g + 1]
            out_ref[:, 128 * g:128 * (g + 1)] = (guv[:, 128 * g:128 * (g + 1)] * sv).astype(BF16)

        k_same, k_swap, v_same, v_swap = _kv_layouts(kvp_ref[...], z_ref[:, 1536:1792])
        valid = _band_mask(n)
        lane = lax.broadcasted_iota(jnp.int32, (1, 128), 1)
        lane_half = lane // 64
        shares = jnp.zeros((CHUNK, 128), F32)
        for pair in range(N_HEAD // 2):
            qq = z_ref[:, 1024 + 128 * pair:1024 + 128 * (pair + 1)]
            acc = jnp.zeros((CHUNK, 128), F32)
            for pos in range(2):
                h = 2 * pair + pos
                _, _, same = _head_place(h)
                qm = jnp.where(lane_half == pos, qq, 0.0).astype(BF16)
                p, p_sink = _softmax_sink(qm, k_same if same else k_swap, bias_ref[h], sink_ref[h], valid)
                pb = p.astype(BF16)
                probs_ref[0, h] = pb
                probs_t_ref[0, h] = p.T.astype(BF16)
                shares = jnp.where(lane == h, p_sink, shares)
                vm = jnp.where(lane_half == pos, v_same if same else v_swap, jnp.zeros((), BF16))
                acc = acc + _dot(pb, vm)
            out_ref[:, 512 + 128 * pair:512 + 128 * (pair + 1)] = acc.astype(BF16)
        share_ref[...] = shares

    return _call(
        body, (z, z, v_gain, w_spatial, b_spatial_t, sinks, rel_table, bucket), grid=(N_BLOCK,), name="mixer_fwd",
        in_specs=[pl.BlockSpec((CHUNK, D_IN), lambda n: (n, 0)),
                  pl.BlockSpec((CHUNK, 256), lambda n: (jnp.maximum(n - 1, 0), 6)),
                  pl.BlockSpec((1, 512), lambda n: (0, 0)),
                  pl.BlockSpec((N_GROUP, CHUNK, CHUNK), lambda n: (0, 0, 0)),
                  pl.BlockSpec((CHUNK, N_GROUP), lambda n: (0, 0)),
                  pl.BlockSpec(memory_space=pltpu.SMEM),
                  pl.BlockSpec(memory_space=pltpu.SMEM),
                  pl.BlockSpec((CHUNK, 2 * CHUNK), lambda n: (0, 0))],
        out_specs=[pl.BlockSpec((CHUNK, D), lambda n: (n, 0)),
                   pl.BlockSpec((1, N_HEAD, CHUNK, 2 * CHUNK), lambda n: (n, 0, 0, 0)),
                   pl.BlockSpec((1, N_HEAD, 2 * CHUNK, CHUNK), lambda n: (n, 0, 0, 0)),
                   pl.BlockSpec((CHUNK, 128), lambda n: (n, 0)),
                   pl.BlockSpec((CHUNK, 1024), lambda n: (n, 0)), pl.BlockSpec((CHUNK, 1024), lambda n: (n, 0))],
        out_shape=[jax.ShapeDtypeStruct((S, D), BF16), jax.ShapeDtypeStruct((N_BLOCK, N_HEAD, CHUNK, 2 * CHUNK), BF16),
                   jax.ShapeDtypeStruct((N_BLOCK, N_HEAD, 2 * CHUNK, CHUNK), BF16), jax.ShapeDtypeStruct((S, 128), F32),
                   jax.ShapeDtypeStruct((S, 1024), F32), jax.ShapeDtypeStruct((S, 1024), F32)],
        scratch_shapes=[pltpu.VMEM((N_HEAD, CHUNK, 2 * CHUNK), F32), pltpu.VMEM((N_GROUP, CHUNK, CHUNK), BF16)],
        compiler_params=_params(32), exchange=exchange)


def _out_proj(x, mix, w_out, gain2, exchange=None):
    tm = 512

    def body(x_ref, mix_ref, w_ref, g_ref, h1_ref, hn_ref, hnt_ref):
        h1 = x_ref[...] + _dot(mix_ref[...], w_ref[...])
        h1_ref[...] = h1
        hn = h1 * _rms_scale(h1) * g_ref[...]
        hn_ref[...] = hn.astype(BF16)
        hnt_ref[...] = hn.T.astype(BF16)

    return _call(
        body, (x, mix, w_out, gain2), grid=(S // tm,), name="out_proj",
        in_specs=[pl.BlockSpec((tm, D), lambda i: (i, 0)), pl.BlockSpec((tm, D), lambda i: (i, 0)),
                  pl.BlockSpec((D, D), lambda i: (0, 0)), pl.BlockSpec((1, D), lambda i: (0, 0))],
        out_specs=[pl.BlockSpec((tm, D), lambda i: (i, 0)), pl.BlockSpec((tm, D), lambda i: (i, 0)),
                   pl.BlockSpec((D, tm), lambda i: (0, i))],
        out_shape=[jax.ShapeDtypeStruct((S, D), F32), jax.ShapeDtypeStruct((S, D), BF16),
                   jax.ShapeDtypeStruct((D, S), BF16)],
        compiler_params=_params(32), exchange=exchange)


def _ffn_up(hn2, w_ff1, exchange=None):
    tm = 512
    nj = D_FF // 1024

    def body(hn_ref, w1_ref, r_ref, a_ref, at_ref):
        r = jnp.maximum(_dot(hn_ref[...], w1_ref[0]), 0.0)
        r_ref[...] = r.astype(BF16)
        a = r * r
        a_ref[...] = a.astype(BF16)
        at_ref[...] = a.T.astype(BF16)

    return _call(
        body, (hn2, w_ff1), grid=(nj, S // tm), name="ffn_up",
        in_specs=[pl.BlockSpec((tm, D), lambda j, i: (i, 0)), pl.BlockSpec((1, D, 1024), lambda j, i: (j, 0, 0))],
        out_specs=[pl.BlockSpec((tm, 1024), lambda j, i: (i, j)), pl.BlockSpec((tm, 1024), lambda j, i: (i, j)),
                   pl.BlockSpec((1024, tm), lambda j, i: (j, i))],
        out_shape=[jax.ShapeDtypeStruct((S, D_FF), BF16), jax.ShapeDtypeStruct((S, D_FF), BF16),
                   jax.ShapeDtypeStruct((D_FF, S), BF16)],
        compiler_params=_params(40, 2), exchange=exchange)


def _ffn_down(h1, a, w_ff2, exchange=None):
    tm = 512
    nj = D_FF // 1024

    def body(h1_ref, a_ref, w2_ref, h2_ref):
        h2_ref[...] = h1_ref[...] + _dot(a_ref[...], w2_ref[...].reshape(D_FF, D))

    return _call(
        body, (h1, a, w_ff2), grid=(S // tm,), name="ffn_down",
        in_specs=[pl.BlockSpec((tm, D), lambda i: (i, 0)), pl.BlockSpec((tm, D_FF), lambda i: (i, 0)),
                  pl.BlockSpec((nj, 1024, D), lambda i: (0, 0, 0), pipeline_mode=pl.Buffered(1))],
        out_specs=[pl.BlockSpec((tm, D), lambda i: (i, 0))],
        out_shape=[jax.ShapeDtypeStruct((S, D), F32)],
        compiler_params=_params(40), exchange=exchange)


def _tail(h2, p, target, w_gate, w_proj, final_gain):
    tm = 256
    steps = S // tm

    def body(h2_ref, p_ref, t_ref, wg_ref, wp_ref, gf_ref, dh2_ref, dwg_ref, dwp_ref, dgf_ref, loss_ref, dh2b_ref,
             dwp_acc):
        i = pl.program_id(0)
        h2 = h2_ref[...]
        h2b = h2.astype(BF16)
        pb = p_ref[...].astype(BF16)
        gate = jax.nn.sigmoid(_dot(h2b, wg_ref[...]))
        pp = jnp.concatenate([_dot(pb, wp_ref[j]) for j in range(N_CHIP)], axis=1)
        h3 = h2 + gate * pp
        r3 = _rms_scale(h3)
        xhat = h3 * r3
        gf = gf_ref[...]
        err = xhat * gf - t_ref[...]
        dy = err * (1.0 / D)
        dh3 = _rms_bwd(dy * gf, xhat, r3)
        dgp = (dh3 * pp * gate * (1.0 - gate)).astype(BF16)
        dpp = (dh3 * gate).astype(BF16)
        dh2 = dh3 + _dot_nt(dgp, wg_ref[...])
        dh2_ref[...] = dh2
        dh2b_ref[...] = dh2.astype(BF16)
        dwg = _dot_tn(h2b, dgp)
        dwp = _dot_tn(pb, dpp)
        dgf = jnp.sum(dy * xhat, axis=0, keepdims=True)
        sq = jnp.sum(jnp.sum(err * err, axis=1, keepdims=True), axis=0, keepdims=True)

        @pl.when(i == 0)
        def _():
            dwg_ref[...] = dwg
            dwp_acc[...] = dwp
            dgf_ref[...] = dgf
            loss_ref[...] = jnp.broadcast_to(sq, (8, 128))

        @pl.when(i > 0)
        def _():
            dwg_ref[...] += dwg
            dwp_acc[...] += dwp
            dgf_ref[...] += dgf
            loss_ref[...] += jnp.broadcast_to(sq, (8, 128))

        @pl.when(i == steps - 1)
        def _():
            for j in range(N_CHIP):
                dwp_ref[j] = dwp_acc[:, 256 * j:256 * (j + 1)]

    return _call(
        body, (h2, p, target, w_gate, w_proj, final_gain), grid=(steps,), name="tail",
        in_specs=[pl.BlockSpec((tm, D), lambda i: (i, 0)), pl.BlockSpec((tm, PLE), lambda i: (i, 0)),
                  pl.BlockSpec((tm, D), lambda i: (i, 0)), pl.BlockSpec((D, D), lambda i: (0, 0)),
                  pl.BlockSpec((N_CHIP, PLE, 256), lambda i: (0, 0, 0)), pl.BlockSpec((1, D), lambda i: (0, 0))],
        out_specs=[pl.BlockSpec((tm, D), lambda i: (i, 0)), pl.BlockSpec((D, D), lambda i: (0, 0)),
                   pl.BlockSpec((N_CHIP, PLE, 256), lambda i: (0, 0, 0)), pl.BlockSpec((1, D), lambda i: (0, 0)),
                   pl.BlockSpec((8, 128), lambda i: (0, 0)), pl.BlockSpec((tm, D), lambda i: (i, 0))],
        out_shape=[jax.ShapeDtypeStruct((S, D), F32), jax.ShapeDtypeStruct((D, D), F32),
                   jax.ShapeDtypeStruct((N_CHIP, PLE, 256), F32), jax.ShapeDtypeStruct((1, D), F32),
                   jax.ShapeDtypeStruct((8, 128), F32), jax.ShapeDtypeStruct((S, D), BF16)],
        scratch_shapes=[pltpu.VMEM((PLE, D), F32)],
        compiler_params=_params(48))[0]


def _ffn_bwd_down(dh2b, r, a_t, w_ff2, exchange=None):
    tm = 1024
    nj = D_FF // 1024

    def body(dh2_ref, r_ref, at_ref, w2_ref, df_ref, dw2_ref):
        i = pl.program_id(1)
        dh2b = dh2_ref[...]
        da = _dot_nt(dh2b, w2_ref[0])
        df_ref[...] = (da * (2.0 * r_ref[...].astype(F32))).astype(BF16)
        dw2 = _dot(at_ref[...], dh2b)

        @pl.when(i == 0)
        def _():
            dw2_ref[0] = dw2

        @pl.when(i > 0)
        def _():
            dw2_ref[0] += dw2

    return _call(
        body, (dh2b, r, a_t, w_ff2), grid=(nj, S // tm), name="ffn_bwd_down",
        in_specs=[pl.BlockSpec((tm, D), lambda j, i: (i, 0)), pl.BlockSpec((tm, 1024), lambda j, i: (i, j)),
                  pl.BlockSpec((1024, tm), lambda j, i: (j, i)), pl.BlockSpec((1, 1024, D), lambda j, i: (j, 0, 0))],
        out_specs=[pl.BlockSpec((tm, 1024), lambda j, i: (i, j)), pl.BlockSpec((1, 1024, D), lambda j, i: (j, 0, 0))],
        out_shape=[jax.ShapeDtypeStruct((S, D_FF), BF16), jax.ShapeDtypeStruct((nj, 1024, D), F32)],
        compiler_params=_params(48, 2), exchange=exchange)


def _ffn_bwd_up(df, hn2_t, exchange=None):
    tm = 2048
    nj = D_FF // 1024

    def body(df_ref, hnt_ref, dw1_ref):
        i = pl.program_id(1)
        dw1 = _dot(hnt_ref[...], df_ref[...])

        @pl.when(i == 0)
        def _():
            dw1_ref[0] = dw1

        @pl.when(i > 0)
        def _():
            dw1_ref[0] += dw1

    return _call(
        body, (df, hn2_t), grid=(nj, S // tm), name="ffn_bwd_up",
        in_specs=[pl.BlockSpec((tm, 1024), lambda j, i: (i, j)), pl.BlockSpec((D, tm), lambda j, i: (0, i))],
        out_specs=[pl.BlockSpec((1, D, 1024), lambda j, i: (j, 0, 0))],
        out_shape=[jax.ShapeDtypeStruct((nj, D, 1024), F32)],
        compiler_params=_params(40, 2), exchange=exchange)


def _ffn_bwd_input(df, w_ff1, dh2, h1, gain2, mix, w_out, exchange=None):
    tm = 512
    nj = D_FF // 1024
    steps = S // tm

    def body(df_ref, w1_ref, dh2_ref, h1_ref, g_ref, mix_ref, wo_ref, dh1_ref, dmix_ref, dwo_ref, dg_ref, acc_ref):
        i = pl.program_id(0)
        j = pl.program_id(1)
        part = _dot_nt(df_ref[...], w1_ref[j])

        @pl.when(j == 0)
        def _():
            acc_ref[...] = part

        @pl.when(j > 0)
        def _():
            acc_ref[...] += part

        @pl.when(j == nj - 1)
        def _():
            dhn = acc_ref[...]
            h1 = h1_ref[...]
            r2 = _rms_scale(h1)
            xhat = h1 * r2
            dh1 = dh2_ref[...] + _rms_bwd(dhn * g_ref[...], xhat, r2)
            dh1_ref[...] = dh1
            dh1b = dh1.astype(BF16)
            dmix_ref[...] = _dot_nt(dh1b, wo_ref[...])
            dwo = _dot_tn(mix_ref[...], dh1b)
            dg = jnp.sum(dhn * xhat, axis=0, keepdims=True)

            @pl.when(i == 0)
            def _():
                dwo_ref[...] = dwo
                dg_ref[...] = dg

            @pl.when(i > 0)
            def _():
                dwo_ref[...] += dwo
                dg_ref[...] += dg

    return _call(
        body, (df, w_ff1, dh2, h1, gain2, mix, w_out), grid=(steps, nj), name="ffn_bwd_input",
        in_specs=[pl.BlockSpec((tm, 1024), lambda i, j: (i, j)),
                  pl.BlockSpec((nj, D, 1024), lambda i, j: (0, 0, 0), pipeline_mode=pl.Buffered(1)),
                  pl.BlockSpec((tm, D), lambda i, j: (i, 0)), pl.BlockSpec((tm, D), lambda i, j: (i, 0)),
                  pl.BlockSpec((1, D), lambda i, j: (0, 0)), pl.BlockSpec((tm, D), lambda i, j: (i, 0)),
                  pl.BlockSpec((D, D), lambda i, j: (0, 0), pipeline_mode=pl.Buffered(1))],
        out_specs=[pl.BlockSpec((tm, D), lambda i, j: (i, 0)), pl.BlockSpec((tm, D), lambda i, j: (i, 0)),
                   pl.BlockSpec((D, D), lambda i, j: (0, 0)), pl.BlockSpec((1, D), lambda i, j: (0, 0))],
        out_shape=[jax.ShapeDtypeStruct((S, D), F32), jax.ShapeDtypeStruct((S, D), F32),
                   jax.ShapeDtypeStruct((D, D), F32), jax.ShapeDtypeStruct((1, D), F32)],
        scratch_shapes=[pltpu.VMEM((tm, D), F32)],
        compiler_params=_params(56, 2), exchange=exchange)


IN_GROUP = 8


def _mixer_bwd(z, dmix, v_gain, w_spatial, b_spatial_t, saved, bucket, hn1, exchange=None):
    def body(z_ref, kvp_ref, dm_ref, gain_ref, ws_ref, bt_ref, probs_ref, probs_t_ref, share_ref, guv_ref, dgelu_ref,
             bucket_ref, hn_ref,
             dz_ref, dws_ref, db_ref, dgain_ref, dsink_ref, drel_ref, dwin_ref,
             wt_ref, wtt_ref, dbias_ref, dsv_ref, carry_ref):
        n = pl.program_id(0)

        @pl.when(n == 0)
        def _():
            _fill_tril(ws_ref, wt_ref, wtt_ref)
            dwin_ref[...] = jnp.zeros_like(dwin_ref)
            dbias_ref[...] = jnp.zeros_like(dbias_ref)
            dsv_ref[...] = jnp.zeros_like(dsv_ref)
            dws_ref[...] = jnp.zeros_like(dws_ref)
            dgain_ref[...] = jnp.zeros_like(dgain_ref)
            dsink_ref[...] = jnp.zeros_like(dsink_ref)

        rows = pl.ds(pl.multiple_of(n * CHUNK, CHUNK), CHUNK)

        guv = guv_ref[...]
        dgelu = dgelu_ref[...]
        for g in range(N_GROUP):
            lo, hi = 128 * g, 128 * (g + 1)
            u = guv[:, lo:hi]
            vg = guv[:, 512 + lo:512 + hi]
            rr = _rms_scale(vg)
            vhat = vg * rr
            gain = gain_ref[:, lo:hi]
            vnb = (vhat * gain).astype(BF16)
            sv = _dot(wt_ref[g], vnb) + bt_ref[:, g:g + 1]
            da = dm_ref[:, lo:hi]
            dsv = da * u
            dsvb = dsv.astype(BF16)
            dsv_ref[g] += dsv
            dws_ref[g] += _dot_nt(dsvb, vnb)
            dvn = _dot(wtt_ref[g], dsvb)
            dgain_ref[:, lo:hi] += jnp.sum(dvn * vhat, axis=0, keepdims=True)
            dvg = _rms_bwd(dvn * gain, vhat, rr)
            dz_ref[rows, lo:hi] = (da * sv * dgelu[:, lo:hi]).astype(BF16)
            dz_ref[rows, 512 + lo:512 + hi] = (dvg * dgelu[:, 512 + lo:512 + hi]).astype(BF16)

        k_same, k_swap, v_same, v_swap = _kv_layouts(kvp_ref[...], z_ref[:, 1536:1792])
        lane_half = lax.broadcasted_iota(jnp.int32, (1, 128), 1) // 64
        zero = jnp.zeros((2 * CHUNK, 128), F32)
        dk_same, dk_swap, dv_same, dv_swap = zero, zero, zero, zero
        for pair in range(N_HEAD // 2):
            cols = slice(1024 + 128 * pair, 1024 + 128 * (pair + 1))
            qq = z_ref[:, cols]
            do_pair = dm_ref[:, 512 + 128 * pair:512 + 128 * (pair + 1)]
            dq = jnp.zeros((CHUNK, 128), F32)
            for pos in range(2):
                h = 2 * pair + pos
                _, _, same = _head_place(h)
                on_half = lane_half == pos
                qm = jnp.where(on_half, qq, 0.0).astype(BF16)
                k_use = k_same if same else k_swap
                v_use = v_same if same else v_swap
                pb = probs_ref[0, h]
                p = pb.astype(F32)
                p_sink = share_ref[:, h:h + 1]
                dom = jnp.where(on_half, do_pair, 0.0).astype(BF16)
                dp = _dot_nt(dom, v_use)
                dsum = jnp.sum(p * dp, axis=-1, keepdims=True)
                ds = p * (dp - dsum)
                dbias_ref[h] += ds
                dsink_ref[h:h + 1, :] += jnp.broadcast_to(jnp.sum(-p_sink * dsum, axis=0, keepdims=True), (1, 128))
                dsb = ds.astype(BF16)
                dq = dq + jnp.where(on_half, _dot(dsb, k_use), 0.0)
                dk_h = _dot_tn(dsb, qm)
                dv_h = _dot(probs_t_ref[0, h], dom)
                if same:
                    dk_same, dv_same = dk_same + dk_h, dv_same + dv_h
                else:
                    dk_swap, dv_swap = dk_swap + dk_h, dv_swap + dv_h
            dz_ref[rows, cols] = (dq * QK_SCALE).astype(BF16)
        dk = (dk_same + pltpu.roll(dk_swap, 64, axis=1)) * QK_SCALE
        dv = dv_same + pltpu.roll(dv_swap, 64, axis=1)
        dkv = jnp.concatenate([dk, dv], axis=1)

        @pl.when(n > 0)
        def _():
            prev_rows = pl.ds(pl.multiple_of((n - 1) * CHUNK, CHUNK), CHUNK)
            dz_ref[prev_rows, 1536:1792] = (carry_ref[...] + dkv[:CHUNK]).astype(BF16)

        carry_ref[...] = dkv[CHUNK:]

        @pl.when((n > 0) & (n % IN_GROUP == 0))
        def _():
            done = pl.ds(pl.multiple_of((n - IN_GROUP) * CHUNK, IN_GROUP * CHUNK), IN_GROUP * CHUNK)
            dwin_ref[...] += _dot_tn(dz_ref[done, :], hn_ref[...])

        @pl.when(n == N_BLOCK - 1)
        def _():
            dz_ref[rows, 1536:1792] = dkv[CHUNK:].astype(BF16)
            last = pl.ds((N_BLOCK - IN_GROUP) * CHUNK, IN_GROUP * CHUNK)
            dwin_ref[...] += _dot_tn(dz_ref[last, :], hn_ref[...])
            r = lax.broadcasted_iota(jnp.int32, (CHUNK, CHUNK), 0)
            c = lax.broadcasted_iota(jnp.int32, (CHUNK, CHUNK), 1)
            for g in range(N_GROUP):
                dws_ref[g] = jnp.where(c <= r, dws_ref[g], 0.0)
                db_ref[g] = jnp.sum(dsv_ref[g], axis=1, keepdims=True)
            bucket = bucket_ref[...]
            for h in range(N_HEAD):
                dbh = dbias_ref[h]
                per_bucket = [jnp.sum(jnp.where(bucket == b, dbh, 0.0), axis=0, keepdims=True) for b in range(N_BUCKET)]
                drel_ref[h] = jnp.sum(jnp.concatenate(per_bucket, axis=0), axis=1, keepdims=True)

    def hn_group(n):
        return jnp.where(n == N_BLOCK - 1, N_BLOCK // IN_GROUP - 1, jnp.maximum(n // IN_GROUP - 1, 0))

    return _call(
        body, (z, z, dmix, v_gain, w_spatial, b_spatial_t, *saved, bucket, hn1), grid=(N_BLOCK,),
        name="mixer_bwd",
        in_specs=[pl.BlockSpec((CHUNK, D_IN), lambda n: (n, 0)),
                  pl.BlockSpec((CHUNK, 256), lambda n: (jnp.maximum(n - 1, 0), 6)),
                  pl.BlockSpec((CHUNK, D), lambda n: (n, 0)),
                  pl.BlockSpec((1, 512), lambda n: (0, 0)),
                  pl.BlockSpec((N_GROUP, CHUNK, CHUNK), lambda n: (0, 0, 0)),
                  pl.BlockSpec((CHUNK, N_GROUP), lambda n: (0, 0)),
                  pl.BlockSpec((1, N_HEAD, CHUNK, 2 * CHUNK), lambda n: (n, 0, 0, 0)),
                  pl.BlockSpec((1, N_HEAD, 2 * CHUNK, CHUNK), lambda n: (n, 0, 0, 0)),
                  pl.BlockSpec((CHUNK, 128), lambda n: (n, 0)),
                  pl.BlockSpec((CHUNK, 1024), lambda n: (n, 0)), pl.BlockSpec((CHUNK, 1024), lambda n: (n, 0)),
                  pl.BlockSpec((CHUNK, 2 * CHUNK), lambda n: (0, 0)),
                  pl.BlockSpec((IN_GROUP * CHUNK, D), lambda n: (hn_group(n), 0))],
        out_specs=[pl.BlockSpec((S, D_IN), lambda n: (0, 0)),
                   pl.BlockSpec((N_GROUP, CHUNK, CHUNK), lambda n: (0, 0, 0)),
                   pl.BlockSpec((N_GROUP, CHUNK, 1), lambda n: (0, 0, 0)),
                   pl.BlockSpec((1, 512), lambda n: (0, 0)),
                   pl.BlockSpec((N_HEAD, 128), lambda n: (0, 0)),
                   pl.BlockSpec((N_HEAD, N_BUCKET, 1), lambda n: (0, 0, 0)),
                   pl.BlockSpec((D_IN, D), lambda n: (0, 0))],
        out_shape=[jax.ShapeDtypeStruct((S, D_IN), BF16), jax.ShapeDtypeStruct((N_GROUP, CHUNK, CHUNK), F32),
                   jax.ShapeDtypeStruct((N_GROUP, CHUNK, 1), F32), jax.ShapeDtypeStruct((1, 512), F32),
                   jax.ShapeDtypeStruct((N_HEAD, 128), F32), jax.ShapeDtypeStruct((N_HEAD, N_BUCKET, 1), F32),
                   jax.ShapeDtypeStruct((D_IN, D), F32)],
        scratch_shapes=[pltpu.VMEM((N_GROUP, CHUNK, CHUNK), BF16),
                        pltpu.VMEM((N_GROUP, CHUNK, CHUNK), BF16), pltpu.VMEM((N_HEAD, CHUNK, 2 * CHUNK), F32),
                        pltpu.VMEM((N_GROUP, CHUNK, CHUNK), F32), pltpu.VMEM((CHUNK, 256), F32)],
        compiler_params=_params(56), exchange=exchange)


def _in_bwd_input(dz, w_in_t, x, dh1, gain1, exchange=None):
    tm = 512

    def body(dz_ref, w_ref, x_ref, dh1_ref, g_ref, dx_ref, dg_ref):
        i = pl.program_id(0)
        dhn = _dot(dz_ref[...], w_ref[...])
        xv = x_ref[...]
        r1 = _rms_scale(xv)
        xhat = xv * r1
        dx_ref[...] = dh1_ref[...] + _rms_bwd(dhn * g_ref[...], xhat, r1)
        dg = jnp.sum(dhn * xhat, axis=0, keepdims=True)

        @pl.when(i == 0)
        def _():
            dg_ref[...] = dg

        @pl.when(i > 0)
        def _():
            dg_ref[...] += dg

    return _call(
        body, (dz, w_in_t, x, dh1, gain1), grid=(S // tm,), name="in_bwd_input",
        in_specs=[pl.BlockSpec((tm, D_IN), lambda i: (i, 0)), pl.BlockSpec((D_IN, D), lambda i: (0, 0)),
                  pl.BlockSpec((tm, D), lambda i: (i, 0)), pl.BlockSpec((tm, D), lambda i: (i, 0)),
                  pl.BlockSpec((1, D), lambda i: (0, 0))],
        out_specs=[pl.BlockSpec((tm, D), lambda i: (i, 0)), pl.BlockSpec((1, D), lambda i: (0, 0))],
        out_shape=[jax.ShapeDtypeStruct((S, D), F32), jax.ShapeDtypeStruct((1, D), F32)],
        compiler_params=_params(48), exchange=exchange)


def _rel_bucket():
    a = jnp.arange(CHUNK)[:, None]
    j = jnp.arange(2 * CHUNK)[None, :]
    n = jnp.maximum(CHUNK + a - j, 0)
    max_exact = N_BUCKET // 2
    nf = jnp.maximum(n, 1).astype(jnp.float32)
    large = max_exact + (jnp.log(nf / max_exact) / math.log(CHUNK / max_exact) * (N_BUCKET - max_exact)).astype(jnp.int32)
    large = jnp.minimum(large, N_BUCKET - 1)
    return jnp.where(n < max_exact, n, large).astype(jnp.int32)


def _step(x, p, target, small, bufs, place):
    bucket = _rel_bucket()
    sinks = small["attn_sinks"].reshape(N_HEAD)
    b_t = jnp.transpose(small["b_spatial"].reshape(N_GROUP, CHUNK))
    ws = small["w_spatial"].reshape(N_GROUP, CHUNK, CHUNK)
    gain1, gain2 = small["norm1_gain"], small["norm2_gain"]
    v_gain = small["gmlp_v_gain"]
    final_gain = small["final_gain"].reshape(1, D)
    table = small["rel_bias_table"]
    bufs = dict(bufs)

    def gather(*names):
        return _RelayGather([bufs[n] for n in names])

    def took(names, got):
        bufs.update(zip(names, got))

    w_in_t = _whole(bufs["w_in"]).reshape(D_IN, D)
    (z, hn1), got = _in_proj(x, gain1, w_in_t, gather("w_out"))
    took(["w_out"], got)
    (mix, *saved), got = _mixer_fwd(z, v_gain, ws, b_t, sinks, table, bucket, gather("w_ff1"))
    took(["w_ff1"], got)
    w_out = _whole(bufs["w_out"]).reshape(D, D)
    (h1, hn2, hn2_t), _ = _out_proj(x, mix, w_out, gain2)
    w_ff1 = _whole(bufs["w_ff1"])
    (r, a, a_t), got = _ffn_up(hn2, w_ff1, gather("w_ff2"))
    took(["w_ff2"], got)
    w_ff2 = _whole(bufs["w_ff2"])
    (h2,), got = _ffn_down(h1, a, w_ff2, gather("w_ple_gate", "w_ple_proj"))
    took(["w_ple_gate", "w_ple_proj"], got)
    dh2, d_gate, d_proj, d_final, sq, dh2b = _tail(h2, p, target, _whole(bufs["w_ple_gate"]).reshape(D, D),
                                                   _whole(bufs["w_ple_proj"]), final_gain)

    def pair_sums(halves, from_sibling):
        sums, landing = zip(*[_pair_sum(g, o, place) for g, o in zip(halves, from_sibling)])
        return list(sums), list(landing)

    landed = {}
    halves = [_halves(d_gate.reshape(N_CHIP, 256, D)), _halves(d_proj)]
    (df, d_ff2), got = _ffn_bwd_down(dh2b, r, a_t, w_ff2, _SiblingExchange(halves))
    ex, halves = _ChipExchange(*pair_sums(halves, got)), [_halves(d_ff2)]
    (d_ff1,), got = _ffn_bwd_up(df, hn2_t, _Both(ex, _SiblingExchange(halves)))
    landed.update(zip(["w_ple_gate", "w_ple_proj"], got[:2]))
    ex, halves = _ChipExchange(*pair_sums(halves, got[2:])), [_halves(d_ff1)]
    (dh1, dmix, d_out, d_gain2), got = _ffn_bwd_input(df, w_ff1, dh2, h1, gain2, mix, w_out,
                                                      _Both(ex, _SiblingExchange(halves)))
    landed["w_ff2"] = got[0]
    ex, halves = _ChipExchange(*pair_sums(halves, got[1:])), [_halves(d_out.reshape(N_CHIP, 256, D))]
    (dz, d_ws, d_b, d_vgain, d_sink, d_rel, d_in_t), got = _mixer_bwd(z, dmix, v_gain, ws, b_t, saved, bucket, hn1,
                                                                     _Both(ex, _SiblingExchange(halves)))
    landed["w_ff1"] = got[0]
    small_grads = {
        "gmlp_v_gain": d_vgain, "w_spatial": d_ws.reshape(1, N_GROUP, CHUNK, CHUNK),
        "b_spatial": d_b.reshape(1, N_GROUP, CHUNK), "attn_sinks": d_sink[:, 0].reshape(1, N_HEAD),
        "rel_bias_table": jnp.transpose(d_rel.reshape(N_HEAD, N_BUCKET)), "norm2_gain": d_gain2,
        "final_gain": d_final.reshape(D),
    }
    ex, halves = _ChipExchange(*pair_sums(halves, got[1:])), [_halves(d_in_t.reshape(N_CHIP, 448, D))]
    (dx, small_grads["norm1_gain"]), got = _in_bwd_input(dz, w_in_t, x, dh1, gain1, _Both(ex, _SiblingExchange(halves)))
    landed["w_out"] = got[0]
    return dx, landed, _ChipExchange(*pair_sums(halves, got[1:])), small_grads, sq


HBM_SPEC = pl.BlockSpec(memory_space=pltpu.HBM)
VMEM_SPEC = pl.BlockSpec(memory_space=pltpu.VMEM)


def _mesh_place():
    x, y, c = lax.axis_index("x"), lax.axis_index("y"), lax.axis_index("c")
    others = [(1 - x, y), (x, 1 - y), (1 - x, 1 - y)]
    return x, y, c, others


def _remote(src, dst, send_sem, recv_sem, device):
    return pltpu.make_async_remote_copy(src_ref=src, dst_ref=dst, send_sem=send_sem, recv_sem=recv_sem,
                                        device_id=device, device_id_type=MESH)


def _hbm_like(a, shape=None, dtype=None):
    return pltpu.HBM(a.shape if shape is None else shape, a.dtype if dtype is None else dtype)


def _gather_start(bufs, send_sems, recv_sems):
    x, y, c, others = _mesh_place()
    me = 2 * x + y
    for w, buf in enumerate(bufs):
        for k in range(3):
            mine = buf.at[me, c]
            _remote(mine, mine, send_sems.at[w, k], recv_sems.at[w, k], (*others[k], c)).start()


def _gather_finish(bufs, send_sems, recv_sems):
    x, y, c, others = _mesh_place()
    me = 2 * x + y
    sibling = (x, y, 1 - c)
    idx = [2 * ox + oy for ox, oy in others]
    chips = range(3)
    for w, buf in enumerate(bufs):
        for k in chips:
            landed = buf.at[idx[k], c]
            _remote(landed, landed, send_sems.at[w, k], recv_sems.at[w, k], sibling).wait_recv()
            _remote(landed, landed, send_sems.at[w, 3 + k], recv_sems.at[w, 3 + k], sibling).start()
    for w, buf in enumerate(bufs):
        for k in chips:
            landed = buf.at[idx[k], 1 - c]
            _remote(landed, landed, send_sems.at[w, 3 + k], recv_sems.at[w, 3 + k], sibling).wait_recv()
    for w, buf in enumerate(bufs):
        for k in chips:
            mine, passed = buf.at[me, c], buf.at[idx[k], c]
            _remote(mine, mine, send_sems.at[w, k], recv_sems.at[w, k], sibling).wait_send()
            _remote(passed, passed, send_sems.at[w, 3 + k], recv_sems.at[w, 3 + k], sibling).wait_send()


def _gather_sems(n):
    return [pltpu.SemaphoreType.DMA((n, 6)), pltpu.SemaphoreType.DMA((n, 6))]


def _sibling_copies(grads, landing, send_sems, recv_sems):
    x, y, c, _ = _mesh_place()
    return [_remote(grads[w].at[j, 1 - c], landing[w].at[j], send_sems.at[w, j], recv_sems.at[w, j], (x, y, 1 - c))
            for w in range(len(grads)) for j in range(N_CHIP)]


def _sibling_exchange_start(grads, landing, send_sems, recv_sems):
    for cp in _sibling_copies(grads, landing, send_sems, recv_sems):
        cp.start()


def _sibling_exchange_finish(grads, landing, send_sems, recv_sems):
    copies = _sibling_copies(grads, landing, send_sems, recv_sems)
    for cp in copies:
        cp.wait_recv()
    for cp in copies:
        cp.wait_send()


def _sibling_exchange_sems(n):
    return [pltpu.SemaphoreType.DMA((n, N_CHIP)), pltpu.SemaphoreType.DMA((n, N_CHIP))]


def _sibling_exchange(grads):
    n = len(grads)

    def body(*refs):
        ins, outs = refs[:n], refs[n:2 * n]
        _sibling_exchange_start(ins, outs, *refs[2 * n:])
        _sibling_exchange_finish(ins, outs, *refs[2 * n:])

    return pl.pallas_call(
        body, name="sibling_exchange",
        in_specs=[HBM_SPEC] * n, out_specs=[HBM_SPEC] * n,
        out_shape=[_hbm_like(g, (N_CHIP,) + g.shape[2:]) for g in grads],
        scratch_shapes=_sibling_exchange_sems(n),
    )(*[_in_hbm(g) for g in grads])


def _chip_exchange_start(sums, landing, send_sems, recv_sems):
    x, y, c, others = _mesh_place()
    me = 2 * x + y
    for w in range(len(sums)):
        for k, (ox, oy) in enumerate(others):
            _remote(sums[w].at[2 * ox + oy], landing[w].at[me], send_sems.at[w, k], recv_sems.at[w, k],
                    (ox, oy, c)).start()


def _chip_exchange_finish(sums, landing, send_sems, recv_sems):
    x, y, c, others = _mesh_place()
    for w in range(len(sums)):
        for k, (ox, oy) in enumerate(others):
            piece = landing[w].at[2 * ox + oy]
            _remote(piece, piece, send_sems.at[w, k], recv_sems.at[w, k], (x, y, c)).wait_recv()
    for w in range(len(sums)):
        for k, (ox, oy) in enumerate(others):
            piece = sums[w].at[2 * ox + oy]
            _remote(piece, piece, send_sems.at[w, k], recv_sems.at[w, k], (x, y, c)).wait_send()


def _chip_exchange_sems(n):
    return [pltpu.SemaphoreType.DMA((n, 3)), pltpu.SemaphoreType.DMA((n, 3))]


def _sibling_allgather(bufs, also):
    n = len(bufs)
    k_in, k_out = len(also.operands), also.n_out

    def body(*refs):
        ex_ins, refs = refs[n:n + k_in], refs[n + k_in:]
        outs, refs = refs[:n], refs[n:]
        ex_outs, refs = refs[:k_out], refs[k_out:]
        send_sems, recv_sems, ex_sems = refs[0], refs[1], refs[2:]
        x, y, c, _ = _mesh_place()
        sibling = (x, y, 1 - c)
        also.start(ex_ins, ex_outs, ex_sems)
        sends = [_remote(outs[w].at[c], outs[w].at[c], send_sems.at[w], recv_sems.at[w], sibling) for w in range(n)]
        for cp in sends:
            cp.start()
        for w in range(n):
            landed = outs[w].at[1 - c]
            _remote(landed, landed, send_sems.at[w], recv_sems.at[w], sibling).wait_recv()
        for cp in sends:
            cp.wait_send()
        also.finish(ex_ins, ex_outs, ex_sems)

    res = pl.pallas_call(
        body, name="sibling_allgather",
        in_specs=[HBM_SPEC] * (n + k_in), out_specs=[HBM_SPEC] * (n + k_out),
        out_shape=[_hbm_like(b) for b in bufs] + also.out_shape,
        input_output_aliases={**{w: w for w in range(n)}, **{n + i: n + o for i, o in also.aliases.items()}},
        scratch_shapes=[pltpu.SemaphoreType.DMA((n,)), pltpu.SemaphoreType.DMA((n,))] + also.sems,
    )(*bufs, *[_in_hbm(o) for o in also.operands])
    return list(res[:n]), list(res[n:])


def _pair_sum(grad, other, place):
    _, _, h, cols = grad.shape
    tr = _row_tile(h)

    def body(place_ref, g_ref, o_ref, sums_ref, own_ref):
        s = (g_ref[0, 0] + o_ref[0]).astype(BF16)
        sums_ref[0] = s

        @pl.when(pl.program_id(1) == place_ref[0])
        def _():
            own_ref[0] = s

    return pl.pallas_call(
        body, name="pair_sum",
        grid_spec=pltpu.PrefetchScalarGridSpec(
            num_scalar_prefetch=1, grid=(h // tr, N_CHIP),
            in_specs=[pl.BlockSpec((1, 1, tr, cols), lambda r, j, place_ref: (j, place_ref[1], r, 0)),
                      pl.BlockSpec((1, tr, cols), lambda r, j, place_ref: (j, r, 0))],
            out_specs=[pl.BlockSpec((1, tr, cols), lambda r, j, place_ref: (j, r, 0)),
                       pl.BlockSpec((1, tr, cols), lambda r, j, place_ref: (place_ref[0], r, 0))]),
        out_shape=[pltpu.HBM((N_CHIP, h, cols), BF16)] * 2,
        compiler_params=_params(32, 2),
    )(place, _in_hbm(grad), _in_hbm(other))


def _chip_sum(parts, place):
    _, h, cols = parts.shape
    tr = _row_tile(h)

    def body(place_ref, p_ref, out_ref):
        out_ref[0] = ((p_ref[0].astype(F32) + p_ref[1].astype(F32)) + p_ref[2].astype(F32)) + p_ref[3].astype(F32)

    return pl.pallas_call(
        body, name="chip_sum",
        grid_spec=pltpu.PrefetchScalarGridSpec(
            num_scalar_prefetch=1, grid=(h // tr,),
            in_specs=[pl.BlockSpec((N_CHIP, tr, cols), lambda r, place_ref: (0, r, 0))],
            out_specs=pl.BlockSpec((1, tr, cols), lambda r, place_ref: (place_ref[1], r, 0))),
        out_shape=pltpu.HBM((2, h, cols), F32),
        compiler_params=_params(32),
    )(place, _in_hbm(parts))


def _adamw_math(w, g, m, v):
    m = ADAM_B1 * m + (1.0 - ADAM_B1) * g
    v = ADAM_B2 * v + (1.0 - ADAM_B2) * (g * g)
    m_hat = m / (1.0 - ADAM_B1 ** ADAM_STEP)
    v_hat = v / (1.0 - ADAM_B2 ** ADAM_STEP)
    delta = -ADAM_LR * (m_hat / (jnp.sqrt(v_hat) + ADAM_EPS) + ADAM_WD * w)
    return delta, m, v


def _adamw(w, g, m, v, exchange=None):
    rows, cols = w.shape
    tr = _row_tile(rows)

    def body(w_ref, g_ref, m_ref, v_ref, d_ref, nm_ref, nv_ref, g_out_ref):
        g = g_ref[...]
        d_ref[...], nm_ref[...], nv_ref[...] = _adamw_math(w_ref[...], g, m_ref[...], v_ref[...])
        g_out_ref[...] = g

    spec = pl.BlockSpec((tr, cols), lambda r: (r, 0))
    return _call(
        body, (w, g, m, v), grid=(rows // tr,), name="adamw",
        in_specs=[spec] * 4, out_specs=[spec] * 4,
        out_shape=[jax.ShapeDtypeStruct((rows, cols), F32)] * 4,
        compiler_params=_params(48), exchange=exchange)


SMALL_NAMES = ("norm1_gain", "gmlp_v_gain", "w_spatial", "b_spatial", "attn_sinks", "rel_bias_table", "norm2_gain",
               "final_gain")
PACK_TILE = 8 * 128


def _pack_small(arrays):
    parts = []
    for a in arrays:
        flat = a.reshape(-1)
        rows = -(-flat.shape[0] // PACK_TILE) * 8
        parts.append(jnp.pad(flat, (0, rows * 128 - flat.shape[0])).reshape(rows, 128))
    return jnp.concatenate(parts, axis=0)


def _unpack_small(packed, like):
    out, row = [], 0
    for a in like:
        size = math.prod(a.shape)
        rows = -(-size // PACK_TILE) * 8
        out.append(packed[row:row + rows].reshape(-1)[:size].reshape(a.shape))
        row += rows
    return out


def _small_update(gathered, w, m, v):
    rows = gathered.shape[1]

    def body(g_ref, w_ref, m_ref, v_ref, tot_ref, d_ref, nm_ref, nv_ref):
        total = g_ref[0].astype(F32)
        for dev in range(1, 8):
            total = total + g_ref[dev].astype(F32)
        tot_ref[...] = total
        d_ref[...], nm_ref[...], nv_ref[...] = _adamw_math(w_ref[...], total, m_ref[...], v_ref[...])

    return pl.pallas_call(
        body, name="small_update",
        in_specs=[VMEM_SPEC] * 4, out_specs=[VMEM_SPEC] * 4,
        out_shape=[jax.ShapeDtypeStruct((rows, 128), F32)] * 4,
        compiler_params=pltpu.CompilerParams(vmem_limit_bytes=24 * MIB),
    )(gathered, w, m, v)


def _halves(a):
    return a.reshape(a.shape[:-2] + (2, a.shape[-2] // 2, a.shape[-1]))


def _whole(a):
    return a.reshape(a.shape[:-3] + (2 * a.shape[-2], a.shape[-1]))


def kernel(x, p, norm1_gain, w_in, gmlp_v_gain, w_spatial, b_spatial, attn_sinks, rel_bias_table, w_out, norm2_gain, w_ff1, w_ff2, w_ple_proj, w_ple_gate, final_gain, loss_target, m_norm1_gain, m_w_in, m_gmlp_v_gain, m_w_spatial, m_b_spatial, m_attn_sinks, m_rel_bias_table, m_w_out, m_norm2_gain, m_w_ff1, m_w_ff2, m_w_ple_proj, m_w_ple_gate, m_final_gain, v_norm1_gain, v_w_in, v_gmlp_v_gain, v_w_spatial, v_b_spatial, v_attn_sinks, v_rel_bias_table, v_w_out, v_norm2_gain, v_w_ff1, v_w_ff2, v_w_ple_proj, v_w_ple_gate, v_final_gain):
    given = dict(locals())
    small = {n: given[n] for n in SMALL_NAMES}
    chip = 2 * lax.axis_index("x") + lax.axis_index("y")
    place = jnp.stack([chip, lax.axis_index("c")]).astype(jnp.int32)

    big_names = ("w_in", "w_out", "w_ff1", "w_ff2", "w_ple_proj", "w_ple_gate")
    shards = {n: given[n][0] for n in big_names}
    travel = dict(shards, w_in=jnp.transpose(shards["w_in"]))
    rest = [n for n in big_names if n != "w_in"]
    cast, gathered = _cast_shards_beside_gather([travel[n] for n in rest], place[:1],
                                                [_cast_shard(travel["w_in"], place[:1])])
    bufs = dict(zip(rest + ["w_in"], cast + gathered))
    dx, landed, exchange_in, small_grads, sq = _step(x[0], p[0, 0], loss_target[0], small, bufs, place)

    out_grad, out_delta, out_m, out_v = {}, {}, {}, {}

    def update(n, g, exchange=None):
        to = jnp.transpose if n == "w_in" else (lambda a: a)
        (delta, new_m, new_v, g_out), got = _adamw(to(shards[n]), g, to(given["m_" + n][0]), to(given["v_" + n][0]),
                                                   exchange)
        out_grad[n], out_delta[n], out_m[n], out_v[n] = [to(a)[None] for a in (g_out, delta, new_m, new_v)]
        return got

    spare = jnp.zeros((8, 128), F32)
    small_packed = _pack_small([small_grads[n] for n in SMALL_NAMES] + [spare]).astype(BF16)
    early = [n for n in big_names if n != "w_in"]
    reduced, (small_gathered, sq_gathered, landed_in) = _sibling_allgather(
        [_chip_sum(landed[n], place) for n in early], _Both(_Both(_GatherAll(small_packed), _GatherAll(sq)), exchange_in))
    for n, r in zip(early, reduced):
        update(n, _whole(r))
    (reduced_in,), _ = _sibling_allgather([_chip_sum(landed_in, place)], _Nothing())
    update("w_in", _whole(reduced_in))

    like = [given[n] for n in SMALL_NAMES] + [spare]
    packed = _small_update(small_gathered, *[_pack_small([given[pre + n] for n in SMALL_NAMES] + [spare])
                                             for pre in ("", "m_", "v_")])
    for res, out in zip(packed, (out_grad, out_delta, out_m, out_v)):
        out.update(zip(SMALL_NAMES, _unpack_small(res, like)))
    loss = 0.5 * jnp.sum(sq_gathered[:, 0, 0]) / D

    order = ("norm1_gain", "w_in", "gmlp_v_gain", "w_spatial", "b_spatial", "attn_sinks", "rel_bias_table", "w_out",
             "norm2_gain", "w_ff1", "w_ff2", "w_ple_proj", "w_ple_gate", "final_gain")
    return (loss, dx[None], *[out_grad[n] for n in order], *[out_delta[n] for n in order],
            *[out_m[n] for n in order], *[out_v[n] for n in order])
```

```python
import functools
import math

import jax
import jax.numpy as jnp
from jax import lax
from jax.experimental import pallas as pl
from jax.experimental.pallas import tpu as pltpu

S = 2048
D = 1024
D_IN = 1792
D_FF = 4096
PLE = 256
N_CHIP = 4
N_GROUP = 4
CHUNK = 128
N_HEAD = 8
N_BLOCK = S // CHUNK
N_BUCKET = 32
EPS = 1e-6
NEG_INF = -1e30
QK_SCALE = 0.125
GELU_C = math.sqrt(2.0 / math.pi)

ADAM_LR = 0.001
ADAM_B1 = 0.9
ADAM_B2 = 0.999
ADAM_EPS = 1e-08
ADAM_WD = 0.01
ADAM_STEP = 10

F32 = jnp.float32
BF16 = jnp.bfloat16
MIB = 1024 * 1024
MESH = pl.DeviceIdType.MESH

NT = (((1,), (1,)), ((), ()))
TN = (((0,), (0,)), ((), ()))


def _dot(a, b):
    return jnp.dot(a, b, preferred_element_type=F32)


def _dot_nt(a, b):
    return lax.dot_general(a, b, NT, preferred_element_type=F32)


def _dot_tn(a, b):
    return lax.dot_general(a, b, TN, preferred_element_type=F32)


def _params(vmem_mib, n_axes=1):
    return pltpu.CompilerParams(dimension_semantics=("arbitrary",) * n_axes, vmem_limit_bytes=vmem_mib * MIB)


def _rms_scale(v):
    return lax.rsqrt(jnp.mean(v * v, axis=-1, keepdims=True) + EPS)


def _rms_bwd(dy_gain, xhat, r):
    return r * (dy_gain - xhat * jnp.mean(dy_gain * xhat, axis=-1, keepdims=True))


class _Gather:
    def __init__(self, bufs):
        self.operands = list(bufs)
        self.n_out = len(self.operands)
        self.out_shape = [_hbm_like(b) for b in bufs]
        self.aliases = {w: w for w in range(self.n_out)}
        self.sems = _gather_sems(self.n_out)

    def start(self, ins, outs, sems):
        _gather_start(outs, *sems)

    def finish(self, ins, outs, sems):
        _gather_finish(outs, *sems)


class _RelayGather(_Gather):
    TOP, BOTTOM = 6, 7
    DIAGONAL_PASSED = 5
    MIDDLE_AT, LATE_AT = (5, 8), (7, 8)

    def __init__(self, bufs):
        super().__init__(bufs)
        self.sems = [pltpu.SemaphoreType.DMA((self.n_out, 8)), pltpu.SemaphoreType.DMA((self.n_out, 8))]

    def _copies(self, bufs, send_sems, recv_sems):
        x, y, c, others = _mesh_place()
        me = 2 * x + y
        idx = [2 * ox + oy for ox, oy in others]
        sibling = (x, y, 1 - c)
        direct, passed, relayed = [], [], []
        for w, buf in enumerate(bufs):
            rows = buf.shape[2] // 2
            upper, lower = pl.ds(0, rows), pl.ds(rows, rows)
            for k in (0, 1):
                mine = buf.at[me, c]
                direct.append((_remote(mine, mine, send_sems.at[w, k], recv_sems.at[w, k], (*others[k], c)),
                               buf.at[idx[k], c], w, k))
            for k in (0, 1, 2):
                here = buf.at[idx[k], c]
                passed.append((_remote(here, here, send_sems.at[w, 3 + k], recv_sems.at[w, 3 + k], sibling),
                               buf.at[idx[k], 1 - c], w, 3 + k))
            from_x, from_y = buf.at[idx[0], c, upper], buf.at[idx[1], c, lower]
            relayed.append((_remote(from_x, from_x, send_sems.at[w, self.TOP], recv_sems.at[w, self.TOP],
                                    (*others[1], c)), buf.at[idx[2], c, upper], w, self.TOP))
            relayed.append((_remote(from_y, from_y, send_sems.at[w, self.BOTTOM], recv_sems.at[w, self.BOTTOM],
                                    (*others[0], c)), buf.at[idx[2], c, lower], w, self.BOTTOM))
        return direct, passed, relayed

    @staticmethod
    def _landed(piece, send_sems, recv_sems, w, col):
        x, y, c, _ = _mesh_place()
        _remote(piece, piece, send_sems.at[w, col], recv_sems.at[w, col], (x, y, c)).wait_recv()

    def start(self, ins, outs, sems):
        for cp, _, _, _ in self._copies(outs, *sems)[0]:
            cp.start()

    def middle(self, ins, outs, sems):
        direct, passed, relayed = self._copies(outs, *sems)
        for _, piece, w, col in direct:
            self._landed(piece, *sems, w, col)
        for cp, _, _, col in passed:
            if col != self.DIAGONAL_PASSED:
                cp.start()
        for cp, _, _, _ in relayed:
            cp.start()

    def late(self, ins, outs, sems):
        direct, passed, relayed = self._copies(outs, *sems)
        for _, piece, w, col in relayed:
            self._landed(piece, *sems, w, col)
        for cp, _, _, col in passed:
            if col == self.DIAGONAL_PASSED:
                cp.start()

    def finish(self, ins, outs, sems):
        direct, passed, relayed = self._copies(outs, *sems)
        for _, piece, w, col in passed:
            self._landed(piece, *sems, w, col)
        for cp, _, _, _ in direct + passed + relayed:
            cp.wait_send()


class _ChipExchange:
    def __init__(self, sums, landing):
        self.n_out = len(landing)
        self.operands = list(sums) + list(landing)
        self.out_shape = [_hbm_like(b) for b in landing]
        self.aliases = {self.n_out + w: w for w in range(self.n_out)}
        self.sems = _chip_exchange_sems(self.n_out)

    def start(self, ins, outs, sems):
        _chip_exchange_start(ins[:self.n_out], outs, *sems)

    def finish(self, ins, outs, sems):
        _chip_exchange_finish(ins[:self.n_out], outs, *sems)


class _GatherAll:
    def __init__(self, packed):
        self.operands = [packed]
        self.n_out = 1
        self.out_shape = [_hbm_like(packed, (8,) + packed.shape)]
        self.aliases = {}
        self.sems = [pltpu.SemaphoreType.DMA((8,)), pltpu.SemaphoreType.DMA((8,))]

    def _copies(self, ins, outs, sems):
        x, y, c, _ = _mesh_place()
        me = 4 * x + 2 * y + c
        send_sems, recv_sems = sems
        copies = []
        for k in range(1, 8):
            peer = (1 - x if k // 4 else x, 1 - y if (k // 2) % 2 else y, 1 - c if k % 2 else c)
            src = 4 * peer[0] + 2 * peer[1] + peer[2]
            copies.append((_remote(ins[0], outs[0].at[me], send_sems.at[k], recv_sems.at[k], peer), outs[0].at[src]))
        own = pltpu.make_async_copy(ins[0], outs[0].at[me], send_sems.at[0])
        return own, copies

    def start(self, ins, outs, sems):
        own, copies = self._copies(ins, outs, sems)
        own.start()
        for cp, _ in copies:
            cp.start()

    def finish(self, ins, outs, sems):
        own, copies = self._copies(ins, outs, sems)
        x, y, c, _ = _mesh_place()
        for k, (cp, landed) in enumerate(copies):
            _remote(landed, landed, sems[0].at[k + 1], sems[1].at[k + 1], (x, y, c)).wait_recv()
        for cp, _ in copies:
            cp.wait_send()
        own.wait()


class _Nothing:
    operands, n_out, out_shape, aliases, sems = [], 0, [], {}, []

    def start(self, ins, outs, sems):
        pass

    def finish(self, ins, outs, sems):
        pass


class _Both:
    def __init__(self, a, b):
        self.a, self.b = a, b
        self.operands = a.operands + b.operands
        self.n_out = a.n_out + b.n_out
        self.out_shape = a.out_shape + b.out_shape
        self.aliases = dict(a.aliases)
        self.aliases.update({len(a.operands) + i: a.n_out + o for i, o in b.aliases.items()})
        self.sems = a.sems + b.sems

    def _split(self, ins, outs, sems):
        ka, na, sa = len(self.a.operands), self.a.n_out, len(self.a.sems)
        return (ins[:ka], outs[:na], sems[:sa]), (ins[ka:], outs[na:], sems[sa:])

    def start(self, ins, outs, sems):
        for ex, args in zip((self.a, self.b), self._split(ins, outs, sems)):
            ex.start(*args)

    def finish(self, ins, outs, sems):
        for ex, args in zip((self.a, self.b), self._split(ins, outs, sems)):
            ex.finish(*args)


class _SiblingExchange:
    def __init__(self, grads):
        self.operands = list(grads)
        self.n_out = len(self.operands)
        self.out_shape = [_hbm_like(g, (N_CHIP,) + g.shape[2:]) for g in grads]
        self.aliases = {}
        self.sems = _sibling_exchange_sems(self.n_out)

    def start(self, ins, outs, sems):
        _sibling_exchange_start(ins, outs, *sems)

    def finish(self, ins, outs, sems):
        _sibling_exchange_finish(ins, outs, *sems)


def _call(body, operands, *, grid, in_specs, out_specs, out_shape, name, compiler_params, scratch_shapes=(),
          exchange=None):
    operands = [o if getattr(spec, "memory_space", None) == pltpu.SMEM else _in_hbm(o)
                for o, spec in zip(operands, in_specs)]
    out_shape = [pltpu.HBM(s.shape, s.dtype) for s in out_shape]
    if exchange is None:
        res = pl.pallas_call(body, grid=grid, in_specs=in_specs, out_specs=out_specs, out_shape=out_shape, name=name,
                             scratch_shapes=list(scratch_shapes), compiler_params=compiler_params)(*operands)
        return list(res), []
    n_in, n_out, n_scr = len(in_specs), len(out_specs), len(scratch_shapes)
    k_in, k_out = len(exchange.operands), exchange.n_out

    def fused(*refs):
        ins, refs = refs[:n_in], refs[n_in:]
        ex_ins, refs = refs[:k_in], refs[k_in:]
        outs, refs = refs[:n_out], refs[n_out:]
        ex_outs, refs = refs[:k_out], refs[k_out:]
        scratch, sems = refs[:n_scr], refs[n_scr:]
        ids = [pl.program_id(a) for a in range(len(grid))]
        first = functools.reduce(jnp.logical_and, [i == 0 for i in ids])
        last = functools.reduce(jnp.logical_and, [i == g - 1 for i, g in zip(ids, grid)])

        @pl.when(first)
        def _():
            exchange.start(ex_ins, ex_outs, sems)

        def at_step(numerator, denominator):
            at = (numerator * math.prod(grid)) // denominator
            place = [(at // math.prod(grid[a + 1:])) % grid[a] for a in range(len(grid))]
            return functools.reduce(jnp.logical_and, [i == p for i, p in zip(ids, place)])

        if hasattr(exchange, "middle"):
            @pl.when(at_step(*exchange.MIDDLE_AT))
            def _():
                exchange.middle(ex_ins, ex_outs, sems)

            @pl.when(at_step(*exchange.LATE_AT))
            def _():
                exchange.late(ex_ins, ex_outs, sems)

        body(*ins, *outs, *scratch)

        @pl.when(last)
        def _():
            exchange.finish(ex_ins, ex_outs, sems)

    res = pl.pallas_call(
        fused, grid=grid, name=name,
        in_specs=list(in_specs) + [HBM_SPEC] * k_in, out_specs=list(out_specs) + [HBM_SPEC] * k_out,
        out_shape=list(out_shape) + exchange.out_shape,
        input_output_aliases={n_in + i: n_out + o for i, o in exchange.aliases.items()},
        scratch_shapes=list(scratch_shapes) + exchange.sems, compiler_params=compiler_params,
    )(*operands, *[_in_hbm(o) for o in exchange.operands])
    return list(res[:n_out]), list(res[n_out:])


def _in_hbm(a):
    return pltpu.with_memory_space_constraint(a, pltpu.HBM)


def _row_tile(h):
    return max(t for t in range(16, 513, 16) if h % t == 0)


def _cast_shard(a, chip):
    rows, cols = a.shape
    h = rows // 2
    tr = _row_tile(h)

    def body(chip_ref, a_ref, o_ref):
        o_ref[0, 0] = a_ref[0].astype(BF16)

    return pl.pallas_call(
        body, name="cast_shard",
        grid_spec=pltpu.PrefetchScalarGridSpec(
            num_scalar_prefetch=1, grid=(2, h // tr),
            in_specs=[pl.BlockSpec((1, tr, cols), lambda s, r, chip_ref: (s, r, 0))],
            out_specs=pl.BlockSpec((1, 1, tr, cols), lambda s, r, chip_ref: (chip_ref[0], s, r, 0))),
        out_shape=pltpu.HBM((N_CHIP, 2, h, cols), BF16),
        compiler_params=_params(16, 2),
    )(chip, _in_hbm(a.reshape(2, h, cols)))


def _cast_shards_beside_gather(arrays, chip, gathered):
    n, k = len(arrays), len(gathered)
    shapes = [(a.shape[0] // 2, a.shape[1]) for a in arrays]

    def body(chip_ref, *refs):
        ins, refs = refs[:n], refs[n + k:]
        outs, refs = refs[:n], refs[n:]
        bufs, sems = refs[:k], refs[k:]
        half = pl.program_id(0)

        @pl.when(half == 0)
        def _():
            _gather_start(bufs, *sems)

        for a_ref, o_ref in zip(ins, outs):
            o_ref[0, 0] = a_ref[0].astype(BF16)

        @pl.when(half == 1)
        def _():
            _gather_finish(bufs, *sems)

    res = pl.pallas_call(
        body, name="cast_shards",
        grid_spec=pltpu.PrefetchScalarGridSpec(
            num_scalar_prefetch=1, grid=(2,),
            in_specs=[pl.BlockSpec((1, h, c), lambda s, chip_ref: (s, 0, 0)) for h, c in shapes] + [HBM_SPEC] * k,
            out_specs=[pl.BlockSpec((1, 1, h, c), lambda s, chip_ref: (chip_ref[0], s, 0, 0)) for h, c in shapes]
            + [HBM_SPEC] * k,
            scratch_shapes=_gather_sems(k)),
        out_shape=[pltpu.HBM((N_CHIP, 2, h, c), BF16) for h, c in shapes] + [_hbm_like(b) for b in gathered],
        input_output_aliases={1 + n + i: n + i for i in range(k)},
        compiler_params=_params(32),
    )(chip, *[_in_hbm(a.reshape(2, h, c)) for a, (h, c) in zip(arrays, shapes)], *gathered)
    return list(res[:n]), list(res[n:])


def _in_proj(x, gain1, w_in_t, exchange=None):
    tm = 256

    def body(x_ref, g_ref, w_ref, z_ref, hn_ref):
        xv = x_ref[...]
        hn = (xv * _rms_scale(xv) * g_ref[...]).astype(BF16)
        hn_ref[...] = hn
        z_ref[...] = _dot_nt(hn, w_ref[...])

    return _call(
        body, (x, gain1, w_in_t), grid=(S // tm,), name="in_proj",
        in_specs=[pl.BlockSpec((tm, D), lambda i: (i, 0)), pl.BlockSpec((1, D), lambda i: (0, 0)),
                  pl.BlockSpec((D_IN, D), lambda i: (0, 0))],
        out_specs=[pl.BlockSpec((tm, D_IN), lambda i: (i, 0)), pl.BlockSpec((tm, D), lambda i: (i, 0))],
        out_shape=[jax.ShapeDtypeStruct((S, D_IN), F32), jax.ShapeDtypeStruct((S, D), BF16)],
        compiler_params=_params(40), exchange=exchange)


def _gelu_parts(v):
    t = jnp.tanh(GELU_C * (v + 0.044715 * (v * v * v)))
    cdf = 0.5 * (1.0 + t)
    return cdf, t


def _band_mask(n):
    a = lax.broadcasted_iota(jnp.int32, (CHUNK, 2 * CHUNK), 0)
    j = lax.broadcasted_iota(jnp.int32, (CHUNK, 2 * CHUNK), 1)
    dist = CHUNK + a - j
    valid = (dist >= 0) & (dist < CHUNK)
    return valid & ((n > 0) | (j >= CHUNK))


def _fill_bias(bucket_ref, table_ref, bias_ref):
    bucket = bucket_ref[...]
    for h in range(N_HEAD):
        acc = jnp.zeros((CHUNK, 2 * CHUNK), F32)
        for b in range(N_BUCKET):
            acc = jnp.where(bucket == b, table_ref[b, h], acc)
        bias_ref[h] = acc


def _fill_tril(ws_ref, wt_ref, wtt_ref=None):
    r = lax.broadcasted_iota(jnp.int32, (CHUNK, CHUNK), 0)
    c = lax.broadcasted_iota(jnp.int32, (CHUNK, CHUNK), 1)
    for g in range(N_GROUP):
        w = jnp.where(c <= r, ws_ref[g], 0.0)
        wt_ref[g] = w.astype(BF16)
        if wtt_ref is not None:
            wtt_ref[g] = w.T.astype(BF16)


def _kv_layouts(kv_prev, kv_cur):
    both = jnp.concatenate([kv_prev, kv_cur], axis=0)
    k = both[:, :128]
    v = both[:, 128:]
    return (k.astype(BF16), pltpu.roll(k, 64, axis=1).astype(BF16),
            v.astype(BF16), pltpu.roll(v, 64, axis=1).astype(BF16))


def _head_place(h):
    pair, pos, kvh = h // 2, h % 2, h // 4
    return pair, pos, kvh == pos


def _softmax_sink(qm, k_use, bias_h, sink, valid):
    s = _dot_nt(qm, k_use) * QK_SCALE + bias_h
    s = jnp.where(valid, s, NEG_INF)
    m = jnp.maximum(jnp.max(s, axis=-1, keepdims=True), sink)
    e = jnp.exp(s - m)
    es = jnp.exp(sink - m)
    inv = 1.0 / (jnp.sum(e, axis=-1, keepdims=True) + es)
    return e * inv, es * inv


def _mixer_fwd(z, v_gain, w_spatial, b_spatial_t, sinks, rel_table, bucket, exchange=None):
    def body(z_ref, kvp_ref, gain_ref, ws_ref, bt_ref, sink_ref, table_ref, bucket_ref, out_ref, probs_ref, probs_t_ref,
             share_ref, guv_ref, dgelu_ref, bias_ref, wt_ref):
        n = pl.program_id(0)

        @pl.when(n == 0)
        def _():
            _fill_bias(bucket_ref, table_ref, bias_ref)
            _fill_tril(ws_ref, wt_ref)

        zuv = z_ref[:, :1024]
        cdf, t = _gelu_parts(zuv)
        guv = zuv * cdf
        guv_ref[...] = guv
        dgelu_ref[...] = cdf + zuv * (0.5 * (1.0 - t * t)) * (GELU_C * (1.0 + 3.0 * 0.044715 * (zuv * zuv)))
        for g in range(N_GROUP):
            vg = guv[:, 512 + 128 * g:512 + 128 * (g + 1)]
            vn = vg * _rms_scale(vg) * gain_ref[:, 128 * g:128 * (g + 1)]
            sv = _dot(wt_ref[g], vn.astype(BF16)) + bt_ref[:, g:g + 1]
            out_ref[:, 128 * g:128 * (g + 1)] = (guv[:, 128 * g:128 * (g + 1)] * sv).astype(BF16)

        k_same, k_swap, v_same, v_swap = _kv_layouts(kvp_ref[...], z_ref[:, 1536:1792])
        valid = _band_mask(n)
        lane = lax.broadcasted_iota(jnp.int32, (1, 128), 1)
        lane_half = lane // 64
        shares = jnp.zeros((CHUNK, 128), F32)
        for pair in range(N_HEAD // 2):
            qq = z_ref[:, 1024 + 128 * pair:1024 + 128 * (pair + 1)]
            acc = jnp.zeros((CHUNK, 128), F32)
            for pos in range(2):
                h = 2 * pair + pos
                _, _, same = _head_place(h)
                qm = jnp.where(lane_half == pos, qq, 0.0).astype(BF16)
                p, p_sink = _softmax_sink(qm, k_same if same else k_swap, bias_ref[h], sink_ref[h], valid)
                pb = p.astype(BF16)
                probs_ref[0, h] = pb
                probs_t_ref[0, h] = p.T.astype(BF16)
                shares = jnp.where(lane == h, p_sink, shares)
                vm = jnp.where(lane_half == pos, v_same if same else v_swap, jnp.zeros((), BF16))
                acc = acc + _dot(pb, vm)
            out_ref[:, 512 + 128 * pair:512 + 128 * (pair + 1)] = acc.astype(BF16)
        share_ref[...] = shares

    return _call(
        body, (z, z, v_gain, w_spatial, b_spatial_t, sinks, rel_table, bucket), grid=(N_BLOCK,), name="mixer_fwd",
        in_specs=[pl.BlockSpec((CHUNK, D_IN), lambda n: (n, 0)),
                  pl.BlockSpec((CHUNK, 256), lambda n: (jnp.maximum(n - 1, 0), 6)),
                  pl.BlockSpec((1, 512), lambda n: (0, 0)),
                  pl.BlockSpec((N_GROUP, CHUNK, CHUNK), lambda n: (0, 0, 0)),
                  pl.BlockSpec((CHUNK, N_GROUP), lambda n: (0, 0)),
                  pl.BlockSpec(memory_space=pltpu.SMEM),
                  pl.BlockSpec(memory_space=pltpu.SMEM),
                  pl.BlockSpec((CHUNK, 2 * CHUNK), lambda n: (0, 0))],
        out_specs=[pl.BlockSpec((CHUNK, D), lambda n: (n, 0)),
                   pl.BlockSpec((1, N_HEAD, CHUNK, 2 * CHUNK), lambda n: (n, 0, 0, 0)),
                   pl.BlockSpec((1, N_HEAD, 2 * CHUNK, CHUNK), lambda n: (n, 0, 0, 0)),
                   pl.BlockSpec((CHUNK, 128), lambda n: (n, 0)),
                   pl.BlockSpec((CHUNK, 1024), lambda n: (n, 0)), pl.BlockSpec((CHUNK, 1024), lambda n: (n, 0))],
        out_shape=[jax.ShapeDtypeStruct((S, D), BF16), jax.ShapeDtypeStruct((N_BLOCK, N_HEAD, CHUNK, 2 * CHUNK), BF16),
                   jax.ShapeDtypeStruct((N_BLOCK, N_HEAD, 2 * CHUNK, CHUNK), BF16), jax.ShapeDtypeStruct((S, 128), F32),
                   jax.ShapeDtypeStruct((S, 1024), F32), jax.ShapeDtypeStruct((S, 1024), F32)],
        scratch_shapes=[pltpu.VMEM((N_HEAD, CHUNK, 2 * CHUNK), F32), pltpu.VMEM((N_GROUP, CHUNK, CHUNK), BF16)],
        compiler_params=_params(32), exchange=exchange)


def _out_proj(x, mix, w_out, gain2, exchange=None):
    tm = 1024

    def body(x_ref, mix_ref, w_ref, g_ref, h1_ref, hn_ref, hnt_ref):
        h1 = x_ref[...] + _dot(mix_ref[...], w_ref[...])
        h1_ref[...] = h1
        hn = h1 * _rms_scale(h1) * g_ref[...]
        hn_ref[...] = hn.astype(BF16)
        hnt_ref[...] = hn.T.astype(BF16)

    return _call(
        body, (x, mix, w_out, gain2), grid=(S // tm,), name="out_proj",
        in_specs=[pl.BlockSpec((tm, D), lambda i: (i, 0)), pl.BlockSpec((tm, D), lambda i: (i, 0)),
                  pl.BlockSpec((D, D), lambda i: (0, 0)), pl.BlockSpec((1, D), lambda i: (0, 0))],
        out_specs=[pl.BlockSpec((tm, D), lambda i: (i, 0)), pl.BlockSpec((tm, D), lambda i: (i, 0)),
                   pl.BlockSpec((D, tm), lambda i: (0, i))],
        out_shape=[jax.ShapeDtypeStruct((S, D), F32), jax.ShapeDtypeStruct((S, D), BF16),
                   jax.ShapeDtypeStruct((D, S), BF16)],
        compiler_params=_params(32), exchange=exchange)


def _ffn_up(hn2, w_ff1, exchange=None):
    tm = 512
    nj = D_FF // 1024

    def body(hn_ref, w1_ref, r_ref, a_ref, at_ref):
        r = jnp.maximum(_dot(hn_ref[...], w1_ref[0]), 0.0)
        r_ref[...] = r.astype(BF16)
        a = r * r
        a_ref[...] = a.astype(BF16)
        at_ref[...] = a.T.astype(BF16)

    return _call(
        body, (hn2, w_ff1), grid=(nj, S // tm), name="ffn_up",
        in_specs=[pl.BlockSpec((tm, D), lambda j, i: (i, 0)), pl.BlockSpec((1, D, 1024), lambda j, i: (j, 0, 0))],
        out_specs=[pl.BlockSpec((tm, 1024), lambda j, i: (i, j)), pl.BlockSpec((tm, 1024), lambda j, i: (i, j)),
                   pl.BlockSpec((1024, tm), lambda j, i: (j, i))],
        out_shape=[jax.ShapeDtypeStruct((S, D_FF), BF16), jax.ShapeDtypeStruct((S, D_FF), BF16),
                   jax.ShapeDtypeStruct((D_FF, S), BF16)],
        compiler_params=_params(40, 2), exchange=exchange)


def _ffn_down(h1, a, w_ff2, exchange=None):
    tm = 512
    nj = D_FF // 1024

    def body(h1_ref, a_ref, w2_ref, h2_ref):
        h2_ref[...] = h1_ref[...] + _dot(a_ref[...], w2_ref[...].reshape(D_FF, D))

    return _call(
        body, (h1, a, w_ff2), grid=(S // tm,), name="ffn_down",
        in_specs=[pl.BlockSpec((tm, D), lambda i: (i, 0)), pl.BlockSpec((tm, D_FF), lambda i: (i, 0)),
                  pl.BlockSpec((nj, 1024, D), lambda i: (0, 0, 0), pipeline_mode=pl.Buffered(1))],
        out_specs=[pl.BlockSpec((tm, D), lambda i: (i, 0))],
        out_shape=[jax.ShapeDtypeStruct((S, D), F32)],
        compiler_params=_params(40), exchange=exchange)


def _tail(h2, p, target, w_gate, w_proj, final_gain):
    tm = 256
    steps = S // tm

    def body(h2_ref, p_ref, t_ref, wg_ref, wp_ref, gf_ref, dh2_ref, dwg_ref, dwp_ref, dgf_ref, loss_ref, dh2b_ref,
             dwp_acc):
        i = pl.program_id(0)
        h2 = h2_ref[...]
        h2b = h2.astype(BF16)
        pb = p_ref[...].astype(BF16)
        gate = jax.nn.sigmoid(_dot(h2b, wg_ref[...]))
        pp = jnp.concatenate([_dot(pb, wp_ref[j]) for j in range(N_CHIP)], axis=1)
        h3 = h2 + gate * pp
        r3 = _rms_scale(h3)
        xhat = h3 * r3
        gf = gf_ref[...]
        err = xhat * gf - t_ref[...]
        dy = err * (1.0 / D)
        dh3 = _rms_bwd(dy * gf, xhat, r3)
        dgp = (dh3 * pp * gate * (1.0 - gate)).astype(BF16)
        dpp = (dh3 * gate).astype(BF16)
        dh2 = dh3 + _dot_nt(dgp, wg_ref[...])
        dh2_ref[...] = dh2
        dh2b_ref[...] = dh2.astype(BF16)
        dwg = _dot_tn(h2b, dgp)
        dwp = _dot_tn(pb, dpp)
        dgf = jnp.sum(dy * xhat, axis=0, keepdims=True)
        sq = jnp.sum(jnp.sum(err * err, axis=1, keepdims=True), axis=0, keepdims=True)

        @pl.when(i == 0)
        def _():
            dwg_ref[...] = dwg
            dwp_acc[...] = dwp
            dgf_ref[...] = dgf
            loss_ref[...] = jnp.broadcast_to(sq, (8, 128))

        @pl.when(i > 0)
        def _():
            dwg_ref[...] += dwg
            dwp_acc[...] += dwp
            dgf_ref[...] += dgf
            loss_ref[...] += jnp.broadcast_to(sq, (8, 128))

        @pl.when(i == steps - 1)
        def _():
            for j in range(N_CHIP):
                dwp_ref[j] = dwp_acc[:, 256 * j:256 * (j + 1)]

    return _call(
        body, (h2, p, target, w_gate, w_proj, final_gain), grid=(steps,), name="tail",
        in_specs=[pl.BlockSpec((tm, D), lambda i: (i, 0)), pl.BlockSpec((tm, PLE), lambda i: (i, 0)),
                  pl.BlockSpec((tm, D), lambda i: (i, 0)), pl.BlockSpec((D, D), lambda i: (0, 0)),
                  pl.BlockSpec((N_CHIP, PLE, 256), lambda i: (0, 0, 0)), pl.BlockSpec((1, D), lambda i: (0, 0))],
        out_specs=[pl.BlockSpec((tm, D), lambda i: (i, 0)), pl.BlockSpec((D, D), lambda i: (0, 0)),
                   pl.BlockSpec((N_CHIP, PLE, 256), lambda i: (0, 0, 0)), pl.BlockSpec((1, D), lambda i: (0, 0)),
                   pl.BlockSpec((8, 128), lambda i: (0, 0)), pl.BlockSpec((tm, D), lambda i: (i, 0))],
        out_shape=[jax.ShapeDtypeStruct((S, D), F32), jax.ShapeDtypeStruct((D, D), F32),
                   jax.ShapeDtypeStruct((N_CHIP, PLE, 256), F32), jax.ShapeDtypeStruct((1, D), F32),
                   jax.ShapeDtypeStruct((8, 128), F32), jax.ShapeDtypeStruct((S, D), BF16)],
        scratch_shapes=[pltpu.VMEM((PLE, D), F32)],
        compiler_params=_params(48))[0]


def _ffn_bwd_down(dh2b, r, a_t, w_ff2, exchange=None):
    tm = 1024
    nj = D_FF // 1024

    def body(dh2_ref, r_ref, at_ref, w2_ref, df_ref, dw2_ref):
        i = pl.program_id(1)
        dh2b = dh2_ref[...]
        da = _dot_nt(dh2b, w2_ref[0])
        df_ref[...] = (da * (2.0 * r_ref[...].astype(F32))).astype(BF16)
        dw2 = _dot(at_ref[...], dh2b)

        @pl.when(i == 0)
        def _():
            dw2_ref[0] = dw2

        @pl.when(i > 0)
        def _():
            dw2_ref[0] += dw2

    return _call(
        body, (dh2b, r, a_t, w_ff2), grid=(nj, S // tm), name="ffn_bwd_down",
        in_specs=[pl.BlockSpec((tm, D), lambda j, i: (i, 0)), pl.BlockSpec((tm, 1024), lambda j, i: (i, j)),
                  pl.BlockSpec((1024, tm), lambda j, i: (j, i)), pl.BlockSpec((1, 1024, D), lambda j, i: (j, 0, 0))],
        out_specs=[pl.BlockSpec((tm, 1024), lambda j, i: (i, j)), pl.BlockSpec((1, 1024, D), lambda j, i: (j, 0, 0))],
        out_shape=[jax.ShapeDtypeStruct((S, D_FF), BF16), jax.ShapeDtypeStruct((nj, 1024, D), F32)],
        compiler_params=_params(48, 2), exchange=exchange)


def _ffn_bwd_up(df, hn2_t, exchange=None):
    tm = 2048
    nj = D_FF // 1024

    def body(df_ref, hnt_ref, dw1_ref):
        i = pl.program_id(1)
        dw1 = _dot(hnt_ref[...], df_ref[...])

        @pl.when(i == 0)
        def _():
            dw1_ref[0] = dw1

        @pl.when(i > 0)
        def _():
            dw1_ref[0] += dw1

    return _call(
        body, (df, hn2_t), grid=(nj, S // tm), name="ffn_bwd_up",
        in_specs=[pl.BlockSpec((tm, 1024), lambda j, i: (i, j)), pl.BlockSpec((D, tm), lambda j, i: (0, i))],
        out_specs=[pl.BlockSpec((1, D, 1024), lambda j, i: (j, 0, 0))],
        out_shape=[jax.ShapeDtypeStruct((nj, D, 1024), F32)],
        compiler_params=_params(40, 2), exchange=exchange)


def _ffn_bwd_input(df, w_ff1, dh2, h1, gain2, mix, w_out, exchange=None):
    tm = 512
    nj = D_FF // 1024
    steps = S // tm

    def body(df_ref, w1_ref, dh2_ref, h1_ref, g_ref, mix_ref, wo_ref, dh1_ref, dmix_ref, dwo_ref, dg_ref, acc_ref):
        i = pl.program_id(0)
        j = pl.program_id(1)
        part = _dot_nt(df_ref[...], w1_ref[j])

        @pl.when(j == 0)
        def _():
            acc_ref[...] = part

        @pl.when(j > 0)
        def _():
            acc_ref[...] += part

        @pl.when(j == nj - 1)
        def _():
            dhn = acc_ref[...]
            h1 = h1_ref[...]
            r2 = _rms_scale(h1)
            xhat = h1 * r2
            dh1 = dh2_ref[...] + _rms_bwd(dhn * g_ref[...], xhat, r2)
            dh1_ref[...] = dh1
            dh1b = dh1.astype(BF16)
            dmix_ref[...] = _dot_nt(dh1b, wo_ref[...])
            dwo = _dot_tn(mix_ref[...], dh1b)
            dg = jnp.sum(dhn * xhat, axis=0, keepdims=True)

            @pl.when(i == 0)
            def _():
                dwo_ref[...] = dwo
                dg_ref[...] = dg

            @pl.when(i > 0)
            def _():
                dwo_ref[...] += dwo
                dg_ref[...] += dg

    return _call(
        body, (df, w_ff1, dh2, h1, gain2, mix, w_out), grid=(steps, nj), name="ffn_bwd_input",
        in_specs=[pl.BlockSpec((tm, 1024), lambda i, j: (i, j)),
                  pl.BlockSpec((nj, D, 1024), lambda i, j: (0, 0, 0), pipeline_mode=pl.Buffered(1)),
                  pl.BlockSpec((tm, D), lambda i, j: (i, 0)), pl.BlockSpec((tm, D), lambda i, j: (i, 0)),
                  pl.BlockSpec((1, D), lambda i, j: (0, 0)), pl.BlockSpec((tm, D), lambda i, j: (i, 0)),
                  pl.BlockSpec((D, D), lambda i, j: (0, 0), pipeline_mode=pl.Buffered(1))],
        out_specs=[pl.BlockSpec((tm, D), lambda i, j: (i, 0)), pl.BlockSpec((tm, D), lambda i, j: (i, 0)),
                   pl.BlockSpec((D, D), lambda i, j: (0, 0)), pl.BlockSpec((1, D), lambda i, j: (0, 0))],
        out_shape=[jax.ShapeDtypeStruct((S, D), F32), jax.ShapeDtypeStruct((S, D), F32),
                   jax.ShapeDtypeStruct((D, D), F32), jax.ShapeDtypeStruct((1, D), F32)],
        scratch_shapes=[pltpu.VMEM((tm, D), F32)],
        compiler_params=_params(56, 2), exchange=exchange)


IN_GROUP = 8


def _mixer_bwd(z, dmix, v_gain, w_spatial, b_spatial_t, saved, bucket, hn1, exchange=None):
    def body(z_ref, kvp_ref, dm_ref, gain_ref, ws_ref, bt_ref, probs_ref, probs_t_ref, share_ref, guv_ref, dgelu_ref,
             bucket_ref, hn_ref,
             dz_ref, dws_ref, db_ref, dgain_ref, dsink_ref, drel_ref, dwin_ref,
             wt_ref, wtt_ref, dbias_ref, dsv_ref, carry_ref):
        n = pl.program_id(0)

        @pl.when(n == 0)
        def _():
            _fill_tril(ws_ref, wt_ref, wtt_ref)
            dwin_ref[...] = jnp.zeros_like(dwin_ref)
            dbias_ref[...] = jnp.zeros_like(dbias_ref)
            dsv_ref[...] = jnp.zeros_like(dsv_ref)
            dws_ref[...] = jnp.zeros_like(dws_ref)
            dgain_ref[...] = jnp.zeros_like(dgain_ref)
            dsink_ref[...] = jnp.zeros_like(dsink_ref)

        rows = pl.ds(pl.multiple_of(n * CHUNK, CHUNK), CHUNK)

        guv = guv_ref[...]
        dgelu = dgelu_ref[...]
        for g in range(N_GROUP):
            lo, hi = 128 * g, 128 * (g + 1)
            u = guv[:, lo:hi]
            vg = guv[:, 512 + lo:512 + hi]
            rr = _rms_scale(vg)
            vhat = vg * rr
            gain = gain_ref[:, lo:hi]
            vnb = (vhat * gain).astype(BF16)
            sv = _dot(wt_ref[g], vnb) + bt_ref[:, g:g + 1]
            da = dm_ref[:, lo:hi]
            dsv = da * u
            dsvb = dsv.astype(BF16)
            dsv_ref[g] += dsv
            dws_ref[g] += _dot_nt(dsvb, vnb)
            dvn = _dot(wtt_ref[g], dsvb)
            dgain_ref[:, lo:hi] += jnp.sum(dvn * vhat, axis=0, keepdims=True)
            dvg = _rms_bwd(dvn * gain, vhat, rr)
            dz_ref[rows, lo:hi] = (da * sv * dgelu[:, lo:hi]).astype(BF16)
            dz_ref[rows, 512 + lo:512 + hi] = (dvg * dgelu[:, 512 + lo:512 + hi]).astype(BF16)

        k_same, k_swap, v_same, v_swap = _kv_layouts(kvp_ref[...], z_ref[:, 1536:1792])
        lane_half = lax.broadcasted_iota(jnp.int32, (1, 128), 1) // 64
        zero = jnp.zeros((2 * CHUNK, 128), F32)
        dk_same, dk_swap, dv_same, dv_swap = zero, zero, zero, zero
        for pair in range(N_HEAD // 2):
            cols = slice(1024 + 128 * pair, 1024 + 128 * (pair + 1))
            qq = z_ref[:, cols]
            do_pair = dm_ref[:, 512 + 128 * pair:512 + 128 * (pair + 1)]
            dq = jnp.zeros((CHUNK, 128), F32)
            for pos in range(2):
                h = 2 * pair + pos
                _, _, same = _head_place(h)
                on_half = lane_half == pos
                qm = jnp.where(on_half, qq, 0.0).astype(BF16)
                k_use = k_same if same else k_swap
                v_use = v_same if same else v_swap
                pb = probs_ref[0, h]
                p = pb.astype(F32)
                p_sink = share_ref[:, h:h + 1]
                dom = jnp.where(on_half, do_pair, 0.0).astype(BF16)
                dp = _dot_nt(dom, v_use)
                dsum = jnp.sum(p * dp, axis=-1, keepdims=True)
                ds = p * (dp - dsum)
                dbias_ref[h] += ds
                dsink_ref[h:h + 1, :] += jnp.broadcast_to(jnp.sum(-p_sink * dsum, axis=0, keepdims=True), (1, 128))
                dsb = ds.astype(BF16)
                dq = dq + jnp.where(on_half, _dot(dsb, k_use), 0.0)
                dk_h = _dot_tn(dsb, qm)
                dv_h = _dot(probs_t_ref[0, h], dom)
                if same:
                    dk_same, dv_same = dk_same + dk_h, dv_same + dv_h
                else:
                    dk_swap, dv_swap = dk_swap + dk_h, dv_swap + dv_h
            dz_ref[rows, cols] = (dq * QK_SCALE).astype(BF16)
        dk = (dk_same + pltpu.roll(dk_swap, 64, axis=1)) * QK_SCALE
        dv = dv_same + pltpu.roll(dv_swap, 64, axis=1)
        dkv = jnp.concatenate([dk, dv], axis=1)

        @pl.when(n > 0)
        def _():
            prev_rows = pl.ds(pl.multiple_of((n - 1) * CHUNK, CHUNK), CHUNK)
            dz_ref[prev_rows, 1536:1792] = (carry_ref[...] + dkv[:CHUNK]).astype(BF16)

        carry_ref[...] = dkv[CHUNK:]

        @pl.when((n > 0) & (n % IN_GROUP == 0))
        def _():
            done = pl.ds(pl.multiple_of((n - IN_GROUP) * CHUNK, IN_GROUP * CHUNK), IN_GROUP * CHUNK)
            dwin_ref[...] += _dot_tn(dz_ref[done, :], hn_ref[...])

        @pl.when(n == N_BLOCK - 1)
        def _():
            dz_ref[rows, 1536:1792] = dkv[CHUNK:].astype(BF16)
            last = pl.ds((N_BLOCK - IN_GROUP) * CHUNK, IN_GROUP * CHUNK)
            dwin_ref[...] += _dot_tn(dz_ref[last, :], hn_ref[...])
            r = lax.broadcasted_iota(jnp.int32, (CHUNK, CHUNK), 0)
            c = lax.broadcasted_iota(jnp.int32, (CHUNK, CHUNK), 1)
            for g in range(N_GROUP):
                dws_ref[g] = jnp.where(c <= r, dws_ref[g], 0.0)
                db_ref[g] = jnp.sum(dsv_ref[g], axis=1, keepdims=True)
            bucket = bucket_ref[...]
            for h in range(N_HEAD):
                dbh = dbias_ref[h]
                per_bucket = [jnp.sum(jnp.where(bucket == b, dbh, 0.0), axis=0, keepdims=True) for b in range(N_BUCKET)]
                drel_ref[h] = jnp.sum(jnp.concatenate(per_bucket, axis=0), axis=1, keepdims=True)

    def hn_group(n):
        return jnp.where(n == N_BLOCK - 1, N_BLOCK // IN_GROUP - 1, jnp.maximum(n // IN_GROUP - 1, 0))

    return _call(
        body, (z, z, dmix, v_gain, w_spatial, b_spatial_t, *saved, bucket, hn1), grid=(N_BLOCK,),
        name="mixer_bwd",
        in_specs=[pl.BlockSpec((CHUNK, D_IN), lambda n: (n, 0)),
                  pl.BlockSpec((CHUNK, 256), lambda n: (jnp.maximum(n - 1, 0), 6)),
                  pl.BlockSpec((CHUNK, D), lambda n: (n, 0)),
                  pl.BlockSpec((1, 512), lambda n: (0, 0)),
                  pl.BlockSpec((N_GROUP, CHUNK, CHUNK), lambda n: (0, 0, 0)),
                  pl.BlockSpec((CHUNK, N_GROUP), lambda n: (0, 0)),
                  pl.BlockSpec((1, N_HEAD, CHUNK, 2 * CHUNK), lambda n: (n, 0, 0, 0)),
                  pl.BlockSpec((1, N_HEAD, 2 * CHUNK, CHUNK), lambda n: (n, 0, 0, 0)),
                  pl.BlockSpec((CHUNK, 128), lambda n: (n, 0)),
                  pl.BlockSpec((CHUNK, 1024), lambda n: (n, 0)), pl.BlockSpec((CHUNK, 1024), lambda n: (n, 0)),
                  pl.BlockSpec((CHUNK, 2 * CHUNK), lambda n: (0, 0)),
                  pl.BlockSpec((IN_GROUP * CHUNK, D), lambda n: (hn_group(n), 0))],
        out_specs=[pl.BlockSpec((S, D_IN), lambda n: (0, 0)),
                   pl.BlockSpec((N_GROUP, CHUNK, CHUNK), lambda n: (0, 0, 0)),
                   pl.BlockSpec((N_GROUP, CHUNK, 1), lambda n: (0, 0, 0)),
                   pl.BlockSpec((1, 512), lambda n: (0, 0)),
                   pl.BlockSpec((N_HEAD, 128), lambda n: (0, 0)),
                   pl.BlockSpec((N_HEAD, N_BUCKET, 1), lambda n: (0, 0, 0)),
                   pl.BlockSpec((D_IN, D), lambda n: (0, 0))],
        out_shape=[jax.ShapeDtypeStruct((S, D_IN), BF16), jax.ShapeDtypeStruct((N_GROUP, CHUNK, CHUNK), F32),
                   jax.ShapeDtypeStruct((N_GROUP, CHUNK, 1), F32), jax.ShapeDtypeStruct((1, 512), F32),
                   jax.ShapeDtypeStruct((N_HEAD, 128), F32), jax.ShapeDtypeStruct((N_HEAD, N_BUCKET, 1), F32),
                   jax.ShapeDtypeStruct((D_IN, D), F32)],
        scratch_shapes=[pltpu.VMEM((N_GROUP, CHUNK, CHUNK), BF16),
                        pltpu.VMEM((N_GROUP, CHUNK, CHUNK), BF16), pltpu.VMEM((N_HEAD, CHUNK, 2 * CHUNK), F32),
                        pltpu.VMEM((N_GROUP, CHUNK, CHUNK), F32), pltpu.VMEM((CHUNK, 256), F32)],
        compiler_params=_params(56), exchange=exchange)


def _in_bwd_input(dz, w_in_t, x, dh1, gain1, exchange=None):
    tm = 512

    def body(dz_ref, w_ref, x_ref, dh1_ref, g_ref, dx_ref, dg_ref):
        i = pl.program_id(0)
        dhn = _dot(dz_ref[...], w_ref[...])
        xv = x_ref[...]
        r1 = _rms_scale(xv)
        xhat = xv * r1
        dx_ref[...] = dh1_ref[...] + _rms_bwd(dhn * g_ref[...], xhat, r1)
        dg = jnp.sum(dhn * xhat, axis=0, keepdims=True)

        @pl.when(i == 0)
        def _():
            dg_ref[...] = dg

        @pl.when(i > 0)
        def _():
            dg_ref[...] += dg

    return _call(
        body, (dz, w_in_t, x, dh1, gain1), grid=(S // tm,), name="in_bwd_input",
        in_specs=[pl.BlockSpec((tm, D_IN), lambda i: (i, 0)), pl.BlockSpec((D_IN, D), lambda i: (0, 0)),
                  pl.BlockSpec((tm, D), lambda i: (i, 0)), pl.BlockSpec((tm, D), lambda i: (i, 0)),
                  pl.BlockSpec((1, D), lambda i: (0, 0))],
        out_specs=[pl.BlockSpec((tm, D), lambda i: (i, 0)), pl.BlockSpec((1, D), lambda i: (0, 0))],
        out_shape=[jax.ShapeDtypeStruct((S, D), F32), jax.ShapeDtypeStruct((1, D), F32)],
        compiler_params=_params(48), exchange=exchange)


def _rel_bucket():
    a = jnp.arange(CHUNK)[:, None]
    j = jnp.arange(2 * CHUNK)[None, :]
    n = jnp.maximum(CHUNK + a - j, 0)
    max_exact = N_BUCKET // 2
    nf = jnp.maximum(n, 1).astype(jnp.float32)
    large = max_exact + (jnp.log(nf / max_exact) / math.log(CHUNK / max_exact) * (N_BUCKET - max_exact)).astype(jnp.int32)
    large = jnp.minimum(large, N_BUCKET - 1)
    return jnp.where(n < max_exact, n, large).astype(jnp.int32)


def _step(x, p, target, small, bufs, place):
    bucket = _rel_bucket()
    sinks = small["attn_sinks"].reshape(N_HEAD)
    b_t = jnp.transpose(small["b_spatial"].reshape(N_GROUP, CHUNK))
    ws = small["w_spatial"].reshape(N_GROUP, CHUNK, CHUNK)
    gain1, gain2 = small["norm1_gain"], small["norm2_gain"]
    v_gain = small["gmlp_v_gain"]
    final_gain = small["final_gain"].reshape(1, D)
    table = small["rel_bias_table"]
    bufs = dict(bufs)

    def gather(*names):
        return _RelayGather([bufs[n] for n in names])

    def took(names, got):
        bufs.update(zip(names, got))

    w_in_t = _whole(bufs["w_in"]).reshape(D_IN, D)
    (z, hn1), got = _in_proj(x, gain1, w_in_t, gather("w_out"))
    took(["w_out"], got)
    (mix, *saved), got = _mixer_fwd(z, v_gain, ws, b_t, sinks, table, bucket, gather("w_ff1"))
    took(["w_ff1"], got)
    w_out = _whole(bufs["w_out"]).reshape(D, D)
    (h1, hn2, hn2_t), _ = _out_proj(x, mix, w_out, gain2)
    w_ff1 = _whole(bufs["w_ff1"])
    (r, a, a_t), got = _ffn_up(hn2, w_ff1, gather("w_ff2"))
    took(["w_ff2"], got)
    w_ff2 = _whole(bufs["w_ff2"])
    (h2,), got = _ffn_down(h1, a, w_ff2, gather("w_ple_gate", "w_ple_proj"))
    took(["w_ple_gate", "w_ple_proj"], got)
    dh2, d_gate, d_proj, d_final, sq, dh2b = _tail(h2, p, target, _whole(bufs["w_ple_gate"]).reshape(D, D),
                                                   _whole(bufs["w_ple_proj"]), final_gain)

    def pair_sums(halves, from_sibling):
        sums, landing = zip(*[_pair_sum(g, o, place) for g, o in zip(halves, from_sibling)])
        return list(sums), list(landing)

    landed = {}
    halves = [_halves(d_gate.reshape(N_CHIP, 256, D)), _halves(d_proj)]
    (df, d_ff2), got = _ffn_bwd_down(dh2b, r, a_t, w_ff2, _SiblingExchange(halves))
    ex, halves = _ChipExchange(*pair_sums(halves, got)), [_halves(d_ff2)]
    (d_ff1,), got = _ffn_bwd_up(df, hn2_t, _Both(ex, _SiblingExchange(halves)))
    landed.update(zip(["w_ple_gate", "w_ple_proj"], got[:2]))
    ex, halves = _ChipExchange(*pair_sums(halves, got[2:])), [_halves(d_ff1)]
    (dh1, dmix, d_out, d_gain2), got = _ffn_bwd_input(df, w_ff1, dh2, h1, gain2, mix, w_out,
                                                      _Both(ex, _SiblingExchange(halves)))
    landed["w_ff2"] = got[0]
    ex, halves = _ChipExchange(*pair_sums(halves, got[1:])), [_halves(d_out.reshape(N_CHIP, 256, D))]
    (dz, d_ws, d_b, d_vgain, d_sink, d_rel, d_in_t), got = _mixer_bwd(z, dmix, v_gain, ws, b_t, saved, bucket, hn1,
                                                                     _Both(ex, _SiblingExchange(halves)))
    landed["w_ff1"] = got[0]
    small_grads = {
        "gmlp_v_gain": d_vgain, "w_spatial": d_ws.reshape(1, N_GROUP, CHUNK, CHUNK),
        "b_spatial": d_b.reshape(1, N_GROUP, CHUNK), "attn_sinks": d_sink[:, 0].reshape(1, N_HEAD),
        "rel_bias_table": jnp.transpose(d_rel.reshape(N_HEAD, N_BUCKET)), "norm2_gain": d_gain2,
        "final_gain": d_final.reshape(D),
    }
    ex, halves = _ChipExchange(*pair_sums(halves, got[1:])), [_halves(d_in_t.reshape(N_CHIP, 448, D))]
    (dx, small_grads["norm1_gain"]), got = _in_bwd_input(dz, w_in_t, x, dh1, gain1, _Both(ex, _SiblingExchange(halves)))
    landed["w_out"] = got[0]
    return dx, landed, _ChipExchange(*pair_sums(halves, got[1:])), small_grads, sq


HBM_SPEC = pl.BlockSpec(memory_space=pltpu.HBM)
VMEM_SPEC = pl.BlockSpec(memory_space=pltpu.VMEM)


def _mesh_place():
    x, y, c = lax.axis_index("x"), lax.axis_index("y"), lax.axis_index("c")
    others = [(1 - x, y), (x, 1 - y), (1 - x, 1 - y)]
    return x, y, c, others


def _remote(src, dst, send_sem, recv_sem, device):
    return pltpu.make_async_remote_copy(src_ref=src, dst_ref=dst, send_sem=send_sem, recv_sem=recv_sem,
                                        device_id=device, device_id_type=MESH)


def _hbm_like(a, shape=None, dtype=None):
    return pltpu.HBM(a.shape if shape is None else shape, a.dtype if dtype is None else dtype)


def _gather_start(bufs, send_sems, recv_sems):
    x, y, c, others = _mesh_place()
    me = 2 * x + y
    for w, buf in enumerate(bufs):
        for k in range(3):
            mine = buf.at[me, c]
            _remote(mine, mine, send_sems.at[w, k], recv_sems.at[w, k], (*others[k], c)).start()


def _gather_finish(bufs, send_sems, recv_sems):
    x, y, c, others = _mesh_place()
    me = 2 * x + y
    sibling = (x, y, 1 - c)
    idx = [2 * ox + oy for ox, oy in others]
    chips = range(3)
    for w, buf in enumerate(bufs):
        for k in chips:
            landed = buf.at[idx[k], c]
            _remote(landed, landed, send_sems.at[w, k], recv_sems.at[w, k], sibling).wait_recv()
            _remote(landed, landed, send_sems.at[w, 3 + k], recv_sems.at[w, 3 + k], sibling).start()
    for w, buf in enumerate(bufs):
        for k in chips:
            landed = buf.at[idx[k], 1 - c]
            _remote(landed, landed, send_sems.at[w, 3 + k], recv_sems.at[w, 3 + k], sibling).wait_recv()
    for w, buf in enumerate(bufs):
        for k in chips:
            mine, passed = buf.at[me, c], buf.at[idx[k], c]
            _remote(mine, mine, send_sems.at[w, k], recv_sems.at[w, k], sibling).wait_send()
            _remote(passed, passed, send_sems.at[w, 3 + k], recv_sems.at[w, 3 + k], sibling).wait_send()


def _gather_sems(n):
    return [pltpu.SemaphoreType.DMA((n, 6)), pltpu.SemaphoreType.DMA((n, 6))]


def _sibling_copies(grads, landing, send_sems, recv_sems):
    x, y, c, _ = _mesh_place()
    return [_remote(grads[w].at[j, 1 - c], landing[w].at[j], send_sems.at[w, j], recv_sems.at[w, j], (x, y, 1 - c))
            for w in range(len(grads)) for j in range(N_CHIP)]


def _sibling_exchange_start(grads, landing, send_sems, recv_sems):
    for cp in _sibling_copies(grads, landing, send_sems, recv_sems):
        cp.start()


def _sibling_exchange_finish(grads, landing, send_sems, recv_sems):
    copies = _sibling_copies(grads, landing, send_sems, recv_sems)
    for cp in copies:
        cp.wait_recv()
    for cp in copies:
        cp.wait_send()


def _sibling_exchange_sems(n):
    return [pltpu.SemaphoreType.DMA((n, N_CHIP)), pltpu.SemaphoreType.DMA((n, N_CHIP))]


def _sibling_exchange(grads):
    n = len(grads)

    def body(*refs):
        ins, outs = refs[:n], refs[n:2 * n]
        _sibling_exchange_start(ins, outs, *refs[2 * n:])
        _sibling_exchange_finish(ins, outs, *refs[2 * n:])

    return pl.pallas_call(
        body, name="sibling_exchange",
        in_specs=[HBM_SPEC] * n, out_specs=[HBM_SPEC] * n,
        out_shape=[_hbm_like(g, (N_CHIP,) + g.shape[2:]) for g in grads],
        scratch_shapes=_sibling_exchange_sems(n),
    )(*[_in_hbm(g) for g in grads])


def _chip_exchange_start(sums, landing, send_sems, recv_sems):
    x, y, c, others = _mesh_place()
    me = 2 * x + y
    for w in range(len(sums)):
        for k, (ox, oy) in enumerate(others):
            _remote(sums[w].at[2 * ox + oy], landing[w].at[me], send_sems.at[w, k], recv_sems.at[w, k],
                    (ox, oy, c)).start()


def _chip_exchange_finish(sums, landing, send_sems, recv_sems):
    x, y, c, others = _mesh_place()
    for w in range(len(sums)):
        for k, (ox, oy) in enumerate(others):
            piece = landing[w].at[2 * ox + oy]
            _remote(piece, piece, send_sems.at[w, k], recv_sems.at[w, k], (x, y, c)).wait_recv()
    for w in range(len(sums)):
        for k, (ox, oy) in enumerate(others):
            piece = sums[w].at[2 * ox + oy]
            _remote(piece, piece, send_sems.at[w, k], recv_sems.at[w, k], (x, y, c)).wait_send()


def _chip_exchange_sems(n):
    return [pltpu.SemaphoreType.DMA((n, 3)), pltpu.SemaphoreType.DMA((n, 3))]


def _sibling_allgather(bufs, also):
    n = len(bufs)
    k_in, k_out = len(also.operands), also.n_out

    def body(*refs):
        ex_ins, refs = refs[n:n + k_in], refs[n + k_in:]
        outs, refs = refs[:n], refs[n:]
        ex_outs, refs = refs[:k_out], refs[k_out:]
        send_sems, recv_sems, ex_sems = refs[0], refs[1], refs[2:]
        x, y, c, _ = _mesh_place()
        sibling = (x, y, 1 - c)
        also.start(ex_ins, ex_outs, ex_sems)
        sends = [_remote(outs[w].at[c], outs[w].at[c], send_sems.at[w], recv_sems.at[w], sibling) for w in range(n)]
        for cp in sends:
            cp.start()
        for w in range(n):
            landed = outs[w].at[1 - c]
            _remote(landed, landed, send_sems.at[w], recv_sems.at[w], sibling).wait_recv()
        for cp in sends:
            cp.wait_send()
        also.finish(ex_ins, ex_outs, ex_sems)

    res = pl.pallas_call(
        body, name="sibling_allgather",
        in_specs=[HBM_SPEC] * (n + k_in), out_specs=[HBM_SPEC] * (n + k_out),
        out_shape=[_hbm_like(b) for b in bufs] + also.out_shape,
        input_output_aliases={**{w: w for w in range(n)}, **{n + i: n + o for i, o in also.aliases.items()}},
        scratch_shapes=[pltpu.SemaphoreType.DMA((n,)), pltpu.SemaphoreType.DMA((n,))] + also.sems,
    )(*bufs, *[_in_hbm(o) for o in also.operands])
    return list(res[:n]), list(res[n:])


def _pair_sum(grad, other, place):
    _, _, h, cols = grad.shape
    tr = _row_tile(h)

    def body(place_ref, g_ref, o_ref, sums_ref, own_ref):
        s = (g_ref[0, 0] + o_ref[0]).astype(BF16)
        sums_ref[0] = s

        @pl.when(pl.program_id(1) == place_ref[0])
        def _():
            own_ref[0] = s

    return pl.pallas_call(
        body, name="pair_sum",
        grid_spec=pltpu.PrefetchScalarGridSpec(
            num_scalar_prefetch=1, grid=(h // tr, N_CHIP),
            in_specs=[pl.BlockSpec((1, 1, tr, cols), lambda r, j, place_ref: (j, place_ref[1], r, 0)),
                      pl.BlockSpec((1, tr, cols), lambda r, j, place_ref: (j, r, 0))],
            out_specs=[pl.BlockSpec((1, tr, cols), lambda r, j, place_ref: (j, r, 0)),
                       pl.BlockSpec((1, tr, cols), lambda r, j, place_ref: (place_ref[0], r, 0))]),
        out_shape=[pltpu.HBM((N_CHIP, h, cols), BF16)] * 2,
        compiler_params=_params(32, 2),
    )(place, _in_hbm(grad), _in_hbm(other))


def _chip_sum(parts, place):
    _, h, cols = parts.shape
    tr = _row_tile(h)

    def body(place_ref, p_ref, out_ref):
        out_ref[0] = ((p_ref[0].astype(F32) + p_ref[1].astype(F32)) + p_ref[2].astype(F32)) + p_ref[3].astype(F32)

    return pl.pallas_call(
        body, name="chip_sum",
        grid_spec=pltpu.PrefetchScalarGridSpec(
            num_scalar_prefetch=1, grid=(h // tr,),
            in_specs=[pl.BlockSpec((N_CHIP, tr, cols), lambda r, place_ref: (0, r, 0))],
            out_specs=pl.BlockSpec((1, tr, cols), lambda r, place_ref: (place_ref[1], r, 0))),
        out_shape=pltpu.HBM((2, h, cols), F32),
        compiler_params=_params(32),
    )(place, _in_hbm(parts))


def _adamw_math(w, g, m, v):
    m = ADAM_B1 * m + (1.0 - ADAM_B1) * g
    v = ADAM_B2 * v + (1.0 - ADAM_B2) * (g * g)
    m_hat = m / (1.0 - ADAM_B1 ** ADAM_STEP)
    v_hat = v / (1.0 - ADAM_B2 ** ADAM_STEP)
    delta = -ADAM_LR * (m_hat / (jnp.sqrt(v_hat) + ADAM_EPS) + ADAM_WD * w)
    return delta, m, v


def _adamw(w, g, m, v, exchange=None):
    rows, cols = w.shape
    tr = _row_tile(rows)

    def body(w_ref, g_ref, m_ref, v_ref, d_ref, nm_ref, nv_ref, g_out_ref):
        g = g_ref[...]
        d_ref[...], nm_ref[...], nv_ref[...] = _adamw_math(w_ref[...], g, m_ref[...], v_ref[...])
        g_out_ref[...] = g

    spec = pl.BlockSpec((tr, cols), lambda r: (r, 0))
    return _call(
        body, (w, g, m, v), grid=(rows // tr,), name="adamw",
        in_specs=[spec] * 4, out_specs=[spec] * 4,
        out_shape=[jax.ShapeDtypeStruct((rows, cols), F32)] * 4,
        compiler_params=_params(48), exchange=exchange)


SMALL_NAMES = ("norm1_gain", "gmlp_v_gain", "w_spatial", "b_spatial", "attn_sinks", "rel_bias_table", "norm2_gain",
               "final_gain")
PACK_TILE = 8 * 128


def _pack_small(arrays):
    parts = []
    for a in arrays:
        flat = a.reshape(-1)
        rows = -(-flat.shape[0] // PACK_TILE) * 8
        parts.append(jnp.pad(flat, (0, rows * 128 - flat.shape[0])).reshape(rows, 128))
    return jnp.concatenate(parts, axis=0)


def _unpack_small(packed, like):
    out, row = [], 0
    for a in like:
        size = math.prod(a.shape)
        rows = -(-size // PACK_TILE) * 8
        out.append(packed[row:row + rows].reshape(-1)[:size].reshape(a.shape))
        row += rows
    return out


def _small_update(gathered, w, m, v):
    rows = gathered.shape[1]

    def body(g_ref, w_ref, m_ref, v_ref, tot_ref, d_ref, nm_ref, nv_ref):
        total = g_ref[0].astype(F32)
        for dev in range(1, 8):
            total = total + g_ref[dev].astype(F32)
        tot_ref[...] = total
        d_ref[...], nm_ref[...], nv_ref[...] = _adamw_math(w_ref[...], total, m_ref[...], v_ref[...])

    return pl.pallas_call(
        body, name="small_update",
        in_specs=[VMEM_SPEC] * 4, out_specs=[VMEM_SPEC] * 4,
        out_shape=[jax.ShapeDtypeStruct((rows, 128), F32)] * 4,
        compiler_params=pltpu.CompilerParams(vmem_limit_bytes=24 * MIB),
    )(gathered, w, m, v)


def _halves(a):
    return a.reshape(a.shape[:-2] + (2, a.shape[-2] // 2, a.shape[-1]))


def _whole(a):
    return a.reshape(a.shape[:-3] + (2 * a.shape[-2], a.shape[-1]))


def kernel(x, p, norm1_gain, w_in, gmlp_v_gain, w_spatial, b_spatial, attn_sinks, rel_bias_table, w_out, norm2_gain, w_ff1, w_ff2, w_ple_proj, w_ple_gate, final_gain, loss_target, m_norm1_gain, m_w_in, m_gmlp_v_gain, m_w_spatial, m_b_spatial, m_attn_sinks, m_rel_bias_table, m_w_out, m_norm2_gain, m_w_ff1, m_w_ff2, m_w_ple_proj, m_w_ple_gate, m_final_gain, v_norm1_gain, v_w_in, v_gmlp_v_gain, v_w_spatial, v_b_spatial, v_attn_sinks, v_rel_bias_table, v_w_out, v_norm2_gain, v_w_ff1, v_w_ff2, v_w_ple_proj, v_w_ple_gate, v_final_gain):
    given = dict(locals())
    small = {n: given[n] for n in SMALL_NAMES}
    chip = 2 * lax.axis_index("x") + lax.axis_index("y")
    place = jnp.stack([chip, lax.axis_index("c")]).astype(jnp.int32)

    big_names = ("w_in", "w_out", "w_ff1", "w_ff2", "w_ple_proj", "w_ple_gate")
    shards = {n: given[n][0] for n in big_names}
    travel = dict(shards, w_in=jnp.transpose(shards["w_in"]))
    rest = [n for n in big_names if n != "w_in"]
    cast, gathered = _cast_shards_beside_gather([travel[n] for n in rest], place[:1],
                                                [_cast_shard(travel["w_in"], place[:1])])
    bufs = dict(zip(rest + ["w_in"], cast + gathered))
    dx, landed, exchange_in, small_grads, sq = _step(x[0], p[0, 0], loss_target[0], small, bufs, place)

    out_grad, out_delta, out_m, out_v = {}, {}, {}, {}

    def update(n, g, exchange=None):
        to = jnp.transpose if n == "w_in" else (lambda a: a)
        (delta, new_m, new_v, g_out), got = _adamw(to(shards[n]), g, to(given["m_" + n][0]), to(given["v_" + n][0]),
                                                   exchange)
        out_grad[n], out_delta[n], out_m[n], out_v[n] = [to(a)[None] for a in (g_out, delta, new_m, new_v)]
        return got

    spare = jnp.zeros((8, 128), F32)
    small_packed = _pack_small([small_grads[n] for n in SMALL_NAMES] + [spare]).astype(BF16)
    early = [n for n in big_names if n != "w_in"]
    reduced, (small_gathered, sq_gathered, landed_in) = _sibling_allgather(
        [_chip_sum(landed[n], place) for n in early], _Both(_Both(_GatherAll(small_packed), _GatherAll(sq)), exchange_in))
    for n, r in zip(early, reduced):
        update(n, _whole(r))
    (reduced_in,), _ = _sibling_allgather([_chip_sum(landed_in, place)], _Nothing())
    update("w_in", _whole(reduced_in))

    like = [given[n] for n in SMALL_NAMES] + [spare]
    packed = _small_update(small_gathered, *[_pack_small([given[pre + n] for n in SMALL_NAMES] + [spare])
                                             for pre in ("", "m_", "v_")])
    for res, out in zip(packed, (out_grad, out_delta, out_m, out_v)):
        out.update(zip(SMALL_NAMES, _unpack_small(res, like)))
    loss = 0.5 * jnp.sum(sq_gathered[:, 0, 0]) / D

    order = ("norm1_gain", "w_in", "gmlp_v_gain", "w_spatial", "b_spatial", "attn_sinks", "rel_bias_table", "w_out",
             "norm2_gain", "w_ff1", "w_ff2", "w_ple_proj", "w_ple_gate", "final_gain")
    return (loss, dx[None], *[out_grad[n] for n in order], *[out_delta[n] for n in order],
            *[out_m[n] for n in order], *[out_v[n] for n in order])
```
